```python
import jax, jax.numpy as jnp
from jax import lax
import numpy as np

D_MODEL = 1024
BATCH = 8
SEQ = 8192
DEPTH = 2

N_MIXERS = 2
N_HGRN_LAYERS = (DEPTH + 1) // 2
N_ATTN_LAYERS = DEPTH // 2

HGRN_EXPAND = 128
HGRN_HEADS = D_MODEL // HGRN_EXPAND
HGRN_KEY_DIM = HGRN_EXPAND
HGRN_VAL_DIM = D_MODEL // HGRN_HEADS
HGRN_QK_WIDTH = HGRN_HEADS * HGRN_KEY_DIM
HGRN_V_WIDTH = HGRN_HEADS * HGRN_VAL_DIM
HGRN_IN_WIDTH = 2 * HGRN_QK_WIDTH + 2 * HGRN_V_WIDTH
HGRN_CHUNK = 64

ATTN_HEAD_DIM = 128
DILATED_GROUPS = ((128, 1), (512, 4), (2048, 16))
HEADS_PER_GROUP = 4
N_GROUPS = len(DILATED_GROUPS)
ATTN_HEADS = HEADS_PER_GROUP * N_GROUPS
ATTN_WIDTH = ATTN_HEADS * ATTN_HEAD_DIM
ROPE_THETA = 10000.0

D_FF = ((8 * D_MODEL // 3 + 255) // 256) * 256

NORM_EPS = 1e-6

kernel_name = "hgrn2_dilated_swa_interleaved_trunk"


def rmsnorm(x, gain):
    xf = x.astype(jnp.float32)
    y = xf * lax.rsqrt(jnp.mean(xf * xf, axis=-1, keepdims=True) + NORM_EPS)
    return (y * gain.astype(jnp.float32)).astype(x.dtype)


def rope_tables(seq_len):
    inv_freq = 1.0 / (ROPE_THETA ** (jnp.arange(0, ATTN_HEAD_DIM, 2, dtype=jnp.float32) / ATTN_HEAD_DIM))
    pos = jnp.arange(seq_len, dtype=jnp.float32)
    ang = pos[:, None] * inv_freq[None, :]
    return jnp.cos(ang)[:, None, :], jnp.sin(ang)[:, None, :]


def apply_rope(x, cos, sin):
    xf = x.astype(jnp.float32)
    x1, x2 = jnp.split(xf, 2, axis=-1)
    out = jnp.concatenate([x1 * cos - x2 * sin, x2 * cos + x1 * sin], axis=-1)
    return out.astype(x.dtype)


def hgrn2_mixer(u, w_in, lower_bound, out_gain, w_out):
    B, S, _ = u.shape
    H, K, V, C = HGRN_HEADS, HGRN_KEY_DIM, HGRN_VAL_DIM, HGRN_CHUNK
    proj = u @ w_in
    q, f, i, g = jnp.split(proj, [HGRN_QK_WIDTH, 2 * HGRN_QK_WIDTH, 2 * HGRN_QK_WIDTH + HGRN_V_WIDTH], axis=-1)
    lb = lower_bound.astype(jnp.float32)
    forget = lb + (1.0 - lb) * jax.nn.sigmoid(f.astype(jnp.float32))
    key = 1.0 - forget
    log_f = jnp.log(forget)
    query = jax.nn.silu(q.astype(jnp.float32))
    value = i.astype(jnp.float32)

    def to_chunks(t, d):
        return t.reshape(B, S // C, C, H, d).transpose(1, 0, 3, 2, 4)

    xs = (to_chunks(query, K), to_chunks(key, K), to_chunks(value, V), to_chunks(log_f, K))
    causal = jnp.tril(jnp.ones((C, C), dtype=bool))[:, :, None]

    def chunk_step(state, inp):
        qc, kc, vc, gc = inp
        b = jnp.cumsum(gc, axis=2)
        o_inter = jnp.einsum('bhck,bhkv->bhcv', qc * jnp.exp(b), state)
        diff = b[:, :, :, None, :] - b[:, :, None, :, :]
        decay = jnp.exp(jnp.where(causal, diff, -jnp.inf))
        scores = jnp.einsum('bhtk,bhsk,bhtsk->bhts', qc, kc, decay)
        o_intra = jnp.einsum('bhts,bhsv->bhtv', scores, vc)
        b_last = b[:, :, -1, :]
        k_to_end = kc * jnp.exp(b_last[:, :, None, :] - b)
        new_state = jnp.exp(b_last)[..., None] * state + jnp.einsum('bhck,bhcv->bhkv', k_to_end, vc)
        return new_state, o_inter + o_intra

    state0 = jnp.zeros((B, H, K, V), dtype=jnp.float32)
    _, o = lax.scan(chunk_step, state0, xs)
    o = o.transpose(1, 0, 3, 2, 4).reshape(B, S, H, V)
    o = rmsnorm(o, out_gain)
    o = o * jax.nn.silu(g.astype(jnp.float32)).reshape(B, S, H, V)
    return o.reshape(B, S, H * V).astype(u.dtype) @ w_out


def dilated_window_group(q, k, v, window, dilation):
    B, S, Hg, D = q.shape
    span = window // dilation
    L = S // dilation
    nb = -(-L // span)
    Lp = nb * span

    def to_blocks(t):
        t = t.reshape(B, L, dilation, Hg, D).transpose(0, 2, 3, 1, 4)
        t = jnp.pad(t, ((0, 0), (0, 0), (0, 0), (0, Lp - L), (0, 0)))
        return t.reshape(B, dilation, Hg, nb, span, D)

    qb, kb, vb = to_blocks(q), to_blocks(k), to_blocks(v)

    def with_prev(t):
        prev = jnp.pad(t, ((0, 0), (0, 0), (0, 0), (1, 0), (0, 0), (0, 0)))[:, :, :, :-1]
        return jnp.concatenate([prev, t], axis=4)

    kw, vw = with_prev(kb), with_prev(vb)
    scores = jnp.einsum('bdhnqe,bdhnke->bdhnqk', qb, kw, preferred_element_type=jnp.float32) * (D ** -0.5)
    blk = jnp.arange(nb)[:, None, None] * span
    qpos = blk + jnp.arange(span)[None, :, None]
    kpos = blk - span + jnp.arange(2 * span)[None, None, :]
    mask = (kpos <= qpos) & (kpos >= qpos - span) & (kpos >= 0)
    scores = jnp.where(mask, scores, -jnp.inf)
    m = jnp.max(scores, axis=-1, keepdims=True)
    p = jnp.exp(scores - m)
    l = jnp.sum(p, axis=-1, keepdims=True)
    out = jnp.einsum('bdhnqk,bdhnke->bdhnqe', p, vw.astype(jnp.float32)) / l
    lse = (m + jnp.log(l))[..., 0]
    out = out.reshape(B, dilation, Hg, Lp, D)[:, :, :, :L].transpose(0, 3, 1, 2, 4).reshape(B, S, Hg, D)
    lse = lse.reshape(B, dilation, Hg, Lp)[:, :, :, :L].transpose(0, 3, 1, 2).reshape(B, S, Hg)
    return out, lse


def dilated_attention_mixer(u, w_qkv, w_out, cos, sin):
    B, S, _ = u.shape
    qkv = (u @ w_qkv).reshape(B, S, 3, ATTN_HEADS, ATTN_HEAD_DIM)
    q = apply_rope(qkv[:, :, 0], cos, sin)
    k = apply_rope(qkv[:, :, 1], cos, sin)
    v = qkv[:, :, 2]
    outs, lses = [], []
    for gi, (window, dilation) in enumerate(DILATED_GROUPS):
        hs = slice(gi * HEADS_PER_GROUP, (gi + 1) * HEADS_PER_GROUP)
        o_g, lse_g = dilated_window_group(q[:, :, hs], k[:, :, hs], v[:, :, hs], window, dilation)
        outs.append(o_g)
        lses.append(lse_g)
    o = jnp.stack(outs, axis=2)
    lse = jnp.stack(lses, axis=2)
    alpha = jax.nn.softmax(lse, axis=2)
    o = (o * alpha[..., None]).reshape(B, S, ATTN_WIDTH).astype(u.dtype)
    return o @ w_out


def swiglu_ffn(u, w_in, w_down):
    gate, up = jnp.split(u @ w_in, 2, axis=-1)
    return (jax.nn.silu(gate) * up) @ w_down


def _fwd_setup_inputs(seed: int = 0) -> dict:
    key = jax.random.key(seed)
    ks = jax.random.split(key, 13)
    f32 = jnp.float32

    def dense(k, shape, fan_in):
        return jax.random.normal(k, shape, f32) * (fan_in ** -0.5)

    def gain(k, shape):
        return 1.0 + 0.02 * jax.random.normal(k, shape, f32)

    return {
        "x": jax.random.normal(ks[0], (BATCH, SEQ, D_MODEL), f32),
        "norm_mix": gain(ks[1], (DEPTH, D_MODEL)),
        "norm_ffn": gain(ks[2], (DEPTH, D_MODEL)),
        "hgrn_w_in": dense(ks[3], (N_HGRN_LAYERS, D_MODEL, HGRN_IN_WIDTH), D_MODEL),
        "hgrn_lb_logits": 0.5 * jax.random.normal(ks[4], (DEPTH + 1, HGRN_QK_WIDTH), f32),
        "hgrn_out_norm": gain(ks[5], (N_HGRN_LAYERS, HGRN_VAL_DIM)),
        "hgrn_w_out": dense(ks[6], (N_HGRN_LAYERS, HGRN_V_WIDTH, D_MODEL), HGRN_V_WIDTH),
        "attn_w_qkv": dense(ks[7], (N_ATTN_LAYERS, D_MODEL, 3 * ATTN_WIDTH), D_MODEL),
        "attn_w_out": dense(ks[8], (N_ATTN_LAYERS, ATTN_WIDTH, D_MODEL), ATTN_WIDTH),
        "ffn_w_in": dense(ks[9], (DEPTH, D_MODEL, 2 * D_FF), D_MODEL),
        "ffn_w_down": dense(ks[10], (DEPTH, D_FF, D_MODEL), D_FF),
        "final_norm": gain(ks[11], (D_MODEL,)),
    }


def _fwd_reference(x, norm_mix, norm_ffn, hgrn_w_in, hgrn_lb_logits, hgrn_out_norm, hgrn_w_out,
              attn_w_qkv, attn_w_out, ffn_w_in, ffn_w_down, final_norm):
    lb_table = jnp.cumsum(jax.nn.softmax(hgrn_lb_logits.astype(jnp.float32), axis=0), axis=0)
    cos, sin = rope_tables(x.shape[1])
    h = x
    for layer in range(DEPTH):
        u = rmsnorm(h, norm_mix[layer])
        if layer % N_MIXERS == 0:
            a = layer // N_MIXERS
            mix = hgrn2_mixer(u, hgrn_w_in[a], lb_table[layer], hgrn_out_norm[a], hgrn_w_out[a])
        else:
            a = layer // N_MIXERS
            mix = dilated_attention_mixer(u, attn_w_qkv[a], attn_w_out[a], cos, sin)
        h = h + mix
        h = h + swiglu_ffn(rmsnorm(h, norm_ffn[layer]), ffn_w_in[layer], ffn_w_down[layer])
    return rmsnorm(h, final_norm)


import jax as _jax
import jax.numpy as _jnp

TWIN_FORMAT = 'train_step'
FWD_PARAMS = ['x', 'norm_mix', 'norm_ffn', 'hgrn_w_in', 'hgrn_lb_logits', 'hgrn_out_norm', 'hgrn_w_out', 'attn_w_qkv', 'attn_w_out', 'ffn_w_in', 'ffn_w_down', 'final_norm']
TWIN_WEIGHTS = ['norm_mix', 'norm_ffn', 'hgrn_w_in', 'hgrn_lb_logits', 'hgrn_out_norm', 'hgrn_w_out', 'attn_w_qkv', 'attn_w_out', 'ffn_w_in', 'ffn_w_down', 'final_norm']
TWIN_DIFF_INPUT = 'x'
TWIN_INPUTS = ['x', 'norm_mix', 'norm_ffn', 'hgrn_w_in', 'hgrn_lb_logits', 'hgrn_out_norm', 'hgrn_w_out', 'attn_w_qkv', 'attn_w_out', 'ffn_w_in', 'ffn_w_down', 'final_norm', 'loss_target', 'm_norm_mix', 'm_norm_ffn', 'm_hgrn_w_in', 'm_hgrn_lb_logits', 'm_hgrn_out_norm', 'm_hgrn_w_out', 'm_attn_w_qkv', 'm_attn_w_out', 'm_ffn_w_in', 'm_ffn_w_down', 'm_final_norm', 'v_norm_mix', 'v_norm_ffn', 'v_hgrn_w_in', 'v_hgrn_lb_logits', 'v_hgrn_out_norm', 'v_hgrn_w_out', 'v_attn_w_qkv', 'v_attn_w_out', 'v_ffn_w_in', 'v_ffn_w_down', 'v_final_norm']
TWIN_OUTPUTS = ['loss', 'grad_x', 'grad_norm_mix', 'grad_norm_ffn', 'grad_hgrn_w_in', 'grad_hgrn_lb_logits', 'grad_hgrn_out_norm', 'grad_hgrn_w_out', 'grad_attn_w_qkv', 'grad_attn_w_out', 'grad_ffn_w_in', 'grad_ffn_w_down', 'grad_final_norm', 'delta_norm_mix', 'delta_norm_ffn', 'delta_hgrn_w_in', 'delta_hgrn_lb_logits', 'delta_hgrn_out_norm', 'delta_hgrn_w_out', 'delta_attn_w_qkv', 'delta_attn_w_out', 'delta_ffn_w_in', 'delta_ffn_w_down', 'delta_final_norm', 'new_m_norm_mix', 'new_m_norm_ffn', 'new_m_hgrn_w_in', 'new_m_hgrn_lb_logits', 'new_m_hgrn_out_norm', 'new_m_hgrn_w_out', 'new_m_attn_w_qkv', 'new_m_attn_w_out', 'new_m_ffn_w_in', 'new_m_ffn_w_down', 'new_m_final_norm', 'new_v_norm_mix', 'new_v_norm_ffn', 'new_v_hgrn_w_in', 'new_v_hgrn_lb_logits', 'new_v_hgrn_out_norm', 'new_v_hgrn_w_out', 'new_v_attn_w_qkv', 'new_v_attn_w_out', 'new_v_ffn_w_in', 'new_v_ffn_w_down', 'new_v_final_norm']
TWIN_LEAF_KINDS = {'loss': 'loss', 'grad_x': 'grad_x', 'grad_norm_mix': 'grad_w', 'grad_norm_ffn': 'grad_w', 'grad_hgrn_w_in': 'grad_w', 'grad_hgrn_lb_logits': 'grad_w', 'grad_hgrn_out_norm': 'grad_w', 'grad_hgrn_w_out': 'grad_w', 'grad_attn_w_qkv': 'grad_w', 'grad_attn_w_out': 'grad_w', 'grad_ffn_w_in': 'grad_w', 'grad_ffn_w_down': 'grad_w', 'grad_final_norm': 'grad_w', 'delta_norm_mix': 'delta_w', 'delta_norm_ffn': 'delta_w', 'delta_hgrn_w_in': 'delta_w', 'delta_hgrn_lb_logits': 'delta_w', 'delta_hgrn_out_norm': 'delta_w', 'delta_hgrn_w_out': 'delta_w', 'delta_attn_w_qkv': 'delta_w', 'delta_attn_w_out': 'delta_w', 'delta_ffn_w_in': 'delta_w', 'delta_ffn_w_down': 'delta_w', 'delta_final_norm': 'delta_w', 'new_m_norm_mix': 'new_m', 'new_m_norm_ffn': 'new_m', 'new_m_hgrn_w_in': 'new_m', 'new_m_hgrn_lb_logits': 'new_m', 'new_m_hgrn_out_norm': 'new_m', 'new_m_hgrn_w_out': 'new_m', 'new_m_attn_w_qkv': 'new_m', 'new_m_attn_w_out': 'new_m', 'new_m_ffn_w_in': 'new_m', 'new_m_ffn_w_down': 'new_m', 'new_m_final_norm': 'new_m', 'new_v_norm_mix': 'new_v', 'new_v_norm_ffn': 'new_v', 'new_v_hgrn_w_in': 'new_v', 'new_v_hgrn_lb_logits': 'new_v', 'new_v_hgrn_out_norm': 'new_v', 'new_v_hgrn_w_out': 'new_v', 'new_v_attn_w_qkv': 'new_v', 'new_v_attn_w_out': 'new_v', 'new_v_ffn_w_in': 'new_v', 'new_v_ffn_w_down': 'new_v', 'new_v_final_norm': 'new_v'}


def _forward(args):
    return _fwd_reference(*[args[k] for k in FWD_PARAMS])


def _output_shape():
    def fwd():
        inp = _fwd_setup_inputs(0)
        return _fwd_reference(*[inp[k] for k in FWD_PARAMS])
    out = _jax.eval_shape(fwd)
    return out.shape, out.dtype

N_MICROBATCH = 1
ADAM_LR = 0.001
ADAM_B1 = 0.9
ADAM_B2 = 0.999
ADAM_EPS = 1e-08
ADAM_WD = 0.01
ADAM_STEP = 10
PER_EXAMPLE_BATCH_AXIS = {'x': 0, 'loss_target': 0}
SHARED_INPUTS = []
_WEIGHT_DTYPES = {'norm_mix': _jnp.float32, 'norm_ffn': _jnp.float32, 'hgrn_w_in': _jnp.float32, 'hgrn_lb_logits': _jnp.float32, 'hgrn_out_norm': _jnp.float32, 'hgrn_w_out': _jnp.float32, 'attn_w_qkv': _jnp.float32, 'attn_w_out': _jnp.float32, 'ffn_w_in': _jnp.float32, 'ffn_w_down': _jnp.float32, 'final_norm': _jnp.float32}
MOMENT_SCALE = {'norm_mix': 1.632916e-01, 'norm_ffn': 1.638536e-01, 'hgrn_w_in': 1.126292e-01, 'hgrn_lb_logits': 7.436946e-03, 'hgrn_out_norm': 4.352156e-01, 'hgrn_w_out': 1.552158e-01, 'attn_w_qkv': 1.688943e-02, 'attn_w_out': 2.271278e-02, 'ffn_w_in': 7.040400e-02, 'ffn_w_down': 1.148325e-01, 'final_norm': 6.403716e+01}


def _to_microbatches(a, axis):
    t = _jnp.moveaxis(a, axis, 0)
    t = t.reshape((N_MICROBATCH, t.shape[0] // N_MICROBATCH) + t.shape[1:])
    return _jnp.moveaxis(t, 1, axis + 1)


def setup_inputs(seed: int = 0) -> dict:
    inp = _fwd_setup_inputs(seed)
    key = _jax.random.fold_in(_jax.random.key(seed), 7919)
    shape, _ = _output_shape()
    out = dict(inp)
    out["loss_target"] = _jax.random.normal(_jax.random.fold_in(key, 0), shape, _jnp.float32)
    for i, name in enumerate(TWIN_WEIGHTS):
        w = inp[name].astype(_jnp.float32)
        if MOMENT_SCALE is None:
            s = _jnp.sqrt(_jnp.mean(_jnp.square(w)) + 1e-30)
        else:
            s = MOMENT_SCALE[name]
        km, kv = _jax.random.split(_jax.random.fold_in(key, i + 1))
        out[name] = w
        out["m_" + name] = s * _jax.random.normal(km, w.shape, _jnp.float32)
        out["v_" + name] = (s * s) * _jax.random.uniform(kv, w.shape, _jnp.float32, 0.5, 1.5)
    if N_MICROBATCH > 1:
        for name, axis in PER_EXAMPLE_BATCH_AXIS.items():
            out[name] = _to_microbatches(out[name], axis)
    return {'x': out['x'], 'norm_mix': out['norm_mix'], 'norm_ffn': out['norm_ffn'], 'hgrn_w_in': out['hgrn_w_in'], 'hgrn_lb_logits': out['hgrn_lb_logits'], 'hgrn_out_norm': out['hgrn_out_norm'], 'hgrn_w_out': out['hgrn_w_out'], 'attn_w_qkv': out['attn_w_qkv'], 'attn_w_out': out['attn_w_out'], 'ffn_w_in': out['ffn_w_in'], 'ffn_w_down': out['ffn_w_down'], 'final_norm': out['final_norm'], 'loss_target': out['loss_target'], 'm_norm_mix': out['m_norm_mix'], 'm_norm_ffn': out['m_norm_ffn'], 'm_hgrn_w_in': out['m_hgrn_w_in'], 'm_hgrn_lb_logits': out['m_hgrn_lb_logits'], 'm_hgrn_out_norm': out['m_hgrn_out_norm'], 'm_hgrn_w_out': out['m_hgrn_w_out'], 'm_attn_w_qkv': out['m_attn_w_qkv'], 'm_attn_w_out': out['m_attn_w_out'], 'm_ffn_w_in': out['m_ffn_w_in'], 'm_ffn_w_down': out['m_ffn_w_down'], 'm_final_norm': out['m_final_norm'], 'v_norm_mix': out['v_norm_mix'], 'v_norm_ffn': out['v_norm_ffn'], 'v_hgrn_w_in': out['v_hgrn_w_in'], 'v_hgrn_lb_logits': out['v_hgrn_lb_logits'], 'v_hgrn_out_norm': out['v_hgrn_out_norm'], 'v_hgrn_w_out': out['v_hgrn_w_out'], 'v_attn_w_qkv': out['v_attn_w_qkv'], 'v_attn_w_out': out['v_attn_w_out'], 'v_ffn_w_in': out['v_ffn_w_in'], 'v_ffn_w_down': out['v_ffn_w_down'], 'v_final_norm': out['v_final_norm']}


def _loss(weights, diff, rest, loss_target):
    with _jax.named_scope("forward"):
        args = {**rest, TWIN_DIFF_INPUT: diff, **{k: w.astype(_WEIGHT_DTYPES[k]) for k, w in weights.items()}}
        y = _forward(args)
    with _jax.named_scope("loss_head"):
        err = _jnp.square(y.astype(_jnp.float32) - loss_target)
        return 0.5 * _jnp.sum(_jnp.mean(err, axis=-1)) if err.ndim else 0.5 * err


def _adamw(w, g, m, v):
    m = ADAM_B1 * m + (1.0 - ADAM_B1) * g
    v = ADAM_B2 * v + (1.0 - ADAM_B2) * _jnp.square(g)
    m_hat = m / (1.0 - ADAM_B1 ** ADAM_STEP)
    v_hat = v / (1.0 - ADAM_B2 ** ADAM_STEP)
    delta = -ADAM_LR * (m_hat / (_jnp.sqrt(v_hat) + ADAM_EPS) + ADAM_WD * w)
    return delta, m, v


def reference(x, norm_mix, norm_ffn, hgrn_w_in, hgrn_lb_logits, hgrn_out_norm, hgrn_w_out, attn_w_qkv, attn_w_out, ffn_w_in, ffn_w_down, final_norm, loss_target, m_norm_mix, m_norm_ffn, m_hgrn_w_in, m_hgrn_lb_logits, m_hgrn_out_norm, m_hgrn_w_out, m_attn_w_qkv, m_attn_w_out, m_ffn_w_in, m_ffn_w_down, m_final_norm, v_norm_mix, v_norm_ffn, v_hgrn_w_in, v_hgrn_lb_logits, v_hgrn_out_norm, v_hgrn_w_out, v_attn_w_qkv, v_attn_w_out, v_ffn_w_in, v_ffn_w_down, v_final_norm):
    given = dict(x=x, norm_mix=norm_mix, norm_ffn=norm_ffn, hgrn_w_in=hgrn_w_in, hgrn_lb_logits=hgrn_lb_logits, hgrn_out_norm=hgrn_out_norm, hgrn_w_out=hgrn_w_out, attn_w_qkv=attn_w_qkv, attn_w_out=attn_w_out, ffn_w_in=ffn_w_in, ffn_w_down=ffn_w_down, final_norm=final_norm, loss_target=loss_target, m_norm_mix=m_norm_mix, m_norm_ffn=m_norm_ffn, m_hgrn_w_in=m_hgrn_w_in, m_hgrn_lb_logits=m_hgrn_lb_logits, m_hgrn_out_norm=m_hgrn_out_norm, m_hgrn_w_out=m_hgrn_w_out, m_attn_w_qkv=m_attn_w_qkv, m_attn_w_out=m_attn_w_out, m_ffn_w_in=m_ffn_w_in, m_ffn_w_down=m_ffn_w_down, m_final_norm=m_final_norm, v_norm_mix=v_norm_mix, v_norm_ffn=v_norm_ffn, v_hgrn_w_in=v_hgrn_w_in, v_hgrn_lb_logits=v_hgrn_lb_logits, v_hgrn_out_norm=v_hgrn_out_norm, v_hgrn_w_out=v_hgrn_w_out, v_attn_w_qkv=v_attn_w_qkv, v_attn_w_out=v_attn_w_out, v_ffn_w_in=v_ffn_w_in, v_ffn_w_down=v_ffn_w_down, v_final_norm=v_final_norm)
    weights = {n: given[n] for n in TWIN_WEIGHTS}
    shared = {n: given[n] for n in SHARED_INPUTS}
    per_example = {n: given[n] for n in ['x']}
    grad_fn = _jax.value_and_grad(_loss, argnums=(0, 1))

    def one_microbatch(ex, loss_target):
        ex = dict(ex)
        diff = ex.pop(TWIN_DIFF_INPUT)
        return grad_fn(weights, diff, {**shared, **ex}, loss_target)

    if N_MICROBATCH == 1:
        loss, (grad_w, grad_x) = one_microbatch(per_example, given["loss_target"])
    else:
        def body(carry, xs):
            loss_sum, grad_sum = carry
            l_k, (gw_k, gx_k) = one_microbatch(xs[0], xs[1])
            with _jax.named_scope("update"):
                return (loss_sum + l_k, _jax.tree.map(_jnp.add, grad_sum, gw_k)), gx_k

        init = (_jnp.zeros((), _jnp.float32), _jax.tree.map(_jnp.zeros_like, weights))
        (loss, grad_w), grad_x = _jax.lax.scan(body, init, (per_example, given["loss_target"]))
    with _jax.named_scope("update"):
        delta_w, new_m, new_v = {}, {}, {}
        for n in TWIN_WEIGHTS:
            delta_w[n], new_m[n], new_v[n] = _adamw(weights[n], grad_w[n], given["m_" + n], given["v_" + n])
    return (loss, grad_x, *[grad_w[n] for n in TWIN_WEIGHTS], *[delta_w[n] for n in TWIN_WEIGHTS],
            *[new_m[n] for n in TWIN_WEIGHTS], *[new_v[n] for n in TWIN_WEIGHTS])
```

```python
import functools

import jax
import jax.numpy as jnp
from jax import lax
from jax.experimental import pallas as pl
from jax.experimental.pallas import tpu as pltpu

F32 = jnp.float32
BF16 = jnp.bfloat16
MESH = pl.DeviceIdType.MESH

D_MODEL = 1024
HEAD = 128
HGRN_HEADS = 8
HGRN_CHUNK = 64
ATTN_GROUPS = ((128, 1), (512, 4), (2048, 16))
ATTN_SPAN = 128
HEADS_PER_GROUP = 4
GROUP_W = HEADS_PER_GROUP * HEAD
D_FF = 2816
NORM_EPS = 1e-6
ROPE_THETA = 10000.0
NEG = -1e30

ADAM_LR, ADAM_B1, ADAM_B2, ADAM_EPS, ADAM_WD, ADAM_STEP = 0.001, 0.9, 0.999, 1e-08, 0.01, 10

N_CHIPS = 4
VMEM_LIMIT = 56 * 1024 * 1024
SMALL_ROWS = 16


def _params(sem=None):
    return pltpu.CompilerParams(dimension_semantics=sem, vmem_limit_bytes=VMEM_LIMIT)


def _row_tile(rows, cols, budget_bytes=3 * 512 * 1024):
    best = 8
    for t in range(8, rows + 1, 8):
        if rows % t == 0 and t * cols * 4 <= budget_bytes:
            best = t
    assert rows % best == 0
    return best


def _sigmoid(v):
    return 1.0 / (1.0 + jnp.exp(-v))


def _dot(a, b):
    return jnp.dot(a, b, preferred_element_type=F32)


def _dot_nt(a, b):
    return lax.dot_general(a, b, (((1,), (1,)), ((), ())), preferred_element_type=F32)


def _dot_tn(a, b):
    return lax.dot_general(a, b, (((0,), (0,)), ((), ())), preferred_element_type=F32)


def _dot_exact(a, b):
    return jnp.dot(a, b, preferred_element_type=F32, precision=lax.Precision.HIGHEST)


def _rstd(v):
    return lax.rsqrt(jnp.mean(v * v, axis=-1, keepdims=True) + NORM_EPS)


def _norm_mm(h, gain, w3, tn, name, tm=512):
    S, K = h.shape
    J, _, n = w3.shape
    tpn = n // tn

    def body(h_ref, g_ref, w_ref, y_ref, u_scr):
        @pl.when(pl.program_id(1) == 0)
        def _():
            v = h_ref[...]
            u_scr[...] = (v * _rstd(v) * g_ref[...]).astype(BF16)

        y_ref[...] = _dot(u_scr[...], w_ref[...])

    return pl.pallas_call(
        body, name=name, grid=(S // tm, J * tpn),
        in_specs=[pl.BlockSpec((tm, K), lambda i, j: (i, 0)),
                  pl.BlockSpec((1, K), lambda i, j: (0, 0)),
                  pl.BlockSpec((None, K, tn), lambda i, j: (j // tpn, 0, j % tpn))],
        out_specs=pl.BlockSpec((tm, tn), lambda i, j: (i, j)),
        out_shape=jax.ShapeDtypeStruct((S, J * n), F32),
        scratch_shapes=[pltpu.VMEM((tm, K), BF16)],
        compiler_params=_params(("parallel", "arbitrary")))(h, gain, w3)


def _mm_res(h, a, w2, name, tm=512):
    S, N = h.shape
    K = a.shape[1]

    def body(h_ref, a_ref, w_ref, o_ref):
        o_ref[...] = h_ref[...] + _dot(a_ref[...], w_ref[...])

    return pl.pallas_call(
        body, name=name, grid=(S // tm,),
        in_specs=[pl.BlockSpec((tm, N), lambda i: (i, 0)),
                  pl.BlockSpec((tm, K), lambda i: (i, 0)),
                  pl.BlockSpec((K, N), lambda i: (0, 0))],
        out_specs=pl.BlockSpec((tm, N), lambda i: (i, 0)),
        out_shape=jax.ShapeDtypeStruct((S, N), F32),
        compiler_params=_params(("parallel",)))(h, a, w2)


def _swiglu(z_ref, F):
    g = z_ref[:, :F]
    return (g * _sigmoid(g) * z_ref[:, F:]).astype(BF16)


def _swiglu_mm_res(h, z, w2, name, tm=256):
    S, N = h.shape
    F = w2.shape[0]

    def body(h_ref, z_ref, w_ref, o_ref):
        o_ref[...] = h_ref[...] + _dot(_swiglu(z_ref, F), w_ref[...])

    return pl.pallas_call(
        body, name=name, grid=(S // tm,),
        in_specs=[pl.BlockSpec((tm, N), lambda i: (i, 0)),
                  pl.BlockSpec((tm, 2 * F), lambda i: (i, 0)),
                  pl.BlockSpec((F, N), lambda i: (0, 0))],
        out_specs=pl.BlockSpec((tm, N), lambda i: (i, 0)),
        out_shape=jax.ShapeDtypeStruct((S, N), F32),
        compiler_params=_params(("parallel",)))(h, z, w2)


def _dy_specs(dy, J, n, tm):
    if dy.ndim == 3:
        return [pl.BlockSpec((None, tm, n), functools.partial(lambda i, j: (j, i, 0), j=j)) for j in range(J)]
    return [pl.BlockSpec((tm, n), functools.partial(lambda i, j: (i, j), j=j)) for j in range(J)]


def _acc_nt(dy_refs, w_ref):
    acc = None
    for j, r in enumerate(dy_refs):
        t = _dot_nt(r[...].astype(BF16), w_ref[j])
        acc = t if acc is None else acc + t
    return acc


def _mm_nt(dy, w3, name, out_dtype=F32, tm=512):
    J, K, n = w3.shape
    S = dy.shape[-2]

    def body(*refs):
        dy_refs, w_ref, o_ref = refs[:J], refs[J], refs[J + 1]
        o_ref[...] = _acc_nt(dy_refs, w_ref).astype(o_ref.dtype)

    return pl.pallas_call(
        body, name=name, grid=(S // tm,),
        in_specs=_dy_specs(dy, J, n, tm) + [pl.BlockSpec((J, K, n), lambda i: (0, 0, 0))],
        out_specs=pl.BlockSpec((tm, K), lambda i: (i, 0)),
        out_shape=jax.ShapeDtypeStruct((S, K), out_dtype),
        compiler_params=_params(("parallel",)))(*([dy] * J), w3)


def _mm_nt_normbwd(dy, w3, h, gain, dh, name, tm=512):
    J, K, n = w3.shape
    S = h.shape[0]

    def body(*refs):
        dy_refs, w_ref, h_ref, g_ref, dh_ref, o_ref, dg_ref = refs[:J], *refs[J:]
        du = _acc_nt(dy_refs, w_ref)
        v = h_ref[...]
        r = _rstd(v)
        xh = v * r
        dyg = du * g_ref[...]
        o_ref[...] = dh_ref[...] + r * (dyg - xh * jnp.mean(dyg * xh, axis=-1, keepdims=True))

        @pl.when(pl.program_id(0) == 0)
        def _():
            dg_ref[...] = jnp.zeros_like(dg_ref)

        dg_ref[...] += jnp.sum(du * xh, axis=0, keepdims=True)

    row = pl.BlockSpec((tm, K), lambda i: (i, 0))
    vec = pl.BlockSpec((1, K), lambda i: (0, 0))
    return pl.pallas_call(
        body, name=name, grid=(S // tm,),
        in_specs=_dy_specs(dy, J, n, tm) + [pl.BlockSpec((J, K, n), lambda i: (0, 0, 0)), row, vec, row],
        out_specs=[row, vec],
        out_shape=[jax.ShapeDtypeStruct((S, K), F32), jax.ShapeDtypeStruct((1, K), F32)],
        compiler_params=_params(("arbitrary",)))(*([dy] * J), w3, h, gain, dh)


def _mm_nt_swiglu_bwd(dh, w2, z, name, tm=256):
    F, N = w2.shape
    S = dh.shape[0]

    def body(dh_ref, w_ref, z_ref, o_ref):
        da = _dot_nt(dh_ref[...].astype(BF16), w_ref[...])
        g = z_ref[:, :F]
        u = z_ref[:, F:]
        sg = _sigmoid(g)
        o_ref[:, :F] = (da * u * (sg * (1.0 + g * (1.0 - sg)))).astype(BF16)
        o_ref[:, F:] = (da * (g * sg)).astype(BF16)

    return pl.pallas_call(
        body, name=name, grid=(S // tm,),
        in_specs=[pl.BlockSpec((tm, N), lambda i: (i, 0)),
                  pl.BlockSpec((F, N), lambda i: (0, 0)),
                  pl.BlockSpec((tm, 2 * F), lambda i: (i, 0))],
        out_specs=pl.BlockSpec((tm, 2 * F), lambda i: (i, 0)),
        out_shape=jax.ShapeDtypeStruct((S, 2 * F), BF16),
        compiler_params=_params(("parallel",)))(dh, w2, z)


def _mm_tn(kind, xs, dy, J, n, tn, name):
    tpn = n // tn
    ts = 256 if kind == "swiglu" else 512
    S = xs[0].shape[0]
    if kind == "norm":
        K = xs[0].shape[1]
        x_specs = [pl.BlockSpec((ts, K), lambda c, s: (s, 0)), pl.BlockSpec((1, K), lambda c, s: (0, 0))]
    elif kind == "swiglu":
        K = xs[0].shape[1] // 2
        x_specs = [pl.BlockSpec((ts, 2 * K), lambda c, s: (s, 0))]
    else:
        K = xs[0].shape[1]
        x_specs = [pl.BlockSpec((ts, K), lambda c, s: (s, 0))]
    nx = len(xs)
    if dy.ndim == 3:
        dy_spec = pl.BlockSpec((None, ts, tn), lambda c, s: (c // tpn, s, c % tpn))
    else:
        dy_spec = pl.BlockSpec((ts, tn), lambda c, s: (s, c))

    def body(*refs):
        x_refs, dy_ref, o_ref = refs[:nx], refs[nx], refs[nx + 1]
        if kind == "norm":
            v = x_refs[0][...]
            xb = (v * _rstd(v) * x_refs[1][...]).astype(BF16)
        elif kind == "swiglu":
            xb = _swiglu(x_refs[0], K)
        else:
            xb = x_refs[0][...].astype(BF16)

        @pl.when(pl.program_id(1) == 0)
        def _():
            o_ref[...] = jnp.zeros_like(o_ref)

        o_ref[...] += _dot_tn(xb, dy_ref[...].astype(BF16))

    return pl.pallas_call(
        body, name=name, grid=(J * tpn, S // ts),
        in_specs=x_specs + [dy_spec],
        out_specs=pl.BlockSpec((None, K, tn), lambda c, s: (c // tpn, 0, c % tpn)),
        out_shape=jax.ShapeDtypeStruct((J, K, n), F32),
        compiler_params=_params(("parallel", "arbitrary")))(*xs, dy)


def _loss_head(h, gain, target, tm=512):
    S, K = h.shape

    def body(h_ref, g_ref, t_ref, dh_ref, loss_ref, dg_ref):
        v = h_ref[...]
        r = _rstd(v)
        xh = v * r
        g = g_ref[...]
        dy = (xh * g - t_ref[...]) * (1.0 / K)
        dyg = dy * g
        dh_ref[...] = r * (dyg - xh * jnp.mean(dyg * xh, axis=-1, keepdims=True))

        @pl.when(pl.program_id(0) == 0)
        def _():
            loss_ref[...] = jnp.zeros_like(loss_ref)
            dg_ref[...] = jnp.zeros_like(dg_ref)

        part = jnp.sum(jnp.sum(dy * dy, axis=-1, keepdims=True), axis=0, keepdims=True) * (0.5 * K)
        lane = lax.broadcasted_iota(jnp.int32, loss_ref.shape, 1)
        loss_ref[...] += jnp.where(lane == 0, part, 0.0)
        dg_ref[...] += jnp.sum(dy * xh, axis=0, keepdims=True)

    row = pl.BlockSpec((tm, K), lambda i: (i, 0))
    vec = pl.BlockSpec((1, K), lambda i: (0, 0))
    return pl.pallas_call(
        body, name="loss_head", grid=(S // tm,),
        in_specs=[row, vec, row],
        out_specs=[row, pl.BlockSpec((1, HEAD), lambda i: (0, 0)), vec],
        out_shape=[jax.ShapeDtypeStruct((S, K), F32), jax.ShapeDtypeStruct((1, HEAD), F32),
                   jax.ShapeDtypeStruct((1, K), F32)],
        compiler_params=_params(("arbitrary",)))(h, gain, target)


def _lower_bound(lg_ref):
    l0, l1, l2 = lg_ref[0:1, :], lg_ref[1:2, :], lg_ref[2:3, :]
    mx = jnp.maximum(jnp.maximum(l0, l1), l2)
    e0, e1, e2 = jnp.exp(l0 - mx), jnp.exp(l1 - mx), jnp.exp(l2 - mx)
    return e0 / (e0 + e1 + e2)


def _chunk_gates(qz, fz, lb, tri, first_half):
    sig = _sigmoid(fz)
    fg = lb + (1.0 - lb) * sig
    key = 1.0 - fg
    lg = jnp.log(fg)
    b = _dot_exact(tri, lg)
    r = jnp.sum(jnp.where(first_half, lg, 0.0), axis=0, keepdims=True)
    bl = jnp.sum(lg, axis=0, keepdims=True)
    sq = _sigmoid(qz)
    qy = qz * sq
    return sig, fg, key, b, r, bl, sq, qy


def _hgrn_fwd(proj, logits, gain, tb=512):
    S = proj.shape[0]
    H, C = HGRN_HEADS, HGRN_CHUNK
    ncb = tb // C

    def body(q_ref, f_ref, i_ref, g_ref, lg_ref, gn_ref, o_ref, og_ref, st_ref, state):
        @pl.when(pl.program_id(1) == 0)
        def _():
            state[...] = jnp.zeros_like(state)

        lb = _lower_bound(lg_ref)
        gn = gn_ref[...]
        row = lax.broadcasted_iota(jnp.int32, (C, C), 0)
        col = lax.broadcasted_iota(jnp.int32, (C, C), 1)
        causal = col <= row
        tri = causal.astype(F32)
        first_half = lax.broadcasted_iota(jnp.int32, (C, HEAD), 0) < C // 2

        def chunk(ci, carry):
            rows = pl.ds(pl.multiple_of(ci * C, C), C)
            qz, fz, iz, gz = q_ref[rows, :], f_ref[rows, :], i_ref[rows, :], g_ref[rows, :]
            _, _, key, b, r, bl, _, qy = _chunk_gates(qz, fz, lb, tri, first_half)
            qs = (qy * jnp.exp(b - r)).astype(BF16)
            ks = (key * jnp.exp(r - b)).astype(BF16)
            qb = (qy * jnp.exp(b)).astype(BF16)
            ke = (key * jnp.exp(bl - b)).astype(BF16)
            vb = iz.astype(BF16)
            a = jnp.where(causal, _dot_nt(qs, ks), 0.0)
            st = state[...]
            st_ref[ci] = st
            o = _dot_nt(qb, st.astype(BF16)) + _dot(a.astype(BF16), vb)
            state[...] = st * jnp.exp(bl) + _dot_tn(vb, ke)
            o_ref[rows, :] = o
            og_ref[rows, :] = ((o * _rstd(o) * gn) * (gz * _sigmoid(gz))).astype(BF16)
            return carry

        lax.fori_loop(0, ncb, chunk, 0)

    def part(p):
        return pl.BlockSpec((tb, HEAD), functools.partial(lambda h, i, p: (i, p * H + h), p=p))

    return pl.pallas_call(
        body, name="hgrn_fwd", grid=(H, S // tb),
        in_specs=[part(0), part(1), part(2), part(3),
                  pl.BlockSpec((3, HEAD), lambda h, i: (0, h)),
                  pl.BlockSpec((1, HEAD), lambda h, i: (0, 0))],
        out_specs=[pl.BlockSpec((tb, HEAD), lambda h, i: (i, h)),
                   pl.BlockSpec((tb, HEAD), lambda h, i: (i, h)),
                   pl.BlockSpec((None, ncb, HEAD, HEAD), lambda h, i: (h, i, 0, 0))],
        out_shape=[jax.ShapeDtypeStruct((S, H * HEAD), F32),
                   jax.ShapeDtypeStruct((S, H * HEAD), BF16),
                   jax.ShapeDtypeStruct((H, S // C, HEAD, HEAD), F32)],
        scratch_shapes=[pltpu.VMEM((HEAD, HEAD), F32)],
        compiler_params=_params(("parallel", "arbitrary")))(proj, proj, proj, proj, logits, gain)


def _hgrn_bwd(proj, logits, gain, o, states, dog, tb=512):
    S = proj.shape[0]
    H, C = HGRN_HEADS, HGRN_CHUNK
    ncb = tb // C
    nb = S // tb

    def body(q_ref, f_ref, i_ref, g_ref, lg_ref, gn_ref, o_ref, st_ref, dog_ref,
             dp_ref, dlb_ref, dgn_ref, dstate):
        @pl.when(pl.program_id(1) == 0)
        def _():
            dstate[...] = jnp.zeros_like(dstate)
            dlb_ref[...] = jnp.zeros_like(dlb_ref)
            dgn_ref[...] = jnp.zeros_like(dgn_ref)

        lb = _lower_bound(lg_ref)
        oml = 1.0 - lb
        gn = gn_ref[...]
        row = lax.broadcasted_iota(jnp.int32, (C, C), 0)
        col = lax.broadcasted_iota(jnp.int32, (C, C), 1)
        causal = col <= row
        tri = causal.astype(F32)
        tri_up = (col >= row).astype(F32)
        first_half = lax.broadcasted_iota(jnp.int32, (C, HEAD), 0) < C // 2

        def chunk(cj, carry):
            ci = ncb - 1 - cj
            rows = pl.ds(pl.multiple_of(ci * C, C), C)
            qz, fz, iz, gz = q_ref[rows, :], f_ref[rows, :], i_ref[rows, :], g_ref[rows, :]
            sig, fg, key, b, r, bl, sq, qy = _chunk_gates(qz, fz, lb, tri, first_half)
            e_br, e_rb, e_b, e_lb = jnp.exp(b - r), jnp.exp(r - b), jnp.exp(b), jnp.exp(bl - b)
            qs_f, ks_f, qb_f, ke_f = qy * e_br, key * e_rb, qy * e_b, key * e_lb
            qs, ks, qb, ke = qs_f.astype(BF16), ks_f.astype(BF16), qb_f.astype(BF16), ke_f.astype(BF16)
            vb = iz.astype(BF16)
            ov = o_ref[rows, :]
            rs = _rstd(ov)
            xh = ov * rs
            sg = _sigmoid(gz)
            dog_v = dog_ref[rows, :]
            dgz = dog_v * (xh * gn) * (sg * (1.0 + gz * (1.0 - sg)))
            don = dog_v * (gz * sg)
            dgn_ref[...] += jnp.sum(don * xh, axis=0, keepdims=True)
            dyg = don * gn
            do = rs * (dyg - xh * jnp.mean(dyg * xh, axis=-1, keepdims=True))
            dob = do.astype(BF16)
            st0 = st_ref[ci]
            dst1 = dstate[...]
            st0b, dst1b = st0.astype(BF16), dst1.astype(BF16)
            a = jnp.where(causal, _dot_nt(qs, ks), 0.0).astype(BF16)
            da = jnp.where(causal, _dot_nt(dob, vb), 0.0).astype(BF16)
            dv = _dot_tn(a, dob) + _dot_nt(ke, dst1b)
            dqs = _dot(da, ks)
            dks = _dot_tn(da, qs)
            dqb = _dot(dob, st0b)
            dke = _dot(vb, dst1b)
            ke_r = ke.astype(F32)
            db = dqs * qs.astype(F32) - dks * ks.astype(F32) + dqb * qb.astype(F32) - dke * ke_r
            e_l = jnp.exp(bl)
            dbl = jnp.sum(dke * ke_r, axis=0, keepdims=True) + jnp.sum(dst1 * st0, axis=0, keepdims=True) * e_l
            dlg = _dot_exact(tri_up, db) + dbl
            dkey = dks * e_rb + dke * e_lb
            dqy = dqs * e_br + dqb * e_b
            dfg = dlg / fg - dkey
            dlb_ref[...] += jnp.sum(dfg * (1.0 - sig), axis=0, keepdims=True)
            dstate[...] = _dot_tn(dob, qb) + dst1 * e_l
            dp_ref[0, rows, :] = (dqy * (sq * (1.0 + qz * (1.0 - sq)))).astype(BF16)
            dp_ref[1, rows, :] = (dfg * oml * sig * (1.0 - sig)).astype(BF16)
            dp_ref[2, rows, :] = dv.astype(BF16)
            dp_ref[3, rows, :] = dgz.astype(BF16)
            return carry

        lax.fori_loop(0, ncb, chunk, 0)

    def part(p):
        return pl.BlockSpec((tb, HEAD), functools.partial(lambda h, i, p: (nb - 1 - i, p * H + h), p=p))

    blk = pl.BlockSpec((tb, HEAD), lambda h, i: (nb - 1 - i, h))
    acc = pl.BlockSpec((None, 1, HEAD), lambda h, i: (h, 0, 0))
    return pl.pallas_call(
        body, name="hgrn_bwd", grid=(H, nb),
        in_specs=[part(0), part(1), part(2), part(3),
                  pl.BlockSpec((3, HEAD), lambda h, i: (0, h)),
                  pl.BlockSpec((1, HEAD), lambda h, i: (0, 0)),
                  blk,
                  pl.BlockSpec((None, ncb, HEAD, HEAD), lambda h, i: (h, nb - 1 - i, 0, 0)),
                  blk],
        out_specs=[pl.BlockSpec((4, tb, HEAD), lambda h, i: (0, nb - 1 - i, h)), acc, acc],
        out_shape=[jax.ShapeDtypeStruct((4, S, H * HEAD), BF16),
                   jax.ShapeDtypeStruct((H, 1, HEAD), F32),
                   jax.ShapeDtypeStruct((H, 1, HEAD), F32)],
        scratch_shapes=[pltpu.VMEM((HEAD, HEAD), F32)],
        compiler_params=_params(("parallel", "arbitrary")))(proj, proj, proj, proj, logits, gain, o, states, dog)


def _rope(v, cos, sin):
    return v * cos + pltpu.roll(v, HEAD // 2, 1) * sin


def _band_masks():
    qi = lax.broadcasted_iota(jnp.int32, (ATTN_SPAN, ATTN_SPAN), 0)
    kj = lax.broadcasted_iota(jnp.int32, (ATTN_SPAN, ATTN_SPAN), 1)
    return kj <= qi, kj >= qi


def _attn_fwd(a, cos, sin):
    d, L, _ = a.shape
    nb = L // ATTN_SPAN
    scale = HEAD ** -0.5

    def body(q_ref, kc_ref, kp_ref, vc_ref, vp_ref, cc_ref, cp_ref, sc_ref, sp_ref, o_ref, lse_ref):
        n = pl.program_id(1)
        mask_c, mask_p0 = _band_masks()
        mask_p = jnp.logical_and(mask_p0, n > 0)
        cc, cp, sc, sp = cc_ref[...], cp_ref[...], sc_ref[...], sp_ref[...]
        for hh in range(HEADS_PER_GROUP):
            cols = slice(hh * HEAD, (hh + 1) * HEAD)
            q = _rope(q_ref[:, cols], cc, sc).astype(BF16)
            kc = _rope(kc_ref[:, cols], cc, sc).astype(BF16)
            kp = _rope(kp_ref[:, cols], cp, sp).astype(BF16)
            s_c = jnp.where(mask_c, _dot_nt(q, kc) * scale, NEG)
            s_p = jnp.where(mask_p, _dot_nt(q, kp) * scale, NEG)
            m = jnp.maximum(jnp.max(s_c, axis=-1, keepdims=True), jnp.max(s_p, axis=-1, keepdims=True))
            p_c = jnp.exp(s_c - m)
            p_p = jnp.exp(s_p - m)
            l = jnp.sum(p_c, axis=-1, keepdims=True) + jnp.sum(p_p, axis=-1, keepdims=True)
            acc = _dot(p_c.astype(BF16), vc_ref[:, cols].astype(BF16)) + _dot(p_p.astype(BF16), vp_ref[:, cols].astype(BF16))
            o_ref[:, cols] = acc / l
            lse_ref[:, cols] = jnp.broadcast_to(m + jnp.log(l), (ATTN_SPAN, HEAD))

    def blk(part, prev):
        if prev:
            return pl.BlockSpec((None, ATTN_SPAN, GROUP_W), functools.partial(lambda r, n, p: (r, jnp.maximum(n - 1, 0), p), p=part))
        return pl.BlockSpec((None, ATTN_SPAN, GROUP_W), functools.partial(lambda r, n, p: (r, n, p), p=part))

    tab_c = pl.BlockSpec((None, ATTN_SPAN, HEAD), lambda r, n: (r, n, 0))
    tab_p = pl.BlockSpec((None, ATTN_SPAN, HEAD), lambda r, n: (r, jnp.maximum(n - 1, 0), 0))
    out = pl.BlockSpec((None, ATTN_SPAN, GROUP_W), lambda r, n: (r, n, 0))
    return pl.pallas_call(
        body, name=f"attn_fwd_d{d}", grid=(d, nb),
        in_specs=[blk(0, False), blk(1, False), blk(1, True), blk(2, False), blk(2, True), tab_c, tab_p, tab_c, tab_p],
        out_specs=[out, out],
        out_shape=[jax.ShapeDtypeStruct((d, L, GROUP_W), F32), jax.ShapeDtypeStruct((d, L, GROUP_W), F32)],
        compiler_params=_params(("parallel", "arbitrary")))(a, a, a, a, a, cos, cos, sin, sin)


def _attn_bwd(a, cos, sin, do, lse, dd):
    d, L, _ = a.shape
    nb = L // ATTN_SPAN
    scale = HEAD ** -0.5

    def body(qc_ref, qn_ref, kp_ref, kc_ref, vp_ref, vc_ref, doc_ref, don_ref, lc_ref, ln_ref, ddc_ref, ddn_ref,
             cp_ref, cc_ref, cn_ref, sp_ref, sc_ref, sn_ref, da_ref):
        n = pl.program_id(1)
        mask_c, mask_p0 = _band_masks()
        mask_p = jnp.logical_and(mask_p0, n > 0)
        mask_n = jnp.logical_and(mask_p0, n < nb - 1)
        cp, cc, cn, sp, sc, sn = cp_ref[...], cc_ref[...], cn_ref[...], sp_ref[...], sc_ref[...], sn_ref[...]
        for hh in range(HEADS_PER_GROUP):
            cols = slice(hh * HEAD, (hh + 1) * HEAD)
            q = _rope(qc_ref[:, cols], cc, sc).astype(BF16)
            qn = _rope(qn_ref[:, cols], cn, sn).astype(BF16)
            kc = _rope(kc_ref[:, cols], cc, sc).astype(BF16)
            kp = _rope(kp_ref[:, cols], cp, sp).astype(BF16)
            vc = vc_ref[:, cols].astype(BF16)
            vp = vp_ref[:, cols].astype(BF16)
            do_c = doc_ref[:, cols].astype(BF16)
            do_n = don_ref[:, cols].astype(BF16)
            lse_c, lse_n = lc_ref[:, cols], ln_ref[:, cols]
            dd_c, dd_n = ddc_ref[:, cols], ddn_ref[:, cols]
            p_c = jnp.where(mask_c, jnp.exp(_dot_nt(q, kc) * scale - lse_c), 0.0)
            p_p = jnp.where(mask_p, jnp.exp(_dot_nt(q, kp) * scale - lse_c), 0.0)
            ds_c = (p_c * (_dot_nt(do_c, vc) + dd_c)).astype(BF16)
            ds_p = (p_p * (_dot_nt(do_c, vp) + dd_c)).astype(BF16)
            dq = (_dot(ds_c, kc) + _dot(ds_p, kp)) * scale
            p_n = jnp.where(mask_n, jnp.exp(_dot_nt(qn, kc) * scale - lse_n), 0.0)
            ds_n = (p_n * (_dot_nt(do_n, vc) + dd_n)).astype(BF16)
            dk = (_dot_tn(ds_c, q) + _dot_tn(ds_n, qn)) * scale
            dv = _dot_tn(p_c.astype(BF16), do_c) + _dot_tn(p_n.astype(BF16), do_n)
            da_ref[:, cols] = _rope(dq, cc, -sc).astype(BF16)
            da_ref[:, GROUP_W + hh * HEAD:GROUP_W + (hh + 1) * HEAD] = _rope(dk, cc, -sc).astype(BF16)
            da_ref[:, 2 * GROUP_W + hh * HEAD:2 * GROUP_W + (hh + 1) * HEAD] = dv.astype(BF16)

    def rel(delta):
        if delta < 0:
            return lambda n: jnp.maximum(n - 1, 0)
        if delta > 0:
            return lambda n: jnp.minimum(n + 1, nb - 1)
        return lambda n: n

    def blk(width, part, delta):
        f = rel(delta)
        return pl.BlockSpec((None, ATTN_SPAN, width), functools.partial(lambda r, n, p, f: (r, f(n), p), p=part, f=f))

    g = GROUP_W
    return pl.pallas_call(
        body, name=f"attn_bwd_d{d}", grid=(d, nb),
        in_specs=[blk(g, 0, 0), blk(g, 0, 1), blk(g, 1, -1), blk(g, 1, 0), blk(g, 2, -1), blk(g, 2, 0),
                  blk(g, 0, 0), blk(g, 0, 1), blk(g, 0, 0), blk(g, 0, 1), blk(g, 0, 0), blk(g, 0, 1),
                  blk(HEAD, 0, -1), blk(HEAD, 0, 0), blk(HEAD, 0, 1), blk(HEAD, 0, -1), blk(HEAD, 0, 0), blk(HEAD, 0, 1)],
        out_specs=pl.BlockSpec((None, ATTN_SPAN, 3 * g), lambda r, n: (r, n, 0)),
        out_shape=jax.ShapeDtypeStruct((d, L, 3 * g), BF16),
        compiler_params=_params(("parallel", "arbitrary")))(
            a, a, a, a, a, a, do, do, lse, lse, dd, dd, cos, cos, cos, sin, sin, sin)


def _group_weights(lse_refs, cols):
    ls = [r[:, cols] for r in lse_refs]
    mx = jnp.maximum(jnp.maximum(ls[0], ls[1]), ls[2])
    es = [jnp.exp(v - mx) for v in ls]
    tot = es[0] + es[1] + es[2]
    return [e / tot for e in es]


def _attn_merge(os_, lses, tm=512):
    S = os_[0].shape[0]
    G = len(os_)

    def body(*refs):
        o_refs, l_refs, out_ref = refs[:G], refs[G:2 * G], refs[2 * G]
        for hh in range(HEADS_PER_GROUP):
            cols = slice(hh * HEAD, (hh + 1) * HEAD)
            al = _group_weights(l_refs, cols)
            for g in range(G):
                out_ref[:, g * GROUP_W + hh * HEAD:g * GROUP_W + (hh + 1) * HEAD] = (o_refs[g][:, cols] * al[g]).astype(BF16)

    blk = pl.BlockSpec((tm, GROUP_W), lambda i: (i, 0))
    return pl.pallas_call(
        body, name="attn_merge", grid=(S // tm,),
        in_specs=[blk] * (2 * G),
        out_specs=pl.BlockSpec((tm, G * GROUP_W), lambda i: (i, 0)),
        out_shape=jax.ShapeDtypeStruct((S, G * GROUP_W), BF16),
        compiler_params=_params(("parallel",)))(*os_, *lses)


def _attn_merge_bwd(os_, lses, doa, tm=512):
    S = os_[0].shape[0]
    G = len(os_)

    def body(*refs):
        o_refs, l_refs, doa_ref = refs[:G], refs[G:2 * G], refs[2 * G]
        do_refs, dd_refs = refs[2 * G + 1:3 * G + 1], refs[3 * G + 1:]
        for hh in range(HEADS_PER_GROUP):
            cols = slice(hh * HEAD, (hh + 1) * HEAD)
            al = _group_weights(l_refs, cols)
            mix = None
            for g in range(G):
                dg = doa_ref[:, g * GROUP_W + hh * HEAD:g * GROUP_W + (hh + 1) * HEAD]
                do_refs[g][:, cols] = dg * al[g]
                t = al[g] * jnp.sum(dg * o_refs[g][:, cols], axis=-1, keepdims=True)
                mix = t if mix is None else mix + t
            for g in range(G):
                dd_refs[g][:, cols] = -al[g] * mix

    blk = pl.BlockSpec((tm, GROUP_W), lambda i: (i, 0))
    shp = jax.ShapeDtypeStruct((S, GROUP_W), F32)
    return pl.pallas_call(
        body, name="attn_merge_bwd", grid=(S // tm,),
        in_specs=[blk] * (2 * G) + [pl.BlockSpec((tm, G * GROUP_W), lambda i: (i, 0))],
        out_specs=[blk] * (2 * G),
        out_shape=[shp] * (2 * G),
        compiler_params=_params(("parallel",)))(*os_, *lses, doa)


def _to_dilated(v, d):
    S, W = v.shape
    return v.reshape(S // d, d, W).transpose(1, 0, 2)


def _from_dilated(v):
    d, L, W = v.shape
    return v.transpose(1, 0, 2).reshape(L * d, W)


def _rope_tables(S):
    inv_freq = 1.0 / (ROPE_THETA ** (jnp.arange(0, HEAD, 2, dtype=F32) / HEAD))
    ang = jnp.arange(S, dtype=F32)[:, None] * inv_freq[None, :]
    cos, sin = jnp.cos(ang), jnp.sin(ang)
    return jnp.concatenate([cos, cos], axis=-1), jnp.concatenate([-sin, sin], axis=-1)


def _local_step(x, target, norm_mix, norm_ffn, lb_logits, out_gain, final_norm, w):
    S = x.shape[0]
    nm0, nm1 = norm_mix[0:1], norm_mix[1:2]
    nf0, nf1 = norm_ffn[0:1], norm_ffn[1:2]
    fin_tn = w["fin0"].shape[2]

    proj = _norm_mm(x, nm0, w["hin"], 512, "hgrn_in")
    o, og, states = _hgrn_fwd(proj, lb_logits, out_gain)
    h1 = _mm_res(x, og, w["hout"], "hgrn_out")
    z0 = _norm_mm(h1, nf0, w["fin0"], fin_tn, "ffn0_in")
    h2 = _swiglu_mm_res(h1, z0, w["fdn0"], "ffn0_down")
    qkv = _norm_mm(h2, nm1, w["qkv"], 384, "attn_qkv")
    cos, sin = _rope_tables(S)
    qkv4 = qkv.reshape(S, 3, len(ATTN_GROUPS), GROUP_W)
    a_g, cos_g, sin_g, o_g, lse_g = [], [], [], [], []
    for gi, (_, d) in enumerate(ATTN_GROUPS):
        a = _to_dilated(qkv4[:, :, gi, :].reshape(S, 3 * GROUP_W), d)
        cd, sd = _to_dilated(cos, d), _to_dilated(sin, d)
        og_, lse_ = _attn_fwd(a, cd, sd)
        a_g.append(a), cos_g.append(cd), sin_g.append(sd)
        o_g.append(_from_dilated(og_)), lse_g.append(_from_dilated(lse_))
    oa = _attn_merge(o_g, lse_g)
    h3 = _mm_res(h2, oa, w["aout"], "attn_out")
    z1 = _norm_mm(h3, nf1, w["fin1"], fin_tn, "ffn1_in")
    h4 = _swiglu_mm_res(h3, z1, w["fdn1"], "ffn1_down")
    dh4, loss, d_final = _loss_head(h4, final_norm, target)

    grads, small = {}, {"final_norm": d_final}

    def ffn_bwd(dh, h_in, z, gain, w_in, w_dn, tag):
        dz = _mm_nt_swiglu_bwd(dh, w_dn, z, tag + "_down_dx")
        g_dn = _mm_tn("swiglu", (z,), dh, 1, D_MODEL, 512, tag + "_down_dw")
        g_in = _mm_tn("norm", (h_in, gain), dz, N_CHIPS, fin_tn, fin_tn, tag + "_in_dw")
        dh_in, dgain = _mm_nt_normbwd(dz, w_in, h_in, gain, dh, tag + "_in_dx")
        return dh_in, dgain, g_in, g_dn[0]

    dh3, d_nf1, grads["fin1"], grads["fdn1"] = ffn_bwd(dh4, h3, z1, nf1, w["fin1"], w["fdn1"], "ffn1")
    doa = _mm_nt(dh3, w["aout"][None], "attn_out_dx")
    grads["aout"] = _mm_tn("plain", (oa,), dh3, 1, D_MODEL, 512, "attn_out_dw")[0]
    merged = _attn_merge_bwd(o_g, lse_g, doa)
    parts = []
    for gi, (_, d) in enumerate(ATTN_GROUPS):
        do_d, dd_d = _to_dilated(merged[gi], d), _to_dilated(merged[len(ATTN_GROUPS) + gi], d)
        lse_d = _to_dilated(lse_g[gi], d)
        da = _attn_bwd(a_g[gi], cos_g[gi], sin_g[gi], do_d, lse_d, dd_d)
        parts.append(_from_dilated(da).reshape(S, 3, 1, GROUP_W))
    dqkv = jnp.concatenate(parts, axis=2).reshape(S, 3 * len(ATTN_GROUPS) * GROUP_W)
    n_qkv = w["qkv"].shape[2]
    grads["qkv"] = _mm_tn("norm", (h2, nm1), dqkv, N_CHIPS, n_qkv, 384, "attn_qkv_dw")
    dh2, d_nm1 = _mm_nt_normbwd(dqkv, w["qkv"], h2, nm1, dh3, "attn_qkv_dx")

    dh1, d_nf0, grads["fin0"], grads["fdn0"] = ffn_bwd(dh2, h1, z0, nf0, w["fin0"], w["fdn0"], "ffn0")
    dog = _mm_nt(dh1, w["hout"][None], "hgrn_out_dx")
    grads["hout"] = _mm_tn("plain", (og,), dh1, 1, D_MODEL, 512, "hgrn_out_dw")[0]
    dproj, dlb, dgn = _hgrn_bwd(proj, lb_logits, out_gain, o, states, dog)
    grads["hin"] = _mm_tn("norm", (x, nm0), dproj, N_CHIPS, D_MODEL, 512, "hgrn_in_dw")
    dx, d_nm0 = _mm_nt_normbwd(dproj, w["hin"], x, nm0, dh1, "hgrn_in_dx")

    small["norm_mix"] = jnp.concatenate([d_nm0, d_nm1], axis=0)
    small["norm_ffn"] = jnp.concatenate([d_nf0, d_nf1], axis=0)
    small["lb"] = dlb.reshape(1, HGRN_HEADS * HEAD)
    small["out_norm"] = dgn.reshape(HGRN_HEADS, HEAD)
    return loss, dx, grads, small


def _place():
    x, y, c = lax.axis_index("x"), lax.axis_index("y"), lax.axis_index("c")
    others = [(1 - x, y), (x, 1 - y), (1 - x, 1 - y)]
    return x, y, c, others


ANY = pl.BlockSpec(memory_space=pl.ANY)


def _gather_weights(shards):
    n = len(shards)

    def body(*refs):
        ins, outs = refs[:n], refs[n:2 * n]
        ici_send, ici_recv, d2d_send, d2d_recv, loc_sem = refs[2 * n:]
        x, y, c, others = _place()
        me = 2 * x + y
        sibling = (x, y, 1 - c)
        local = [pltpu.make_async_copy(ins[a], outs[a].at[me], loc_sem.at[a]) for a in range(n)]
        for cp in local:
            cp.start()
        sends, passes = [], []
        for a in range(n):
            for k, (ox, oy) in enumerate(others):
                s = a * 3 + k
                sends.append(pltpu.make_async_remote_copy(
                    src_ref=ins[a].at[c], dst_ref=outs[a].at[me, c], send_sem=ici_send.at[s], recv_sem=ici_recv.at[s],
                    device_id=(ox, oy, c), device_id_type=MESH))
        for cp in sends:
            cp.start()
        for a in range(n):
            for k, (ox, oy) in enumerate(others):
                s = a * 3 + k
                got = outs[a].at[2 * ox + oy, c]
                pltpu.make_async_remote_copy(
                    src_ref=got, dst_ref=got, send_sem=ici_send.at[s], recv_sem=ici_recv.at[s],
                    device_id=(ox, oy, c), device_id_type=MESH).wait_recv()
                fwd = pltpu.make_async_remote_copy(
                    src_ref=got, dst_ref=got, send_sem=d2d_send.at[s], recv_sem=d2d_recv.at[s],
                    device_id=sibling, device_id_type=MESH)
                fwd.start()
                passes.append(fwd)
        for a in range(n):
            for k, (ox, oy) in enumerate(others):
                s = a * 3 + k
                theirs = outs[a].at[2 * ox + oy, 1 - c]
                pltpu.make_async_remote_copy(
                    src_ref=theirs, dst_ref=theirs, send_sem=d2d_send.at[s], recv_sem=d2d_recv.at[s],
                    device_id=sibling, device_id_type=MESH).wait_recv()
        for cp in sends + passes:
            cp.wait_send()
        for cp in local:
            cp.wait()

    return pl.pallas_call(
        body, name="gather_weights",
        in_specs=[ANY] * n, out_specs=[ANY] * n,
        out_shape=[jax.ShapeDtypeStruct((N_CHIPS,) + s.shape, s.dtype) for s in shards],
        scratch_shapes=[pltpu.SemaphoreType.DMA((3 * n,)), pltpu.SemaphoreType.DMA((3 * n,)),
                        pltpu.SemaphoreType.DMA((3 * n,)), pltpu.SemaphoreType.DMA((3 * n,)),
                        pltpu.SemaphoreType.DMA((n,))],
        )(*shards)


def _pair_exchange(grads):
    n = len(grads)

    def body(*refs):
        ins, outs = refs[:n], refs[n:2 * n]
        send_sem, recv_sem = refs[2 * n:]
        x, y, c, _ = _place()
        sibling = (x, y, 1 - c)
        cps = []
        for a in range(n):
            for j in range(N_CHIPS):
                s = a * N_CHIPS + j
                cps.append(pltpu.make_async_remote_copy(
                    src_ref=ins[a].at[j, 1 - c], dst_ref=outs[a].at[j], send_sem=send_sem.at[s], recv_sem=recv_sem.at[s],
                    device_id=sibling, device_id_type=MESH))
        for cp in cps:
            cp.start()
        for cp in cps:
            cp.wait()

    return pl.pallas_call(
        body, name="grad_pair_exchange",
        in_specs=[ANY] * n, out_specs=[ANY] * n,
        out_shape=[jax.ShapeDtypeStruct((N_CHIPS,) + g.shape[2:], F32) for g in grads],
        scratch_shapes=[pltpu.SemaphoreType.DMA((N_CHIPS * n,)), pltpu.SemaphoreType.DMA((N_CHIPS * n,))],
        )(*grads)


def _pair_sum(g, got, c_idx):
    _, _, r, cw = g.shape
    tr = _row_tile(r, cw)

    def body(c_ref, g_ref, got_ref, p_ref, pb_ref):
        v = g_ref[...] + got_ref[...]
        p_ref[...] = v
        pb_ref[...] = v.astype(BF16)

    blk = pl.BlockSpec((None, tr, cw), lambda j, i, c_ref: (j, i, 0))
    return pl.pallas_call(
        body, name="grad_pair_sum",
        grid_spec=pltpu.PrefetchScalarGridSpec(
            num_scalar_prefetch=1, grid=(N_CHIPS, r // tr),
            in_specs=[pl.BlockSpec((None, None, tr, cw), lambda j, i, c_ref: (j, c_ref[0], i, 0)), blk],
            out_specs=[blk, blk]),
        out_shape=[jax.ShapeDtypeStruct((N_CHIPS, r, cw), F32), jax.ShapeDtypeStruct((N_CHIPS, r, cw), BF16)],
        compiler_params=_params(("parallel", "parallel")))(c_idx, g, got)


def _chip_exchange(parts):
    n = len(parts)

    def body(*refs):
        ins, outs = refs[:n], refs[n:2 * n]
        send_sem, recv_sem = refs[2 * n:]
        x, y, c, others = _place()
        me = 2 * x + y
        cps = []
        for a in range(n):
            for k, (ox, oy) in enumerate(others):
                s = a * 3 + k
                cps.append(pltpu.make_async_remote_copy(
                    src_ref=ins[a].at[2 * ox + oy], dst_ref=outs[a].at[me], send_sem=send_sem.at[s], recv_sem=recv_sem.at[s],
                    device_id=(ox, oy, c), device_id_type=MESH))
        for cp in cps:
            cp.start()
        for a in range(n):
            for k, (ox, oy) in enumerate(others):
                s = a * 3 + k
                got = outs[a].at[2 * ox + oy]
                pltpu.make_async_remote_copy(
                    src_ref=got, dst_ref=got, send_sem=send_sem.at[s], recv_sem=recv_sem.at[s],
                    device_id=(ox, oy, c), device_id_type=MESH).wait_recv()
        for cp in cps:
            cp.wait_send()

    return pl.pallas_call(
        body, name="grad_chip_exchange",
        in_specs=[ANY] * n, out_specs=[ANY] * n,
        out_shape=[jax.ShapeDtypeStruct(p.shape, BF16) for p in parts],
        scratch_shapes=[pltpu.SemaphoreType.DMA((3 * n,)), pltpu.SemaphoreType.DMA((3 * n,))],
        )(*parts)


def _chip_sum(p, got, me_idx):
    _, r, cw = p.shape
    tr = _row_tile(r, cw)

    def body(me_ref, own_ref, got_ref, t_ref):
        me = me_ref[0]
        acc = None
        for s in range(N_CHIPS):
            term = jnp.where(me == s, own_ref[...], got_ref[s].astype(F32))
            acc = term if acc is None else acc + term
        t_ref[...] = acc

    return pl.pallas_call(
        body, name="grad_chip_sum",
        grid_spec=pltpu.PrefetchScalarGridSpec(
            num_scalar_prefetch=1, grid=(r // tr,),
            in_specs=[pl.BlockSpec((None, tr, cw), lambda i, me_ref: (me_ref[0], i, 0)),
                      pl.BlockSpec((N_CHIPS, tr, cw), lambda i, me_ref: (0, i, 0))],
            out_specs=pl.BlockSpec((tr, cw), lambda i, me_ref: (i, 0))),
        out_shape=jax.ShapeDtypeStruct((r, cw), F32),
        compiler_params=_params(("parallel",)))(me_idx, p, got)


def _pair_share(halves):
    n = len(halves)

    def body(*refs):
        ins, outs = refs[:n], refs[n:2 * n]
        send_sem, recv_sem, loc_sem = refs[2 * n:]
        x, y, c, _ = _place()
        sibling = (x, y, 1 - c)
        local = [pltpu.make_async_copy(ins[a], outs[a].at[c], loc_sem.at[a]) for a in range(n)]
        cps = [pltpu.make_async_remote_copy(
            src_ref=ins[a], dst_ref=outs[a].at[c], send_sem=send_sem.at[a], recv_sem=recv_sem.at[a],
            device_id=sibling, device_id_type=MESH) for a in range(n)]
        for cp in local + cps:
            cp.start()
        for a in range(n):
            theirs = outs[a].at[1 - c]
            pltpu.make_async_remote_copy(
                src_ref=theirs, dst_ref=theirs, send_sem=send_sem.at[a], recv_sem=recv_sem.at[a],
                device_id=sibling, device_id_type=MESH).wait_recv()
        for cp in cps:
            cp.wait_send()
        for cp in local:
            cp.wait()

    return pl.pallas_call(
        body, name="grad_pair_share",
        in_specs=[ANY] * n, out_specs=[ANY] * n,
        out_shape=[jax.ShapeDtypeStruct((2,) + h.shape, F32) for h in halves],
        scratch_shapes=[pltpu.SemaphoreType.DMA((n,)), pltpu.SemaphoreType.DMA((n,)), pltpu.SemaphoreType.DMA((n,))],
        )(*halves)


def _small_allreduce(pack):
    m_per, ncol = pack.shape
    n_dev = 8

    def body(x_ref, sum_ref, all_ref, send_sems, recv_sems, local_sem):
        x, y, c, others = _place()
        me, sibling = (x, y, c), (x, y, 1 - c)

        def rows(px, py, pc):
            return all_ref.at[pl.ds((4 * px + 2 * py + pc) * m_per, m_per), :]

        def copy(k, block, to, src=None):
            return pltpu.make_async_remote_copy(
                src_ref=rows(*block) if src is None else src, dst_ref=rows(*block),
                send_sem=send_sems.at[k], recv_sem=recv_sems.at[k], device_id=to, device_id_type=MESH)

        mine = pltpu.make_async_copy(x_ref, rows(*me), local_sem)
        mine.start()
        first = [copy(0, me, sibling, src=x_ref)]
        first += [copy(1 + j, me, (*chip, c), src=x_ref) for j, chip in enumerate(others)]
        for cp in first:
            cp.start()
        passed = [copy(4 + j, (*chip, c), sibling) for j, chip in enumerate(others)]
        for j, chip in enumerate(others):
            copy(1 + j, (*chip, c), me).wait_recv()
            passed[j].start()
        copy(0, sibling, me).wait_recv()
        for j, chip in enumerate(others):
            copy(4 + j, (*chip, 1 - c), me).wait_recv()
        for cp in first + passed:
            cp.wait_send()
        mine.wait()
        acc = all_ref[0:m_per, :]
        for dvc in range(1, n_dev):
            acc = acc + all_ref[dvc * m_per:(dvc + 1) * m_per, :]
        sum_ref[...] = acc

    return pl.pallas_call(
        body, name="small_allreduce",
        in_specs=[pl.BlockSpec(memory_space=pltpu.VMEM)],
        out_specs=pl.BlockSpec(memory_space=pltpu.VMEM),
        out_shape=jax.ShapeDtypeStruct((m_per, ncol), F32),
        scratch_shapes=[pltpu.VMEM((n_dev * m_per, ncol), F32),
                        pltpu.SemaphoreType.DMA((7,)), pltpu.SemaphoreType.DMA((7,)), pltpu.SemaphoreType.DMA],
        )(pack)


def _adam_math(w, g, m, v):
    m = ADAM_B1 * m + (1.0 - ADAM_B1) * g
    v = ADAM_B2 * v + (1.0 - ADAM_B2) * (g * g)
    m_hat = m / (1.0 - ADAM_B1 ** ADAM_STEP)
    v_hat = v / (1.0 - ADAM_B2 ** ADAM_STEP)
    delta = -ADAM_LR * (m_hat / (jnp.sqrt(v_hat) + ADAM_EPS) + ADAM_WD * w)
    return delta, m, v


def _adamw(g, w, m, v, name):
    R, C = g.shape
    tr = _row_tile(R, C, 512 * 1024)

    def body(g_ref, w_ref, m_ref, v_ref, d_ref, nm_ref, nv_ref):
        d_ref[...], nm_ref[...], nv_ref[...] = _adam_math(w_ref[...], g_ref[...], m_ref[...], v_ref[...])

    blk = pl.BlockSpec((tr, C), lambda i: (i, 0))
    shp = jax.ShapeDtypeStruct((R, C), F32)
    return pl.pallas_call(
        body, name=name, grid=(R // tr,),
        in_specs=[blk] * 4, out_specs=[blk] * 3, out_shape=[shp] * 3,
        compiler_params=_params(("parallel",)))(g, w, m, v)


def _small_update(gsum, logits_pack, w, m, v):
    def body(gs_ref, lg_ref, w_ref, m_ref, v_ref, g_ref, d_ref, nm_ref, nv_ref):
        g_ref[...] = gs_ref[...]
        l0, l1, l2 = lg_ref[0:1, :], lg_ref[1:2, :], lg_ref[2:3, :]
        mx = jnp.maximum(jnp.maximum(l0, l1), l2)
        e0, e1, e2 = jnp.exp(l0 - mx), jnp.exp(l1 - mx), jnp.exp(l2 - mx)
        tot = e0 + e1 + e2
        p0, p1, p2 = e0 / tot, e1 / tot, e2 / tot
        dlb = gs_ref[4:5, :]
        g_ref[4:5, :] = dlb * p0 * (1.0 - p0)
        g_ref[5:6, :] = -dlb * p0 * p1
        g_ref[6:7, :] = -dlb * p0 * p2
        d_ref[...], nm_ref[...], nv_ref[...] = _adam_math(w_ref[...], g_ref[...], m_ref[...], v_ref[...])

    full = pl.BlockSpec(memory_space=pltpu.VMEM)
    shp = jax.ShapeDtypeStruct(gsum.shape, F32)
    return pl.pallas_call(
        body, name="small_update", in_specs=[full] * 5, out_specs=[full] * 4, out_shape=[shp] * 4)(
            gsum, logits_pack, w, m, v)


def _pack_small(norm_mix, norm_ffn, lb3, out_norm, final_norm, extra=None):
    ncol = norm_mix.shape[1]
    on = jnp.pad(out_norm.reshape(1, -1), ((0, 0), (0, ncol - out_norm.size)))
    rows = [norm_mix, norm_ffn, lb3, on, final_norm.reshape(1, ncol)]
    if extra is not None:
        rows.append(extra)
    used = sum(r.shape[0] for r in rows)
    rows.append(jnp.zeros((SMALL_ROWS - used, ncol), F32))
    return jnp.concatenate(rows, axis=0)


WEIGHT_NAMES = ("hin", "hout", "qkv", "aout", "fin0", "fin1", "fdn0", "fdn1")


def _split_weights(hgrn_w_in, hgrn_w_out, attn_w_qkv, attn_w_out, ffn_w_in, ffn_w_down):
    return {"hin": hgrn_w_in[0], "hout": hgrn_w_out[0], "qkv": attn_w_qkv[0], "aout": attn_w_out[0],
            "fin0": ffn_w_in[0], "fin1": ffn_w_in[1], "fdn0": ffn_w_down[0], "fdn1": ffn_w_down[1]}


def _halves(v):
    r, c = v.shape
    return v.reshape(2, r // 2, c)


def _full_weights(gathered):
    out = {}
    for k, g in gathered.items():
        _, _, r, c = g.shape
        if k in ("hin", "qkv", "fin0", "fin1"):
            out[k] = g.reshape(N_CHIPS, 2 * r, c)
        else:
            out[k] = g.reshape(N_CHIPS * 2 * r, c)
    return out


def kernel(x, norm_mix, norm_ffn, hgrn_w_in, hgrn_lb_logits, hgrn_out_norm, hgrn_w_out, attn_w_qkv, attn_w_out, ffn_w_in, ffn_w_down, final_norm, loss_target, m_norm_mix, m_norm_ffn, m_hgrn_w_in, m_hgrn_lb_logits, m_hgrn_out_norm, m_hgrn_w_out, m_attn_w_qkv, m_attn_w_out, m_ffn_w_in, m_ffn_w_down, m_final_norm, v_norm_mix, v_norm_ffn, v_hgrn_w_in, v_hgrn_lb_logits, v_hgrn_out_norm, v_hgrn_w_out, v_attn_w_qkv, v_attn_w_out, v_ffn_w_in, v_ffn_w_down, v_final_norm):
    S = x.shape[1]
    xi, yi, ci = lax.axis_index("x"), lax.axis_index("y"), lax.axis_index("c")
    c_idx = jnp.reshape(ci, (1,)).astype(jnp.int32)
    me_idx = jnp.reshape(2 * xi + yi, (1,)).astype(jnp.int32)

    w_own = _split_weights(hgrn_w_in, hgrn_w_out, attn_w_qkv, attn_w_out, ffn_w_in, ffn_w_down)
    m_own = _split_weights(m_hgrn_w_in, m_hgrn_w_out, m_attn_w_qkv, m_attn_w_out, m_ffn_w_in, m_ffn_w_down)
    v_own = _split_weights(v_hgrn_w_in, v_hgrn_w_out, v_attn_w_qkv, v_attn_w_out, v_ffn_w_in, v_ffn_w_down)

    gathered = _gather_weights([_halves(w_own[k].astype(BF16)) for k in WEIGHT_NAMES])
    w_full = _full_weights(dict(zip(WEIGHT_NAMES, gathered)))

    loss, dx, grads, small = _local_step(
        x.reshape(S, D_MODEL), loss_target.reshape(S, D_MODEL), norm_mix, norm_ffn, hgrn_lb_logits,
        hgrn_out_norm, final_norm.reshape(1, D_MODEL), w_full)

    g4 = []
    for k in WEIGHT_NAMES:
        r, c = w_own[k].shape
        g4.append(grads[k].reshape(N_CHIPS, 2, r // 2, c))
    from_sibling = _pair_exchange(g4)
    sums = [_pair_sum(g, got, c_idx) for g, got in zip(g4, from_sibling)]
    from_chips = _chip_exchange([s[1] for s in sums])
    halves = [_chip_sum(s[0], got, me_idx) for s, got in zip(sums, from_chips)]
    shared = _pair_share(halves)

    g_out, d_out, m_out, v_out = {}, {}, {}, {}
    for k, sh in zip(WEIGHT_NAMES, shared):
        r, c = w_own[k].shape
        g = sh.reshape(r, c)
        g_out[k] = g
        d_out[k], m_out[k], v_out[k] = _adamw(g, w_own[k], m_own[k], v_own[k], "adamw_" + k)

    loss_row = jnp.pad(loss, ((0, 0), (0, D_MODEL - loss.shape[1])))
    lb3 = jnp.concatenate([small["lb"], jnp.zeros((2, D_MODEL), F32)], axis=0)
    on_grad = jnp.sum(small["out_norm"], axis=0, keepdims=True)
    pack = _pack_small(small["norm_mix"], small["norm_ffn"], lb3, on_grad, small["final_norm"], loss_row)
    gsum = _small_allreduce(pack)
    w_s = _pack_small(norm_mix, norm_ffn, hgrn_lb_logits, hgrn_out_norm, final_norm)
    m_s = _pack_small(m_norm_mix, m_norm_ffn, m_hgrn_lb_logits, m_hgrn_out_norm, m_final_norm)
    v_s = _pack_small(v_norm_mix, v_norm_ffn, v_hgrn_lb_logits, v_hgrn_out_norm, v_final_norm)
    lg_pack = jnp.pad(hgrn_lb_logits, ((0, 8 - hgrn_lb_logits.shape[0]), (0, 0)))
    sg, sd, sm, sv = _small_update(gsum, lg_pack, w_s, m_s, v_s)

    def unpack(p):
        return (p[0:2], p[2:4], p[4:7], p[7:8, :HEAD], p[8])

    def big(dct):
        return (dct["hin"][None], dct["hout"][None], dct["qkv"][None], dct["aout"][None],
                jnp.stack([dct["fin0"], dct["fin1"]]), jnp.stack([dct["fdn0"], dct["fdn1"]]))

    def assemble(p, dct):
        nmx, nff, lbl, onm, fnm = unpack(p)
        hin, hout, qkv, aout, fin, fdn = big(dct)
        return (nmx, nff, hin, lbl, onm, hout, qkv, aout, fin, fdn, fnm)

    total_loss = gsum[9, 0]
    return (total_loss, dx.reshape(1, S, D_MODEL), *assemble(sg, g_out), *assemble(sd, d_out),
            *assemble(sm, m_out), *assemble(sv, v_out))
```

```python
import functools

import jax
import jax.numpy as jnp
from jax import lax
from jax.experimental import pallas as pl
from jax.experimental.pallas import tpu as pltpu

F32 = jnp.float32
BF16 = jnp.bfloat16
MESH = pl.DeviceIdType.MESH

D_MODEL = 1024
HEAD = 128
HGRN_HEADS = 8
HGRN_CHUNK = 64
ATTN_GROUPS = ((128, 1), (512, 4), (2048, 16))
ATTN_SPAN = 128
HEADS_PER_GROUP = 4
GROUP_W = HEADS_PER_GROUP * HEAD
D_FF = 2816
NORM_EPS = 1e-6
ROPE_THETA = 10000.0
NEG = -1e30

ADAM_LR, ADAM_B1, ADAM_B2, ADAM_EPS, ADAM_WD, ADAM_STEP = 0.001, 0.9, 0.999, 1e-08, 0.01, 10

N_CHIPS = 4
VMEM_LIMIT = 56 * 1024 * 1024
SMALL_ROWS = 16


def _params(sem=None):
    return pltpu.CompilerParams(dimension_semantics=sem, vmem_limit_bytes=VMEM_LIMIT)


def _row_tile(rows, cols, budget_bytes=3 * 512 * 1024):
    best = 8
    for t in range(8, rows + 1, 8):
        if rows % t == 0 and t * cols * 4 <= budget_bytes:
            best = t
    assert rows % best == 0
    return best


def _sigmoid(v):
    return 1.0 / (1.0 + jnp.exp(-v))


def _dot(a, b):
    return jnp.dot(a, b, preferred_element_type=F32)


def _dot_nt(a, b):
    return lax.dot_general(a, b, (((1,), (1,)), ((), ())), preferred_element_type=F32)


def _dot_tn(a, b):
    return lax.dot_general(a, b, (((0,), (0,)), ((), ())), preferred_element_type=F32)


def _dot_exact(a, b):
    return jnp.dot(a, b, preferred_element_type=F32, precision=lax.Precision.HIGHEST)


def _rstd(v):
    return lax.rsqrt(jnp.mean(v * v, axis=-1, keepdims=True) + NORM_EPS)


def _norm_mm(h, gain, w3, tn, name, tm=512):
    S, K = h.shape
    J, _, n = w3.shape
    tpn = n // tn

    def body(h_ref, g_ref, w_ref, y_ref, u_scr):
        @pl.when(pl.program_id(1) == 0)
        def _():
            v = h_ref[...]
            u_scr[...] = (v * _rstd(v) * g_ref[...]).astype(BF16)

        y_ref[...] = _dot(u_scr[...], w_ref[...])

    return pl.pallas_call(
        body, name=name, grid=(S // tm, J * tpn),
        in_specs=[pl.BlockSpec((tm, K), lambda i, j: (i, 0)),
                  pl.BlockSpec((1, K), lambda i, j: (0, 0)),
                  pl.BlockSpec((None, K, tn), lambda i, j: (j // tpn, 0, j % tpn))],
        out_specs=pl.BlockSpec((tm, tn), lambda i, j: (i, j)),
        out_shape=jax.ShapeDtypeStruct((S, J * n), F32),
        scratch_shapes=[pltpu.VMEM((tm, K), BF16)],
        compiler_params=_params(("parallel", "arbitrary")))(h, gain, w3)


def _mm_res(h, a, w2, name, tm=512):
    S, N = h.shape
    K = a.shape[1]

    def body(h_ref, a_ref, w_ref, o_ref):
        o_ref[...] = h_ref[...] + _dot(a_ref[...], w_ref[...])

    return pl.pallas_call(
        body, name=name, grid=(S // tm,),
        in_specs=[pl.BlockSpec((tm, N), lambda i: (i, 0)),
                  pl.BlockSpec((tm, K), lambda i: (i, 0)),
                  pl.BlockSpec((K, N), lambda i: (0, 0))],
        out_specs=pl.BlockSpec((tm, N), lambda i: (i, 0)),
        out_shape=jax.ShapeDtypeStruct((S, N), F32),
        compiler_params=_params(("parallel",)))(h, a, w2)


def _swiglu(z_ref, F):
    g = z_ref[:, :F]
    return (g * _sigmoid(g) * z_ref[:, F:]).astype(BF16)


def _swiglu_mm_res(h, z, w2, name, tm=256):
    S, N = h.shape
    F = w2.shape[0]

    def body(h_ref, z_ref, w_ref, o_ref):
        o_ref[...] = h_ref[...] + _dot(_swiglu(z_ref, F), w_ref[...])

    return pl.pallas_call(
        body, name=name, grid=(S // tm,),
        in_specs=[pl.BlockSpec((tm, N), lambda i: (i, 0)),
                  pl.BlockSpec((tm, 2 * F), lambda i: (i, 0)),
                  pl.BlockSpec((F, N), lambda i: (0, 0))],
        out_specs=pl.BlockSpec((tm, N), lambda i: (i, 0)),
        out_shape=jax.ShapeDtypeStruct((S, N), F32),
        compiler_params=_params(("parallel",)))(h, z, w2)


def _dy_specs(dy, J, n, tm):
    if dy.ndim == 3:
        return [pl.BlockSpec((None, tm, n), functools.partial(lambda i, j: (j, i, 0), j=j)) for j in range(J)]
    return [pl.BlockSpec((tm, n), functools.partial(lambda i, j: (i, j), j=j)) for j in range(J)]


def _acc_nt(dy_refs, w_ref):
    acc = None
    for j, r in enumerate(dy_refs):
        t = _dot_nt(r[...].astype(BF16), w_ref[j])
        acc = t if acc is None else acc + t
    return acc


def _mm_nt(dy, w3, name, out_dtype=F32, tm=512):
    J, K, n = w3.shape
    S = dy.shape[-2]

    def body(*refs):
        dy_refs, w_ref, o_ref = refs[:J], refs[J], refs[J + 1]
        o_ref[...] = _acc_nt(dy_refs, w_ref).astype(o_ref.dtype)

    return pl.pallas_call(
        body, name=name, grid=(S // tm,),
        in_specs=_dy_specs(dy, J, n, tm) + [pl.BlockSpec((J, K, n), lambda i: (0, 0, 0))],
        out_specs=pl.BlockSpec((tm, K), lambda i: (i, 0)),
        out_shape=jax.ShapeDtypeStruct((S, K), out_dtype),
        compiler_params=_params(("parallel",)))(*([dy] * J), w3)


def _mm_nt_normbwd(dy, w3, h, gain, dh, name, tm=512):
    J, K, n = w3.shape
    S = h.shape[0]

    def body(*refs):
        dy_refs, w_ref, h_ref, g_ref, dh_ref, o_ref, dg_ref = refs[:J], *refs[J:]
        du = _acc_nt(dy_refs, w_ref)
        v = h_ref[...]
        r = _rstd(v)
        xh = v * r
        dyg = du * g_ref[...]
        o_ref[...] = dh_ref[...] + r * (dyg - xh * jnp.mean(dyg * xh, axis=-1, keepdims=True))

        @pl.when(pl.program_id(0) == 0)
        def _():
            dg_ref[...] = jnp.zeros_like(dg_ref)

        dg_ref[...] += jnp.sum(du * xh, axis=0, keepdims=True)

    row = pl.BlockSpec((tm, K), lambda i: (i, 0))
    vec = pl.BlockSpec((1, K), lambda i: (0, 0))
    return pl.pallas_call(
        body, name=name, grid=(S // tm,),
        in_specs=_dy_specs(dy, J, n, tm) + [pl.BlockSpec((J, K, n), lambda i: (0, 0, 0)), row, vec, row],
        out_specs=[row, vec],
        out_shape=[jax.ShapeDtypeStruct((S, K), F32), jax.ShapeDtypeStruct((1, K), F32)],
        compiler_params=_params(("arbitrary",)))(*([dy] * J), w3, h, gain, dh)


def _mm_nt_swiglu_bwd(dh, w2, z, name, tm=256):
    F, N = w2.shape
    S = dh.shape[0]

    def body(dh_ref, w_ref, z_ref, o_ref):
        da = _dot_nt(dh_ref[...].astype(BF16), w_ref[...])
        g = z_ref[:, :F]
        u = z_ref[:, F:]
        sg = _sigmoid(g)
        o_ref[:, :F] = (da * u * (sg * (1.0 + g * (1.0 - sg)))).astype(BF16)
        o_ref[:, F:] = (da * (g * sg)).astype(BF16)

    return pl.pallas_call(
        body, name=name, grid=(S // tm,),
        in_specs=[pl.BlockSpec((tm, N), lambda i: (i, 0)),
                  pl.BlockSpec((F, N), lambda i: (0, 0)),
                  pl.BlockSpec((tm, 2 * F), lambda i: (i, 0))],
        out_specs=pl.BlockSpec((tm, 2 * F), lambda i: (i, 0)),
        out_shape=jax.ShapeDtypeStruct((S, 2 * F), BF16),
        compiler_params=_params(("parallel",)))(dh, w2, z)


def _mm_tn(kind, xs, dy, J, n, tn, name):
    tpn = n // tn
    ts = 256 if kind == "swiglu" else 512
    S = xs[0].shape[0]
    if kind == "norm":
        K = xs[0].shape[1]
        x_specs = [pl.BlockSpec((ts, K), lambda c, s: (s, 0)), pl.BlockSpec((1, K), lambda c, s: (0, 0))]
    elif kind == "swiglu":
        K = xs[0].shape[1] // 2
        x_specs = [pl.BlockSpec((ts, 2 * K), lambda c, s: (s, 0))]
    else:
        K = xs[0].shape[1]
        x_specs = [pl.BlockSpec((ts, K), lambda c, s: (s, 0))]
    nx = len(xs)
    if dy.ndim == 3:
        dy_spec = pl.BlockSpec((None, ts, tn), lambda c, s: (c // tpn, s, c % tpn))
    else:
        dy_spec = pl.BlockSpec((ts, tn), lambda c, s: (s, c))

    def body(*refs):
        x_refs, dy_ref, o_ref = refs[:nx], refs[nx], refs[nx + 1]
        if kind == "norm":
            v = x_refs[0][...]
            xb = (v * _rstd(v) * x_refs[1][...]).astype(BF16)
        elif kind == "swiglu":
            xb = _swiglu(x_refs[0], K)
        else:
            xb = x_refs[0][...].astype(BF16)

        @pl.when(pl.program_id(1) == 0)
        def _():
            o_ref[...] = jnp.zeros_like(o_ref)

        o_ref[...] += _dot_tn(xb, dy_ref[...].astype(BF16))

    return pl.pallas_call(
        body, name=name, grid=(J * tpn, S // ts),
        in_specs=x_specs + [dy_spec],
        out_specs=pl.BlockSpec((None, K, tn), lambda c, s: (c // tpn, 0, c % tpn)),
        out_shape=jax.ShapeDtypeStruct((J, K, n), F32),
        compiler_params=_params(("parallel", "arbitrary")))(*xs, dy)


def _loss_head(h, gain, target, tm=512):
    S, K = h.shape

    def body(h_ref, g_ref, t_ref, dh_ref, loss_ref, dg_ref):
        v = h_ref[...]
        r = _rstd(v)
        xh = v * r
        g = g_ref[...]
        dy = (xh * g - t_ref[...]) * (1.0 / K)
        dyg = dy * g
        dh_ref[...] = r * (dyg - xh * jnp.mean(dyg * xh, axis=-1, keepdims=True))

        @pl.when(pl.program_id(0) == 0)
        def _():
            loss_ref[...] = jnp.zeros_like(loss_ref)
            dg_ref[...] = jnp.zeros_like(dg_ref)

        part = jnp.sum(jnp.sum(dy * dy, axis=-1, keepdims=True), axis=0, keepdims=True) * (0.5 * K)
        lane = lax.broadcasted_iota(jnp.int32, loss_ref.shape, 1)
        loss_ref[...] += jnp.where(lane == 0, part, 0.0)
        dg_ref[...] += jnp.sum(dy * xh, axis=0, keepdims=True)

    row = pl.BlockSpec((tm, K), lambda i: (i, 0))
    vec = pl.BlockSpec((1, K), lambda i: (0, 0))
    return pl.pallas_call(
        body, name="loss_head", grid=(S // tm,),
        in_specs=[row, vec, row],
        out_specs=[row, pl.BlockSpec((1, HEAD), lambda i: (0, 0)), vec],
        out_shape=[jax.ShapeDtypeStruct((S, K), F32), jax.ShapeDtypeStruct((1, HEAD), F32),
                   jax.ShapeDtypeStruct((1, K), F32)],
        compiler_params=_params(("arbitrary",)))(h, gain, target)


def _lower_bound(lg_ref):
    l0, l1, l2 = lg_ref[0:1, :], lg_ref[1:2, :], lg_ref[2:3, :]
    mx = jnp.maximum(jnp.maximum(l0, l1), l2)
    e0, e1, e2 = jnp.exp(l0 - mx), jnp.exp(l1 - mx), jnp.exp(l2 - mx)
    return e0 / (e0 + e1 + e2)


def _chunk_gates(qz, fz, lb, tri, first_half):
    sig = _sigmoid(fz)
    fg = lb + (1.0 - lb) * sig
    key = 1.0 - fg
    lg = jnp.log(fg)
    b = _dot_exact(tri, lg)
    r = jnp.sum(jnp.where(first_half, lg, 0.0), axis=0, keepdims=True)
    bl = jnp.sum(lg, axis=0, keepdims=True)
    sq = _sigmoid(qz)
    qy = qz * sq
    return sig, fg, key, b, r, bl, sq, qy


def _hgrn_fwd(proj, logits, gain, tb=512):
    S = proj.shape[0]
    H, C = HGRN_HEADS, HGRN_CHUNK
    ncb = tb // C

    def body(q_ref, f_ref, i_ref, g_ref, lg_ref, gn_ref, o_ref, og_ref, st_ref, state):
        @pl.when(pl.program_id(1) == 0)
        def _():
            state[...] = jnp.zeros_like(state)

        lb = _lower_bound(lg_ref)
        gn = gn_ref[...]
        row = lax.broadcasted_iota(jnp.int32, (C, C), 0)
        col = lax.broadcasted_iota(jnp.int32, (C, C), 1)
        causal = col <= row
        tri = causal.astype(F32)
        first_half = lax.broadcasted_iota(jnp.int32, (C, HEAD), 0) < C // 2

        def chunk(ci, carry):
            rows = pl.ds(pl.multiple_of(ci * C, C), C)
            qz, fz, iz, gz = q_ref[rows, :], f_ref[rows, :], i_ref[rows, :], g_ref[rows, :]
            _, _, key, b, r, bl, _, qy = _chunk_gates(qz, fz, lb, tri, first_half)
            qs = (qy * jnp.exp(b - r)).astype(BF16)
            ks = (key * jnp.exp(r - b)).astype(BF16)
            qb = (qy * jnp.exp(b)).astype(BF16)
            ke = (key * jnp.exp(bl - b)).astype(BF16)
            vb = iz.astype(BF16)
            a = jnp.where(causal, _dot_nt(qs, ks), 0.0)
            st = state[...]
            st_ref[ci] = st
            o = _dot_nt(qb, st.astype(BF16)) + _dot(a.astype(BF16), vb)
            state[...] = st * jnp.exp(bl) + _dot_tn(vb, ke)
            o_ref[rows, :] = o
            og_ref[rows, :] = ((o * _rstd(o) * gn) * (gz * _sigmoid(gz))).astype(BF16)
            return carry

        lax.fori_loop(0, ncb, chunk, 0)

    def part(p):
        return pl.BlockSpec((tb, HEAD), functools.partial(lambda h, i, p: (i, p * H + h), p=p))

    return pl.pallas_call(
        body, name="hgrn_fwd", grid=(H, S // tb),
        in_specs=[part(0), part(1), part(2), part(3),
                  pl.BlockSpec((3, HEAD), lambda h, i: (0, h)),
                  pl.BlockSpec((1, HEAD), lambda h, i: (0, 0))],
        out_specs=[pl.BlockSpec((tb, HEAD), lambda h, i: (i, h)),
                   pl.BlockSpec((tb, HEAD), lambda h, i: (i, h)),
                   pl.BlockSpec((None, ncb, HEAD, HEAD), lambda h, i: (h, i, 0, 0))],
        out_shape=[jax.ShapeDtypeStruct((S, H * HEAD), F32),
                   jax.ShapeDtypeStruct((S, H * HEAD), BF16),
                   jax.ShapeDtypeStruct((H, S // C, HEAD, HEAD), F32)],
        scratch_shapes=[pltpu.VMEM((HEAD, HEAD), F32)],
        compiler_params=_params(("parallel", "arbitrary")))(proj, proj, proj, proj, logits, gain)


def _hgrn_bwd(proj, logits, gain, o, states, dog, tb=512):
    S = proj.shape[0]
    H, C = HGRN_HEADS, HGRN_CHUNK
    ncb = tb // C
    nb = S // tb

    def body(q_ref, f_ref, i_ref, g_ref, lg_ref, gn_ref, o_ref, st_ref, dog_ref,
             dp_ref, dlb_ref, dgn_ref, dstate):
        @pl.when(pl.program_id(1) == 0)
        def _():
            dstate[...] = jnp.zeros_like(dstate)
            dlb_ref[...] = jnp.zeros_like(dlb_ref)
            dgn_ref[...] = jnp.zeros_like(dgn_ref)

        lb = _lower_bound(lg_ref)
        oml = 1.0 - lb
        gn = gn_ref[...]
        row = lax.broadcasted_iota(jnp.int32, (C, C), 0)
        col = lax.broadcasted_iota(jnp.int32, (C, C), 1)
        causal = col <= row
        tri = causal.astype(F32)
        tri_up = (col >= row).astype(F32)
        first_half = lax.broadcasted_iota(jnp.int32, (C, HEAD), 0) < C // 2

        def chunk(cj, carry):
            ci = ncb - 1 - cj
            rows = pl.ds(pl.multiple_of(ci * C, C), C)
            qz, fz, iz, gz = q_ref[rows, :], f_ref[rows, :], i_ref[rows, :], g_ref[rows, :]
            sig, fg, key, b, r, bl, sq, qy = _chunk_gates(qz, fz, lb, tri, first_half)
            e_br, e_rb, e_b, e_lb = jnp.exp(b - r), jnp.exp(r - b), jnp.exp(b), jnp.exp(bl - b)
            qs_f, ks_f, qb_f, ke_f = qy * e_br, key * e_rb, qy * e_b, key * e_lb
            qs, ks, qb, ke = qs_f.astype(BF16), ks_f.astype(BF16), qb_f.astype(BF16), ke_f.astype(BF16)
            vb = iz.astype(BF16)
            ov = o_ref[rows, :]
            rs = _rstd(ov)
            xh = ov * rs
            sg = _sigmoid(gz)
            dog_v = dog_ref[rows, :]
            dgz = dog_v * (xh * gn) * (sg * (1.0 + gz * (1.0 - sg)))
            don = dog_v * (gz * sg)
            dgn_ref[...] += jnp.sum(don * xh, axis=0, keepdims=True)
            dyg = don * gn
            do = rs * (dyg - xh * jnp.mean(dyg * xh, axis=-1, keepdims=True))
            dob = do.astype(BF16)
            st0 = st_ref[ci]
            dst1 = dstate[...]
            st0b, dst1b = st0.astype(BF16), dst1.astype(BF16)
            a = jnp.where(causal, _dot_nt(qs, ks), 0.0).astype(BF16)
            da = jnp.where(causal, _dot_nt(dob, vb), 0.0).astype(BF16)
            dv = _dot_tn(a, dob) + _dot_nt(ke, dst1b)
            dqs = _dot(da, ks)
            dks = _dot_tn(da, qs)
            dqb = _dot(dob, st0b)
            dke = _dot(vb, dst1b)
            ke_r = ke.astype(F32)
            db = dqs * qs.astype(F32) - dks * ks.astype(F32) + dqb * qb.astype(F32) - dke * ke_r
            e_l = jnp.exp(bl)
            dbl = jnp.sum(dke * ke_r, axis=0, keepdims=True) + jnp.sum(dst1 * st0, axis=0, keepdims=True) * e_l
            dlg = _dot_exact(tri_up, db) + dbl
            dkey = dks * e_rb + dke * e_lb
            dqy = dqs * e_br + dqb * e_b
            dfg = dlg / fg - dkey
            dlb_ref[...] += jnp.sum(dfg * (1.0 - sig), axis=0, keepdims=True)
            dstate[...] = _dot_tn(dob, qb) + dst1 * e_l
            dp_ref[0, rows, :] = (dqy * (sq * (1.0 + qz * (1.0 - sq)))).astype(BF16)
            dp_ref[1, rows, :] = (dfg * oml * sig * (1.0 - sig)).astype(BF16)
            dp_ref[2, rows, :] = dv.astype(BF16)
            dp_ref[3, rows, :] = dgz.astype(BF16)
            return carry

        lax.fori_loop(0, ncb, chunk, 0)

    def part(p):
        return pl.BlockSpec((tb, HEAD), functools.partial(lambda h, i, p: (nb - 1 - i, p * H + h), p=p))

    blk = pl.BlockSpec((tb, HEAD), lambda h, i: (nb - 1 - i, h))
    acc = pl.BlockSpec((None, 1, HEAD), lambda h, i: (h, 0, 0))
    return pl.pallas_call(
        body, name="hgrn_bwd", grid=(H, nb),
        in_specs=[part(0), part(1), part(2), part(3),
                  pl.BlockSpec((3, HEAD), lambda h, i: (0, h)),
                  pl.BlockSpec((1, HEAD), lambda h, i: (0, 0)),
                  blk,
                  pl.BlockSpec((None, ncb, HEAD, HEAD), lambda h, i: (h, nb - 1 - i, 0, 0)),
                  blk],
        out_specs=[pl.BlockSpec((4, tb, HEAD), lambda h, i: (0, nb - 1 - i, h)), acc, acc],
        out_shape=[jax.ShapeDtypeStruct((4, S, H * HEAD), BF16),
                   jax.ShapeDtypeStruct((H, 1, HEAD), F32),
                   jax.ShapeDtypeStruct((H, 1, HEAD), F32)],
        scratch_shapes=[pltpu.VMEM((HEAD, HEAD), F32)],
        compiler_params=_params(("parallel", "arbitrary")))(proj, proj, proj, proj, logits, gain, o, states, dog)


def _rope(v, cos, sin):
    return v * cos + pltpu.roll(v, HEAD // 2, 1) * sin


def _band_masks():
    qi = lax.broadcasted_iota(jnp.int32, (ATTN_SPAN, ATTN_SPAN), 0)
    kj = lax.broadcasted_iota(jnp.int32, (ATTN_SPAN, ATTN_SPAN), 1)
    return kj <= qi, kj >= qi


def _attn_fwd(a, cos, sin):
    d, L, _ = a.shape
    nb = L // ATTN_SPAN
    scale = HEAD ** -0.5

    def body(q_ref, kc_ref, kp_ref, vc_ref, vp_ref, cc_ref, cp_ref, sc_ref, sp_ref, o_ref, lse_ref):
        n = pl.program_id(1)
        mask_c, mask_p0 = _band_masks()
        mask_p = jnp.logical_and(mask_p0, n > 0)
        cc, cp, sc, sp = cc_ref[...], cp_ref[...], sc_ref[...], sp_ref[...]
        for hh in range(HEADS_PER_GROUP):
            cols = slice(hh * HEAD, (hh + 1) * HEAD)
            q = _rope(q_ref[:, cols], cc, sc).astype(BF16)
            kc = _rope(kc_ref[:, cols], cc, sc).astype(BF16)
            kp = _rope(kp_ref[:, cols], cp, sp).astype(BF16)
            s_c = jnp.where(mask_c, _dot_nt(q, kc) * scale, NEG)
            s_p = jnp.where(mask_p, _dot_nt(q, kp) * scale, NEG)
            m = jnp.maximum(jnp.max(s_c, axis=-1, keepdims=True), jnp.max(s_p, axis=-1, keepdims=True))
            p_c = jnp.exp(s_c - m)
            p_p = jnp.exp(s_p - m)
            l = jnp.sum(p_c, axis=-1, keepdims=True) + jnp.sum(p_p, axis=-1, keepdims=True)
            acc = _dot(p_c.astype(BF16), vc_ref[:, cols].astype(BF16)) + _dot(p_p.astype(BF16), vp_ref[:, cols].astype(BF16))
            o_ref[:, cols] = acc / l
            lse_ref[:, cols] = jnp.broadcast_to(m + jnp.log(l), (ATTN_SPAN, HEAD))

    def blk(part, prev):
        if prev:
            return pl.BlockSpec((None, ATTN_SPAN, GROUP_W), functools.partial(lambda r, n, p: (r, jnp.maximum(n - 1, 0), p), p=part))
        return pl.BlockSpec((None, ATTN_SPAN, GROUP_W), functools.partial(lambda r, n, p: (r, n, p), p=part))

    tab_c = pl.BlockSpec((None, ATTN_SPAN, HEAD), lambda r, n: (r, n, 0))
    tab_p = pl.BlockSpec((None, ATTN_SPAN, HEAD), lambda r, n: (r, jnp.maximum(n - 1, 0), 0))
    out = pl.BlockSpec((None, ATTN_SPAN, GROUP_W), lambda r, n: (r, n, 0))
    return pl.pallas_call(
        body, name=f"attn_fwd_d{d}", grid=(d, nb),
        in_specs=[blk(0, False), blk(1, False), blk(1, True), blk(2, False), blk(2, True), tab_c, tab_p, tab_c, tab_p],
        out_specs=[out, out],
        out_shape=[jax.ShapeDtypeStruct((d, L, GROUP_W), F32), jax.ShapeDtypeStruct((d, L, GROUP_W), F32)],
        compiler_params=_params(("parallel", "arbitrary")))(a, a, a, a, a, cos, cos, sin, sin)


def _attn_bwd(a, cos, sin, do, lse, dd):
    d, L, _ = a.shape
    nb = L // ATTN_SPAN
    scale = HEAD ** -0.5

    def body(qc_ref, qn_ref, kp_ref, kc_ref, vp_ref, vc_ref, doc_ref, don_ref, lc_ref, ln_ref, ddc_ref, ddn_ref,
             cp_ref, cc_ref, cn_ref, sp_ref, sc_ref, sn_ref, da_ref):
        n = pl.program_id(1)
        mask_c, mask_p0 = _band_masks()
        mask_p = jnp.logical_and(mask_p0, n > 0)
        mask_n = jnp.logical_and(mask_p0, n < nb - 1)
        cp, cc, cn, sp, sc, sn = cp_ref[...], cc_ref[...], cn_ref[...], sp_ref[...], sc_ref[...], sn_ref[...]
        for hh in range(HEADS_PER_GROUP):
            cols = slice(hh * HEAD, (hh + 1) * HEAD)
            q = _rope(qc_ref[:, cols], cc, sc).astype(BF16)
            qn = _rope(qn_ref[:, cols], cn, sn).astype(BF16)
            kc = _rope(kc_ref[:, cols], cc, sc).astype(BF16)
            kp = _rope(kp_ref[:, cols], cp, sp).astype(BF16)
            vc = vc_ref[:, cols].astype(BF16)
            vp = vp_ref[:, cols].astype(BF16)
            do_c = doc_ref[:, cols].astype(BF16)
            do_n = don_ref[:, cols].astype(BF16)
            lse_c, lse_n = lc_ref[:, cols], ln_ref[:, cols]
            dd_c, dd_n = ddc_ref[:, cols], ddn_ref[:, cols]
            p_c = jnp.where(mask_c, jnp.exp(_dot_nt(q, kc) * scale - lse_c), 0.0)
            p_p = jnp.where(mask_p, jnp.exp(_dot_nt(q, kp) * scale - lse_c), 0.0)
            ds_c = (p_c * (_dot_nt(do_c, vc) + dd_c)).astype(BF16)
            ds_p = (p_p * (_dot_nt(do_c, vp) + dd_c)).astype(BF16)
            dq = (_dot(ds_c, kc) + _dot(ds_p, kp)) * scale
            p_n = jnp.where(mask_n, jnp.exp(_dot_nt(qn, kc) * scale - lse_n), 0.0)
            ds_n = (p_n * (_dot_nt(do_n, vc) + dd_n)).astype(BF16)
            dk = (_dot_tn(ds_c, q) + _dot_tn(ds_n, qn)) * scale
            dv = _dot_tn(p_c.astype(BF16), do_c) + _dot_tn(p_n.astype(BF16), do_n)
            da_ref[:, cols] = _rope(dq, cc, -sc).astype(BF16)
            da_ref[:, GROUP_W + hh * HEAD:GROUP_W + (hh + 1) * HEAD] = _rope(dk, cc, -sc).astype(BF16)
            da_ref[:, 2 * GROUP_W + hh * HEAD:2 * GROUP_W + (hh + 1) * HEAD] = dv.astype(BF16)

    def rel(delta):
        if delta < 0:
            return lambda n: jnp.maximum(n - 1, 0)
        if delta > 0:
            return lambda n: jnp.minimum(n + 1, nb - 1)
        return lambda n: n

    def blk(width, part, delta):
        f = rel(delta)
        return pl.BlockSpec((None, ATTN_SPAN, width), functools.partial(lambda r, n, p, f: (r, f(n), p), p=part, f=f))

    g = GROUP_W
    return pl.pallas_call(
        body, name=f"attn_bwd_d{d}", grid=(d, nb),
        in_specs=[blk(g, 0, 0), blk(g, 0, 1), blk(g, 1, -1), blk(g, 1, 0), blk(g, 2, -1), blk(g, 2, 0),
                  blk(g, 0, 0), blk(g, 0, 1), blk(g, 0, 0), blk(g, 0, 1), blk(g, 0, 0), blk(g, 0, 1),
                  blk(HEAD, 0, -1), blk(HEAD, 0, 0), blk(HEAD, 0, 1), blk(HEAD, 0, -1), blk(HEAD, 0, 0), blk(HEAD, 0, 1)],
        out_specs=pl.BlockSpec((None, ATTN_SPAN, 3 * g), lambda r, n: (r, n, 0)),
        out_shape=jax.ShapeDtypeStruct((d, L, 3 * g), BF16),
        compiler_params=_params(("parallel", "arbitrary")))(
            a, a, a, a, a, a, do, do, lse, lse, dd, dd, cos, cos, cos, sin, sin, sin)


def _group_weights(lse_refs, cols):
    ls = [r[:, cols] for r in lse_refs]
    mx = jnp.maximum(jnp.maximum(ls[0], ls[1]), ls[2])
    es = [jnp.exp(v - mx) for v in ls]
    tot = es[0] + es[1] + es[2]
    return [e / tot for e in es]


def _attn_merge(os_, lses, tm=512):
    S = os_[0].shape[0]
    G = len(os_)

    def body(*refs):
        o_refs, l_refs, out_ref = refs[:G], refs[G:2 * G], refs[2 * G]
        for hh in range(HEADS_PER_GROUP):
            cols = slice(hh * HEAD, (hh + 1) * HEAD)
            al = _group_weights(l_refs, cols)
            for g in range(G):
                out_ref[:, g * GROUP_W + hh * HEAD:g * GROUP_W + (hh + 1) * HEAD] = (o_refs[g][:, cols] * al[g]).astype(BF16)

    blk = pl.BlockSpec((tm, GROUP_W), lambda i: (i, 0))
    return pl.pallas_call(
        body, name="attn_merge", grid=(S // tm,),
        in_specs=[blk] * (2 * G),
        out_specs=pl.BlockSpec((tm, G * GROUP_W), lambda i: (i, 0)),
        out_shape=jax.ShapeDtypeStruct((S, G * GROUP_W), BF16),
        compiler_params=_params(("parallel",)))(*os_, *lses)


def _attn_merge_bwd(os_, lses, doa, tm=512):
    S = os_[0].shape[0]
    G = len(os_)

    def body(*refs):
        o_refs, l_refs, doa_ref = refs[:G], refs[G:2 * G], refs[2 * G]
        do_refs, dd_refs = refs[2 * G + 1:3 * G + 1], refs[3 * G + 1:]
        for hh in range(HEADS_PER_GROUP):
            cols = slice(hh * HEAD, (hh + 1) * HEAD)
            al = _group_weights(l_refs, cols)
            mix = None
            for g in range(G):
                dg = doa_ref[:, g * GROUP_W + hh * HEAD:g * GROUP_W + (hh + 1) * HEAD]
                do_refs[g][:, cols] = dg * al[g]
                t = al[g] * jnp.sum(dg * o_refs[g][:, cols], axis=-1, keepdims=True)
                mix = t if mix is None else mix + t
            for g in range(G):
                dd_refs[g][:, cols] = -al[g] * mix

    blk = pl.BlockSpec((tm, GROUP_W), lambda i: (i, 0))
    shp = jax.ShapeDtypeStruct((S, GROUP_W), F32)
    return pl.pallas_call(
        body, name="attn_merge_bwd", grid=(S // tm,),
        in_specs=[blk] * (2 * G) + [pl.BlockSpec((tm, G * GROUP_W), lambda i: (i, 0))],
        out_specs=[blk] * (2 * G),
        out_shape=[shp] * (2 * G),
        compiler_params=_params(("parallel",)))(*os_, *lses, doa)


def _to_dilated(v, d):
    S, W = v.shape
    return v.reshape(S // d, d, W).transpose(1, 0, 2)


def _from_dilated(v):
    d, L, W = v.shape
    return v.transpose(1, 0, 2).reshape(L * d, W)


def _rope_tables(S):
    inv_freq = 1.0 / (ROPE_THETA ** (jnp.arange(0, HEAD, 2, dtype=F32) / HEAD))
    ang = jnp.arange(S, dtype=F32)[:, None] * inv_freq[None, :]
    cos, sin = jnp.cos(ang), jnp.sin(ang)
    return jnp.concatenate([cos, cos], axis=-1), jnp.concatenate([-sin, sin], axis=-1)


def _local_step(x, target, norm_mix, norm_ffn, lb_logits, out_gain, final_norm, w):
    S = x.shape[0]
    nm0, nm1 = norm_mix[0:1], norm_mix[1:2]
    nf0, nf1 = norm_ffn[0:1], norm_ffn[1:2]
    fin_tn = w["fin0"].shape[2]

    proj = _norm_mm(x, nm0, w["hin"], D_MODEL, "hgrn_in")
    o, og, states = _hgrn_fwd(proj, lb_logits, out_gain)
    h1 = _mm_res(x, og, w["hout"], "hgrn_out")
    z0 = _norm_mm(h1, nf0, w["fin0"], fin_tn, "ffn0_in")
    h2 = _swiglu_mm_res(h1, z0, w["fdn0"], "ffn0_down")
    qkv = _norm_mm(h2, nm1, w["qkv"], w["qkv"].shape[2], "attn_qkv")
    cos, sin = _rope_tables(S)
    qkv4 = qkv.reshape(S, 3, len(ATTN_GROUPS), GROUP_W)
    a_g, cos_g, sin_g, o_g, lse_g = [], [], [], [], []
    for gi, (_, d) in enumerate(ATTN_GROUPS):
        a = _to_dilated(qkv4[:, :, gi, :].reshape(S, 3 * GROUP_W), d)
        cd, sd = _to_dilated(cos, d), _to_dilated(sin, d)
        og_, lse_ = _attn_fwd(a, cd, sd)
        a_g.append(a), cos_g.append(cd), sin_g.append(sd)
        o_g.append(_from_dilated(og_)), lse_g.append(_from_dilated(lse_))
    oa = _attn_merge(o_g, lse_g)
    h3 = _mm_res(h2, oa, w["aout"], "attn_out")
    z1 = _norm_mm(h3, nf1, w["fin1"], fin_tn, "ffn1_in")
    h4 = _swiglu_mm_res(h3, z1, w["fdn1"], "ffn1_down")
    dh4, loss, d_final = _loss_head(h4, final_norm, target)

    grads, small = {}, {"final_norm": d_final}

    def ffn_bwd(dh, h_in, z, gain, w_in, w_dn, tag):
        dz = _mm_nt_swiglu_bwd(dh, w_dn, z, tag + "_down_dx")
        g_dn = _mm_tn("swiglu", (z,), dh, 1, D_MODEL, 512, tag + "_down_dw")
        g_in = _mm_tn("norm", (h_in, gain), dz, N_CHIPS, fin_tn, fin_tn, tag + "_in_dw")
        dh_in, dgain = _mm_nt_normbwd(dz, w_in, h_in, gain, dh, tag + "_in_dx")
        return dh_in, dgain, g_in, g_dn[0]

    dh3, d_nf1, grads["fin1"], grads["fdn1"] = ffn_bwd(dh4, h3, z1, nf1, w["fin1"], w["fdn1"], "ffn1")
    doa = _mm_nt(dh3, w["aout"][None], "attn_out_dx")
    grads["aout"] = _mm_tn("plain", (oa,), dh3, 1, D_MODEL, 512, "attn_out_dw")[0]
    merged = _attn_merge_bwd(o_g, lse_g, doa)
    parts = []
    for gi, (_, d) in enumerate(ATTN_GROUPS):
        do_d, dd_d = _to_dilated(merged[gi], d), _to_dilated(merged[len(ATTN_GROUPS) + gi], d)
        lse_d = _to_dilated(lse_g[gi], d)
        da = _attn_bwd(a_g[gi], cos_g[gi], sin_g[gi], do_d, lse_d, dd_d)
        parts.append(_from_dilated(da).reshape(S, 3, 1, GROUP_W))
    dqkv = jnp.concatenate(parts, axis=2).reshape(S, 3 * len(ATTN_GROUPS) * GROUP_W)
    n_qkv = w["qkv"].shape[2]
    grads["qkv"] = _mm_tn("norm", (h2, nm1), dqkv, N_CHIPS, n_qkv, n_qkv, "attn_qkv_dw")
    dh2, d_nm1 = _mm_nt_normbwd(dqkv, w["qkv"], h2, nm1, dh3, "attn_qkv_dx")

    dh1, d_nf0, grads["fin0"], grads["fdn0"] = ffn_bwd(dh2, h1, z0, nf0, w["fin0"], w["fdn0"], "ffn0")
    dog = _mm_nt(dh1, w["hout"][None], "hgrn_out_dx")
    grads["hout"] = _mm_tn("plain", (og,), dh1, 1, D_MODEL, 512, "hgrn_out_dw")[0]
    dproj, dlb, dgn = _hgrn_bwd(proj, lb_logits, out_gain, o, states, dog)
    grads["hin"] = _mm_tn("norm", (x, nm0), dproj, N_CHIPS, D_MODEL, D_MODEL, "hgrn_in_dw")
    dx, d_nm0 = _mm_nt_normbwd(dproj, w["hin"], x, nm0, dh1, "hgrn_in_dx")

    small["norm_mix"] = jnp.concatenate([d_nm0, d_nm1], axis=0)
    small["norm_ffn"] = jnp.concatenate([d_nf0, d_nf1], axis=0)
    small["lb"] = dlb.reshape(1, HGRN_HEADS * HEAD)
    small["out_norm"] = dgn.reshape(HGRN_HEADS, HEAD)
    return loss, dx, grads, small


def _place():
    x, y, c = lax.axis_index("x"), lax.axis_index("y"), lax.axis_index("c")
    others = [(1 - x, y), (x, 1 - y), (1 - x, 1 - y)]
    return x, y, c, others


ANY = pl.BlockSpec(memory_space=pl.ANY)


def _gather_weights(shards):
    n = len(shards)

    def body(*refs):
        ins, outs = refs[:n], refs[n:2 * n]
        ici_send, ici_recv, d2d_send, d2d_recv, own_send, own_recv = refs[2 * n:]
        x, y, c, others = _place()
        me = 2 * x + y
        sibling = (x, y, 1 - c)
        own = [pltpu.make_async_remote_copy(
            src_ref=ins[a], dst_ref=outs[a].at[me], send_sem=own_send.at[a], recv_sem=own_recv.at[a],
            device_id=sibling, device_id_type=MESH) for a in range(n)]
        for cp in own:
            cp.start()
        sends, passes = [], []
        for a in range(n):
            for k, (ox, oy) in enumerate(others):
                s = a * 3 + k
                sends.append(pltpu.make_async_remote_copy(
                    src_ref=ins[a].at[c], dst_ref=outs[a].at[me, c], send_sem=ici_send.at[s], recv_sem=ici_recv.at[s],
                    device_id=(ox, oy, c), device_id_type=MESH))
        for cp in sends:
            cp.start()
        for a in range(n):
            for k, (ox, oy) in enumerate(others):
                s = a * 3 + k
                got = outs[a].at[2 * ox + oy, c]
                pltpu.make_async_remote_copy(
                    src_ref=got, dst_ref=got, send_sem=ici_send.at[s], recv_sem=ici_recv.at[s],
                    device_id=(ox, oy, c), device_id_type=MESH).wait_recv()
                fwd = pltpu.make_async_remote_copy(
                    src_ref=got, dst_ref=got, send_sem=d2d_send.at[s], recv_sem=d2d_recv.at[s],
                    device_id=sibling, device_id_type=MESH)
                fwd.start()
                passes.append(fwd)
        for a in range(n):
            for k, (ox, oy) in enumerate(others):
                s = a * 3 + k
                theirs = outs[a].at[2 * ox + oy, 1 - c]
                pltpu.make_async_remote_copy(
                    src_ref=theirs, dst_ref=theirs, send_sem=d2d_send.at[s], recv_sem=d2d_recv.at[s],
                    device_id=sibling, device_id_type=MESH).wait_recv()
        for cp in own:
            cp.wait()
        for cp in sends + passes:
            cp.wait_send()

    return pl.pallas_call(
        body, name="gather_weights",
        in_specs=[ANY] * n, out_specs=[ANY] * n,
        out_shape=[jax.ShapeDtypeStruct((N_CHIPS,) + s.shape, s.dtype) for s in shards],
        scratch_shapes=[pltpu.SemaphoreType.DMA((3 * n,)), pltpu.SemaphoreType.DMA((3 * n,)),
                        pltpu.SemaphoreType.DMA((3 * n,)), pltpu.SemaphoreType.DMA((3 * n,)),
                        pltpu.SemaphoreType.DMA((n,)), pltpu.SemaphoreType.DMA((n,))],
        )(*shards)


def _pair_exchange(grads):
    n = len(grads)

    def body(*refs):
        ins, outs = refs[:n], refs[n:2 * n]
        send_sem, recv_sem = refs[2 * n:]
        x, y, c, _ = _place()
        sibling = (x, y, 1 - c)
        cps = []
        for a in range(n):
            for j in range(N_CHIPS):
                s = a * N_CHIPS + j
                cps.append(pltpu.make_async_remote_copy(
                    src_ref=ins[a].at[j, 1 - c], dst_ref=outs[a].at[j], send_sem=send_sem.at[s], recv_sem=recv_sem.at[s],
                    device_id=sibling, device_id_type=MESH))
        for cp in cps:
            cp.start()
        for cp in cps:
            cp.wait()

    return pl.pallas_call(
        body, name="grad_pair_exchange",
        in_specs=[ANY] * n, out_specs=[ANY] * n,
        out_shape=[jax.ShapeDtypeStruct((N_CHIPS,) + g.shape[2:], F32) for g in grads],
        scratch_shapes=[pltpu.SemaphoreType.DMA((N_CHIPS * n,)), pltpu.SemaphoreType.DMA((N_CHIPS * n,))],
        )(*grads)


def _pair_sum(g, got, c_idx):
    _, _, r, cw = g.shape
    tr = _row_tile(r, cw)

    def body(c_ref, g_ref, got_ref, p_ref, pb_ref):
        v = g_ref[...] + got_ref[...]
        p_ref[...] = v
        pb_ref[...] = v.astype(BF16)

    blk = pl.BlockSpec((None, tr, cw), lambda j, i, c_ref: (j, i, 0))
    return pl.pallas_call(
        body, name="grad_pair_sum",
        grid_spec=pltpu.PrefetchScalarGridSpec(
            num_scalar_prefetch=1, grid=(N_CHIPS, r // tr),
            in_specs=[pl.BlockSpec((None, None, tr, cw), lambda j, i, c_ref: (j, c_ref[0], i, 0)), blk],
            out_specs=[blk, blk]),
        out_shape=[jax.ShapeDtypeStruct((N_CHIPS, r, cw), F32), jax.ShapeDtypeStruct((N_CHIPS, r, cw), BF16)],
        compiler_params=_params(("parallel", "parallel")))(c_idx, g, got)


def _chip_exchange(parts):
    n = len(parts)

    def body(*refs):
        ins, outs = refs[:n], refs[n:2 * n]
        send_sem, recv_sem = refs[2 * n:]
        x, y, c, others = _place()
        me = 2 * x + y
        cps = []
        for a in range(n):
            for k, (ox, oy) in enumerate(others):
                s = a * 3 + k
                cps.append(pltpu.make_async_remote_copy(
                    src_ref=ins[a].at[2 * ox + oy], dst_ref=outs[a].at[me], send_sem=send_sem.at[s], recv_sem=recv_sem.at[s],
                    device_id=(ox, oy, c), device_id_type=MESH))
        for cp in cps:
            cp.start()
        for a in range(n):
            for k, (ox, oy) in enumerate(others):
                s = a * 3 + k
                got = outs[a].at[2 * ox + oy]
                pltpu.make_async_remote_copy(
                    src_ref=got, dst_ref=got, send_sem=send_sem.at[s], recv_sem=recv_sem.at[s],
                    device_id=(ox, oy, c), device_id_type=MESH).wait_recv()
        for cp in cps:
            cp.wait_send()

    return pl.pallas_call(
        body, name="grad_chip_exchange",
        in_specs=[ANY] * n, out_specs=[ANY] * n,
        out_shape=[jax.ShapeDtypeStruct(p.shape, BF16) for p in parts],
        scratch_shapes=[pltpu.SemaphoreType.DMA((3 * n,)), pltpu.SemaphoreType.DMA((3 * n,))],
        )(*parts)


def _chip_sum(p, got, me_idx):
    _, r, cw = p.shape
    tr = _row_tile(r, cw)

    def body(me_ref, own_ref, got_ref, t_ref):
        me = me_ref[0]
        acc = None
        for s in range(N_CHIPS):
            term = jnp.where(me == s, own_ref[...], got_ref[s].astype(F32))
            acc = term if acc is None else acc + term
        t_ref[...] = acc

    return pl.pallas_call(
        body, name="grad_chip_sum",
        grid_spec=pltpu.PrefetchScalarGridSpec(
            num_scalar_prefetch=1, grid=(r // tr,),
            in_specs=[pl.BlockSpec((None, tr, cw), lambda i, me_ref: (me_ref[0], i, 0)),
                      pl.BlockSpec((N_CHIPS, tr, cw), lambda i, me_ref: (0, i, 0))],
            out_specs=pl.BlockSpec((tr, cw), lambda i, me_ref: (i, 0))),
        out_shape=jax.ShapeDtypeStruct((r, cw), F32),
        compiler_params=_params(("parallel",)))(me_idx, p, got)


def _pair_share(halves):
    n = len(halves)

    def body(*refs):
        ins, outs = refs[:n], refs[n:2 * n]
        send_sem, recv_sem = refs[2 * n:]
        x, y, c, _ = _place()
        cps = [pltpu.make_async_remote_copy(
            src_ref=ins[a], dst_ref=outs[a], send_sem=send_sem.at[a], recv_sem=recv_sem.at[a],
            device_id=(x, y, 1 - c), device_id_type=MESH) for a in range(n)]
        for cp in cps:
            cp.start()
        for cp in cps:
            cp.wait()

    return pl.pallas_call(
        body, name="grad_pair_share",
        in_specs=[ANY] * n, out_specs=[ANY] * n,
        out_shape=[jax.ShapeDtypeStruct(h.shape, F32) for h in halves],
        scratch_shapes=[pltpu.SemaphoreType.DMA((n,)), pltpu.SemaphoreType.DMA((n,))],
        )(*halves)


def _small_allreduce(pack):
    m_per, ncol = pack.shape
    n_dev = 8

    def body(x_ref, sum_ref, all_ref, send_sems, recv_sems, local_sem):
        x, y, c, others = _place()
        me, sibling = (x, y, c), (x, y, 1 - c)

        def rows(px, py, pc):
            return all_ref.at[pl.ds((4 * px + 2 * py + pc) * m_per, m_per), :]

        def copy(k, block, to, src=None):
            return pltpu.make_async_remote_copy(
                src_ref=rows(*block) if src is None else src, dst_ref=rows(*block),
                send_sem=send_sems.at[k], recv_sem=recv_sems.at[k], device_id=to, device_id_type=MESH)

        mine = pltpu.make_async_copy(x_ref, rows(*me), local_sem)
        mine.start()
        first = [copy(0, me, sibling, src=x_ref)]
        first += [copy(1 + j, me, (*chip, c), src=x_ref) for j, chip in enumerate(others)]
        for cp in first:
            cp.start()
        passed = [copy(4 + j, (*chip, c), sibling) for j, chip in enumerate(others)]
        for j, chip in enumerate(others):
            copy(1 + j, (*chip, c), me).wait_recv()
            passed[j].start()
        copy(0, sibling, me).wait_recv()
        for j, chip in enumerate(others):
            copy(4 + j, (*chip, 1 - c), me).wait_recv()
        for cp in first + passed:
            cp.wait_send()
        mine.wait()
        acc = all_ref[0:m_per, :]
        for dvc in range(1, n_dev):
            acc = acc + all_ref[dvc * m_per:(dvc + 1) * m_per, :]
        sum_ref[...] = acc

    return pl.pallas_call(
        body, name="small_allreduce",
        in_specs=[pl.BlockSpec(memory_space=pltpu.VMEM)],
        out_specs=pl.BlockSpec(memory_space=pltpu.VMEM),
        out_shape=jax.ShapeDtypeStruct((m_per, ncol), F32),
        scratch_shapes=[pltpu.VMEM((n_dev * m_per, ncol), F32),
                        pltpu.SemaphoreType.DMA((7,)), pltpu.SemaphoreType.DMA((7,)), pltpu.SemaphoreType.DMA],
        )(pack)


def _adam_math(w, g, m, v):
    m = ADAM_B1 * m + (1.0 - ADAM_B1) * g
    v = ADAM_B2 * v + (1.0 - ADAM_B2) * (g * g)
    m_hat = m / (1.0 - ADAM_B1 ** ADAM_STEP)
    v_hat = v / (1.0 - ADAM_B2 ** ADAM_STEP)
    delta = -ADAM_LR * (m_hat / (jnp.sqrt(v_hat) + ADAM_EPS) + ADAM_WD * w)
    return delta, m, v


def _adamw(mine, theirs, c_idx, w, m, v, name):
    r, C = mine.shape
    tr = _row_tile(r, C, 512 * 1024)
    nt = r // tr

    def body(c_ref, mine_ref, theirs_ref, w_ref, m_ref, v_ref, g_ref, d_ref, nm_ref, nv_ref):
        g = jnp.where(pl.program_id(0) == c_ref[0], mine_ref[...], theirs_ref[...])
        g_ref[...] = g
        d_ref[...], nm_ref[...], nv_ref[...] = _adam_math(w_ref[...], g, m_ref[...], v_ref[...])

    half = pl.BlockSpec((tr, C), lambda h, i, c_ref: (i, 0))
    full = pl.BlockSpec((tr, C), lambda h, i, c_ref: (h * nt + i, 0))
    shp = jax.ShapeDtypeStruct((2 * r, C), F32)
    return pl.pallas_call(
        body, name=name,
        grid_spec=pltpu.PrefetchScalarGridSpec(
            num_scalar_prefetch=1, grid=(2, nt),
            in_specs=[half, half, full, full, full], out_specs=[full] * 4),
        out_shape=[shp] * 4,
        compiler_params=_params(("parallel", "parallel")))(c_idx, mine, theirs, w, m, v)


def _small_update(gsum, logits_pack, w, m, v):
    def body(gs_ref, lg_ref, w_ref, m_ref, v_ref, g_ref, d_ref, nm_ref, nv_ref):
        g_ref[...] = gs_ref[...]
        l0, l1, l2 = lg_ref[0:1, :], lg_ref[1:2, :], lg_ref[2:3, :]
        mx = jnp.maximum(jnp.maximum(l0, l1), l2)
        e0, e1, e2 = jnp.exp(l0 - mx), jnp.exp(l1 - mx), jnp.exp(l2 - mx)
        tot = e0 + e1 + e2
        p0, p1, p2 = e0 / tot, e1 / tot, e2 / tot
        dlb = gs_ref[4:5, :]
        g_ref[4:5, :] = dlb * p0 * (1.0 - p0)
        g_ref[5:6, :] = -dlb * p0 * p1
        g_ref[6:7, :] = -dlb * p0 * p2
        d_ref[...], nm_ref[...], nv_ref[...] = _adam_math(w_ref[...], g_ref[...], m_ref[...], v_ref[...])

    full = pl.BlockSpec(memory_space=pltpu.VMEM)
    shp = jax.ShapeDtypeStruct(gsum.shape, F32)
    return pl.pallas_call(
        body, name="small_update", in_specs=[full] * 5, out_specs=[full] * 4, out_shape=[shp] * 4)(
            gsum, logits_pack, w, m, v)


def _pack_small(norm_mix, norm_ffn, lb3, out_norm, final_norm, extra=None):
    ncol = norm_mix.shape[1]
    on = jnp.pad(out_norm.reshape(1, -1), ((0, 0), (0, ncol - out_norm.size)))
    rows = [norm_mix, norm_ffn, lb3, on, final_norm.reshape(1, ncol)]
    if extra is not None:
        rows.append(extra)
    used = sum(r.shape[0] for r in rows)
    rows.append(jnp.zeros((SMALL_ROWS - used, ncol), F32))
    return jnp.concatenate(rows, axis=0)


WEIGHT_NAMES = ("hin", "hout", "qkv", "aout", "fin0", "fin1", "fdn0", "fdn1")


def _split_weights(hgrn_w_in, hgrn_w_out, attn_w_qkv, attn_w_out, ffn_w_in, ffn_w_down):
    return {"hin": hgrn_w_in[0], "hout": hgrn_w_out[0], "qkv": attn_w_qkv[0], "aout": attn_w_out[0],
            "fin0": ffn_w_in[0], "fin1": ffn_w_in[1], "fdn0": ffn_w_down[0], "fdn1": ffn_w_down[1]}


def _halves(v):
    r, c = v.shape
    return v.reshape(2, r // 2, c)


def _full_weights(gathered):
    out = {}
    for k, g in gathered.items():
        _, _, r, c = g.shape
        if k in ("hin", "qkv", "fin0", "fin1"):
            out[k] = g.reshape(N_CHIPS, 2 * r, c)
        else:
            out[k] = g.reshape(N_CHIPS * 2 * r, c)
    return out


def kernel(x, norm_mix, norm_ffn, hgrn_w_in, hgrn_lb_logits, hgrn_out_norm, hgrn_w_out, attn_w_qkv, attn_w_out, ffn_w_in, ffn_w_down, final_norm, loss_target, m_norm_mix, m_norm_ffn, m_hgrn_w_in, m_hgrn_lb_logits, m_hgrn_out_norm, m_hgrn_w_out, m_attn_w_qkv, m_attn_w_out, m_ffn_w_in, m_ffn_w_down, m_final_norm, v_norm_mix, v_norm_ffn, v_hgrn_w_in, v_hgrn_lb_logits, v_hgrn_out_norm, v_hgrn_w_out, v_attn_w_qkv, v_attn_w_out, v_ffn_w_in, v_ffn_w_down, v_final_norm):
    S = x.shape[1]
    xi, yi, ci = lax.axis_index("x"), lax.axis_index("y"), lax.axis_index("c")
    c_idx = jnp.reshape(ci, (1,)).astype(jnp.int32)
    me_idx = jnp.reshape(2 * xi + yi, (1,)).astype(jnp.int32)

    w_own = _split_weights(hgrn_w_in, hgrn_w_out, attn_w_qkv, attn_w_out, ffn_w_in, ffn_w_down)
    m_own = _split_weights(m_hgrn_w_in, m_hgrn_w_out, m_attn_w_qkv, m_attn_w_out, m_ffn_w_in, m_ffn_w_down)
    v_own = _split_weights(v_hgrn_w_in, v_hgrn_w_out, v_attn_w_qkv, v_attn_w_out, v_ffn_w_in, v_ffn_w_down)

    gathered = _gather_weights([_halves(w_own[k].astype(BF16)) for k in WEIGHT_NAMES])
    w_full = _full_weights(dict(zip(WEIGHT_NAMES, gathered)))

    loss, dx, grads, small = _local_step(
        x.reshape(S, D_MODEL), loss_target.reshape(S, D_MODEL), norm_mix, norm_ffn, hgrn_lb_logits,
        hgrn_out_norm, final_norm.reshape(1, D_MODEL), w_full)

    g4 = []
    for k in WEIGHT_NAMES:
        r, c = w_own[k].shape
        g4.append(grads[k].reshape(N_CHIPS, 2, r // 2, c))
    from_sibling = _pair_exchange(g4)
    sums = [_pair_sum(g, got, c_idx) for g, got in zip(g4, from_sibling)]
    from_chips = _chip_exchange([s[1] for s in sums])
    halves = [_chip_sum(s[0], got, me_idx) for s, got in zip(sums, from_chips)]
    from_pair = _pair_share(halves)

    g_out, d_out, m_out, v_out = {}, {}, {}, {}
    for k, mine, theirs in zip(WEIGHT_NAMES, halves, from_pair):
        g_out[k], d_out[k], m_out[k], v_out[k] = _adamw(mine, theirs, c_idx, w_own[k], m_own[k], v_own[k], "adamw_" + k)

    loss_row = jnp.pad(loss, ((0, 0), (0, D_MODEL - loss.shape[1])))
    lb3 = jnp.concatenate([small["lb"], jnp.zeros((2, D_MODEL), F32)], axis=0)
    on_grad = jnp.sum(small["out_norm"], axis=0, keepdims=True)
    pack = _pack_small(small["norm_mix"], small["norm_ffn"], lb3, on_grad, small["final_norm"], loss_row)
    gsum = _small_allreduce(pack)
    w_s = _pack_small(norm_mix, norm_ffn, hgrn_lb_logits, hgrn_out_norm, final_norm)
    m_s = _pack_small(m_norm_mix, m_norm_ffn, m_hgrn_lb_logits, m_hgrn_out_norm, m_final_norm)
    v_s = _pack_small(v_norm_mix, v_norm_ffn, v_hgrn_lb_logits, v_hgrn_out_norm, v_final_norm)
    lg_pack = jnp.pad(hgrn_lb_logits, ((0, 8 - hgrn_lb_logits.shape[0]), (0, 0)))
    sg, sd, sm, sv = _small_update(gsum, lg_pack, w_s, m_s, v_s)

    def unpack(p):
        return (p[0:2], p[2:4], p[4:7], p[7:8, :HEAD], p[8])

    def big(dct):
        return (dct["hin"][None], dct["hout"][None], dct["qkv"][None], dct["aout"][None],
                jnp.stack([dct["fin0"], dct["fin1"]]), jnp.stack([dct["fdn0"], dct["fdn1"]]))

    def assemble(p, dct):
        nmx, nff, lbl, onm, fnm = unpack(p)
        hin, hout, qkv, aout, fin, fdn = big(dct)
        return (nmx, nff, hin, lbl, onm, hout, qkv, aout, fin, fdn, fnm)

    total_loss = gsum[9, 0]
    return (total_loss, dx.reshape(1, S, D_MODEL), *assemble(sg, g_out), *assemble(sd, d_out),
            *assemble(sm, m_out), *assemble(sv, v_out))
```

```python
import functools

import jax
import jax.numpy as jnp
from jax import lax
from jax.experimental import pallas as pl
from jax.experimental.pallas import tpu as pltpu

F32 = jnp.float32
BF16 = jnp.bfloat16
MESH = pl.DeviceIdType.MESH

D_MODEL = 1024
HEAD = 128
HGRN_HEADS = 8
HGRN_CHUNK = 64
ATTN_GROUPS = ((128, 1), (512, 4), (2048, 16))
ATTN_SPAN = 128
HEADS_PER_GROUP = 4
GROUP_W = HEADS_PER_GROUP * HEAD
D_FF = 2816
NORM_EPS = 1e-6
ROPE_THETA = 10000.0
NEG = -1e30

ADAM_LR, ADAM_B1, ADAM_B2, ADAM_EPS, ADAM_WD, ADAM_STEP = 0.001, 0.9, 0.999, 1e-08, 0.01, 10

N_CHIPS = 4
VMEM_LIMIT = 56 * 1024 * 1024
SMALL_ROWS = 16


def _params(sem=None):
    return pltpu.CompilerParams(dimension_semantics=sem, vmem_limit_bytes=VMEM_LIMIT)


def _row_tile(rows, cols, budget_bytes=3 * 512 * 1024):
    best = 8
    for t in range(8, rows + 1, 8):
        if rows % t == 0 and t * cols * 4 <= budget_bytes:
            best = t
    assert rows % best == 0
    return best


def _sigmoid(v):
    return 1.0 / (1.0 + jnp.exp(-v))


def _dot(a, b):
    return jnp.dot(a, b, preferred_element_type=F32)


def _dot_nt(a, b):
    return lax.dot_general(a, b, (((1,), (1,)), ((), ())), preferred_element_type=F32)


def _dot_tn(a, b):
    return lax.dot_general(a, b, (((0,), (0,)), ((), ())), preferred_element_type=F32)


def _dot_exact(a, b):
    return jnp.dot(a, b, preferred_element_type=F32, precision=lax.Precision.HIGHEST)


def _rstd(v):
    return lax.rsqrt(jnp.mean(v * v, axis=-1, keepdims=True) + NORM_EPS)


def _norm_mm(h, gain, w3, tn, name, tm=512):
    S, K = h.shape
    J, _, n = w3.shape
    tpn = n // tn

    def body(h_ref, g_ref, w_ref, y_ref, u_scr):
        @pl.when(pl.program_id(1) == 0)
        def _():
            v = h_ref[...]
            u_scr[...] = (v * _rstd(v) * g_ref[...]).astype(BF16)

        y_ref[...] = _dot(u_scr[...], w_ref[...])

    return pl.pallas_call(
        body, name=name, grid=(S // tm, J * tpn),
        in_specs=[pl.BlockSpec((tm, K), lambda i, j: (i, 0)),
                  pl.BlockSpec((1, K), lambda i, j: (0, 0)),
                  pl.BlockSpec((None, K, tn), lambda i, j: (j // tpn, 0, j % tpn))],
        out_specs=pl.BlockSpec((tm, tn), lambda i, j: (i, j)),
        out_shape=jax.ShapeDtypeStruct((S, J * n), F32),
        scratch_shapes=[pltpu.VMEM((tm, K), BF16)],
        compiler_params=_params(("parallel", "arbitrary")))(h, gain, w3)


def _mm_res(h, a, w2, name, tm=512):
    S, N = h.shape
    K = a.shape[1]

    def body(h_ref, a_ref, w_ref, o_ref):
        o_ref[...] = h_ref[...] + _dot(a_ref[...], w_ref[...])

    return pl.pallas_call(
        body, name=name, grid=(S // tm,),
        in_specs=[pl.BlockSpec((tm, N), lambda i: (i, 0)),
                  pl.BlockSpec((tm, K), lambda i: (i, 0)),
                  pl.BlockSpec((K, N), lambda i: (0, 0))],
        out_specs=pl.BlockSpec((tm, N), lambda i: (i, 0)),
        out_shape=jax.ShapeDtypeStruct((S, N), F32),
        compiler_params=_params(("parallel",)))(h, a, w2)


def _swiglu(z_ref, F):
    g = z_ref[:, :F]
    return (g * _sigmoid(g) * z_ref[:, F:]).astype(BF16)


def _swiglu_mm_res(h, z, w2, name, tm=256):
    S, N = h.shape
    F = w2.shape[0]

    def body(h_ref, z_ref, w_ref, o_ref):
        o_ref[...] = h_ref[...] + _dot(_swiglu(z_ref, F), w_ref[...])

    return pl.pallas_call(
        body, name=name, grid=(S // tm,),
        in_specs=[pl.BlockSpec((tm, N), lambda i: (i, 0)),
                  pl.BlockSpec((tm, 2 * F), lambda i: (i, 0)),
                  pl.BlockSpec((F, N), lambda i: (0, 0))],
        out_specs=pl.BlockSpec((tm, N), lambda i: (i, 0)),
        out_shape=jax.ShapeDtypeStruct((S, N), F32),
        compiler_params=_params(("parallel",)))(h, z, w2)


def _dy_specs(dy, J, n, tm):
    if dy.ndim == 3:
        return [pl.BlockSpec((None, tm, n), functools.partial(lambda i, j: (j, i, 0), j=j)) for j in range(J)]
    return [pl.BlockSpec((tm, n), functools.partial(lambda i, j: (i, j), j=j)) for j in range(J)]


def _acc_nt(dy_refs, w_ref):
    acc = None
    for j, r in enumerate(dy_refs):
        t = _dot_nt(r[...].astype(BF16), w_ref[j])
        acc = t if acc is None else acc + t
    return acc


def _mm_nt(dy, w3, name, out_dtype=F32, tm=512):
    J, K, n = w3.shape
    S = dy.shape[-2]

    def body(*refs):
        dy_refs, w_ref, o_ref = refs[:J], refs[J], refs[J + 1]
        o_ref[...] = _acc_nt(dy_refs, w_ref).astype(o_ref.dtype)

    return pl.pallas_call(
        body, name=name, grid=(S // tm,),
        in_specs=_dy_specs(dy, J, n, tm) + [pl.BlockSpec((J, K, n), lambda i: (0, 0, 0))],
        out_specs=pl.BlockSpec((tm, K), lambda i: (i, 0)),
        out_shape=jax.ShapeDtypeStruct((S, K), out_dtype),
        compiler_params=_params(("parallel",)))(*([dy] * J), w3)


def _mm_nt_normbwd(dy, w3, h, gain, dh, name, tm=512):
    J, K, n = w3.shape
    S = h.shape[0]

    def body(*refs):
        dy_refs, w_ref, h_ref, g_ref, dh_ref, o_ref, dg_ref = refs[:J], *refs[J:]
        du = _acc_nt(dy_refs, w_ref)
        v = h_ref[...]
        r = _rstd(v)
        xh = v * r
        dyg = du * g_ref[...]
        o_ref[...] = dh_ref[...] + r * (dyg - xh * jnp.mean(dyg * xh, axis=-1, keepdims=True))

        @pl.when(pl.program_id(0) == 0)
        def _():
            dg_ref[...] = jnp.zeros_like(dg_ref)

        dg_ref[...] += jnp.sum(du * xh, axis=0, keepdims=True)

    row = pl.BlockSpec((tm, K), lambda i: (i, 0))
    vec = pl.BlockSpec((1, K), lambda i: (0, 0))
    return pl.pallas_call(
        body, name=name, grid=(S // tm,),
        in_specs=_dy_specs(dy, J, n, tm) + [pl.BlockSpec((J, K, n), lambda i: (0, 0, 0)), row, vec, row],
        out_specs=[row, vec],
        out_shape=[jax.ShapeDtypeStruct((S, K), F32), jax.ShapeDtypeStruct((1, K), F32)],
        compiler_params=_params(("arbitrary",)))(*([dy] * J), w3, h, gain, dh)


def _mm_nt_swiglu_bwd(dh, w2, z, name, tm=256):
    F, N = w2.shape
    S = dh.shape[0]

    def body(dh_ref, w_ref, z_ref, o_ref):
        da = _dot_nt(dh_ref[...].astype(BF16), w_ref[...])
        g = z_ref[:, :F]
        u = z_ref[:, F:]
        sg = _sigmoid(g)
        o_ref[:, :F] = (da * u * (sg * (1.0 + g * (1.0 - sg)))).astype(BF16)
        o_ref[:, F:] = (da * (g * sg)).astype(BF16)

    return pl.pallas_call(
        body, name=name, grid=(S // tm,),
        in_specs=[pl.BlockSpec((tm, N), lambda i: (i, 0)),
                  pl.BlockSpec((F, N), lambda i: (0, 0)),
                  pl.BlockSpec((tm, 2 * F), lambda i: (i, 0))],
        out_specs=pl.BlockSpec((tm, 2 * F), lambda i: (i, 0)),
        out_shape=jax.ShapeDtypeStruct((S, 2 * F), BF16),
        compiler_params=_params(("parallel",)))(dh, w2, z)


def _mm_tn(kind, xs, dy, J, n, tn, name):
    tpn = n // tn
    ts = 256 if kind == "swiglu" else 512
    S = xs[0].shape[0]
    if kind == "norm":
        K = xs[0].shape[1]
        x_specs = [pl.BlockSpec((ts, K), lambda c, s: (s, 0)), pl.BlockSpec((1, K), lambda c, s: (0, 0))]
    elif kind == "swiglu":
        K = xs[0].shape[1] // 2
        x_specs = [pl.BlockSpec((ts, 2 * K), lambda c, s: (s, 0))]
    else:
        K = xs[0].shape[1]
        x_specs = [pl.BlockSpec((ts, K), lambda c, s: (s, 0))]
    nx = len(xs)
    if dy.ndim == 3:
        dy_spec = pl.BlockSpec((None, ts, tn), lambda c, s: (c // tpn, s, c % tpn))
    else:
        dy_spec = pl.BlockSpec((ts, tn), lambda c, s: (s, c))

    def body(*refs):
        x_refs, dy_ref, o_ref = refs[:nx], refs[nx], refs[nx + 1]
        if kind == "norm":
            v = x_refs[0][...]
            xb = (v * _rstd(v) * x_refs[1][...]).astype(BF16)
        elif kind == "swiglu":
            xb = _swiglu(x_refs[0], K)
        else:
            xb = x_refs[0][...].astype(BF16)

        @pl.when(pl.program_id(1) == 0)
        def _():
            o_ref[...] = jnp.zeros_like(o_ref)

        o_ref[...] += _dot_tn(xb, dy_ref[...].astype(BF16))

    return pl.pallas_call(
        body, name=name, grid=(J * tpn, S // ts),
        in_specs=x_specs + [dy_spec],
        out_specs=pl.BlockSpec((None, K, tn), lambda c, s: (c // tpn, 0, c % tpn)),
        out_shape=jax.ShapeDtypeStruct((J, K, n), F32),
        compiler_params=_params(("parallel", "arbitrary")))(*xs, dy)


def _loss_head(h, gain, target, tm=512):
    S, K = h.shape

    def body(h_ref, g_ref, t_ref, dh_ref, loss_ref, dg_ref):
        v = h_ref[...]
        r = _rstd(v)
        xh = v * r
        g = g_ref[...]
        dy = (xh * g - t_ref[...]) * (1.0 / K)
        dyg = dy * g
        dh_ref[...] = r * (dyg - xh * jnp.mean(dyg * xh, axis=-1, keepdims=True))

        @pl.when(pl.program_id(0) == 0)
        def _():
            loss_ref[...] = jnp.zeros_like(loss_ref)
            dg_ref[...] = jnp.zeros_like(dg_ref)

        part = jnp.sum(jnp.sum(dy * dy, axis=-1, keepdims=True), axis=0, keepdims=True) * (0.5 * K)
        lane = lax.broadcasted_iota(jnp.int32, loss_ref.shape, 1)
        loss_ref[...] += jnp.where(lane == 0, part, 0.0)
        dg_ref[...] += jnp.sum(dy * xh, axis=0, keepdims=True)

    row = pl.BlockSpec((tm, K), lambda i: (i, 0))
    vec = pl.BlockSpec((1, K), lambda i: (0, 0))
    return pl.pallas_call(
        body, name="loss_head", grid=(S // tm,),
        in_specs=[row, vec, row],
        out_specs=[row, pl.BlockSpec((1, HEAD), lambda i: (0, 0)), vec],
        out_shape=[jax.ShapeDtypeStruct((S, K), F32), jax.ShapeDtypeStruct((1, HEAD), F32),
                   jax.ShapeDtypeStruct((1, K), F32)],
        compiler_params=_params(("arbitrary",)))(h, gain, target)


def _lower_bound(lg_ref):
    l0, l1, l2 = lg_ref[0:1, :], lg_ref[1:2, :], lg_ref[2:3, :]
    mx = jnp.maximum(jnp.maximum(l0, l1), l2)
    e0, e1, e2 = jnp.exp(l0 - mx), jnp.exp(l1 - mx), jnp.exp(l2 - mx)
    return e0 / (e0 + e1 + e2)


def _chunks(v, ncb):
    C = HGRN_CHUNK
    return [v[c * C:(c + 1) * C] for c in range(ncb)]


def _rows(parts):
    return jnp.concatenate(parts, axis=0)


def _block_gates(qz, fz, lb, ncb):
    C = HGRN_CHUNK
    row = lax.broadcasted_iota(jnp.int32, (C, C), 0)
    col = lax.broadcasted_iota(jnp.int32, (C, C), 1)
    tri = (col <= row).astype(F32)
    first_half = lax.broadcasted_iota(jnp.int32, (C, HEAD), 0) < C // 2
    sig = _sigmoid(fz)
    fg = lb + (1.0 - lb) * sig
    key = 1.0 - fg
    lg = jnp.log(fg)
    lgs = _chunks(lg, ncb)
    b = _rows([_dot_exact(tri, v) for v in lgs])
    r_c = [jnp.sum(jnp.where(first_half, v, 0.0), axis=0, keepdims=True) for v in lgs]
    bl_c = [jnp.sum(v, axis=0, keepdims=True) for v in lgs]
    r = _rows([jnp.broadcast_to(v, (C, HEAD)) for v in r_c])
    bl = _rows([jnp.broadcast_to(v, (C, HEAD)) for v in bl_c])
    sq = _sigmoid(qz)
    qy = qz * sq
    return sig, fg, key, b, r, bl, bl_c, sq, qy


def _hgrn_fwd(proj, logits, gain, tb=512):
    S = proj.shape[0]
    H, C = HGRN_HEADS, HGRN_CHUNK
    ncb = tb // C

    def body(q_ref, f_ref, i_ref, g_ref, lg_ref, gn_ref, o_ref, og_ref, st_ref, state):
        @pl.when(pl.program_id(1) == 0)
        def _():
            state[...] = jnp.zeros_like(state)

        lb = _lower_bound(lg_ref)
        causal = lax.broadcasted_iota(jnp.int32, (C, C), 1) <= lax.broadcasted_iota(jnp.int32, (C, C), 0)
        qz, fz, gz = q_ref[...], f_ref[...], g_ref[...]
        _, _, key, b, r, bl, bl_c, _, qy = _block_gates(qz, fz, lb, ncb)
        qs = _chunks((qy * jnp.exp(b - r)).astype(BF16), ncb)
        ks = _chunks((key * jnp.exp(r - b)).astype(BF16), ncb)
        qb = _chunks((qy * jnp.exp(b)).astype(BF16), ncb)
        ke = _chunks((key * jnp.exp(bl - b)).astype(BF16), ncb)
        vb = _chunks(i_ref[...].astype(BF16), ncb)
        o_intra, upd = [], []
        for c in range(ncb):
            a = jnp.where(causal, _dot_nt(qs[c], ks[c]), 0.0).astype(BF16)
            o_intra.append(_dot(a, vb[c]))
            upd.append(_dot_tn(vb[c], ke[c]))
        st = state[...]
        for c in range(ncb):
            st_ref[c] = st
            st = st * jnp.exp(bl_c[c]) + upd[c]
        state[...] = st
        o = _rows([_dot_nt(qb[c], st_ref[c].astype(BF16)) + o_intra[c] for c in range(ncb)])
        o_ref[...] = o
        og_ref[...] = ((o * _rstd(o) * gn_ref[...]) * (gz * _sigmoid(gz))).astype(BF16)

    def part(p):
        return pl.BlockSpec((tb, HEAD), functools.partial(lambda h, i, p: (i, p * H + h), p=p))

    return pl.pallas_call(
        body, name="hgrn_fwd", grid=(H, S // tb),
        in_specs=[part(0), part(1), part(2), part(3),
                  pl.BlockSpec((3, HEAD), lambda h, i: (0, h)),
                  pl.BlockSpec((1, HEAD), lambda h, i: (0, 0))],
        out_specs=[pl.BlockSpec((tb, HEAD), lambda h, i: (i, h)),
                   pl.BlockSpec((tb, HEAD), lambda h, i: (i, h)),
                   pl.BlockSpec((None, ncb, HEAD, HEAD), lambda h, i: (h, i, 0, 0))],
        out_shape=[jax.ShapeDtypeStruct((S, H * HEAD), F32),
                   jax.ShapeDtypeStruct((S, H * HEAD), BF16),
                   jax.ShapeDtypeStruct((H, S // C, HEAD, HEAD), F32)],
        scratch_shapes=[pltpu.VMEM((HEAD, HEAD), F32)],
        compiler_params=_params(("parallel", "arbitrary")))(proj, proj, proj, proj, logits, gain)


def _hgrn_bwd(proj, logits, gain, o, states, dog, tb=512):
    S = proj.shape[0]
    H, C = HGRN_HEADS, HGRN_CHUNK
    ncb = tb // C
    nb = S // tb

    def body(q_ref, f_ref, i_ref, g_ref, lg_ref, gn_ref, o_ref, st_ref, dog_ref,
             dp_ref, dlb_ref, dgn_ref, dstate, dst_scr):
        @pl.when(pl.program_id(1) == 0)
        def _():
            dstate[...] = jnp.zeros_like(dstate)
            dlb_ref[...] = jnp.zeros_like(dlb_ref)
            dgn_ref[...] = jnp.zeros_like(dgn_ref)

        lb = _lower_bound(lg_ref)
        oml = 1.0 - lb
        gn = gn_ref[...]
        row = lax.broadcasted_iota(jnp.int32, (C, C), 0)
        col = lax.broadcasted_iota(jnp.int32, (C, C), 1)
        causal = col <= row
        tri_up = (col >= row).astype(F32)
        qz, fz, gz = q_ref[...], f_ref[...], g_ref[...]
        sig, fg, key, b, r, bl, bl_c, sq, qy = _block_gates(qz, fz, lb, ncb)
        e_br, e_rb, e_b, e_lb = jnp.exp(b - r), jnp.exp(r - b), jnp.exp(b), jnp.exp(bl - b)
        qs_v, ks_v = (qy * e_br).astype(BF16), (key * e_rb).astype(BF16)
        qb_v, ke_v = (qy * e_b).astype(BF16), (key * e_lb).astype(BF16)
        qs, ks, qb, ke = _chunks(qs_v, ncb), _chunks(ks_v, ncb), _chunks(qb_v, ncb), _chunks(ke_v, ncb)
        vb = _chunks(i_ref[...].astype(BF16), ncb)
        ov = o_ref[...]
        rs = _rstd(ov)
        xh = ov * rs
        sg = _sigmoid(gz)
        dog_v = dog_ref[...]
        dgz = dog_v * (xh * gn) * (sg * (1.0 + gz * (1.0 - sg)))
        don = dog_v * (gz * sg)
        dgn_ref[...] += jnp.sum(don * xh, axis=0, keepdims=True)
        dyg = don * gn
        do = rs * (dyg - xh * jnp.mean(dyg * xh, axis=-1, keepdims=True))
        dob = _chunks(do.astype(BF16), ncb)
        dv_in, dqs, dks, wst = [], [], [], []
        for c in range(ncb):
            a = jnp.where(causal, _dot_nt(qs[c], ks[c]), 0.0).astype(BF16)
            da = jnp.where(causal, _dot_nt(dob[c], vb[c]), 0.0).astype(BF16)
            dv_in.append(_dot_tn(a, dob[c]))
            dqs.append(_dot(da, ks[c]))
            dks.append(_dot_tn(da, qs[c]))
            wst.append(_dot_tn(dob[c], qb[c]))
        e_l = [jnp.exp(v) for v in bl_c]
        dst = dstate[...]
        for c in reversed(range(ncb)):
            dst_scr[c] = dst
            dst = wst[c] + dst * e_l[c]
        dstate[...] = dst
        dv, dqb, dke, dbl_st = [], [], [], []
        for c in range(ncb):
            dst1 = dst_scr[c]
            st0 = st_ref[c]
            dst1b = dst1.astype(BF16)
            dv.append(dv_in[c] + _dot_nt(ke[c], dst1b))
            dqb.append(_dot(dob[c], st0.astype(BF16)))
            dke.append(_dot(vb[c], dst1b))
            dbl_st.append(jnp.sum(dst1 * st0, axis=0, keepdims=True) * e_l[c])
        dqs, dks, dqb, dke, dv = _rows(dqs), _rows(dks), _rows(dqb), _rows(dke), _rows(dv)
        dke_ke = dke * ke_v.astype(F32)
        db = dqs * qs_v.astype(F32) - dks * ks_v.astype(F32) + dqb * qb_v.astype(F32) - dke_ke
        dlg = []
        for c, (db_c, kk_c) in enumerate(zip(_chunks(db, ncb), _chunks(dke_ke, ncb))):
            dbl = jnp.sum(kk_c, axis=0, keepdims=True) + dbl_st[c]
            dlg.append(_dot_exact(tri_up, db_c) + dbl)
        dlg = _rows(dlg)
        dkey = dks * e_rb + dke * e_lb
        dqy = dqs * e_br + dqb * e_b
        dfg = dlg / fg - dkey
        dlb_ref[...] += jnp.sum(dfg * (1.0 - sig), axis=0, keepdims=True)
        dp_ref[0] = (dqy * (sq * (1.0 + qz * (1.0 - sq)))).astype(BF16)
        dp_ref[1] = (dfg * oml * sig * (1.0 - sig)).astype(BF16)
        dp_ref[2] = dv.astype(BF16)
        dp_ref[3] = dgz.astype(BF16)

    def part(p):
        return pl.BlockSpec((tb, HEAD), functools.partial(lambda h, i, p: (nb - 1 - i, p * H + h), p=p))

    blk = pl.BlockSpec((tb, HEAD), lambda h, i: (nb - 1 - i, h))
    acc = pl.BlockSpec((None, 1, HEAD), lambda h, i: (h, 0, 0))
    return pl.pallas_call(
        body, name="hgrn_bwd", grid=(H, nb),
        in_specs=[part(0), part(1), part(2), part(3),
                  pl.BlockSpec((3, HEAD), lambda h, i: (0, h)),
                  pl.BlockSpec((1, HEAD), lambda h, i: (0, 0)),
                  blk,
                  pl.BlockSpec((None, ncb, HEAD, HEAD), lambda h, i: (h, nb - 1 - i, 0, 0)),
                  blk],
        out_specs=[pl.BlockSpec((4, tb, HEAD), lambda h, i: (0, nb - 1 - i, h)), acc, acc],
        out_shape=[jax.ShapeDtypeStruct((4, S, H * HEAD), BF16),
                   jax.ShapeDtypeStruct((H, 1, HEAD), F32),
                   jax.ShapeDtypeStruct((H, 1, HEAD), F32)],
        scratch_shapes=[pltpu.VMEM((HEAD, HEAD), F32), pltpu.VMEM((ncb, HEAD, HEAD), F32)],
        compiler_params=_params(("parallel", "arbitrary")))(proj, proj, proj, proj, logits, gain, o, states, dog)


def _rope(v, cos, sin):
    return v * cos + pltpu.roll(v, HEAD // 2, 1) * sin


def _band_masks():
    qi = lax.broadcasted_iota(jnp.int32, (ATTN_SPAN, ATTN_SPAN), 0)
    kj = lax.broadcasted_iota(jnp.int32, (ATTN_SPAN, ATTN_SPAN), 1)
    return kj <= qi, kj >= qi


def _attn_fwd(a, cos, sin):
    d, L, _ = a.shape
    nb = L // ATTN_SPAN
    scale = HEAD ** -0.5

    def body(q_ref, kc_ref, kp_ref, vc_ref, vp_ref, cc_ref, cp_ref, sc_ref, sp_ref, o_ref, lse_ref):
        n = pl.program_id(1)
        mask_c, mask_p0 = _band_masks()
        mask_p = jnp.logical_and(mask_p0, n > 0)
        cc, cp, sc, sp = cc_ref[...], cp_ref[...], sc_ref[...], sp_ref[...]
        for hh in range(HEADS_PER_GROUP):
            cols = slice(hh * HEAD, (hh + 1) * HEAD)
            q = _rope(q_ref[:, cols], cc, sc).astype(BF16)
            kc = _rope(kc_ref[:, cols], cc, sc).astype(BF16)
            kp = _rope(kp_ref[:, cols], cp, sp).astype(BF16)
            s_c = jnp.where(mask_c, _dot_nt(q, kc) * scale, NEG)
            s_p = jnp.where(mask_p, _dot_nt(q, kp) * scale, NEG)
            m = jnp.maximum(jnp.max(s_c, axis=-1, keepdims=True), jnp.max(s_p, axis=-1, keepdims=True))
            p_c = jnp.exp(s_c - m)
            p_p = jnp.exp(s_p - m)
            l = jnp.sum(p_c, axis=-1, keepdims=True) + jnp.sum(p_p, axis=-1, keepdims=True)
            acc = _dot(p_c.astype(BF16), vc_ref[:, cols].astype(BF16)) + _dot(p_p.astype(BF16), vp_ref[:, cols].astype(BF16))
            o_ref[:, cols] = acc / l
            lse_ref[:, cols] = jnp.broadcast_to(m + jnp.log(l), (ATTN_SPAN, HEAD))

    def blk(part, prev):
        if prev:
            return pl.BlockSpec((None, ATTN_SPAN, GROUP_W), functools.partial(lambda r, n, p: (r, jnp.maximum(n - 1, 0), p), p=part))
        return pl.BlockSpec((None, ATTN_SPAN, GROUP_W), functools.partial(lambda r, n, p: (r, n, p), p=part))

    tab_c = pl.BlockSpec((None, ATTN_SPAN, HEAD), lambda r, n: (r, n, 0))
    tab_p = pl.BlockSpec((None, ATTN_SPAN, HEAD), lambda r, n: (r, jnp.maximum(n - 1, 0), 0))
    out = pl.BlockSpec((None, ATTN_SPAN, GROUP_W), lambda r, n: (r, n, 0))
    return pl.pallas_call(
        body, name=f"attn_fwd_d{d}", grid=(d, nb),
        in_specs=[blk(0, False), blk(1, False), blk(1, True), blk(2, False), blk(2, True), tab_c, tab_p, tab_c, tab_p],
        out_specs=[out, out],
        out_shape=[jax.ShapeDtypeStruct((d, L, GROUP_W), F32), jax.ShapeDtypeStruct((d, L, GROUP_W), F32)],
        compiler_params=_params(("parallel", "arbitrary")))(a, a, a, a, a, cos, cos, sin, sin)


def _attn_bwd(a, cos, sin, do, lse, dd):
    d, L, _ = a.shape
    nb = L // ATTN_SPAN
    scale = HEAD ** -0.5

    def body(qc_ref, qn_ref, kp_ref, kc_ref, vp_ref, vc_ref, doc_ref, don_ref, lc_ref, ln_ref, ddc_ref, ddn_ref,
             cp_ref, cc_ref, cn_ref, sp_ref, sc_ref, sn_ref, da_ref):
        n = pl.program_id(1)
        mask_c, mask_p0 = _band_masks()
        mask_p = jnp.logical_and(mask_p0, n > 0)
        mask_n = jnp.logical_and(mask_p0, n < nb - 1)
        cp, cc, cn, sp, sc, sn = cp_ref[...], cc_ref[...], cn_ref[...], sp_ref[...], sc_ref[...], sn_ref[...]
        for hh in range(HEADS_PER_GROUP):
            cols = slice(hh * HEAD, (hh + 1) * HEAD)
            q = _rope(qc_ref[:, cols], cc, sc).astype(BF16)
            qn = _rope(qn_ref[:, cols], cn, sn).astype(BF16)
            kc = _rope(kc_ref[:, cols], cc, sc).astype(BF16)
            kp = _rope(kp_ref[:, cols], cp, sp).astype(BF16)
            vc = vc_ref[:, cols].astype(BF16)
            vp = vp_ref[:, cols].astype(BF16)
            do_c = doc_ref[:, cols].astype(BF16)
            do_n = don_ref[:, cols].astype(BF16)
            lse_c, lse_n = lc_ref[:, cols], ln_ref[:, cols]
            dd_c, dd_n = ddc_ref[:, cols], ddn_ref[:, cols]
            p_c = jnp.where(mask_c, jnp.exp(_dot_nt(q, kc) * scale - lse_c), 0.0)
            p_p = jnp.where(mask_p, jnp.exp(_dot_nt(q, kp) * scale - lse_c), 0.0)
            ds_c = (p_c * (_dot_nt(do_c, vc) + dd_c)).astype(BF16)
            ds_p = (p_p * (_dot_nt(do_c, vp) + dd_c)).astype(BF16)
            dq = (_dot(ds_c, kc) + _dot(ds_p, kp)) * scale
            p_n = jnp.where(mask_n, jnp.exp(_dot_nt(qn, kc) * scale - lse_n), 0.0)
            ds_n = (p_n * (_dot_nt(do_n, vc) + dd_n)).astype(BF16)
            dk = (_dot_tn(ds_c, q) + _dot_tn(ds_n, qn)) * scale
            dv = _dot_tn(p_c.astype(BF16), do_c) + _dot_tn(p_n.astype(BF16), do_n)
            da_ref[:, cols] = _rope(dq, cc, -sc)
            da_ref[:, GROUP_W + hh * HEAD:GROUP_W + (hh + 1) * HEAD] = _rope(dk, cc, -sc)
            da_ref[:, 2 * GROUP_W + hh * HEAD:2 * GROUP_W + (hh + 1) * HEAD] = dv

    def rel(delta):
        if delta < 0:
            return lambda n: jnp.maximum(n - 1, 0)
        if delta > 0:
            return lambda n: jnp.minimum(n + 1, nb - 1)
        return lambda n: n

    def blk(width, part, delta):
        f = rel(delta)
        return pl.BlockSpec((None, ATTN_SPAN, width), functools.partial(lambda r, n, p, f: (r, f(n), p), p=part, f=f))

    g = GROUP_W
    return pl.pallas_call(
        body, name=f"attn_bwd_d{d}", grid=(d, nb),
        in_specs=[blk(g, 0, 0), blk(g, 0, 1), blk(g, 1, -1), blk(g, 1, 0), blk(g, 2, -1), blk(g, 2, 0),
                  blk(g, 0, 0), blk(g, 0, 1), blk(g, 0, 0), blk(g, 0, 1), blk(g, 0, 0), blk(g, 0, 1),
                  blk(HEAD, 0, -1), blk(HEAD, 0, 0), blk(HEAD, 0, 1), blk(HEAD, 0, -1), blk(HEAD, 0, 0), blk(HEAD, 0, 1)],
        out_specs=pl.BlockSpec((None, ATTN_SPAN, 3 * g), lambda r, n: (r, n, 0)),
        out_shape=jax.ShapeDtypeStruct((d, L, 3 * g), F32),
        compiler_params=_params(("parallel", "arbitrary")))(
            a, a, a, a, a, a, do, do, lse, lse, dd, dd, cos, cos, cos, sin, sin, sin)


def _group_weights(lse_refs, cols):
    ls = [r[:, cols] for r in lse_refs]
    mx = jnp.maximum(jnp.maximum(ls[0], ls[1]), ls[2])
    es = [jnp.exp(v - mx) for v in ls]
    tot = es[0] + es[1] + es[2]
    return [e / tot for e in es]


def _gather_tokens(ref, scr, d, tm):
    if d == 1:
        return ref.at[0]
    for r in range(d):
        scr[pl.ds(r, tm // d, stride=d), :] = ref[r]
    return scr


def _scatter_tokens(scr, ref, d, tm):
    if d == 1:
        ref[0] = scr[...]
        return
    for r in range(d):
        ref[r] = scr[pl.ds(r, tm // d, stride=d), :]


def _head_spec(d, tm):
    return pl.BlockSpec((d, tm // d, HEAD), lambda i, j: (0, i, j))


def _dilate_group(qkv, g, d, tm=1024):
    S = qkv.shape[0]
    G = len(ATTN_GROUPS)

    def body(x_ref, out_ref):
        _scatter_tokens(x_ref, out_ref, d, tm)

    def src(i, j):
        return i, ((j // HEADS_PER_GROUP) * G + g) * HEADS_PER_GROUP + j % HEADS_PER_GROUP

    return pl.pallas_call(
        body, name=f"attn_dilate_d{d}", grid=(S // tm, 3 * HEADS_PER_GROUP),
        in_specs=[pl.BlockSpec((tm, HEAD), src)],
        out_specs=_head_spec(d, tm),
        out_shape=jax.ShapeDtypeStruct((d, S // d, 3 * GROUP_W), F32),
        compiler_params=_params(("parallel", "parallel")))(qkv)


def _undilate_group(da, dqkv, g, tm=1024):
    d, L, _ = da.shape
    S = d * L
    G = len(ATTN_GROUPS)

    def body(*refs):
        da_ref, out_ref, scr = refs[0], refs[-2], refs[-1]
        out_ref[...] = _gather_tokens(da_ref, scr, d, tm)[...].astype(BF16)

    def dst(i, j):
        return i, ((j // HEADS_PER_GROUP) * G + g) * HEADS_PER_GROUP + j % HEADS_PER_GROUP

    operands = (da,) if dqkv is None else (da, dqkv)
    return pl.pallas_call(
        body, name=f"attn_undilate_d{d}", grid=(S // tm, 3 * HEADS_PER_GROUP),
        in_specs=[_head_spec(d, tm)] + ([] if dqkv is None else [ANY]),
        out_specs=pl.BlockSpec((tm, HEAD), dst),
        out_shape=jax.ShapeDtypeStruct((S, 3 * G * GROUP_W), BF16),
        input_output_aliases={} if dqkv is None else {1: 0},
        scratch_shapes=[pltpu.VMEM((tm, HEAD), F32)],
        compiler_params=_params(("parallel", "parallel")))(*operands)


def _attn_merge(os_, lses, tm=512):
    G = len(os_)
    S = os_[0].shape[0] * os_[0].shape[1]

    def body(*refs):
        o_refs, l_refs, out_ref = refs[:G], refs[G:2 * G], refs[2 * G]
        scr = refs[2 * G + 1:]
        hh = pl.program_id(1)
        o_tok = [_gather_tokens(o_refs[g], scr[g], d, tm) for g, (_, d) in enumerate(ATTN_GROUPS)]
        l_tok = [_gather_tokens(l_refs[g], scr[G + g], d, tm) for g, (_, d) in enumerate(ATTN_GROUPS)]
        al = _group_weights(l_tok, slice(None))
        for g in range(G):
            cols = pl.ds(pl.multiple_of(g * GROUP_W + hh * HEAD, HEAD), HEAD)
            out_ref[:, cols] = (o_tok[g][...] * al[g]).astype(BF16)

    specs = [_head_spec(d, tm) for _, d in ATTN_GROUPS]
    return pl.pallas_call(
        body, name="attn_merge", grid=(S // tm, HEADS_PER_GROUP),
        in_specs=specs + specs,
        out_specs=pl.BlockSpec((tm, G * GROUP_W), lambda i, j: (i, 0)),
        out_shape=jax.ShapeDtypeStruct((S, G * GROUP_W), BF16),
        scratch_shapes=[pltpu.VMEM((tm, HEAD), F32)] * (2 * G),
        compiler_params=_params(("parallel", "arbitrary")))(*os_, *lses)


def _attn_merge_bwd(os_, lses, doa, tm=512):
    G = len(os_)
    S = doa.shape[0]

    def body(*refs):
        o_refs, l_refs, doa_ref = refs[:G], refs[G:2 * G], refs[2 * G]
        do_refs, dd_refs = refs[2 * G + 1:3 * G + 1], refs[3 * G + 1:4 * G + 1]
        scr = refs[4 * G + 1:]
        hh = pl.program_id(1)
        o_tok = [_gather_tokens(o_refs[g], scr[g], d, tm) for g, (_, d) in enumerate(ATTN_GROUPS)]
        l_tok = [_gather_tokens(l_refs[g], scr[G + g], d, tm) for g, (_, d) in enumerate(ATTN_GROUPS)]
        do_tok, dd_tok = scr[2 * G:3 * G], scr[3 * G:]
        al = _group_weights(l_tok, slice(None))
        mix = None
        for g in range(G):
            dg = doa_ref[:, pl.ds(pl.multiple_of(g * GROUP_W + hh * HEAD, HEAD), HEAD)]
            do_tok[g][...] = dg * al[g]
            t = al[g] * jnp.sum(dg * o_tok[g][...], axis=-1, keepdims=True)
            mix = t if mix is None else mix + t
        for g, (_, d) in enumerate(ATTN_GROUPS):
            dd_tok[g][...] = jnp.broadcast_to(-al[g] * mix, (tm, HEAD))
            _scatter_tokens(do_tok[g], do_refs[g], d, tm)
            _scatter_tokens(dd_tok[g], dd_refs[g], d, tm)

    specs = [_head_spec(d, tm) for _, d in ATTN_GROUPS]
    shapes = [jax.ShapeDtypeStruct((d, S // d, GROUP_W), F32) for _, d in ATTN_GROUPS]
    return pl.pallas_call(
        body, name="attn_merge_bwd", grid=(S // tm, HEADS_PER_GROUP),
        in_specs=specs + specs + [pl.BlockSpec((tm, G * GROUP_W), lambda i, j: (i, 0))],
        out_specs=specs + specs,
        out_shape=shapes + shapes,
        scratch_shapes=[pltpu.VMEM((tm, HEAD), F32)] * (4 * G),
        compiler_params=_params(("parallel", "arbitrary")))(*os_, *lses, doa)


def _to_dilated(v, d):
    S, W = v.shape
    return v.reshape(S // d, d, W).transpose(1, 0, 2)


def _rope_tables(S):
    inv_freq = 1.0 / (ROPE_THETA ** (jnp.arange(0, HEAD, 2, dtype=F32) / HEAD))
    ang = jnp.arange(S, dtype=F32)[:, None] * inv_freq[None, :]
    cos, sin = jnp.cos(ang), jnp.sin(ang)
    return jnp.concatenate([cos, cos], axis=-1), jnp.concatenate([-sin, sin], axis=-1)


def _local_step(x, target, norm_mix, norm_ffn, lb_logits, out_gain, final_norm, w):
    S = x.shape[0]
    nm0, nm1 = norm_mix[0:1], norm_mix[1:2]
    nf0, nf1 = norm_ffn[0:1], norm_ffn[1:2]
    fin_tn = w["fin0"].shape[2]

    proj = _norm_mm(x, nm0, w["hin"], D_MODEL, "hgrn_in")
    o, og, states = _hgrn_fwd(proj, lb_logits, out_gain)
    h1 = _mm_res(x, og, w["hout"], "hgrn_out")
    z0 = _norm_mm(h1, nf0, w["fin0"], fin_tn, "ffn0_in")
    h2 = _swiglu_mm_res(h1, z0, w["fdn0"], "ffn0_down")
    qkv = _norm_mm(h2, nm1, w["qkv"], w["qkv"].shape[2], "attn_qkv")
    cos, sin = _rope_tables(S)
    a_g = [_dilate_group(qkv, gi, d) for gi, (_, d) in enumerate(ATTN_GROUPS)]
    cos_g, sin_g, o_g, lse_g = [], [], [], []
    for gi, (_, d) in enumerate(ATTN_GROUPS):
        cd, sd = _to_dilated(cos, d), _to_dilated(sin, d)
        og_, lse_ = _attn_fwd(a_g[gi], cd, sd)
        cos_g.append(cd), sin_g.append(sd), o_g.append(og_), lse_g.append(lse_)
    oa = _attn_merge(o_g, lse_g)
    h3 = _mm_res(h2, oa, w["aout"], "attn_out")
    z1 = _norm_mm(h3, nf1, w["fin1"], fin_tn, "ffn1_in")
    h4 = _swiglu_mm_res(h3, z1, w["fdn1"], "ffn1_down")
    dh4, loss, d_final = _loss_head(h4, final_norm, target)

    grads, small = {}, {"final_norm": d_final}

    def ffn_bwd(dh, h_in, z, gain, w_in, w_dn, tag):
        dz = _mm_nt_swiglu_bwd(dh, w_dn, z, tag + "_down_dx")
        g_dn = _mm_tn("swiglu", (z,), dh, 1, D_MODEL, 512, tag + "_down_dw")
        g_in = _mm_tn("norm", (h_in, gain), dz, N_CHIPS, fin_tn, fin_tn, tag + "_in_dw")
        dh_in, dgain = _mm_nt_normbwd(dz, w_in, h_in, gain, dh, tag + "_in_dx")
        return dh_in, dgain, g_in, g_dn[0]

    dh3, d_nf1, grads["fin1"], grads["fdn1"] = ffn_bwd(dh4, h3, z1, nf1, w["fin1"], w["fdn1"], "ffn1")
    doa = _mm_nt(dh3, w["aout"][None], "attn_out_dx")
    grads["aout"] = _mm_tn("plain", (oa,), dh3, 1, D_MODEL, 512, "attn_out_dw")[0]
    merged = _attn_merge_bwd(o_g, lse_g, doa)
    G = len(ATTN_GROUPS)
    das = [_attn_bwd(a_g[gi], cos_g[gi], sin_g[gi], merged[gi], lse_g[gi], merged[G + gi]) for gi in range(G)]
    dqkv = None
    for gi in range(G):
        dqkv = _undilate_group(das[gi], dqkv, gi)
    n_qkv = w["qkv"].shape[2]
    grads["qkv"] = _mm_tn("norm", (h2, nm1), dqkv, N_CHIPS, n_qkv, n_qkv, "attn_qkv_dw")
    dh2, d_nm1 = _mm_nt_normbwd(dqkv, w["qkv"], h2, nm1, dh3, "attn_qkv_dx")

    dh1, d_nf0, grads["fin0"], grads["fdn0"] = ffn_bwd(dh2, h1, z0, nf0, w["fin0"], w["fdn0"], "ffn0")
    dog = _mm_nt(dh1, w["hout"][None], "hgrn_out_dx")
    grads["hout"] = _mm_tn("plain", (og,), dh1, 1, D_MODEL, 512, "hgrn_out_dw")[0]
    dproj, dlb, dgn = _hgrn_bwd(proj, lb_logits, out_gain, o, states, dog)
    grads["hin"] = _mm_tn("norm", (x, nm0), dproj, N_CHIPS, D_MODEL, D_MODEL, "hgrn_in_dw")
    dx, d_nm0 = _mm_nt_normbwd(dproj, w["hin"], x, nm0, dh1, "hgrn_in_dx")

    small["norm_mix"] = jnp.concatenate([d_nm0, d_nm1], axis=0)
    small["norm_ffn"] = jnp.concatenate([d_nf0, d_nf1], axis=0)
    small["lb"] = dlb.reshape(1, HGRN_HEADS * HEAD)
    small["out_norm"] = dgn.reshape(HGRN_HEADS, HEAD)
    return loss, dx, grads, small


def _place():
    x, y, c = lax.axis_index("x"), lax.axis_index("y"), lax.axis_index("c")
    others = [(1 - x, y), (x, 1 - y), (1 - x, 1 - y)]
    return x, y, c, others


ANY = pl.BlockSpec(memory_space=pl.ANY)


def _gather_weights(shards):
    n = len(shards)

    def body(*refs):
        ins, outs = refs[:n], refs[n:2 * n]
        ici_send, ici_recv, d2d_send, d2d_recv, own_send, own_recv = refs[2 * n:]
        x, y, c, others = _place()
        me = 2 * x + y
        sibling = (x, y, 1 - c)
        own = [pltpu.make_async_remote_copy(
            src_ref=ins[a], dst_ref=outs[a].at[me], send_sem=own_send.at[a], recv_sem=own_recv.at[a],
            device_id=sibling, device_id_type=MESH) for a in range(n)]
        for cp in own:
            cp.start()
        sends, passes = [], []
        for a in range(n):
            for k, (ox, oy) in enumerate(others):
                s = a * 3 + k
                sends.append(pltpu.make_async_remote_copy(
                    src_ref=ins[a].at[c], dst_ref=outs[a].at[me, c], send_sem=ici_send.at[s], recv_sem=ici_recv.at[s],
                    device_id=(ox, oy, c), device_id_type=MESH))
        for cp in sends:
            cp.start()
        for a in range(n):
            for k, (ox, oy) in enumerate(others):
                s = a * 3 + k
                got = outs[a].at[2 * ox + oy, c]
                pltpu.make_async_remote_copy(
                    src_ref=got, dst_ref=got, send_sem=ici_send.at[s], recv_sem=ici_recv.at[s],
                    device_id=(ox, oy, c), device_id_type=MESH).wait_recv()
                fwd = pltpu.make_async_remote_copy(
                    src_ref=got, dst_ref=got, send_sem=d2d_send.at[s], recv_sem=d2d_recv.at[s],
                    device_id=sibling, device_id_type=MESH)
                fwd.start()
                passes.append(fwd)
        for a in range(n):
            for k, (ox, oy) in enumerate(others):
                s = a * 3 + k
                theirs = outs[a].at[2 * ox + oy, 1 - c]
                pltpu.make_async_remote_copy(
                    src_ref=theirs, dst_ref=theirs, send_sem=d2d_send.at[s], recv_sem=d2d_recv.at[s],
                    device_id=sibling, device_id_type=MESH).wait_recv()
        for cp in own:
            cp.wait()
        for cp in sends + passes:
            cp.wait_send()

    return pl.pallas_call(
        body, name="gather_weights",
        in_specs=[ANY] * n, out_specs=[ANY] * n,
        out_shape=[jax.ShapeDtypeStruct((N_CHIPS,) + s.shape, s.dtype) for s in shards],
        scratch_shapes=[pltpu.SemaphoreType.DMA((3 * n,)), pltpu.SemaphoreType.DMA((3 * n,)),
                        pltpu.SemaphoreType.DMA((3 * n,)), pltpu.SemaphoreType.DMA((3 * n,)),
                        pltpu.SemaphoreType.DMA((n,)), pltpu.SemaphoreType.DMA((n,))],
        )(*shards)


def _pair_exchange(grads):
    n = len(grads)

    def body(*refs):
        ins, outs = refs[:n], refs[n:2 * n]
        send_sem, recv_sem = refs[2 * n:]
        x, y, c, _ = _place()
        sibling = (x, y, 1 - c)
        cps = []
        for a in range(n):
            for j in range(N_CHIPS):
                s = a * N_CHIPS + j
                cps.append(pltpu.make_async_remote_copy(
                    src_ref=ins[a].at[j, 1 - c], dst_ref=outs[a].at[j], send_sem=send_sem.at[s], recv_sem=recv_sem.at[s],
                    device_id=sibling, device_id_type=MESH))
        for cp in cps:
            cp.start()
        for cp in cps:
            cp.wait()

    return pl.pallas_call(
        body, name="grad_pair_exchange",
        in_specs=[ANY] * n, out_specs=[ANY] * n,
        out_shape=[jax.ShapeDtypeStruct((N_CHIPS,) + g.shape[2:], F32) for g in grads],
        scratch_shapes=[pltpu.SemaphoreType.DMA((N_CHIPS * n,)), pltpu.SemaphoreType.DMA((N_CHIPS * n,))],
        )(*grads)


def _pair_sum(g, got, c_idx):
    _, _, r, cw = g.shape
    tr = _row_tile(r, cw)

    def body(c_ref, g_ref, got_ref, p_ref, pb_ref):
        v = g_ref[...] + got_ref[...]
        p_ref[...] = v
        pb_ref[...] = v.astype(BF16)

    blk = pl.BlockSpec((None, tr, cw), lambda j, i, c_ref: (j, i, 0))
    return pl.pallas_call(
        body, name="grad_pair_sum",
        grid_spec=pltpu.PrefetchScalarGridSpec(
            num_scalar_prefetch=1, grid=(N_CHIPS, r // tr),
            in_specs=[pl.BlockSpec((None, None, tr, cw), lambda j, i, c_ref: (j, c_ref[0], i, 0)), blk],
            out_specs=[blk, blk]),
        out_shape=[jax.ShapeDtypeStruct((N_CHIPS, r, cw), F32), jax.ShapeDtypeStruct((N_CHIPS, r, cw), BF16)],
        compiler_params=_params(("parallel", "parallel")))(c_idx, g, got)


def _chip_exchange(parts):
    n = len(parts)

    def body(*refs):
        ins, outs = refs[:n], refs[n:2 * n]
        send_sem, recv_sem = refs[2 * n:]
        x, y, c, others = _place()
        me = 2 * x + y
        cps = []
        for a in range(n):
            for k, (ox, oy) in enumerate(others):
                s = a * 3 + k
                cps.append(pltpu.make_async_remote_copy(
                    src_ref=ins[a].at[2 * ox + oy], dst_ref=outs[a].at[me], send_sem=send_sem.at[s], recv_sem=recv_sem.at[s],
                    device_id=(ox, oy, c), device_id_type=MESH))
        for cp in cps:
            cp.start()
        for a in range(n):
            for k, (ox, oy) in enumerate(others):
                s = a * 3 + k
                got = outs[a].at[2 * ox + oy]
                pltpu.make_async_remote_copy(
                    src_ref=got, dst_ref=got, send_sem=send_sem.at[s], recv_sem=recv_sem.at[s],
                    device_id=(ox, oy, c), device_id_type=MESH).wait_recv()
        for cp in cps:
            cp.wait_send()

    return pl.pallas_call(
        body, name="grad_chip_exchange",
        in_specs=[ANY] * n, out_specs=[ANY] * n,
        out_shape=[jax.ShapeDtypeStruct(p.shape, BF16) for p in parts],
        scratch_shapes=[pltpu.SemaphoreType.DMA((3 * n,)), pltpu.SemaphoreType.DMA((3 * n,))],
        )(*parts)


def _chip_sum(p, got, me_idx):
    _, r, cw = p.shape
    tr = _row_tile(r, cw)

    def body(me_ref, own_ref, got_ref, t_ref):
        me = me_ref[0]
        acc = None
        for s in range(N_CHIPS):
            term = jnp.where(me == s, own_ref[...], got_ref[s].astype(F32))
            acc = term if acc is None else acc + term
        t_ref[...] = acc

    return pl.pallas_call(
        body, name="grad_chip_sum",
        grid_spec=pltpu.PrefetchScalarGridSpec(
            num_scalar_prefetch=1, grid=(r // tr,),
            in_specs=[pl.BlockSpec((None, tr, cw), lambda i, me_ref: (me_ref[0], i, 0)),
                      pl.BlockSpec((N_CHIPS, tr, cw), lambda i, me_ref: (0, i, 0))],
            out_specs=pl.BlockSpec((tr, cw), lambda i, me_ref: (i, 0))),
        out_shape=jax.ShapeDtypeStruct((r, cw), F32),
        compiler_params=_params(("parallel",)))(me_idx, p, got)


def _pair_share(halves):
    n = len(halves)

    def body(*refs):
        ins, outs = refs[:n], refs[n:2 * n]
        send_sem, recv_sem = refs[2 * n:]
        x, y, c, _ = _place()
        cps = [pltpu.make_async_remote_copy(
            src_ref=ins[a], dst_ref=outs[a], send_sem=send_sem.at[a], recv_sem=recv_sem.at[a],
            device_id=(x, y, 1 - c), device_id_type=MESH) for a in range(n)]
        for cp in cps:
            cp.start()
        for cp in cps:
            cp.wait()

    return pl.pallas_call(
        body, name="grad_pair_share",
        in_specs=[ANY] * n, out_specs=[ANY] * n,
        out_shape=[jax.ShapeDtypeStruct(h.shape, F32) for h in halves],
        scratch_shapes=[pltpu.SemaphoreType.DMA((n,)), pltpu.SemaphoreType.DMA((n,))],
        )(*halves)


def _small_allreduce(pack):
    m_per, ncol = pack.shape
    n_dev = 8

    def body(x_ref, sum_ref, all_ref, send_sems, recv_sems, local_sem):
        x, y, c, others = _place()
        me, sibling = (x, y, c), (x, y, 1 - c)

        def rows(px, py, pc):
            return all_ref.at[pl.ds((4 * px + 2 * py + pc) * m_per, m_per), :]

        def copy(k, block, to, src=None):
            return pltpu.make_async_remote_copy(
                src_ref=rows(*block) if src is None else src, dst_ref=rows(*block),
                send_sem=send_sems.at[k], recv_sem=recv_sems.at[k], device_id=to, device_id_type=MESH)

        mine = pltpu.make_async_copy(x_ref, rows(*me), local_sem)
        mine.start()
        first = [copy(0, me, sibling, src=x_ref)]
        first += [copy(1 + j, me, (*chip, c), src=x_ref) for j, chip in enumerate(others)]
        for cp in first:
            cp.start()
        passed = [copy(4 + j, (*chip, c), sibling) for j, chip in enumerate(others)]
        for j, chip in enumerate(others):
            copy(1 + j, (*chip, c), me).wait_recv()
            passed[j].start()
        copy(0, sibling, me).wait_recv()
        for j, chip in enumerate(others):
            copy(4 + j, (*chip, 1 - c), me).wait_recv()
        for cp in first + passed:
            cp.wait_send()
        mine.wait()
        acc = all_ref[0:m_per, :]
        for dvc in range(1, n_dev):
            acc = acc + all_ref[dvc * m_per:(dvc + 1) * m_per, :]
        sum_ref[...] = acc

    return pl.pallas_call(
        body, name="small_allreduce",
        in_specs=[pl.BlockSpec(memory_space=pltpu.VMEM)],
        out_specs=pl.BlockSpec(memory_space=pltpu.VMEM),
        out_shape=jax.ShapeDtypeStruct((m_per, ncol), F32),
        scratch_shapes=[pltpu.VMEM((n_dev * m_per, ncol), F32),
                        pltpu.SemaphoreType.DMA((7,)), pltpu.SemaphoreType.DMA((7,)), pltpu.SemaphoreType.DMA],
        )(pack)


def _adam_math(w, g, m, v):
    m = ADAM_B1 * m + (1.0 - ADAM_B1) * g
    v = ADAM_B2 * v + (1.0 - ADAM_B2) * (g * g)
    m_hat = m / (1.0 - ADAM_B1 ** ADAM_STEP)
    v_hat = v / (1.0 - ADAM_B2 ** ADAM_STEP)
    delta = -ADAM_LR * (m_hat / (jnp.sqrt(v_hat) + ADAM_EPS) + ADAM_WD * w)
    return delta, m, v


def _adamw(mine, theirs, c_idx, w, m, v, name):
    r, C = mine.shape
    tr = _row_tile(r, C, 512 * 1024)
    nt = r // tr

    def body(c_ref, mine_ref, theirs_ref, w_ref, m_ref, v_ref, g_ref, d_ref, nm_ref, nv_ref):
        g = jnp.where(pl.program_id(0) == c_ref[0], mine_ref[...], theirs_ref[...])
        g_ref[...] = g
        d_ref[...], nm_ref[...], nv_ref[...] = _adam_math(w_ref[...], g, m_ref[...], v_ref[...])

    half = pl.BlockSpec((tr, C), lambda h, i, c_ref: (i, 0))
    full = pl.BlockSpec((tr, C), lambda h, i, c_ref: (h * nt + i, 0))
    shp = jax.ShapeDtypeStruct((2 * r, C), F32)
    return pl.pallas_call(
        body, name=name,
        grid_spec=pltpu.PrefetchScalarGridSpec(
            num_scalar_prefetch=1, grid=(2, nt),
            in_specs=[half, half, full, full, full], out_specs=[full] * 4),
        out_shape=[shp] * 4,
        compiler_params=_params(("parallel", "parallel")))(c_idx, mine, theirs, w, m, v)


def _small_update(gsum, logits_pack, w, m, v):
    def body(gs_ref, lg_ref, w_ref, m_ref, v_ref, g_ref, d_ref, nm_ref, nv_ref):
        g_ref[...] = gs_ref[...]
        l0, l1, l2 = lg_ref[0:1, :], lg_ref[1:2, :], lg_ref[2:3, :]
        mx = jnp.maximum(jnp.maximum(l0, l1), l2)
        e0, e1, e2 = jnp.exp(l0 - mx), jnp.exp(l1 - mx), jnp.exp(l2 - mx)
        tot = e0 + e1 + e2
        p0, p1, p2 = e0 / tot, e1 / tot, e2 / tot
        dlb = gs_ref[4:5, :]
        g_ref[4:5, :] = dlb * p0 * (1.0 - p0)
        g_ref[5:6, :] = -dlb * p0 * p1
        g_ref[6:7, :] = -dlb * p0 * p2
        d_ref[...], nm_ref[...], nv_ref[...] = _adam_math(w_ref[...], g_ref[...], m_ref[...], v_ref[...])

    full = pl.BlockSpec(memory_space=pltpu.VMEM)
    shp = jax.ShapeDtypeStruct(gsum.shape, F32)
    return pl.pallas_call(
        body, name="small_update", in_specs=[full] * 5, out_specs=[full] * 4, out_shape=[shp] * 4)(
            gsum, logits_pack, w, m, v)


def _pack_small(norm_mix, norm_ffn, lb3, out_norm, final_norm, extra=None):
    ncol = norm_mix.shape[1]
    on = jnp.pad(out_norm.reshape(1, -1), ((0, 0), (0, ncol - out_norm.size)))
    rows = [norm_mix, norm_ffn, lb3, on, final_norm.reshape(1, ncol)]
    if extra is not None:
        rows.append(extra)
    used = sum(r.shape[0] for r in rows)
    rows.append(jnp.zeros((SMALL_ROWS - used, ncol), F32))
    return jnp.concatenate(rows, axis=0)


WEIGHT_NAMES = ("hin", "hout", "qkv", "aout", "fin0", "fin1", "fdn0", "fdn1")


def _split_weights(hgrn_w_in, hgrn_w_out, attn_w_qkv, attn_w_out, ffn_w_in, ffn_w_down):
    return {"hin": hgrn_w_in[0], "hout": hgrn_w_out[0], "qkv": attn_w_qkv[0], "aout": attn_w_out[0],
            "fin0": ffn_w_in[0], "fin1": ffn_w_in[1], "fdn0": ffn_w_down[0], "fdn1": ffn_w_down[1]}


def _halves(v):
    r, c = v.shape
    return v.reshape(2, r // 2, c)


def _full_weights(gathered):
    out = {}
    for k, g in gathered.items():
        _, _, r, c = g.shape
        if k in ("hin", "qkv", "fin0", "fin1"):
            out[k] = g.reshape(N_CHIPS, 2 * r, c)
        else:
            out[k] = g.reshape(N_CHIPS * 2 * r, c)
    return out


def kernel(x, norm_mix, norm_ffn, hgrn_w_in, hgrn_lb_logits, hgrn_out_norm, hgrn_w_out, attn_w_qkv, attn_w_out, ffn_w_in, ffn_w_down, final_norm, loss_target, m_norm_mix, m_norm_ffn, m_hgrn_w_in, m_hgrn_lb_logits, m_hgrn_out_norm, m_hgrn_w_out, m_attn_w_qkv, m_attn_w_out, m_ffn_w_in, m_ffn_w_down, m_final_norm, v_norm_mix, v_norm_ffn, v_hgrn_w_in, v_hgrn_lb_logits, v_hgrn_out_norm, v_hgrn_w_out, v_attn_w_qkv, v_attn_w_out, v_ffn_w_in, v_ffn_w_down, v_final_norm):
    S = x.shape[1]
    xi, yi, ci = lax.axis_index("x"), lax.axis_index("y"), lax.axis_index("c")
    c_idx = jnp.reshape(ci, (1,)).astype(jnp.int32)
    me_idx = jnp.reshape(2 * xi + yi, (1,)).astype(jnp.int32)

    w_own = _split_weights(hgrn_w_in, hgrn_w_out, attn_w_qkv, attn_w_out, ffn_w_in, ffn_w_down)
    m_own = _split_weights(m_hgrn_w_in, m_hgrn_w_out, m_attn_w_qkv, m_attn_w_out, m_ffn_w_in, m_ffn_w_down)
    v_own = _split_weights(v_hgrn_w_in, v_hgrn_w_out, v_attn_w_qkv, v_attn_w_out, v_ffn_w_in, v_ffn_w_down)

    gathered = _gather_weights([_halves(w_own[k].astype(BF16)) for k in WEIGHT_NAMES])
    w_full = _full_weights(dict(zip(WEIGHT_NAMES, gathered)))

    loss, dx, grads, small = _local_step(
        x.reshape(S, D_MODEL), loss_target.reshape(S, D_MODEL), norm_mix, norm_ffn, hgrn_lb_logits,
        hgrn_out_norm, final_norm.reshape(1, D_MODEL), w_full)

    g4 = []
    for k in WEIGHT_NAMES:
        r, c = w_own[k].shape
        g4.append(grads[k].reshape(N_CHIPS, 2, r // 2, c))
    from_sibling = _pair_exchange(g4)
    sums = [_pair_sum(g, got, c_idx) for g, got in zip(g4, from_sibling)]
    from_chips = _chip_exchange([s[1] for s in sums])
    halves = [_chip_sum(s[0], got, me_idx) for s, got in zip(sums, from_chips)]
    from_pair = _pair_share(halves)

    g_out, d_out, m_out, v_out = {}, {}, {}, {}
    for k, mine, theirs in zip(WEIGHT_NAMES, halves, from_pair):
        g_out[k], d_out[k], m_out[k], v_out[k] = _adamw(mine, theirs, c_idx, w_own[k], m_own[k], v_own[k], "adamw_" + k)

    loss_row = jnp.pad(loss, ((0, 0), (0, D_MODEL - loss.shape[1])))
    lb3 = jnp.concatenate([small["lb"], jnp.zeros((2, D_MODEL), F32)], axis=0)
    on_grad = jnp.sum(small["out_norm"], axis=0, keepdims=True)
    pack = _pack_small(small["norm_mix"], small["norm_ffn"], lb3, on_grad, small["final_norm"], loss_row)
    gsum = _small_allreduce(pack)
    w_s = _pack_small(norm_mix, norm_ffn, hgrn_lb_logits, hgrn_out_norm, final_norm)
    m_s = _pack_small(m_norm_mix, m_norm_ffn, m_hgrn_lb_logits, m_hgrn_out_norm, m_final_norm)
    v_s = _pack_small(v_norm_mix, v_norm_ffn, v_hgrn_lb_logits, v_hgrn_out_norm, v_final_norm)
    lg_pack = jnp.pad(hgrn_lb_logits, ((0, 8 - hgrn_lb_logits.shape[0]), (0, 0)))
    sg, sd, sm, sv = _small_update(gsum, lg_pack, w_s, m_s, v_s)

    def unpack(p):
        return (p[0:2], p[2:4], p[4:7], p[7:8, :HEAD], p[8])

    def big(dct):
        return (dct["hin"][None], dct["hout"][None], dct["qkv"][None], dct["aout"][None],
                jnp.stack([dct["fin0"], dct["fin1"]]), jnp.stack([dct["fdn0"], dct["fdn1"]]))

    def assemble(p, dct):
        nmx, nff, lbl, onm, fnm = unpack(p)
        hin, hout, qkv, aout, fin, fdn = big(dct)
        return (nmx, nff, hin, lbl, onm, hout, qkv, aout, fin, fdn, fnm)

    total_loss = gsum[9, 0]
    return (total_loss, dx.reshape(1, S, D_MODEL), *assemble(sg, g_out), *assemble(sd, d_out),
            *assemble(sm, m_out), *assemble(sv, v_out))
```

```python
import functools

import jax
import jax.numpy as jnp
from jax import lax
from jax.experimental import pallas as pl
from jax.experimental.pallas import tpu as pltpu

F32 = jnp.float32
BF16 = jnp.bfloat16
MESH = pl.DeviceIdType.MESH

D_MODEL = 1024
HEAD = 128
HGRN_HEADS = 8
HGRN_CHUNK = 64
ATTN_GROUPS = ((128, 1), (512, 4), (2048, 16))
ATTN_SPAN = 128
HEADS_PER_GROUP = 4
GROUP_W = HEADS_PER_GROUP * HEAD
D_FF = 2816
NORM_EPS = 1e-6
ROPE_THETA = 10000.0
NEG = -1e30

ADAM_LR, ADAM_B1, ADAM_B2, ADAM_EPS, ADAM_WD, ADAM_STEP = 0.001, 0.9, 0.999, 1e-08, 0.01, 10

N_CHIPS = 4
VMEM_LIMIT = 56 * 1024 * 1024
SMALL_ROWS = 16


def _params(sem=None):
    return pltpu.CompilerParams(dimension_semantics=sem, vmem_limit_bytes=VMEM_LIMIT)


def _row_tile(rows, cols, budget_bytes=3 * 512 * 1024):
    best = 8
    for t in range(8, rows + 1, 8):
        if rows % t == 0 and t * cols * 4 <= budget_bytes:
            best = t
    assert rows % best == 0
    return best


def _grid_corner(i, j):
    return jnp.logical_and(pl.program_id(0) == i, pl.program_id(1) == j)


def _sigmoid(v):
    return 1.0 / (1.0 + jnp.exp(-v))


def _dot(a, b):
    return jnp.dot(a, b, preferred_element_type=F32)


def _dot_nt(a, b):
    return lax.dot_general(a, b, (((1,), (1,)), ((), ())), preferred_element_type=F32)


def _dot_tn(a, b):
    return lax.dot_general(a, b, (((0,), (0,)), ((), ())), preferred_element_type=F32)


def _dot_exact(a, b):
    return jnp.dot(a, b, preferred_element_type=F32, precision=lax.Precision.HIGHEST)


def _rstd(v):
    return lax.rsqrt(jnp.mean(v * v, axis=-1, keepdims=True) + NORM_EPS)


def _norm_mm(h, gain, w3, tn, name, tm=512, rider=None):
    S, K = h.shape
    J, _, n = w3.shape
    tpn = n // tn
    gi, gj = S // tm, J * tpn

    def body(h_ref, g_ref, w_ref, y_ref, u_scr):
        @pl.when(pl.program_id(1) == 0)
        def _():
            v = h_ref[...]
            u_scr[...] = (v * _rstd(v) * g_ref[...]).astype(BF16)

        y_ref[...] = _dot(u_scr[...], w_ref[...])

    r_ops, r_in, r_out, r_shape, r_scr = _rider_args(rider)
    res = pl.pallas_call(
        _ride(rider, body, 3, 1, functools.partial(_grid_corner, 0, 0), functools.partial(_grid_corner, gi - 1, gj - 1)),
        name=name, grid=(gi, gj),
        in_specs=[pl.BlockSpec((tm, K), lambda i, j: (i, 0)),
                  pl.BlockSpec((1, K), lambda i, j: (0, 0)),
                  pl.BlockSpec((None, K, tn), lambda i, j: (j // tpn, 0, j % tpn))] + r_in,
        out_specs=[pl.BlockSpec((tm, tn), lambda i, j: (i, j))] + r_out,
        out_shape=[jax.ShapeDtypeStruct((S, J * n), F32)] + r_shape,
        scratch_shapes=[pltpu.VMEM((tm, K), BF16)] + r_scr,
        compiler_params=_params(("arbitrary", "arbitrary")))(h, gain, w3, *r_ops)
    return res[0] if rider is None else (res[0], res[1:])


def _mm_res(h, a, w2, name, tm=512):
    S, N = h.shape
    K = a.shape[1]

    def body(h_ref, a_ref, w_ref, o_ref):
        o_ref[...] = h_ref[...] + _dot(a_ref[...], w_ref[...])

    return pl.pallas_call(
        body, name=name, grid=(S // tm,),
        in_specs=[pl.BlockSpec((tm, N), lambda i: (i, 0)),
                  pl.BlockSpec((tm, K), lambda i: (i, 0)),
                  pl.BlockSpec((K, N), lambda i: (0, 0))],
        out_specs=pl.BlockSpec((tm, N), lambda i: (i, 0)),
        out_shape=jax.ShapeDtypeStruct((S, N), F32),
        compiler_params=_params(("parallel",)))(h, a, w2)


def _swiglu(z_ref, F):
    g = z_ref[:, :F]
    return (g * _sigmoid(g) * z_ref[:, F:]).astype(BF16)


def _swiglu_mm_res(h, z, w2, name, tm=256):
    S, N = h.shape
    F = w2.shape[0]

    def body(h_ref, z_ref, w_ref, o_ref):
        o_ref[...] = h_ref[...] + _dot(_swiglu(z_ref, F), w_ref[...])

    return pl.pallas_call(
        body, name=name, grid=(S // tm,),
        in_specs=[pl.BlockSpec((tm, N), lambda i: (i, 0)),
                  pl.BlockSpec((tm, 2 * F), lambda i: (i, 0)),
                  pl.BlockSpec((F, N), lambda i: (0, 0))],
        out_specs=pl.BlockSpec((tm, N), lambda i: (i, 0)),
        out_shape=jax.ShapeDtypeStruct((S, N), F32),
        compiler_params=_params(("parallel",)))(h, z, w2)


def _dy_specs(dy, J, n, tm):
    if dy.ndim == 3:
        return [pl.BlockSpec((None, tm, n), functools.partial(lambda i, j: (j, i, 0), j=j)) for j in range(J)]
    return [pl.BlockSpec((tm, n), functools.partial(lambda i, j: (i, j), j=j)) for j in range(J)]


def _acc_nt(dy_refs, w_ref):
    acc = None
    for j, r in enumerate(dy_refs):
        t = _dot_nt(r[...].astype(BF16), w_ref[j])
        acc = t if acc is None else acc + t
    return acc


def _mm_nt(dy, w3, name, out_dtype=F32, tm=512):
    J, K, n = w3.shape
    S = dy.shape[-2]

    def body(*refs):
        dy_refs, w_ref, o_ref = refs[:J], refs[J], refs[J + 1]
        o_ref[...] = _acc_nt(dy_refs, w_ref).astype(o_ref.dtype)

    return pl.pallas_call(
        body, name=name, grid=(S // tm,),
        in_specs=_dy_specs(dy, J, n, tm) + [pl.BlockSpec((J, K, n), lambda i: (0, 0, 0))],
        out_specs=pl.BlockSpec((tm, K), lambda i: (i, 0)),
        out_shape=jax.ShapeDtypeStruct((S, K), out_dtype),
        compiler_params=_params(("parallel",)))(*([dy] * J), w3)


def _mm_nt_normbwd(dy, w3, h, gain, dh, name, tm=512):
    J, K, n = w3.shape
    S = h.shape[0]

    def body(*refs):
        dy_refs, w_ref, h_ref, g_ref, dh_ref, o_ref, dg_ref = refs[:J], *refs[J:]
        du = _acc_nt(dy_refs, w_ref)
        v = h_ref[...]
        r = _rstd(v)
        xh = v * r
        dyg = du * g_ref[...]
        o_ref[...] = dh_ref[...] + r * (dyg - xh * jnp.mean(dyg * xh, axis=-1, keepdims=True))

        @pl.when(pl.program_id(0) == 0)
        def _():
            dg_ref[...] = jnp.zeros_like(dg_ref)

        dg_ref[...] += jnp.sum(du * xh, axis=0, keepdims=True)

    row = pl.BlockSpec((tm, K), lambda i: (i, 0))
    vec = pl.BlockSpec((1, K), lambda i: (0, 0))
    return pl.pallas_call(
        body, name=name, grid=(S // tm,),
        in_specs=_dy_specs(dy, J, n, tm) + [pl.BlockSpec((J, K, n), lambda i: (0, 0, 0)), row, vec, row],
        out_specs=[row, vec],
        out_shape=[jax.ShapeDtypeStruct((S, K), F32), jax.ShapeDtypeStruct((1, K), F32)],
        compiler_params=_params(("arbitrary",)))(*([dy] * J), w3, h, gain, dh)


def _mm_nt_swiglu_bwd(dh, w2, z, name, tm=256):
    F, N = w2.shape
    S = dh.shape[0]

    def body(dh_ref, w_ref, z_ref, o_ref):
        da = _dot_nt(dh_ref[...].astype(BF16), w_ref[...])
        g = z_ref[:, :F]
        u = z_ref[:, F:]
        sg = _sigmoid(g)
        o_ref[:, :F] = (da * u * (sg * (1.0 + g * (1.0 - sg)))).astype(BF16)
        o_ref[:, F:] = (da * (g * sg)).astype(BF16)

    return pl.pallas_call(
        body, name=name, grid=(S // tm,),
        in_specs=[pl.BlockSpec((tm, N), lambda i: (i, 0)),
                  pl.BlockSpec((F, N), lambda i: (0, 0)),
                  pl.BlockSpec((tm, 2 * F), lambda i: (i, 0))],
        out_specs=pl.BlockSpec((tm, 2 * F), lambda i: (i, 0)),
        out_shape=jax.ShapeDtypeStruct((S, 2 * F), BF16),
        compiler_params=_params(("parallel",)))(dh, w2, z)


def _mm_tn(kind, xs, dy, J, n, tn, name):
    tpn = n // tn
    ts = 256 if kind == "swiglu" else 512
    S = xs[0].shape[0]
    if kind == "norm":
        K = xs[0].shape[1]
        x_specs = [pl.BlockSpec((ts, K), lambda c, s: (s, 0)), pl.BlockSpec((1, K), lambda c, s: (0, 0))]
    elif kind == "swiglu":
        K = xs[0].shape[1] // 2
        x_specs = [pl.BlockSpec((ts, 2 * K), lambda c, s: (s, 0))]
    else:
        K = xs[0].shape[1]
        x_specs = [pl.BlockSpec((ts, K), lambda c, s: (s, 0))]
    nx = len(xs)
    if dy.ndim == 3:
        dy_spec = pl.BlockSpec((None, ts, tn), lambda c, s: (c // tpn, s, c % tpn))
    else:
        dy_spec = pl.BlockSpec((ts, tn), lambda c, s: (s, c))

    def body(*refs):
        x_refs, dy_ref, o_ref = refs[:nx], refs[nx], refs[nx + 1]
        if kind == "norm":
            v = x_refs[0][...]
            xb = (v * _rstd(v) * x_refs[1][...]).astype(BF16)
        elif kind == "swiglu":
            xb = _swiglu(x_refs[0], K)
        else:
            xb = x_refs[0][...].astype(BF16)

        @pl.when(pl.program_id(1) == 0)
        def _():
            o_ref[...] = jnp.zeros_like(o_ref)

        o_ref[...] += _dot_tn(xb, dy_ref[...].astype(BF16))

    return pl.pallas_call(
        body, name=name, grid=(J * tpn, S // ts),
        in_specs=x_specs + [dy_spec],
        out_specs=pl.BlockSpec((None, K, tn), lambda c, s: (c // tpn, 0, c % tpn)),
        out_shape=jax.ShapeDtypeStruct((J, K, n), F32),
        compiler_params=_params(("parallel", "arbitrary")))(*xs, dy)


def _loss_head(h, gain, target, tm=512):
    S, K = h.shape

    def body(h_ref, g_ref, t_ref, dh_ref, loss_ref, dg_ref):
        v = h_ref[...]
        r = _rstd(v)
        xh = v * r
        g = g_ref[...]
        dy = (xh * g - t_ref[...]) * (1.0 / K)
        dyg = dy * g
        dh_ref[...] = r * (dyg - xh * jnp.mean(dyg * xh, axis=-1, keepdims=True))

        @pl.when(pl.program_id(0) == 0)
        def _():
            loss_ref[...] = jnp.zeros_like(loss_ref)
            dg_ref[...] = jnp.zeros_like(dg_ref)

        part = jnp.sum(jnp.sum(dy * dy, axis=-1, keepdims=True), axis=0, keepdims=True) * (0.5 * K)
        lane = lax.broadcasted_iota(jnp.int32, loss_ref.shape, 1)
        loss_ref[...] += jnp.where(lane == 0, part, 0.0)
        dg_ref[...] += jnp.sum(dy * xh, axis=0, keepdims=True)

    row = pl.BlockSpec((tm, K), lambda i: (i, 0))
    vec = pl.BlockSpec((1, K), lambda i: (0, 0))
    return pl.pallas_call(
        body, name="loss_head", grid=(S // tm,),
        in_specs=[row, vec, row],
        out_specs=[row, pl.BlockSpec((1, HEAD), lambda i: (0, 0)), vec],
        out_shape=[jax.ShapeDtypeStruct((S, K), F32), jax.ShapeDtypeStruct((1, HEAD), F32),
                   jax.ShapeDtypeStruct((1, K), F32)],
        compiler_params=_params(("arbitrary",)))(h, gain, target)


def _lower_bound(lg_ref):
    l0, l1, l2 = lg_ref[0:1, :], lg_ref[1:2, :], lg_ref[2:3, :]
    mx = jnp.maximum(jnp.maximum(l0, l1), l2)
    e0, e1, e2 = jnp.exp(l0 - mx), jnp.exp(l1 - mx), jnp.exp(l2 - mx)
    return e0 / (e0 + e1 + e2)


def _chunks(v, ncb):
    C = HGRN_CHUNK
    return [v[c * C:(c + 1) * C] for c in range(ncb)]


def _rows(parts):
    return jnp.concatenate(parts, axis=0)


def _block_gates(qz, fz, lb, ncb):
    C = HGRN_CHUNK
    row = lax.broadcasted_iota(jnp.int32, (C, C), 0)
    col = lax.broadcasted_iota(jnp.int32, (C, C), 1)
    tri = (col <= row).astype(F32)
    first_half = lax.broadcasted_iota(jnp.int32, (C, HEAD), 0) < C // 2
    sig = _sigmoid(fz)
    fg = lb + (1.0 - lb) * sig
    key = 1.0 - fg
    lg = jnp.log(fg)
    lgs = _chunks(lg, ncb)
    b = _rows([_dot_exact(tri, v) for v in lgs])
    r_c = [jnp.sum(jnp.where(first_half, v, 0.0), axis=0, keepdims=True) for v in lgs]
    bl_c = [jnp.sum(v, axis=0, keepdims=True) for v in lgs]
    r = _rows([jnp.broadcast_to(v, (C, HEAD)) for v in r_c])
    bl = _rows([jnp.broadcast_to(v, (C, HEAD)) for v in bl_c])
    sq = _sigmoid(qz)
    qy = qz * sq
    return sig, fg, key, b, r, bl, bl_c, sq, qy


def _hgrn_fwd(proj, logits, gain, tb=512, rider=None):
    S = proj.shape[0]
    H, C = HGRN_HEADS, HGRN_CHUNK
    ncb = tb // C

    def body(q_ref, f_ref, i_ref, g_ref, lg_ref, gn_ref, o_ref, og_ref, st_ref, state):
        @pl.when(pl.program_id(1) == 0)
        def _():
            state[...] = jnp.zeros_like(state)

        lb = _lower_bound(lg_ref)
        causal = lax.broadcasted_iota(jnp.int32, (C, C), 1) <= lax.broadcasted_iota(jnp.int32, (C, C), 0)
        qz, fz, gz = q_ref[...], f_ref[...], g_ref[...]
        _, _, key, b, r, bl, bl_c, _, qy = _block_gates(qz, fz, lb, ncb)
        qs = _chunks((qy * jnp.exp(b - r)).astype(BF16), ncb)
        ks = _chunks((key * jnp.exp(r - b)).astype(BF16), ncb)
        qb = _chunks((qy * jnp.exp(b)).astype(BF16), ncb)
        ke = _chunks((key * jnp.exp(bl - b)).astype(BF16), ncb)
        vb = _chunks(i_ref[...].astype(BF16), ncb)
        o_intra, upd = [], []
        for c in range(ncb):
            a = jnp.where(causal, _dot_nt(qs[c], ks[c]), 0.0).astype(BF16)
            o_intra.append(_dot(a, vb[c]))
            upd.append(_dot_tn(vb[c], ke[c]))
        st = state[...]
        for c in range(ncb):
            st_ref[c] = st
            st = st * jnp.exp(bl_c[c]) + upd[c]
        state[...] = st
        o = _rows([_dot_nt(qb[c], st_ref[c].astype(BF16)) + o_intra[c] for c in range(ncb)])
        o_ref[...] = o
        og_ref[...] = ((o * _rstd(o) * gn_ref[...]) * (gz * _sigmoid(gz))).astype(BF16)

    def part(p):
        return pl.BlockSpec((tb, HEAD), functools.partial(lambda h, i, p: (i, p * H + h), p=p))

    nb = S // tb
    r_ops, r_in, r_out, r_shape, r_scr = _rider_args(rider)
    res = pl.pallas_call(
        _ride(rider, body, 6, 3, functools.partial(_grid_corner, 0, 0), functools.partial(_grid_corner, H - 1, nb - 1)),
        name="hgrn_fwd", grid=(H, nb),
        in_specs=[part(0), part(1), part(2), part(3),
                  pl.BlockSpec((3, HEAD), lambda h, i: (0, h)),
                  pl.BlockSpec((1, HEAD), lambda h, i: (0, 0))] + r_in,
        out_specs=[pl.BlockSpec((tb, HEAD), lambda h, i: (i, h)),
                   pl.BlockSpec((tb, HEAD), lambda h, i: (i, h)),
                   pl.BlockSpec((None, ncb, HEAD, HEAD), lambda h, i: (h, i, 0, 0))] + r_out,
        out_shape=[jax.ShapeDtypeStruct((S, H * HEAD), F32),
                   jax.ShapeDtypeStruct((S, H * HEAD), BF16),
                   jax.ShapeDtypeStruct((H, S // C, HEAD, HEAD), F32)] + r_shape,
        scratch_shapes=[pltpu.VMEM((HEAD, HEAD), F32)] + r_scr,
        compiler_params=_params(("arbitrary", "arbitrary")))(proj, proj, proj, proj, logits, gain, *r_ops)
    return res[:3], res[3:]


def _hgrn_bwd(proj, logits, gain, o, states, dog, tb=512, rider=None):
    S = proj.shape[0]
    H, C = HGRN_HEADS, HGRN_CHUNK
    ncb = tb // C
    nb = S // tb

    def body(q_ref, f_ref, i_ref, g_ref, lg_ref, gn_ref, o_ref, st_ref, dog_ref,
             dp_ref, dlb_ref, dgn_ref, dstate, dst_scr):
        @pl.when(pl.program_id(1) == 0)
        def _():
            dstate[...] = jnp.zeros_like(dstate)
            dlb_ref[...] = jnp.zeros_like(dlb_ref)
            dgn_ref[...] = jnp.zeros_like(dgn_ref)

        lb = _lower_bound(lg_ref)
        oml = 1.0 - lb
        gn = gn_ref[...]
        row = lax.broadcasted_iota(jnp.int32, (C, C), 0)
        col = lax.broadcasted_iota(jnp.int32, (C, C), 1)
        causal = col <= row
        tri_up = (col >= row).astype(F32)
        qz, fz, gz = q_ref[...], f_ref[...], g_ref[...]
        sig, fg, key, b, r, bl, bl_c, sq, qy = _block_gates(qz, fz, lb, ncb)
        e_br, e_rb, e_b, e_lb = jnp.exp(b - r), jnp.exp(r - b), jnp.exp(b), jnp.exp(bl - b)
        qs_v, ks_v = (qy * e_br).astype(BF16), (key * e_rb).astype(BF16)
        qb_v, ke_v = (qy * e_b).astype(BF16), (key * e_lb).astype(BF16)
        qs, ks, qb, ke = _chunks(qs_v, ncb), _chunks(ks_v, ncb), _chunks(qb_v, ncb), _chunks(ke_v, ncb)
        vb = _chunks(i_ref[...].astype(BF16), ncb)
        ov = o_ref[...]
        rs = _rstd(ov)
        xh = ov * rs
        sg = _sigmoid(gz)
        dog_v = dog_ref[...]
        dgz = dog_v * (xh * gn) * (sg * (1.0 + gz * (1.0 - sg)))
        don = dog_v * (gz * sg)
        dgn_ref[...] += jnp.sum(don * xh, axis=0, keepdims=True)
        dyg = don * gn
        do = rs * (dyg - xh * jnp.mean(dyg * xh, axis=-1, keepdims=True))
        dob = _chunks(do.astype(BF16), ncb)
        dv_in, dqs, dks, wst = [], [], [], []
        for c in range(ncb):
            a = jnp.where(causal, _dot_nt(qs[c], ks[c]), 0.0).astype(BF16)
            da = jnp.where(causal, _dot_nt(dob[c], vb[c]), 0.0).astype(BF16)
            dv_in.append(_dot_tn(a, dob[c]))
            dqs.append(_dot(da, ks[c]))
            dks.append(_dot_tn(da, qs[c]))
            wst.append(_dot_tn(dob[c], qb[c]))
        e_l = [jnp.exp(v) for v in bl_c]
        dst = dstate[...]
        for c in reversed(range(ncb)):
            dst_scr[c] = dst
            dst = wst[c] + dst * e_l[c]
        dstate[...] = dst
        dv, dqb, dke, dbl_st = [], [], [], []
        for c in range(ncb):
            dst1 = dst_scr[c]
            st0 = st_ref[c]
            dst1b = dst1.astype(BF16)
            dv.append(dv_in[c] + _dot_nt(ke[c], dst1b))
            dqb.append(_dot(dob[c], st0.astype(BF16)))
            dke.append(_dot(vb[c], dst1b))
            dbl_st.append(jnp.sum(dst1 * st0, axis=0, keepdims=True) * e_l[c])
        dqs, dks, dqb, dke, dv = _rows(dqs), _rows(dks), _rows(dqb), _rows(dke), _rows(dv)
        dke_ke = dke * ke_v.astype(F32)
        db = dqs * qs_v.astype(F32) - dks * ks_v.astype(F32) + dqb * qb_v.astype(F32) - dke_ke
        dlg = []
        for c, (db_c, kk_c) in enumerate(zip(_chunks(db, ncb), _chunks(dke_ke, ncb))):
            dbl = jnp.sum(kk_c, axis=0, keepdims=True) + dbl_st[c]
            dlg.append(_dot_exact(tri_up, db_c) + dbl)
        dlg = _rows(dlg)
        dkey = dks * e_rb + dke * e_lb
        dqy = dqs * e_br + dqb * e_b
        dfg = dlg / fg - dkey
        dlb_ref[...] += jnp.sum(dfg * (1.0 - sig), axis=0, keepdims=True)
        dp_ref[0] = (dqy * (sq * (1.0 + qz * (1.0 - sq)))).astype(BF16)
        dp_ref[1] = (dfg * oml * sig * (1.0 - sig)).astype(BF16)
        dp_ref[2] = dv.astype(BF16)
        dp_ref[3] = dgz.astype(BF16)

    def part(p):
        return pl.BlockSpec((tb, HEAD), functools.partial(lambda h, i, p: (nb - 1 - i, p * H + h), p=p))

    blk = pl.BlockSpec((tb, HEAD), lambda h, i: (nb - 1 - i, h))
    acc = pl.BlockSpec((None, 1, HEAD), lambda h, i: (h, 0, 0))
    r_ops, r_in, r_out, r_shape, r_scr = _rider_args(rider)
    res = pl.pallas_call(
        _ride(rider, body, 9, 3, functools.partial(_grid_corner, 0, 0), functools.partial(_grid_corner, H - 1, nb - 1)),
        name="hgrn_bwd", grid=(H, nb),
        in_specs=[part(0), part(1), part(2), part(3),
                  pl.BlockSpec((3, HEAD), lambda h, i: (0, h)),
                  pl.BlockSpec((1, HEAD), lambda h, i: (0, 0)),
                  blk,
                  pl.BlockSpec((None, ncb, HEAD, HEAD), lambda h, i: (h, nb - 1 - i, 0, 0)),
                  blk] + r_in,
        out_specs=[pl.BlockSpec((4, tb, HEAD), lambda h, i: (0, nb - 1 - i, h)), acc, acc] + r_out,
        out_shape=[jax.ShapeDtypeStruct((4, S, H * HEAD), BF16),
                   jax.ShapeDtypeStruct((H, 1, HEAD), F32),
                   jax.ShapeDtypeStruct((H, 1, HEAD), F32)] + r_shape,
        scratch_shapes=[pltpu.VMEM((HEAD, HEAD), F32), pltpu.VMEM((ncb, HEAD, HEAD), F32)] + r_scr,
        compiler_params=_params(("arbitrary", "arbitrary")))(
            proj, proj, proj, proj, logits, gain, o, states, dog, *r_ops)
    return res[:3], res[3:]


def _rope(v, cos, sin):
    return v * cos + pltpu.roll(v, HEAD // 2, 1) * sin


def _band_masks():
    qi = lax.broadcasted_iota(jnp.int32, (ATTN_SPAN, ATTN_SPAN), 0)
    kj = lax.broadcasted_iota(jnp.int32, (ATTN_SPAN, ATTN_SPAN), 1)
    return kj <= qi, kj >= qi


def _attn_fwd(a, cos, sin):
    d, L, _ = a.shape
    nb = L // ATTN_SPAN
    scale = HEAD ** -0.5

    def body(q_ref, kc_ref, kp_ref, vc_ref, vp_ref, cc_ref, cp_ref, sc_ref, sp_ref, o_ref, lse_ref):
        n = pl.program_id(1)
        mask_c, mask_p0 = _band_masks()
        mask_p = jnp.logical_and(mask_p0, n > 0)
        cc, cp, sc, sp = cc_ref[...], cp_ref[...], sc_ref[...], sp_ref[...]
        for hh in range(HEADS_PER_GROUP):
            cols = slice(hh * HEAD, (hh + 1) * HEAD)
            q = _rope(q_ref[:, cols], cc, sc).astype(BF16)
            kc = _rope(kc_ref[:, cols], cc, sc).astype(BF16)
            kp = _rope(kp_ref[:, cols], cp, sp).astype(BF16)
            s_c = jnp.where(mask_c, _dot_nt(q, kc) * scale, NEG)
            s_p = jnp.where(mask_p, _dot_nt(q, kp) * scale, NEG)
            m = jnp.maximum(jnp.max(s_c, axis=-1, keepdims=True), jnp.max(s_p, axis=-1, keepdims=True))
            p_c = jnp.exp(s_c - m)
            p_p = jnp.exp(s_p - m)
            l = jnp.sum(p_c, axis=-1, keepdims=True) + jnp.sum(p_p, axis=-1, keepdims=True)
            acc = _dot(p_c.astype(BF16), vc_ref[:, cols].astype(BF16)) + _dot(p_p.astype(BF16), vp_ref[:, cols].astype(BF16))
            o_ref[:, cols] = acc / l
            lse_ref[:, cols] = jnp.broadcast_to(m + jnp.log(l), (ATTN_SPAN, HEAD))

    def blk(part, prev):
        if prev:
            return pl.BlockSpec((None, ATTN_SPAN, GROUP_W), functools.partial(lambda r, n, p: (r, jnp.maximum(n - 1, 0), p), p=part))
        return pl.BlockSpec((None, ATTN_SPAN, GROUP_W), functools.partial(lambda r, n, p: (r, n, p), p=part))

    tab_c = pl.BlockSpec((None, ATTN_SPAN, HEAD), lambda r, n: (r, n, 0))
    tab_p = pl.BlockSpec((None, ATTN_SPAN, HEAD), lambda r, n: (r, jnp.maximum(n - 1, 0), 0))
    out = pl.BlockSpec((None, ATTN_SPAN, GROUP_W), lambda r, n: (r, n, 0))
    return pl.pallas_call(
        body, name=f"attn_fwd_d{d}", grid=(d, nb),
        in_specs=[blk(0, False), blk(1, False), blk(1, True), blk(2, False), blk(2, True), tab_c, tab_p, tab_c, tab_p],
        out_specs=[out, out],
        out_shape=[jax.ShapeDtypeStruct((d, L, GROUP_W), F32), jax.ShapeDtypeStruct((d, L, GROUP_W), F32)],
        compiler_params=_params(("parallel", "arbitrary")))(a, a, a, a, a, cos, cos, sin, sin)


def _attn_bwd(a, cos, sin, do, lse, dd):
    d, L, _ = a.shape
    nb = L // ATTN_SPAN
    scale = HEAD ** -0.5

    def body(qc_ref, qn_ref, kp_ref, kc_ref, vp_ref, vc_ref, doc_ref, don_ref, lc_ref, ln_ref, ddc_ref, ddn_ref,
             cp_ref, cc_ref, cn_ref, sp_ref, sc_ref, sn_ref, da_ref):
        n = pl.program_id(1)
        mask_c, mask_p0 = _band_masks()
        mask_p = jnp.logical_and(mask_p0, n > 0)
        mask_n = jnp.logical_and(mask_p0, n < nb - 1)
        cp, cc, cn, sp, sc, sn = cp_ref[...], cc_ref[...], cn_ref[...], sp_ref[...], sc_ref[...], sn_ref[...]
        for hh in range(HEADS_PER_GROUP):
            cols = slice(hh * HEAD, (hh + 1) * HEAD)
            q = _rope(qc_ref[:, cols], cc, sc).astype(BF16)
            qn = _rope(qn_ref[:, cols], cn, sn).astype(BF16)
            kc = _rope(kc_ref[:, cols], cc, sc).astype(BF16)
            kp = _rope(kp_ref[:, cols], cp, sp).astype(BF16)
            vc = vc_ref[:, cols].astype(BF16)
            vp = vp_ref[:, cols].astype(BF16)
            do_c = doc_ref[:, cols].astype(BF16)
            do_n = don_ref[:, cols].astype(BF16)
            lse_c, lse_n = lc_ref[:, cols], ln_ref[:, cols]
            dd_c, dd_n = ddc_ref[:, cols], ddn_ref[:, cols]
            p_c = jnp.where(mask_c, jnp.exp(_dot_nt(q, kc) * scale - lse_c), 0.0)
            p_p = jnp.where(mask_p, jnp.exp(_dot_nt(q, kp) * scale - lse_c), 0.0)
            ds_c = (p_c * (_dot_nt(do_c, vc) + dd_c)).astype(BF16)
            ds_p = (p_p * (_dot_nt(do_c, vp) + dd_c)).astype(BF16)
            dq = (_dot(ds_c, kc) + _dot(ds_p, kp)) * scale
            p_n = jnp.where(mask_n, jnp.exp(_dot_nt(qn, kc) * scale - lse_n), 0.0)
            ds_n = (p_n * (_dot_nt(do_n, vc) + dd_n)).astype(BF16)
            dk = (_dot_tn(ds_c, q) + _dot_tn(ds_n, qn)) * scale
            dv = _dot_tn(p_c.astype(BF16), do_c) + _dot_tn(p_n.astype(BF16), do_n)
            da_ref[:, cols] = _rope(dq, cc, -sc)
            da_ref[:, GROUP_W + hh * HEAD:GROUP_W + (hh + 1) * HEAD] = _rope(dk, cc, -sc)
            da_ref[:, 2 * GROUP_W + hh * HEAD:2 * GROUP_W + (hh + 1) * HEAD] = dv

    def rel(delta):
        if delta < 0:
            return lambda n: jnp.maximum(n - 1, 0)
        if delta > 0:
            return lambda n: jnp.minimum(n + 1, nb - 1)
        return lambda n: n

    def blk(width, part, delta):
        f = rel(delta)
        return pl.BlockSpec((None, ATTN_SPAN, width), functools.partial(lambda r, n, p, f: (r, f(n), p), p=part, f=f))

    g = GROUP_W
    return pl.pallas_call(
        body, name=f"attn_bwd_d{d}", grid=(d, nb),
        in_specs=[blk(g, 0, 0), blk(g, 0, 1), blk(g, 1, -1), blk(g, 1, 0), blk(g, 2, -1), blk(g, 2, 0),
                  blk(g, 0, 0), blk(g, 0, 1), blk(g, 0, 0), blk(g, 0, 1), blk(g, 0, 0), blk(g, 0, 1),
                  blk(HEAD, 0, -1), blk(HEAD, 0, 0), blk(HEAD, 0, 1), blk(HEAD, 0, -1), blk(HEAD, 0, 0), blk(HEAD, 0, 1)],
        out_specs=pl.BlockSpec((None, ATTN_SPAN, 3 * g), lambda r, n: (r, n, 0)),
        out_shape=jax.ShapeDtypeStruct((d, L, 3 * g), F32),
        compiler_params=_params(("parallel", "arbitrary")))(
            a, a, a, a, a, a, do, do, lse, lse, dd, dd, cos, cos, cos, sin, sin, sin)


def _group_weights(lse_refs, cols):
    ls = [r[:, cols] for r in lse_refs]
    mx = jnp.maximum(jnp.maximum(ls[0], ls[1]), ls[2])
    es = [jnp.exp(v - mx) for v in ls]
    tot = es[0] + es[1] + es[2]
    return [e / tot for e in es]


def _gather_tokens(ref, scr, d, tm):
    if d == 1:
        return ref.at[0]
    for r in range(d):
        scr[pl.ds(r, tm // d, stride=d), :] = ref[r]
    return scr


def _scatter_tokens(scr, ref, d, tm):
    if d == 1:
        ref[0] = scr[...]
        return
    for r in range(d):
        ref[r] = scr[pl.ds(r, tm // d, stride=d), :]


def _head_spec(d, tm):
    return pl.BlockSpec((d, tm // d, HEAD), lambda i, j: (0, i, j))


def _dilate_group(qkv, g, d, tm=1024):
    S = qkv.shape[0]
    G = len(ATTN_GROUPS)

    def body(x_ref, out_ref):
        _scatter_tokens(x_ref, out_ref, d, tm)

    def src(i, j):
        return i, ((j // HEADS_PER_GROUP) * G + g) * HEADS_PER_GROUP + j % HEADS_PER_GROUP

    return pl.pallas_call(
        body, name=f"attn_dilate_d{d}", grid=(S // tm, 3 * HEADS_PER_GROUP),
        in_specs=[pl.BlockSpec((tm, HEAD), src)],
        out_specs=_head_spec(d, tm),
        out_shape=jax.ShapeDtypeStruct((d, S // d, 3 * GROUP_W), F32),
        compiler_params=_params(("parallel", "parallel")))(qkv)


def _undilate_group(da, dqkv, g, tm=1024):
    d, L, _ = da.shape
    S = d * L
    G = len(ATTN_GROUPS)

    def body(*refs):
        da_ref, out_ref, scr = refs[0], refs[-2], refs[-1]
        out_ref[...] = _gather_tokens(da_ref, scr, d, tm)[...].astype(BF16)

    def dst(i, j):
        return i, ((j // HEADS_PER_GROUP) * G + g) * HEADS_PER_GROUP + j % HEADS_PER_GROUP

    operands = (da,) if dqkv is None else (da, dqkv)
    return pl.pallas_call(
        body, name=f"attn_undilate_d{d}", grid=(S // tm, 3 * HEADS_PER_GROUP),
        in_specs=[_head_spec(d, tm)] + ([] if dqkv is None else [ANY]),
        out_specs=pl.BlockSpec((tm, HEAD), dst),
        out_shape=jax.ShapeDtypeStruct((S, 3 * G * GROUP_W), BF16),
        input_output_aliases={} if dqkv is None else {1: 0},
        scratch_shapes=[pltpu.VMEM((tm, HEAD), F32)],
        compiler_params=_params(("parallel", "parallel")))(*operands)


def _attn_merge(os_, lses, tm=512):
    G = len(os_)
    S = os_[0].shape[0] * os_[0].shape[1]

    def body(*refs):
        o_refs, l_refs, out_ref = refs[:G], refs[G:2 * G], refs[2 * G]
        scr = refs[2 * G + 1:]
        hh = pl.program_id(1)
        o_tok = [_gather_tokens(o_refs[g], scr[g], d, tm) for g, (_, d) in enumerate(ATTN_GROUPS)]
        l_tok = [_gather_tokens(l_refs[g], scr[G + g], d, tm) for g, (_, d) in enumerate(ATTN_GROUPS)]
        al = _group_weights(l_tok, slice(None))
        for g in range(G):
            cols = pl.ds(pl.multiple_of(g * GROUP_W + hh * HEAD, HEAD), HEAD)
            out_ref[:, cols] = (o_tok[g][...] * al[g]).astype(BF16)

    specs = [_head_spec(d, tm) for _, d in ATTN_GROUPS]
    return pl.pallas_call(
        body, name="attn_merge", grid=(S // tm, HEADS_PER_GROUP),
        in_specs=specs + specs,
        out_specs=pl.BlockSpec((tm, G * GROUP_W), lambda i, j: (i, 0)),
        out_shape=jax.ShapeDtypeStruct((S, G * GROUP_W), BF16),
        scratch_shapes=[pltpu.VMEM((tm, HEAD), F32)] * (2 * G),
        compiler_params=_params(("parallel", "arbitrary")))(*os_, *lses)


def _attn_merge_bwd(os_, lses, doa, tm=512):
    G = len(os_)
    S = doa.shape[0]

    def body(*refs):
        o_refs, l_refs, doa_ref = refs[:G], refs[G:2 * G], refs[2 * G]
        do_refs, dd_refs = refs[2 * G + 1:3 * G + 1], refs[3 * G + 1:4 * G + 1]
        scr = refs[4 * G + 1:]
        hh = pl.program_id(1)
        o_tok = [_gather_tokens(o_refs[g], scr[g], d, tm) for g, (_, d) in enumerate(ATTN_GROUPS)]
        l_tok = [_gather_tokens(l_refs[g], scr[G + g], d, tm) for g, (_, d) in enumerate(ATTN_GROUPS)]
        do_tok, dd_tok = scr[2 * G:3 * G], scr[3 * G:]
        al = _group_weights(l_tok, slice(None))
        mix = None
        for g in range(G):
            dg = doa_ref[:, pl.ds(pl.multiple_of(g * GROUP_W + hh * HEAD, HEAD), HEAD)]
            do_tok[g][...] = dg * al[g]
            t = al[g] * jnp.sum(dg * o_tok[g][...], axis=-1, keepdims=True)
            mix = t if mix is None else mix + t
        for g, (_, d) in enumerate(ATTN_GROUPS):
            dd_tok[g][...] = jnp.broadcast_to(-al[g] * mix, (tm, HEAD))
            _scatter_tokens(do_tok[g], do_refs[g], d, tm)
            _scatter_tokens(dd_tok[g], dd_refs[g], d, tm)

    specs = [_head_spec(d, tm) for _, d in ATTN_GROUPS]
    shapes = [jax.ShapeDtypeStruct((d, S // d, GROUP_W), F32) for _, d in ATTN_GROUPS]
    return pl.pallas_call(
        body, name="attn_merge_bwd", grid=(S // tm, HEADS_PER_GROUP),
        in_specs=specs + specs + [pl.BlockSpec((tm, G * GROUP_W), lambda i, j: (i, 0))],
        out_specs=specs + specs,
        out_shape=shapes + shapes,
        scratch_shapes=[pltpu.VMEM((tm, HEAD), F32)] * (4 * G),
        compiler_params=_params(("parallel", "arbitrary")))(*os_, *lses, doa)


def _to_dilated(v, d):
    S, W = v.shape
    return v.reshape(S // d, d, W).transpose(1, 0, 2)


def _rope_tables(S):
    inv_freq = 1.0 / (ROPE_THETA ** (jnp.arange(0, HEAD, 2, dtype=F32) / HEAD))
    ang = jnp.arange(S, dtype=F32)[:, None] * inv_freq[None, :]
    cos, sin = jnp.cos(ang), jnp.sin(ang)
    return jnp.concatenate([cos, cos], axis=-1), jnp.concatenate([-sin, sin], axis=-1)


def _local_step(x, target, norm_mix, norm_ffn, lb_logits, out_gain, final_norm, comm):
    S = x.shape[0]
    nm0, nm1 = norm_mix[0:1], norm_mix[1:2]
    nf0, nf1 = norm_ffn[0:1], norm_ffn[1:2]
    w = comm.first_weights()

    rider = comm.gather_rider(LATE_WEIGHTS_A)
    res = _norm_mm(x, nm0, w["hin"], D_MODEL, "hgrn_in", rider=rider)
    proj, got = (res, ()) if rider is None else res
    w.update(comm.gathered(LATE_WEIGHTS_A, got))
    (o, og, states), got = _hgrn_fwd(proj, lb_logits, out_gain, rider=comm.gather_rider(LATE_WEIGHTS_B))
    w.update(comm.gathered(LATE_WEIGHTS_B, got))
    fin_tn = w["fin0"].shape[2]
    h1 = _mm_res(x, og, w["hout"], "hgrn_out")
    z0 = _norm_mm(h1, nf0, w["fin0"], fin_tn, "ffn0_in")
    h2 = _swiglu_mm_res(h1, z0, w["fdn0"], "ffn0_down")
    qkv = _norm_mm(h2, nm1, w["qkv"], w["qkv"].shape[2], "attn_qkv")
    cos, sin = _rope_tables(S)
    a_g = [_dilate_group(qkv, gi, d) for gi, (_, d) in enumerate(ATTN_GROUPS)]
    cos_g, sin_g, o_g, lse_g = [], [], [], []
    for gi, (_, d) in enumerate(ATTN_GROUPS):
        cd, sd = _to_dilated(cos, d), _to_dilated(sin, d)
        og_, lse_ = _attn_fwd(a_g[gi], cd, sd)
        cos_g.append(cd), sin_g.append(sd), o_g.append(og_), lse_g.append(lse_)
    oa = _attn_merge(o_g, lse_g)
    h3 = _mm_res(h2, oa, w["aout"], "attn_out")
    z1 = _norm_mm(h3, nf1, w["fin1"], fin_tn, "ffn1_in")
    h4 = _swiglu_mm_res(h3, z1, w["fdn1"], "ffn1_down")
    dh4, loss, d_final = _loss_head(h4, final_norm, target)

    grads, small = {}, {"final_norm": d_final}

    def ffn_bwd(dh, h_in, z, gain, w_in, w_dn, tag):
        dz = _mm_nt_swiglu_bwd(dh, w_dn, z, tag + "_down_dx")
        g_dn = _mm_tn("swiglu", (z,), dh, 1, D_MODEL, 512, tag + "_down_dw")
        g_in = _mm_tn("norm", (h_in, gain), dz, N_CHIPS, fin_tn, fin_tn, tag + "_in_dw")
        dh_in, dgain = _mm_nt_normbwd(dz, w_in, h_in, gain, dh, tag + "_in_dx")
        return dh_in, dgain, g_in, g_dn[0]

    dh3, d_nf1, grads["fin1"], grads["fdn1"] = ffn_bwd(dh4, h3, z1, nf1, w["fin1"], w["fdn1"], "ffn1")
    doa = _mm_nt(dh3, w["aout"][None], "attn_out_dx")
    grads["aout"] = _mm_tn("plain", (oa,), dh3, 1, D_MODEL, 512, "attn_out_dw")[0]
    merged = _attn_merge_bwd(o_g, lse_g, doa)
    G = len(ATTN_GROUPS)
    das = [_attn_bwd(a_g[gi], cos_g[gi], sin_g[gi], merged[gi], lse_g[gi], merged[G + gi]) for gi in range(G)]
    dqkv = None
    for gi in range(G):
        dqkv = _undilate_group(das[gi], dqkv, gi)
    n_qkv = w["qkv"].shape[2]
    grads["qkv"] = _mm_tn("norm", (h2, nm1), dqkv, N_CHIPS, n_qkv, n_qkv, "attn_qkv_dw")
    dh2, d_nm1 = _mm_nt_normbwd(dqkv, w["qkv"], h2, nm1, dh3, "attn_qkv_dx")

    dh1, d_nf0, grads["fin0"], grads["fdn0"] = ffn_bwd(dh2, h1, z0, nf0, w["fin0"], w["fdn0"], "ffn0")
    dog = _mm_nt(dh1, w["hout"][None], "hgrn_out_dx")
    rider = comm.exchange_rider({k: grads.pop(k) for k in EARLY_GRADS})
    (dproj, dlb, dgn), got = _hgrn_bwd(proj, lb_logits, out_gain, o, states, dog, rider=rider)
    comm.exchanged(got)
    grads["hout"] = _mm_tn("plain", (og,), dh1, 1, D_MODEL, 512, "hgrn_out_dw")[0]
    grads["hin"] = _mm_tn("norm", (x, nm0), dproj, N_CHIPS, D_MODEL, D_MODEL, "hgrn_in_dw")
    dx, d_nm0 = _mm_nt_normbwd(dproj, w["hin"], x, nm0, dh1, "hgrn_in_dx")

    small["norm_mix"] = jnp.concatenate([d_nm0, d_nm1], axis=0)
    small["norm_ffn"] = jnp.concatenate([d_nf0, d_nf1], axis=0)
    small["lb"] = dlb.reshape(1, HGRN_HEADS * HEAD)
    small["out_norm"] = dgn.reshape(HGRN_HEADS, HEAD)
    return loss, dx, grads, small


def _place():
    x, y, c = lax.axis_index("x"), lax.axis_index("y"), lax.axis_index("c")
    others = [(1 - x, y), (x, 1 - y), (1 - x, 1 - y)]
    return x, y, c, others


ANY = pl.BlockSpec(memory_space=pl.ANY)


class _GatherRider:
    def __init__(self, shards):
        self.operands = list(shards)
        n = self.n = len(shards)
        self.out_shape = [jax.ShapeDtypeStruct((N_CHIPS,) + s.shape, s.dtype) for s in shards]
        self.scratch = [pltpu.SemaphoreType.DMA((3 * n,)), pltpu.SemaphoreType.DMA((3 * n,)),
                        pltpu.SemaphoreType.DMA((3 * n,)), pltpu.SemaphoreType.DMA((3 * n,)),
                        pltpu.SemaphoreType.DMA((n,)), pltpu.SemaphoreType.DMA((n,))]

    def _copies(self, ins, outs, sems):
        ici_send, ici_recv, _, _, own_send, own_recv = sems
        x, y, c, others = _place()
        me = 2 * x + y
        own = [pltpu.make_async_remote_copy(
            src_ref=ins[a], dst_ref=outs[a].at[me], send_sem=own_send.at[a], recv_sem=own_recv.at[a],
            device_id=(x, y, 1 - c), device_id_type=MESH) for a in range(self.n)]
        sends = [pltpu.make_async_remote_copy(
            src_ref=ins[a].at[c], dst_ref=outs[a].at[me, c], send_sem=ici_send.at[a * 3 + k], recv_sem=ici_recv.at[a * 3 + k],
            device_id=(ox, oy, c), device_id_type=MESH) for a in range(self.n) for k, (ox, oy) in enumerate(others)]
        return own, sends

    def start(self, ins, outs, sems):
        own, sends = self._copies(ins, outs, sems)
        for cp in own + sends:
            cp.start()

    def finish(self, ins, outs, sems):
        ici_send, ici_recv, d2d_send, d2d_recv, _, _ = sems
        x, y, c, others = _place()
        sibling = (x, y, 1 - c)
        own, sends = self._copies(ins, outs, sems)
        passes = []
        for a in range(self.n):
            for k, (ox, oy) in enumerate(others):
                s = a * 3 + k
                got = outs[a].at[2 * ox + oy, c]
                pltpu.make_async_remote_copy(
                    src_ref=got, dst_ref=got, send_sem=ici_send.at[s], recv_sem=ici_recv.at[s],
                    device_id=(ox, oy, c), device_id_type=MESH).wait_recv()
                fwd = pltpu.make_async_remote_copy(
                    src_ref=got, dst_ref=got, send_sem=d2d_send.at[s], recv_sem=d2d_recv.at[s],
                    device_id=sibling, device_id_type=MESH)
                fwd.start()
                passes.append(fwd)
        for a in range(self.n):
            for k, (ox, oy) in enumerate(others):
                s = a * 3 + k
                theirs = outs[a].at[2 * ox + oy, 1 - c]
                pltpu.make_async_remote_copy(
                    src_ref=theirs, dst_ref=theirs, send_sem=d2d_send.at[s], recv_sem=d2d_recv.at[s],
                    device_id=sibling, device_id_type=MESH).wait_recv()
        for cp in own:
            cp.wait()
        for cp in sends + passes:
            cp.wait_send()


class _ExchangeRider:
    def __init__(self, parts):
        self.operands = list(parts)
        n = self.n = len(parts)
        self.out_shape = [jax.ShapeDtypeStruct(p.shape, p.dtype) for p in parts]
        self.scratch = [pltpu.SemaphoreType.DMA((3 * n,)), pltpu.SemaphoreType.DMA((3 * n,))]

    def _copies(self, ins, outs, sems):
        send_sem, recv_sem = sems
        x, y, c, others = _place()
        me = 2 * x + y
        return [pltpu.make_async_remote_copy(
            src_ref=ins[a].at[2 * ox + oy], dst_ref=outs[a].at[me], send_sem=send_sem.at[a * 3 + k],
            recv_sem=recv_sem.at[a * 3 + k], device_id=(ox, oy, c), device_id_type=MESH)
            for a in range(self.n) for k, (ox, oy) in enumerate(others)]

    def start(self, ins, outs, sems):
        for cp in self._copies(ins, outs, sems):
            cp.start()

    def finish(self, ins, outs, sems):
        send_sem, recv_sem = sems
        x, y, c, others = _place()
        for a in range(self.n):
            for k, (ox, oy) in enumerate(others):
                s = a * 3 + k
                got = outs[a].at[2 * ox + oy]
                pltpu.make_async_remote_copy(
                    src_ref=got, dst_ref=got, send_sem=send_sem.at[s], recv_sem=recv_sem.at[s],
                    device_id=(ox, oy, c), device_id_type=MESH).wait_recv()
        for cp in self._copies(ins, outs, sems):
            cp.wait_send()


def _run_rider(rider, name):
    n = rider.n

    def body(*refs):
        ins, outs, sems = refs[:n], refs[n:2 * n], refs[2 * n:]
        rider.start(ins, outs, sems)
        rider.finish(ins, outs, sems)

    return pl.pallas_call(
        body, name=name, in_specs=[ANY] * n, out_specs=[ANY] * n,
        out_shape=rider.out_shape, scratch_shapes=rider.scratch)(*rider.operands)


def _ride(rider, body, n_in, n_out, first, last):
    if rider is None:
        return body
    n = rider.n

    def wrapped(*refs):
        host_in, r_in = refs[:n_in], refs[n_in:n_in + n]
        host_out = refs[n_in + n:n_in + n + n_out]
        r_out = refs[n_in + n + n_out:n_in + 2 * n + n_out]
        rest = refs[n_in + 2 * n + n_out:]
        host_scr, sems = rest[:len(rest) - len(rider.scratch)], rest[len(rest) - len(rider.scratch):]

        @pl.when(first())
        def _():
            rider.start(r_in, r_out, sems)

        body(*host_in, *host_out, *host_scr)

        @pl.when(last())
        def _():
            rider.finish(r_in, r_out, sems)

    return wrapped


def _rider_args(rider):
    if rider is None:
        return [], [], [], [], []
    return rider.operands, [ANY] * rider.n, [ANY] * rider.n, rider.out_shape, rider.scratch


def _pair_exchange(grads, name):
    n = len(grads)

    def body(*refs):
        ins, outs = refs[:n], refs[n:2 * n]
        send_sem, recv_sem = refs[2 * n:]
        x, y, c, _ = _place()
        sibling = (x, y, 1 - c)
        cps = []
        for a in range(n):
            for j in range(N_CHIPS):
                s = a * N_CHIPS + j
                cps.append(pltpu.make_async_remote_copy(
                    src_ref=ins[a].at[j, 1 - c], dst_ref=outs[a].at[j], send_sem=send_sem.at[s], recv_sem=recv_sem.at[s],
                    device_id=sibling, device_id_type=MESH))
        for cp in cps:
            cp.start()
        for cp in cps:
            cp.wait()

    return pl.pallas_call(
        body, name=name,
        in_specs=[ANY] * n, out_specs=[ANY] * n,
        out_shape=[jax.ShapeDtypeStruct((N_CHIPS,) + g.shape[2:], F32) for g in grads],
        scratch_shapes=[pltpu.SemaphoreType.DMA((N_CHIPS * n,)), pltpu.SemaphoreType.DMA((N_CHIPS * n,))],
        )(*grads)


def _pair_sum(g, got, c_idx):
    _, _, r, cw = g.shape
    tr = _row_tile(r, cw)

    def body(c_ref, g_ref, got_ref, p_ref, pb_ref):
        v = g_ref[...] + got_ref[...]
        p_ref[...] = v
        pb_ref[...] = v.astype(BF16)

    blk = pl.BlockSpec((None, tr, cw), lambda j, i, c_ref: (j, i, 0))
    return pl.pallas_call(
        body, name="grad_pair_sum",
        grid_spec=pltpu.PrefetchScalarGridSpec(
            num_scalar_prefetch=1, grid=(N_CHIPS, r // tr),
            in_specs=[pl.BlockSpec((None, None, tr, cw), lambda j, i, c_ref: (j, c_ref[0], i, 0)), blk],
            out_specs=[blk, blk]),
        out_shape=[jax.ShapeDtypeStruct((N_CHIPS, r, cw), F32), jax.ShapeDtypeStruct((N_CHIPS, r, cw), BF16)],
        compiler_params=_params(("parallel", "parallel")))(c_idx, g, got)


def _chip_sum(p, got, me_idx):
    _, r, cw = p.shape
    tr = _row_tile(r, cw)

    def body(me_ref, own_ref, got_ref, t_ref):
        me = me_ref[0]
        acc = None
        for s in range(N_CHIPS):
            term = jnp.where(me == s, own_ref[...], got_ref[s].astype(F32))
            acc = term if acc is None else acc + term
        t_ref[...] = acc

    return pl.pallas_call(
        body, name="grad_chip_sum",
        grid_spec=pltpu.PrefetchScalarGridSpec(
            num_scalar_prefetch=1, grid=(r // tr,),
            in_specs=[pl.BlockSpec((None, tr, cw), lambda i, me_ref: (me_ref[0], i, 0)),
                      pl.BlockSpec((N_CHIPS, tr, cw), lambda i, me_ref: (0, i, 0))],
            out_specs=pl.BlockSpec((tr, cw), lambda i, me_ref: (i, 0))),
        out_shape=jax.ShapeDtypeStruct((r, cw), F32),
        compiler_params=_params(("parallel",)))(me_idx, p, got)


def _pair_share(halves):
    n = len(halves)

    def body(*refs):
        ins, outs = refs[:n], refs[n:2 * n]
        send_sem, recv_sem = refs[2 * n:]
        x, y, c, _ = _place()
        cps = [pltpu.make_async_remote_copy(
            src_ref=ins[a], dst_ref=outs[a], send_sem=send_sem.at[a], recv_sem=recv_sem.at[a],
            device_id=(x, y, 1 - c), device_id_type=MESH) for a in range(n)]
        for cp in cps:
            cp.start()
        for cp in cps:
            cp.wait()

    return pl.pallas_call(
        body, name="grad_pair_share",
        in_specs=[ANY] * n, out_specs=[ANY] * n,
        out_shape=[jax.ShapeDtypeStruct(h.shape, F32) for h in halves],
        scratch_shapes=[pltpu.SemaphoreType.DMA((n,)), pltpu.SemaphoreType.DMA((n,))],
        )(*halves)


def _small_allreduce(pack):
    m_per, ncol = pack.shape
    n_dev = 8

    def body(x_ref, sum_ref, all_ref, send_sems, recv_sems, local_sem):
        x, y, c, others = _place()
        me, sibling = (x, y, c), (x, y, 1 - c)

        def rows(px, py, pc):
            return all_ref.at[pl.ds((4 * px + 2 * py + pc) * m_per, m_per), :]

        def copy(k, block, to, src=None):
            return pltpu.make_async_remote_copy(
                src_ref=rows(*block) if src is None else src, dst_ref=rows(*block),
                send_sem=send_sems.at[k], recv_sem=recv_sems.at[k], device_id=to, device_id_type=MESH)

        mine = pltpu.make_async_copy(x_ref, rows(*me), local_sem)
        mine.start()
        first = [copy(0, me, sibling, src=x_ref)]
        first += [copy(1 + j, me, (*chip, c), src=x_ref) for j, chip in enumerate(others)]
        for cp in first:
            cp.start()
        passed = [copy(4 + j, (*chip, c), sibling) for j, chip in enumerate(others)]
        for j, chip in enumerate(others):
            copy(1 + j, (*chip, c), me).wait_recv()
            passed[j].start()
        copy(0, sibling, me).wait_recv()
        for j, chip in enumerate(others):
            copy(4 + j, (*chip, 1 - c), me).wait_recv()
        for cp in first + passed:
            cp.wait_send()
        mine.wait()
        acc = all_ref[0:m_per, :]
        for dvc in range(1, n_dev):
            acc = acc + all_ref[dvc * m_per:(dvc + 1) * m_per, :]
        sum_ref[...] = acc

    return pl.pallas_call(
        body, name="small_allreduce",
        in_specs=[pl.BlockSpec(memory_space=pltpu.VMEM)],
        out_specs=pl.BlockSpec(memory_space=pltpu.VMEM),
        out_shape=jax.ShapeDtypeStruct((m_per, ncol), F32),
        scratch_shapes=[pltpu.VMEM((n_dev * m_per, ncol), F32),
                        pltpu.SemaphoreType.DMA((7,)), pltpu.SemaphoreType.DMA((7,)), pltpu.SemaphoreType.DMA],
        )(pack)


def _adam_math(w, g, m, v):
    m = ADAM_B1 * m + (1.0 - ADAM_B1) * g
    v = ADAM_B2 * v + (1.0 - ADAM_B2) * (g * g)
    m_hat = m / (1.0 - ADAM_B1 ** ADAM_STEP)
    v_hat = v / (1.0 - ADAM_B2 ** ADAM_STEP)
    delta = -ADAM_LR * (m_hat / (jnp.sqrt(v_hat) + ADAM_EPS) + ADAM_WD * w)
    return delta, m, v


def _adamw(mine, theirs, c_idx, w, m, v, name):
    r, C = mine.shape
    tr = _row_tile(r, C, 512 * 1024)
    nt = r // tr

    def body(c_ref, mine_ref, theirs_ref, w_ref, m_ref, v_ref, g_ref, d_ref, nm_ref, nv_ref):
        g = jnp.where(pl.program_id(0) == c_ref[0], mine_ref[...], theirs_ref[...])
        g_ref[...] = g
        d_ref[...], nm_ref[...], nv_ref[...] = _adam_math(w_ref[...], g, m_ref[...], v_ref[...])

    half = pl.BlockSpec((tr, C), lambda h, i, c_ref: (i, 0))
    full = pl.BlockSpec((tr, C), lambda h, i, c_ref: (h * nt + i, 0))
    shp = jax.ShapeDtypeStruct((2 * r, C), F32)
    return pl.pallas_call(
        body, name=name,
        grid_spec=pltpu.PrefetchScalarGridSpec(
            num_scalar_prefetch=1, grid=(2, nt),
            in_specs=[half, half, full, full, full], out_specs=[full] * 4),
        out_shape=[shp] * 4,
        compiler_params=_params(("parallel", "parallel")))(c_idx, mine, theirs, w, m, v)


def _small_update(gsum, logits_pack, w, m, v):
    def body(gs_ref, lg_ref, w_ref, m_ref, v_ref, g_ref, d_ref, nm_ref, nv_ref):
        g_ref[...] = gs_ref[...]
        l0, l1, l2 = lg_ref[0:1, :], lg_ref[1:2, :], lg_ref[2:3, :]
        mx = jnp.maximum(jnp.maximum(l0, l1), l2)
        e0, e1, e2 = jnp.exp(l0 - mx), jnp.exp(l1 - mx), jnp.exp(l2 - mx)
        tot = e0 + e1 + e2
        p0, p1, p2 = e0 / tot, e1 / tot, e2 / tot
        dlb = gs_ref[4:5, :]
        g_ref[4:5, :] = dlb * p0 * (1.0 - p0)
        g_ref[5:6, :] = -dlb * p0 * p1
        g_ref[6:7, :] = -dlb * p0 * p2
        d_ref[...], nm_ref[...], nv_ref[...] = _adam_math(w_ref[...], g_ref[...], m_ref[...], v_ref[...])

    full = pl.BlockSpec(memory_space=pltpu.VMEM)
    shp = jax.ShapeDtypeStruct(gsum.shape, F32)
    return pl.pallas_call(
        body, name="small_update", in_specs=[full] * 5, out_specs=[full] * 4, out_shape=[shp] * 4)(
            gsum, logits_pack, w, m, v)


def _pack_small(norm_mix, norm_ffn, lb3, out_norm, final_norm, extra=None):
    ncol = norm_mix.shape[1]
    on = jnp.pad(out_norm.reshape(1, -1), ((0, 0), (0, ncol - out_norm.size)))
    rows = [norm_mix, norm_ffn, lb3, on, final_norm.reshape(1, ncol)]
    if extra is not None:
        rows.append(extra)
    used = sum(r.shape[0] for r in rows)
    rows.append(jnp.zeros((SMALL_ROWS - used, ncol), F32))
    return jnp.concatenate(rows, axis=0)


WEIGHT_NAMES = ("hin", "hout", "qkv", "aout", "fin0", "fin1", "fdn0", "fdn1")
FIRST_WEIGHTS = ("hin", "hout")
LATE_WEIGHTS_A = ("fin0", "fdn0")
LATE_WEIGHTS_B = ("qkv", "aout", "fin1", "fdn1")
EARLY_GRADS = ("qkv", "aout", "fin0", "fin1", "fdn0", "fdn1")


def _split_weights(hgrn_w_in, hgrn_w_out, attn_w_qkv, attn_w_out, ffn_w_in, ffn_w_down):
    return {"hin": hgrn_w_in[0], "hout": hgrn_w_out[0], "qkv": attn_w_qkv[0], "aout": attn_w_out[0],
            "fin0": ffn_w_in[0], "fin1": ffn_w_in[1], "fdn0": ffn_w_down[0], "fdn1": ffn_w_down[1]}


def _halves(v):
    r, c = v.shape
    return v.reshape(2, r // 2, c)


def _full_weights(gathered):
    out = {}
    for k, g in gathered.items():
        _, _, r, c = g.shape
        if k in ("hin", "qkv", "fin0", "fin1"):
            out[k] = g.reshape(N_CHIPS, 2 * r, c)
        else:
            out[k] = g.reshape(N_CHIPS * 2 * r, c)
    return out


class _StepComm:
    def __init__(self, shards, c_idx, me_idx):
        self.shards, self.c_idx, self.me_idx = shards, c_idx, me_idx
        self.halves = {}

    def gather_rider(self, names):
        return _GatherRider([_halves(self.shards[k].astype(BF16)) for k in names])

    def gathered(self, names, got):
        return _full_weights(dict(zip(names, got)))

    def first_weights(self):
        return self.gathered(FIRST_WEIGHTS, _run_rider(self.gather_rider(FIRST_WEIGHTS), "gather_first"))

    def _pair_sums(self, grads, name):
        names = list(grads)
        g4 = []
        for k in names:
            r, c = self.shards[k].shape
            g4.append(grads[k].reshape(N_CHIPS, 2, r // 2, c))
        from_sibling = _pair_exchange(g4, name)
        return names, [_pair_sum(g, got, self.c_idx) for g, got in zip(g4, from_sibling)]

    def _chip_sums(self, names, sums, got):
        for k, s, g in zip(names, sums, got):
            self.halves[k] = _chip_sum(s[0], g, self.me_idx)

    def exchange_rider(self, grads):
        self._riding = self._pair_sums(grads, "grad_pair_exchange_early")
        return _ExchangeRider([s[1] for s in self._riding[1]])

    def exchanged(self, got):
        self._chip_sums(*self._riding, got)

    def reduce_rest(self, grads):
        names, sums = self._pair_sums(grads, "grad_pair_exchange_late")
        self._chip_sums(names, sums, _run_rider(_ExchangeRider([s[1] for s in sums]), "grad_chip_exchange_late"))

    def shared_halves(self):
        mine = [self.halves[k] for k in WEIGHT_NAMES]
        return dict(zip(WEIGHT_NAMES, zip(mine, _pair_share(mine))))


def kernel(x, norm_mix, norm_ffn, hgrn_w_in, hgrn_lb_logits, hgrn_out_norm, hgrn_w_out, attn_w_qkv, attn_w_out, ffn_w_in, ffn_w_down, final_norm, loss_target, m_norm_mix, m_norm_ffn, m_hgrn_w_in, m_hgrn_lb_logits, m_hgrn_out_norm, m_hgrn_w_out, m_attn_w_qkv, m_attn_w_out, m_ffn_w_in, m_ffn_w_down, m_final_norm, v_norm_mix, v_norm_ffn, v_hgrn_w_in, v_hgrn_lb_logits, v_hgrn_out_norm, v_hgrn_w_out, v_attn_w_qkv, v_attn_w_out, v_ffn_w_in, v_ffn_w_down, v_final_norm):
    S = x.shape[1]
    xi, yi, ci = lax.axis_index("x"), lax.axis_index("y"), lax.axis_index("c")
    c_idx = jnp.reshape(ci, (1,)).astype(jnp.int32)
    me_idx = jnp.reshape(2 * xi + yi, (1,)).astype(jnp.int32)

    w_own = _split_weights(hgrn_w_in, hgrn_w_out, attn_w_qkv, attn_w_out, ffn_w_in, ffn_w_down)
    m_own = _split_weights(m_hgrn_w_in, m_hgrn_w_out, m_attn_w_qkv, m_attn_w_out, m_ffn_w_in, m_ffn_w_down)
    v_own = _split_weights(v_hgrn_w_in, v_hgrn_w_out, v_attn_w_qkv, v_attn_w_out, v_ffn_w_in, v_ffn_w_down)

    comm = _StepComm(w_own, c_idx, me_idx)
    loss, dx, grads, small = _local_step(
        x.reshape(S, D_MODEL), loss_target.reshape(S, D_MODEL), norm_mix, norm_ffn, hgrn_lb_logits,
        hgrn_out_norm, final_norm.reshape(1, D_MODEL), comm)
    comm.reduce_rest(grads)

    g_out, d_out, m_out, v_out = {}, {}, {}, {}
    for k, (mine, theirs) in comm.shared_halves().items():
        g_out[k], d_out[k], m_out[k], v_out[k] = _adamw(mine, theirs, c_idx, w_own[k], m_own[k], v_own[k], "adamw_" + k)

    loss_row = jnp.pad(loss, ((0, 0), (0, D_MODEL - loss.shape[1])))
    lb3 = jnp.concatenate([small["lb"], jnp.zeros((2, D_MODEL), F32)], axis=0)
    on_grad = jnp.sum(small["out_norm"], axis=0, keepdims=True)
    pack = _pack_small(small["norm_mix"], small["norm_ffn"], lb3, on_grad, small["final_norm"], loss_row)
    gsum = _small_allreduce(pack)
    w_s = _pack_small(norm_mix, norm_ffn, hgrn_lb_logits, hgrn_out_norm, final_norm)
    m_s = _pack_small(m_norm_mix, m_norm_ffn, m_hgrn_lb_logits, m_hgrn_out_norm, m_final_norm)
    v_s = _pack_small(v_norm_mix, v_norm_ffn, v_hgrn_lb_logits, v_hgrn_out_norm, v_final_norm)
    lg_pack = jnp.pad(hgrn_lb_logits, ((0, 8 - hgrn_lb_logits.shape[0]), (0, 0)))
    sg, sd, sm, sv = _small_update(gsum, lg_pack, w_s, m_s, v_s)

    def unpack(p):
        return (p[0:2], p[2:4], p[4:7], p[7:8, :HEAD], p[8])

    def big(dct):
        return (dct["hin"][None], dct["hout"][None], dct["qkv"][None], dct["aout"][None],
                jnp.stack([dct["fin0"], dct["fin1"]]), jnp.stack([dct["fdn0"], dct["fdn1"]]))

    def assemble(p, dct):
        nmx, nff, lbl, onm, fnm = unpack(p)
        hin, hout, qkv, aout, fin, fdn = big(dct)
        return (nmx, nff, hin, lbl, onm, hout, qkv, aout, fin, fdn, fnm)

    total_loss = gsum[9, 0]
    return (total_loss, dx.reshape(1, S, D_MODEL), *assemble(sg, g_out), *assemble(sd, d_out),
            *assemble(sm, m_out), *assemble(sv, v_out))
```

```python
import functools

import jax
import jax.numpy as jnp
from jax import lax
from jax.experimental import pallas as pl
from jax.experimental.pallas import tpu as pltpu

F32 = jnp.float32
BF16 = jnp.bfloat16
MESH = pl.DeviceIdType.MESH

D_MODEL = 1024
HEAD = 128
HGRN_HEADS = 8
HGRN_CHUNK = 64
ATTN_GROUPS = ((128, 1), (512, 4), (2048, 16))
ATTN_SPAN = 128
HEADS_PER_GROUP = 4
GROUP_W = HEADS_PER_GROUP * HEAD
D_FF = 2816
NORM_EPS = 1e-6
ROPE_THETA = 10000.0
NEG = -1e30

ADAM_LR, ADAM_B1, ADAM_B2, ADAM_EPS, ADAM_WD, ADAM_STEP = 0.001, 0.9, 0.999, 1e-08, 0.01, 10

N_CHIPS = 4
VMEM_LIMIT = 56 * 1024 * 1024
SMALL_ROWS = 16


def _params(sem=None):
    return pltpu.CompilerParams(dimension_semantics=sem, vmem_limit_bytes=VMEM_LIMIT)


def _row_tile(rows, cols, budget_bytes=3 * 512 * 1024):
    best = 8
    for t in range(8, rows + 1, 8):
        if rows % t == 0 and t * cols * 4 <= budget_bytes:
            best = t
    assert rows % best == 0
    return best


def _grid_corner(i, j):
    return jnp.logical_and(pl.program_id(0) == i, pl.program_id(1) == j)


def _sigmoid(v):
    return 1.0 / (1.0 + jnp.exp(-v))


def _dot(a, b):
    return jnp.dot(a, b, preferred_element_type=F32)


def _dot_nt(a, b):
    return lax.dot_general(a, b, (((1,), (1,)), ((), ())), preferred_element_type=F32)


def _dot_tn(a, b):
    return lax.dot_general(a, b, (((0,), (0,)), ((), ())), preferred_element_type=F32)


def _dot_exact(a, b):
    return jnp.dot(a, b, preferred_element_type=F32, precision=lax.Precision.HIGHEST)


def _rstd(v):
    return lax.rsqrt(jnp.mean(v * v, axis=-1, keepdims=True) + NORM_EPS)


def _norm_mm(h, gain, w3, tn, name, tm=512, rider=None):
    S, K = h.shape
    J, _, n = w3.shape
    tpn = n // tn
    gi, gj = S // tm, J * tpn

    def body(h_ref, g_ref, w_ref, y_ref, u_scr):
        @pl.when(pl.program_id(1) == 0)
        def _():
            v = h_ref[...]
            u_scr[...] = (v * _rstd(v) * g_ref[...]).astype(BF16)

        y_ref[...] = _dot(u_scr[...], w_ref[...])

    r_ops, r_in, r_out, r_shape, r_scr = _rider_args(rider)
    res = pl.pallas_call(
        _ride(rider, body, 3, 1, functools.partial(_grid_corner, 0, 0), functools.partial(_grid_corner, gi - 1, gj - 1)),
        name=name, grid=(gi, gj),
        in_specs=[pl.BlockSpec((tm, K), lambda i, j: (i, 0)),
                  pl.BlockSpec((1, K), lambda i, j: (0, 0)),
                  pl.BlockSpec((None, K, tn), lambda i, j: (j // tpn, 0, j % tpn))] + r_in,
        out_specs=[pl.BlockSpec((tm, tn), lambda i, j: (i, j))] + r_out,
        out_shape=[jax.ShapeDtypeStruct((S, J * n), F32)] + r_shape,
        scratch_shapes=[pltpu.VMEM((tm, K), BF16)] + r_scr,
        compiler_params=_params(("arbitrary", "arbitrary")))(h, gain, w3, *r_ops)
    return res[0] if rider is None else (res[0], res[1:])


def _mm_res(h, a, w2, name, tm=512):
    S, N = h.shape
    K = a.shape[1]

    def body(h_ref, a_ref, w_ref, o_ref):
        o_ref[...] = h_ref[...] + _dot(a_ref[...], w_ref[...])

    return pl.pallas_call(
        body, name=name, grid=(S // tm,),
        in_specs=[pl.BlockSpec((tm, N), lambda i: (i, 0)),
                  pl.BlockSpec((tm, K), lambda i: (i, 0)),
                  pl.BlockSpec((K, N), lambda i: (0, 0))],
        out_specs=pl.BlockSpec((tm, N), lambda i: (i, 0)),
        out_shape=jax.ShapeDtypeStruct((S, N), F32),
        compiler_params=_params(("parallel",)))(h, a, w2)


def _swiglu(z_ref, F):
    g = z_ref[:, :F]
    return (g * _sigmoid(g) * z_ref[:, F:]).astype(BF16)


def _swiglu_mm_res(h, z, w2, name, tm=256):
    S, N = h.shape
    F = w2.shape[0]

    def body(h_ref, z_ref, w_ref, o_ref, a_ref):
        a = _swiglu(z_ref, F)
        a_ref[...] = a
        o_ref[...] = h_ref[...] + _dot(a, w_ref[...])

    return pl.pallas_call(
        body, name=name, grid=(S // tm,),
        in_specs=[pl.BlockSpec((tm, N), lambda i: (i, 0)),
                  pl.BlockSpec((tm, 2 * F), lambda i: (i, 0)),
                  pl.BlockSpec((F, N), lambda i: (0, 0))],
        out_specs=[pl.BlockSpec((tm, N), lambda i: (i, 0)), pl.BlockSpec((tm, F), lambda i: (i, 0))],
        out_shape=[jax.ShapeDtypeStruct((S, N), F32), jax.ShapeDtypeStruct((S, F), BF16)],
        compiler_params=_params(("parallel",)))(h, z, w2)


def _dy_specs(dy, J, n, tm):
    if dy.ndim == 3:
        return [pl.BlockSpec((None, tm, n), functools.partial(lambda i, j: (j, i, 0), j=j)) for j in range(J)]
    return [pl.BlockSpec((tm, n), functools.partial(lambda i, j: (i, j), j=j)) for j in range(J)]


def _acc_nt(dy_refs, w_ref):
    acc = None
    for j, r in enumerate(dy_refs):
        t = _dot_nt(r[...].astype(BF16), w_ref[j])
        acc = t if acc is None else acc + t
    return acc


def _mm_nt(dy, w3, name, out_dtype=F32, tm=512):
    J, K, n = w3.shape
    S = dy.shape[-2]

    def body(*refs):
        dy_refs, w_ref, o_ref = refs[:J], refs[J], refs[J + 1]
        o_ref[...] = _acc_nt(dy_refs, w_ref).astype(o_ref.dtype)

    return pl.pallas_call(
        body, name=name, grid=(S // tm,),
        in_specs=_dy_specs(dy, J, n, tm) + [pl.BlockSpec((J, K, n), lambda i: (0, 0, 0))],
        out_specs=pl.BlockSpec((tm, K), lambda i: (i, 0)),
        out_shape=jax.ShapeDtypeStruct((S, K), out_dtype),
        compiler_params=_params(("parallel",)))(*([dy] * J), w3)


def _mm_nt_normbwd(dy, w3, h, gain, dh, name, tm=512):
    J, K, n = w3.shape
    S = h.shape[0]

    def body(*refs):
        dy_refs, w_ref, h_ref, g_ref, dh_ref, o_ref, dg_ref = refs[:J], *refs[J:]
        du = _acc_nt(dy_refs, w_ref)
        v = h_ref[...]
        r = _rstd(v)
        xh = v * r
        dyg = du * g_ref[...]
        o_ref[...] = dh_ref[...] + r * (dyg - xh * jnp.mean(dyg * xh, axis=-1, keepdims=True))

        @pl.when(pl.program_id(0) == 0)
        def _():
            dg_ref[...] = jnp.zeros_like(dg_ref)

        dg_ref[...] += jnp.sum(du * xh, axis=0, keepdims=True)

    row = pl.BlockSpec((tm, K), lambda i: (i, 0))
    vec = pl.BlockSpec((1, K), lambda i: (0, 0))
    return pl.pallas_call(
        body, name=name, grid=(S // tm,),
        in_specs=_dy_specs(dy, J, n, tm) + [pl.BlockSpec((J, K, n), lambda i: (0, 0, 0)), row, vec, row],
        out_specs=[row, vec],
        out_shape=[jax.ShapeDtypeStruct((S, K), F32), jax.ShapeDtypeStruct((1, K), F32)],
        compiler_params=_params(("arbitrary",)))(*([dy] * J), w3, h, gain, dh)


def _mm_nt_swiglu_bwd(dh, w2, z, name, tm=256):
    F, N = w2.shape
    S = dh.shape[0]

    def body(dh_ref, w_ref, z_ref, o_ref):
        da = _dot_nt(dh_ref[...].astype(BF16), w_ref[...])
        g = z_ref[:, :F]
        u = z_ref[:, F:]
        sg = _sigmoid(g)
        o_ref[:, :F] = (da * u * (sg * (1.0 + g * (1.0 - sg)))).astype(BF16)
        o_ref[:, F:] = (da * (g * sg)).astype(BF16)

    return pl.pallas_call(
        body, name=name, grid=(S // tm,),
        in_specs=[pl.BlockSpec((tm, N), lambda i: (i, 0)),
                  pl.BlockSpec((F, N), lambda i: (0, 0)),
                  pl.BlockSpec((tm, 2 * F), lambda i: (i, 0))],
        out_specs=pl.BlockSpec((tm, 2 * F), lambda i: (i, 0)),
        out_shape=jax.ShapeDtypeStruct((S, 2 * F), BF16),
        compiler_params=_params(("parallel",)))(dh, w2, z)


def _mm_tn(kind, xs, dy, J, n, tn, name):
    tpn = n // tn
    ts = 512
    S, K = xs[0].shape
    x_specs = [pl.BlockSpec((ts, K), lambda c, s: (s, 0))]
    if kind == "norm":
        x_specs.append(pl.BlockSpec((1, K), lambda c, s: (0, 0)))
    nx = len(xs)
    if dy.ndim == 3:
        dy_spec = pl.BlockSpec((None, ts, tn), lambda c, s: (c // tpn, s, c % tpn))
    else:
        dy_spec = pl.BlockSpec((ts, tn), lambda c, s: (s, c))

    def body(*refs):
        x_refs, dy_ref, o_ref = refs[:nx], refs[nx], refs[nx + 1]
        if kind == "norm":
            v = x_refs[0][...]
            xb = (v * _rstd(v) * x_refs[1][...]).astype(BF16)
        else:
            xb = x_refs[0][...].astype(BF16)

        @pl.when(pl.program_id(1) == 0)
        def _():
            o_ref[...] = jnp.zeros_like(o_ref)

        o_ref[...] += _dot_tn(xb, dy_ref[...].astype(BF16))

    return pl.pallas_call(
        body, name=name, grid=(J * tpn, S // ts),
        in_specs=x_specs + [dy_spec],
        out_specs=pl.BlockSpec((None, K, tn), lambda c, s: (c // tpn, 0, c % tpn)),
        out_shape=jax.ShapeDtypeStruct((J, K, n), F32),
        compiler_params=_params(("parallel", "arbitrary")))(*xs, dy)


def _loss_head(h, gain, target, tm=512):
    S, K = h.shape

    def body(h_ref, g_ref, t_ref, dh_ref, loss_ref, dg_ref):
        v = h_ref[...]
        r = _rstd(v)
        xh = v * r
        g = g_ref[...]
        dy = (xh * g - t_ref[...]) * (1.0 / K)
        dyg = dy * g
        dh_ref[...] = r * (dyg - xh * jnp.mean(dyg * xh, axis=-1, keepdims=True))

        @pl.when(pl.program_id(0) == 0)
        def _():
            loss_ref[...] = jnp.zeros_like(loss_ref)
            dg_ref[...] = jnp.zeros_like(dg_ref)

        part = jnp.sum(jnp.sum(dy * dy, axis=-1, keepdims=True), axis=0, keepdims=True) * (0.5 * K)
        lane = lax.broadcasted_iota(jnp.int32, loss_ref.shape, 1)
        loss_ref[...] += jnp.where(lane == 0, part, 0.0)
        dg_ref[...] += jnp.sum(dy * xh, axis=0, keepdims=True)

    row = pl.BlockSpec((tm, K), lambda i: (i, 0))
    vec = pl.BlockSpec((1, K), lambda i: (0, 0))
    return pl.pallas_call(
        body, name="loss_head", grid=(S // tm,),
        in_specs=[row, vec, row],
        out_specs=[row, pl.BlockSpec((1, HEAD), lambda i: (0, 0)), vec],
        out_shape=[jax.ShapeDtypeStruct((S, K), F32), jax.ShapeDtypeStruct((1, HEAD), F32),
                   jax.ShapeDtypeStruct((1, K), F32)],
        compiler_params=_params(("arbitrary",)))(h, gain, target)


def _lower_bound(lg_ref):
    l0, l1, l2 = lg_ref[0:1, :], lg_ref[1:2, :], lg_ref[2:3, :]
    mx = jnp.maximum(jnp.maximum(l0, l1), l2)
    e0, e1, e2 = jnp.exp(l0 - mx), jnp.exp(l1 - mx), jnp.exp(l2 - mx)
    return e0 / (e0 + e1 + e2)


def _chunks(v, ncb):
    C = HGRN_CHUNK
    return [v[c * C:(c + 1) * C] for c in range(ncb)]


def _rows(parts):
    return jnp.concatenate(parts, axis=0)


def _block_gates(qz, fz, lb, ncb):
    C = HGRN_CHUNK
    row = lax.broadcasted_iota(jnp.int32, (C, C), 0)
    col = lax.broadcasted_iota(jnp.int32, (C, C), 1)
    tri = (col <= row).astype(F32)
    first_half = lax.broadcasted_iota(jnp.int32, (C, HEAD), 0) < C // 2
    sig = _sigmoid(fz)
    fg = lb + (1.0 - lb) * sig
    key = 1.0 - fg
    lg = jnp.log(fg)
    lgs = _chunks(lg, ncb)
    b = _rows([_dot_exact(tri, v) for v in lgs])
    r_c = [jnp.sum(jnp.where(first_half, v, 0.0), axis=0, keepdims=True) for v in lgs]
    bl_c = [jnp.sum(v, axis=0, keepdims=True) for v in lgs]
    r = _rows([jnp.broadcast_to(v, (C, HEAD)) for v in r_c])
    bl = _rows([jnp.broadcast_to(v, (C, HEAD)) for v in bl_c])
    sq = _sigmoid(qz)
    qy = qz * sq
    return sig, fg, key, b, r, bl, bl_c, sq, qy


def _hgrn_fwd(proj, logits, gain, tb=1024, rider=None):
    S = proj.shape[0]
    H, C = HGRN_HEADS, HGRN_CHUNK
    ncb = tb // C

    def body(q_ref, f_ref, i_ref, g_ref, lg_ref, gn_ref, o_ref, og_ref, st_ref, state):
        @pl.when(pl.program_id(1) == 0)
        def _():
            state[...] = jnp.zeros_like(state)

        lb = _lower_bound(lg_ref)
        causal = lax.broadcasted_iota(jnp.int32, (C, C), 1) <= lax.broadcasted_iota(jnp.int32, (C, C), 0)
        qz, fz, gz = q_ref[...], f_ref[...], g_ref[...]
        _, _, key, b, r, bl, bl_c, _, qy = _block_gates(qz, fz, lb, ncb)
        qs = _chunks((qy * jnp.exp(b - r)).astype(BF16), ncb)
        ks = _chunks((key * jnp.exp(r - b)).astype(BF16), ncb)
        qb = _chunks((qy * jnp.exp(b)).astype(BF16), ncb)
        ke = _chunks((key * jnp.exp(bl - b)).astype(BF16), ncb)
        vb = _chunks(i_ref[...].astype(BF16), ncb)
        o_intra, upd = [], []
        for c in range(ncb):
            a = jnp.where(causal, _dot_nt(qs[c], ks[c]), 0.0).astype(BF16)
            o_intra.append(_dot(a, vb[c]))
            upd.append(_dot_tn(vb[c], ke[c]))
        st = state[...]
        for c in range(ncb):
            st_ref[c] = st
            st = st * jnp.exp(bl_c[c]) + upd[c]
        state[...] = st
        o = _rows([_dot_nt(qb[c], st_ref[c].astype(BF16)) + o_intra[c] for c in range(ncb)])
        o_ref[...] = o
        og_ref[...] = ((o * _rstd(o) * gn_ref[...]) * (gz * _sigmoid(gz))).astype(BF16)

    def part(p):
        return pl.BlockSpec((tb, HEAD), functools.partial(lambda h, i, p: (i, p * H + h), p=p))

    nb = S // tb
    r_ops, r_in, r_out, r_shape, r_scr = _rider_args(rider)
    res = pl.pallas_call(
        _ride(rider, body, 6, 3, functools.partial(_grid_corner, 0, 0), functools.partial(_grid_corner, H - 1, nb - 1)),
        name="hgrn_fwd", grid=(H, nb),
        in_specs=[part(0), part(1), part(2), part(3),
                  pl.BlockSpec((3, HEAD), lambda h, i: (0, h)),
                  pl.BlockSpec((1, HEAD), lambda h, i: (0, 0))] + r_in,
        out_specs=[pl.BlockSpec((tb, HEAD), lambda h, i: (i, h)),
                   pl.BlockSpec((tb, HEAD), lambda h, i: (i, h)),
                   pl.BlockSpec((None, ncb, HEAD, HEAD), lambda h, i: (h, i, 0, 0))] + r_out,
        out_shape=[jax.ShapeDtypeStruct((S, H * HEAD), F32),
                   jax.ShapeDtypeStruct((S, H * HEAD), BF16),
                   jax.ShapeDtypeStruct((H, S // C, HEAD, HEAD), F32)] + r_shape,
        scratch_shapes=[pltpu.VMEM((HEAD, HEAD), F32)] + r_scr,
        compiler_params=_params(("arbitrary", "arbitrary")))(proj, proj, proj, proj, logits, gain, *r_ops)
    return res[:3], res[3:]


def _hgrn_bwd(proj, logits, gain, o, states, dog, tb=1024, rider=None):
    S = proj.shape[0]
    H, C = HGRN_HEADS, HGRN_CHUNK
    ncb = tb // C
    nb = S // tb

    def body(q_ref, f_ref, i_ref, g_ref, lg_ref, gn_ref, o_ref, st_ref, dog_ref,
             dp_ref, dlb_ref, dgn_ref, dstate, dst_scr):
        @pl.when(pl.program_id(1) == 0)
        def _():
            dstate[...] = jnp.zeros_like(dstate)
            dlb_ref[...] = jnp.zeros_like(dlb_ref)
            dgn_ref[...] = jnp.zeros_like(dgn_ref)

        lb = _lower_bound(lg_ref)
        oml = 1.0 - lb
        gn = gn_ref[...]
        row = lax.broadcasted_iota(jnp.int32, (C, C), 0)
        col = lax.broadcasted_iota(jnp.int32, (C, C), 1)
        causal = col <= row
        tri_up = (col >= row).astype(F32)
        qz, fz, gz = q_ref[...], f_ref[...], g_ref[...]
        sig, fg, key, b, r, bl, bl_c, sq, qy = _block_gates(qz, fz, lb, ncb)
        e_br, e_rb, e_b, e_lb = jnp.exp(b - r), jnp.exp(r - b), jnp.exp(b), jnp.exp(bl - b)
        qs_v, ks_v = (qy * e_br).astype(BF16), (key * e_rb).astype(BF16)
        qb_v, ke_v = (qy * e_b).astype(BF16), (key * e_lb).astype(BF16)
        qs, ks, qb, ke = _chunks(qs_v, ncb), _chunks(ks_v, ncb), _chunks(qb_v, ncb), _chunks(ke_v, ncb)
        vb = _chunks(i_ref[...].astype(BF16), ncb)
        ov = o_ref[...]
        rs = _rstd(ov)
        xh = ov * rs
        sg = _sigmoid(gz)
        dog_v = dog_ref[...]
        dgz = dog_v * (xh * gn) * (sg * (1.0 + gz * (1.0 - sg)))
        don = dog_v * (gz * sg)
        dgn_ref[...] += jnp.sum(don * xh, axis=0, keepdims=True)
        dyg = don * gn
        do = rs * (dyg - xh * jnp.mean(dyg * xh, axis=-1, keepdims=True))
        dob = _chunks(do.astype(BF16), ncb)
        dv_in, dqs, dks, wst = [], [], [], []
        for c in range(ncb):
            a = jnp.where(causal, _dot_nt(qs[c], ks[c]), 0.0).astype(BF16)
            da = jnp.where(causal, _dot_nt(dob[c], vb[c]), 0.0).astype(BF16)
            dv_in.append(_dot_tn(a, dob[c]))
            dqs.append(_dot(da, ks[c]))
            dks.append(_dot_tn(da, qs[c]))
            wst.append(_dot_tn(dob[c], qb[c]))
        e_l = [jnp.exp(v) for v in bl_c]
        dst = dstate[...]
        for c in reversed(range(ncb)):
            dst_scr[c] = dst
            dst = wst[c] + dst * e_l[c]
        dstate[...] = dst
        dv, dqb, dke, dbl_st = [], [], [], []
        for c in range(ncb):
            dst1 = dst_scr[c]
            st0 = st_ref[c]
            dst1b = dst1.astype(BF16)
            dv.append(dv_in[c] + _dot_nt(ke[c], dst1b))
            dqb.append(_dot(dob[c], st0.astype(BF16)))
            dke.append(_dot(vb[c], dst1b))
            dbl_st.append(jnp.sum(dst1 * st0, axis=0, keepdims=True) * e_l[c])
        dqs, dks, dqb, dke, dv = _rows(dqs), _rows(dks), _rows(dqb), _rows(dke), _rows(dv)
        dke_ke = dke * ke_v.astype(F32)
        db = dqs * qs_v.astype(F32) - dks * ks_v.astype(F32) + dqb * qb_v.astype(F32) - dke_ke
        dlg = []
        for c, (db_c, kk_c) in enumerate(zip(_chunks(db, ncb), _chunks(dke_ke, ncb))):
            dbl = jnp.sum(kk_c, axis=0, keepdims=True) + dbl_st[c]
            dlg.append(_dot_exact(tri_up, db_c) + dbl)
        dlg = _rows(dlg)
        dkey = dks * e_rb + dke * e_lb
        dqy = dqs * e_br + dqb * e_b
        dfg = dlg / fg - dkey
        dlb_ref[...] += jnp.sum(dfg * (1.0 - sig), axis=0, keepdims=True)
        dp_ref[0] = (dqy * (sq * (1.0 + qz * (1.0 - sq)))).astype(BF16)
        dp_ref[1] = (dfg * oml * sig * (1.0 - sig)).astype(BF16)
        dp_ref[2] = dv.astype(BF16)
        dp_ref[3] = dgz.astype(BF16)

    def part(p):
        return pl.BlockSpec((tb, HEAD), functools.partial(lambda h, i, p: (nb - 1 - i, p * H + h), p=p))

    blk = pl.BlockSpec((tb, HEAD), lambda h, i: (nb - 1 - i, h))
    acc = pl.BlockSpec((None, 1, HEAD), lambda h, i: (h, 0, 0))
    r_ops, r_in, r_out, r_shape, r_scr = _rider_args(rider)
    res = pl.pallas_call(
        _ride(rider, body, 9, 3, functools.partial(_grid_corner, 0, 0), functools.partial(_grid_corner, H - 1, nb - 1)),
        name="hgrn_bwd", grid=(H, nb),
        in_specs=[part(0), part(1), part(2), part(3),
                  pl.BlockSpec((3, HEAD), lambda h, i: (0, h)),
                  pl.BlockSpec((1, HEAD), lambda h, i: (0, 0)),
                  blk,
                  pl.BlockSpec((None, ncb, HEAD, HEAD), lambda h, i: (h, nb - 1 - i, 0, 0)),
                  blk] + r_in,
        out_specs=[pl.BlockSpec((4, tb, HEAD), lambda h, i: (0, nb - 1 - i, h)), acc, acc] + r_out,
        out_shape=[jax.ShapeDtypeStruct((4, S, H * HEAD), BF16),
                   jax.ShapeDtypeStruct((H, 1, HEAD), F32),
                   jax.ShapeDtypeStruct((H, 1, HEAD), F32)] + r_shape,
        scratch_shapes=[pltpu.VMEM((HEAD, HEAD), F32), pltpu.VMEM((ncb, HEAD, HEAD), F32)] + r_scr,
        compiler_params=_params(("arbitrary", "arbitrary")))(
            proj, proj, proj, proj, logits, gain, o, states, dog, *r_ops)
    return res[:3], res[3:]


def _rope(v, cos, sin):
    return v * cos + pltpu.roll(v, HEAD // 2, 1) * sin


def _band_masks():
    qi = lax.broadcasted_iota(jnp.int32, (ATTN_SPAN, ATTN_SPAN), 0)
    kj = lax.broadcasted_iota(jnp.int32, (ATTN_SPAN, ATTN_SPAN), 1)
    return kj <= qi, kj >= qi


def _attn_fwd(a):
    d, L, _ = a.shape
    nb = L // ATTN_SPAN
    scale = HEAD ** -0.5

    def body(q_ref, kc_ref, kp_ref, vc_ref, vp_ref, o_ref, lse_ref):
        n = pl.program_id(1)
        mask_c, mask_p0 = _band_masks()
        mask_p = jnp.logical_and(mask_p0, n > 0)
        for hh in range(HEADS_PER_GROUP):
            cols = slice(hh * HEAD, (hh + 1) * HEAD)
            q, kc, kp = q_ref[:, cols], kc_ref[:, cols], kp_ref[:, cols]
            s_c = jnp.where(mask_c, _dot_nt(q, kc) * scale, NEG)
            s_p = jnp.where(mask_p, _dot_nt(q, kp) * scale, NEG)
            m = jnp.maximum(jnp.max(s_c, axis=-1, keepdims=True), jnp.max(s_p, axis=-1, keepdims=True))
            p_c = jnp.exp(s_c - m)
            p_p = jnp.exp(s_p - m)
            l = jnp.sum(p_c, axis=-1, keepdims=True) + jnp.sum(p_p, axis=-1, keepdims=True)
            acc = _dot(p_c.astype(BF16), vc_ref[:, cols]) + _dot(p_p.astype(BF16), vp_ref[:, cols])
            o_ref[:, cols] = acc / l
            lse_ref[:, cols] = jnp.broadcast_to(m + jnp.log(l), (ATTN_SPAN, HEAD))

    def blk(part, prev):
        if prev:
            return pl.BlockSpec((None, ATTN_SPAN, GROUP_W), functools.partial(lambda r, n, p: (r, jnp.maximum(n - 1, 0), p), p=part))
        return pl.BlockSpec((None, ATTN_SPAN, GROUP_W), functools.partial(lambda r, n, p: (r, n, p), p=part))

    out = pl.BlockSpec((None, ATTN_SPAN, GROUP_W), lambda r, n: (r, n, 0))
    return pl.pallas_call(
        body, name=f"attn_fwd_d{d}", grid=(d, nb),
        in_specs=[blk(0, False), blk(1, False), blk(1, True), blk(2, False), blk(2, True)],
        out_specs=[out, out],
        out_shape=[jax.ShapeDtypeStruct((d, L, GROUP_W), F32), jax.ShapeDtypeStruct((d, L, GROUP_W), F32)],
        compiler_params=_params(("parallel", "arbitrary")))(a, a, a, a, a)


def _attn_bwd(a, do, lse, dd):
    d, L, _ = a.shape
    nb = L // ATTN_SPAN
    scale = HEAD ** -0.5

    def body(qc_ref, qn_ref, kp_ref, kc_ref, vp_ref, vc_ref, doc_ref, don_ref, lc_ref, ln_ref, ddc_ref, ddn_ref, da_ref):
        n = pl.program_id(1)
        mask_c, mask_p0 = _band_masks()
        mask_p = jnp.logical_and(mask_p0, n > 0)
        mask_n = jnp.logical_and(mask_p0, n < nb - 1)
        for hh in range(HEADS_PER_GROUP):
            cols = slice(hh * HEAD, (hh + 1) * HEAD)
            q, qn, kc, kp = qc_ref[:, cols], qn_ref[:, cols], kc_ref[:, cols], kp_ref[:, cols]
            vc, vp = vc_ref[:, cols], vp_ref[:, cols]
            do_c = doc_ref[:, cols].astype(BF16)
            do_n = don_ref[:, cols].astype(BF16)
            lse_c, lse_n = lc_ref[:, cols], ln_ref[:, cols]
            dd_c, dd_n = ddc_ref[:, cols], ddn_ref[:, cols]
            p_c = jnp.where(mask_c, jnp.exp(_dot_nt(q, kc) * scale - lse_c), 0.0)
            p_p = jnp.where(mask_p, jnp.exp(_dot_nt(q, kp) * scale - lse_c), 0.0)
            ds_c = (p_c * (_dot_nt(do_c, vc) + dd_c)).astype(BF16)
            ds_p = (p_p * (_dot_nt(do_c, vp) + dd_c)).astype(BF16)
            dq = (_dot(ds_c, kc) + _dot(ds_p, kp)) * scale
            p_n = jnp.where(mask_n, jnp.exp(_dot_nt(qn, kc) * scale - lse_n), 0.0)
            ds_n = (p_n * (_dot_nt(do_n, vc) + dd_n)).astype(BF16)
            dk = (_dot_tn(ds_c, q) + _dot_tn(ds_n, qn)) * scale
            dv = _dot_tn(p_c.astype(BF16), do_c) + _dot_tn(p_n.astype(BF16), do_n)
            da_ref[:, cols] = dq
            da_ref[:, GROUP_W + hh * HEAD:GROUP_W + (hh + 1) * HEAD] = dk
            da_ref[:, 2 * GROUP_W + hh * HEAD:2 * GROUP_W + (hh + 1) * HEAD] = dv

    def rel(delta):
        if delta < 0:
            return lambda n: jnp.maximum(n - 1, 0)
        if delta > 0:
            return lambda n: jnp.minimum(n + 1, nb - 1)
        return lambda n: n

    def blk(width, part, delta):
        f = rel(delta)
        return pl.BlockSpec((None, ATTN_SPAN, width), functools.partial(lambda r, n, p, f: (r, f(n), p), p=part, f=f))

    g = GROUP_W
    return pl.pallas_call(
        body, name=f"attn_bwd_d{d}", grid=(d, nb),
        in_specs=[blk(g, 0, 0), blk(g, 0, 1), blk(g, 1, -1), blk(g, 1, 0), blk(g, 2, -1), blk(g, 2, 0),
                  blk(g, 0, 0), blk(g, 0, 1), blk(g, 0, 0), blk(g, 0, 1), blk(g, 0, 0), blk(g, 0, 1)],
        out_specs=pl.BlockSpec((None, ATTN_SPAN, 3 * g), lambda r, n: (r, n, 0)),
        out_shape=jax.ShapeDtypeStruct((d, L, 3 * g), F32),
        compiler_params=_params(("parallel", "arbitrary")))(
            a, a, a, a, a, a, do, do, lse, lse, dd, dd)


def _group_weights(lse_refs, cols):
    ls = [r[:, cols] for r in lse_refs]
    mx = jnp.maximum(jnp.maximum(ls[0], ls[1]), ls[2])
    es = [jnp.exp(v - mx) for v in ls]
    tot = es[0] + es[1] + es[2]
    return [e / tot for e in es]


def _gather_tokens(ref, scr, d, tm):
    if d == 1:
        return ref.at[0]
    for r in range(d):
        scr[pl.ds(r, tm // d, stride=d), :] = ref[r]
    return scr


def _scatter_tokens(scr, ref, d, tm):
    if d == 1:
        ref[0] = scr[...]
        return
    for r in range(d):
        ref[r] = scr[pl.ds(r, tm // d, stride=d), :]


def _head_spec(d, tm):
    return pl.BlockSpec((d, tm // d, HEAD), lambda i, j: (0, i, j))


def _dilate_group(qkv, cos, sin, g, d, tm=1024):
    S = qkv.shape[0]
    G = len(ATTN_GROUPS)

    def body(x_ref, cos_ref, sin_ref, out_ref):
        rotate = pl.program_id(1) < 2 * HEADS_PER_GROUP
        for r in range(d):
            rows = slice(None) if d == 1 else pl.ds(r, tm // d, stride=d)
            v = x_ref[rows, :]
            out_ref[r] = jnp.where(rotate, _rope(v, cos_ref[rows, :], sin_ref[rows, :]), v).astype(BF16)

    def src(i, j):
        return i, ((j // HEADS_PER_GROUP) * G + g) * HEADS_PER_GROUP + j % HEADS_PER_GROUP

    tab = pl.BlockSpec((tm, HEAD), lambda i, j: (i, 0))
    return pl.pallas_call(
        body, name=f"attn_dilate_d{d}", grid=(S // tm, 3 * HEADS_PER_GROUP),
        in_specs=[pl.BlockSpec((tm, HEAD), src), tab, tab],
        out_specs=_head_spec(d, tm),
        out_shape=jax.ShapeDtypeStruct((d, S // d, 3 * GROUP_W), BF16),
        compiler_params=_params(("parallel", "arbitrary")))(qkv, cos, sin)


def _undilate_group(da, dqkv, cos, sin, g, tm=1024):
    d, L, _ = da.shape
    S = d * L
    G = len(ATTN_GROUPS)

    def body(*refs):
        da_ref, cos_ref, sin_ref, out_ref, scr = refs[0], refs[1], refs[2], refs[-2], refs[-1]
        tok = _gather_tokens(da_ref, scr, d, tm)[...]
        rotate = pl.program_id(1) < 2 * HEADS_PER_GROUP
        out_ref[...] = jnp.where(rotate, _rope(tok, cos_ref[...], -sin_ref[...]), tok).astype(BF16)

    def dst(i, j):
        return i, ((j // HEADS_PER_GROUP) * G + g) * HEADS_PER_GROUP + j % HEADS_PER_GROUP

    tab = pl.BlockSpec((tm, HEAD), lambda i, j: (i, 0))
    operands = (da, cos, sin) if dqkv is None else (da, cos, sin, dqkv)
    return pl.pallas_call(
        body, name=f"attn_undilate_d{d}", grid=(S // tm, 3 * HEADS_PER_GROUP),
        in_specs=[_head_spec(d, tm), tab, tab] + ([] if dqkv is None else [ANY]),
        out_specs=pl.BlockSpec((tm, HEAD), dst),
        out_shape=jax.ShapeDtypeStruct((S, 3 * G * GROUP_W), BF16),
        input_output_aliases={} if dqkv is None else {3: 0},
        scratch_shapes=[pltpu.VMEM((tm, HEAD), F32)],
        compiler_params=_params(("parallel", "arbitrary")))(*operands)


def _attn_merge(os_, lses, tm=512):
    G = len(os_)
    S = os_[0].shape[0] * os_[0].shape[1]

    def body(*refs):
        o_refs, l_refs, out_ref = refs[:G], refs[G:2 * G], refs[2 * G]
        scr = refs[2 * G + 1:]
        hh = pl.program_id(1)
        o_tok = [_gather_tokens(o_refs[g], scr[g], d, tm) for g, (_, d) in enumerate(ATTN_GROUPS)]
        l_tok = [_gather_tokens(l_refs[g], scr[G + g], d, tm) for g, (_, d) in enumerate(ATTN_GROUPS)]
        al = _group_weights(l_tok, slice(None))
        for g in range(G):
            cols = pl.ds(pl.multiple_of(g * GROUP_W + hh * HEAD, HEAD), HEAD)
            out_ref[:, cols] = (o_tok[g][...] * al[g]).astype(BF16)

    specs = [_head_spec(d, tm) for _, d in ATTN_GROUPS]
    return pl.pallas_call(
        body, name="attn_merge", grid=(S // tm, HEADS_PER_GROUP),
        in_specs=specs + specs,
        out_specs=pl.BlockSpec((tm, G * GROUP_W), lambda i, j: (i, 0)),
        out_shape=jax.ShapeDtypeStruct((S, G * GROUP_W), BF16),
        scratch_shapes=[pltpu.VMEM((tm, HEAD), F32)] * (2 * G),
        compiler_params=_params(("parallel", "arbitrary")))(*os_, *lses)


def _attn_merge_bwd(os_, lses, doa, tm=512):
    G = len(os_)
    S = doa.shape[0]

    def body(*refs):
        o_refs, l_refs, doa_ref = refs[:G], refs[G:2 * G], refs[2 * G]
        do_refs, dd_refs = refs[2 * G + 1:3 * G + 1], refs[3 * G + 1:4 * G + 1]
        scr = refs[4 * G + 1:]
        hh = pl.program_id(1)
        o_tok = [_gather_tokens(o_refs[g], scr[g], d, tm) for g, (_, d) in enumerate(ATTN_GROUPS)]
        l_tok = [_gather_tokens(l_refs[g], scr[G + g], d, tm) for g, (_, d) in enumerate(ATTN_GROUPS)]
        do_tok, dd_tok = scr[2 * G:3 * G], scr[3 * G:]
        al = _group_weights(l_tok, slice(None))
        mix = None
        for g in range(G):
            dg = doa_ref[:, pl.ds(pl.multiple_of(g * GROUP_W + hh * HEAD, HEAD), HEAD)]
            do_tok[g][...] = dg * al[g]
            t = al[g] * jnp.sum(dg * o_tok[g][...], axis=-1, keepdims=True)
            mix = t if mix is None else mix + t
        for g, (_, d) in enumerate(ATTN_GROUPS):
            dd_tok[g][...] = jnp.broadcast_to(-al[g] * mix, (tm, HEAD))
            _scatter_tokens(do_tok[g], do_refs[g], d, tm)
            _scatter_tokens(dd_tok[g], dd_refs[g], d, tm)

    specs = [_head_spec(d, tm) for _, d in ATTN_GROUPS]
    shapes = [jax.ShapeDtypeStruct((d, S // d, GROUP_W), F32) for _, d in ATTN_GROUPS]
    return pl.pallas_call(
        body, name="attn_merge_bwd", grid=(S // tm, HEADS_PER_GROUP),
        in_specs=specs + specs + [pl.BlockSpec((tm, G * GROUP_W), lambda i, j: (i, 0))],
        out_specs=specs + specs,
        out_shape=shapes + shapes,
        scratch_shapes=[pltpu.VMEM((tm, HEAD), F32)] * (4 * G),
        compiler_params=_params(("parallel", "arbitrary")))(*os_, *lses, doa)


def _rope_tables(S):
    inv_freq = 1.0 / (ROPE_THETA ** (jnp.arange(0, HEAD, 2, dtype=F32) / HEAD))
    ang = jnp.arange(S, dtype=F32)[:, None] * inv_freq[None, :]
    cos, sin = jnp.cos(ang), jnp.sin(ang)
    return jnp.concatenate([cos, cos], axis=-1), jnp.concatenate([-sin, sin], axis=-1)


def _local_step(x, target, norm_mix, norm_ffn, lb_logits, out_gain, final_norm, comm):
    S = x.shape[0]
    nm0, nm1 = norm_mix[0:1], norm_mix[1:2]
    nf0, nf1 = norm_ffn[0:1], norm_ffn[1:2]
    w = comm.first_weights()

    rider = comm.gather_rider(LATE_WEIGHTS_A)
    res = _norm_mm(x, nm0, w["hin"], D_MODEL, "hgrn_in", rider=rider)
    proj, got = (res, ()) if rider is None else res
    w.update(comm.gathered(LATE_WEIGHTS_A, got))
    (o, og, states), got = _hgrn_fwd(proj, lb_logits, out_gain, rider=comm.gather_rider(LATE_WEIGHTS_B))
    w.update(comm.gathered(LATE_WEIGHTS_B, got))
    fin_tn = w["fin0"].shape[2]
    h1 = _mm_res(x, og, w["hout"], "hgrn_out")
    z0 = _norm_mm(h1, nf0, w["fin0"], fin_tn, "ffn0_in")
    h2, act0 = _swiglu_mm_res(h1, z0, w["fdn0"], "ffn0_down")
    qkv = _norm_mm(h2, nm1, w["qkv"], w["qkv"].shape[2], "attn_qkv")
    cos, sin = _rope_tables(S)
    a_g = [_dilate_group(qkv, cos, sin, gi, d) for gi, (_, d) in enumerate(ATTN_GROUPS)]
    o_g, lse_g = zip(*[_attn_fwd(a) for a in a_g])
    oa = _attn_merge(o_g, lse_g)
    h3 = _mm_res(h2, oa, w["aout"], "attn_out")
    z1 = _norm_mm(h3, nf1, w["fin1"], fin_tn, "ffn1_in")
    h4, act1 = _swiglu_mm_res(h3, z1, w["fdn1"], "ffn1_down")
    dh4, loss, d_final = _loss_head(h4, final_norm, target)

    grads, small = {}, {"final_norm": d_final}

    def ffn_bwd(dh, h_in, z, act, gain, w_in, w_dn, tag):
        dz = _mm_nt_swiglu_bwd(dh, w_dn, z, tag + "_down_dx")
        g_dn = _mm_tn("plain", (act,), dh, 1, D_MODEL, D_MODEL, tag + "_down_dw")
        g_in = _mm_tn("norm", (h_in, gain), dz, N_CHIPS, fin_tn, fin_tn, tag + "_in_dw")
        dh_in, dgain = _mm_nt_normbwd(dz, w_in, h_in, gain, dh, tag + "_in_dx")
        return dh_in, dgain, g_in, g_dn[0]

    dh3, d_nf1, grads["fin1"], grads["fdn1"] = ffn_bwd(dh4, h3, z1, act1, nf1, w["fin1"], w["fdn1"], "ffn1")
    doa = _mm_nt(dh3, w["aout"][None], "attn_out_dx")
    grads["aout"] = _mm_tn("plain", (oa,), dh3, 1, D_MODEL, 512, "attn_out_dw")[0]
    merged = _attn_merge_bwd(o_g, lse_g, doa)
    G = len(ATTN_GROUPS)
    das = [_attn_bwd(a_g[gi], merged[gi], lse_g[gi], merged[G + gi]) for gi in range(G)]
    dqkv = None
    for gi in range(G):
        dqkv = _undilate_group(das[gi], dqkv, cos, sin, gi)
    n_qkv = w["qkv"].shape[2]
    grads["qkv"] = _mm_tn("norm", (h2, nm1), dqkv, N_CHIPS, n_qkv, n_qkv, "attn_qkv_dw")
    dh2, d_nm1 = _mm_nt_normbwd(dqkv, w["qkv"], h2, nm1, dh3, "attn_qkv_dx")

    dh1, d_nf0, grads["fin0"], grads["fdn0"] = ffn_bwd(dh2, h1, z0, act0, nf0, w["fin0"], w["fdn0"], "ffn0")
    dog = _mm_nt(dh1, w["hout"][None], "hgrn_out_dx")
    rider = comm.exchange_rider({k: grads.pop(k) for k in EARLY_GRADS})
    (dproj, dlb, dgn), got = _hgrn_bwd(proj, lb_logits, out_gain, o, states, dog, rider=rider)
    comm.exchanged(got)
    grads["hout"] = _mm_tn("plain", (og,), dh1, 1, D_MODEL, 512, "hgrn_out_dw")[0]
    grads["hin"] = _mm_tn("norm", (x, nm0), dproj, N_CHIPS, D_MODEL, D_MODEL, "hgrn_in_dw")
    dx, d_nm0 = _mm_nt_normbwd(dproj, w["hin"], x, nm0, dh1, "hgrn_in_dx")

    small["norm_mix"] = jnp.concatenate([d_nm0, d_nm1], axis=0)
    small["norm_ffn"] = jnp.concatenate([d_nf0, d_nf1], axis=0)
    small["lb"] = dlb.reshape(1, HGRN_HEADS * HEAD)
    small["out_norm"] = dgn.reshape(HGRN_HEADS, HEAD)
    return loss, dx, grads, small


def _place():
    x, y, c = lax.axis_index("x"), lax.axis_index("y"), lax.axis_index("c")
    others = [(1 - x, y), (x, 1 - y), (1 - x, 1 - y)]
    return x, y, c, others


ANY = pl.BlockSpec(memory_space=pl.ANY)


class _GatherRider:
    def __init__(self, shards):
        self.operands = list(shards)
        n = self.n = len(shards)
        self.out_shape = [jax.ShapeDtypeStruct((N_CHIPS,) + s.shape, s.dtype) for s in shards]
        self.scratch = [pltpu.SemaphoreType.DMA((3 * n,)), pltpu.SemaphoreType.DMA((3 * n,)),
                        pltpu.SemaphoreType.DMA((3 * n,)), pltpu.SemaphoreType.DMA((3 * n,)),
                        pltpu.SemaphoreType.DMA((n,)), pltpu.SemaphoreType.DMA((n,))]

    def _copies(self, ins, outs, sems):
        ici_send, ici_recv, _, _, own_send, own_recv = sems
        x, y, c, others = _place()
        me = 2 * x + y
        own = [pltpu.make_async_remote_copy(
            src_ref=ins[a], dst_ref=outs[a].at[me], send_sem=own_send.at[a], recv_sem=own_recv.at[a],
            device_id=(x, y, 1 - c), device_id_type=MESH) for a in range(self.n)]
        sends = [pltpu.make_async_remote_copy(
            src_ref=ins[a].at[c], dst_ref=outs[a].at[me, c], send_sem=ici_send.at[a * 3 + k], recv_sem=ici_recv.at[a * 3 + k],
            device_id=(ox, oy, c), device_id_type=MESH) for a in range(self.n) for k, (ox, oy) in enumerate(others)]
        return own, sends

    def start(self, ins, outs, sems):
        own, sends = self._copies(ins, outs, sems)
        for cp in own + sends:
            cp.start()

    def finish(self, ins, outs, sems):
        ici_send, ici_recv, d2d_send, d2d_recv, _, _ = sems
        x, y, c, others = _place()
        sibling = (x, y, 1 - c)
        own, sends = self._copies(ins, outs, sems)
        passes = []
        for a in range(self.n):
            for k, (ox, oy) in enumerate(others):
                s = a * 3 + k
                got = outs[a].at[2 * ox + oy, c]
                pltpu.make_async_remote_copy(
                    src_ref=got, dst_ref=got, send_sem=ici_send.at[s], recv_sem=ici_recv.at[s],
                    device_id=(ox, oy, c), device_id_type=MESH).wait_recv()
                fwd = pltpu.make_async_remote_copy(
                    src_ref=got, dst_ref=got, send_sem=d2d_send.at[s], recv_sem=d2d_recv.at[s],
                    device_id=sibling, device_id_type=MESH)
                fwd.start()
                passes.append(fwd)
        for a in range(self.n):
            for k, (ox, oy) in enumerate(others):
                s = a * 3 + k
                theirs = outs[a].at[2 * ox + oy, 1 - c]
                pltpu.make_async_remote_copy(
                    src_ref=theirs, dst_ref=theirs, send_sem=d2d_send.at[s], recv_sem=d2d_recv.at[s],
                    device_id=sibling, device_id_type=MESH).wait_recv()
        for cp in own:
            cp.wait()
        for cp in sends + passes:
            cp.wait_send()


class _ExchangeRider:
    def __init__(self, parts):
        self.operands = list(parts)
        n = self.n = len(parts)
        self.out_shape = [jax.ShapeDtypeStruct(p.shape, p.dtype) for p in parts]
        self.scratch = [pltpu.SemaphoreType.DMA((3 * n,)), pltpu.SemaphoreType.DMA((3 * n,))]

    def _copies(self, ins, outs, sems):
        send_sem, recv_sem = sems
        x, y, c, others = _place()
        me = 2 * x + y
        return [pltpu.make_async_remote_copy(
            src_ref=ins[a].at[2 * ox + oy], dst_ref=outs[a].at[me], send_sem=send_sem.at[a * 3 + k],
            recv_sem=recv_sem.at[a * 3 + k], device_id=(ox, oy, c), device_id_type=MESH)
            for a in range(self.n) for k, (ox, oy) in enumerate(others)]

    def start(self, ins, outs, sems):
        for cp in self._copies(ins, outs, sems):
            cp.start()

    def finish(self, ins, outs, sems):
        send_sem, recv_sem = sems
        x, y, c, others = _place()
        for a in range(self.n):
            for k, (ox, oy) in enumerate(others):
                s = a * 3 + k
                got = outs[a].at[2 * ox + oy]
                pltpu.make_async_remote_copy(
                    src_ref=got, dst_ref=got, send_sem=send_sem.at[s], recv_sem=recv_sem.at[s],
                    device_id=(ox, oy, c), device_id_type=MESH).wait_recv()
        for cp in self._copies(ins, outs, sems):
            cp.wait_send()


def _run_rider(rider, name):
    n = rider.n

    def body(*refs):
        ins, outs, sems = refs[:n], refs[n:2 * n], refs[2 * n:]
        rider.start(ins, outs, sems)
        rider.finish(ins, outs, sems)

    return pl.pallas_call(
        body, name=name, in_specs=[ANY] * n, out_specs=[ANY] * n,
        out_shape=rider.out_shape, scratch_shapes=rider.scratch)(*rider.operands)


def _ride(rider, body, n_in, n_out, first, last):
    if rider is None:
        return body
    n = rider.n

    def wrapped(*refs):
        host_in, r_in = refs[:n_in], refs[n_in:n_in + n]
        host_out = refs[n_in + n:n_in + n + n_out]
        r_out = refs[n_in + n + n_out:n_in + 2 * n + n_out]
        rest = refs[n_in + 2 * n + n_out:]
        host_scr, sems = rest[:len(rest) - len(rider.scratch)], rest[len(rest) - len(rider.scratch):]

        @pl.when(first())
        def _():
            rider.start(r_in, r_out, sems)

        body(*host_in, *host_out, *host_scr)

        @pl.when(last())
        def _():
            rider.finish(r_in, r_out, sems)

    return wrapped


def _rider_args(rider):
    if rider is None:
        return [], [], [], [], []
    return rider.operands, [ANY] * rider.n, [ANY] * rider.n, rider.out_shape, rider.scratch


def _pair_exchange(grads, name):
    n = len(grads)

    def body(*refs):
        ins, outs = refs[:n], refs[n:2 * n]
        send_sem, recv_sem = refs[2 * n:]
        x, y, c, _ = _place()
        sibling = (x, y, 1 - c)
        cps = []
        for a in range(n):
            for j in range(N_CHIPS):
                s = a * N_CHIPS + j
                cps.append(pltpu.make_async_remote_copy(
                    src_ref=ins[a].at[j, 1 - c], dst_ref=outs[a].at[j], send_sem=send_sem.at[s], recv_sem=recv_sem.at[s],
                    device_id=sibling, device_id_type=MESH))
        for cp in cps:
            cp.start()
        for cp in cps:
            cp.wait()

    return pl.pallas_call(
        body, name=name,
        in_specs=[ANY] * n, out_specs=[ANY] * n,
        out_shape=[jax.ShapeDtypeStruct((N_CHIPS,) + g.shape[2:], F32) for g in grads],
        scratch_shapes=[pltpu.SemaphoreType.DMA((N_CHIPS * n,)), pltpu.SemaphoreType.DMA((N_CHIPS * n,))],
        )(*grads)


def _pair_sum(g, got, c_idx):
    _, _, r, cw = g.shape
    tr = _row_tile(r, cw)

    def body(c_ref, g_ref, got_ref, p_ref, pb_ref):
        v = g_ref[...] + got_ref[...]
        p_ref[...] = v
        pb_ref[...] = v.astype(BF16)

    blk = pl.BlockSpec((None, tr, cw), lambda j, i, c_ref: (j, i, 0))
    return pl.pallas_call(
        body, name="grad_pair_sum",
        grid_spec=pltpu.PrefetchScalarGridSpec(
            num_scalar_prefetch=1, grid=(N_CHIPS, r // tr),
            in_specs=[pl.BlockSpec((None, None, tr, cw), lambda j, i, c_ref: (j, c_ref[0], i, 0)), blk],
            out_specs=[blk, blk]),
        out_shape=[jax.ShapeDtypeStruct((N_CHIPS, r, cw), F32), jax.ShapeDtypeStruct((N_CHIPS, r, cw), BF16)],
        compiler_params=_params(("parallel", "parallel")))(c_idx, g, got)


def _chip_sum(p, got, me_idx):
    _, r, cw = p.shape
    tr = _row_tile(r, cw)

    def body(me_ref, own_ref, got_ref, t_ref):
        me = me_ref[0]
        acc = None
        for s in range(N_CHIPS):
            term = jnp.where(me == s, own_ref[...], got_ref[s].astype(F32))
            acc = term if acc is None else acc + term
        t_ref[...] = acc

    return pl.pallas_call(
        body, name="grad_chip_sum",
        grid_spec=pltpu.PrefetchScalarGridSpec(
            num_scalar_prefetch=1, grid=(r // tr,),
            in_specs=[pl.BlockSpec((None, tr, cw), lambda i, me_ref: (me_ref[0], i, 0)),
                      pl.BlockSpec((N_CHIPS, tr, cw), lambda i, me_ref: (0, i, 0))],
            out_specs=pl.BlockSpec((tr, cw), lambda i, me_ref: (i, 0))),
        out_shape=jax.ShapeDtypeStruct((r, cw), F32),
        compiler_params=_params(("parallel",)))(me_idx, p, got)


def _pair_share(halves):
    n = len(halves)

    def body(*refs):
        ins, outs = refs[:n], refs[n:2 * n]
        send_sem, recv_sem = refs[2 * n:]
        x, y, c, _ = _place()
        cps = [pltpu.make_async_remote_copy(
            src_ref=ins[a], dst_ref=outs[a], send_sem=send_sem.at[a], recv_sem=recv_sem.at[a],
            device_id=(x, y, 1 - c), device_id_type=MESH) for a in range(n)]
        for cp in cps:
            cp.start()
        for cp in cps:
            cp.wait()

    return pl.pallas_call(
        body, name="grad_pair_share",
        in_specs=[ANY] * n, out_specs=[ANY] * n,
        out_shape=[jax.ShapeDtypeStruct(h.shape, F32) for h in halves],
        scratch_shapes=[pltpu.SemaphoreType.DMA((n,)), pltpu.SemaphoreType.DMA((n,))],
        )(*halves)


def _small_allreduce(pack):
    m_per, ncol = pack.shape
    n_dev = 8

    def body(x_ref, sum_ref, all_ref, send_sems, recv_sems, local_sem):
        x, y, c, others = _place()
        me, sibling = (x, y, c), (x, y, 1 - c)

        def rows(px, py, pc):
            return all_ref.at[pl.ds((4 * px + 2 * py + pc) * m_per, m_per), :]

        def copy(k, block, to, src=None):
            return pltpu.make_async_remote_copy(
                src_ref=rows(*block) if src is None else src, dst_ref=rows(*block),
                send_sem=send_sems.at[k], recv_sem=recv_sems.at[k], device_id=to, device_id_type=MESH)

        mine = pltpu.make_async_copy(x_ref, rows(*me), local_sem)
        mine.start()
        first = [copy(0, me, sibling, src=x_ref)]
        first += [copy(1 + j, me, (*chip, c), src=x_ref) for j, chip in enumerate(others)]
        for cp in first:
            cp.start()
        passed = [copy(4 + j, (*chip, c), sibling) for j, chip in enumerate(others)]
        for j, chip in enumerate(others):
            copy(1 + j, (*chip, c), me).wait_recv()
            passed[j].start()
        copy(0, sibling, me).wait_recv()
        for j, chip in enumerate(others):
            copy(4 + j, (*chip, 1 - c), me).wait_recv()
        for cp in first + passed:
            cp.wait_send()
        mine.wait()
        acc = all_ref[0:m_per, :]
        for dvc in range(1, n_dev):
            acc = acc + all_ref[dvc * m_per:(dvc + 1) * m_per, :]
        sum_ref[...] = acc

    return pl.pallas_call(
        body, name="small_allreduce",
        in_specs=[pl.BlockSpec(memory_space=pltpu.VMEM)],
        out_specs=pl.BlockSpec(memory_space=pltpu.VMEM),
        out_shape=jax.ShapeDtypeStruct((m_per, ncol), F32),
        scratch_shapes=[pltpu.VMEM((n_dev * m_per, ncol), F32),
                        pltpu.SemaphoreType.DMA((7,)), pltpu.SemaphoreType.DMA((7,)), pltpu.SemaphoreType.DMA],
        )(pack)


def _adam_math(w, g, m, v):
    m = ADAM_B1 * m + (1.0 - ADAM_B1) * g
    v = ADAM_B2 * v + (1.0 - ADAM_B2) * (g * g)
    m_hat = m / (1.0 - ADAM_B1 ** ADAM_STEP)
    v_hat = v / (1.0 - ADAM_B2 ** ADAM_STEP)
    delta = -ADAM_LR * (m_hat / (jnp.sqrt(v_hat) + ADAM_EPS) + ADAM_WD * w)
    return delta, m, v


def _adamw(mine, theirs, c_idx, w, m, v, name):
    r, C = mine.shape
    tr = _row_tile(r, C, 512 * 1024)
    nt = r // tr

    def body(c_ref, mine_ref, theirs_ref, w_ref, m_ref, v_ref, g_ref, d_ref, nm_ref, nv_ref):
        g = jnp.where(pl.program_id(0) == c_ref[0], mine_ref[...], theirs_ref[...])
        g_ref[...] = g
        d_ref[...], nm_ref[...], nv_ref[...] = _adam_math(w_ref[...], g, m_ref[...], v_ref[...])

    half = pl.BlockSpec((tr, C), lambda h, i, c_ref: (i, 0))
    full = pl.BlockSpec((tr, C), lambda h, i, c_ref: (h * nt + i, 0))
    shp = jax.ShapeDtypeStruct((2 * r, C), F32)
    return pl.pallas_call(
        body, name=name,
        grid_spec=pltpu.PrefetchScalarGridSpec(
            num_scalar_prefetch=1, grid=(2, nt),
            in_specs=[half, half, full, full, full], out_specs=[full] * 4),
        out_shape=[shp] * 4,
        compiler_params=_params(("parallel", "parallel")))(c_idx, mine, theirs, w, m, v)


def _small_update(gsum, logits_pack, w, m, v):
    def body(gs_ref, lg_ref, w_ref, m_ref, v_ref, g_ref, d_ref, nm_ref, nv_ref):
        g_ref[...] = gs_ref[...]
        l0, l1, l2 = lg_ref[0:1, :], lg_ref[1:2, :], lg_ref[2:3, :]
        mx = jnp.maximum(jnp.maximum(l0, l1), l2)
        e0, e1, e2 = jnp.exp(l0 - mx), jnp.exp(l1 - mx), jnp.exp(l2 - mx)
        tot = e0 + e1 + e2
        p0, p1, p2 = e0 / tot, e1 / tot, e2 / tot
        dlb = gs_ref[4:5, :]
        g_ref[4:5, :] = dlb * p0 * (1.0 - p0)
        g_ref[5:6, :] = -dlb * p0 * p1
        g_ref[6:7, :] = -dlb * p0 * p2
        d_ref[...], nm_ref[...], nv_ref[...] = _adam_math(w_ref[...], g_ref[...], m_ref[...], v_ref[...])

    full = pl.BlockSpec(memory_space=pltpu.VMEM)
    shp = jax.ShapeDtypeStruct(gsum.shape, F32)
    return pl.pallas_call(
        body, name="small_update", in_specs=[full] * 5, out_specs=[full] * 4, out_shape=[shp] * 4)(
            gsum, logits_pack, w, m, v)


def _pack_small(norm_mix, norm_ffn, lb3, out_norm, final_norm, extra=None):
    ncol = norm_mix.shape[1]
    on = jnp.pad(out_norm.reshape(1, -1), ((0, 0), (0, ncol - out_norm.size)))
    rows = [norm_mix, norm_ffn, lb3, on, final_norm.reshape(1, ncol)]
    if extra is not None:
        rows.append(extra)
    used = sum(r.shape[0] for r in rows)
    rows.append(jnp.zeros((SMALL_ROWS - used, ncol), F32))
    return jnp.concatenate(rows, axis=0)


WEIGHT_NAMES = ("hin", "hout", "qkv", "aout", "fin0", "fin1", "fdn0", "fdn1")
FIRST_WEIGHTS = ("hin", "hout")
LATE_WEIGHTS_A = ("fin0", "fdn0")
LATE_WEIGHTS_B = ("qkv", "aout", "fin1", "fdn1")
EARLY_GRADS = ("qkv", "aout", "fin0", "fin1", "fdn0", "fdn1")


def _split_weights(hgrn_w_in, hgrn_w_out, attn_w_qkv, attn_w_out, ffn_w_in, ffn_w_down):
    return {"hin": hgrn_w_in[0], "hout": hgrn_w_out[0], "qkv": attn_w_qkv[0], "aout": attn_w_out[0],
            "fin0": ffn_w_in[0], "fin1": ffn_w_in[1], "fdn0": ffn_w_down[0], "fdn1": ffn_w_down[1]}


def _halves(v):
    r, c = v.shape
    return v.reshape(2, r // 2, c)


def _full_weights(gathered):
    out = {}
    for k, g in gathered.items():
        _, _, r, c = g.shape
        if k in ("hin", "qkv", "fin0", "fin1"):
            out[k] = g.reshape(N_CHIPS, 2 * r, c)
        else:
            out[k] = g.reshape(N_CHIPS * 2 * r, c)
    return out


class _StepComm:
    def __init__(self, shards, c_idx, me_idx):
        self.shards, self.c_idx, self.me_idx = shards, c_idx, me_idx
        self.halves = {}

    def gather_rider(self, names):
        return _GatherRider([_halves(self.shards[k].astype(BF16)) for k in names])

    def gathered(self, names, got):
        return _full_weights(dict(zip(names, got)))

    def first_weights(self):
        return self.gathered(FIRST_WEIGHTS, _run_rider(self.gather_rider(FIRST_WEIGHTS), "gather_first"))

    def _pair_sums(self, grads, name):
        names = list(grads)
        g4 = []
        for k in names:
            r, c = self.shards[k].shape
            g4.append(grads[k].reshape(N_CHIPS, 2, r // 2, c))
        from_sibling = _pair_exchange(g4, name)
        return names, [_pair_sum(g, got, self.c_idx) for g, got in zip(g4, from_sibling)]

    def _chip_sums(self, names, sums, got):
        for k, s, g in zip(names, sums, got):
            self.halves[k] = _chip_sum(s[0], g, self.me_idx)

    def exchange_rider(self, grads):
        self._riding = self._pair_sums(grads, "grad_pair_exchange_early")
        return _ExchangeRider([s[1] for s in self._riding[1]])

    def exchanged(self, got):
        self._chip_sums(*self._riding, got)

    def reduce_rest(self, grads):
        names, sums = self._pair_sums(grads, "grad_pair_exchange_late")
        self._chip_sums(names, sums, _run_rider(_ExchangeRider([s[1] for s in sums]), "grad_chip_exchange_late"))

    def shared_halves(self):
        mine = [self.halves[k] for k in WEIGHT_NAMES]
        return dict(zip(WEIGHT_NAMES, zip(mine, _pair_share(mine))))


def kernel(x, norm_mix, norm_ffn, hgrn_w_in, hgrn_lb_logits, hgrn_out_norm, hgrn_w_out, attn_w_qkv, attn_w_out, ffn_w_in, ffn_w_down, final_norm, loss_target, m_norm_mix, m_norm_ffn, m_hgrn_w_in, m_hgrn_lb_logits, m_hgrn_out_norm, m_hgrn_w_out, m_attn_w_qkv, m_attn_w_out, m_ffn_w_in, m_ffn_w_down, m_final_norm, v_norm_mix, v_norm_ffn, v_hgrn_w_in, v_hgrn_lb_logits, v_hgrn_out_norm, v_hgrn_w_out, v_attn_w_qkv, v_attn_w_out, v_ffn_w_in, v_ffn_w_down, v_final_norm):
    S = x.shape[1]
    xi, yi, ci = lax.axis_index("x"), lax.axis_index("y"), lax.axis_index("c")
    c_idx = jnp.reshape(ci, (1,)).astype(jnp.int32)
    me_idx = jnp.reshape(2 * xi + yi, (1,)).astype(jnp.int32)

    w_own = _split_weights(hgrn_w_in, hgrn_w_out, attn_w_qkv, attn_w_out, ffn_w_in, ffn_w_down)
    m_own = _split_weights(m_hgrn_w_in, m_hgrn_w_out, m_attn_w_qkv, m_attn_w_out, m_ffn_w_in, m_ffn_w_down)
    v_own = _split_weights(v_hgrn_w_in, v_hgrn_w_out, v_attn_w_qkv, v_attn_w_out, v_ffn_w_in, v_ffn_w_down)

    comm = _StepComm(w_own, c_idx, me_idx)
    loss, dx, grads, small = _local_step(
        x.reshape(S, D_MODEL), loss_target.reshape(S, D_MODEL), norm_mix, norm_ffn, hgrn_lb_logits,
        hgrn_out_norm, final_norm.reshape(1, D_MODEL), comm)
    comm.reduce_rest(grads)

    g_out, d_out, m_out, v_out = {}, {}, {}, {}
    for k, (mine, theirs) in comm.shared_halves().items():
        g_out[k], d_out[k], m_out[k], v_out[k] = _adamw(mine, theirs, c_idx, w_own[k], m_own[k], v_own[k], "adamw_" + k)

    loss_row = jnp.pad(loss, ((0, 0), (0, D_MODEL - loss.shape[1])))
    lb3 = jnp.concatenate([small["lb"], jnp.zeros((2, D_MODEL), F32)], axis=0)
    on_grad = jnp.sum(small["out_norm"], axis=0, keepdims=True)
    pack = _pack_small(small["norm_mix"], small["norm_ffn"], lb3, on_grad, small["final_norm"], loss_row)
    gsum = _small_allreduce(pack)
    w_s = _pack_small(norm_mix, norm_ffn, hgrn_lb_logits, hgrn_out_norm, final_norm)
    m_s = _pack_small(m_norm_mix, m_norm_ffn, m_hgrn_lb_logits, m_hgrn_out_norm, m_final_norm)
    v_s = _pack_small(v_norm_mix, v_norm_ffn, v_hgrn_lb_logits, v_hgrn_out_norm, v_final_norm)
    lg_pack = jnp.pad(hgrn_lb_logits, ((0, 8 - hgrn_lb_logits.shape[0]), (0, 0)))
    sg, sd, sm, sv = _small_update(gsum, lg_pack, w_s, m_s, v_s)

    def unpack(p):
        return (p[0:2], p[2:4], p[4:7], p[7:8, :HEAD], p[8])

    def big(dct):
        return (dct["hin"][None], dct["hout"][None], dct["qkv"][None], dct["aout"][None],
                jnp.stack([dct["fin0"], dct["fin1"]]), jnp.stack([dct["fdn0"], dct["fdn1"]]))

    def assemble(p, dct):
        nmx, nff, lbl, onm, fnm = unpack(p)
        hin, hout, qkv, aout, fin, fdn = big(dct)
        return (nmx, nff, hin, lbl, onm, hout, qkv, aout, fin, fdn, fnm)

    total_loss = gsum[9, 0]
    return (total_loss, dx.reshape(1, S, D_MODEL), *assemble(sg, g_out), *assemble(sd, d_out),
            *assemble(sm, m_out), *assemble(sv, v_out))
```

```python
import functools

import jax
import jax.numpy as jnp
from jax import lax
from jax.experimental import pallas as pl
from jax.experimental.pallas import tpu as pltpu

F32 = jnp.float32
BF16 = jnp.bfloat16
MESH = pl.DeviceIdType.MESH

D_MODEL = 1024
HEAD = 128
HGRN_HEADS = 8
HGRN_CHUNK = 64
ATTN_GROUPS = ((128, 1), (512, 4), (2048, 16))
ATTN_SPAN = 128
HEADS_PER_GROUP = 4
GROUP_W = HEADS_PER_GROUP * HEAD
D_FF = 2816
NORM_EPS = 1e-6
ROPE_THETA = 10000.0
NEG = -1e30

ADAM_LR, ADAM_B1, ADAM_B2, ADAM_EPS, ADAM_WD, ADAM_STEP = 0.001, 0.9, 0.999, 1e-08, 0.01, 10

N_CHIPS = 4
VMEM_LIMIT = 56 * 1024 * 1024
SMALL_ROWS = 16


def _params(sem=None):
    return pltpu.CompilerParams(dimension_semantics=sem, vmem_limit_bytes=VMEM_LIMIT)


def _row_tile(rows, cols, budget_bytes=3 * 512 * 1024):
    best = 8
    for t in range(8, rows + 1, 8):
        if rows % t == 0 and t * cols * 4 <= budget_bytes:
            best = t
    assert rows % best == 0
    return best


def _grid_corner(i, j):
    return jnp.logical_and(pl.program_id(0) == i, pl.program_id(1) == j)


def _sigmoid(v):
    return 1.0 / (1.0 + jnp.exp(-v))


def _dot(a, b):
    return jnp.dot(a, b, preferred_element_type=F32)


def _dot_nt(a, b):
    return lax.dot_general(a, b, (((1,), (1,)), ((), ())), preferred_element_type=F32)


def _dot_tn(a, b):
    return lax.dot_general(a, b, (((0,), (0,)), ((), ())), preferred_element_type=F32)


def _dot_exact(a, b):
    return jnp.dot(a, b, preferred_element_type=F32, precision=lax.Precision.HIGHEST)


def _rstd(v):
    return lax.rsqrt(jnp.mean(v * v, axis=-1, keepdims=True) + NORM_EPS)


def _norm_mm(h, gain, w3, name, out_dtype=F32, tm=512, rider=None):
    S, K = h.shape
    J, _, n = w3.shape
    gi = S // tm

    def body(h_ref, g_ref, w_ref, y_ref, u_scr):
        @pl.when(pl.program_id(1) == 0)
        def _():
            v = h_ref[...]
            u_scr[...] = (v * _rstd(v) * g_ref[...]).astype(BF16)

        y_ref[...] = _dot(u_scr[...], w_ref[pl.program_id(1)]).astype(y_ref.dtype)

    r_ops, r_in, r_out, r_shape, r_scr = _rider_args(rider)
    res = pl.pallas_call(
        _ride(rider, body, 3, 1, functools.partial(_grid_corner, 0, 0), functools.partial(_grid_corner, gi - 1, J - 1)),
        name=name, grid=(gi, J),
        in_specs=[pl.BlockSpec((tm, K), lambda i, j: (i, 0)),
                  pl.BlockSpec((1, K), lambda i, j: (0, 0)),
                  pl.BlockSpec((J, K, n), lambda i, j: (0, 0, 0))] + r_in,
        out_specs=[pl.BlockSpec((tm, n), lambda i, j: (i, j))] + r_out,
        out_shape=[jax.ShapeDtypeStruct((S, J * n), out_dtype)] + r_shape,
        scratch_shapes=[pltpu.VMEM((tm, K), BF16)] + r_scr,
        compiler_params=_params(("arbitrary", "arbitrary")))(h, gain, w3, *r_ops)
    return res[0] if rider is None else (res[0], res[1:])


def _mm_res(h, a, w2, name, tm=512):
    S, N = h.shape
    K = a.shape[1]

    def body(h_ref, a_ref, w_ref, o_ref):
        o_ref[...] = h_ref[...] + _dot(a_ref[...], w_ref[...])

    return pl.pallas_call(
        body, name=name, grid=(S // tm,),
        in_specs=[pl.BlockSpec((tm, N), lambda i: (i, 0)),
                  pl.BlockSpec((tm, K), lambda i: (i, 0)),
                  pl.BlockSpec((K, N), lambda i: (0, 0))],
        out_specs=pl.BlockSpec((tm, N), lambda i: (i, 0)),
        out_shape=jax.ShapeDtypeStruct((S, N), F32),
        compiler_params=_params(("parallel",)))(h, a, w2)


def _swiglu(z_ref, F):
    g = z_ref[:, :F].astype(F32)
    return (g * _sigmoid(g) * z_ref[:, F:].astype(F32)).astype(BF16)


def _swiglu_mm_res(h, z, w2, name, tm=256):
    S, N = h.shape
    F = w2.shape[0]

    def body(h_ref, z_ref, w_ref, o_ref, a_ref):
        a = _swiglu(z_ref, F)
        a_ref[...] = a
        o_ref[...] = h_ref[...] + _dot(a, w_ref[...])

    return pl.pallas_call(
        body, name=name, grid=(S // tm,),
        in_specs=[pl.BlockSpec((tm, N), lambda i: (i, 0)),
                  pl.BlockSpec((tm, 2 * F), lambda i: (i, 0)),
                  pl.BlockSpec((F, N), lambda i: (0, 0))],
        out_specs=[pl.BlockSpec((tm, N), lambda i: (i, 0)), pl.BlockSpec((tm, F), lambda i: (i, 0))],
        out_shape=[jax.ShapeDtypeStruct((S, N), F32), jax.ShapeDtypeStruct((S, F), BF16)],
        compiler_params=_params(("parallel",)))(h, z, w2)


def _dy_specs(dy, J, n, tm):
    if dy.ndim == 3:
        return [pl.BlockSpec((None, tm, n), functools.partial(lambda i, j: (j, i, 0), j=j)) for j in range(J)]
    return [pl.BlockSpec((tm, n), functools.partial(lambda i, j: (i, j), j=j)) for j in range(J)]


def _acc_nt(dy_refs, w_ref):
    acc = None
    for j, r in enumerate(dy_refs):
        t = _dot_nt(r[...].astype(BF16), w_ref[j])
        acc = t if acc is None else acc + t
    return acc


def _mm_nt(dy, w3, name, out_dtype=F32, tm=512):
    J, K, n = w3.shape
    S = dy.shape[-2]

    def body(*refs):
        dy_refs, w_ref, o_ref = refs[:J], refs[J], refs[J + 1]
        o_ref[...] = _acc_nt(dy_refs, w_ref).astype(o_ref.dtype)

    return pl.pallas_call(
        body, name=name, grid=(S // tm,),
        in_specs=_dy_specs(dy, J, n, tm) + [pl.BlockSpec((J, K, n), lambda i: (0, 0, 0))],
        out_specs=pl.BlockSpec((tm, K), lambda i: (i, 0)),
        out_shape=jax.ShapeDtypeStruct((S, K), out_dtype),
        compiler_params=_params(("parallel",)))(*([dy] * J), w3)


def _mm_nt_normbwd(dy, w3, h, gain, dh, name, tm=512):
    J, K, n = w3.shape
    S = h.shape[0]

    def body(*refs):
        dy_refs, w_ref, h_ref, g_ref, dh_ref, o_ref, dg_ref = refs[:J], *refs[J:]
        du = _acc_nt(dy_refs, w_ref)
        v = h_ref[...]
        r = _rstd(v)
        xh = v * r
        dyg = du * g_ref[...]
        o_ref[...] = dh_ref[...] + r * (dyg - xh * jnp.mean(dyg * xh, axis=-1, keepdims=True))

        @pl.when(pl.program_id(0) == 0)
        def _():
            dg_ref[...] = jnp.zeros_like(dg_ref)

        dg_ref[...] += jnp.sum(du * xh, axis=0, keepdims=True)

    row = pl.BlockSpec((tm, K), lambda i: (i, 0))
    vec = pl.BlockSpec((1, K), lambda i: (0, 0))
    return pl.pallas_call(
        body, name=name, grid=(S // tm,),
        in_specs=_dy_specs(dy, J, n, tm) + [pl.BlockSpec((J, K, n), lambda i: (0, 0, 0)), row, vec, row],
        out_specs=[row, vec],
        out_shape=[jax.ShapeDtypeStruct((S, K), F32), jax.ShapeDtypeStruct((1, K), F32)],
        compiler_params=_params(("arbitrary",)))(*([dy] * J), w3, h, gain, dh)


def _mm_nt_swiglu_bwd(dh, w2, z, name, tm=256):
    F, N = w2.shape
    S = dh.shape[0]

    def body(dh_ref, w_ref, z_ref, o_ref):
        da = _dot_nt(dh_ref[...].astype(BF16), w_ref[...])
        g = z_ref[:, :F].astype(F32)
        u = z_ref[:, F:].astype(F32)
        sg = _sigmoid(g)
        o_ref[:, :F] = (da * u * (sg * (1.0 + g * (1.0 - sg)))).astype(BF16)
        o_ref[:, F:] = (da * (g * sg)).astype(BF16)

    return pl.pallas_call(
        body, name=name, grid=(S // tm,),
        in_specs=[pl.BlockSpec((tm, N), lambda i: (i, 0)),
                  pl.BlockSpec((F, N), lambda i: (0, 0)),
                  pl.BlockSpec((tm, 2 * F), lambda i: (i, 0))],
        out_specs=pl.BlockSpec((tm, 2 * F), lambda i: (i, 0)),
        out_shape=jax.ShapeDtypeStruct((S, 2 * F), BF16),
        compiler_params=_params(("parallel",)))(dh, w2, z)


def _mm_tn(kind, xs, dy, J, n, tn, name):
    tpn = n // tn
    ts = 512
    S, K = xs[0].shape
    x_specs = [pl.BlockSpec((ts, K), lambda c, s: (s, 0))]
    if kind == "norm":
        x_specs.append(pl.BlockSpec((1, K), lambda c, s: (0, 0)))
    nx = len(xs)
    if dy.ndim == 3:
        dy_spec = pl.BlockSpec((None, ts, tn), lambda c, s: (c // tpn, s, c % tpn))
    else:
        dy_spec = pl.BlockSpec((ts, tn), lambda c, s: (s, c))

    def body(*refs):
        x_refs, dy_ref, o_ref = refs[:nx], refs[nx], refs[nx + 1]
        if kind == "norm":
            v = x_refs[0][...]
            xb = (v * _rstd(v) * x_refs[1][...]).astype(BF16)
        else:
            xb = x_refs[0][...].astype(BF16)

        @pl.when(pl.program_id(1) == 0)
        def _():
            o_ref[...] = jnp.zeros_like(o_ref)

        o_ref[...] += _dot_tn(xb, dy_ref[...].astype(BF16))

    return pl.pallas_call(
        body, name=name, grid=(J * tpn, S // ts),
        in_specs=x_specs + [dy_spec],
        out_specs=pl.BlockSpec((None, K, tn), lambda c, s: (c // tpn, 0, c % tpn)),
        out_shape=jax.ShapeDtypeStruct((J, K, n), F32),
        compiler_params=_params(("parallel", "arbitrary")))(*xs, dy)


def _loss_head(h, gain, target, tm=512):
    S, K = h.shape

    def body(h_ref, g_ref, t_ref, dh_ref, loss_ref, dg_ref):
        v = h_ref[...]
        r = _rstd(v)
        xh = v * r
        g = g_ref[...]
        dy = (xh * g - t_ref[...]) * (1.0 / K)
        dyg = dy * g
        dh_ref[...] = r * (dyg - xh * jnp.mean(dyg * xh, axis=-1, keepdims=True))

        @pl.when(pl.program_id(0) == 0)
        def _():
            loss_ref[...] = jnp.zeros_like(loss_ref)
            dg_ref[...] = jnp.zeros_like(dg_ref)

        part = jnp.sum(jnp.sum(dy * dy, axis=-1, keepdims=True), axis=0, keepdims=True) * (0.5 * K)
        lane = lax.broadcasted_iota(jnp.int32, loss_ref.shape, 1)
        loss_ref[...] += jnp.where(lane == 0, part, 0.0)
        dg_ref[...] += jnp.sum(dy * xh, axis=0, keepdims=True)

    row = pl.BlockSpec((tm, K), lambda i: (i, 0))
    vec = pl.BlockSpec((1, K), lambda i: (0, 0))
    return pl.pallas_call(
        body, name="loss_head", grid=(S // tm,),
        in_specs=[row, vec, row],
        out_specs=[row, pl.BlockSpec((1, HEAD), lambda i: (0, 0)), vec],
        out_shape=[jax.ShapeDtypeStruct((S, K), F32), jax.ShapeDtypeStruct((1, HEAD), F32),
                   jax.ShapeDtypeStruct((1, K), F32)],
        compiler_params=_params(("arbitrary",)))(h, gain, target)


def _lower_bound(lg_ref):
    l0, l1, l2 = lg_ref[0:1, :], lg_ref[1:2, :], lg_ref[2:3, :]
    mx = jnp.maximum(jnp.maximum(l0, l1), l2)
    e0, e1, e2 = jnp.exp(l0 - mx), jnp.exp(l1 - mx), jnp.exp(l2 - mx)
    return e0 / (e0 + e1 + e2)


def _chunks(v, ncb):
    C = HGRN_CHUNK
    return [v[c * C:(c + 1) * C] for c in range(ncb)]


def _rows(parts):
    return jnp.concatenate(parts, axis=0)


def _block_gates(qz, fz, lb, ncb):
    C = HGRN_CHUNK
    row = lax.broadcasted_iota(jnp.int32, (C, C), 0)
    col = lax.broadcasted_iota(jnp.int32, (C, C), 1)
    tri = (col <= row).astype(F32)
    first_half = lax.broadcasted_iota(jnp.int32, (C, HEAD), 0) < C // 2
    sig = _sigmoid(fz)
    fg = lb + (1.0 - lb) * sig
    key = 1.0 - fg
    lg = jnp.log(fg)
    lgs = _chunks(lg, ncb)
    b = _rows([_dot_exact(tri, v) for v in lgs])
    r_c = [jnp.sum(jnp.where(first_half, v, 0.0), axis=0, keepdims=True) for v in lgs]
    bl_c = [jnp.sum(v, axis=0, keepdims=True) for v in lgs]
    r = _rows([jnp.broadcast_to(v, (C, HEAD)) for v in r_c])
    bl = _rows([jnp.broadcast_to(v, (C, HEAD)) for v in bl_c])
    sq = _sigmoid(qz)
    qy = qz * sq
    return sig, fg, key, b, r, bl, bl_c, sq, qy


def _hgrn_fwd(proj, logits, gain, tb=1024, rider=None):
    S = proj.shape[0]
    H, C = HGRN_HEADS, HGRN_CHUNK
    ncb = tb // C

    def body(q_ref, f_ref, i_ref, g_ref, lg_ref, gn_ref, o_ref, og_ref, st_ref, state):
        @pl.when(pl.program_id(1) == 0)
        def _():
            state[...] = jnp.zeros_like(state)

        lb = _lower_bound(lg_ref)
        causal = lax.broadcasted_iota(jnp.int32, (C, C), 1) <= lax.broadcasted_iota(jnp.int32, (C, C), 0)
        qz, fz, gz = q_ref[...], f_ref[...], g_ref[...]
        _, _, key, b, r, bl, bl_c, _, qy = _block_gates(qz, fz, lb, ncb)
        qs = _chunks((qy * jnp.exp(b - r)).astype(BF16), ncb)
        ks = _chunks((key * jnp.exp(r - b)).astype(BF16), ncb)
        qb = _chunks((qy * jnp.exp(b)).astype(BF16), ncb)
        ke = _chunks((key * jnp.exp(bl - b)).astype(BF16), ncb)
        vb = _chunks(i_ref[...].astype(BF16), ncb)
        o_intra, upd = [], []
        for c in range(ncb):
            a = jnp.where(causal, _dot_nt(qs[c], ks[c]), 0.0).astype(BF16)
            o_intra.append(_dot(a, vb[c]))
            upd.append(_dot_tn(vb[c], ke[c]))
        st = state[...]
        for c in range(ncb):
            st_ref[c] = st
            st = st * jnp.exp(bl_c[c]) + upd[c]
        state[...] = st
        o = _rows([_dot_nt(qb[c], st_ref[c].astype(BF16)) + o_intra[c] for c in range(ncb)])
        o_ref[...] = o
        og_ref[...] = ((o * _rstd(o) * gn_ref[...]) * (gz * _sigmoid(gz))).astype(BF16)

    def part(p):
        return pl.BlockSpec((tb, HEAD), functools.partial(lambda h, i, p: (i, p * H + h), p=p))

    nb = S // tb
    r_ops, r_in, r_out, r_shape, r_scr = _rider_args(rider)
    res = pl.pallas_call(
        _ride(rider, body, 6, 3, functools.partial(_grid_corner, 0, 0), functools.partial(_grid_corner, H - 1, nb - 1)),
        name="hgrn_fwd", grid=(H, nb),
        in_specs=[part(0), part(1), part(2), part(3),
                  pl.BlockSpec((3, HEAD), lambda h, i: (0, h)),
                  pl.BlockSpec((1, HEAD), lambda h, i: (0, 0))] + r_in,
        out_specs=[pl.BlockSpec((tb, HEAD), lambda h, i: (i, h)),
                   pl.BlockSpec((tb, HEAD), lambda h, i: (i, h)),
                   pl.BlockSpec((None, ncb, HEAD, HEAD), lambda h, i: (h, i, 0, 0))] + r_out,
        out_shape=[jax.ShapeDtypeStruct((S, H * HEAD), F32),
                   jax.ShapeDtypeStruct((S, H * HEAD), BF16),
                   jax.ShapeDtypeStruct((H, S // C, HEAD, HEAD), F32)] + r_shape,
        scratch_shapes=[pltpu.VMEM((HEAD, HEAD), F32)] + r_scr,
        compiler_params=_params(("arbitrary", "arbitrary")))(proj, proj, proj, proj, logits, gain, *r_ops)
    return res[:3], res[3:]


def _hgrn_bwd(proj, logits, gain, o, states, dog, tb=1024, rider=None):
    S = proj.shape[0]
    H, C = HGRN_HEADS, HGRN_CHUNK
    ncb = tb // C
    nb = S // tb

    def body(q_ref, f_ref, i_ref, g_ref, lg_ref, gn_ref, o_ref, st_ref, dog_ref,
             dp_ref, dlb_ref, dgn_ref, dstate, dst_scr):
        @pl.when(pl.program_id(1) == 0)
        def _():
            dstate[...] = jnp.zeros_like(dstate)
            dlb_ref[...] = jnp.zeros_like(dlb_ref)
            dgn_ref[...] = jnp.zeros_like(dgn_ref)

        lb = _lower_bound(lg_ref)
        oml = 1.0 - lb
        gn = gn_ref[...]
        row = lax.broadcasted_iota(jnp.int32, (C, C), 0)
        col = lax.broadcasted_iota(jnp.int32, (C, C), 1)
        causal = col <= row
        tri_up = (col >= row).astype(F32)
        qz, fz, gz = q_ref[...], f_ref[...], g_ref[...]
        sig, fg, key, b, r, bl, bl_c, sq, qy = _block_gates(qz, fz, lb, ncb)
        e_br, e_rb, e_b, e_lb = jnp.exp(b - r), jnp.exp(r - b), jnp.exp(b), jnp.exp(bl - b)
        qs_v, ks_v = (qy * e_br).astype(BF16), (key * e_rb).astype(BF16)
        qb_v, ke_v = (qy * e_b).astype(BF16), (key * e_lb).astype(BF16)
        qs, ks, qb, ke = _chunks(qs_v, ncb), _chunks(ks_v, ncb), _chunks(qb_v, ncb), _chunks(ke_v, ncb)
        vb = _chunks(i_ref[...].astype(BF16), ncb)
        ov = o_ref[...]
        rs = _rstd(ov)
        xh = ov * rs
        sg = _sigmoid(gz)
        dog_v = dog_ref[...]
        dgz = dog_v * (xh * gn) * (sg * (1.0 + gz * (1.0 - sg)))
        don = dog_v * (gz * sg)
        dgn_ref[...] += jnp.sum(don * xh, axis=0, keepdims=True)
        dyg = don * gn
        do = rs * (dyg - xh * jnp.mean(dyg * xh, axis=-1, keepdims=True))
        dob = _chunks(do.astype(BF16), ncb)
        dv_in, dqs, dks, wst = [], [], [], []
        for c in range(ncb):
            a = jnp.where(causal, _dot_nt(qs[c], ks[c]), 0.0).astype(BF16)
            da = jnp.where(causal, _dot_nt(dob[c], vb[c]), 0.0).astype(BF16)
            dv_in.append(_dot_tn(a, dob[c]))
            dqs.append(_dot(da, ks[c]))
            dks.append(_dot_tn(da, qs[c]))
            wst.append(_dot_tn(dob[c], qb[c]))
        e_l = [jnp.exp(v) for v in bl_c]
        dst = dstate[...]
        for c in reversed(range(ncb)):
            dst_scr[c] = dst
            dst = wst[c] + dst * e_l[c]
        dstate[...] = dst
        dv, dqb, dke, dbl_st = [], [], [], []
        for c in range(ncb):
            dst1 = dst_scr[c]
            st0 = st_ref[c]
            dst1b = dst1.astype(BF16)
            dv.append(dv_in[c] + _dot_nt(ke[c], dst1b))
            dqb.append(_dot(dob[c], st0.astype(BF16)))
            dke.append(_dot(vb[c], dst1b))
            dbl_st.append(jnp.sum(dst1 * st0, axis=0, keepdims=True) * e_l[c])
        dqs, dks, dqb, dke, dv = _rows(dqs), _rows(dks), _rows(dqb), _rows(dke), _rows(dv)
        dke_ke = dke * ke_v.astype(F32)
        db = dqs * qs_v.astype(F32) - dks * ks_v.astype(F32) + dqb * qb_v.astype(F32) - dke_ke
        dlg = []
        for c, (db_c, kk_c) in enumerate(zip(_chunks(db, ncb), _chunks(dke_ke, ncb))):
            dbl = jnp.sum(kk_c, axis=0, keepdims=True) + dbl_st[c]
            dlg.append(_dot_exact(tri_up, db_c) + dbl)
        dlg = _rows(dlg)
        dkey = dks * e_rb + dke * e_lb
        dqy = dqs * e_br + dqb * e_b
        dfg = dlg / fg - dkey
        dlb_ref[...] += jnp.sum(dfg * (1.0 - sig), axis=0, keepdims=True)
        dp_ref[0] = (dqy * (sq * (1.0 + qz * (1.0 - sq)))).astype(BF16)
        dp_ref[1] = (dfg * oml * sig * (1.0 - sig)).astype(BF16)
        dp_ref[2] = dv.astype(BF16)
        dp_ref[3] = dgz.astype(BF16)

    def part(p):
        return pl.BlockSpec((tb, HEAD), functools.partial(lambda h, i, p: (nb - 1 - i, p * H + h), p=p))

    blk = pl.BlockSpec((tb, HEAD), lambda h, i: (nb - 1 - i, h))
    acc = pl.BlockSpec((None, 1, HEAD), lambda h, i: (h, 0, 0))
    r_ops, r_in, r_out, r_shape, r_scr = _rider_args(rider)
    res = pl.pallas_call(
        _ride(rider, body, 9, 3, functools.partial(_grid_corner, 0, 0), functools.partial(_grid_corner, H - 1, nb - 1)),
        name="hgrn_bwd", grid=(H, nb),
        in_specs=[part(0), part(1), part(2), part(3),
                  pl.BlockSpec((3, HEAD), lambda h, i: (0, h)),
                  pl.BlockSpec((1, HEAD), lambda h, i: (0, 0)),
                  blk,
                  pl.BlockSpec((None, ncb, HEAD, HEAD), lambda h, i: (h, nb - 1 - i, 0, 0)),
                  blk] + r_in,
        out_specs=[pl.BlockSpec((4, tb, HEAD), lambda h, i: (0, nb - 1 - i, h)), acc, acc] + r_out,
        out_shape=[jax.ShapeDtypeStruct((4, S, H * HEAD), BF16),
                   jax.ShapeDtypeStruct((H, 1, HEAD), F32),
                   jax.ShapeDtypeStruct((H, 1, HEAD), F32)] + r_shape,
        scratch_shapes=[pltpu.VMEM((HEAD, HEAD), F32), pltpu.VMEM((ncb, HEAD, HEAD), F32)] + r_scr,
        compiler_params=_params(("arbitrary", "arbitrary")))(
            proj, proj, proj, proj, logits, gain, o, states, dog, *r_ops)
    return res[:3], res[3:]


def _rope(v, cos, sin):
    return v * cos + pltpu.roll(v, HEAD // 2, 1) * sin


def _band_masks():
    qi = lax.broadcasted_iota(jnp.int32, (ATTN_SPAN, ATTN_SPAN), 0)
    kj = lax.broadcasted_iota(jnp.int32, (ATTN_SPAN, ATTN_SPAN), 1)
    return kj <= qi, kj >= qi


def _attn_fwd(a):
    d, L, _ = a.shape
    nb = L // ATTN_SPAN
    scale = HEAD ** -0.5

    def body(q_ref, kc_ref, kp_ref, vc_ref, vp_ref, o_ref, lse_ref):
        n = pl.program_id(1)
        mask_c, mask_p0 = _band_masks()
        mask_p = jnp.logical_and(mask_p0, n > 0)
        for hh in range(HEADS_PER_GROUP):
            cols = slice(hh * HEAD, (hh + 1) * HEAD)
            q, kc, kp = q_ref[:, cols], kc_ref[:, cols], kp_ref[:, cols]
            s_c = jnp.where(mask_c, _dot_nt(q, kc) * scale, NEG)
            s_p = jnp.where(mask_p, _dot_nt(q, kp) * scale, NEG)
            m = jnp.maximum(jnp.max(s_c, axis=-1, keepdims=True), jnp.max(s_p, axis=-1, keepdims=True))
            p_c = jnp.exp(s_c - m)
            p_p = jnp.exp(s_p - m)
            l = jnp.sum(p_c, axis=-1, keepdims=True) + jnp.sum(p_p, axis=-1, keepdims=True)
            acc = _dot(p_c.astype(BF16), vc_ref[:, cols]) + _dot(p_p.astype(BF16), vp_ref[:, cols])
            o_ref[:, cols] = acc / l
            lse_ref[:, cols] = jnp.broadcast_to(m + jnp.log(l), (ATTN_SPAN, HEAD))

    def blk(part, prev):
        if prev:
            return pl.BlockSpec((None, ATTN_SPAN, GROUP_W), functools.partial(lambda r, n, p: (r, jnp.maximum(n - 1, 0), p), p=part))
        return pl.BlockSpec((None, ATTN_SPAN, GROUP_W), functools.partial(lambda r, n, p: (r, n, p), p=part))

    out = pl.BlockSpec((None, ATTN_SPAN, GROUP_W), lambda r, n: (r, n, 0))
    return pl.pallas_call(
        body, name=f"attn_fwd_d{d}", grid=(d, nb),
        in_specs=[blk(0, False), blk(1, False), blk(1, True), blk(2, False), blk(2, True)],
        out_specs=[out, out],
        out_shape=[jax.ShapeDtypeStruct((d, L, GROUP_W), F32), jax.ShapeDtypeStruct((d, L, GROUP_W), F32)],
        compiler_params=_params(("parallel", "arbitrary")))(a, a, a, a, a)


def _attn_bwd(a, do, lse, dd):
    d, L, _ = a.shape
    nb = L // ATTN_SPAN
    scale = HEAD ** -0.5

    def body(qc_ref, qn_ref, kp_ref, kc_ref, vp_ref, vc_ref, doc_ref, don_ref, lc_ref, ln_ref, ddc_ref, ddn_ref, da_ref):
        n = pl.program_id(1)
        mask_c, mask_p0 = _band_masks()
        mask_p = jnp.logical_and(mask_p0, n > 0)
        mask_n = jnp.logical_and(mask_p0, n < nb - 1)
        for hh in range(HEADS_PER_GROUP):
            cols = slice(hh * HEAD, (hh + 1) * HEAD)
            q, qn, kc, kp = qc_ref[:, cols], qn_ref[:, cols], kc_ref[:, cols], kp_ref[:, cols]
            vc, vp = vc_ref[:, cols], vp_ref[:, cols]
            do_c = doc_ref[:, cols].astype(BF16)
            do_n = don_ref[:, cols].astype(BF16)
            lse_c, lse_n = lc_ref[:, cols], ln_ref[:, cols]
            dd_c, dd_n = ddc_ref[:, cols], ddn_ref[:, cols]
            p_c = jnp.where(mask_c, jnp.exp(_dot_nt(q, kc) * scale - lse_c), 0.0)
            p_p = jnp.where(mask_p, jnp.exp(_dot_nt(q, kp) * scale - lse_c), 0.0)
            ds_c = (p_c * (_dot_nt(do_c, vc) + dd_c)).astype(BF16)
            ds_p = (p_p * (_dot_nt(do_c, vp) + dd_c)).astype(BF16)
            dq = (_dot(ds_c, kc) + _dot(ds_p, kp)) * scale
            p_n = jnp.where(mask_n, jnp.exp(_dot_nt(qn, kc) * scale - lse_n), 0.0)
            ds_n = (p_n * (_dot_nt(do_n, vc) + dd_n)).astype(BF16)
            dk = (_dot_tn(ds_c, q) + _dot_tn(ds_n, qn)) * scale
            dv = _dot_tn(p_c.astype(BF16), do_c) + _dot_tn(p_n.astype(BF16), do_n)
            da_ref[:, cols] = dq
            da_ref[:, GROUP_W + hh * HEAD:GROUP_W + (hh + 1) * HEAD] = dk
            da_ref[:, 2 * GROUP_W + hh * HEAD:2 * GROUP_W + (hh + 1) * HEAD] = dv

    def rel(delta):
        if delta < 0:
            return lambda n: jnp.maximum(n - 1, 0)
        if delta > 0:
            return lambda n: jnp.minimum(n + 1, nb - 1)
        return lambda n: n

    def blk(width, part, delta):
        f = rel(delta)
        return pl.BlockSpec((None, ATTN_SPAN, width), functools.partial(lambda r, n, p, f: (r, f(n), p), p=part, f=f))

    g = GROUP_W
    return pl.pallas_call(
        body, name=f"attn_bwd_d{d}", grid=(d, nb),
        in_specs=[blk(g, 0, 0), blk(g, 0, 1), blk(g, 1, -1), blk(g, 1, 0), blk(g, 2, -1), blk(g, 2, 0),
                  blk(g, 0, 0), blk(g, 0, 1), blk(g, 0, 0), blk(g, 0, 1), blk(g, 0, 0), blk(g, 0, 1)],
        out_specs=pl.BlockSpec((None, ATTN_SPAN, 3 * g), lambda r, n: (r, n, 0)),
        out_shape=jax.ShapeDtypeStruct((d, L, 3 * g), F32),
        compiler_params=_params(("parallel", "arbitrary")))(
            a, a, a, a, a, a, do, do, lse, lse, dd, dd)


def _group_weights(lse_refs, cols):
    ls = [r[:, cols] for r in lse_refs]
    mx = jnp.maximum(jnp.maximum(ls[0], ls[1]), ls[2])
    es = [jnp.exp(v - mx) for v in ls]
    tot = es[0] + es[1] + es[2]
    return [e / tot for e in es]


def _gather_tokens(ref, scr, d, tm):
    if d == 1:
        return ref.at[0]
    for r in range(d):
        scr[pl.ds(r, tm // d, stride=d), :] = ref[r]
    return scr


def _scatter_tokens(scr, ref, d, tm):
    if d == 1:
        ref[0] = scr[...]
        return
    for r in range(d):
        ref[r] = scr[pl.ds(r, tm // d, stride=d), :]


def _head_spec(d, tm):
    return pl.BlockSpec((d, tm // d, HEAD), lambda i, j: (0, i, j))


def _dilate_group(qkv, cos, sin, g, d, tm=1024):
    S = qkv.shape[0]
    G = len(ATTN_GROUPS)

    def body(x_ref, cos_ref, sin_ref, out_ref):
        rotate = pl.program_id(1) < 2 * HEADS_PER_GROUP
        for r in range(d):
            rows = slice(None) if d == 1 else pl.ds(r, tm // d, stride=d)
            v = x_ref[rows, :]
            out_ref[r] = jnp.where(rotate, _rope(v, cos_ref[rows, :], sin_ref[rows, :]), v).astype(BF16)

    def src(i, j):
        return i, ((j // HEADS_PER_GROUP) * G + g) * HEADS_PER_GROUP + j % HEADS_PER_GROUP

    tab = pl.BlockSpec((tm, HEAD), lambda i, j: (i, 0))
    return pl.pallas_call(
        body, name=f"attn_dilate_d{d}", grid=(S // tm, 3 * HEADS_PER_GROUP),
        in_specs=[pl.BlockSpec((tm, HEAD), src), tab, tab],
        out_specs=_head_spec(d, tm),
        out_shape=jax.ShapeDtypeStruct((d, S // d, 3 * GROUP_W), BF16),
        compiler_params=_params(("parallel", "arbitrary")))(qkv, cos, sin)


def _undilate_group(da, dqkv, cos, sin, g, tm=1024):
    d, L, _ = da.shape
    S = d * L
    G = len(ATTN_GROUPS)

    def body(*refs):
        da_ref, cos_ref, sin_ref, out_ref, scr = refs[0], refs[1], refs[2], refs[-2], refs[-1]
        tok = _gather_tokens(da_ref, scr, d, tm)[...]
        rotate = pl.program_id(1) < 2 * HEADS_PER_GROUP
        out_ref[...] = jnp.where(rotate, _rope(tok, cos_ref[...], -sin_ref[...]), tok).astype(BF16)

    def dst(i, j):
        return i, ((j // HEADS_PER_GROUP) * G + g) * HEADS_PER_GROUP + j % HEADS_PER_GROUP

    tab = pl.BlockSpec((tm, HEAD), lambda i, j: (i, 0))
    operands = (da, cos, sin) if dqkv is None else (da, cos, sin, dqkv)
    return pl.pallas_call(
        body, name=f"attn_undilate_d{d}", grid=(S // tm, 3 * HEADS_PER_GROUP),
        in_specs=[_head_spec(d, tm), tab, tab] + ([] if dqkv is None else [ANY]),
        out_specs=pl.BlockSpec((tm, HEAD), dst),
        out_shape=jax.ShapeDtypeStruct((S, 3 * G * GROUP_W), BF16),
        input_output_aliases={} if dqkv is None else {3: 0},
        scratch_shapes=[pltpu.VMEM((tm, HEAD), F32)],
        compiler_params=_params(("parallel", "arbitrary")))(*operands)


def _attn_merge(os_, lses, tm=512):
    G = len(os_)
    S = os_[0].shape[0] * os_[0].shape[1]

    def body(*refs):
        o_refs, l_refs, out_ref = refs[:G], refs[G:2 * G], refs[2 * G]
        scr = refs[2 * G + 1:]
        hh = pl.program_id(1)
        o_tok = [_gather_tokens(o_refs[g], scr[g], d, tm) for g, (_, d) in enumerate(ATTN_GROUPS)]
        l_tok = [_gather_tokens(l_refs[g], scr[G + g], d, tm) for g, (_, d) in enumerate(ATTN_GROUPS)]
        al = _group_weights(l_tok, slice(None))
        for g in range(G):
            cols = pl.ds(pl.multiple_of(g * GROUP_W + hh * HEAD, HEAD), HEAD)
            out_ref[:, cols] = (o_tok[g][...] * al[g]).astype(BF16)

    specs = [_head_spec(d, tm) for _, d in ATTN_GROUPS]
    return pl.pallas_call(
        body, name="attn_merge", grid=(S // tm, HEADS_PER_GROUP),
        in_specs=specs + specs,
        out_specs=pl.BlockSpec((tm, G * GROUP_W), lambda i, j: (i, 0)),
        out_shape=jax.ShapeDtypeStruct((S, G * GROUP_W), BF16),
        scratch_shapes=[pltpu.VMEM((tm, HEAD), F32)] * (2 * G),
        compiler_params=_params(("parallel", "arbitrary")))(*os_, *lses)


def _attn_merge_bwd(os_, lses, doa, tm=512):
    G = len(os_)
    S = doa.shape[0]

    def body(*refs):
        o_refs, l_refs, doa_ref = refs[:G], refs[G:2 * G], refs[2 * G]
        do_refs, dd_refs = refs[2 * G + 1:3 * G + 1], refs[3 * G + 1:4 * G + 1]
        scr = refs[4 * G + 1:]
        hh = pl.program_id(1)
        o_tok = [_gather_tokens(o_refs[g], scr[g], d, tm) for g, (_, d) in enumerate(ATTN_GROUPS)]
        l_tok = [_gather_tokens(l_refs[g], scr[G + g], d, tm) for g, (_, d) in enumerate(ATTN_GROUPS)]
        do_tok, dd_tok = scr[2 * G:3 * G], scr[3 * G:]
        al = _group_weights(l_tok, slice(None))
        mix = None
        for g in range(G):
            dg = doa_ref[:, pl.ds(pl.multiple_of(g * GROUP_W + hh * HEAD, HEAD), HEAD)]
            do_tok[g][...] = dg * al[g]
            t = al[g] * jnp.sum(dg * o_tok[g][...], axis=-1, keepdims=True)
            mix = t if mix is None else mix + t
        for g, (_, d) in enumerate(ATTN_GROUPS):
            dd_tok[g][...] = jnp.broadcast_to(-al[g] * mix, (tm, HEAD))
            _scatter_tokens(do_tok[g], do_refs[g], d, tm)
            _scatter_tokens(dd_tok[g], dd_refs[g], d, tm)

    specs = [_head_spec(d, tm) for _, d in ATTN_GROUPS]
    shapes = [jax.ShapeDtypeStruct((d, S // d, GROUP_W), F32) for _, d in ATTN_GROUPS]
    return pl.pallas_call(
        body, name="attn_merge_bwd", grid=(S // tm, HEADS_PER_GROUP),
        in_specs=specs + specs + [pl.BlockSpec((tm, G * GROUP_W), lambda i, j: (i, 0))],
        out_specs=specs + specs,
        out_shape=shapes + shapes,
        scratch_shapes=[pltpu.VMEM((tm, HEAD), F32)] * (4 * G),
        compiler_params=_params(("parallel", "arbitrary")))(*os_, *lses, doa)


def _rope_tables(S):
    inv_freq = 1.0 / (ROPE_THETA ** (jnp.arange(0, HEAD, 2, dtype=F32) / HEAD))
    ang = jnp.arange(S, dtype=F32)[:, None] * inv_freq[None, :]
    cos, sin = jnp.cos(ang), jnp.sin(ang)
    return jnp.concatenate([cos, cos], axis=-1), jnp.concatenate([-sin, sin], axis=-1)


def _local_step(x, target, norm_mix, norm_ffn, lb_logits, out_gain, final_norm, comm):
    S = x.shape[0]
    nm0, nm1 = norm_mix[0:1], norm_mix[1:2]
    nf0, nf1 = norm_ffn[0:1], norm_ffn[1:2]
    w = comm.first_weights()

    rider = comm.gather_rider(LATE_WEIGHTS_A)
    res = _norm_mm(x, nm0, w["hin"], "hgrn_in", rider=rider)
    proj, got = (res, ()) if rider is None else res
    w.update(comm.gathered(LATE_WEIGHTS_A, got))
    (o, og, states), got = _hgrn_fwd(proj, lb_logits, out_gain, rider=comm.gather_rider(LATE_WEIGHTS_B))
    w.update(comm.gathered(LATE_WEIGHTS_B, got))
    fin_tn = w["fin0"].shape[2]
    h1 = _mm_res(x, og, w["hout"], "hgrn_out")
    z0 = _norm_mm(h1, nf0, w["fin0"], "ffn0_in", out_dtype=BF16)
    h2, act0 = _swiglu_mm_res(h1, z0, w["fdn0"], "ffn0_down")
    qkv = _norm_mm(h2, nm1, w["qkv"], "attn_qkv")
    cos, sin = _rope_tables(S)
    a_g = [_dilate_group(qkv, cos, sin, gi, d) for gi, (_, d) in enumerate(ATTN_GROUPS)]
    o_g, lse_g = zip(*[_attn_fwd(a) for a in a_g])
    oa = _attn_merge(o_g, lse_g)
    h3 = _mm_res(h2, oa, w["aout"], "attn_out")
    z1 = _norm_mm(h3, nf1, w["fin1"], "ffn1_in", out_dtype=BF16)
    h4, act1 = _swiglu_mm_res(h3, z1, w["fdn1"], "ffn1_down")
    dh4, loss, d_final = _loss_head(h4, final_norm, target)

    grads, small = {}, {"final_norm": d_final}

    def ffn_bwd(dh, h_in, z, act, gain, w_in, w_dn, tag):
        dz = _mm_nt_swiglu_bwd(dh, w_dn, z, tag + "_down_dx")
        g_dn = _mm_tn("plain", (act,), dh, 1, D_MODEL, D_MODEL, tag + "_down_dw")
        g_in = _mm_tn("norm", (h_in, gain), dz, N_CHIPS, fin_tn, fin_tn, tag + "_in_dw")
        dh_in, dgain = _mm_nt_normbwd(dz, w_in, h_in, gain, dh, tag + "_in_dx")
        return dh_in, dgain, g_in, g_dn[0]

    dh3, d_nf1, grads["fin1"], grads["fdn1"] = ffn_bwd(dh4, h3, z1, act1, nf1, w["fin1"], w["fdn1"], "ffn1")
    doa = _mm_nt(dh3, w["aout"][None], "attn_out_dx")
    grads["aout"] = _mm_tn("plain", (oa,), dh3, 1, D_MODEL, 512, "attn_out_dw")[0]
    merged = _attn_merge_bwd(o_g, lse_g, doa)
    G = len(ATTN_GROUPS)
    das = [_attn_bwd(a_g[gi], merged[gi], lse_g[gi], merged[G + gi]) for gi in range(G)]
    dqkv = None
    for gi in range(G):
        dqkv = _undilate_group(das[gi], dqkv, cos, sin, gi)
    n_qkv = w["qkv"].shape[2]
    grads["qkv"] = _mm_tn("norm", (h2, nm1), dqkv, N_CHIPS, n_qkv, n_qkv, "attn_qkv_dw")
    dh2, d_nm1 = _mm_nt_normbwd(dqkv, w["qkv"], h2, nm1, dh3, "attn_qkv_dx")

    dh1, d_nf0, grads["fin0"], grads["fdn0"] = ffn_bwd(dh2, h1, z0, act0, nf0, w["fin0"], w["fdn0"], "ffn0")
    dog = _mm_nt(dh1, w["hout"][None], "hgrn_out_dx")
    rider = comm.exchange_rider({k: grads.pop(k) for k in EARLY_GRADS})
    (dproj, dlb, dgn), got = _hgrn_bwd(proj, lb_logits, out_gain, o, states, dog, rider=rider)
    comm.exchanged(got)
    grads["hout"] = _mm_tn("plain", (og,), dh1, 1, D_MODEL, 512, "hgrn_out_dw")[0]
    grads["hin"] = _mm_tn("norm", (x, nm0), dproj, N_CHIPS, D_MODEL, D_MODEL, "hgrn_in_dw")
    dx, d_nm0 = _mm_nt_normbwd(dproj, w["hin"], x, nm0, dh1, "hgrn_in_dx")

    small["norm_mix"] = jnp.concatenate([d_nm0, d_nm1], axis=0)
    small["norm_ffn"] = jnp.concatenate([d_nf0, d_nf1], axis=0)
    small["lb"] = dlb.reshape(1, HGRN_HEADS * HEAD)
    small["out_norm"] = dgn.reshape(HGRN_HEADS, HEAD)
    return loss, dx, grads, small


def _place():
    x, y, c = lax.axis_index("x"), lax.axis_index("y"), lax.axis_index("c")
    others = [(1 - x, y), (x, 1 - y), (1 - x, 1 - y)]
    return x, y, c, others


ANY = pl.BlockSpec(memory_space=pl.ANY)


class _GatherRider:
    def __init__(self, shards):
        self.operands = list(shards)
        n = self.n = len(shards)
        self.out_shape = [jax.ShapeDtypeStruct((N_CHIPS,) + s.shape, s.dtype) for s in shards]
        self.scratch = [pltpu.SemaphoreType.DMA((3 * n,)), pltpu.SemaphoreType.DMA((3 * n,)),
                        pltpu.SemaphoreType.DMA((3 * n,)), pltpu.SemaphoreType.DMA((3 * n,)),
                        pltpu.SemaphoreType.DMA((n,)), pltpu.SemaphoreType.DMA((n,))]

    def _copies(self, ins, outs, sems):
        ici_send, ici_recv, _, _, own_send, own_recv = sems
        x, y, c, others = _place()
        me = 2 * x + y
        own = [pltpu.make_async_remote_copy(
            src_ref=ins[a], dst_ref=outs[a].at[me], send_sem=own_send.at[a], recv_sem=own_recv.at[a],
            device_id=(x, y, 1 - c), device_id_type=MESH) for a in range(self.n)]
        sends = [pltpu.make_async_remote_copy(
            src_ref=ins[a].at[c], dst_ref=outs[a].at[me, c], send_sem=ici_send.at[a * 3 + k], recv_sem=ici_recv.at[a * 3 + k],
            device_id=(ox, oy, c), device_id_type=MESH) for a in range(self.n) for k, (ox, oy) in enumerate(others)]
        return own, sends

    def start(self, ins, outs, sems):
        own, sends = self._copies(ins, outs, sems)
        for cp in own + sends:
            cp.start()

    def finish(self, ins, outs, sems):
        ici_send, ici_recv, d2d_send, d2d_recv, _, _ = sems
        x, y, c, others = _place()
        sibling = (x, y, 1 - c)
        own, sends = self._copies(ins, outs, sems)
        passes = []
        for a in range(self.n):
            for k, (ox, oy) in enumerate(others):
                s = a * 3 + k
                got = outs[a].at[2 * ox + oy, c]
                pltpu.make_async_remote_copy(
                    src_ref=got, dst_ref=got, send_sem=ici_send.at[s], recv_sem=ici_recv.at[s],
                    device_id=(ox, oy, c), device_id_type=MESH).wait_recv()
                fwd = pltpu.make_async_remote_copy(
                    src_ref=got, dst_ref=got, send_sem=d2d_send.at[s], recv_sem=d2d_recv.at[s],
                    device_id=sibling, device_id_type=MESH)
                fwd.start()
                passes.append(fwd)
        for a in range(self.n):
            for k, (ox, oy) in enumerate(others):
                s = a * 3 + k
                theirs = outs[a].at[2 * ox + oy, 1 - c]
                pltpu.make_async_remote_copy(
                    src_ref=theirs, dst_ref=theirs, send_sem=d2d_send.at[s], recv_sem=d2d_recv.at[s],
                    device_id=sibling, device_id_type=MESH).wait_recv()
        for cp in own:
            cp.wait()
        for cp in sends + passes:
            cp.wait_send()


class _ExchangeRider:
    def __init__(self, parts):
        self.operands = list(parts)
        n = self.n = len(parts)
        self.out_shape = [jax.ShapeDtypeStruct(p.shape, p.dtype) for p in parts]
        self.scratch = [pltpu.SemaphoreType.DMA((3 * n,)), pltpu.SemaphoreType.DMA((3 * n,))]

    def _copies(self, ins, outs, sems):
        send_sem, recv_sem = sems
        x, y, c, others = _place()
        me = 2 * x + y
        return [pltpu.make_async_remote_copy(
            src_ref=ins[a].at[2 * ox + oy], dst_ref=outs[a].at[me], send_sem=send_sem.at[a * 3 + k],
            recv_sem=recv_sem.at[a * 3 + k], device_id=(ox, oy, c), device_id_type=MESH)
            for a in range(self.n) for k, (ox, oy) in enumerate(others)]

    def start(self, ins, outs, sems):
        for cp in self._copies(ins, outs, sems):
            cp.start()

    def finish(self, ins, outs, sems):
        send_sem, recv_sem = sems
        x, y, c, others = _place()
        for a in range(self.n):
            for k, (ox, oy) in enumerate(others):
                s = a * 3 + k
                got = outs[a].at[2 * ox + oy]
                pltpu.make_async_remote_copy(
                    src_ref=got, dst_ref=got, send_sem=send_sem.at[s], recv_sem=recv_sem.at[s],
                    device_id=(ox, oy, c), device_id_type=MESH).wait_recv()
        for cp in self._copies(ins, outs, sems):
            cp.wait_send()


def _run_rider(rider, name):
    n = rider.n

    def body(*refs):
        ins, outs, sems = refs[:n], refs[n:2 * n], refs[2 * n:]
        rider.start(ins, outs, sems)
        rider.finish(ins, outs, sems)

    return pl.pallas_call(
        body, name=name, in_specs=[ANY] * n, out_specs=[ANY] * n,
        out_shape=rider.out_shape, scratch_shapes=rider.scratch)(*rider.operands)


def _ride(rider, body, n_in, n_out, first, last):
    if rider is None:
        return body
    n = rider.n

    def wrapped(*refs):
        host_in, r_in = refs[:n_in], refs[n_in:n_in + n]
        host_out = refs[n_in + n:n_in + n + n_out]
        r_out = refs[n_in + n + n_out:n_in + 2 * n + n_out]
        rest = refs[n_in + 2 * n + n_out:]
        host_scr, sems = rest[:len(rest) - len(rider.scratch)], rest[len(rest) - len(rider.scratch):]

        @pl.when(first())
        def _():
            rider.start(r_in, r_out, sems)

        body(*host_in, *host_out, *host_scr)

        @pl.when(last())
        def _():
            rider.finish(r_in, r_out, sems)

    return wrapped


def _rider_args(rider):
    if rider is None:
        return [], [], [], [], []
    return rider.operands, [ANY] * rider.n, [ANY] * rider.n, rider.out_shape, rider.scratch


def _pair_exchange(grads, name):
    n = len(grads)

    def body(*refs):
        ins, outs = refs[:n], refs[n:2 * n]
        send_sem, recv_sem = refs[2 * n:]
        x, y, c, _ = _place()
        sibling = (x, y, 1 - c)
        cps = []
        for a in range(n):
            for j in range(N_CHIPS):
                s = a * N_CHIPS + j
                cps.append(pltpu.make_async_remote_copy(
                    src_ref=ins[a].at[j, 1 - c], dst_ref=outs[a].at[j], send_sem=send_sem.at[s], recv_sem=recv_sem.at[s],
                    device_id=sibling, device_id_type=MESH))
        for cp in cps:
            cp.start()
        for cp in cps:
            cp.wait()

    return pl.pallas_call(
        body, name=name,
        in_specs=[ANY] * n, out_specs=[ANY] * n,
        out_shape=[jax.ShapeDtypeStruct((N_CHIPS,) + g.shape[2:], F32) for g in grads],
        scratch_shapes=[pltpu.SemaphoreType.DMA((N_CHIPS * n,)), pltpu.SemaphoreType.DMA((N_CHIPS * n,))],
        )(*grads)


def _pair_sum(g, got, c_idx):
    _, _, r, cw = g.shape
    tr = _row_tile(r, cw)

    def body(c_ref, g_ref, got_ref, p_ref, pb_ref):
        v = g_ref[...] + got_ref[...]
        p_ref[...] = v
        pb_ref[...] = v.astype(BF16)

    blk = pl.BlockSpec((None, tr, cw), lambda j, i, c_ref: (j, i, 0))
    return pl.pallas_call(
        body, name="grad_pair_sum",
        grid_spec=pltpu.PrefetchScalarGridSpec(
            num_scalar_prefetch=1, grid=(N_CHIPS, r // tr),
            in_specs=[pl.BlockSpec((None, None, tr, cw), lambda j, i, c_ref: (j, c_ref[0], i, 0)), blk],
            out_specs=[blk, blk]),
        out_shape=[jax.ShapeDtypeStruct((N_CHIPS, r, cw), F32), jax.ShapeDtypeStruct((N_CHIPS, r, cw), BF16)],
        compiler_params=_params(("parallel", "parallel")))(c_idx, g, got)


def _chip_sum(p, got, me_idx):
    _, r, cw = p.shape
    tr = _row_tile(r, cw)

    def body(me_ref, own_ref, got_ref, t_ref):
        me = me_ref[0]
        acc = None
        for s in range(N_CHIPS):
            term = jnp.where(me == s, own_ref[...], got_ref[s].astype(F32))
            acc = term if acc is None else acc + term
        t_ref[...] = acc

    return pl.pallas_call(
        body, name="grad_chip_sum",
        grid_spec=pltpu.PrefetchScalarGridSpec(
            num_scalar_prefetch=1, grid=(r // tr,),
            in_specs=[pl.BlockSpec((None, tr, cw), lambda i, me_ref: (me_ref[0], i, 0)),
                      pl.BlockSpec((N_CHIPS, tr, cw), lambda i, me_ref: (0, i, 0))],
            out_specs=pl.BlockSpec((tr, cw), lambda i, me_ref: (i, 0))),
        out_shape=jax.ShapeDtypeStruct((r, cw), F32),
        compiler_params=_params(("parallel",)))(me_idx, p, got)


def _pair_share(halves):
    n = len(halves)

    def body(*refs):
        ins, outs = refs[:n], refs[n:2 * n]
        send_sem, recv_sem = refs[2 * n:]
        x, y, c, _ = _place()
        cps = [pltpu.make_async_remote_copy(
            src_ref=ins[a], dst_ref=outs[a], send_sem=send_sem.at[a], recv_sem=recv_sem.at[a],
            device_id=(x, y, 1 - c), device_id_type=MESH) for a in range(n)]
        for cp in cps:
            cp.start()
        for cp in cps:
            cp.wait()

    return pl.pallas_call(
        body, name="grad_pair_share",
        in_specs=[ANY] * n, out_specs=[ANY] * n,
        out_shape=[jax.ShapeDtypeStruct(h.shape, F32) for h in halves],
        scratch_shapes=[pltpu.SemaphoreType.DMA((n,)), pltpu.SemaphoreType.DMA((n,))],
        )(*halves)


def _small_allreduce(pack):
    m_per, ncol = pack.shape
    n_dev = 8

    def body(x_ref, sum_ref, all_ref, send_sems, recv_sems, local_sem):
        x, y, c, others = _place()
        me, sibling = (x, y, c), (x, y, 1 - c)

        def rows(px, py, pc):
            return all_ref.at[pl.ds((4 * px + 2 * py + pc) * m_per, m_per), :]

        def copy(k, block, to, src=None):
            return pltpu.make_async_remote_copy(
                src_ref=rows(*block) if src is None else src, dst_ref=rows(*block),
                send_sem=send_sems.at[k], recv_sem=recv_sems.at[k], device_id=to, device_id_type=MESH)

        mine = pltpu.make_async_copy(x_ref, rows(*me), local_sem)
        mine.start()
        first = [copy(0, me, sibling, src=x_ref)]
        first += [copy(1 + j, me, (*chip, c), src=x_ref) for j, chip in enumerate(others)]
        for cp in first:
            cp.start()
        passed = [copy(4 + j, (*chip, c), sibling) for j, chip in enumerate(others)]
        for j, chip in enumerate(others):
            copy(1 + j, (*chip, c), me).wait_recv()
            passed[j].start()
        copy(0, sibling, me).wait_recv()
        for j, chip in enumerate(others):
            copy(4 + j, (*chip, 1 - c), me).wait_recv()
        for cp in first + passed:
            cp.wait_send()
        mine.wait()
        acc = all_ref[0:m_per, :]
        for dvc in range(1, n_dev):
            acc = acc + all_ref[dvc * m_per:(dvc + 1) * m_per, :]
        sum_ref[...] = acc

    return pl.pallas_call(
        body, name="small_allreduce",
        in_specs=[pl.BlockSpec(memory_space=pltpu.VMEM)],
        out_specs=pl.BlockSpec(memory_space=pltpu.VMEM),
        out_shape=jax.ShapeDtypeStruct((m_per, ncol), F32),
        scratch_shapes=[pltpu.VMEM((n_dev * m_per, ncol), F32),
                        pltpu.SemaphoreType.DMA((7,)), pltpu.SemaphoreType.DMA((7,)), pltpu.SemaphoreType.DMA],
        )(pack)


def _adam_math(w, g, m, v):
    m = ADAM_B1 * m + (1.0 - ADAM_B1) * g
    v = ADAM_B2 * v + (1.0 - ADAM_B2) * (g * g)
    m_hat = m / (1.0 - ADAM_B1 ** ADAM_STEP)
    v_hat = v / (1.0 - ADAM_B2 ** ADAM_STEP)
    delta = -ADAM_LR * (m_hat / (jnp.sqrt(v_hat) + ADAM_EPS) + ADAM_WD * w)
    return delta, m, v


def _adamw(mine, theirs, c_idx, w, m, v, name):
    r, C = mine.shape
    tr = _row_tile(r, C, 512 * 1024)
    nt = r // tr

    def body(c_ref, mine_ref, theirs_ref, w_ref, m_ref, v_ref, g_ref, d_ref, nm_ref, nv_ref):
        g = jnp.where(pl.program_id(0) == c_ref[0], mine_ref[...], theirs_ref[...])
        g_ref[...] = g
        d_ref[...], nm_ref[...], nv_ref[...] = _adam_math(w_ref[...], g, m_ref[...], v_ref[...])

    half = pl.BlockSpec((tr, C), lambda h, i, c_ref: (i, 0))
    full = pl.BlockSpec((tr, C), lambda h, i, c_ref: (h * nt + i, 0))
    shp = jax.ShapeDtypeStruct((2 * r, C), F32)
    return pl.pallas_call(
        body, name=name,
        grid_spec=pltpu.PrefetchScalarGridSpec(
            num_scalar_prefetch=1, grid=(2, nt),
            in_specs=[half, half, full, full, full], out_specs=[full] * 4),
        out_shape=[shp] * 4,
        compiler_params=_params(("parallel", "parallel")))(c_idx, mine, theirs, w, m, v)


def _small_update(gsum, logits_pack, w, m, v):
    def body(gs_ref, lg_ref, w_ref, m_ref, v_ref, g_ref, d_ref, nm_ref, nv_ref):
        g_ref[...] = gs_ref[...]
        l0, l1, l2 = lg_ref[0:1, :], lg_ref[1:2, :], lg_ref[2:3, :]
        mx = jnp.maximum(jnp.maximum(l0, l1), l2)
        e0, e1, e2 = jnp.exp(l0 - mx), jnp.exp(l1 - mx), jnp.exp(l2 - mx)
        tot = e0 + e1 + e2
        p0, p1, p2 = e0 / tot, e1 / tot, e2 / tot
        dlb = gs_ref[4:5, :]
        g_ref[4:5, :] = dlb * p0 * (1.0 - p0)
        g_ref[5:6, :] = -dlb * p0 * p1
        g_ref[6:7, :] = -dlb * p0 * p2
        d_ref[...], nm_ref[...], nv_ref[...] = _adam_math(w_ref[...], g_ref[...], m_ref[...], v_ref[...])

    full = pl.BlockSpec(memory_space=pltpu.VMEM)
    shp = jax.ShapeDtypeStruct(gsum.shape, F32)
    return pl.pallas_call(
        body, name="small_update", in_specs=[full] * 5, out_specs=[full] * 4, out_shape=[shp] * 4)(
            gsum, logits_pack, w, m, v)


def _pack_small(norm_mix, norm_ffn, lb3, out_norm, final_norm, extra=None):
    ncol = norm_mix.shape[1]
    on = jnp.pad(out_norm.reshape(1, -1), ((0, 0), (0, ncol - out_norm.size)))
    rows = [norm_mix, norm_ffn, lb3, on, final_norm.reshape(1, ncol)]
    if extra is not None:
        rows.append(extra)
    used = sum(r.shape[0] for r in rows)
    rows.append(jnp.zeros((SMALL_ROWS - used, ncol), F32))
    return jnp.concatenate(rows, axis=0)


WEIGHT_NAMES = ("hin", "hout", "qkv", "aout", "fin0", "fin1", "fdn0", "fdn1")
FIRST_WEIGHTS = ("hin", "hout")
LATE_WEIGHTS_A = ("fin0", "fdn0")
LATE_WEIGHTS_B = ("qkv", "aout", "fin1", "fdn1")
EARLY_GRADS = ("qkv", "aout", "fin0", "fin1", "fdn0", "fdn1")


def _split_weights(hgrn_w_in, hgrn_w_out, attn_w_qkv, attn_w_out, ffn_w_in, ffn_w_down):
    return {"hin": hgrn_w_in[0], "hout": hgrn_w_out[0], "qkv": attn_w_qkv[0], "aout": attn_w_out[0],
            "fin0": ffn_w_in[0], "fin1": ffn_w_in[1], "fdn0": ffn_w_down[0], "fdn1": ffn_w_down[1]}


def _halves(v):
    r, c = v.shape
    return v.reshape(2, r // 2, c)


def _full_weights(gathered):
    out = {}
    for k, g in gathered.items():
        _, _, r, c = g.shape
        if k in ("hin", "qkv", "fin0", "fin1"):
            out[k] = g.reshape(N_CHIPS, 2 * r, c)
        else:
            out[k] = g.reshape(N_CHIPS * 2 * r, c)
    return out


class _StepComm:
    def __init__(self, shards, c_idx, me_idx):
        self.shards, self.c_idx, self.me_idx = shards, c_idx, me_idx
        self.halves = {}

    def gather_rider(self, names):
        return _GatherRider([_halves(self.shards[k].astype(BF16)) for k in names])

    def gathered(self, names, got):
        return _full_weights(dict(zip(names, got)))

    def first_weights(self):
        return self.gathered(FIRST_WEIGHTS, _run_rider(self.gather_rider(FIRST_WEIGHTS), "gather_first"))

    def _pair_sums(self, grads, name):
        names = list(grads)
        g4 = []
        for k in names:
            r, c = self.shards[k].shape
            g4.append(grads[k].reshape(N_CHIPS, 2, r // 2, c))
        from_sibling = _pair_exchange(g4, name)
        return names, [_pair_sum(g, got, self.c_idx) for g, got in zip(g4, from_sibling)]

    def _chip_sums(self, names, sums, got):
        for k, s, g in zip(names, sums, got):
            self.halves[k] = _chip_sum(s[0], g, self.me_idx)

    def exchange_rider(self, grads):
        self._riding = self._pair_sums(grads, "grad_pair_exchange_early")
        return _ExchangeRider([s[1] for s in self._riding[1]])

    def exchanged(self, got):
        self._chip_sums(*self._riding, got)

    def reduce_rest(self, grads):
        names, sums = self._pair_sums(grads, "grad_pair_exchange_late")
        self._chip_sums(names, sums, _run_rider(_ExchangeRider([s[1] for s in sums]), "grad_chip_exchange_late"))

    def shared_halves(self):
        mine = [self.halves[k] for k in WEIGHT_NAMES]
        return dict(zip(WEIGHT_NAMES, zip(mine, _pair_share(mine))))


def kernel(x, norm_mix, norm_ffn, hgrn_w_in, hgrn_lb_logits, hgrn_out_norm, hgrn_w_out, attn_w_qkv, attn_w_out, ffn_w_in, ffn_w_down, final_norm, loss_target, m_norm_mix, m_norm_ffn, m_hgrn_w_in, m_hgrn_lb_logits, m_hgrn_out_norm, m_hgrn_w_out, m_attn_w_qkv, m_attn_w_out, m_ffn_w_in, m_ffn_w_down, m_final_norm, v_norm_mix, v_norm_ffn, v_hgrn_w_in, v_hgrn_lb_logits, v_hgrn_out_norm, v_hgrn_w_out, v_attn_w_qkv, v_attn_w_out, v_ffn_w_in, v_ffn_w_down, v_final_norm):
    S = x.shape[1]
    xi, yi, ci = lax.axis_index("x"), lax.axis_index("y"), lax.axis_index("c")
    c_idx = jnp.reshape(ci, (1,)).astype(jnp.int32)
    me_idx = jnp.reshape(2 * xi + yi, (1,)).astype(jnp.int32)

    w_own = _split_weights(hgrn_w_in, hgrn_w_out, attn_w_qkv, attn_w_out, ffn_w_in, ffn_w_down)
    m_own = _split_weights(m_hgrn_w_in, m_hgrn_w_out, m_attn_w_qkv, m_attn_w_out, m_ffn_w_in, m_ffn_w_down)
    v_own = _split_weights(v_hgrn_w_in, v_hgrn_w_out, v_attn_w_qkv, v_attn_w_out, v_ffn_w_in, v_ffn_w_down)

    comm = _StepComm(w_own, c_idx, me_idx)
    loss, dx, grads, small = _local_step(
        x.reshape(S, D_MODEL), loss_target.reshape(S, D_MODEL), norm_mix, norm_ffn, hgrn_lb_logits,
        hgrn_out_norm, final_norm.reshape(1, D_MODEL), comm)
    comm.reduce_rest(grads)

    g_out, d_out, m_out, v_out = {}, {}, {}, {}
    for k, (mine, theirs) in comm.shared_halves().items():
        g_out[k], d_out[k], m_out[k], v_out[k] = _adamw(mine, theirs, c_idx, w_own[k], m_own[k], v_own[k], "adamw_" + k)

    loss_row = jnp.pad(loss, ((0, 0), (0, D_MODEL - loss.shape[1])))
    lb3 = jnp.concatenate([small["lb"], jnp.zeros((2, D_MODEL), F32)], axis=0)
    on_grad = jnp.sum(small["out_norm"], axis=0, keepdims=True)
    pack = _pack_small(small["norm_mix"], small["norm_ffn"], lb3, on_grad, small["final_norm"], loss_row)
    gsum = _small_allreduce(pack)
    w_s = _pack_small(norm_mix, norm_ffn, hgrn_lb_logits, hgrn_out_norm, final_norm)
    m_s = _pack_small(m_norm_mix, m_norm_ffn, m_hgrn_lb_logits, m_hgrn_out_norm, m_final_norm)
    v_s = _pack_small(v_norm_mix, v_norm_ffn, v_hgrn_lb_logits, v_hgrn_out_norm, v_final_norm)
    lg_pack = jnp.pad(hgrn_lb_logits, ((0, 8 - hgrn_lb_logits.shape[0]), (0, 0)))
    sg, sd, sm, sv = _small_update(gsum, lg_pack, w_s, m_s, v_s)

    def unpack(p):
        return (p[0:2], p[2:4], p[4:7], p[7:8, :HEAD], p[8])

    def big(dct):
        return (dct["hin"][None], dct["hout"][None], dct["qkv"][None], dct["aout"][None],
                jnp.stack([dct["fin0"], dct["fin1"]]), jnp.stack([dct["fdn0"], dct["fdn1"]]))

    def assemble(p, dct):
        nmx, nff, lbl, onm, fnm = unpack(p)
        hin, hout, qkv, aout, fin, fdn = big(dct)
        return (nmx, nff, hin, lbl, onm, hout, qkv, aout, fin, fdn, fnm)

    total_loss = gsum[9, 0]
    return (total_loss, dx.reshape(1, S, D_MODEL), *assemble(sg, g_out), *assemble(sd, d_out),
            *assemble(sm, m_out), *assemble(sv, v_out))
```

```python
import functools

import jax
import jax.numpy as jnp
from jax import lax
from jax.experimental import pallas as pl
from jax.experimental.pallas import tpu as pltpu

F32 = jnp.float32
BF16 = jnp.bfloat16
MESH = pl.DeviceIdType.MESH

D_MODEL = 1024
HEAD = 128
HGRN_HEADS = 8
HGRN_CHUNK = 64
ATTN_GROUPS = ((128, 1), (512, 4), (2048, 16))
ATTN_SPAN = 128
HEADS_PER_GROUP = 4
GROUP_W = HEADS_PER_GROUP * HEAD
D_FF = 2816
NORM_EPS = 1e-6
ROPE_THETA = 10000.0
NEG = -1e30

ADAM_LR, ADAM_B1, ADAM_B2, ADAM_EPS, ADAM_WD, ADAM_STEP = 0.001, 0.9, 0.999, 1e-08, 0.01, 10

N_CHIPS = 4
VMEM_LIMIT = 56 * 1024 * 1024
SMALL_ROWS = 16


def _params(sem=None):
    return pltpu.CompilerParams(dimension_semantics=sem, vmem_limit_bytes=VMEM_LIMIT)


def _row_tile(rows, cols, budget_bytes=3 * 512 * 1024):
    best = 8
    for t in range(8, rows + 1, 8):
        if rows % t == 0 and t * cols * 4 <= budget_bytes:
            best = t
    assert rows % best == 0
    return best


def _grid_corner(i, j):
    return jnp.logical_and(pl.program_id(0) == i, pl.program_id(1) == j)


def _sigmoid(v):
    return 1.0 / (1.0 + jnp.exp(-v))


def _dot(a, b):
    return jnp.dot(a, b, preferred_element_type=F32)


def _dot_nt(a, b):
    return lax.dot_general(a, b, (((1,), (1,)), ((), ())), preferred_element_type=F32)


def _dot_tn(a, b):
    return lax.dot_general(a, b, (((0,), (0,)), ((), ())), preferred_element_type=F32)


def _dot_exact(ones, b):
    ones = ones.astype(BF16)
    hi = b.astype(BF16)
    rest = b - hi.astype(F32)
    mid = rest.astype(BF16)
    low = (rest - mid.astype(F32)).astype(BF16)
    return _dot(ones, hi) + _dot(ones, mid) + _dot(ones, low)


def _rstd(v):
    return lax.rsqrt(jnp.mean(v * v, axis=-1, keepdims=True) + NORM_EPS)


def _norm_mm(h, gain, w3, name, out_dtype=F32, tm=512, rider=None):
    S, K = h.shape
    J, _, n = w3.shape
    gi = S // tm

    def body(h_ref, g_ref, w_ref, y_ref, u_ref):
        @pl.when(pl.program_id(1) == 0)
        def _():
            v = h_ref[...]
            u_ref[...] = (v * _rstd(v) * g_ref[...]).astype(BF16)

        y_ref[...] = _dot(u_ref[...], w_ref[pl.program_id(1)]).astype(y_ref.dtype)

    r_ops, r_in, r_out, r_shape, r_scr = _rider_args(rider)
    res = pl.pallas_call(
        _ride(rider, body, 3, 2, functools.partial(_grid_corner, 0, 0), functools.partial(_grid_corner, gi - 1, J - 1)),
        name=name, grid=(gi, J),
        in_specs=[pl.BlockSpec((tm, K), lambda i, j: (i, 0)),
                  pl.BlockSpec((1, K), lambda i, j: (0, 0)),
                  pl.BlockSpec((J, K, n), lambda i, j: (0, 0, 0))] + r_in,
        out_specs=[pl.BlockSpec((tm, n), lambda i, j: (i, j)), pl.BlockSpec((tm, K), lambda i, j: (i, 0))] + r_out,
        out_shape=[jax.ShapeDtypeStruct((S, J * n), out_dtype), jax.ShapeDtypeStruct((S, K), BF16)] + r_shape,
        scratch_shapes=r_scr,
        compiler_params=_params(("arbitrary", "arbitrary")))(h, gain, w3, *r_ops)
    return res[0], res[1], res[2:]


def _mm_res(h, a, w2, name, tm=512):
    S, N = h.shape
    K = a.shape[1]

    def body(h_ref, a_ref, w_ref, o_ref):
        o_ref[...] = h_ref[...] + _dot(a_ref[...], w_ref[...])

    return pl.pallas_call(
        body, name=name, grid=(S // tm,),
        in_specs=[pl.BlockSpec((tm, N), lambda i: (i, 0)),
                  pl.BlockSpec((tm, K), lambda i: (i, 0)),
                  pl.BlockSpec((K, N), lambda i: (0, 0))],
        out_specs=pl.BlockSpec((tm, N), lambda i: (i, 0)),
        out_shape=jax.ShapeDtypeStruct((S, N), F32),
        compiler_params=_params(("parallel",)))(h, a, w2)


def _swiglu(z_ref, F):
    g = z_ref[:, :F].astype(F32)
    return (g * _sigmoid(g) * z_ref[:, F:].astype(F32)).astype(BF16)


def _swiglu_mm_res(h, z, w2, name, tm=256):
    S, N = h.shape
    F = w2.shape[0]

    def body(h_ref, z_ref, w_ref, o_ref, a_ref):
        a = _swiglu(z_ref, F)
        a_ref[...] = a
        o_ref[...] = h_ref[...] + _dot(a, w_ref[...])

    return pl.pallas_call(
        body, name=name, grid=(S // tm,),
        in_specs=[pl.BlockSpec((tm, N), lambda i: (i, 0)),
                  pl.BlockSpec((tm, 2 * F), lambda i: (i, 0)),
                  pl.BlockSpec((F, N), lambda i: (0, 0))],
        out_specs=[pl.BlockSpec((tm, N), lambda i: (i, 0)), pl.BlockSpec((tm, F), lambda i: (i, 0))],
        out_shape=[jax.ShapeDtypeStruct((S, N), F32), jax.ShapeDtypeStruct((S, F), BF16)],
        compiler_params=_params(("parallel",)))(h, z, w2)


def _dy_specs(dy, J, n, tm):
    if dy.ndim == 3:
        return [pl.BlockSpec((None, tm, n), functools.partial(lambda i, j: (j, i, 0), j=j)) for j in range(J)]
    return [pl.BlockSpec((tm, n), functools.partial(lambda i, j: (i, j), j=j)) for j in range(J)]


def _acc_nt(dy_refs, w_ref):
    acc = None
    for j, r in enumerate(dy_refs):
        t = _dot_nt(r[...].astype(BF16), w_ref[j])
        acc = t if acc is None else acc + t
    return acc


def _mm_nt(dy, w3, name, out_dtype=F32, tm=512):
    J, K, n = w3.shape
    S = dy.shape[-2]

    def body(*refs):
        dy_refs, w_ref, o_ref = refs[:J], refs[J], refs[J + 1]
        o_ref[...] = _acc_nt(dy_refs, w_ref).astype(o_ref.dtype)

    return pl.pallas_call(
        body, name=name, grid=(S // tm,),
        in_specs=_dy_specs(dy, J, n, tm) + [pl.BlockSpec((J, K, n), lambda i: (0, 0, 0))],
        out_specs=pl.BlockSpec((tm, K), lambda i: (i, 0)),
        out_shape=jax.ShapeDtypeStruct((S, K), out_dtype),
        compiler_params=_params(("parallel",)))(*([dy] * J), w3)


def _mm_nt_normbwd(dy, w3, h, gain, dh, name, tm=512):
    J, K, n = w3.shape
    S = h.shape[0]

    def body(*refs):
        dy_refs, w_ref, h_ref, g_ref, dh_ref, o_ref, dg_ref = refs[:J], *refs[J:]
        du = _acc_nt(dy_refs, w_ref)
        v = h_ref[...]
        r = _rstd(v)
        xh = v * r
        dyg = du * g_ref[...]
        o_ref[...] = dh_ref[...] + r * (dyg - xh * jnp.mean(dyg * xh, axis=-1, keepdims=True))

        @pl.when(pl.program_id(0) == 0)
        def _():
            dg_ref[...] = jnp.zeros_like(dg_ref)

        dg_ref[...] += jnp.sum(du * xh, axis=0, keepdims=True)

    row = pl.BlockSpec((tm, K), lambda i: (i, 0))
    vec = pl.BlockSpec((1, K), lambda i: (0, 0))
    return pl.pallas_call(
        body, name=name, grid=(S // tm,),
        in_specs=_dy_specs(dy, J, n, tm) + [pl.BlockSpec((J, K, n), lambda i: (0, 0, 0)), row, vec, row],
        out_specs=[row, vec],
        out_shape=[jax.ShapeDtypeStruct((S, K), F32), jax.ShapeDtypeStruct((1, K), F32)],
        compiler_params=_params(("arbitrary",)))(*([dy] * J), w3, h, gain, dh)


def _mm_nt_swiglu_bwd(dh, w2, z, name, tm=256):
    F, N = w2.shape
    S = dh.shape[0]

    def body(dh_ref, w_ref, z_ref, o_ref):
        da = _dot_nt(dh_ref[...].astype(BF16), w_ref[...])
        g = z_ref[:, :F].astype(F32)
        u = z_ref[:, F:].astype(F32)
        sg = _sigmoid(g)
        o_ref[:, :F] = (da * u * (sg * (1.0 + g * (1.0 - sg)))).astype(BF16)
        o_ref[:, F:] = (da * (g * sg)).astype(BF16)

    return pl.pallas_call(
        body, name=name, grid=(S // tm,),
        in_specs=[pl.BlockSpec((tm, N), lambda i: (i, 0)),
                  pl.BlockSpec((F, N), lambda i: (0, 0)),
                  pl.BlockSpec((tm, 2 * F), lambda i: (i, 0))],
        out_specs=pl.BlockSpec((tm, 2 * F), lambda i: (i, 0)),
        out_shape=jax.ShapeDtypeStruct((S, 2 * F), BF16),
        compiler_params=_params(("parallel",)))(dh, w2, z)


def _mm_tn(x, dy, J, n, tn, name):
    tpn = n // tn
    ts = 512
    S, K = x.shape
    if dy.ndim == 3:
        dy_spec = pl.BlockSpec((None, ts, tn), lambda c, s: (c // tpn, s, c % tpn))
    else:
        dy_spec = pl.BlockSpec((ts, tn), lambda c, s: (s, c))

    def body(x_ref, dy_ref, o_ref):
        @pl.when(pl.program_id(1) == 0)
        def _():
            o_ref[...] = jnp.zeros_like(o_ref)

        o_ref[...] += _dot_tn(x_ref[...], dy_ref[...].astype(BF16))

    return pl.pallas_call(
        body, name=name, grid=(J * tpn, S // ts),
        in_specs=[pl.BlockSpec((ts, K), lambda c, s: (s, 0)), dy_spec],
        out_specs=pl.BlockSpec((None, K, tn), lambda c, s: (c // tpn, 0, c % tpn)),
        out_shape=jax.ShapeDtypeStruct((J, K, n), F32),
        compiler_params=_params(("parallel", "arbitrary")))(x, dy)


def _loss_head(h, gain, target, tm=512):
    S, K = h.shape

    def body(h_ref, g_ref, t_ref, dh_ref, loss_ref, dg_ref):
        v = h_ref[...]
        r = _rstd(v)
        xh = v * r
        g = g_ref[...]
        dy = (xh * g - t_ref[...]) * (1.0 / K)
        dyg = dy * g
        dh_ref[...] = r * (dyg - xh * jnp.mean(dyg * xh, axis=-1, keepdims=True))

        @pl.when(pl.program_id(0) == 0)
        def _():
            loss_ref[...] = jnp.zeros_like(loss_ref)
            dg_ref[...] = jnp.zeros_like(dg_ref)

        part = jnp.sum(jnp.sum(dy * dy, axis=-1, keepdims=True), axis=0, keepdims=True) * (0.5 * K)
        lane = lax.broadcasted_iota(jnp.int32, loss_ref.shape, 1)
        loss_ref[...] += jnp.where(lane == 0, part, 0.0)
        dg_ref[...] += jnp.sum(dy * xh, axis=0, keepdims=True)

    row = pl.BlockSpec((tm, K), lambda i: (i, 0))
    vec = pl.BlockSpec((1, K), lambda i: (0, 0))
    return pl.pallas_call(
        body, name="loss_head", grid=(S // tm,),
        in_specs=[row, vec, row],
        out_specs=[row, pl.BlockSpec((1, HEAD), lambda i: (0, 0)), vec],
        out_shape=[jax.ShapeDtypeStruct((S, K), F32), jax.ShapeDtypeStruct((1, HEAD), F32),
                   jax.ShapeDtypeStruct((1, K), F32)],
        compiler_params=_params(("arbitrary",)))(h, gain, target)


def _lower_bound(lg_ref):
    l0, l1, l2 = lg_ref[0:1, :], lg_ref[1:2, :], lg_ref[2:3, :]
    mx = jnp.maximum(jnp.maximum(l0, l1), l2)
    e0, e1, e2 = jnp.exp(l0 - mx), jnp.exp(l1 - mx), jnp.exp(l2 - mx)
    return e0 / (e0 + e1 + e2)


def _chunks(v, ncb):
    C = HGRN_CHUNK
    return [v[c * C:(c + 1) * C] for c in range(ncb)]


def _rows(parts):
    return jnp.concatenate(parts, axis=0)


def _block_gates(qz, fz, lb, ncb):
    C = HGRN_CHUNK
    row = lax.broadcasted_iota(jnp.int32, (C, C), 0)
    col = lax.broadcasted_iota(jnp.int32, (C, C), 1)
    tri = (col <= row).astype(F32)
    first_half = lax.broadcasted_iota(jnp.int32, (C, HEAD), 0) < C // 2
    sig = _sigmoid(fz)
    fg = lb + (1.0 - lb) * sig
    key = 1.0 - fg
    lg = jnp.log(fg)
    lgs = _chunks(lg, ncb)
    b = _rows([_dot_exact(tri, v) for v in lgs])
    r_c = [jnp.sum(jnp.where(first_half, v, 0.0), axis=0, keepdims=True) for v in lgs]
    bl_c = [jnp.sum(v, axis=0, keepdims=True) for v in lgs]
    r = _rows([jnp.broadcast_to(v, (C, HEAD)) for v in r_c])
    bl = _rows([jnp.broadcast_to(v, (C, HEAD)) for v in bl_c])
    sq = _sigmoid(qz)
    qy = qz * sq
    return sig, fg, key, b, r, bl, bl_c, sq, qy


def _hgrn_fwd(proj, logits, gain, tb=1024, rider=None):
    S = proj.shape[0]
    H, C = HGRN_HEADS, HGRN_CHUNK
    ncb = tb // C

    def body(q_ref, f_ref, i_ref, g_ref, lg_ref, gn_ref, o_ref, og_ref, st_ref, state):
        @pl.when(pl.program_id(1) == 0)
        def _():
            state[...] = jnp.zeros_like(state)

        lb = _lower_bound(lg_ref)
        causal = lax.broadcasted_iota(jnp.int32, (C, C), 1) <= lax.broadcasted_iota(jnp.int32, (C, C), 0)
        qz, fz, gz = q_ref[...], f_ref[...], g_ref[...]
        _, _, key, b, r, bl, bl_c, _, qy = _block_gates(qz, fz, lb, ncb)
        qs = _chunks((qy * jnp.exp(b - r)).astype(BF16), ncb)
        ks = _chunks((key * jnp.exp(r - b)).astype(BF16), ncb)
        qb = _chunks((qy * jnp.exp(b)).astype(BF16), ncb)
        ke = _chunks((key * jnp.exp(bl - b)).astype(BF16), ncb)
        vb = _chunks(i_ref[...].astype(BF16), ncb)
        o_intra, upd = [], []
        for c in range(ncb):
            a = jnp.where(causal, _dot_nt(qs[c], ks[c]), 0.0).astype(BF16)
            o_intra.append(_dot(a, vb[c]))
            upd.append(_dot_tn(vb[c], ke[c]))
        st = state[...]
        for c in range(ncb):
            st_ref[c] = st
            st = st * jnp.exp(bl_c[c]) + upd[c]
        state[...] = st
        o = _rows([_dot_nt(qb[c], st_ref[c].astype(BF16)) + o_intra[c] for c in range(ncb)])
        o_ref[...] = o
        og_ref[...] = ((o * _rstd(o) * gn_ref[...]) * (gz * _sigmoid(gz))).astype(BF16)

    def part(p):
        return pl.BlockSpec((tb, HEAD), functools.partial(lambda h, i, p: (i, p * H + h), p=p))

    nb = S // tb
    r_ops, r_in, r_out, r_shape, r_scr = _rider_args(rider)
    res = pl.pallas_call(
        _ride(rider, body, 6, 3, functools.partial(_grid_corner, 0, 0), functools.partial(_grid_corner, H - 1, nb - 1)),
        name="hgrn_fwd", grid=(H, nb),
        in_specs=[part(0), part(1), part(2), part(3),
                  pl.BlockSpec((3, HEAD), lambda h, i: (0, h)),
                  pl.BlockSpec((1, HEAD), lambda h, i: (0, 0))] + r_in,
        out_specs=[pl.BlockSpec((tb, HEAD), lambda h, i: (i, h)),
                   pl.BlockSpec((tb, HEAD), lambda h, i: (i, h)),
                   pl.BlockSpec((None, ncb, HEAD, HEAD), lambda h, i: (h, i, 0, 0))] + r_out,
        out_shape=[jax.ShapeDtypeStruct((S, H * HEAD), F32),
                   jax.ShapeDtypeStruct((S, H * HEAD), BF16),
                   jax.ShapeDtypeStruct((H, S // C, HEAD, HEAD), F32)] + r_shape,
        scratch_shapes=[pltpu.VMEM((HEAD, HEAD), F32)] + r_scr,
        compiler_params=_params(("arbitrary", "arbitrary")))(proj, proj, proj, proj, logits, gain, *r_ops)
    return res[:3], res[3:]


def _hgrn_bwd(proj, logits, gain, o, states, dog, tb=1024, rider=None):
    S = proj.shape[0]
    H, C = HGRN_HEADS, HGRN_CHUNK
    ncb = tb // C
    nb = S // tb

    def body(q_ref, f_ref, i_ref, g_ref, lg_ref, gn_ref, o_ref, st_ref, dog_ref,
             dp_ref, dlb_ref, dgn_ref, dstate, dst_scr):
        @pl.when(pl.program_id(1) == 0)
        def _():
            dstate[...] = jnp.zeros_like(dstate)
            dlb_ref[...] = jnp.zeros_like(dlb_ref)
            dgn_ref[...] = jnp.zeros_like(dgn_ref)

        lb = _lower_bound(lg_ref)
        oml = 1.0 - lb
        gn = gn_ref[...]
        row = lax.broadcasted_iota(jnp.int32, (C, C), 0)
        col = lax.broadcasted_iota(jnp.int32, (C, C), 1)
        causal = col <= row
        tri_up = (col >= row).astype(F32)
        qz, fz, gz = q_ref[...], f_ref[...], g_ref[...]
        sig, fg, key, b, r, bl, bl_c, sq, qy = _block_gates(qz, fz, lb, ncb)
        e_br, e_rb, e_b, e_lb = jnp.exp(b - r), jnp.exp(r - b), jnp.exp(b), jnp.exp(bl - b)
        qs_v, ks_v = (qy * e_br).astype(BF16), (key * e_rb).astype(BF16)
        qb_v, ke_v = (qy * e_b).astype(BF16), (key * e_lb).astype(BF16)
        qs, ks, qb, ke = _chunks(qs_v, ncb), _chunks(ks_v, ncb), _chunks(qb_v, ncb), _chunks(ke_v, ncb)
        vb = _chunks(i_ref[...].astype(BF16), ncb)
        ov = o_ref[...]
        rs = _rstd(ov)
        xh = ov * rs
        sg = _sigmoid(gz)
        dog_v = dog_ref[...]
        dgz = dog_v * (xh * gn) * (sg * (1.0 + gz * (1.0 - sg)))
        don = dog_v * (gz * sg)
        dgn_ref[...] += jnp.sum(don * xh, axis=0, keepdims=True)
        dyg = don * gn
        do = rs * (dyg - xh * jnp.mean(dyg * xh, axis=-1, keepdims=True))
        dob = _chunks(do.astype(BF16), ncb)
        dv_in, dqs, dks, wst = [], [], [], []
        for c in range(ncb):
            a = jnp.where(causal, _dot_nt(qs[c], ks[c]), 0.0).astype(BF16)
            da = jnp.where(causal, _dot_nt(dob[c], vb[c]), 0.0).astype(BF16)
            dv_in.append(_dot_tn(a, dob[c]))
            dqs.append(_dot(da, ks[c]))
            dks.append(_dot_tn(da, qs[c]))
            wst.append(_dot_tn(dob[c], qb[c]))
        e_l = [jnp.exp(v) for v in bl_c]
        dst = dstate[...]
        for c in reversed(range(ncb)):
            dst_scr[c] = dst
            dst = wst[c] + dst * e_l[c]
        dstate[...] = dst
        dv, dqb, dke, dbl_st = [], [], [], []
        for c in range(ncb):
            dst1 = dst_scr[c]
            st0 = st_ref[c]
            dst1b = dst1.astype(BF16)
            dv.append(dv_in[c] + _dot_nt(ke[c], dst1b))
            dqb.append(_dot(dob[c], st0.astype(BF16)))
            dke.append(_dot(vb[c], dst1b))
            dbl_st.append(jnp.sum(dst1 * st0, axis=0, keepdims=True) * e_l[c])
        dqs, dks, dqb, dke, dv = _rows(dqs), _rows(dks), _rows(dqb), _rows(dke), _rows(dv)
        dke_ke = dke * ke_v.astype(F32)
        db = dqs * qs_v.astype(F32) - dks * ks_v.astype(F32) + dqb * qb_v.astype(F32) - dke_ke
        dlg = []
        for c, (db_c, kk_c) in enumerate(zip(_chunks(db, ncb), _chunks(dke_ke, ncb))):
            dbl = jnp.sum(kk_c, axis=0, keepdims=True) + dbl_st[c]
            dlg.append(_dot_exact(tri_up, db_c) + dbl)
        dlg = _rows(dlg)
        dkey = dks * e_rb + dke * e_lb
        dqy = dqs * e_br + dqb * e_b
        dfg = dlg / fg - dkey
        dlb_ref[...] += jnp.sum(dfg * (1.0 - sig), axis=0, keepdims=True)
        dp_ref[0] = (dqy * (sq * (1.0 + qz * (1.0 - sq)))).astype(BF16)
        dp_ref[1] = (dfg * oml * sig * (1.0 - sig)).astype(BF16)
        dp_ref[2] = dv.astype(BF16)
        dp_ref[3] = dgz.astype(BF16)

    def part(p):
        return pl.BlockSpec((tb, HEAD), functools.partial(lambda h, i, p: (nb - 1 - i, p * H + h), p=p))

    blk = pl.BlockSpec((tb, HEAD), lambda h, i: (nb - 1 - i, h))
    acc = pl.BlockSpec((None, 1, HEAD), lambda h, i: (h, 0, 0))
    r_ops, r_in, r_out, r_shape, r_scr = _rider_args(rider)
    res = pl.pallas_call(
        _ride(rider, body, 9, 3, functools.partial(_grid_corner, 0, 0), functools.partial(_grid_corner, H - 1, nb - 1)),
        name="hgrn_bwd", grid=(H, nb),
        in_specs=[part(0), part(1), part(2), part(3),
                  pl.BlockSpec((3, HEAD), lambda h, i: (0, h)),
                  pl.BlockSpec((1, HEAD), lambda h, i: (0, 0)),
                  blk,
                  pl.BlockSpec((None, ncb, HEAD, HEAD), lambda h, i: (h, nb - 1 - i, 0, 0)),
                  blk] + r_in,
        out_specs=[pl.BlockSpec((4, tb, HEAD), lambda h, i: (0, nb - 1 - i, h)), acc, acc] + r_out,
        out_shape=[jax.ShapeDtypeStruct((4, S, H * HEAD), BF16),
                   jax.ShapeDtypeStruct((H, 1, HEAD), F32),
                   jax.ShapeDtypeStruct((H, 1, HEAD), F32)] + r_shape,
        scratch_shapes=[pltpu.VMEM((HEAD, HEAD), F32), pltpu.VMEM((ncb, HEAD, HEAD), F32)] + r_scr,
        compiler_params=_params(("arbitrary", "arbitrary")))(
            proj, proj, proj, proj, logits, gain, o, states, dog, *r_ops)
    return res[:3], res[3:]


def _rope(v, cos, sin):
    return v * cos + pltpu.roll(v, HEAD // 2, 1) * sin


def _band_masks():
    qi = lax.broadcasted_iota(jnp.int32, (ATTN_SPAN, ATTN_SPAN), 0)
    kj = lax.broadcasted_iota(jnp.int32, (ATTN_SPAN, ATTN_SPAN), 1)
    return kj <= qi, kj >= qi


def _attn_fwd(a):
    d, L, _ = a.shape
    nb = L // ATTN_SPAN
    scale = HEAD ** -0.5

    def body(q_ref, kc_ref, kp_ref, vc_ref, vp_ref, o_ref, lse_ref):
        n = pl.program_id(1)
        mask_c, mask_p0 = _band_masks()
        mask_p = jnp.logical_and(mask_p0, n > 0)
        for hh in range(HEADS_PER_GROUP):
            cols = slice(hh * HEAD, (hh + 1) * HEAD)
            q, kc, kp = q_ref[:, cols], kc_ref[:, cols], kp_ref[:, cols]
            s_c = jnp.where(mask_c, _dot_nt(q, kc) * scale, NEG)
            s_p = jnp.where(mask_p, _dot_nt(q, kp) * scale, NEG)
            m = jnp.maximum(jnp.max(s_c, axis=-1, keepdims=True), jnp.max(s_p, axis=-1, keepdims=True))
            p_c = jnp.exp(s_c - m)
            p_p = jnp.exp(s_p - m)
            l = jnp.sum(p_c, axis=-1, keepdims=True) + jnp.sum(p_p, axis=-1, keepdims=True)
            acc = _dot(p_c.astype(BF16), vc_ref[:, cols]) + _dot(p_p.astype(BF16), vp_ref[:, cols])
            o_ref[:, cols] = acc / l
            lse_ref[:, cols] = jnp.broadcast_to(m + jnp.log(l), (ATTN_SPAN, HEAD))

    def blk(part, prev):
        if prev:
            return pl.BlockSpec((None, ATTN_SPAN, GROUP_W), functools.partial(lambda r, n, p: (r, jnp.maximum(n - 1, 0), p), p=part))
        return pl.BlockSpec((None, ATTN_SPAN, GROUP_W), functools.partial(lambda r, n, p: (r, n, p), p=part))

    out = pl.BlockSpec((None, ATTN_SPAN, GROUP_W), lambda r, n: (r, n, 0))
    return pl.pallas_call(
        body, name=f"attn_fwd_d{d}", grid=(d, nb),
        in_specs=[blk(0, False), blk(1, False), blk(1, True), blk(2, False), blk(2, True)],
        out_specs=[out, out],
        out_shape=[jax.ShapeDtypeStruct((d, L, GROUP_W), F32), jax.ShapeDtypeStruct((d, L, GROUP_W), F32)],
        compiler_params=_params(("parallel", "arbitrary")))(a, a, a, a, a)


def _attn_bwd(a, do, lse, dd):
    d, L, _ = a.shape
    nb = L // ATTN_SPAN
    scale = HEAD ** -0.5

    def body(qc_ref, qn_ref, kp_ref, kc_ref, vp_ref, vc_ref, doc_ref, don_ref, lc_ref, ln_ref, ddc_ref, ddn_ref, da_ref):
        n = pl.program_id(1)
        mask_c, mask_p0 = _band_masks()
        mask_p = jnp.logical_and(mask_p0, n > 0)
        mask_n = jnp.logical_and(mask_p0, n < nb - 1)
        for hh in range(HEADS_PER_GROUP):
            cols = slice(hh * HEAD, (hh + 1) * HEAD)
            q, qn, kc, kp = qc_ref[:, cols], qn_ref[:, cols], kc_ref[:, cols], kp_ref[:, cols]
            vc, vp = vc_ref[:, cols], vp_ref[:, cols]
            do_c = doc_ref[:, cols].astype(BF16)
            do_n = don_ref[:, cols].astype(BF16)
            lse_c, lse_n = lc_ref[:, cols], ln_ref[:, cols]
            dd_c, dd_n = ddc_ref[:, cols], ddn_ref[:, cols]
            p_c = jnp.where(mask_c, jnp.exp(_dot_nt(q, kc) * scale - lse_c), 0.0)
            p_p = jnp.where(mask_p, jnp.exp(_dot_nt(q, kp) * scale - lse_c), 0.0)
            ds_c = (p_c * (_dot_nt(do_c, vc) + dd_c)).astype(BF16)
            ds_p = (p_p * (_dot_nt(do_c, vp) + dd_c)).astype(BF16)
            dq = (_dot(ds_c, kc) + _dot(ds_p, kp)) * scale
            p_n = jnp.where(mask_n, jnp.exp(_dot_nt(qn, kc) * scale - lse_n), 0.0)
            ds_n = (p_n * (_dot_nt(do_n, vc) + dd_n)).astype(BF16)
            dk = (_dot_tn(ds_c, q) + _dot_tn(ds_n, qn)) * scale
            dv = _dot_tn(p_c.astype(BF16), do_c) + _dot_tn(p_n.astype(BF16), do_n)
            da_ref[:, cols] = dq
            da_ref[:, GROUP_W + hh * HEAD:GROUP_W + (hh + 1) * HEAD] = dk
            da_ref[:, 2 * GROUP_W + hh * HEAD:2 * GROUP_W + (hh + 1) * HEAD] = dv

    def rel(delta):
        if delta < 0:
            return lambda n: jnp.maximum(n - 1, 0)
        if delta > 0:
            return lambda n: jnp.minimum(n + 1, nb - 1)
        return lambda n: n

    def blk(width, part, delta):
        f = rel(delta)
        return pl.BlockSpec((None, ATTN_SPAN, width), functools.partial(lambda r, n, p, f: (r, f(n), p), p=part, f=f))

    g = GROUP_W
    return pl.pallas_call(
        body, name=f"attn_bwd_d{d}", grid=(d, nb),
        in_specs=[blk(g, 0, 0), blk(g, 0, 1), blk(g, 1, -1), blk(g, 1, 0), blk(g, 2, -1), blk(g, 2, 0),
                  blk(g, 0, 0), blk(g, 0, 1), blk(g, 0, 0), blk(g, 0, 1), blk(g, 0, 0), blk(g, 0, 1)],
        out_specs=pl.BlockSpec((None, ATTN_SPAN, 3 * g), lambda r, n: (r, n, 0)),
        out_shape=jax.ShapeDtypeStruct((d, L, 3 * g), F32),
        compiler_params=_params(("parallel", "arbitrary")))(
            a, a, a, a, a, a, do, do, lse, lse, dd, dd)


def _group_weights(lse_refs, cols):
    ls = [r[:, cols] for r in lse_refs]
    mx = jnp.maximum(jnp.maximum(ls[0], ls[1]), ls[2])
    es = [jnp.exp(v - mx) for v in ls]
    tot = es[0] + es[1] + es[2]
    return [e / tot for e in es]


def _gather_tokens(ref, scr, d, tm):
    if d == 1:
        return ref.at[0]
    for r in range(d):
        scr[pl.ds(r, tm // d, stride=d), :] = ref[r]
    return scr


def _scatter_tokens(scr, ref, d, tm):
    if d == 1:
        ref[0] = scr[...]
        return
    for r in range(d):
        ref[r] = scr[pl.ds(r, tm // d, stride=d), :]


def _head_spec(d, tm):
    return pl.BlockSpec((d, tm // d, HEAD), lambda i, j: (0, i, j))


def _qkv_dilated(h, gain, wg, cos, sin, d, tm=512):
    S, K = h.shape

    def body(h_ref, g_ref, w_ref, cos_ref, sin_ref, out_ref, u_ref, y_scr):
        p = pl.program_id(1)

        @pl.when(p == 0)
        def _():
            v = h_ref[...]
            u_ref[...] = (v * _rstd(v) * g_ref[...]).astype(BF16)

        y = _dot(u_ref[...], w_ref[...])
        for hh in range(HEADS_PER_GROUP):
            cols = slice(hh * HEAD, (hh + 1) * HEAD)
            y_scr[...] = y[:, cols]

            @pl.when(p < 2)
            def _():
                for r in range(d):
                    rows = slice(None) if d == 1 else pl.ds(r, tm // d, stride=d)
                    out_ref[r, :, cols] = _rope(y_scr[rows, :], cos_ref[rows, :], sin_ref[rows, :]).astype(BF16)

            @pl.when(p == 2)
            def _():
                for r in range(d):
                    rows = slice(None) if d == 1 else pl.ds(r, tm // d, stride=d)
                    out_ref[r, :, cols] = y_scr[rows, :].astype(BF16)

    tab = pl.BlockSpec((tm, HEAD), lambda i, p: (i, 0))
    return pl.pallas_call(
        body, name=f"attn_qkv_d{d}", grid=(S // tm, 3),
        in_specs=[pl.BlockSpec((tm, K), lambda i, p: (i, 0)),
                  pl.BlockSpec((1, K), lambda i, p: (0, 0)),
                  pl.BlockSpec((K, GROUP_W), lambda i, p: (0, p)), tab, tab],
        out_specs=[pl.BlockSpec((d, tm // d, GROUP_W), lambda i, p: (0, i, p)), pl.BlockSpec((tm, K), lambda i, p: (i, 0))],
        out_shape=[jax.ShapeDtypeStruct((d, S // d, 3 * GROUP_W), BF16), jax.ShapeDtypeStruct((S, K), BF16)],
        scratch_shapes=[pltpu.VMEM((tm, HEAD), F32)],
        compiler_params=_params(("parallel", "arbitrary")))(h, gain, wg, cos, sin)


def _undilate_group(da, dqkv, cos, sin, g, tm=1024):
    d, L, _ = da.shape
    S = d * L
    G = len(ATTN_GROUPS)

    def body(*refs):
        da_ref, cos_ref, sin_ref, out_ref, scr = refs[0], refs[1], refs[2], refs[-2], refs[-1]
        tok = _gather_tokens(da_ref, scr, d, tm)[...]
        rotate = pl.program_id(1) < 2 * HEADS_PER_GROUP
        out_ref[...] = jnp.where(rotate, _rope(tok, cos_ref[...], -sin_ref[...]), tok).astype(BF16)

    def dst(i, j):
        return i, ((j // HEADS_PER_GROUP) * G + g) * HEADS_PER_GROUP + j % HEADS_PER_GROUP

    tab = pl.BlockSpec((tm, HEAD), lambda i, j: (i, 0))
    operands = (da, cos, sin) if dqkv is None else (da, cos, sin, dqkv)
    return pl.pallas_call(
        body, name=f"attn_undilate_d{d}", grid=(S // tm, 3 * HEADS_PER_GROUP),
        in_specs=[_head_spec(d, tm), tab, tab] + ([] if dqkv is None else [ANY]),
        out_specs=pl.BlockSpec((tm, HEAD), dst),
        out_shape=jax.ShapeDtypeStruct((S, 3 * G * GROUP_W), BF16),
        input_output_aliases={} if dqkv is None else {3: 0},
        scratch_shapes=[pltpu.VMEM((tm, HEAD), F32)],
        compiler_params=_params(("parallel", "arbitrary")))(*operands)


def _attn_merge(os_, lses, tm=512):
    G = len(os_)
    S = os_[0].shape[0] * os_[0].shape[1]

    def body(*refs):
        o_refs, l_refs, out_ref = refs[:G], refs[G:2 * G], refs[2 * G]
        scr = refs[2 * G + 1:]
        hh = pl.program_id(1)
        o_tok = [_gather_tokens(o_refs[g], scr[g], d, tm) for g, (_, d) in enumerate(ATTN_GROUPS)]
        l_tok = [_gather_tokens(l_refs[g], scr[G + g], d, tm) for g, (_, d) in enumerate(ATTN_GROUPS)]
        al = _group_weights(l_tok, slice(None))
        for g in range(G):
            cols = pl.ds(pl.multiple_of(g * GROUP_W + hh * HEAD, HEAD), HEAD)
            out_ref[:, cols] = (o_tok[g][...] * al[g]).astype(BF16)

    specs = [_head_spec(d, tm) for _, d in ATTN_GROUPS]
    return pl.pallas_call(
        body, name="attn_merge", grid=(S // tm, HEADS_PER_GROUP),
        in_specs=specs + specs,
        out_specs=pl.BlockSpec((tm, G * GROUP_W), lambda i, j: (i, 0)),
        out_shape=jax.ShapeDtypeStruct((S, G * GROUP_W), BF16),
        scratch_shapes=[pltpu.VMEM((tm, HEAD), F32)] * (2 * G),
        compiler_params=_params(("parallel", "arbitrary")))(*os_, *lses)


def _attn_merge_bwd(os_, lses, doa, tm=512):
    G = len(os_)
    S = doa.shape[0]

    def body(*refs):
        o_refs, l_refs, doa_ref = refs[:G], refs[G:2 * G], refs[2 * G]
        do_refs, dd_refs = refs[2 * G + 1:3 * G + 1], refs[3 * G + 1:4 * G + 1]
        scr = refs[4 * G + 1:]
        hh = pl.program_id(1)
        o_tok = [_gather_tokens(o_refs[g], scr[g], d, tm) for g, (_, d) in enumerate(ATTN_GROUPS)]
        l_tok = [_gather_tokens(l_refs[g], scr[G + g], d, tm) for g, (_, d) in enumerate(ATTN_GROUPS)]
        do_tok, dd_tok = scr[2 * G:3 * G], scr[3 * G:]
        al = _group_weights(l_tok, slice(None))
        mix = None
        for g in range(G):
            dg = doa_ref[:, pl.ds(pl.multiple_of(g * GROUP_W + hh * HEAD, HEAD), HEAD)]
            do_tok[g][...] = dg * al[g]
            t = al[g] * jnp.sum(dg * o_tok[g][...], axis=-1, keepdims=True)
            mix = t if mix is None else mix + t
        for g, (_, d) in enumerate(ATTN_GROUPS):
            dd_tok[g][...] = jnp.broadcast_to(-al[g] * mix, (tm, HEAD))
            _scatter_tokens(do_tok[g], do_refs[g], d, tm)
            _scatter_tokens(dd_tok[g], dd_refs[g], d, tm)

    specs = [_head_spec(d, tm) for _, d in ATTN_GROUPS]
    shapes = [jax.ShapeDtypeStruct((d, S // d, GROUP_W), F32) for _, d in ATTN_GROUPS]
    return pl.pallas_call(
        body, name="attn_merge_bwd", grid=(S // tm, HEADS_PER_GROUP),
        in_specs=specs + specs + [pl.BlockSpec((tm, G * GROUP_W), lambda i, j: (i, 0))],
        out_specs=specs + specs,
        out_shape=shapes + shapes,
        scratch_shapes=[pltpu.VMEM((tm, HEAD), F32)] * (4 * G),
        compiler_params=_params(("parallel", "arbitrary")))(*os_, *lses, doa)


def _rope_tables(S):
    inv_freq = 1.0 / (ROPE_THETA ** (jnp.arange(0, HEAD, 2, dtype=F32) / HEAD))
    ang = jnp.arange(S, dtype=F32)[:, None] * inv_freq[None, :]
    cos, sin = jnp.cos(ang), jnp.sin(ang)
    return jnp.concatenate([cos, cos], axis=-1), jnp.concatenate([-sin, sin], axis=-1)


def _local_step(x, target, norm_mix, norm_ffn, lb_logits, out_gain, final_norm, comm):
    S = x.shape[0]
    nm0, nm1 = norm_mix[0:1], norm_mix[1:2]
    nf0, nf1 = norm_ffn[0:1], norm_ffn[1:2]
    w = comm.first_weights()

    proj, u0, got = _norm_mm(x, nm0, w["hin"], "hgrn_in", rider=comm.gather_rider(LATE_WEIGHTS_A))
    w.update(comm.gathered(LATE_WEIGHTS_A, got))
    (o, og, states), got = _hgrn_fwd(proj, lb_logits, out_gain, rider=comm.gather_rider(LATE_WEIGHTS_B))
    w.update(comm.gathered(LATE_WEIGHTS_B, got))
    fin_tn = w["fin0"].shape[2]
    h1 = _mm_res(x, og, w["hout"], "hgrn_out")
    z0, u1, _ = _norm_mm(h1, nf0, w["fin0"], "ffn0_in", out_dtype=BF16)
    h2, act0 = _swiglu_mm_res(h1, z0, w["fdn0"], "ffn0_down")
    cos, sin = _rope_tables(S)
    G = len(ATTN_GROUPS)
    w_groups = w["qkv"].transpose(1, 0, 2).reshape(D_MODEL, 3, G, GROUP_W)
    a_g, u2 = zip(*[_qkv_dilated(h2, nm1, w_groups[:, :, gi, :].reshape(D_MODEL, 3 * GROUP_W), cos, sin, d)
                    for gi, (_, d) in enumerate(ATTN_GROUPS)])
    o_g, lse_g = zip(*[_attn_fwd(a) for a in a_g])
    oa = _attn_merge(o_g, lse_g)
    h3 = _mm_res(h2, oa, w["aout"], "attn_out")
    z1, u3, _ = _norm_mm(h3, nf1, w["fin1"], "ffn1_in", out_dtype=BF16)
    h4, act1 = _swiglu_mm_res(h3, z1, w["fdn1"], "ffn1_down")
    dh4, loss, d_final = _loss_head(h4, final_norm, target)

    grads, small = {}, {"final_norm": d_final}

    def ffn_bwd(dh, h_in, u_in, z, act, gain, w_in, w_dn, tag):
        dz = _mm_nt_swiglu_bwd(dh, w_dn, z, tag + "_down_dx")
        g_dn = _mm_tn(act, dh, 1, D_MODEL, D_MODEL, tag + "_down_dw")
        g_in = _mm_tn(u_in, dz, N_CHIPS, fin_tn, fin_tn, tag + "_in_dw")
        dh_in, dgain = _mm_nt_normbwd(dz, w_in, h_in, gain, dh, tag + "_in_dx")
        return dh_in, dgain, g_in, g_dn[0]

    dh3, d_nf1, grads["fin1"], grads["fdn1"] = ffn_bwd(dh4, h3, u3, z1, act1, nf1, w["fin1"], w["fdn1"], "ffn1")
    doa = _mm_nt(dh3, w["aout"][None], "attn_out_dx")
    grads["aout"] = _mm_tn(oa, dh3, 1, D_MODEL, D_MODEL, "attn_out_dw")[0]
    merged = _attn_merge_bwd(o_g, lse_g, doa)
    G = len(ATTN_GROUPS)
    das = [_attn_bwd(a_g[gi], merged[gi], lse_g[gi], merged[G + gi]) for gi in range(G)]
    dqkv = None
    for gi in range(G):
        dqkv = _undilate_group(das[gi], dqkv, cos, sin, gi)
    n_qkv = w["qkv"].shape[2]
    grads["qkv"] = _mm_tn(u2[0], dqkv, N_CHIPS, n_qkv, n_qkv, "attn_qkv_dw")
    dh2, d_nm1 = _mm_nt_normbwd(dqkv, w["qkv"], h2, nm1, dh3, "attn_qkv_dx")

    dh1, d_nf0, grads["fin0"], grads["fdn0"] = ffn_bwd(dh2, h1, u1, z0, act0, nf0, w["fin0"], w["fdn0"], "ffn0")
    dog = _mm_nt(dh1, w["hout"][None], "hgrn_out_dx")
    rider = comm.exchange_rider({k: grads.pop(k) for k in EARLY_GRADS})
    (dproj, dlb, dgn), got = _hgrn_bwd(proj, lb_logits, out_gain, o, states, dog, rider=rider)
    comm.exchanged(got)
    grads["hout"] = _mm_tn(og, dh1, 1, D_MODEL, D_MODEL, "hgrn_out_dw")[0]
    grads["hin"] = _mm_tn(u0, dproj, N_CHIPS, D_MODEL, D_MODEL, "hgrn_in_dw")
    dx, d_nm0 = _mm_nt_normbwd(dproj, w["hin"], x, nm0, dh1, "hgrn_in_dx")

    small["norm_mix"] = jnp.concatenate([d_nm0, d_nm1], axis=0)
    small["norm_ffn"] = jnp.concatenate([d_nf0, d_nf1], axis=0)
    small["lb"] = dlb.reshape(1, HGRN_HEADS * HEAD)
    small["out_norm"] = dgn.reshape(HGRN_HEADS, HEAD)
    return loss, dx, grads, small


def _place():
    x, y, c = lax.axis_index("x"), lax.axis_index("y"), lax.axis_index("c")
    others = [(1 - x, y), (x, 1 - y), (1 - x, 1 - y)]
    return x, y, c, others


ANY = pl.BlockSpec(memory_space=pl.ANY)


class _GatherRider:
    def __init__(self, shards):
        self.operands = list(shards)
        n = self.n = len(shards)
        self.out_shape = [jax.ShapeDtypeStruct((N_CHIPS,) + s.shape, s.dtype) for s in shards]
        self.scratch = [pltpu.SemaphoreType.DMA((3 * n,)), pltpu.SemaphoreType.DMA((3 * n,)),
                        pltpu.SemaphoreType.DMA((3 * n,)), pltpu.SemaphoreType.DMA((3 * n,)),
                        pltpu.SemaphoreType.DMA((n,)), pltpu.SemaphoreType.DMA((n,))]

    def _copies(self, ins, outs, sems):
        ici_send, ici_recv, _, _, own_send, own_recv = sems
        x, y, c, others = _place()
        me = 2 * x + y
        own = [pltpu.make_async_remote_copy(
            src_ref=ins[a], dst_ref=outs[a].at[me], send_sem=own_send.at[a], recv_sem=own_recv.at[a],
            device_id=(x, y, 1 - c), device_id_type=MESH) for a in range(self.n)]
        sends = [pltpu.make_async_remote_copy(
            src_ref=ins[a].at[c], dst_ref=outs[a].at[me, c], send_sem=ici_send.at[a * 3 + k], recv_sem=ici_recv.at[a * 3 + k],
            device_id=(ox, oy, c), device_id_type=MESH) for a in range(self.n) for k, (ox, oy) in enumerate(others)]
        return own, sends

    def start(self, ins, outs, sems):
        own, sends = self._copies(ins, outs, sems)
        for cp in own + sends:
            cp.start()

    def finish(self, ins, outs, sems):
        ici_send, ici_recv, d2d_send, d2d_recv, _, _ = sems
        x, y, c, others = _place()
        sibling = (x, y, 1 - c)
        own, sends = self._copies(ins, outs, sems)
        passes = []
        for a in range(self.n):
            for k, (ox, oy) in enumerate(others):
                s = a * 3 + k
                got = outs[a].at[2 * ox + oy, c]
                pltpu.make_async_remote_copy(
                    src_ref=got, dst_ref=got, send_sem=ici_send.at[s], recv_sem=ici_recv.at[s],
                    device_id=(ox, oy, c), device_id_type=MESH).wait_recv()
                fwd = pltpu.make_async_remote_copy(
                    src_ref=got, dst_ref=got, send_sem=d2d_send.at[s], recv_sem=d2d_recv.at[s],
                    device_id=sibling, device_id_type=MESH)
                fwd.start()
                passes.append(fwd)
        for a in range(self.n):
            for k, (ox, oy) in enumerate(others):
                s = a * 3 + k
                theirs = outs[a].at[2 * ox + oy, 1 - c]
                pltpu.make_async_remote_copy(
                    src_ref=theirs, dst_ref=theirs, send_sem=d2d_send.at[s], recv_sem=d2d_recv.at[s],
                    device_id=sibling, device_id_type=MESH).wait_recv()
        for cp in own:
            cp.wait()
        for cp in sends + passes:
            cp.wait_send()


class _ExchangeRider:
    def __init__(self, parts):
        self.operands = list(parts)
        n = self.n = len(parts)
        self.out_shape = [jax.ShapeDtypeStruct(p.shape, p.dtype) for p in parts]
        self.scratch = [pltpu.SemaphoreType.DMA((3 * n,)), pltpu.SemaphoreType.DMA((3 * n,))]

    def _copies(self, ins, outs, sems):
        send_sem, recv_sem = sems
        x, y, c, others = _place()
        me = 2 * x + y
        return [pltpu.make_async_remote_copy(
            src_ref=ins[a].at[2 * ox + oy], dst_ref=outs[a].at[me], send_sem=send_sem.at[a * 3 + k],
            recv_sem=recv_sem.at[a * 3 + k], device_id=(ox, oy, c), device_id_type=MESH)
            for a in range(self.n) for k, (ox, oy) in enumerate(others)]

    def start(self, ins, outs, sems):
        for cp in self._copies(ins, outs, sems):
            cp.start()

    def finish(self, ins, outs, sems):
        send_sem, recv_sem = sems
        x, y, c, others = _place()
        for a in range(self.n):
            for k, (ox, oy) in enumerate(others):
                s = a * 3 + k
                got = outs[a].at[2 * ox + oy]
                pltpu.make_async_remote_copy(
                    src_ref=got, dst_ref=got, send_sem=send_sem.at[s], recv_sem=recv_sem.at[s],
                    device_id=(ox, oy, c), device_id_type=MESH).wait_recv()
        for cp in self._copies(ins, outs, sems):
            cp.wait_send()


def _run_rider(rider, name):
    n = rider.n

    def body(*refs):
        ins, outs, sems = refs[:n], refs[n:2 * n], refs[2 * n:]
        rider.start(ins, outs, sems)
        rider.finish(ins, outs, sems)

    return pl.pallas_call(
        body, name=name, in_specs=[ANY] * n, out_specs=[ANY] * n,
        out_shape=rider.out_shape, scratch_shapes=rider.scratch)(*rider.operands)


def _ride(rider, body, n_in, n_out, first, last):
    if rider is None:
        return body
    n = rider.n

    def wrapped(*refs):
        host_in, r_in = refs[:n_in], refs[n_in:n_in + n]
        host_out = refs[n_in + n:n_in + n + n_out]
        r_out = refs[n_in + n + n_out:n_in + 2 * n + n_out]
        rest = refs[n_in + 2 * n + n_out:]
        host_scr, sems = rest[:len(rest) - len(rider.scratch)], rest[len(rest) - len(rider.scratch):]

        @pl.when(first())
        def _():
            rider.start(r_in, r_out, sems)

        body(*host_in, *host_out, *host_scr)

        @pl.when(last())
        def _():
            rider.finish(r_in, r_out, sems)

    return wrapped


def _rider_args(rider):
    if rider is None:
        return [], [], [], [], []
    return rider.operands, [ANY] * rider.n, [ANY] * rider.n, rider.out_shape, rider.scratch


def _pair_exchange(grads, name):
    n = len(grads)

    def body(*refs):
        ins, outs = refs[:n], refs[n:2 * n]
        send_sem, recv_sem = refs[2 * n:]
        x, y, c, _ = _place()
        sibling = (x, y, 1 - c)
        cps = []
        for a in range(n):
            for j in range(N_CHIPS):
                s = a * N_CHIPS + j
                cps.append(pltpu.make_async_remote_copy(
                    src_ref=ins[a].at[j, 1 - c], dst_ref=outs[a].at[j], send_sem=send_sem.at[s], recv_sem=recv_sem.at[s],
                    device_id=sibling, device_id_type=MESH))
        for cp in cps:
            cp.start()
        for cp in cps:
            cp.wait()

    return pl.pallas_call(
        body, name=name,
        in_specs=[ANY] * n, out_specs=[ANY] * n,
        out_shape=[jax.ShapeDtypeStruct((N_CHIPS,) + g.shape[2:], F32) for g in grads],
        scratch_shapes=[pltpu.SemaphoreType.DMA((N_CHIPS * n,)), pltpu.SemaphoreType.DMA((N_CHIPS * n,))],
        )(*grads)


def _pair_sum(g, got, c_idx):
    _, _, r, cw = g.shape
    tr = _row_tile(r, cw)

    def body(c_ref, g_ref, got_ref, p_ref, pb_ref):
        v = g_ref[...] + got_ref[...]
        p_ref[...] = v
        pb_ref[...] = v.astype(BF16)

    blk = pl.BlockSpec((None, tr, cw), lambda j, i, c_ref: (j, i, 0))
    return pl.pallas_call(
        body, name="grad_pair_sum",
        grid_spec=pltpu.PrefetchScalarGridSpec(
            num_scalar_prefetch=1, grid=(N_CHIPS, r // tr),
            in_specs=[pl.BlockSpec((None, None, tr, cw), lambda j, i, c_ref: (j, c_ref[0], i, 0)), blk],
            out_specs=[blk, blk]),
        out_shape=[jax.ShapeDtypeStruct((N_CHIPS, r, cw), F32), jax.ShapeDtypeStruct((N_CHIPS, r, cw), BF16)],
        compiler_params=_params(("parallel", "parallel")))(c_idx, g, got)


def _chip_sum(p, got, me_idx):
    _, r, cw = p.shape
    tr = _row_tile(r, cw)

    def body(me_ref, own_ref, got_ref, t_ref):
        me = me_ref[0]
        acc = None
        for s in range(N_CHIPS):
            term = jnp.where(me == s, own_ref[...], got_ref[s].astype(F32))
            acc = term if acc is None else acc + term
        t_ref[...] = acc

    return pl.pallas_call(
        body, name="grad_chip_sum",
        grid_spec=pltpu.PrefetchScalarGridSpec(
            num_scalar_prefetch=1, grid=(r // tr,),
            in_specs=[pl.BlockSpec((None, tr, cw), lambda i, me_ref: (me_ref[0], i, 0)),
                      pl.BlockSpec((N_CHIPS, tr, cw), lambda i, me_ref: (0, i, 0))],
            out_specs=pl.BlockSpec((tr, cw), lambda i, me_ref: (i, 0))),
        out_shape=jax.ShapeDtypeStruct((r, cw), F32),
        compiler_params=_params(("parallel",)))(me_idx, p, got)


def _pair_share(halves):
    n = len(halves)

    def body(*refs):
        ins, outs = refs[:n], refs[n:2 * n]
        send_sem, recv_sem = refs[2 * n:]
        x, y, c, _ = _place()
        cps = [pltpu.make_async_remote_copy(
            src_ref=ins[a], dst_ref=outs[a], send_sem=send_sem.at[a], recv_sem=recv_sem.at[a],
            device_id=(x, y, 1 - c), device_id_type=MESH) for a in range(n)]
        for cp in cps:
            cp.start()
        for cp in cps:
            cp.wait()

    return pl.pallas_call(
        body, name="grad_pair_share",
        in_specs=[ANY] * n, out_specs=[ANY] * n,
        out_shape=[jax.ShapeDtypeStruct(h.shape, F32) for h in halves],
        scratch_shapes=[pltpu.SemaphoreType.DMA((n,)), pltpu.SemaphoreType.DMA((n,))],
        )(*halves)


def _small_allreduce(pack):
    m_per, ncol = pack.shape
    n_dev = 8

    def body(x_ref, sum_ref, all_ref, send_sems, recv_sems, local_sem):
        x, y, c, others = _place()
        me, sibling = (x, y, c), (x, y, 1 - c)

        def rows(px, py, pc):
            return all_ref.at[pl.ds((4 * px + 2 * py + pc) * m_per, m_per), :]

        def copy(k, block, to, src=None):
            return pltpu.make_async_remote_copy(
                src_ref=rows(*block) if src is None else src, dst_ref=rows(*block),
                send_sem=send_sems.at[k], recv_sem=recv_sems.at[k], device_id=to, device_id_type=MESH)

        mine = pltpu.make_async_copy(x_ref, rows(*me), local_sem)
        mine.start()
        first = [copy(0, me, sibling, src=x_ref)]
        first += [copy(1 + j, me, (*chip, c), src=x_ref) for j, chip in enumerate(others)]
        for cp in first:
            cp.start()
        passed = [copy(4 + j, (*chip, c), sibling) for j, chip in enumerate(others)]
        for j, chip in enumerate(others):
            copy(1 + j, (*chip, c), me).wait_recv()
            passed[j].start()
        copy(0, sibling, me).wait_recv()
        for j, chip in enumerate(others):
            copy(4 + j, (*chip, 1 - c), me).wait_recv()
        for cp in first + passed:
            cp.wait_send()
        mine.wait()
        acc = all_ref[0:m_per, :]
        for dvc in range(1, n_dev):
            acc = acc + all_ref[dvc * m_per:(dvc + 1) * m_per, :]
        sum_ref[...] = acc

    return pl.pallas_call(
        body, name="small_allreduce",
        in_specs=[pl.BlockSpec(memory_space=pltpu.VMEM)],
        out_specs=pl.BlockSpec(memory_space=pltpu.VMEM),
        out_shape=jax.ShapeDtypeStruct((m_per, ncol), F32),
        scratch_shapes=[pltpu.VMEM((n_dev * m_per, ncol), F32),
                        pltpu.SemaphoreType.DMA((7,)), pltpu.SemaphoreType.DMA((7,)), pltpu.SemaphoreType.DMA],
        )(pack)


def _adam_math(w, g, m, v):
    m = ADAM_B1 * m + (1.0 - ADAM_B1) * g
    v = ADAM_B2 * v + (1.0 - ADAM_B2) * (g * g)
    m_hat = m / (1.0 - ADAM_B1 ** ADAM_STEP)
    v_hat = v / (1.0 - ADAM_B2 ** ADAM_STEP)
    delta = -ADAM_LR * (m_hat / (jnp.sqrt(v_hat) + ADAM_EPS) + ADAM_WD * w)
    return delta, m, v


def _adamw(mine, theirs, c_idx, w, m, v, name):
    r, C = mine.shape
    tr = _row_tile(r, C, 512 * 1024)
    nt = r // tr

    def body(c_ref, mine_ref, theirs_ref, w_ref, m_ref, v_ref, g_ref, d_ref, nm_ref, nv_ref):
        g = jnp.where(pl.program_id(0) == c_ref[0], mine_ref[...], theirs_ref[...])
        g_ref[...] = g
        d_ref[...], nm_ref[...], nv_ref[...] = _adam_math(w_ref[...], g, m_ref[...], v_ref[...])

    half = pl.BlockSpec((tr, C), lambda h, i, c_ref: (i, 0))
    full = pl.BlockSpec((tr, C), lambda h, i, c_ref: (h * nt + i, 0))
    shp = jax.ShapeDtypeStruct((2 * r, C), F32)
    return pl.pallas_call(
        body, name=name,
        grid_spec=pltpu.PrefetchScalarGridSpec(
            num_scalar_prefetch=1, grid=(2, nt),
            in_specs=[half, half, full, full, full], out_specs=[full] * 4),
        out_shape=[shp] * 4,
        compiler_params=_params(("parallel", "parallel")))(c_idx, mine, theirs, w, m, v)


def _small_update(gsum, logits_pack, w, m, v):
    def body(gs_ref, lg_ref, w_ref, m_ref, v_ref, g_ref, d_ref, nm_ref, nv_ref):
        g_ref[...] = gs_ref[...]
        l0, l1, l2 = lg_ref[0:1, :], lg_ref[1:2, :], lg_ref[2:3, :]
        mx = jnp.maximum(jnp.maximum(l0, l1), l2)
        e0, e1, e2 = jnp.exp(l0 - mx), jnp.exp(l1 - mx), jnp.exp(l2 - mx)
        tot = e0 + e1 + e2
        p0, p1, p2 = e0 / tot, e1 / tot, e2 / tot
        dlb = gs_ref[4:5, :]
        g_ref[4:5, :] = dlb * p0 * (1.0 - p0)
        g_ref[5:6, :] = -dlb * p0 * p1
        g_ref[6:7, :] = -dlb * p0 * p2
        d_ref[...], nm_ref[...], nv_ref[...] = _adam_math(w_ref[...], g_ref[...], m_ref[...], v_ref[...])

    full = pl.BlockSpec(memory_space=pltpu.VMEM)
    shp = jax.ShapeDtypeStruct(gsum.shape, F32)
    return pl.pallas_call(
        body, name="small_update", in_specs=[full] * 5, out_specs=[full] * 4, out_shape=[shp] * 4)(
            gsum, logits_pack, w, m, v)


def _pack_small(norm_mix, norm_ffn, lb3, out_norm, final_norm, extra=None):
    ncol = norm_mix.shape[1]
    on = jnp.pad(out_norm.reshape(1, -1), ((0, 0), (0, ncol - out_norm.size)))
    rows = [norm_mix, norm_ffn, lb3, on, final_norm.reshape(1, ncol)]
    if extra is not None:
        rows.append(extra)
    used = sum(r.shape[0] for r in rows)
    rows.append(jnp.zeros((SMALL_ROWS - used, ncol), F32))
    return jnp.concatenate(rows, axis=0)


WEIGHT_NAMES = ("hin", "hout", "qkv", "aout", "fin0", "fin1", "fdn0", "fdn1")
FIRST_WEIGHTS = ("hin", "hout")
LATE_WEIGHTS_A = ("fin0", "fdn0")
LATE_WEIGHTS_B = ("qkv", "aout", "fin1", "fdn1")
EARLY_GRADS = ("qkv", "aout", "fin0", "fin1", "fdn0", "fdn1")


def _split_weights(hgrn_w_in, hgrn_w_out, attn_w_qkv, attn_w_out, ffn_w_in, ffn_w_down):
    return {"hin": hgrn_w_in[0], "hout": hgrn_w_out[0], "qkv": attn_w_qkv[0], "aout": attn_w_out[0],
            "fin0": ffn_w_in[0], "fin1": ffn_w_in[1], "fdn0": ffn_w_down[0], "fdn1": ffn_w_down[1]}


def _halves(v):
    r, c = v.shape
    return v.reshape(2, r // 2, c)


def _full_weights(gathered):
    out = {}
    for k, g in gathered.items():
        _, _, r, c = g.shape
        if k in ("hin", "qkv", "fin0", "fin1"):
            out[k] = g.reshape(N_CHIPS, 2 * r, c)
        else:
            out[k] = g.reshape(N_CHIPS * 2 * r, c)
    return out


class _StepComm:
    def __init__(self, shards, c_idx, me_idx):
        self.shards, self.c_idx, self.me_idx = shards, c_idx, me_idx
        self.halves = {}

    def gather_rider(self, names):
        return _GatherRider([_halves(self.shards[k].astype(BF16)) for k in names])

    def gathered(self, names, got):
        return _full_weights(dict(zip(names, got)))

    def first_weights(self):
        return self.gathered(FIRST_WEIGHTS, _run_rider(self.gather_rider(FIRST_WEIGHTS), "gather_first"))

    def _pair_sums(self, grads, name):
        names = list(grads)
        g4 = []
        for k in names:
            r, c = self.shards[k].shape
            g4.append(grads[k].reshape(N_CHIPS, 2, r // 2, c))
        from_sibling = _pair_exchange(g4, name)
        return names, [_pair_sum(g, got, self.c_idx) for g, got in zip(g4, from_sibling)]

    def _chip_sums(self, names, sums, got):
        for k, s, g in zip(names, sums, got):
            self.halves[k] = _chip_sum(s[0], g, self.me_idx)

    def exchange_rider(self, grads):
        self._riding = self._pair_sums(grads, "grad_pair_exchange_early")
        return _ExchangeRider([s[1] for s in self._riding[1]])

    def exchanged(self, got):
        self._chip_sums(*self._riding, got)

    def reduce_rest(self, grads):
        names, sums = self._pair_sums(grads, "grad_pair_exchange_late")
        self._chip_sums(names, sums, _run_rider(_ExchangeRider([s[1] for s in sums]), "grad_chip_exchange_late"))

    def shared_halves(self):
        mine = [self.halves[k] for k in WEIGHT_NAMES]
        return dict(zip(WEIGHT_NAMES, zip(mine, _pair_share(mine))))


def kernel(x, norm_mix, norm_ffn, hgrn_w_in, hgrn_lb_logits, hgrn_out_norm, hgrn_w_out, attn_w_qkv, attn_w_out, ffn_w_in, ffn_w_down, final_norm, loss_target, m_norm_mix, m_norm_ffn, m_hgrn_w_in, m_hgrn_lb_logits, m_hgrn_out_norm, m_hgrn_w_out, m_attn_w_qkv, m_attn_w_out, m_ffn_w_in, m_ffn_w_down, m_final_norm, v_norm_mix, v_norm_ffn, v_hgrn_w_in, v_hgrn_lb_logits, v_hgrn_out_norm, v_hgrn_w_out, v_attn_w_qkv, v_attn_w_out, v_ffn_w_in, v_ffn_w_down, v_final_norm):
    S = x.shape[1]
    xi, yi, ci = lax.axis_index("x"), lax.axis_index("y"), lax.axis_index("c")
    c_idx = jnp.reshape(ci, (1,)).astype(jnp.int32)
    me_idx = jnp.reshape(2 * xi + yi, (1,)).astype(jnp.int32)

    w_own = _split_weights(hgrn_w_in, hgrn_w_out, attn_w_qkv, attn_w_out, ffn_w_in, ffn_w_down)
    m_own = _split_weights(m_hgrn_w_in, m_hgrn_w_out, m_attn_w_qkv, m_attn_w_out, m_ffn_w_in, m_ffn_w_down)
    v_own = _split_weights(v_hgrn_w_in, v_hgrn_w_out, v_attn_w_qkv, v_attn_w_out, v_ffn_w_in, v_ffn_w_down)

    comm = _StepComm(w_own, c_idx, me_idx)
    loss, dx, grads, small = _local_step(
        x.reshape(S, D_MODEL), loss_target.reshape(S, D_MODEL), norm_mix, norm_ffn, hgrn_lb_logits,
        hgrn_out_norm, final_norm.reshape(1, D_MODEL), comm)
    comm.reduce_rest(grads)

    g_out, d_out, m_out, v_out = {}, {}, {}, {}
    for k, (mine, theirs) in comm.shared_halves().items():
        g_out[k], d_out[k], m_out[k], v_out[k] = _adamw(mine, theirs, c_idx, w_own[k], m_own[k], v_own[k], "adamw_" + k)

    loss_row = jnp.pad(loss, ((0, 0), (0, D_MODEL - loss.shape[1])))
    lb3 = jnp.concatenate([small["lb"], jnp.zeros((2, D_MODEL), F32)], axis=0)
    on_grad = jnp.sum(small["out_norm"], axis=0, keepdims=True)
    pack = _pack_small(small["norm_mix"], small["norm_ffn"], lb3, on_grad, small["final_norm"], loss_row)
    gsum = _small_allreduce(pack)
    w_s = _pack_small(norm_mix, norm_ffn, hgrn_lb_logits, hgrn_out_norm, final_norm)
    m_s = _pack_small(m_norm_mix, m_norm_ffn, m_hgrn_lb_logits, m_hgrn_out_norm, m_final_norm)
    v_s = _pack_small(v_norm_mix, v_norm_ffn, v_hgrn_lb_logits, v_hgrn_out_norm, v_final_norm)
    lg_pack = jnp.pad(hgrn_lb_logits, ((0, 8 - hgrn_lb_logits.shape[0]), (0, 0)))
    sg, sd, sm, sv = _small_update(gsum, lg_pack, w_s, m_s, v_s)

    def unpack(p):
        return (p[0:2], p[2:4], p[4:7], p[7:8, :HEAD], p[8])

    def big(dct):
        return (dct["hin"][None], dct["hout"][None], dct["qkv"][None], dct["aout"][None],
                jnp.stack([dct["fin0"], dct["fin1"]]), jnp.stack([dct["fdn0"], dct["fdn1"]]))

    def assemble(p, dct):
        nmx, nff, lbl, onm, fnm = unpack(p)
        hin, hout, qkv, aout, fin, fdn = big(dct)
        return (nmx, nff, hin, lbl, onm, hout, qkv, aout, fin, fdn, fnm)

    total_loss = gsum[9, 0]
    return (total_loss, dx.reshape(1, S, D_MODEL), *assemble(sg, g_out), *assemble(sd, d_out),
            *assemble(sm, m_out), *assemble(sv, v_out))
```

```python
import functools

import jax
import jax.numpy as jnp
from jax import lax
from jax.experimental import pallas as pl
from jax.experimental.pallas import tpu as pltpu

F32 = jnp.float32
BF16 = jnp.bfloat16
MESH = pl.DeviceIdType.MESH

D_MODEL = 1024
HEAD = 128
HGRN_HEADS = 8
HGRN_CHUNK = 64
ATTN_GROUPS = ((128, 1), (512, 4), (2048, 16))
ATTN_SPAN = 128
HEADS_PER_GROUP = 4
GROUP_W = HEADS_PER_GROUP * HEAD
D_FF = 2816
NORM_EPS = 1e-6
ROPE_THETA = 10000.0
NEG = -1e30

ADAM_LR, ADAM_B1, ADAM_B2, ADAM_EPS, ADAM_WD, ADAM_STEP = 0.001, 0.9, 0.999, 1e-08, 0.01, 10

N_CHIPS = 4
VMEM_LIMIT = 56 * 1024 * 1024
SMALL_ROWS = 16


def _params(sem=None):
    return pltpu.CompilerParams(dimension_semantics=sem, vmem_limit_bytes=VMEM_LIMIT)


def _row_tile(rows, cols, budget_bytes=3 * 512 * 1024):
    best = 8
    for t in range(8, rows + 1, 8):
        if rows % t == 0 and t * cols * 4 <= budget_bytes:
            best = t
    assert rows % best == 0
    return best


def _grid_corner(i, j):
    return jnp.logical_and(pl.program_id(0) == i, pl.program_id(1) == j)


def _sigmoid(v):
    return 1.0 / (1.0 + jnp.exp(-v))


def _dot(a, b):
    return jnp.dot(a, b, preferred_element_type=F32)


def _dot_nt(a, b):
    return lax.dot_general(a, b, (((1,), (1,)), ((), ())), preferred_element_type=F32)


def _dot_tn(a, b):
    return lax.dot_general(a, b, (((0,), (0,)), ((), ())), preferred_element_type=F32)


def _dot_exact(ones, b):
    ones = ones.astype(BF16)
    hi = b.astype(BF16)
    rest = b - hi.astype(F32)
    mid = rest.astype(BF16)
    low = (rest - mid.astype(F32)).astype(BF16)
    return _dot(ones, hi) + _dot(ones, mid) + _dot(ones, low)


def _rstd(v):
    return lax.rsqrt(jnp.mean(v * v, axis=-1, keepdims=True) + NORM_EPS)


def _norm_mm(h, gain, w3, name, out_dtype=F32, tm=512, rider=None):
    S, K = h.shape
    J, _, n = w3.shape
    gi = S // tm

    def body(h_ref, g_ref, w_ref, y_ref, u_ref):
        @pl.when(pl.program_id(1) == 0)
        def _():
            v = h_ref[...]
            u_ref[...] = (v * _rstd(v) * g_ref[...]).astype(BF16)

        y_ref[...] = _dot(u_ref[...], w_ref[pl.program_id(1)]).astype(y_ref.dtype)

    r_ops, r_in, r_out, r_shape, r_scr = _rider_args(rider)
    res = pl.pallas_call(
        _ride(rider, body, 3, 2, functools.partial(_grid_corner, 0, 0), functools.partial(_grid_corner, gi - 1, J - 1)),
        name=name, grid=(gi, J),
        in_specs=[pl.BlockSpec((tm, K), lambda i, j: (i, 0)),
                  pl.BlockSpec((1, K), lambda i, j: (0, 0)),
                  pl.BlockSpec((J, K, n), lambda i, j: (0, 0, 0))] + r_in,
        out_specs=[pl.BlockSpec((tm, n), lambda i, j: (i, j)), pl.BlockSpec((tm, K), lambda i, j: (i, 0))] + r_out,
        out_shape=[jax.ShapeDtypeStruct((S, J * n), out_dtype), jax.ShapeDtypeStruct((S, K), BF16)] + r_shape,
        scratch_shapes=r_scr,
        compiler_params=_params(("arbitrary", "arbitrary")))(h, gain, w3, *r_ops)
    return res[0], res[1], res[2:]


def _mm_res(h, a, w2, name, tm=512):
    S, N = h.shape
    K = a.shape[1]

    def body(h_ref, a_ref, w_ref, o_ref):
        o_ref[...] = h_ref[...] + _dot(a_ref[...], w_ref[...])

    return pl.pallas_call(
        body, name=name, grid=(S // tm,),
        in_specs=[pl.BlockSpec((tm, N), lambda i: (i, 0)),
                  pl.BlockSpec((tm, K), lambda i: (i, 0)),
                  pl.BlockSpec((K, N), lambda i: (0, 0))],
        out_specs=pl.BlockSpec((tm, N), lambda i: (i, 0)),
        out_shape=jax.ShapeDtypeStruct((S, N), F32),
        compiler_params=_params(("parallel",)))(h, a, w2)


def _swiglu(z_ref, F):
    g = z_ref[:, :F].astype(F32)
    return (g * _sigmoid(g) * z_ref[:, F:].astype(F32)).astype(BF16)


def _swiglu_mm_res(h, z, w2, name, tm=256):
    S, N = h.shape
    F = w2.shape[0]

    def body(h_ref, z_ref, w_ref, o_ref, a_ref):
        a = _swiglu(z_ref, F)
        a_ref[...] = a
        o_ref[...] = h_ref[...] + _dot(a, w_ref[...])

    return pl.pallas_call(
        body, name=name, grid=(S // tm,),
        in_specs=[pl.BlockSpec((tm, N), lambda i: (i, 0)),
                  pl.BlockSpec((tm, 2 * F), lambda i: (i, 0)),
                  pl.BlockSpec((F, N), lambda i: (0, 0))],
        out_specs=[pl.BlockSpec((tm, N), lambda i: (i, 0)), pl.BlockSpec((tm, F), lambda i: (i, 0))],
        out_shape=[jax.ShapeDtypeStruct((S, N), F32), jax.ShapeDtypeStruct((S, F), BF16)],
        compiler_params=_params(("parallel",)))(h, z, w2)


def _dy_specs(dy, J, n, tm):
    if dy.ndim == 3:
        return [pl.BlockSpec((None, tm, n), functools.partial(lambda i, j: (j, i, 0), j=j)) for j in range(J)]
    return [pl.BlockSpec((tm, n), functools.partial(lambda i, j: (i, j), j=j)) for j in range(J)]


def _acc_nt(dy_refs, w_ref):
    acc = None
    for j, r in enumerate(dy_refs):
        t = _dot_nt(r[...].astype(BF16), w_ref[j])
        acc = t if acc is None else acc + t
    return acc


def _mm_nt(dy, w3, name, out_dtype=F32, tm=512):
    J, K, n = w3.shape
    S = dy.shape[-2]

    def body(*refs):
        dy_refs, w_ref, o_ref = refs[:J], refs[J], refs[J + 1]
        o_ref[...] = _acc_nt(dy_refs, w_ref).astype(o_ref.dtype)

    return pl.pallas_call(
        body, name=name, grid=(S // tm,),
        in_specs=_dy_specs(dy, J, n, tm) + [pl.BlockSpec((J, K, n), lambda i: (0, 0, 0))],
        out_specs=pl.BlockSpec((tm, K), lambda i: (i, 0)),
        out_shape=jax.ShapeDtypeStruct((S, K), out_dtype),
        compiler_params=_params(("parallel",)))(*([dy] * J), w3)


def _mm_nt_normbwd(dy, w3, h, gain, dh, name, tm=512):
    J, K, n = w3.shape
    S = h.shape[0]

    def body(*refs):
        dy_refs, w_ref, h_ref, g_ref, dh_ref, o_ref, dg_ref = refs[:J], *refs[J:]
        du = _acc_nt(dy_refs, w_ref)
        v = h_ref[...]
        r = _rstd(v)
        xh = v * r
        dyg = du * g_ref[...]
        o_ref[...] = dh_ref[...] + r * (dyg - xh * jnp.mean(dyg * xh, axis=-1, keepdims=True))

        @pl.when(pl.program_id(0) == 0)
        def _():
            dg_ref[...] = jnp.zeros_like(dg_ref)

        dg_ref[...] += jnp.sum(du * xh, axis=0, keepdims=True)

    row = pl.BlockSpec((tm, K), lambda i: (i, 0))
    vec = pl.BlockSpec((1, K), lambda i: (0, 0))
    return pl.pallas_call(
        body, name=name, grid=(S // tm,),
        in_specs=_dy_specs(dy, J, n, tm) + [pl.BlockSpec((J, K, n), lambda i: (0, 0, 0)), row, vec, row],
        out_specs=[row, vec],
        out_shape=[jax.ShapeDtypeStruct((S, K), F32), jax.ShapeDtypeStruct((1, K), F32)],
        compiler_params=_params(("arbitrary",)))(*([dy] * J), w3, h, gain, dh)


def _mm_nt_swiglu_bwd(dh, w2, z, name, tm=256):
    F, N = w2.shape
    S = dh.shape[0]

    def body(dh_ref, w_ref, z_ref, o_ref):
        da = _dot_nt(dh_ref[...].astype(BF16), w_ref[...])
        g = z_ref[:, :F].astype(F32)
        u = z_ref[:, F:].astype(F32)
        sg = _sigmoid(g)
        o_ref[:, :F] = (da * u * (sg * (1.0 + g * (1.0 - sg)))).astype(BF16)
        o_ref[:, F:] = (da * (g * sg)).astype(BF16)

    return pl.pallas_call(
        body, name=name, grid=(S // tm,),
        in_specs=[pl.BlockSpec((tm, N), lambda i: (i, 0)),
                  pl.BlockSpec((F, N), lambda i: (0, 0)),
                  pl.BlockSpec((tm, 2 * F), lambda i: (i, 0))],
        out_specs=pl.BlockSpec((tm, 2 * F), lambda i: (i, 0)),
        out_shape=jax.ShapeDtypeStruct((S, 2 * F), BF16),
        compiler_params=_params(("parallel",)))(dh, w2, z)


def _mm_tn(x, dy, J, n, tn, name):
    tpn = n // tn
    ts = 512
    S, K = x.shape
    if dy.ndim == 3:
        dy_spec = pl.BlockSpec((None, ts, tn), lambda c, s: (c // tpn, s, c % tpn))
    else:
        dy_spec = pl.BlockSpec((ts, tn), lambda c, s: (s, c))

    def body(x_ref, dy_ref, o_ref):
        @pl.when(pl.program_id(1) == 0)
        def _():
            o_ref[...] = jnp.zeros_like(o_ref)

        o_ref[...] += _dot_tn(x_ref[...], dy_ref[...].astype(BF16))

    return pl.pallas_call(
        body, name=name, grid=(J * tpn, S // ts),
        in_specs=[pl.BlockSpec((ts, K), lambda c, s: (s, 0)), dy_spec],
        out_specs=pl.BlockSpec((None, K, tn), lambda c, s: (c // tpn, 0, c % tpn)),
        out_shape=jax.ShapeDtypeStruct((J, K, n), F32),
        compiler_params=_params(("parallel", "arbitrary")))(x, dy)


def _loss_head(h, gain, target, tm=512):
    S, K = h.shape

    def body(h_ref, g_ref, t_ref, dh_ref, loss_ref, dg_ref):
        v = h_ref[...]
        r = _rstd(v)
        xh = v * r
        g = g_ref[...]
        dy = (xh * g - t_ref[...]) * (1.0 / K)
        dyg = dy * g
        dh_ref[...] = r * (dyg - xh * jnp.mean(dyg * xh, axis=-1, keepdims=True))

        @pl.when(pl.program_id(0) == 0)
        def _():
            loss_ref[...] = jnp.zeros_like(loss_ref)
            dg_ref[...] = jnp.zeros_like(dg_ref)

        part = jnp.sum(jnp.sum(dy * dy, axis=-1, keepdims=True), axis=0, keepdims=True) * (0.5 * K)
        lane = lax.broadcasted_iota(jnp.int32, loss_ref.shape, 1)
        loss_ref[...] += jnp.where(lane == 0, part, 0.0)
        dg_ref[...] += jnp.sum(dy * xh, axis=0, keepdims=True)

    row = pl.BlockSpec((tm, K), lambda i: (i, 0))
    vec = pl.BlockSpec((1, K), lambda i: (0, 0))
    return pl.pallas_call(
        body, name="loss_head", grid=(S // tm,),
        in_specs=[row, vec, row],
        out_specs=[row, pl.BlockSpec((1, HEAD), lambda i: (0, 0)), vec],
        out_shape=[jax.ShapeDtypeStruct((S, K), F32), jax.ShapeDtypeStruct((1, HEAD), F32),
                   jax.ShapeDtypeStruct((1, K), F32)],
        compiler_params=_params(("arbitrary",)))(h, gain, target)


def _lower_bound(lg_ref):
    l0, l1, l2 = lg_ref[0:1, :], lg_ref[1:2, :], lg_ref[2:3, :]
    mx = jnp.maximum(jnp.maximum(l0, l1), l2)
    e0, e1, e2 = jnp.exp(l0 - mx), jnp.exp(l1 - mx), jnp.exp(l2 - mx)
    return e0 / (e0 + e1 + e2)


def _chunks(v, ncb):
    C = HGRN_CHUNK
    return [v[c * C:(c + 1) * C] for c in range(ncb)]


def _rows(parts):
    return jnp.concatenate(parts, axis=0)


def _block_gates(qz, fz, lb, ncb):
    C = HGRN_CHUNK
    row = lax.broadcasted_iota(jnp.int32, (C, C), 0)
    col = lax.broadcasted_iota(jnp.int32, (C, C), 1)
    tri = (col <= row).astype(F32)
    first_half = lax.broadcasted_iota(jnp.int32, (C, HEAD), 0) < C // 2
    sig = _sigmoid(fz)
    fg = lb + (1.0 - lb) * sig
    key = 1.0 - fg
    lg = jnp.log(fg)
    lgs = _chunks(lg, ncb)
    b = _rows([_dot_exact(tri, v) for v in lgs])
    r_c = [jnp.sum(jnp.where(first_half, v, 0.0), axis=0, keepdims=True) for v in lgs]
    bl_c = [jnp.sum(v, axis=0, keepdims=True) for v in lgs]
    r = _rows([jnp.broadcast_to(v, (C, HEAD)) for v in r_c])
    bl = _rows([jnp.broadcast_to(v, (C, HEAD)) for v in bl_c])
    sq = _sigmoid(qz)
    qy = qz * sq
    return sig, fg, key, b, r, bl, bl_c, sq, qy


def _hgrn_fwd(proj, logits, gain, tb=1024, rider=None):
    S = proj.shape[0]
    H, C = HGRN_HEADS, HGRN_CHUNK
    ncb = tb // C

    def body(q_ref, f_ref, i_ref, g_ref, lg_ref, gn_ref, o_ref, og_ref, st_ref, state):
        @pl.when(pl.program_id(1) == 0)
        def _():
            state[...] = jnp.zeros_like(state)

        lb = _lower_bound(lg_ref)
        causal = lax.broadcasted_iota(jnp.int32, (C, C), 1) <= lax.broadcasted_iota(jnp.int32, (C, C), 0)
        qz, fz, gz = q_ref[...], f_ref[...], g_ref[...]
        _, _, key, b, r, bl, bl_c, _, qy = _block_gates(qz, fz, lb, ncb)
        qs = _chunks((qy * jnp.exp(b - r)).astype(BF16), ncb)
        ks = _chunks((key * jnp.exp(r - b)).astype(BF16), ncb)
        qb = _chunks((qy * jnp.exp(b)).astype(BF16), ncb)
        ke = _chunks((key * jnp.exp(bl - b)).astype(BF16), ncb)
        vb = _chunks(i_ref[...].astype(BF16), ncb)
        o_intra, upd = [], []
        for c in range(ncb):
            a = jnp.where(causal, _dot_nt(qs[c], ks[c]), 0.0).astype(BF16)
            o_intra.append(_dot(a, vb[c]))
            upd.append(_dot_tn(vb[c], ke[c]))
        st = state[...]
        for c in range(ncb):
            st_ref[c] = st
            st = st * jnp.exp(bl_c[c]) + upd[c]
        state[...] = st
        o = _rows([_dot_nt(qb[c], st_ref[c].astype(BF16)) + o_intra[c] for c in range(ncb)])
        o_ref[...] = o
        og_ref[...] = ((o * _rstd(o) * gn_ref[...]) * (gz * _sigmoid(gz))).astype(BF16)

    def part(p):
        return pl.BlockSpec((tb, HEAD), functools.partial(lambda h, i, p: (i, p * H + h), p=p))

    nb = S // tb
    r_ops, r_in, r_out, r_shape, r_scr = _rider_args(rider)
    res = pl.pallas_call(
        _ride(rider, body, 6, 3, functools.partial(_grid_corner, 0, 0), functools.partial(_grid_corner, H - 1, nb - 1)),
        name="hgrn_fwd", grid=(H, nb),
        in_specs=[part(0), part(1), part(2), part(3),
                  pl.BlockSpec((3, HEAD), lambda h, i: (0, h)),
                  pl.BlockSpec((1, HEAD), lambda h, i: (0, 0))] + r_in,
        out_specs=[pl.BlockSpec((tb, HEAD), lambda h, i: (i, h)),
                   pl.BlockSpec((tb, HEAD), lambda h, i: (i, h)),
                   pl.BlockSpec((None, ncb, HEAD, HEAD), lambda h, i: (h, i, 0, 0))] + r_out,
        out_shape=[jax.ShapeDtypeStruct((S, H * HEAD), F32),
                   jax.ShapeDtypeStruct((S, H * HEAD), BF16),
                   jax.ShapeDtypeStruct((H, S // C, HEAD, HEAD), F32)] + r_shape,
        scratch_shapes=[pltpu.VMEM((HEAD, HEAD), F32)] + r_scr,
        compiler_params=_params(("arbitrary", "arbitrary")))(proj, proj, proj, proj, logits, gain, *r_ops)
    return res[:3], res[3:]


def _hgrn_bwd(proj, logits, gain, o, states, dog, tb=1024, rider=None):
    S = proj.shape[0]
    H, C = HGRN_HEADS, HGRN_CHUNK
    ncb = tb // C
    nb = S // tb

    def body(q_ref, f_ref, i_ref, g_ref, lg_ref, gn_ref, o_ref, st_ref, dog_ref,
             dp_ref, dlb_ref, dgn_ref, dstate, dst_scr):
        @pl.when(pl.program_id(1) == 0)
        def _():
            dstate[...] = jnp.zeros_like(dstate)
            dlb_ref[...] = jnp.zeros_like(dlb_ref)
            dgn_ref[...] = jnp.zeros_like(dgn_ref)

        lb = _lower_bound(lg_ref)
        oml = 1.0 - lb
        gn = gn_ref[...]
        row = lax.broadcasted_iota(jnp.int32, (C, C), 0)
        col = lax.broadcasted_iota(jnp.int32, (C, C), 1)
        causal = col <= row
        tri_up = (col >= row).astype(F32)
        qz, fz, gz = q_ref[...], f_ref[...], g_ref[...]
        sig, fg, key, b, r, bl, bl_c, sq, qy = _block_gates(qz, fz, lb, ncb)
        e_br, e_rb, e_b, e_lb = jnp.exp(b - r), jnp.exp(r - b), jnp.exp(b), jnp.exp(bl - b)
        qs_v, ks_v = (qy * e_br).astype(BF16), (key * e_rb).astype(BF16)
        qb_v, ke_v = (qy * e_b).astype(BF16), (key * e_lb).astype(BF16)
        qs, ks, qb, ke = _chunks(qs_v, ncb), _chunks(ks_v, ncb), _chunks(qb_v, ncb), _chunks(ke_v, ncb)
        vb = _chunks(i_ref[...].astype(BF16), ncb)
        ov = o_ref[...]
        rs = _rstd(ov)
        xh = ov * rs
        sg = _sigmoid(gz)
        dog_v = dog_ref[...]
        dgz = dog_v * (xh * gn) * (sg * (1.0 + gz * (1.0 - sg)))
        don = dog_v * (gz * sg)
        dgn_ref[...] += jnp.sum(don * xh, axis=0, keepdims=True)
        dyg = don * gn
        do = rs * (dyg - xh * jnp.mean(dyg * xh, axis=-1, keepdims=True))
        dob = _chunks(do.astype(BF16), ncb)
        dv_in, dqs, dks, wst = [], [], [], []
        for c in range(ncb):
            a = jnp.where(causal, _dot_nt(qs[c], ks[c]), 0.0).astype(BF16)
            da = jnp.where(causal, _dot_nt(dob[c], vb[c]), 0.0).astype(BF16)
            dv_in.append(_dot_tn(a, dob[c]))
            dqs.append(_dot(da, ks[c]))
            dks.append(_dot_tn(da, qs[c]))
            wst.append(_dot_tn(dob[c], qb[c]))
        e_l = [jnp.exp(v) for v in bl_c]
        dst = dstate[...]
        for c in reversed(range(ncb)):
            dst_scr[c] = dst
            dst = wst[c] + dst * e_l[c]
        dstate[...] = dst
        dv, dqb, dke, dbl_st = [], [], [], []
        for c in range(ncb):
            dst1 = dst_scr[c]
            st0 = st_ref[c]
            dst1b = dst1.astype(BF16)
            dv.append(dv_in[c] + _dot_nt(ke[c], dst1b))
            dqb.append(_dot(dob[c], st0.astype(BF16)))
            dke.append(_dot(vb[c], dst1b))
            dbl_st.append(jnp.sum(dst1 * st0, axis=0, keepdims=True) * e_l[c])
        dqs, dks, dqb, dke, dv = _rows(dqs), _rows(dks), _rows(dqb), _rows(dke), _rows(dv)
        dke_ke = dke * ke_v.astype(F32)
        db = dqs * qs_v.astype(F32) - dks * ks_v.astype(F32) + dqb * qb_v.astype(F32) - dke_ke
        dlg = []
        for c, (db_c, kk_c) in enumerate(zip(_chunks(db, ncb), _chunks(dke_ke, ncb))):
            dbl = jnp.sum(kk_c, axis=0, keepdims=True) + dbl_st[c]
            dlg.append(_dot_exact(tri_up, db_c) + dbl)
        dlg = _rows(dlg)
        dkey = dks * e_rb + dke * e_lb
        dqy = dqs * e_br + dqb * e_b
        dfg = dlg / fg - dkey
        dlb_ref[...] += jnp.sum(dfg * (1.0 - sig), axis=0, keepdims=True)
        dp_ref[0] = (dqy * (sq * (1.0 + qz * (1.0 - sq)))).astype(BF16)
        dp_ref[1] = (dfg * oml * sig * (1.0 - sig)).astype(BF16)
        dp_ref[2] = dv.astype(BF16)
        dp_ref[3] = dgz.astype(BF16)

    def part(p):
        return pl.BlockSpec((tb, HEAD), functools.partial(lambda h, i, p: (nb - 1 - i, p * H + h), p=p))

    blk = pl.BlockSpec((tb, HEAD), lambda h, i: (nb - 1 - i, h))
    acc = pl.BlockSpec((None, 1, HEAD), lambda h, i: (h, 0, 0))
    r_ops, r_in, r_out, r_shape, r_scr = _rider_args(rider)
    res = pl.pallas_call(
        _ride(rider, body, 9, 3, functools.partial(_grid_corner, 0, 0), functools.partial(_grid_corner, H - 1, nb - 1)),
        name="hgrn_bwd", grid=(H, nb),
        in_specs=[part(0), part(1), part(2), part(3),
                  pl.BlockSpec((3, HEAD), lambda h, i: (0, h)),
                  pl.BlockSpec((1, HEAD), lambda h, i: (0, 0)),
                  blk,
                  pl.BlockSpec((None, ncb, HEAD, HEAD), lambda h, i: (h, nb - 1 - i, 0, 0)),
                  blk] + r_in,
        out_specs=[pl.BlockSpec((4, tb, HEAD), lambda h, i: (0, nb - 1 - i, h)), acc, acc] + r_out,
        out_shape=[jax.ShapeDtypeStruct((4, S, H * HEAD), BF16),
                   jax.ShapeDtypeStruct((H, 1, HEAD), F32),
                   jax.ShapeDtypeStruct((H, 1, HEAD), F32)] + r_shape,
        scratch_shapes=[pltpu.VMEM((HEAD, HEAD), F32), pltpu.VMEM((ncb, HEAD, HEAD), F32)] + r_scr,
        compiler_params=_params(("arbitrary", "arbitrary")))(
            proj, proj, proj, proj, logits, gain, o, states, dog, *r_ops)
    return res[:3], res[3:]


def _rope(v, cos, sin):
    return v * cos + pltpu.roll(v, HEAD // 2, 1) * sin


def _band_masks():
    qi = lax.broadcasted_iota(jnp.int32, (ATTN_SPAN, ATTN_SPAN), 0)
    kj = lax.broadcasted_iota(jnp.int32, (ATTN_SPAN, ATTN_SPAN), 1)
    return kj <= qi, kj >= qi


def _attn_fwd(a):
    d, L, _ = a.shape
    nb = L // ATTN_SPAN
    scale = HEAD ** -0.5

    def body(q_ref, kc_ref, kp_ref, vc_ref, vp_ref, o_ref, lse_ref):
        n = pl.program_id(1)
        mask_c, mask_p0 = _band_masks()
        mask_p = jnp.logical_and(mask_p0, n > 0)
        heads = [slice(hh * HEAD, (hh + 1) * HEAD) for hh in range(HEADS_PER_GROUP)]
        s_c = [jnp.where(mask_c, _dot_nt(q_ref[:, c], kc_ref[:, c]) * scale, NEG) for c in heads]
        s_p = [jnp.where(mask_p, _dot_nt(q_ref[:, c], kp_ref[:, c]) * scale, NEG) for c in heads]
        m = [jnp.maximum(jnp.max(a, axis=-1, keepdims=True), jnp.max(b, axis=-1, keepdims=True)) for a, b in zip(s_c, s_p)]
        p_c = [jnp.exp(a - mm) for a, mm in zip(s_c, m)]
        p_p = [jnp.exp(b - mm) for b, mm in zip(s_p, m)]
        l = [jnp.sum(a, axis=-1, keepdims=True) + jnp.sum(b, axis=-1, keepdims=True) for a, b in zip(p_c, p_p)]
        acc = [_dot(a.astype(BF16), vc_ref[:, c]) + _dot(b.astype(BF16), vp_ref[:, c]) for a, b, c in zip(p_c, p_p, heads)]
        for c, a, ll, mm in zip(heads, acc, l, m):
            o_ref[:, c] = a / ll
            lse_ref[:, c] = jnp.broadcast_to(mm + jnp.log(ll), (ATTN_SPAN, HEAD))

    def blk(part, prev):
        if prev:
            return pl.BlockSpec((None, ATTN_SPAN, GROUP_W), functools.partial(lambda r, n, p: (r, jnp.maximum(n - 1, 0), p), p=part))
        return pl.BlockSpec((None, ATTN_SPAN, GROUP_W), functools.partial(lambda r, n, p: (r, n, p), p=part))

    out = pl.BlockSpec((None, ATTN_SPAN, GROUP_W), lambda r, n: (r, n, 0))
    return pl.pallas_call(
        body, name=f"attn_fwd_d{d}", grid=(d, nb),
        in_specs=[blk(0, False), blk(1, False), blk(1, True), blk(2, False), blk(2, True)],
        out_specs=[out, out],
        out_shape=[jax.ShapeDtypeStruct((d, L, GROUP_W), F32), jax.ShapeDtypeStruct((d, L, GROUP_W), F32)],
        compiler_params=_params(("parallel", "arbitrary")))(a, a, a, a, a)


def _attn_bwd(a, do, lse, dd):
    d, L, _ = a.shape
    nb = L // ATTN_SPAN
    scale = HEAD ** -0.5

    def body(qc_ref, qn_ref, kp_ref, kc_ref, vp_ref, vc_ref, doc_ref, don_ref, lc_ref, ln_ref, ddc_ref, ddn_ref, da_ref):
        n = pl.program_id(1)
        mask_c, mask_p0 = _band_masks()
        mask_p = jnp.logical_and(mask_p0, n > 0)
        mask_n = jnp.logical_and(mask_p0, n < nb - 1)
        H4 = range(HEADS_PER_GROUP)
        heads = [slice(hh * HEAD, (hh + 1) * HEAD) for hh in H4]
        do_c = [doc_ref[:, c].astype(BF16) for c in heads]
        do_n = [don_ref[:, c].astype(BF16) for c in heads]
        p_c = [jnp.where(mask_c, jnp.exp(_dot_nt(qc_ref[:, c], kc_ref[:, c]) * scale - lc_ref[:, c]), 0.0) for c in heads]
        p_p = [jnp.where(mask_p, jnp.exp(_dot_nt(qc_ref[:, c], kp_ref[:, c]) * scale - lc_ref[:, c]), 0.0) for c in heads]
        p_n = [jnp.where(mask_n, jnp.exp(_dot_nt(qn_ref[:, c], kc_ref[:, c]) * scale - ln_ref[:, c]), 0.0) for c in heads]
        ds_c = [(p_c[i] * (_dot_nt(do_c[i], vc_ref[:, heads[i]]) + ddc_ref[:, heads[i]])).astype(BF16) for i in H4]
        ds_p = [(p_p[i] * (_dot_nt(do_c[i], vp_ref[:, heads[i]]) + ddc_ref[:, heads[i]])).astype(BF16) for i in H4]
        ds_n = [(p_n[i] * (_dot_nt(do_n[i], vc_ref[:, heads[i]]) + ddn_ref[:, heads[i]])).astype(BF16) for i in H4]
        dq = [(_dot(ds_c[i], kc_ref[:, heads[i]]) + _dot(ds_p[i], kp_ref[:, heads[i]])) * scale for i in H4]
        dk = [(_dot_tn(ds_c[i], qc_ref[:, heads[i]]) + _dot_tn(ds_n[i], qn_ref[:, heads[i]])) * scale for i in H4]
        dv = [_dot_tn(p_c[i].astype(BF16), do_c[i]) + _dot_tn(p_n[i].astype(BF16), do_n[i]) for i in H4]
        for i in H4:
            da_ref[:, heads[i]] = dq[i]
            da_ref[:, GROUP_W + i * HEAD:GROUP_W + (i + 1) * HEAD] = dk[i]
            da_ref[:, 2 * GROUP_W + i * HEAD:2 * GROUP_W + (i + 1) * HEAD] = dv[i]

    def rel(delta):
        if delta < 0:
            return lambda n: jnp.maximum(n - 1, 0)
        if delta > 0:
            return lambda n: jnp.minimum(n + 1, nb - 1)
        return lambda n: n

    def blk(width, part, delta):
        f = rel(delta)
        return pl.BlockSpec((None, ATTN_SPAN, width), functools.partial(lambda r, n, p, f: (r, f(n), p), p=part, f=f))

    g = GROUP_W
    return pl.pallas_call(
        body, name=f"attn_bwd_d{d}", grid=(d, nb),
        in_specs=[blk(g, 0, 0), blk(g, 0, 1), blk(g, 1, -1), blk(g, 1, 0), blk(g, 2, -1), blk(g, 2, 0),
                  blk(g, 0, 0), blk(g, 0, 1), blk(g, 0, 0), blk(g, 0, 1), blk(g, 0, 0), blk(g, 0, 1)],
        out_specs=pl.BlockSpec((None, ATTN_SPAN, 3 * g), lambda r, n: (r, n, 0)),
        out_shape=jax.ShapeDtypeStruct((d, L, 3 * g), F32),
        compiler_params=_params(("parallel", "arbitrary")))(
            a, a, a, a, a, a, do, do, lse, lse, dd, dd)


def _group_weights(lse_refs, cols):
    ls = [r[:, cols] for r in lse_refs]
    mx = jnp.maximum(jnp.maximum(ls[0], ls[1]), ls[2])
    es = [jnp.exp(v - mx) for v in ls]
    tot = es[0] + es[1] + es[2]
    return [e / tot for e in es]


def _gather_tokens(ref, scr, d, tm):
    if d == 1:
        return ref.at[0]
    for r in range(d):
        scr[pl.ds(r, tm // d, stride=d), :] = ref[r]
    return scr


def _scatter_tokens(scr, ref, d, tm):
    if d == 1:
        ref[0] = scr[...]
        return
    for r in range(d):
        ref[r] = scr[pl.ds(r, tm // d, stride=d), :]


def _head_spec(d, tm):
    return pl.BlockSpec((d, tm // d, HEAD), lambda i, j: (0, i, j))


def _qkv_dilated(h, gain, wg, cos, sin, d, tm=512):
    S, K = h.shape

    def body(h_ref, g_ref, w_ref, cos_ref, sin_ref, out_ref, u_ref, y_scr):
        p = pl.program_id(1)

        @pl.when(p == 0)
        def _():
            v = h_ref[...]
            u_ref[...] = (v * _rstd(v) * g_ref[...]).astype(BF16)

        y = _dot(u_ref[...], w_ref[...])
        for hh in range(HEADS_PER_GROUP):
            cols = slice(hh * HEAD, (hh + 1) * HEAD)
            y_scr[...] = y[:, cols]

            @pl.when(p < 2)
            def _():
                for r in range(d):
                    rows = slice(None) if d == 1 else pl.ds(r, tm // d, stride=d)
                    out_ref[r, :, cols] = _rope(y_scr[rows, :], cos_ref[rows, :], sin_ref[rows, :]).astype(BF16)

            @pl.when(p == 2)
            def _():
                for r in range(d):
                    rows = slice(None) if d == 1 else pl.ds(r, tm // d, stride=d)
                    out_ref[r, :, cols] = y_scr[rows, :].astype(BF16)

    tab = pl.BlockSpec((tm, HEAD), lambda i, p: (i, 0))
    return pl.pallas_call(
        body, name=f"attn_qkv_d{d}", grid=(S // tm, 3),
        in_specs=[pl.BlockSpec((tm, K), lambda i, p: (i, 0)),
                  pl.BlockSpec((1, K), lambda i, p: (0, 0)),
                  pl.BlockSpec((K, GROUP_W), lambda i, p: (0, p)), tab, tab],
        out_specs=[pl.BlockSpec((d, tm // d, GROUP_W), lambda i, p: (0, i, p)), pl.BlockSpec((tm, K), lambda i, p: (i, 0))],
        out_shape=[jax.ShapeDtypeStruct((d, S // d, 3 * GROUP_W), BF16), jax.ShapeDtypeStruct((S, K), BF16)],
        scratch_shapes=[pltpu.VMEM((tm, HEAD), F32)],
        compiler_params=_params(("parallel", "arbitrary")))(h, gain, wg, cos, sin)


def _undilate_group(da, dqkv, cos, sin, g, tm=1024):
    d, L, _ = da.shape
    S = d * L
    G = len(ATTN_GROUPS)

    def body(*refs):
        da_ref, cos_ref, sin_ref, out_ref, scr = refs[0], refs[1], refs[2], refs[-2], refs[-1]
        tok = _gather_tokens(da_ref, scr, d, tm)[...]
        rotate = pl.program_id(1) < 2 * HEADS_PER_GROUP
        out_ref[...] = jnp.where(rotate, _rope(tok, cos_ref[...], -sin_ref[...]), tok).astype(BF16)

    def dst(i, j):
        return i, ((j // HEADS_PER_GROUP) * G + g) * HEADS_PER_GROUP + j % HEADS_PER_GROUP

    tab = pl.BlockSpec((tm, HEAD), lambda i, j: (i, 0))
    operands = (da, cos, sin) if dqkv is None else (da, cos, sin, dqkv)
    return pl.pallas_call(
        body, name=f"attn_undilate_d{d}", grid=(S // tm, 3 * HEADS_PER_GROUP),
        in_specs=[_head_spec(d, tm), tab, tab] + ([] if dqkv is None else [ANY]),
        out_specs=pl.BlockSpec((tm, HEAD), dst),
        out_shape=jax.ShapeDtypeStruct((S, 3 * G * GROUP_W), BF16),
        input_output_aliases={} if dqkv is None else {3: 0},
        scratch_shapes=[pltpu.VMEM((tm, HEAD), F32)],
        compiler_params=_params(("parallel", "arbitrary")))(*operands)


def _attn_merge(os_, lses, tm=512):
    G = len(os_)
    S = os_[0].shape[0] * os_[0].shape[1]

    def body(*refs):
        o_refs, l_refs, out_ref = refs[:G], refs[G:2 * G], refs[2 * G]
        scr = refs[2 * G + 1:]
        hh = pl.program_id(1)
        o_tok = [_gather_tokens(o_refs[g], scr[g], d, tm) for g, (_, d) in enumerate(ATTN_GROUPS)]
        l_tok = [_gather_tokens(l_refs[g], scr[G + g], d, tm) for g, (_, d) in enumerate(ATTN_GROUPS)]
        al = _group_weights(l_tok, slice(None))
        for g in range(G):
            cols = pl.ds(pl.multiple_of(g * GROUP_W + hh * HEAD, HEAD), HEAD)
            out_ref[:, cols] = (o_tok[g][...] * al[g]).astype(BF16)

    specs = [_head_spec(d, tm) for _, d in ATTN_GROUPS]
    return pl.pallas_call(
        body, name="attn_merge", grid=(S // tm, HEADS_PER_GROUP),
        in_specs=specs + specs,
        out_specs=pl.BlockSpec((tm, G * GROUP_W), lambda i, j: (i, 0)),
        out_shape=jax.ShapeDtypeStruct((S, G * GROUP_W), BF16),
        scratch_shapes=[pltpu.VMEM((tm, HEAD), F32)] * (2 * G),
        compiler_params=_params(("parallel", "arbitrary")))(*os_, *lses)


def _attn_merge_bwd(os_, lses, doa, tm=512):
    G = len(os_)
    S = doa.shape[0]

    def body(*refs):
        o_refs, l_refs, doa_ref = refs[:G], refs[G:2 * G], refs[2 * G]
        do_refs, dd_refs = refs[2 * G + 1:3 * G + 1], refs[3 * G + 1:4 * G + 1]
        scr = refs[4 * G + 1:]
        hh = pl.program_id(1)
        o_tok = [_gather_tokens(o_refs[g], scr[g], d, tm) for g, (_, d) in enumerate(ATTN_GROUPS)]
        l_tok = [_gather_tokens(l_refs[g], scr[G + g], d, tm) for g, (_, d) in enumerate(ATTN_GROUPS)]
        do_tok, dd_tok = scr[2 * G:3 * G], scr[3 * G:]
        al = _group_weights(l_tok, slice(None))
        mix = None
        for g in range(G):
            dg = doa_ref[:, pl.ds(pl.multiple_of(g * GROUP_W + hh * HEAD, HEAD), HEAD)]
            do_tok[g][...] = dg * al[g]
            t = al[g] * jnp.sum(dg * o_tok[g][...], axis=-1, keepdims=True)
            mix = t if mix is None else mix + t
        for g, (_, d) in enumerate(ATTN_GROUPS):
            dd_tok[g][...] = jnp.broadcast_to(-al[g] * mix, (tm, HEAD))
            _scatter_tokens(do_tok[g], do_refs[g], d, tm)
            _scatter_tokens(dd_tok[g], dd_refs[g], d, tm)

    specs = [_head_spec(d, tm) for _, d in ATTN_GROUPS]
    shapes = [jax.ShapeDtypeStruct((d, S // d, GROUP_W), F32) for _, d in ATTN_GROUPS]
    return pl.pallas_call(
        body, name="attn_merge_bwd", grid=(S // tm, HEADS_PER_GROUP),
        in_specs=specs + specs + [pl.BlockSpec((tm, G * GROUP_W), lambda i, j: (i, 0))],
        out_specs=specs + specs,
        out_shape=shapes + shapes,
        scratch_shapes=[pltpu.VMEM((tm, HEAD), F32)] * (4 * G),
        compiler_params=_params(("parallel", "arbitrary")))(*os_, *lses, doa)


def _rope_tables(S):
    inv_freq = 1.0 / (ROPE_THETA ** (jnp.arange(0, HEAD, 2, dtype=F32) / HEAD))
    ang = jnp.arange(S, dtype=F32)[:, None] * inv_freq[None, :]
    cos, sin = jnp.cos(ang), jnp.sin(ang)
    return jnp.concatenate([cos, cos], axis=-1), jnp.concatenate([-sin, sin], axis=-1)


def _local_step(x, target, norm_mix, norm_ffn, lb_logits, out_gain, final_norm, comm):
    S = x.shape[0]
    nm0, nm1 = norm_mix[0:1], norm_mix[1:2]
    nf0, nf1 = norm_ffn[0:1], norm_ffn[1:2]
    w = comm.first_weights()

    proj, u0, got = _norm_mm(x, nm0, w["hin"], "hgrn_in", rider=comm.gather_rider(LATE_WEIGHTS_A))
    w.update(comm.gathered(LATE_WEIGHTS_A, got))
    (o, og, states), got = _hgrn_fwd(proj, lb_logits, out_gain, rider=comm.gather_rider(LATE_WEIGHTS_B))
    w.update(comm.gathered(LATE_WEIGHTS_B, got))
    fin_tn = w["fin0"].shape[2]
    h1 = _mm_res(x, og, w["hout"], "hgrn_out")
    z0, u1, _ = _norm_mm(h1, nf0, w["fin0"], "ffn0_in", out_dtype=BF16)
    h2, act0 = _swiglu_mm_res(h1, z0, w["fdn0"], "ffn0_down")
    cos, sin = _rope_tables(S)
    G = len(ATTN_GROUPS)
    w_groups = w["qkv"].transpose(1, 0, 2).reshape(D_MODEL, 3, G, GROUP_W)
    a_g, u2 = zip(*[_qkv_dilated(h2, nm1, w_groups[:, :, gi, :].reshape(D_MODEL, 3 * GROUP_W), cos, sin, d)
                    for gi, (_, d) in enumerate(ATTN_GROUPS)])
    o_g, lse_g = zip(*[_attn_fwd(a) for a in a_g])
    oa = _attn_merge(o_g, lse_g)
    h3 = _mm_res(h2, oa, w["aout"], "attn_out")
    z1, u3, _ = _norm_mm(h3, nf1, w["fin1"], "ffn1_in", out_dtype=BF16)
    h4, act1 = _swiglu_mm_res(h3, z1, w["fdn1"], "ffn1_down")
    dh4, loss, d_final = _loss_head(h4, final_norm, target)

    grads, small = {}, {"final_norm": d_final}

    def ffn_bwd(dh, h_in, u_in, z, act, gain, w_in, w_dn, tag):
        dz = _mm_nt_swiglu_bwd(dh, w_dn, z, tag + "_down_dx")
        g_dn = _mm_tn(act, dh, 1, D_MODEL, D_MODEL, tag + "_down_dw")
        g_in = _mm_tn(u_in, dz, N_CHIPS, fin_tn, fin_tn, tag + "_in_dw")
        dh_in, dgain = _mm_nt_normbwd(dz, w_in, h_in, gain, dh, tag + "_in_dx")
        return dh_in, dgain, g_in, g_dn[0]

    dh3, d_nf1, grads["fin1"], grads["fdn1"] = ffn_bwd(dh4, h3, u3, z1, act1, nf1, w["fin1"], w["fdn1"], "ffn1")
    doa = _mm_nt(dh3, w["aout"][None], "attn_out_dx")
    grads["aout"] = _mm_tn(oa, dh3, 1, D_MODEL, D_MODEL, "attn_out_dw")[0]
    merged = _attn_merge_bwd(o_g, lse_g, doa)
    G = len(ATTN_GROUPS)
    das = [_attn_bwd(a_g[gi], merged[gi], lse_g[gi], merged[G + gi]) for gi in range(G)]
    dqkv = None
    for gi in range(G):
        dqkv = _undilate_group(das[gi], dqkv, cos, sin, gi)
    n_qkv = w["qkv"].shape[2]
    grads["qkv"] = _mm_tn(u2[0], dqkv, N_CHIPS, n_qkv, n_qkv, "attn_qkv_dw")
    dh2, d_nm1 = _mm_nt_normbwd(dqkv, w["qkv"], h2, nm1, dh3, "attn_qkv_dx")

    dh1, d_nf0, grads["fin0"], grads["fdn0"] = ffn_bwd(dh2, h1, u1, z0, act0, nf0, w["fin0"], w["fdn0"], "ffn0")
    dog = _mm_nt(dh1, w["hout"][None], "hgrn_out_dx")
    rider = comm.exchange_rider({k: grads.pop(k) for k in EARLY_GRADS})
    (dproj, dlb, dgn), got = _hgrn_bwd(proj, lb_logits, out_gain, o, states, dog, rider=rider)
    comm.exchanged(got)
    grads["hout"] = _mm_tn(og, dh1, 1, D_MODEL, D_MODEL, "hgrn_out_dw")[0]
    grads["hin"] = _mm_tn(u0, dproj, N_CHIPS, D_MODEL, D_MODEL, "hgrn_in_dw")
    dx, d_nm0 = _mm_nt_normbwd(dproj, w["hin"], x, nm0, dh1, "hgrn_in_dx")

    small["norm_mix"] = jnp.concatenate([d_nm0, d_nm1], axis=0)
    small["norm_ffn"] = jnp.concatenate([d_nf0, d_nf1], axis=0)
    small["lb"] = dlb.reshape(1, HGRN_HEADS * HEAD)
    small["out_norm"] = dgn.reshape(HGRN_HEADS, HEAD)
    return loss, dx, grads, small


def _place():
    x, y, c = lax.axis_index("x"), lax.axis_index("y"), lax.axis_index("c")
    others = [(1 - x, y), (x, 1 - y), (1 - x, 1 - y)]
    return x, y, c, others


ANY = pl.BlockSpec(memory_space=pl.ANY)


class _GatherRider:
    def __init__(self, shards):
        self.operands = list(shards)
        n = self.n = len(shards)
        self.out_shape = [jax.ShapeDtypeStruct((N_CHIPS,) + s.shape, s.dtype) for s in shards]
        self.scratch = [pltpu.SemaphoreType.DMA((3 * n,)), pltpu.SemaphoreType.DMA((3 * n,)),
                        pltpu.SemaphoreType.DMA((3 * n,)), pltpu.SemaphoreType.DMA((3 * n,)),
                        pltpu.SemaphoreType.DMA((n,)), pltpu.SemaphoreType.DMA((n,))]

    def _copies(self, ins, outs, sems):
        ici_send, ici_recv, _, _, own_send, own_recv = sems
        x, y, c, others = _place()
        me = 2 * x + y
        own = [pltpu.make_async_remote_copy(
            src_ref=ins[a], dst_ref=outs[a].at[me], send_sem=own_send.at[a], recv_sem=own_recv.at[a],
            device_id=(x, y, 1 - c), device_id_type=MESH) for a in range(self.n)]
        sends = [pltpu.make_async_remote_copy(
            src_ref=ins[a].at[c], dst_ref=outs[a].at[me, c], send_sem=ici_send.at[a * 3 + k], recv_sem=ici_recv.at[a * 3 + k],
            device_id=(ox, oy, c), device_id_type=MESH) for a in range(self.n) for k, (ox, oy) in enumerate(others)]
        return own, sends

    def start(self, ins, outs, sems):
        own, sends = self._copies(ins, outs, sems)
        for cp in own + sends:
            cp.start()

    def finish(self, ins, outs, sems):
        ici_send, ici_recv, d2d_send, d2d_recv, _, _ = sems
        x, y, c, others = _place()
        sibling = (x, y, 1 - c)
        own, sends = self._copies(ins, outs, sems)
        passes = []
        for a in range(self.n):
            for k, (ox, oy) in enumerate(others):
                s = a * 3 + k
                got = outs[a].at[2 * ox + oy, c]
                pltpu.make_async_remote_copy(
                    src_ref=got, dst_ref=got, send_sem=ici_send.at[s], recv_sem=ici_recv.at[s],
                    device_id=(ox, oy, c), device_id_type=MESH).wait_recv()
                fwd = pltpu.make_async_remote_copy(
                    src_ref=got, dst_ref=got, send_sem=d2d_send.at[s], recv_sem=d2d_recv.at[s],
                    device_id=sibling, device_id_type=MESH)
                fwd.start()
                passes.append(fwd)
        for a in range(self.n):
            for k, (ox, oy) in enumerate(others):
                s = a * 3 + k
                theirs = outs[a].at[2 * ox + oy, 1 - c]
                pltpu.make_async_remote_copy(
                    src_ref=theirs, dst_ref=theirs, send_sem=d2d_send.at[s], recv_sem=d2d_recv.at[s],
                    device_id=sibling, device_id_type=MESH).wait_recv()
        for cp in own:
            cp.wait()
        for cp in sends + passes:
            cp.wait_send()


class _ExchangeRider:
    def __init__(self, parts):
        self.operands = list(parts)
        n = self.n = len(parts)
        self.out_shape = [jax.ShapeDtypeStruct(p.shape, p.dtype) for p in parts]
        self.scratch = [pltpu.SemaphoreType.DMA((3 * n,)), pltpu.SemaphoreType.DMA((3 * n,))]

    def _copies(self, ins, outs, sems):
        send_sem, recv_sem = sems
        x, y, c, others = _place()
        me = 2 * x + y
        return [pltpu.make_async_remote_copy(
            src_ref=ins[a].at[2 * ox + oy], dst_ref=outs[a].at[me], send_sem=send_sem.at[a * 3 + k],
            recv_sem=recv_sem.at[a * 3 + k], device_id=(ox, oy, c), device_id_type=MESH)
            for a in range(self.n) for k, (ox, oy) in enumerate(others)]

    def start(self, ins, outs, sems):
        for cp in self._copies(ins, outs, sems):
            cp.start()

    def finish(self, ins, outs, sems):
        send_sem, recv_sem = sems
        x, y, c, others = _place()
        for a in range(self.n):
            for k, (ox, oy) in enumerate(others):
                s = a * 3 + k
                got = outs[a].at[2 * ox + oy]
                pltpu.make_async_remote_copy(
                    src_ref=got, dst_ref=got, send_sem=send_sem.at[s], recv_sem=recv_sem.at[s],
                    device_id=(ox, oy, c), device_id_type=MESH).wait_recv()
        for cp in self._copies(ins, outs, sems):
            cp.wait_send()


def _run_rider(rider, name):
    n = rider.n

    def body(*refs):
        ins, outs, sems = refs[:n], refs[n:2 * n], refs[2 * n:]
        rider.start(ins, outs, sems)
        rider.finish(ins, outs, sems)

    return pl.pallas_call(
        body, name=name, in_specs=[ANY] * n, out_specs=[ANY] * n,
        out_shape=rider.out_shape, scratch_shapes=rider.scratch)(*rider.operands)


def _ride(rider, body, n_in, n_out, first, last):
    if rider is None:
        return body
    n = rider.n

    def wrapped(*refs):
        host_in, r_in = refs[:n_in], refs[n_in:n_in + n]
        host_out = refs[n_in + n:n_in + n + n_out]
        r_out = refs[n_in + n + n_out:n_in + 2 * n + n_out]
        rest = refs[n_in + 2 * n + n_out:]
        host_scr, sems = rest[:len(rest) - len(rider.scratch)], rest[len(rest) - len(rider.scratch):]

        @pl.when(first())
        def _():
            rider.start(r_in, r_out, sems)

        body(*host_in, *host_out, *host_scr)

        @pl.when(last())
        def _():
            rider.finish(r_in, r_out, sems)

    return wrapped


def _rider_args(rider):
    if rider is None:
        return [], [], [], [], []
    return rider.operands, [ANY] * rider.n, [ANY] * rider.n, rider.out_shape, rider.scratch


def _pair_exchange(grads, name):
    n = len(grads)

    def body(*refs):
        ins, outs = refs[:n], refs[n:2 * n]
        send_sem, recv_sem = refs[2 * n:]
        x, y, c, _ = _place()
        sibling = (x, y, 1 - c)
        cps = []
        for a in range(n):
            for j in range(N_CHIPS):
                s = a * N_CHIPS + j
                cps.append(pltpu.make_async_remote_copy(
                    src_ref=ins[a].at[j, 1 - c], dst_ref=outs[a].at[j], send_sem=send_sem.at[s], recv_sem=recv_sem.at[s],
                    device_id=sibling, device_id_type=MESH))
        for cp in cps:
            cp.start()
        for cp in cps:
            cp.wait()

    return pl.pallas_call(
        body, name=name,
        in_specs=[ANY] * n, out_specs=[ANY] * n,
        out_shape=[jax.ShapeDtypeStruct((N_CHIPS,) + g.shape[2:], F32) for g in grads],
        scratch_shapes=[pltpu.SemaphoreType.DMA((N_CHIPS * n,)), pltpu.SemaphoreType.DMA((N_CHIPS * n,))],
        )(*grads)


def _pair_sum(g, got, c_idx):
    _, _, r, cw = g.shape
    tr = _row_tile(r, cw)

    def body(c_ref, g_ref, got_ref, p_ref, pb_ref):
        v = g_ref[...] + got_ref[...]
        p_ref[...] = v
        pb_ref[...] = v.astype(BF16)

    blk = pl.BlockSpec((None, tr, cw), lambda j, i, c_ref: (j, i, 0))
    return pl.pallas_call(
        body, name="grad_pair_sum",
        grid_spec=pltpu.PrefetchScalarGridSpec(
            num_scalar_prefetch=1, grid=(N_CHIPS, r // tr),
            in_specs=[pl.BlockSpec((None, None, tr, cw), lambda j, i, c_ref: (j, c_ref[0], i, 0)), blk],
            out_specs=[blk, blk]),
        out_shape=[jax.ShapeDtypeStruct((N_CHIPS, r, cw), F32), jax.ShapeDtypeStruct((N_CHIPS, r, cw), BF16)],
        compiler_params=_params(("parallel", "parallel")))(c_idx, g, got)


def _chip_sum(p, got, me_idx):
    _, r, cw = p.shape
    tr = _row_tile(r, cw)

    def body(me_ref, own_ref, got_ref, t_ref):
        me = me_ref[0]
        acc = None
        for s in range(N_CHIPS):
            term = jnp.where(me == s, own_ref[...], got_ref[s].astype(F32))
            acc = term if acc is None else acc + term
        t_ref[...] = acc

    return pl.pallas_call(
        body, name="grad_chip_sum",
        grid_spec=pltpu.PrefetchScalarGridSpec(
            num_scalar_prefetch=1, grid=(r // tr,),
            in_specs=[pl.BlockSpec((None, tr, cw), lambda i, me_ref: (me_ref[0], i, 0)),
                      pl.BlockSpec((N_CHIPS, tr, cw), lambda i, me_ref: (0, i, 0))],
            out_specs=pl.BlockSpec((tr, cw), lambda i, me_ref: (i, 0))),
        out_shape=jax.ShapeDtypeStruct((r, cw), F32),
        compiler_params=_params(("parallel",)))(me_idx, p, got)


def _pair_share(halves):
    n = len(halves)

    def body(*refs):
        ins, outs = refs[:n], refs[n:2 * n]
        send_sem, recv_sem = refs[2 * n:]
        x, y, c, _ = _place()
        cps = [pltpu.make_async_remote_copy(
            src_ref=ins[a], dst_ref=outs[a], send_sem=send_sem.at[a], recv_sem=recv_sem.at[a],
            device_id=(x, y, 1 - c), device_id_type=MESH) for a in range(n)]
        for cp in cps:
            cp.start()
        for cp in cps:
            cp.wait()

    return pl.pallas_call(
        body, name="grad_pair_share",
        in_specs=[ANY] * n, out_specs=[ANY] * n,
        out_shape=[jax.ShapeDtypeStruct(h.shape, F32) for h in halves],
        scratch_shapes=[pltpu.SemaphoreType.DMA((n,)), pltpu.SemaphoreType.DMA((n,))],
        )(*halves)


def _small_allreduce(pack):
    m_per, ncol = pack.shape
    n_dev = 8

    def body(x_ref, sum_ref, all_ref, send_sems, recv_sems, local_sem):
        x, y, c, others = _place()
        me, sibling = (x, y, c), (x, y, 1 - c)

        def rows(px, py, pc):
            return all_ref.at[pl.ds((4 * px + 2 * py + pc) * m_per, m_per), :]

        def copy(k, block, to, src=None):
            return pltpu.make_async_remote_copy(
                src_ref=rows(*block) if src is None else src, dst_ref=rows(*block),
                send_sem=send_sems.at[k], recv_sem=recv_sems.at[k], device_id=to, device_id_type=MESH)

        mine = pltpu.make_async_copy(x_ref, rows(*me), local_sem)
        mine.start()
        first = [copy(0, me, sibling, src=x_ref)]
        first += [copy(1 + j, me, (*chip, c), src=x_ref) for j, chip in enumerate(others)]
        for cp in first:
            cp.start()
        passed = [copy(4 + j, (*chip, c), sibling) for j, chip in enumerate(others)]
        for j, chip in enumerate(others):
            copy(1 + j, (*chip, c), me).wait_recv()
            passed[j].start()
        copy(0, sibling, me).wait_recv()
        for j, chip in enumerate(others):
            copy(4 + j, (*chip, 1 - c), me).wait_recv()
        for cp in first + passed:
            cp.wait_send()
        mine.wait()
        acc = all_ref[0:m_per, :]
        for dvc in range(1, n_dev):
            acc = acc + all_ref[dvc * m_per:(dvc + 1) * m_per, :]
        sum_ref[...] = acc

    return pl.pallas_call(
        body, name="small_allreduce",
        in_specs=[pl.BlockSpec(memory_space=pltpu.VMEM)],
        out_specs=pl.BlockSpec(memory_space=pltpu.VMEM),
        out_shape=jax.ShapeDtypeStruct((m_per, ncol), F32),
        scratch_shapes=[pltpu.VMEM((n_dev * m_per, ncol), F32),
                        pltpu.SemaphoreType.DMA((7,)), pltpu.SemaphoreType.DMA((7,)), pltpu.SemaphoreType.DMA],
        )(pack)


def _adam_math(w, g, m, v):
    m = ADAM_B1 * m + (1.0 - ADAM_B1) * g
    v = ADAM_B2 * v + (1.0 - ADAM_B2) * (g * g)
    m_hat = m / (1.0 - ADAM_B1 ** ADAM_STEP)
    v_hat = v / (1.0 - ADAM_B2 ** ADAM_STEP)
    delta = -ADAM_LR * (m_hat / (jnp.sqrt(v_hat) + ADAM_EPS) + ADAM_WD * w)
    return delta, m, v


def _adamw(halves, c_idx, w, m, v, name):
    L = len(halves)
    r, C = halves[0][0].shape
    tr = _row_tile(r, C, 512 * 1024)
    nt = r // tr

    def body(c_ref, *refs):
        g_refs, (w_ref, m_ref, v_ref), (g_ref, d_ref, nm_ref, nv_ref) = refs[:2 * L], refs[2 * L:2 * L + 3], refs[2 * L + 3:]
        own = pl.program_id(1) == c_ref[0]
        g = None
        for l in range(L):
            cand = jnp.where(own, g_refs[2 * l][...], g_refs[2 * l + 1][...])
            g = cand if g is None else jnp.where(pl.program_id(0) == l, cand, g)
        g_ref[...] = g
        d_ref[...], nm_ref[...], nv_ref[...] = _adam_math(w_ref[...], g, m_ref[...], v_ref[...])

    def half(l, mine):
        def index(ll, h, i, c_ref):
            read = (h == c_ref[0]) if mine else (h != c_ref[0])
            return jnp.where(jnp.logical_and(ll == l, read), i, 0), 0
        return pl.BlockSpec((tr, C), index)

    full = pl.BlockSpec((None, tr, C), lambda ll, h, i, c_ref: (ll, h * nt + i, 0))
    shp = jax.ShapeDtypeStruct((L, 2 * r, C), F32)
    g_specs = [half(l, mine) for l in range(L) for mine in (True, False)]
    return pl.pallas_call(
        body, name=name,
        grid_spec=pltpu.PrefetchScalarGridSpec(
            num_scalar_prefetch=1, grid=(L, 2, nt),
            in_specs=g_specs + [full] * 3, out_specs=[full] * 4),
        out_shape=[shp] * 4,
        compiler_params=_params(("arbitrary", "arbitrary", "arbitrary")))(
            c_idx, *[a for pair in halves for a in pair], w, m, v)


def _small_update(gsum, logits_pack, w, m, v):
    def body(gs_ref, lg_ref, w_ref, m_ref, v_ref, g_ref, d_ref, nm_ref, nv_ref):
        g_ref[...] = gs_ref[...]
        l0, l1, l2 = lg_ref[0:1, :], lg_ref[1:2, :], lg_ref[2:3, :]
        mx = jnp.maximum(jnp.maximum(l0, l1), l2)
        e0, e1, e2 = jnp.exp(l0 - mx), jnp.exp(l1 - mx), jnp.exp(l2 - mx)
        tot = e0 + e1 + e2
        p0, p1, p2 = e0 / tot, e1 / tot, e2 / tot
        dlb = gs_ref[4:5, :]
        g_ref[4:5, :] = dlb * p0 * (1.0 - p0)
        g_ref[5:6, :] = -dlb * p0 * p1
        g_ref[6:7, :] = -dlb * p0 * p2
        d_ref[...], nm_ref[...], nv_ref[...] = _adam_math(w_ref[...], g_ref[...], m_ref[...], v_ref[...])

    full = pl.BlockSpec(memory_space=pltpu.VMEM)
    shp = jax.ShapeDtypeStruct(gsum.shape, F32)
    return pl.pallas_call(
        body, name="small_update", in_specs=[full] * 5, out_specs=[full] * 4, out_shape=[shp] * 4)(
            gsum, logits_pack, w, m, v)


def _pack_small(norm_mix, norm_ffn, lb3, out_norm, final_norm, extra=None):
    ncol = norm_mix.shape[1]
    on = jnp.pad(out_norm.reshape(1, -1), ((0, 0), (0, ncol - out_norm.size)))
    rows = [norm_mix, norm_ffn, lb3, on, final_norm.reshape(1, ncol)]
    if extra is not None:
        rows.append(extra)
    used = sum(r.shape[0] for r in rows)
    rows.append(jnp.zeros((SMALL_ROWS - used, ncol), F32))
    return jnp.concatenate(rows, axis=0)


WEIGHT_NAMES = ("hin", "hout", "qkv", "aout", "fin0", "fin1", "fdn0", "fdn1")
FIRST_WEIGHTS = ("hin",)
LATE_WEIGHTS_A = ("hout", "fin0", "fdn0")
LATE_WEIGHTS_B = ("qkv", "aout", "fin1", "fdn1")
EARLY_GRADS = ("qkv", "aout", "fin0", "fin1", "fdn0", "fdn1")


def _split_weights(hgrn_w_in, hgrn_w_out, attn_w_qkv, attn_w_out, ffn_w_in, ffn_w_down):
    return {"hin": hgrn_w_in[0], "hout": hgrn_w_out[0], "qkv": attn_w_qkv[0], "aout": attn_w_out[0],
            "fin0": ffn_w_in[0], "fin1": ffn_w_in[1], "fdn0": ffn_w_down[0], "fdn1": ffn_w_down[1]}


def _halves(v):
    r, c = v.shape
    return v.reshape(2, r // 2, c)


def _full_weights(gathered):
    out = {}
    for k, g in gathered.items():
        _, _, r, c = g.shape
        if k in ("hin", "qkv", "fin0", "fin1"):
            out[k] = g.reshape(N_CHIPS, 2 * r, c)
        else:
            out[k] = g.reshape(N_CHIPS * 2 * r, c)
    return out


class _StepComm:
    def __init__(self, shards, c_idx, me_idx):
        self.shards, self.c_idx, self.me_idx = shards, c_idx, me_idx
        self.halves = {}

    def gather_rider(self, names):
        return _GatherRider([_halves(self.shards[k].astype(BF16)) for k in names])

    def gathered(self, names, got):
        return _full_weights(dict(zip(names, got)))

    def first_weights(self):
        return self.gathered(FIRST_WEIGHTS, _run_rider(self.gather_rider(FIRST_WEIGHTS), "gather_first"))

    def _pair_sums(self, grads, name):
        names = list(grads)
        g4 = []
        for k in names:
            r, c = self.shards[k].shape
            g4.append(grads[k].reshape(N_CHIPS, 2, r // 2, c))
        from_sibling = _pair_exchange(g4, name)
        return names, [_pair_sum(g, got, self.c_idx) for g, got in zip(g4, from_sibling)]

    def _chip_sums(self, names, sums, got):
        for k, s, g in zip(names, sums, got):
            self.halves[k] = _chip_sum(s[0], g, self.me_idx)

    def exchange_rider(self, grads):
        self._riding = self._pair_sums(grads, "grad_pair_exchange_early")
        return _ExchangeRider([s[1] for s in self._riding[1]])

    def exchanged(self, got):
        self._chip_sums(*self._riding, got)

    def reduce_rest(self, grads):
        names, sums = self._pair_sums(grads, "grad_pair_exchange_late")
        self._chip_sums(names, sums, _run_rider(_ExchangeRider([s[1] for s in sums]), "grad_chip_exchange_late"))

    def shared_halves(self):
        mine = [self.halves[k] for k in WEIGHT_NAMES]
        return dict(zip(WEIGHT_NAMES, zip(mine, _pair_share(mine))))


def kernel(x, norm_mix, norm_ffn, hgrn_w_in, hgrn_lb_logits, hgrn_out_norm, hgrn_w_out, attn_w_qkv, attn_w_out, ffn_w_in, ffn_w_down, final_norm, loss_target, m_norm_mix, m_norm_ffn, m_hgrn_w_in, m_hgrn_lb_logits, m_hgrn_out_norm, m_hgrn_w_out, m_attn_w_qkv, m_attn_w_out, m_ffn_w_in, m_ffn_w_down, m_final_norm, v_norm_mix, v_norm_ffn, v_hgrn_w_in, v_hgrn_lb_logits, v_hgrn_out_norm, v_hgrn_w_out, v_attn_w_qkv, v_attn_w_out, v_ffn_w_in, v_ffn_w_down, v_final_norm):
    S = x.shape[1]
    xi, yi, ci = lax.axis_index("x"), lax.axis_index("y"), lax.axis_index("c")
    c_idx = jnp.reshape(ci, (1,)).astype(jnp.int32)
    me_idx = jnp.reshape(2 * xi + yi, (1,)).astype(jnp.int32)

    w_own = _split_weights(hgrn_w_in, hgrn_w_out, attn_w_qkv, attn_w_out, ffn_w_in, ffn_w_down)

    comm = _StepComm(w_own, c_idx, me_idx)
    loss, dx, grads, small = _local_step(
        x.reshape(S, D_MODEL), loss_target.reshape(S, D_MODEL), norm_mix, norm_ffn, hgrn_lb_logits,
        hgrn_out_norm, final_norm.reshape(1, D_MODEL), comm)
    comm.reduce_rest(grads)

    halves = comm.shared_halves()
    updated = {}
    for tensor, layers, (wt, mt, vt) in (
            ("hgrn_w_in", ("hin",), (hgrn_w_in, m_hgrn_w_in, v_hgrn_w_in)),
            ("hgrn_w_out", ("hout",), (hgrn_w_out, m_hgrn_w_out, v_hgrn_w_out)),
            ("attn_w_qkv", ("qkv",), (attn_w_qkv, m_attn_w_qkv, v_attn_w_qkv)),
            ("attn_w_out", ("aout",), (attn_w_out, m_attn_w_out, v_attn_w_out)),
            ("ffn_w_in", ("fin0", "fin1"), (ffn_w_in, m_ffn_w_in, v_ffn_w_in)),
            ("ffn_w_down", ("fdn0", "fdn1"), (ffn_w_down, m_ffn_w_down, v_ffn_w_down))):
        updated[tensor] = _adamw([halves[k] for k in layers], c_idx, wt, mt, vt, "adamw_" + tensor)

    loss_row = jnp.pad(loss, ((0, 0), (0, D_MODEL - loss.shape[1])))
    lb3 = jnp.concatenate([small["lb"], jnp.zeros((2, D_MODEL), F32)], axis=0)
    on_grad = jnp.sum(small["out_norm"], axis=0, keepdims=True)
    pack = _pack_small(small["norm_mix"], small["norm_ffn"], lb3, on_grad, small["final_norm"], loss_row)
    gsum = _small_allreduce(pack)
    w_s = _pack_small(norm_mix, norm_ffn, hgrn_lb_logits, hgrn_out_norm, final_norm)
    m_s = _pack_small(m_norm_mix, m_norm_ffn, m_hgrn_lb_logits, m_hgrn_out_norm, m_final_norm)
    v_s = _pack_small(v_norm_mix, v_norm_ffn, v_hgrn_lb_logits, v_hgrn_out_norm, v_final_norm)
    lg_pack = jnp.pad(hgrn_lb_logits, ((0, 8 - hgrn_lb_logits.shape[0]), (0, 0)))
    sg, sd, sm, sv = _small_update(gsum, lg_pack, w_s, m_s, v_s)

    def unpack(p):
        return (p[0:2], p[2:4], p[4:7], p[7:8, :HEAD], p[8])

    def assemble(p, which):
        nmx, nff, lbl, onm, fnm = unpack(p)
        hin, hout, qkv, aout, fin, fdn = [updated[t][which] for t in
                                          ("hgrn_w_in", "hgrn_w_out", "attn_w_qkv", "attn_w_out", "ffn_w_in", "ffn_w_down")]
        return (nmx, nff, hin, lbl, onm, hout, qkv, aout, fin, fdn, fnm)

    total_loss = gsum[9, 0]
    return (total_loss, dx.reshape(1, S, D_MODEL), *assemble(sg, 0), *assemble(sd, 1), *assemble(sm, 2), *assemble(sv, 3))
```

```python
import functools

import jax
import jax.numpy as jnp
from jax import lax
from jax.experimental import pallas as pl
from jax.experimental.pallas import tpu as pltpu

F32 = jnp.float32
BF16 = jnp.bfloat16
MESH = pl.DeviceIdType.MESH

D_MODEL = 1024
HEAD = 128
HGRN_HEADS = 8
HGRN_CHUNK = 64
ATTN_GROUPS = ((128, 1), (512, 4), (2048, 16))
ATTN_SPAN = 128
HEADS_PER_GROUP = 4
GROUP_W = HEADS_PER_GROUP * HEAD
D_FF = 2816
NORM_EPS = 1e-6
ROPE_THETA = 10000.0
NEG = -1e30

ADAM_LR, ADAM_B1, ADAM_B2, ADAM_EPS, ADAM_WD, ADAM_STEP = 0.001, 0.9, 0.999, 1e-08, 0.01, 10

N_CHIPS = 4
VMEM_LIMIT = 56 * 1024 * 1024
SMALL_ROWS = 16


def _params(sem=None):
    return pltpu.CompilerParams(dimension_semantics=sem, vmem_limit_bytes=VMEM_LIMIT)


def _row_tile(rows, cols, budget_bytes=3 * 512 * 1024):
    best = 8
    for t in range(8, rows + 1, 8):
        if rows % t == 0 and t * cols * 4 <= budget_bytes:
            best = t
    assert rows % best == 0
    return best


def _grid_corner(i, j):
    return jnp.logical_and(pl.program_id(0) == i, pl.program_id(1) == j)


def _sigmoid(v):
    return 0.5 * jnp.tanh(0.5 * v) + 0.5


def _dot(a, b):
    return jnp.dot(a, b, preferred_element_type=F32)


def _dot_nt(a, b):
    return lax.dot_general(a, b, (((1,), (1,)), ((), ())), preferred_element_type=F32)


def _dot_tn(a, b):
    return lax.dot_general(a, b, (((0,), (0,)), ((), ())), preferred_element_type=F32)


def _dot_exact(ones, b):
    ones = ones.astype(BF16)
    hi = b.astype(BF16)
    rest = b - hi.astype(F32)
    mid = rest.astype(BF16)
    low = (rest - mid.astype(F32)).astype(BF16)
    return _dot(ones, hi) + _dot(ones, mid) + _dot(ones, low)


def _rstd(v):
    return lax.rsqrt(jnp.mean(v * v, axis=-1, keepdims=True) + NORM_EPS)


def _norm_mm(h, gain, w3, name, out_dtype=F32, tm=512, rider=None):
    S, K = h.shape
    J, _, n = w3.shape
    gi = S // tm

    def body(h_ref, g_ref, w_ref, y_ref, u_ref):
        @pl.when(pl.program_id(1) == 0)
        def _():
            v = h_ref[...]
            u_ref[...] = (v * _rstd(v) * g_ref[...]).astype(BF16)

        y_ref[...] = _dot(u_ref[...], w_ref[pl.program_id(1)]).astype(y_ref.dtype)

    r_ops, r_in, r_out, r_shape, r_scr = _rider_args(rider)
    res = pl.pallas_call(
        _ride(rider, body, 3, 2, functools.partial(_grid_corner, 0, 0), functools.partial(_grid_corner, gi - 1, J - 1)),
        name=name, grid=(gi, J),
        in_specs=[pl.BlockSpec((tm, K), lambda i, j: (i, 0)),
                  pl.BlockSpec((1, K), lambda i, j: (0, 0)),
                  pl.BlockSpec((J, K, n), lambda i, j: (0, 0, 0))] + r_in,
        out_specs=[pl.BlockSpec((tm, n), lambda i, j: (i, j)), pl.BlockSpec((tm, K), lambda i, j: (i, 0))] + r_out,
        out_shape=[jax.ShapeDtypeStruct((S, J * n), out_dtype), jax.ShapeDtypeStruct((S, K), BF16)] + r_shape,
        scratch_shapes=r_scr,
        compiler_params=_params(("arbitrary", "arbitrary")))(h, gain, w3, *r_ops)
    return res[0], res[1], res[2:]


def _mm_res(h, a, w2, name, tm=512):
    S, N = h.shape
    K = a.shape[1]

    def body(h_ref, a_ref, w_ref, o_ref):
        o_ref[...] = h_ref[...] + _dot(a_ref[...], w_ref[...])

    return pl.pallas_call(
        body, name=name, grid=(S // tm,),
        in_specs=[pl.BlockSpec((tm, N), lambda i: (i, 0)),
                  pl.BlockSpec((tm, K), lambda i: (i, 0)),
                  pl.BlockSpec((K, N), lambda i: (0, 0))],
        out_specs=pl.BlockSpec((tm, N), lambda i: (i, 0)),
        out_shape=jax.ShapeDtypeStruct((S, N), F32),
        compiler_params=_params(("parallel",)))(h, a, w2)


def _swiglu(z_ref, F):
    g = z_ref[:, :F].astype(F32)
    return (g * _sigmoid(g) * z_ref[:, F:].astype(F32)).astype(BF16)


def _swiglu_mm_res(h, z, w2, name, tm=256):
    S, N = h.shape
    F = w2.shape[0]

    def body(h_ref, z_ref, w_ref, o_ref, a_ref):
        a = _swiglu(z_ref, F)
        a_ref[...] = a
        o_ref[...] = h_ref[...] + _dot(a, w_ref[...])

    return pl.pallas_call(
        body, name=name, grid=(S // tm,),
        in_specs=[pl.BlockSpec((tm, N), lambda i: (i, 0)),
                  pl.BlockSpec((tm, 2 * F), lambda i: (i, 0)),
                  pl.BlockSpec((F, N), lambda i: (0, 0))],
        out_specs=[pl.BlockSpec((tm, N), lambda i: (i, 0)), pl.BlockSpec((tm, F), lambda i: (i, 0))],
        out_shape=[jax.ShapeDtypeStruct((S, N), F32), jax.ShapeDtypeStruct((S, F), BF16)],
        compiler_params=_params(("parallel",)))(h, z, w2)


def _dy_specs(dy, J, n, tm):
    if dy.ndim == 3:
        return [pl.BlockSpec((None, tm, n), functools.partial(lambda i, j: (j, i, 0), j=j)) for j in range(J)]
    return [pl.BlockSpec((tm, n), functools.partial(lambda i, j: (i, j), j=j)) for j in range(J)]


def _acc_nt(dy_refs, w_ref):
    acc = None
    for j, r in enumerate(dy_refs):
        t = _dot_nt(r[...].astype(BF16), w_ref[j])
        acc = t if acc is None else acc + t
    return acc


def _mm_nt(dy, w3, name, out_dtype=F32, tm=512):
    J, K, n = w3.shape
    S = dy.shape[-2]

    def body(*refs):
        dy_refs, w_ref, o_ref = refs[:J], refs[J], refs[J + 1]
        o_ref[...] = _acc_nt(dy_refs, w_ref).astype(o_ref.dtype)

    return pl.pallas_call(
        body, name=name, grid=(S // tm,),
        in_specs=_dy_specs(dy, J, n, tm) + [pl.BlockSpec((J, K, n), lambda i: (0, 0, 0))],
        out_specs=pl.BlockSpec((tm, K), lambda i: (i, 0)),
        out_shape=jax.ShapeDtypeStruct((S, K), out_dtype),
        compiler_params=_params(("parallel",)))(*([dy] * J), w3)


def _mm_nt_normbwd(dy, w3, h, gain, dh, name, tm=512):
    J, K, n = w3.shape
    S = h.shape[0]

    def body(*refs):
        dy_refs, w_ref, h_ref, g_ref, dh_ref, o_ref, dg_ref = refs[:J], *refs[J:]
        du = _acc_nt(dy_refs, w_ref)
        v = h_ref[...]
        r = _rstd(v)
        xh = v * r
        dyg = du * g_ref[...]
        o_ref[...] = dh_ref[...] + r * (dyg - xh * jnp.mean(dyg * xh, axis=-1, keepdims=True))

        @pl.when(pl.program_id(0) == 0)
        def _():
            dg_ref[...] = jnp.zeros_like(dg_ref)

        dg_ref[...] += jnp.sum(du * xh, axis=0, keepdims=True)

    row = pl.BlockSpec((tm, K), lambda i: (i, 0))
    vec = pl.BlockSpec((1, K), lambda i: (0, 0))
    return pl.pallas_call(
        body, name=name, grid=(S // tm,),
        in_specs=_dy_specs(dy, J, n, tm) + [pl.BlockSpec((J, K, n), lambda i: (0, 0, 0)), row, vec, row],
        out_specs=[row, vec],
        out_shape=[jax.ShapeDtypeStruct((S, K), F32), jax.ShapeDtypeStruct((1, K), F32)],
        compiler_params=_params(("arbitrary",)))(*([dy] * J), w3, h, gain, dh)


def _mm_nt_swiglu_bwd(dh, w2, z, name, tm=256):
    F, N = w2.shape
    S = dh.shape[0]

    def body(dh_ref, w_ref, z_ref, o_ref):
        da = _dot_nt(dh_ref[...].astype(BF16), w_ref[...])
        g = z_ref[:, :F].astype(F32)
        u = z_ref[:, F:].astype(F32)
        sg = _sigmoid(g)
        o_ref[:, :F] = (da * u * (sg * (1.0 + g * (1.0 - sg)))).astype(BF16)
        o_ref[:, F:] = (da * (g * sg)).astype(BF16)

    return pl.pallas_call(
        body, name=name, grid=(S // tm,),
        in_specs=[pl.BlockSpec((tm, N), lambda i: (i, 0)),
                  pl.BlockSpec((F, N), lambda i: (0, 0)),
                  pl.BlockSpec((tm, 2 * F), lambda i: (i, 0))],
        out_specs=pl.BlockSpec((tm, 2 * F), lambda i: (i, 0)),
        out_shape=jax.ShapeDtypeStruct((S, 2 * F), BF16),
        compiler_params=_params(("parallel",)))(dh, w2, z)


def _mm_tn(x, dy, J, n, tn, name):
    tpn = n // tn
    ts = 512
    S, K = x.shape
    if dy.ndim == 3:
        dy_spec = pl.BlockSpec((None, ts, tn), lambda c, s: (c // tpn, s, c % tpn))
    else:
        dy_spec = pl.BlockSpec((ts, tn), lambda c, s: (s, c))

    def body(x_ref, dy_ref, o_ref):
        @pl.when(pl.program_id(1) == 0)
        def _():
            o_ref[...] = jnp.zeros_like(o_ref)

        o_ref[...] += _dot_tn(x_ref[...], dy_ref[...].astype(BF16))

    return pl.pallas_call(
        body, name=name, grid=(J * tpn, S // ts),
        in_specs=[pl.BlockSpec((ts, K), lambda c, s: (s, 0)), dy_spec],
        out_specs=pl.BlockSpec((None, K, tn), lambda c, s: (c // tpn, 0, c % tpn)),
        out_shape=jax.ShapeDtypeStruct((J, K, n), F32),
        compiler_params=_params(("parallel", "arbitrary")))(x, dy)


def _loss_head(h, gain, target, tm=512):
    S, K = h.shape

    def body(h_ref, g_ref, t_ref, dh_ref, loss_ref, dg_ref):
        v = h_ref[...]
        r = _rstd(v)
        xh = v * r
        g = g_ref[...]
        dy = (xh * g - t_ref[...]) * (1.0 / K)
        dyg = dy * g
        dh_ref[...] = r * (dyg - xh * jnp.mean(dyg * xh, axis=-1, keepdims=True))

        @pl.when(pl.program_id(0) == 0)
        def _():
            loss_ref[...] = jnp.zeros_like(loss_ref)
            dg_ref[...] = jnp.zeros_like(dg_ref)

        part = jnp.sum(jnp.sum(dy * dy, axis=-1, keepdims=True), axis=0, keepdims=True) * (0.5 * K)
        lane = lax.broadcasted_iota(jnp.int32, loss_ref.shape, 1)
        loss_ref[...] += jnp.where(lane == 0, part, 0.0)
        dg_ref[...] += jnp.sum(dy * xh, axis=0, keepdims=True)

    row = pl.BlockSpec((tm, K), lambda i: (i, 0))
    vec = pl.BlockSpec((1, K), lambda i: (0, 0))
    return pl.pallas_call(
        body, name="loss_head", grid=(S // tm,),
        in_specs=[row, vec, row],
        out_specs=[row, pl.BlockSpec((1, HEAD), lambda i: (0, 0)), vec],
        out_shape=[jax.ShapeDtypeStruct((S, K), F32), jax.ShapeDtypeStruct((1, HEAD), F32),
                   jax.ShapeDtypeStruct((1, K), F32)],
        compiler_params=_params(("arbitrary",)))(h, gain, target)


def _lower_bound(lg_ref):
    l0, l1, l2 = lg_ref[0:1, :], lg_ref[1:2, :], lg_ref[2:3, :]
    mx = jnp.maximum(jnp.maximum(l0, l1), l2)
    e0, e1, e2 = jnp.exp(l0 - mx), jnp.exp(l1 - mx), jnp.exp(l2 - mx)
    return e0 / (e0 + e1 + e2)


def _chunks(v, ncb):
    C = HGRN_CHUNK
    return [v[c * C:(c + 1) * C] for c in range(ncb)]


def _rows(parts):
    return jnp.concatenate(parts, axis=0)


def _block_gates(qz, fz, lb, ncb):
    C = HGRN_CHUNK
    row = lax.broadcasted_iota(jnp.int32, (C, C), 0)
    col = lax.broadcasted_iota(jnp.int32, (C, C), 1)
    tri = (col <= row).astype(F32)
    first_half = lax.broadcasted_iota(jnp.int32, (C, HEAD), 0) < C // 2
    sig = _sigmoid(fz)
    fg = lb + (1.0 - lb) * sig
    key = 1.0 - fg
    lg = jnp.log(fg)
    lgs = _chunks(lg, ncb)
    b = _rows([_dot_exact(tri, v) for v in lgs])
    r_c = [jnp.sum(jnp.where(first_half, v, 0.0), axis=0, keepdims=True) for v in lgs]
    bl_c = [jnp.sum(v, axis=0, keepdims=True) for v in lgs]
    r = _rows([jnp.broadcast_to(v, (C, HEAD)) for v in r_c])
    e_br, e_rb = jnp.exp(b - r), jnp.exp(r - b)
    e_b = e_br * _rows([jnp.broadcast_to(jnp.exp(v), (C, HEAD)) for v in r_c])
    e_lb = e_rb * _rows([jnp.broadcast_to(jnp.exp(e - v), (C, HEAD)) for e, v in zip(bl_c, r_c)])
    sq = _sigmoid(qz)
    qy = qz * sq
    return sig, fg, key, (e_br, e_rb, e_b, e_lb), bl_c, sq, qy


def _hgrn_fwd(proj, logits, gain, tb=1024, rider=None):
    S = proj.shape[0]
    H, C = HGRN_HEADS, HGRN_CHUNK
    ncb = tb // C

    def body(q_ref, f_ref, i_ref, g_ref, lg_ref, gn_ref, o_ref, og_ref, st_ref, state):
        @pl.when(pl.program_id(1) == 0)
        def _():
            state[...] = jnp.zeros_like(state)

        lb = _lower_bound(lg_ref)
        causal = lax.broadcasted_iota(jnp.int32, (C, C), 1) <= lax.broadcasted_iota(jnp.int32, (C, C), 0)
        qz, fz, gz = q_ref[...], f_ref[...], g_ref[...]
        _, _, key, (e_br, e_rb, e_b, e_lb), bl_c, _, qy = _block_gates(qz, fz, lb, ncb)
        qs = _chunks((qy * e_br).astype(BF16), ncb)
        ks = _chunks((key * e_rb).astype(BF16), ncb)
        qb = _chunks((qy * e_b).astype(BF16), ncb)
        ke = _chunks((key * e_lb).astype(BF16), ncb)
        vb = _chunks(i_ref[...].astype(BF16), ncb)
        o_intra, upd = [], []
        for c in range(ncb):
            a = jnp.where(causal, _dot_nt(qs[c], ks[c]), 0.0).astype(BF16)
            o_intra.append(_dot(a, vb[c]))
            upd.append(_dot_tn(vb[c], ke[c]))
        st = state[...]
        for c in range(ncb):
            st_ref[c] = st
            st = st * jnp.exp(bl_c[c]) + upd[c]
        state[...] = st
        o = _rows([_dot_nt(qb[c], st_ref[c].astype(BF16)) + o_intra[c] for c in range(ncb)])
        o_ref[...] = o
        og_ref[...] = ((o * _rstd(o) * gn_ref[...]) * (gz * _sigmoid(gz))).astype(BF16)

    def part(p):
        return pl.BlockSpec((tb, HEAD), functools.partial(lambda h, i, p: (i, p * H + h), p=p))

    nb = S // tb
    r_ops, r_in, r_out, r_shape, r_scr = _rider_args(rider)
    res = pl.pallas_call(
        _ride(rider, body, 6, 3, functools.partial(_grid_corner, 0, 0), functools.partial(_grid_corner, H - 1, nb - 1)),
        name="hgrn_fwd", grid=(H, nb),
        in_specs=[part(0), part(1), part(2), part(3),
                  pl.BlockSpec((3, HEAD), lambda h, i: (0, h)),
                  pl.BlockSpec((1, HEAD), lambda h, i: (0, 0))] + r_in,
        out_specs=[pl.BlockSpec((tb, HEAD), lambda h, i: (i, h)),
                   pl.BlockSpec((tb, HEAD), lambda h, i: (i, h)),
                   pl.BlockSpec((None, ncb, HEAD, HEAD), lambda h, i: (h, i, 0, 0))] + r_out,
        out_shape=[jax.ShapeDtypeStruct((S, H * HEAD), F32),
                   jax.ShapeDtypeStruct((S, H * HEAD), BF16),
                   jax.ShapeDtypeStruct((H, S // C, HEAD, HEAD), F32)] + r_shape,
        scratch_shapes=[pltpu.VMEM((HEAD, HEAD), F32)] + r_scr,
        compiler_params=_params(("arbitrary", "arbitrary")))(proj, proj, proj, proj, logits, gain, *r_ops)
    return res[:3], res[3:]


def _hgrn_bwd(proj, logits, gain, o, states, dog, tb=1024, rider=None):
    S = proj.shape[0]
    H, C = HGRN_HEADS, HGRN_CHUNK
    ncb = tb // C
    nb = S // tb

    def body(q_ref, f_ref, i_ref, g_ref, lg_ref, gn_ref, o_ref, st_ref, dog_ref,
             dp_ref, dlb_ref, dgn_ref, dstate, dst_scr):
        @pl.when(pl.program_id(1) == 0)
        def _():
            dstate[...] = jnp.zeros_like(dstate)
            dlb_ref[...] = jnp.zeros_like(dlb_ref)
            dgn_ref[...] = jnp.zeros_like(dgn_ref)

        lb = _lower_bound(lg_ref)
        oml = 1.0 - lb
        gn = gn_ref[...]
        row = lax.broadcasted_iota(jnp.int32, (C, C), 0)
        col = lax.broadcasted_iota(jnp.int32, (C, C), 1)
        causal = col <= row
        tri_up = (col >= row).astype(F32)
        qz, fz, gz = q_ref[...], f_ref[...], g_ref[...]
        sig, fg, key, (e_br, e_rb, e_b, e_lb), bl_c, sq, qy = _block_gates(qz, fz, lb, ncb)
        qs_v, ks_v = (qy * e_br).astype(BF16), (key * e_rb).astype(BF16)
        qb_v, ke_v = (qy * e_b).astype(BF16), (key * e_lb).astype(BF16)
        qs, ks, qb, ke = _chunks(qs_v, ncb), _chunks(ks_v, ncb), _chunks(qb_v, ncb), _chunks(ke_v, ncb)
        vb = _chunks(i_ref[...].astype(BF16), ncb)
        ov = o_ref[...]
        rs = _rstd(ov)
        xh = ov * rs
        sg = _sigmoid(gz)
        dog_v = dog_ref[...]
        dgz = dog_v * (xh * gn) * (sg * (1.0 + gz * (1.0 - sg)))
        don = dog_v * (gz * sg)
        dgn_ref[...] += jnp.sum(don * xh, axis=0, keepdims=True)
        dyg = don * gn
        do = rs * (dyg - xh * jnp.mean(dyg * xh, axis=-1, keepdims=True))
        dob = _chunks(do.astype(BF16), ncb)
        dv_in, dqs, dks, wst = [], [], [], []
        for c in range(ncb):
            a = jnp.where(causal, _dot_nt(qs[c], ks[c]), 0.0).astype(BF16)
            da = jnp.where(causal, _dot_nt(dob[c], vb[c]), 0.0).astype(BF16)
            dv_in.append(_dot_tn(a, dob[c]))
            dqs.append(_dot(da, ks[c]))
            dks.append(_dot_tn(da, qs[c]))
            wst.append(_dot_tn(dob[c], qb[c]))
        e_l = [jnp.exp(v) for v in bl_c]
        dst = dstate[...]
        for c in reversed(range(ncb)):
            dst_scr[c] = dst
            dst = wst[c] + dst * e_l[c]
        dstate[...] = dst
        dv, dqb, dke, dbl_st = [], [], [], []
        for c in range(ncb):
            dst1 = dst_scr[c]
            st0 = st_ref[c]
            dst1b = dst1.astype(BF16)
            dv.append(dv_in[c] + _dot_nt(ke[c], dst1b))
            dqb.append(_dot(dob[c], st0.astype(BF16)))
            dke.append(_dot(vb[c], dst1b))
            dbl_st.append(jnp.sum(dst1 * st0, axis=0, keepdims=True) * e_l[c])
        dqs, dks, dqb, dke, dv = _rows(dqs), _rows(dks), _rows(dqb), _rows(dke), _rows(dv)
        dke_ke = dke * ke_v.astype(F32)
        db = dqs * qs_v.astype(F32) - dks * ks_v.astype(F32) + dqb * qb_v.astype(F32) - dke_ke
        dlg = []
        for c, (db_c, kk_c) in enumerate(zip(_chunks(db, ncb), _chunks(dke_ke, ncb))):
            dbl = jnp.sum(kk_c, axis=0, keepdims=True) + dbl_st[c]
            dlg.append(_dot_exact(tri_up, db_c) + dbl)
        dlg = _rows(dlg)
        dkey = dks * e_rb + dke * e_lb
        dqy = dqs * e_br + dqb * e_b
        dfg = dlg / fg - dkey
        dlb_ref[...] += jnp.sum(dfg * (1.0 - sig), axis=0, keepdims=True)
        dp_ref[0] = (dqy * (sq * (1.0 + qz * (1.0 - sq)))).astype(BF16)
        dp_ref[1] = (dfg * oml * sig * (1.0 - sig)).astype(BF16)
        dp_ref[2] = dv.astype(BF16)
        dp_ref[3] = dgz.astype(BF16)

    def part(p):
        return pl.BlockSpec((tb, HEAD), functools.partial(lambda h, i, p: (nb - 1 - i, p * H + h), p=p))

    blk = pl.BlockSpec((tb, HEAD), lambda h, i: (nb - 1 - i, h))
    acc = pl.BlockSpec((None, 1, HEAD), lambda h, i: (h, 0, 0))
    r_ops, r_in, r_out, r_shape, r_scr = _rider_args(rider)
    res = pl.pallas_call(
        _ride(rider, body, 9, 3, functools.partial(_grid_corner, 0, 0), functools.partial(_grid_corner, H - 1, nb - 1)),
        name="hgrn_bwd", grid=(H, nb),
        in_specs=[part(0), part(1), part(2), part(3),
                  pl.BlockSpec((3, HEAD), lambda h, i: (0, h)),
                  pl.BlockSpec((1, HEAD), lambda h, i: (0, 0)),
                  blk,
                  pl.BlockSpec((None, ncb, HEAD, HEAD), lambda h, i: (h, nb - 1 - i, 0, 0)),
                  blk] + r_in,
        out_specs=[pl.BlockSpec((4, tb, HEAD), lambda h, i: (0, nb - 1 - i, h)), acc, acc] + r_out,
        out_shape=[jax.ShapeDtypeStruct((4, S, H * HEAD), BF16),
                   jax.ShapeDtypeStruct((H, 1, HEAD), F32),
                   jax.ShapeDtypeStruct((H, 1, HEAD), F32)] + r_shape,
        scratch_shapes=[pltpu.VMEM((HEAD, HEAD), F32), pltpu.VMEM((ncb, HEAD, HEAD), F32)] + r_scr,
        compiler_params=_params(("arbitrary", "arbitrary")))(
            proj, proj, proj, proj, logits, gain, o, states, dog, *r_ops)
    return res[:3], res[3:]


def _rope(v, cos, sin):
    return v * cos + pltpu.roll(v, HEAD // 2, 1) * sin


def _band_masks():
    qi = lax.broadcasted_iota(jnp.int32, (ATTN_SPAN, ATTN_SPAN), 0)
    kj = lax.broadcasted_iota(jnp.int32, (ATTN_SPAN, ATTN_SPAN), 1)
    return kj <= qi, kj >= qi


def _attn_fwd(a):
    d, L, _ = a.shape
    nb = L // ATTN_SPAN
    scale = HEAD ** -0.5

    def body(q_ref, kc_ref, kp_ref, vc_ref, vp_ref, o_ref, lse_ref):
        n = pl.program_id(1)
        mask_c, mask_p0 = _band_masks()
        mask_p = jnp.logical_and(mask_p0, n > 0)
        heads = [slice(hh * HEAD, (hh + 1) * HEAD) for hh in range(HEADS_PER_GROUP)]
        s_c = [jnp.where(mask_c, _dot_nt(q_ref[:, c], kc_ref[:, c]) * scale, NEG) for c in heads]
        s_p = [jnp.where(mask_p, _dot_nt(q_ref[:, c], kp_ref[:, c]) * scale, NEG) for c in heads]
        m = [jnp.maximum(jnp.max(a, axis=-1, keepdims=True), jnp.max(b, axis=-1, keepdims=True)) for a, b in zip(s_c, s_p)]
        p_c = [jnp.exp(a - mm) for a, mm in zip(s_c, m)]
        p_p = [jnp.exp(b - mm) for b, mm in zip(s_p, m)]
        l = [jnp.sum(a, axis=-1, keepdims=True) + jnp.sum(b, axis=-1, keepdims=True) for a, b in zip(p_c, p_p)]
        acc = [_dot(a.astype(BF16), vc_ref[:, c]) + _dot(b.astype(BF16), vp_ref[:, c]) for a, b, c in zip(p_c, p_p, heads)]
        for c, a, ll, mm in zip(heads, acc, l, m):
            o_ref[:, c] = a / ll
            lse_ref[:, c] = jnp.broadcast_to(mm + jnp.log(ll), (ATTN_SPAN, HEAD))

    def blk(part, prev):
        if prev:
            return pl.BlockSpec((None, ATTN_SPAN, GROUP_W), functools.partial(lambda r, n, p: (r, jnp.maximum(n - 1, 0), p), p=part))
        return pl.BlockSpec((None, ATTN_SPAN, GROUP_W), functools.partial(lambda r, n, p: (r, n, p), p=part))

    out = pl.BlockSpec((None, ATTN_SPAN, GROUP_W), lambda r, n: (r, n, 0))
    return pl.pallas_call(
        body, name=f"attn_fwd_d{d}", grid=(d, nb),
        in_specs=[blk(0, False), blk(1, False), blk(1, True), blk(2, False), blk(2, True)],
        out_specs=[out, out],
        out_shape=[jax.ShapeDtypeStruct((d, L, GROUP_W), F32), jax.ShapeDtypeStruct((d, L, GROUP_W), F32)],
        compiler_params=_params(("parallel", "arbitrary")))(a, a, a, a, a)


def _attn_bwd(a, do, lse, dd):
    d, L, _ = a.shape
    nb = L // ATTN_SPAN
    scale = HEAD ** -0.5

    def body(qc_ref, qn_ref, kp_ref, kc_ref, vp_ref, vc_ref, doc_ref, don_ref, lc_ref, ln_ref, ddc_ref, ddn_ref, da_ref):
        n = pl.program_id(1)
        mask_c, mask_p0 = _band_masks()
        mask_p = jnp.logical_and(mask_p0, n > 0)
        mask_n = jnp.logical_and(mask_p0, n < nb - 1)
        H4 = range(HEADS_PER_GROUP)
        heads = [slice(hh * HEAD, (hh + 1) * HEAD) for hh in H4]
        do_c = [doc_ref[:, c] for c in heads]
        do_n = [don_ref[:, c] for c in heads]
        p_c = [jnp.where(mask_c, jnp.exp(_dot_nt(qc_ref[:, c], kc_ref[:, c]) * scale - lc_ref[:, c]), 0.0) for c in heads]
        p_p = [jnp.where(mask_p, jnp.exp(_dot_nt(qc_ref[:, c], kp_ref[:, c]) * scale - lc_ref[:, c]), 0.0) for c in heads]
        p_n = [jnp.where(mask_n, jnp.exp(_dot_nt(qn_ref[:, c], kc_ref[:, c]) * scale - ln_ref[:, c]), 0.0) for c in heads]
        ds_c = [(p_c[i] * (_dot_nt(do_c[i], vc_ref[:, heads[i]]) + ddc_ref[:, heads[i]])).astype(BF16) for i in H4]
        ds_p = [(p_p[i] * (_dot_nt(do_c[i], vp_ref[:, heads[i]]) + ddc_ref[:, heads[i]])).astype(BF16) for i in H4]
        ds_n = [(p_n[i] * (_dot_nt(do_n[i], vc_ref[:, heads[i]]) + ddn_ref[:, heads[i]])).astype(BF16) for i in H4]
        dq = [(_dot(ds_c[i], kc_ref[:, heads[i]]) + _dot(ds_p[i], kp_ref[:, heads[i]])) * scale for i in H4]
        dk = [(_dot_tn(ds_c[i], qc_ref[:, heads[i]]) + _dot_tn(ds_n[i], qn_ref[:, heads[i]])) * scale for i in H4]
        dv = [_dot_tn(p_c[i].astype(BF16), do_c[i]) + _dot_tn(p_n[i].astype(BF16), do_n[i]) for i in H4]
        for i in H4:
            da_ref[:, heads[i]] = dq[i].astype(BF16)
            da_ref[:, GROUP_W + i * HEAD:GROUP_W + (i + 1) * HEAD] = dk[i].astype(BF16)
            da_ref[:, 2 * GROUP_W + i * HEAD:2 * GROUP_W + (i + 1) * HEAD] = dv[i].astype(BF16)

    def rel(delta):
        if delta < 0:
            return lambda n: jnp.maximum(n - 1, 0)
        if delta > 0:
            return lambda n: jnp.minimum(n + 1, nb - 1)
        return lambda n: n

    def blk(width, part, delta):
        f = rel(delta)
        return pl.BlockSpec((None, ATTN_SPAN, width), functools.partial(lambda r, n, p, f: (r, f(n), p), p=part, f=f))

    g = GROUP_W
    return pl.pallas_call(
        body, name=f"attn_bwd_d{d}", grid=(d, nb),
        in_specs=[blk(g, 0, 0), blk(g, 0, 1), blk(g, 1, -1), blk(g, 1, 0), blk(g, 2, -1), blk(g, 2, 0),
                  blk(g, 0, 0), blk(g, 0, 1), blk(g, 0, 0), blk(g, 0, 1), blk(g, 0, 0), blk(g, 0, 1)],
        out_specs=pl.BlockSpec((None, ATTN_SPAN, 3 * g), lambda r, n: (r, n, 0)),
        out_shape=jax.ShapeDtypeStruct((d, L, 3 * g), BF16),
        compiler_params=_params(("parallel", "arbitrary")))(
            a, a, a, a, a, a, do, do, lse, lse, dd, dd)


def _group_weights(lse_refs, cols):
    ls = [r[:, cols] for r in lse_refs]
    mx = jnp.maximum(jnp.maximum(ls[0], ls[1]), ls[2])
    es = [jnp.exp(v - mx) for v in ls]
    tot = es[0] + es[1] + es[2]
    return [e / tot for e in es]


def _gather_tokens(ref, scr, d, tm):
    if d == 1:
        return ref.at[0]
    for r in range(d):
        scr[pl.ds(r, tm // d, stride=d), :] = ref[r]
    return scr


def _scatter_tokens(scr, ref, d, tm):
    if d == 1:
        ref[0] = scr[...].astype(ref.dtype)
        return
    for r in range(d):
        ref[r] = scr[pl.ds(r, tm // d, stride=d), :].astype(ref.dtype)


def _head_spec(d, tm):
    return pl.BlockSpec((d, tm // d, HEAD), lambda i, j: (0, i, j))


def _qkv_dilated(h, gain, wg, cos, sin, d, tm=512):
    S, K = h.shape

    def body(h_ref, g_ref, w_ref, cos_ref, sin_ref, out_ref, u_ref, y_scr):
        p = pl.program_id(1)

        @pl.when(p == 0)
        def _():
            v = h_ref[...]
            u_ref[...] = (v * _rstd(v) * g_ref[...]).astype(BF16)

        y = _dot(u_ref[...], w_ref[...])
        heads = [slice(hh * HEAD, (hh + 1) * HEAD) for hh in range(HEADS_PER_GROUP)]
        for hh, cols in enumerate(heads):
            y_scr[hh] = y[:, cols]

        @pl.when(p < 2)
        def _():
            for r in range(d):
                rows = slice(None) if d == 1 else pl.ds(r, tm // d, stride=d)
                cr, sr = cos_ref[rows, :], sin_ref[rows, :]
                for hh, cols in enumerate(heads):
                    out_ref[r, :, cols] = _rope(y_scr.at[hh][rows, :], cr, sr).astype(BF16)

        @pl.when(p == 2)
        def _():
            for r in range(d):
                rows = slice(None) if d == 1 else pl.ds(r, tm // d, stride=d)
                for hh, cols in enumerate(heads):
                    out_ref[r, :, cols] = y_scr.at[hh][rows, :].astype(BF16)

    tab = pl.BlockSpec((tm, HEAD), lambda i, p: (i, 0))
    return pl.pallas_call(
        body, name=f"attn_qkv_d{d}", grid=(S // tm, 3),
        in_specs=[pl.BlockSpec((tm, K), lambda i, p: (i, 0)),
                  pl.BlockSpec((1, K), lambda i, p: (0, 0)),
                  pl.BlockSpec((K, GROUP_W), lambda i, p: (0, p)), tab, tab],
        out_specs=[pl.BlockSpec((d, tm // d, GROUP_W), lambda i, p: (0, i, p)), pl.BlockSpec((tm, K), lambda i, p: (i, 0))],
        out_shape=[jax.ShapeDtypeStruct((d, S // d, 3 * GROUP_W), BF16), jax.ShapeDtypeStruct((S, K), BF16)],
        scratch_shapes=[pltpu.VMEM((HEADS_PER_GROUP, tm, HEAD), F32)],
        compiler_params=_params(("parallel", "arbitrary")))(h, gain, wg, cos, sin)


def _undilate_group(da, dqkv, cos, sin, g, tm=512):
    d, L, _ = da.shape
    S = d * L
    G = len(ATTN_GROUPS)

    def body(*refs):
        da_ref, cos_ref, sin_ref, out_ref, scr = refs[0], refs[1], refs[2], refs[-2], refs[-1]
        p = pl.program_id(1)
        heads = [slice(hh * HEAD, (hh + 1) * HEAD) for hh in range(HEADS_PER_GROUP)]
        for hh, cols in enumerate(heads):
            if d == 1:
                scr[hh] = da_ref[0, :, cols].astype(F32)
            else:
                for r in range(d):
                    scr.at[hh][pl.ds(r, tm // d, stride=d), :] = da_ref[r, :, cols].astype(F32)

        @pl.when(p < 2)
        def _():
            cr, sr = cos_ref[...], -sin_ref[...]
            for hh, cols in enumerate(heads):
                out_ref[:, cols] = _rope(scr[hh], cr, sr).astype(BF16)

        @pl.when(p == 2)
        def _():
            for hh, cols in enumerate(heads):
                out_ref[:, cols] = scr[hh].astype(BF16)

    tab = pl.BlockSpec((tm, HEAD), lambda i, p: (i, 0))
    operands = (da, cos, sin) if dqkv is None else (da, cos, sin, dqkv)
    return pl.pallas_call(
        body, name=f"attn_undilate_d{d}", grid=(S // tm, 3),
        in_specs=[pl.BlockSpec((d, tm // d, GROUP_W), lambda i, p: (0, i, p)), tab, tab] + ([] if dqkv is None else [ANY]),
        out_specs=pl.BlockSpec((tm, GROUP_W), lambda i, p: (i, p * G + g)),
        out_shape=jax.ShapeDtypeStruct((S, 3 * G * GROUP_W), BF16),
        input_output_aliases={} if dqkv is None else {3: 0},
        scratch_shapes=[pltpu.VMEM((HEADS_PER_GROUP, tm, HEAD), F32)],
        compiler_params=_params(("parallel", "arbitrary")))(*operands)


def _attn_merge(os_, lses, tm=512):
    G = len(os_)
    S = os_[0].shape[0] * os_[0].shape[1]

    def body(*refs):
        o_refs, l_refs, out_ref = refs[:G], refs[G:2 * G], refs[2 * G]
        scr = refs[2 * G + 1:]
        hh = pl.program_id(1)
        o_tok = [_gather_tokens(o_refs[g], scr[g], d, tm) for g, (_, d) in enumerate(ATTN_GROUPS)]
        l_tok = [_gather_tokens(l_refs[g], scr[G + g], d, tm) for g, (_, d) in enumerate(ATTN_GROUPS)]
        al = _group_weights(l_tok, slice(None))
        for g in range(G):
            cols = pl.ds(pl.multiple_of(g * GROUP_W + hh * HEAD, HEAD), HEAD)
            out_ref[:, cols] = (o_tok[g][...] * al[g]).astype(BF16)

    specs = [_head_spec(d, tm) for _, d in ATTN_GROUPS]
    return pl.pallas_call(
        body, name="attn_merge", grid=(S // tm, HEADS_PER_GROUP),
        in_specs=specs + specs,
        out_specs=pl.BlockSpec((tm, G * GROUP_W), lambda i, j: (i, 0)),
        out_shape=jax.ShapeDtypeStruct((S, G * GROUP_W), BF16),
        scratch_shapes=[pltpu.VMEM((tm, HEAD), F32)] * (2 * G),
        compiler_params=_params(("parallel", "arbitrary")))(*os_, *lses)


def _attn_merge_bwd(os_, lses, doa, tm=512):
    G = len(os_)
    S = doa.shape[0]

    def body(*refs):
        o_refs, l_refs, doa_ref = refs[:G], refs[G:2 * G], refs[2 * G]
        do_refs, dd_refs = refs[2 * G + 1:3 * G + 1], refs[3 * G + 1:4 * G + 1]
        scr = refs[4 * G + 1:]
        hh = pl.program_id(1)
        o_tok = [_gather_tokens(o_refs[g], scr[g], d, tm) for g, (_, d) in enumerate(ATTN_GROUPS)]
        l_tok = [_gather_tokens(l_refs[g], scr[G + g], d, tm) for g, (_, d) in enumerate(ATTN_GROUPS)]
        do_tok, dd_tok = scr[2 * G:3 * G], scr[3 * G:]
        al = _group_weights(l_tok, slice(None))
        mix = None
        for g in range(G):
            dg = doa_ref[:, pl.ds(pl.multiple_of(g * GROUP_W + hh * HEAD, HEAD), HEAD)]
            do_tok[g][...] = dg * al[g]
            t = al[g] * jnp.sum(dg * o_tok[g][...], axis=-1, keepdims=True)
            mix = t if mix is None else mix + t
        for g, (_, d) in enumerate(ATTN_GROUPS):
            dd_tok[g][...] = jnp.broadcast_to(-al[g] * mix, (tm, HEAD))
            _scatter_tokens(do_tok[g], do_refs[g], d, tm)
            _scatter_tokens(dd_tok[g], dd_refs[g], d, tm)

    specs = [_head_spec(d, tm) for _, d in ATTN_GROUPS]
    do_shapes = [jax.ShapeDtypeStruct((d, S // d, GROUP_W), BF16) for _, d in ATTN_GROUPS]
    dd_shapes = [jax.ShapeDtypeStruct((d, S // d, GROUP_W), F32) for _, d in ATTN_GROUPS]
    return pl.pallas_call(
        body, name="attn_merge_bwd", grid=(S // tm, HEADS_PER_GROUP),
        in_specs=specs + specs + [pl.BlockSpec((tm, G * GROUP_W), lambda i, j: (i, 0))],
        out_specs=specs + specs,
        out_shape=do_shapes + dd_shapes,
        scratch_shapes=[pltpu.VMEM((tm, HEAD), F32)] * (4 * G),
        compiler_params=_params(("parallel", "arbitrary")))(*os_, *lses, doa)


def _rope_tables(S):
    inv_freq = 1.0 / (ROPE_THETA ** (jnp.arange(0, HEAD, 2, dtype=F32) / HEAD))
    ang = jnp.arange(S, dtype=F32)[:, None] * inv_freq[None, :]
    cos, sin = jnp.cos(ang), jnp.sin(ang)
    return jnp.concatenate([cos, cos], axis=-1), jnp.concatenate([-sin, sin], axis=-1)


def _local_step(x, target, norm_mix, norm_ffn, lb_logits, out_gain, final_norm, comm):
    S = x.shape[0]
    nm0, nm1 = norm_mix[0:1], norm_mix[1:2]
    nf0, nf1 = norm_ffn[0:1], norm_ffn[1:2]
    w = comm.first_weights()

    proj, u0, got = _norm_mm(x, nm0, w["hin"], "hgrn_in", rider=comm.gather_rider(LATE_WEIGHTS_A))
    w.update(comm.gathered(LATE_WEIGHTS_A, got))
    (o, og, states), got = _hgrn_fwd(proj, lb_logits, out_gain, rider=comm.gather_rider(LATE_WEIGHTS_B))
    w.update(comm.gathered(LATE_WEIGHTS_B, got))
    fin_tn = w["fin0"].shape[2]
    h1 = _mm_res(x, og, w["hout"], "hgrn_out")
    z0, u1, _ = _norm_mm(h1, nf0, w["fin0"], "ffn0_in", out_dtype=BF16)
    h2, act0 = _swiglu_mm_res(h1, z0, w["fdn0"], "ffn0_down")
    cos, sin = _rope_tables(S)
    G = len(ATTN_GROUPS)
    w_groups = w["qkv"].transpose(1, 0, 2).reshape(D_MODEL, 3, G, GROUP_W)
    a_g, u2 = zip(*[_qkv_dilated(h2, nm1, w_groups[:, :, gi, :].reshape(D_MODEL, 3 * GROUP_W), cos, sin, d)
                    for gi, (_, d) in enumerate(ATTN_GROUPS)])
    o_g, lse_g = zip(*[_attn_fwd(a) for a in a_g])
    oa = _attn_merge(o_g, lse_g)
    h3 = _mm_res(h2, oa, w["aout"], "attn_out")
    z1, u3, _ = _norm_mm(h3, nf1, w["fin1"], "ffn1_in", out_dtype=BF16)
    h4, act1 = _swiglu_mm_res(h3, z1, w["fdn1"], "ffn1_down")
    dh4, loss, d_final = _loss_head(h4, final_norm, target)

    grads, small = {}, {"final_norm": d_final}

    def ffn_bwd(dh, h_in, u_in, z, act, gain, w_in, w_dn, tag):
        dz = _mm_nt_swiglu_bwd(dh, w_dn, z, tag + "_down_dx")
        g_dn = _mm_tn(act, dh, 1, D_MODEL, D_MODEL, tag + "_down_dw")
        g_in = _mm_tn(u_in, dz, N_CHIPS, fin_tn, fin_tn, tag + "_in_dw")
        dh_in, dgain = _mm_nt_normbwd(dz, w_in, h_in, gain, dh, tag + "_in_dx")
        return dh_in, dgain, g_in, g_dn[0]

    dh3, d_nf1, grads["fin1"], grads["fdn1"] = ffn_bwd(dh4, h3, u3, z1, act1, nf1, w["fin1"], w["fdn1"], "ffn1")
    doa = _mm_nt(dh3, w["aout"][None], "attn_out_dx")
    grads["aout"] = _mm_tn(oa, dh3, 1, D_MODEL, D_MODEL, "attn_out_dw")[0]
    merged = _attn_merge_bwd(o_g, lse_g, doa)
    G = len(ATTN_GROUPS)
    das = [_attn_bwd(a_g[gi], merged[gi], lse_g[gi], merged[G + gi]) for gi in range(G)]
    dqkv = None
    for gi in range(G):
        dqkv = _undilate_group(das[gi], dqkv, cos, sin, gi)
    n_qkv = w["qkv"].shape[2]
    grads["qkv"] = _mm_tn(u2[0], dqkv, N_CHIPS, n_qkv, n_qkv, "attn_qkv_dw")
    dh2, d_nm1 = _mm_nt_normbwd(dqkv, w["qkv"], h2, nm1, dh3, "attn_qkv_dx")

    dh1, d_nf0, grads["fin0"], grads["fdn0"] = ffn_bwd(dh2, h1, u1, z0, act0, nf0, w["fin0"], w["fdn0"], "ffn0")
    dog = _mm_nt(dh1, w["hout"][None], "hgrn_out_dx")
    rider = comm.exchange_rider({k: grads.pop(k) for k in EARLY_GRADS})
    (dproj, dlb, dgn), got = _hgrn_bwd(proj, lb_logits, out_gain, o, states, dog, rider=rider)
    comm.exchanged(got)
    grads["hout"] = _mm_tn(og, dh1, 1, D_MODEL, D_MODEL, "hgrn_out_dw")[0]
    grads["hin"] = _mm_tn(u0, dproj, N_CHIPS, D_MODEL, D_MODEL, "hgrn_in_dw")
    dx, d_nm0 = _mm_nt_normbwd(dproj, w["hin"], x, nm0, dh1, "hgrn_in_dx")

    small["norm_mix"] = jnp.concatenate([d_nm0, d_nm1], axis=0)
    small["norm_ffn"] = jnp.concatenate([d_nf0, d_nf1], axis=0)
    small["lb"] = dlb.reshape(1, HGRN_HEADS * HEAD)
    small["out_norm"] = dgn.reshape(HGRN_HEADS, HEAD)
    return loss, dx, grads, small


def _place():
    x, y, c = lax.axis_index("x"), lax.axis_index("y"), lax.axis_index("c")
    others = [(1 - x, y), (x, 1 - y), (1 - x, 1 - y)]
    return x, y, c, others


ANY = pl.BlockSpec(memory_space=pl.ANY)


class _GatherRider:
    def __init__(self, shards):
        self.operands = list(shards)
        n = self.n = len(shards)
        self.out_shape = [jax.ShapeDtypeStruct((N_CHIPS,) + s.shape, s.dtype) for s in shards]
        self.scratch = [pltpu.SemaphoreType.DMA((3 * n,)), pltpu.SemaphoreType.DMA((3 * n,)),
                        pltpu.SemaphoreType.DMA((3 * n,)), pltpu.SemaphoreType.DMA((3 * n,)),
                        pltpu.SemaphoreType.DMA((n,)), pltpu.SemaphoreType.DMA((n,))]

    def _copies(self, ins, outs, sems):
        ici_send, ici_recv, _, _, own_send, own_recv = sems
        x, y, c, others = _place()
        me = 2 * x + y
        own = [pltpu.make_async_remote_copy(
            src_ref=ins[a], dst_ref=outs[a].at[me], send_sem=own_send.at[a], recv_sem=own_recv.at[a],
            device_id=(x, y, 1 - c), device_id_type=MESH) for a in range(self.n)]
        sends = [pltpu.make_async_remote_copy(
            src_ref=ins[a].at[c], dst_ref=outs[a].at[me, c], send_sem=ici_send.at[a * 3 + k], recv_sem=ici_recv.at[a * 3 + k],
            device_id=(ox, oy, c), device_id_type=MESH) for a in range(self.n) for k, (ox, oy) in enumerate(others)]
        return own, sends

    def start(self, ins, outs, sems):
        own, sends = self._copies(ins, outs, sems)
        for cp in own + sends:
            cp.start()

    def finish(self, ins, outs, sems):
        ici_send, ici_recv, d2d_send, d2d_recv, _, _ = sems
        x, y, c, others = _place()
        sibling = (x, y, 1 - c)
        own, sends = self._copies(ins, outs, sems)
        passes = []
        for a in range(self.n):
            for k, (ox, oy) in enumerate(others):
                s = a * 3 + k
                got = outs[a].at[2 * ox + oy, c]
                pltpu.make_async_remote_copy(
                    src_ref=got, dst_ref=got, send_sem=ici_send.at[s], recv_sem=ici_recv.at[s],
                    device_id=(ox, oy, c), device_id_type=MESH).wait_recv()
                fwd = pltpu.make_async_remote_copy(
                    src_ref=got, dst_ref=got, send_sem=d2d_send.at[s], recv_sem=d2d_recv.at[s],
                    device_id=sibling, device_id_type=MESH)
                fwd.start()
                passes.append(fwd)
        for a in range(self.n):
            for k, (ox, oy) in enumerate(others):
                s = a * 3 + k
                theirs = outs[a].at[2 * ox + oy, 1 - c]
                pltpu.make_async_remote_copy(
                    src_ref=theirs, dst_ref=theirs, send_sem=d2d_send.at[s], recv_sem=d2d_recv.at[s],
                    device_id=sibling, device_id_type=MESH).wait_recv()
        for cp in own:
            cp.wait()
        for cp in sends + passes:
            cp.wait_send()


class _ExchangeRider:
    def __init__(self, parts):
        self.operands = list(parts)
        n = self.n = len(parts)
        self.out_shape = [jax.ShapeDtypeStruct(p.shape, p.dtype) for p in parts]
        self.scratch = [pltpu.SemaphoreType.DMA((3 * n,)), pltpu.SemaphoreType.DMA((3 * n,))]

    def _copies(self, ins, outs, sems):
        send_sem, recv_sem = sems
        x, y, c, others = _place()
        me = 2 * x + y
        return [pltpu.make_async_remote_copy(
            src_ref=ins[a].at[2 * ox + oy], dst_ref=outs[a].at[me], send_sem=send_sem.at[a * 3 + k],
            recv_sem=recv_sem.at[a * 3 + k], device_id=(ox, oy, c), device_id_type=MESH)
            for a in range(self.n) for k, (ox, oy) in enumerate(others)]

    def start(self, ins, outs, sems):
        for cp in self._copies(ins, outs, sems):
            cp.start()

    def finish(self, ins, outs, sems):
        send_sem, recv_sem = sems
        x, y, c, others = _place()
        for a in range(self.n):
            for k, (ox, oy) in enumerate(others):
                s = a * 3 + k
                got = outs[a].at[2 * ox + oy]
                pltpu.make_async_remote_copy(
                    src_ref=got, dst_ref=got, send_sem=send_sem.at[s], recv_sem=recv_sem.at[s],
                    device_id=(ox, oy, c), device_id_type=MESH).wait_recv()
        for cp in self._copies(ins, outs, sems):
            cp.wait_send()


def _run_rider(rider, name):
    n = rider.n

    def body(*refs):
        ins, outs, sems = refs[:n], refs[n:2 * n], refs[2 * n:]
        rider.start(ins, outs, sems)
        rider.finish(ins, outs, sems)

    return pl.pallas_call(
        body, name=name, in_specs=[ANY] * n, out_specs=[ANY] * n,
        out_shape=rider.out_shape, scratch_shapes=rider.scratch)(*rider.operands)


def _ride(rider, body, n_in, n_out, first, last):
    if rider is None:
        return body
    n = rider.n

    def wrapped(*refs):
        host_in, r_in = refs[:n_in], refs[n_in:n_in + n]
        host_out = refs[n_in + n:n_in + n + n_out]
        r_out = refs[n_in + n + n_out:n_in + 2 * n + n_out]
        rest = refs[n_in + 2 * n + n_out:]
        host_scr, sems = rest[:len(rest) - len(rider.scratch)], rest[len(rest) - len(rider.scratch):]

        @pl.when(first())
        def _():
            rider.start(r_in, r_out, sems)

        body(*host_in, *host_out, *host_scr)

        @pl.when(last())
        def _():
            rider.finish(r_in, r_out, sems)

    return wrapped


def _rider_args(rider):
    if rider is None:
        return [], [], [], [], []
    return rider.operands, [ANY] * rider.n, [ANY] * rider.n, rider.out_shape, rider.scratch


def _pair_exchange(grads, name):
    n = len(grads)

    def body(*refs):
        ins, outs = refs[:n], refs[n:2 * n]
        send_sem, recv_sem = refs[2 * n:]
        x, y, c, _ = _place()
        sibling = (x, y, 1 - c)
        cps = []
        for a in range(n):
            for j in range(N_CHIPS):
                s = a * N_CHIPS + j
                cps.append(pltpu.make_async_remote_copy(
                    src_ref=ins[a].at[j, 1 - c], dst_ref=outs[a].at[j], send_sem=send_sem.at[s], recv_sem=recv_sem.at[s],
                    device_id=sibling, device_id_type=MESH))
        for cp in cps:
            cp.start()
        for cp in cps:
            cp.wait()

    return pl.pallas_call(
        body, name=name,
        in_specs=[ANY] * n, out_specs=[ANY] * n,
        out_shape=[jax.ShapeDtypeStruct((N_CHIPS,) + g.shape[2:], F32) for g in grads],
        scratch_shapes=[pltpu.SemaphoreType.DMA((N_CHIPS * n,)), pltpu.SemaphoreType.DMA((N_CHIPS * n,))],
        )(*grads)


def _pair_sum(g, got, c_idx):
    _, _, r, cw = g.shape
    tr = _row_tile(r, cw)

    def body(c_ref, g_ref, got_ref, p_ref, pb_ref):
        v = g_ref[...] + got_ref[...]
        p_ref[...] = v
        pb_ref[...] = v.astype(BF16)

    blk = pl.BlockSpec((None, tr, cw), lambda j, i, c_ref: (j, i, 0))
    return pl.pallas_call(
        body, name="grad_pair_sum",
        grid_spec=pltpu.PrefetchScalarGridSpec(
            num_scalar_prefetch=1, grid=(N_CHIPS, r // tr),
            in_specs=[pl.BlockSpec((None, None, tr, cw), lambda j, i, c_ref: (j, c_ref[0], i, 0)), blk],
            out_specs=[blk, blk]),
        out_shape=[jax.ShapeDtypeStruct((N_CHIPS, r, cw), F32), jax.ShapeDtypeStruct((N_CHIPS, r, cw), BF16)],
        compiler_params=_params(("parallel", "parallel")))(c_idx, g, got)


def _chip_sum(p, got, me_idx):
    _, r, cw = p.shape
    tr = _row_tile(r, cw)

    def body(me_ref, own_ref, got_ref, t_ref):
        me = me_ref[0]
        acc = None
        for s in range(N_CHIPS):
            term = jnp.where(me == s, own_ref[...], got_ref[s].astype(F32))
            acc = term if acc is None else acc + term
        t_ref[...] = acc

    return pl.pallas_call(
        body, name="grad_chip_sum",
        grid_spec=pltpu.PrefetchScalarGridSpec(
            num_scalar_prefetch=1, grid=(r // tr,),
            in_specs=[pl.BlockSpec((None, tr, cw), lambda i, me_ref: (me_ref[0], i, 0)),
                      pl.BlockSpec((N_CHIPS, tr, cw), lambda i, me_ref: (0, i, 0))],
            out_specs=pl.BlockSpec((tr, cw), lambda i, me_ref: (i, 0))),
        out_shape=jax.ShapeDtypeStruct((r, cw), F32),
        compiler_params=_params(("parallel",)))(me_idx, p, got)


def _pair_share(halves):
    n = len(halves)

    def body(*refs):
        ins, outs = refs[:n], refs[n:2 * n]
        send_sem, recv_sem = refs[2 * n:]
        x, y, c, _ = _place()
        cps = [pltpu.make_async_remote_copy(
            src_ref=ins[a], dst_ref=outs[a], send_sem=send_sem.at[a], recv_sem=recv_sem.at[a],
            device_id=(x, y, 1 - c), device_id_type=MESH) for a in range(n)]
        for cp in cps:
            cp.start()
        for cp in cps:
            cp.wait()

    return pl.pallas_call(
        body, name="grad_pair_share",
        in_specs=[ANY] * n, out_specs=[ANY] * n,
        out_shape=[jax.ShapeDtypeStruct(h.shape, F32) for h in halves],
        scratch_shapes=[pltpu.SemaphoreType.DMA((n,)), pltpu.SemaphoreType.DMA((n,))],
        )(*halves)


def _small_allreduce(pack):
    m_per, ncol = pack.shape
    n_dev = 8

    def body(x_ref, sum_ref, all_ref, send_sems, recv_sems, local_sem):
        x, y, c, others = _place()
        me, sibling = (x, y, c), (x, y, 1 - c)

        def rows(px, py, pc):
            return all_ref.at[pl.ds((4 * px + 2 * py + pc) * m_per, m_per), :]

        def copy(k, block, to, src=None):
            return pltpu.make_async_remote_copy(
                src_ref=rows(*block) if src is None else src, dst_ref=rows(*block),
                send_sem=send_sems.at[k], recv_sem=recv_sems.at[k], device_id=to, device_id_type=MESH)

        mine = pltpu.make_async_copy(x_ref, rows(*me), local_sem)
        mine.start()
        first = [copy(0, me, sibling, src=x_ref)]
        first += [copy(1 + j, me, (*chip, c), src=x_ref) for j, chip in enumerate(others)]
        for cp in first:
            cp.start()
        passed = [copy(4 + j, (*chip, c), sibling) for j, chip in enumerate(others)]
        for j, chip in enumerate(others):
            copy(1 + j, (*chip, c), me).wait_recv()
            passed[j].start()
        copy(0, sibling, me).wait_recv()
        for j, chip in enumerate(others):
            copy(4 + j, (*chip, 1 - c), me).wait_recv()
        for cp in first + passed:
            cp.wait_send()
        mine.wait()
        acc = all_ref[0:m_per, :]
        for dvc in range(1, n_dev):
            acc = acc + all_ref[dvc * m_per:(dvc + 1) * m_per, :]
        sum_ref[...] = acc

    return pl.pallas_call(
        body, name="small_allreduce",
        in_specs=[pl.BlockSpec(memory_space=pltpu.VMEM)],
        out_specs=pl.BlockSpec(memory_space=pltpu.VMEM),
        out_shape=jax.ShapeDtypeStruct((m_per, ncol), F32),
        scratch_shapes=[pltpu.VMEM((n_dev * m_per, ncol), F32),
                        pltpu.SemaphoreType.DMA((7,)), pltpu.SemaphoreType.DMA((7,)), pltpu.SemaphoreType.DMA],
        )(pack)


def _adam_math(w, g, m, v):
    m = ADAM_B1 * m + (1.0 - ADAM_B1) * g
    v = ADAM_B2 * v + (1.0 - ADAM_B2) * (g * g)
    m_hat = m / (1.0 - ADAM_B1 ** ADAM_STEP)
    v_hat = v / (1.0 - ADAM_B2 ** ADAM_STEP)
    delta = -ADAM_LR * (m_hat / (jnp.sqrt(v_hat) + ADAM_EPS) + ADAM_WD * w)
    return delta, m, v


def _adamw(halves, c_idx, w, m, v, name):
    L = len(halves)
    r, C = halves[0][0].shape
    tr = _row_tile(r, C, 512 * 1024)
    nt = r // tr

    def body(c_ref, *refs):
        g_refs, (w_ref, m_ref, v_ref), (g_ref, d_ref, nm_ref, nv_ref) = refs[:2 * L], refs[2 * L:2 * L + 3], refs[2 * L + 3:]
        own = pl.program_id(1) == c_ref[0]
        g = None
        for l in range(L):
            cand = jnp.where(own, g_refs[2 * l][...], g_refs[2 * l + 1][...])
            g = cand if g is None else jnp.where(pl.program_id(0) == l, cand, g)
        g_ref[...] = g
        d_ref[...], nm_ref[...], nv_ref[...] = _adam_math(w_ref[...], g, m_ref[...], v_ref[...])

    def half(l, mine):
        def index(ll, h, i, c_ref):
            read = (h == c_ref[0]) if mine else (h != c_ref[0])
            return jnp.where(jnp.logical_and(ll == l, read), i, 0), 0
        return pl.BlockSpec((tr, C), index)

    full = pl.BlockSpec((None, tr, C), lambda ll, h, i, c_ref: (ll, h * nt + i, 0))
    shp = jax.ShapeDtypeStruct((L, 2 * r, C), F32)
    g_specs = [half(l, mine) for l in range(L) for mine in (True, False)]
    return pl.pallas_call(
        body, name=name,
        grid_spec=pltpu.PrefetchScalarGridSpec(
            num_scalar_prefetch=1, grid=(L, 2, nt),
            in_specs=g_specs + [full] * 3, out_specs=[full] * 4),
        out_shape=[shp] * 4,
        compiler_params=_params(("arbitrary", "arbitrary", "arbitrary")))(
            c_idx, *[a for pair in halves for a in pair], w, m, v)


def _small_update(gsum, logits_pack, w, m, v):
    def body(gs_ref, lg_ref, w_ref, m_ref, v_ref, g_ref, d_ref, nm_ref, nv_ref):
        g_ref[...] = gs_ref[...]
        l0, l1, l2 = lg_ref[0:1, :], lg_ref[1:2, :], lg_ref[2:3, :]
        mx = jnp.maximum(jnp.maximum(l0, l1), l2)
        e0, e1, e2 = jnp.exp(l0 - mx), jnp.exp(l1 - mx), jnp.exp(l2 - mx)
        tot = e0 + e1 + e2
        p0, p1, p2 = e0 / tot, e1 / tot, e2 / tot
        dlb = gs_ref[4:5, :]
        g_ref[4:5, :] = dlb * p0 * (1.0 - p0)
        g_ref[5:6, :] = -dlb * p0 * p1
        g_ref[6:7, :] = -dlb * p0 * p2
        d_ref[...], nm_ref[...], nv_ref[...] = _adam_math(w_ref[...], g_ref[...], m_ref[...], v_ref[...])

    full = pl.BlockSpec(memory_space=pltpu.VMEM)
    shp = jax.ShapeDtypeStruct(gsum.shape, F32)
    return pl.pallas_call(
        body, name="small_update", in_specs=[full] * 5, out_specs=[full] * 4, out_shape=[shp] * 4)(
            gsum, logits_pack, w, m, v)


def _pack_small(norm_mix, norm_ffn, lb3, out_norm, final_norm, extra=None):
    ncol = norm_mix.shape[1]
    on = jnp.pad(out_norm.reshape(1, -1), ((0, 0), (0, ncol - out_norm.size)))
    rows = [norm_mix, norm_ffn, lb3, on, final_norm.reshape(1, ncol)]
    if extra is not None:
        rows.append(extra)
    used = sum(r.shape[0] for r in rows)
    rows.append(jnp.zeros((SMALL_ROWS - used, ncol), F32))
    return jnp.concatenate(rows, axis=0)


WEIGHT_NAMES = ("hin", "hout", "qkv", "aout", "fin0", "fin1", "fdn0", "fdn1")
FIRST_WEIGHTS = ("hin",)
LATE_WEIGHTS_A = ("hout", "fin0", "fdn0")
LATE_WEIGHTS_B = ("qkv", "aout", "fin1", "fdn1")
EARLY_GRADS = ("qkv", "aout", "fin0", "fin1", "fdn0", "fdn1")


def _split_weights(hgrn_w_in, hgrn_w_out, attn_w_qkv, attn_w_out, ffn_w_in, ffn_w_down):
    return {"hin": hgrn_w_in[0], "hout": hgrn_w_out[0], "qkv": attn_w_qkv[0], "aout": attn_w_out[0],
            "fin0": ffn_w_in[0], "fin1": ffn_w_in[1], "fdn0": ffn_w_down[0], "fdn1": ffn_w_down[1]}


def _halves(v):
    r, c = v.shape
    return v.reshape(2, r // 2, c)


def _full_weights(gathered):
    out = {}
    for k, g in gathered.items():
        _, _, r, c = g.shape
        if k in ("hin", "qkv", "fin0", "fin1"):
            out[k] = g.reshape(N_CHIPS, 2 * r, c)
        else:
            out[k] = g.reshape(N_CHIPS * 2 * r, c)
    return out


class _StepComm:
    def __init__(self, shards, c_idx, me_idx):
        self.shards, self.c_idx, self.me_idx = shards, c_idx, me_idx
        self.halves = {}

    def gather_rider(self, names):
        return _GatherRider([_halves(self.shards[k].astype(BF16)) for k in names])

    def gathered(self, names, got):
        return _full_weights(dict(zip(names, got)))

    def first_weights(self):
        return self.gathered(FIRST_WEIGHTS, _run_rider(self.gather_rider(FIRST_WEIGHTS), "gather_first"))

    def _pair_sums(self, grads, name):
        names = list(grads)
        g4 = []
        for k in names:
            r, c = self.shards[k].shape
            g4.append(grads[k].reshape(N_CHIPS, 2, r // 2, c))
        from_sibling = _pair_exchange(g4, name)
        return names, [_pair_sum(g, got, self.c_idx) for g, got in zip(g4, from_sibling)]

    def _chip_sums(self, names, sums, got):
        for k, s, g in zip(names, sums, got):
            self.halves[k] = _chip_sum(s[0], g, self.me_idx)

    def exchange_rider(self, grads):
        self._riding = self._pair_sums(grads, "grad_pair_exchange_early")
        return _ExchangeRider([s[1] for s in self._riding[1]])

    def exchanged(self, got):
        self._chip_sums(*self._riding, got)

    def reduce_rest(self, grads):
        names, sums = self._pair_sums(grads, "grad_pair_exchange_late")
        self._chip_sums(names, sums, _run_rider(_ExchangeRider([s[1] for s in sums]), "grad_chip_exchange_late"))

    def shared_halves(self):
        mine = [self.halves[k] for k in WEIGHT_NAMES]
        return dict(zip(WEIGHT_NAMES, zip(mine, _pair_share(mine))))


def kernel(x, norm_mix, norm_ffn, hgrn_w_in, hgrn_lb_logits, hgrn_out_norm, hgrn_w_out, attn_w_qkv, attn_w_out, ffn_w_in, ffn_w_down, final_norm, loss_target, m_norm_mix, m_norm_ffn, m_hgrn_w_in, m_hgrn_lb_logits, m_hgrn_out_norm, m_hgrn_w_out, m_attn_w_qkv, m_attn_w_out, m_ffn_w_in, m_ffn_w_down, m_final_norm, v_norm_mix, v_norm_ffn, v_hgrn_w_in, v_hgrn_lb_logits, v_hgrn_out_norm, v_hgrn_w_out, v_attn_w_qkv, v_attn_w_out, v_ffn_w_in, v_ffn_w_down, v_final_norm):
    S = x.shape[1]
    xi, yi, ci = lax.axis_index("x"), lax.axis_index("y"), lax.axis_index("c")
    c_idx = jnp.reshape(ci, (1,)).astype(jnp.int32)
    me_idx = jnp.reshape(2 * xi + yi, (1,)).astype(jnp.int32)

    w_own = _split_weights(hgrn_w_in, hgrn_w_out, attn_w_qkv, attn_w_out, ffn_w_in, ffn_w_down)

    comm = _StepComm(w_own, c_idx, me_idx)
    loss, dx, grads, small = _local_step(
        x.reshape(S, D_MODEL), loss_target.reshape(S, D_MODEL), norm_mix, norm_ffn, hgrn_lb_logits,
        hgrn_out_norm, final_norm.reshape(1, D_MODEL), comm)
    comm.reduce_rest(grads)

    halves = comm.shared_halves()
    updated = {}
    for tensor, layers, (wt, mt, vt) in (
            ("hgrn_w_in", ("hin",), (hgrn_w_in, m_hgrn_w_in, v_hgrn_w_in)),
            ("hgrn_w_out", ("hout",), (hgrn_w_out, m_hgrn_w_out, v_hgrn_w_out)),
            ("attn_w_qkv", ("qkv",), (attn_w_qkv, m_attn_w_qkv, v_attn_w_qkv)),
            ("attn_w_out", ("aout",), (attn_w_out, m_attn_w_out, v_attn_w_out)),
            ("ffn_w_in", ("fin0", "fin1"), (ffn_w_in, m_ffn_w_in, v_ffn_w_in)),
            ("ffn_w_down", ("fdn0", "fdn1"), (ffn_w_down, m_ffn_w_down, v_ffn_w_down))):
        updated[tensor] = _adamw([halves[k] for k in layers], c_idx, wt, mt, vt, "adamw_" + tensor)

    loss_row = jnp.pad(loss, ((0, 0), (0, D_MODEL - loss.shape[1])))
    lb3 = jnp.concatenate([small["lb"], jnp.zeros((2, D_MODEL), F32)], axis=0)
    on_grad = jnp.sum(small["out_norm"], axis=0, keepdims=True)
    pack = _pack_small(small["norm_mix"], small["norm_ffn"], lb3, on_grad, small["final_norm"], loss_row)
    gsum = _small_allreduce(pack)
    w_s = _pack_small(norm_mix, norm_ffn, hgrn_lb_logits, hgrn_out_norm, final_norm)
    m_s = _pack_small(m_norm_mix, m_norm_ffn, m_hgrn_lb_logits, m_hgrn_out_norm, m_final_norm)
    v_s = _pack_small(v_norm_mix, v_norm_ffn, v_hgrn_lb_logits, v_hgrn_out_norm, v_final_norm)
    lg_pack = jnp.pad(hgrn_lb_logits, ((0, 8 - hgrn_lb_logits.shape[0]), (0, 0)))
    sg, sd, sm, sv = _small_update(gsum, lg_pack, w_s, m_s, v_s)

    def unpack(p):
        return (p[0:2], p[2:4], p[4:7], p[7:8, :HEAD], p[8])

    def assemble(p, which):
        nmx, nff, lbl, onm, fnm = unpack(p)
        hin, hout, qkv, aout, fin, fdn = [updated[t][which] for t in
                                          ("hgrn_w_in", "hgrn_w_out", "attn_w_qkv", "attn_w_out", "ffn_w_in", "ffn_w_down")]
        return (nmx, nff, hin, lbl, onm, hout, qkv, aout, fin, fdn, fnm)

    total_loss = gsum[9, 0]
    return (total_loss, dx.reshape(1, S, D_MODEL), *assemble(sg, 0), *assemble(sd, 1), *assemble(sm, 2), *assemble(sv, 3))
```

```python
import functools

import jax
import jax.numpy as jnp
from jax import lax
from jax.experimental import pallas as pl
from jax.experimental.pallas import tpu as pltpu

F32 = jnp.float32
BF16 = jnp.bfloat16
MESH = pl.DeviceIdType.MESH

D_MODEL = 1024
HEAD = 128
HGRN_HEADS = 8
HGRN_CHUNK = 64
ATTN_GROUPS = ((128, 1), (512, 4), (2048, 16))
ATTN_SPAN = 128
HEADS_PER_GROUP = 4
GROUP_W = HEADS_PER_GROUP * HEAD
D_FF = 2816
NORM_EPS = 1e-6
ROPE_THETA = 10000.0
NEG = -1e30

ADAM_LR, ADAM_B1, ADAM_B2, ADAM_EPS, ADAM_WD, ADAM_STEP = 0.001, 0.9, 0.999, 1e-08, 0.01, 10

N_CHIPS = 4
VMEM_LIMIT = 56 * 1024 * 1024
SMALL_ROWS = 16


def _params(sem=None):
    return pltpu.CompilerParams(dimension_semantics=sem, vmem_limit_bytes=VMEM_LIMIT)


def _row_tile(rows, cols, budget_bytes=3 * 512 * 1024):
    best = 8
    for t in range(8, rows + 1, 8):
        if rows % t == 0 and t * cols * 4 <= budget_bytes:
            best = t
    assert rows % best == 0
    return best


def _grid_corner(i, j):
    return jnp.logical_and(pl.program_id(0) == i, pl.program_id(1) == j)


def _sigmoid(v):
    return 0.5 * jnp.tanh(0.5 * v) + 0.5


def _dot(a, b):
    return jnp.dot(a, b, preferred_element_type=F32)


def _dot_nt(a, b):
    return lax.dot_general(a, b, (((1,), (1,)), ((), ())), preferred_element_type=F32)


def _dot_tn(a, b):
    return lax.dot_general(a, b, (((0,), (0,)), ((), ())), preferred_element_type=F32)


def _dot_exact(ones, b):
    ones = ones.astype(BF16)
    hi = b.astype(BF16)
    rest = b - hi.astype(F32)
    mid = rest.astype(BF16)
    low = (rest - mid.astype(F32)).astype(BF16)
    return _dot(ones, hi) + _dot(ones, mid) + _dot(ones, low)


def _rstd(v):
    return lax.rsqrt(jnp.mean(v * v, axis=-1, keepdims=True) + NORM_EPS)


def _norm_mm(h, gain, w3, name, out_dtype=F32, tm=1024, rider=None):
    S, K = h.shape
    J, _, n = w3.shape
    gi = S // tm

    def body(h_ref, g_ref, w_ref, y_ref, u_ref):
        @pl.when(pl.program_id(1) == 0)
        def _():
            v = h_ref[...]
            u_ref[...] = (v * _rstd(v) * g_ref[...]).astype(BF16)

        y_ref[...] = _dot(u_ref[...], w_ref[pl.program_id(1)]).astype(y_ref.dtype)

    r_ops, r_in, r_out, r_shape, r_scr = _rider_args(rider)
    res = pl.pallas_call(
        _ride(rider, body, 3, 2, functools.partial(_grid_corner, 0, 0), functools.partial(_grid_corner, gi - 1, J - 1)),
        name=name, grid=(gi, J),
        in_specs=[pl.BlockSpec((tm, K), lambda i, j: (i, 0)),
                  pl.BlockSpec((1, K), lambda i, j: (0, 0)),
                  pl.BlockSpec((J, K, n), lambda i, j: (0, 0, 0))] + r_in,
        out_specs=[pl.BlockSpec((tm, n), lambda i, j: (i, j)), pl.BlockSpec((tm, K), lambda i, j: (i, 0))] + r_out,
        out_shape=[jax.ShapeDtypeStruct((S, J * n), out_dtype), jax.ShapeDtypeStruct((S, K), BF16)] + r_shape,
        scratch_shapes=r_scr,
        compiler_params=_params(("arbitrary", "arbitrary")))(h, gain, w3, *r_ops)
    return res[0], res[1], res[2:]


def _mm_res(h, a, w2, name, tm=512):
    S, N = h.shape
    K = a.shape[1]

    def body(h_ref, a_ref, w_ref, o_ref):
        o_ref[...] = h_ref[...] + _dot(a_ref[...], w_ref[...])

    return pl.pallas_call(
        body, name=name, grid=(S // tm,),
        in_specs=[pl.BlockSpec((tm, N), lambda i: (i, 0)),
                  pl.BlockSpec((tm, K), lambda i: (i, 0)),
                  pl.BlockSpec((K, N), lambda i: (0, 0))],
        out_specs=pl.BlockSpec((tm, N), lambda i: (i, 0)),
        out_shape=jax.ShapeDtypeStruct((S, N), F32),
        compiler_params=_params(("parallel",)))(h, a, w2)


def _swiglu(z_ref, F):
    g = z_ref[:, :F].astype(F32)
    return (g * _sigmoid(g) * z_ref[:, F:].astype(F32)).astype(BF16)


def _swiglu_mm_res(h, z, w2, name, tm=256):
    S, N = h.shape
    F = w2.shape[0]

    def body(h_ref, z_ref, w_ref, o_ref, a_ref):
        a = _swiglu(z_ref, F)
        a_ref[...] = a
        o_ref[...] = h_ref[...] + _dot(a, w_ref[...])

    return pl.pallas_call(
        body, name=name, grid=(S // tm,),
        in_specs=[pl.BlockSpec((tm, N), lambda i: (i, 0)),
                  pl.BlockSpec((tm, 2 * F), lambda i: (i, 0)),
                  pl.BlockSpec((F, N), lambda i: (0, 0))],
        out_specs=[pl.BlockSpec((tm, N), lambda i: (i, 0)), pl.BlockSpec((tm, F), lambda i: (i, 0))],
        out_shape=[jax.ShapeDtypeStruct((S, N), F32), jax.ShapeDtypeStruct((S, F), BF16)],
        compiler_params=_params(("parallel",)))(h, z, w2)


def _dy_specs(dy, J, n, tm):
    if dy.ndim == 3:
        return [pl.BlockSpec((None, tm, n), functools.partial(lambda i, j: (j, i, 0), j=j)) for j in range(J)]
    return [pl.BlockSpec((tm, n), functools.partial(lambda i, j: (i, j), j=j)) for j in range(J)]


def _acc_nt(dy_refs, w_ref):
    acc = None
    for j, r in enumerate(dy_refs):
        t = _dot_nt(r[...].astype(BF16), w_ref[j])
        acc = t if acc is None else acc + t
    return acc


def _mm_nt(dy, w3, name, out_dtype=F32, tm=512):
    J, K, n = w3.shape
    S = dy.shape[-2]

    def body(*refs):
        dy_refs, w_ref, o_ref = refs[:J], refs[J], refs[J + 1]
        o_ref[...] = _acc_nt(dy_refs, w_ref).astype(o_ref.dtype)

    return pl.pallas_call(
        body, name=name, grid=(S // tm,),
        in_specs=_dy_specs(dy, J, n, tm) + [pl.BlockSpec((J, K, n), lambda i: (0, 0, 0))],
        out_specs=pl.BlockSpec((tm, K), lambda i: (i, 0)),
        out_shape=jax.ShapeDtypeStruct((S, K), out_dtype),
        compiler_params=_params(("parallel",)))(*([dy] * J), w3)


def _mm_nt_normbwd(dy, w3, h, gain, dh, name, tm=512, rider=None):
    J, K, n = w3.shape
    S = h.shape[0]
    steps = S // tm

    def body(*refs):
        dy_refs, w_ref, h_ref, g_ref, dh_ref, o_ref, dg_ref = refs[:J], *refs[J:]
        du = _acc_nt(dy_refs, w_ref)
        v = h_ref[...]
        r = _rstd(v)
        xh = v * r
        dyg = du * g_ref[...]
        o_ref[...] = dh_ref[...] + r * (dyg - xh * jnp.mean(dyg * xh, axis=-1, keepdims=True))

        @pl.when(pl.program_id(0) == 0)
        def _():
            dg_ref[...] = jnp.zeros_like(dg_ref)

        dg_ref[...] += jnp.sum(du * xh, axis=0, keepdims=True)

    row = pl.BlockSpec((tm, K), lambda i: (i, 0))
    vec = pl.BlockSpec((1, K), lambda i: (0, 0))
    r_ops, r_in, r_out, r_shape, r_scr = _rider_args(rider)
    res = pl.pallas_call(
        _ride(rider, body, J + 4, 2, lambda: pl.program_id(0) == 0, lambda: pl.program_id(0) == steps - 1),
        name=name, grid=(steps,),
        in_specs=_dy_specs(dy, J, n, tm) + [pl.BlockSpec((J, K, n), lambda i: (0, 0, 0)), row, vec, row] + r_in,
        out_specs=[row, vec] + r_out,
        out_shape=[jax.ShapeDtypeStruct((S, K), F32), jax.ShapeDtypeStruct((1, K), F32)] + r_shape,
        scratch_shapes=r_scr,
        compiler_params=_params(("arbitrary",)))(*([dy] * J), w3, h, gain, dh, *r_ops)
    return res[0], res[1], res[2:]


def _mm_nt_swiglu_bwd(dh, w2, z, name, tm=256):
    F, N = w2.shape
    S = dh.shape[0]

    def body(dh_ref, w_ref, z_ref, o_ref):
        da = _dot_nt(dh_ref[...].astype(BF16), w_ref[...])
        g = z_ref[:, :F].astype(F32)
        u = z_ref[:, F:].astype(F32)
        sg = _sigmoid(g)
        o_ref[:, :F] = (da * u * (sg * (1.0 + g * (1.0 - sg)))).astype(BF16)
        o_ref[:, F:] = (da * (g * sg)).astype(BF16)

    return pl.pallas_call(
        body, name=name, grid=(S // tm,),
        in_specs=[pl.BlockSpec((tm, N), lambda i: (i, 0)),
                  pl.BlockSpec((F, N), lambda i: (0, 0)),
                  pl.BlockSpec((tm, 2 * F), lambda i: (i, 0))],
        out_specs=pl.BlockSpec((tm, 2 * F), lambda i: (i, 0)),
        out_shape=jax.ShapeDtypeStruct((S, 2 * F), BF16),
        compiler_params=_params(("parallel",)))(dh, w2, z)


def _mm_tn(x, dy, J, n, tn, name):
    tpn = n // tn
    ts = 1024
    S, K = x.shape
    if dy.ndim == 3:
        dy_spec = pl.BlockSpec((None, ts, tn), lambda c, s: (c // tpn, s, c % tpn))
    else:
        dy_spec = pl.BlockSpec((ts, tn), lambda c, s: (s, c))

    def body(x_ref, dy_ref, o_ref):
        @pl.when(pl.program_id(1) == 0)
        def _():
            o_ref[...] = jnp.zeros_like(o_ref)

        o_ref[...] += _dot_tn(x_ref[...], dy_ref[...].astype(BF16))

    return pl.pallas_call(
        body, name=name, grid=(J * tpn, S // ts),
        in_specs=[pl.BlockSpec((ts, K), lambda c, s: (s, 0)), dy_spec],
        out_specs=pl.BlockSpec((None, K, tn), lambda c, s: (c // tpn, 0, c % tpn)),
        out_shape=jax.ShapeDtypeStruct((J, K, n), F32),
        compiler_params=_params(("parallel", "arbitrary")))(x, dy)


def _loss_head(h, gain, target, tm=512):
    S, K = h.shape

    def body(h_ref, g_ref, t_ref, dh_ref, loss_ref, dg_ref):
        v = h_ref[...]
        r = _rstd(v)
        xh = v * r
        g = g_ref[...]
        dy = (xh * g - t_ref[...]) * (1.0 / K)
        dyg = dy * g
        dh_ref[...] = r * (dyg - xh * jnp.mean(dyg * xh, axis=-1, keepdims=True))

        @pl.when(pl.program_id(0) == 0)
        def _():
            loss_ref[...] = jnp.zeros_like(loss_ref)
            dg_ref[...] = jnp.zeros_like(dg_ref)

        part = jnp.sum(jnp.sum(dy * dy, axis=-1, keepdims=True), axis=0, keepdims=True) * (0.5 * K)
        lane = lax.broadcasted_iota(jnp.int32, loss_ref.shape, 1)
        loss_ref[...] += jnp.where(lane == 0, part, 0.0)
        dg_ref[...] += jnp.sum(dy * xh, axis=0, keepdims=True)

    row = pl.BlockSpec((tm, K), lambda i: (i, 0))
    vec = pl.BlockSpec((1, K), lambda i: (0, 0))
    return pl.pallas_call(
        body, name="loss_head", grid=(S // tm,),
        in_specs=[row, vec, row],
        out_specs=[row, pl.BlockSpec((1, HEAD), lambda i: (0, 0)), vec],
        out_shape=[jax.ShapeDtypeStruct((S, K), F32), jax.ShapeDtypeStruct((1, HEAD), F32),
                   jax.ShapeDtypeStruct((1, K), F32)],
        compiler_params=_params(("arbitrary",)))(h, gain, target)


def _lower_bound(lg_ref):
    l0, l1, l2 = lg_ref[0:1, :], lg_ref[1:2, :], lg_ref[2:3, :]
    mx = jnp.maximum(jnp.maximum(l0, l1), l2)
    e0, e1, e2 = jnp.exp(l0 - mx), jnp.exp(l1 - mx), jnp.exp(l2 - mx)
    return e0 / (e0 + e1 + e2)


def _chunks(v, ncb):
    C = HGRN_CHUNK
    return [v[c * C:(c + 1) * C] for c in range(ncb)]


def _rows(parts):
    return jnp.concatenate(parts, axis=0)


def _block_gates(qz, fz, lb, ncb):
    C = HGRN_CHUNK
    row = lax.broadcasted_iota(jnp.int32, (C, C), 0)
    col = lax.broadcasted_iota(jnp.int32, (C, C), 1)
    tri = (col <= row).astype(F32)
    first_half = lax.broadcasted_iota(jnp.int32, (C, HEAD), 0) < C // 2
    sig = _sigmoid(fz)
    fg = lb + (1.0 - lb) * sig
    key = 1.0 - fg
    lg = jnp.log(fg)
    lgs = _chunks(lg, ncb)
    b = _rows([_dot_exact(tri, v) for v in lgs])
    r_c = [jnp.sum(jnp.where(first_half, v, 0.0), axis=0, keepdims=True) for v in lgs]
    bl_c = [jnp.sum(v, axis=0, keepdims=True) for v in lgs]
    r = _rows([jnp.broadcast_to(v, (C, HEAD)) for v in r_c])
    e_br, e_rb = jnp.exp(b - r), jnp.exp(r - b)
    e_b = e_br * _rows([jnp.broadcast_to(jnp.exp(v), (C, HEAD)) for v in r_c])
    e_lb = e_rb * _rows([jnp.broadcast_to(jnp.exp(e - v), (C, HEAD)) for e, v in zip(bl_c, r_c)])
    sq = _sigmoid(qz)
    qy = qz * sq
    return sig, fg, key, (e_br, e_rb, e_b, e_lb), bl_c, sq, qy


def _hgrn_fwd(proj, logits, gain, tb=1024, rider=None):
    S = proj.shape[0]
    H, C = HGRN_HEADS, HGRN_CHUNK
    ncb = tb // C

    def body(q_ref, f_ref, i_ref, g_ref, lg_ref, gn_ref, o_ref, og_ref, st_ref, state):
        @pl.when(pl.program_id(1) == 0)
        def _():
            state[...] = jnp.zeros_like(state)

        lb = _lower_bound(lg_ref)
        causal = lax.broadcasted_iota(jnp.int32, (C, C), 1) <= lax.broadcasted_iota(jnp.int32, (C, C), 0)
        qz, fz, gz = q_ref[...], f_ref[...], g_ref[...]
        _, _, key, (e_br, e_rb, e_b, e_lb), bl_c, _, qy = _block_gates(qz, fz, lb, ncb)
        qs = _chunks((qy * e_br).astype(BF16), ncb)
        ks = _chunks((key * e_rb).astype(BF16), ncb)
        qb = _chunks((qy * e_b).astype(BF16), ncb)
        ke = _chunks((key * e_lb).astype(BF16), ncb)
        vb = _chunks(i_ref[...].astype(BF16), ncb)
        o_intra, upd = [], []
        for c in range(ncb):
            a = jnp.where(causal, _dot_nt(qs[c], ks[c]), 0.0).astype(BF16)
            o_intra.append(_dot(a, vb[c]))
            upd.append(_dot_tn(vb[c], ke[c]))
        st = state[...]
        for c in range(ncb):
            st_ref[c] = st
            st = st * jnp.exp(bl_c[c]) + upd[c]
        state[...] = st
        o = _rows([_dot_nt(qb[c], st_ref[c].astype(BF16)) + o_intra[c] for c in range(ncb)])
        o_ref[...] = o
        og_ref[...] = ((o * _rstd(o) * gn_ref[...]) * (gz * _sigmoid(gz))).astype(BF16)

    def part(p):
        return pl.BlockSpec((tb, HEAD), functools.partial(lambda h, i, p: (i, p * H + h), p=p))

    nb = S // tb
    r_ops, r_in, r_out, r_shape, r_scr = _rider_args(rider)
    res = pl.pallas_call(
        _ride(rider, body, 6, 3, functools.partial(_grid_corner, 0, 0), functools.partial(_grid_corner, H - 1, nb - 1)),
        name="hgrn_fwd", grid=(H, nb),
        in_specs=[part(0), part(1), part(2), part(3),
                  pl.BlockSpec((3, HEAD), lambda h, i: (0, h)),
                  pl.BlockSpec((1, HEAD), lambda h, i: (0, 0))] + r_in,
        out_specs=[pl.BlockSpec((tb, HEAD), lambda h, i: (i, h)),
                   pl.BlockSpec((tb, HEAD), lambda h, i: (i, h)),
                   pl.BlockSpec((None, ncb, HEAD, HEAD), lambda h, i: (h, i, 0, 0))] + r_out,
        out_shape=[jax.ShapeDtypeStruct((S, H * HEAD), F32),
                   jax.ShapeDtypeStruct((S, H * HEAD), BF16),
                   jax.ShapeDtypeStruct((H, S // C, HEAD, HEAD), F32)] + r_shape,
        scratch_shapes=[pltpu.VMEM((HEAD, HEAD), F32)] + r_scr,
        compiler_params=_params(("arbitrary", "arbitrary")))(proj, proj, proj, proj, logits, gain, *r_ops)
    return res[:3], res[3:]


def _hgrn_bwd(proj, logits, gain, o, states, dog, tb=1024, rider=None):
    S = proj.shape[0]
    H, C = HGRN_HEADS, HGRN_CHUNK
    ncb = tb // C
    nb = S // tb

    def body(q_ref, f_ref, i_ref, g_ref, lg_ref, gn_ref, o_ref, st_ref, dog_ref,
             dp_ref, dlb_ref, dgn_ref, dstate, dst_scr):
        @pl.when(pl.program_id(1) == 0)
        def _():
            dstate[...] = jnp.zeros_like(dstate)
            dlb_ref[...] = jnp.zeros_like(dlb_ref)
            dgn_ref[...] = jnp.zeros_like(dgn_ref)

        lb = _lower_bound(lg_ref)
        oml = 1.0 - lb
        gn = gn_ref[...]
        row = lax.broadcasted_iota(jnp.int32, (C, C), 0)
        col = lax.broadcasted_iota(jnp.int32, (C, C), 1)
        causal = col <= row
        tri_up = (col >= row).astype(F32)
        qz, fz, gz = q_ref[...], f_ref[...], g_ref[...]
        sig, fg, key, (e_br, e_rb, e_b, e_lb), bl_c, sq, qy = _block_gates(qz, fz, lb, ncb)
        qs_v, ks_v = (qy * e_br).astype(BF16), (key * e_rb).astype(BF16)
        qb_v, ke_v = (qy * e_b).astype(BF16), (key * e_lb).astype(BF16)
        qs, ks, qb, ke = _chunks(qs_v, ncb), _chunks(ks_v, ncb), _chunks(qb_v, ncb), _chunks(ke_v, ncb)
        vb = _chunks(i_ref[...].astype(BF16), ncb)
        ov = o_ref[...]
        rs = _rstd(ov)
        xh = ov * rs
        sg = _sigmoid(gz)
        dog_v = dog_ref[...]
        dgz = dog_v * (xh * gn) * (sg * (1.0 + gz * (1.0 - sg)))
        don = dog_v * (gz * sg)
        dgn_ref[...] += jnp.sum(don * xh, axis=0, keepdims=True)
        dyg = don * gn
        do = rs * (dyg - xh * jnp.mean(dyg * xh, axis=-1, keepdims=True))
        dob = _chunks(do.astype(BF16), ncb)
        dv_in, dqs, dks, wst = [], [], [], []
        for c in range(ncb):
            a = jnp.where(causal, _dot_nt(qs[c], ks[c]), 0.0).astype(BF16)
            da = jnp.where(causal, _dot_nt(dob[c], vb[c]), 0.0).astype(BF16)
            dv_in.append(_dot_tn(a, dob[c]))
            dqs.append(_dot(da, ks[c]))
            dks.append(_dot_tn(da, qs[c]))
            wst.append(_dot_tn(dob[c], qb[c]))
        e_l = [jnp.exp(v) for v in bl_c]
        dst = dstate[...]
        for c in reversed(range(ncb)):
            dst_scr[c] = dst
            dst = wst[c] + dst * e_l[c]
        dstate[...] = dst
        dv, dqb, dke, dbl_st = [], [], [], []
        for c in range(ncb):
            dst1 = dst_scr[c]
            st0 = st_ref[c]
            dst1b = dst1.astype(BF16)
            dv.append(dv_in[c] + _dot_nt(ke[c], dst1b))
            dqb.append(_dot(dob[c], st0.astype(BF16)))
            dke.append(_dot(vb[c], dst1b))
            dbl_st.append(jnp.sum(dst1 * st0, axis=0, keepdims=True) * e_l[c])
        dqs, dks, dqb, dke, dv = _rows(dqs), _rows(dks), _rows(dqb), _rows(dke), _rows(dv)
        dke_ke = dke * ke_v.astype(F32)
        db = dqs * qs_v.astype(F32) - dks * ks_v.astype(F32) + dqb * qb_v.astype(F32) - dke_ke
        dlg = []
        for c, (db_c, kk_c) in enumerate(zip(_chunks(db, ncb), _chunks(dke_ke, ncb))):
            dbl = jnp.sum(kk_c, axis=0, keepdims=True) + dbl_st[c]
            dlg.append(_dot_exact(tri_up, db_c) + dbl)
        dlg = _rows(dlg)
        dkey = dks * e_rb + dke * e_lb
        dqy = dqs * e_br + dqb * e_b
        dfg = dlg / fg - dkey
        dlb_ref[...] += jnp.sum(dfg * (1.0 - sig), axis=0, keepdims=True)
        dp_ref[0] = (dqy * (sq * (1.0 + qz * (1.0 - sq)))).astype(BF16)
        dp_ref[1] = (dfg * oml * sig * (1.0 - sig)).astype(BF16)
        dp_ref[2] = dv.astype(BF16)
        dp_ref[3] = dgz.astype(BF16)

    def part(p):
        return pl.BlockSpec((tb, HEAD), functools.partial(lambda h, i, p: (nb - 1 - i, p * H + h), p=p))

    blk = pl.BlockSpec((tb, HEAD), lambda h, i: (nb - 1 - i, h))
    acc = pl.BlockSpec((None, 1, HEAD), lambda h, i: (h, 0, 0))
    r_ops, r_in, r_out, r_shape, r_scr = _rider_args(rider)
    res = pl.pallas_call(
        _ride(rider, body, 9, 3, functools.partial(_grid_corner, 0, 0), functools.partial(_grid_corner, H - 1, nb - 1)),
        name="hgrn_bwd", grid=(H, nb),
        in_specs=[part(0), part(1), part(2), part(3),
                  pl.BlockSpec((3, HEAD), lambda h, i: (0, h)),
                  pl.BlockSpec((1, HEAD), lambda h, i: (0, 0)),
                  blk,
                  pl.BlockSpec((None, ncb, HEAD, HEAD), lambda h, i: (h, nb - 1 - i, 0, 0)),
                  blk] + r_in,
        out_specs=[pl.BlockSpec((4, tb, HEAD), lambda h, i: (0, nb - 1 - i, h)), acc, acc] + r_out,
        out_shape=[jax.ShapeDtypeStruct((4, S, H * HEAD), BF16),
                   jax.ShapeDtypeStruct((H, 1, HEAD), F32),
                   jax.ShapeDtypeStruct((H, 1, HEAD), F32)] + r_shape,
        scratch_shapes=[pltpu.VMEM((HEAD, HEAD), F32), pltpu.VMEM((ncb, HEAD, HEAD), F32)] + r_scr,
        compiler_params=_params(("arbitrary", "arbitrary")))(
            proj, proj, proj, proj, logits, gain, o, states, dog, *r_ops)
    return res[:3], res[3:]


def _rope(v, cos, sin):
    return v * cos + pltpu.roll(v, HEAD // 2, 1) * sin


def _band_masks():
    qi = lax.broadcasted_iota(jnp.int32, (ATTN_SPAN, ATTN_SPAN), 0)
    kj = lax.broadcasted_iota(jnp.int32, (ATTN_SPAN, ATTN_SPAN), 1)
    return kj <= qi, kj >= qi


def _attn_fwd(a):
    d, L, _ = a.shape
    nb = L // ATTN_SPAN
    scale = HEAD ** -0.5

    def body(q_ref, kc_ref, kp_ref, vc_ref, vp_ref, o_ref, lse_ref):
        n = pl.program_id(1)
        mask_c, mask_p0 = _band_masks()
        mask_p = jnp.logical_and(mask_p0, n > 0)
        heads = [slice(hh * HEAD, (hh + 1) * HEAD) for hh in range(HEADS_PER_GROUP)]
        s_c = [jnp.where(mask_c, _dot_nt(q_ref[:, c], kc_ref[:, c]) * scale, NEG) for c in heads]
        s_p = [jnp.where(mask_p, _dot_nt(q_ref[:, c], kp_ref[:, c]) * scale, NEG) for c in heads]
        m = [jnp.maximum(jnp.max(a, axis=-1, keepdims=True), jnp.max(b, axis=-1, keepdims=True)) for a, b in zip(s_c, s_p)]
        p_c = [jnp.exp(a - mm) for a, mm in zip(s_c, m)]
        p_p = [jnp.exp(b - mm) for b, mm in zip(s_p, m)]
        l = [jnp.sum(a, axis=-1, keepdims=True) + jnp.sum(b, axis=-1, keepdims=True) for a, b in zip(p_c, p_p)]
        acc = [_dot(a.astype(BF16), vc_ref[:, c]) + _dot(b.astype(BF16), vp_ref[:, c]) for a, b, c in zip(p_c, p_p, heads)]
        for c, a, ll, mm in zip(heads, acc, l, m):
            o_ref[:, c] = a / ll
            lse_ref[:, c] = jnp.broadcast_to(mm + jnp.log(ll), (ATTN_SPAN, HEAD))

    def blk(part, prev):
        if prev:
            return pl.BlockSpec((None, ATTN_SPAN, GROUP_W), functools.partial(lambda r, n, p: (r, jnp.maximum(n - 1, 0), p), p=part))
        return pl.BlockSpec((None, ATTN_SPAN, GROUP_W), functools.partial(lambda r, n, p: (r, n, p), p=part))

    out = pl.BlockSpec((None, ATTN_SPAN, GROUP_W), lambda r, n: (r, n, 0))
    return pl.pallas_call(
        body, name=f"attn_fwd_d{d}", grid=(d, nb),
        in_specs=[blk(0, False), blk(1, False), blk(1, True), blk(2, False), blk(2, True)],
        out_specs=[out, out],
        out_shape=[jax.ShapeDtypeStruct((d, L, GROUP_W), F32), jax.ShapeDtypeStruct((d, L, GROUP_W), F32)],
        compiler_params=_params(("parallel", "arbitrary")))(a, a, a, a, a)


def _attn_bwd(a, do, lse, dd):
    d, L, _ = a.shape
    nb = L // ATTN_SPAN
    scale = HEAD ** -0.5

    def body(qc_ref, qn_ref, kp_ref, kc_ref, vp_ref, vc_ref, doc_ref, don_ref, lc_ref, ln_ref, ddc_ref, ddn_ref, da_ref):
        n = pl.program_id(1)
        mask_c, mask_p0 = _band_masks()
        mask_p = jnp.logical_and(mask_p0, n > 0)
        mask_n = jnp.logical_and(mask_p0, n < nb - 1)
        H4 = range(HEADS_PER_GROUP)
        heads = [slice(hh * HEAD, (hh + 1) * HEAD) for hh in H4]
        do_c = [doc_ref[:, c] for c in heads]
        do_n = [don_ref[:, c] for c in heads]
        p_c = [jnp.where(mask_c, jnp.exp(_dot_nt(qc_ref[:, c], kc_ref[:, c]) * scale - lc_ref[:, c]), 0.0) for c in heads]
        p_p = [jnp.where(mask_p, jnp.exp(_dot_nt(qc_ref[:, c], kp_ref[:, c]) * scale - lc_ref[:, c]), 0.0) for c in heads]
        p_n = [jnp.where(mask_n, jnp.exp(_dot_nt(qn_ref[:, c], kc_ref[:, c]) * scale - ln_ref[:, c]), 0.0) for c in heads]
        ds_c = [(p_c[i] * (_dot_nt(do_c[i], vc_ref[:, heads[i]]) + ddc_ref[:, heads[i]])).astype(BF16) for i in H4]
        ds_p = [(p_p[i] * (_dot_nt(do_c[i], vp_ref[:, heads[i]]) + ddc_ref[:, heads[i]])).astype(BF16) for i in H4]
        ds_n = [(p_n[i] * (_dot_nt(do_n[i], vc_ref[:, heads[i]]) + ddn_ref[:, heads[i]])).astype(BF16) for i in H4]
        dq = [(_dot(ds_c[i], kc_ref[:, heads[i]]) + _dot(ds_p[i], kp_ref[:, heads[i]])) * scale for i in H4]
        dk = [(_dot_tn(ds_c[i], qc_ref[:, heads[i]]) + _dot_tn(ds_n[i], qn_ref[:, heads[i]])) * scale for i in H4]
        dv = [_dot_tn(p_c[i].astype(BF16), do_c[i]) + _dot_tn(p_n[i].astype(BF16), do_n[i]) for i in H4]
        for i in H4:
            da_ref[:, heads[i]] = dq[i].astype(BF16)
            da_ref[:, GROUP_W + i * HEAD:GROUP_W + (i + 1) * HEAD] = dk[i].astype(BF16)
            da_ref[:, 2 * GROUP_W + i * HEAD:2 * GROUP_W + (i + 1) * HEAD] = dv[i].astype(BF16)

    def rel(delta):
        if delta < 0:
            return lambda n: jnp.maximum(n - 1, 0)
        if delta > 0:
            return lambda n: jnp.minimum(n + 1, nb - 1)
        return lambda n: n

    def blk(width, part, delta):
        f = rel(delta)
        return pl.BlockSpec((None, ATTN_SPAN, width), functools.partial(lambda r, n, p, f: (r, f(n), p), p=part, f=f))

    g = GROUP_W
    return pl.pallas_call(
        body, name=f"attn_bwd_d{d}", grid=(d, nb),
        in_specs=[blk(g, 0, 0), blk(g, 0, 1), blk(g, 1, -1), blk(g, 1, 0), blk(g, 2, -1), blk(g, 2, 0),
                  blk(g, 0, 0), blk(g, 0, 1), blk(g, 0, 0), blk(g, 0, 1), blk(g, 0, 0), blk(g, 0, 1)],
        out_specs=pl.BlockSpec((None, ATTN_SPAN, 3 * g), lambda r, n: (r, n, 0)),
        out_shape=jax.ShapeDtypeStruct((d, L, 3 * g), BF16),
        compiler_params=_params(("parallel", "arbitrary")))(
            a, a, a, a, a, a, do, do, lse, lse, dd, dd)


def _group_weights(lse_refs, cols):
    ls = [r[:, cols] for r in lse_refs]
    mx = jnp.maximum(jnp.maximum(ls[0], ls[1]), ls[2])
    es = [jnp.exp(v - mx) for v in ls]
    tot = es[0] + es[1] + es[2]
    return [e / tot for e in es]


def _gather_tokens(ref, scr, d, tm):
    if d == 1:
        return ref.at[0]
    for r in range(d):
        scr[pl.ds(r, tm // d, stride=d), :] = ref[r]
    return scr


def _scatter_tokens(scr, ref, d, tm):
    if d == 1:
        ref[0] = scr[...].astype(ref.dtype)
        return
    for r in range(d):
        ref[r] = scr[pl.ds(r, tm // d, stride=d), :].astype(ref.dtype)


def _head_spec(d, tm):
    return pl.BlockSpec((d, tm // d, HEAD), lambda i, j: (0, i, j))


def _qkv_dilated(h, gain, wg, cos, sin, d, tm=512):
    S, K = h.shape

    def body(h_ref, g_ref, w_ref, cos_ref, sin_ref, out_ref, u_ref, y_scr):
        p = pl.program_id(1)

        @pl.when(p == 0)
        def _():
            v = h_ref[...]
            u_ref[...] = (v * _rstd(v) * g_ref[...]).astype(BF16)

        y = _dot(u_ref[...], w_ref[...])
        heads = [slice(hh * HEAD, (hh + 1) * HEAD) for hh in range(HEADS_PER_GROUP)]
        for hh, cols in enumerate(heads):
            y_scr[hh] = y[:, cols]

        @pl.when(p < 2)
        def _():
            for r in range(d):
                rows = slice(None) if d == 1 else pl.ds(r, tm // d, stride=d)
                cr, sr = cos_ref[rows, :], sin_ref[rows, :]
                for hh, cols in enumerate(heads):
                    out_ref[r, :, cols] = _rope(y_scr.at[hh][rows, :], cr, sr).astype(BF16)

        @pl.when(p == 2)
        def _():
            for r in range(d):
                rows = slice(None) if d == 1 else pl.ds(r, tm // d, stride=d)
                for hh, cols in enumerate(heads):
                    out_ref[r, :, cols] = y_scr.at[hh][rows, :].astype(BF16)

    tab = pl.BlockSpec((tm, HEAD), lambda i, p: (i, 0))
    return pl.pallas_call(
        body, name=f"attn_qkv_d{d}", grid=(S // tm, 3),
        in_specs=[pl.BlockSpec((tm, K), lambda i, p: (i, 0)),
                  pl.BlockSpec((1, K), lambda i, p: (0, 0)),
                  pl.BlockSpec((K, GROUP_W), lambda i, p: (0, p)), tab, tab],
        out_specs=[pl.BlockSpec((d, tm // d, GROUP_W), lambda i, p: (0, i, p)), pl.BlockSpec((tm, K), lambda i, p: (i, 0))],
        out_shape=[jax.ShapeDtypeStruct((d, S // d, 3 * GROUP_W), BF16), jax.ShapeDtypeStruct((S, K), BF16)],
        scratch_shapes=[pltpu.VMEM((HEADS_PER_GROUP, tm, HEAD), F32)],
        compiler_params=_params(("parallel", "arbitrary")))(h, gain, wg, cos, sin)


def _undilate_group(da, dqkv, cos, sin, g, tm=512):
    d, L, _ = da.shape
    S = d * L
    G = len(ATTN_GROUPS)

    def body(*refs):
        da_ref, cos_ref, sin_ref, out_ref, scr = refs[0], refs[1], refs[2], refs[-2], refs[-1]
        p = pl.program_id(1)
        heads = [slice(hh * HEAD, (hh + 1) * HEAD) for hh in range(HEADS_PER_GROUP)]
        for hh, cols in enumerate(heads):
            if d == 1:
                scr[hh] = da_ref[0, :, cols].astype(F32)
            else:
                for r in range(d):
                    scr.at[hh][pl.ds(r, tm // d, stride=d), :] = da_ref[r, :, cols].astype(F32)

        @pl.when(p < 2)
        def _():
            cr, sr = cos_ref[...], -sin_ref[...]
            for hh, cols in enumerate(heads):
                out_ref[:, cols] = _rope(scr[hh], cr, sr).astype(BF16)

        @pl.when(p == 2)
        def _():
            for hh, cols in enumerate(heads):
                out_ref[:, cols] = scr[hh].astype(BF16)

    tab = pl.BlockSpec((tm, HEAD), lambda i, p: (i, 0))
    operands = (da, cos, sin) if dqkv is None else (da, cos, sin, dqkv)
    return pl.pallas_call(
        body, name=f"attn_undilate_d{d}", grid=(S // tm, 3),
        in_specs=[pl.BlockSpec((d, tm // d, GROUP_W), lambda i, p: (0, i, p)), tab, tab] + ([] if dqkv is None else [ANY]),
        out_specs=pl.BlockSpec((tm, GROUP_W), lambda i, p: (i, p * G + g)),
        out_shape=jax.ShapeDtypeStruct((S, 3 * G * GROUP_W), BF16),
        input_output_aliases={} if dqkv is None else {3: 0},
        scratch_shapes=[pltpu.VMEM((HEADS_PER_GROUP, tm, HEAD), F32)],
        compiler_params=_params(("parallel", "arbitrary")))(*operands)


def _attn_merge(os_, lses, tm=512):
    G = len(os_)
    S = os_[0].shape[0] * os_[0].shape[1]

    def body(*refs):
        o_refs, l_refs, out_ref = refs[:G], refs[G:2 * G], refs[2 * G]
        scr = refs[2 * G + 1:]
        hh = pl.program_id(1)
        o_tok = [_gather_tokens(o_refs[g], scr[g], d, tm) for g, (_, d) in enumerate(ATTN_GROUPS)]
        l_tok = [_gather_tokens(l_refs[g], scr[G + g], d, tm) for g, (_, d) in enumerate(ATTN_GROUPS)]
        al = _group_weights(l_tok, slice(None))
        for g in range(G):
            cols = pl.ds(pl.multiple_of(g * GROUP_W + hh * HEAD, HEAD), HEAD)
            out_ref[:, cols] = (o_tok[g][...] * al[g]).astype(BF16)

    specs = [_head_spec(d, tm) for _, d in ATTN_GROUPS]
    return pl.pallas_call(
        body, name="attn_merge", grid=(S // tm, HEADS_PER_GROUP),
        in_specs=specs + specs,
        out_specs=pl.BlockSpec((tm, G * GROUP_W), lambda i, j: (i, 0)),
        out_shape=jax.ShapeDtypeStruct((S, G * GROUP_W), BF16),
        scratch_shapes=[pltpu.VMEM((tm, HEAD), F32)] * (2 * G),
        compiler_params=_params(("parallel", "arbitrary")))(*os_, *lses)


def _attn_merge_bwd(os_, lses, doa, tm=512):
    G = len(os_)
    S = doa.shape[0]

    def body(*refs):
        o_refs, l_refs, doa_ref = refs[:G], refs[G:2 * G], refs[2 * G]
        do_refs, dd_refs = refs[2 * G + 1:3 * G + 1], refs[3 * G + 1:4 * G + 1]
        scr = refs[4 * G + 1:]
        hh = pl.program_id(1)
        o_tok = [_gather_tokens(o_refs[g], scr[g], d, tm) for g, (_, d) in enumerate(ATTN_GROUPS)]
        l_tok = [_gather_tokens(l_refs[g], scr[G + g], d, tm) for g, (_, d) in enumerate(ATTN_GROUPS)]
        do_tok, dd_tok = scr[2 * G:3 * G], scr[3 * G:]
        al = _group_weights(l_tok, slice(None))
        mix = None
        for g in range(G):
            dg = doa_ref[:, pl.ds(pl.multiple_of(g * GROUP_W + hh * HEAD, HEAD), HEAD)]
            do_tok[g][...] = dg * al[g]
            t = al[g] * jnp.sum(dg * o_tok[g][...], axis=-1, keepdims=True)
            mix = t if mix is None else mix + t
        for g, (_, d) in enumerate(ATTN_GROUPS):
            dd_tok[g][...] = jnp.broadcast_to(-al[g] * mix, (tm, HEAD))
            _scatter_tokens(do_tok[g], do_refs[g], d, tm)
            _scatter_tokens(dd_tok[g], dd_refs[g], d, tm)

    specs = [_head_spec(d, tm) for _, d in ATTN_GROUPS]
    do_shapes = [jax.ShapeDtypeStruct((d, S // d, GROUP_W), BF16) for _, d in ATTN_GROUPS]
    dd_shapes = [jax.ShapeDtypeStruct((d, S // d, GROUP_W), F32) for _, d in ATTN_GROUPS]
    return pl.pallas_call(
        body, name="attn_merge_bwd", grid=(S // tm, HEADS_PER_GROUP),
        in_specs=specs + specs + [pl.BlockSpec((tm, G * GROUP_W), lambda i, j: (i, 0))],
        out_specs=specs + specs,
        out_shape=do_shapes + dd_shapes,
        scratch_shapes=[pltpu.VMEM((tm, HEAD), F32)] * (4 * G),
        compiler_params=_params(("parallel", "arbitrary")))(*os_, *lses, doa)


def _rope_tables(S):
    inv_freq = 1.0 / (ROPE_THETA ** (jnp.arange(0, HEAD, 2, dtype=F32) / HEAD))
    ang = jnp.arange(S, dtype=F32)[:, None] * inv_freq[None, :]
    cos, sin = jnp.cos(ang), jnp.sin(ang)
    return jnp.concatenate([cos, cos], axis=-1), jnp.concatenate([-sin, sin], axis=-1)


def _local_step(x, target, norm_mix, norm_ffn, lb_logits, out_gain, final_norm, comm):
    S = x.shape[0]
    nm0, nm1 = norm_mix[0:1], norm_mix[1:2]
    nf0, nf1 = norm_ffn[0:1], norm_ffn[1:2]
    w = comm.first_weights()

    proj, u0, got = _norm_mm(x, nm0, w["hin"], "hgrn_in", rider=comm.gather_rider(LATE_WEIGHTS_A))
    w.update(comm.gathered(LATE_WEIGHTS_A, got))
    (o, og, states), got = _hgrn_fwd(proj, lb_logits, out_gain, rider=comm.gather_rider(LATE_WEIGHTS_B))
    w.update(comm.gathered(LATE_WEIGHTS_B, got))
    fin_tn = w["fin0"].shape[2]
    h1 = _mm_res(x, og, w["hout"], "hgrn_out")
    z0, u1, _ = _norm_mm(h1, nf0, w["fin0"], "ffn0_in", out_dtype=BF16)
    h2, act0 = _swiglu_mm_res(h1, z0, w["fdn0"], "ffn0_down")
    cos, sin = _rope_tables(S)
    G = len(ATTN_GROUPS)
    w_groups = w["qkv"].transpose(1, 0, 2).reshape(D_MODEL, 3, G, GROUP_W)
    a_g, u2 = zip(*[_qkv_dilated(h2, nm1, w_groups[:, :, gi, :].reshape(D_MODEL, 3 * GROUP_W), cos, sin, d)
                    for gi, (_, d) in enumerate(ATTN_GROUPS)])
    o_g, lse_g = zip(*[_attn_fwd(a) for a in a_g])
    oa = _attn_merge(o_g, lse_g)
    h3 = _mm_res(h2, oa, w["aout"], "attn_out")
    z1, u3, _ = _norm_mm(h3, nf1, w["fin1"], "ffn1_in", out_dtype=BF16)
    h4, act1 = _swiglu_mm_res(h3, z1, w["fdn1"], "ffn1_down")
    dh4, loss, d_final = _loss_head(h4, final_norm, target)

    grads, small = {}, {"final_norm": d_final}

    def ffn_bwd(dh, h_in, u_in, z, act, gain, w_in, w_dn, tag, ride=None):
        dz = _mm_nt_swiglu_bwd(dh, w_dn, z, tag + "_down_dx")
        g_dn = _mm_tn(act, dh, 1, D_MODEL, D_MODEL, tag + "_down_dw")[0]
        g_in = _mm_tn(u_in, dz, N_CHIPS, fin_tn, fin_tn, tag + "_in_dw")
        rider = None if ride is None else ride(g_in, g_dn)
        dh_in, dgain, got = _mm_nt_normbwd(dz, w_in, h_in, gain, dh, tag + "_in_dx", rider=rider)
        return dh_in, dgain, g_in, g_dn, got

    dh3, d_nf1, grads["fin1"], grads["fdn1"], _ = ffn_bwd(dh4, h3, u3, z1, act1, nf1, w["fin1"], w["fdn1"], "ffn1")
    doa = _mm_nt(dh3, w["aout"][None], "attn_out_dx")
    grads["aout"] = _mm_tn(oa, dh3, 1, D_MODEL, D_MODEL, "attn_out_dw")[0]
    merged = _attn_merge_bwd(o_g, lse_g, doa)
    G = len(ATTN_GROUPS)
    das = [_attn_bwd(a_g[gi], merged[gi], lse_g[gi], merged[G + gi]) for gi in range(G)]
    dqkv = None
    for gi in range(G):
        dqkv = _undilate_group(das[gi], dqkv, cos, sin, gi)
    n_qkv = w["qkv"].shape[2]
    grads["qkv"] = _mm_tn(u2[0], dqkv, N_CHIPS, n_qkv, n_qkv, "attn_qkv_dw")
    dh2, d_nm1, _ = _mm_nt_normbwd(dqkv, w["qkv"], h2, nm1, dh3, "attn_qkv_dx")

    def ride_early(g_in, g_dn):
        return comm.pair_rider({**grads, "fin0": g_in, "fdn0": g_dn}, "early")

    dh1, d_nf0, _, _, got = ffn_bwd(dh2, h1, u1, z0, act0, nf0, w["fin0"], w["fdn0"], "ffn0", ride=ride_early)
    comm.paired("early", got)
    dog = _mm_nt(dh1, w["hout"][None], "hgrn_out_dx")
    (dproj, dlb, dgn), got = _hgrn_bwd(proj, lb_logits, out_gain, o, states, dog, rider=comm.exchange_rider("early"))
    comm.exchanged("early", got)
    late = {"hout": _mm_tn(og, dh1, 1, D_MODEL, D_MODEL, "hgrn_out_dw")[0],
            "hin": _mm_tn(u0, dproj, N_CHIPS, D_MODEL, D_MODEL, "hgrn_in_dw")}
    comm.pair_now(late, "late")
    dx, d_nm0, got = _mm_nt_normbwd(dproj, w["hin"], x, nm0, dh1, "hgrn_in_dx", rider=comm.exchange_rider("late"))
    comm.exchanged("late", got)

    small["norm_mix"] = jnp.concatenate([d_nm0, d_nm1], axis=0)
    small["norm_ffn"] = jnp.concatenate([d_nf0, d_nf1], axis=0)
    small["lb"] = dlb.reshape(1, HGRN_HEADS * HEAD)
    small["out_norm"] = dgn.reshape(HGRN_HEADS, HEAD)
    return loss, dx, small


def _place():
    x, y, c = lax.axis_index("x"), lax.axis_index("y"), lax.axis_index("c")
    others = [(1 - x, y), (x, 1 - y), (1 - x, 1 - y)]
    return x, y, c, others


ANY = pl.BlockSpec(memory_space=pl.ANY)


class _GatherRider:
    def __init__(self, shards):
        self.operands = list(shards)
        n = self.n = len(shards)
        self.out_shape = [jax.ShapeDtypeStruct((N_CHIPS,) + s.shape, s.dtype) for s in shards]
        self.scratch = [pltpu.SemaphoreType.DMA((3 * n,)), pltpu.SemaphoreType.DMA((3 * n,)),
                        pltpu.SemaphoreType.DMA((3 * n,)), pltpu.SemaphoreType.DMA((3 * n,)),
                        pltpu.SemaphoreType.DMA((n,)), pltpu.SemaphoreType.DMA((n,))]

    def _copies(self, ins, outs, sems):
        ici_send, ici_recv, _, _, own_send, own_recv = sems
        x, y, c, others = _place()
        me = 2 * x + y
        own = [pltpu.make_async_remote_copy(
            src_ref=ins[a], dst_ref=outs[a].at[me], send_sem=own_send.at[a], recv_sem=own_recv.at[a],
            device_id=(x, y, 1 - c), device_id_type=MESH) for a in range(self.n)]
        sends = [pltpu.make_async_remote_copy(
            src_ref=ins[a].at[c], dst_ref=outs[a].at[me, c], send_sem=ici_send.at[a * 3 + k], recv_sem=ici_recv.at[a * 3 + k],
            device_id=(ox, oy, c), device_id_type=MESH) for a in range(self.n) for k, (ox, oy) in enumerate(others)]
        return own, sends

    def start(self, ins, outs, sems):
        own, sends = self._copies(ins, outs, sems)
        for cp in own + sends:
            cp.start()

    def finish(self, ins, outs, sems):
        ici_send, ici_recv, d2d_send, d2d_recv, _, _ = sems
        x, y, c, others = _place()
        sibling = (x, y, 1 - c)
        own, sends = self._copies(ins, outs, sems)
        passes = []
        for a in range(self.n):
            for k, (ox, oy) in enumerate(others):
                s = a * 3 + k
                got = outs[a].at[2 * ox + oy, c]
                pltpu.make_async_remote_copy(
                    src_ref=got, dst_ref=got, send_sem=ici_send.at[s], recv_sem=ici_recv.at[s],
                    device_id=(ox, oy, c), device_id_type=MESH).wait_recv()
                fwd = pltpu.make_async_remote_copy(
                    src_ref=got, dst_ref=got, send_sem=d2d_send.at[s], recv_sem=d2d_recv.at[s],
                    device_id=sibling, device_id_type=MESH)
                fwd.start()
                passes.append(fwd)
        for a in range(self.n):
            for k, (ox, oy) in enumerate(others):
                s = a * 3 + k
                theirs = outs[a].at[2 * ox + oy, 1 - c]
                pltpu.make_async_remote_copy(
                    src_ref=theirs, dst_ref=theirs, send_sem=d2d_send.at[s], recv_sem=d2d_recv.at[s],
                    device_id=sibling, device_id_type=MESH).wait_recv()
        for cp in own:
            cp.wait()
        for cp in sends + passes:
            cp.wait_send()


class _PairRider:
    def __init__(self, grads):
        self.operands = list(grads)
        n = self.n = len(grads)
        self.out_shape = [jax.ShapeDtypeStruct((N_CHIPS,) + g.shape[2:], F32) for g in grads]
        self.scratch = [pltpu.SemaphoreType.DMA((N_CHIPS * n,)), pltpu.SemaphoreType.DMA((N_CHIPS * n,))]

    def _copies(self, ins, outs, sems):
        send_sem, recv_sem = sems
        x, y, c, _ = _place()
        return [pltpu.make_async_remote_copy(
            src_ref=ins[a].at[j, 1 - c], dst_ref=outs[a].at[j], send_sem=send_sem.at[a * N_CHIPS + j],
            recv_sem=recv_sem.at[a * N_CHIPS + j], device_id=(x, y, 1 - c), device_id_type=MESH)
            for a in range(self.n) for j in range(N_CHIPS)]

    def start(self, ins, outs, sems):
        for cp in self._copies(ins, outs, sems):
            cp.start()

    def finish(self, ins, outs, sems):
        for cp in self._copies(ins, outs, sems):
            cp.wait()


class _ExchangeRider:
    def __init__(self, parts):
        self.operands = list(parts)
        n = self.n = len(parts)
        self.out_shape = [jax.ShapeDtypeStruct(p.shape, p.dtype) for p in parts]
        self.scratch = [pltpu.SemaphoreType.DMA((3 * n,)), pltpu.SemaphoreType.DMA((3 * n,))]

    def _copies(self, ins, outs, sems):
        send_sem, recv_sem = sems
        x, y, c, others = _place()
        me = 2 * x + y
        return [pltpu.make_async_remote_copy(
            src_ref=ins[a].at[2 * ox + oy], dst_ref=outs[a].at[me], send_sem=send_sem.at[a * 3 + k],
            recv_sem=recv_sem.at[a * 3 + k], device_id=(ox, oy, c), device_id_type=MESH)
            for a in range(self.n) for k, (ox, oy) in enumerate(others)]

    def start(self, ins, outs, sems):
        for cp in self._copies(ins, outs, sems):
            cp.start()

    def finish(self, ins, outs, sems):
        send_sem, recv_sem = sems
        x, y, c, others = _place()
        for a in range(self.n):
            for k, (ox, oy) in enumerate(others):
                s = a * 3 + k
                got = outs[a].at[2 * ox + oy]
                pltpu.make_async_remote_copy(
                    src_ref=got, dst_ref=got, send_sem=send_sem.at[s], recv_sem=recv_sem.at[s],
                    device_id=(ox, oy, c), device_id_type=MESH).wait_recv()
        for cp in self._copies(ins, outs, sems):
            cp.wait_send()


def _run_rider(rider, name):
    n = rider.n

    def body(*refs):
        ins, outs, sems = refs[:n], refs[n:2 * n], refs[2 * n:]
        rider.start(ins, outs, sems)
        rider.finish(ins, outs, sems)

    return pl.pallas_call(
        body, name=name, in_specs=[ANY] * n, out_specs=[ANY] * n,
        out_shape=rider.out_shape, scratch_shapes=rider.scratch)(*rider.operands)


def _ride(rider, body, n_in, n_out, first, last):
    if rider is None:
        return body
    n = rider.n

    def wrapped(*refs):
        host_in, r_in = refs[:n_in], refs[n_in:n_in + n]
        host_out = refs[n_in + n:n_in + n + n_out]
        r_out = refs[n_in + n + n_out:n_in + 2 * n + n_out]
        rest = refs[n_in + 2 * n + n_out:]
        host_scr, sems = rest[:len(rest) - len(rider.scratch)], rest[len(rest) - len(rider.scratch):]

        @pl.when(first())
        def _():
            rider.start(r_in, r_out, sems)

        body(*host_in, *host_out, *host_scr)

        @pl.when(last())
        def _():
            rider.finish(r_in, r_out, sems)

    return wrapped


def _rider_args(rider):
    if rider is None:
        return [], [], [], [], []
    return rider.operands, [ANY] * rider.n, [ANY] * rider.n, rider.out_shape, rider.scratch


def _pair_sum(g, got, c_idx):
    _, _, r, cw = g.shape
    tr = _row_tile(r, cw)

    def body(c_ref, g_ref, got_ref, p_ref, pb_ref):
        v = g_ref[...] + got_ref[...]
        p_ref[...] = v
        pb_ref[...] = v.astype(BF16)

    blk = pl.BlockSpec((None, tr, cw), lambda j, i, c_ref: (j, i, 0))
    return pl.pallas_call(
        body, name="grad_pair_sum",
        grid_spec=pltpu.PrefetchScalarGridSpec(
            num_scalar_prefetch=1, grid=(N_CHIPS, r // tr),
            in_specs=[pl.BlockSpec((None, None, tr, cw), lambda j, i, c_ref: (j, c_ref[0], i, 0)), blk],
            out_specs=[blk, blk]),
        out_shape=[jax.ShapeDtypeStruct((N_CHIPS, r, cw), F32), jax.ShapeDtypeStruct((N_CHIPS, r, cw), BF16)],
        compiler_params=_params(("parallel", "parallel")))(c_idx, g, got)


def _chip_sum(p, got, me_idx):
    _, r, cw = p.shape
    tr = _row_tile(r, cw)

    def body(me_ref, own_ref, got_ref, t_ref):
        me = me_ref[0]
        acc = None
        for s in range(N_CHIPS):
            term = jnp.where(me == s, own_ref[...], got_ref[s].astype(F32))
            acc = term if acc is None else acc + term
        t_ref[...] = acc

    return pl.pallas_call(
        body, name="grad_chip_sum",
        grid_spec=pltpu.PrefetchScalarGridSpec(
            num_scalar_prefetch=1, grid=(r // tr,),
            in_specs=[pl.BlockSpec((None, tr, cw), lambda i, me_ref: (me_ref[0], i, 0)),
                      pl.BlockSpec((N_CHIPS, tr, cw), lambda i, me_ref: (0, i, 0))],
            out_specs=pl.BlockSpec((tr, cw), lambda i, me_ref: (i, 0))),
        out_shape=jax.ShapeDtypeStruct((r, cw), F32),
        compiler_params=_params(("parallel",)))(me_idx, p, got)


def _pair_share(halves):
    n = len(halves)

    def body(*refs):
        ins, outs = refs[:n], refs[n:2 * n]
        send_sem, recv_sem = refs[2 * n:]
        x, y, c, _ = _place()
        cps = [pltpu.make_async_remote_copy(
            src_ref=ins[a], dst_ref=outs[a], send_sem=send_sem.at[a], recv_sem=recv_sem.at[a],
            device_id=(x, y, 1 - c), device_id_type=MESH) for a in range(n)]
        for cp in cps:
            cp.start()
        for cp in cps:
            cp.wait()

    return pl.pallas_call(
        body, name="grad_pair_share",
        in_specs=[ANY] * n, out_specs=[ANY] * n,
        out_shape=[jax.ShapeDtypeStruct(h.shape, F32) for h in halves],
        scratch_shapes=[pltpu.SemaphoreType.DMA((n,)), pltpu.SemaphoreType.DMA((n,))],
        )(*halves)


def _small_allreduce(pack):
    m_per, ncol = pack.shape
    n_dev = 8

    def body(x_ref, sum_ref, all_ref, send_sems, recv_sems, local_sem):
        x, y, c, others = _place()
        me, sibling = (x, y, c), (x, y, 1 - c)

        def rows(px, py, pc):
            return all_ref.at[pl.ds((4 * px + 2 * py + pc) * m_per, m_per), :]

        def copy(k, block, to, src=None):
            return pltpu.make_async_remote_copy(
                src_ref=rows(*block) if src is None else src, dst_ref=rows(*block),
                send_sem=send_sems.at[k], recv_sem=recv_sems.at[k], device_id=to, device_id_type=MESH)

        mine = pltpu.make_async_copy(x_ref, rows(*me), local_sem)
        mine.start()
        first = [copy(0, me, sibling, src=x_ref)]
        first += [copy(1 + j, me, (*chip, c), src=x_ref) for j, chip in enumerate(others)]
        for cp in first:
            cp.start()
        passed = [copy(4 + j, (*chip, c), sibling) for j, chip in enumerate(others)]
        for j, chip in enumerate(others):
            copy(1 + j, (*chip, c), me).wait_recv()
            passed[j].start()
        copy(0, sibling, me).wait_recv()
        for j, chip in enumerate(others):
            copy(4 + j, (*chip, 1 - c), me).wait_recv()
        for cp in first + passed:
            cp.wait_send()
        mine.wait()
        acc = all_ref[0:m_per, :]
        for dvc in range(1, n_dev):
            acc = acc + all_ref[dvc * m_per:(dvc + 1) * m_per, :]
        sum_ref[...] = acc

    return pl.pallas_call(
        body, name="small_allreduce",
        in_specs=[pl.BlockSpec(memory_space=pltpu.VMEM)],
        out_specs=pl.BlockSpec(memory_space=pltpu.VMEM),
        out_shape=jax.ShapeDtypeStruct((m_per, ncol), F32),
        scratch_shapes=[pltpu.VMEM((n_dev * m_per, ncol), F32),
                        pltpu.SemaphoreType.DMA((7,)), pltpu.SemaphoreType.DMA((7,)), pltpu.SemaphoreType.DMA],
        )(pack)


def _adam_math(w, g, m, v):
    m = ADAM_B1 * m + (1.0 - ADAM_B1) * g
    v = ADAM_B2 * v + (1.0 - ADAM_B2) * (g * g)
    m_hat = m / (1.0 - ADAM_B1 ** ADAM_STEP)
    v_hat = v / (1.0 - ADAM_B2 ** ADAM_STEP)
    delta = -ADAM_LR * (m_hat / (jnp.sqrt(v_hat) + ADAM_EPS) + ADAM_WD * w)
    return delta, m, v


def _adamw(halves, c_idx, w, m, v, name):
    L = len(halves)
    r, C = halves[0][0].shape
    tr = _row_tile(r, C, 512 * 1024)
    nt = r // tr

    def body(c_ref, *refs):
        g_refs, (w_ref, m_ref, v_ref), (g_ref, d_ref, nm_ref, nv_ref) = refs[:2 * L], refs[2 * L:2 * L + 3], refs[2 * L + 3:]
        own = pl.program_id(1) == c_ref[0]
        g = None
        for l in range(L):
            cand = jnp.where(own, g_refs[2 * l][...], g_refs[2 * l + 1][...])
            g = cand if g is None else jnp.where(pl.program_id(0) == l, cand, g)
        g_ref[...] = g
        d_ref[...], nm_ref[...], nv_ref[...] = _adam_math(w_ref[...], g, m_ref[...], v_ref[...])

    def half(l, mine):
        def index(ll, h, i, c_ref):
            read = (h == c_ref[0]) if mine else (h != c_ref[0])
            return jnp.where(jnp.logical_and(ll == l, read), i, 0), 0
        return pl.BlockSpec((tr, C), index)

    full = pl.BlockSpec((None, tr, C), lambda ll, h, i, c_ref: (ll, h * nt + i, 0))
    shp = jax.ShapeDtypeStruct((L, 2 * r, C), F32)
    g_specs = [half(l, mine) for l in range(L) for mine in (True, False)]
    return pl.pallas_call(
        body, name=name,
        grid_spec=pltpu.PrefetchScalarGridSpec(
            num_scalar_prefetch=1, grid=(L, 2, nt),
            in_specs=g_specs + [full] * 3, out_specs=[full] * 4),
        out_shape=[shp] * 4,
        compiler_params=_params(("arbitrary", "arbitrary", "arbitrary")))(
            c_idx, *[a for pair in halves for a in pair], w, m, v)


def _small_update(gsum, logits_pack, w, m, v):
    def body(gs_ref, lg_ref, w_ref, m_ref, v_ref, g_ref, d_ref, nm_ref, nv_ref):
        g_ref[...] = gs_ref[...]
        l0, l1, l2 = lg_ref[0:1, :], lg_ref[1:2, :], lg_ref[2:3, :]
        mx = jnp.maximum(jnp.maximum(l0, l1), l2)
        e0, e1, e2 = jnp.exp(l0 - mx), jnp.exp(l1 - mx), jnp.exp(l2 - mx)
        tot = e0 + e1 + e2
        p0, p1, p2 = e0 / tot, e1 / tot, e2 / tot
        dlb = gs_ref[4:5, :]
        g_ref[4:5, :] = dlb * p0 * (1.0 - p0)
        g_ref[5:6, :] = -dlb * p0 * p1
        g_ref[6:7, :] = -dlb * p0 * p2
        d_ref[...], nm_ref[...], nv_ref[...] = _adam_math(w_ref[...], g_ref[...], m_ref[...], v_ref[...])

    full = pl.BlockSpec(memory_space=pltpu.VMEM)
    shp = jax.ShapeDtypeStruct(gsum.shape, F32)
    return pl.pallas_call(
        body, name="small_update", in_specs=[full] * 5, out_specs=[full] * 4, out_shape=[shp] * 4)(
            gsum, logits_pack, w, m, v)


def _pack_small(norm_mix, norm_ffn, lb3, out_norm, final_norm, extra=None):
    ncol = norm_mix.shape[1]
    on = jnp.pad(out_norm.reshape(1, -1), ((0, 0), (0, ncol - out_norm.size)))
    rows = [norm_mix, norm_ffn, lb3, on, final_norm.reshape(1, ncol)]
    if extra is not None:
        rows.append(extra)
    used = sum(r.shape[0] for r in rows)
    rows.append(jnp.zeros((SMALL_ROWS - used, ncol), F32))
    return jnp.concatenate(rows, axis=0)


WEIGHT_NAMES = ("hin", "hout", "qkv", "aout", "fin0", "fin1", "fdn0", "fdn1")
FIRST_WEIGHTS = ("hin",)
LATE_WEIGHTS_A = ("hout", "fin0", "fdn0")
LATE_WEIGHTS_B = ("qkv", "aout", "fin1", "fdn1")


def _split_weights(hgrn_w_in, hgrn_w_out, attn_w_qkv, attn_w_out, ffn_w_in, ffn_w_down):
    return {"hin": hgrn_w_in[0], "hout": hgrn_w_out[0], "qkv": attn_w_qkv[0], "aout": attn_w_out[0],
            "fin0": ffn_w_in[0], "fin1": ffn_w_in[1], "fdn0": ffn_w_down[0], "fdn1": ffn_w_down[1]}


def _halves(v):
    r, c = v.shape
    return v.reshape(2, r // 2, c)


def _full_weights(gathered):
    out = {}
    for k, g in gathered.items():
        _, _, r, c = g.shape
        if k in ("hin", "qkv", "fin0", "fin1"):
            out[k] = g.reshape(N_CHIPS, 2 * r, c)
        else:
            out[k] = g.reshape(N_CHIPS * 2 * r, c)
    return out


class _StepComm:
    def __init__(self, shards, c_idx, me_idx):
        self.shards, self.c_idx, self.me_idx = shards, c_idx, me_idx
        self.halves = {}
        self._stage = {}

    def gather_rider(self, names):
        return _GatherRider([_halves(self.shards[k].astype(BF16)) for k in names])

    def gathered(self, names, got):
        return _full_weights(dict(zip(names, got)))

    def first_weights(self):
        return self.gathered(FIRST_WEIGHTS, _run_rider(self.gather_rider(FIRST_WEIGHTS), "gather_first"))

    def pair_rider(self, grads, tag):
        names = list(grads)
        g4 = []
        for k in names:
            r, c = self.shards[k].shape
            g4.append(grads[k].reshape(N_CHIPS, 2, r // 2, c))
        self._stage[tag] = (names, g4)
        return _PairRider(g4)

    def pair_now(self, grads, tag):
        self.paired(tag, _run_rider(self.pair_rider(grads, tag), "grad_pair_exchange_" + tag))

    def paired(self, tag, got):
        names, g4 = self._stage[tag]
        self._stage[tag] = (names, [_pair_sum(g, s, self.c_idx) for g, s in zip(g4, got)])

    def exchange_rider(self, tag):
        return _ExchangeRider([s[1] for s in self._stage[tag][1]])

    def exchanged(self, tag, got):
        names, sums = self._stage.pop(tag)
        for k, s, g in zip(names, sums, got):
            self.halves[k] = _chip_sum(s[0], g, self.me_idx)

    def shared_halves(self):
        mine = [self.halves[k] for k in WEIGHT_NAMES]
        return dict(zip(WEIGHT_NAMES, zip(mine, _pair_share(mine))))


def kernel(x, norm_mix, norm_ffn, hgrn_w_in, hgrn_lb_logits, hgrn_out_norm, hgrn_w_out, attn_w_qkv, attn_w_out, ffn_w_in, ffn_w_down, final_norm, loss_target, m_norm_mix, m_norm_ffn, m_hgrn_w_in, m_hgrn_lb_logits, m_hgrn_out_norm, m_hgrn_w_out, m_attn_w_qkv, m_attn_w_out, m_ffn_w_in, m_ffn_w_down, m_final_norm, v_norm_mix, v_norm_ffn, v_hgrn_w_in, v_hgrn_lb_logits, v_hgrn_out_norm, v_hgrn_w_out, v_attn_w_qkv, v_attn_w_out, v_ffn_w_in, v_ffn_w_down, v_final_norm):
    S = x.shape[1]
    xi, yi, ci = lax.axis_index("x"), lax.axis_index("y"), lax.axis_index("c")
    c_idx = jnp.reshape(ci, (1,)).astype(jnp.int32)
    me_idx = jnp.reshape(2 * xi + yi, (1,)).astype(jnp.int32)

    w_own = _split_weights(hgrn_w_in, hgrn_w_out, attn_w_qkv, attn_w_out, ffn_w_in, ffn_w_down)

    comm = _StepComm(w_own, c_idx, me_idx)
    loss, dx, small = _local_step(
        x.reshape(S, D_MODEL), loss_target.reshape(S, D_MODEL), norm_mix, norm_ffn, hgrn_lb_logits,
        hgrn_out_norm, final_norm.reshape(1, D_MODEL), comm)

    halves = comm.shared_halves()
    updated = {}
    for tensor, layers, (wt, mt, vt) in (
            ("hgrn_w_in", ("hin",), (hgrn_w_in, m_hgrn_w_in, v_hgrn_w_in)),
            ("hgrn_w_out", ("hout",), (hgrn_w_out, m_hgrn_w_out, v_hgrn_w_out)),
            ("attn_w_qkv", ("qkv",), (attn_w_qkv, m_attn_w_qkv, v_attn_w_qkv)),
            ("attn_w_out", ("aout",), (attn_w_out, m_attn_w_out, v_attn_w_out)),
            ("ffn_w_in", ("fin0", "fin1"), (ffn_w_in, m_ffn_w_in, v_ffn_w_in)),
            ("ffn_w_down", ("fdn0", "fdn1"), (ffn_w_down, m_ffn_w_down, v_ffn_w_down))):
        updated[tensor] = _adamw([halves[k] for k in layers], c_idx, wt, mt, vt, "adamw_" + tensor)

    loss_row = jnp.pad(loss, ((0, 0), (0, D_MODEL - loss.shape[1])))
    lb3 = jnp.concatenate([small["lb"], jnp.zeros((2, D_MODEL), F32)], axis=0)
    on_grad = jnp.sum(small["out_norm"], axis=0, keepdims=True)
    pack = _pack_small(small["norm_mix"], small["norm_ffn"], lb3, on_grad, small["final_norm"], loss_row)
    gsum = _small_allreduce(pack)
    w_s = _pack_small(norm_mix, norm_ffn, hgrn_lb_logits, hgrn_out_norm, final_norm)
    m_s = _pack_small(m_norm_mix, m_norm_ffn, m_hgrn_lb_logits, m_hgrn_out_norm, m_final_norm)
    v_s = _pack_small(v_norm_mix, v_norm_ffn, v_hgrn_lb_logits, v_hgrn_out_norm, v_final_norm)
    lg_pack = jnp.pad(hgrn_lb_logits, ((0, 8 - hgrn_lb_logits.shape[0]), (0, 0)))
    sg, sd, sm, sv = _small_update(gsum, lg_pack, w_s, m_s, v_s)

    def unpack(p):
        return (p[0:2], p[2:4], p[4:7], p[7:8, :HEAD], p[8])

    def assemble(p, which):
        nmx, nff, lbl, onm, fnm = unpack(p)
        hin, hout, qkv, aout, fin, fdn = [updated[t][which] for t in
                                          ("hgrn_w_in", "hgrn_w_out", "attn_w_qkv", "attn_w_out", "ffn_w_in", "ffn_w_down")]
        return (nmx, nff, hin, lbl, onm, hout, qkv, aout, fin, fdn, fnm)

    total_loss = gsum[9, 0]
    return (total_loss, dx.reshape(1, S, D_MODEL), *assemble(sg, 0), *assemble(sd, 1), *assemble(sm, 2), *assemble(sv, 3))
```

```python
import functools

import jax
import jax.numpy as jnp
from jax import lax
from jax.experimental import pallas as pl
from jax.experimental.pallas import tpu as pltpu

F32 = jnp.float32
BF16 = jnp.bfloat16
MESH = pl.DeviceIdType.MESH

D_MODEL = 1024
HEAD = 128
HGRN_HEADS = 8
HGRN_CHUNK = 64
ATTN_GROUPS = ((128, 1), (512, 4), (2048, 16))
ATTN_SPAN = 128
HEADS_PER_GROUP = 4
GROUP_W = HEADS_PER_GROUP * HEAD
D_FF = 2816
NORM_EPS = 1e-6
ROPE_THETA = 10000.0
NEG = -1e30

ADAM_LR, ADAM_B1, ADAM_B2, ADAM_EPS, ADAM_WD, ADAM_STEP = 0.001, 0.9, 0.999, 1e-08, 0.01, 10

N_CHIPS = 4
VMEM_LIMIT = 56 * 1024 * 1024
SMALL_ROWS = 16


def _params(sem=None):
    return pltpu.CompilerParams(dimension_semantics=sem, vmem_limit_bytes=VMEM_LIMIT)


def _row_tile(rows, cols, budget_bytes=3 * 512 * 1024):
    best = 8
    for t in range(8, rows + 1, 8):
        if rows % t == 0 and t * cols * 4 <= budget_bytes:
            best = t
    assert rows % best == 0
    return best


def _grid_corner(i, j):
    return jnp.logical_and(pl.program_id(0) == i, pl.program_id(1) == j)


def _sigmoid(v):
    return 0.5 * jnp.tanh(0.5 * v) + 0.5


def _dot(a, b):
    return jnp.dot(a, b, preferred_element_type=F32)


def _dot_nt(a, b):
    return lax.dot_general(a, b, (((1,), (1,)), ((), ())), preferred_element_type=F32)


def _dot_tn(a, b):
    return lax.dot_general(a, b, (((0,), (0,)), ((), ())), preferred_element_type=F32)


def _dot_exact(ones, b):
    ones = ones.astype(BF16)
    hi = b.astype(BF16)
    rest = b - hi.astype(F32)
    mid = rest.astype(BF16)
    low = (rest - mid.astype(F32)).astype(BF16)
    return _dot(ones, hi) + _dot(ones, mid) + _dot(ones, low)


def _rstd(v):
    return lax.rsqrt(jnp.mean(v * v, axis=-1, keepdims=True) + NORM_EPS)


def _norm_mm(h, gain, w3, name, out_dtype=F32, tm=1024, rider=None):
    S, K = h.shape
    J, _, n = w3.shape
    gi = S // tm

    def body(h_ref, g_ref, w_ref, y_ref, u_ref):
        @pl.when(pl.program_id(1) == 0)
        def _():
            v = h_ref[...]
            u_ref[...] = (v * _rstd(v) * g_ref[...]).astype(BF16)

        y_ref[...] = _dot(u_ref[...], w_ref[pl.program_id(1)]).astype(y_ref.dtype)

    r_ops, r_in, r_out, r_shape, r_scr = _rider_args(rider)
    res = pl.pallas_call(
        _ride(rider, body, 3, 2, functools.partial(_grid_corner, 0, 0), functools.partial(_grid_corner, gi - 1, J - 1)),
        name=name, grid=(gi, J),
        in_specs=[pl.BlockSpec((tm, K), lambda i, j: (i, 0)),
                  pl.BlockSpec((1, K), lambda i, j: (0, 0)),
                  pl.BlockSpec((J, K, n), lambda i, j: (0, 0, 0))] + r_in,
        out_specs=[pl.BlockSpec((tm, n), lambda i, j: (i, j)), pl.BlockSpec((tm, K), lambda i, j: (i, 0))] + r_out,
        out_shape=[jax.ShapeDtypeStruct((S, J * n), out_dtype), jax.ShapeDtypeStruct((S, K), BF16)] + r_shape,
        scratch_shapes=r_scr,
        compiler_params=_params(("arbitrary", "arbitrary")))(h, gain, w3, *r_ops)
    return res[0], res[1], res[2:]


def _mm_res(h, a, w2, name, tm=512):
    S, N = h.shape
    K = a.shape[1]

    def body(h_ref, a_ref, w_ref, o_ref):
        o_ref[...] = h_ref[...] + _dot(a_ref[...], w_ref[...])

    return pl.pallas_call(
        body, name=name, grid=(S // tm,),
        in_specs=[pl.BlockSpec((tm, N), lambda i: (i, 0)),
                  pl.BlockSpec((tm, K), lambda i: (i, 0)),
                  pl.BlockSpec((K, N), lambda i: (0, 0))],
        out_specs=pl.BlockSpec((tm, N), lambda i: (i, 0)),
        out_shape=jax.ShapeDtypeStruct((S, N), F32),
        compiler_params=_params(("parallel",)))(h, a, w2)


def _swiglu(z_ref, F):
    g = z_ref[:, :F].astype(F32)
    return (g * _sigmoid(g) * z_ref[:, F:].astype(F32)).astype(BF16)


def _swiglu_mm_res(h, z, w2, name, tm=256):
    S, N = h.shape
    F = w2.shape[0]

    def body(h_ref, z_ref, w_ref, o_ref, a_ref):
        a = _swiglu(z_ref, F)
        a_ref[...] = a
        o_ref[...] = h_ref[...] + _dot(a, w_ref[...])

    return pl.pallas_call(
        body, name=name, grid=(S // tm,),
        in_specs=[pl.BlockSpec((tm, N), lambda i: (i, 0)),
                  pl.BlockSpec((tm, 2 * F), lambda i: (i, 0)),
                  pl.BlockSpec((F, N), lambda i: (0, 0))],
        out_specs=[pl.BlockSpec((tm, N), lambda i: (i, 0)), pl.BlockSpec((tm, F), lambda i: (i, 0))],
        out_shape=[jax.ShapeDtypeStruct((S, N), F32), jax.ShapeDtypeStruct((S, F), BF16)],
        compiler_params=_params(("parallel",)))(h, z, w2)


def _dy_specs(dy, J, n, tm):
    if dy.ndim == 3:
        return [pl.BlockSpec((None, tm, n), functools.partial(lambda i, j: (j, i, 0), j=j)) for j in range(J)]
    return [pl.BlockSpec((tm, n), functools.partial(lambda i, j: (i, j), j=j)) for j in range(J)]


def _acc_nt(dy_refs, w_ref):
    acc = None
    for j, r in enumerate(dy_refs):
        t = _dot_nt(r[...].astype(BF16), w_ref[j])
        acc = t if acc is None else acc + t
    return acc


def _mm_nt(dy, w3, name, out_dtype=F32, tm=512):
    J, K, n = w3.shape
    S = dy.shape[-2]

    def body(*refs):
        dy_refs, w_ref, o_ref = refs[:J], refs[J], refs[J + 1]
        o_ref[...] = _acc_nt(dy_refs, w_ref).astype(o_ref.dtype)

    return pl.pallas_call(
        body, name=name, grid=(S // tm,),
        in_specs=_dy_specs(dy, J, n, tm) + [pl.BlockSpec((J, K, n), lambda i: (0, 0, 0))],
        out_specs=pl.BlockSpec((tm, K), lambda i: (i, 0)),
        out_shape=jax.ShapeDtypeStruct((S, K), out_dtype),
        compiler_params=_params(("parallel",)))(*([dy] * J), w3)


def _mm_nt_normbwd(dy, w3, h, gain, dh, name, tm=512, rider=None):
    J, K, n = w3.shape
    S = h.shape[0]
    steps = S // tm

    def body(*refs):
        dy_refs, w_ref, h_ref, g_ref, dh_ref, o_ref, dg_ref = refs[:J], *refs[J:]
        du = _acc_nt(dy_refs, w_ref)
        v = h_ref[...]
        r = _rstd(v)
        xh = v * r
        dyg = du * g_ref[...]
        o_ref[...] = dh_ref[...] + r * (dyg - xh * jnp.mean(dyg * xh, axis=-1, keepdims=True))

        @pl.when(pl.program_id(0) == 0)
        def _():
            dg_ref[...] = jnp.zeros_like(dg_ref)

        dg_ref[...] += jnp.sum(du * xh, axis=0, keepdims=True)

    row = pl.BlockSpec((tm, K), lambda i: (i, 0))
    vec = pl.BlockSpec((1, K), lambda i: (0, 0))
    r_ops, r_in, r_out, r_shape, r_scr = _rider_args(rider)
    res = pl.pallas_call(
        _ride(rider, body, J + 4, 2, lambda: pl.program_id(0) == 0, lambda: pl.program_id(0) == steps - 1),
        name=name, grid=(steps,),
        in_specs=_dy_specs(dy, J, n, tm) + [pl.BlockSpec((J, K, n), lambda i: (0, 0, 0)), row, vec, row] + r_in,
        out_specs=[row, vec] + r_out,
        out_shape=[jax.ShapeDtypeStruct((S, K), F32), jax.ShapeDtypeStruct((1, K), F32)] + r_shape,
        scratch_shapes=r_scr,
        compiler_params=_params(("arbitrary",)))(*([dy] * J), w3, h, gain, dh, *r_ops)
    return res[0], res[1], res[2:]


def _mm_nt_swiglu_bwd(dh, w2, z, name, tm=256):
    F, N = w2.shape
    S = dh.shape[0]

    def body(dh_ref, w_ref, z_ref, o_ref):
        da = _dot_nt(dh_ref[...].astype(BF16), w_ref[...])
        g = z_ref[:, :F].astype(F32)
        u = z_ref[:, F:].astype(F32)
        sg = _sigmoid(g)
        o_ref[:, :F] = (da * u * (sg * (1.0 + g * (1.0 - sg)))).astype(BF16)
        o_ref[:, F:] = (da * (g * sg)).astype(BF16)

    return pl.pallas_call(
        body, name=name, grid=(S // tm,),
        in_specs=[pl.BlockSpec((tm, N), lambda i: (i, 0)),
                  pl.BlockSpec((F, N), lambda i: (0, 0)),
                  pl.BlockSpec((tm, 2 * F), lambda i: (i, 0))],
        out_specs=pl.BlockSpec((tm, 2 * F), lambda i: (i, 0)),
        out_shape=jax.ShapeDtypeStruct((S, 2 * F), BF16),
        compiler_params=_params(("parallel",)))(dh, w2, z)


def _mm_tn(x, dy, J, n, tn, name):
    tpn = n // tn
    ts = 1024
    S, K = x.shape
    if dy.ndim == 3:
        dy_spec = pl.BlockSpec((None, ts, tn), lambda c, s: (c // tpn, s, c % tpn))
    else:
        dy_spec = pl.BlockSpec((ts, tn), lambda c, s: (s, c))

    def body(x_ref, dy_ref, o_ref):
        @pl.when(pl.program_id(1) == 0)
        def _():
            o_ref[...] = jnp.zeros_like(o_ref)

        o_ref[...] += _dot_tn(x_ref[...], dy_ref[...].astype(BF16))

    return pl.pallas_call(
        body, name=name, grid=(J * tpn, S // ts),
        in_specs=[pl.BlockSpec((ts, K), lambda c, s: (s, 0)), dy_spec],
        out_specs=pl.BlockSpec((None, K, tn), lambda c, s: (c // tpn, 0, c % tpn)),
        out_shape=jax.ShapeDtypeStruct((J, K, n), F32),
        compiler_params=_params(("parallel", "arbitrary")))(x, dy)


def _loss_head(h, gain, target, tm=512):
    S, K = h.shape

    def body(h_ref, g_ref, t_ref, dh_ref, loss_ref, dg_ref):
        v = h_ref[...]
        r = _rstd(v)
        xh = v * r
        g = g_ref[...]
        dy = (xh * g - t_ref[...]) * (1.0 / K)
        dyg = dy * g
        dh_ref[...] = r * (dyg - xh * jnp.mean(dyg * xh, axis=-1, keepdims=True))

        @pl.when(pl.program_id(0) == 0)
        def _():
            loss_ref[...] = jnp.zeros_like(loss_ref)
            dg_ref[...] = jnp.zeros_like(dg_ref)

        part = jnp.sum(jnp.sum(dy * dy, axis=-1, keepdims=True), axis=0, keepdims=True) * (0.5 * K)
        lane = lax.broadcasted_iota(jnp.int32, loss_ref.shape, 1)
        loss_ref[...] += jnp.where(lane == 0, part, 0.0)
        dg_ref[...] += jnp.sum(dy * xh, axis=0, keepdims=True)

    row = pl.BlockSpec((tm, K), lambda i: (i, 0))
    vec = pl.BlockSpec((1, K), lambda i: (0, 0))
    return pl.pallas_call(
        body, name="loss_head", grid=(S // tm,),
        in_specs=[row, vec, row],
        out_specs=[row, pl.BlockSpec((1, HEAD), lambda i: (0, 0)), vec],
        out_shape=[jax.ShapeDtypeStruct((S, K), F32), jax.ShapeDtypeStruct((1, HEAD), F32),
                   jax.ShapeDtypeStruct((1, K), F32)],
        compiler_params=_params(("arbitrary",)))(h, gain, target)


def _lower_bound(lg_ref):
    l0, l1, l2 = lg_ref[0:1, :], lg_ref[1:2, :], lg_ref[2:3, :]
    mx = jnp.maximum(jnp.maximum(l0, l1), l2)
    e0, e1, e2 = jnp.exp(l0 - mx), jnp.exp(l1 - mx), jnp.exp(l2 - mx)
    return e0 / (e0 + e1 + e2)


def _chunks(v, ncb):
    C = HGRN_CHUNK
    return [v[c * C:(c + 1) * C] for c in range(ncb)]


def _rows(parts):
    return jnp.concatenate(parts, axis=0)


def _block_gates(qz, fz, lb, ncb):
    C = HGRN_CHUNK
    row = lax.broadcasted_iota(jnp.int32, (C, C), 0)
    col = lax.broadcasted_iota(jnp.int32, (C, C), 1)
    tri = (col <= row).astype(F32)
    first_half = lax.broadcasted_iota(jnp.int32, (C, HEAD), 0) < C // 2
    sig = _sigmoid(fz)
    fg = lb + (1.0 - lb) * sig
    key = 1.0 - fg
    lg = jnp.log(fg)
    lgs = _chunks(lg, ncb)
    b = _rows([_dot_exact(tri, v) for v in lgs])
    r_c = [jnp.sum(jnp.where(first_half, v, 0.0), axis=0, keepdims=True) for v in lgs]
    bl_c = [jnp.sum(v, axis=0, keepdims=True) for v in lgs]
    r = _rows([jnp.broadcast_to(v, (C, HEAD)) for v in r_c])
    e_br, e_rb = jnp.exp(b - r), jnp.exp(r - b)
    e_b = e_br * _rows([jnp.broadcast_to(jnp.exp(v), (C, HEAD)) for v in r_c])
    e_lb = e_rb * _rows([jnp.broadcast_to(jnp.exp(e - v), (C, HEAD)) for e, v in zip(bl_c, r_c)])
    sq = _sigmoid(qz)
    qy = qz * sq
    return sig, fg, key, (e_br, e_rb, e_b, e_lb), bl_c, sq, qy


def _hgrn_fwd(proj, logits, gain, tb=1024, rider=None):
    S = proj.shape[0]
    H, C = HGRN_HEADS, HGRN_CHUNK
    ncb = tb // C

    def body(q_ref, f_ref, i_ref, g_ref, lg_ref, gn_ref, o_ref, og_ref, st_ref, state):
        @pl.when(pl.program_id(1) == 0)
        def _():
            state[...] = jnp.zeros_like(state)

        lb = _lower_bound(lg_ref)
        causal = lax.broadcasted_iota(jnp.int32, (C, C), 1) <= lax.broadcasted_iota(jnp.int32, (C, C), 0)
        qz, fz, gz = q_ref[...], f_ref[...], g_ref[...]
        _, _, key, (e_br, e_rb, e_b, e_lb), bl_c, _, qy = _block_gates(qz, fz, lb, ncb)
        qs = _chunks((qy * e_br).astype(BF16), ncb)
        ks = _chunks((key * e_rb).astype(BF16), ncb)
        qb = _chunks((qy * e_b).astype(BF16), ncb)
        ke = _chunks((key * e_lb).astype(BF16), ncb)
        vb = _chunks(i_ref[...].astype(BF16), ncb)
        o_intra, upd = [], []
        for c in range(ncb):
            a = jnp.where(causal, _dot_nt(qs[c], ks[c]), 0.0).astype(BF16)
            o_intra.append(_dot(a, vb[c]))
            upd.append(_dot_tn(vb[c], ke[c]))
        st = state[...]
        for c in range(ncb):
            st_ref[c] = st
            st = st * jnp.exp(bl_c[c]) + upd[c]
        state[...] = st
        o = _rows([_dot_nt(qb[c], st_ref[c].astype(BF16)) + o_intra[c] for c in range(ncb)])
        o_ref[...] = o
        og_ref[...] = ((o * _rstd(o) * gn_ref[...]) * (gz * _sigmoid(gz))).astype(BF16)

    def part(p):
        return pl.BlockSpec((tb, HEAD), functools.partial(lambda h, i, p: (i, p * H + h), p=p))

    nb = S // tb
    r_ops, r_in, r_out, r_shape, r_scr = _rider_args(rider)
    res = pl.pallas_call(
        _ride(rider, body, 6, 3, functools.partial(_grid_corner, 0, 0), functools.partial(_grid_corner, H - 1, nb - 1)),
        name="hgrn_fwd", grid=(H, nb),
        in_specs=[part(0), part(1), part(2), part(3),
                  pl.BlockSpec((3, HEAD), lambda h, i: (0, h)),
                  pl.BlockSpec((1, HEAD), lambda h, i: (0, 0))] + r_in,
        out_specs=[pl.BlockSpec((tb, HEAD), lambda h, i: (i, h)),
                   pl.BlockSpec((tb, HEAD), lambda h, i: (i, h)),
                   pl.BlockSpec((None, ncb, HEAD, HEAD), lambda h, i: (h, i, 0, 0))] + r_out,
        out_shape=[jax.ShapeDtypeStruct((S, H * HEAD), F32),
                   jax.ShapeDtypeStruct((S, H * HEAD), BF16),
                   jax.ShapeDtypeStruct((H, S // C, HEAD, HEAD), F32)] + r_shape,
        scratch_shapes=[pltpu.VMEM((HEAD, HEAD), F32)] + r_scr,
        compiler_params=_params(("arbitrary", "arbitrary")))(proj, proj, proj, proj, logits, gain, *r_ops)
    return res[:3], res[3:]


def _hgrn_bwd(proj, logits, gain, o, states, dog, tb=1024, rider=None):
    S = proj.shape[0]
    H, C = HGRN_HEADS, HGRN_CHUNK
    ncb = tb // C
    nb = S // tb

    def body(q_ref, f_ref, i_ref, g_ref, lg_ref, gn_ref, o_ref, st_ref, dog_ref,
             dp_ref, dlb_ref, dgn_ref, dstate, dst_scr):
        @pl.when(pl.program_id(1) == 0)
        def _():
            dstate[...] = jnp.zeros_like(dstate)
            dlb_ref[...] = jnp.zeros_like(dlb_ref)
            dgn_ref[...] = jnp.zeros_like(dgn_ref)

        lb = _lower_bound(lg_ref)
        oml = 1.0 - lb
        gn = gn_ref[...]
        row = lax.broadcasted_iota(jnp.int32, (C, C), 0)
        col = lax.broadcasted_iota(jnp.int32, (C, C), 1)
        causal = col <= row
        tri_up = (col >= row).astype(F32)
        qz, fz, gz = q_ref[...], f_ref[...], g_ref[...]
        sig, fg, key, (e_br, e_rb, e_b, e_lb), bl_c, sq, qy = _block_gates(qz, fz, lb, ncb)
        qs_v, ks_v = (qy * e_br).astype(BF16), (key * e_rb).astype(BF16)
        qb_v, ke_v = (qy * e_b).astype(BF16), (key * e_lb).astype(BF16)
        qs, ks, qb, ke = _chunks(qs_v, ncb), _chunks(ks_v, ncb), _chunks(qb_v, ncb), _chunks(ke_v, ncb)
        vb = _chunks(i_ref[...].astype(BF16), ncb)
        ov = o_ref[...]
        rs = _rstd(ov)
        xh = ov * rs
        sg = _sigmoid(gz)
        dog_v = dog_ref[...]
        dgz = dog_v * (xh * gn) * (sg * (1.0 + gz * (1.0 - sg)))
        don = dog_v * (gz * sg)
        dgn_ref[...] += jnp.sum(don * xh, axis=0, keepdims=True)
        dyg = don * gn
        do = rs * (dyg - xh * jnp.mean(dyg * xh, axis=-1, keepdims=True))
        dob = _chunks(do.astype(BF16), ncb)
        dv_in, dqs, dks, wst = [], [], [], []
        for c in range(ncb):
            a = jnp.where(causal, _dot_nt(qs[c], ks[c]), 0.0).astype(BF16)
            da = jnp.where(causal, _dot_nt(dob[c], vb[c]), 0.0).astype(BF16)
            dv_in.append(_dot_tn(a, dob[c]))
            dqs.append(_dot(da, ks[c]))
            dks.append(_dot_tn(da, qs[c]))
            wst.append(_dot_tn(dob[c], qb[c]))
        e_l = [jnp.exp(v) for v in bl_c]
        dst = dstate[...]
        for c in reversed(range(ncb)):
            dst_scr[c] = dst
            dst = wst[c] + dst * e_l[c]
        dstate[...] = dst
        dv, dqb, dke, dbl_st = [], [], [], []
        for c in range(ncb):
            dst1 = dst_scr[c]
            st0 = st_ref[c]
            dst1b = dst1.astype(BF16)
            dv.append(dv_in[c] + _dot_nt(ke[c], dst1b))
            dqb.append(_dot(dob[c], st0.astype(BF16)))
            dke.append(_dot(vb[c], dst1b))
            dbl_st.append(jnp.sum(dst1 * st0, axis=0, keepdims=True) * e_l[c])
        dqs, dks, dqb, dke, dv = _rows(dqs), _rows(dks), _rows(dqb), _rows(dke), _rows(dv)
        dke_ke = dke * ke_v.astype(F32)
        db = dqs * qs_v.astype(F32) - dks * ks_v.astype(F32) + dqb * qb_v.astype(F32) - dke_ke
        dlg = []
        for c, (db_c, kk_c) in enumerate(zip(_chunks(db, ncb), _chunks(dke_ke, ncb))):
            dbl = jnp.sum(kk_c, axis=0, keepdims=True) + dbl_st[c]
            dlg.append(_dot_exact(tri_up, db_c) + dbl)
        dlg = _rows(dlg)
        dkey = dks * e_rb + dke * e_lb
        dqy = dqs * e_br + dqb * e_b
        dfg = dlg / fg - dkey
        dlb_ref[...] += jnp.sum(dfg * (1.0 - sig), axis=0, keepdims=True)
        dp_ref[0] = (dqy * (sq * (1.0 + qz * (1.0 - sq)))).astype(BF16)
        dp_ref[1] = (dfg * oml * sig * (1.0 - sig)).astype(BF16)
        dp_ref[2] = dv.astype(BF16)
        dp_ref[3] = dgz.astype(BF16)

    def part(p):
        return pl.BlockSpec((tb, HEAD), functools.partial(lambda h, i, p: (nb - 1 - i, p * H + h), p=p))

    blk = pl.BlockSpec((tb, HEAD), lambda h, i: (nb - 1 - i, h))
    acc = pl.BlockSpec((None, 1, HEAD), lambda h, i: (h, 0, 0))
    r_ops, r_in, r_out, r_shape, r_scr = _rider_args(rider)
    res = pl.pallas_call(
        _ride(rider, body, 9, 3, functools.partial(_grid_corner, 0, 0), functools.partial(_grid_corner, H - 1, nb - 1)),
        name="hgrn_bwd", grid=(H, nb),
        in_specs=[part(0), part(1), part(2), part(3),
                  pl.BlockSpec((3, HEAD), lambda h, i: (0, h)),
                  pl.BlockSpec((1, HEAD), lambda h, i: (0, 0)),
                  blk,
                  pl.BlockSpec((None, ncb, HEAD, HEAD), lambda h, i: (h, nb - 1 - i, 0, 0)),
                  blk] + r_in,
        out_specs=[pl.BlockSpec((4, tb, HEAD), lambda h, i: (0, nb - 1 - i, h)), acc, acc] + r_out,
        out_shape=[jax.ShapeDtypeStruct((4, S, H * HEAD), BF16),
                   jax.ShapeDtypeStruct((H, 1, HEAD), F32),
                   jax.ShapeDtypeStruct((H, 1, HEAD), F32)] + r_shape,
        scratch_shapes=[pltpu.VMEM((HEAD, HEAD), F32), pltpu.VMEM((ncb, HEAD, HEAD), F32)] + r_scr,
        compiler_params=_params(("arbitrary", "arbitrary")))(
            proj, proj, proj, proj, logits, gain, o, states, dog, *r_ops)
    return res[:3], res[3:]


def _rope(v, cos, sin):
    return v * cos + pltpu.roll(v, HEAD // 2, 1) * sin


def _lane_pick(tile, hh):
    lane = lax.broadcasted_iota(jnp.int32, tile.shape, 1)
    return jnp.sum(jnp.where(lane == hh, tile, 0.0), axis=-1, keepdims=True)


def _lane_place(cols):
    rows = cols[0].shape[0]
    lane = lax.broadcasted_iota(jnp.int32, (rows, HEAD), 1)
    tile = jnp.zeros((rows, HEAD), F32)
    for hh, v in enumerate(cols):
        tile = jnp.where(lane == hh, v, tile)
    return tile


def _band_masks():
    qi = lax.broadcasted_iota(jnp.int32, (ATTN_SPAN, ATTN_SPAN), 0)
    kj = lax.broadcasted_iota(jnp.int32, (ATTN_SPAN, ATTN_SPAN), 1)
    return kj <= qi, kj >= qi


def _attn_fwd(a):
    d, L, _ = a.shape
    nb = L // ATTN_SPAN
    scale = HEAD ** -0.5

    def body(q_ref, kc_ref, kp_ref, vc_ref, vp_ref, o_ref, lse_ref):
        n = pl.program_id(1)
        mask_c, mask_p0 = _band_masks()
        mask_p = jnp.logical_and(mask_p0, n > 0)
        heads = [slice(hh * HEAD, (hh + 1) * HEAD) for hh in range(HEADS_PER_GROUP)]
        s_c = [jnp.where(mask_c, _dot_nt(q_ref[:, c], kc_ref[:, c]) * scale, NEG) for c in heads]
        s_p = [jnp.where(mask_p, _dot_nt(q_ref[:, c], kp_ref[:, c]) * scale, NEG) for c in heads]
        m = [jnp.maximum(jnp.max(a, axis=-1, keepdims=True), jnp.max(b, axis=-1, keepdims=True)) for a, b in zip(s_c, s_p)]
        p_c = [jnp.exp(a - mm) for a, mm in zip(s_c, m)]
        p_p = [jnp.exp(b - mm) for b, mm in zip(s_p, m)]
        l = [jnp.sum(a, axis=-1, keepdims=True) + jnp.sum(b, axis=-1, keepdims=True) for a, b in zip(p_c, p_p)]
        acc = [_dot(a.astype(BF16), vc_ref[:, c]) + _dot(b.astype(BF16), vp_ref[:, c]) for a, b, c in zip(p_c, p_p, heads)]
        for c, a, ll in zip(heads, acc, l):
            o_ref[:, c] = a / ll
        lse_ref[...] = _lane_place([mm + jnp.log(ll) for mm, ll in zip(m, l)])

    def blk(part, prev):
        if prev:
            return pl.BlockSpec((None, ATTN_SPAN, GROUP_W), functools.partial(lambda r, n, p: (r, jnp.maximum(n - 1, 0), p), p=part))
        return pl.BlockSpec((None, ATTN_SPAN, GROUP_W), functools.partial(lambda r, n, p: (r, n, p), p=part))

    out = pl.BlockSpec((None, ATTN_SPAN, GROUP_W), lambda r, n: (r, n, 0))
    return pl.pallas_call(
        body, name=f"attn_fwd_d{d}", grid=(d, nb),
        in_specs=[blk(0, False), blk(1, False), blk(1, True), blk(2, False), blk(2, True)],
        out_specs=[out, pl.BlockSpec((None, ATTN_SPAN, HEAD), lambda r, n: (r, n, 0))],
        out_shape=[jax.ShapeDtypeStruct((d, L, GROUP_W), F32), jax.ShapeDtypeStruct((d, L, HEAD), F32)],
        compiler_params=_params(("parallel", "arbitrary")))(a, a, a, a, a)


def _attn_bwd(a, do, lse, dd):
    d, L, _ = a.shape
    nb = L // ATTN_SPAN
    scale = HEAD ** -0.5

    def body(qc_ref, qn_ref, kp_ref, kc_ref, vp_ref, vc_ref, doc_ref, don_ref, lc_ref, ln_ref, ddc_ref, ddn_ref, da_ref):
        n = pl.program_id(1)
        mask_c, mask_p0 = _band_masks()
        mask_p = jnp.logical_and(mask_p0, n > 0)
        mask_n = jnp.logical_and(mask_p0, n < nb - 1)
        H4 = range(HEADS_PER_GROUP)
        heads = [slice(hh * HEAD, (hh + 1) * HEAD) for hh in H4]
        do_c = [doc_ref[:, c] for c in heads]
        do_n = [don_ref[:, c] for c in heads]
        lse_c, lse_n = [_lane_pick(lc_ref[...], i) for i in H4], [_lane_pick(ln_ref[...], i) for i in H4]
        dd_c, dd_n = [_lane_pick(ddc_ref[...], i) for i in H4], [_lane_pick(ddn_ref[...], i) for i in H4]
        p_c = [jnp.where(mask_c, jnp.exp(_dot_nt(qc_ref[:, heads[i]], kc_ref[:, heads[i]]) * scale - lse_c[i]), 0.0) for i in H4]
        p_p = [jnp.where(mask_p, jnp.exp(_dot_nt(qc_ref[:, heads[i]], kp_ref[:, heads[i]]) * scale - lse_c[i]), 0.0) for i in H4]
        p_n = [jnp.where(mask_n, jnp.exp(_dot_nt(qn_ref[:, heads[i]], kc_ref[:, heads[i]]) * scale - lse_n[i]), 0.0) for i in H4]
        ds_c = [(p_c[i] * (_dot_nt(do_c[i], vc_ref[:, heads[i]]) + dd_c[i])).astype(BF16) for i in H4]
        ds_p = [(p_p[i] * (_dot_nt(do_c[i], vp_ref[:, heads[i]]) + dd_c[i])).astype(BF16) for i in H4]
        ds_n = [(p_n[i] * (_dot_nt(do_n[i], vc_ref[:, heads[i]]) + dd_n[i])).astype(BF16) for i in H4]
        dq = [(_dot(ds_c[i], kc_ref[:, heads[i]]) + _dot(ds_p[i], kp_ref[:, heads[i]])) * scale for i in H4]
        dk = [(_dot_tn(ds_c[i], qc_ref[:, heads[i]]) + _dot_tn(ds_n[i], qn_ref[:, heads[i]])) * scale for i in H4]
        dv = [_dot_tn(p_c[i].astype(BF16), do_c[i]) + _dot_tn(p_n[i].astype(BF16), do_n[i]) for i in H4]
        for i in H4:
            da_ref[:, heads[i]] = dq[i].astype(BF16)
            da_ref[:, GROUP_W + i * HEAD:GROUP_W + (i + 1) * HEAD] = dk[i].astype(BF16)
            da_ref[:, 2 * GROUP_W + i * HEAD:2 * GROUP_W + (i + 1) * HEAD] = dv[i].astype(BF16)

    def rel(delta):
        if delta < 0:
            return lambda n: jnp.maximum(n - 1, 0)
        if delta > 0:
            return lambda n: jnp.minimum(n + 1, nb - 1)
        return lambda n: n

    def blk(width, part, delta):
        f = rel(delta)
        return pl.BlockSpec((None, ATTN_SPAN, width), functools.partial(lambda r, n, p, f: (r, f(n), p), p=part, f=f))

    g = GROUP_W
    return pl.pallas_call(
        body, name=f"attn_bwd_d{d}", grid=(d, nb),
        in_specs=[blk(g, 0, 0), blk(g, 0, 1), blk(g, 1, -1), blk(g, 1, 0), blk(g, 2, -1), blk(g, 2, 0),
                  blk(g, 0, 0), blk(g, 0, 1), blk(HEAD, 0, 0), blk(HEAD, 0, 1), blk(HEAD, 0, 0), blk(HEAD, 0, 1)],
        out_specs=pl.BlockSpec((None, ATTN_SPAN, 3 * g), lambda r, n: (r, n, 0)),
        out_shape=jax.ShapeDtypeStruct((d, L, 3 * g), BF16),
        compiler_params=_params(("parallel", "arbitrary")))(
            a, a, a, a, a, a, do, do, lse, lse, dd, dd)


def _softmax3(ls):
    mx = jnp.maximum(jnp.maximum(ls[0], ls[1]), ls[2])
    es = [jnp.exp(v - mx) for v in ls]
    tot = es[0] + es[1] + es[2]
    return [e / tot for e in es]


HEAD_COLS = [slice(hh * HEAD, (hh + 1) * HEAD) for hh in range(HEADS_PER_GROUP)]


def _group_spec(d, tm):
    return pl.BlockSpec((d, tm // d, GROUP_W), lambda i: (0, i, 0))


def _gather_heads(ref, scr, d, tm):
    if d == 1:
        return [ref[0, :, cols].astype(F32) for cols in HEAD_COLS]
    for hh, cols in enumerate(HEAD_COLS):
        for r in range(d):
            scr.at[hh][pl.ds(r, tm // d, stride=d), :] = ref[r, :, cols].astype(F32)
    return [scr[hh] for hh in range(HEADS_PER_GROUP)]


def _tile_spec(d, tm):
    return pl.BlockSpec((d, tm // d, HEAD), lambda i: (0, i, 0))


def _gather_tile(ref, scr, d, tm):
    if d == 1:
        return ref[0]
    for r in range(d):
        scr[pl.ds(r, tm // d, stride=d), :] = ref[r]
    return scr[...]


def _scatter_tile(val, scr, ref, d, tm):
    if d == 1:
        ref[0] = val
        return
    scr[...] = val
    for r in range(d):
        ref[r] = scr[pl.ds(r, tm // d, stride=d), :]


def _scatter_heads(vals, scr, ref, d, tm):
    if d == 1:
        for cols, v in zip(HEAD_COLS, vals):
            ref[0, :, cols] = v.astype(ref.dtype)
        return
    for hh, v in enumerate(vals):
        scr[hh] = v
    for hh, cols in enumerate(HEAD_COLS):
        for r in range(d):
            ref[r, :, cols] = scr.at[hh][pl.ds(r, tm // d, stride=d), :].astype(ref.dtype)


def _qkv_dilated(h, gain, wg, cos, sin, d, tm=512):
    S, K = h.shape

    def body(h_ref, g_ref, w_ref, cos_ref, sin_ref, out_ref, u_ref, y_scr):
        p = pl.program_id(1)

        @pl.when(p == 0)
        def _():
            v = h_ref[...]
            u_ref[...] = (v * _rstd(v) * g_ref[...]).astype(BF16)

        y = _dot(u_ref[...], w_ref[...])
        heads = [slice(hh * HEAD, (hh + 1) * HEAD) for hh in range(HEADS_PER_GROUP)]
        for hh, cols in enumerate(heads):
            y_scr[hh] = y[:, cols]

        @pl.when(p < 2)
        def _():
            for r in range(d):
                rows = slice(None) if d == 1 else pl.ds(r, tm // d, stride=d)
                cr, sr = cos_ref[rows, :], sin_ref[rows, :]
                for hh, cols in enumerate(heads):
                    out_ref[r, :, cols] = _rope(y_scr.at[hh][rows, :], cr, sr).astype(BF16)

        @pl.when(p == 2)
        def _():
            for r in range(d):
                rows = slice(None) if d == 1 else pl.ds(r, tm // d, stride=d)
                for hh, cols in enumerate(heads):
                    out_ref[r, :, cols] = y_scr.at[hh][rows, :].astype(BF16)

    tab = pl.BlockSpec((tm, HEAD), lambda i, p: (i, 0))
    return pl.pallas_call(
        body, name=f"attn_qkv_d{d}", grid=(S // tm, 3),
        in_specs=[pl.BlockSpec((tm, K), lambda i, p: (i, 0)),
                  pl.BlockSpec((1, K), lambda i, p: (0, 0)),
                  pl.BlockSpec((K, GROUP_W), lambda i, p: (0, p)), tab, tab],
        out_specs=[pl.BlockSpec((d, tm // d, GROUP_W), lambda i, p: (0, i, p)), pl.BlockSpec((tm, K), lambda i, p: (i, 0))],
        out_shape=[jax.ShapeDtypeStruct((d, S // d, 3 * GROUP_W), BF16), jax.ShapeDtypeStruct((S, K), BF16)],
        scratch_shapes=[pltpu.VMEM((HEADS_PER_GROUP, tm, HEAD), F32)],
        compiler_params=_params(("parallel", "arbitrary")))(h, gain, wg, cos, sin)


def _undilate_group(da, dqkv, cos, sin, g, tm=512):
    d, L, _ = da.shape
    S = d * L
    G = len(ATTN_GROUPS)

    def body(*refs):
        da_ref, cos_ref, sin_ref, out_ref, scr = refs[0], refs[1], refs[2], refs[-2], refs[-1]
        p = pl.program_id(1)
        heads = [slice(hh * HEAD, (hh + 1) * HEAD) for hh in range(HEADS_PER_GROUP)]
        for hh, cols in enumerate(heads):
            if d == 1:
                scr[hh] = da_ref[0, :, cols].astype(F32)
            else:
                for r in range(d):
                    scr.at[hh][pl.ds(r, tm // d, stride=d), :] = da_ref[r, :, cols].astype(F32)

        @pl.when(p < 2)
        def _():
            cr, sr = cos_ref[...], -sin_ref[...]
            for hh, cols in enumerate(heads):
                out_ref[:, cols] = _rope(scr[hh], cr, sr).astype(BF16)

        @pl.when(p == 2)
        def _():
            for hh, cols in enumerate(heads):
                out_ref[:, cols] = scr[hh].astype(BF16)

    tab = pl.BlockSpec((tm, HEAD), lambda i, p: (i, 0))
    operands = (da, cos, sin) if dqkv is None else (da, cos, sin, dqkv)
    return pl.pallas_call(
        body, name=f"attn_undilate_d{d}", grid=(S // tm, 3),
        in_specs=[pl.BlockSpec((d, tm // d, GROUP_W), lambda i, p: (0, i, p)), tab, tab] + ([] if dqkv is None else [ANY]),
        out_specs=pl.BlockSpec((tm, GROUP_W), lambda i, p: (i, p * G + g)),
        out_shape=jax.ShapeDtypeStruct((S, 3 * G * GROUP_W), BF16),
        input_output_aliases={} if dqkv is None else {3: 0},
        scratch_shapes=[pltpu.VMEM((HEADS_PER_GROUP, tm, HEAD), F32)],
        compiler_params=_params(("parallel", "arbitrary")))(*operands)


def _attn_merge(os_, lses, tm=512):
    G = len(os_)
    S = os_[0].shape[0] * os_[0].shape[1]

    def body(*refs):
        o_refs, l_refs, out_ref = refs[:G], refs[G:2 * G], refs[2 * G]
        scr = refs[2 * G + 1:]
        o = [_gather_heads(o_refs[g], scr[g], d, tm) for g, (_, d) in enumerate(ATTN_GROUPS)]
        l = [_gather_tile(l_refs[g], scr[G + g].at[0], d, tm) for g, (_, d) in enumerate(ATTN_GROUPS)]
        for hh in range(HEADS_PER_GROUP):
            al = _softmax3([_lane_pick(l[g], hh) for g in range(G)])
            for g in range(G):
                out_ref[:, g * GROUP_W + hh * HEAD:g * GROUP_W + (hh + 1) * HEAD] = (o[g][hh] * al[g]).astype(BF16)

    specs = [_group_spec(d, tm) for _, d in ATTN_GROUPS]
    return pl.pallas_call(
        body, name="attn_merge", grid=(S // tm,),
        in_specs=specs + [_tile_spec(d, tm) for _, d in ATTN_GROUPS],
        out_specs=pl.BlockSpec((tm, G * GROUP_W), lambda i: (i, 0)),
        out_shape=jax.ShapeDtypeStruct((S, G * GROUP_W), BF16),
        scratch_shapes=[pltpu.VMEM((HEADS_PER_GROUP, tm, HEAD), F32)] * (2 * G),
        compiler_params=_params(("parallel",)))(*os_, *lses)


def _attn_merge_bwd(os_, lses, doa, tm=512):
    G = len(os_)
    S = doa.shape[0]

    def body(*refs):
        o_refs, l_refs, doa_ref = refs[:G], refs[G:2 * G], refs[2 * G]
        do_refs, dd_refs = refs[2 * G + 1:3 * G + 1], refs[3 * G + 1:4 * G + 1]
        scr = refs[4 * G + 1:]
        o = [_gather_heads(o_refs[g], scr[g], d, tm) for g, (_, d) in enumerate(ATTN_GROUPS)]
        l = [_gather_tile(l_refs[g], scr[G + g].at[0], d, tm) for g, (_, d) in enumerate(ATTN_GROUPS)]
        do = [[None] * HEADS_PER_GROUP for _ in range(G)]
        dd = [[None] * HEADS_PER_GROUP for _ in range(G)]
        for hh in range(HEADS_PER_GROUP):
            al = _softmax3([_lane_pick(l[g], hh) for g in range(G)])
            mix = None
            for g in range(G):
                dg = doa_ref[:, g * GROUP_W + hh * HEAD:g * GROUP_W + (hh + 1) * HEAD]
                do[g][hh] = dg * al[g]
                t = al[g] * jnp.sum(dg * o[g][hh], axis=-1, keepdims=True)
                mix = t if mix is None else mix + t
            for g in range(G):
                dd[g][hh] = -al[g] * mix
        for g, (_, d) in enumerate(ATTN_GROUPS):
            _scatter_heads(do[g], scr[2 * G + g], do_refs[g], d, tm)
            _scatter_tile(_lane_place(dd[g]), scr[3 * G + g].at[0], dd_refs[g], d, tm)

    specs = [_group_spec(d, tm) for _, d in ATTN_GROUPS]
    tiles = [_tile_spec(d, tm) for _, d in ATTN_GROUPS]
    do_shapes = [jax.ShapeDtypeStruct((d, S // d, GROUP_W), BF16) for _, d in ATTN_GROUPS]
    dd_shapes = [jax.ShapeDtypeStruct((d, S // d, HEAD), F32) for _, d in ATTN_GROUPS]
    return pl.pallas_call(
        body, name="attn_merge_bwd", grid=(S // tm,),
        in_specs=specs + tiles + [pl.BlockSpec((tm, G * GROUP_W), lambda i: (i, 0))],
        out_specs=specs + tiles,
        out_shape=do_shapes + dd_shapes,
        scratch_shapes=[pltpu.VMEM((HEADS_PER_GROUP, tm, HEAD), F32)] * (4 * G),
        compiler_params=_params(("parallel",)))(*os_, *lses, doa)


def _rope_tables(S):
    inv_freq = 1.0 / (ROPE_THETA ** (jnp.arange(0, HEAD, 2, dtype=F32) / HEAD))
    ang = jnp.arange(S, dtype=F32)[:, None] * inv_freq[None, :]
    cos, sin = jnp.cos(ang), jnp.sin(ang)
    return jnp.concatenate([cos, cos], axis=-1), jnp.concatenate([-sin, sin], axis=-1)


def _local_step(x, target, norm_mix, norm_ffn, lb_logits, out_gain, final_norm, comm):
    S = x.shape[0]
    nm0, nm1 = norm_mix[0:1], norm_mix[1:2]
    nf0, nf1 = norm_ffn[0:1], norm_ffn[1:2]
    w = comm.first_weights()

    proj, u0, got = _norm_mm(x, nm0, w["hin"], "hgrn_in", rider=comm.gather_rider(LATE_WEIGHTS_A))
    w.update(comm.gathered(LATE_WEIGHTS_A, got))
    (o, og, states), got = _hgrn_fwd(proj, lb_logits, out_gain, rider=comm.gather_rider(LATE_WEIGHTS_B))
    w.update(comm.gathered(LATE_WEIGHTS_B, got))
    fin_tn = w["fin0"].shape[2]
    h1 = _mm_res(x, og, w["hout"], "hgrn_out")
    z0, u1, _ = _norm_mm(h1, nf0, w["fin0"], "ffn0_in", out_dtype=BF16)
    h2, act0 = _swiglu_mm_res(h1, z0, w["fdn0"], "ffn0_down")
    cos, sin = _rope_tables(S)
    G = len(ATTN_GROUPS)
    w_groups = w["qkv"].transpose(1, 0, 2).reshape(D_MODEL, 3, G, GROUP_W)
    a_g, u2 = zip(*[_qkv_dilated(h2, nm1, w_groups[:, :, gi, :].reshape(D_MODEL, 3 * GROUP_W), cos, sin, d)
                    for gi, (_, d) in enumerate(ATTN_GROUPS)])
    o_g, lse_g = zip(*[_attn_fwd(a) for a in a_g])
    oa = _attn_merge(o_g, lse_g)
    h3 = _mm_res(h2, oa, w["aout"], "attn_out")
    z1, u3, _ = _norm_mm(h3, nf1, w["fin1"], "ffn1_in", out_dtype=BF16)
    h4, act1 = _swiglu_mm_res(h3, z1, w["fdn1"], "ffn1_down")
    dh4, loss, d_final = _loss_head(h4, final_norm, target)

    grads, small = {}, {"final_norm": d_final}

    def ffn_bwd(dh, h_in, u_in, z, act, gain, w_in, w_dn, tag, ride=None):
        dz = _mm_nt_swiglu_bwd(dh, w_dn, z, tag + "_down_dx")
        g_dn = _mm_tn(act, dh, 1, D_MODEL, D_MODEL, tag + "_down_dw")[0]
        g_in = _mm_tn(u_in, dz, N_CHIPS, fin_tn, fin_tn, tag + "_in_dw")
        rider = None if ride is None else ride(g_in, g_dn)
        dh_in, dgain, got = _mm_nt_normbwd(dz, w_in, h_in, gain, dh, tag + "_in_dx", rider=rider)
        return dh_in, dgain, g_in, g_dn, got

    dh3, d_nf1, grads["fin1"], grads["fdn1"], _ = ffn_bwd(dh4, h3, u3, z1, act1, nf1, w["fin1"], w["fdn1"], "ffn1")
    doa = _mm_nt(dh3, w["aout"][None], "attn_out_dx")
    grads["aout"] = _mm_tn(oa, dh3, 1, D_MODEL, D_MODEL, "attn_out_dw")[0]
    merged = _attn_merge_bwd(o_g, lse_g, doa)
    G = len(ATTN_GROUPS)
    das = [_attn_bwd(a_g[gi], merged[gi], lse_g[gi], merged[G + gi]) for gi in range(G)]
    dqkv = None
    for gi in range(G):
        dqkv = _undilate_group(das[gi], dqkv, cos, sin, gi)
    n_qkv = w["qkv"].shape[2]
    grads["qkv"] = _mm_tn(u2[0], dqkv, N_CHIPS, n_qkv, n_qkv, "attn_qkv_dw")
    dh2, d_nm1, _ = _mm_nt_normbwd(dqkv, w["qkv"], h2, nm1, dh3, "attn_qkv_dx")

    def ride_early(g_in, g_dn):
        return comm.pair_rider({**grads, "fin0": g_in, "fdn0": g_dn}, "early")

    dh1, d_nf0, _, _, got = ffn_bwd(dh2, h1, u1, z0, act0, nf0, w["fin0"], w["fdn0"], "ffn0", ride=ride_early)
    comm.paired("early", got)
    dog = _mm_nt(dh1, w["hout"][None], "hgrn_out_dx")
    (dproj, dlb, dgn), got = _hgrn_bwd(proj, lb_logits, out_gain, o, states, dog, rider=comm.exchange_rider("early"))
    comm.exchanged("early", got)
    late = {"hout": _mm_tn(og, dh1, 1, D_MODEL, D_MODEL, "hgrn_out_dw")[0],
            "hin": _mm_tn(u0, dproj, N_CHIPS, D_MODEL, D_MODEL, "hgrn_in_dw")}
    comm.pair_now(late, "late")
    dx, d_nm0, got = _mm_nt_normbwd(dproj, w["hin"], x, nm0, dh1, "hgrn_in_dx", rider=comm.exchange_rider("late"))
    comm.exchanged("late", got)

    small["norm_mix"] = jnp.concatenate([d_nm0, d_nm1], axis=0)
    small["norm_ffn"] = jnp.concatenate([d_nf0, d_nf1], axis=0)
    small["lb"] = dlb.reshape(1, HGRN_HEADS * HEAD)
    small["out_norm"] = dgn.reshape(HGRN_HEADS, HEAD)
    return loss, dx, small


def _place():
    x, y, c = lax.axis_index("x"), lax.axis_index("y"), lax.axis_index("c")
    others = [(1 - x, y), (x, 1 - y), (1 - x, 1 - y)]
    return x, y, c, others


ANY = pl.BlockSpec(memory_space=pl.ANY)


class _GatherRider:
    def __init__(self, shards):
        self.operands = list(shards)
        n = self.n = len(shards)
        self.out_shape = [jax.ShapeDtypeStruct((N_CHIPS,) + s.shape, s.dtype) for s in shards]
        self.scratch = [pltpu.SemaphoreType.DMA((3 * n,)), pltpu.SemaphoreType.DMA((3 * n,)),
                        pltpu.SemaphoreType.DMA((3 * n,)), pltpu.SemaphoreType.DMA((3 * n,)),
                        pltpu.SemaphoreType.DMA((n,)), pltpu.SemaphoreType.DMA((n,))]

    def _copies(self, ins, outs, sems):
        ici_send, ici_recv, _, _, own_send, own_recv = sems
        x, y, c, others = _place()
        me = 2 * x + y
        own = [pltpu.make_async_remote_copy(
            src_ref=ins[a], dst_ref=outs[a].at[me], send_sem=own_send.at[a], recv_sem=own_recv.at[a],
            device_id=(x, y, 1 - c), device_id_type=MESH) for a in range(self.n)]
        sends = [pltpu.make_async_remote_copy(
            src_ref=ins[a].at[c], dst_ref=outs[a].at[me, c], send_sem=ici_send.at[a * 3 + k], recv_sem=ici_recv.at[a * 3 + k],
            device_id=(ox, oy, c), device_id_type=MESH) for a in range(self.n) for k, (ox, oy) in enumerate(others)]
        return own, sends

    def start(self, ins, outs, sems):
        own, sends = self._copies(ins, outs, sems)
        for cp in own + sends:
            cp.start()

    def finish(self, ins, outs, sems):
        ici_send, ici_recv, d2d_send, d2d_recv, _, _ = sems
        x, y, c, others = _place()
        sibling = (x, y, 1 - c)
        own, sends = self._copies(ins, outs, sems)
        passes = []
        for a in range(self.n):
            for k, (ox, oy) in enumerate(others):
                s = a * 3 + k
                got = outs[a].at[2 * ox + oy, c]
                pltpu.make_async_remote_copy(
                    src_ref=got, dst_ref=got, send_sem=ici_send.at[s], recv_sem=ici_recv.at[s],
                    device_id=(ox, oy, c), device_id_type=MESH).wait_recv()
                fwd = pltpu.make_async_remote_copy(
                    src_ref=got, dst_ref=got, send_sem=d2d_send.at[s], recv_sem=d2d_recv.at[s],
                    device_id=sibling, device_id_type=MESH)
                fwd.start()
                passes.append(fwd)
        for a in range(self.n):
            for k, (ox, oy) in enumerate(others):
                s = a * 3 + k
                theirs = outs[a].at[2 * ox + oy, 1 - c]
                pltpu.make_async_remote_copy(
                    src_ref=theirs, dst_ref=theirs, send_sem=d2d_send.at[s], recv_sem=d2d_recv.at[s],
                    device_id=sibling, device_id_type=MESH).wait_recv()
        for cp in own:
            cp.wait()
        for cp in sends + passes:
            cp.wait_send()


class _PairRider:
    def __init__(self, grads):
        self.operands = list(grads)
        n = self.n = len(grads)
        self.out_shape = [jax.ShapeDtypeStruct((N_CHIPS,) + g.shape[2:], F32) for g in grads]
        self.scratch = [pltpu.SemaphoreType.DMA((N_CHIPS * n,)), pltpu.SemaphoreType.DMA((N_CHIPS * n,))]

    def _copies(self, ins, outs, sems):
        send_sem, recv_sem = sems
        x, y, c, _ = _place()
        return [pltpu.make_async_remote_copy(
            src_ref=ins[a].at[j, 1 - c], dst_ref=outs[a].at[j], send_sem=send_sem.at[a * N_CHIPS + j],
            recv_sem=recv_sem.at[a * N_CHIPS + j], device_id=(x, y, 1 - c), device_id_type=MESH)
            for a in range(self.n) for j in range(N_CHIPS)]

    def start(self, ins, outs, sems):
        for cp in self._copies(ins, outs, sems):
            cp.start()

    def finish(self, ins, outs, sems):
        for cp in self._copies(ins, outs, sems):
            cp.wait()


class _ExchangeRider:
    def __init__(self, parts):
        self.operands = list(parts)
        n = self.n = len(parts)
        self.out_shape = [jax.ShapeDtypeStruct(p.shape, p.dtype) for p in parts]
        self.scratch = [pltpu.SemaphoreType.DMA((3 * n,)), pltpu.SemaphoreType.DMA((3 * n,))]

    def _copies(self, ins, outs, sems):
        send_sem, recv_sem = sems
        x, y, c, others = _place()
        me = 2 * x + y
        return [pltpu.make_async_remote_copy(
            src_ref=ins[a].at[2 * ox + oy], dst_ref=outs[a].at[me], send_sem=send_sem.at[a * 3 + k],
            recv_sem=recv_sem.at[a * 3 + k], device_id=(ox, oy, c), device_id_type=MESH)
            for a in range(self.n) for k, (ox, oy) in enumerate(others)]

    def start(self, ins, outs, sems):
        for cp in self._copies(ins, outs, sems):
            cp.start()

    def finish(self, ins, outs, sems):
        send_sem, recv_sem = sems
        x, y, c, others = _place()
        for a in range(self.n):
            for k, (ox, oy) in enumerate(others):
                s = a * 3 + k
                got = outs[a].at[2 * ox + oy]
                pltpu.make_async_remote_copy(
                    src_ref=got, dst_ref=got, send_sem=send_sem.at[s], recv_sem=recv_sem.at[s],
                    device_id=(ox, oy, c), device_id_type=MESH).wait_recv()
        for cp in self._copies(ins, outs, sems):
            cp.wait_send()


def _run_rider(rider, name):
    n = rider.n

    def body(*refs):
        ins, outs, sems = refs[:n], refs[n:2 * n], refs[2 * n:]
        rider.start(ins, outs, sems)
        rider.finish(ins, outs, sems)

    return pl.pallas_call(
        body, name=name, in_specs=[ANY] * n, out_specs=[ANY] * n,
        out_shape=rider.out_shape, scratch_shapes=rider.scratch)(*rider.operands)


def _ride(rider, body, n_in, n_out, first, last):
    if rider is None:
        return body
    n = rider.n

    def wrapped(*refs):
        host_in, r_in = refs[:n_in], refs[n_in:n_in + n]
        host_out = refs[n_in + n:n_in + n + n_out]
        r_out = refs[n_in + n + n_out:n_in + 2 * n + n_out]
        rest = refs[n_in + 2 * n + n_out:]
        host_scr, sems = rest[:len(rest) - len(rider.scratch)], rest[len(rest) - len(rider.scratch):]

        @pl.when(first())
        def _():
            rider.start(r_in, r_out, sems)

        body(*host_in, *host_out, *host_scr)

        @pl.when(last())
        def _():
            rider.finish(r_in, r_out, sems)

    return wrapped


def _rider_args(rider):
    if rider is None:
        return [], [], [], [], []
    return rider.operands, [ANY] * rider.n, [ANY] * rider.n, rider.out_shape, rider.scratch


def _pair_sum(g, got, c_idx):
    _, _, r, cw = g.shape
    tr = _row_tile(r, cw)

    def body(c_ref, g_ref, got_ref, p_ref, pb_ref):
        v = g_ref[...] + got_ref[...]
        p_ref[...] = v
        pb_ref[...] = v.astype(BF16)

    blk = pl.BlockSpec((None, tr, cw), lambda j, i, c_ref: (j, i, 0))
    return pl.pallas_call(
        body, name="grad_pair_sum",
        grid_spec=pltpu.PrefetchScalarGridSpec(
            num_scalar_prefetch=1, grid=(N_CHIPS, r // tr),
            in_specs=[pl.BlockSpec((None, None, tr, cw), lambda j, i, c_ref: (j, c_ref[0], i, 0)), blk],
            out_specs=[blk, blk]),
        out_shape=[jax.ShapeDtypeStruct((N_CHIPS, r, cw), F32), jax.ShapeDtypeStruct((N_CHIPS, r, cw), BF16)],
        compiler_params=_params(("parallel", "parallel")))(c_idx, g, got)


def _chip_sum(p, got, me_idx):
    _, r, cw = p.shape
    tr = _row_tile(r, cw)

    def body(me_ref, own_ref, got_ref, t_ref):
        me = me_ref[0]
        acc = None
        for s in range(N_CHIPS):
            term = jnp.where(me == s, own_ref[...], got_ref[s].astype(F32))
            acc = term if acc is None else acc + term
        t_ref[...] = acc

    return pl.pallas_call(
        body, name="grad_chip_sum",
        grid_spec=pltpu.PrefetchScalarGridSpec(
            num_scalar_prefetch=1, grid=(r // tr,),
            in_specs=[pl.BlockSpec((None, tr, cw), lambda i, me_ref: (me_ref[0], i, 0)),
                      pl.BlockSpec((N_CHIPS, tr, cw), lambda i, me_ref: (0, i, 0))],
            out_specs=pl.BlockSpec((tr, cw), lambda i, me_ref: (i, 0))),
        out_shape=jax.ShapeDtypeStruct((r, cw), F32),
        compiler_params=_params(("parallel",)))(me_idx, p, got)


def _pair_share(halves):
    n = len(halves)

    def body(*refs):
        ins, outs = refs[:n], refs[n:2 * n]
        send_sem, recv_sem = refs[2 * n:]
        x, y, c, _ = _place()
        cps = [pltpu.make_async_remote_copy(
            src_ref=ins[a], dst_ref=outs[a], send_sem=send_sem.at[a], recv_sem=recv_sem.at[a],
            device_id=(x, y, 1 - c), device_id_type=MESH) for a in range(n)]
        for cp in cps:
            cp.start()
        for cp in cps:
            cp.wait()

    return pl.pallas_call(
        body, name="grad_pair_share",
        in_specs=[ANY] * n, out_specs=[ANY] * n,
        out_shape=[jax.ShapeDtypeStruct(h.shape, F32) for h in halves],
        scratch_shapes=[pltpu.SemaphoreType.DMA((n,)), pltpu.SemaphoreType.DMA((n,))],
        )(*halves)


def _small_allreduce(pack):
    m_per, ncol = pack.shape
    n_dev = 8

    def body(x_ref, sum_ref, all_ref, send_sems, recv_sems, local_sem):
        x, y, c, others = _place()
        me, sibling = (x, y, c), (x, y, 1 - c)

        def rows(px, py, pc):
            return all_ref.at[pl.ds((4 * px + 2 * py + pc) * m_per, m_per), :]

        def copy(k, block, to, src=None):
            return pltpu.make_async_remote_copy(
                src_ref=rows(*block) if src is None else src, dst_ref=rows(*block),
                send_sem=send_sems.at[k], recv_sem=recv_sems.at[k], device_id=to, device_id_type=MESH)

        mine = pltpu.make_async_copy(x_ref, rows(*me), local_sem)
        mine.start()
        first = [copy(0, me, sibling, src=x_ref)]
        first += [copy(1 + j, me, (*chip, c), src=x_ref) for j, chip in enumerate(others)]
        for cp in first:
            cp.start()
        passed = [copy(4 + j, (*chip, c), sibling) for j, chip in enumerate(others)]
        for j, chip in enumerate(others):
            copy(1 + j, (*chip, c), me).wait_recv()
            passed[j].start()
        copy(0, sibling, me).wait_recv()
        for j, chip in enumerate(others):
            copy(4 + j, (*chip, 1 - c), me).wait_recv()
        for cp in first + passed:
            cp.wait_send()
        mine.wait()
        acc = all_ref[0:m_per, :]
        for dvc in range(1, n_dev):
            acc = acc + all_ref[dvc * m_per:(dvc + 1) * m_per, :]
        sum_ref[...] = acc

    return pl.pallas_call(
        body, name="small_allreduce",
        in_specs=[pl.BlockSpec(memory_space=pltpu.VMEM)],
        out_specs=pl.BlockSpec(memory_space=pltpu.VMEM),
        out_shape=jax.ShapeDtypeStruct((m_per, ncol), F32),
        scratch_shapes=[pltpu.VMEM((n_dev * m_per, ncol), F32),
                        pltpu.SemaphoreType.DMA((7,)), pltpu.SemaphoreType.DMA((7,)), pltpu.SemaphoreType.DMA],
        )(pack)


def _adam_math(w, g, m, v):
    m = ADAM_B1 * m + (1.0 - ADAM_B1) * g
    v = ADAM_B2 * v + (1.0 - ADAM_B2) * (g * g)
    m_hat = m / (1.0 - ADAM_B1 ** ADAM_STEP)
    v_hat = v / (1.0 - ADAM_B2 ** ADAM_STEP)
    delta = -ADAM_LR * (m_hat / (jnp.sqrt(v_hat) + ADAM_EPS) + ADAM_WD * w)
    return delta, m, v


def _adamw(halves, c_idx, w, m, v, name):
    L = len(halves)
    r, C = halves[0][0].shape
    tr = _row_tile(r, C, 512 * 1024)
    nt = r // tr

    def body(c_ref, *refs):
        g_refs, (w_ref, m_ref, v_ref), (g_ref, d_ref, nm_ref, nv_ref) = refs[:2 * L], refs[2 * L:2 * L + 3], refs[2 * L + 3:]
        own = pl.program_id(1) == c_ref[0]
        g = None
        for l in range(L):
            cand = jnp.where(own, g_refs[2 * l][...], g_refs[2 * l + 1][...])
            g = cand if g is None else jnp.where(pl.program_id(0) == l, cand, g)
        g_ref[...] = g
        d_ref[...], nm_ref[...], nv_ref[...] = _adam_math(w_ref[...], g, m_ref[...], v_ref[...])

    def half(l, mine):
        def index(ll, h, i, c_ref):
            read = (h == c_ref[0]) if mine else (h != c_ref[0])
            return jnp.where(jnp.logical_and(ll == l, read), i, 0), 0
        return pl.BlockSpec((tr, C), index)

    full = pl.BlockSpec((None, tr, C), lambda ll, h, i, c_ref: (ll, h * nt + i, 0))
    shp = jax.ShapeDtypeStruct((L, 2 * r, C), F32)
    g_specs = [half(l, mine) for l in range(L) for mine in (True, False)]
    return pl.pallas_call(
        body, name=name,
        grid_spec=pltpu.PrefetchScalarGridSpec(
            num_scalar_prefetch=1, grid=(L, 2, nt),
            in_specs=g_specs + [full] * 3, out_specs=[full] * 4),
        out_shape=[shp] * 4,
        compiler_params=_params(("arbitrary", "arbitrary", "arbitrary")))(
            c_idx, *[a for pair in halves for a in pair], w, m, v)


def _small_update(gsum, logits_pack, w, m, v):
    def body(gs_ref, lg_ref, w_ref, m_ref, v_ref, g_ref, d_ref, nm_ref, nv_ref):
        g_ref[...] = gs_ref[...]
        l0, l1, l2 = lg_ref[0:1, :], lg_ref[1:2, :], lg_ref[2:3, :]
        mx = jnp.maximum(jnp.maximum(l0, l1), l2)
        e0, e1, e2 = jnp.exp(l0 - mx), jnp.exp(l1 - mx), jnp.exp(l2 - mx)
        tot = e0 + e1 + e2
        p0, p1, p2 = e0 / tot, e1 / tot, e2 / tot
        dlb = gs_ref[4:5, :]
        g_ref[4:5, :] = dlb * p0 * (1.0 - p0)
        g_ref[5:6, :] = -dlb * p0 * p1
        g_ref[6:7, :] = -dlb * p0 * p2
        d_ref[...], nm_ref[...], nv_ref[...] = _adam_math(w_ref[...], g_ref[...], m_ref[...], v_ref[...])

    full = pl.BlockSpec(memory_space=pltpu.VMEM)
    shp = jax.ShapeDtypeStruct(gsum.shape, F32)
    return pl.pallas_call(
        body, name="small_update", in_specs=[full] * 5, out_specs=[full] * 4, out_shape=[shp] * 4)(
            gsum, logits_pack, w, m, v)


def _pack_small(norm_mix, norm_ffn, lb3, out_norm, final_norm, extra=None):
    ncol = norm_mix.shape[1]
    on = jnp.pad(out_norm.reshape(1, -1), ((0, 0), (0, ncol - out_norm.size)))
    rows = [norm_mix, norm_ffn, lb3, on, final_norm.reshape(1, ncol)]
    if extra is not None:
        rows.append(extra)
    used = sum(r.shape[0] for r in rows)
    rows.append(jnp.zeros((SMALL_ROWS - used, ncol), F32))
    return jnp.concatenate(rows, axis=0)


WEIGHT_NAMES = ("hin", "hout", "qkv", "aout", "fin0", "fin1", "fdn0", "fdn1")
FIRST_WEIGHTS = ("hin",)
LATE_WEIGHTS_A = ("hout", "fin0", "fdn0")
LATE_WEIGHTS_B = ("qkv", "aout", "fin1", "fdn1")


def _split_weights(hgrn_w_in, hgrn_w_out, attn_w_qkv, attn_w_out, ffn_w_in, ffn_w_down):
    return {"hin": hgrn_w_in[0], "hout": hgrn_w_out[0], "qkv": attn_w_qkv[0], "aout": attn_w_out[0],
            "fin0": ffn_w_in[0], "fin1": ffn_w_in[1], "fdn0": ffn_w_down[0], "fdn1": ffn_w_down[1]}


def _halves(v):
    r, c = v.shape
    return v.reshape(2, r // 2, c)


def _full_weights(gathered):
    out = {}
    for k, g in gathered.items():
        _, _, r, c = g.shape
        if k in ("hin", "qkv", "fin0", "fin1"):
            out[k] = g.reshape(N_CHIPS, 2 * r, c)
        else:
            out[k] = g.reshape(N_CHIPS * 2 * r, c)
    return out


class _StepComm:
    def __init__(self, shards, c_idx, me_idx):
        self.shards, self.c_idx, self.me_idx = shards, c_idx, me_idx
        self.halves = {}
        self._stage = {}

    def gather_rider(self, names):
        return _GatherRider([_halves(self.shards[k].astype(BF16)) for k in names])

    def gathered(self, names, got):
        return _full_weights(dict(zip(names, got)))

    def first_weights(self):
        return self.gathered(FIRST_WEIGHTS, _run_rider(self.gather_rider(FIRST_WEIGHTS), "gather_first"))

    def pair_rider(self, grads, tag):
        names = list(grads)
        g4 = []
        for k in names:
            r, c = self.shards[k].shape
            g4.append(grads[k].reshape(N_CHIPS, 2, r // 2, c))
        self._stage[tag] = (names, g4)
        return _PairRider(g4)

    def pair_now(self, grads, tag):
        self.paired(tag, _run_rider(self.pair_rider(grads, tag), "grad_pair_exchange_" + tag))

    def paired(self, tag, got):
        names, g4 = self._stage[tag]
        self._stage[tag] = (names, [_pair_sum(g, s, self.c_idx) for g, s in zip(g4, got)])

    def exchange_rider(self, tag):
        return _ExchangeRider([s[1] for s in self._stage[tag][1]])

    def exchanged(self, tag, got):
        names, sums = self._stage.pop(tag)
        for k, s, g in zip(names, sums, got):
            self.halves[k] = _chip_sum(s[0], g, self.me_idx)

    def shared_halves(self):
        mine = [self.halves[k] for k in WEIGHT_NAMES]
        return dict(zip(WEIGHT_NAMES, zip(mine, _pair_share(mine))))


def kernel(x, norm_mix, norm_ffn, hgrn_w_in, hgrn_lb_logits, hgrn_out_norm, hgrn_w_out, attn_w_qkv, attn_w_out, ffn_w_in, ffn_w_down, final_norm, loss_target, m_norm_mix, m_norm_ffn, m_hgrn_w_in, m_hgrn_lb_logits, m_hgrn_out_norm, m_hgrn_w_out, m_attn_w_qkv, m_attn_w_out, m_ffn_w_in, m_ffn_w_down, m_final_norm, v_norm_mix, v_norm_ffn, v_hgrn_w_in, v_hgrn_lb_logits, v_hgrn_out_norm, v_hgrn_w_out, v_attn_w_qkv, v_attn_w_out, v_ffn_w_in, v_ffn_w_down, v_final_norm):
    S = x.shape[1]
    xi, yi, ci = lax.axis_index("x"), lax.axis_index("y"), lax.axis_index("c")
    c_idx = jnp.reshape(ci, (1,)).astype(jnp.int32)
    me_idx = jnp.reshape(2 * xi + yi, (1,)).astype(jnp.int32)

    w_own = _split_weights(hgrn_w_in, hgrn_w_out, attn_w_qkv, attn_w_out, ffn_w_in, ffn_w_down)

    comm = _StepComm(w_own, c_idx, me_idx)
    loss, dx, small = _local_step(
        x.reshape(S, D_MODEL), loss_target.reshape(S, D_MODEL), norm_mix, norm_ffn, hgrn_lb_logits,
        hgrn_out_norm, final_norm.reshape(1, D_MODEL), comm)

    halves = comm.shared_halves()
    updated = {}
    for tensor, layers, (wt, mt, vt) in (
            ("hgrn_w_in", ("hin",), (hgrn_w_in, m_hgrn_w_in, v_hgrn_w_in)),
            ("hgrn_w_out", ("hout",), (hgrn_w_out, m_hgrn_w_out, v_hgrn_w_out)),
            ("attn_w_qkv", ("qkv",), (attn_w_qkv, m_attn_w_qkv, v_attn_w_qkv)),
            ("attn_w_out", ("aout",), (attn_w_out, m_attn_w_out, v_attn_w_out)),
            ("ffn_w_in", ("fin0", "fin1"), (ffn_w_in, m_ffn_w_in, v_ffn_w_in)),
            ("ffn_w_down", ("fdn0", "fdn1"), (ffn_w_down, m_ffn_w_down, v_ffn_w_down))):
        updated[tensor] = _adamw([halves[k] for k in layers], c_idx, wt, mt, vt, "adamw_" + tensor)

    loss_row = jnp.pad(loss, ((0, 0), (0, D_MODEL - loss.shape[1])))
    lb3 = jnp.concatenate([small["lb"], jnp.zeros((2, D_MODEL), F32)], axis=0)
    on_grad = jnp.sum(small["out_norm"], axis=0, keepdims=True)
    pack = _pack_small(small["norm_mix"], small["norm_ffn"], lb3, on_grad, small["final_norm"], loss_row)
    gsum = _small_allreduce(pack)
    w_s = _pack_small(norm_mix, norm_ffn, hgrn_lb_logits, hgrn_out_norm, final_norm)
    m_s = _pack_small(m_norm_mix, m_norm_ffn, m_hgrn_lb_logits, m_hgrn_out_norm, m_final_norm)
    v_s = _pack_small(v_norm_mix, v_norm_ffn, v_hgrn_lb_logits, v_hgrn_out_norm, v_final_norm)
    lg_pack = jnp.pad(hgrn_lb_logits, ((0, 8 - hgrn_lb_logits.shape[0]), (0, 0)))
    sg, sd, sm, sv = _small_update(gsum, lg_pack, w_s, m_s, v_s)

    def unpack(p):
        return (p[0:2], p[2:4], p[4:7], p[7:8, :HEAD], p[8])

    def assemble(p, which):
        nmx, nff, lbl, onm, fnm = unpack(p)
        hin, hout, qkv, aout, fin, fdn = [updated[t][which] for t in
                                          ("hgrn_w_in", "hgrn_w_out", "attn_w_qkv", "attn_w_out", "ffn_w_in", "ffn_w_down")]
        return (nmx, nff, hin, lbl, onm, hout, qkv, aout, fin, fdn, fnm)

    total_loss = gsum[9, 0]
    return (total_loss, dx.reshape(1, S, D_MODEL), *assemble(sg, 0), *assemble(sd, 1), *assemble(sm, 2), *assemble(sv, 3))
```

```python
import functools

import jax
import jax.numpy as jnp
from jax import lax
from jax.experimental import pallas as pl
from jax.experimental.pallas import tpu as pltpu

F32 = jnp.float32
BF16 = jnp.bfloat16
MESH = pl.DeviceIdType.MESH

D_MODEL = 1024
HEAD = 128
HGRN_HEADS = 8
HGRN_CHUNK = 64
ATTN_GROUPS = ((128, 1), (512, 4), (2048, 16))
ATTN_SPAN = 128
HEADS_PER_GROUP = 4
GROUP_W = HEADS_PER_GROUP * HEAD
D_FF = 2816
NORM_EPS = 1e-6
ROPE_THETA = 10000.0
NEG = -1e30

ADAM_LR, ADAM_B1, ADAM_B2, ADAM_EPS, ADAM_WD, ADAM_STEP = 0.001, 0.9, 0.999, 1e-08, 0.01, 10

N_CHIPS = 4
VMEM_LIMIT = 56 * 1024 * 1024
SMALL_ROWS = 16


def _params(sem=None):
    return pltpu.CompilerParams(dimension_semantics=sem, vmem_limit_bytes=VMEM_LIMIT)


def _row_tile(rows, cols, budget_bytes=3 * 512 * 1024):
    best = 8
    for t in range(8, rows + 1, 8):
        if rows % t == 0 and t * cols * 4 <= budget_bytes:
            best = t
    assert rows % best == 0
    return best


def _grid_corner(i, j):
    return jnp.logical_and(pl.program_id(0) == i, pl.program_id(1) == j)


def _sigmoid(v):
    return 0.5 * jnp.tanh(0.5 * v) + 0.5


def _dot(a, b):
    return jnp.dot(a, b, preferred_element_type=F32)


def _dot_nt(a, b):
    return lax.dot_general(a, b, (((1,), (1,)), ((), ())), preferred_element_type=F32)


def _dot_tn(a, b):
    return lax.dot_general(a, b, (((0,), (0,)), ((), ())), preferred_element_type=F32)


def _dot_exact(ones, b):
    ones = ones.astype(BF16)
    hi = b.astype(BF16)
    rest = b - hi.astype(F32)
    mid = rest.astype(BF16)
    low = (rest - mid.astype(F32)).astype(BF16)
    return _dot(ones, hi) + _dot(ones, mid) + _dot(ones, low)


def _rstd(v):
    return lax.rsqrt(jnp.mean(v * v, axis=-1, keepdims=True) + NORM_EPS)


def _norm_mm(h, gain, w3, name, out_dtype=F32, tm=1024, rider=None):
    S, K = h.shape
    J, _, n = w3.shape
    gi = S // tm

    def body(h_ref, g_ref, w_ref, y_ref, u_ref):
        @pl.when(pl.program_id(1) == 0)
        def _():
            v = h_ref[...]
            u_ref[...] = (v * _rstd(v) * g_ref[...]).astype(BF16)

        y_ref[...] = _dot(u_ref[...], w_ref[pl.program_id(1)]).astype(y_ref.dtype)

    r_ops, r_in, r_out, r_shape, r_scr = _rider_args(rider)
    res = pl.pallas_call(
        _ride(rider, body, 3, 2, functools.partial(_grid_corner, 0, 0), functools.partial(_grid_corner, gi - 1, J - 1)),
        name=name, grid=(gi, J),
        in_specs=[pl.BlockSpec((tm, K), lambda i, j: (i, 0)),
                  pl.BlockSpec((1, K), lambda i, j: (0, 0)),
                  pl.BlockSpec((J, K, n), lambda i, j: (0, 0, 0))] + r_in,
        out_specs=[pl.BlockSpec((tm, n), lambda i, j: (i, j)), pl.BlockSpec((tm, K), lambda i, j: (i, 0))] + r_out,
        out_shape=[jax.ShapeDtypeStruct((S, J * n), out_dtype), jax.ShapeDtypeStruct((S, K), BF16)] + r_shape,
        scratch_shapes=r_scr,
        compiler_params=_params(("arbitrary", "arbitrary")))(h, gain, w3, *r_ops)
    return res[0], res[1], res[2:]


def _mm_res(h, a, w2, name, tm=512):
    S, N = h.shape
    K = a.shape[1]

    def body(h_ref, a_ref, w_ref, o_ref):
        o_ref[...] = h_ref[...] + _dot(a_ref[...], w_ref[...])

    return pl.pallas_call(
        body, name=name, grid=(S // tm,),
        in_specs=[pl.BlockSpec((tm, N), lambda i: (i, 0)),
                  pl.BlockSpec((tm, K), lambda i: (i, 0)),
                  pl.BlockSpec((K, N), lambda i: (0, 0))],
        out_specs=pl.BlockSpec((tm, N), lambda i: (i, 0)),
        out_shape=jax.ShapeDtypeStruct((S, N), F32),
        compiler_params=_params(("parallel",)))(h, a, w2)


def _swiglu(z_ref, F):
    g = z_ref[:, :F].astype(F32)
    return (g * _sigmoid(g) * z_ref[:, F:].astype(F32)).astype(BF16)


def _swiglu_mm_res(h, z, w2, name, tm=256):
    S, N = h.shape
    F = w2.shape[0]

    def body(h_ref, z_ref, w_ref, o_ref, a_ref):
        a = _swiglu(z_ref, F)
        a_ref[...] = a
        o_ref[...] = h_ref[...] + _dot(a, w_ref[...])

    return pl.pallas_call(
        body, name=name, grid=(S // tm,),
        in_specs=[pl.BlockSpec((tm, N), lambda i: (i, 0)),
                  pl.BlockSpec((tm, 2 * F), lambda i: (i, 0)),
                  pl.BlockSpec((F, N), lambda i: (0, 0))],
        out_specs=[pl.BlockSpec((tm, N), lambda i: (i, 0)), pl.BlockSpec((tm, F), lambda i: (i, 0))],
        out_shape=[jax.ShapeDtypeStruct((S, N), F32), jax.ShapeDtypeStruct((S, F), BF16)],
        compiler_params=_params(("parallel",)))(h, z, w2)


def _dy_specs(dy, J, n, tm):
    if dy.ndim == 3:
        return [pl.BlockSpec((None, tm, n), functools.partial(lambda i, j: (j, i, 0), j=j)) for j in range(J)]
    return [pl.BlockSpec((tm, n), functools.partial(lambda i, j: (i, j), j=j)) for j in range(J)]


def _acc_nt(dy_refs, w_ref):
    acc = None
    for j, r in enumerate(dy_refs):
        t = _dot_nt(r[...].astype(BF16), w_ref[j])
        acc = t if acc is None else acc + t
    return acc


def _mm_nt(dy, w3, name, out_dtype=F32, tm=512):
    J, K, n = w3.shape
    S = dy.shape[-2]

    def body(*refs):
        dy_refs, w_ref, o_ref = refs[:J], refs[J], refs[J + 1]
        o_ref[...] = _acc_nt(dy_refs, w_ref).astype(o_ref.dtype)

    return pl.pallas_call(
        body, name=name, grid=(S // tm,),
        in_specs=_dy_specs(dy, J, n, tm) + [pl.BlockSpec((J, K, n), lambda i: (0, 0, 0))],
        out_specs=pl.BlockSpec((tm, K), lambda i: (i, 0)),
        out_shape=jax.ShapeDtypeStruct((S, K), out_dtype),
        compiler_params=_params(("parallel",)))(*([dy] * J), w3)


def _mm_nt_normbwd(dy, w3, h, gain, dh, name, tm=512, rider=None):
    J, K, n = w3.shape
    S = h.shape[0]
    steps = S // tm

    def body(*refs):
        dy_refs, w_ref, h_ref, g_ref, dh_ref, o_ref, dg_ref = refs[:J], *refs[J:]
        du = _acc_nt(dy_refs, w_ref)
        v = h_ref[...]
        r = _rstd(v)
        xh = v * r
        dyg = du * g_ref[...]
        o_ref[...] = dh_ref[...] + r * (dyg - xh * jnp.mean(dyg * xh, axis=-1, keepdims=True))

        @pl.when(pl.program_id(0) == 0)
        def _():
            dg_ref[...] = jnp.zeros_like(dg_ref)

        dg_ref[...] += jnp.sum(du * xh, axis=0, keepdims=True)

    row = pl.BlockSpec((tm, K), lambda i: (i, 0))
    vec = pl.BlockSpec((1, K), lambda i: (0, 0))
    r_ops, r_in, r_out, r_shape, r_scr = _rider_args(rider)
    res = pl.pallas_call(
        _ride(rider, body, J + 4, 2, lambda: pl.program_id(0) == 0, lambda: pl.program_id(0) == steps - 1),
        name=name, grid=(steps,),
        in_specs=_dy_specs(dy, J, n, tm) + [pl.BlockSpec((J, K, n), lambda i: (0, 0, 0)), row, vec, row] + r_in,
        out_specs=[row, vec] + r_out,
        out_shape=[jax.ShapeDtypeStruct((S, K), F32), jax.ShapeDtypeStruct((1, K), F32)] + r_shape,
        scratch_shapes=r_scr,
        compiler_params=_params(("arbitrary",)))(*([dy] * J), w3, h, gain, dh, *r_ops)
    return res[0], res[1], res[2:]


def _mm_nt_swiglu_bwd(dh, w2, z, name, tm=256):
    F, N = w2.shape
    S = dh.shape[0]

    def body(dh_ref, w_ref, z_ref, o_ref):
        da = _dot_nt(dh_ref[...].astype(BF16), w_ref[...])
        g = z_ref[:, :F].astype(F32)
        u = z_ref[:, F:].astype(F32)
        sg = _sigmoid(g)
        o_ref[:, :F] = (da * u * (sg * (1.0 + g * (1.0 - sg)))).astype(BF16)
        o_ref[:, F:] = (da * (g * sg)).astype(BF16)

    return pl.pallas_call(
        body, name=name, grid=(S // tm,),
        in_specs=[pl.BlockSpec((tm, N), lambda i: (i, 0)),
                  pl.BlockSpec((F, N), lambda i: (0, 0)),
                  pl.BlockSpec((tm, 2 * F), lambda i: (i, 0))],
        out_specs=pl.BlockSpec((tm, 2 * F), lambda i: (i, 0)),
        out_shape=jax.ShapeDtypeStruct((S, 2 * F), BF16),
        compiler_params=_params(("parallel",)))(dh, w2, z)


def _mm_tn(x, dy, J, n, tn, name):
    tpn = n // tn
    ts = 1024
    S, K = x.shape
    if dy.ndim == 3:
        dy_spec = pl.BlockSpec((None, ts, tn), lambda c, s: (c // tpn, s, c % tpn))
    else:
        dy_spec = pl.BlockSpec((ts, tn), lambda c, s: (s, c))

    def body(x_ref, dy_ref, o_ref):
        @pl.when(pl.program_id(1) == 0)
        def _():
            o_ref[...] = jnp.zeros_like(o_ref)

        o_ref[...] += _dot_tn(x_ref[...], dy_ref[...].astype(BF16))

    return pl.pallas_call(
        body, name=name, grid=(J * tpn, S // ts),
        in_specs=[pl.BlockSpec((ts, K), lambda c, s: (s, 0)), dy_spec],
        out_specs=pl.BlockSpec((None, K, tn), lambda c, s: (c // tpn, 0, c % tpn)),
        out_shape=jax.ShapeDtypeStruct((J, K, n), F32),
        compiler_params=_params(("parallel", "arbitrary")))(x, dy)


def _loss_head(h, gain, target, tm=512):
    S, K = h.shape

    def body(h_ref, g_ref, t_ref, dh_ref, loss_ref, dg_ref):
        v = h_ref[...]
        r = _rstd(v)
        xh = v * r
        g = g_ref[...]
        dy = (xh * g - t_ref[...]) * (1.0 / K)
        dyg = dy * g
        dh_ref[...] = r * (dyg - xh * jnp.mean(dyg * xh, axis=-1, keepdims=True))

        @pl.when(pl.program_id(0) == 0)
        def _():
            loss_ref[...] = jnp.zeros_like(loss_ref)
            dg_ref[...] = jnp.zeros_like(dg_ref)

        part = jnp.sum(jnp.sum(dy * dy, axis=-1, keepdims=True), axis=0, keepdims=True) * (0.5 * K)
        lane = lax.broadcasted_iota(jnp.int32, loss_ref.shape, 1)
        loss_ref[...] += jnp.where(lane == 0, part, 0.0)
        dg_ref[...] += jnp.sum(dy * xh, axis=0, keepdims=True)

    row = pl.BlockSpec((tm, K), lambda i: (i, 0))
    vec = pl.BlockSpec((1, K), lambda i: (0, 0))
    return pl.pallas_call(
        body, name="loss_head", grid=(S // tm,),
        in_specs=[row, vec, row],
        out_specs=[row, pl.BlockSpec((1, HEAD), lambda i: (0, 0)), vec],
        out_shape=[jax.ShapeDtypeStruct((S, K), F32), jax.ShapeDtypeStruct((1, HEAD), F32),
                   jax.ShapeDtypeStruct((1, K), F32)],
        compiler_params=_params(("arbitrary",)))(h, gain, target)


def _lower_bound(lg_ref):
    l0, l1, l2 = lg_ref[0:1, :], lg_ref[1:2, :], lg_ref[2:3, :]
    mx = jnp.maximum(jnp.maximum(l0, l1), l2)
    e0, e1, e2 = jnp.exp(l0 - mx), jnp.exp(l1 - mx), jnp.exp(l2 - mx)
    return e0 / (e0 + e1 + e2)


def _chunks(v, ncb):
    C = HGRN_CHUNK
    return [v[c * C:(c + 1) * C] for c in range(ncb)]


def _rows(parts):
    return jnp.concatenate(parts, axis=0)


def _block_gates(qz, fz, lb, ncb):
    C = HGRN_CHUNK
    row = lax.broadcasted_iota(jnp.int32, (C, C), 0)
    col = lax.broadcasted_iota(jnp.int32, (C, C), 1)
    tri = (col <= row).astype(F32)
    first_half = lax.broadcasted_iota(jnp.int32, (C, HEAD), 0) < C // 2
    sig = _sigmoid(fz)
    fg = lb + (1.0 - lb) * sig
    key = 1.0 - fg
    lg = jnp.log(fg)
    lgs = _chunks(lg, ncb)
    b = _rows([_dot_exact(tri, v) for v in lgs])
    r_c = [jnp.sum(jnp.where(first_half, v, 0.0), axis=0, keepdims=True) for v in lgs]
    bl_c = [jnp.sum(v, axis=0, keepdims=True) for v in lgs]
    r = _rows([jnp.broadcast_to(v, (C, HEAD)) for v in r_c])
    e_br, e_rb = jnp.exp(b - r), jnp.exp(r - b)
    e_b = e_br * _rows([jnp.broadcast_to(jnp.exp(v), (C, HEAD)) for v in r_c])
    e_lb = e_rb * _rows([jnp.broadcast_to(jnp.exp(e - v), (C, HEAD)) for e, v in zip(bl_c, r_c)])
    sq = _sigmoid(qz)
    qy = qz * sq
    return sig, fg, key, (e_br, e_rb, e_b, e_lb), bl_c, sq, qy


def _hgrn_fwd(proj, logits, gain, tb=1024, rider=None):
    S = proj.shape[0]
    H, C = HGRN_HEADS, HGRN_CHUNK
    ncb = tb // C

    def body(q_ref, f_ref, i_ref, g_ref, lg_ref, gn_ref, o_ref, og_ref, st_ref, state):
        @pl.when(pl.program_id(1) == 0)
        def _():
            state[...] = jnp.zeros_like(state)

        lb = _lower_bound(lg_ref)
        causal = lax.broadcasted_iota(jnp.int32, (C, C), 1) <= lax.broadcasted_iota(jnp.int32, (C, C), 0)
        qz, fz, gz = q_ref[...], f_ref[...], g_ref[...]
        _, _, key, (e_br, e_rb, e_b, e_lb), bl_c, _, qy = _block_gates(qz, fz, lb, ncb)
        qs = _chunks((qy * e_br).astype(BF16), ncb)
        ks = _chunks((key * e_rb).astype(BF16), ncb)
        qb = _chunks((qy * e_b).astype(BF16), ncb)
        ke = _chunks((key * e_lb).astype(BF16), ncb)
        vb = _chunks(i_ref[...].astype(BF16), ncb)
        o_intra, upd = [], []
        for c in range(ncb):
            a = jnp.where(causal, _dot_nt(qs[c], ks[c]), 0.0).astype(BF16)
            o_intra.append(_dot(a, vb[c]))
            upd.append(_dot_tn(vb[c], ke[c]))
        st = state[...]
        for c in range(ncb):
            st_ref[c] = st
            st = st * jnp.exp(bl_c[c]) + upd[c]
        state[...] = st
        o = _rows([_dot_nt(qb[c], st_ref[c].astype(BF16)) + o_intra[c] for c in range(ncb)])
        o_ref[...] = o
        og_ref[...] = ((o * _rstd(o) * gn_ref[...]) * (gz * _sigmoid(gz))).astype(BF16)

    def part(p):
        return pl.BlockSpec((tb, HEAD), functools.partial(lambda h, i, p: (i, p * H + h), p=p))

    nb = S // tb
    r_ops, r_in, r_out, r_shape, r_scr = _rider_args(rider)
    res = pl.pallas_call(
        _ride(rider, body, 6, 3, functools.partial(_grid_corner, 0, 0), functools.partial(_grid_corner, H - 1, nb - 1)),
        name="hgrn_fwd", grid=(H, nb),
        in_specs=[part(0), part(1), part(2), part(3),
                  pl.BlockSpec((3, HEAD), lambda h, i: (0, h)),
                  pl.BlockSpec((1, HEAD), lambda h, i: (0, 0))] + r_in,
        out_specs=[pl.BlockSpec((tb, HEAD), lambda h, i: (i, h)),
                   pl.BlockSpec((tb, HEAD), lambda h, i: (i, h)),
                   pl.BlockSpec((None, ncb, HEAD, HEAD), lambda h, i: (h, i, 0, 0))] + r_out,
        out_shape=[jax.ShapeDtypeStruct((S, H * HEAD), F32),
                   jax.ShapeDtypeStruct((S, H * HEAD), BF16),
                   jax.ShapeDtypeStruct((H, S // C, HEAD, HEAD), F32)] + r_shape,
        scratch_shapes=[pltpu.VMEM((HEAD, HEAD), F32)] + r_scr,
        compiler_params=_params(("arbitrary", "arbitrary")))(proj, proj, proj, proj, logits, gain, *r_ops)
    return res[:3], res[3:]


def _hgrn_bwd(proj, logits, gain, o, states, dog, tb=1024, rider=None):
    S = proj.shape[0]
    H, C = HGRN_HEADS, HGRN_CHUNK
    ncb = tb // C
    nb = S // tb

    def body(q_ref, f_ref, i_ref, g_ref, lg_ref, gn_ref, o_ref, st_ref, dog_ref,
             dp_ref, dlb_ref, dgn_ref, dstate, dst_scr):
        @pl.when(pl.program_id(1) == 0)
        def _():
            dstate[...] = jnp.zeros_like(dstate)
            dlb_ref[...] = jnp.zeros_like(dlb_ref)
            dgn_ref[...] = jnp.zeros_like(dgn_ref)

        lb = _lower_bound(lg_ref)
        oml = 1.0 - lb
        gn = gn_ref[...]
        row = lax.broadcasted_iota(jnp.int32, (C, C), 0)
        col = lax.broadcasted_iota(jnp.int32, (C, C), 1)
        causal = col <= row
        tri_up = (col >= row).astype(F32)
        qz, fz, gz = q_ref[...], f_ref[...], g_ref[...]
        sig, fg, key, (e_br, e_rb, e_b, e_lb), bl_c, sq, qy = _block_gates(qz, fz, lb, ncb)
        qs_v, ks_v = (qy * e_br).astype(BF16), (key * e_rb).astype(BF16)
        qb_v, ke_v = (qy * e_b).astype(BF16), (key * e_lb).astype(BF16)
        qs, ks, qb, ke = _chunks(qs_v, ncb), _chunks(ks_v, ncb), _chunks(qb_v, ncb), _chunks(ke_v, ncb)
        vb = _chunks(i_ref[...].astype(BF16), ncb)
        ov = o_ref[...]
        rs = _rstd(ov)
        xh = ov * rs
        sg = _sigmoid(gz)
        dog_v = dog_ref[...]
        dgz = dog_v * (xh * gn) * (sg * (1.0 + gz * (1.0 - sg)))
        don = dog_v * (gz * sg)
        dgn_ref[...] += jnp.sum(don * xh, axis=0, keepdims=True)
        dyg = don * gn
        do = rs * (dyg - xh * jnp.mean(dyg * xh, axis=-1, keepdims=True))
        dob = _chunks(do.astype(BF16), ncb)
        dv_in, dqs, dks, wst = [], [], [], []
        for c in range(ncb):
            a = jnp.where(causal, _dot_nt(qs[c], ks[c]), 0.0).astype(BF16)
            da = jnp.where(causal, _dot_nt(dob[c], vb[c]), 0.0).astype(BF16)
            dv_in.append(_dot_tn(a, dob[c]))
            dqs.append(_dot(da, ks[c]))
            dks.append(_dot_tn(da, qs[c]))
            wst.append(_dot_tn(dob[c], qb[c]))
        e_l = [jnp.exp(v) for v in bl_c]
        dst = dstate[...]
        for c in reversed(range(ncb)):
            dst_scr[c] = dst
            dst = wst[c] + dst * e_l[c]
        dstate[...] = dst
        dv, dqb, dke, dbl_st = [], [], [], []
        for c in range(ncb):
            dst1 = dst_scr[c]
            st0 = st_ref[c]
            dst1b = dst1.astype(BF16)
            dv.append(dv_in[c] + _dot_nt(ke[c], dst1b))
            dqb.append(_dot(dob[c], st0.astype(BF16)))
            dke.append(_dot(vb[c], dst1b))
            dbl_st.append(jnp.sum(dst1 * st0, axis=0, keepdims=True) * e_l[c])
        dqs, dks, dqb, dke, dv = _rows(dqs), _rows(dks), _rows(dqb), _rows(dke), _rows(dv)
        dke_ke = dke * ke_v.astype(F32)
        db = dqs * qs_v.astype(F32) - dks * ks_v.astype(F32) + dqb * qb_v.astype(F32) - dke_ke
        dlg = []
        for c, (db_c, kk_c) in enumerate(zip(_chunks(db, ncb), _chunks(dke_ke, ncb))):
            dbl = jnp.sum(kk_c, axis=0, keepdims=True) + dbl_st[c]
            dlg.append(_dot_exact(tri_up, db_c) + dbl)
        dlg = _rows(dlg)
        dkey = dks * e_rb + dke * e_lb
        dqy = dqs * e_br + dqb * e_b
        dfg = dlg / fg - dkey
        dlb_ref[...] += jnp.sum(dfg * (1.0 - sig), axis=0, keepdims=True)
        dp_ref[0] = (dqy * (sq * (1.0 + qz * (1.0 - sq)))).astype(BF16)
        dp_ref[1] = (dfg * oml * sig * (1.0 - sig)).astype(BF16)
        dp_ref[2] = dv.astype(BF16)
        dp_ref[3] = dgz.astype(BF16)

    def part(p):
        return pl.BlockSpec((tb, HEAD), functools.partial(lambda h, i, p: (nb - 1 - i, p * H + h), p=p))

    blk = pl.BlockSpec((tb, HEAD), lambda h, i: (nb - 1 - i, h))
    acc = pl.BlockSpec((None, 1, HEAD), lambda h, i: (h, 0, 0))
    r_ops, r_in, r_out, r_shape, r_scr = _rider_args(rider)
    res = pl.pallas_call(
        _ride(rider, body, 9, 3, functools.partial(_grid_corner, 0, 0), functools.partial(_grid_corner, H - 1, nb - 1)),
        name="hgrn_bwd", grid=(H, nb),
        in_specs=[part(0), part(1), part(2), part(3),
                  pl.BlockSpec((3, HEAD), lambda h, i: (0, h)),
                  pl.BlockSpec((1, HEAD), lambda h, i: (0, 0)),
                  blk,
                  pl.BlockSpec((None, ncb, HEAD, HEAD), lambda h, i: (h, nb - 1 - i, 0, 0)),
                  blk] + r_in,
        out_specs=[pl.BlockSpec((4, tb, HEAD), lambda h, i: (0, nb - 1 - i, h)), acc, acc] + r_out,
        out_shape=[jax.ShapeDtypeStruct((4, S, H * HEAD), BF16),
                   jax.ShapeDtypeStruct((H, 1, HEAD), F32),
                   jax.ShapeDtypeStruct((H, 1, HEAD), F32)] + r_shape,
        scratch_shapes=[pltpu.VMEM((HEAD, HEAD), F32), pltpu.VMEM((ncb, HEAD, HEAD), F32)] + r_scr,
        compiler_params=_params(("arbitrary", "arbitrary")))(
            proj, proj, proj, proj, logits, gain, o, states, dog, *r_ops)
    return res[:3], res[3:]


def _rope(v, cos, sin):
    return v * cos + pltpu.roll(v, HEAD // 2, 1) * sin


def _lane_pick(tile, hh):
    lane = lax.broadcasted_iota(jnp.int32, tile.shape, 1)
    return jnp.sum(jnp.where(lane == hh, tile, 0.0), axis=-1, keepdims=True)


def _lane_place(cols):
    rows = cols[0].shape[0]
    lane = lax.broadcasted_iota(jnp.int32, (rows, HEAD), 1)
    tile = jnp.zeros((rows, HEAD), F32)
    for hh, v in enumerate(cols):
        tile = jnp.where(lane == hh, v, tile)
    return tile


def _band_masks():
    qi = lax.broadcasted_iota(jnp.int32, (ATTN_SPAN, ATTN_SPAN), 0)
    kj = lax.broadcasted_iota(jnp.int32, (ATTN_SPAN, ATTN_SPAN), 1)
    return kj <= qi, kj >= qi


ATTN_TILE_BLOCKS = 4


def _attn_fwd(a):
    d, L, _ = a.shape
    B, W = ATTN_TILE_BLOCKS, ATTN_SPAN
    T = B * W
    assert L % T == 0
    steps = L // T
    scale = HEAD ** -0.5

    def body(q_ref, kc_ref, kp_ref, vc_ref, vp_ref, o_ref, lse_ref):
        n = pl.program_id(1)
        mask_c, mask_p0 = _band_masks()
        first = jnp.logical_and(mask_p0, n > 0)
        units = [(b, hh) for b in range(B) for hh in range(HEADS_PER_GROUP)]
        rows = [slice(b * W, (b + 1) * W) for b in range(B)]
        cols = [slice(hh * HEAD, (hh + 1) * HEAD) for hh in range(HEADS_PER_GROUP)]

        def prev_keys(ref, tile, b, hh):
            return ref[:, cols[hh]] if b == 0 else tile[rows[b - 1], cols[hh]]

        s_c = [jnp.where(mask_c, _dot_nt(q_ref[rows[b], cols[hh]], kc_ref[rows[b], cols[hh]]) * scale, NEG) for b, hh in units]
        s_p = [jnp.where(first if b == 0 else mask_p0,
                         _dot_nt(q_ref[rows[b], cols[hh]], prev_keys(kp_ref, kc_ref, b, hh)) * scale, NEG) for b, hh in units]
        m = [jnp.maximum(jnp.max(x, axis=-1, keepdims=True), jnp.max(y, axis=-1, keepdims=True)) for x, y in zip(s_c, s_p)]
        p_c = [jnp.exp(x - mm) for x, mm in zip(s_c, m)]
        p_p = [jnp.exp(y - mm) for y, mm in zip(s_p, m)]
        l = [jnp.sum(x, axis=-1, keepdims=True) + jnp.sum(y, axis=-1, keepdims=True) for x, y in zip(p_c, p_p)]
        acc = [_dot(p_c[i].astype(BF16), vc_ref[rows[b], cols[hh]]) + _dot(p_p[i].astype(BF16), prev_keys(vp_ref, vc_ref, b, hh))
               for i, (b, hh) in enumerate(units)]
        for i, (b, hh) in enumerate(units):
            o_ref[rows[b], cols[hh]] = acc[i] / l[i]
        for b in range(B):
            lse_ref[rows[b], :] = _lane_place([m[i] + jnp.log(l[i]) for i, (bb, _) in enumerate(units) if bb == b])

    def cur(part):
        return pl.BlockSpec((None, T, GROUP_W), functools.partial(lambda r, n, p: (r, n, p), p=part))

    def prev(part):
        return pl.BlockSpec((None, W, GROUP_W), functools.partial(lambda r, n, p: (r, jnp.maximum(n * B - 1, 0), p), p=part))

    return pl.pallas_call(
        body, name=f"attn_fwd_d{d}", grid=(d, steps),
        in_specs=[cur(0), cur(1), prev(1), cur(2), prev(2)],
        out_specs=[pl.BlockSpec((None, T, GROUP_W), lambda r, n: (r, n, 0)), pl.BlockSpec((None, T, HEAD), lambda r, n: (r, n, 0))],
        out_shape=[jax.ShapeDtypeStruct((d, L, GROUP_W), F32), jax.ShapeDtypeStruct((d, L, HEAD), F32)],
        compiler_params=_params(("parallel", "arbitrary")))(a, a, a, a, a)


def _attn_bwd(a, do, lse, dd):
    d, L, _ = a.shape
    B, W = ATTN_TILE_BLOCKS, ATTN_SPAN
    T = B * W
    assert L % T == 0
    steps = L // T
    scale = HEAD ** -0.5

    def body(qc_ref, qn_ref, kp_ref, kc_ref, vp_ref, vc_ref, doc_ref, don_ref, lc_ref, ln_ref, ddc_ref, ddn_ref, da_ref):
        n = pl.program_id(1)
        mask_c, mask_p0 = _band_masks()
        first = jnp.logical_and(mask_p0, n > 0)
        last = jnp.logical_and(mask_p0, n < steps - 1)
        H4 = range(HEADS_PER_GROUP)
        units = [(b, hh) for b in range(B) for hh in H4]
        rows = [slice(b * W, (b + 1) * W) for b in range(B)]
        cols = [slice(hh * HEAD, (hh + 1) * HEAD) for hh in H4]
        q = {u: qc_ref[rows[u[0]], cols[u[1]]] for u in units}
        k = {u: kc_ref[rows[u[0]], cols[u[1]]] for u in units}
        v = {u: vc_ref[rows[u[0]], cols[u[1]]] for u in units}
        g_o = {u: doc_ref[rows[u[0]], cols[u[1]]] for u in units}
        kb = {(b, hh): kp_ref[:, cols[hh]] if b == 0 else k[(b - 1, hh)] for b, hh in units}
        vb = {(b, hh): vp_ref[:, cols[hh]] if b == 0 else v[(b - 1, hh)] for b, hh in units}
        lse_t = {(b, hh): _lane_pick(lc_ref[rows[b], :], hh) for b, hh in units}
        dd_t = {(b, hh): _lane_pick(ddc_ref[rows[b], :], hh) for b, hh in units}
        p_c = {u: jnp.where(mask_c, jnp.exp(_dot_nt(q[u], k[u]) * scale - lse_t[u]), 0.0) for u in units}
        p_p = {u: jnp.where(first if u[0] == 0 else mask_p0, jnp.exp(_dot_nt(q[u], kb[u]) * scale - lse_t[u]), 0.0) for u in units}
        ds_c = {u: (p_c[u] * (_dot_nt(g_o[u], v[u]) + dd_t[u])).astype(BF16) for u in units}
        ds_p = {u: (p_p[u] * (_dot_nt(g_o[u], vb[u]) + dd_t[u])).astype(BF16) for u in units}
        qn = [qn_ref[:, c] for c in cols]
        g_n = [don_ref[:, c] for c in cols]
        p_n = [jnp.where(last, jnp.exp(_dot_nt(qn[hh], k[(B - 1, hh)]) * scale - _lane_pick(ln_ref[...], hh)), 0.0) for hh in H4]
        ds_n = [(p_n[hh] * (_dot_nt(g_n[hh], v[(B - 1, hh)]) + _lane_pick(ddn_ref[...], hh))).astype(BF16) for hh in H4]
        dq = {u: (_dot(ds_c[u], k[u]) + _dot(ds_p[u], kb[u])) * scale for u in units}
        dk, dv = {}, {}
        for b, hh in units:
            if b < B - 1:
                nxt = (b + 1, hh)
                dk[(b, hh)] = (_dot_tn(ds_c[(b, hh)], q[(b, hh)]) + _dot_tn(ds_p[nxt], q[nxt])) * scale
                dv[(b, hh)] = _dot_tn(p_c[(b, hh)].astype(BF16), g_o[(b, hh)]) + _dot_tn(p_p[nxt].astype(BF16), g_o[nxt])
            else:
                dk[(b, hh)] = (_dot_tn(ds_c[(b, hh)], q[(b, hh)]) + _dot_tn(ds_n[hh], qn[hh])) * scale
                dv[(b, hh)] = _dot_tn(p_c[(b, hh)].astype(BF16), g_o[(b, hh)]) + _dot_tn(p_n[hh].astype(BF16), g_n[hh])
        for b, hh in units:
            da_ref[rows[b], cols[hh]] = dq[(b, hh)].astype(BF16)
            da_ref[rows[b], GROUP_W + hh * HEAD:GROUP_W + (hh + 1) * HEAD] = dk[(b, hh)].astype(BF16)
            da_ref[rows[b], 2 * GROUP_W + hh * HEAD:2 * GROUP_W + (hh + 1) * HEAD] = dv[(b, hh)].astype(BF16)

    nb = L // W

    def cur(width, part):
        return pl.BlockSpec((None, T, width), functools.partial(lambda r, n, p: (r, n, p), p=part))

    def prev(width, part):
        return pl.BlockSpec((None, W, width), functools.partial(lambda r, n, p: (r, jnp.maximum(n * B - 1, 0), p), p=part))

    def nxt(width, part):
        return pl.BlockSpec((None, W, width), functools.partial(lambda r, n, p: (r, jnp.minimum(n * B + B, nb - 1), p), p=part))

    g = GROUP_W
    return pl.pallas_call(
        body, name=f"attn_bwd_d{d}", grid=(d, steps),
        in_specs=[cur(g, 0), nxt(g, 0), prev(g, 1), cur(g, 1), prev(g, 2), cur(g, 2),
                  cur(g, 0), nxt(g, 0), cur(HEAD, 0), nxt(HEAD, 0), cur(HEAD, 0), nxt(HEAD, 0)],
        out_specs=pl.BlockSpec((None, T, 3 * g), lambda r, n: (r, n, 0)),
        out_shape=jax.ShapeDtypeStruct((d, L, 3 * g), BF16),
        compiler_params=_params(("parallel", "arbitrary")))(
            a, a, a, a, a, a, do, do, lse, lse, dd, dd)


def _softmax3(ls):
    mx = jnp.maximum(jnp.maximum(ls[0], ls[1]), ls[2])
    es = [jnp.exp(v - mx) for v in ls]
    tot = es[0] + es[1] + es[2]
    return [e / tot for e in es]


HEAD_COLS = [slice(hh * HEAD, (hh + 1) * HEAD) for hh in range(HEADS_PER_GROUP)]


def _group_spec(d, tm):
    return pl.BlockSpec((d, tm // d, GROUP_W), lambda i: (0, i, 0))


def _gather_heads(ref, scr, d, tm):
    if d == 1:
        return [ref[0, :, cols].astype(F32) for cols in HEAD_COLS]
    for hh, cols in enumerate(HEAD_COLS):
        for r in range(d):
            scr.at[hh][pl.ds(r, tm // d, stride=d), :] = ref[r, :, cols].astype(F32)
    return [scr[hh] for hh in range(HEADS_PER_GROUP)]


def _tile_spec(d, tm):
    return pl.BlockSpec((d, tm // d, HEAD), lambda i: (0, i, 0))


def _gather_tile(ref, scr, d, tm):
    if d == 1:
        return ref[0]
    for r in range(d):
        scr[pl.ds(r, tm // d, stride=d), :] = ref[r]
    return scr[...]


def _scatter_tile(val, scr, ref, d, tm):
    if d == 1:
        ref[0] = val
        return
    scr[...] = val
    for r in range(d):
        ref[r] = scr[pl.ds(r, tm // d, stride=d), :]


def _scatter_heads(vals, scr, ref, d, tm):
    if d == 1:
        for cols, v in zip(HEAD_COLS, vals):
            ref[0, :, cols] = v.astype(ref.dtype)
        return
    for hh, v in enumerate(vals):
        scr[hh] = v
    for hh, cols in enumerate(HEAD_COLS):
        for r in range(d):
            ref[r, :, cols] = scr.at[hh][pl.ds(r, tm // d, stride=d), :].astype(ref.dtype)


def _qkv_dilated(h, gain, wg, cos, sin, d, tm=512):
    S, K = h.shape

    def body(h_ref, g_ref, w_ref, cos_ref, sin_ref, out_ref, u_ref, y_scr):
        p = pl.program_id(1)

        @pl.when(p == 0)
        def _():
            v = h_ref[...]
            u_ref[...] = (v * _rstd(v) * g_ref[...]).astype(BF16)

        y = _dot(u_ref[...], w_ref[...])
        heads = [slice(hh * HEAD, (hh + 1) * HEAD) for hh in range(HEADS_PER_GROUP)]
        for hh, cols in enumerate(heads):
            y_scr[hh] = y[:, cols]

        @pl.when(p < 2)
        def _():
            for r in range(d):
                rows = slice(None) if d == 1 else pl.ds(r, tm // d, stride=d)
                cr, sr = cos_ref[rows, :], sin_ref[rows, :]
                for hh, cols in enumerate(heads):
                    out_ref[r, :, cols] = _rope(y_scr.at[hh][rows, :], cr, sr).astype(BF16)

        @pl.when(p == 2)
        def _():
            for r in range(d):
                rows = slice(None) if d == 1 else pl.ds(r, tm // d, stride=d)
                for hh, cols in enumerate(heads):
                    out_ref[r, :, cols] = y_scr.at[hh][rows, :].astype(BF16)

    tab = pl.BlockSpec((tm, HEAD), lambda i, p: (i, 0))
    return pl.pallas_call(
        body, name=f"attn_qkv_d{d}", grid=(S // tm, 3),
        in_specs=[pl.BlockSpec((tm, K), lambda i, p: (i, 0)),
                  pl.BlockSpec((1, K), lambda i, p: (0, 0)),
                  pl.BlockSpec((K, GROUP_W), lambda i, p: (0, p)), tab, tab],
        out_specs=[pl.BlockSpec((d, tm // d, GROUP_W), lambda i, p: (0, i, p)), pl.BlockSpec((tm, K), lambda i, p: (i, 0))],
        out_shape=[jax.ShapeDtypeStruct((d, S // d, 3 * GROUP_W), BF16), jax.ShapeDtypeStruct((S, K), BF16)],
        scratch_shapes=[pltpu.VMEM((HEADS_PER_GROUP, tm, HEAD), F32)],
        compiler_params=_params(("parallel", "arbitrary")))(h, gain, wg, cos, sin)


def _undilate_group(da, dqkv, cos, sin, g, tm=512):
    d, L, _ = da.shape
    S = d * L
    G = len(ATTN_GROUPS)

    def body(*refs):
        da_ref, cos_ref, sin_ref, out_ref, scr = refs[0], refs[1], refs[2], refs[-2], refs[-1]
        p = pl.program_id(1)
        heads = [slice(hh * HEAD, (hh + 1) * HEAD) for hh in range(HEADS_PER_GROUP)]
        for hh, cols in enumerate(heads):
            if d == 1:
                scr[hh] = da_ref[0, :, cols].astype(F32)
            else:
                for r in range(d):
                    scr.at[hh][pl.ds(r, tm // d, stride=d), :] = da_ref[r, :, cols].astype(F32)

        @pl.when(p < 2)
        def _():
            cr, sr = cos_ref[...], -sin_ref[...]
            for hh, cols in enumerate(heads):
                out_ref[:, cols] = _rope(scr[hh], cr, sr).astype(BF16)

        @pl.when(p == 2)
        def _():
            for hh, cols in enumerate(heads):
                out_ref[:, cols] = scr[hh].astype(BF16)

    tab = pl.BlockSpec((tm, HEAD), lambda i, p: (i, 0))
    operands = (da, cos, sin) if dqkv is None else (da, cos, sin, dqkv)
    return pl.pallas_call(
        body, name=f"attn_undilate_d{d}", grid=(S // tm, 3),
        in_specs=[pl.BlockSpec((d, tm // d, GROUP_W), lambda i, p: (0, i, p)), tab, tab] + ([] if dqkv is None else [ANY]),
        out_specs=pl.BlockSpec((tm, GROUP_W), lambda i, p: (i, p * G + g)),
        out_shape=jax.ShapeDtypeStruct((S, 3 * G * GROUP_W), BF16),
        input_output_aliases={} if dqkv is None else {3: 0},
        scratch_shapes=[pltpu.VMEM((HEADS_PER_GROUP, tm, HEAD), F32)],
        compiler_params=_params(("parallel", "arbitrary")))(*operands)


def _attn_merge(os_, lses, tm=512):
    G = len(os_)
    S = os_[0].shape[0] * os_[0].shape[1]

    def body(*refs):
        o_refs, l_refs, out_ref = refs[:G], refs[G:2 * G], refs[2 * G]
        scr = refs[2 * G + 1:]
        o = [_gather_heads(o_refs[g], scr[g], d, tm) for g, (_, d) in enumerate(ATTN_GROUPS)]
        l = [_gather_tile(l_refs[g], scr[G + g].at[0], d, tm) for g, (_, d) in enumerate(ATTN_GROUPS)]
        for hh in range(HEADS_PER_GROUP):
            al = _softmax3([_lane_pick(l[g], hh) for g in range(G)])
            for g in range(G):
                out_ref[:, g * GROUP_W + hh * HEAD:g * GROUP_W + (hh + 1) * HEAD] = (o[g][hh] * al[g]).astype(BF16)

    specs = [_group_spec(d, tm) for _, d in ATTN_GROUPS]
    return pl.pallas_call(
        body, name="attn_merge", grid=(S // tm,),
        in_specs=specs + [_tile_spec(d, tm) for _, d in ATTN_GROUPS],
        out_specs=pl.BlockSpec((tm, G * GROUP_W), lambda i: (i, 0)),
        out_shape=jax.ShapeDtypeStruct((S, G * GROUP_W), BF16),
        scratch_shapes=[pltpu.VMEM((HEADS_PER_GROUP, tm, HEAD), F32)] * (2 * G),
        compiler_params=_params(("parallel",)))(*os_, *lses)


def _attn_merge_bwd(os_, lses, doa, tm=512):
    G = len(os_)
    S = doa.shape[0]

    def body(*refs):
        o_refs, l_refs, doa_ref = refs[:G], refs[G:2 * G], refs[2 * G]
        do_refs, dd_refs = refs[2 * G + 1:3 * G + 1], refs[3 * G + 1:4 * G + 1]
        scr = refs[4 * G + 1:]
        o = [_gather_heads(o_refs[g], scr[g], d, tm) for g, (_, d) in enumerate(ATTN_GROUPS)]
        l = [_gather_tile(l_refs[g], scr[G + g].at[0], d, tm) for g, (_, d) in enumerate(ATTN_GROUPS)]
        do = [[None] * HEADS_PER_GROUP for _ in range(G)]
        dd = [[None] * HEADS_PER_GROUP for _ in range(G)]
        for hh in range(HEADS_PER_GROUP):
            al = _softmax3([_lane_pick(l[g], hh) for g in range(G)])
            mix = None
            for g in range(G):
                dg = doa_ref[:, g * GROUP_W + hh * HEAD:g * GROUP_W + (hh + 1) * HEAD]
                do[g][hh] = dg * al[g]
                t = al[g] * jnp.sum(dg * o[g][hh], axis=-1, keepdims=True)
                mix = t if mix is None else mix + t
            for g in range(G):
                dd[g][hh] = -al[g] * mix
        for g, (_, d) in enumerate(ATTN_GROUPS):
            _scatter_heads(do[g], scr[2 * G + g], do_refs[g], d, tm)
            _scatter_tile(_lane_place(dd[g]), scr[3 * G + g].at[0], dd_refs[g], d, tm)

    specs = [_group_spec(d, tm) for _, d in ATTN_GROUPS]
    tiles = [_tile_spec(d, tm) for _, d in ATTN_GROUPS]
    do_shapes = [jax.ShapeDtypeStruct((d, S // d, GROUP_W), BF16) for _, d in ATTN_GROUPS]
    dd_shapes = [jax.ShapeDtypeStruct((d, S // d, HEAD), F32) for _, d in ATTN_GROUPS]
    return pl.pallas_call(
        body, name="attn_merge_bwd", grid=(S // tm,),
        in_specs=specs + tiles + [pl.BlockSpec((tm, G * GROUP_W), lambda i: (i, 0))],
        out_specs=specs + tiles,
        out_shape=do_shapes + dd_shapes,
        scratch_shapes=[pltpu.VMEM((HEADS_PER_GROUP, tm, HEAD), F32)] * (4 * G),
        compiler_params=_params(("parallel",)))(*os_, *lses, doa)


def _rope_tables(S):
    inv_freq = 1.0 / (ROPE_THETA ** (jnp.arange(0, HEAD, 2, dtype=F32) / HEAD))
    ang = jnp.arange(S, dtype=F32)[:, None] * inv_freq[None, :]
    cos, sin = jnp.cos(ang), jnp.sin(ang)
    return jnp.concatenate([cos, cos], axis=-1), jnp.concatenate([-sin, sin], axis=-1)


def _local_step(x, target, norm_mix, norm_ffn, lb_logits, out_gain, final_norm, comm):
    S = x.shape[0]
    nm0, nm1 = norm_mix[0:1], norm_mix[1:2]
    nf0, nf1 = norm_ffn[0:1], norm_ffn[1:2]
    w = comm.first_weights()

    proj, u0, got = _norm_mm(x, nm0, w["hin"], "hgrn_in", rider=comm.gather_rider(LATE_WEIGHTS_A))
    w.update(comm.gathered(LATE_WEIGHTS_A, got))
    (o, og, states), got = _hgrn_fwd(proj, lb_logits, out_gain, rider=comm.gather_rider(LATE_WEIGHTS_B))
    w.update(comm.gathered(LATE_WEIGHTS_B, got))
    fin_tn = w["fin0"].shape[2]
    h1 = _mm_res(x, og, w["hout"], "hgrn_out")
    z0, u1, _ = _norm_mm(h1, nf0, w["fin0"], "ffn0_in", out_dtype=BF16)
    h2, act0 = _swiglu_mm_res(h1, z0, w["fdn0"], "ffn0_down")
    cos, sin = _rope_tables(S)
    G = len(ATTN_GROUPS)
    w_groups = w["qkv"].transpose(1, 0, 2).reshape(D_MODEL, 3, G, GROUP_W)
    a_g, u2 = zip(*[_qkv_dilated(h2, nm1, w_groups[:, :, gi, :].reshape(D_MODEL, 3 * GROUP_W), cos, sin, d)
                    for gi, (_, d) in enumerate(ATTN_GROUPS)])
    o_g, lse_g = zip(*[_attn_fwd(a) for a in a_g])
    oa = _attn_merge(o_g, lse_g)
    h3 = _mm_res(h2, oa, w["aout"], "attn_out")
    z1, u3, _ = _norm_mm(h3, nf1, w["fin1"], "ffn1_in", out_dtype=BF16)
    h4, act1 = _swiglu_mm_res(h3, z1, w["fdn1"], "ffn1_down")
    dh4, loss, d_final = _loss_head(h4, final_norm, target)

    grads, small = {}, {"final_norm": d_final}

    def ffn_bwd(dh, h_in, u_in, z, act, gain, w_in, w_dn, tag, ride=None):
        dz = _mm_nt_swiglu_bwd(dh, w_dn, z, tag + "_down_dx")
        g_dn = _mm_tn(act, dh, 1, D_MODEL, D_MODEL, tag + "_down_dw")[0]
        g_in = _mm_tn(u_in, dz, N_CHIPS, fin_tn, fin_tn, tag + "_in_dw")
        rider = None if ride is None else ride(g_in, g_dn)
        dh_in, dgain, got = _mm_nt_normbwd(dz, w_in, h_in, gain, dh, tag + "_in_dx", rider=rider)
        return dh_in, dgain, g_in, g_dn, got

    dh3, d_nf1, grads["fin1"], grads["fdn1"], _ = ffn_bwd(dh4, h3, u3, z1, act1, nf1, w["fin1"], w["fdn1"], "ffn1")
    doa = _mm_nt(dh3, w["aout"][None], "attn_out_dx")
    grads["aout"] = _mm_tn(oa, dh3, 1, D_MODEL, D_MODEL, "attn_out_dw")[0]
    merged = _attn_merge_bwd(o_g, lse_g, doa)
    G = len(ATTN_GROUPS)
    das = [_attn_bwd(a_g[gi], merged[gi], lse_g[gi], merged[G + gi]) for gi in range(G)]
    dqkv = None
    for gi in range(G):
        dqkv = _undilate_group(das[gi], dqkv, cos, sin, gi)
    n_qkv = w["qkv"].shape[2]
    grads["qkv"] = _mm_tn(u2[0], dqkv, N_CHIPS, n_qkv, n_qkv, "attn_qkv_dw")
    dh2, d_nm1, _ = _mm_nt_normbwd(dqkv, w["qkv"], h2, nm1, dh3, "attn_qkv_dx")

    def ride_early(g_in, g_dn):
        return comm.pair_rider({**grads, "fin0": g_in, "fdn0": g_dn}, "early")

    dh1, d_nf0, _, _, got = ffn_bwd(dh2, h1, u1, z0, act0, nf0, w["fin0"], w["fdn0"], "ffn0", ride=ride_early)
    comm.paired("early", got)
    dog = _mm_nt(dh1, w["hout"][None], "hgrn_out_dx")
    (dproj, dlb, dgn), got = _hgrn_bwd(proj, lb_logits, out_gain, o, states, dog, rider=comm.exchange_rider("early"))
    comm.exchanged("early", got)
    late = {"hout": _mm_tn(og, dh1, 1, D_MODEL, D_MODEL, "hgrn_out_dw")[0],
            "hin": _mm_tn(u0, dproj, N_CHIPS, D_MODEL, D_MODEL, "hgrn_in_dw")}
    comm.pair_now(late, "late")
    dx, d_nm0, got = _mm_nt_normbwd(dproj, w["hin"], x, nm0, dh1, "hgrn_in_dx", rider=comm.exchange_rider("late"))
    comm.exchanged("late", got)

    small["norm_mix"] = jnp.concatenate([d_nm0, d_nm1], axis=0)
    small["norm_ffn"] = jnp.concatenate([d_nf0, d_nf1], axis=0)
    small["lb"] = dlb.reshape(1, HGRN_HEADS * HEAD)
    small["out_norm"] = dgn.reshape(HGRN_HEADS, HEAD)
    return loss, dx, small


def _place():
    x, y, c = lax.axis_index("x"), lax.axis_index("y"), lax.axis_index("c")
    others = [(1 - x, y), (x, 1 - y), (1 - x, 1 - y)]
    return x, y, c, others


ANY = pl.BlockSpec(memory_space=pl.ANY)


class _GatherRider:
    def __init__(self, shards):
        self.operands = list(shards)
        n = self.n = len(shards)
        self.out_shape = [jax.ShapeDtypeStruct((N_CHIPS,) + s.shape, s.dtype) for s in shards]
        self.scratch = [pltpu.SemaphoreType.DMA((3 * n,)), pltpu.SemaphoreType.DMA((3 * n,)),
                        pltpu.SemaphoreType.DMA((3 * n,)), pltpu.SemaphoreType.DMA((3 * n,)),
                        pltpu.SemaphoreType.DMA((n,)), pltpu.SemaphoreType.DMA((n,))]

    def _copies(self, ins, outs, sems):
        ici_send, ici_recv, _, _, own_send, own_recv = sems
        x, y, c, others = _place()
        me = 2 * x + y
        own = [pltpu.make_async_remote_copy(
            src_ref=ins[a], dst_ref=outs[a].at[me], send_sem=own_send.at[a], recv_sem=own_recv.at[a],
            device_id=(x, y, 1 - c), device_id_type=MESH) for a in range(self.n)]
        sends = [pltpu.make_async_remote_copy(
            src_ref=ins[a].at[c], dst_ref=outs[a].at[me, c], send_sem=ici_send.at[a * 3 + k], recv_sem=ici_recv.at[a * 3 + k],
            device_id=(ox, oy, c), device_id_type=MESH) for a in range(self.n) for k, (ox, oy) in enumerate(others)]
        return own, sends

    def start(self, ins, outs, sems):
        own, sends = self._copies(ins, outs, sems)
        for cp in own + sends:
            cp.start()

    def finish(self, ins, outs, sems):
        ici_send, ici_recv, d2d_send, d2d_recv, _, _ = sems
        x, y, c, others = _place()
        sibling = (x, y, 1 - c)
        own, sends = self._copies(ins, outs, sems)
        passes = []
        for a in range(self.n):
            for k, (ox, oy) in enumerate(others):
                s = a * 3 + k
                got = outs[a].at[2 * ox + oy, c]
                pltpu.make_async_remote_copy(
                    src_ref=got, dst_ref=got, send_sem=ici_send.at[s], recv_sem=ici_recv.at[s],
                    device_id=(ox, oy, c), device_id_type=MESH).wait_recv()
                fwd = pltpu.make_async_remote_copy(
                    src_ref=got, dst_ref=got, send_sem=d2d_send.at[s], recv_sem=d2d_recv.at[s],
                    device_id=sibling, device_id_type=MESH)
                fwd.start()
                passes.append(fwd)
        for a in range(self.n):
            for k, (ox, oy) in enumerate(others):
                s = a * 3 + k
                theirs = outs[a].at[2 * ox + oy, 1 - c]
                pltpu.make_async_remote_copy(
                    src_ref=theirs, dst_ref=theirs, send_sem=d2d_send.at[s], recv_sem=d2d_recv.at[s],
                    device_id=sibling, device_id_type=MESH).wait_recv()
        for cp in own:
            cp.wait()
        for cp in sends + passes:
            cp.wait_send()


class _PairRider:
    def __init__(self, grads):
        self.operands = list(grads)
        n = self.n = len(grads)
        self.out_shape = [jax.ShapeDtypeStruct((N_CHIPS,) + g.shape[2:], F32) for g in grads]
        self.scratch = [pltpu.SemaphoreType.DMA((N_CHIPS * n,)), pltpu.SemaphoreType.DMA((N_CHIPS * n,))]

    def _copies(self, ins, outs, sems):
        send_sem, recv_sem = sems
        x, y, c, _ = _place()
        return [pltpu.make_async_remote_copy(
            src_ref=ins[a].at[j, 1 - c], dst_ref=outs[a].at[j], send_sem=send_sem.at[a * N_CHIPS + j],
            recv_sem=recv_sem.at[a * N_CHIPS + j], device_id=(x, y, 1 - c), device_id_type=MESH)
            for a in range(self.n) for j in range(N_CHIPS)]

    def start(self, ins, outs, sems):
        for cp in self._copies(ins, outs, sems):
            cp.start()

    def finish(self, ins, outs, sems):
        for cp in self._copies(ins, outs, sems):
            cp.wait()


class _ExchangeRider:
    def __init__(self, parts):
        self.operands = list(parts)
        n = self.n = len(parts)
        self.out_shape = [jax.ShapeDtypeStruct(p.shape, p.dtype) for p in parts]
        self.scratch = [pltpu.SemaphoreType.DMA((3 * n,)), pltpu.SemaphoreType.DMA((3 * n,))]

    def _copies(self, ins, outs, sems):
        send_sem, recv_sem = sems
        x, y, c, others = _place()
        me = 2 * x + y
        return [pltpu.make_async_remote_copy(
            src_ref=ins[a].at[2 * ox + oy], dst_ref=outs[a].at[me], send_sem=send_sem.at[a * 3 + k],
            recv_sem=recv_sem.at[a * 3 + k], device_id=(ox, oy, c), device_id_type=MESH)
            for a in range(self.n) for k, (ox, oy) in enumerate(others)]

    def start(self, ins, outs, sems):
        for cp in self._copies(ins, outs, sems):
            cp.start()

    def finish(self, ins, outs, sems):
        send_sem, recv_sem = sems
        x, y, c, others = _place()
        for a in range(self.n):
            for k, (ox, oy) in enumerate(others):
                s = a * 3 + k
                got = outs[a].at[2 * ox + oy]
                pltpu.make_async_remote_copy(
                    src_ref=got, dst_ref=got, send_sem=send_sem.at[s], recv_sem=recv_sem.at[s],
                    device_id=(ox, oy, c), device_id_type=MESH).wait_recv()
        for cp in self._copies(ins, outs, sems):
            cp.wait_send()


def _run_rider(rider, name):
    n = rider.n

    def body(*refs):
        ins, outs, sems = refs[:n], refs[n:2 * n], refs[2 * n:]
        rider.start(ins, outs, sems)
        rider.finish(ins, outs, sems)

    return pl.pallas_call(
        body, name=name, in_specs=[ANY] * n, out_specs=[ANY] * n,
        out_shape=rider.out_shape, scratch_shapes=rider.scratch)(*rider.operands)


def _ride(rider, body, n_in, n_out, first, last):
    if rider is None:
        return body
    n = rider.n

    def wrapped(*refs):
        host_in, r_in = refs[:n_in], refs[n_in:n_in + n]
        host_out = refs[n_in + n:n_in + n + n_out]
        r_out = refs[n_in + n + n_out:n_in + 2 * n + n_out]
        rest = refs[n_in + 2 * n + n_out:]
        host_scr, sems = rest[:len(rest) - len(rider.scratch)], rest[len(rest) - len(rider.scratch):]

        @pl.when(first())
        def _():
            rider.start(r_in, r_out, sems)

        body(*host_in, *host_out, *host_scr)

        @pl.when(last())
        def _():
            rider.finish(r_in, r_out, sems)

    return wrapped


def _rider_args(rider):
    if rider is None:
        return [], [], [], [], []
    return rider.operands, [ANY] * rider.n, [ANY] * rider.n, rider.out_shape, rider.scratch


def _pair_sum(g, got, c_idx):
    _, _, r, cw = g.shape
    tr = _row_tile(r, cw)

    def body(c_ref, g_ref, got_ref, p_ref, pb_ref):
        v = g_ref[...] + got_ref[...]
        p_ref[...] = v
        pb_ref[...] = v.astype(BF16)

    blk = pl.BlockSpec((None, tr, cw), lambda j, i, c_ref: (j, i, 0))
    return pl.pallas_call(
        body, name="grad_pair_sum",
        grid_spec=pltpu.PrefetchScalarGridSpec(
            num_scalar_prefetch=1, grid=(N_CHIPS, r // tr),
            in_specs=[pl.BlockSpec((None, None, tr, cw), lambda j, i, c_ref: (j, c_ref[0], i, 0)), blk],
            out_specs=[blk, blk]),
        out_shape=[jax.ShapeDtypeStruct((N_CHIPS, r, cw), F32), jax.ShapeDtypeStruct((N_CHIPS, r, cw), BF16)],
        compiler_params=_params(("parallel", "parallel")))(c_idx, g, got)


def _chip_sum(p, got, me_idx):
    _, r, cw = p.shape
    tr = _row_tile(r, cw)

    def body(me_ref, own_ref, got_ref, t_ref):
        me = me_ref[0]
        acc = None
        for s in range(N_CHIPS):
            term = jnp.where(me == s, own_ref[...], got_ref[s].astype(F32))
            acc = term if acc is None else acc + term
        t_ref[...] = acc

    return pl.pallas_call(
        body, name="grad_chip_sum",
        grid_spec=pltpu.PrefetchScalarGridSpec(
            num_scalar_prefetch=1, grid=(r // tr,),
            in_specs=[pl.BlockSpec((None, tr, cw), lambda i, me_ref: (me_ref[0], i, 0)),
                      pl.BlockSpec((N_CHIPS, tr, cw), lambda i, me_ref: (0, i, 0))],
            out_specs=pl.BlockSpec((tr, cw), lambda i, me_ref: (i, 0))),
        out_shape=jax.ShapeDtypeStruct((r, cw), F32),
        compiler_params=_params(("parallel",)))(me_idx, p, got)


def _pair_share(halves):
    n = len(halves)

    def body(*refs):
        ins, outs = refs[:n], refs[n:2 * n]
        send_sem, recv_sem = refs[2 * n:]
        x, y, c, _ = _place()
        cps = [pltpu.make_async_remote_copy(
            src_ref=ins[a], dst_ref=outs[a], send_sem=send_sem.at[a], recv_sem=recv_sem.at[a],
            device_id=(x, y, 1 - c), device_id_type=MESH) for a in range(n)]
        for cp in cps:
            cp.start()
        for cp in cps:
            cp.wait()

    return pl.pallas_call(
        body, name="grad_pair_share",
        in_specs=[ANY] * n, out_specs=[ANY] * n,
        out_shape=[jax.ShapeDtypeStruct(h.shape, F32) for h in halves],
        scratch_shapes=[pltpu.SemaphoreType.DMA((n,)), pltpu.SemaphoreType.DMA((n,))],
        )(*halves)


def _small_allreduce(pack):
    m_per, ncol = pack.shape
    n_dev = 8

    def body(x_ref, sum_ref, all_ref, send_sems, recv_sems, local_sem):
        x, y, c, others = _place()
        me, sibling = (x, y, c), (x, y, 1 - c)

        def rows(px, py, pc):
            return all_ref.at[pl.ds((4 * px + 2 * py + pc) * m_per, m_per), :]

        def copy(k, block, to, src=None):
            return pltpu.make_async_remote_copy(
                src_ref=rows(*block) if src is None else src, dst_ref=rows(*block),
                send_sem=send_sems.at[k], recv_sem=recv_sems.at[k], device_id=to, device_id_type=MESH)

        mine = pltpu.make_async_copy(x_ref, rows(*me), local_sem)
        mine.start()
        first = [copy(0, me, sibling, src=x_ref)]
        first += [copy(1 + j, me, (*chip, c), src=x_ref) for j, chip in enumerate(others)]
        for cp in first:
            cp.start()
        passed = [copy(4 + j, (*chip, c), sibling) for j, chip in enumerate(others)]
        for j, chip in enumerate(others):
            copy(1 + j, (*chip, c), me).wait_recv()
            passed[j].start()
        copy(0, sibling, me).wait_recv()
        for j, chip in enumerate(others):
            copy(4 + j, (*chip, 1 - c), me).wait_recv()
        for cp in first + passed:
            cp.wait_send()
        mine.wait()
        acc = all_ref[0:m_per, :]
        for dvc in range(1, n_dev):
            acc = acc + all_ref[dvc * m_per:(dvc + 1) * m_per, :]
        sum_ref[...] = acc

    return pl.pallas_call(
        body, name="small_allreduce",
        in_specs=[pl.BlockSpec(memory_space=pltpu.VMEM)],
        out_specs=pl.BlockSpec(memory_space=pltpu.VMEM),
        out_shape=jax.ShapeDtypeStruct((m_per, ncol), F32),
        scratch_shapes=[pltpu.VMEM((n_dev * m_per, ncol), F32),
                        pltpu.SemaphoreType.DMA((7,)), pltpu.SemaphoreType.DMA((7,)), pltpu.SemaphoreType.DMA],
        )(pack)


def _adam_math(w, g, m, v):
    m = ADAM_B1 * m + (1.0 - ADAM_B1) * g
    v = ADAM_B2 * v + (1.0 - ADAM_B2) * (g * g)
    m_hat = m / (1.0 - ADAM_B1 ** ADAM_STEP)
    v_hat = v / (1.0 - ADAM_B2 ** ADAM_STEP)
    delta = -ADAM_LR * (m_hat / (jnp.sqrt(v_hat) + ADAM_EPS) + ADAM_WD * w)
    return delta, m, v


def _adamw(halves, c_idx, w, m, v, name):
    L = len(halves)
    r, C = halves[0][0].shape
    tr = _row_tile(r, C, 512 * 1024)
    nt = r // tr

    def body(c_ref, *refs):
        g_refs, (w_ref, m_ref, v_ref), (g_ref, d_ref, nm_ref, nv_ref) = refs[:2 * L], refs[2 * L:2 * L + 3], refs[2 * L + 3:]
        own = pl.program_id(1) == c_ref[0]
        g = None
        for l in range(L):
            cand = jnp.where(own, g_refs[2 * l][...], g_refs[2 * l + 1][...])
            g = cand if g is None else jnp.where(pl.program_id(0) == l, cand, g)
        g_ref[...] = g
        d_ref[...], nm_ref[...], nv_ref[...] = _adam_math(w_ref[...], g, m_ref[...], v_ref[...])

    def half(l, mine):
        def index(ll, h, i, c_ref):
            read = (h == c_ref[0]) if mine else (h != c_ref[0])
            return jnp.where(jnp.logical_and(ll == l, read), i, 0), 0
        return pl.BlockSpec((tr, C), index)

    full = pl.BlockSpec((None, tr, C), lambda ll, h, i, c_ref: (ll, h * nt + i, 0))
    shp = jax.ShapeDtypeStruct((L, 2 * r, C), F32)
    g_specs = [half(l, mine) for l in range(L) for mine in (True, False)]
    return pl.pallas_call(
        body, name=name,
        grid_spec=pltpu.PrefetchScalarGridSpec(
            num_scalar_prefetch=1, grid=(L, 2, nt),
            in_specs=g_specs + [full] * 3, out_specs=[full] * 4),
        out_shape=[shp] * 4,
        compiler_params=_params(("arbitrary", "arbitrary", "arbitrary")))(
            c_idx, *[a for pair in halves for a in pair], w, m, v)


def _small_update(gsum, logits_pack, w, m, v):
    def body(gs_ref, lg_ref, w_ref, m_ref, v_ref, g_ref, d_ref, nm_ref, nv_ref):
        g_ref[...] = gs_ref[...]
        l0, l1, l2 = lg_ref[0:1, :], lg_ref[1:2, :], lg_ref[2:3, :]
        mx = jnp.maximum(jnp.maximum(l0, l1), l2)
        e0, e1, e2 = jnp.exp(l0 - mx), jnp.exp(l1 - mx), jnp.exp(l2 - mx)
        tot = e0 + e1 + e2
        p0, p1, p2 = e0 / tot, e1 / tot, e2 / tot
        dlb = gs_ref[4:5, :]
        g_ref[4:5, :] = dlb * p0 * (1.0 - p0)
        g_ref[5:6, :] = -dlb * p0 * p1
        g_ref[6:7, :] = -dlb * p0 * p2
        d_ref[...], nm_ref[...], nv_ref[...] = _adam_math(w_ref[...], g_ref[...], m_ref[...], v_ref[...])

    full = pl.BlockSpec(memory_space=pltpu.VMEM)
    shp = jax.ShapeDtypeStruct(gsum.shape, F32)
    return pl.pallas_call(
        body, name="small_update", in_specs=[full] * 5, out_specs=[full] * 4, out_shape=[shp] * 4)(
            gsum, logits_pack, w, m, v)


def _pack_small(norm_mix, norm_ffn, lb3, out_norm, final_norm, extra=None):
    ncol = norm_mix.shape[1]
    on = jnp.pad(out_norm.reshape(1, -1), ((0, 0), (0, ncol - out_norm.size)))
    rows = [norm_mix, norm_ffn, lb3, on, final_norm.reshape(1, ncol)]
    if extra is not None:
        rows.append(extra)
    used = sum(r.shape[0] for r in rows)
    rows.append(jnp.zeros((SMALL_ROWS - used, ncol), F32))
    return jnp.concatenate(rows, axis=0)


WEIGHT_NAMES = ("hin", "hout", "qkv", "aout", "fin0", "fin1", "fdn0", "fdn1")
FIRST_WEIGHTS = ("hin",)
LATE_WEIGHTS_A = ("hout", "fin0", "fdn0")
LATE_WEIGHTS_B = ("qkv", "aout", "fin1", "fdn1")


def _split_weights(hgrn_w_in, hgrn_w_out, attn_w_qkv, attn_w_out, ffn_w_in, ffn_w_down):
    return {"hin": hgrn_w_in[0], "hout": hgrn_w_out[0], "qkv": attn_w_qkv[0], "aout": attn_w_out[0],
            "fin0": ffn_w_in[0], "fin1": ffn_w_in[1], "fdn0": ffn_w_down[0], "fdn1": ffn_w_down[1]}


def _halves(v):
    r, c = v.shape
    return v.reshape(2, r // 2, c)


def _full_weights(gathered):
    out = {}
    for k, g in gathered.items():
        _, _, r, c = g.shape
        if k in ("hin", "qkv", "fin0", "fin1"):
            out[k] = g.reshape(N_CHIPS, 2 * r, c)
        else:
            out[k] = g.reshape(N_CHIPS * 2 * r, c)
    return out


class _StepComm:
    def __init__(self, shards, c_idx, me_idx):
        self.shards, self.c_idx, self.me_idx = shards, c_idx, me_idx
        self.halves = {}
        self._stage = {}

    def gather_rider(self, names):
        return _GatherRider([_halves(self.shards[k].astype(BF16)) for k in names])

    def gathered(self, names, got):
        return _full_weights(dict(zip(names, got)))

    def first_weights(self):
        return self.gathered(FIRST_WEIGHTS, _run_rider(self.gather_rider(FIRST_WEIGHTS), "gather_first"))

    def pair_rider(self, grads, tag):
        names = list(grads)
        g4 = []
        for k in names:
            r, c = self.shards[k].shape
            g4.append(grads[k].reshape(N_CHIPS, 2, r // 2, c))
        self._stage[tag] = (names, g4)
        return _PairRider(g4)

    def pair_now(self, grads, tag):
        self.paired(tag, _run_rider(self.pair_rider(grads, tag), "grad_pair_exchange_" + tag))

    def paired(self, tag, got):
        names, g4 = self._stage[tag]
        self._stage[tag] = (names, [_pair_sum(g, s, self.c_idx) for g, s in zip(g4, got)])

    def exchange_rider(self, tag):
        return _ExchangeRider([s[1] for s in self._stage[tag][1]])

    def exchanged(self, tag, got):
        names, sums = self._stage.pop(tag)
        for k, s, g in zip(names, sums, got):
            self.halves[k] = _chip_sum(s[0], g, self.me_idx)

    def shared_halves(self):
        mine = [self.halves[k] for k in WEIGHT_NAMES]
        return dict(zip(WEIGHT_NAMES, zip(mine, _pair_share(mine))))


def kernel(x, norm_mix, norm_ffn, hgrn_w_in, hgrn_lb_logits, hgrn_out_norm, hgrn_w_out, attn_w_qkv, attn_w_out, ffn_w_in, ffn_w_down, final_norm, loss_target, m_norm_mix, m_norm_ffn, m_hgrn_w_in, m_hgrn_lb_logits, m_hgrn_out_norm, m_hgrn_w_out, m_attn_w_qkv, m_attn_w_out, m_ffn_w_in, m_ffn_w_down, m_final_norm, v_norm_mix, v_norm_ffn, v_hgrn_w_in, v_hgrn_lb_logits, v_hgrn_out_norm, v_hgrn_w_out, v_attn_w_qkv, v_attn_w_out, v_ffn_w_in, v_ffn_w_down, v_final_norm):
    S = x.shape[1]
    xi, yi, ci = lax.axis_index("x"), lax.axis_index("y"), lax.axis_index("c")
    c_idx = jnp.reshape(ci, (1,)).astype(jnp.int32)
    me_idx = jnp.reshape(2 * xi + yi, (1,)).astype(jnp.int32)

    w_own = _split_weights(hgrn_w_in, hgrn_w_out, attn_w_qkv, attn_w_out, ffn_w_in, ffn_w_down)

    comm = _StepComm(w_own, c_idx, me_idx)
    loss, dx, small = _local_step(
        x.reshape(S, D_MODEL), loss_target.reshape(S, D_MODEL), norm_mix, norm_ffn, hgrn_lb_logits,
        hgrn_out_norm, final_norm.reshape(1, D_MODEL), comm)

    halves = comm.shared_halves()
    updated = {}
    for tensor, layers, (wt, mt, vt) in (
            ("hgrn_w_in", ("hin",), (hgrn_w_in, m_hgrn_w_in, v_hgrn_w_in)),
            ("hgrn_w_out", ("hout",), (hgrn_w_out, m_hgrn_w_out, v_hgrn_w_out)),
            ("attn_w_qkv", ("qkv",), (attn_w_qkv, m_attn_w_qkv, v_attn_w_qkv)),
            ("attn_w_out", ("aout",), (attn_w_out, m_attn_w_out, v_attn_w_out)),
            ("ffn_w_in", ("fin0", "fin1"), (ffn_w_in, m_ffn_w_in, v_ffn_w_in)),
            ("ffn_w_down", ("fdn0", "fdn1"), (ffn_w_down, m_ffn_w_down, v_ffn_w_down))):
        updated[tensor] = _adamw([halves[k] for k in layers], c_idx, wt, mt, vt, "adamw_" + tensor)

    loss_row = jnp.pad(loss, ((0, 0), (0, D_MODEL - loss.shape[1])))
    lb3 = jnp.concatenate([small["lb"], jnp.zeros((2, D_MODEL), F32)], axis=0)
    on_grad = jnp.sum(small["out_norm"], axis=0, keepdims=True)
    pack = _pack_small(small["norm_mix"], small["norm_ffn"], lb3, on_grad, small["final_norm"], loss_row)
    gsum = _small_allreduce(pack)
    w_s = _pack_small(norm_mix, norm_ffn, hgrn_lb_logits, hgrn_out_norm, final_norm)
    m_s = _pack_small(m_norm_mix, m_norm_ffn, m_hgrn_lb_logits, m_hgrn_out_norm, m_final_norm)
    v_s = _pack_small(v_norm_mix, v_norm_ffn, v_hgrn_lb_logits, v_hgrn_out_norm, v_final_norm)
    lg_pack = jnp.pad(hgrn_lb_logits, ((0, 8 - hgrn_lb_logits.shape[0]), (0, 0)))
    sg, sd, sm, sv = _small_update(gsum, lg_pack, w_s, m_s, v_s)

    def unpack(p):
        return (p[0:2], p[2:4], p[4:7], p[7:8, :HEAD], p[8])

    def assemble(p, which):
        nmx, nff, lbl, onm, fnm = unpack(p)
        hin, hout, qkv, aout, fin, fdn = [updated[t][which] for t in
                                          ("hgrn_w_in", "hgrn_w_out", "attn_w_qkv", "attn_w_out", "ffn_w_in", "ffn_w_down")]
        return (nmx, nff, hin, lbl, onm, hout, qkv, aout, fin, fdn, fnm)

    total_loss = gsum[9, 0]
    return (total_loss, dx.reshape(1, S, D_MODEL), *assemble(sg, 0), *assemble(sd, 1), *assemble(sm, 2), *assemble(sv, 3))
```

```python
import functools

import jax
import jax.numpy as jnp
from jax import lax
from jax.experimental import pallas as pl
from jax.experimental.pallas import tpu as pltpu

F32 = jnp.float32
BF16 = jnp.bfloat16
MESH = pl.DeviceIdType.MESH

D_MODEL = 1024
HEAD = 128
HGRN_HEADS = 8
HGRN_CHUNK = 64
ATTN_GROUPS = ((128, 1), (512, 4), (2048, 16))
ATTN_SPAN = 128
HEADS_PER_GROUP = 4
GROUP_W = HEADS_PER_GROUP * HEAD
D_FF = 2816
NORM_EPS = 1e-6
ROPE_THETA = 10000.0
NEG = -1e30

ADAM_LR, ADAM_B1, ADAM_B2, ADAM_EPS, ADAM_WD, ADAM_STEP = 0.001, 0.9, 0.999, 1e-08, 0.01, 10

N_CHIPS = 4
VMEM_LIMIT = 56 * 1024 * 1024
SMALL_ROWS = 16


def _params(sem=None):
    return pltpu.CompilerParams(dimension_semantics=sem, vmem_limit_bytes=VMEM_LIMIT)


def _row_tile(rows, cols, budget_bytes=3 * 512 * 1024):
    best = 8
    for t in range(8, rows + 1, 8):
        if rows % t == 0 and t * cols * 4 <= budget_bytes:
            best = t
    assert rows % best == 0
    return best


def _grid_corner(i, j):
    return jnp.logical_and(pl.program_id(0) == i, pl.program_id(1) == j)


def _sigmoid(v):
    return 0.5 * jnp.tanh(0.5 * v) + 0.5


def _dot(a, b):
    return jnp.dot(a, b, preferred_element_type=F32)


def _dot_nt(a, b):
    return lax.dot_general(a, b, (((1,), (1,)), ((), ())), preferred_element_type=F32)


def _dot_tn(a, b):
    return lax.dot_general(a, b, (((0,), (0,)), ((), ())), preferred_element_type=F32)


def _dot_exact(ones, b):
    ones = ones.astype(BF16)
    hi = b.astype(BF16)
    rest = b - hi.astype(F32)
    mid = rest.astype(BF16)
    low = (rest - mid.astype(F32)).astype(BF16)
    return _dot(ones, hi) + _dot(ones, mid) + _dot(ones, low)


def _rstd(v):
    return lax.rsqrt(jnp.mean(v * v, axis=-1, keepdims=True) + NORM_EPS)


def _norm_mm(h, gain, w3, name, out_dtype=F32, tm=1024, rider=None):
    S, K = h.shape
    J, _, n = w3.shape
    gi = S // tm

    def body(h_ref, g_ref, w_ref, y_ref, u_ref):
        @pl.when(pl.program_id(1) == 0)
        def _():
            v = h_ref[...]
            u_ref[...] = (v * _rstd(v) * g_ref[...]).astype(BF16)

        y_ref[...] = _dot(u_ref[...], w_ref[pl.program_id(1)]).astype(y_ref.dtype)

    r_ops, r_in, r_out, r_shape, r_scr = _rider_args(rider)
    res = pl.pallas_call(
        _ride(rider, body, 3, 2, functools.partial(_grid_corner, 0, 0), functools.partial(_grid_corner, gi - 1, J - 1)),
        name=name, grid=(gi, J),
        in_specs=[pl.BlockSpec((tm, K), lambda i, j: (i, 0)),
                  pl.BlockSpec((1, K), lambda i, j: (0, 0)),
                  pl.BlockSpec((J, K, n), lambda i, j: (0, 0, 0))] + r_in,
        out_specs=[pl.BlockSpec((tm, n), lambda i, j: (i, j)), pl.BlockSpec((tm, K), lambda i, j: (i, 0))] + r_out,
        out_shape=[jax.ShapeDtypeStruct((S, J * n), out_dtype), jax.ShapeDtypeStruct((S, K), BF16)] + r_shape,
        scratch_shapes=r_scr,
        compiler_params=_params(("arbitrary", "arbitrary")))(h, gain, w3, *r_ops)
    return res[0], res[1], res[2:]


def _mm_res(h, a, w2, name, tm=512):
    S, N = h.shape
    K = a.shape[1]

    def body(h_ref, a_ref, w_ref, o_ref):
        o_ref[...] = h_ref[...] + _dot(a_ref[...], w_ref[...])

    return pl.pallas_call(
        body, name=name, grid=(S // tm,),
        in_specs=[pl.BlockSpec((tm, N), lambda i: (i, 0)),
                  pl.BlockSpec((tm, K), lambda i: (i, 0)),
                  pl.BlockSpec((K, N), lambda i: (0, 0))],
        out_specs=pl.BlockSpec((tm, N), lambda i: (i, 0)),
        out_shape=jax.ShapeDtypeStruct((S, N), F32),
        compiler_params=_params(("parallel",)))(h, a, w2)


def _swiglu(z_ref, F):
    g = z_ref[:, :F].astype(F32)
    return (g * _sigmoid(g) * z_ref[:, F:].astype(F32)).astype(BF16)


def _swiglu_mm_res(h, z, w2, name, tm=256):
    S, N = h.shape
    F = w2.shape[0]

    def body(h_ref, z_ref, w_ref, o_ref, a_ref):
        a = _swiglu(z_ref, F)
        a_ref[...] = a
        o_ref[...] = h_ref[...] + _dot(a, w_ref[...])

    return pl.pallas_call(
        body, name=name, grid=(S // tm,),
        in_specs=[pl.BlockSpec((tm, N), lambda i: (i, 0)),
                  pl.BlockSpec((tm, 2 * F), lambda i: (i, 0)),
                  pl.BlockSpec((F, N), lambda i: (0, 0))],
        out_specs=[pl.BlockSpec((tm, N), lambda i: (i, 0)), pl.BlockSpec((tm, F), lambda i: (i, 0))],
        out_shape=[jax.ShapeDtypeStruct((S, N), F32), jax.ShapeDtypeStruct((S, F), BF16)],
        compiler_params=_params(("parallel",)))(h, z, w2)


def _dy_specs(dy, J, n, tm):
    if dy.ndim == 3:
        return [pl.BlockSpec((None, tm, n), functools.partial(lambda i, j: (j, i, 0), j=j)) for j in range(J)]
    return [pl.BlockSpec((tm, n), functools.partial(lambda i, j: (i, j), j=j)) for j in range(J)]


def _acc_nt(dy_refs, w_ref):
    acc = None
    for j, r in enumerate(dy_refs):
        t = _dot_nt(r[...].astype(BF16), w_ref[j])
        acc = t if acc is None else acc + t
    return acc


def _mm_nt(dy, w3, name, out_dtype=F32, tm=512):
    J, K, n = w3.shape
    S = dy.shape[-2]

    def body(*refs):
        dy_refs, w_ref, o_ref = refs[:J], refs[J], refs[J + 1]
        o_ref[...] = _acc_nt(dy_refs, w_ref).astype(o_ref.dtype)

    return pl.pallas_call(
        body, name=name, grid=(S // tm,),
        in_specs=_dy_specs(dy, J, n, tm) + [pl.BlockSpec((J, K, n), lambda i: (0, 0, 0))],
        out_specs=pl.BlockSpec((tm, K), lambda i: (i, 0)),
        out_shape=jax.ShapeDtypeStruct((S, K), out_dtype),
        compiler_params=_params(("parallel",)))(*([dy] * J), w3)


def _mm_nt_normbwd(dy, w3, h, gain, dh, name, tm=512, rider=None):
    J, K, n = w3.shape
    S = h.shape[0]
    steps = S // tm

    def body(*refs):
        dy_refs, w_ref, h_ref, g_ref, dh_ref, o_ref, dg_ref = refs[:J], *refs[J:]
        du = _acc_nt(dy_refs, w_ref)
        v = h_ref[...]
        r = _rstd(v)
        xh = v * r
        dyg = du * g_ref[...]
        o_ref[...] = dh_ref[...] + r * (dyg - xh * jnp.mean(dyg * xh, axis=-1, keepdims=True))

        @pl.when(pl.program_id(0) == 0)
        def _():
            dg_ref[...] = jnp.zeros_like(dg_ref)

        dg_ref[...] += jnp.sum(du * xh, axis=0, keepdims=True)

    row = pl.BlockSpec((tm, K), lambda i: (i, 0))
    vec = pl.BlockSpec((1, K), lambda i: (0, 0))
    r_ops, r_in, r_out, r_shape, r_scr = _rider_args(rider)
    res = pl.pallas_call(
        _ride(rider, body, J + 4, 2, lambda: pl.program_id(0) == 0, lambda: pl.program_id(0) == steps - 1),
        name=name, grid=(steps,),
        in_specs=_dy_specs(dy, J, n, tm) + [pl.BlockSpec((J, K, n), lambda i: (0, 0, 0)), row, vec, row] + r_in,
        out_specs=[row, vec] + r_out,
        out_shape=[jax.ShapeDtypeStruct((S, K), F32), jax.ShapeDtypeStruct((1, K), F32)] + r_shape,
        scratch_shapes=r_scr,
        compiler_params=_params(("arbitrary",)))(*([dy] * J), w3, h, gain, dh, *r_ops)
    return res[0], res[1], res[2:]


def _mm_nt_swiglu_bwd(dh, w2, z, name, tm=256):
    F, N = w2.shape
    S = dh.shape[0]

    def body(dh_ref, w_ref, z_ref, o_ref):
        da = _dot_nt(dh_ref[...].astype(BF16), w_ref[...])
        g = z_ref[:, :F].astype(F32)
        u = z_ref[:, F:].astype(F32)
        sg = _sigmoid(g)
        o_ref[:, :F] = (da * u * (sg * (1.0 + g * (1.0 - sg)))).astype(BF16)
        o_ref[:, F:] = (da * (g * sg)).astype(BF16)

    return pl.pallas_call(
        body, name=name, grid=(S // tm,),
        in_specs=[pl.BlockSpec((tm, N), lambda i: (i, 0)),
                  pl.BlockSpec((F, N), lambda i: (0, 0)),
                  pl.BlockSpec((tm, 2 * F), lambda i: (i, 0))],
        out_specs=pl.BlockSpec((tm, 2 * F), lambda i: (i, 0)),
        out_shape=jax.ShapeDtypeStruct((S, 2 * F), BF16),
        compiler_params=_params(("parallel",)))(dh, w2, z)


def _mm_tn(x, dy, J, n, tn, name):
    tpn = n // tn
    ts = 1024
    S, K = x.shape
    if dy.ndim == 3:
        dy_spec = pl.BlockSpec((None, ts, tn), lambda c, s: (c // tpn, s, c % tpn))
    else:
        dy_spec = pl.BlockSpec((ts, tn), lambda c, s: (s, c))

    def body(x_ref, dy_ref, o_ref):
        @pl.when(pl.program_id(1) == 0)
        def _():
            o_ref[...] = jnp.zeros_like(o_ref)

        o_ref[...] += _dot_tn(x_ref[...], dy_ref[...].astype(BF16))

    return pl.pallas_call(
        body, name=name, grid=(J * tpn, S // ts),
        in_specs=[pl.BlockSpec((ts, K), lambda c, s: (s, 0)), dy_spec],
        out_specs=pl.BlockSpec((None, K, tn), lambda c, s: (c // tpn, 0, c % tpn)),
        out_shape=jax.ShapeDtypeStruct((J, K, n), F32),
        compiler_params=_params(("parallel", "arbitrary")))(x, dy)


def _loss_head(h, gain, target, tm=512):
    S, K = h.shape

    def body(h_ref, g_ref, t_ref, dh_ref, loss_ref, dg_ref):
        v = h_ref[...]
        r = _rstd(v)
        xh = v * r
        g = g_ref[...]
        dy = (xh * g - t_ref[...]) * (1.0 / K)
        dyg = dy * g
        dh_ref[...] = r * (dyg - xh * jnp.mean(dyg * xh, axis=-1, keepdims=True))

        @pl.when(pl.program_id(0) == 0)
        def _():
            loss_ref[...] = jnp.zeros_like(loss_ref)
            dg_ref[...] = jnp.zeros_like(dg_ref)

        part = jnp.sum(jnp.sum(dy * dy, axis=-1, keepdims=True), axis=0, keepdims=True) * (0.5 * K)
        lane = lax.broadcasted_iota(jnp.int32, loss_ref.shape, 1)
        loss_ref[...] += jnp.where(lane == 0, part, 0.0)
        dg_ref[...] += jnp.sum(dy * xh, axis=0, keepdims=True)

    row = pl.BlockSpec((tm, K), lambda i: (i, 0))
    vec = pl.BlockSpec((1, K), lambda i: (0, 0))
    return pl.pallas_call(
        body, name="loss_head", grid=(S // tm,),
        in_specs=[row, vec, row],
        out_specs=[row, pl.BlockSpec((1, HEAD), lambda i: (0, 0)), vec],
        out_shape=[jax.ShapeDtypeStruct((S, K), F32), jax.ShapeDtypeStruct((1, HEAD), F32),
                   jax.ShapeDtypeStruct((1, K), F32)],
        compiler_params=_params(("arbitrary",)))(h, gain, target)


def _lower_bound(lg_ref):
    l0, l1, l2 = lg_ref[0:1, :], lg_ref[1:2, :], lg_ref[2:3, :]
    mx = jnp.maximum(jnp.maximum(l0, l1), l2)
    e0, e1, e2 = jnp.exp(l0 - mx), jnp.exp(l1 - mx), jnp.exp(l2 - mx)
    return e0 / (e0 + e1 + e2)


def _chunks(v, ncb):
    C = HGRN_CHUNK
    return [v[c * C:(c + 1) * C] for c in range(ncb)]


def _rows(parts):
    return jnp.concatenate(parts, axis=0)


def _block_gates(qz, fz, lb, ncb):
    C = HGRN_CHUNK
    row = lax.broadcasted_iota(jnp.int32, (C, C), 0)
    col = lax.broadcasted_iota(jnp.int32, (C, C), 1)
    tri = (col <= row).astype(F32)
    first_half = lax.broadcasted_iota(jnp.int32, (C, HEAD), 0) < C // 2
    sig = _sigmoid(fz)
    fg = lb + (1.0 - lb) * sig
    key = 1.0 - fg
    lg = jnp.log(fg)
    lgs = _chunks(lg, ncb)
    b = _rows([_dot_exact(tri, v) for v in lgs])
    r_c = [jnp.sum(jnp.where(first_half, v, 0.0), axis=0, keepdims=True) for v in lgs]
    bl_c = [jnp.sum(v, axis=0, keepdims=True) for v in lgs]
    r = _rows([jnp.broadcast_to(v, (C, HEAD)) for v in r_c])
    e_br, e_rb = jnp.exp(b - r), jnp.exp(r - b)
    e_b = e_br * _rows([jnp.broadcast_to(jnp.exp(v), (C, HEAD)) for v in r_c])
    e_lb = e_rb * _rows([jnp.broadcast_to(jnp.exp(e - v), (C, HEAD)) for e, v in zip(bl_c, r_c)])
    sq = _sigmoid(qz)
    qy = qz * sq
    return sig, fg, key, (e_br, e_rb, e_b, e_lb), bl_c, sq, qy


def _hgrn_fwd(proj, logits, gain, tb=1024, rider=None):
    S = proj.shape[0]
    H, C = HGRN_HEADS, HGRN_CHUNK
    ncb = tb // C

    def body(q_ref, f_ref, i_ref, g_ref, lg_ref, gn_ref, o_ref, og_ref, st_ref, state):
        @pl.when(pl.program_id(1) == 0)
        def _():
            state[...] = jnp.zeros_like(state)

        lb = _lower_bound(lg_ref)
        causal = lax.broadcasted_iota(jnp.int32, (C, C), 1) <= lax.broadcasted_iota(jnp.int32, (C, C), 0)
        qz, fz, gz = q_ref[...], f_ref[...], g_ref[...]
        _, _, key, (e_br, e_rb, e_b, e_lb), bl_c, _, qy = _block_gates(qz, fz, lb, ncb)
        qs = _chunks((qy * e_br).astype(BF16), ncb)
        ks = _chunks((key * e_rb).astype(BF16), ncb)
        qb = _chunks((qy * e_b).astype(BF16), ncb)
        ke = _chunks((key * e_lb).astype(BF16), ncb)
        vb = _chunks(i_ref[...].astype(BF16), ncb)
        a = [jnp.where(causal, _dot_nt(qs[c], ks[c]), 0.0).astype(BF16) for c in range(ncb)]
        upd = [_dot_tn(vb[c], ke[c]) for c in range(ncb)]
        o_intra = [_dot(a[c], vb[c]) for c in range(ncb)]
        st = state[...]
        e_l = [jnp.exp(v) for v in bl_c]
        sts = []
        for c in range(ncb):
            sts.append(st)
            st = st * e_l[c] + upd[c]
        state[...] = st
        for c in range(ncb):
            st_ref[c] = sts[c]
        o = _rows([_dot_nt(qb[c], sts[c].astype(BF16)) + o_intra[c] for c in range(ncb)])
        o_ref[...] = o
        og_ref[...] = ((o * _rstd(o) * gn_ref[...]) * (gz * _sigmoid(gz))).astype(BF16)

    def part(p):
        return pl.BlockSpec((tb, HEAD), functools.partial(lambda h, i, p: (i, p * H + h), p=p))

    nb = S // tb
    r_ops, r_in, r_out, r_shape, r_scr = _rider_args(rider)
    res = pl.pallas_call(
        _ride(rider, body, 6, 3, functools.partial(_grid_corner, 0, 0), functools.partial(_grid_corner, H - 1, nb - 1)),
        name="hgrn_fwd", grid=(H, nb),
        in_specs=[part(0), part(1), part(2), part(3),
                  pl.BlockSpec((3, HEAD), lambda h, i: (0, h)),
                  pl.BlockSpec((1, HEAD), lambda h, i: (0, 0))] + r_in,
        out_specs=[pl.BlockSpec((tb, HEAD), lambda h, i: (i, h)),
                   pl.BlockSpec((tb, HEAD), lambda h, i: (i, h)),
                   pl.BlockSpec((None, ncb, HEAD, HEAD), lambda h, i: (h, i, 0, 0))] + r_out,
        out_shape=[jax.ShapeDtypeStruct((S, H * HEAD), F32),
                   jax.ShapeDtypeStruct((S, H * HEAD), BF16),
                   jax.ShapeDtypeStruct((H, S // C, HEAD, HEAD), F32)] + r_shape,
        scratch_shapes=[pltpu.VMEM((HEAD, HEAD), F32)] + r_scr,
        compiler_params=_params(("arbitrary", "arbitrary")))(proj, proj, proj, proj, logits, gain, *r_ops)
    return res[:3], res[3:]


def _hgrn_bwd(proj, logits, gain, o, states, dog, tb=1024, rider=None):
    S = proj.shape[0]
    H, C = HGRN_HEADS, HGRN_CHUNK
    ncb = tb // C
    nb = S // tb

    def body(q_ref, f_ref, i_ref, g_ref, lg_ref, gn_ref, o_ref, st_ref, dog_ref,
             dp_ref, dlb_ref, dgn_ref, dstate, dst_scr):
        @pl.when(pl.program_id(1) == 0)
        def _():
            dstate[...] = jnp.zeros_like(dstate)
            dlb_ref[...] = jnp.zeros_like(dlb_ref)
            dgn_ref[...] = jnp.zeros_like(dgn_ref)

        lb = _lower_bound(lg_ref)
        oml = 1.0 - lb
        gn = gn_ref[...]
        row = lax.broadcasted_iota(jnp.int32, (C, C), 0)
        col = lax.broadcasted_iota(jnp.int32, (C, C), 1)
        causal = col <= row
        tri_up = (col >= row).astype(F32)
        qz, fz, gz = q_ref[...], f_ref[...], g_ref[...]
        sig, fg, key, (e_br, e_rb, e_b, e_lb), bl_c, sq, qy = _block_gates(qz, fz, lb, ncb)
        qs_v, ks_v = (qy * e_br).astype(BF16), (key * e_rb).astype(BF16)
        qb_v, ke_v = (qy * e_b).astype(BF16), (key * e_lb).astype(BF16)
        qs, ks, qb, ke = _chunks(qs_v, ncb), _chunks(ks_v, ncb), _chunks(qb_v, ncb), _chunks(ke_v, ncb)
        vb = _chunks(i_ref[...].astype(BF16), ncb)
        ov = o_ref[...]
        rs = _rstd(ov)
        xh = ov * rs
        sg = _sigmoid(gz)
        dog_v = dog_ref[...]
        dgz = dog_v * (xh * gn) * (sg * (1.0 + gz * (1.0 - sg)))
        don = dog_v * (gz * sg)
        dgn_ref[...] += jnp.sum(don * xh, axis=0, keepdims=True)
        dyg = don * gn
        do = rs * (dyg - xh * jnp.mean(dyg * xh, axis=-1, keepdims=True))
        dob = _chunks(do.astype(BF16), ncb)
        CH = range(ncb)
        a = [jnp.where(causal, _dot_nt(qs[c], ks[c]), 0.0).astype(BF16) for c in CH]
        da = [jnp.where(causal, _dot_nt(dob[c], vb[c]), 0.0).astype(BF16) for c in CH]
        wst = [_dot_tn(dob[c], qb[c]) for c in CH]
        dv_in = [_dot_tn(a[c], dob[c]) for c in CH]
        dqs = [_dot(da[c], ks[c]) for c in CH]
        dks = [_dot_tn(da[c], qs[c]) for c in CH]
        e_l = [jnp.exp(v) for v in bl_c]
        dst = dstate[...]
        for c in reversed(range(ncb)):
            dst_scr[c] = dst
            dst = wst[c] + dst * e_l[c]
        dstate[...] = dst
        dst1b = [dst_scr[c].astype(BF16) for c in CH]
        dqb = [_dot(dob[c], st_ref[c].astype(BF16)) for c in CH]
        dke = [_dot(vb[c], dst1b[c]) for c in CH]
        dv = [dv_in[c] + _dot_nt(ke[c], dst1b[c]) for c in CH]
        dbl_st = [jnp.sum(dst_scr[c] * st_ref[c], axis=0, keepdims=True) * e_l[c] for c in CH]
        dqs, dks, dqb, dke, dv = _rows(dqs), _rows(dks), _rows(dqb), _rows(dke), _rows(dv)
        dke_ke = dke * ke_v.astype(F32)
        db = dqs * qs_v.astype(F32) - dks * ks_v.astype(F32) + dqb * qb_v.astype(F32) - dke_ke
        dlg = []
        for c, (db_c, kk_c) in enumerate(zip(_chunks(db, ncb), _chunks(dke_ke, ncb))):
            dbl = jnp.sum(kk_c, axis=0, keepdims=True) + dbl_st[c]
            dlg.append(_dot_exact(tri_up, db_c) + dbl)
        dlg = _rows(dlg)
        dkey = dks * e_rb + dke * e_lb
        dqy = dqs * e_br + dqb * e_b
        dfg = dlg / fg - dkey
        dlb_ref[...] += jnp.sum(dfg * (1.0 - sig), axis=0, keepdims=True)
        dp_ref[0] = (dqy * (sq * (1.0 + qz * (1.0 - sq)))).astype(BF16)
        dp_ref[1] = (dfg * oml * sig * (1.0 - sig)).astype(BF16)
        dp_ref[2] = dv.astype(BF16)
        dp_ref[3] = dgz.astype(BF16)

    def part(p):
        return pl.BlockSpec((tb, HEAD), functools.partial(lambda h, i, p: (nb - 1 - i, p * H + h), p=p))

    blk = pl.BlockSpec((tb, HEAD), lambda h, i: (nb - 1 - i, h))
    acc = pl.BlockSpec((None, 1, HEAD), lambda h, i: (h, 0, 0))
    r_ops, r_in, r_out, r_shape, r_scr = _rider_args(rider)
    res = pl.pallas_call(
        _ride(rider, body, 9, 3, functools.partial(_grid_corner, 0, 0), functools.partial(_grid_corner, H - 1, nb - 1)),
        name="hgrn_bwd", grid=(H, nb),
        in_specs=[part(0), part(1), part(2), part(3),
                  pl.BlockSpec((3, HEAD), lambda h, i: (0, h)),
                  pl.BlockSpec((1, HEAD), lambda h, i: (0, 0)),
                  blk,
                  pl.BlockSpec((None, ncb, HEAD, HEAD), lambda h, i: (h, nb - 1 - i, 0, 0)),
                  blk] + r_in,
        out_specs=[pl.BlockSpec((4, tb, HEAD), lambda h, i: (0, nb - 1 - i, h)), acc, acc] + r_out,
        out_shape=[jax.ShapeDtypeStruct((4, S, H * HEAD), BF16),
                   jax.ShapeDtypeStruct((H, 1, HEAD), F32),
                   jax.ShapeDtypeStruct((H, 1, HEAD), F32)] + r_shape,
        scratch_shapes=[pltpu.VMEM((HEAD, HEAD), F32), pltpu.VMEM((ncb, HEAD, HEAD), F32)] + r_scr,
        compiler_params=_params(("arbitrary", "arbitrary")))(
            proj, proj, proj, proj, logits, gain, o, states, dog, *r_ops)
    return res[:3], res[3:]


def _rope(v, cos, sin):
    return v * cos + pltpu.roll(v, HEAD // 2, 1) * sin


def _lane_pick(tile, hh):
    lane = lax.broadcasted_iota(jnp.int32, tile.shape, 1)
    return jnp.sum(jnp.where(lane == hh, tile, 0.0), axis=-1, keepdims=True)


def _lane_place(cols):
    rows = cols[0].shape[0]
    lane = lax.broadcasted_iota(jnp.int32, (rows, HEAD), 1)
    tile = jnp.zeros((rows, HEAD), F32)
    for hh, v in enumerate(cols):
        tile = jnp.where(lane == hh, v, tile)
    return tile


def _band_masks():
    qi = lax.broadcasted_iota(jnp.int32, (ATTN_SPAN, ATTN_SPAN), 0)
    kj = lax.broadcasted_iota(jnp.int32, (ATTN_SPAN, ATTN_SPAN), 1)
    return kj <= qi, kj >= qi


ATTN_TILE_BLOCKS = 4


def _attn_fwd(a):
    d, L, _ = a.shape
    B, W = ATTN_TILE_BLOCKS, ATTN_SPAN
    T = B * W
    assert L % T == 0
    steps = L // T
    scale = HEAD ** -0.5

    def body(q_ref, kc_ref, kp_ref, vc_ref, vp_ref, o_ref, lse_ref):
        n = pl.program_id(1)
        mask_c, mask_p0 = _band_masks()
        first = jnp.logical_and(mask_p0, n > 0)
        units = [(b, hh) for b in range(B) for hh in range(HEADS_PER_GROUP)]
        rows = [slice(b * W, (b + 1) * W) for b in range(B)]
        cols = [slice(hh * HEAD, (hh + 1) * HEAD) for hh in range(HEADS_PER_GROUP)]

        def prev_keys(ref, tile, b, hh):
            return ref[:, cols[hh]] if b == 0 else tile[rows[b - 1], cols[hh]]

        s_c = [jnp.where(mask_c, _dot_nt(q_ref[rows[b], cols[hh]], kc_ref[rows[b], cols[hh]]) * scale, NEG) for b, hh in units]
        s_p = [jnp.where(first if b == 0 else mask_p0,
                         _dot_nt(q_ref[rows[b], cols[hh]], prev_keys(kp_ref, kc_ref, b, hh)) * scale, NEG) for b, hh in units]
        m = [jnp.maximum(jnp.max(x, axis=-1, keepdims=True), jnp.max(y, axis=-1, keepdims=True)) for x, y in zip(s_c, s_p)]
        p_c = [jnp.exp(x - mm) for x, mm in zip(s_c, m)]
        p_p = [jnp.exp(y - mm) for y, mm in zip(s_p, m)]
        l = [jnp.sum(x, axis=-1, keepdims=True) + jnp.sum(y, axis=-1, keepdims=True) for x, y in zip(p_c, p_p)]
        acc = [_dot(p_c[i].astype(BF16), vc_ref[rows[b], cols[hh]]) + _dot(p_p[i].astype(BF16), prev_keys(vp_ref, vc_ref, b, hh))
               for i, (b, hh) in enumerate(units)]
        for i, (b, hh) in enumerate(units):
            o_ref[rows[b], cols[hh]] = acc[i] / l[i]
        for b in range(B):
            lse_ref[rows[b], :] = _lane_place([m[i] + jnp.log(l[i]) for i, (bb, _) in enumerate(units) if bb == b])

    def cur(part):
        return pl.BlockSpec((None, T, GROUP_W), functools.partial(lambda r, n, p: (r, n, p), p=part))

    def prev(part):
        return pl.BlockSpec((None, W, GROUP_W), functools.partial(lambda r, n, p: (r, jnp.maximum(n * B - 1, 0), p), p=part))

    return pl.pallas_call(
        body, name=f"attn_fwd_d{d}", grid=(d, steps),
        in_specs=[cur(0), cur(1), prev(1), cur(2), prev(2)],
        out_specs=[pl.BlockSpec((None, T, GROUP_W), lambda r, n: (r, n, 0)), pl.BlockSpec((None, T, HEAD), lambda r, n: (r, n, 0))],
        out_shape=[jax.ShapeDtypeStruct((d, L, GROUP_W), F32), jax.ShapeDtypeStruct((d, L, HEAD), F32)],
        compiler_params=_params(("parallel", "arbitrary")))(a, a, a, a, a)


def _attn_bwd(a, do, lse, dd):
    d, L, _ = a.shape
    B, W = ATTN_TILE_BLOCKS, ATTN_SPAN
    T = B * W
    assert L % T == 0
    steps = L // T
    scale = HEAD ** -0.5

    def body(qc_ref, qn_ref, kp_ref, kc_ref, vp_ref, vc_ref, doc_ref, don_ref, lc_ref, ln_ref, ddc_ref, ddn_ref, da_ref):
        n = pl.program_id(1)
        mask_c, mask_p0 = _band_masks()
        first = jnp.logical_and(mask_p0, n > 0)
        last = jnp.logical_and(mask_p0, n < steps - 1)
        H4 = range(HEADS_PER_GROUP)
        units = [(b, hh) for b in range(B) for hh in H4]
        rows = [slice(b * W, (b + 1) * W) for b in range(B)]
        cols = [slice(hh * HEAD, (hh + 1) * HEAD) for hh in H4]
        q = {u: qc_ref[rows[u[0]], cols[u[1]]] for u in units}
        k = {u: kc_ref[rows[u[0]], cols[u[1]]] for u in units}
        v = {u: vc_ref[rows[u[0]], cols[u[1]]] for u in units}
        g_o = {u: doc_ref[rows[u[0]], cols[u[1]]] for u in units}
        kb = {(b, hh): kp_ref[:, cols[hh]] if b == 0 else k[(b - 1, hh)] for b, hh in units}
        vb = {(b, hh): vp_ref[:, cols[hh]] if b == 0 else v[(b - 1, hh)] for b, hh in units}
        lse_t = {(b, hh): _lane_pick(lc_ref[rows[b], :], hh) for b, hh in units}
        dd_t = {(b, hh): _lane_pick(ddc_ref[rows[b], :], hh) for b, hh in units}
        p_c = {u: jnp.where(mask_c, jnp.exp(_dot_nt(q[u], k[u]) * scale - lse_t[u]), 0.0) for u in units}
        p_p = {u: jnp.where(first if u[0] == 0 else mask_p0, jnp.exp(_dot_nt(q[u], kb[u]) * scale - lse_t[u]), 0.0) for u in units}
        ds_c = {u: (p_c[u] * (_dot_nt(g_o[u], v[u]) + dd_t[u])).astype(BF16) for u in units}
        ds_p = {u: (p_p[u] * (_dot_nt(g_o[u], vb[u]) + dd_t[u])).astype(BF16) for u in units}
        qn = [qn_ref[:, c] for c in cols]
        g_n = [don_ref[:, c] for c in cols]
        p_n = [jnp.where(last, jnp.exp(_dot_nt(qn[hh], k[(B - 1, hh)]) * scale - _lane_pick(ln_ref[...], hh)), 0.0) for hh in H4]
        ds_n = [(p_n[hh] * (_dot_nt(g_n[hh], v[(B - 1, hh)]) + _lane_pick(ddn_ref[...], hh))).astype(BF16) for hh in H4]
        dq = {u: (_dot(ds_c[u], k[u]) + _dot(ds_p[u], kb[u])) * scale for u in units}
        dk, dv = {}, {}
        for b, hh in units:
            if b < B - 1:
                nxt = (b + 1, hh)
                dk[(b, hh)] = (_dot_tn(ds_c[(b, hh)], q[(b, hh)]) + _dot_tn(ds_p[nxt], q[nxt])) * scale
                dv[(b, hh)] = _dot_tn(p_c[(b, hh)].astype(BF16), g_o[(b, hh)]) + _dot_tn(p_p[nxt].astype(BF16), g_o[nxt])
            else:
                dk[(b, hh)] = (_dot_tn(ds_c[(b, hh)], q[(b, hh)]) + _dot_tn(ds_n[hh], qn[hh])) * scale
                dv[(b, hh)] = _dot_tn(p_c[(b, hh)].astype(BF16), g_o[(b, hh)]) + _dot_tn(p_n[hh].astype(BF16), g_n[hh])
        for b, hh in units:
            da_ref[rows[b], cols[hh]] = dq[(b, hh)].astype(BF16)
            da_ref[rows[b], GROUP_W + hh * HEAD:GROUP_W + (hh + 1) * HEAD] = dk[(b, hh)].astype(BF16)
            da_ref[rows[b], 2 * GROUP_W + hh * HEAD:2 * GROUP_W + (hh + 1) * HEAD] = dv[(b, hh)].astype(BF16)

    nb = L // W

    def cur(width, part):
        return pl.BlockSpec((None, T, width), functools.partial(lambda r, n, p: (r, n, p), p=part))

    def prev(width, part):
        return pl.BlockSpec((None, W, width), functools.partial(lambda r, n, p: (r, jnp.maximum(n * B - 1, 0), p), p=part))

    def nxt(width, part):
        return pl.BlockSpec((None, W, width), functools.partial(lambda r, n, p: (r, jnp.minimum(n * B + B, nb - 1), p), p=part))

    g = GROUP_W
    return pl.pallas_call(
        body, name=f"attn_bwd_d{d}", grid=(d, steps),
        in_specs=[cur(g, 0), nxt(g, 0), prev(g, 1), cur(g, 1), prev(g, 2), cur(g, 2),
                  cur(g, 0), nxt(g, 0), cur(HEAD, 0), nxt(HEAD, 0), cur(HEAD, 0), nxt(HEAD, 0)],
        out_specs=pl.BlockSpec((None, T, 3 * g), lambda r, n: (r, n, 0)),
        out_shape=jax.ShapeDtypeStruct((d, L, 3 * g), BF16),
        compiler_params=_params(("parallel", "arbitrary")))(
            a, a, a, a, a, a, do, do, lse, lse, dd, dd)


def _softmax3(ls):
    mx = jnp.maximum(jnp.maximum(ls[0], ls[1]), ls[2])
    es = [jnp.exp(v - mx) for v in ls]
    tot = es[0] + es[1] + es[2]
    return [e / tot for e in es]


HEAD_COLS = [slice(hh * HEAD, (hh + 1) * HEAD) for hh in range(HEADS_PER_GROUP)]


def _group_spec(d, tm):
    return pl.BlockSpec((d, tm // d, GROUP_W), lambda i: (0, i, 0))


def _gather_heads(ref, scr, d, tm):
    if d == 1:
        return [ref[0, :, cols].astype(F32) for cols in HEAD_COLS]
    for hh, cols in enumerate(HEAD_COLS):
        for r in range(d):
            scr.at[hh][pl.ds(r, tm // d, stride=d), :] = ref[r, :, cols].astype(F32)
    return [scr[hh] for hh in range(HEADS_PER_GROUP)]


def _tile_spec(d, tm):
    return pl.BlockSpec((d, tm // d, HEAD), lambda i: (0, i, 0))


def _gather_tile(ref, scr, d, tm):
    if d == 1:
        return ref[0]
    for r in range(d):
        scr[pl.ds(r, tm // d, stride=d), :] = ref[r]
    return scr[...]


def _scatter_tile(val, scr, ref, d, tm):
    if d == 1:
        ref[0] = val
        return
    scr[...] = val
    for r in range(d):
        ref[r] = scr[pl.ds(r, tm // d, stride=d), :]


def _scatter_heads(vals, scr, ref, d, tm):
    if d == 1:
        for cols, v in zip(HEAD_COLS, vals):
            ref[0, :, cols] = v.astype(ref.dtype)
        return
    for hh, v in enumerate(vals):
        scr[hh] = v
    for hh, cols in enumerate(HEAD_COLS):
        for r in range(d):
            ref[r, :, cols] = scr.at[hh][pl.ds(r, tm // d, stride=d), :].astype(ref.dtype)


def _qkv_dilated(h, gain, wg, cos, sin, d, tm=512):
    S, K = h.shape

    def body(h_ref, g_ref, w_ref, cos_ref, sin_ref, out_ref, u_ref, y_scr):
        p = pl.program_id(1)

        @pl.when(p == 0)
        def _():
            v = h_ref[...]
            u_ref[...] = (v * _rstd(v) * g_ref[...]).astype(BF16)

        y = _dot(u_ref[...], w_ref[...])
        heads = [slice(hh * HEAD, (hh + 1) * HEAD) for hh in range(HEADS_PER_GROUP)]
        for hh, cols in enumerate(heads):
            y_scr[hh] = y[:, cols]

        @pl.when(p < 2)
        def _():
            for r in range(d):
                rows = slice(None) if d == 1 else pl.ds(r, tm // d, stride=d)
                cr, sr = cos_ref[rows, :], sin_ref[rows, :]
                for hh, cols in enumerate(heads):
                    out_ref[r, :, cols] = _rope(y_scr.at[hh][rows, :], cr, sr).astype(BF16)

        @pl.when(p == 2)
        def _():
            for r in range(d):
                rows = slice(None) if d == 1 else pl.ds(r, tm // d, stride=d)
                for hh, cols in enumerate(heads):
                    out_ref[r, :, cols] = y_scr.at[hh][rows, :].astype(BF16)

    tab = pl.BlockSpec((tm, HEAD), lambda i, p: (i, 0))
    return pl.pallas_call(
        body, name=f"attn_qkv_d{d}", grid=(S // tm, 3),
        in_specs=[pl.BlockSpec((tm, K), lambda i, p: (i, 0)),
                  pl.BlockSpec((1, K), lambda i, p: (0, 0)),
                  pl.BlockSpec((K, GROUP_W), lambda i, p: (0, p)), tab, tab],
        out_specs=[pl.BlockSpec((d, tm // d, GROUP_W), lambda i, p: (0, i, p)), pl.BlockSpec((tm, K), lambda i, p: (i, 0))],
        out_shape=[jax.ShapeDtypeStruct((d, S // d, 3 * GROUP_W), BF16), jax.ShapeDtypeStruct((S, K), BF16)],
        scratch_shapes=[pltpu.VMEM((HEADS_PER_GROUP, tm, HEAD), F32)],
        compiler_params=_params(("parallel", "arbitrary")))(h, gain, wg, cos, sin)


def _undilate_group(da, dqkv, cos, sin, g, tm=512):
    d, L, _ = da.shape
    S = d * L
    G = len(ATTN_GROUPS)

    def body(*refs):
        da_ref, cos_ref, sin_ref, out_ref, scr = refs[0], refs[1], refs[2], refs[-2], refs[-1]
        p = pl.program_id(1)
        heads = [slice(hh * HEAD, (hh + 1) * HEAD) for hh in range(HEADS_PER_GROUP)]
        for hh, cols in enumerate(heads):
            if d == 1:
                scr[hh] = da_ref[0, :, cols].astype(F32)
            else:
                for r in range(d):
                    scr.at[hh][pl.ds(r, tm // d, stride=d), :] = da_ref[r, :, cols].astype(F32)

        @pl.when(p < 2)
        def _():
            cr, sr = cos_ref[...], -sin_ref[...]
            for hh, cols in enumerate(heads):
                out_ref[:, cols] = _rope(scr[hh], cr, sr).astype(BF16)

        @pl.when(p == 2)
        def _():
            for hh, cols in enumerate(heads):
                out_ref[:, cols] = scr[hh].astype(BF16)

    tab = pl.BlockSpec((tm, HEAD), lambda i, p: (i, 0))
    operands = (da, cos, sin) if dqkv is None else (da, cos, sin, dqkv)
    return pl.pallas_call(
        body, name=f"attn_undilate_d{d}", grid=(S // tm, 3),
        in_specs=[pl.BlockSpec((d, tm // d, GROUP_W), lambda i, p: (0, i, p)), tab, tab] + ([] if dqkv is None else [ANY]),
        out_specs=pl.BlockSpec((tm, GROUP_W), lambda i, p: (i, p * G + g)),
        out_shape=jax.ShapeDtypeStruct((S, 3 * G * GROUP_W), BF16),
        input_output_aliases={} if dqkv is None else {3: 0},
        scratch_shapes=[pltpu.VMEM((HEADS_PER_GROUP, tm, HEAD), F32)],
        compiler_params=_params(("parallel", "arbitrary")))(*operands)


def _attn_merge(os_, lses, tm=512):
    G = len(os_)
    S = os_[0].shape[0] * os_[0].shape[1]

    def body(*refs):
        o_refs, l_refs, out_ref = refs[:G], refs[G:2 * G], refs[2 * G]
        scr = refs[2 * G + 1:]
        o = [_gather_heads(o_refs[g], scr[g], d, tm) for g, (_, d) in enumerate(ATTN_GROUPS)]
        l = [_gather_tile(l_refs[g], scr[G + g].at[0], d, tm) for g, (_, d) in enumerate(ATTN_GROUPS)]
        for hh in range(HEADS_PER_GROUP):
            al = _softmax3([_lane_pick(l[g], hh) for g in range(G)])
            for g in range(G):
                out_ref[:, g * GROUP_W + hh * HEAD:g * GROUP_W + (hh + 1) * HEAD] = (o[g][hh] * al[g]).astype(BF16)

    specs = [_group_spec(d, tm) for _, d in ATTN_GROUPS]
    return pl.pallas_call(
        body, name="attn_merge", grid=(S // tm,),
        in_specs=specs + [_tile_spec(d, tm) for _, d in ATTN_GROUPS],
        out_specs=pl.BlockSpec((tm, G * GROUP_W), lambda i: (i, 0)),
        out_shape=jax.ShapeDtypeStruct((S, G * GROUP_W), BF16),
        scratch_shapes=[pltpu.VMEM((HEADS_PER_GROUP, tm, HEAD), F32)] * (2 * G),
        compiler_params=_params(("parallel",)))(*os_, *lses)


def _attn_merge_bwd(os_, lses, doa, tm=512):
    G = len(os_)
    S = doa.shape[0]

    def body(*refs):
        o_refs, l_refs, doa_ref = refs[:G], refs[G:2 * G], refs[2 * G]
        do_refs, dd_refs = refs[2 * G + 1:3 * G + 1], refs[3 * G + 1:4 * G + 1]
        scr = refs[4 * G + 1:]
        o = [_gather_heads(o_refs[g], scr[g], d, tm) for g, (_, d) in enumerate(ATTN_GROUPS)]
        l = [_gather_tile(l_refs[g], scr[G + g].at[0], d, tm) for g, (_, d) in enumerate(ATTN_GROUPS)]
        do = [[None] * HEADS_PER_GROUP for _ in range(G)]
        dd = [[None] * HEADS_PER_GROUP for _ in range(G)]
        for hh in range(HEADS_PER_GROUP):
            al = _softmax3([_lane_pick(l[g], hh) for g in range(G)])
            mix = None
            for g in range(G):
                dg = doa_ref[:, g * GROUP_W + hh * HEAD:g * GROUP_W + (hh + 1) * HEAD]
                do[g][hh] = dg * al[g]
                t = al[g] * jnp.sum(dg * o[g][hh], axis=-1, keepdims=True)
                mix = t if mix is None else mix + t
            for g in range(G):
                dd[g][hh] = -al[g] * mix
        for g, (_, d) in enumerate(ATTN_GROUPS):
            _scatter_heads(do[g], scr[2 * G + g], do_refs[g], d, tm)
            _scatter_tile(_lane_place(dd[g]), scr[3 * G + g].at[0], dd_refs[g], d, tm)

    specs = [_group_spec(d, tm) for _, d in ATTN_GROUPS]
    tiles = [_tile_spec(d, tm) for _, d in ATTN_GROUPS]
    do_shapes = [jax.ShapeDtypeStruct((d, S // d, GROUP_W), BF16) for _, d in ATTN_GROUPS]
    dd_shapes = [jax.ShapeDtypeStruct((d, S // d, HEAD), F32) for _, d in ATTN_GROUPS]
    return pl.pallas_call(
        body, name="attn_merge_bwd", grid=(S // tm,),
        in_specs=specs + tiles + [pl.BlockSpec((tm, G * GROUP_W), lambda i: (i, 0))],
        out_specs=specs + tiles,
        out_shape=do_shapes + dd_shapes,
        scratch_shapes=[pltpu.VMEM((HEADS_PER_GROUP, tm, HEAD), F32)] * (4 * G),
        compiler_params=_params(("parallel",)))(*os_, *lses, doa)


def _rope_tables(S):
    inv_freq = 1.0 / (ROPE_THETA ** (jnp.arange(0, HEAD, 2, dtype=F32) / HEAD))
    ang = jnp.arange(S, dtype=F32)[:, None] * inv_freq[None, :]
    cos, sin = jnp.cos(ang), jnp.sin(ang)
    return jnp.concatenate([cos, cos], axis=-1), jnp.concatenate([-sin, sin], axis=-1)


def _local_step(x, target, norm_mix, norm_ffn, lb_logits, out_gain, final_norm, comm):
    S = x.shape[0]
    nm0, nm1 = norm_mix[0:1], norm_mix[1:2]
    nf0, nf1 = norm_ffn[0:1], norm_ffn[1:2]
    w = comm.first_weights()

    proj, u0, got = _norm_mm(x, nm0, w["hin"], "hgrn_in", rider=comm.gather_rider(LATE_WEIGHTS_A))
    w.update(comm.gathered(LATE_WEIGHTS_A, got))
    (o, og, states), got = _hgrn_fwd(proj, lb_logits, out_gain, rider=comm.gather_rider(LATE_WEIGHTS_B))
    w.update(comm.gathered(LATE_WEIGHTS_B, got))
    fin_tn = w["fin0"].shape[2]
    h1 = _mm_res(x, og, w["hout"], "hgrn_out")
    z0, u1, _ = _norm_mm(h1, nf0, w["fin0"], "ffn0_in", out_dtype=BF16)
    h2, act0 = _swiglu_mm_res(h1, z0, w["fdn0"], "ffn0_down")
    cos, sin = _rope_tables(S)
    G = len(ATTN_GROUPS)
    w_groups = w["qkv"].transpose(1, 0, 2).reshape(D_MODEL, 3, G, GROUP_W)
    a_g, u2 = zip(*[_qkv_dilated(h2, nm1, w_groups[:, :, gi, :].reshape(D_MODEL, 3 * GROUP_W), cos, sin, d)
                    for gi, (_, d) in enumerate(ATTN_GROUPS)])
    o_g, lse_g = zip(*[_attn_fwd(a) for a in a_g])
    oa = _attn_merge(o_g, lse_g)
    h3 = _mm_res(h2, oa, w["aout"], "attn_out")
    z1, u3, _ = _norm_mm(h3, nf1, w["fin1"], "ffn1_in", out_dtype=BF16)
    h4, act1 = _swiglu_mm_res(h3, z1, w["fdn1"], "ffn1_down")
    dh4, loss, d_final = _loss_head(h4, final_norm, target)

    grads, small = {}, {"final_norm": d_final}

    def ffn_bwd(dh, h_in, u_in, z, act, gain, w_in, w_dn, tag, ride=None):
        dz = _mm_nt_swiglu_bwd(dh, w_dn, z, tag + "_down_dx")
        g_dn = _mm_tn(act, dh, 1, D_MODEL, D_MODEL, tag + "_down_dw")[0]
        g_in = _mm_tn(u_in, dz, N_CHIPS, fin_tn, fin_tn, tag + "_in_dw")
        rider = None if ride is None else ride(g_in, g_dn)
        dh_in, dgain, got = _mm_nt_normbwd(dz, w_in, h_in, gain, dh, tag + "_in_dx", rider=rider)
        return dh_in, dgain, g_in, g_dn, got

    dh3, d_nf1, grads["fin1"], grads["fdn1"], _ = ffn_bwd(dh4, h3, u3, z1, act1, nf1, w["fin1"], w["fdn1"], "ffn1")
    doa = _mm_nt(dh3, w["aout"][None], "attn_out_dx")
    grads["aout"] = _mm_tn(oa, dh3, 1, D_MODEL, D_MODEL, "attn_out_dw")[0]
    merged = _attn_merge_bwd(o_g, lse_g, doa)
    G = len(ATTN_GROUPS)
    das = [_attn_bwd(a_g[gi], merged[gi], lse_g[gi], merged[G + gi]) for gi in range(G)]
    dqkv = None
    for gi in range(G):
        dqkv = _undilate_group(das[gi], dqkv, cos, sin, gi)
    n_qkv = w["qkv"].shape[2]
    grads["qkv"] = _mm_tn(u2[0], dqkv, N_CHIPS, n_qkv, n_qkv, "attn_qkv_dw")
    dh2, d_nm1, _ = _mm_nt_normbwd(dqkv, w["qkv"], h2, nm1, dh3, "attn_qkv_dx")

    def ride_early(g_in, g_dn):
        return comm.pair_rider({**grads, "fin0": g_in, "fdn0": g_dn}, "early")

    dh1, d_nf0, _, _, got = ffn_bwd(dh2, h1, u1, z0, act0, nf0, w["fin0"], w["fdn0"], "ffn0", ride=ride_early)
    comm.paired("early", got)
    dog = _mm_nt(dh1, w["hout"][None], "hgrn_out_dx")
    (dproj, dlb, dgn), got = _hgrn_bwd(proj, lb_logits, out_gain, o, states, dog, rider=comm.exchange_rider("early"))
    comm.exchanged("early", got)
    late = {"hout": _mm_tn(og, dh1, 1, D_MODEL, D_MODEL, "hgrn_out_dw")[0],
            "hin": _mm_tn(u0, dproj, N_CHIPS, D_MODEL, D_MODEL, "hgrn_in_dw")}
    comm.pair_now(late, "late")
    dx, d_nm0, got = _mm_nt_normbwd(dproj, w["hin"], x, nm0, dh1, "hgrn_in_dx", rider=comm.exchange_rider("late"))
    comm.exchanged("late", got)

    small["norm_mix"] = jnp.concatenate([d_nm0, d_nm1], axis=0)
    small["norm_ffn"] = jnp.concatenate([d_nf0, d_nf1], axis=0)
    small["lb"] = dlb.reshape(1, HGRN_HEADS * HEAD)
    small["out_norm"] = dgn.reshape(HGRN_HEADS, HEAD)
    return loss, dx, small


def _place():
    x, y, c = lax.axis_index("x"), lax.axis_index("y"), lax.axis_index("c")
    others = [(1 - x, y), (x, 1 - y), (1 - x, 1 - y)]
    return x, y, c, others


ANY = pl.BlockSpec(memory_space=pl.ANY)


class _GatherRider:
    def __init__(self, shards):
        self.operands = list(shards)
        n = self.n = len(shards)
        self.out_shape = [jax.ShapeDtypeStruct((N_CHIPS,) + s.shape, s.dtype) for s in shards]
        self.scratch = [pltpu.SemaphoreType.DMA((3 * n,)), pltpu.SemaphoreType.DMA((3 * n,)),
                        pltpu.SemaphoreType.DMA((3 * n,)), pltpu.SemaphoreType.DMA((3 * n,)),
                        pltpu.SemaphoreType.DMA((n,)), pltpu.SemaphoreType.DMA((n,))]

    def _copies(self, ins, outs, sems):
        ici_send, ici_recv, _, _, own_send, own_recv = sems
        x, y, c, others = _place()
        me = 2 * x + y
        own = [pltpu.make_async_remote_copy(
            src_ref=ins[a], dst_ref=outs[a].at[me], send_sem=own_send.at[a], recv_sem=own_recv.at[a],
            device_id=(x, y, 1 - c), device_id_type=MESH) for a in range(self.n)]
        sends = [pltpu.make_async_remote_copy(
            src_ref=ins[a].at[c], dst_ref=outs[a].at[me, c], send_sem=ici_send.at[a * 3 + k], recv_sem=ici_recv.at[a * 3 + k],
            device_id=(ox, oy, c), device_id_type=MESH) for a in range(self.n) for k, (ox, oy) in enumerate(others)]
        return own, sends

    def start(self, ins, outs, sems):
        own, sends = self._copies(ins, outs, sems)
        for cp in own + sends:
            cp.start()

    def finish(self, ins, outs, sems):
        ici_send, ici_recv, d2d_send, d2d_recv, _, _ = sems
        x, y, c, others = _place()
        sibling = (x, y, 1 - c)
        own, sends = self._copies(ins, outs, sems)
        passes = []
        for a in range(self.n):
            for k, (ox, oy) in enumerate(others):
                s = a * 3 + k
                got = outs[a].at[2 * ox + oy, c]
                pltpu.make_async_remote_copy(
                    src_ref=got, dst_ref=got, send_sem=ici_send.at[s], recv_sem=ici_recv.at[s],
                    device_id=(ox, oy, c), device_id_type=MESH).wait_recv()
                fwd = pltpu.make_async_remote_copy(
                    src_ref=got, dst_ref=got, send_sem=d2d_send.at[s], recv_sem=d2d_recv.at[s],
                    device_id=sibling, device_id_type=MESH)
                fwd.start()
                passes.append(fwd)
        for a in range(self.n):
            for k, (ox, oy) in enumerate(others):
                s = a * 3 + k
                theirs = outs[a].at[2 * ox + oy, 1 - c]
                pltpu.make_async_remote_copy(
                    src_ref=theirs, dst_ref=theirs, send_sem=d2d_send.at[s], recv_sem=d2d_recv.at[s],
                    device_id=sibling, device_id_type=MESH).wait_recv()
        for cp in own:
            cp.wait()
        for cp in sends + passes:
            cp.wait_send()


class _PairRider:
    def __init__(self, grads):
        self.operands = list(grads)
        n = self.n = len(grads)
        self.out_shape = [jax.ShapeDtypeStruct((N_CHIPS,) + g.shape[2:], F32) for g in grads]
        self.scratch = [pltpu.SemaphoreType.DMA((N_CHIPS * n,)), pltpu.SemaphoreType.DMA((N_CHIPS * n,))]

    def _copies(self, ins, outs, sems):
        send_sem, recv_sem = sems
        x, y, c, _ = _place()
        return [pltpu.make_async_remote_copy(
            src_ref=ins[a].at[j, 1 - c], dst_ref=outs[a].at[j], send_sem=send_sem.at[a * N_CHIPS + j],
            recv_sem=recv_sem.at[a * N_CHIPS + j], device_id=(x, y, 1 - c), device_id_type=MESH)
            for a in range(self.n) for j in range(N_CHIPS)]

    def start(self, ins, outs, sems):
        for cp in self._copies(ins, outs, sems):
            cp.start()

    def finish(self, ins, outs, sems):
        for cp in self._copies(ins, outs, sems):
            cp.wait()


class _ExchangeRider:
    def __init__(self, parts):
        self.operands = list(parts)
        n = self.n = len(parts)
        self.out_shape = [jax.ShapeDtypeStruct(p.shape, p.dtype) for p in parts]
        self.scratch = [pltpu.SemaphoreType.DMA((3 * n,)), pltpu.SemaphoreType.DMA((3 * n,))]

    def _copies(self, ins, outs, sems):
        send_sem, recv_sem = sems
        x, y, c, others = _place()
        me = 2 * x + y
        return [pltpu.make_async_remote_copy(
            src_ref=ins[a].at[2 * ox + oy], dst_ref=outs[a].at[me], send_sem=send_sem.at[a * 3 + k],
            recv_sem=recv_sem.at[a * 3 + k], device_id=(ox, oy, c), device_id_type=MESH)
            for a in range(self.n) for k, (ox, oy) in enumerate(others)]

    def start(self, ins, outs, sems):
        for cp in self._copies(ins, outs, sems):
            cp.start()

    def finish(self, ins, outs, sems):
        send_sem, recv_sem = sems
        x, y, c, others = _place()
        for a in range(self.n):
            for k, (ox, oy) in enumerate(others):
                s = a * 3 + k
                got = outs[a].at[2 * ox + oy]
                pltpu.make_async_remote_copy(
                    src_ref=got, dst_ref=got, send_sem=send_sem.at[s], recv_sem=recv_sem.at[s],
                    device_id=(ox, oy, c), device_id_type=MESH).wait_recv()
        for cp in self._copies(ins, outs, sems):
            cp.wait_send()


def _run_rider(rider, name):
    n = rider.n

    def body(*refs):
        ins, outs, sems = refs[:n], refs[n:2 * n], refs[2 * n:]
        rider.start(ins, outs, sems)
        rider.finish(ins, outs, sems)

    return pl.pallas_call(
        body, name=name, in_specs=[ANY] * n, out_specs=[ANY] * n,
        out_shape=rider.out_shape, scratch_shapes=rider.scratch)(*rider.operands)


def _ride(rider, body, n_in, n_out, first, last):
    if rider is None:
        return body
    n = rider.n

    def wrapped(*refs):
        host_in, r_in = refs[:n_in], refs[n_in:n_in + n]
        host_out = refs[n_in + n:n_in + n + n_out]
        r_out = refs[n_in + n + n_out:n_in + 2 * n + n_out]
        rest = refs[n_in + 2 * n + n_out:]
        host_scr, sems = rest[:len(rest) - len(rider.scratch)], rest[len(rest) - len(rider.scratch):]

        @pl.when(first())
        def _():
            rider.start(r_in, r_out, sems)

        body(*host_in, *host_out, *host_scr)

        @pl.when(last())
        def _():
            rider.finish(r_in, r_out, sems)

    return wrapped


def _rider_args(rider):
    if rider is None:
        return [], [], [], [], []
    return rider.operands, [ANY] * rider.n, [ANY] * rider.n, rider.out_shape, rider.scratch


def _pair_sum(g, got, c_idx):
    _, _, r, cw = g.shape
    tr = _row_tile(r, cw)

    def body(c_ref, g_ref, got_ref, p_ref, pb_ref):
        v = g_ref[...] + got_ref[...]
        p_ref[...] = v
        pb_ref[...] = v.astype(BF16)

    blk = pl.BlockSpec((None, tr, cw), lambda j, i, c_ref: (j, i, 0))
    return pl.pallas_call(
        body, name="grad_pair_sum",
        grid_spec=pltpu.PrefetchScalarGridSpec(
            num_scalar_prefetch=1, grid=(N_CHIPS, r // tr),
            in_specs=[pl.BlockSpec((None, None, tr, cw), lambda j, i, c_ref: (j, c_ref[0], i, 0)), blk],
            out_specs=[blk, blk]),
        out_shape=[jax.ShapeDtypeStruct((N_CHIPS, r, cw), F32), jax.ShapeDtypeStruct((N_CHIPS, r, cw), BF16)],
        compiler_params=_params(("parallel", "parallel")))(c_idx, g, got)


def _chip_sum(p, got, me_idx):
    _, r, cw = p.shape
    tr = _row_tile(r, cw)

    def body(me_ref, own_ref, got_ref, t_ref):
        me = me_ref[0]
        acc = None
        for s in range(N_CHIPS):
            term = jnp.where(me == s, own_ref[...], got_ref[s].astype(F32))
            acc = term if acc is None else acc + term
        t_ref[...] = acc

    return pl.pallas_call(
        body, name="grad_chip_sum",
        grid_spec=pltpu.PrefetchScalarGridSpec(
            num_scalar_prefetch=1, grid=(r // tr,),
            in_specs=[pl.BlockSpec((None, tr, cw), lambda i, me_ref: (me_ref[0], i, 0)),
                      pl.BlockSpec((N_CHIPS, tr, cw), lambda i, me_ref: (0, i, 0))],
            out_specs=pl.BlockSpec((tr, cw), lambda i, me_ref: (i, 0))),
        out_shape=jax.ShapeDtypeStruct((r, cw), F32),
        compiler_params=_params(("parallel",)))(me_idx, p, got)


def _pair_share(halves):
    n = len(halves)

    def body(*refs):
        ins, outs = refs[:n], refs[n:2 * n]
        send_sem, recv_sem = refs[2 * n:]
        x, y, c, _ = _place()
        cps = [pltpu.make_async_remote_copy(
            src_ref=ins[a], dst_ref=outs[a], send_sem=send_sem.at[a], recv_sem=recv_sem.at[a],
            device_id=(x, y, 1 - c), device_id_type=MESH) for a in range(n)]
        for cp in cps:
            cp.start()
        for cp in cps:
            cp.wait()

    return pl.pallas_call(
        body, name="grad_pair_share",
        in_specs=[ANY] * n, out_specs=[ANY] * n,
        out_shape=[jax.ShapeDtypeStruct(h.shape, F32) for h in halves],
        scratch_shapes=[pltpu.SemaphoreType.DMA((n,)), pltpu.SemaphoreType.DMA((n,))],
        )(*halves)


def _small_allreduce(pack):
    m_per, ncol = pack.shape
    n_dev = 8

    def body(x_ref, sum_ref, all_ref, send_sems, recv_sems, local_sem):
        x, y, c, others = _place()
        me, sibling = (x, y, c), (x, y, 1 - c)

        def rows(px, py, pc):
            return all_ref.at[pl.ds((4 * px + 2 * py + pc) * m_per, m_per), :]

        def copy(k, block, to, src=None):
            return pltpu.make_async_remote_copy(
                src_ref=rows(*block) if src is None else src, dst_ref=rows(*block),
                send_sem=send_sems.at[k], recv_sem=recv_sems.at[k], device_id=to, device_id_type=MESH)

        mine = pltpu.make_async_copy(x_ref, rows(*me), local_sem)
        mine.start()
        first = [copy(0, me, sibling, src=x_ref)]
        first += [copy(1 + j, me, (*chip, c), src=x_ref) for j, chip in enumerate(others)]
        for cp in first:
            cp.start()
        passed = [copy(4 + j, (*chip, c), sibling) for j, chip in enumerate(others)]
        for j, chip in enumerate(others):
            copy(1 + j, (*chip, c), me).wait_recv()
            passed[j].start()
        copy(0, sibling, me).wait_recv()
        for j, chip in enumerate(others):
            copy(4 + j, (*chip, 1 - c), me).wait_recv()
        for cp in first + passed:
            cp.wait_send()
        mine.wait()
        acc = all_ref[0:m_per, :]
        for dvc in range(1, n_dev):
            acc = acc + all_ref[dvc * m_per:(dvc + 1) * m_per, :]
        sum_ref[...] = acc

    return pl.pallas_call(
        body, name="small_allreduce",
        in_specs=[pl.BlockSpec(memory_space=pltpu.VMEM)],
        out_specs=pl.BlockSpec(memory_space=pltpu.VMEM),
        out_shape=jax.ShapeDtypeStruct((m_per, ncol), F32),
        scratch_shapes=[pltpu.VMEM((n_dev * m_per, ncol), F32),
                        pltpu.SemaphoreType.DMA((7,)), pltpu.SemaphoreType.DMA((7,)), pltpu.SemaphoreType.DMA],
        )(pack)


def _adam_math(w, g, m, v):
    m = ADAM_B1 * m + (1.0 - ADAM_B1) * g
    v = ADAM_B2 * v + (1.0 - ADAM_B2) * (g * g)
    m_hat = m / (1.0 - ADAM_B1 ** ADAM_STEP)
    v_hat = v / (1.0 - ADAM_B2 ** ADAM_STEP)
    delta = -ADAM_LR * (m_hat / (jnp.sqrt(v_hat) + ADAM_EPS) + ADAM_WD * w)
    return delta, m, v


def _adamw(halves, c_idx, w, m, v, name):
    L = len(halves)
    r, C = halves[0][0].shape
    tr = _row_tile(r, C, 512 * 1024)
    nt = r // tr

    def body(c_ref, *refs):
        g_refs, (w_ref, m_ref, v_ref), (g_ref, d_ref, nm_ref, nv_ref) = refs[:2 * L], refs[2 * L:2 * L + 3], refs[2 * L + 3:]
        own = pl.program_id(1) == c_ref[0]
        g = None
        for l in range(L):
            cand = jnp.where(own, g_refs[2 * l][...], g_refs[2 * l + 1][...])
            g = cand if g is None else jnp.where(pl.program_id(0) == l, cand, g)
        g_ref[...] = g
        d_ref[...], nm_ref[...], nv_ref[...] = _adam_math(w_ref[...], g, m_ref[...], v_ref[...])

    def half(l, mine):
        def index(ll, h, i, c_ref):
            read = (h == c_ref[0]) if mine else (h != c_ref[0])
            return jnp.where(jnp.logical_and(ll == l, read), i, 0), 0
        return pl.BlockSpec((tr, C), index)

    full = pl.BlockSpec((None, tr, C), lambda ll, h, i, c_ref: (ll, h * nt + i, 0))
    shp = jax.ShapeDtypeStruct((L, 2 * r, C), F32)
    g_specs = [half(l, mine) for l in range(L) for mine in (True, False)]
    return pl.pallas_call(
        body, name=name,
        grid_spec=pltpu.PrefetchScalarGridSpec(
            num_scalar_prefetch=1, grid=(L, 2, nt),
            in_specs=g_specs + [full] * 3, out_specs=[full] * 4),
        out_shape=[shp] * 4,
        compiler_params=_params(("arbitrary", "arbitrary", "arbitrary")))(
            c_idx, *[a for pair in halves for a in pair], w, m, v)


def _small_update(gsum, logits_pack, w, m, v):
    def body(gs_ref, lg_ref, w_ref, m_ref, v_ref, g_ref, d_ref, nm_ref, nv_ref):
        g_ref[...] = gs_ref[...]
        l0, l1, l2 = lg_ref[0:1, :], lg_ref[1:2, :], lg_ref[2:3, :]
        mx = jnp.maximum(jnp.maximum(l0, l1), l2)
        e0, e1, e2 = jnp.exp(l0 - mx), jnp.exp(l1 - mx), jnp.exp(l2 - mx)
        tot = e0 + e1 + e2
        p0, p1, p2 = e0 / tot, e1 / tot, e2 / tot
        dlb = gs_ref[4:5, :]
        g_ref[4:5, :] = dlb * p0 * (1.0 - p0)
        g_ref[5:6, :] = -dlb * p0 * p1
        g_ref[6:7, :] = -dlb * p0 * p2
        d_ref[...], nm_ref[...], nv_ref[...] = _adam_math(w_ref[...], g_ref[...], m_ref[...], v_ref[...])

    full = pl.BlockSpec(memory_space=pltpu.VMEM)
    shp = jax.ShapeDtypeStruct(gsum.shape, F32)
    return pl.pallas_call(
        body, name="small_update", in_specs=[full] * 5, out_specs=[full] * 4, out_shape=[shp] * 4)(
            gsum, logits_pack, w, m, v)


def _pack_small(norm_mix, norm_ffn, lb3, out_norm, final_norm, extra=None):
    ncol = norm_mix.shape[1]
    on = jnp.pad(out_norm.reshape(1, -1), ((0, 0), (0, ncol - out_norm.size)))
    rows = [norm_mix, norm_ffn, lb3, on, final_norm.reshape(1, ncol)]
    if extra is not None:
        rows.append(extra)
    used = sum(r.shape[0] for r in rows)
    rows.append(jnp.zeros((SMALL_ROWS - used, ncol), F32))
    return jnp.concatenate(rows, axis=0)


WEIGHT_NAMES = ("hin", "hout", "qkv", "aout", "fin0", "fin1", "fdn0", "fdn1")
FIRST_WEIGHTS = ("hin",)
LATE_WEIGHTS_A = ("hout", "fin0", "fdn0")
LATE_WEIGHTS_B = ("qkv", "aout", "fin1", "fdn1")


def _split_weights(hgrn_w_in, hgrn_w_out, attn_w_qkv, attn_w_out, ffn_w_in, ffn_w_down):
    return {"hin": hgrn_w_in[0], "hout": hgrn_w_out[0], "qkv": attn_w_qkv[0], "aout": attn_w_out[0],
            "fin0": ffn_w_in[0], "fin1": ffn_w_in[1], "fdn0": ffn_w_down[0], "fdn1": ffn_w_down[1]}


def _halves(v):
    r, c = v.shape
    return v.reshape(2, r // 2, c)


def _full_weights(gathered):
    out = {}
    for k, g in gathered.items():
        _, _, r, c = g.shape
        if k in ("hin", "qkv", "fin0", "fin1"):
            out[k] = g.reshape(N_CHIPS, 2 * r, c)
        else:
            out[k] = g.reshape(N_CHIPS * 2 * r, c)
    return out


class _StepComm:
    def __init__(self, shards, c_idx, me_idx):
        self.shards, self.c_idx, self.me_idx = shards, c_idx, me_idx
        self.halves = {}
        self._stage = {}

    def gather_rider(self, names):
        return _GatherRider([_halves(self.shards[k].astype(BF16)) for k in names])

    def gathered(self, names, got):
        return _full_weights(dict(zip(names, got)))

    def first_weights(self):
        return self.gathered(FIRST_WEIGHTS, _run_rider(self.gather_rider(FIRST_WEIGHTS), "gather_first"))

    def pair_rider(self, grads, tag):
        names = list(grads)
        g4 = []
        for k in names:
            r, c = self.shards[k].shape
            g4.append(grads[k].reshape(N_CHIPS, 2, r // 2, c))
        self._stage[tag] = (names, g4)
        return _PairRider(g4)

    def pair_now(self, grads, tag):
        self.paired(tag, _run_rider(self.pair_rider(grads, tag), "grad_pair_exchange_" + tag))

    def paired(self, tag, got):
        names, g4 = self._stage[tag]
        self._stage[tag] = (names, [_pair_sum(g, s, self.c_idx) for g, s in zip(g4, got)])

    def exchange_rider(self, tag):
        return _ExchangeRider([s[1] for s in self._stage[tag][1]])

    def exchanged(self, tag, got):
        names, sums = self._stage.pop(tag)
        for k, s, g in zip(names, sums, got):
            self.halves[k] = _chip_sum(s[0], g, self.me_idx)

    def shared_halves(self):
        mine = [self.halves[k] for k in WEIGHT_NAMES]
        return dict(zip(WEIGHT_NAMES, zip(mine, _pair_share(mine))))


def kernel(x, norm_mix, norm_ffn, hgrn_w_in, hgrn_lb_logits, hgrn_out_norm, hgrn_w_out, attn_w_qkv, attn_w_out, ffn_w_in, ffn_w_down, final_norm, loss_target, m_norm_mix, m_norm_ffn, m_hgrn_w_in, m_hgrn_lb_logits, m_hgrn_out_norm, m_hgrn_w_out, m_attn_w_qkv, m_attn_w_out, m_ffn_w_in, m_ffn_w_down, m_final_norm, v_norm_mix, v_norm_ffn, v_hgrn_w_in, v_hgrn_lb_logits, v_hgrn_out_norm, v_hgrn_w_out, v_attn_w_qkv, v_attn_w_out, v_ffn_w_in, v_ffn_w_down, v_final_norm):
    S = x.shape[1]
    xi, yi, ci = lax.axis_index("x"), lax.axis_index("y"), lax.axis_index("c")
    c_idx = jnp.reshape(ci, (1,)).astype(jnp.int32)
    me_idx = jnp.reshape(2 * xi + yi, (1,)).astype(jnp.int32)

    w_own = _split_weights(hgrn_w_in, hgrn_w_out, attn_w_qkv, attn_w_out, ffn_w_in, ffn_w_down)

    comm = _StepComm(w_own, c_idx, me_idx)
    loss, dx, small = _local_step(
        x.reshape(S, D_MODEL), loss_target.reshape(S, D_MODEL), norm_mix, norm_ffn, hgrn_lb_logits,
        hgrn_out_norm, final_norm.reshape(1, D_MODEL), comm)

    halves = comm.shared_halves()
    updated = {}
    for tensor, layers, (wt, mt, vt) in (
            ("hgrn_w_in", ("hin",), (hgrn_w_in, m_hgrn_w_in, v_hgrn_w_in)),
            ("hgrn_w_out", ("hout",), (hgrn_w_out, m_hgrn_w_out, v_hgrn_w_out)),
            ("attn_w_qkv", ("qkv",), (attn_w_qkv, m_attn_w_qkv, v_attn_w_qkv)),
            ("attn_w_out", ("aout",), (attn_w_out, m_attn_w_out, v_attn_w_out)),
            ("ffn_w_in", ("fin0", "fin1"), (ffn_w_in, m_ffn_w_in, v_ffn_w_in)),
            ("ffn_w_down", ("fdn0", "fdn1"), (ffn_w_down, m_ffn_w_down, v_ffn_w_down))):
        updated[tensor] = _adamw([halves[k] for k in layers], c_idx, wt, mt, vt, "adamw_" + tensor)

    loss_row = jnp.pad(loss, ((0, 0), (0, D_MODEL - loss.shape[1])))
    lb3 = jnp.concatenate([small["lb"], jnp.zeros((2, D_MODEL), F32)], axis=0)
    on_grad = jnp.sum(small["out_norm"], axis=0, keepdims=True)
    pack = _pack_small(small["norm_mix"], small["norm_ffn"], lb3, on_grad, small["final_norm"], loss_row)
    gsum = _small_allreduce(pack)
    w_s = _pack_small(norm_mix, norm_ffn, hgrn_lb_logits, hgrn_out_norm, final_norm)
    m_s = _pack_small(m_norm_mix, m_norm_ffn, m_hgrn_lb_logits, m_hgrn_out_norm, m_final_norm)
    v_s = _pack_small(v_norm_mix, v_norm_ffn, v_hgrn_lb_logits, v_hgrn_out_norm, v_final_norm)
    lg_pack = jnp.pad(hgrn_lb_logits, ((0, 8 - hgrn_lb_logits.shape[0]), (0, 0)))
    sg, sd, sm, sv = _small_update(gsum, lg_pack, w_s, m_s, v_s)

    def unpack(p):
        return (p[0:2], p[2:4], p[4:7], p[7:8, :HEAD], p[8])

    def assemble(p, which):
        nmx, nff, lbl, onm, fnm = unpack(p)
        hin, hout, qkv, aout, fin, fdn = [updated[t][which] for t in
                                          ("hgrn_w_in", "hgrn_w_out", "attn_w_qkv", "attn_w_out", "ffn_w_in", "ffn_w_down")]
        return (nmx, nff, hin, lbl, onm, hout, qkv, aout, fin, fdn, fnm)

    total_loss = gsum[9, 0]
    return (total_loss, dx.reshape(1, S, D_MODEL), *assemble(sg, 0), *assemble(sd, 1), *assemble(sm, 2), *assemble(sv, 3))
```

```python
import functools

import jax
import jax.numpy as jnp
from jax import lax
from jax.experimental import pallas as pl
from jax.experimental.pallas import tpu as pltpu

F32 = jnp.float32
BF16 = jnp.bfloat16
MESH = pl.DeviceIdType.MESH

D_MODEL = 1024
HEAD = 128
HGRN_HEADS = 8
HGRN_CHUNK = 64
HGRN_HEADS_PER_STEP = 2
ATTN_GROUPS = ((128, 1), (512, 4), (2048, 16))
ATTN_SPAN = 128
HEADS_PER_GROUP = 4
GROUP_W = HEADS_PER_GROUP * HEAD
D_FF = 2816
NORM_EPS = 1e-6
ROPE_THETA = 10000.0
NEG = -1e30

ADAM_LR, ADAM_B1, ADAM_B2, ADAM_EPS, ADAM_WD, ADAM_STEP = 0.001, 0.9, 0.999, 1e-08, 0.01, 10

N_CHIPS = 4
VMEM_LIMIT = 56 * 1024 * 1024
SMALL_ROWS = 16


def _params(sem=None):
    return pltpu.CompilerParams(dimension_semantics=sem, vmem_limit_bytes=VMEM_LIMIT)


def _row_tile(rows, cols, budget_bytes=3 * 512 * 1024):
    best = 8
    for t in range(8, rows + 1, 8):
        if rows % t == 0 and t * cols * 4 <= budget_bytes:
            best = t
    assert rows % best == 0
    return best


def _grid_corner(i, j):
    return jnp.logical_and(pl.program_id(0) == i, pl.program_id(1) == j)


def _sigmoid(v):
    return 0.5 * jnp.tanh(0.5 * v) + 0.5


def _dot(a, b):
    return jnp.dot(a, b, preferred_element_type=F32)


def _dot_nt(a, b):
    return lax.dot_general(a, b, (((1,), (1,)), ((), ())), preferred_element_type=F32)


def _dot_tn(a, b):
    return lax.dot_general(a, b, (((0,), (0,)), ((), ())), preferred_element_type=F32)


def _dot_exact(ones, b):
    ones = ones.astype(BF16)
    hi = b.astype(BF16)
    rest = b - hi.astype(F32)
    mid = rest.astype(BF16)
    low = (rest - mid.astype(F32)).astype(BF16)
    return _dot(ones, hi) + _dot(ones, mid) + _dot(ones, low)


def _rstd(v):
    return lax.rsqrt(jnp.mean(v * v, axis=-1, keepdims=True) + NORM_EPS)


def _norm_mm(h, gain, w3, name, out_dtype=F32, tm=1024, rider=None):
    S, K = h.shape
    J, _, n = w3.shape
    gi = S // tm

    def body(h_ref, g_ref, w_ref, y_ref, u_ref):
        @pl.when(pl.program_id(1) == 0)
        def _():
            v = h_ref[...]
            u_ref[...] = (v * _rstd(v) * g_ref[...]).astype(BF16)

        y_ref[...] = _dot(u_ref[...], w_ref[pl.program_id(1)]).astype(y_ref.dtype)

    r_ops, r_in, r_out, r_shape, r_scr = _rider_args(rider)
    res = pl.pallas_call(
        _ride(rider, body, 3, 2, functools.partial(_grid_corner, 0, 0), functools.partial(_grid_corner, gi - 1, J - 1)),
        name=name, grid=(gi, J),
        in_specs=[pl.BlockSpec((tm, K), lambda i, j: (i, 0)),
                  pl.BlockSpec((1, K), lambda i, j: (0, 0)),
                  pl.BlockSpec((J, K, n), lambda i, j: (0, 0, 0))] + r_in,
        out_specs=[pl.BlockSpec((tm, n), lambda i, j: (i, j)), pl.BlockSpec((tm, K), lambda i, j: (i, 0))] + r_out,
        out_shape=[jax.ShapeDtypeStruct((S, J * n), out_dtype), jax.ShapeDtypeStruct((S, K), BF16)] + r_shape,
        scratch_shapes=r_scr,
        compiler_params=_params(("arbitrary", "arbitrary")))(h, gain, w3, *r_ops)
    return res[0], res[1], res[2:]


def _mm_res(h, a, w2, name, tm=512):
    S, N = h.shape
    K = a.shape[1]

    def body(h_ref, a_ref, w_ref, o_ref):
        o_ref[...] = h_ref[...] + _dot(a_ref[...], w_ref[...])

    return pl.pallas_call(
        body, name=name, grid=(S // tm,),
        in_specs=[pl.BlockSpec((tm, N), lambda i: (i, 0)),
                  pl.BlockSpec((tm, K), lambda i: (i, 0)),
                  pl.BlockSpec((K, N), lambda i: (0, 0))],
        out_specs=pl.BlockSpec((tm, N), lambda i: (i, 0)),
        out_shape=jax.ShapeDtypeStruct((S, N), F32),
        compiler_params=_params(("parallel",)))(h, a, w2)


def _swiglu(z_ref, F):
    g = z_ref[:, :F].astype(F32)
    return (g * _sigmoid(g) * z_ref[:, F:].astype(F32)).astype(BF16)


def _swiglu_mm_res(h, z, w2, name, tm=256):
    S, N = h.shape
    F = w2.shape[0]

    def body(h_ref, z_ref, w_ref, o_ref, a_ref):
        a = _swiglu(z_ref, F)
        a_ref[...] = a
        o_ref[...] = h_ref[...] + _dot(a, w_ref[...])

    return pl.pallas_call(
        body, name=name, grid=(S // tm,),
        in_specs=[pl.BlockSpec((tm, N), lambda i: (i, 0)),
                  pl.BlockSpec((tm, 2 * F), lambda i: (i, 0)),
                  pl.BlockSpec((F, N), lambda i: (0, 0))],
        out_specs=[pl.BlockSpec((tm, N), lambda i: (i, 0)), pl.BlockSpec((tm, F), lambda i: (i, 0))],
        out_shape=[jax.ShapeDtypeStruct((S, N), F32), jax.ShapeDtypeStruct((S, F), BF16)],
        compiler_params=_params(("parallel",)))(h, z, w2)


def _dy_specs(dy, J, n, tm):
    if dy.ndim == 3:
        return [pl.BlockSpec((None, tm, n), functools.partial(lambda i, j: (j, i, 0), j=j)) for j in range(J)]
    return [pl.BlockSpec((tm, n), functools.partial(lambda i, j: (i, j), j=j)) for j in range(J)]


def _acc_nt(dy_refs, w_ref):
    acc = None
    for j, r in enumerate(dy_refs):
        t = _dot_nt(r[...].astype(BF16), w_ref[j])
        acc = t if acc is None else acc + t
    return acc


def _mm_nt(dy, w3, name, out_dtype=F32, tm=512):
    J, K, n = w3.shape
    S = dy.shape[-2]

    def body(*refs):
        dy_refs, w_ref, o_ref = refs[:J], refs[J], refs[J + 1]
        o_ref[...] = _acc_nt(dy_refs, w_ref).astype(o_ref.dtype)

    return pl.pallas_call(
        body, name=name, grid=(S // tm,),
        in_specs=_dy_specs(dy, J, n, tm) + [pl.BlockSpec((J, K, n), lambda i: (0, 0, 0))],
        out_specs=pl.BlockSpec((tm, K), lambda i: (i, 0)),
        out_shape=jax.ShapeDtypeStruct((S, K), out_dtype),
        compiler_params=_params(("parallel",)))(*([dy] * J), w3)


def _mm_nt_normbwd(dy, w3, h, gain, dh, name, tm=512, rider=None):
    J, K, n = w3.shape
    S = h.shape[0]
    steps = S // tm

    def body(*refs):
        dy_refs, w_ref, h_ref, g_ref, dh_ref, o_ref, dg_ref = refs[:J], *refs[J:]
        du = _acc_nt(dy_refs, w_ref)
        v = h_ref[...]
        r = _rstd(v)
        xh = v * r
        dyg = du * g_ref[...]
        o_ref[...] = dh_ref[...] + r * (dyg - xh * jnp.mean(dyg * xh, axis=-1, keepdims=True))

        @pl.when(pl.program_id(0) == 0)
        def _():
            dg_ref[...] = jnp.zeros_like(dg_ref)

        dg_ref[...] += jnp.sum(du * xh, axis=0, keepdims=True)

    row = pl.BlockSpec((tm, K), lambda i: (i, 0))
    vec = pl.BlockSpec((1, K), lambda i: (0, 0))
    r_ops, r_in, r_out, r_shape, r_scr = _rider_args(rider)
    res = pl.pallas_call(
        _ride(rider, body, J + 4, 2, lambda: pl.program_id(0) == 0, lambda: pl.program_id(0) == steps - 1),
        name=name, grid=(steps,),
        in_specs=_dy_specs(dy, J, n, tm) + [pl.BlockSpec((J, K, n), lambda i: (0, 0, 0)), row, vec, row] + r_in,
        out_specs=[row, vec] + r_out,
        out_shape=[jax.ShapeDtypeStruct((S, K), F32), jax.ShapeDtypeStruct((1, K), F32)] + r_shape,
        scratch_shapes=r_scr,
        compiler_params=_params(("arbitrary",)))(*([dy] * J), w3, h, gain, dh, *r_ops)
    return res[0], res[1], res[2:]


def _mm_nt_swiglu_bwd(dh, w2, z, name, tm=256):
    F, N = w2.shape
    S = dh.shape[0]

    def body(dh_ref, w_ref, z_ref, o_ref):
        da = _dot_nt(dh_ref[...].astype(BF16), w_ref[...])
        g = z_ref[:, :F].astype(F32)
        u = z_ref[:, F:].astype(F32)
        sg = _sigmoid(g)
        o_ref[:, :F] = (da * u * (sg * (1.0 + g * (1.0 - sg)))).astype(BF16)
        o_ref[:, F:] = (da * (g * sg)).astype(BF16)

    return pl.pallas_call(
        body, name=name, grid=(S // tm,),
        in_specs=[pl.BlockSpec((tm, N), lambda i: (i, 0)),
                  pl.BlockSpec((F, N), lambda i: (0, 0)),
                  pl.BlockSpec((tm, 2 * F), lambda i: (i, 0))],
        out_specs=pl.BlockSpec((tm, 2 * F), lambda i: (i, 0)),
        out_shape=jax.ShapeDtypeStruct((S, 2 * F), BF16),
        compiler_params=_params(("parallel",)))(dh, w2, z)


def _mm_tn(x, dy, J, n, tn, name):
    tpn = n // tn
    ts = 1024
    S, K = x.shape
    if dy.ndim == 3:
        dy_spec = pl.BlockSpec((None, ts, tn), lambda c, s: (c // tpn, s, c % tpn))
    else:
        dy_spec = pl.BlockSpec((ts, tn), lambda c, s: (s, c))

    def body(x_ref, dy_ref, o_ref):
        @pl.when(pl.program_id(1) == 0)
        def _():
            o_ref[...] = jnp.zeros_like(o_ref)

        o_ref[...] += _dot_tn(x_ref[...], dy_ref[...].astype(BF16))

    return pl.pallas_call(
        body, name=name, grid=(J * tpn, S // ts),
        in_specs=[pl.BlockSpec((ts, K), lambda c, s: (s, 0)), dy_spec],
        out_specs=pl.BlockSpec((None, K, tn), lambda c, s: (c // tpn, 0, c % tpn)),
        out_shape=jax.ShapeDtypeStruct((J, K, n), F32),
        compiler_params=_params(("parallel", "arbitrary")))(x, dy)


def _loss_head(h, gain, target, tm=512):
    S, K = h.shape

    def body(h_ref, g_ref, t_ref, dh_ref, loss_ref, dg_ref):
        v = h_ref[...]
        r = _rstd(v)
        xh = v * r
        g = g_ref[...]
        dy = (xh * g - t_ref[...]) * (1.0 / K)
        dyg = dy * g
        dh_ref[...] = r * (dyg - xh * jnp.mean(dyg * xh, axis=-1, keepdims=True))

        @pl.when(pl.program_id(0) == 0)
        def _():
            loss_ref[...] = jnp.zeros_like(loss_ref)
            dg_ref[...] = jnp.zeros_like(dg_ref)

        part = jnp.sum(jnp.sum(dy * dy, axis=-1, keepdims=True), axis=0, keepdims=True) * (0.5 * K)
        lane = lax.broadcasted_iota(jnp.int32, loss_ref.shape, 1)
        loss_ref[...] += jnp.where(lane == 0, part, 0.0)
        dg_ref[...] += jnp.sum(dy * xh, axis=0, keepdims=True)

    row = pl.BlockSpec((tm, K), lambda i: (i, 0))
    vec = pl.BlockSpec((1, K), lambda i: (0, 0))
    return pl.pallas_call(
        body, name="loss_head", grid=(S // tm,),
        in_specs=[row, vec, row],
        out_specs=[row, pl.BlockSpec((1, HEAD), lambda i: (0, 0)), vec],
        out_shape=[jax.ShapeDtypeStruct((S, K), F32), jax.ShapeDtypeStruct((1, HEAD), F32),
                   jax.ShapeDtypeStruct((1, K), F32)],
        compiler_params=_params(("arbitrary",)))(h, gain, target)


def _lower_bound(lg_ref):
    l0, l1, l2 = lg_ref[0:1, :], lg_ref[1:2, :], lg_ref[2:3, :]
    mx = jnp.maximum(jnp.maximum(l0, l1), l2)
    e0, e1, e2 = jnp.exp(l0 - mx), jnp.exp(l1 - mx), jnp.exp(l2 - mx)
    return e0 / (e0 + e1 + e2)


def _chunks(v, ncb):
    C = HGRN_CHUNK
    return [v[c * C:(c + 1) * C] for c in range(ncb)]


def _rows(parts):
    return jnp.concatenate(parts, axis=0)


def _block_gates(qz, fz, lb, ncb):
    C = HGRN_CHUNK
    row = lax.broadcasted_iota(jnp.int32, (C, C), 0)
    col = lax.broadcasted_iota(jnp.int32, (C, C), 1)
    tri = (col <= row).astype(F32)
    first_half = lax.broadcasted_iota(jnp.int32, (C, HEAD), 0) < C // 2
    sig = _sigmoid(fz)
    fg = lb + (1.0 - lb) * sig
    key = 1.0 - fg
    lg = jnp.log(fg)
    lgs = _chunks(lg, ncb)
    b = _rows([_dot_exact(tri, v) for v in lgs])
    r_c = [jnp.sum(jnp.where(first_half, v, 0.0), axis=0, keepdims=True) for v in lgs]
    bl_c = [jnp.sum(v, axis=0, keepdims=True) for v in lgs]
    r = _rows([jnp.broadcast_to(v, (C, HEAD)) for v in r_c])
    e_br, e_rb = jnp.exp(b - r), jnp.exp(r - b)
    e_b = e_br * _rows([jnp.broadcast_to(jnp.exp(v), (C, HEAD)) for v in r_c])
    e_lb = e_rb * _rows([jnp.broadcast_to(jnp.exp(e - v), (C, HEAD)) for e, v in zip(bl_c, r_c)])
    sq = _sigmoid(qz)
    qy = qz * sq
    return sig, fg, key, (e_br, e_rb, e_b, e_lb), bl_c, sq, qy


def _hgrn_fwd(proj, logits, gain, tb=1024, rider=None):
    S = proj.shape[0]
    H, C = HGRN_HEADS, HGRN_CHUNK
    ncb = tb // C

    def one_head(q_ref, f_ref, i_ref, g_ref, lg_ref, gn_ref, o_ref, og_ref, st_ref, state):
        @pl.when(pl.program_id(1) == 0)
        def _():
            state[...] = jnp.zeros_like(state)

        lb = _lower_bound(lg_ref)
        causal = lax.broadcasted_iota(jnp.int32, (C, C), 1) <= lax.broadcasted_iota(jnp.int32, (C, C), 0)
        qz, fz, gz = q_ref[...], f_ref[...], g_ref[...]
        _, _, key, (e_br, e_rb, e_b, e_lb), bl_c, _, qy = _block_gates(qz, fz, lb, ncb)
        qs = _chunks((qy * e_br).astype(BF16), ncb)
        ks = _chunks((key * e_rb).astype(BF16), ncb)
        qb = _chunks((qy * e_b).astype(BF16), ncb)
        ke = _chunks((key * e_lb).astype(BF16), ncb)
        vb = _chunks(i_ref[...].astype(BF16), ncb)
        a = [jnp.where(causal, _dot_nt(qs[c], ks[c]), 0.0).astype(BF16) for c in range(ncb)]
        upd = [_dot_tn(vb[c], ke[c]) for c in range(ncb)]
        o_intra = [_dot(a[c], vb[c]) for c in range(ncb)]
        st = state[...]
        e_l = [jnp.exp(v) for v in bl_c]
        sts = []
        for c in range(ncb):
            sts.append(st)
            st = st * e_l[c] + upd[c]
        state[...] = st
        for c in range(ncb):
            st_ref[c] = sts[c]
        o = _rows([_dot_nt(qb[c], sts[c].astype(BF16)) + o_intra[c] for c in range(ncb)])
        o_ref[...] = o
        og_ref[...] = ((o * _rstd(o) * gn_ref[...]) * (gz * _sigmoid(gz))).astype(BF16)

    def body(q_ref, f_ref, i_ref, g_ref, lg_ref, gn_ref, o_ref, og_ref, st_ref, state):
        for hs in range(HP):
            cols = slice(hs * HEAD, (hs + 1) * HEAD)
            one_head(q_ref.at[:, cols], f_ref.at[:, cols], i_ref.at[:, cols], g_ref.at[:, cols], lg_ref.at[:, cols],
                     gn_ref, o_ref.at[:, cols], og_ref.at[:, cols], st_ref.at[hs], state.at[hs])

    HP, wide = HGRN_HEADS_PER_STEP, HGRN_HEADS_PER_STEP * HEAD
    hg = H // HP

    def part(p):
        return pl.BlockSpec((tb, wide), functools.partial(lambda h, i, p: (i, p * hg + h), p=p))

    nb = S // tb
    r_ops, r_in, r_out, r_shape, r_scr = _rider_args(rider)
    res = pl.pallas_call(
        _ride(rider, body, 6, 3, functools.partial(_grid_corner, 0, 0), functools.partial(_grid_corner, hg - 1, nb - 1)),
        name="hgrn_fwd", grid=(hg, nb),
        in_specs=[part(0), part(1), part(2), part(3),
                  pl.BlockSpec((3, wide), lambda h, i: (0, h)),
                  pl.BlockSpec((1, HEAD), lambda h, i: (0, 0))] + r_in,
        out_specs=[pl.BlockSpec((tb, wide), lambda h, i: (i, h)),
                   pl.BlockSpec((tb, wide), lambda h, i: (i, h)),
                   pl.BlockSpec((HP, ncb, HEAD, HEAD), lambda h, i: (h, i, 0, 0))] + r_out,
        out_shape=[jax.ShapeDtypeStruct((S, H * HEAD), F32),
                   jax.ShapeDtypeStruct((S, H * HEAD), BF16),
                   jax.ShapeDtypeStruct((H, S // C, HEAD, HEAD), F32)] + r_shape,
        scratch_shapes=[pltpu.VMEM((HP, HEAD, HEAD), F32)] + r_scr,
        compiler_params=_params(("arbitrary", "arbitrary")))(proj, proj, proj, proj, logits, gain, *r_ops)
    return res[:3], res[3:]


def _hgrn_bwd(proj, logits, gain, o, states, dog, tb=1024, rider=None):
    S = proj.shape[0]
    H, C = HGRN_HEADS, HGRN_CHUNK
    ncb = tb // C
    nb = S // tb

    def one_head(q_ref, f_ref, i_ref, g_ref, lg_ref, gn_ref, o_ref, st_ref, dog_ref,
                 dp_ref, dlb_ref, dgn_ref, dstate, dst_scr):
        @pl.when(pl.program_id(1) == 0)
        def _():
            dstate[...] = jnp.zeros_like(dstate)
            dlb_ref[...] = jnp.zeros_like(dlb_ref)
            dgn_ref[...] = jnp.zeros_like(dgn_ref)

        lb = _lower_bound(lg_ref)
        oml = 1.0 - lb
        gn = gn_ref[...]
        row = lax.broadcasted_iota(jnp.int32, (C, C), 0)
        col = lax.broadcasted_iota(jnp.int32, (C, C), 1)
        causal = col <= row
        tri_up = (col >= row).astype(F32)
        qz, fz, gz = q_ref[...], f_ref[...], g_ref[...]
        sig, fg, key, (e_br, e_rb, e_b, e_lb), bl_c, sq, qy = _block_gates(qz, fz, lb, ncb)
        qs_v, ks_v = (qy * e_br).astype(BF16), (key * e_rb).astype(BF16)
        qb_v, ke_v = (qy * e_b).astype(BF16), (key * e_lb).astype(BF16)
        qs, ks, qb, ke = _chunks(qs_v, ncb), _chunks(ks_v, ncb), _chunks(qb_v, ncb), _chunks(ke_v, ncb)
        vb = _chunks(i_ref[...].astype(BF16), ncb)
        ov = o_ref[...]
        rs = _rstd(ov)
        xh = ov * rs
        sg = _sigmoid(gz)
        dog_v = dog_ref[...]
        dgz = dog_v * (xh * gn) * (sg * (1.0 + gz * (1.0 - sg)))
        don = dog_v * (gz * sg)
        dgn_ref[...] += jnp.sum(don * xh, axis=0, keepdims=True)
        dyg = don * gn
        do = rs * (dyg - xh * jnp.mean(dyg * xh, axis=-1, keepdims=True))
        dob = _chunks(do.astype(BF16), ncb)
        CH = range(ncb)
        a = [jnp.where(causal, _dot_nt(qs[c], ks[c]), 0.0).astype(BF16) for c in CH]
        da = [jnp.where(causal, _dot_nt(dob[c], vb[c]), 0.0).astype(BF16) for c in CH]
        wst = [_dot_tn(dob[c], qb[c]) for c in CH]
        dv_in = [_dot_tn(a[c], dob[c]) for c in CH]
        dqs = [_dot(da[c], ks[c]) for c in CH]
        dks = [_dot_tn(da[c], qs[c]) for c in CH]
        e_l = [jnp.exp(v) for v in bl_c]
        dst = dstate[...]
        for c in reversed(range(ncb)):
            dst_scr[c] = dst
            dst = wst[c] + dst * e_l[c]
        dstate[...] = dst
        dst1b = [dst_scr[c].astype(BF16) for c in CH]
        dqb = [_dot(dob[c], st_ref[c].astype(BF16)) for c in CH]
        dke = [_dot(vb[c], dst1b[c]) for c in CH]
        dv = [dv_in[c] + _dot_nt(ke[c], dst1b[c]) for c in CH]
        dbl_st = [jnp.sum(dst_scr[c] * st_ref[c], axis=0, keepdims=True) * e_l[c] for c in CH]
        dqs, dks, dqb, dke, dv = _rows(dqs), _rows(dks), _rows(dqb), _rows(dke), _rows(dv)
        dke_ke = dke * ke_v.astype(F32)
        db = dqs * qs_v.astype(F32) - dks * ks_v.astype(F32) + dqb * qb_v.astype(F32) - dke_ke
        dlg = []
        for c, (db_c, kk_c) in enumerate(zip(_chunks(db, ncb), _chunks(dke_ke, ncb))):
            dbl = jnp.sum(kk_c, axis=0, keepdims=True) + dbl_st[c]
            dlg.append(_dot_exact(tri_up, db_c) + dbl)
        dlg = _rows(dlg)
        dkey = dks * e_rb + dke * e_lb
        dqy = dqs * e_br + dqb * e_b
        dfg = dlg / fg - dkey
        dlb_ref[...] += jnp.sum(dfg * (1.0 - sig), axis=0, keepdims=True)
        dp_ref[0] = (dqy * (sq * (1.0 + qz * (1.0 - sq)))).astype(BF16)
        dp_ref[1] = (dfg * oml * sig * (1.0 - sig)).astype(BF16)
        dp_ref[2] = dv.astype(BF16)
        dp_ref[3] = dgz.astype(BF16)

    def body(q_ref, f_ref, i_ref, g_ref, lg_ref, gn_ref, o_ref, st_ref, dog_ref,
             dp_ref, dlb_ref, dgn_ref, dstate, dst_scr):
        for hs in range(HP):
            cols = slice(hs * HEAD, (hs + 1) * HEAD)
            one_head(q_ref.at[:, cols], f_ref.at[:, cols], i_ref.at[:, cols], g_ref.at[:, cols], lg_ref.at[:, cols],
                     gn_ref, o_ref.at[:, cols], st_ref.at[hs], dog_ref.at[:, cols],
                     dp_ref.at[:, :, cols], dlb_ref.at[hs], dgn_ref.at[hs], dstate.at[hs], dst_scr)

    HP, wide = HGRN_HEADS_PER_STEP, HGRN_HEADS_PER_STEP * HEAD
    hg = H // HP

    def part(p):
        return pl.BlockSpec((tb, wide), functools.partial(lambda h, i, p: (nb - 1 - i, p * hg + h), p=p))

    blk = pl.BlockSpec((tb, wide), lambda h, i: (nb - 1 - i, h))
    acc = pl.BlockSpec((HP, 1, HEAD), lambda h, i: (h, 0, 0))
    r_ops, r_in, r_out, r_shape, r_scr = _rider_args(rider)
    res = pl.pallas_call(
        _ride(rider, body, 9, 3, functools.partial(_grid_corner, 0, 0), functools.partial(_grid_corner, hg - 1, nb - 1)),
        name="hgrn_bwd", grid=(hg, nb),
        in_specs=[part(0), part(1), part(2), part(3),
                  pl.BlockSpec((3, wide), lambda h, i: (0, h)),
                  pl.BlockSpec((1, HEAD), lambda h, i: (0, 0)),
                  blk,
                  pl.BlockSpec((HP, ncb, HEAD, HEAD), lambda h, i: (h, nb - 1 - i, 0, 0)),
                  blk] + r_in,
        out_specs=[pl.BlockSpec((4, tb, wide), lambda h, i: (0, nb - 1 - i, h)), acc, acc] + r_out,
        out_shape=[jax.ShapeDtypeStruct((4, S, H * HEAD), BF16),
                   jax.ShapeDtypeStruct((H, 1, HEAD), F32),
                   jax.ShapeDtypeStruct((H, 1, HEAD), F32)] + r_shape,
        scratch_shapes=[pltpu.VMEM((HP, HEAD, HEAD), F32), pltpu.VMEM((ncb, HEAD, HEAD), F32)] + r_scr,
        compiler_params=_params(("arbitrary", "arbitrary")))(
            proj, proj, proj, proj, logits, gain, o, states, dog, *r_ops)
    return res[:3], res[3:]


def _rope(v, cos, sin):
    return v * cos + pltpu.roll(v, HEAD // 2, 1) * sin


def _lane_pick(tile, hh):
    lane = lax.broadcasted_iota(jnp.int32, tile.shape, 1)
    return jnp.sum(jnp.where(lane == hh, tile, 0.0), axis=-1, keepdims=True)


def _lane_place(cols):
    rows = cols[0].shape[0]
    lane = lax.broadcasted_iota(jnp.int32, (rows, HEAD), 1)
    tile = jnp.zeros((rows, HEAD), F32)
    for hh, v in enumerate(cols):
        tile = jnp.where(lane == hh, v, tile)
    return tile


def _band_masks():
    qi = lax.broadcasted_iota(jnp.int32, (ATTN_SPAN, ATTN_SPAN), 0)
    kj = lax.broadcasted_iota(jnp.int32, (ATTN_SPAN, ATTN_SPAN), 1)
    return kj <= qi, kj >= qi


ATTN_TILE_BLOCKS = 4


def _attn_fwd(a):
    d, L, _ = a.shape
    B, W = ATTN_TILE_BLOCKS, ATTN_SPAN
    T = B * W
    assert L % T == 0
    steps = L // T
    scale = HEAD ** -0.5

    def body(q_ref, kc_ref, kp_ref, vc_ref, vp_ref, o_ref, lse_ref):
        n = pl.program_id(1)
        mask_c, mask_p0 = _band_masks()
        first = jnp.logical_and(mask_p0, n > 0)
        units = [(b, hh) for b in range(B) for hh in range(HEADS_PER_GROUP)]
        rows = [slice(b * W, (b + 1) * W) for b in range(B)]
        cols = [slice(hh * HEAD, (hh + 1) * HEAD) for hh in range(HEADS_PER_GROUP)]

        def prev_keys(ref, tile, b, hh):
            return ref[:, cols[hh]] if b == 0 else tile[rows[b - 1], cols[hh]]

        s_c = [jnp.where(mask_c, _dot_nt(q_ref[rows[b], cols[hh]], kc_ref[rows[b], cols[hh]]) * scale, NEG) for b, hh in units]
        s_p = [jnp.where(first if b == 0 else mask_p0,
                         _dot_nt(q_ref[rows[b], cols[hh]], prev_keys(kp_ref, kc_ref, b, hh)) * scale, NEG) for b, hh in units]
        m = [jnp.maximum(jnp.max(x, axis=-1, keepdims=True), jnp.max(y, axis=-1, keepdims=True)) for x, y in zip(s_c, s_p)]
        p_c = [jnp.exp(x - mm) for x, mm in zip(s_c, m)]
        p_p = [jnp.exp(y - mm) for y, mm in zip(s_p, m)]
        l = [jnp.sum(x, axis=-1, keepdims=True) + jnp.sum(y, axis=-1, keepdims=True) for x, y in zip(p_c, p_p)]
        acc = [_dot(p_c[i].astype(BF16), vc_ref[rows[b], cols[hh]]) + _dot(p_p[i].astype(BF16), prev_keys(vp_ref, vc_ref, b, hh))
               for i, (b, hh) in enumerate(units)]
        for i, (b, hh) in enumerate(units):
            o_ref[rows[b], cols[hh]] = acc[i] / l[i]
        for b in range(B):
            lse_ref[rows[b], :] = _lane_place([m[i] + jnp.log(l[i]) for i, (bb, _) in enumerate(units) if bb == b])

    def cur(part):
        return pl.BlockSpec((None, T, GROUP_W), functools.partial(lambda r, n, p: (r, n, p), p=part))

    def prev(part):
        return pl.BlockSpec((None, W, GROUP_W), functools.partial(lambda r, n, p: (r, jnp.maximum(n * B - 1, 0), p), p=part))

    return pl.pallas_call(
        body, name=f"attn_fwd_d{d}", grid=(d, steps),
        in_specs=[cur(0), cur(1), prev(1), cur(2), prev(2)],
        out_specs=[pl.BlockSpec((None, T, GROUP_W), lambda r, n: (r, n, 0)), pl.BlockSpec((None, T, HEAD), lambda r, n: (r, n, 0))],
        out_shape=[jax.ShapeDtypeStruct((d, L, GROUP_W), F32), jax.ShapeDtypeStruct((d, L, HEAD), F32)],
        compiler_params=_params(("parallel", "arbitrary")))(a, a, a, a, a)


def _attn_bwd(a, do, lse, dd):
    d, L, _ = a.shape
    B, W = ATTN_TILE_BLOCKS, ATTN_SPAN
    T = B * W
    assert L % T == 0
    steps = L // T
    scale = HEAD ** -0.5

    def body(qc_ref, qn_ref, kp_ref, kc_ref, vp_ref, vc_ref, doc_ref, don_ref, lc_ref, ln_ref, ddc_ref, ddn_ref, da_ref):
        n = pl.program_id(1)
        mask_c, mask_p0 = _band_masks()
        first = jnp.logical_and(mask_p0, n > 0)
        last = jnp.logical_and(mask_p0, n < steps - 1)
        H4 = range(HEADS_PER_GROUP)
        units = [(b, hh) for b in range(B) for hh in H4]
        rows = [slice(b * W, (b + 1) * W) for b in range(B)]
        cols = [slice(hh * HEAD, (hh + 1) * HEAD) for hh in H4]
        q = {u: qc_ref[rows[u[0]], cols[u[1]]] for u in units}
        k = {u: kc_ref[rows[u[0]], cols[u[1]]] for u in units}
        v = {u: vc_ref[rows[u[0]], cols[u[1]]] for u in units}
        g_o = {u: doc_ref[rows[u[0]], cols[u[1]]] for u in units}
        kb = {(b, hh): kp_ref[:, cols[hh]] if b == 0 else k[(b - 1, hh)] for b, hh in units}
        vb = {(b, hh): vp_ref[:, cols[hh]] if b == 0 else v[(b - 1, hh)] for b, hh in units}
        lse_t = {(b, hh): _lane_pick(lc_ref[rows[b], :], hh) for b, hh in units}
        dd_t = {(b, hh): _lane_pick(ddc_ref[rows[b], :], hh) for b, hh in units}
        p_c = {u: jnp.where(mask_c, jnp.exp(_dot_nt(q[u], k[u]) * scale - lse_t[u]), 0.0) for u in units}
        p_p = {u: jnp.where(first if u[0] == 0 else mask_p0, jnp.exp(_dot_nt(q[u], kb[u]) * scale - lse_t[u]), 0.0) for u in units}
        ds_c = {u: (p_c[u] * (_dot_nt(g_o[u], v[u]) + dd_t[u])).astype(BF16) for u in units}
        ds_p = {u: (p_p[u] * (_dot_nt(g_o[u], vb[u]) + dd_t[u])).astype(BF16) for u in units}
        qn = [qn_ref[:, c] for c in cols]
        g_n = [don_ref[:, c] for c in cols]
        p_n = [jnp.where(last, jnp.exp(_dot_nt(qn[hh], k[(B - 1, hh)]) * scale - _lane_pick(ln_ref[...], hh)), 0.0) for hh in H4]
        ds_n = [(p_n[hh] * (_dot_nt(g_n[hh], v[(B - 1, hh)]) + _lane_pick(ddn_ref[...], hh))).astype(BF16) for hh in H4]
        dq = {u: (_dot(ds_c[u], k[u]) + _dot(ds_p[u], kb[u])) * scale for u in units}
        dk, dv = {}, {}
        for b, hh in units:
            if b < B - 1:
                nxt = (b + 1, hh)
                dk[(b, hh)] = (_dot_tn(ds_c[(b, hh)], q[(b, hh)]) + _dot_tn(ds_p[nxt], q[nxt])) * scale
                dv[(b, hh)] = _dot_tn(p_c[(b, hh)].astype(BF16), g_o[(b, hh)]) + _dot_tn(p_p[nxt].astype(BF16), g_o[nxt])
            else:
                dk[(b, hh)] = (_dot_tn(ds_c[(b, hh)], q[(b, hh)]) + _dot_tn(ds_n[hh], qn[hh])) * scale
                dv[(b, hh)] = _dot_tn(p_c[(b, hh)].astype(BF16), g_o[(b, hh)]) + _dot_tn(p_n[hh].astype(BF16), g_n[hh])
        for b, hh in units:
            da_ref[rows[b], cols[hh]] = dq[(b, hh)].astype(BF16)
            da_ref[rows[b], GROUP_W + hh * HEAD:GROUP_W + (hh + 1) * HEAD] = dk[(b, hh)].astype(BF16)
            da_ref[rows[b], 2 * GROUP_W + hh * HEAD:2 * GROUP_W + (hh + 1) * HEAD] = dv[(b, hh)].astype(BF16)

    nb = L // W

    def cur(width, part):
        return pl.BlockSpec((None, T, width), functools.partial(lambda r, n, p: (r, n, p), p=part))

    def prev(width, part):
        return pl.BlockSpec((None, W, width), functools.partial(lambda r, n, p: (r, jnp.maximum(n * B - 1, 0), p), p=part))

    def nxt(width, part):
        return pl.BlockSpec((None, W, width), functools.partial(lambda r, n, p: (r, jnp.minimum(n * B + B, nb - 1), p), p=part))

    g = GROUP_W
    return pl.pallas_call(
        body, name=f"attn_bwd_d{d}", grid=(d, steps),
        in_specs=[cur(g, 0), nxt(g, 0), prev(g, 1), cur(g, 1), prev(g, 2), cur(g, 2),
                  cur(g, 0), nxt(g, 0), cur(HEAD, 0), nxt(HEAD, 0), cur(HEAD, 0), nxt(HEAD, 0)],
        out_specs=pl.BlockSpec((None, T, 3 * g), lambda r, n: (r, n, 0)),
        out_shape=jax.ShapeDtypeStruct((d, L, 3 * g), BF16),
        compiler_params=_params(("parallel", "arbitrary")))(
            a, a, a, a, a, a, do, do, lse, lse, dd, dd)


def _softmax3(ls):
    mx = jnp.maximum(jnp.maximum(ls[0], ls[1]), ls[2])
    es = [jnp.exp(v - mx) for v in ls]
    tot = es[0] + es[1] + es[2]
    return [e / tot for e in es]


HEAD_COLS = [slice(hh * HEAD, (hh + 1) * HEAD) for hh in range(HEADS_PER_GROUP)]


def _group_spec(d, tm):
    return pl.BlockSpec((d, tm // d, GROUP_W), lambda i: (0, i, 0))


def _gather_heads(ref, scr, d, tm):
    if d == 1:
        return [ref[0, :, cols].astype(F32) for cols in HEAD_COLS]
    for hh, cols in enumerate(HEAD_COLS):
        for r in range(d):
            scr.at[hh][pl.ds(r, tm // d, stride=d), :] = ref[r, :, cols].astype(F32)
    return [scr[hh] for hh in range(HEADS_PER_GROUP)]


def _tile_spec(d, tm):
    return pl.BlockSpec((d, tm // d, HEAD), lambda i: (0, i, 0))


def _gather_tile(ref, scr, d, tm):
    if d == 1:
        return ref[0]
    for r in range(d):
        scr[pl.ds(r, tm // d, stride=d), :] = ref[r]
    return scr[...]


def _scatter_tile(val, scr, ref, d, tm):
    if d == 1:
        ref[0] = val
        return
    scr[...] = val
    for r in range(d):
        ref[r] = scr[pl.ds(r, tm // d, stride=d), :]


def _scatter_heads(vals, scr, ref, d, tm):
    if d == 1:
        for cols, v in zip(HEAD_COLS, vals):
            ref[0, :, cols] = v.astype(ref.dtype)
        return
    for hh, v in enumerate(vals):
        scr[hh] = v
    for hh, cols in enumerate(HEAD_COLS):
        for r in range(d):
            ref[r, :, cols] = scr.at[hh][pl.ds(r, tm // d, stride=d), :].astype(ref.dtype)


def _qkv_dilated(h, gain, wg, cos, sin, d, tm=512):
    S, K = h.shape

    def body(h_ref, g_ref, w_ref, cos_ref, sin_ref, out_ref, u_ref, y_scr):
        p = pl.program_id(1)

        @pl.when(p == 0)
        def _():
            v = h_ref[...]
            u_ref[...] = (v * _rstd(v) * g_ref[...]).astype(BF16)

        y = _dot(u_ref[...], w_ref[...])
        heads = [slice(hh * HEAD, (hh + 1) * HEAD) for hh in range(HEADS_PER_GROUP)]
        for hh, cols in enumerate(heads):
            y_scr[hh] = y[:, cols]

        @pl.when(p < 2)
        def _():
            for r in range(d):
                rows = slice(None) if d == 1 else pl.ds(r, tm // d, stride=d)
                cr, sr = cos_ref[rows, :], sin_ref[rows, :]
                for hh, cols in enumerate(heads):
                    out_ref[r, :, cols] = _rope(y_scr.at[hh][rows, :], cr, sr).astype(BF16)

        @pl.when(p == 2)
        def _():
            for r in range(d):
                rows = slice(None) if d == 1 else pl.ds(r, tm // d, stride=d)
                for hh, cols in enumerate(heads):
                    out_ref[r, :, cols] = y_scr.at[hh][rows, :].astype(BF16)

    tab = pl.BlockSpec((tm, HEAD), lambda i, p: (i, 0))
    return pl.pallas_call(
        body, name=f"attn_qkv_d{d}", grid=(S // tm, 3),
        in_specs=[pl.BlockSpec((tm, K), lambda i, p: (i, 0)),
                  pl.BlockSpec((1, K), lambda i, p: (0, 0)),
                  pl.BlockSpec((K, GROUP_W), lambda i, p: (0, p)), tab, tab],
        out_specs=[pl.BlockSpec((d, tm // d, GROUP_W), lambda i, p: (0, i, p)), pl.BlockSpec((tm, K), lambda i, p: (i, 0))],
        out_shape=[jax.ShapeDtypeStruct((d, S // d, 3 * GROUP_W), BF16), jax.ShapeDtypeStruct((S, K), BF16)],
        scratch_shapes=[pltpu.VMEM((HEADS_PER_GROUP, tm, HEAD), F32)],
        compiler_params=_params(("parallel", "arbitrary")))(h, gain, wg, cos, sin)


def _undilate_group(da, dqkv, cos, sin, g, tm=512):
    d, L, _ = da.shape
    S = d * L
    G = len(ATTN_GROUPS)

    def body(*refs):
        da_ref, cos_ref, sin_ref, out_ref, scr = refs[0], refs[1], refs[2], refs[-2], refs[-1]
        p = pl.program_id(1)
        heads = [slice(hh * HEAD, (hh + 1) * HEAD) for hh in range(HEADS_PER_GROUP)]
        for hh, cols in enumerate(heads):
            if d == 1:
                scr[hh] = da_ref[0, :, cols].astype(F32)
            else:
                for r in range(d):
                    scr.at[hh][pl.ds(r, tm // d, stride=d), :] = da_ref[r, :, cols].astype(F32)

        @pl.when(p < 2)
        def _():
            cr, sr = cos_ref[...], -sin_ref[...]
            for hh, cols in enumerate(heads):
                out_ref[:, cols] = _rope(scr[hh], cr, sr).astype(BF16)

        @pl.when(p == 2)
        def _():
            for hh, cols in enumerate(heads):
                out_ref[:, cols] = scr[hh].astype(BF16)

    tab = pl.BlockSpec((tm, HEAD), lambda i, p: (i, 0))
    operands = (da, cos, sin) if dqkv is None else (da, cos, sin, dqkv)
    return pl.pallas_call(
        body, name=f"attn_undilate_d{d}", grid=(S // tm, 3),
        in_specs=[pl.BlockSpec((d, tm // d, GROUP_W), lambda i, p: (0, i, p)), tab, tab] + ([] if dqkv is None else [ANY]),
        out_specs=pl.BlockSpec((tm, GROUP_W), lambda i, p: (i, p * G + g)),
        out_shape=jax.ShapeDtypeStruct((S, 3 * G * GROUP_W), BF16),
        input_output_aliases={} if dqkv is None else {3: 0},
        scratch_shapes=[pltpu.VMEM((HEADS_PER_GROUP, tm, HEAD), F32)],
        compiler_params=_params(("parallel", "arbitrary")))(*operands)


def _attn_merge(os_, lses, tm=512):
    G = len(os_)
    S = os_[0].shape[0] * os_[0].shape[1]

    def body(*refs):
        o_refs, l_refs, out_ref = refs[:G], refs[G:2 * G], refs[2 * G]
        scr = refs[2 * G + 1:]
        o = [_gather_heads(o_refs[g], scr[g], d, tm) for g, (_, d) in enumerate(ATTN_GROUPS)]
        l = [_gather_tile(l_refs[g], scr[G + g].at[0], d, tm) for g, (_, d) in enumerate(ATTN_GROUPS)]
        for hh in range(HEADS_PER_GROUP):
            al = _softmax3([_lane_pick(l[g], hh) for g in range(G)])
            for g in range(G):
                out_ref[:, g * GROUP_W + hh * HEAD:g * GROUP_W + (hh + 1) * HEAD] = (o[g][hh] * al[g]).astype(BF16)

    specs = [_group_spec(d, tm) for _, d in ATTN_GROUPS]
    return pl.pallas_call(
        body, name="attn_merge", grid=(S // tm,),
        in_specs=specs + [_tile_spec(d, tm) for _, d in ATTN_GROUPS],
        out_specs=pl.BlockSpec((tm, G * GROUP_W), lambda i: (i, 0)),
        out_shape=jax.ShapeDtypeStruct((S, G * GROUP_W), BF16),
        scratch_shapes=[pltpu.VMEM((HEADS_PER_GROUP, tm, HEAD), F32)] * (2 * G),
        compiler_params=_params(("parallel",)))(*os_, *lses)


def _attn_merge_bwd(os_, lses, doa, tm=512):
    G = len(os_)
    S = doa.shape[0]

    def body(*refs):
        o_refs, l_refs, doa_ref = refs[:G], refs[G:2 * G], refs[2 * G]
        do_refs, dd_refs = refs[2 * G + 1:3 * G + 1], refs[3 * G + 1:4 * G + 1]
        scr = refs[4 * G + 1:]
        o = [_gather_heads(o_refs[g], scr[g], d, tm) for g, (_, d) in enumerate(ATTN_GROUPS)]
        l = [_gather_tile(l_refs[g], scr[G + g].at[0], d, tm) for g, (_, d) in enumerate(ATTN_GROUPS)]
        do = [[None] * HEADS_PER_GROUP for _ in range(G)]
        dd = [[None] * HEADS_PER_GROUP for _ in range(G)]
        for hh in range(HEADS_PER_GROUP):
            al = _softmax3([_lane_pick(l[g], hh) for g in range(G)])
            mix = None
            for g in range(G):
                dg = doa_ref[:, g * GROUP_W + hh * HEAD:g * GROUP_W + (hh + 1) * HEAD]
                do[g][hh] = dg * al[g]
                t = al[g] * jnp.sum(dg * o[g][hh], axis=-1, keepdims=True)
                mix = t if mix is None else mix + t
            for g in range(G):
                dd[g][hh] = -al[g] * mix
        for g, (_, d) in enumerate(ATTN_GROUPS):
            _scatter_heads(do[g], scr[2 * G + g], do_refs[g], d, tm)
            _scatter_tile(_lane_place(dd[g]), scr[3 * G + g].at[0], dd_refs[g], d, tm)

    specs = [_group_spec(d, tm) for _, d in ATTN_GROUPS]
    tiles = [_tile_spec(d, tm) for _, d in ATTN_GROUPS]
    do_shapes = [jax.ShapeDtypeStruct((d, S // d, GROUP_W), BF16) for _, d in ATTN_GROUPS]
    dd_shapes = [jax.ShapeDtypeStruct((d, S // d, HEAD), F32) for _, d in ATTN_GROUPS]
    return pl.pallas_call(
        body, name="attn_merge_bwd", grid=(S // tm,),
        in_specs=specs + tiles + [pl.BlockSpec((tm, G * GROUP_W), lambda i: (i, 0))],
        out_specs=specs + tiles,
        out_shape=do_shapes + dd_shapes,
        scratch_shapes=[pltpu.VMEM((HEADS_PER_GROUP, tm, HEAD), F32)] * (4 * G),
        compiler_params=_params(("parallel",)))(*os_, *lses, doa)


def _rope_tables(S):
    inv_freq = 1.0 / (ROPE_THETA ** (jnp.arange(0, HEAD, 2, dtype=F32) / HEAD))
    ang = jnp.arange(S, dtype=F32)[:, None] * inv_freq[None, :]
    cos, sin = jnp.cos(ang), jnp.sin(ang)
    return jnp.concatenate([cos, cos], axis=-1), jnp.concatenate([-sin, sin], axis=-1)


def _local_step(x, target, norm_mix, norm_ffn, lb_logits, out_gain, final_norm, comm):
    S = x.shape[0]
    nm0, nm1 = norm_mix[0:1], norm_mix[1:2]
    nf0, nf1 = norm_ffn[0:1], norm_ffn[1:2]
    w = comm.first_weights()

    proj, u0, got = _norm_mm(x, nm0, w["hin"], "hgrn_in", rider=comm.gather_rider(LATE_WEIGHTS_A))
    w.update(comm.gathered(LATE_WEIGHTS_A, got))
    (o, og, states), got = _hgrn_fwd(proj, lb_logits, out_gain, rider=comm.gather_rider(LATE_WEIGHTS_B))
    w.update(comm.gathered(LATE_WEIGHTS_B, got))
    fin_tn = w["fin0"].shape[2]
    h1 = _mm_res(x, og, w["hout"], "hgrn_out")
    z0, u1, _ = _norm_mm(h1, nf0, w["fin0"], "ffn0_in", out_dtype=BF16)
    h2, act0 = _swiglu_mm_res(h1, z0, w["fdn0"], "ffn0_down")
    cos, sin = _rope_tables(S)
    G = len(ATTN_GROUPS)
    w_groups = w["qkv"].transpose(1, 0, 2).reshape(D_MODEL, 3, G, GROUP_W)
    a_g, u2 = zip(*[_qkv_dilated(h2, nm1, w_groups[:, :, gi, :].reshape(D_MODEL, 3 * GROUP_W), cos, sin, d)
                    for gi, (_, d) in enumerate(ATTN_GROUPS)])
    o_g, lse_g = zip(*[_attn_fwd(a) for a in a_g])
    oa = _attn_merge(o_g, lse_g)
    h3 = _mm_res(h2, oa, w["aout"], "attn_out")
    z1, u3, _ = _norm_mm(h3, nf1, w["fin1"], "ffn1_in", out_dtype=BF16)
    h4, act1 = _swiglu_mm_res(h3, z1, w["fdn1"], "ffn1_down")
    dh4, loss, d_final = _loss_head(h4, final_norm, target)

    grads, small = {}, {"final_norm": d_final}

    def ffn_bwd(dh, h_in, u_in, z, act, gain, w_in, w_dn, tag, ride=None):
        dz = _mm_nt_swiglu_bwd(dh, w_dn, z, tag + "_down_dx")
        g_dn = _mm_tn(act, dh, 1, D_MODEL, D_MODEL, tag + "_down_dw")[0]
        g_in = _mm_tn(u_in, dz, N_CHIPS, fin_tn, fin_tn, tag + "_in_dw")
        rider = None if ride is None else ride(g_in, g_dn)
        dh_in, dgain, got = _mm_nt_normbwd(dz, w_in, h_in, gain, dh, tag + "_in_dx", rider=rider)
        return dh_in, dgain, g_in, g_dn, got

    dh3, d_nf1, grads["fin1"], grads["fdn1"], _ = ffn_bwd(dh4, h3, u3, z1, act1, nf1, w["fin1"], w["fdn1"], "ffn1")
    doa = _mm_nt(dh3, w["aout"][None], "attn_out_dx")
    grads["aout"] = _mm_tn(oa, dh3, 1, D_MODEL, D_MODEL, "attn_out_dw")[0]
    merged = _attn_merge_bwd(o_g, lse_g, doa)
    G = len(ATTN_GROUPS)
    das = [_attn_bwd(a_g[gi], merged[gi], lse_g[gi], merged[G + gi]) for gi in range(G)]
    dqkv = None
    for gi in range(G):
        dqkv = _undilate_group(das[gi], dqkv, cos, sin, gi)
    n_qkv = w["qkv"].shape[2]
    grads["qkv"] = _mm_tn(u2[0], dqkv, N_CHIPS, n_qkv, n_qkv, "attn_qkv_dw")
    dh2, d_nm1, _ = _mm_nt_normbwd(dqkv, w["qkv"], h2, nm1, dh3, "attn_qkv_dx")

    def ride_early(g_in, g_dn):
        return comm.pair_rider({**grads, "fin0": g_in, "fdn0": g_dn}, "early")

    dh1, d_nf0, _, _, got = ffn_bwd(dh2, h1, u1, z0, act0, nf0, w["fin0"], w["fdn0"], "ffn0", ride=ride_early)
    comm.paired("early", got)
    dog = _mm_nt(dh1, w["hout"][None], "hgrn_out_dx")
    (dproj, dlb, dgn), got = _hgrn_bwd(proj, lb_logits, out_gain, o, states, dog, rider=comm.exchange_rider("early"))
    comm.exchanged("early", got)
    late = {"hout": _mm_tn(og, dh1, 1, D_MODEL, D_MODEL, "hgrn_out_dw")[0],
            "hin": _mm_tn(u0, dproj, N_CHIPS, D_MODEL, D_MODEL, "hgrn_in_dw")}
    comm.pair_now(late, "late")
    dx, d_nm0, got = _mm_nt_normbwd(dproj, w["hin"], x, nm0, dh1, "hgrn_in_dx", rider=comm.exchange_rider("late"))
    comm.exchanged("late", got)

    small["norm_mix"] = jnp.concatenate([d_nm0, d_nm1], axis=0)
    small["norm_ffn"] = jnp.concatenate([d_nf0, d_nf1], axis=0)
    small["lb"] = dlb.reshape(1, HGRN_HEADS * HEAD)
    small["out_norm"] = dgn.reshape(HGRN_HEADS, HEAD)
    return loss, dx, small


def _place():
    x, y, c = lax.axis_index("x"), lax.axis_index("y"), lax.axis_index("c")
    others = [(1 - x, y), (x, 1 - y), (1 - x, 1 - y)]
    return x, y, c, others


ANY = pl.BlockSpec(memory_space=pl.ANY)


class _GatherRider:
    def __init__(self, shards):
        self.operands = list(shards)
        n = self.n = len(shards)
        self.out_shape = [jax.ShapeDtypeStruct((N_CHIPS,) + s.shape, s.dtype) for s in shards]
        self.scratch = [pltpu.SemaphoreType.DMA((3 * n,)), pltpu.SemaphoreType.DMA((3 * n,)),
                        pltpu.SemaphoreType.DMA((3 * n,)), pltpu.SemaphoreType.DMA((3 * n,)),
                        pltpu.SemaphoreType.DMA((n,)), pltpu.SemaphoreType.DMA((n,))]

    def _copies(self, ins, outs, sems):
        ici_send, ici_recv, _, _, own_send, own_recv = sems
        x, y, c, others = _place()
        me = 2 * x + y
        own = [pltpu.make_async_remote_copy(
            src_ref=ins[a], dst_ref=outs[a].at[me], send_sem=own_send.at[a], recv_sem=own_recv.at[a],
            device_id=(x, y, 1 - c), device_id_type=MESH) for a in range(self.n)]
        sends = [pltpu.make_async_remote_copy(
            src_ref=ins[a].at[c], dst_ref=outs[a].at[me, c], send_sem=ici_send.at[a * 3 + k], recv_sem=ici_recv.at[a * 3 + k],
            device_id=(ox, oy, c), device_id_type=MESH) for a in range(self.n) for k, (ox, oy) in enumerate(others)]
        return own, sends

    def start(self, ins, outs, sems):
        own, sends = self._copies(ins, outs, sems)
        for cp in own + sends:
            cp.start()

    def finish(self, ins, outs, sems):
        ici_send, ici_recv, d2d_send, d2d_recv, _, _ = sems
        x, y, c, others = _place()
        sibling = (x, y, 1 - c)
        own, sends = self._copies(ins, outs, sems)
        passes = []
        for a in range(self.n):
            for k, (ox, oy) in enumerate(others):
                s = a * 3 + k
                got = outs[a].at[2 * ox + oy, c]
                pltpu.make_async_remote_copy(
                    src_ref=got, dst_ref=got, send_sem=ici_send.at[s], recv_sem=ici_recv.at[s],
                    device_id=(ox, oy, c), device_id_type=MESH).wait_recv()
                fwd = pltpu.make_async_remote_copy(
                    src_ref=got, dst_ref=got, send_sem=d2d_send.at[s], recv_sem=d2d_recv.at[s],
                    device_id=sibling, device_id_type=MESH)
                fwd.start()
                passes.append(fwd)
        for a in range(self.n):
            for k, (ox, oy) in enumerate(others):
                s = a * 3 + k
                theirs = outs[a].at[2 * ox + oy, 1 - c]
                pltpu.make_async_remote_copy(
                    src_ref=theirs, dst_ref=theirs, send_sem=d2d_send.at[s], recv_sem=d2d_recv.at[s],
                    device_id=sibling, device_id_type=MESH).wait_recv()
        for cp in own:
            cp.wait()
        for cp in sends + passes:
            cp.wait_send()


class _PairRider:
    def __init__(self, grads):
        self.operands = list(grads)
        n = self.n = len(grads)
        self.out_shape = [jax.ShapeDtypeStruct((N_CHIPS,) + g.shape[2:], F32) for g in grads]
        self.scratch = [pltpu.SemaphoreType.DMA((N_CHIPS * n,)), pltpu.SemaphoreType.DMA((N_CHIPS * n,))]

    def _copies(self, ins, outs, sems):
        send_sem, recv_sem = sems
        x, y, c, _ = _place()
        return [pltpu.make_async_remote_copy(
            src_ref=ins[a].at[j, 1 - c], dst_ref=outs[a].at[j], send_sem=send_sem.at[a * N_CHIPS + j],
            recv_sem=recv_sem.at[a * N_CHIPS + j], device_id=(x, y, 1 - c), device_id_type=MESH)
            for a in range(self.n) for j in range(N_CHIPS)]

    def start(self, ins, outs, sems):
        for cp in self._copies(ins, outs, sems):
            cp.start()

    def finish(self, ins, outs, sems):
        for cp in self._copies(ins, outs, sems):
            cp.wait()


class _ExchangeRider:
    def __init__(self, parts):
        self.operands = list(parts)
        n = self.n = len(parts)
        self.out_shape = [jax.ShapeDtypeStruct(p.shape, p.dtype) for p in parts]
        self.scratch = [pltpu.SemaphoreType.DMA((3 * n,)), pltpu.SemaphoreType.DMA((3 * n,))]

    def _copies(self, ins, outs, sems):
        send_sem, recv_sem = sems
        x, y, c, others = _place()
        me = 2 * x + y
        return [pltpu.make_async_remote_copy(
            src_ref=ins[a].at[2 * ox + oy], dst_ref=outs[a].at[me], send_sem=send_sem.at[a * 3 + k],
            recv_sem=recv_sem.at[a * 3 + k], device_id=(ox, oy, c), device_id_type=MESH)
            for a in range(self.n) for k, (ox, oy) in enumerate(others)]

    def start(self, ins, outs, sems):
        for cp in self._copies(ins, outs, sems):
            cp.start()

    def finish(self, ins, outs, sems):
        send_sem, recv_sem = sems
        x, y, c, others = _place()
        for a in range(self.n):
            for k, (ox, oy) in enumerate(others):
                s = a * 3 + k
                got = outs[a].at[2 * ox + oy]
                pltpu.make_async_remote_copy(
                    src_ref=got, dst_ref=got, send_sem=send_sem.at[s], recv_sem=recv_sem.at[s],
                    device_id=(ox, oy, c), device_id_type=MESH).wait_recv()
        for cp in self._copies(ins, outs, sems):
            cp.wait_send()


def _run_rider(rider, name):
    n = rider.n

    def body(*refs):
        ins, outs, sems = refs[:n], refs[n:2 * n], refs[2 * n:]
        rider.start(ins, outs, sems)
        rider.finish(ins, outs, sems)

    return pl.pallas_call(
        body, name=name, in_specs=[ANY] * n, out_specs=[ANY] * n,
        out_shape=rider.out_shape, scratch_shapes=rider.scratch)(*rider.operands)


def _ride(rider, body, n_in, n_out, first, last):
    if rider is None:
        return body
    n = rider.n

    def wrapped(*refs):
        host_in, r_in = refs[:n_in], refs[n_in:n_in + n]
        host_out = refs[n_in + n:n_in + n + n_out]
        r_out = refs[n_in + n + n_out:n_in + 2 * n + n_out]
        rest = refs[n_in + 2 * n + n_out:]
        host_scr, sems = rest[:len(rest) - len(rider.scratch)], rest[len(rest) - len(rider.scratch):]

        @pl.when(first())
        def _():
            rider.start(r_in, r_out, sems)

        body(*host_in, *host_out, *host_scr)

        @pl.when(last())
        def _():
            rider.finish(r_in, r_out, sems)

    return wrapped


def _rider_args(rider):
    if rider is None:
        return [], [], [], [], []
    return rider.operands, [ANY] * rider.n, [ANY] * rider.n, rider.out_shape, rider.scratch


def _pair_sum(g, got, c_idx):
    _, _, r, cw = g.shape
    tr = _row_tile(r, cw)

    def body(c_ref, g_ref, got_ref, p_ref, pb_ref):
        v = g_ref[...] + got_ref[...]
        p_ref[...] = v
        pb_ref[...] = v.astype(BF16)

    blk = pl.BlockSpec((None, tr, cw), lambda j, i, c_ref: (j, i, 0))
    return pl.pallas_call(
        body, name="grad_pair_sum",
        grid_spec=pltpu.PrefetchScalarGridSpec(
            num_scalar_prefetch=1, grid=(N_CHIPS, r // tr),
            in_specs=[pl.BlockSpec((None, None, tr, cw), lambda j, i, c_ref: (j, c_ref[0], i, 0)), blk],
            out_specs=[blk, blk]),
        out_shape=[jax.ShapeDtypeStruct((N_CHIPS, r, cw), F32), jax.ShapeDtypeStruct((N_CHIPS, r, cw), BF16)],
        compiler_params=_params(("parallel", "parallel")))(c_idx, g, got)


def _chip_sum(p, got, me_idx):
    _, r, cw = p.shape
    tr = _row_tile(r, cw)

    def body(me_ref, own_ref, got_ref, t_ref):
        me = me_ref[0]
        acc = None
        for s in range(N_CHIPS):
            term = jnp.where(me == s, own_ref[...], got_ref[s].astype(F32))
            acc = term if acc is None else acc + term
        t_ref[...] = acc

    return pl.pallas_call(
        body, name="grad_chip_sum",
        grid_spec=pltpu.PrefetchScalarGridSpec(
            num_scalar_prefetch=1, grid=(r // tr,),
            in_specs=[pl.BlockSpec((None, tr, cw), lambda i, me_ref: (me_ref[0], i, 0)),
                      pl.BlockSpec((N_CHIPS, tr, cw), lambda i, me_ref: (0, i, 0))],
            out_specs=pl.BlockSpec((tr, cw), lambda i, me_ref: (i, 0))),
        out_shape=jax.ShapeDtypeStruct((r, cw), F32),
        compiler_params=_params(("parallel",)))(me_idx, p, got)


def _pair_share(halves):
    n = len(halves)

    def body(*refs):
        ins, outs = refs[:n], refs[n:2 * n]
        send_sem, recv_sem = refs[2 * n:]
        x, y, c, _ = _place()
        cps = [pltpu.make_async_remote_copy(
            src_ref=ins[a], dst_ref=outs[a], send_sem=send_sem.at[a], recv_sem=recv_sem.at[a],
            device_id=(x, y, 1 - c), device_id_type=MESH) for a in range(n)]
        for cp in cps:
            cp.start()
        for cp in cps:
            cp.wait()

    return pl.pallas_call(
        body, name="grad_pair_share",
        in_specs=[ANY] * n, out_specs=[ANY] * n,
        out_shape=[jax.ShapeDtypeStruct(h.shape, F32) for h in halves],
        scratch_shapes=[pltpu.SemaphoreType.DMA((n,)), pltpu.SemaphoreType.DMA((n,))],
        )(*halves)


def _small_allreduce(pack):
    m_per, ncol = pack.shape
    n_dev = 8

    def body(x_ref, sum_ref, all_ref, send_sems, recv_sems, local_sem):
        x, y, c, others = _place()
        me, sibling = (x, y, c), (x, y, 1 - c)

        def rows(px, py, pc):
            return all_ref.at[pl.ds((4 * px + 2 * py + pc) * m_per, m_per), :]

        def copy(k, block, to, src=None):
            return pltpu.make_async_remote_copy(
                src_ref=rows(*block) if src is None else src, dst_ref=rows(*block),
                send_sem=send_sems.at[k], recv_sem=recv_sems.at[k], device_id=to, device_id_type=MESH)

        mine = pltpu.make_async_copy(x_ref, rows(*me), local_sem)
        mine.start()
        first = [copy(0, me, sibling, src=x_ref)]
        first += [copy(1 + j, me, (*chip, c), src=x_ref) for j, chip in enumerate(others)]
        for cp in first:
            cp.start()
        passed = [copy(4 + j, (*chip, c), sibling) for j, chip in enumerate(others)]
        for j, chip in enumerate(others):
            copy(1 + j, (*chip, c), me).wait_recv()
            passed[j].start()
        copy(0, sibling, me).wait_recv()
        for j, chip in enumerate(others):
            copy(4 + j, (*chip, 1 - c), me).wait_recv()
        for cp in first + passed:
            cp.wait_send()
        mine.wait()
        acc = all_ref[0:m_per, :]
        for dvc in range(1, n_dev):
            acc = acc + all_ref[dvc * m_per:(dvc + 1) * m_per, :]
        sum_ref[...] = acc

    return pl.pallas_call(
        body, name="small_allreduce",
        in_specs=[pl.BlockSpec(memory_space=pltpu.VMEM)],
        out_specs=pl.BlockSpec(memory_space=pltpu.VMEM),
        out_shape=jax.ShapeDtypeStruct((m_per, ncol), F32),
        scratch_shapes=[pltpu.VMEM((n_dev * m_per, ncol), F32),
                        pltpu.SemaphoreType.DMA((7,)), pltpu.SemaphoreType.DMA((7,)), pltpu.SemaphoreType.DMA],
        )(pack)


def _adam_math(w, g, m, v):
    m = ADAM_B1 * m + (1.0 - ADAM_B1) * g
    v = ADAM_B2 * v + (1.0 - ADAM_B2) * (g * g)
    m_hat = m / (1.0 - ADAM_B1 ** ADAM_STEP)
    v_hat = v / (1.0 - ADAM_B2 ** ADAM_STEP)
    delta = -ADAM_LR * (m_hat / (jnp.sqrt(v_hat) + ADAM_EPS) + ADAM_WD * w)
    return delta, m, v


def _adamw(halves, c_idx, w, m, v, name):
    L = len(halves)
    r, C = halves[0][0].shape
    tr = _row_tile(r, C, 512 * 1024)
    nt = r // tr

    def body(c_ref, *refs):
        g_refs, (w_ref, m_ref, v_ref), (g_ref, d_ref, nm_ref, nv_ref) = refs[:2 * L], refs[2 * L:2 * L + 3], refs[2 * L + 3:]
        own = pl.program_id(1) == c_ref[0]
        g = None
        for l in range(L):
            cand = jnp.where(own, g_refs[2 * l][...], g_refs[2 * l + 1][...])
            g = cand if g is None else jnp.where(pl.program_id(0) == l, cand, g)
        g_ref[...] = g
        d_ref[...], nm_ref[...], nv_ref[...] = _adam_math(w_ref[...], g, m_ref[...], v_ref[...])

    def half(l, mine):
        def index(ll, h, i, c_ref):
            read = (h == c_ref[0]) if mine else (h != c_ref[0])
            return jnp.where(jnp.logical_and(ll == l, read), i, 0), 0
        return pl.BlockSpec((tr, C), index)

    full = pl.BlockSpec((None, tr, C), lambda ll, h, i, c_ref: (ll, h * nt + i, 0))
    shp = jax.ShapeDtypeStruct((L, 2 * r, C), F32)
    g_specs = [half(l, mine) for l in range(L) for mine in (True, False)]
    return pl.pallas_call(
        body, name=name,
        grid_spec=pltpu.PrefetchScalarGridSpec(
            num_scalar_prefetch=1, grid=(L, 2, nt),
            in_specs=g_specs + [full] * 3, out_specs=[full] * 4),
        out_shape=[shp] * 4,
        compiler_params=_params(("arbitrary", "arbitrary", "arbitrary")))(
            c_idx, *[a for pair in halves for a in pair], w, m, v)


def _small_update(gsum, logits_pack, w, m, v):
    def body(gs_ref, lg_ref, w_ref, m_ref, v_ref, g_ref, d_ref, nm_ref, nv_ref):
        g_ref[...] = gs_ref[...]
        l0, l1, l2 = lg_ref[0:1, :], lg_ref[1:2, :], lg_ref[2:3, :]
        mx = jnp.maximum(jnp.maximum(l0, l1), l2)
        e0, e1, e2 = jnp.exp(l0 - mx), jnp.exp(l1 - mx), jnp.exp(l2 - mx)
        tot = e0 + e1 + e2
        p0, p1, p2 = e0 / tot, e1 / tot, e2 / tot
        dlb = gs_ref[4:5, :]
        g_ref[4:5, :] = dlb * p0 * (1.0 - p0)
        g_ref[5:6, :] = -dlb * p0 * p1
        g_ref[6:7, :] = -dlb * p0 * p2
        d_ref[...], nm_ref[...], nv_ref[...] = _adam_math(w_ref[...], g_ref[...], m_ref[...], v_ref[...])

    full = pl.BlockSpec(memory_space=pltpu.VMEM)
    shp = jax.ShapeDtypeStruct(gsum.shape, F32)
    return pl.pallas_call(
        body, name="small_update", in_specs=[full] * 5, out_specs=[full] * 4, out_shape=[shp] * 4)(
            gsum, logits_pack, w, m, v)


def _pack_small(norm_mix, norm_ffn, lb3, out_norm, final_norm, extra=None):
    ncol = norm_mix.shape[1]
    on = jnp.pad(out_norm.reshape(1, -1), ((0, 0), (0, ncol - out_norm.size)))
    rows = [norm_mix, norm_ffn, lb3, on, final_norm.reshape(1, ncol)]
    if extra is not None:
        rows.append(extra)
    used = sum(r.shape[0] for r in rows)
    rows.append(jnp.zeros((SMALL_ROWS - used, ncol), F32))
    return jnp.concatenate(rows, axis=0)


WEIGHT_NAMES = ("hin", "hout", "qkv", "aout", "fin0", "fin1", "fdn0", "fdn1")
FIRST_WEIGHTS = ("hin",)
LATE_WEIGHTS_A = ("hout", "fin0", "fdn0")
LATE_WEIGHTS_B = ("qkv", "aout", "fin1", "fdn1")


def _split_weights(hgrn_w_in, hgrn_w_out, attn_w_qkv, attn_w_out, ffn_w_in, ffn_w_down):
    return {"hin": hgrn_w_in[0], "hout": hgrn_w_out[0], "qkv": attn_w_qkv[0], "aout": attn_w_out[0],
            "fin0": ffn_w_in[0], "fin1": ffn_w_in[1], "fdn0": ffn_w_down[0], "fdn1": ffn_w_down[1]}


def _halves(v):
    r, c = v.shape
    return v.reshape(2, r // 2, c)


def _full_weights(gathered):
    out = {}
    for k, g in gathered.items():
        _, _, r, c = g.shape
        if k in ("hin", "qkv", "fin0", "fin1"):
            out[k] = g.reshape(N_CHIPS, 2 * r, c)
        else:
            out[k] = g.reshape(N_CHIPS * 2 * r, c)
    return out


class _StepComm:
    def __init__(self, shards, c_idx, me_idx):
        self.shards, self.c_idx, self.me_idx = shards, c_idx, me_idx
        self.halves = {}
        self._stage = {}

    def gather_rider(self, names):
        return _GatherRider([_halves(self.shards[k].astype(BF16)) for k in names])

    def gathered(self, names, got):
        return _full_weights(dict(zip(names, got)))

    def first_weights(self):
        return self.gathered(FIRST_WEIGHTS, _run_rider(self.gather_rider(FIRST_WEIGHTS), "gather_first"))

    def pair_rider(self, grads, tag):
        names = list(grads)
        g4 = []
        for k in names:
            r, c = self.shards[k].shape
            g4.append(grads[k].reshape(N_CHIPS, 2, r // 2, c))
        self._stage[tag] = (names, g4)
        return _PairRider(g4)

    def pair_now(self, grads, tag):
        self.paired(tag, _run_rider(self.pair_rider(grads, tag), "grad_pair_exchange_" + tag))

    def paired(self, tag, got):
        names, g4 = self._stage[tag]
        self._stage[tag] = (names, [_pair_sum(g, s, self.c_idx) for g, s in zip(g4, got)])

    def exchange_rider(self, tag):
        return _ExchangeRider([s[1] for s in self._stage[tag][1]])

    def exchanged(self, tag, got):
        names, sums = self._stage.pop(tag)
        for k, s, g in zip(names, sums, got):
            self.halves[k] = _chip_sum(s[0], g, self.me_idx)

    def shared_halves(self):
        mine = [self.halves[k] for k in WEIGHT_NAMES]
        return dict(zip(WEIGHT_NAMES, zip(mine, _pair_share(mine))))


def kernel(x, norm_mix, norm_ffn, hgrn_w_in, hgrn_lb_logits, hgrn_out_norm, hgrn_w_out, attn_w_qkv, attn_w_out, ffn_w_in, ffn_w_down, final_norm, loss_target, m_norm_mix, m_norm_ffn, m_hgrn_w_in, m_hgrn_lb_logits, m_hgrn_out_norm, m_hgrn_w_out, m_attn_w_qkv, m_attn_w_out, m_ffn_w_in, m_ffn_w_down, m_final_norm, v_norm_mix, v_norm_ffn, v_hgrn_w_in, v_hgrn_lb_logits, v_hgrn_out_norm, v_hgrn_w_out, v_attn_w_qkv, v_attn_w_out, v_ffn_w_in, v_ffn_w_down, v_final_norm):
    S = x.shape[1]
    xi, yi, ci = lax.axis_index("x"), lax.axis_index("y"), lax.axis_index("c")
    c_idx = jnp.reshape(ci, (1,)).astype(jnp.int32)
    me_idx = jnp.reshape(2 * xi + yi, (1,)).astype(jnp.int32)

    w_own = _split_weights(hgrn_w_in, hgrn_w_out, attn_w_qkv, attn_w_out, ffn_w_in, ffn_w_down)

    comm = _StepComm(w_own, c_idx, me_idx)
    loss, dx, small = _local_step(
        x.reshape(S, D_MODEL), loss_target.reshape(S, D_MODEL), norm_mix, norm_ffn, hgrn_lb_logits,
        hgrn_out_norm, final_norm.reshape(1, D_MODEL), comm)

    halves = comm.shared_halves()
    updated = {}
    for tensor, layers, (wt, mt, vt) in (
            ("hgrn_w_in", ("hin",), (hgrn_w_in, m_hgrn_w_in, v_hgrn_w_in)),
            ("hgrn_w_out", ("hout",), (hgrn_w_out, m_hgrn_w_out, v_hgrn_w_out)),
            ("attn_w_qkv", ("qkv",), (attn_w_qkv, m_attn_w_qkv, v_attn_w_qkv)),
            ("attn_w_out", ("aout",), (attn_w_out, m_attn_w_out, v_attn_w_out)),
            ("ffn_w_in", ("fin0", "fin1"), (ffn_w_in, m_ffn_w_in, v_ffn_w_in)),
            ("ffn_w_down", ("fdn0", "fdn1"), (ffn_w_down, m_ffn_w_down, v_ffn_w_down))):
        updated[tensor] = _adamw([halves[k] for k in layers], c_idx, wt, mt, vt, "adamw_" + tensor)

    loss_row = jnp.pad(loss, ((0, 0), (0, D_MODEL - loss.shape[1])))
    lb3 = jnp.concatenate([small["lb"], jnp.zeros((2, D_MODEL), F32)], axis=0)
    on_grad = jnp.sum(small["out_norm"], axis=0, keepdims=True)
    pack = _pack_small(small["norm_mix"], small["norm_ffn"], lb3, on_grad, small["final_norm"], loss_row)
    gsum = _small_allreduce(pack)
    w_s = _pack_small(norm_mix, norm_ffn, hgrn_lb_logits, hgrn_out_norm, final_norm)
    m_s = _pack_small(m_norm_mix, m_norm_ffn, m_hgrn_lb_logits, m_hgrn_out_norm, m_final_norm)
    v_s = _pack_small(v_norm_mix, v_norm_ffn, v_hgrn_lb_logits, v_hgrn_out_norm, v_final_norm)
    lg_pack = jnp.pad(hgrn_lb_logits, ((0, 8 - hgrn_lb_logits.shape[0]), (0, 0)))
    sg, sd, sm, sv = _small_update(gsum, lg_pack, w_s, m_s, v_s)

    def unpack(p):
        return (p[0:2], p[2:4], p[4:7], p[7:8, :HEAD], p[8])

    def assemble(p, which):
        nmx, nff, lbl, onm, fnm = unpack(p)
        hin, hout, qkv, aout, fin, fdn = [updated[t][which] for t in
                                          ("hgrn_w_in", "hgrn_w_out", "attn_w_qkv", "attn_w_out", "ffn_w_in", "ffn_w_down")]
        return (nmx, nff, hin, lbl, onm, hout, qkv, aout, fin, fdn, fnm)

    total_loss = gsum[9, 0]
    return (total_loss, dx.reshape(1, S, D_MODEL), *assemble(sg, 0), *assemble(sd, 1), *assemble(sm, 2), *assemble(sv, 3))
```

```python
import functools

import jax
import jax.numpy as jnp
from jax import lax
from jax.experimental import pallas as pl
from jax.experimental.pallas import tpu as pltpu

F32 = jnp.float32
BF16 = jnp.bfloat16
MESH = pl.DeviceIdType.MESH

D_MODEL = 1024
HEAD = 128
HGRN_HEADS = 8
HGRN_CHUNK = 64
HGRN_HEADS_PER_STEP = 2
ATTN_GROUPS = ((128, 1), (512, 4), (2048, 16))
ATTN_SPAN = 128
HEADS_PER_GROUP = 4
GROUP_W = HEADS_PER_GROUP * HEAD
D_FF = 2816
NORM_EPS = 1e-6
ROPE_THETA = 10000.0
NEG = -1e30

ADAM_LR, ADAM_B1, ADAM_B2, ADAM_EPS, ADAM_WD, ADAM_STEP = 0.001, 0.9, 0.999, 1e-08, 0.01, 10

N_CHIPS = 4
VMEM_LIMIT = 56 * 1024 * 1024
SMALL_ROWS = 16


def _params(sem=None):
    return pltpu.CompilerParams(dimension_semantics=sem, vmem_limit_bytes=VMEM_LIMIT)


def _row_tile(rows, cols, budget_bytes=3 * 512 * 1024):
    best = 8
    for t in range(8, rows + 1, 8):
        if rows % t == 0 and t * cols * 4 <= budget_bytes:
            best = t
    assert rows % best == 0
    return best


def _grid_corner(i, j):
    return jnp.logical_and(pl.program_id(0) == i, pl.program_id(1) == j)


def _sigmoid(v):
    return 0.5 * jnp.tanh(0.5 * v) + 0.5


def _dot(a, b):
    return jnp.dot(a, b, preferred_element_type=F32)


def _dot_nt(a, b):
    return lax.dot_general(a, b, (((1,), (1,)), ((), ())), preferred_element_type=F32)


def _dot_tn(a, b):
    return lax.dot_general(a, b, (((0,), (0,)), ((), ())), preferred_element_type=F32)


def _dot_exact(ones, b):
    ones = ones.astype(BF16)
    hi = b.astype(BF16)
    rest = b - hi.astype(F32)
    mid = rest.astype(BF16)
    low = (rest - mid.astype(F32)).astype(BF16)
    return _dot(ones, hi) + _dot(ones, mid) + _dot(ones, low)


def _rstd(v):
    return lax.rsqrt(jnp.mean(v * v, axis=-1, keepdims=True) + NORM_EPS)


def _norm_mm(h, gain, w3, name, out_dtype=F32, tm=1024, rider=None):
    S, K = h.shape
    J, _, n = w3.shape
    gi = S // tm

    def body(h_ref, g_ref, w_ref, y_ref, u_ref):
        @pl.when(pl.program_id(1) == 0)
        def _():
            v = h_ref[...]
            u_ref[...] = (v * _rstd(v) * g_ref[...]).astype(BF16)

        y_ref[...] = _dot(u_ref[...], w_ref[pl.program_id(1)]).astype(y_ref.dtype)

    r_ops, r_in, r_out, r_shape, r_scr = _rider_args(rider)
    res = pl.pallas_call(
        _ride(rider, body, 3, 2, functools.partial(_grid_corner, 0, 0), functools.partial(_grid_corner, gi - 1, J - 1)),
        name=name, grid=(gi, J),
        in_specs=[pl.BlockSpec((tm, K), lambda i, j: (i, 0)),
                  pl.BlockSpec((1, K), lambda i, j: (0, 0)),
                  pl.BlockSpec((J, K, n), lambda i, j: (0, 0, 0))] + r_in,
        out_specs=[pl.BlockSpec((tm, n), lambda i, j: (i, j)), pl.BlockSpec((tm, K), lambda i, j: (i, 0))] + r_out,
        out_shape=[jax.ShapeDtypeStruct((S, J * n), out_dtype), jax.ShapeDtypeStruct((S, K), BF16)] + r_shape,
        scratch_shapes=r_scr,
        compiler_params=_params(("arbitrary", "arbitrary")))(h, gain, w3, *r_ops)
    return res[0], res[1], res[2:]


def _mm_res(h, a, w2, name, tm=512):
    S, N = h.shape
    K = a.shape[1]

    def body(h_ref, a_ref, w_ref, o_ref):
        o_ref[...] = h_ref[...] + _dot(a_ref[...], w_ref[...])

    return pl.pallas_call(
        body, name=name, grid=(S // tm,),
        in_specs=[pl.BlockSpec((tm, N), lambda i: (i, 0)),
                  pl.BlockSpec((tm, K), lambda i: (i, 0)),
                  pl.BlockSpec((K, N), lambda i: (0, 0))],
        out_specs=pl.BlockSpec((tm, N), lambda i: (i, 0)),
        out_shape=jax.ShapeDtypeStruct((S, N), F32),
        compiler_params=_params(("parallel",)))(h, a, w2)


def _swiglu(z_ref, F):
    g = z_ref[:, :F].astype(F32)
    return (g * _sigmoid(g) * z_ref[:, F:].astype(F32)).astype(BF16)


def _swiglu_mm_res(h, z, w2, name, tm=256):
    S, N = h.shape
    F = w2.shape[0]

    def body(h_ref, z_ref, w_ref, o_ref, a_ref):
        a = _swiglu(z_ref, F)
        a_ref[...] = a
        o_ref[...] = h_ref[...] + _dot(a, w_ref[...])

    return pl.pallas_call(
        body, name=name, grid=(S // tm,),
        in_specs=[pl.BlockSpec((tm, N), lambda i: (i, 0)),
                  pl.BlockSpec((tm, 2 * F), lambda i: (i, 0)),
                  pl.BlockSpec((F, N), lambda i: (0, 0))],
        out_specs=[pl.BlockSpec((tm, N), lambda i: (i, 0)), pl.BlockSpec((tm, F), lambda i: (i, 0))],
        out_shape=[jax.ShapeDtypeStruct((S, N), F32), jax.ShapeDtypeStruct((S, F), BF16)],
        compiler_params=_params(("parallel",)))(h, z, w2)


def _dy_specs(dy, J, n, tm):
    if dy.ndim == 3:
        return [pl.BlockSpec((None, tm, n), functools.partial(lambda i, j: (j, i, 0), j=j)) for j in range(J)]
    return [pl.BlockSpec((tm, n), functools.partial(lambda i, j: (i, j), j=j)) for j in range(J)]


def _acc_nt(dy_refs, w_ref):
    acc = None
    for j, r in enumerate(dy_refs):
        t = _dot_nt(r[...].astype(BF16), w_ref[j])
        acc = t if acc is None else acc + t
    return acc


def _mm_nt(dy, w3, name, out_dtype=F32, tm=512):
    J, K, n = w3.shape
    S = dy.shape[-2]

    def body(*refs):
        dy_refs, w_ref, o_ref = refs[:J], refs[J], refs[J + 1]
        o_ref[...] = _acc_nt(dy_refs, w_ref).astype(o_ref.dtype)

    return pl.pallas_call(
        body, name=name, grid=(S // tm,),
        in_specs=_dy_specs(dy, J, n, tm) + [pl.BlockSpec((J, K, n), lambda i: (0, 0, 0))],
        out_specs=pl.BlockSpec((tm, K), lambda i: (i, 0)),
        out_shape=jax.ShapeDtypeStruct((S, K), out_dtype),
        compiler_params=_params(("parallel",)))(*([dy] * J), w3)


def _mm_nt_normbwd(dy, w3, h, gain, dh, name, tm=512, rider=None):
    J, K, n = w3.shape
    S = h.shape[0]
    steps = S // tm

    def body(*refs):
        dy_refs, w_ref, h_ref, g_ref, dh_ref, o_ref, dg_ref = refs[:J], *refs[J:]
        du = _acc_nt(dy_refs, w_ref)
        v = h_ref[...]
        r = _rstd(v)
        xh = v * r
        dyg = du * g_ref[...]
        o_ref[...] = dh_ref[...] + r * (dyg - xh * jnp.mean(dyg * xh, axis=-1, keepdims=True))

        @pl.when(pl.program_id(0) == 0)
        def _():
            dg_ref[...] = jnp.zeros_like(dg_ref)

        dg_ref[...] += jnp.sum(du * xh, axis=0, keepdims=True)

    row = pl.BlockSpec((tm, K), lambda i: (i, 0))
    vec = pl.BlockSpec((1, K), lambda i: (0, 0))
    r_ops, r_in, r_out, r_shape, r_scr = _rider_args(rider)
    res = pl.pallas_call(
        _ride(rider, body, J + 4, 2, lambda: pl.program_id(0) == 0, lambda: pl.program_id(0) == steps - 1),
        name=name, grid=(steps,),
        in_specs=_dy_specs(dy, J, n, tm) + [pl.BlockSpec((J, K, n), lambda i: (0, 0, 0)), row, vec, row] + r_in,
        out_specs=[row, vec] + r_out,
        out_shape=[jax.ShapeDtypeStruct((S, K), F32), jax.ShapeDtypeStruct((1, K), F32)] + r_shape,
        scratch_shapes=r_scr,
        compiler_params=_params(("arbitrary",)))(*([dy] * J), w3, h, gain, dh, *r_ops)
    return res[0], res[1], res[2:]


def _mm_nt_swiglu_bwd(dh, w2, z, name, tm=256):
    F, N = w2.shape
    S = dh.shape[0]

    def body(dh_ref, w_ref, z_ref, o_ref):
        da = _dot_nt(dh_ref[...].astype(BF16), w_ref[...])
        g = z_ref[:, :F].astype(F32)
        u = z_ref[:, F:].astype(F32)
        sg = _sigmoid(g)
        o_ref[:, :F] = (da * u * (sg * (1.0 + g * (1.0 - sg)))).astype(BF16)
        o_ref[:, F:] = (da * (g * sg)).astype(BF16)

    return pl.pallas_call(
        body, name=name, grid=(S // tm,),
        in_specs=[pl.BlockSpec((tm, N), lambda i: (i, 0)),
                  pl.BlockSpec((F, N), lambda i: (0, 0)),
                  pl.BlockSpec((tm, 2 * F), lambda i: (i, 0))],
        out_specs=pl.BlockSpec((tm, 2 * F), lambda i: (i, 0)),
        out_shape=jax.ShapeDtypeStruct((S, 2 * F), BF16),
        compiler_params=_params(("parallel",)))(dh, w2, z)


def _mm_tn(x, dy, J, n, tn, name):
    tpn = n // tn
    ts = 1024
    S, K = x.shape
    if dy.ndim == 3:
        dy_spec = pl.BlockSpec((None, ts, tn), lambda c, s: (c // tpn, s, c % tpn))
    else:
        dy_spec = pl.BlockSpec((ts, tn), lambda c, s: (s, c))

    def body(x_ref, dy_ref, o_ref):
        @pl.when(pl.program_id(1) == 0)
        def _():
            o_ref[...] = jnp.zeros_like(o_ref)

        o_ref[...] += _dot_tn(x_ref[...], dy_ref[...].astype(BF16))

    return pl.pallas_call(
        body, name=name, grid=(J * tpn, S // ts),
        in_specs=[pl.BlockSpec((ts, K), lambda c, s: (s, 0)), dy_spec],
        out_specs=pl.BlockSpec((None, K, tn), lambda c, s: (c // tpn, 0, c % tpn)),
        out_shape=jax.ShapeDtypeStruct((J, K, n), F32),
        compiler_params=_params(("parallel", "arbitrary")))(x, dy)


def _loss_head(h, gain, target, tm=512):
    S, K = h.shape

    def body(h_ref, g_ref, t_ref, dh_ref, loss_ref, dg_ref):
        v = h_ref[...]
        r = _rstd(v)
        xh = v * r
        g = g_ref[...]
        dy = (xh * g - t_ref[...]) * (1.0 / K)
        dyg = dy * g
        dh_ref[...] = r * (dyg - xh * jnp.mean(dyg * xh, axis=-1, keepdims=True))

        @pl.when(pl.program_id(0) == 0)
        def _():
            loss_ref[...] = jnp.zeros_like(loss_ref)
            dg_ref[...] = jnp.zeros_like(dg_ref)

        part = jnp.sum(jnp.sum(dy * dy, axis=-1, keepdims=True), axis=0, keepdims=True) * (0.5 * K)
        lane = lax.broadcasted_iota(jnp.int32, loss_ref.shape, 1)
        loss_ref[...] += jnp.where(lane == 0, part, 0.0)
        dg_ref[...] += jnp.sum(dy * xh, axis=0, keepdims=True)

    row = pl.BlockSpec((tm, K), lambda i: (i, 0))
    vec = pl.BlockSpec((1, K), lambda i: (0, 0))
    return pl.pallas_call(
        body, name="loss_head", grid=(S // tm,),
        in_specs=[row, vec, row],
        out_specs=[row, pl.BlockSpec((1, HEAD), lambda i: (0, 0)), vec],
        out_shape=[jax.ShapeDtypeStruct((S, K), F32), jax.ShapeDtypeStruct((1, HEAD), F32),
                   jax.ShapeDtypeStruct((1, K), F32)],
        compiler_params=_params(("arbitrary",)))(h, gain, target)


def _lower_bound(lg_ref):
    l0, l1, l2 = lg_ref[0:1, :], lg_ref[1:2, :], lg_ref[2:3, :]
    mx = jnp.maximum(jnp.maximum(l0, l1), l2)
    e0, e1, e2 = jnp.exp(l0 - mx), jnp.exp(l1 - mx), jnp.exp(l2 - mx)
    return e0 / (e0 + e1 + e2)


def _chunks(v, ncb):
    C = HGRN_CHUNK
    return [v[c * C:(c + 1) * C] for c in range(ncb)]


def _rows(parts):
    return jnp.concatenate(parts, axis=0)


def _block_gates(qz, fz, lb, ncb):
    C = HGRN_CHUNK
    row = lax.broadcasted_iota(jnp.int32, (C, C), 0)
    col = lax.broadcasted_iota(jnp.int32, (C, C), 1)
    tri = (col <= row).astype(F32)
    first_half = lax.broadcasted_iota(jnp.int32, (C, HEAD), 0) < C // 2
    sig = _sigmoid(fz)
    fg = lb + (1.0 - lb) * sig
    key = 1.0 - fg
    lg = jnp.log(fg)
    lgs = _chunks(lg, ncb)
    b = _rows([_dot_exact(tri, v) for v in lgs])
    r_c = [jnp.sum(jnp.where(first_half, v, 0.0), axis=0, keepdims=True) for v in lgs]
    bl_c = [jnp.sum(v, axis=0, keepdims=True) for v in lgs]
    r = _rows([jnp.broadcast_to(v, (C, HEAD)) for v in r_c])
    e_br, e_rb = jnp.exp(b - r), jnp.exp(r - b)
    e_b = e_br * _rows([jnp.broadcast_to(jnp.exp(v), (C, HEAD)) for v in r_c])
    e_lb = e_rb * _rows([jnp.broadcast_to(jnp.exp(e - v), (C, HEAD)) for e, v in zip(bl_c, r_c)])
    sq = _sigmoid(qz)
    qy = qz * sq
    return sig, fg, key, (e_br, e_rb, e_b, e_lb), bl_c, sq, qy


def _hgrn_fwd(proj, logits, gain, tb=1024, rider=None):
    S = proj.shape[0]
    H, C = HGRN_HEADS, HGRN_CHUNK
    ncb = tb // C

    def one_head(q_ref, f_ref, i_ref, g_ref, lg_ref, gn_ref, o_ref, og_ref, st_ref, state):
        @pl.when(pl.program_id(1) == 0)
        def _():
            state[...] = jnp.zeros_like(state)

        lb = _lower_bound(lg_ref)
        causal = lax.broadcasted_iota(jnp.int32, (C, C), 1) <= lax.broadcasted_iota(jnp.int32, (C, C), 0)
        qz, fz, gz = q_ref[...], f_ref[...], g_ref[...]
        _, _, key, (e_br, e_rb, e_b, e_lb), bl_c, _, qy = _block_gates(qz, fz, lb, ncb)
        qs = _chunks((qy * e_br).astype(BF16), ncb)
        ks = _chunks((key * e_rb).astype(BF16), ncb)
        qb = _chunks((qy * e_b).astype(BF16), ncb)
        ke = _chunks((key * e_lb).astype(BF16), ncb)
        vb = _chunks(i_ref[...].astype(BF16), ncb)
        a = [jnp.where(causal, _dot_nt(qs[c], ks[c]), 0.0).astype(BF16) for c in range(ncb)]
        upd = [_dot_tn(vb[c], ke[c]) for c in range(ncb)]
        o_intra = [_dot(a[c], vb[c]) for c in range(ncb)]
        st = state[...]
        e_l = [jnp.exp(v) for v in bl_c]
        sts = []
        for c in range(ncb):
            sts.append(st)
            st = st * e_l[c] + upd[c]
        state[...] = st
        for c in range(ncb):
            st_ref[c] = sts[c]
        o = _rows([_dot_nt(qb[c], sts[c].astype(BF16)) + o_intra[c] for c in range(ncb)])
        o_ref[...] = o
        og_ref[...] = ((o * _rstd(o) * gn_ref[...]) * (gz * _sigmoid(gz))).astype(BF16)

    def body(q_ref, f_ref, i_ref, g_ref, lg_ref, gn_ref, o_ref, og_ref, st_ref, state):
        for hs in range(HP):
            cols = slice(hs * HEAD, (hs + 1) * HEAD)
            one_head(q_ref.at[:, cols], f_ref.at[:, cols], i_ref.at[:, cols], g_ref.at[:, cols], lg_ref.at[:, cols],
                     gn_ref, o_ref.at[:, cols], og_ref.at[:, cols], st_ref.at[hs], state.at[hs])

    HP, wide = HGRN_HEADS_PER_STEP, HGRN_HEADS_PER_STEP * HEAD
    hg = H // HP

    def part(p):
        return pl.BlockSpec((tb, wide), functools.partial(lambda h, i, p: (i, p * hg + h), p=p))

    nb = S // tb
    r_ops, r_in, r_out, r_shape, r_scr = _rider_args(rider)
    res = pl.pallas_call(
        _ride(rider, body, 6, 3, functools.partial(_grid_corner, 0, 0), functools.partial(_grid_corner, hg - 1, nb - 1)),
        name="hgrn_fwd", grid=(hg, nb),
        in_specs=[part(0), part(1), part(2), part(3),
                  pl.BlockSpec((3, wide), lambda h, i: (0, h)),
                  pl.BlockSpec((1, HEAD), lambda h, i: (0, 0))] + r_in,
        out_specs=[pl.BlockSpec((tb, wide), lambda h, i: (i, h)),
                   pl.BlockSpec((tb, wide), lambda h, i: (i, h)),
                   pl.BlockSpec((HP, ncb, HEAD, HEAD), lambda h, i: (h, i, 0, 0))] + r_out,
        out_shape=[jax.ShapeDtypeStruct((S, H * HEAD), F32),
                   jax.ShapeDtypeStruct((S, H * HEAD), BF16),
                   jax.ShapeDtypeStruct((H, S // C, HEAD, HEAD), F32)] + r_shape,
        scratch_shapes=[pltpu.VMEM((HP, HEAD, HEAD), F32)] + r_scr,
        compiler_params=_params(("arbitrary", "arbitrary")))(proj, proj, proj, proj, logits, gain, *r_ops)
    return res[:3], res[3:]


def _hgrn_bwd(proj, logits, gain, o, states, dog, tb=1024, rider=None):
    S = proj.shape[0]
    H, C = HGRN_HEADS, HGRN_CHUNK
    ncb = tb // C
    nb = S // tb

    def one_head(q_ref, f_ref, i_ref, g_ref, lg_ref, gn_ref, o_ref, st_ref, dog_ref,
                 dp_ref, dlb_ref, dgn_ref, dstate, dst_scr):
        @pl.when(pl.program_id(1) == 0)
        def _():
            dstate[...] = jnp.zeros_like(dstate)
            dlb_ref[...] = jnp.zeros_like(dlb_ref)
            dgn_ref[...] = jnp.zeros_like(dgn_ref)

        lb = _lower_bound(lg_ref)
        oml = 1.0 - lb
        gn = gn_ref[...]
        row = lax.broadcasted_iota(jnp.int32, (C, C), 0)
        col = lax.broadcasted_iota(jnp.int32, (C, C), 1)
        causal = col <= row
        tri_up = (col >= row).astype(F32)
        qz, fz, gz = q_ref[...], f_ref[...], g_ref[...]
        sig, fg, key, (e_br, e_rb, e_b, e_lb), bl_c, sq, qy = _block_gates(qz, fz, lb, ncb)
        qs_v, ks_v = (qy * e_br).astype(BF16), (key * e_rb).astype(BF16)
        qb_v, ke_v = (qy * e_b).astype(BF16), (key * e_lb).astype(BF16)
        qs, ks, qb, ke = _chunks(qs_v, ncb), _chunks(ks_v, ncb), _chunks(qb_v, ncb), _chunks(ke_v, ncb)
        vb = _chunks(i_ref[...].astype(BF16), ncb)
        ov = o_ref[...]
        rs = _rstd(ov)
        xh = ov * rs
        sg = _sigmoid(gz)
        dog_v = dog_ref[...]
        dgz = dog_v * (xh * gn) * (sg * (1.0 + gz * (1.0 - sg)))
        don = dog_v * (gz * sg)
        dgn_ref[...] += jnp.sum(don * xh, axis=0, keepdims=True)
        dyg = don * gn
        do = rs * (dyg - xh * jnp.mean(dyg * xh, axis=-1, keepdims=True))
        dob = _chunks(do.astype(BF16), ncb)
        CH = range(ncb)
        a = [jnp.where(causal, _dot_nt(qs[c], ks[c]), 0.0).astype(BF16) for c in CH]
        da = [jnp.where(causal, _dot_nt(dob[c], vb[c]), 0.0).astype(BF16) for c in CH]
        wst = [_dot_tn(dob[c], qb[c]) for c in CH]
        dv_in = [_dot_tn(a[c], dob[c]) for c in CH]
        dqs = [_dot(da[c], ks[c]) for c in CH]
        dks = [_dot_tn(da[c], qs[c]) for c in CH]
        e_l = [jnp.exp(v) for v in bl_c]
        dst = dstate[...]
        for c in reversed(range(ncb)):
            dst_scr[c] = dst
            dst = wst[c] + dst * e_l[c]
        dstate[...] = dst
        dst1b = [dst_scr[c].astype(BF16) for c in CH]
        dqb = [_dot(dob[c], st_ref[c].astype(BF16)) for c in CH]
        dke = [_dot(vb[c], dst1b[c]) for c in CH]
        dv = [dv_in[c] + _dot_nt(ke[c], dst1b[c]) for c in CH]
        dbl_st = [jnp.sum(dst_scr[c] * st_ref[c], axis=0, keepdims=True) * e_l[c] for c in CH]
        dqs, dks, dqb, dke, dv = _rows(dqs), _rows(dks), _rows(dqb), _rows(dke), _rows(dv)
        dke_ke = dke * ke_v.astype(F32)
        db = dqs * qs_v.astype(F32) - dks * ks_v.astype(F32) + dqb * qb_v.astype(F32) - dke_ke
        dlg = []
        for c, (db_c, kk_c) in enumerate(zip(_chunks(db, ncb), _chunks(dke_ke, ncb))):
            dbl = jnp.sum(kk_c, axis=0, keepdims=True) + dbl_st[c]
            dlg.append(_dot_exact(tri_up, db_c) + dbl)
        dlg = _rows(dlg)
        dkey = dks * e_rb + dke * e_lb
        dqy = dqs * e_br + dqb * e_b
        dfg = dlg / fg - dkey
        dlb_ref[...] += jnp.sum(dfg * (1.0 - sig), axis=0, keepdims=True)
        dp_ref[0] = (dqy * (sq * (1.0 + qz * (1.0 - sq)))).astype(BF16)
        dp_ref[1] = (dfg * oml * sig * (1.0 - sig)).astype(BF16)
        dp_ref[2] = dv.astype(BF16)
        dp_ref[3] = dgz.astype(BF16)

    def body(q_ref, f_ref, i_ref, g_ref, lg_ref, gn_ref, o_ref, st_ref, dog_ref,
             dp_ref, dlb_ref, dgn_ref, dstate, dst_scr):
        for hs in range(HP):
            cols = slice(hs * HEAD, (hs + 1) * HEAD)
            one_head(q_ref.at[:, cols], f_ref.at[:, cols], i_ref.at[:, cols], g_ref.at[:, cols], lg_ref.at[:, cols],
                     gn_ref, o_ref.at[:, cols], st_ref.at[hs], dog_ref.at[:, cols],
                     dp_ref.at[:, :, cols], dlb_ref.at[hs], dgn_ref.at[hs], dstate.at[hs], dst_scr)

    HP, wide = HGRN_HEADS_PER_STEP, HGRN_HEADS_PER_STEP * HEAD
    hg = H // HP

    def part(p):
        return pl.BlockSpec((tb, wide), functools.partial(lambda h, i, p: (nb - 1 - i, p * hg + h), p=p))

    blk = pl.BlockSpec((tb, wide), lambda h, i: (nb - 1 - i, h))
    acc = pl.BlockSpec((HP, 1, HEAD), lambda h, i: (h, 0, 0))
    r_ops, r_in, r_out, r_shape, r_scr = _rider_args(rider)
    res = pl.pallas_call(
        _ride(rider, body, 9, 3, functools.partial(_grid_corner, 0, 0), functools.partial(_grid_corner, hg - 1, nb - 1)),
        name="hgrn_bwd", grid=(hg, nb),
        in_specs=[part(0), part(1), part(2), part(3),
                  pl.BlockSpec((3, wide), lambda h, i: (0, h)),
                  pl.BlockSpec((1, HEAD), lambda h, i: (0, 0)),
                  blk,
                  pl.BlockSpec((HP, ncb, HEAD, HEAD), lambda h, i: (h, nb - 1 - i, 0, 0)),
                  blk] + r_in,
        out_specs=[pl.BlockSpec((4, tb, wide), lambda h, i: (0, nb - 1 - i, h)), acc, acc] + r_out,
        out_shape=[jax.ShapeDtypeStruct((4, S, H * HEAD), BF16),
                   jax.ShapeDtypeStruct((H, 1, HEAD), F32),
                   jax.ShapeDtypeStruct((H, 1, HEAD), F32)] + r_shape,
        scratch_shapes=[pltpu.VMEM((HP, HEAD, HEAD), F32), pltpu.VMEM((ncb, HEAD, HEAD), F32)] + r_scr,
        compiler_params=_params(("arbitrary", "arbitrary")))(
            proj, proj, proj, proj, logits, gain, o, states, dog, *r_ops)
    return res[:3], res[3:]


def _rope(v, cos, sin):
    return v * cos + pltpu.roll(v, HEAD // 2, 1) * sin


def _lane_pick(tile, hh):
    lane = lax.broadcasted_iota(jnp.int32, tile.shape, 1)
    return jnp.sum(jnp.where(lane == hh, tile, 0.0), axis=-1, keepdims=True)


def _lane_place(cols):
    rows = cols[0].shape[0]
    lane = lax.broadcasted_iota(jnp.int32, (rows, HEAD), 1)
    tile = jnp.zeros((rows, HEAD), F32)
    for hh, v in enumerate(cols):
        tile = jnp.where(lane == hh, v, tile)
    return tile


def _band_masks():
    qi = lax.broadcasted_iota(jnp.int32, (ATTN_SPAN, ATTN_SPAN), 0)
    kj = lax.broadcasted_iota(jnp.int32, (ATTN_SPAN, ATTN_SPAN), 1)
    return kj <= qi, kj >= qi


ATTN_TILE_BLOCKS = 4


def _attn_fwd(a):
    d, L, _ = a.shape
    B, W = ATTN_TILE_BLOCKS, ATTN_SPAN
    T = B * W
    assert L % T == 0
    steps = L // T
    scale = HEAD ** -0.5

    def body(q_ref, kc_ref, kp_ref, vc_ref, vp_ref, o_ref, lse_ref):
        n = pl.program_id(1)
        mask_c, mask_p0 = _band_masks()
        first = jnp.logical_and(mask_p0, n > 0)
        units = [(b, hh) for b in range(B) for hh in range(HEADS_PER_GROUP)]
        rows = [slice(b * W, (b + 1) * W) for b in range(B)]
        cols = [slice(hh * HEAD, (hh + 1) * HEAD) for hh in range(HEADS_PER_GROUP)]

        def prev_keys(ref, tile, b, hh):
            return ref[:, cols[hh]] if b == 0 else tile[rows[b - 1], cols[hh]]

        s_c = [jnp.where(mask_c, _dot_nt(q_ref[rows[b], cols[hh]], kc_ref[rows[b], cols[hh]]) * scale, NEG) for b, hh in units]
        s_p = [jnp.where(first if b == 0 else mask_p0,
                         _dot_nt(q_ref[rows[b], cols[hh]], prev_keys(kp_ref, kc_ref, b, hh)) * scale, NEG) for b, hh in units]
        m = [jnp.maximum(jnp.max(x, axis=-1, keepdims=True), jnp.max(y, axis=-1, keepdims=True)) for x, y in zip(s_c, s_p)]
        p_c = [jnp.exp(x - mm) for x, mm in zip(s_c, m)]
        p_p = [jnp.exp(y - mm) for y, mm in zip(s_p, m)]
        l = [jnp.sum(x, axis=-1, keepdims=True) + jnp.sum(y, axis=-1, keepdims=True) for x, y in zip(p_c, p_p)]
        acc = [_dot(p_c[i].astype(BF16), vc_ref[rows[b], cols[hh]]) + _dot(p_p[i].astype(BF16), prev_keys(vp_ref, vc_ref, b, hh))
               for i, (b, hh) in enumerate(units)]
        for i, (b, hh) in enumerate(units):
            o_ref[rows[b], cols[hh]] = (acc[i] / l[i]).astype(BF16)
        for b in range(B):
            lse_ref[rows[b], :] = _lane_place([m[i] + jnp.log(l[i]) for i, (bb, _) in enumerate(units) if bb == b])

    def cur(part):
        return pl.BlockSpec((None, T, GROUP_W), functools.partial(lambda r, n, p: (r, n, p), p=part))

    def prev(part):
        return pl.BlockSpec((None, W, GROUP_W), functools.partial(lambda r, n, p: (r, jnp.maximum(n * B - 1, 0), p), p=part))

    return pl.pallas_call(
        body, name=f"attn_fwd_d{d}", grid=(d, steps),
        in_specs=[cur(0), cur(1), prev(1), cur(2), prev(2)],
        out_specs=[pl.BlockSpec((None, T, GROUP_W), lambda r, n: (r, n, 0)), pl.BlockSpec((None, T, HEAD), lambda r, n: (r, n, 0))],
        out_shape=[jax.ShapeDtypeStruct((d, L, GROUP_W), BF16), jax.ShapeDtypeStruct((d, L, HEAD), F32)],
        compiler_params=_params(("parallel", "arbitrary")))(a, a, a, a, a)


def _attn_bwd(a, do, lse, dd):
    d, L, _ = a.shape
    B, W = ATTN_TILE_BLOCKS, ATTN_SPAN
    T = B * W
    assert L % T == 0
    steps = L // T
    scale = HEAD ** -0.5

    def body(qc_ref, qn_ref, kp_ref, kc_ref, vp_ref, vc_ref, doc_ref, don_ref, lc_ref, ln_ref, ddc_ref, ddn_ref, da_ref):
        n = pl.program_id(1)
        mask_c, mask_p0 = _band_masks()
        first = jnp.logical_and(mask_p0, n > 0)
        last = jnp.logical_and(mask_p0, n < steps - 1)
        H4 = range(HEADS_PER_GROUP)
        units = [(b, hh) for b in range(B) for hh in H4]
        rows = [slice(b * W, (b + 1) * W) for b in range(B)]
        cols = [slice(hh * HEAD, (hh + 1) * HEAD) for hh in H4]
        q = {u: qc_ref[rows[u[0]], cols[u[1]]] for u in units}
        k = {u: kc_ref[rows[u[0]], cols[u[1]]] for u in units}
        v = {u: vc_ref[rows[u[0]], cols[u[1]]] for u in units}
        g_o = {u: doc_ref[rows[u[0]], cols[u[1]]] for u in units}
        kb = {(b, hh): kp_ref[:, cols[hh]] if b == 0 else k[(b - 1, hh)] for b, hh in units}
        vb = {(b, hh): vp_ref[:, cols[hh]] if b == 0 else v[(b - 1, hh)] for b, hh in units}
        lse_t = {(b, hh): _lane_pick(lc_ref[rows[b], :], hh) for b, hh in units}
        dd_t = {(b, hh): _lane_pick(ddc_ref[rows[b], :], hh) for b, hh in units}
        p_c = {u: jnp.where(mask_c, jnp.exp(_dot_nt(q[u], k[u]) * scale - lse_t[u]), 0.0) for u in units}
        p_p = {u: jnp.where(first if u[0] == 0 else mask_p0, jnp.exp(_dot_nt(q[u], kb[u]) * scale - lse_t[u]), 0.0) for u in units}
        ds_c = {u: (p_c[u] * (_dot_nt(g_o[u], v[u]) + dd_t[u])).astype(BF16) for u in units}
        ds_p = {u: (p_p[u] * (_dot_nt(g_o[u], vb[u]) + dd_t[u])).astype(BF16) for u in units}
        qn = [qn_ref[:, c] for c in cols]
        g_n = [don_ref[:, c] for c in cols]
        p_n = [jnp.where(last, jnp.exp(_dot_nt(qn[hh], k[(B - 1, hh)]) * scale - _lane_pick(ln_ref[...], hh)), 0.0) for hh in H4]
        ds_n = [(p_n[hh] * (_dot_nt(g_n[hh], v[(B - 1, hh)]) + _lane_pick(ddn_ref[...], hh))).astype(BF16) for hh in H4]
        dq = {u: (_dot(ds_c[u], k[u]) + _dot(ds_p[u], kb[u])) * scale for u in units}
        dk, dv = {}, {}
        for b, hh in units:
            if b < B - 1:
                nxt = (b + 1, hh)
                dk[(b, hh)] = (_dot_tn(ds_c[(b, hh)], q[(b, hh)]) + _dot_tn(ds_p[nxt], q[nxt])) * scale
                dv[(b, hh)] = _dot_tn(p_c[(b, hh)].astype(BF16), g_o[(b, hh)]) + _dot_tn(p_p[nxt].astype(BF16), g_o[nxt])
            else:
                dk[(b, hh)] = (_dot_tn(ds_c[(b, hh)], q[(b, hh)]) + _dot_tn(ds_n[hh], qn[hh])) * scale
                dv[(b, hh)] = _dot_tn(p_c[(b, hh)].astype(BF16), g_o[(b, hh)]) + _dot_tn(p_n[hh].astype(BF16), g_n[hh])
        for b, hh in units:
            da_ref[rows[b], cols[hh]] = dq[(b, hh)].astype(BF16)
            da_ref[rows[b], GROUP_W + hh * HEAD:GROUP_W + (hh + 1) * HEAD] = dk[(b, hh)].astype(BF16)
            da_ref[rows[b], 2 * GROUP_W + hh * HEAD:2 * GROUP_W + (hh + 1) * HEAD] = dv[(b, hh)].astype(BF16)

    nb = L // W

    def cur(width, part):
        return pl.BlockSpec((None, T, width), functools.partial(lambda r, n, p: (r, n, p), p=part))

    def prev(width, part):
        return pl.BlockSpec((None, W, width), functools.partial(lambda r, n, p: (r, jnp.maximum(n * B - 1, 0), p), p=part))

    def nxt(width, part):
        return pl.BlockSpec((None, W, width), functools.partial(lambda r, n, p: (r, jnp.minimum(n * B + B, nb - 1), p), p=part))

    g = GROUP_W
    return pl.pallas_call(
        body, name=f"attn_bwd_d{d}", grid=(d, steps),
        in_specs=[cur(g, 0), nxt(g, 0), prev(g, 1), cur(g, 1), prev(g, 2), cur(g, 2),
                  cur(g, 0), nxt(g, 0), cur(HEAD, 0), nxt(HEAD, 0), cur(HEAD, 0), nxt(HEAD, 0)],
        out_specs=pl.BlockSpec((None, T, 3 * g), lambda r, n: (r, n, 0)),
        out_shape=jax.ShapeDtypeStruct((d, L, 3 * g), BF16),
        compiler_params=_params(("parallel", "arbitrary")))(
            a, a, a, a, a, a, do, do, lse, lse, dd, dd)


def _softmax3(ls):
    mx = jnp.maximum(jnp.maximum(ls[0], ls[1]), ls[2])
    es = [jnp.exp(v - mx) for v in ls]
    tot = es[0] + es[1] + es[2]
    return [e / tot for e in es]


HEAD_COLS = [slice(hh * HEAD, (hh + 1) * HEAD) for hh in range(HEADS_PER_GROUP)]


def _group_spec(d, tm):
    return pl.BlockSpec((d, tm // d, GROUP_W), lambda i: (0, i, 0))


def _gather_heads(ref, scr, d, tm):
    if d == 1:
        return [ref[0, :, cols].astype(F32) for cols in HEAD_COLS]
    for hh, cols in enumerate(HEAD_COLS):
        for r in range(d):
            scr.at[hh][pl.ds(r, tm // d, stride=d), :] = ref[r, :, cols].astype(F32)
    return [scr[hh] for hh in range(HEADS_PER_GROUP)]


def _tile_spec(d, tm):
    return pl.BlockSpec((d, tm // d, HEAD), lambda i: (0, i, 0))


def _gather_tile(ref, scr, d, tm):
    if d == 1:
        return ref[0]
    for r in range(d):
        scr[pl.ds(r, tm // d, stride=d), :] = ref[r]
    return scr[...]


def _scatter_tile(val, scr, ref, d, tm):
    if d == 1:
        ref[0] = val
        return
    scr[...] = val
    for r in range(d):
        ref[r] = scr[pl.ds(r, tm // d, stride=d), :]


def _scatter_heads(vals, scr, ref, d, tm):
    if d == 1:
        for cols, v in zip(HEAD_COLS, vals):
            ref[0, :, cols] = v.astype(ref.dtype)
        return
    for hh, v in enumerate(vals):
        scr[hh] = v
    for hh, cols in enumerate(HEAD_COLS):
        for r in range(d):
            ref[r, :, cols] = scr.at[hh][pl.ds(r, tm // d, stride=d), :].astype(ref.dtype)


def _qkv_dilated(h, gain, wg, cos, sin, d, tm=512):
    S, K = h.shape

    def body(h_ref, g_ref, w_ref, cos_ref, sin_ref, out_ref, u_ref, y_scr):
        p = pl.program_id(1)

        @pl.when(p == 0)
        def _():
            v = h_ref[...]
            u_ref[...] = (v * _rstd(v) * g_ref[...]).astype(BF16)

        y = _dot(u_ref[...], w_ref[...])
        heads = [slice(hh * HEAD, (hh + 1) * HEAD) for hh in range(HEADS_PER_GROUP)]
        if d > 1:
            for hh, cols in enumerate(heads):
                y_scr[hh] = y[:, cols]

        def rows_of(hh, r):
            return y[:, heads[hh]] if d == 1 else y_scr.at[hh][pl.ds(r, tm // d, stride=d), :]

        @pl.when(p < 2)
        def _():
            for r in range(d):
                rows = slice(None) if d == 1 else pl.ds(r, tm // d, stride=d)
                cr, sr = cos_ref[rows, :], sin_ref[rows, :]
                for hh, cols in enumerate(heads):
                    out_ref[r, :, cols] = _rope(rows_of(hh, r), cr, sr).astype(BF16)

        @pl.when(p == 2)
        def _():
            for r in range(d):
                for hh, cols in enumerate(heads):
                    out_ref[r, :, cols] = rows_of(hh, r).astype(BF16)

    tab = pl.BlockSpec((tm, HEAD), lambda i, p: (i, 0))
    return pl.pallas_call(
        body, name=f"attn_qkv_d{d}", grid=(S // tm, 3),
        in_specs=[pl.BlockSpec((tm, K), lambda i, p: (i, 0)),
                  pl.BlockSpec((1, K), lambda i, p: (0, 0)),
                  pl.BlockSpec((K, GROUP_W), lambda i, p: (0, p)), tab, tab],
        out_specs=[pl.BlockSpec((d, tm // d, GROUP_W), lambda i, p: (0, i, p)), pl.BlockSpec((tm, K), lambda i, p: (i, 0))],
        out_shape=[jax.ShapeDtypeStruct((d, S // d, 3 * GROUP_W), BF16), jax.ShapeDtypeStruct((S, K), BF16)],
        scratch_shapes=[pltpu.VMEM((HEADS_PER_GROUP, tm, HEAD), F32)],
        compiler_params=_params(("parallel", "arbitrary")))(h, gain, wg, cos, sin)


def _undilate_group(da, dqkv, cos, sin, g, tm=512):
    d, L, _ = da.shape
    S = d * L
    G = len(ATTN_GROUPS)

    def body(*refs):
        da_ref, cos_ref, sin_ref, out_ref, scr = refs[0], refs[1], refs[2], refs[-2], refs[-1]
        p = pl.program_id(1)
        heads = [slice(hh * HEAD, (hh + 1) * HEAD) for hh in range(HEADS_PER_GROUP)]
        if d > 1:
            for hh, cols in enumerate(heads):
                for r in range(d):
                    scr.at[hh][pl.ds(r, tm // d, stride=d), :] = da_ref[r, :, cols].astype(F32)

        def tokens(hh):
            return da_ref[0, :, heads[hh]].astype(F32) if d == 1 else scr[hh]

        @pl.when(p < 2)
        def _():
            cr, sr = cos_ref[...], -sin_ref[...]
            for hh, cols in enumerate(heads):
                out_ref[:, cols] = _rope(tokens(hh), cr, sr).astype(BF16)

        @pl.when(p == 2)
        def _():
            for hh, cols in enumerate(heads):
                out_ref[:, cols] = tokens(hh).astype(BF16)

    tab = pl.BlockSpec((tm, HEAD), lambda i, p: (i, 0))
    operands = (da, cos, sin) if dqkv is None else (da, cos, sin, dqkv)
    return pl.pallas_call(
        body, name=f"attn_undilate_d{d}", grid=(S // tm, 3),
        in_specs=[pl.BlockSpec((d, tm // d, GROUP_W), lambda i, p: (0, i, p)), tab, tab] + ([] if dqkv is None else [ANY]),
        out_specs=pl.BlockSpec((tm, GROUP_W), lambda i, p: (i, p * G + g)),
        out_shape=jax.ShapeDtypeStruct((S, 3 * G * GROUP_W), BF16),
        input_output_aliases={} if dqkv is None else {3: 0},
        scratch_shapes=[pltpu.VMEM((HEADS_PER_GROUP, tm, HEAD), F32)],
        compiler_params=_params(("parallel", "arbitrary")))(*operands)


def _attn_merge(os_, lses, tm=512):
    G = len(os_)
    S = os_[0].shape[0] * os_[0].shape[1]

    def body(*refs):
        o_refs, l_refs, out_ref = refs[:G], refs[G:2 * G], refs[2 * G]
        scr = refs[2 * G + 1:]
        o = [_gather_heads(o_refs[g], scr[g], d, tm) for g, (_, d) in enumerate(ATTN_GROUPS)]
        l = [_gather_tile(l_refs[g], scr[G + g].at[0], d, tm) for g, (_, d) in enumerate(ATTN_GROUPS)]
        for hh in range(HEADS_PER_GROUP):
            al = _softmax3([_lane_pick(l[g], hh) for g in range(G)])
            for g in range(G):
                out_ref[:, g * GROUP_W + hh * HEAD:g * GROUP_W + (hh + 1) * HEAD] = (o[g][hh] * al[g]).astype(BF16)

    specs = [_group_spec(d, tm) for _, d in ATTN_GROUPS]
    return pl.pallas_call(
        body, name="attn_merge", grid=(S // tm,),
        in_specs=specs + [_tile_spec(d, tm) for _, d in ATTN_GROUPS],
        out_specs=pl.BlockSpec((tm, G * GROUP_W), lambda i: (i, 0)),
        out_shape=jax.ShapeDtypeStruct((S, G * GROUP_W), BF16),
        scratch_shapes=[pltpu.VMEM((HEADS_PER_GROUP, tm, HEAD), F32)] * (2 * G),
        compiler_params=_params(("parallel",)))(*os_, *lses)


def _attn_merge_bwd(os_, lses, doa, tm=512):
    G = len(os_)
    S = doa.shape[0]

    def body(*refs):
        o_refs, l_refs, doa_ref = refs[:G], refs[G:2 * G], refs[2 * G]
        do_refs, dd_refs = refs[2 * G + 1:3 * G + 1], refs[3 * G + 1:4 * G + 1]
        scr = refs[4 * G + 1:]
        o = [_gather_heads(o_refs[g], scr[g], d, tm) for g, (_, d) in enumerate(ATTN_GROUPS)]
        l = [_gather_tile(l_refs[g], scr[G + g].at[0], d, tm) for g, (_, d) in enumerate(ATTN_GROUPS)]
        do = [[None] * HEADS_PER_GROUP for _ in range(G)]
        dd = [[None] * HEADS_PER_GROUP for _ in range(G)]
        for hh in range(HEADS_PER_GROUP):
            al = _softmax3([_lane_pick(l[g], hh) for g in range(G)])
            mix = None
            for g in range(G):
                dg = doa_ref[:, g * GROUP_W + hh * HEAD:g * GROUP_W + (hh + 1) * HEAD]
                do[g][hh] = dg * al[g]
                t = al[g] * jnp.sum(dg * o[g][hh], axis=-1, keepdims=True)
                mix = t if mix is None else mix + t
            for g in range(G):
                dd[g][hh] = -al[g] * mix
        for g, (_, d) in enumerate(ATTN_GROUPS):
            _scatter_heads(do[g], scr[2 * G + g], do_refs[g], d, tm)
            _scatter_tile(_lane_place(dd[g]), scr[3 * G + g].at[0], dd_refs[g], d, tm)

    specs = [_group_spec(d, tm) for _, d in ATTN_GROUPS]
    tiles = [_tile_spec(d, tm) for _, d in ATTN_GROUPS]
    do_shapes = [jax.ShapeDtypeStruct((d, S // d, GROUP_W), BF16) for _, d in ATTN_GROUPS]
    dd_shapes = [jax.ShapeDtypeStruct((d, S // d, HEAD), F32) for _, d in ATTN_GROUPS]
    return pl.pallas_call(
        body, name="attn_merge_bwd", grid=(S // tm,),
        in_specs=specs + tiles + [pl.BlockSpec((tm, G * GROUP_W), lambda i: (i, 0))],
        out_specs=specs + tiles,
        out_shape=do_shapes + dd_shapes,
        scratch_shapes=[pltpu.VMEM((HEADS_PER_GROUP, tm, HEAD), F32)] * (4 * G),
        compiler_params=_params(("parallel",)))(*os_, *lses, doa)


def _rope_tables(S):
    inv_freq = 1.0 / (ROPE_THETA ** (jnp.arange(0, HEAD, 2, dtype=F32) / HEAD))
    ang = jnp.arange(S, dtype=F32)[:, None] * inv_freq[None, :]
    cos, sin = jnp.cos(ang), jnp.sin(ang)
    return jnp.concatenate([cos, cos], axis=-1), jnp.concatenate([-sin, sin], axis=-1)


def _local_step(x, target, norm_mix, norm_ffn, lb_logits, out_gain, final_norm, comm):
    S = x.shape[0]
    nm0, nm1 = norm_mix[0:1], norm_mix[1:2]
    nf0, nf1 = norm_ffn[0:1], norm_ffn[1:2]
    w = comm.first_weights()

    proj, u0, got = _norm_mm(x, nm0, w["hin"], "hgrn_in", rider=comm.gather_rider(LATE_WEIGHTS_A))
    w.update(comm.gathered(LATE_WEIGHTS_A, got))
    (o, og, states), got = _hgrn_fwd(proj, lb_logits, out_gain, rider=comm.gather_rider(LATE_WEIGHTS_B))
    w.update(comm.gathered(LATE_WEIGHTS_B, got))
    fin_tn = w["fin0"].shape[2]
    h1 = _mm_res(x, og, w["hout"], "hgrn_out")
    z0, u1, _ = _norm_mm(h1, nf0, w["fin0"], "ffn0_in", out_dtype=BF16)
    h2, act0 = _swiglu_mm_res(h1, z0, w["fdn0"], "ffn0_down")
    cos, sin = _rope_tables(S)
    G = len(ATTN_GROUPS)
    w_groups = w["qkv"].transpose(1, 0, 2).reshape(D_MODEL, 3, G, GROUP_W)
    a_g, u2 = zip(*[_qkv_dilated(h2, nm1, w_groups[:, :, gi, :].reshape(D_MODEL, 3 * GROUP_W), cos, sin, d)
                    for gi, (_, d) in enumerate(ATTN_GROUPS)])
    o_g, lse_g = zip(*[_attn_fwd(a) for a in a_g])
    oa = _attn_merge(o_g, lse_g)
    h3 = _mm_res(h2, oa, w["aout"], "attn_out")
    z1, u3, _ = _norm_mm(h3, nf1, w["fin1"], "ffn1_in", out_dtype=BF16)
    h4, act1 = _swiglu_mm_res(h3, z1, w["fdn1"], "ffn1_down")
    dh4, loss, d_final = _loss_head(h4, final_norm, target)

    grads, small = {}, {"final_norm": d_final}

    def ffn_bwd(dh, h_in, u_in, z, act, gain, w_in, w_dn, tag, ride=None):
        dz = _mm_nt_swiglu_bwd(dh, w_dn, z, tag + "_down_dx")
        g_dn = _mm_tn(act, dh, 1, D_MODEL, D_MODEL, tag + "_down_dw")[0]
        g_in = _mm_tn(u_in, dz, N_CHIPS, fin_tn, fin_tn, tag + "_in_dw")
        rider = None if ride is None else ride(g_in, g_dn)
        dh_in, dgain, got = _mm_nt_normbwd(dz, w_in, h_in, gain, dh, tag + "_in_dx", rider=rider)
        return dh_in, dgain, g_in, g_dn, got

    dh3, d_nf1, grads["fin1"], grads["fdn1"], _ = ffn_bwd(dh4, h3, u3, z1, act1, nf1, w["fin1"], w["fdn1"], "ffn1")
    doa = _mm_nt(dh3, w["aout"][None], "attn_out_dx")
    grads["aout"] = _mm_tn(oa, dh3, 1, D_MODEL, D_MODEL, "attn_out_dw")[0]
    merged = _attn_merge_bwd(o_g, lse_g, doa)
    G = len(ATTN_GROUPS)
    das = [_attn_bwd(a_g[gi], merged[gi], lse_g[gi], merged[G + gi]) for gi in range(G)]
    dqkv = None
    for gi in range(G):
        dqkv = _undilate_group(das[gi], dqkv, cos, sin, gi)
    n_qkv = w["qkv"].shape[2]
    grads["qkv"] = _mm_tn(u2[0], dqkv, N_CHIPS, n_qkv, n_qkv, "attn_qkv_dw")
    dh2, d_nm1, _ = _mm_nt_normbwd(dqkv, w["qkv"], h2, nm1, dh3, "attn_qkv_dx")

    def ride_early(g_in, g_dn):
        return comm.pair_rider({**grads, "fin0": g_in, "fdn0": g_dn}, "early")

    dh1, d_nf0, _, _, got = ffn_bwd(dh2, h1, u1, z0, act0, nf0, w["fin0"], w["fdn0"], "ffn0", ride=ride_early)
    comm.paired("early", got)
    dog = _mm_nt(dh1, w["hout"][None], "hgrn_out_dx")
    (dproj, dlb, dgn), got = _hgrn_bwd(proj, lb_logits, out_gain, o, states, dog, rider=comm.exchange_rider("early"))
    comm.exchanged("early", got)
    late = {"hout": _mm_tn(og, dh1, 1, D_MODEL, D_MODEL, "hgrn_out_dw")[0],
            "hin": _mm_tn(u0, dproj, N_CHIPS, D_MODEL, D_MODEL, "hgrn_in_dw")}
    comm.pair_now(late, "late")
    dx, d_nm0, got = _mm_nt_normbwd(dproj, w["hin"], x, nm0, dh1, "hgrn_in_dx", rider=comm.exchange_rider("late"))
    comm.exchanged("late", got)

    small["norm_mix"] = jnp.concatenate([d_nm0, d_nm1], axis=0)
    small["norm_ffn"] = jnp.concatenate([d_nf0, d_nf1], axis=0)
    small["lb"] = dlb.reshape(1, HGRN_HEADS * HEAD)
    small["out_norm"] = dgn.reshape(HGRN_HEADS, HEAD)
    return loss, dx, small


def _place():
    x, y, c = lax.axis_index("x"), lax.axis_index("y"), lax.axis_index("c")
    others = [(1 - x, y), (x, 1 - y), (1 - x, 1 - y)]
    return x, y, c, others


ANY = pl.BlockSpec(memory_space=pl.ANY)


class _GatherRider:
    def __init__(self, shards):
        self.operands = list(shards)
        n = self.n = len(shards)
        self.out_shape = [jax.ShapeDtypeStruct((N_CHIPS,) + s.shape, s.dtype) for s in shards]
        self.scratch = [pltpu.SemaphoreType.DMA((3 * n,)), pltpu.SemaphoreType.DMA((3 * n,)),
                        pltpu.SemaphoreType.DMA((3 * n,)), pltpu.SemaphoreType.DMA((3 * n,)),
                        pltpu.SemaphoreType.DMA((n,)), pltpu.SemaphoreType.DMA((n,))]

    def _copies(self, ins, outs, sems):
        ici_send, ici_recv, _, _, own_send, own_recv = sems
        x, y, c, others = _place()
        me = 2 * x + y
        own = [pltpu.make_async_remote_copy(
            src_ref=ins[a], dst_ref=outs[a].at[me], send_sem=own_send.at[a], recv_sem=own_recv.at[a],
            device_id=(x, y, 1 - c), device_id_type=MESH) for a in range(self.n)]
        sends = [pltpu.make_async_remote_copy(
            src_ref=ins[a].at[c], dst_ref=outs[a].at[me, c], send_sem=ici_send.at[a * 3 + k], recv_sem=ici_recv.at[a * 3 + k],
            device_id=(ox, oy, c), device_id_type=MESH) for a in range(self.n) for k, (ox, oy) in enumerate(others)]
        return own, sends

    def start(self, ins, outs, sems):
        own, sends = self._copies(ins, outs, sems)
        for cp in own + sends:
            cp.start()

    def finish(self, ins, outs, sems):
        ici_send, ici_recv, d2d_send, d2d_recv, _, _ = sems
        x, y, c, others = _place()
        sibling = (x, y, 1 - c)
        own, sends = self._copies(ins, outs, sems)
        passes = []
        for a in range(self.n):
            for k, (ox, oy) in enumerate(others):
                s = a * 3 + k
                got = outs[a].at[2 * ox + oy, c]
                pltpu.make_async_remote_copy(
                    src_ref=got, dst_ref=got, send_sem=ici_send.at[s], recv_sem=ici_recv.at[s],
                    device_id=(ox, oy, c), device_id_type=MESH).wait_recv()
                fwd = pltpu.make_async_remote_copy(
                    src_ref=got, dst_ref=got, send_sem=d2d_send.at[s], recv_sem=d2d_recv.at[s],
                    device_id=sibling, device_id_type=MESH)
                fwd.start()
                passes.append(fwd)
        for a in range(self.n):
            for k, (ox, oy) in enumerate(others):
                s = a * 3 + k
                theirs = outs[a].at[2 * ox + oy, 1 - c]
                pltpu.make_async_remote_copy(
                    src_ref=theirs, dst_ref=theirs, send_sem=d2d_send.at[s], recv_sem=d2d_recv.at[s],
                    device_id=sibling, device_id_type=MESH).wait_recv()
        for cp in own:
            cp.wait()
        for cp in sends + passes:
            cp.wait_send()


class _PairRider:
    def __init__(self, grads):
        self.operands = list(grads)
        n = self.n = len(grads)
        self.out_shape = [jax.ShapeDtypeStruct((N_CHIPS,) + g.shape[2:], F32) for g in grads]
        self.scratch = [pltpu.SemaphoreType.DMA((N_CHIPS * n,)), pltpu.SemaphoreType.DMA((N_CHIPS * n,))]

    def _copies(self, ins, outs, sems):
        send_sem, recv_sem = sems
        x, y, c, _ = _place()
        return [pltpu.make_async_remote_copy(
            src_ref=ins[a].at[j, 1 - c], dst_ref=outs[a].at[j], send_sem=send_sem.at[a * N_CHIPS + j],
            recv_sem=recv_sem.at[a * N_CHIPS + j], device_id=(x, y, 1 - c), device_id_type=MESH)
            for a in range(self.n) for j in range(N_CHIPS)]

    def start(self, ins, outs, sems):
        for cp in self._copies(ins, outs, sems):
            cp.start()

    def finish(self, ins, outs, sems):
        for cp in self._copies(ins, outs, sems):
            cp.wait()


class _ExchangeRider:
    def __init__(self, parts):
        self.operands = list(parts)
        n = self.n = len(parts)
        self.out_shape = [jax.ShapeDtypeStruct(p.shape, p.dtype) for p in parts]
        self.scratch = [pltpu.SemaphoreType.DMA((3 * n,)), pltpu.SemaphoreType.DMA((3 * n,))]

    def _copies(self, ins, outs, sems):
        send_sem, recv_sem = sems
        x, y, c, others = _place()
        me = 2 * x + y
        return [pltpu.make_async_remote_copy(
            src_ref=ins[a].at[2 * ox + oy], dst_ref=outs[a].at[me], send_sem=send_sem.at[a * 3 + k],
            recv_sem=recv_sem.at[a * 3 + k], device_id=(ox, oy, c), device_id_type=MESH)
            for a in range(self.n) for k, (ox, oy) in enumerate(others)]

    def start(self, ins, outs, sems):
        for cp in self._copies(ins, outs, sems):
            cp.start()

    def finish(self, ins, outs, sems):
        send_sem, recv_sem = sems
        x, y, c, others = _place()
        for a in range(self.n):
            for k, (ox, oy) in enumerate(others):
                s = a * 3 + k
                got = outs[a].at[2 * ox + oy]
                pltpu.make_async_remote_copy(
                    src_ref=got, dst_ref=got, send_sem=send_sem.at[s], recv_sem=recv_sem.at[s],
                    device_id=(ox, oy, c), device_id_type=MESH).wait_recv()
        for cp in self._copies(ins, outs, sems):
            cp.wait_send()


def _run_rider(rider, name):
    n = rider.n

    def body(*refs):
        ins, outs, sems = refs[:n], refs[n:2 * n], refs[2 * n:]
        rider.start(ins, outs, sems)
        rider.finish(ins, outs, sems)

    return pl.pallas_call(
        body, name=name, in_specs=[ANY] * n, out_specs=[ANY] * n,
        out_shape=rider.out_shape, scratch_shapes=rider.scratch)(*rider.operands)


def _ride(rider, body, n_in, n_out, first, last):
    if rider is None:
        return body
    n = rider.n

    def wrapped(*refs):
        host_in, r_in = refs[:n_in], refs[n_in:n_in + n]
        host_out = refs[n_in + n:n_in + n + n_out]
        r_out = refs[n_in + n + n_out:n_in + 2 * n + n_out]
        rest = refs[n_in + 2 * n + n_out:]
        host_scr, sems = rest[:len(rest) - len(rider.scratch)], rest[len(rest) - len(rider.scratch):]

        @pl.when(first())
        def _():
            rider.start(r_in, r_out, sems)

        body(*host_in, *host_out, *host_scr)

        @pl.when(last())
        def _():
            rider.finish(r_in, r_out, sems)

    return wrapped


def _rider_args(rider):
    if rider is None:
        return [], [], [], [], []
    return rider.operands, [ANY] * rider.n, [ANY] * rider.n, rider.out_shape, rider.scratch


def _pair_sum(g, got, c_idx):
    _, _, r, cw = g.shape
    tr = _row_tile(r, cw)

    def body(c_ref, g_ref, got_ref, pb_ref):
        pb_ref[...] = (g_ref[...] + got_ref[...]).astype(BF16)

    blk = pl.BlockSpec((None, tr, cw), lambda j, i, c_ref: (j, i, 0))
    return pl.pallas_call(
        body, name="grad_pair_sum",
        grid_spec=pltpu.PrefetchScalarGridSpec(
            num_scalar_prefetch=1, grid=(N_CHIPS, r // tr),
            in_specs=[pl.BlockSpec((None, None, tr, cw), lambda j, i, c_ref: (j, c_ref[0], i, 0)), blk],
            out_specs=blk),
        out_shape=jax.ShapeDtypeStruct((N_CHIPS, r, cw), BF16),
        compiler_params=_params(("parallel", "parallel")))(c_idx, g, got)


def _chip_sum(g, sib, got, place):
    _, _, r, cw = g.shape
    tr = _row_tile(r, cw)

    def body(place_ref, g_ref, sib_ref, got_ref, t_ref):
        me = place_ref[0]
        own = g_ref[...] + sib_ref[...]
        acc = None
        for s in range(N_CHIPS):
            term = jnp.where(me == s, own, got_ref[s].astype(F32))
            acc = term if acc is None else acc + term
        t_ref[...] = acc

    return pl.pallas_call(
        body, name="grad_chip_sum",
        grid_spec=pltpu.PrefetchScalarGridSpec(
            num_scalar_prefetch=1, grid=(r // tr,),
            in_specs=[pl.BlockSpec((None, None, tr, cw), lambda i, pr: (pr[0], pr[1], i, 0)),
                      pl.BlockSpec((None, tr, cw), lambda i, pr: (pr[0], i, 0)),
                      pl.BlockSpec((N_CHIPS, tr, cw), lambda i, pr: (0, i, 0))],
            out_specs=pl.BlockSpec((tr, cw), lambda i, pr: (i, 0))),
        out_shape=jax.ShapeDtypeStruct((r, cw), F32),
        compiler_params=_params(("parallel",)))(place, g, sib, got)


def _pair_share(halves):
    n = len(halves)

    def body(*refs):
        ins, outs = refs[:n], refs[n:2 * n]
        send_sem, recv_sem = refs[2 * n:]
        x, y, c, _ = _place()
        cps = [pltpu.make_async_remote_copy(
            src_ref=ins[a], dst_ref=outs[a], send_sem=send_sem.at[a], recv_sem=recv_sem.at[a],
            device_id=(x, y, 1 - c), device_id_type=MESH) for a in range(n)]
        for cp in cps:
            cp.start()
        for cp in cps:
            cp.wait()

    return pl.pallas_call(
        body, name="grad_pair_share",
        in_specs=[ANY] * n, out_specs=[ANY] * n,
        out_shape=[jax.ShapeDtypeStruct(h.shape, F32) for h in halves],
        scratch_shapes=[pltpu.SemaphoreType.DMA((n,)), pltpu.SemaphoreType.DMA((n,))],
        )(*halves)


def _small_allreduce(pack):
    m_per, ncol = pack.shape
    n_dev = 8

    def body(x_ref, sum_ref, all_ref, send_sems, recv_sems, local_sem):
        x, y, c, others = _place()
        me, sibling = (x, y, c), (x, y, 1 - c)

        def rows(px, py, pc):
            return all_ref.at[pl.ds((4 * px + 2 * py + pc) * m_per, m_per), :]

        def copy(k, block, to, src=None):
            return pltpu.make_async_remote_copy(
                src_ref=rows(*block) if src is None else src, dst_ref=rows(*block),
                send_sem=send_sems.at[k], recv_sem=recv_sems.at[k], device_id=to, device_id_type=MESH)

        mine = pltpu.make_async_copy(x_ref, rows(*me), local_sem)
        mine.start()
        first = [copy(0, me, sibling, src=x_ref)]
        first += [copy(1 + j, me, (*chip, c), src=x_ref) for j, chip in enumerate(others)]
        for cp in first:
            cp.start()
        passed = [copy(4 + j, (*chip, c), sibling) for j, chip in enumerate(others)]
        for j, chip in enumerate(others):
            copy(1 + j, (*chip, c), me).wait_recv()
            passed[j].start()
        copy(0, sibling, me).wait_recv()
        for j, chip in enumerate(others):
            copy(4 + j, (*chip, 1 - c), me).wait_recv()
        for cp in first + passed:
            cp.wait_send()
        mine.wait()
        acc = all_ref[0:m_per, :]
        for dvc in range(1, n_dev):
            acc = acc + all_ref[dvc * m_per:(dvc + 1) * m_per, :]
        sum_ref[...] = acc

    return pl.pallas_call(
        body, name="small_allreduce",
        in_specs=[pl.BlockSpec(memory_space=pltpu.VMEM)],
        out_specs=pl.BlockSpec(memory_space=pltpu.VMEM),
        out_shape=jax.ShapeDtypeStruct((m_per, ncol), F32),
        scratch_shapes=[pltpu.VMEM((n_dev * m_per, ncol), F32),
                        pltpu.SemaphoreType.DMA((7,)), pltpu.SemaphoreType.DMA((7,)), pltpu.SemaphoreType.DMA],
        )(pack)


def _adam_math(w, g, m, v):
    m = ADAM_B1 * m + (1.0 - ADAM_B1) * g
    v = ADAM_B2 * v + (1.0 - ADAM_B2) * (g * g)
    m_hat = m / (1.0 - ADAM_B1 ** ADAM_STEP)
    v_hat = v / (1.0 - ADAM_B2 ** ADAM_STEP)
    delta = -ADAM_LR * (m_hat / (jnp.sqrt(v_hat) + ADAM_EPS) + ADAM_WD * w)
    return delta, m, v


def _adamw(halves, c_idx, w, m, v, name):
    L = len(halves)
    r, C = halves[0][0].shape
    tr = _row_tile(r, C, 512 * 1024)
    nt = r // tr

    def body(c_ref, *refs):
        g_refs, (w_ref, m_ref, v_ref), (g_ref, d_ref, nm_ref, nv_ref) = refs[:2 * L], refs[2 * L:2 * L + 3], refs[2 * L + 3:]
        own = pl.program_id(1) == c_ref[0]
        g = None
        for l in range(L):
            cand = jnp.where(own, g_refs[2 * l][...], g_refs[2 * l + 1][...])
            g = cand if g is None else jnp.where(pl.program_id(0) == l, cand, g)
        g_ref[...] = g
        d_ref[...], nm_ref[...], nv_ref[...] = _adam_math(w_ref[...], g, m_ref[...], v_ref[...])

    def half(l, mine):
        def index(ll, h, i, c_ref):
            read = (h == c_ref[0]) if mine else (h != c_ref[0])
            return jnp.where(jnp.logical_and(ll == l, read), i, 0), 0
        return pl.BlockSpec((tr, C), index)

    full = pl.BlockSpec((None, tr, C), lambda ll, h, i, c_ref: (ll, h * nt + i, 0))
    shp = jax.ShapeDtypeStruct((L, 2 * r, C), F32)
    g_specs = [half(l, mine) for l in range(L) for mine in (True, False)]
    return pl.pallas_call(
        body, name=name,
        grid_spec=pltpu.PrefetchScalarGridSpec(
            num_scalar_prefetch=1, grid=(L, 2, nt),
            in_specs=g_specs + [full] * 3, out_specs=[full] * 4),
        out_shape=[shp] * 4,
        compiler_params=_params(("arbitrary", "arbitrary", "arbitrary")))(
            c_idx, *[a for pair in halves for a in pair], w, m, v)


def _small_update(gsum, logits_pack, w, m, v):
    def body(gs_ref, lg_ref, w_ref, m_ref, v_ref, g_ref, d_ref, nm_ref, nv_ref):
        g_ref[...] = gs_ref[...]
        l0, l1, l2 = lg_ref[0:1, :], lg_ref[1:2, :], lg_ref[2:3, :]
        mx = jnp.maximum(jnp.maximum(l0, l1), l2)
        e0, e1, e2 = jnp.exp(l0 - mx), jnp.exp(l1 - mx), jnp.exp(l2 - mx)
        tot = e0 + e1 + e2
        p0, p1, p2 = e0 / tot, e1 / tot, e2 / tot
        dlb = gs_ref[4:5, :]
        g_ref[4:5, :] = dlb * p0 * (1.0 - p0)
        g_ref[5:6, :] = -dlb * p0 * p1
        g_ref[6:7, :] = -dlb * p0 * p2
        d_ref[...], nm_ref[...], nv_ref[...] = _adam_math(w_ref[...], g_ref[...], m_ref[...], v_ref[...])

    full = pl.BlockSpec(memory_space=pltpu.VMEM)
    shp = jax.ShapeDtypeStruct(gsum.shape, F32)
    return pl.pallas_call(
        body, name="small_update", in_specs=[full] * 5, out_specs=[full] * 4, out_shape=[shp] * 4)(
            gsum, logits_pack, w, m, v)


def _pack_small(norm_mix, norm_ffn, lb3, out_norm, final_norm, extra=None):
    ncol = norm_mix.shape[1]
    on = jnp.pad(out_norm.reshape(1, -1), ((0, 0), (0, ncol - out_norm.size)))
    rows = [norm_mix, norm_ffn, lb3, on, final_norm.reshape(1, ncol)]
    if extra is not None:
        rows.append(extra)
    used = sum(r.shape[0] for r in rows)
    rows.append(jnp.zeros((SMALL_ROWS - used, ncol), F32))
    return jnp.concatenate(rows, axis=0)


WEIGHT_NAMES = ("hin", "hout", "qkv", "aout", "fin0", "fin1", "fdn0", "fdn1")
FIRST_WEIGHTS = ("hin",)
LATE_WEIGHTS_A = ("hout", "fin0", "fdn0")
LATE_WEIGHTS_B = ("qkv", "aout", "fin1", "fdn1")


def _split_weights(hgrn_w_in, hgrn_w_out, attn_w_qkv, attn_w_out, ffn_w_in, ffn_w_down):
    return {"hin": hgrn_w_in[0], "hout": hgrn_w_out[0], "qkv": attn_w_qkv[0], "aout": attn_w_out[0],
            "fin0": ffn_w_in[0], "fin1": ffn_w_in[1], "fdn0": ffn_w_down[0], "fdn1": ffn_w_down[1]}


def _halves(v):
    r, c = v.shape
    return v.reshape(2, r // 2, c)


def _full_weights(gathered):
    out = {}
    for k, g in gathered.items():
        _, _, r, c = g.shape
        if k in ("hin", "qkv", "fin0", "fin1"):
            out[k] = g.reshape(N_CHIPS, 2 * r, c)
        else:
            out[k] = g.reshape(N_CHIPS * 2 * r, c)
    return out


class _StepComm:
    def __init__(self, shards, c_idx, me_idx):
        self.shards, self.c_idx, self.me_idx = shards, c_idx, me_idx
        self.halves = {}
        self._stage = {}

    def gather_rider(self, names):
        return _GatherRider([_halves(self.shards[k].astype(BF16)) for k in names])

    def gathered(self, names, got):
        return _full_weights(dict(zip(names, got)))

    def first_weights(self):
        return self.gathered(FIRST_WEIGHTS, _run_rider(self.gather_rider(FIRST_WEIGHTS), "gather_first"))

    def pair_rider(self, grads, tag):
        names = list(grads)
        g4 = []
        for k in names:
            r, c = self.shards[k].shape
            g4.append(grads[k].reshape(N_CHIPS, 2, r // 2, c))
        self._stage[tag] = (names, g4)
        return _PairRider(g4)

    def pair_now(self, grads, tag):
        self.paired(tag, _run_rider(self.pair_rider(grads, tag), "grad_pair_exchange_" + tag))

    def paired(self, tag, got):
        names, g4 = self._stage[tag]
        self._stage[tag] = (names, [(g, s, _pair_sum(g, s, self.c_idx)) for g, s in zip(g4, got)])

    def exchange_rider(self, tag):
        return _ExchangeRider([s[2] for s in self._stage[tag][1]])

    def exchanged(self, tag, got):
        names, sums = self._stage.pop(tag)
        place = jnp.concatenate([self.me_idx, self.c_idx])
        for k, (g, sib, _), recv in zip(names, sums, got):
            self.halves[k] = _chip_sum(g, sib, recv, place)

    def shared_halves(self):
        mine = [self.halves[k] for k in WEIGHT_NAMES]
        return dict(zip(WEIGHT_NAMES, zip(mine, _pair_share(mine))))


def kernel(x, norm_mix, norm_ffn, hgrn_w_in, hgrn_lb_logits, hgrn_out_norm, hgrn_w_out, attn_w_qkv, attn_w_out, ffn_w_in, ffn_w_down, final_norm, loss_target, m_norm_mix, m_norm_ffn, m_hgrn_w_in, m_hgrn_lb_logits, m_hgrn_out_norm, m_hgrn_w_out, m_attn_w_qkv, m_attn_w_out, m_ffn_w_in, m_ffn_w_down, m_final_norm, v_norm_mix, v_norm_ffn, v_hgrn_w_in, v_hgrn_lb_logits, v_hgrn_out_norm, v_hgrn_w_out, v_attn_w_qkv, v_attn_w_out, v_ffn_w_in, v_ffn_w_down, v_final_norm):
    S = x.shape[1]
    xi, yi, ci = lax.axis_index("x"), lax.axis_index("y"), lax.axis_index("c")
    c_idx = jnp.reshape(ci, (1,)).astype(jnp.int32)
    me_idx = jnp.reshape(2 * xi + yi, (1,)).astype(jnp.int32)

    w_own = _split_weights(hgrn_w_in, hgrn_w_out, attn_w_qkv, attn_w_out, ffn_w_in, ffn_w_down)

    comm = _StepComm(w_own, c_idx, me_idx)
    loss, dx, small = _local_step(
        x.reshape(S, D_MODEL), loss_target.reshape(S, D_MODEL), norm_mix, norm_ffn, hgrn_lb_logits,
        hgrn_out_norm, final_norm.reshape(1, D_MODEL), comm)

    halves = comm.shared_halves()
    updated = {}
    for tensor, layers, (wt, mt, vt) in (
            ("hgrn_w_in", ("hin",), (hgrn_w_in, m_hgrn_w_in, v_hgrn_w_in)),
            ("hgrn_w_out", ("hout",), (hgrn_w_out, m_hgrn_w_out, v_hgrn_w_out)),
            ("attn_w_qkv", ("qkv",), (attn_w_qkv, m_attn_w_qkv, v_attn_w_qkv)),
            ("attn_w_out", ("aout",), (attn_w_out, m_attn_w_out, v_attn_w_out)),
            ("ffn_w_in", ("fin0", "fin1"), (ffn_w_in, m_ffn_w_in, v_ffn_w_in)),
            ("ffn_w_down", ("fdn0", "fdn1"), (ffn_w_down, m_ffn_w_down, v_ffn_w_down))):
        updated[tensor] = _adamw([halves[k] for k in layers], c_idx, wt, mt, vt, "adamw_" + tensor)

    loss_row = jnp.pad(loss, ((0, 0), (0, D_MODEL - loss.shape[1])))
    lb3 = jnp.concatenate([small["lb"], jnp.zeros((2, D_MODEL), F32)], axis=0)
    on_grad = jnp.sum(small["out_norm"], axis=0, keepdims=True)
    pack = _pack_small(small["norm_mix"], small["norm_ffn"], lb3, on_grad, small["final_norm"], loss_row)
    gsum = _small_allreduce(pack)
    w_s = _pack_small(norm_mix, norm_ffn, hgrn_lb_logits, hgrn_out_norm, final_norm)
    m_s = _pack_small(m_norm_mix, m_norm_ffn, m_hgrn_lb_logits, m_hgrn_out_norm, m_final_norm)
    v_s = _pack_small(v_norm_mix, v_norm_ffn, v_hgrn_lb_logits, v_hgrn_out_norm, v_final_norm)
    lg_pack = jnp.pad(hgrn_lb_logits, ((0, 8 - hgrn_lb_logits.shape[0]), (0, 0)))
    sg, sd, sm, sv = _small_update(gsum, lg_pack, w_s, m_s, v_s)

    def unpack(p):
        return (p[0:2], p[2:4], p[4:7], p[7:8, :HEAD], p[8])

    def assemble(p, which):
        nmx, nff, lbl, onm, fnm = unpack(p)
        hin, hout, qkv, aout, fin, fdn = [updated[t][which] for t in
                                          ("hgrn_w_in", "hgrn_w_out", "attn_w_qkv", "attn_w_out", "ffn_w_in", "ffn_w_down")]
        return (nmx, nff, hin, lbl, onm, hout, qkv, aout, fin, fdn, fnm)

    total_loss = gsum[9, 0]
    return (total_loss, dx.reshape(1, S, D_MODEL), *assemble(sg, 0), *assemble(sd, 1), *assemble(sm, 2), *assemble(sv, 3))
```

```python
import functools

import jax
import jax.numpy as jnp
from jax import lax
from jax.experimental import pallas as pl
from jax.experimental.pallas import tpu as pltpu

F32 = jnp.float32
BF16 = jnp.bfloat16
MESH = pl.DeviceIdType.MESH

D_MODEL = 1024
HEAD = 128
HGRN_HEADS = 8
HGRN_CHUNK = 64
HGRN_HEADS_PER_STEP = 2
ATTN_GROUPS = ((128, 1), (512, 4), (2048, 16))
ATTN_SPAN = 128
HEADS_PER_GROUP = 4
GROUP_W = HEADS_PER_GROUP * HEAD
D_FF = 2816
NORM_EPS = 1e-6
ROPE_THETA = 10000.0
NEG = -1e30

ADAM_LR, ADAM_B1, ADAM_B2, ADAM_EPS, ADAM_WD, ADAM_STEP = 0.001, 0.9, 0.999, 1e-08, 0.01, 10

N_CHIPS = 4
VMEM_LIMIT = 56 * 1024 * 1024
SMALL_ROWS = 16


def _params(sem=None):
    return pltpu.CompilerParams(dimension_semantics=sem, vmem_limit_bytes=VMEM_LIMIT)


def _row_tile(rows, cols, budget_bytes=3 * 512 * 1024):
    best = 8
    for t in range(8, rows + 1, 8):
        if rows % t == 0 and t * cols * 4 <= budget_bytes:
            best = t
    assert rows % best == 0
    return best


def _grid_corner(i, j):
    return jnp.logical_and(pl.program_id(0) == i, pl.program_id(1) == j)


def _sigmoid(v):
    return 0.5 * jnp.tanh(0.5 * v) + 0.5


def _dot(a, b):
    return jnp.dot(a, b, preferred_element_type=F32)


def _dot_nt(a, b):
    return lax.dot_general(a, b, (((1,), (1,)), ((), ())), preferred_element_type=F32)


def _dot_tn(a, b):
    return lax.dot_general(a, b, (((0,), (0,)), ((), ())), preferred_element_type=F32)


def _dot_exact(ones, b):
    ones = ones.astype(BF16)
    hi = b.astype(BF16)
    rest = b - hi.astype(F32)
    mid = rest.astype(BF16)
    low = (rest - mid.astype(F32)).astype(BF16)
    return _dot(ones, hi) + _dot(ones, mid) + _dot(ones, low)


def _rstd(v):
    return lax.rsqrt(jnp.mean(v * v, axis=-1, keepdims=True) + NORM_EPS)


def _norm_mm(h, gain, w3, name, out_dtype=F32, tm=1024, rider=None):
    S, K = h.shape
    J, _, n = w3.shape
    gi = S // tm

    def body(h_ref, g_ref, w_ref, y_ref, u_ref):
        @pl.when(pl.program_id(1) == 0)
        def _():
            v = h_ref[...]
            u_ref[...] = (v * _rstd(v) * g_ref[...]).astype(BF16)

        y_ref[...] = _dot(u_ref[...], w_ref[pl.program_id(1)]).astype(y_ref.dtype)

    r_ops, r_in, r_out, r_shape, r_scr = _rider_args(rider)
    res = pl.pallas_call(
        _ride(rider, body, 3, 2, functools.partial(_grid_corner, 0, 0), functools.partial(_grid_corner, gi - 1, J - 1)),
        name=name, grid=(gi, J),
        in_specs=[pl.BlockSpec((tm, K), lambda i, j: (i, 0)),
                  pl.BlockSpec((1, K), lambda i, j: (0, 0)),
                  pl.BlockSpec((J, K, n), lambda i, j: (0, 0, 0))] + r_in,
        out_specs=[pl.BlockSpec((tm, n), lambda i, j: (i, j)), pl.BlockSpec((tm, K), lambda i, j: (i, 0))] + r_out,
        out_shape=[jax.ShapeDtypeStruct((S, J * n), out_dtype), jax.ShapeDtypeStruct((S, K), BF16)] + r_shape,
        scratch_shapes=r_scr,
        compiler_params=_params(("arbitrary", "arbitrary")))(h, gain, w3, *r_ops)
    return res[0], res[1], res[2:]


def _mm_res(h, a, w2, name, tm=1024):
    S, N = h.shape
    K = a.shape[1]

    def body(h_ref, a_ref, w_ref, o_ref):
        o_ref[...] = h_ref[...] + _dot(a_ref[...], w_ref[...])

    return pl.pallas_call(
        body, name=name, grid=(S // tm,),
        in_specs=[pl.BlockSpec((tm, N), lambda i: (i, 0)),
                  pl.BlockSpec((tm, K), lambda i: (i, 0)),
                  pl.BlockSpec((K, N), lambda i: (0, 0))],
        out_specs=pl.BlockSpec((tm, N), lambda i: (i, 0)),
        out_shape=jax.ShapeDtypeStruct((S, N), F32),
        compiler_params=_params(("parallel",)))(h, a, w2)


def _swiglu(z_ref, F):
    g = z_ref[:, :F].astype(F32)
    return (g * _sigmoid(g) * z_ref[:, F:].astype(F32)).astype(BF16)


def _swiglu_mm_res(h, z, w2, name, tm=256):
    S, N = h.shape
    F = w2.shape[0]

    def body(h_ref, z_ref, w_ref, o_ref, a_ref):
        a = _swiglu(z_ref, F)
        a_ref[...] = a
        o_ref[...] = h_ref[...] + _dot(a, w_ref[...])

    return pl.pallas_call(
        body, name=name, grid=(S // tm,),
        in_specs=[pl.BlockSpec((tm, N), lambda i: (i, 0)),
                  pl.BlockSpec((tm, 2 * F), lambda i: (i, 0)),
                  pl.BlockSpec((F, N), lambda i: (0, 0))],
        out_specs=[pl.BlockSpec((tm, N), lambda i: (i, 0)), pl.BlockSpec((tm, F), lambda i: (i, 0))],
        out_shape=[jax.ShapeDtypeStruct((S, N), F32), jax.ShapeDtypeStruct((S, F), BF16)],
        compiler_params=_params(("parallel",)))(h, z, w2)


def _dy_specs(dy, J, n, tm):
    if dy.ndim == 3:
        return [pl.BlockSpec((None, tm, n), functools.partial(lambda i, j: (j, i, 0), j=j)) for j in range(J)]
    return [pl.BlockSpec((tm, n), functools.partial(lambda i, j: (i, j), j=j)) for j in range(J)]


def _acc_nt(dy_refs, w_ref):
    acc = None
    for j, r in enumerate(dy_refs):
        t = _dot_nt(r[...].astype(BF16), w_ref[j])
        acc = t if acc is None else acc + t
    return acc


def _mm_nt(dy, w3, name, out_dtype=F32, tm=1024):
    J, K, n = w3.shape
    S = dy.shape[-2]

    def body(*refs):
        dy_refs, w_ref, o_ref = refs[:J], refs[J], refs[J + 1]
        o_ref[...] = _acc_nt(dy_refs, w_ref).astype(o_ref.dtype)

    return pl.pallas_call(
        body, name=name, grid=(S // tm,),
        in_specs=_dy_specs(dy, J, n, tm) + [pl.BlockSpec((J, K, n), lambda i: (0, 0, 0))],
        out_specs=pl.BlockSpec((tm, K), lambda i: (i, 0)),
        out_shape=jax.ShapeDtypeStruct((S, K), out_dtype),
        compiler_params=_params(("parallel",)))(*([dy] * J), w3)


def _mm_nt_normbwd(dy, w3, h, gain, dh, name, tm=512, rider=None):
    J, K, n = w3.shape
    S = h.shape[0]
    steps = S // tm

    def body(*refs):
        dy_refs, w_ref, h_ref, g_ref, dh_ref, o_ref, dg_ref = refs[:J], *refs[J:]
        du = _acc_nt(dy_refs, w_ref)
        v = h_ref[...]
        r = _rstd(v)
        xh = v * r
        dyg = du * g_ref[...]
        o_ref[...] = dh_ref[...] + r * (dyg - xh * jnp.mean(dyg * xh, axis=-1, keepdims=True))

        @pl.when(pl.program_id(0) == 0)
        def _():
            dg_ref[...] = jnp.zeros_like(dg_ref)

        dg_ref[...] += jnp.sum(du * xh, axis=0, keepdims=True)

    row = pl.BlockSpec((tm, K), lambda i: (i, 0))
    vec = pl.BlockSpec((1, K), lambda i: (0, 0))
    r_ops, r_in, r_out, r_shape, r_scr = _rider_args(rider)
    res = pl.pallas_call(
        _ride(rider, body, J + 4, 2, lambda: pl.program_id(0) == 0, lambda: pl.program_id(0) == steps - 1),
        name=name, grid=(steps,),
        in_specs=_dy_specs(dy, J, n, tm) + [pl.BlockSpec((J, K, n), lambda i: (0, 0, 0)), row, vec, row] + r_in,
        out_specs=[row, vec] + r_out,
        out_shape=[jax.ShapeDtypeStruct((S, K), F32), jax.ShapeDtypeStruct((1, K), F32)] + r_shape,
        scratch_shapes=r_scr,
        compiler_params=_params(("arbitrary",)))(*([dy] * J), w3, h, gain, dh, *r_ops)
    return res[0], res[1], res[2:]


def _mm_nt_swiglu_bwd(dh, w2, z, name, tm=256):
    F, N = w2.shape
    S = dh.shape[0]

    def body(dh_ref, w_ref, z_ref, o_ref):
        da = _dot_nt(dh_ref[...].astype(BF16), w_ref[...])
        g = z_ref[:, :F].astype(F32)
        u = z_ref[:, F:].astype(F32)
        sg = _sigmoid(g)
        o_ref[:, :F] = (da * u * (sg * (1.0 + g * (1.0 - sg)))).astype(BF16)
        o_ref[:, F:] = (da * (g * sg)).astype(BF16)

    return pl.pallas_call(
        body, name=name, grid=(S // tm,),
        in_specs=[pl.BlockSpec((tm, N), lambda i: (i, 0)),
                  pl.BlockSpec((F, N), lambda i: (0, 0)),
                  pl.BlockSpec((tm, 2 * F), lambda i: (i, 0))],
        out_specs=pl.BlockSpec((tm, 2 * F), lambda i: (i, 0)),
        out_shape=jax.ShapeDtypeStruct((S, 2 * F), BF16),
        compiler_params=_params(("parallel",)))(dh, w2, z)


def _mm_tn(x, dy, J, n, tn, name):
    tpn = n // tn
    ts = 1024
    S, K = x.shape
    if dy.ndim == 3:
        dy_spec = pl.BlockSpec((None, ts, tn), lambda c, s: (c // tpn, s, c % tpn))
    else:
        dy_spec = pl.BlockSpec((ts, tn), lambda c, s: (s, c))

    def body(x_ref, dy_ref, o_ref):
        @pl.when(pl.program_id(1) == 0)
        def _():
            o_ref[...] = jnp.zeros_like(o_ref)

        o_ref[...] += _dot_tn(x_ref[...], dy_ref[...].astype(BF16))

    return pl.pallas_call(
        body, name=name, grid=(J * tpn, S // ts),
        in_specs=[pl.BlockSpec((ts, K), lambda c, s: (s, 0)), dy_spec],
        out_specs=pl.BlockSpec((None, K, tn), lambda c, s: (c // tpn, 0, c % tpn)),
        out_shape=jax.ShapeDtypeStruct((J, K, n), F32),
        compiler_params=_params(("parallel", "arbitrary")))(x, dy)


def _loss_head(h, gain, target, tm=1024):
    S, K = h.shape

    def body(h_ref, g_ref, t_ref, dh_ref, loss_ref, dg_ref):
        v = h_ref[...]
        r = _rstd(v)
        xh = v * r
        g = g_ref[...]
        dy = (xh * g - t_ref[...]) * (1.0 / K)
        dyg = dy * g
        dh_ref[...] = r * (dyg - xh * jnp.mean(dyg * xh, axis=-1, keepdims=True))

        @pl.when(pl.program_id(0) == 0)
        def _():
            loss_ref[...] = jnp.zeros_like(loss_ref)
            dg_ref[...] = jnp.zeros_like(dg_ref)

        part = jnp.sum(jnp.sum(dy * dy, axis=-1, keepdims=True), axis=0, keepdims=True) * (0.5 * K)
        lane = lax.broadcasted_iota(jnp.int32, loss_ref.shape, 1)
        loss_ref[...] += jnp.where(lane == 0, part, 0.0)
        dg_ref[...] += jnp.sum(dy * xh, axis=0, keepdims=True)

    row = pl.BlockSpec((tm, K), lambda i: (i, 0))
    vec = pl.BlockSpec((1, K), lambda i: (0, 0))
    return pl.pallas_call(
        body, name="loss_head", grid=(S // tm,),
        in_specs=[row, vec, row],
        out_specs=[row, pl.BlockSpec((1, HEAD), lambda i: (0, 0)), vec],
        out_shape=[jax.ShapeDtypeStruct((S, K), F32), jax.ShapeDtypeStruct((1, HEAD), F32),
                   jax.ShapeDtypeStruct((1, K), F32)],
        compiler_params=_params(("arbitrary",)))(h, gain, target)


def _lower_bound(lg_ref):
    l0, l1, l2 = lg_ref[0:1, :], lg_ref[1:2, :], lg_ref[2:3, :]
    mx = jnp.maximum(jnp.maximum(l0, l1), l2)
    e0, e1, e2 = jnp.exp(l0 - mx), jnp.exp(l1 - mx), jnp.exp(l2 - mx)
    return e0 / (e0 + e1 + e2)


def _chunks(v, ncb):
    C = HGRN_CHUNK
    return [v[c * C:(c + 1) * C] for c in range(ncb)]


def _rows(parts):
    return jnp.concatenate(parts, axis=0)


def _block_gates(qz, fz, lb, ncb):
    C = HGRN_CHUNK
    row = lax.broadcasted_iota(jnp.int32, (C, C), 0)
    col = lax.broadcasted_iota(jnp.int32, (C, C), 1)
    tri = (col <= row).astype(F32)
    first_half = lax.broadcasted_iota(jnp.int32, (C, HEAD), 0) < C // 2
    sig = _sigmoid(fz)
    fg = lb + (1.0 - lb) * sig
    key = 1.0 - fg
    lg = jnp.log(fg)
    lgs = _chunks(lg, ncb)
    b = _rows([_dot_exact(tri, v) for v in lgs])
    r_c = [jnp.sum(jnp.where(first_half, v, 0.0), axis=0, keepdims=True) for v in lgs]
    bl_c = [jnp.sum(v, axis=0, keepdims=True) for v in lgs]
    r = _rows([jnp.broadcast_to(v, (C, HEAD)) for v in r_c])
    e_br, e_rb = jnp.exp(b - r), jnp.exp(r - b)
    e_b = e_br * _rows([jnp.broadcast_to(jnp.exp(v), (C, HEAD)) for v in r_c])
    e_lb = e_rb * _rows([jnp.broadcast_to(jnp.exp(e - v), (C, HEAD)) for e, v in zip(bl_c, r_c)])
    sq = _sigmoid(qz)
    qy = qz * sq
    return sig, fg, key, (e_br, e_rb, e_b, e_lb), bl_c, sq, qy


def _hgrn_fwd(proj, logits, gain, tb=1024, rider=None):
    S = proj.shape[0]
    H, C = HGRN_HEADS, HGRN_CHUNK
    ncb = tb // C

    def one_head(q_ref, f_ref, i_ref, g_ref, lg_ref, gn_ref, o_ref, og_ref, st_ref, state):
        @pl.when(pl.program_id(1) == 0)
        def _():
            state[...] = jnp.zeros_like(state)

        lb = _lower_bound(lg_ref)
        causal = lax.broadcasted_iota(jnp.int32, (C, C), 1) <= lax.broadcasted_iota(jnp.int32, (C, C), 0)
        qz, fz, gz = q_ref[...], f_ref[...], g_ref[...]
        _, _, key, (e_br, e_rb, e_b, e_lb), bl_c, _, qy = _block_gates(qz, fz, lb, ncb)
        qs = _chunks((qy * e_br).astype(BF16), ncb)
        ks = _chunks((key * e_rb).astype(BF16), ncb)
        qb = _chunks((qy * e_b).astype(BF16), ncb)
        ke = _chunks((key * e_lb).astype(BF16), ncb)
        vb = _chunks(i_ref[...].astype(BF16), ncb)
        a = [jnp.where(causal, _dot_nt(qs[c], ks[c]), 0.0).astype(BF16) for c in range(ncb)]
        upd = [_dot_tn(vb[c], ke[c]) for c in range(ncb)]
        o_intra = [_dot(a[c], vb[c]) for c in range(ncb)]
        st = state[...]
        e_l = [jnp.exp(v) for v in bl_c]
        sts = []
        for c in range(ncb):
            sts.append(st)
            st = st * e_l[c] + upd[c]
        state[...] = st
        for c in range(ncb):
            st_ref[c] = sts[c]
        o = _rows([_dot_nt(qb[c], sts[c].astype(BF16)) + o_intra[c] for c in range(ncb)])
        o_ref[...] = o
        og_ref[...] = ((o * _rstd(o) * gn_ref[...]) * (gz * _sigmoid(gz))).astype(BF16)

    def body(q_ref, f_ref, i_ref, g_ref, lg_ref, gn_ref, o_ref, og_ref, st_ref, state):
        for hs in range(HP):
            cols = slice(hs * HEAD, (hs + 1) * HEAD)
            one_head(q_ref.at[:, cols], f_ref.at[:, cols], i_ref.at[:, cols], g_ref.at[:, cols], lg_ref.at[:, cols],
                     gn_ref, o_ref.at[:, cols], og_ref.at[:, cols], st_ref.at[hs], state.at[hs])

    HP, wide = HGRN_HEADS_PER_STEP, HGRN_HEADS_PER_STEP * HEAD
    hg = H // HP

    def part(p):
        return pl.BlockSpec((tb, wide), functools.partial(lambda h, i, p: (i, p * hg + h), p=p))

    nb = S // tb
    r_ops, r_in, r_out, r_shape, r_scr = _rider_args(rider)
    res = pl.pallas_call(
        _ride(rider, body, 6, 3, functools.partial(_grid_corner, 0, 0), functools.partial(_grid_corner, hg - 1, nb - 1)),
        name="hgrn_fwd", grid=(hg, nb),
        in_specs=[part(0), part(1), part(2), part(3),
                  pl.BlockSpec((3, wide), lambda h, i: (0, h)),
                  pl.BlockSpec((1, HEAD), lambda h, i: (0, 0))] + r_in,
        out_specs=[pl.BlockSpec((tb, wide), lambda h, i: (i, h)),
                   pl.BlockSpec((tb, wide), lambda h, i: (i, h)),
                   pl.BlockSpec((HP, ncb, HEAD, HEAD), lambda h, i: (h, i, 0, 0))] + r_out,
        out_shape=[jax.ShapeDtypeStruct((S, H * HEAD), F32),
                   jax.ShapeDtypeStruct((S, H * HEAD), BF16),
                   jax.ShapeDtypeStruct((H, S // C, HEAD, HEAD), F32)] + r_shape,
        scratch_shapes=[pltpu.VMEM((HP, HEAD, HEAD), F32)] + r_scr,
        compiler_params=_params(("arbitrary", "arbitrary")))(proj, proj, proj, proj, logits, gain, *r_ops)
    return res[:3], res[3:]


def _hgrn_bwd(proj, logits, gain, o, states, dog, tb=1024, rider=None):
    S = proj.shape[0]
    H, C = HGRN_HEADS, HGRN_CHUNK
    ncb = tb // C
    nb = S // tb

    def one_head(q_ref, f_ref, i_ref, g_ref, lg_ref, gn_ref, o_ref, st_ref, dog_ref,
                 dp_ref, dlb_ref, dgn_ref, dstate, dst_scr):
        @pl.when(pl.program_id(1) == 0)
        def _():
            dstate[...] = jnp.zeros_like(dstate)
            dlb_ref[...] = jnp.zeros_like(dlb_ref)
            dgn_ref[...] = jnp.zeros_like(dgn_ref)

        lb = _lower_bound(lg_ref)
        oml = 1.0 - lb
        gn = gn_ref[...]
        row = lax.broadcasted_iota(jnp.int32, (C, C), 0)
        col = lax.broadcasted_iota(jnp.int32, (C, C), 1)
        causal = col <= row
        tri_up = (col >= row).astype(F32)
        qz, fz, gz = q_ref[...], f_ref[...], g_ref[...]
        sig, fg, key, (e_br, e_rb, e_b, e_lb), bl_c, sq, qy = _block_gates(qz, fz, lb, ncb)
        qs_v, ks_v = (qy * e_br).astype(BF16), (key * e_rb).astype(BF16)
        qb_v, ke_v = (qy * e_b).astype(BF16), (key * e_lb).astype(BF16)
        qs, ks, qb, ke = _chunks(qs_v, ncb), _chunks(ks_v, ncb), _chunks(qb_v, ncb), _chunks(ke_v, ncb)
        vb = _chunks(i_ref[...].astype(BF16), ncb)
        ov = o_ref[...]
        rs = _rstd(ov)
        xh = ov * rs
        sg = _sigmoid(gz)
        dog_v = dog_ref[...]
        dgz = dog_v * (xh * gn) * (sg * (1.0 + gz * (1.0 - sg)))
        don = dog_v * (gz * sg)
        dgn_ref[...] += jnp.sum(don * xh, axis=0, keepdims=True)
        dyg = don * gn
        do = rs * (dyg - xh * jnp.mean(dyg * xh, axis=-1, keepdims=True))
        dob = _chunks(do.astype(BF16), ncb)
        CH = range(ncb)
        a = [jnp.where(causal, _dot_nt(qs[c], ks[c]), 0.0).astype(BF16) for c in CH]
        da = [jnp.where(causal, _dot_nt(dob[c], vb[c]), 0.0).astype(BF16) for c in CH]
        wst = [_dot_tn(dob[c], qb[c]) for c in CH]
        dv_in = [_dot_tn(a[c], dob[c]) for c in CH]
        dqs = [_dot(da[c], ks[c]) for c in CH]
        dks = [_dot_tn(da[c], qs[c]) for c in CH]
        e_l = [jnp.exp(v) for v in bl_c]
        dst = dstate[...]
        for c in reversed(range(ncb)):
            dst_scr[c] = dst
            dst = wst[c] + dst * e_l[c]
        dstate[...] = dst
        dst1b = [dst_scr[c].astype(BF16) for c in CH]
        dqb = [_dot(dob[c], st_ref[c].astype(BF16)) for c in CH]
        dke = [_dot(vb[c], dst1b[c]) for c in CH]
        dv = [dv_in[c] + _dot_nt(ke[c], dst1b[c]) for c in CH]
        dbl_st = [jnp.sum(dst_scr[c] * st_ref[c], axis=0, keepdims=True) * e_l[c] for c in CH]
        dqs, dks, dqb, dke, dv = _rows(dqs), _rows(dks), _rows(dqb), _rows(dke), _rows(dv)
        dke_ke = dke * ke_v.astype(F32)
        db = dqs * qs_v.astype(F32) - dks * ks_v.astype(F32) + dqb * qb_v.astype(F32) - dke_ke
        dlg = []
        for c, (db_c, kk_c) in enumerate(zip(_chunks(db, ncb), _chunks(dke_ke, ncb))):
            dbl = jnp.sum(kk_c, axis=0, keepdims=True) + dbl_st[c]
            dlg.append(_dot_exact(tri_up, db_c) + dbl)
        dlg = _rows(dlg)
        dkey = dks * e_rb + dke * e_lb
        dqy = dqs * e_br + dqb * e_b
        dfg = dlg / fg - dkey
        dlb_ref[...] += jnp.sum(dfg * (1.0 - sig), axis=0, keepdims=True)
        dp_ref[0] = (dqy * (sq * (1.0 + qz * (1.0 - sq)))).astype(BF16)
        dp_ref[1] = (dfg * oml * sig * (1.0 - sig)).astype(BF16)
        dp_ref[2] = dv.astype(BF16)
        dp_ref[3] = dgz.astype(BF16)

    def body(q_ref, f_ref, i_ref, g_ref, lg_ref, gn_ref, o_ref, st_ref, dog_ref,
             dp_ref, dlb_ref, dgn_ref, dstate, dst_scr):
        for hs in range(HP):
            cols = slice(hs * HEAD, (hs + 1) * HEAD)
            one_head(q_ref.at[:, cols], f_ref.at[:, cols], i_ref.at[:, cols], g_ref.at[:, cols], lg_ref.at[:, cols],
                     gn_ref, o_ref.at[:, cols], st_ref.at[hs], dog_ref.at[:, cols],
                     dp_ref.at[:, :, cols], dlb_ref.at[hs], dgn_ref.at[hs], dstate.at[hs], dst_scr)

    HP, wide = HGRN_HEADS_PER_STEP, HGRN_HEADS_PER_STEP * HEAD
    hg = H // HP

    def part(p):
        return pl.BlockSpec((tb, wide), functools.partial(lambda h, i, p: (nb - 1 - i, p * hg + h), p=p))

    blk = pl.BlockSpec((tb, wide), lambda h, i: (nb - 1 - i, h))
    acc = pl.BlockSpec((HP, 1, HEAD), lambda h, i: (h, 0, 0))
    r_ops, r_in, r_out, r_shape, r_scr = _rider_args(rider)
    res = pl.pallas_call(
        _ride(rider, body, 9, 3, functools.partial(_grid_corner, 0, 0), functools.partial(_grid_corner, hg - 1, nb - 1)),
        name="hgrn_bwd", grid=(hg, nb),
        in_specs=[part(0), part(1), part(2), part(3),
                  pl.BlockSpec((3, wide), lambda h, i: (0, h)),
                  pl.BlockSpec((1, HEAD), lambda h, i: (0, 0)),
                  blk,
                  pl.BlockSpec((HP, ncb, HEAD, HEAD), lambda h, i: (h, nb - 1 - i, 0, 0)),
                  blk] + r_in,
        out_specs=[pl.BlockSpec((4, tb, wide), lambda h, i: (0, nb - 1 - i, h)), acc, acc] + r_out,
        out_shape=[jax.ShapeDtypeStruct((4, S, H * HEAD), BF16),
                   jax.ShapeDtypeStruct((H, 1, HEAD), F32),
                   jax.ShapeDtypeStruct((H, 1, HEAD), F32)] + r_shape,
        scratch_shapes=[pltpu.VMEM((HP, HEAD, HEAD), F32), pltpu.VMEM((ncb, HEAD, HEAD), F32)] + r_scr,
        compiler_params=_params(("arbitrary", "arbitrary")))(
            proj, proj, proj, proj, logits, gain, o, states, dog, *r_ops)
    return res[:3], res[3:]


def _rope(v, cos, sin):
    return v * cos + pltpu.roll(v, HEAD // 2, 1) * sin


def _lane_pick(tile, hh):
    lane = lax.broadcasted_iota(jnp.int32, tile.shape, 1)
    return jnp.sum(jnp.where(lane == hh, tile, 0.0), axis=-1, keepdims=True)


def _lane_place(cols):
    rows = cols[0].shape[0]
    lane = lax.broadcasted_iota(jnp.int32, (rows, HEAD), 1)
    tile = jnp.zeros((rows, HEAD), F32)
    for hh, v in enumerate(cols):
        tile = jnp.where(lane == hh, v, tile)
    return tile


def _band_masks():
    qi = lax.broadcasted_iota(jnp.int32, (ATTN_SPAN, ATTN_SPAN), 0)
    kj = lax.broadcasted_iota(jnp.int32, (ATTN_SPAN, ATTN_SPAN), 1)
    return kj <= qi, kj >= qi


ATTN_TILE_BLOCKS = 4


def _attn_fwd(a):
    d, L, _ = a.shape
    B, W = ATTN_TILE_BLOCKS, ATTN_SPAN
    T = B * W
    assert L % T == 0
    steps = L // T
    scale = HEAD ** -0.5

    def body(q_ref, kc_ref, kp_ref, vc_ref, vp_ref, o_ref, lse_ref):
        n = pl.program_id(1)
        mask_c, mask_p0 = _band_masks()
        first = jnp.logical_and(mask_p0, n > 0)
        units = [(b, hh) for b in range(B) for hh in range(HEADS_PER_GROUP)]
        rows = [slice(b * W, (b + 1) * W) for b in range(B)]
        cols = [slice(hh * HEAD, (hh + 1) * HEAD) for hh in range(HEADS_PER_GROUP)]

        def prev_keys(ref, tile, b, hh):
            return ref[:, cols[hh]] if b == 0 else tile[rows[b - 1], cols[hh]]

        s_c = [jnp.where(mask_c, _dot_nt(q_ref[rows[b], cols[hh]], kc_ref[rows[b], cols[hh]]) * scale, NEG) for b, hh in units]
        s_p = [jnp.where(first if b == 0 else mask_p0,
                         _dot_nt(q_ref[rows[b], cols[hh]], prev_keys(kp_ref, kc_ref, b, hh)) * scale, NEG) for b, hh in units]
        m = [jnp.maximum(jnp.max(x, axis=-1, keepdims=True), jnp.max(y, axis=-1, keepdims=True)) for x, y in zip(s_c, s_p)]
        p_c = [jnp.exp(x - mm) for x, mm in zip(s_c, m)]
        p_p = [jnp.exp(y - mm) for y, mm in zip(s_p, m)]
        l = [jnp.sum(x, axis=-1, keepdims=True) + jnp.sum(y, axis=-1, keepdims=True) for x, y in zip(p_c, p_p)]
        acc = [_dot(p_c[i].astype(BF16), vc_ref[rows[b], cols[hh]]) + _dot(p_p[i].astype(BF16), prev_keys(vp_ref, vc_ref, b, hh))
               for i, (b, hh) in enumerate(units)]
        for i, (b, hh) in enumerate(units):
            o_ref[rows[b], cols[hh]] = (acc[i] / l[i]).astype(BF16)
        for b in range(B):
            lse_ref[rows[b], :] = _lane_place([m[i] + jnp.log(l[i]) for i, (bb, _) in enumerate(units) if bb == b])

    def cur(part):
        return pl.BlockSpec((None, T, GROUP_W), functools.partial(lambda r, n, p: (r, n, p), p=part))

    def prev(part):
        return pl.BlockSpec((None, W, GROUP_W), functools.partial(lambda r, n, p: (r, jnp.maximum(n * B - 1, 0), p), p=part))

    return pl.pallas_call(
        body, name=f"attn_fwd_d{d}", grid=(d, steps),
        in_specs=[cur(0), cur(1), prev(1), cur(2), prev(2)],
        out_specs=[pl.BlockSpec((None, T, GROUP_W), lambda r, n: (r, n, 0)), pl.BlockSpec((None, T, HEAD), lambda r, n: (r, n, 0))],
        out_shape=[jax.ShapeDtypeStruct((d, L, GROUP_W), BF16), jax.ShapeDtypeStruct((d, L, HEAD), F32)],
        compiler_params=_params(("parallel", "arbitrary")))(a, a, a, a, a)


def _attn_bwd(a, do, lse, dd):
    d, L, _ = a.shape
    B, W = ATTN_TILE_BLOCKS, ATTN_SPAN
    T = B * W
    assert L % T == 0
    steps = L // T
    scale = HEAD ** -0.5

    def body(qc_ref, qn_ref, kp_ref, kc_ref, vp_ref, vc_ref, doc_ref, don_ref, lc_ref, ln_ref, ddc_ref, ddn_ref, da_ref):
        n = pl.program_id(1)
        mask_c, mask_p0 = _band_masks()
        first = jnp.logical_and(mask_p0, n > 0)
        last = jnp.logical_and(mask_p0, n < steps - 1)
        H4 = range(HEADS_PER_GROUP)
        units = [(b, hh) for b in range(B) for hh in H4]
        rows = [slice(b * W, (b + 1) * W) for b in range(B)]
        cols = [slice(hh * HEAD, (hh + 1) * HEAD) for hh in H4]
        q = {u: qc_ref[rows[u[0]], cols[u[1]]] for u in units}
        k = {u: kc_ref[rows[u[0]], cols[u[1]]] for u in units}
        v = {u: vc_ref[rows[u[0]], cols[u[1]]] for u in units}
        g_o = {u: doc_ref[rows[u[0]], cols[u[1]]] for u in units}
        kb = {(b, hh): kp_ref[:, cols[hh]] if b == 0 else k[(b - 1, hh)] for b, hh in units}
        vb = {(b, hh): vp_ref[:, cols[hh]] if b == 0 else v[(b - 1, hh)] for b, hh in units}
        lse_t = {(b, hh): _lane_pick(lc_ref[rows[b], :], hh) for b, hh in units}
        dd_t = {(b, hh): _lane_pick(ddc_ref[rows[b], :], hh) for b, hh in units}
        p_c = {u: jnp.where(mask_c, jnp.exp(_dot_nt(q[u], k[u]) * scale - lse_t[u]), 0.0) for u in units}
        p_p = {u: jnp.where(first if u[0] == 0 else mask_p0, jnp.exp(_dot_nt(q[u], kb[u]) * scale - lse_t[u]), 0.0) for u in units}
        ds_c = {u: (p_c[u] * (_dot_nt(g_o[u], v[u]) + dd_t[u])).astype(BF16) for u in units}
        ds_p = {u: (p_p[u] * (_dot_nt(g_o[u], vb[u]) + dd_t[u])).astype(BF16) for u in units}
        qn = [qn_ref[:, c] for c in cols]
        g_n = [don_ref[:, c] for c in cols]
        p_n = [jnp.where(last, jnp.exp(_dot_nt(qn[hh], k[(B - 1, hh)]) * scale - _lane_pick(ln_ref[...], hh)), 0.0) for hh in H4]
        ds_n = [(p_n[hh] * (_dot_nt(g_n[hh], v[(B - 1, hh)]) + _lane_pick(ddn_ref[...], hh))).astype(BF16) for hh in H4]
        dq = {u: (_dot(ds_c[u], k[u]) + _dot(ds_p[u], kb[u])) * scale for u in units}
        dk, dv = {}, {}
        for b, hh in units:
            if b < B - 1:
                nxt = (b + 1, hh)
                dk[(b, hh)] = (_dot_tn(ds_c[(b, hh)], q[(b, hh)]) + _dot_tn(ds_p[nxt], q[nxt])) * scale
                dv[(b, hh)] = _dot_tn(p_c[(b, hh)].astype(BF16), g_o[(b, hh)]) + _dot_tn(p_p[nxt].astype(BF16), g_o[nxt])
            else:
                dk[(b, hh)] = (_dot_tn(ds_c[(b, hh)], q[(b, hh)]) + _dot_tn(ds_n[hh], qn[hh])) * scale
                dv[(b, hh)] = _dot_tn(p_c[(b, hh)].astype(BF16), g_o[(b, hh)]) + _dot_tn(p_n[hh].astype(BF16), g_n[hh])
        for b, hh in units:
            da_ref[rows[b], cols[hh]] = dq[(b, hh)].astype(BF16)
            da_ref[rows[b], GROUP_W + hh * HEAD:GROUP_W + (hh + 1) * HEAD] = dk[(b, hh)].astype(BF16)
            da_ref[rows[b], 2 * GROUP_W + hh * HEAD:2 * GROUP_W + (hh + 1) * HEAD] = dv[(b, hh)].astype(BF16)

    nb = L // W

    def cur(width, part):
        return pl.BlockSpec((None, T, width), functools.partial(lambda r, n, p: (r, n, p), p=part))

    def prev(width, part):
        return pl.BlockSpec((None, W, width), functools.partial(lambda r, n, p: (r, jnp.maximum(n * B - 1, 0), p), p=part))

    def nxt(width, part):
        return pl.BlockSpec((None, W, width), functools.partial(lambda r, n, p: (r, jnp.minimum(n * B + B, nb - 1), p), p=part))

    g = GROUP_W
    return pl.pallas_call(
        body, name=f"attn_bwd_d{d}", grid=(d, steps),
        in_specs=[cur(g, 0), nxt(g, 0), prev(g, 1), cur(g, 1), prev(g, 2), cur(g, 2),
                  cur(g, 0), nxt(g, 0), cur(HEAD, 0), nxt(HEAD, 0), cur(HEAD, 0), nxt(HEAD, 0)],
        out_specs=pl.BlockSpec((None, T, 3 * g), lambda r, n: (r, n, 0)),
        out_shape=jax.ShapeDtypeStruct((d, L, 3 * g), BF16),
        compiler_params=_params(("parallel", "arbitrary")))(
            a, a, a, a, a, a, do, do, lse, lse, dd, dd)


def _softmax3(ls):
    mx = jnp.maximum(jnp.maximum(ls[0], ls[1]), ls[2])
    es = [jnp.exp(v - mx) for v in ls]
    tot = es[0] + es[1] + es[2]
    return [e / tot for e in es]


HEAD_COLS = [slice(hh * HEAD, (hh + 1) * HEAD) for hh in range(HEADS_PER_GROUP)]


def _group_spec(d, tm):
    return pl.BlockSpec((d, tm // d, GROUP_W), lambda i: (0, i, 0))


def _gather_heads(ref, scr, d, tm):
    if d == 1:
        return [ref[0, :, cols].astype(F32) for cols in HEAD_COLS]
    for hh, cols in enumerate(HEAD_COLS):
        for r in range(d):
            scr.at[hh][pl.ds(r, tm // d, stride=d), :] = ref[r, :, cols].astype(F32)
    return [scr[hh] for hh in range(HEADS_PER_GROUP)]


def _tile_spec(d, tm):
    return pl.BlockSpec((d, tm // d, HEAD), lambda i: (0, i, 0))


def _gather_tile(ref, scr, d, tm):
    if d == 1:
        return ref[0]
    for r in range(d):
        scr[pl.ds(r, tm // d, stride=d), :] = ref[r]
    return scr[...]


def _scatter_tile(val, scr, ref, d, tm):
    if d == 1:
        ref[0] = val
        return
    scr[...] = val
    for r in range(d):
        ref[r] = scr[pl.ds(r, tm // d, stride=d), :]


def _scatter_heads(vals, scr, ref, d, tm):
    if d == 1:
        for cols, v in zip(HEAD_COLS, vals):
            ref[0, :, cols] = v.astype(ref.dtype)
        return
    for hh, v in enumerate(vals):
        scr[hh] = v
    for hh, cols in enumerate(HEAD_COLS):
        for r in range(d):
            ref[r, :, cols] = scr.at[hh][pl.ds(r, tm // d, stride=d), :].astype(ref.dtype)


def _qkv_dilated(h, gain, wg, cos, sin, d, tm=2048):
    S, K = h.shape

    def body(h_ref, g_ref, w_ref, cos_ref, sin_ref, out_ref, u_ref, y_scr):
        p = pl.program_id(1)

        @pl.when(p == 0)
        def _():
            v = h_ref[...]
            u_ref[...] = (v * _rstd(v) * g_ref[...]).astype(BF16)

        y = _dot(u_ref[...], w_ref[...])
        heads = [slice(hh * HEAD, (hh + 1) * HEAD) for hh in range(HEADS_PER_GROUP)]
        if d > 1:
            for hh, cols in enumerate(heads):
                y_scr[hh] = y[:, cols]

        def rows_of(hh, r):
            return y[:, heads[hh]] if d == 1 else y_scr.at[hh][pl.ds(r, tm // d, stride=d), :]

        @pl.when(p < 2)
        def _():
            for r in range(d):
                rows = slice(None) if d == 1 else pl.ds(r, tm // d, stride=d)
                cr, sr = cos_ref[rows, :], sin_ref[rows, :]
                for hh, cols in enumerate(heads):
                    out_ref[r, :, cols] = _rope(rows_of(hh, r), cr, sr).astype(BF16)

        @pl.when(p == 2)
        def _():
            for r in range(d):
                for hh, cols in enumerate(heads):
                    out_ref[r, :, cols] = rows_of(hh, r).astype(BF16)

    tab = pl.BlockSpec((tm, HEAD), lambda i, p: (i, 0))
    return pl.pallas_call(
        body, name=f"attn_qkv_d{d}", grid=(S // tm, 3),
        in_specs=[pl.BlockSpec((tm, K), lambda i, p: (i, 0)),
                  pl.BlockSpec((1, K), lambda i, p: (0, 0)),
                  pl.BlockSpec((K, GROUP_W), lambda i, p: (0, p)), tab, tab],
        out_specs=[pl.BlockSpec((d, tm // d, GROUP_W), lambda i, p: (0, i, p)), pl.BlockSpec((tm, K), lambda i, p: (i, 0))],
        out_shape=[jax.ShapeDtypeStruct((d, S // d, 3 * GROUP_W), BF16), jax.ShapeDtypeStruct((S, K), BF16)],
        scratch_shapes=[pltpu.VMEM((HEADS_PER_GROUP, tm, HEAD), F32)],
        compiler_params=_params(("parallel", "arbitrary")))(h, gain, wg, cos, sin)


def _undilate_group(da, dqkv, cos, sin, g, tm=2048):
    d, L, _ = da.shape
    S = d * L
    G = len(ATTN_GROUPS)

    def body(*refs):
        da_ref, cos_ref, sin_ref, out_ref, scr = refs[0], refs[1], refs[2], refs[-2], refs[-1]
        p = pl.program_id(1)
        heads = [slice(hh * HEAD, (hh + 1) * HEAD) for hh in range(HEADS_PER_GROUP)]
        if d > 1:
            for hh, cols in enumerate(heads):
                for r in range(d):
                    scr.at[hh][pl.ds(r, tm // d, stride=d), :] = da_ref[r, :, cols].astype(F32)

        def tokens(hh):
            return da_ref[0, :, heads[hh]].astype(F32) if d == 1 else scr[hh]

        @pl.when(p < 2)
        def _():
            cr, sr = cos_ref[...], -sin_ref[...]
            for hh, cols in enumerate(heads):
                out_ref[:, cols] = _rope(tokens(hh), cr, sr).astype(BF16)

        @pl.when(p == 2)
        def _():
            for hh, cols in enumerate(heads):
                out_ref[:, cols] = tokens(hh).astype(BF16)

    tab = pl.BlockSpec((tm, HEAD), lambda i, p: (i, 0))
    operands = (da, cos, sin) if dqkv is None else (da, cos, sin, dqkv)
    return pl.pallas_call(
        body, name=f"attn_undilate_d{d}", grid=(S // tm, 3),
        in_specs=[pl.BlockSpec((d, tm // d, GROUP_W), lambda i, p: (0, i, p)), tab, tab] + ([] if dqkv is None else [ANY]),
        out_specs=pl.BlockSpec((tm, GROUP_W), lambda i, p: (i, p * G + g)),
        out_shape=jax.ShapeDtypeStruct((S, 3 * G * GROUP_W), BF16),
        input_output_aliases={} if dqkv is None else {3: 0},
        scratch_shapes=[pltpu.VMEM((HEADS_PER_GROUP, tm, HEAD), F32)],
        compiler_params=_params(("parallel", "arbitrary")))(*operands)


def _attn_merge(os_, lses, tm=1024):
    G = len(os_)
    S = os_[0].shape[0] * os_[0].shape[1]

    def body(*refs):
        o_refs, l_refs, out_ref = refs[:G], refs[G:2 * G], refs[2 * G]
        scr = refs[2 * G + 1:]
        o = [_gather_heads(o_refs[g], scr[g], d, tm) for g, (_, d) in enumerate(ATTN_GROUPS)]
        l = [_gather_tile(l_refs[g], scr[G + g].at[0], d, tm) for g, (_, d) in enumerate(ATTN_GROUPS)]
        for hh in range(HEADS_PER_GROUP):
            al = _softmax3([_lane_pick(l[g], hh) for g in range(G)])
            for g in range(G):
                out_ref[:, g * GROUP_W + hh * HEAD:g * GROUP_W + (hh + 1) * HEAD] = (o[g][hh] * al[g]).astype(BF16)

    specs = [_group_spec(d, tm) for _, d in ATTN_GROUPS]
    return pl.pallas_call(
        body, name="attn_merge", grid=(S // tm,),
        in_specs=specs + [_tile_spec(d, tm) for _, d in ATTN_GROUPS],
        out_specs=pl.BlockSpec((tm, G * GROUP_W), lambda i: (i, 0)),
        out_shape=jax.ShapeDtypeStruct((S, G * GROUP_W), BF16),
        scratch_shapes=[pltpu.VMEM((HEADS_PER_GROUP, tm, HEAD), F32)] * (2 * G),
        compiler_params=_params(("parallel",)))(*os_, *lses)


def _attn_merge_bwd(os_, lses, doa, tm=1024):
    G = len(os_)
    S = doa.shape[0]

    def body(*refs):
        o_refs, l_refs, doa_ref = refs[:G], refs[G:2 * G], refs[2 * G]
        do_refs, dd_refs = refs[2 * G + 1:3 * G + 1], refs[3 * G + 1:4 * G + 1]
        scr = refs[4 * G + 1:]
        o = [_gather_heads(o_refs[g], scr[g], d, tm) for g, (_, d) in enumerate(ATTN_GROUPS)]
        l = [_gather_tile(l_refs[g], scr[G + g].at[0], d, tm) for g, (_, d) in enumerate(ATTN_GROUPS)]
        do = [[None] * HEADS_PER_GROUP for _ in range(G)]
        dd = [[None] * HEADS_PER_GROUP for _ in range(G)]
        for hh in range(HEADS_PER_GROUP):
            al = _softmax3([_lane_pick(l[g], hh) for g in range(G)])
            mix = None
            for g in range(G):
                dg = doa_ref[:, g * GROUP_W + hh * HEAD:g * GROUP_W + (hh + 1) * HEAD]
                do[g][hh] = dg * al[g]
                t = al[g] * jnp.sum(dg * o[g][hh], axis=-1, keepdims=True)
                mix = t if mix is None else mix + t
            for g in range(G):
                dd[g][hh] = -al[g] * mix
        for g, (_, d) in enumerate(ATTN_GROUPS):
            _scatter_heads(do[g], scr[2 * G + g], do_refs[g], d, tm)
            _scatter_tile(_lane_place(dd[g]), scr[3 * G + g].at[0], dd_refs[g], d, tm)

    specs = [_group_spec(d, tm) for _, d in ATTN_GROUPS]
    tiles = [_tile_spec(d, tm) for _, d in ATTN_GROUPS]
    do_shapes = [jax.ShapeDtypeStruct((d, S // d, GROUP_W), BF16) for _, d in ATTN_GROUPS]
    dd_shapes = [jax.ShapeDtypeStruct((d, S // d, HEAD), F32) for _, d in ATTN_GROUPS]
    return pl.pallas_call(
        body, name="attn_merge_bwd", grid=(S // tm,),
        in_specs=specs + tiles + [pl.BlockSpec((tm, G * GROUP_W), lambda i: (i, 0))],
        out_specs=specs + tiles,
        out_shape=do_shapes + dd_shapes,
        scratch_shapes=[pltpu.VMEM((HEADS_PER_GROUP, tm, HEAD), F32)] * (4 * G),
        compiler_params=_params(("parallel",)))(*os_, *lses, doa)


def _rope_tables(S):
    inv_freq = 1.0 / (ROPE_THETA ** (jnp.arange(0, HEAD, 2, dtype=F32) / HEAD))
    ang = jnp.arange(S, dtype=F32)[:, None] * inv_freq[None, :]
    cos, sin = jnp.cos(ang), jnp.sin(ang)
    return jnp.concatenate([cos, cos], axis=-1), jnp.concatenate([-sin, sin], axis=-1)


def _local_step(x, target, norm_mix, norm_ffn, lb_logits, out_gain, final_norm, comm):
    S = x.shape[0]
    nm0, nm1 = norm_mix[0:1], norm_mix[1:2]
    nf0, nf1 = norm_ffn[0:1], norm_ffn[1:2]
    w = comm.first_weights()

    proj, u0, got = _norm_mm(x, nm0, w["hin"], "hgrn_in", rider=comm.gather_rider(LATE_WEIGHTS_A))
    w.update(comm.gathered(LATE_WEIGHTS_A, got))
    (o, og, states), got = _hgrn_fwd(proj, lb_logits, out_gain, rider=comm.gather_rider(LATE_WEIGHTS_B))
    w.update(comm.gathered(LATE_WEIGHTS_B, got))
    fin_tn = w["fin0"].shape[2]
    h1 = _mm_res(x, og, w["hout"], "hgrn_out")
    z0, u1, _ = _norm_mm(h1, nf0, w["fin0"], "ffn0_in", out_dtype=BF16)
    h2, act0 = _swiglu_mm_res(h1, z0, w["fdn0"], "ffn0_down")
    cos, sin = _rope_tables(S)
    G = len(ATTN_GROUPS)
    w_groups = w["qkv"].transpose(1, 0, 2).reshape(D_MODEL, 3, G, GROUP_W)
    a_g, u2 = zip(*[_qkv_dilated(h2, nm1, w_groups[:, :, gi, :].reshape(D_MODEL, 3 * GROUP_W), cos, sin, d)
                    for gi, (_, d) in enumerate(ATTN_GROUPS)])
    o_g, lse_g = zip(*[_attn_fwd(a) for a in a_g])
    oa = _attn_merge(o_g, lse_g)
    h3 = _mm_res(h2, oa, w["aout"], "attn_out")
    z1, u3, _ = _norm_mm(h3, nf1, w["fin1"], "ffn1_in", out_dtype=BF16)
    h4, act1 = _swiglu_mm_res(h3, z1, w["fdn1"], "ffn1_down")
    dh4, loss, d_final = _loss_head(h4, final_norm, target)

    grads, small = {}, {"final_norm": d_final}

    def ffn_bwd(dh, h_in, u_in, z, act, gain, w_in, w_dn, tag, ride=None):
        dz = _mm_nt_swiglu_bwd(dh, w_dn, z, tag + "_down_dx")
        g_dn = _mm_tn(act, dh, 1, D_MODEL, D_MODEL, tag + "_down_dw")[0]
        g_in = _mm_tn(u_in, dz, N_CHIPS, fin_tn, fin_tn, tag + "_in_dw")
        rider = None if ride is None else ride(g_in, g_dn)
        dh_in, dgain, got = _mm_nt_normbwd(dz, w_in, h_in, gain, dh, tag + "_in_dx", rider=rider)
        return dh_in, dgain, g_in, g_dn, got

    dh3, d_nf1, grads["fin1"], grads["fdn1"], _ = ffn_bwd(dh4, h3, u3, z1, act1, nf1, w["fin1"], w["fdn1"], "ffn1")
    doa = _mm_nt(dh3, w["aout"][None], "attn_out_dx")
    grads["aout"] = _mm_tn(oa, dh3, 1, D_MODEL, D_MODEL, "attn_out_dw")[0]
    merged = _attn_merge_bwd(o_g, lse_g, doa)
    G = len(ATTN_GROUPS)
    das = [_attn_bwd(a_g[gi], merged[gi], lse_g[gi], merged[G + gi]) for gi in range(G)]
    dqkv = None
    for gi in range(G):
        dqkv = _undilate_group(das[gi], dqkv, cos, sin, gi)
    n_qkv = w["qkv"].shape[2]
    grads["qkv"] = _mm_tn(u2[0], dqkv, N_CHIPS, n_qkv, n_qkv, "attn_qkv_dw")
    dh2, d_nm1, _ = _mm_nt_normbwd(dqkv, w["qkv"], h2, nm1, dh3, "attn_qkv_dx")

    def ride_early(g_in, g_dn):
        return comm.pair_rider({**grads, "fin0": g_in, "fdn0": g_dn}, "early")

    dh1, d_nf0, _, _, got = ffn_bwd(dh2, h1, u1, z0, act0, nf0, w["fin0"], w["fdn0"], "ffn0", ride=ride_early)
    comm.paired("early", got)
    dog = _mm_nt(dh1, w["hout"][None], "hgrn_out_dx")
    (dproj, dlb, dgn), got = _hgrn_bwd(proj, lb_logits, out_gain, o, states, dog, rider=comm.exchange_rider("early"))
    comm.exchanged("early", got)
    late = {"hout": _mm_tn(og, dh1, 1, D_MODEL, D_MODEL, "hgrn_out_dw")[0],
            "hin": _mm_tn(u0, dproj, N_CHIPS, D_MODEL, D_MODEL, "hgrn_in_dw")}
    comm.pair_now(late, "late")
    dx, d_nm0, got = _mm_nt_normbwd(dproj, w["hin"], x, nm0, dh1, "hgrn_in_dx", rider=comm.exchange_rider("late"))
    comm.exchanged("late", got)

    small["norm_mix"] = jnp.concatenate([d_nm0, d_nm1], axis=0)
    small["norm_ffn"] = jnp.concatenate([d_nf0, d_nf1], axis=0)
    small["lb"] = dlb.reshape(1, HGRN_HEADS * HEAD)
    small["out_norm"] = dgn.reshape(HGRN_HEADS, HEAD)
    return loss, dx, small


def _place():
    x, y, c = lax.axis_index("x"), lax.axis_index("y"), lax.axis_index("c")
    others = [(1 - x, y), (x, 1 - y), (1 - x, 1 - y)]
    return x, y, c, others


ANY = pl.BlockSpec(memory_space=pl.ANY)


class _GatherRider:
    def __init__(self, shards):
        self.operands = list(shards)
        n = self.n = len(shards)
        self.out_shape = [jax.ShapeDtypeStruct((N_CHIPS,) + s.shape, s.dtype) for s in shards]
        self.scratch = [pltpu.SemaphoreType.DMA((3 * n,)), pltpu.SemaphoreType.DMA((3 * n,)),
                        pltpu.SemaphoreType.DMA((3 * n,)), pltpu.SemaphoreType.DMA((3 * n,)),
                        pltpu.SemaphoreType.DMA((n,)), pltpu.SemaphoreType.DMA((n,))]

    def _copies(self, ins, outs, sems):
        ici_send, ici_recv, _, _, own_send, own_recv = sems
        x, y, c, others = _place()
        me = 2 * x + y
        own = [pltpu.make_async_remote_copy(
            src_ref=ins[a], dst_ref=outs[a].at[me], send_sem=own_send.at[a], recv_sem=own_recv.at[a],
            device_id=(x, y, 1 - c), device_id_type=MESH) for a in range(self.n)]
        sends = [pltpu.make_async_remote_copy(
            src_ref=ins[a].at[c], dst_ref=outs[a].at[me, c], send_sem=ici_send.at[a * 3 + k], recv_sem=ici_recv.at[a * 3 + k],
            device_id=(ox, oy, c), device_id_type=MESH) for a in range(self.n) for k, (ox, oy) in enumerate(others)]
        return own, sends

    def start(self, ins, outs, sems):
        own, sends = self._copies(ins, outs, sems)
        for cp in own + sends:
            cp.start()

    def finish(self, ins, outs, sems):
        ici_send, ici_recv, d2d_send, d2d_recv, _, _ = sems
        x, y, c, others = _place()
        sibling = (x, y, 1 - c)
        own, sends = self._copies(ins, outs, sems)
        passes = []
        for a in range(self.n):
            for k, (ox, oy) in enumerate(others):
                s = a * 3 + k
                got = outs[a].at[2 * ox + oy, c]
                pltpu.make_async_remote_copy(
                    src_ref=got, dst_ref=got, send_sem=ici_send.at[s], recv_sem=ici_recv.at[s],
                    device_id=(ox, oy, c), device_id_type=MESH).wait_recv()
                fwd = pltpu.make_async_remote_copy(
                    src_ref=got, dst_ref=got, send_sem=d2d_send.at[s], recv_sem=d2d_recv.at[s],
                    device_id=sibling, device_id_type=MESH)
                fwd.start()
                passes.append(fwd)
        for a in range(self.n):
            for k, (ox, oy) in enumerate(others):
                s = a * 3 + k
                theirs = outs[a].at[2 * ox + oy, 1 - c]
                pltpu.make_async_remote_copy(
                    src_ref=theirs, dst_ref=theirs, send_sem=d2d_send.at[s], recv_sem=d2d_recv.at[s],
                    device_id=sibling, device_id_type=MESH).wait_recv()
        for cp in own:
            cp.wait()
        for cp in sends + passes:
            cp.wait_send()


class _PairRider:
    def __init__(self, grads):
        self.operands = list(grads)
        n = self.n = len(grads)
        self.out_shape = [jax.ShapeDtypeStruct((N_CHIPS,) + g.shape[2:], F32) for g in grads]
        self.scratch = [pltpu.SemaphoreType.DMA((N_CHIPS * n,)), pltpu.SemaphoreType.DMA((N_CHIPS * n,))]

    def _copies(self, ins, outs, sems):
        send_sem, recv_sem = sems
        x, y, c, _ = _place()
        return [pltpu.make_async_remote_copy(
            src_ref=ins[a].at[j, 1 - c], dst_ref=outs[a].at[j], send_sem=send_sem.at[a * N_CHIPS + j],
            recv_sem=recv_sem.at[a * N_CHIPS + j], device_id=(x, y, 1 - c), device_id_type=MESH)
            for a in range(self.n) for j in range(N_CHIPS)]

    def start(self, ins, outs, sems):
        for cp in self._copies(ins, outs, sems):
            cp.start()

    def finish(self, ins, outs, sems):
        for cp in self._copies(ins, outs, sems):
            cp.wait()


class _ExchangeRider:
    def __init__(self, parts):
        self.operands = list(parts)
        n = self.n = len(parts)
        self.out_shape = [jax.ShapeDtypeStruct(p.shape, p.dtype) for p in parts]
        self.scratch = [pltpu.SemaphoreType.DMA((3 * n,)), pltpu.SemaphoreType.DMA((3 * n,))]

    def _copies(self, ins, outs, sems):
        send_sem, recv_sem = sems
        x, y, c, others = _place()
        me = 2 * x + y
        return [pltpu.make_async_remote_copy(
            src_ref=ins[a].at[2 * ox + oy], dst_ref=outs[a].at[me], send_sem=send_sem.at[a * 3 + k],
            recv_sem=recv_sem.at[a * 3 + k], device_id=(ox, oy, c), device_id_type=MESH)
            for a in range(self.n) for k, (ox, oy) in enumerate(others)]

    def start(self, ins, outs, sems):
        for cp in self._copies(ins, outs, sems):
            cp.start()

    def finish(self, ins, outs, sems):
        send_sem, recv_sem = sems
        x, y, c, others = _place()
        for a in range(self.n):
            for k, (ox, oy) in enumerate(others):
                s = a * 3 + k
                got = outs[a].at[2 * ox + oy]
                pltpu.make_async_remote_copy(
                    src_ref=got, dst_ref=got, send_sem=send_sem.at[s], recv_sem=recv_sem.at[s],
                    device_id=(ox, oy, c), device_id_type=MESH).wait_recv()
        for cp in self._copies(ins, outs, sems):
            cp.wait_send()


def _run_rider(rider, name):
    n = rider.n

    def body(*refs):
        ins, outs, sems = refs[:n], refs[n:2 * n], refs[2 * n:]
        rider.start(ins, outs, sems)
        rider.finish(ins, outs, sems)

    return pl.pallas_call(
        body, name=name, in_specs=[ANY] * n, out_specs=[ANY] * n,
        out_shape=rider.out_shape, scratch_shapes=rider.scratch)(*rider.operands)


def _ride(rider, body, n_in, n_out, first, last):
    if rider is None:
        return body
    n = rider.n

    def wrapped(*refs):
        host_in, r_in = refs[:n_in], refs[n_in:n_in + n]
        host_out = refs[n_in + n:n_in + n + n_out]
        r_out = refs[n_in + n + n_out:n_in + 2 * n + n_out]
        rest = refs[n_in + 2 * n + n_out:]
        host_scr, sems = rest[:len(rest) - len(rider.scratch)], rest[len(rest) - len(rider.scratch):]

        @pl.when(first())
        def _():
            rider.start(r_in, r_out, sems)

        body(*host_in, *host_out, *host_scr)

        @pl.when(last())
        def _():
            rider.finish(r_in, r_out, sems)

    return wrapped


def _rider_args(rider):
    if rider is None:
        return [], [], [], [], []
    return rider.operands, [ANY] * rider.n, [ANY] * rider.n, rider.out_shape, rider.scratch


def _pair_sum(g, got, c_idx):
    _, _, r, cw = g.shape
    tr = _row_tile(r, cw)

    def body(c_ref, g_ref, got_ref, pb_ref):
        pb_ref[...] = (g_ref[...] + got_ref[...]).astype(BF16)

    blk = pl.BlockSpec((None, tr, cw), lambda j, i, c_ref: (j, i, 0))
    return pl.pallas_call(
        body, name="grad_pair_sum",
        grid_spec=pltpu.PrefetchScalarGridSpec(
            num_scalar_prefetch=1, grid=(N_CHIPS, r // tr),
            in_specs=[pl.BlockSpec((None, None, tr, cw), lambda j, i, c_ref: (j, c_ref[0], i, 0)), blk],
            out_specs=blk),
        out_shape=jax.ShapeDtypeStruct((N_CHIPS, r, cw), BF16),
        compiler_params=_params(("parallel", "parallel")))(c_idx, g, got)


def _chip_sum(g, sib, got, place):
    _, _, r, cw = g.shape
    tr = _row_tile(r, cw)

    def body(place_ref, g_ref, sib_ref, got_ref, t_ref):
        me = place_ref[0]
        own = g_ref[...] + sib_ref[...]
        acc = None
        for s in range(N_CHIPS):
            term = jnp.where(me == s, own, got_ref[s].astype(F32))
            acc = term if acc is None else acc + term
        t_ref[...] = acc

    return pl.pallas_call(
        body, name="grad_chip_sum",
        grid_spec=pltpu.PrefetchScalarGridSpec(
            num_scalar_prefetch=1, grid=(r // tr,),
            in_specs=[pl.BlockSpec((None, None, tr, cw), lambda i, pr: (pr[0], pr[1], i, 0)),
                      pl.BlockSpec((None, tr, cw), lambda i, pr: (pr[0], i, 0)),
                      pl.BlockSpec((N_CHIPS, tr, cw), lambda i, pr: (0, i, 0))],
            out_specs=pl.BlockSpec((tr, cw), lambda i, pr: (i, 0))),
        out_shape=jax.ShapeDtypeStruct((r, cw), F32),
        compiler_params=_params(("parallel",)))(place, g, sib, got)


def _pair_share(halves):
    n = len(halves)

    def body(*refs):
        ins, outs = refs[:n], refs[n:2 * n]
        send_sem, recv_sem = refs[2 * n:]
        x, y, c, _ = _place()
        cps = [pltpu.make_async_remote_copy(
            src_ref=ins[a], dst_ref=outs[a], send_sem=send_sem.at[a], recv_sem=recv_sem.at[a],
            device_id=(x, y, 1 - c), device_id_type=MESH) for a in range(n)]
        for cp in cps:
            cp.start()
        for cp in cps:
            cp.wait()

    return pl.pallas_call(
        body, name="grad_pair_share",
        in_specs=[ANY] * n, out_specs=[ANY] * n,
        out_shape=[jax.ShapeDtypeStruct(h.shape, F32) for h in halves],
        scratch_shapes=[pltpu.SemaphoreType.DMA((n,)), pltpu.SemaphoreType.DMA((n,))],
        )(*halves)


def _small_allreduce(pack):
    m_per, ncol = pack.shape
    n_dev = 8

    def body(x_ref, sum_ref, all_ref, send_sems, recv_sems, local_sem):
        x, y, c, others = _place()
        me, sibling = (x, y, c), (x, y, 1 - c)

        def rows(px, py, pc):
            return all_ref.at[pl.ds((4 * px + 2 * py + pc) * m_per, m_per), :]

        def copy(k, block, to, src=None):
            return pltpu.make_async_remote_copy(
                src_ref=rows(*block) if src is None else src, dst_ref=rows(*block),
                send_sem=send_sems.at[k], recv_sem=recv_sems.at[k], device_id=to, device_id_type=MESH)

        mine = pltpu.make_async_copy(x_ref, rows(*me), local_sem)
        mine.start()
        first = [copy(0, me, sibling, src=x_ref)]
        first += [copy(1 + j, me, (*chip, c), src=x_ref) for j, chip in enumerate(others)]
        for cp in first:
            cp.start()
        passed = [copy(4 + j, (*chip, c), sibling) for j, chip in enumerate(others)]
        for j, chip in enumerate(others):
            copy(1 + j, (*chip, c), me).wait_recv()
            passed[j].start()
        copy(0, sibling, me).wait_recv()
        for j, chip in enumerate(others):
            copy(4 + j, (*chip, 1 - c), me).wait_recv()
        for cp in first + passed:
            cp.wait_send()
        mine.wait()
        acc = all_ref[0:m_per, :]
        for dvc in range(1, n_dev):
            acc = acc + all_ref[dvc * m_per:(dvc + 1) * m_per, :]
        sum_ref[...] = acc

    return pl.pallas_call(
        body, name="small_allreduce",
        in_specs=[pl.BlockSpec(memory_space=pltpu.VMEM)],
        out_specs=pl.BlockSpec(memory_space=pltpu.VMEM),
        out_shape=jax.ShapeDtypeStruct((m_per, ncol), F32),
        scratch_shapes=[pltpu.VMEM((n_dev * m_per, ncol), F32),
                        pltpu.SemaphoreType.DMA((7,)), pltpu.SemaphoreType.DMA((7,)), pltpu.SemaphoreType.DMA],
        )(pack)


def _adam_math(w, g, m, v):
    m = ADAM_B1 * m + (1.0 - ADAM_B1) * g
    v = ADAM_B2 * v + (1.0 - ADAM_B2) * (g * g)
    m_hat = m / (1.0 - ADAM_B1 ** ADAM_STEP)
    v_hat = v / (1.0 - ADAM_B2 ** ADAM_STEP)
    delta = -ADAM_LR * (m_hat / (jnp.sqrt(v_hat) + ADAM_EPS) + ADAM_WD * w)
    return delta, m, v


def _adamw(halves, c_idx, w, m, v, name):
    L = len(halves)
    r, C = halves[0][0].shape
    tr = _row_tile(r, C, 1024 * 1024)
    nt = r // tr

    def body(c_ref, *refs):
        g_refs, (w_ref, m_ref, v_ref), (g_ref, d_ref, nm_ref, nv_ref) = refs[:2 * L], refs[2 * L:2 * L + 3], refs[2 * L + 3:]
        own = pl.program_id(1) == c_ref[0]
        g = None
        for l in range(L):
            cand = jnp.where(own, g_refs[2 * l][...], g_refs[2 * l + 1][...])
            g = cand if g is None else jnp.where(pl.program_id(0) == l, cand, g)
        g_ref[...] = g
        d_ref[...], nm_ref[...], nv_ref[...] = _adam_math(w_ref[...], g, m_ref[...], v_ref[...])

    def half(l, mine):
        def index(ll, h, i, c_ref):
            read = (h == c_ref[0]) if mine else (h != c_ref[0])
            return jnp.where(jnp.logical_and(ll == l, read), i, 0), 0
        return pl.BlockSpec((tr, C), index)

    full = pl.BlockSpec((None, tr, C), lambda ll, h, i, c_ref: (ll, h * nt + i, 0))
    shp = jax.ShapeDtypeStruct((L, 2 * r, C), F32)
    g_specs = [half(l, mine) for l in range(L) for mine in (True, False)]
    return pl.pallas_call(
        body, name=name,
        grid_spec=pltpu.PrefetchScalarGridSpec(
            num_scalar_prefetch=1, grid=(L, 2, nt),
            in_specs=g_specs + [full] * 3, out_specs=[full] * 4),
        out_shape=[shp] * 4,
        compiler_params=_params(("arbitrary", "arbitrary", "arbitrary")))(
            c_idx, *[a for pair in halves for a in pair], w, m, v)


def _small_update(gsum, logits_pack, w, m, v):
    def body(gs_ref, lg_ref, w_ref, m_ref, v_ref, g_ref, d_ref, nm_ref, nv_ref):
        g_ref[...] = gs_ref[...]
        l0, l1, l2 = lg_ref[0:1, :], lg_ref[1:2, :], lg_ref[2:3, :]
        mx = jnp.maximum(jnp.maximum(l0, l1), l2)
        e0, e1, e2 = jnp.exp(l0 - mx), jnp.exp(l1 - mx), jnp.exp(l2 - mx)
        tot = e0 + e1 + e2
        p0, p1, p2 = e0 / tot, e1 / tot, e2 / tot
        dlb = gs_ref[4:5, :]
        g_ref[4:5, :] = dlb * p0 * (1.0 - p0)
        g_ref[5:6, :] = -dlb * p0 * p1
        g_ref[6:7, :] = -dlb * p0 * p2
        d_ref[...], nm_ref[...], nv_ref[...] = _adam_math(w_ref[...], g_ref[...], m_ref[...], v_ref[...])

    full = pl.BlockSpec(memory_space=pltpu.VMEM)
    shp = jax.ShapeDtypeStruct(gsum.shape, F32)
    return pl.pallas_call(
        body, name="small_update", in_specs=[full] * 5, out_specs=[full] * 4, out_shape=[shp] * 4)(
            gsum, logits_pack, w, m, v)


def _pack_small(norm_mix, norm_ffn, lb3, out_norm, final_norm, extra=None):
    ncol = norm_mix.shape[1]
    on = jnp.pad(out_norm.reshape(1, -1), ((0, 0), (0, ncol - out_norm.size)))
    rows = [norm_mix, norm_ffn, lb3, on, final_norm.reshape(1, ncol)]
    if extra is not None:
        rows.append(extra)
    used = sum(r.shape[0] for r in rows)
    rows.append(jnp.zeros((SMALL_ROWS - used, ncol), F32))
    return jnp.concatenate(rows, axis=0)


WEIGHT_NAMES = ("hin", "hout", "qkv", "aout", "fin0", "fin1", "fdn0", "fdn1")
FIRST_WEIGHTS = ("hin",)
LATE_WEIGHTS_A = ("hout", "fin0", "fdn0")
LATE_WEIGHTS_B = ("qkv", "aout", "fin1", "fdn1")


def _split_weights(hgrn_w_in, hgrn_w_out, attn_w_qkv, attn_w_out, ffn_w_in, ffn_w_down):
    return {"hin": hgrn_w_in[0], "hout": hgrn_w_out[0], "qkv": attn_w_qkv[0], "aout": attn_w_out[0],
            "fin0": ffn_w_in[0], "fin1": ffn_w_in[1], "fdn0": ffn_w_down[0], "fdn1": ffn_w_down[1]}


def _halves(v):
    r, c = v.shape
    return v.reshape(2, r // 2, c)


def _full_weights(gathered):
    out = {}
    for k, g in gathered.items():
        _, _, r, c = g.shape
        if k in ("hin", "qkv", "fin0", "fin1"):
            out[k] = g.reshape(N_CHIPS, 2 * r, c)
        else:
            out[k] = g.reshape(N_CHIPS * 2 * r, c)
    return out


class _StepComm:
    def __init__(self, shards, c_idx, me_idx):
        self.shards, self.c_idx, self.me_idx = shards, c_idx, me_idx
        self.halves = {}
        self._stage = {}

    def gather_rider(self, names):
        return _GatherRider([_halves(self.shards[k].astype(BF16)) for k in names])

    def gathered(self, names, got):
        return _full_weights(dict(zip(names, got)))

    def first_weights(self):
        return self.gathered(FIRST_WEIGHTS, _run_rider(self.gather_rider(FIRST_WEIGHTS), "gather_first"))

    def pair_rider(self, grads, tag):
        names = list(grads)
        g4 = []
        for k in names:
            r, c = self.shards[k].shape
            g4.append(grads[k].reshape(N_CHIPS, 2, r // 2, c))
        self._stage[tag] = (names, g4)
        return _PairRider(g4)

    def pair_now(self, grads, tag):
        self.paired(tag, _run_rider(self.pair_rider(grads, tag), "grad_pair_exchange_" + tag))

    def paired(self, tag, got):
        names, g4 = self._stage[tag]
        self._stage[tag] = (names, [(g, s, _pair_sum(g, s, self.c_idx)) for g, s in zip(g4, got)])

    def exchange_rider(self, tag):
        return _ExchangeRider([s[2] for s in self._stage[tag][1]])

    def exchanged(self, tag, got):
        names, sums = self._stage.pop(tag)
        place = jnp.concatenate([self.me_idx, self.c_idx])
        for k, (g, sib, _), recv in zip(names, sums, got):
            self.halves[k] = _chip_sum(g, sib, recv, place)

    def shared_halves(self):
        mine = [self.halves[k] for k in WEIGHT_NAMES]
        return dict(zip(WEIGHT_NAMES, zip(mine, _pair_share(mine))))


def kernel(x, norm_mix, norm_ffn, hgrn_w_in, hgrn_lb_logits, hgrn_out_norm, hgrn_w_out, attn_w_qkv, attn_w_out, ffn_w_in, ffn_w_down, final_norm, loss_target, m_norm_mix, m_norm_ffn, m_hgrn_w_in, m_hgrn_lb_logits, m_hgrn_out_norm, m_hgrn_w_out, m_attn_w_qkv, m_attn_w_out, m_ffn_w_in, m_ffn_w_down, m_final_norm, v_norm_mix, v_norm_ffn, v_hgrn_w_in, v_hgrn_lb_logits, v_hgrn_out_norm, v_hgrn_w_out, v_attn_w_qkv, v_attn_w_out, v_ffn_w_in, v_ffn_w_down, v_final_norm):
    S = x.shape[1]
    xi, yi, ci = lax.axis_index("x"), lax.axis_index("y"), lax.axis_index("c")
    c_idx = jnp.reshape(ci, (1,)).astype(jnp.int32)
    me_idx = jnp.reshape(2 * xi + yi, (1,)).astype(jnp.int32)

    w_own = _split_weights(hgrn_w_in, hgrn_w_out, attn_w_qkv, attn_w_out, ffn_w_in, ffn_w_down)

    comm = _StepComm(w_own, c_idx, me_idx)
    loss, dx, small = _local_step(
        x.reshape(S, D_MODEL), loss_target.reshape(S, D_MODEL), norm_mix, norm_ffn, hgrn_lb_logits,
        hgrn_out_norm, final_norm.reshape(1, D_MODEL), comm)

    halves = comm.shared_halves()
    updated = {}
    for tensor, layers, (wt, mt, vt) in (
            ("hgrn_w_in", ("hin",), (hgrn_w_in, m_hgrn_w_in, v_hgrn_w_in)),
            ("hgrn_w_out", ("hout",), (hgrn_w_out, m_hgrn_w_out, v_hgrn_w_out)),
            ("attn_w_qkv", ("qkv",), (attn_w_qkv, m_attn_w_qkv, v_attn_w_qkv)),
            ("attn_w_out", ("aout",), (attn_w_out, m_attn_w_out, v_attn_w_out)),
            ("ffn_w_in", ("fin0", "fin1"), (ffn_w_in, m_ffn_w_in, v_ffn_w_in)),
            ("ffn_w_down", ("fdn0", "fdn1"), (ffn_w_down, m_ffn_w_down, v_ffn_w_down))):
        updated[tensor] = _adamw([halves[k] for k in layers], c_idx, wt, mt, vt, "adamw_" + tensor)

    loss_row = jnp.pad(loss, ((0, 0), (0, D_MODEL - loss.shape[1])))
    lb3 = jnp.concatenate([small["lb"], jnp.zeros((2, D_MODEL), F32)], axis=0)
    on_grad = jnp.sum(small["out_norm"], axis=0, keepdims=True)
    pack = _pack_small(small["norm_mix"], small["norm_ffn"], lb3, on_grad, small["final_norm"], loss_row)
    gsum = _small_allreduce(pack)
    w_s = _pack_small(norm_mix, norm_ffn, hgrn_lb_logits, hgrn_out_norm, final_norm)
    m_s = _pack_small(m_norm_mix, m_norm_ffn, m_hgrn_lb_logits, m_hgrn_out_norm, m_final_norm)
    v_s = _pack_small(v_norm_mix, v_norm_ffn, v_hgrn_lb_logits, v_hgrn_out_norm, v_final_norm)
    lg_pack = jnp.pad(hgrn_lb_logits, ((0, 8 - hgrn_lb_logits.shape[0]), (0, 0)))
    sg, sd, sm, sv = _small_update(gsum, lg_pack, w_s, m_s, v_s)

    def unpack(p):
        return (p[0:2], p[2:4], p[4:7], p[7:8, :HEAD], p[8])

    def assemble(p, which):
        nmx, nff, lbl, onm, fnm = unpack(p)
        hin, hout, qkv, aout, fin, fdn = [updated[t][which] for t in
                                          ("hgrn_w_in", "hgrn_w_out", "attn_w_qkv", "attn_w_out", "ffn_w_in", "ffn_w_down")]
        return (nmx, nff, hin, lbl, onm, hout, qkv, aout, fin, fdn, fnm)

    total_loss = gsum[9, 0]
    return (total_loss, dx.reshape(1, S, D_MODEL), *assemble(sg, 0), *assemble(sd, 1), *assemble(sm, 2), *assemble(sv, 3))
```

```python
import functools

import jax
import jax.numpy as jnp
from jax import lax
from jax.experimental import pallas as pl
from jax.experimental.pallas import tpu as pltpu

F32 = jnp.float32
BF16 = jnp.bfloat16
MESH = pl.DeviceIdType.MESH

D_MODEL = 1024
HEAD = 128
HGRN_HEADS = 8
HGRN_CHUNK = 64
HGRN_HEADS_PER_STEP = 2
ATTN_GROUPS = ((128, 1), (512, 4), (2048, 16))
ATTN_SPAN = 128
HEADS_PER_GROUP = 4
GROUP_W = HEADS_PER_GROUP * HEAD
D_FF = 2816
NORM_EPS = 1e-6
ROPE_THETA = 10000.0
NEG = -1e30

ADAM_LR, ADAM_B1, ADAM_B2, ADAM_EPS, ADAM_WD, ADAM_STEP = 0.001, 0.9, 0.999, 1e-08, 0.01, 10

N_CHIPS = 4
VMEM_LIMIT = 56 * 1024 * 1024
SMALL_ROWS = 16


def _params(sem=None):
    return pltpu.CompilerParams(dimension_semantics=sem, vmem_limit_bytes=VMEM_LIMIT)


def _row_tile(rows, cols, budget_bytes=3 * 512 * 1024):
    best = 8
    for t in range(8, rows + 1, 8):
        if rows % t == 0 and t * cols * 4 <= budget_bytes:
            best = t
    assert rows % best == 0
    return best


def _grid_corner(i, j):
    return jnp.logical_and(pl.program_id(0) == i, pl.program_id(1) == j)


def _sigmoid(v):
    return 0.5 * jnp.tanh(0.5 * v) + 0.5


def _dot(a, b):
    return jnp.dot(a, b, preferred_element_type=F32)


def _dot_nt(a, b):
    return lax.dot_general(a, b, (((1,), (1,)), ((), ())), preferred_element_type=F32)


def _dot_tn(a, b):
    return lax.dot_general(a, b, (((0,), (0,)), ((), ())), preferred_element_type=F32)


def _dot_exact(ones, b):
    ones = ones.astype(BF16)
    hi = b.astype(BF16)
    rest = b - hi.astype(F32)
    mid = rest.astype(BF16)
    low = (rest - mid.astype(F32)).astype(BF16)
    return _dot(ones, hi) + _dot(ones, mid) + _dot(ones, low)


def _rstd(v):
    return lax.rsqrt(jnp.mean(v * v, axis=-1, keepdims=True) + NORM_EPS)


def _norm_mm(h, gain, w3, name, out_dtype=F32, tm=2048, rider=None):
    S, K = h.shape
    J, _, n = w3.shape
    gi = S // tm

    def body(h_ref, g_ref, w_ref, y_ref, u_ref):
        @pl.when(pl.program_id(1) == 0)
        def _():
            v = h_ref[...]
            u_ref[...] = (v * _rstd(v) * g_ref[...]).astype(BF16)

        y_ref[...] = _dot(u_ref[...], w_ref[pl.program_id(1)]).astype(y_ref.dtype)

    r_ops, r_in, r_out, r_shape, r_scr = _rider_args(rider)
    res = pl.pallas_call(
        _ride(rider, body, 3, 2, functools.partial(_grid_corner, 0, 0), functools.partial(_grid_corner, gi - 1, J - 1)),
        name=name, grid=(gi, J),
        in_specs=[pl.BlockSpec((tm, K), lambda i, j: (i, 0)),
                  pl.BlockSpec((1, K), lambda i, j: (0, 0)),
                  pl.BlockSpec((J, K, n), lambda i, j: (0, 0, 0), pipeline_mode=pl.Buffered(1))] + r_in,
        out_specs=[pl.BlockSpec((tm, n), lambda i, j: (i, j)), pl.BlockSpec((tm, K), lambda i, j: (i, 0))] + r_out,
        out_shape=[jax.ShapeDtypeStruct((S, J * n), out_dtype), jax.ShapeDtypeStruct((S, K), BF16)] + r_shape,
        scratch_shapes=r_scr,
        compiler_params=_params(("arbitrary", "arbitrary")))(h, gain, w3, *r_ops)
    return res[0], res[1], res[2:]


def _mm_res(h, a, w2, name, tm=1024):
    S, N = h.shape
    K = a.shape[1]

    def body(h_ref, a_ref, w_ref, o_ref):
        o_ref[...] = h_ref[...] + _dot(a_ref[...], w_ref[...])

    return pl.pallas_call(
        body, name=name, grid=(S // tm,),
        in_specs=[pl.BlockSpec((tm, N), lambda i: (i, 0)),
                  pl.BlockSpec((tm, K), lambda i: (i, 0)),
                  pl.BlockSpec((K, N), lambda i: (0, 0))],
        out_specs=pl.BlockSpec((tm, N), lambda i: (i, 0)),
        out_shape=jax.ShapeDtypeStruct((S, N), F32),
        compiler_params=_params(("parallel",)))(h, a, w2)


def _swiglu(z_ref, F):
    g = z_ref[:, :F].astype(F32)
    return (g * _sigmoid(g) * z_ref[:, F:].astype(F32)).astype(BF16)


def _swiglu_mm_res(h, z, w2, name, tm=512):
    S, N = h.shape
    F = w2.shape[0]

    def body(h_ref, z_ref, w_ref, o_ref, a_ref):
        a = _swiglu(z_ref, F)
        a_ref[...] = a
        o_ref[...] = h_ref[...] + _dot(a, w_ref[...])

    return pl.pallas_call(
        body, name=name, grid=(S // tm,),
        in_specs=[pl.BlockSpec((tm, N), lambda i: (i, 0)),
                  pl.BlockSpec((tm, 2 * F), lambda i: (i, 0)),
                  pl.BlockSpec((F, N), lambda i: (0, 0), pipeline_mode=pl.Buffered(1))],
        out_specs=[pl.BlockSpec((tm, N), lambda i: (i, 0)), pl.BlockSpec((tm, F), lambda i: (i, 0))],
        out_shape=[jax.ShapeDtypeStruct((S, N), F32), jax.ShapeDtypeStruct((S, F), BF16)],
        compiler_params=_params(("parallel",)))(h, z, w2)


def _dy_specs(dy, J, n, tm):
    if dy.ndim == 3:
        return [pl.BlockSpec((None, tm, n), functools.partial(lambda i, j: (j, i, 0), j=j)) for j in range(J)]
    return [pl.BlockSpec((tm, n), functools.partial(lambda i, j: (i, j), j=j)) for j in range(J)]


def _acc_nt(dy_refs, w_ref):
    acc = None
    for j, r in enumerate(dy_refs):
        t = _dot_nt(r[...].astype(BF16), w_ref[j])
        acc = t if acc is None else acc + t
    return acc


def _mm_nt(dy, w3, name, out_dtype=F32, tm=1024):
    J, K, n = w3.shape
    S = dy.shape[-2]

    def body(*refs):
        dy_refs, w_ref, o_ref = refs[:J], refs[J], refs[J + 1]
        o_ref[...] = _acc_nt(dy_refs, w_ref).astype(o_ref.dtype)

    return pl.pallas_call(
        body, name=name, grid=(S // tm,),
        in_specs=_dy_specs(dy, J, n, tm) + [pl.BlockSpec((J, K, n), lambda i: (0, 0, 0))],
        out_specs=pl.BlockSpec((tm, K), lambda i: (i, 0)),
        out_shape=jax.ShapeDtypeStruct((S, K), out_dtype),
        compiler_params=_params(("parallel",)))(*([dy] * J), w3)


def _mm_nt_normbwd(dy, w3, h, gain, dh, name, tm=512, rider=None):
    J, K, n = w3.shape
    S = h.shape[0]
    steps = S // tm

    def body(*refs):
        dy_refs, w_ref, h_ref, g_ref, dh_ref, o_ref, dg_ref = refs[:J], *refs[J:]
        du = _acc_nt(dy_refs, w_ref)
        v = h_ref[...]
        r = _rstd(v)
        xh = v * r
        dyg = du * g_ref[...]
        o_ref[...] = dh_ref[...] + r * (dyg - xh * jnp.mean(dyg * xh, axis=-1, keepdims=True))

        @pl.when(pl.program_id(0) == 0)
        def _():
            dg_ref[...] = jnp.zeros_like(dg_ref)

        dg_ref[...] += jnp.sum(du * xh, axis=0, keepdims=True)

    row = pl.BlockSpec((tm, K), lambda i: (i, 0))
    vec = pl.BlockSpec((1, K), lambda i: (0, 0))
    r_ops, r_in, r_out, r_shape, r_scr = _rider_args(rider)
    res = pl.pallas_call(
        _ride(rider, body, J + 4, 2, lambda: pl.program_id(0) == 0, lambda: pl.program_id(0) == steps - 1),
        name=name, grid=(steps,),
        in_specs=_dy_specs(dy, J, n, tm) + [pl.BlockSpec((J, K, n), lambda i: (0, 0, 0)), row, vec, row] + r_in,
        out_specs=[row, vec] + r_out,
        out_shape=[jax.ShapeDtypeStruct((S, K), F32), jax.ShapeDtypeStruct((1, K), F32)] + r_shape,
        scratch_shapes=r_scr,
        compiler_params=_params(("arbitrary",)))(*([dy] * J), w3, h, gain, dh, *r_ops)
    return res[0], res[1], res[2:]


def _mm_nt_swiglu_bwd(dh, w2, z, name, tm=256):
    F, N = w2.shape
    S = dh.shape[0]

    def body(dh_ref, w_ref, z_ref, o_ref):
        da = _dot_nt(dh_ref[...].astype(BF16), w_ref[...])
        g = z_ref[:, :F].astype(F32)
        u = z_ref[:, F:].astype(F32)
        sg = _sigmoid(g)
        o_ref[:, :F] = (da * u * (sg * (1.0 + g * (1.0 - sg)))).astype(BF16)
        o_ref[:, F:] = (da * (g * sg)).astype(BF16)

    return pl.pallas_call(
        body, name=name, grid=(S // tm,),
        in_specs=[pl.BlockSpec((tm, N), lambda i: (i, 0)),
                  pl.BlockSpec((F, N), lambda i: (0, 0)),
                  pl.BlockSpec((tm, 2 * F), lambda i: (i, 0))],
        out_specs=pl.BlockSpec((tm, 2 * F), lambda i: (i, 0)),
        out_shape=jax.ShapeDtypeStruct((S, 2 * F), BF16),
        compiler_params=_params(("parallel",)))(dh, w2, z)


def _mm_tn(x, dy, J, n, tn, name):
    tpn = n // tn
    ts = 2048 if x.shape[1] <= 1536 else 1024
    S, K = x.shape
    if dy.ndim == 3:
        dy_spec = pl.BlockSpec((None, ts, tn), lambda c, s: (c // tpn, s, c % tpn))
    else:
        dy_spec = pl.BlockSpec((ts, tn), lambda c, s: (s, c))

    def body(x_ref, dy_ref, o_ref):
        @pl.when(pl.program_id(1) == 0)
        def _():
            o_ref[...] = jnp.zeros_like(o_ref)

        o_ref[...] += _dot_tn(x_ref[...], dy_ref[...].astype(BF16))

    return pl.pallas_call(
        body, name=name, grid=(J * tpn, S // ts),
        in_specs=[pl.BlockSpec((ts, K), lambda c, s: (s, 0)), dy_spec],
        out_specs=pl.BlockSpec((None, K, tn), lambda c, s: (c // tpn, 0, c % tpn)),
        out_shape=jax.ShapeDtypeStruct((J, K, n), F32),
        compiler_params=_params(("parallel", "arbitrary")))(x, dy)


def _loss_head(h, gain, target, tm=1024):
    S, K = h.shape

    def body(h_ref, g_ref, t_ref, dh_ref, loss_ref, dg_ref):
        v = h_ref[...]
        r = _rstd(v)
        xh = v * r
        g = g_ref[...]
        dy = (xh * g - t_ref[...]) * (1.0 / K)
        dyg = dy * g
        dh_ref[...] = r * (dyg - xh * jnp.mean(dyg * xh, axis=-1, keepdims=True))

        @pl.when(pl.program_id(0) == 0)
        def _():
            loss_ref[...] = jnp.zeros_like(loss_ref)
            dg_ref[...] = jnp.zeros_like(dg_ref)

        part = jnp.sum(jnp.sum(dy * dy, axis=-1, keepdims=True), axis=0, keepdims=True) * (0.5 * K)
        lane = lax.broadcasted_iota(jnp.int32, loss_ref.shape, 1)
        loss_ref[...] += jnp.where(lane == 0, part, 0.0)
        dg_ref[...] += jnp.sum(dy * xh, axis=0, keepdims=True)

    row = pl.BlockSpec((tm, K), lambda i: (i, 0))
    vec = pl.BlockSpec((1, K), lambda i: (0, 0))
    return pl.pallas_call(
        body, name="loss_head", grid=(S // tm,),
        in_specs=[row, vec, row],
        out_specs=[row, pl.BlockSpec((1, HEAD), lambda i: (0, 0)), vec],
        out_shape=[jax.ShapeDtypeStruct((S, K), F32), jax.ShapeDtypeStruct((1, HEAD), F32),
                   jax.ShapeDtypeStruct((1, K), F32)],
        compiler_params=_params(("arbitrary",)))(h, gain, target)


def _lower_bound(lg_ref):
    l0, l1, l2 = lg_ref[0:1, :], lg_ref[1:2, :], lg_ref[2:3, :]
    mx = jnp.maximum(jnp.maximum(l0, l1), l2)
    e0, e1, e2 = jnp.exp(l0 - mx), jnp.exp(l1 - mx), jnp.exp(l2 - mx)
    return e0 / (e0 + e1 + e2)


def _chunks(v, ncb):
    C = HGRN_CHUNK
    return [v[c * C:(c + 1) * C] for c in range(ncb)]


def _rows(parts):
    return jnp.concatenate(parts, axis=0)


def _block_gates(qz, fz, lb, ncb):
    C = HGRN_CHUNK
    row = lax.broadcasted_iota(jnp.int32, (C, C), 0)
    col = lax.broadcasted_iota(jnp.int32, (C, C), 1)
    tri = (col <= row).astype(F32)
    first_half = lax.broadcasted_iota(jnp.int32, (C, HEAD), 0) < C // 2
    sig = _sigmoid(fz)
    fg = lb + (1.0 - lb) * sig
    key = 1.0 - fg
    lg = jnp.log(fg)
    lgs = _chunks(lg, ncb)
    b = _rows([_dot_exact(tri, v) for v in lgs])
    r_c = [jnp.sum(jnp.where(first_half, v, 0.0), axis=0, keepdims=True) for v in lgs]
    bl_c = [jnp.sum(v, axis=0, keepdims=True) for v in lgs]
    r = _rows([jnp.broadcast_to(v, (C, HEAD)) for v in r_c])
    e_br, e_rb = jnp.exp(b - r), jnp.exp(r - b)
    e_b = e_br * _rows([jnp.broadcast_to(jnp.exp(v), (C, HEAD)) for v in r_c])
    e_lb = e_rb * _rows([jnp.broadcast_to(jnp.exp(e - v), (C, HEAD)) for e, v in zip(bl_c, r_c)])
    sq = _sigmoid(qz)
    qy = qz * sq
    return sig, fg, key, (e_br, e_rb, e_b, e_lb), bl_c, sq, qy


def _hgrn_fwd(proj, logits, gain, tb=1024, rider=None):
    S = proj.shape[0]
    H, C = HGRN_HEADS, HGRN_CHUNK
    ncb = tb // C

    def one_head(q_ref, f_ref, i_ref, g_ref, lg_ref, gn_ref, o_ref, og_ref, st_ref, state):
        @pl.when(pl.program_id(1) == 0)
        def _():
            state[...] = jnp.zeros_like(state)

        lb = _lower_bound(lg_ref)
        causal = lax.broadcasted_iota(jnp.int32, (C, C), 1) <= lax.broadcasted_iota(jnp.int32, (C, C), 0)
        qz, fz, gz = q_ref[...], f_ref[...], g_ref[...]
        _, _, key, (e_br, e_rb, e_b, e_lb), bl_c, _, qy = _block_gates(qz, fz, lb, ncb)
        qs = _chunks((qy * e_br).astype(BF16), ncb)
        ks = _chunks((key * e_rb).astype(BF16), ncb)
        qb = _chunks((qy * e_b).astype(BF16), ncb)
        ke = _chunks((key * e_lb).astype(BF16), ncb)
        vb = _chunks(i_ref[...].astype(BF16), ncb)
        a = [jnp.where(causal, _dot_nt(qs[c], ks[c]), 0.0).astype(BF16) for c in range(ncb)]
        upd = [_dot_tn(vb[c], ke[c]) for c in range(ncb)]
        o_intra = [_dot(a[c], vb[c]) for c in range(ncb)]
        st = state[...]
        e_l = [jnp.exp(v) for v in bl_c]
        sts = []
        for c in range(ncb):
            sts.append(st)
            st = st * e_l[c] + upd[c]
        state[...] = st
        for c in range(ncb):
            st_ref[c] = sts[c]
        o = _rows([_dot_nt(qb[c], sts[c].astype(BF16)) + o_intra[c] for c in range(ncb)])
        o_ref[...] = o
        og_ref[...] = ((o * _rstd(o) * gn_ref[...]) * (gz * _sigmoid(gz))).astype(BF16)

    def body(q_ref, f_ref, i_ref, g_ref, lg_ref, gn_ref, o_ref, og_ref, st_ref, state):
        for hs in range(HP):
            cols = slice(hs * HEAD, (hs + 1) * HEAD)
            one_head(q_ref.at[:, cols], f_ref.at[:, cols], i_ref.at[:, cols], g_ref.at[:, cols], lg_ref.at[:, cols],
                     gn_ref, o_ref.at[:, cols], og_ref.at[:, cols], st_ref.at[hs], state.at[hs])

    HP, wide = HGRN_HEADS_PER_STEP, HGRN_HEADS_PER_STEP * HEAD
    hg = H // HP

    def part(p):
        return pl.BlockSpec((tb, wide), functools.partial(lambda h, i, p: (i, p * hg + h), p=p))

    nb = S // tb
    r_ops, r_in, r_out, r_shape, r_scr = _rider_args(rider)
    res = pl.pallas_call(
        _ride(rider, body, 6, 3, functools.partial(_grid_corner, 0, 0), functools.partial(_grid_corner, hg - 1, nb - 1)),
        name="hgrn_fwd", grid=(hg, nb),
        in_specs=[part(0), part(1), part(2), part(3),
                  pl.BlockSpec((3, wide), lambda h, i: (0, h)),
                  pl.BlockSpec((1, HEAD), lambda h, i: (0, 0))] + r_in,
        out_specs=[pl.BlockSpec((tb, wide), lambda h, i: (i, h)),
                   pl.BlockSpec((tb, wide), lambda h, i: (i, h)),
                   pl.BlockSpec((HP, ncb, HEAD, HEAD), lambda h, i: (h, i, 0, 0))] + r_out,
        out_shape=[jax.ShapeDtypeStruct((S, H * HEAD), F32),
                   jax.ShapeDtypeStruct((S, H * HEAD), BF16),
                   jax.ShapeDtypeStruct((H, S // C, HEAD, HEAD), F32)] + r_shape,
        scratch_shapes=[pltpu.VMEM((HP, HEAD, HEAD), F32)] + r_scr,
        compiler_params=_params(("arbitrary", "arbitrary")))(proj, proj, proj, proj, logits, gain, *r_ops)
    return res[:3], res[3:]


def _hgrn_bwd(proj, logits, gain, o, states, dog, tb=1024, rider=None):
    S = proj.shape[0]
    H, C = HGRN_HEADS, HGRN_CHUNK
    ncb = tb // C
    nb = S // tb

    def one_head(q_ref, f_ref, i_ref, g_ref, lg_ref, gn_ref, o_ref, st_ref, dog_ref,
                 dp_ref, dlb_ref, dgn_ref, dstate, dst_scr):
        @pl.when(pl.program_id(1) == 0)
        def _():
            dstate[...] = jnp.zeros_like(dstate)
            dlb_ref[...] = jnp.zeros_like(dlb_ref)
            dgn_ref[...] = jnp.zeros_like(dgn_ref)

        lb = _lower_bound(lg_ref)
        oml = 1.0 - lb
        gn = gn_ref[...]
        row = lax.broadcasted_iota(jnp.int32, (C, C), 0)
        col = lax.broadcasted_iota(jnp.int32, (C, C), 1)
        causal = col <= row
        tri_up = (col >= row).astype(F32)
        qz, fz, gz = q_ref[...], f_ref[...], g_ref[...]
        sig, fg, key, (e_br, e_rb, e_b, e_lb), bl_c, sq, qy = _block_gates(qz, fz, lb, ncb)
        qs_v, ks_v = (qy * e_br).astype(BF16), (key * e_rb).astype(BF16)
        qb_v, ke_v = (qy * e_b).astype(BF16), (key * e_lb).astype(BF16)
        qs, ks, qb, ke = _chunks(qs_v, ncb), _chunks(ks_v, ncb), _chunks(qb_v, ncb), _chunks(ke_v, ncb)
        vb = _chunks(i_ref[...].astype(BF16), ncb)
        ov = o_ref[...]
        rs = _rstd(ov)
        xh = ov * rs
        sg = _sigmoid(gz)
        dog_v = dog_ref[...]
        dgz = dog_v * (xh * gn) * (sg * (1.0 + gz * (1.0 - sg)))
        don = dog_v * (gz * sg)
        dgn_ref[...] += jnp.sum(don * xh, axis=0, keepdims=True)
        dyg = don * gn
        do = rs * (dyg - xh * jnp.mean(dyg * xh, axis=-1, keepdims=True))
        dob = _chunks(do.astype(BF16), ncb)
        CH = range(ncb)
        a = [jnp.where(causal, _dot_nt(qs[c], ks[c]), 0.0).astype(BF16) for c in CH]
        da = [jnp.where(causal, _dot_nt(dob[c], vb[c]), 0.0).astype(BF16) for c in CH]
        wst = [_dot_tn(dob[c], qb[c]) for c in CH]
        dv_in = [_dot_tn(a[c], dob[c]) for c in CH]
        dqs = [_dot(da[c], ks[c]) for c in CH]
        dks = [_dot_tn(da[c], qs[c]) for c in CH]
        e_l = [jnp.exp(v) for v in bl_c]
        dst = dstate[...]
        for c in reversed(range(ncb)):
            dst_scr[c] = dst
            dst = wst[c] + dst * e_l[c]
        dstate[...] = dst
        dst1b = [dst_scr[c].astype(BF16) for c in CH]
        dqb = [_dot(dob[c], st_ref[c].astype(BF16)) for c in CH]
        dke = [_dot(vb[c], dst1b[c]) for c in CH]
        dv = [dv_in[c] + _dot_nt(ke[c], dst1b[c]) for c in CH]
        dbl_st = [jnp.sum(dst_scr[c] * st_ref[c], axis=0, keepdims=True) * e_l[c] for c in CH]
        dqs, dks, dqb, dke, dv = _rows(dqs), _rows(dks), _rows(dqb), _rows(dke), _rows(dv)
        dke_ke = dke * ke_v.astype(F32)
        db = dqs * qs_v.astype(F32) - dks * ks_v.astype(F32) + dqb * qb_v.astype(F32) - dke_ke
        dlg = []
        for c, (db_c, kk_c) in enumerate(zip(_chunks(db, ncb), _chunks(dke_ke, ncb))):
            dbl = jnp.sum(kk_c, axis=0, keepdims=True) + dbl_st[c]
            dlg.append(_dot_exact(tri_up, db_c) + dbl)
        dlg = _rows(dlg)
        dkey = dks * e_rb + dke * e_lb
        dqy = dqs * e_br + dqb * e_b
        dfg = dlg / fg - dkey
        dlb_ref[...] += jnp.sum(dfg * (1.0 - sig), axis=0, keepdims=True)
        dp_ref[0] = (dqy * (sq * (1.0 + qz * (1.0 - sq)))).astype(BF16)
        dp_ref[1] = (dfg * oml * sig * (1.0 - sig)).astype(BF16)
        dp_ref[2] = dv.astype(BF16)
        dp_ref[3] = dgz.astype(BF16)

    def body(q_ref, f_ref, i_ref, g_ref, lg_ref, gn_ref, o_ref, st_ref, dog_ref,
             dp_ref, dlb_ref, dgn_ref, dstate, dst_scr):
        for hs in range(HP):
            cols = slice(hs * HEAD, (hs + 1) * HEAD)
            one_head(q_ref.at[:, cols], f_ref.at[:, cols], i_ref.at[:, cols], g_ref.at[:, cols], lg_ref.at[:, cols],
                     gn_ref, o_ref.at[:, cols], st_ref.at[hs], dog_ref.at[:, cols],
                     dp_ref.at[:, :, cols], dlb_ref.at[hs], dgn_ref.at[hs], dstate.at[hs], dst_scr)

    HP, wide = HGRN_HEADS_PER_STEP, HGRN_HEADS_PER_STEP * HEAD
    hg = H // HP

    def part(p):
        return pl.BlockSpec((tb, wide), functools.partial(lambda h, i, p: (nb - 1 - i, p * hg + h), p=p))

    blk = pl.BlockSpec((tb, wide), lambda h, i: (nb - 1 - i, h))
    acc = pl.BlockSpec((HP, 1, HEAD), lambda h, i: (h, 0, 0))
    r_ops, r_in, r_out, r_shape, r_scr = _rider_args(rider)
    res = pl.pallas_call(
        _ride(rider, body, 9, 3, functools.partial(_grid_corner, 0, 0), functools.partial(_grid_corner, hg - 1, nb - 1)),
        name="hgrn_bwd", grid=(hg, nb),
        in_specs=[part(0), part(1), part(2), part(3),
                  pl.BlockSpec((3, wide), lambda h, i: (0, h)),
                  pl.BlockSpec((1, HEAD), lambda h, i: (0, 0)),
                  blk,
                  pl.BlockSpec((HP, ncb, HEAD, HEAD), lambda h, i: (h, nb - 1 - i, 0, 0)),
                  blk] + r_in,
        out_specs=[pl.BlockSpec((4, tb, wide), lambda h, i: (0, nb - 1 - i, h)), acc, acc] + r_out,
        out_shape=[jax.ShapeDtypeStruct((4, S, H * HEAD), BF16),
                   jax.ShapeDtypeStruct((H, 1, HEAD), F32),
                   jax.ShapeDtypeStruct((H, 1, HEAD), F32)] + r_shape,
        scratch_shapes=[pltpu.VMEM((HP, HEAD, HEAD), F32), pltpu.VMEM((ncb, HEAD, HEAD), F32)] + r_scr,
        compiler_params=_params(("arbitrary", "arbitrary")))(
            proj, proj, proj, proj, logits, gain, o, states, dog, *r_ops)
    return res[:3], res[3:]


def _rope(v, cos, sin):
    return v * cos + pltpu.roll(v, HEAD // 2, 1) * sin


def _lane_pick(tile, hh):
    lane = lax.broadcasted_iota(jnp.int32, tile.shape, 1)
    return jnp.sum(jnp.where(lane == hh, tile, 0.0), axis=-1, keepdims=True)


def _lane_place(cols):
    rows = cols[0].shape[0]
    lane = lax.broadcasted_iota(jnp.int32, (rows, HEAD), 1)
    tile = jnp.zeros((rows, HEAD), F32)
    for hh, v in enumerate(cols):
        tile = jnp.where(lane == hh, v, tile)
    return tile


def _band_masks():
    qi = lax.broadcasted_iota(jnp.int32, (ATTN_SPAN, ATTN_SPAN), 0)
    kj = lax.broadcasted_iota(jnp.int32, (ATTN_SPAN, ATTN_SPAN), 1)
    return kj <= qi, kj >= qi


ATTN_TILE_BLOCKS = 8


def _attn_fwd(a):
    d, L, _ = a.shape
    B, W = min(ATTN_TILE_BLOCKS, a.shape[1] // ATTN_SPAN), ATTN_SPAN
    T = B * W
    assert L % T == 0
    steps = L // T
    scale = HEAD ** -0.5

    def body(q_ref, kc_ref, kp_ref, vc_ref, vp_ref, o_ref, lse_ref):
        n = pl.program_id(1)
        mask_c, mask_p0 = _band_masks()
        first = jnp.logical_and(mask_p0, n > 0)
        units = [(b, hh) for b in range(B) for hh in range(HEADS_PER_GROUP)]
        rows = [slice(b * W, (b + 1) * W) for b in range(B)]
        cols = [slice(hh * HEAD, (hh + 1) * HEAD) for hh in range(HEADS_PER_GROUP)]

        def prev_keys(ref, tile, b, hh):
            return ref[:, cols[hh]] if b == 0 else tile[rows[b - 1], cols[hh]]

        s_c = [jnp.where(mask_c, _dot_nt(q_ref[rows[b], cols[hh]], kc_ref[rows[b], cols[hh]]) * scale, NEG) for b, hh in units]
        s_p = [jnp.where(first if b == 0 else mask_p0,
                         _dot_nt(q_ref[rows[b], cols[hh]], prev_keys(kp_ref, kc_ref, b, hh)) * scale, NEG) for b, hh in units]
        m = [jnp.maximum(jnp.max(x, axis=-1, keepdims=True), jnp.max(y, axis=-1, keepdims=True)) for x, y in zip(s_c, s_p)]
        p_c = [jnp.exp(x - mm) for x, mm in zip(s_c, m)]
        p_p = [jnp.exp(y - mm) for y, mm in zip(s_p, m)]
        l = [jnp.sum(x, axis=-1, keepdims=True) + jnp.sum(y, axis=-1, keepdims=True) for x, y in zip(p_c, p_p)]
        acc = [_dot(p_c[i].astype(BF16), vc_ref[rows[b], cols[hh]]) + _dot(p_p[i].astype(BF16), prev_keys(vp_ref, vc_ref, b, hh))
               for i, (b, hh) in enumerate(units)]
        for i, (b, hh) in enumerate(units):
            o_ref[rows[b], cols[hh]] = (acc[i] / l[i]).astype(BF16)
        for b in range(B):
            lse_ref[rows[b], :] = _lane_place([m[i] + jnp.log(l[i]) for i, (bb, _) in enumerate(units) if bb == b])

    def cur(part):
        return pl.BlockSpec((None, T, GROUP_W), functools.partial(lambda r, n, p: (r, n, p), p=part))

    def prev(part):
        return pl.BlockSpec((None, W, GROUP_W), functools.partial(lambda r, n, p: (r, jnp.maximum(n * B - 1, 0), p), p=part))

    return pl.pallas_call(
        body, name=f"attn_fwd_d{d}", grid=(d, steps),
        in_specs=[cur(0), cur(1), prev(1), cur(2), prev(2)],
        out_specs=[pl.BlockSpec((None, T, GROUP_W), lambda r, n: (r, n, 0)), pl.BlockSpec((None, T, HEAD), lambda r, n: (r, n, 0))],
        out_shape=[jax.ShapeDtypeStruct((d, L, GROUP_W), BF16), jax.ShapeDtypeStruct((d, L, HEAD), F32)],
        compiler_params=_params(("parallel", "arbitrary")))(a, a, a, a, a)


def _attn_bwd(a, do, lse, dd):
    d, L, _ = a.shape
    B, W = min(ATTN_TILE_BLOCKS, a.shape[1] // ATTN_SPAN), ATTN_SPAN
    T = B * W
    assert L % T == 0
    steps = L // T
    scale = HEAD ** -0.5

    def body(qc_ref, qn_ref, kp_ref, kc_ref, vp_ref, vc_ref, doc_ref, don_ref, lc_ref, ln_ref, ddc_ref, ddn_ref, da_ref):
        n = pl.program_id(1)
        mask_c, mask_p0 = _band_masks()
        first = jnp.logical_and(mask_p0, n > 0)
        last = jnp.logical_and(mask_p0, n < steps - 1)
        H4 = range(HEADS_PER_GROUP)
        units = [(b, hh) for b in range(B) for hh in H4]
        rows = [slice(b * W, (b + 1) * W) for b in range(B)]
        cols = [slice(hh * HEAD, (hh + 1) * HEAD) for hh in H4]
        q = {u: qc_ref[rows[u[0]], cols[u[1]]] for u in units}
        k = {u: kc_ref[rows[u[0]], cols[u[1]]] for u in units}
        v = {u: vc_ref[rows[u[0]], cols[u[1]]] for u in units}
        g_o = {u: doc_ref[rows[u[0]], cols[u[1]]] for u in units}
        kb = {(b, hh): kp_ref[:, cols[hh]] if b == 0 else k[(b - 1, hh)] for b, hh in units}
        vb = {(b, hh): vp_ref[:, cols[hh]] if b == 0 else v[(b - 1, hh)] for b, hh in units}
        lse_t = {(b, hh): _lane_pick(lc_ref[rows[b], :], hh) for b, hh in units}
        dd_t = {(b, hh): _lane_pick(ddc_ref[rows[b], :], hh) for b, hh in units}
        p_c = {u: jnp.where(mask_c, jnp.exp(_dot_nt(q[u], k[u]) * scale - lse_t[u]), 0.0) for u in units}
        p_p = {u: jnp.where(first if u[0] == 0 else mask_p0, jnp.exp(_dot_nt(q[u], kb[u]) * scale - lse_t[u]), 0.0) for u in units}
        ds_c = {u: (p_c[u] * (_dot_nt(g_o[u], v[u]) + dd_t[u])).astype(BF16) for u in units}
        ds_p = {u: (p_p[u] * (_dot_nt(g_o[u], vb[u]) + dd_t[u])).astype(BF16) for u in units}
        qn = [qn_ref[:, c] for c in cols]
        g_n = [don_ref[:, c] for c in cols]
        p_n = [jnp.where(last, jnp.exp(_dot_nt(qn[hh], k[(B - 1, hh)]) * scale - _lane_pick(ln_ref[...], hh)), 0.0) for hh in H4]
        ds_n = [(p_n[hh] * (_dot_nt(g_n[hh], v[(B - 1, hh)]) + _lane_pick(ddn_ref[...], hh))).astype(BF16) for hh in H4]
        dq = {u: (_dot(ds_c[u], k[u]) + _dot(ds_p[u], kb[u])) * scale for u in units}
        dk, dv = {}, {}
        for b, hh in units:
            if b < B - 1:
                nxt = (b + 1, hh)
                dk[(b, hh)] = (_dot_tn(ds_c[(b, hh)], q[(b, hh)]) + _dot_tn(ds_p[nxt], q[nxt])) * scale
                dv[(b, hh)] = _dot_tn(p_c[(b, hh)].astype(BF16), g_o[(b, hh)]) + _dot_tn(p_p[nxt].astype(BF16), g_o[nxt])
            else:
                dk[(b, hh)] = (_dot_tn(ds_c[(b, hh)], q[(b, hh)]) + _dot_tn(ds_n[hh], qn[hh])) * scale
                dv[(b, hh)] = _dot_tn(p_c[(b, hh)].astype(BF16), g_o[(b, hh)]) + _dot_tn(p_n[hh].astype(BF16), g_n[hh])
        for b, hh in units:
            da_ref[rows[b], cols[hh]] = dq[(b, hh)].astype(BF16)
            da_ref[rows[b], GROUP_W + hh * HEAD:GROUP_W + (hh + 1) * HEAD] = dk[(b, hh)].astype(BF16)
            da_ref[rows[b], 2 * GROUP_W + hh * HEAD:2 * GROUP_W + (hh + 1) * HEAD] = dv[(b, hh)].astype(BF16)

    nb = L // W

    def cur(width, part):
        return pl.BlockSpec((None, T, width), functools.partial(lambda r, n, p: (r, n, p), p=part))

    def prev(width, part):
        return pl.BlockSpec((None, W, width), functools.partial(lambda r, n, p: (r, jnp.maximum(n * B - 1, 0), p), p=part))

    def nxt(width, part):
        return pl.BlockSpec((None, W, width), functools.partial(lambda r, n, p: (r, jnp.minimum(n * B + B, nb - 1), p), p=part))

    g = GROUP_W
    return pl.pallas_call(
        body, name=f"attn_bwd_d{d}", grid=(d, steps),
        in_specs=[cur(g, 0), nxt(g, 0), prev(g, 1), cur(g, 1), prev(g, 2), cur(g, 2),
                  cur(g, 0), nxt(g, 0), cur(HEAD, 0), nxt(HEAD, 0), cur(HEAD, 0), nxt(HEAD, 0)],
        out_specs=pl.BlockSpec((None, T, 3 * g), lambda r, n: (r, n, 0)),
        out_shape=jax.ShapeDtypeStruct((d, L, 3 * g), BF16),
        compiler_params=_params(("parallel", "arbitrary")))(
            a, a, a, a, a, a, do, do, lse, lse, dd, dd)


def _softmax3(ls):
    mx = jnp.maximum(jnp.maximum(ls[0], ls[1]), ls[2])
    es = [jnp.exp(v - mx) for v in ls]
    tot = es[0] + es[1] + es[2]
    return [e / tot for e in es]


HEAD_COLS = [slice(hh * HEAD, (hh + 1) * HEAD) for hh in range(HEADS_PER_GROUP)]


def _group_spec(d, tm):
    return pl.BlockSpec((d, tm // d, GROUP_W), lambda i: (0, i, 0))


def _gather_heads(ref, scr, d, tm):
    if d == 1:
        return [ref[0, :, cols].astype(F32) for cols in HEAD_COLS]
    for hh, cols in enumerate(HEAD_COLS):
        for r in range(d):
            scr.at[hh][pl.ds(r, tm // d, stride=d), :] = ref[r, :, cols].astype(F32)
    return [scr[hh] for hh in range(HEADS_PER_GROUP)]


def _tile_spec(d, tm):
    return pl.BlockSpec((d, tm // d, HEAD), lambda i: (0, i, 0))


def _gather_tile(ref, scr, d, tm):
    if d == 1:
        return ref[0]
    for r in range(d):
        scr[pl.ds(r, tm // d, stride=d), :] = ref[r]
    return scr[...]


def _scatter_tile(val, scr, ref, d, tm):
    if d == 1:
        ref[0] = val
        return
    scr[...] = val
    for r in range(d):
        ref[r] = scr[pl.ds(r, tm // d, stride=d), :]


def _scatter_heads(vals, scr, ref, d, tm):
    if d == 1:
        for cols, v in zip(HEAD_COLS, vals):
            ref[0, :, cols] = v.astype(ref.dtype)
        return
    for hh, v in enumerate(vals):
        scr[hh] = v
    for hh, cols in enumerate(HEAD_COLS):
        for r in range(d):
            ref[r, :, cols] = scr.at[hh][pl.ds(r, tm // d, stride=d), :].astype(ref.dtype)


def _qkv_dilated(h, gain, wg, cos, sin, d, tm=2048):
    S, K = h.shape

    def body(h_ref, g_ref, w_ref, cos_ref, sin_ref, out_ref, u_ref, y_scr):
        p = pl.program_id(1)

        @pl.when(p == 0)
        def _():
            v = h_ref[...]
            u_ref[...] = (v * _rstd(v) * g_ref[...]).astype(BF16)

        y = _dot(u_ref[...], w_ref[...])
        heads = [slice(hh * HEAD, (hh + 1) * HEAD) for hh in range(HEADS_PER_GROUP)]
        if d > 1:
            for hh, cols in enumerate(heads):
                y_scr[hh] = y[:, cols]

        def rows_of(hh, r):
            return y[:, heads[hh]] if d == 1 else y_scr.at[hh][pl.ds(r, tm // d, stride=d), :]

        @pl.when(p < 2)
        def _():
            for r in range(d):
                rows = slice(None) if d == 1 else pl.ds(r, tm // d, stride=d)
                cr, sr = cos_ref[rows, :], sin_ref[rows, :]
                for hh, cols in enumerate(heads):
                    out_ref[r, :, cols] = _rope(rows_of(hh, r), cr, sr).astype(BF16)

        @pl.when(p == 2)
        def _():
            for r in range(d):
                for hh, cols in enumerate(heads):
                    out_ref[r, :, cols] = rows_of(hh, r).astype(BF16)

    tab = pl.BlockSpec((tm, HEAD), lambda i, p: (i, 0))
    return pl.pallas_call(
        body, name=f"attn_qkv_d{d}", grid=(S // tm, 3),
        in_specs=[pl.BlockSpec((tm, K), lambda i, p: (i, 0)),
                  pl.BlockSpec((1, K), lambda i, p: (0, 0)),
                  pl.BlockSpec((K, GROUP_W), lambda i, p: (0, p)), tab, tab],
        out_specs=[pl.BlockSpec((d, tm // d, GROUP_W), lambda i, p: (0, i, p)), pl.BlockSpec((tm, K), lambda i, p: (i, 0))],
        out_shape=[jax.ShapeDtypeStruct((d, S // d, 3 * GROUP_W), BF16), jax.ShapeDtypeStruct((S, K), BF16)],
        scratch_shapes=[pltpu.VMEM((HEADS_PER_GROUP, tm, HEAD), F32)],
        compiler_params=_params(("parallel", "arbitrary")))(h, gain, wg, cos, sin)


def _undilate_group(da, dqkv, cos, sin, g, tm=2048):
    d, L, _ = da.shape
    S = d * L
    G = len(ATTN_GROUPS)

    def body(*refs):
        da_ref, cos_ref, sin_ref, out_ref, scr = refs[0], refs[1], refs[2], refs[-2], refs[-1]
        p = pl.program_id(1)
        heads = [slice(hh * HEAD, (hh + 1) * HEAD) for hh in range(HEADS_PER_GROUP)]
        if d > 1:
            for hh, cols in enumerate(heads):
                for r in range(d):
                    scr.at[hh][pl.ds(r, tm // d, stride=d), :] = da_ref[r, :, cols].astype(F32)

        def tokens(hh):
            return da_ref[0, :, heads[hh]].astype(F32) if d == 1 else scr[hh]

        @pl.when(p < 2)
        def _():
            cr, sr = cos_ref[...], -sin_ref[...]
            for hh, cols in enumerate(heads):
                out_ref[:, cols] = _rope(tokens(hh), cr, sr).astype(BF16)

        @pl.when(p == 2)
        def _():
            for hh, cols in enumerate(heads):
                out_ref[:, cols] = tokens(hh).astype(BF16)

    tab = pl.BlockSpec((tm, HEAD), lambda i, p: (i, 0))
    operands = (da, cos, sin) if dqkv is None else (da, cos, sin, dqkv)
    return pl.pallas_call(
        body, name=f"attn_undilate_d{d}", grid=(S // tm, 3),
        in_specs=[pl.BlockSpec((d, tm // d, GROUP_W), lambda i, p: (0, i, p)), tab, tab] + ([] if dqkv is None else [ANY]),
        out_specs=pl.BlockSpec((tm, GROUP_W), lambda i, p: (i, p * G + g)),
        out_shape=jax.ShapeDtypeStruct((S, 3 * G * GROUP_W), BF16),
        input_output_aliases={} if dqkv is None else {3: 0},
        scratch_shapes=[pltpu.VMEM((HEADS_PER_GROUP, tm, HEAD), F32)],
        compiler_params=_params(("parallel", "arbitrary")))(*operands)


def _attn_merge(os_, lses, tm=1024):
    G = len(os_)
    S = os_[0].shape[0] * os_[0].shape[1]

    def body(*refs):
        o_refs, l_refs, out_ref = refs[:G], refs[G:2 * G], refs[2 * G]
        scr = refs[2 * G + 1:]
        o = [_gather_heads(o_refs[g], scr[g], d, tm) for g, (_, d) in enumerate(ATTN_GROUPS)]
        l = [_gather_tile(l_refs[g], scr[G + g].at[0], d, tm) for g, (_, d) in enumerate(ATTN_GROUPS)]
        for hh in range(HEADS_PER_GROUP):
            al = _softmax3([_lane_pick(l[g], hh) for g in range(G)])
            for g in range(G):
                out_ref[:, g * GROUP_W + hh * HEAD:g * GROUP_W + (hh + 1) * HEAD] = (o[g][hh] * al[g]).astype(BF16)

    specs = [_group_spec(d, tm) for _, d in ATTN_GROUPS]
    return pl.pallas_call(
        body, name="attn_merge", grid=(S // tm,),
        in_specs=specs + [_tile_spec(d, tm) for _, d in ATTN_GROUPS],
        out_specs=pl.BlockSpec((tm, G * GROUP_W), lambda i: (i, 0)),
        out_shape=jax.ShapeDtypeStruct((S, G * GROUP_W), BF16),
        scratch_shapes=[pltpu.VMEM((HEADS_PER_GROUP, tm, HEAD), F32)] * (2 * G),
        compiler_params=_params(("parallel",)))(*os_, *lses)


def _attn_merge_bwd(os_, lses, doa, tm=1024):
    G = len(os_)
    S = doa.shape[0]

    def body(*refs):
        o_refs, l_refs, doa_ref = refs[:G], refs[G:2 * G], refs[2 * G]
        do_refs, dd_refs = refs[2 * G + 1:3 * G + 1], refs[3 * G + 1:4 * G + 1]
        scr = refs[4 * G + 1:]
        o = [_gather_heads(o_refs[g], scr[g], d, tm) for g, (_, d) in enumerate(ATTN_GROUPS)]
        l = [_gather_tile(l_refs[g], scr[G + g].at[0], d, tm) for g, (_, d) in enumerate(ATTN_GROUPS)]
        do = [[None] * HEADS_PER_GROUP for _ in range(G)]
        dd = [[None] * HEADS_PER_GROUP for _ in range(G)]
        for hh in range(HEADS_PER_GROUP):
            al = _softmax3([_lane_pick(l[g], hh) for g in range(G)])
            mix = None
            for g in range(G):
                dg = doa_ref[:, g * GROUP_W + hh * HEAD:g * GROUP_W + (hh + 1) * HEAD]
                do[g][hh] = dg * al[g]
                t = al[g] * jnp.sum(dg * o[g][hh], axis=-1, keepdims=True)
                mix = t if mix is None else mix + t
            for g in range(G):
                dd[g][hh] = -al[g] * mix
        for g, (_, d) in enumerate(ATTN_GROUPS):
            _scatter_heads(do[g], scr[2 * G + g], do_refs[g], d, tm)
            _scatter_tile(_lane_place(dd[g]), scr[3 * G + g].at[0], dd_refs[g], d, tm)

    specs = [_group_spec(d, tm) for _, d in ATTN_GROUPS]
    tiles = [_tile_spec(d, tm) for _, d in ATTN_GROUPS]
    do_shapes = [jax.ShapeDtypeStruct((d, S // d, GROUP_W), BF16) for _, d in ATTN_GROUPS]
    dd_shapes = [jax.ShapeDtypeStruct((d, S // d, HEAD), F32) for _, d in ATTN_GROUPS]
    return pl.pallas_call(
        body, name="attn_merge_bwd", grid=(S // tm,),
        in_specs=specs + tiles + [pl.BlockSpec((tm, G * GROUP_W), lambda i: (i, 0))],
        out_specs=specs + tiles,
        out_shape=do_shapes + dd_shapes,
        scratch_shapes=[pltpu.VMEM((HEADS_PER_GROUP, tm, HEAD), F32)] * (4 * G),
        compiler_params=_params(("parallel",)))(*os_, *lses, doa)


def _rope_tables(S):
    inv_freq = 1.0 / (ROPE_THETA ** (jnp.arange(0, HEAD, 2, dtype=F32) / HEAD))
    ang = jnp.arange(S, dtype=F32)[:, None] * inv_freq[None, :]
    cos, sin = jnp.cos(ang), jnp.sin(ang)
    return jnp.concatenate([cos, cos], axis=-1), jnp.concatenate([-sin, sin], axis=-1)


def _local_step(x, target, norm_mix, norm_ffn, lb_logits, out_gain, final_norm, comm):
    S = x.shape[0]
    nm0, nm1 = norm_mix[0:1], norm_mix[1:2]
    nf0, nf1 = norm_ffn[0:1], norm_ffn[1:2]
    w = comm.first_weights()

    proj, u0, got = _norm_mm(x, nm0, w["hin"], "hgrn_in", rider=comm.gather_rider(LATE_WEIGHTS_A))
    w.update(comm.gathered(LATE_WEIGHTS_A, got))
    (o, og, states), got = _hgrn_fwd(proj, lb_logits, out_gain, rider=comm.gather_rider(LATE_WEIGHTS_B))
    w.update(comm.gathered(LATE_WEIGHTS_B, got))
    fin_tn = w["fin0"].shape[2]
    h1 = _mm_res(x, og, w["hout"], "hgrn_out")
    z0, u1, _ = _norm_mm(h1, nf0, w["fin0"], "ffn0_in", out_dtype=BF16)
    h2, act0 = _swiglu_mm_res(h1, z0, w["fdn0"], "ffn0_down")
    cos, sin = _rope_tables(S)
    G = len(ATTN_GROUPS)
    w_groups = w["qkv"].transpose(1, 0, 2).reshape(D_MODEL, 3, G, GROUP_W)
    a_g, u2 = zip(*[_qkv_dilated(h2, nm1, w_groups[:, :, gi, :].reshape(D_MODEL, 3 * GROUP_W), cos, sin, d)
                    for gi, (_, d) in enumerate(ATTN_GROUPS)])
    o_g, lse_g = zip(*[_attn_fwd(a) for a in a_g])
    oa = _attn_merge(o_g, lse_g)
    h3 = _mm_res(h2, oa, w["aout"], "attn_out")
    z1, u3, _ = _norm_mm(h3, nf1, w["fin1"], "ffn1_in", out_dtype=BF16)
    h4, act1 = _swiglu_mm_res(h3, z1, w["fdn1"], "ffn1_down")
    dh4, loss, d_final = _loss_head(h4, final_norm, target)

    grads, small = {}, {"final_norm": d_final}

    def ffn_bwd(dh, h_in, u_in, z, act, gain, w_in, w_dn, tag, ride=None):
        dz = _mm_nt_swiglu_bwd(dh, w_dn, z, tag + "_down_dx")
        g_dn = _mm_tn(act, dh, 1, D_MODEL, D_MODEL, tag + "_down_dw")[0]
        g_in = _mm_tn(u_in, dz, N_CHIPS, fin_tn, fin_tn, tag + "_in_dw")
        rider = None if ride is None else ride(g_in, g_dn)
        dh_in, dgain, got = _mm_nt_normbwd(dz, w_in, h_in, gain, dh, tag + "_in_dx", rider=rider)
        return dh_in, dgain, g_in, g_dn, got

    dh3, d_nf1, grads["fin1"], grads["fdn1"], _ = ffn_bwd(dh4, h3, u3, z1, act1, nf1, w["fin1"], w["fdn1"], "ffn1")
    doa = _mm_nt(dh3, w["aout"][None], "attn_out_dx")
    grads["aout"] = _mm_tn(oa, dh3, 1, D_MODEL, D_MODEL, "attn_out_dw")[0]
    merged = _attn_merge_bwd(o_g, lse_g, doa)
    G = len(ATTN_GROUPS)
    das = [_attn_bwd(a_g[gi], merged[gi], lse_g[gi], merged[G + gi]) for gi in range(G)]
    dqkv = None
    for gi in range(G):
        dqkv = _undilate_group(das[gi], dqkv, cos, sin, gi)
    n_qkv = w["qkv"].shape[2]
    grads["qkv"] = _mm_tn(u2[0], dqkv, N_CHIPS, n_qkv, n_qkv, "attn_qkv_dw")
    dh2, d_nm1, _ = _mm_nt_normbwd(dqkv, w["qkv"], h2, nm1, dh3, "attn_qkv_dx")

    def ride_early(g_in, g_dn):
        return comm.pair_rider({**grads, "fin0": g_in, "fdn0": g_dn}, "early")

    dh1, d_nf0, _, _, got = ffn_bwd(dh2, h1, u1, z0, act0, nf0, w["fin0"], w["fdn0"], "ffn0", ride=ride_early)
    comm.paired("early", got)
    dog = _mm_nt(dh1, w["hout"][None], "hgrn_out_dx")
    (dproj, dlb, dgn), got = _hgrn_bwd(proj, lb_logits, out_gain, o, states, dog, rider=comm.exchange_rider("early"))
    comm.exchanged("early", got)
    late = {"hout": _mm_tn(og, dh1, 1, D_MODEL, D_MODEL, "hgrn_out_dw")[0],
            "hin": _mm_tn(u0, dproj, N_CHIPS, D_MODEL, D_MODEL, "hgrn_in_dw")}
    comm.pair_now(late, "late")
    dx, d_nm0, got = _mm_nt_normbwd(dproj, w["hin"], x, nm0, dh1, "hgrn_in_dx", rider=comm.exchange_rider("late"))
    comm.exchanged("late", got)

    small["norm_mix"] = jnp.concatenate([d_nm0, d_nm1], axis=0)
    small["norm_ffn"] = jnp.concatenate([d_nf0, d_nf1], axis=0)
    small["lb"] = dlb.reshape(1, HGRN_HEADS * HEAD)
    small["out_norm"] = dgn.reshape(HGRN_HEADS, HEAD)
    return loss, dx, small


def _place():
    x, y, c = lax.axis_index("x"), lax.axis_index("y"), lax.axis_index("c")
    others = [(1 - x, y), (x, 1 - y), (1 - x, 1 - y)]
    return x, y, c, others


ANY = pl.BlockSpec(memory_space=pl.ANY)


class _GatherRider:
    def __init__(self, shards):
        self.operands = list(shards)
        n = self.n = len(shards)
        self.out_shape = [jax.ShapeDtypeStruct((N_CHIPS,) + s.shape, s.dtype) for s in shards]
        self.scratch = [pltpu.SemaphoreType.DMA((3 * n,)), pltpu.SemaphoreType.DMA((3 * n,)),
                        pltpu.SemaphoreType.DMA((3 * n,)), pltpu.SemaphoreType.DMA((3 * n,)),
                        pltpu.SemaphoreType.DMA((n,)), pltpu.SemaphoreType.DMA((n,))]

    def _copies(self, ins, outs, sems):
        ici_send, ici_recv, _, _, own_send, own_recv = sems
        x, y, c, others = _place()
        me = 2 * x + y
        own = [pltpu.make_async_remote_copy(
            src_ref=ins[a], dst_ref=outs[a].at[me], send_sem=own_send.at[a], recv_sem=own_recv.at[a],
            device_id=(x, y, 1 - c), device_id_type=MESH) for a in range(self.n)]
        sends = [pltpu.make_async_remote_copy(
            src_ref=ins[a].at[c], dst_ref=outs[a].at[me, c], send_sem=ici_send.at[a * 3 + k], recv_sem=ici_recv.at[a * 3 + k],
            device_id=(ox, oy, c), device_id_type=MESH) for a in range(self.n) for k, (ox, oy) in enumerate(others)]
        return own, sends

    def start(self, ins, outs, sems):
        own, sends = self._copies(ins, outs, sems)
        for cp in own + sends:
            cp.start()

    def finish(self, ins, outs, sems):
        ici_send, ici_recv, d2d_send, d2d_recv, _, _ = sems
        x, y, c, others = _place()
        sibling = (x, y, 1 - c)
        own, sends = self._copies(ins, outs, sems)
        passes = []
        for a in range(self.n):
            for k, (ox, oy) in enumerate(others):
                s = a * 3 + k
                got = outs[a].at[2 * ox + oy, c]
                pltpu.make_async_remote_copy(
                    src_ref=got, dst_ref=got, send_sem=ici_send.at[s], recv_sem=ici_recv.at[s],
                    device_id=(ox, oy, c), device_id_type=MESH).wait_recv()
                fwd = pltpu.make_async_remote_copy(
                    src_ref=got, dst_ref=got, send_sem=d2d_send.at[s], recv_sem=d2d_recv.at[s],
                    device_id=sibling, device_id_type=MESH)
                fwd.start()
                passes.append(fwd)
        for a in range(self.n):
            for k, (ox, oy) in enumerate(others):
                s = a * 3 + k
                theirs = outs[a].at[2 * ox + oy, 1 - c]
                pltpu.make_async_remote_copy(
                    src_ref=theirs, dst_ref=theirs, send_sem=d2d_send.at[s], recv_sem=d2d_recv.at[s],
                    device_id=sibling, device_id_type=MESH).wait_recv()
        for cp in own:
            cp.wait()
        for cp in sends + passes:
            cp.wait_send()


class _PairRider:
    def __init__(self, grads):
        self.operands = list(grads)
        n = self.n = len(grads)
        self.out_shape = [jax.ShapeDtypeStruct((N_CHIPS,) + g.shape[2:], F32) for g in grads]
        self.scratch = [pltpu.SemaphoreType.DMA((N_CHIPS * n,)), pltpu.SemaphoreType.DMA((N_CHIPS * n,))]

    def _copies(self, ins, outs, sems):
        send_sem, recv_sem = sems
        x, y, c, _ = _place()
        return [pltpu.make_async_remote_copy(
            src_ref=ins[a].at[j, 1 - c], dst_ref=outs[a].at[j], send_sem=send_sem.at[a * N_CHIPS + j],
            recv_sem=recv_sem.at[a * N_CHIPS + j], device_id=(x, y, 1 - c), device_id_type=MESH)
            for a in range(self.n) for j in range(N_CHIPS)]

    def start(self, ins, outs, sems):
        for cp in self._copies(ins, outs, sems):
            cp.start()

    def finish(self, ins, outs, sems):
        for cp in self._copies(ins, outs, sems):
            cp.wait()


class _ExchangeRider:
    def __init__(self, parts):
        self.operands = list(parts)
        n = self.n = len(parts)
        self.out_shape = [jax.ShapeDtypeStruct(p.shape, p.dtype) for p in parts]
        self.scratch = [pltpu.SemaphoreType.DMA((3 * n,)), pltpu.SemaphoreType.DMA((3 * n,))]

    def _copies(self, ins, outs, sems):
        send_sem, recv_sem = sems
        x, y, c, others = _place()
        me = 2 * x + y
        return [pltpu.make_async_remote_copy(
            src_ref=ins[a].at[2 * ox + oy], dst_ref=outs[a].at[me], send_sem=send_sem.at[a * 3 + k],
            recv_sem=recv_sem.at[a * 3 + k], device_id=(ox, oy, c), device_id_type=MESH)
            for a in range(self.n) for k, (ox, oy) in enumerate(others)]

    def start(self, ins, outs, sems):
        for cp in self._copies(ins, outs, sems):
            cp.start()

    def finish(self, ins, outs, sems):
        send_sem, recv_sem = sems
        x, y, c, others = _place()
        for a in range(self.n):
            for k, (ox, oy) in enumerate(others):
                s = a * 3 + k
                got = outs[a].at[2 * ox + oy]
                pltpu.make_async_remote_copy(
                    src_ref=got, dst_ref=got, send_sem=send_sem.at[s], recv_sem=recv_sem.at[s],
                    device_id=(ox, oy, c), device_id_type=MESH).wait_recv()
        for cp in self._copies(ins, outs, sems):
            cp.wait_send()


def _run_rider(rider, name):
    n = rider.n

    def body(*refs):
        ins, outs, sems = refs[:n], refs[n:2 * n], refs[2 * n:]
        rider.start(ins, outs, sems)
        rider.finish(ins, outs, sems)

    return pl.pallas_call(
        body, name=name, in_specs=[ANY] * n, out_specs=[ANY] * n,
        out_shape=rider.out_shape, scratch_shapes=rider.scratch)(*rider.operands)


def _ride(rider, body, n_in, n_out, first, last):
    if rider is None:
        return body
    n = rider.n

    def wrapped(*refs):
        host_in, r_in = refs[:n_in], refs[n_in:n_in + n]
        host_out = refs[n_in + n:n_in + n + n_out]
        r_out = refs[n_in + n + n_out:n_in + 2 * n + n_out]
        rest = refs[n_in + 2 * n + n_out:]
        host_scr, sems = rest[:len(rest) - len(rider.scratch)], rest[len(rest) - len(rider.scratch):]

        @pl.when(first())
        def _():
            rider.start(r_in, r_out, sems)

        body(*host_in, *host_out, *host_scr)

        @pl.when(last())
        def _():
            rider.finish(r_in, r_out, sems)

    return wrapped


def _rider_args(rider):
    if rider is None:
        return [], [], [], [], []
    return rider.operands, [ANY] * rider.n, [ANY] * rider.n, rider.out_shape, rider.scratch


def _pair_sum(g, got, c_idx):
    _, _, r, cw = g.shape
    tr = _row_tile(r, cw)

    def body(c_ref, g_ref, got_ref, pb_ref):
        pb_ref[...] = (g_ref[...] + got_ref[...]).astype(BF16)

    blk = pl.BlockSpec((None, tr, cw), lambda j, i, c_ref: (j, i, 0))
    return pl.pallas_call(
        body, name="grad_pair_sum",
        grid_spec=pltpu.PrefetchScalarGridSpec(
            num_scalar_prefetch=1, grid=(N_CHIPS, r // tr),
            in_specs=[pl.BlockSpec((None, None, tr, cw), lambda j, i, c_ref: (j, c_ref[0], i, 0)), blk],
            out_specs=blk),
        out_shape=jax.ShapeDtypeStruct((N_CHIPS, r, cw), BF16),
        compiler_params=_params(("parallel", "parallel")))(c_idx, g, got)


def _chip_sum(g, sib, got, place):
    _, _, r, cw = g.shape
    tr = _row_tile(r, cw)

    def body(place_ref, g_ref, sib_ref, got_ref, t_ref):
        me = place_ref[0]
        own = g_ref[...] + sib_ref[...]
        acc = None
        for s in range(N_CHIPS):
            term = jnp.where(me == s, own, got_ref[s].astype(F32))
            acc = term if acc is None else acc + term
        t_ref[...] = acc

    return pl.pallas_call(
        body, name="grad_chip_sum",
        grid_spec=pltpu.PrefetchScalarGridSpec(
            num_scalar_prefetch=1, grid=(r // tr,),
            in_specs=[pl.BlockSpec((None, None, tr, cw), lambda i, pr: (pr[0], pr[1], i, 0)),
                      pl.BlockSpec((None, tr, cw), lambda i, pr: (pr[0], i, 0)),
                      pl.BlockSpec((N_CHIPS, tr, cw), lambda i, pr: (0, i, 0))],
            out_specs=pl.BlockSpec((tr, cw), lambda i, pr: (i, 0))),
        out_shape=jax.ShapeDtypeStruct((r, cw), F32),
        compiler_params=_params(("parallel",)))(place, g, sib, got)


def _pair_share(halves):
    n = len(halves)

    def body(*refs):
        ins, outs = refs[:n], refs[n:2 * n]
        send_sem, recv_sem = refs[2 * n:]
        x, y, c, _ = _place()
        cps = [pltpu.make_async_remote_copy(
            src_ref=ins[a], dst_ref=outs[a], send_sem=send_sem.at[a], recv_sem=recv_sem.at[a],
            device_id=(x, y, 1 - c), device_id_type=MESH) for a in range(n)]
        for cp in cps:
            cp.start()
        for cp in cps:
            cp.wait()

    return pl.pallas_call(
        body, name="grad_pair_share",
        in_specs=[ANY] * n, out_specs=[ANY] * n,
        out_shape=[jax.ShapeDtypeStruct(h.shape, F32) for h in halves],
        scratch_shapes=[pltpu.SemaphoreType.DMA((n,)), pltpu.SemaphoreType.DMA((n,))],
        )(*halves)


def _small_allreduce(pack):
    m_per, ncol = pack.shape
    n_dev = 8

    def body(x_ref, sum_ref, all_ref, send_sems, recv_sems, local_sem):
        x, y, c, others = _place()
        me, sibling = (x, y, c), (x, y, 1 - c)

        def rows(px, py, pc):
            return all_ref.at[pl.ds((4 * px + 2 * py + pc) * m_per, m_per), :]

        def copy(k, block, to, src=None):
            return pltpu.make_async_remote_copy(
                src_ref=rows(*block) if src is None else src, dst_ref=rows(*block),
                send_sem=send_sems.at[k], recv_sem=recv_sems.at[k], device_id=to, device_id_type=MESH)

        mine = pltpu.make_async_copy(x_ref, rows(*me), local_sem)
        mine.start()
        first = [copy(0, me, sibling, src=x_ref)]
        first += [copy(1 + j, me, (*chip, c), src=x_ref) for j, chip in enumerate(others)]
        for cp in first:
            cp.start()
        passed = [copy(4 + j, (*chip, c), sibling) for j, chip in enumerate(others)]
        for j, chip in enumerate(others):
            copy(1 + j, (*chip, c), me).wait_recv()
            passed[j].start()
        copy(0, sibling, me).wait_recv()
        for j, chip in enumerate(others):
            copy(4 + j, (*chip, 1 - c), me).wait_recv()
        for cp in first + passed:
            cp.wait_send()
        mine.wait()
        acc = all_ref[0:m_per, :]
        for dvc in range(1, n_dev):
            acc = acc + all_ref[dvc * m_per:(dvc + 1) * m_per, :]
        sum_ref[...] = acc

    return pl.pallas_call(
        body, name="small_allreduce",
        in_specs=[pl.BlockSpec(memory_space=pltpu.VMEM)],
        out_specs=pl.BlockSpec(memory_space=pltpu.VMEM),
        out_shape=jax.ShapeDtypeStruct((m_per, ncol), F32),
        scratch_shapes=[pltpu.VMEM((n_dev * m_per, ncol), F32),
                        pltpu.SemaphoreType.DMA((7,)), pltpu.SemaphoreType.DMA((7,)), pltpu.SemaphoreType.DMA],
        )(pack)


def _adam_math(w, g, m, v):
    m = ADAM_B1 * m + (1.0 - ADAM_B1) * g
    v = ADAM_B2 * v + (1.0 - ADAM_B2) * (g * g)
    m_hat = m / (1.0 - ADAM_B1 ** ADAM_STEP)
    v_hat = v / (1.0 - ADAM_B2 ** ADAM_STEP)
    delta = -ADAM_LR * (m_hat / (jnp.sqrt(v_hat) + ADAM_EPS) + ADAM_WD * w)
    return delta, m, v


def _adamw(halves, c_idx, w, m, v, name):
    L = len(halves)
    r, C = halves[0][0].shape
    tr = _row_tile(r, C, 1024 * 1024)
    nt = r // tr

    def body(c_ref, *refs):
        g_refs, (w_ref, m_ref, v_ref), (g_ref, d_ref, nm_ref, nv_ref) = refs[:2 * L], refs[2 * L:2 * L + 3], refs[2 * L + 3:]
        own = pl.program_id(1) == c_ref[0]
        g = None
        for l in range(L):
            cand = jnp.where(own, g_refs[2 * l][...], g_refs[2 * l + 1][...])
            g = cand if g is None else jnp.where(pl.program_id(0) == l, cand, g)
        g_ref[...] = g
        d_ref[...], nm_ref[...], nv_ref[...] = _adam_math(w_ref[...], g, m_ref[...], v_ref[...])

    def half(l, mine):
        def index(ll, h, i, c_ref):
            read = (h == c_ref[0]) if mine else (h != c_ref[0])
            return jnp.where(jnp.logical_and(ll == l, read), i, 0), 0
        return pl.BlockSpec((tr, C), index)

    full = pl.BlockSpec((None, tr, C), lambda ll, h, i, c_ref: (ll, h * nt + i, 0))
    shp = jax.ShapeDtypeStruct((L, 2 * r, C), F32)
    g_specs = [half(l, mine) for l in range(L) for mine in (True, False)]
    return pl.pallas_call(
        body, name=name,
        grid_spec=pltpu.PrefetchScalarGridSpec(
            num_scalar_prefetch=1, grid=(L, 2, nt),
            in_specs=g_specs + [full] * 3, out_specs=[full] * 4),
        out_shape=[shp] * 4,
        compiler_params=_params(("arbitrary", "arbitrary", "arbitrary")))(
            c_idx, *[a for pair in halves for a in pair], w, m, v)


def _small_update(gsum, logits_pack, w, m, v):
    def body(gs_ref, lg_ref, w_ref, m_ref, v_ref, g_ref, d_ref, nm_ref, nv_ref):
        g_ref[...] = gs_ref[...]
        l0, l1, l2 = lg_ref[0:1, :], lg_ref[1:2, :], lg_ref[2:3, :]
        mx = jnp.maximum(jnp.maximum(l0, l1), l2)
        e0, e1, e2 = jnp.exp(l0 - mx), jnp.exp(l1 - mx), jnp.exp(l2 - mx)
        tot = e0 + e1 + e2
        p0, p1, p2 = e0 / tot, e1 / tot, e2 / tot
        dlb = gs_ref[4:5, :]
        g_ref[4:5, :] = dlb * p0 * (1.0 - p0)
        g_ref[5:6, :] = -dlb * p0 * p1
        g_ref[6:7, :] = -dlb * p0 * p2
        d_ref[...], nm_ref[...], nv_ref[...] = _adam_math(w_ref[...], g_ref[...], m_ref[...], v_ref[...])

    full = pl.BlockSpec(memory_space=pltpu.VMEM)
    shp = jax.ShapeDtypeStruct(gsum.shape, F32)
    return pl.pallas_call(
        body, name="small_update", in_specs=[full] * 5, out_specs=[full] * 4, out_shape=[shp] * 4)(
            gsum, logits_pack, w, m, v)


def _pack_small(norm_mix, norm_ffn, lb3, out_norm, final_norm, extra=None):
    ncol = norm_mix.shape[1]
    on = jnp.pad(out_norm.reshape(1, -1), ((0, 0), (0, ncol - out_norm.size)))
    rows = [norm_mix, norm_ffn, lb3, on, final_norm.reshape(1, ncol)]
    if extra is not None:
        rows.append(extra)
    used = sum(r.shape[0] for r in rows)
    rows.append(jnp.zeros((SMALL_ROWS - used, ncol), F32))
    return jnp.concatenate(rows, axis=0)


WEIGHT_NAMES = ("hin", "hout", "qkv", "aout", "fin0", "fin1", "fdn0", "fdn1")
FIRST_WEIGHTS = ("hin",)
LATE_WEIGHTS_A = ("hout", "fin0", "fdn0")
LATE_WEIGHTS_B = ("qkv", "aout", "fin1", "fdn1")


def _split_weights(hgrn_w_in, hgrn_w_out, attn_w_qkv, attn_w_out, ffn_w_in, ffn_w_down):
    return {"hin": hgrn_w_in[0], "hout": hgrn_w_out[0], "qkv": attn_w_qkv[0], "aout": attn_w_out[0],
            "fin0": ffn_w_in[0], "fin1": ffn_w_in[1], "fdn0": ffn_w_down[0], "fdn1": ffn_w_down[1]}


def _halves(v):
    r, c = v.shape
    return v.reshape(2, r // 2, c)


def _full_weights(gathered):
    out = {}
    for k, g in gathered.items():
        _, _, r, c = g.shape
        if k in ("hin", "qkv", "fin0", "fin1"):
            out[k] = g.reshape(N_CHIPS, 2 * r, c)
        else:
            out[k] = g.reshape(N_CHIPS * 2 * r, c)
    return out


class _StepComm:
    def __init__(self, shards, c_idx, me_idx):
        self.shards, self.c_idx, self.me_idx = shards, c_idx, me_idx
        self.halves = {}
        self._stage = {}

    def gather_rider(self, names):
        return _GatherRider([_halves(self.shards[k].astype(BF16)) for k in names])

    def gathered(self, names, got):
        return _full_weights(dict(zip(names, got)))

    def first_weights(self):
        return self.gathered(FIRST_WEIGHTS, _run_rider(self.gather_rider(FIRST_WEIGHTS), "gather_first"))

    def pair_rider(self, grads, tag):
        names = list(grads)
        g4 = []
        for k in names:
            r, c = self.shards[k].shape
            g4.append(grads[k].reshape(N_CHIPS, 2, r // 2, c))
        self._stage[tag] = (names, g4)
        return _PairRider(g4)

    def pair_now(self, grads, tag):
        self.paired(tag, _run_rider(self.pair_rider(grads, tag), "grad_pair_exchange_" + tag))

    def paired(self, tag, got):
        names, g4 = self._stage[tag]
        self._stage[tag] = (names, [(g, s, _pair_sum(g, s, self.c_idx)) for g, s in zip(g4, got)])

    def exchange_rider(self, tag):
        return _ExchangeRider([s[2] for s in self._stage[tag][1]])

    def exchanged(self, tag, got):
        names, sums = self._stage.pop(tag)
        place = jnp.concatenate([self.me_idx, self.c_idx])
        for k, (g, sib, _), recv in zip(names, sums, got):
            self.halves[k] = _chip_sum(g, sib, recv, place)

    def shared_halves(self):
        mine = [self.halves[k] for k in WEIGHT_NAMES]
        return dict(zip(WEIGHT_NAMES, zip(mine, _pair_share(mine))))


def kernel(x, norm_mix, norm_ffn, hgrn_w_in, hgrn_lb_logits, hgrn_out_norm, hgrn_w_out, attn_w_qkv, attn_w_out, ffn_w_in, ffn_w_down, final_norm, loss_target, m_norm_mix, m_norm_ffn, m_hgrn_w_in, m_hgrn_lb_logits, m_hgrn_out_norm, m_hgrn_w_out, m_attn_w_qkv, m_attn_w_out, m_ffn_w_in, m_ffn_w_down, m_final_norm, v_norm_mix, v_norm_ffn, v_hgrn_w_in, v_hgrn_lb_logits, v_hgrn_out_norm, v_hgrn_w_out, v_attn_w_qkv, v_attn_w_out, v_ffn_w_in, v_ffn_w_down, v_final_norm):
    S = x.shape[1]
    xi, yi, ci = lax.axis_index("x"), lax.axis_index("y"), lax.axis_index("c")
    c_idx = jnp.reshape(ci, (1,)).astype(jnp.int32)
    me_idx = jnp.reshape(2 * xi + yi, (1,)).astype(jnp.int32)

    w_own = _split_weights(hgrn_w_in, hgrn_w_out, attn_w_qkv, attn_w_out, ffn_w_in, ffn_w_down)

    comm = _StepComm(w_own, c_idx, me_idx)
    loss, dx, small = _local_step(
        x.reshape(S, D_MODEL), loss_target.reshape(S, D_MODEL), norm_mix, norm_ffn, hgrn_lb_logits,
        hgrn_out_norm, final_norm.reshape(1, D_MODEL), comm)

    halves = comm.shared_halves()
    updated = {}
    for tensor, layers, (wt, mt, vt) in (
            ("hgrn_w_in", ("hin",), (hgrn_w_in, m_hgrn_w_in, v_hgrn_w_in)),
            ("hgrn_w_out", ("hout",), (hgrn_w_out, m_hgrn_w_out, v_hgrn_w_out)),
            ("attn_w_qkv", ("qkv",), (attn_w_qkv, m_attn_w_qkv, v_attn_w_qkv)),
            ("attn_w_out", ("aout",), (attn_w_out, m_attn_w_out, v_attn_w_out)),
            ("ffn_w_in", ("fin0", "fin1"), (ffn_w_in, m_ffn_w_in, v_ffn_w_in)),
            ("ffn_w_down", ("fdn0", "fdn1"), (ffn_w_down, m_ffn_w_down, v_ffn_w_down))):
        updated[tensor] = _adamw([halves[k] for k in layers], c_idx, wt, mt, vt, "adamw_" + tensor)

    loss_row = jnp.pad(loss, ((0, 0), (0, D_MODEL - loss.shape[1])))
    lb3 = jnp.concatenate([small["lb"], jnp.zeros((2, D_MODEL), F32)], axis=0)
    on_grad = jnp.sum(small["out_norm"], axis=0, keepdims=True)
    pack = _pack_small(small["norm_mix"], small["norm_ffn"], lb3, on_grad, small["final_norm"], loss_row)
    gsum = _small_allreduce(pack)
    w_s = _pack_small(norm_mix, norm_ffn, hgrn_lb_logits, hgrn_out_norm, final_norm)
    m_s = _pack_small(m_norm_mix, m_norm_ffn, m_hgrn_lb_logits, m_hgrn_out_norm, m_final_norm)
    v_s = _pack_small(v_norm_mix, v_norm_ffn, v_hgrn_lb_logits, v_hgrn_out_norm, v_final_norm)
    lg_pack = jnp.pad(hgrn_lb_logits, ((0, 8 - hgrn_lb_logits.shape[0]), (0, 0)))
    sg, sd, sm, sv = _small_update(gsum, lg_pack, w_s, m_s, v_s)

    def unpack(p):
        return (p[0:2], p[2:4], p[4:7], p[7:8, :HEAD], p[8])

    def assemble(p, which):
        nmx, nff, lbl, onm, fnm = unpack(p)
        hin, hout, qkv, aout, fin, fdn = [updated[t][which] for t in
                                          ("hgrn_w_in", "hgrn_w_out", "attn_w_qkv", "attn_w_out", "ffn_w_in", "ffn_w_down")]
        return (nmx, nff, hin, lbl, onm, hout, qkv, aout, fin, fdn, fnm)

    total_loss = gsum[9, 0]
    return (total_loss, dx.reshape(1, S, D_MODEL), *assemble(sg, 0), *assemble(sd, 1), *assemble(sm, 2), *assemble(sv, 3))
```

```python
import functools

import jax
import jax.numpy as jnp
from jax import lax
from jax.experimental import pallas as pl
from jax.experimental.pallas import tpu as pltpu

F32 = jnp.float32
BF16 = jnp.bfloat16
MESH = pl.DeviceIdType.MESH

D_MODEL = 1024
HEAD = 128
HGRN_HEADS = 8
HGRN_CHUNK = 64
HGRN_HEADS_PER_STEP = 2
ATTN_GROUPS = ((128, 1), (512, 4), (2048, 16))
ATTN_SPAN = 128
HEADS_PER_GROUP = 4
GROUP_W = HEADS_PER_GROUP * HEAD
D_FF = 2816
NORM_EPS = 1e-6
ROPE_THETA = 10000.0
NEG = -1e30

ADAM_LR, ADAM_B1, ADAM_B2, ADAM_EPS, ADAM_WD, ADAM_STEP = 0.001, 0.9, 0.999, 1e-08, 0.01, 10

N_CHIPS = 4
VMEM_LIMIT = 56 * 1024 * 1024
SMALL_ROWS = 16


def _params(sem=None):
    return pltpu.CompilerParams(dimension_semantics=sem, vmem_limit_bytes=VMEM_LIMIT)


def _row_tile(rows, cols, budget_bytes=3 * 512 * 1024):
    best = 8
    for t in range(8, rows + 1, 8):
        if rows % t == 0 and t * cols * 4 <= budget_bytes:
            best = t
    assert rows % best == 0
    return best


def _grid_corner(i, j):
    return jnp.logical_and(pl.program_id(0) == i, pl.program_id(1) == j)


def _sigmoid(v):
    return 0.5 * jnp.tanh(0.5 * v) + 0.5


def _dot(a, b):
    return jnp.dot(a, b, preferred_element_type=F32)


def _dot_nt(a, b):
    return lax.dot_general(a, b, (((1,), (1,)), ((), ())), preferred_element_type=F32)


def _dot_tn(a, b):
    return lax.dot_general(a, b, (((0,), (0,)), ((), ())), preferred_element_type=F32)


def _dot_exact(ones, b):
    ones = ones.astype(BF16)
    hi = b.astype(BF16)
    rest = b - hi.astype(F32)
    mid = rest.astype(BF16)
    low = (rest - mid.astype(F32)).astype(BF16)
    return _dot(ones, hi) + _dot(ones, mid) + _dot(ones, low)


def _rstd(v):
    return lax.rsqrt(jnp.mean(v * v, axis=-1, keepdims=True) + NORM_EPS)


def _norm_mm(h, gain, w3, name, out_dtype=F32, tm=2048, rider=None):
    S, K = h.shape
    J, _, n = w3.shape
    gi = S // tm

    def body(h_ref, g_ref, w_ref, y_ref, u_ref):
        @pl.when(pl.program_id(1) == 0)
        def _():
            v = h_ref[...]
            u_ref[...] = (v * _rstd(v) * g_ref[...]).astype(BF16)

        y_ref[...] = _dot(u_ref[...], w_ref[pl.program_id(1)]).astype(y_ref.dtype)

    r_ops, r_in, r_out, r_shape, r_scr = _rider_args(rider)
    res = pl.pallas_call(
        _ride(rider, body, 3, 2, functools.partial(_grid_corner, 0, 0), functools.partial(_grid_corner, gi - 1, J - 1)),
        name=name, grid=(gi, J),
        in_specs=[pl.BlockSpec((tm, K), lambda i, j: (i, 0)),
                  pl.BlockSpec((1, K), lambda i, j: (0, 0)),
                  pl.BlockSpec((J, K, n), lambda i, j: (0, 0, 0), pipeline_mode=pl.Buffered(1))] + r_in,
        out_specs=[pl.BlockSpec((tm, n), lambda i, j: (i, j)), pl.BlockSpec((tm, K), lambda i, j: (i, 0))] + r_out,
        out_shape=[jax.ShapeDtypeStruct((S, J * n), out_dtype), jax.ShapeDtypeStruct((S, K), BF16)] + r_shape,
        scratch_shapes=r_scr,
        compiler_params=_params(("arbitrary", "arbitrary")))(h, gain, w3, *r_ops)
    return res[0], res[1], res[2:]


def _mm_res(h, a, w2, name, tm=1024):
    S, N = h.shape
    K = a.shape[1]

    def body(h_ref, a_ref, w_ref, o_ref):
        o_ref[...] = h_ref[...] + _dot(a_ref[...], w_ref[...])

    return pl.pallas_call(
        body, name=name, grid=(S // tm,),
        in_specs=[pl.BlockSpec((tm, N), lambda i: (i, 0)),
                  pl.BlockSpec((tm, K), lambda i: (i, 0)),
                  pl.BlockSpec((K, N), lambda i: (0, 0))],
        out_specs=pl.BlockSpec((tm, N), lambda i: (i, 0)),
        out_shape=jax.ShapeDtypeStruct((S, N), F32),
        compiler_params=_params(("parallel",)))(h, a, w2)


def _swiglu(z_ref, F):
    g = z_ref[:, :F].astype(F32)
    return (g * _sigmoid(g) * z_ref[:, F:].astype(F32)).astype(BF16)


def _swiglu_mm_res(h, z, w2, name, tm=512):
    S, N = h.shape
    F = w2.shape[0]

    def body(h_ref, z_ref, w_ref, o_ref, a_ref):
        a = _swiglu(z_ref, F)
        a_ref[...] = a
        o_ref[...] = h_ref[...] + _dot(a, w_ref[...])

    return pl.pallas_call(
        body, name=name, grid=(S // tm,),
        in_specs=[pl.BlockSpec((tm, N), lambda i: (i, 0)),
                  pl.BlockSpec((tm, 2 * F), lambda i: (i, 0)),
                  pl.BlockSpec((F, N), lambda i: (0, 0), pipeline_mode=pl.Buffered(1))],
        out_specs=[pl.BlockSpec((tm, N), lambda i: (i, 0)), pl.BlockSpec((tm, F), lambda i: (i, 0))],
        out_shape=[jax.ShapeDtypeStruct((S, N), F32), jax.ShapeDtypeStruct((S, F), BF16)],
        compiler_params=_params(("parallel",)))(h, z, w2)


def _dy_specs(dy, J, n, tm):
    if dy.ndim == 3:
        return [pl.BlockSpec((None, tm, n), functools.partial(lambda i, j: (j, i, 0), j=j)) for j in range(J)]
    return [pl.BlockSpec((tm, n), functools.partial(lambda i, j: (i, j), j=j)) for j in range(J)]


def _acc_nt(dy_refs, w_ref):
    acc = None
    for j, r in enumerate(dy_refs):
        t = _dot_nt(r[...].astype(BF16), w_ref[j])
        acc = t if acc is None else acc + t
    return acc


def _mm_nt(dy, w3, name, out_dtype=F32, tm=1024):
    J, K, n = w3.shape
    S = dy.shape[-2]

    def body(*refs):
        dy_refs, w_ref, o_ref = refs[:J], refs[J], refs[J + 1]
        o_ref[...] = _acc_nt(dy_refs, w_ref).astype(o_ref.dtype)

    return pl.pallas_call(
        body, name=name, grid=(S // tm,),
        in_specs=_dy_specs(dy, J, n, tm) + [pl.BlockSpec((J, K, n), lambda i: (0, 0, 0))],
        out_specs=pl.BlockSpec((tm, K), lambda i: (i, 0)),
        out_shape=jax.ShapeDtypeStruct((S, K), out_dtype),
        compiler_params=_params(("parallel",)))(*([dy] * J), w3)


def _mm_nt_normbwd(dy, w3, h, gain, dh, name, tm=512, rider=None):
    J, K, n = w3.shape
    S = h.shape[0]
    steps = S // tm

    def body(*refs):
        dy_refs, w_ref, h_ref, g_ref, dh_ref, o_ref, dg_ref = refs[:J], *refs[J:]
        du = _acc_nt(dy_refs, w_ref)
        v = h_ref[...]
        r = _rstd(v)
        xh = v * r
        dyg = du * g_ref[...]
        o_ref[...] = dh_ref[...] + r * (dyg - xh * jnp.mean(dyg * xh, axis=-1, keepdims=True))

        @pl.when(pl.program_id(0) == 0)
        def _():
            dg_ref[...] = jnp.zeros_like(dg_ref)

        dg_ref[...] += jnp.sum(du * xh, axis=0, keepdims=True)

    row = pl.BlockSpec((tm, K), lambda i: (i, 0))
    vec = pl.BlockSpec((1, K), lambda i: (0, 0))
    r_ops, r_in, r_out, r_shape, r_scr = _rider_args(rider)
    res = pl.pallas_call(
        _ride(rider, body, J + 4, 2, lambda: pl.program_id(0) == 0, lambda: pl.program_id(0) == steps - 1),
        name=name, grid=(steps,),
        in_specs=_dy_specs(dy, J, n, tm) + [pl.BlockSpec((J, K, n), lambda i: (0, 0, 0)), row, vec, row] + r_in,
        out_specs=[row, vec] + r_out,
        out_shape=[jax.ShapeDtypeStruct((S, K), F32), jax.ShapeDtypeStruct((1, K), F32)] + r_shape,
        scratch_shapes=r_scr,
        compiler_params=_params(("arbitrary",)))(*([dy] * J), w3, h, gain, dh, *r_ops)
    return res[0], res[1], res[2:]


def _mm_nt_swiglu_bwd(dh, w2, z, name, tm=512, chunks=11):
    F, N = w2.shape
    S = dh.shape[0]
    fc = F // chunks
    assert fc * chunks == F and fc % HEAD == 0

    def body(dh_ref, w_ref, z_ref, o_ref):
        dhb = dh_ref[...].astype(BF16)
        da = [_dot_nt(dhb, w_ref[c * fc:(c + 1) * fc, :]) for c in range(chunks)]
        for c in range(chunks):
            g = z_ref[:, c * fc:(c + 1) * fc].astype(F32)
            u = z_ref[:, F + c * fc:F + (c + 1) * fc].astype(F32)
            sg = _sigmoid(g)
            o_ref[:, c * fc:(c + 1) * fc] = (da[c] * u * (sg * (1.0 + g * (1.0 - sg)))).astype(BF16)
            o_ref[:, F + c * fc:F + (c + 1) * fc] = (da[c] * (g * sg)).astype(BF16)

    return pl.pallas_call(
        body, name=name, grid=(S // tm,),
        in_specs=[pl.BlockSpec((tm, N), lambda i: (i, 0)),
                  pl.BlockSpec((F, N), lambda i: (0, 0), pipeline_mode=pl.Buffered(1)),
                  pl.BlockSpec((tm, 2 * F), lambda i: (i, 0))],
        out_specs=pl.BlockSpec((tm, 2 * F), lambda i: (i, 0)),
        out_shape=jax.ShapeDtypeStruct((S, 2 * F), BF16),
        compiler_params=_params(("parallel",)))(dh, w2, z)


def _mm_tn(x, dy, J, n, tn, name):
    tpn = n // tn
    ts = 2048 if x.shape[1] <= 1536 else 1024
    S, K = x.shape
    if dy.ndim == 3:
        dy_spec = pl.BlockSpec((None, ts, tn), lambda c, s: (c // tpn, s, c % tpn))
    else:
        dy_spec = pl.BlockSpec((ts, tn), lambda c, s: (s, c))

    def body(x_ref, dy_ref, o_ref):
        @pl.when(pl.program_id(1) == 0)
        def _():
            o_ref[...] = jnp.zeros_like(o_ref)

        o_ref[...] += _dot_tn(x_ref[...], dy_ref[...].astype(BF16))

    return pl.pallas_call(
        body, name=name, grid=(J * tpn, S // ts),
        in_specs=[pl.BlockSpec((ts, K), lambda c, s: (s, 0)), dy_spec],
        out_specs=pl.BlockSpec((None, K, tn), lambda c, s: (c // tpn, 0, c % tpn)),
        out_shape=jax.ShapeDtypeStruct((J, K, n), F32),
        compiler_params=_params(("parallel", "arbitrary")))(x, dy)


def _loss_head(h, gain, target, tm=1024):
    S, K = h.shape

    def body(h_ref, g_ref, t_ref, dh_ref, loss_ref, dg_ref):
        v = h_ref[...]
        r = _rstd(v)
        xh = v * r
        g = g_ref[...]
        dy = (xh * g - t_ref[...]) * (1.0 / K)
        dyg = dy * g
        dh_ref[...] = r * (dyg - xh * jnp.mean(dyg * xh, axis=-1, keepdims=True))

        @pl.when(pl.program_id(0) == 0)
        def _():
            loss_ref[...] = jnp.zeros_like(loss_ref)
            dg_ref[...] = jnp.zeros_like(dg_ref)

        part = jnp.sum(jnp.sum(dy * dy, axis=-1, keepdims=True), axis=0, keepdims=True) * (0.5 * K)
        lane = lax.broadcasted_iota(jnp.int32, loss_ref.shape, 1)
        loss_ref[...] += jnp.where(lane == 0, part, 0.0)
        dg_ref[...] += jnp.sum(dy * xh, axis=0, keepdims=True)

    row = pl.BlockSpec((tm, K), lambda i: (i, 0))
    vec = pl.BlockSpec((1, K), lambda i: (0, 0))
    return pl.pallas_call(
        body, name="loss_head", grid=(S // tm,),
        in_specs=[row, vec, row],
        out_specs=[row, pl.BlockSpec((1, HEAD), lambda i: (0, 0)), vec],
        out_shape=[jax.ShapeDtypeStruct((S, K), F32), jax.ShapeDtypeStruct((1, HEAD), F32),
                   jax.ShapeDtypeStruct((1, K), F32)],
        compiler_params=_params(("arbitrary",)))(h, gain, target)


def _lower_bound(lg_ref):
    l0, l1, l2 = lg_ref[0:1, :], lg_ref[1:2, :], lg_ref[2:3, :]
    mx = jnp.maximum(jnp.maximum(l0, l1), l2)
    e0, e1, e2 = jnp.exp(l0 - mx), jnp.exp(l1 - mx), jnp.exp(l2 - mx)
    return e0 / (e0 + e1 + e2)


def _chunks(v, ncb):
    C = HGRN_CHUNK
    return [v[c * C:(c + 1) * C] for c in range(ncb)]


def _rows(parts):
    return jnp.concatenate(parts, axis=0)


def _block_gates(qz, fz, lb, ncb):
    C = HGRN_CHUNK
    row = lax.broadcasted_iota(jnp.int32, (C, C), 0)
    col = lax.broadcasted_iota(jnp.int32, (C, C), 1)
    tri = (col <= row).astype(F32)
    first_half = lax.broadcasted_iota(jnp.int32, (C, HEAD), 0) < C // 2
    sig = _sigmoid(fz)
    fg = lb + (1.0 - lb) * sig
    key = 1.0 - fg
    lg = jnp.log(fg)
    lgs = _chunks(lg, ncb)
    b = _rows([_dot_exact(tri, v) for v in lgs])
    r_c = [jnp.sum(jnp.where(first_half, v, 0.0), axis=0, keepdims=True) for v in lgs]
    bl_c = [jnp.sum(v, axis=0, keepdims=True) for v in lgs]
    r = _rows([jnp.broadcast_to(v, (C, HEAD)) for v in r_c])
    e_br, e_rb = jnp.exp(b - r), jnp.exp(r - b)
    e_b = e_br * _rows([jnp.broadcast_to(jnp.exp(v), (C, HEAD)) for v in r_c])
    e_lb = e_rb * _rows([jnp.broadcast_to(jnp.exp(e - v), (C, HEAD)) for e, v in zip(bl_c, r_c)])
    sq = _sigmoid(qz)
    qy = qz * sq
    return sig, fg, key, (e_br, e_rb, e_b, e_lb), bl_c, sq, qy


def _hgrn_fwd(proj, logits, gain, tb=1024, rider=None):
    S = proj.shape[0]
    H, C = HGRN_HEADS, HGRN_CHUNK
    ncb = tb // C

    def one_head(q_ref, f_ref, i_ref, g_ref, lg_ref, gn_ref, o_ref, og_ref, st_ref, state):
        @pl.when(pl.program_id(1) == 0)
        def _():
            state[...] = jnp.zeros_like(state)

        lb = _lower_bound(lg_ref)
        causal = lax.broadcasted_iota(jnp.int32, (C, C), 1) <= lax.broadcasted_iota(jnp.int32, (C, C), 0)
        qz, fz, gz = q_ref[...], f_ref[...], g_ref[...]
        _, _, key, (e_br, e_rb, e_b, e_lb), bl_c, _, qy = _block_gates(qz, fz, lb, ncb)
        qs = _chunks((qy * e_br).astype(BF16), ncb)
        ks = _chunks((key * e_rb).astype(BF16), ncb)
        qb = _chunks((qy * e_b).astype(BF16), ncb)
        ke = _chunks((key * e_lb).astype(BF16), ncb)
        vb = _chunks(i_ref[...].astype(BF16), ncb)
        a = [jnp.where(causal, _dot_nt(qs[c], ks[c]), 0.0).astype(BF16) for c in range(ncb)]
        upd = [_dot_tn(vb[c], ke[c]) for c in range(ncb)]
        o_intra = [_dot(a[c], vb[c]) for c in range(ncb)]
        st = state[...]
        e_l = [jnp.exp(v) for v in bl_c]
        sts = []
        for c in range(ncb):
            sts.append(st)
            st = st * e_l[c] + upd[c]
        state[...] = st
        for c in range(ncb):
            st_ref[c] = sts[c]
        o = _rows([_dot_nt(qb[c], sts[c].astype(BF16)) + o_intra[c] for c in range(ncb)])
        o_ref[...] = o
        og_ref[...] = ((o * _rstd(o) * gn_ref[...]) * (gz * _sigmoid(gz))).astype(BF16)

    def body(q_ref, f_ref, i_ref, g_ref, lg_ref, gn_ref, o_ref, og_ref, st_ref, state):
        for hs in range(HP):
            cols = slice(hs * HEAD, (hs + 1) * HEAD)
            one_head(q_ref.at[:, cols], f_ref.at[:, cols], i_ref.at[:, cols], g_ref.at[:, cols], lg_ref.at[:, cols],
                     gn_ref, o_ref.at[:, cols], og_ref.at[:, cols], st_ref.at[hs], state.at[hs])

    HP, wide = HGRN_HEADS_PER_STEP, HGRN_HEADS_PER_STEP * HEAD
    hg = H // HP

    def part(p):
        return pl.BlockSpec((tb, wide), functools.partial(lambda h, i, p: (i, p * hg + h), p=p))

    nb = S // tb
    r_ops, r_in, r_out, r_shape, r_scr = _rider_args(rider)
    res = pl.pallas_call(
        _ride(rider, body, 6, 3, functools.partial(_grid_corner, 0, 0), functools.partial(_grid_corner, hg - 1, nb - 1)),
        name="hgrn_fwd", grid=(hg, nb),
        in_specs=[part(0), part(1), part(2), part(3),
                  pl.BlockSpec((3, wide), lambda h, i: (0, h)),
                  pl.BlockSpec((1, HEAD), lambda h, i: (0, 0))] + r_in,
        out_specs=[pl.BlockSpec((tb, wide), lambda h, i: (i, h)),
                   pl.BlockSpec((tb, wide), lambda h, i: (i, h)),
                   pl.BlockSpec((HP, ncb, HEAD, HEAD), lambda h, i: (h, i, 0, 0))] + r_out,
        out_shape=[jax.ShapeDtypeStruct((S, H * HEAD), F32),
                   jax.ShapeDtypeStruct((S, H * HEAD), BF16),
                   jax.ShapeDtypeStruct((H, S // C, HEAD, HEAD), F32)] + r_shape,
        scratch_shapes=[pltpu.VMEM((HP, HEAD, HEAD), F32)] + r_scr,
        compiler_params=_params(("arbitrary", "arbitrary")))(proj, proj, proj, proj, logits, gain, *r_ops)
    return res[:3], res[3:]


def _hgrn_bwd(proj, logits, gain, o, states, dog, tb=1024, rider=None):
    S = proj.shape[0]
    H, C = HGRN_HEADS, HGRN_CHUNK
    ncb = tb // C
    nb = S // tb

    def one_head(q_ref, f_ref, i_ref, g_ref, lg_ref, gn_ref, o_ref, st_ref, dog_ref,
                 dp_ref, dlb_ref, dgn_ref, dstate, dst_scr):
        @pl.when(pl.program_id(1) == 0)
        def _():
            dstate[...] = jnp.zeros_like(dstate)
            dlb_ref[...] = jnp.zeros_like(dlb_ref)
            dgn_ref[...] = jnp.zeros_like(dgn_ref)

        lb = _lower_bound(lg_ref)
        oml = 1.0 - lb
        gn = gn_ref[...]
        row = lax.broadcasted_iota(jnp.int32, (C, C), 0)
        col = lax.broadcasted_iota(jnp.int32, (C, C), 1)
        causal = col <= row
        tri_up = (col >= row).astype(F32)
        qz, fz, gz = q_ref[...], f_ref[...], g_ref[...]
        sig, fg, key, (e_br, e_rb, e_b, e_lb), bl_c, sq, qy = _block_gates(qz, fz, lb, ncb)
        qs_v, ks_v = (qy * e_br).astype(BF16), (key * e_rb).astype(BF16)
        qb_v, ke_v = (qy * e_b).astype(BF16), (key * e_lb).astype(BF16)
        qs, ks, qb, ke = _chunks(qs_v, ncb), _chunks(ks_v, ncb), _chunks(qb_v, ncb), _chunks(ke_v, ncb)
        vb = _chunks(i_ref[...].astype(BF16), ncb)
        ov = o_ref[...]
        rs = _rstd(ov)
        xh = ov * rs
        sg = _sigmoid(gz)
        dog_v = dog_ref[...]
        dgz = dog_v * (xh * gn) * (sg * (1.0 + gz * (1.0 - sg)))
        don = dog_v * (gz * sg)
        dgn_ref[...] += jnp.sum(don * xh, axis=0, keepdims=True)
        dyg = don * gn
        do = rs * (dyg - xh * jnp.mean(dyg * xh, axis=-1, keepdims=True))
        dob = _chunks(do.astype(BF16), ncb)
        CH = range(ncb)
        a = [jnp.where(causal, _dot_nt(qs[c], ks[c]), 0.0).astype(BF16) for c in CH]
        da = [jnp.where(causal, _dot_nt(dob[c], vb[c]), 0.0).astype(BF16) for c in CH]
        wst = [_dot_tn(dob[c], qb[c]) for c in CH]
        dv_in = [_dot_tn(a[c], dob[c]) for c in CH]
        dqs = [_dot(da[c], ks[c]) for c in CH]
        dks = [_dot_tn(da[c], qs[c]) for c in CH]
        e_l = [jnp.exp(v) for v in bl_c]
        dst = dstate[...]
        for c in reversed(range(ncb)):
            dst_scr[c] = dst
            dst = wst[c] + dst * e_l[c]
        dstate[...] = dst
        dst1b = [dst_scr[c].astype(BF16) for c in CH]
        dqb = [_dot(dob[c], st_ref[c].astype(BF16)) for c in CH]
        dke = [_dot(vb[c], dst1b[c]) for c in CH]
        dv = [dv_in[c] + _dot_nt(ke[c], dst1b[c]) for c in CH]
        dbl_st = [jnp.sum(dst_scr[c] * st_ref[c], axis=0, keepdims=True) * e_l[c] for c in CH]
        dqs, dks, dqb, dke, dv = _rows(dqs), _rows(dks), _rows(dqb), _rows(dke), _rows(dv)
        dke_ke = dke * ke_v.astype(F32)
        db = dqs * qs_v.astype(F32) - dks * ks_v.astype(F32) + dqb * qb_v.astype(F32) - dke_ke
        dlg = []
        for c, (db_c, kk_c) in enumerate(zip(_chunks(db, ncb), _chunks(dke_ke, ncb))):
            dbl = jnp.sum(kk_c, axis=0, keepdims=True) + dbl_st[c]
            dlg.append(_dot_exact(tri_up, db_c) + dbl)
        dlg = _rows(dlg)
        dkey = dks * e_rb + dke * e_lb
        dqy = dqs * e_br + dqb * e_b
        dfg = dlg / fg - dkey
        dlb_ref[...] += jnp.sum(dfg * (1.0 - sig), axis=0, keepdims=True)
        dp_ref[0] = (dqy * (sq * (1.0 + qz * (1.0 - sq)))).astype(BF16)
        dp_ref[1] = (dfg * oml * sig * (1.0 - sig)).astype(BF16)
        dp_ref[2] = dv.astype(BF16)
        dp_ref[3] = dgz.astype(BF16)

    def body(q_ref, f_ref, i_ref, g_ref, lg_ref, gn_ref, o_ref, st_ref, dog_ref,
             dp_ref, dlb_ref, dgn_ref, dstate, dst_scr):
        for hs in range(HP):
            cols = slice(hs * HEAD, (hs + 1) * HEAD)
            one_head(q_ref.at[:, cols], f_ref.at[:, cols], i_ref.at[:, cols], g_ref.at[:, cols], lg_ref.at[:, cols],
                     gn_ref, o_ref.at[:, cols], st_ref.at[hs], dog_ref.at[:, cols],
                     dp_ref.at[:, :, cols], dlb_ref.at[hs], dgn_ref.at[hs], dstate.at[hs], dst_scr)

    HP, wide = HGRN_HEADS_PER_STEP, HGRN_HEADS_PER_STEP * HEAD
    hg = H // HP

    def part(p):
        return pl.BlockSpec((tb, wide), functools.partial(lambda h, i, p: (nb - 1 - i, p * hg + h), p=p))

    blk = pl.BlockSpec((tb, wide), lambda h, i: (nb - 1 - i, h))
    acc = pl.BlockSpec((HP, 1, HEAD), lambda h, i: (h, 0, 0))
    r_ops, r_in, r_out, r_shape, r_scr = _rider_args(rider)
    res = pl.pallas_call(
        _ride(rider, body, 9, 3, functools.partial(_grid_corner, 0, 0), functools.partial(_grid_corner, hg - 1, nb - 1)),
        name="hgrn_bwd", grid=(hg, nb),
        in_specs=[part(0), part(1), part(2), part(3),
                  pl.BlockSpec((3, wide), lambda h, i: (0, h)),
                  pl.BlockSpec((1, HEAD), lambda h, i: (0, 0)),
                  blk,
                  pl.BlockSpec((HP, ncb, HEAD, HEAD), lambda h, i: (h, nb - 1 - i, 0, 0)),
                  blk] + r_in,
        out_specs=[pl.BlockSpec((4, tb, wide), lambda h, i: (0, nb - 1 - i, h)), acc, acc] + r_out,
        out_shape=[jax.ShapeDtypeStruct((4, S, H * HEAD), BF16),
                   jax.ShapeDtypeStruct((H, 1, HEAD), F32),
                   jax.ShapeDtypeStruct((H, 1, HEAD), F32)] + r_shape,
        scratch_shapes=[pltpu.VMEM((HP, HEAD, HEAD), F32), pltpu.VMEM((ncb, HEAD, HEAD), F32)] + r_scr,
        compiler_params=_params(("arbitrary", "arbitrary")))(
            proj, proj, proj, proj, logits, gain, o, states, dog, *r_ops)
    return res[:3], res[3:]


def _rope(v, cos, sin):
    return v * cos + pltpu.roll(v, HEAD // 2, 1) * sin


def _lane_pick(tile, hh):
    lane = lax.broadcasted_iota(jnp.int32, tile.shape, 1)
    return jnp.sum(jnp.where(lane == hh, tile, 0.0), axis=-1, keepdims=True)


def _lane_place(cols):
    rows = cols[0].shape[0]
    lane = lax.broadcasted_iota(jnp.int32, (rows, HEAD), 1)
    tile = jnp.zeros((rows, HEAD), F32)
    for hh, v in enumerate(cols):
        tile = jnp.where(lane == hh, v, tile)
    return tile


def _band_masks():
    qi = lax.broadcasted_iota(jnp.int32, (ATTN_SPAN, ATTN_SPAN), 0)
    kj = lax.broadcasted_iota(jnp.int32, (ATTN_SPAN, ATTN_SPAN), 1)
    return kj <= qi, kj >= qi


ATTN_TILE_BLOCKS = 8


def _attn_fwd(a):
    d, L, _ = a.shape
    B, W = min(ATTN_TILE_BLOCKS, a.shape[1] // ATTN_SPAN), ATTN_SPAN
    T = B * W
    assert L % T == 0
    steps = L // T
    scale = HEAD ** -0.5

    def body(q_ref, kc_ref, kp_ref, vc_ref, vp_ref, o_ref, lse_ref):
        n = pl.program_id(1)
        mask_c, mask_p0 = _band_masks()
        first = jnp.logical_and(mask_p0, n > 0)
        units = [(b, hh) for b in range(B) for hh in range(HEADS_PER_GROUP)]
        rows = [slice(b * W, (b + 1) * W) for b in range(B)]
        cols = [slice(hh * HEAD, (hh + 1) * HEAD) for hh in range(HEADS_PER_GROUP)]

        def prev_keys(ref, tile, b, hh):
            return ref[:, cols[hh]] if b == 0 else tile[rows[b - 1], cols[hh]]

        s_c = [jnp.where(mask_c, _dot_nt(q_ref[rows[b], cols[hh]], kc_ref[rows[b], cols[hh]]) * scale, NEG) for b, hh in units]
        s_p = [jnp.where(first if b == 0 else mask_p0,
                         _dot_nt(q_ref[rows[b], cols[hh]], prev_keys(kp_ref, kc_ref, b, hh)) * scale, NEG) for b, hh in units]
        m = [jnp.maximum(jnp.max(x, axis=-1, keepdims=True), jnp.max(y, axis=-1, keepdims=True)) for x, y in zip(s_c, s_p)]
        p_c = [jnp.exp(x - mm) for x, mm in zip(s_c, m)]
        p_p = [jnp.exp(y - mm) for y, mm in zip(s_p, m)]
        l = [jnp.sum(x, axis=-1, keepdims=True) + jnp.sum(y, axis=-1, keepdims=True) for x, y in zip(p_c, p_p)]
        acc = [_dot(p_c[i].astype(BF16), vc_ref[rows[b], cols[hh]]) + _dot(p_p[i].astype(BF16), prev_keys(vp_ref, vc_ref, b, hh))
               for i, (b, hh) in enumerate(units)]
        for i, (b, hh) in enumerate(units):
            o_ref[rows[b], cols[hh]] = (acc[i] / l[i]).astype(BF16)
        for b in range(B):
            lse_ref[rows[b], :] = _lane_place([m[i] + jnp.log(l[i]) for i, (bb, _) in enumerate(units) if bb == b])

    def cur(part):
        return pl.BlockSpec((None, T, GROUP_W), functools.partial(lambda r, n, p: (r, n, p), p=part))

    def prev(part):
        return pl.BlockSpec((None, W, GROUP_W), functools.partial(lambda r, n, p: (r, jnp.maximum(n * B - 1, 0), p), p=part))

    return pl.pallas_call(
        body, name=f"attn_fwd_d{d}", grid=(d, steps),
        in_specs=[cur(0), cur(1), prev(1), cur(2), prev(2)],
        out_specs=[pl.BlockSpec((None, T, GROUP_W), lambda r, n: (r, n, 0)), pl.BlockSpec((None, T, HEAD), lambda r, n: (r, n, 0))],
        out_shape=[jax.ShapeDtypeStruct((d, L, GROUP_W), BF16), jax.ShapeDtypeStruct((d, L, HEAD), F32)],
        compiler_params=_params(("parallel", "arbitrary")))(a, a, a, a, a)


def _attn_bwd(a, do, lse, dd):
    d, L, _ = a.shape
    B, W = min(ATTN_TILE_BLOCKS, a.shape[1] // ATTN_SPAN), ATTN_SPAN
    T = B * W
    assert L % T == 0
    steps = L // T
    scale = HEAD ** -0.5

    def body(qc_ref, qn_ref, kp_ref, kc_ref, vp_ref, vc_ref, doc_ref, don_ref, lc_ref, ln_ref, ddc_ref, ddn_ref, da_ref):
        n = pl.program_id(1)
        mask_c, mask_p0 = _band_masks()
        first = jnp.logical_and(mask_p0, n > 0)
        last = jnp.logical_and(mask_p0, n < steps - 1)
        H4 = range(HEADS_PER_GROUP)
        units = [(b, hh) for b in range(B) for hh in H4]
        rows = [slice(b * W, (b + 1) * W) for b in range(B)]
        cols = [slice(hh * HEAD, (hh + 1) * HEAD) for hh in H4]
        q = {u: qc_ref[rows[u[0]], cols[u[1]]] for u in units}
        k = {u: kc_ref[rows[u[0]], cols[u[1]]] for u in units}
        v = {u: vc_ref[rows[u[0]], cols[u[1]]] for u in units}
        g_o = {u: doc_ref[rows[u[0]], cols[u[1]]] for u in units}
        kb = {(b, hh): kp_ref[:, cols[hh]] if b == 0 else k[(b - 1, hh)] for b, hh in units}
        vb = {(b, hh): vp_ref[:, cols[hh]] if b == 0 else v[(b - 1, hh)] for b, hh in units}
        lse_t = {(b, hh): _lane_pick(lc_ref[rows[b], :], hh) for b, hh in units}
        dd_t = {(b, hh): _lane_pick(ddc_ref[rows[b], :], hh) for b, hh in units}
        p_c = {u: jnp.where(mask_c, jnp.exp(_dot_nt(q[u], k[u]) * scale - lse_t[u]), 0.0) for u in units}
        p_p = {u: jnp.where(first if u[0] == 0 else mask_p0, jnp.exp(_dot_nt(q[u], kb[u]) * scale - lse_t[u]), 0.0) for u in units}
        ds_c = {u: (p_c[u] * (_dot_nt(g_o[u], v[u]) + dd_t[u])).astype(BF16) for u in units}
        ds_p = {u: (p_p[u] * (_dot_nt(g_o[u], vb[u]) + dd_t[u])).astype(BF16) for u in units}
        qn = [qn_ref[:, c] for c in cols]
        g_n = [don_ref[:, c] for c in cols]
        p_n = [jnp.where(last, jnp.exp(_dot_nt(qn[hh], k[(B - 1, hh)]) * scale - _lane_pick(ln_ref[...], hh)), 0.0) for hh in H4]
        ds_n = [(p_n[hh] * (_dot_nt(g_n[hh], v[(B - 1, hh)]) + _lane_pick(ddn_ref[...], hh))).astype(BF16) for hh in H4]
        dq = {u: (_dot(ds_c[u], k[u]) + _dot(ds_p[u], kb[u])) * scale for u in units}
        dk, dv = {}, {}
        for b, hh in units:
            if b < B - 1:
                nxt = (b + 1, hh)
                dk[(b, hh)] = (_dot_tn(ds_c[(b, hh)], q[(b, hh)]) + _dot_tn(ds_p[nxt], q[nxt])) * scale
                dv[(b, hh)] = _dot_tn(p_c[(b, hh)].astype(BF16), g_o[(b, hh)]) + _dot_tn(p_p[nxt].astype(BF16), g_o[nxt])
            else:
                dk[(b, hh)] = (_dot_tn(ds_c[(b, hh)], q[(b, hh)]) + _dot_tn(ds_n[hh], qn[hh])) * scale
                dv[(b, hh)] = _dot_tn(p_c[(b, hh)].astype(BF16), g_o[(b, hh)]) + _dot_tn(p_n[hh].astype(BF16), g_n[hh])
        for b, hh in units:
            da_ref[rows[b], cols[hh]] = dq[(b, hh)].astype(BF16)
            da_ref[rows[b], GROUP_W + hh * HEAD:GROUP_W + (hh + 1) * HEAD] = dk[(b, hh)].astype(BF16)
            da_ref[rows[b], 2 * GROUP_W + hh * HEAD:2 * GROUP_W + (hh + 1) * HEAD] = dv[(b, hh)].astype(BF16)

    nb = L // W

    def cur(width, part):
        return pl.BlockSpec((None, T, width), functools.partial(lambda r, n, p: (r, n, p), p=part))

    def prev(width, part):
        return pl.BlockSpec((None, W, width), functools.partial(lambda r, n, p: (r, jnp.maximum(n * B - 1, 0), p), p=part))

    def nxt(width, part):
        return pl.BlockSpec((None, W, width), functools.partial(lambda r, n, p: (r, jnp.minimum(n * B + B, nb - 1), p), p=part))

    g = GROUP_W
    return pl.pallas_call(
        body, name=f"attn_bwd_d{d}", grid=(d, steps),
        in_specs=[cur(g, 0), nxt(g, 0), prev(g, 1), cur(g, 1), prev(g, 2), cur(g, 2),
                  cur(g, 0), nxt(g, 0), cur(HEAD, 0), nxt(HEAD, 0), cur(HEAD, 0), nxt(HEAD, 0)],
        out_specs=pl.BlockSpec((None, T, 3 * g), lambda r, n: (r, n, 0)),
        out_shape=jax.ShapeDtypeStruct((d, L, 3 * g), BF16),
        compiler_params=_params(("parallel", "arbitrary")))(
            a, a, a, a, a, a, do, do, lse, lse, dd, dd)


def _softmax3(ls):
    mx = jnp.maximum(jnp.maximum(ls[0], ls[1]), ls[2])
    es = [jnp.exp(v - mx) for v in ls]
    tot = es[0] + es[1] + es[2]
    return [e / tot for e in es]


HEAD_COLS = [slice(hh * HEAD, (hh + 1) * HEAD) for hh in range(HEADS_PER_GROUP)]


def _group_spec(d, tm):
    return pl.BlockSpec((d, tm // d, GROUP_W), lambda i: (0, i, 0))


def _gather_heads(ref, scr, d, tm):
    if d == 1:
        return [ref[0, :, cols].astype(F32) for cols in HEAD_COLS]
    for hh, cols in enumerate(HEAD_COLS):
        for r in range(d):
            scr.at[hh][pl.ds(r, tm // d, stride=d), :] = ref[r, :, cols].astype(F32)
    return [scr[hh] for hh in range(HEADS_PER_GROUP)]


def _tile_spec(d, tm):
    return pl.BlockSpec((d, tm // d, HEAD), lambda i: (0, i, 0))


def _gather_tile(ref, scr, d, tm):
    if d == 1:
        return ref[0]
    for r in range(d):
        scr[pl.ds(r, tm // d, stride=d), :] = ref[r]
    return scr[...]


def _scatter_tile(val, scr, ref, d, tm):
    if d == 1:
        ref[0] = val
        return
    scr[...] = val
    for r in range(d):
        ref[r] = scr[pl.ds(r, tm // d, stride=d), :]


def _scatter_heads(vals, scr, ref, d, tm):
    if d == 1:
        for cols, v in zip(HEAD_COLS, vals):
            ref[0, :, cols] = v.astype(ref.dtype)
        return
    for hh, v in enumerate(vals):
        scr[hh] = v
    for hh, cols in enumerate(HEAD_COLS):
        for r in range(d):
            ref[r, :, cols] = scr.at[hh][pl.ds(r, tm // d, stride=d), :].astype(ref.dtype)


def _qkv_dilated(h, gain, wg, cos, sin, d, tm=2048):
    S, K = h.shape

    def body(h_ref, g_ref, w_ref, cos_ref, sin_ref, out_ref, u_ref, y_scr):
        p = pl.program_id(1)

        @pl.when(p == 0)
        def _():
            v = h_ref[...]
            u_ref[...] = (v * _rstd(v) * g_ref[...]).astype(BF16)

        y = _dot(u_ref[...], w_ref[...])
        heads = [slice(hh * HEAD, (hh + 1) * HEAD) for hh in range(HEADS_PER_GROUP)]
        if d > 1:
            for hh, cols in enumerate(heads):
                y_scr[hh] = y[:, cols]

        def rows_of(hh, r):
            return y[:, heads[hh]] if d == 1 else y_scr.at[hh][pl.ds(r, tm // d, stride=d), :]

        @pl.when(p < 2)
        def _():
            for r in range(d):
                rows = slice(None) if d == 1 else pl.ds(r, tm // d, stride=d)
                cr, sr = cos_ref[rows, :], sin_ref[rows, :]
                for hh, cols in enumerate(heads):
                    out_ref[r, :, cols] = _rope(rows_of(hh, r), cr, sr).astype(BF16)

        @pl.when(p == 2)
        def _():
            for r in range(d):
                for hh, cols in enumerate(heads):
                    out_ref[r, :, cols] = rows_of(hh, r).astype(BF16)

    tab = pl.BlockSpec((tm, HEAD), lambda i, p: (i, 0))
    return pl.pallas_call(
        body, name=f"attn_qkv_d{d}", grid=(S // tm, 3),
        in_specs=[pl.BlockSpec((tm, K), lambda i, p: (i, 0)),
                  pl.BlockSpec((1, K), lambda i, p: (0, 0)),
                  pl.BlockSpec((K, GROUP_W), lambda i, p: (0, p)), tab, tab],
        out_specs=[pl.BlockSpec((d, tm // d, GROUP_W), lambda i, p: (0, i, p)), pl.BlockSpec((tm, K), lambda i, p: (i, 0))],
        out_shape=[jax.ShapeDtypeStruct((d, S // d, 3 * GROUP_W), BF16), jax.ShapeDtypeStruct((S, K), BF16)],
        scratch_shapes=[pltpu.VMEM((HEADS_PER_GROUP, tm, HEAD), F32)],
        compiler_params=_params(("parallel", "arbitrary")))(h, gain, wg, cos, sin)


def _undilate_group(da, dqkv, cos, sin, g, tm=2048):
    d, L, _ = da.shape
    S = d * L
    G = len(ATTN_GROUPS)

    def body(*refs):
        da_ref, cos_ref, sin_ref, out_ref, scr = refs[0], refs[1], refs[2], refs[-2], refs[-1]
        p = pl.program_id(1)
        heads = [slice(hh * HEAD, (hh + 1) * HEAD) for hh in range(HEADS_PER_GROUP)]
        if d > 1:
            for hh, cols in enumerate(heads):
                for r in range(d):
                    scr.at[hh][pl.ds(r, tm // d, stride=d), :] = da_ref[r, :, cols].astype(F32)

        def tokens(hh):
            return da_ref[0, :, heads[hh]].astype(F32) if d == 1 else scr[hh]

        @pl.when(p < 2)
        def _():
            cr, sr = cos_ref[...], -sin_ref[...]
            for hh, cols in enumerate(heads):
                out_ref[:, cols] = _rope(tokens(hh), cr, sr).astype(BF16)

        @pl.when(p == 2)
        def _():
            for hh, cols in enumerate(heads):
                out_ref[:, cols] = tokens(hh).astype(BF16)

    tab = pl.BlockSpec((tm, HEAD), lambda i, p: (i, 0))
    operands = (da, cos, sin) if dqkv is None else (da, cos, sin, dqkv)
    return pl.pallas_call(
        body, name=f"attn_undilate_d{d}", grid=(S // tm, 3),
        in_specs=[pl.BlockSpec((d, tm // d, GROUP_W), lambda i, p: (0, i, p)), tab, tab] + ([] if dqkv is None else [ANY]),
        out_specs=pl.BlockSpec((tm, GROUP_W), lambda i, p: (i, p * G + g)),
        out_shape=jax.ShapeDtypeStruct((S, 3 * G * GROUP_W), BF16),
        input_output_aliases={} if dqkv is None else {3: 0},
        scratch_shapes=[pltpu.VMEM((HEADS_PER_GROUP, tm, HEAD), F32)],
        compiler_params=_params(("parallel", "arbitrary")))(*operands)


def _attn_merge(os_, lses, tm=1024):
    G = len(os_)
    S = os_[0].shape[0] * os_[0].shape[1]

    def body(*refs):
        o_refs, l_refs, out_ref = refs[:G], refs[G:2 * G], refs[2 * G]
        scr = refs[2 * G + 1:]
        o = [_gather_heads(o_refs[g], scr[g], d, tm) for g, (_, d) in enumerate(ATTN_GROUPS)]
        l = [_gather_tile(l_refs[g], scr[G + g].at[0], d, tm) for g, (_, d) in enumerate(ATTN_GROUPS)]
        for hh in range(HEADS_PER_GROUP):
            al = _softmax3([_lane_pick(l[g], hh) for g in range(G)])
            for g in range(G):
                out_ref[:, g * GROUP_W + hh * HEAD:g * GROUP_W + (hh + 1) * HEAD] = (o[g][hh] * al[g]).astype(BF16)

    specs = [_group_spec(d, tm) for _, d in ATTN_GROUPS]
    return pl.pallas_call(
        body, name="attn_merge", grid=(S // tm,),
        in_specs=specs + [_tile_spec(d, tm) for _, d in ATTN_GROUPS],
        out_specs=pl.BlockSpec((tm, G * GROUP_W), lambda i: (i, 0)),
        out_shape=jax.ShapeDtypeStruct((S, G * GROUP_W), BF16),
        scratch_shapes=[pltpu.VMEM((HEADS_PER_GROUP, tm, HEAD), F32)] * (2 * G),
        compiler_params=_params(("parallel",)))(*os_, *lses)


def _attn_merge_bwd(os_, lses, doa, tm=1024):
    G = len(os_)
    S = doa.shape[0]

    def body(*refs):
        o_refs, l_refs, doa_ref = refs[:G], refs[G:2 * G], refs[2 * G]
        do_refs, dd_refs = refs[2 * G + 1:3 * G + 1], refs[3 * G + 1:4 * G + 1]
        scr = refs[4 * G + 1:]
        o = [_gather_heads(o_refs[g], scr[g], d, tm) for g, (_, d) in enumerate(ATTN_GROUPS)]
        l = [_gather_tile(l_refs[g], scr[G + g].at[0], d, tm) for g, (_, d) in enumerate(ATTN_GROUPS)]
        do = [[None] * HEADS_PER_GROUP for _ in range(G)]
        dd = [[None] * HEADS_PER_GROUP for _ in range(G)]
        for hh in range(HEADS_PER_GROUP):
            al = _softmax3([_lane_pick(l[g], hh) for g in range(G)])
            mix = None
            for g in range(G):
                dg = doa_ref[:, g * GROUP_W + hh * HEAD:g * GROUP_W + (hh + 1) * HEAD]
                do[g][hh] = dg * al[g]
                t = al[g] * jnp.sum(dg * o[g][hh], axis=-1, keepdims=True)
                mix = t if mix is None else mix + t
            for g in range(G):
                dd[g][hh] = -al[g] * mix
        for g, (_, d) in enumerate(ATTN_GROUPS):
            _scatter_heads(do[g], scr[2 * G + g], do_refs[g], d, tm)
            _scatter_tile(_lane_place(dd[g]), scr[3 * G + g].at[0], dd_refs[g], d, tm)

    specs = [_group_spec(d, tm) for _, d in ATTN_GROUPS]
    tiles = [_tile_spec(d, tm) for _, d in ATTN_GROUPS]
    do_shapes = [jax.ShapeDtypeStruct((d, S // d, GROUP_W), BF16) for _, d in ATTN_GROUPS]
    dd_shapes = [jax.ShapeDtypeStruct((d, S // d, HEAD), F32) for _, d in ATTN_GROUPS]
    return pl.pallas_call(
        body, name="attn_merge_bwd", grid=(S // tm,),
        in_specs=specs + tiles + [pl.BlockSpec((tm, G * GROUP_W), lambda i: (i, 0))],
        out_specs=specs + tiles,
        out_shape=do_shapes + dd_shapes,
        scratch_shapes=[pltpu.VMEM((HEADS_PER_GROUP, tm, HEAD), F32)] * (4 * G),
        compiler_params=_params(("parallel",)))(*os_, *lses, doa)


def _rope_tables(S):
    inv_freq = 1.0 / (ROPE_THETA ** (jnp.arange(0, HEAD, 2, dtype=F32) / HEAD))
    ang = jnp.arange(S, dtype=F32)[:, None] * inv_freq[None, :]
    cos, sin = jnp.cos(ang), jnp.sin(ang)
    return jnp.concatenate([cos, cos], axis=-1), jnp.concatenate([-sin, sin], axis=-1)


def _local_step(x, target, norm_mix, norm_ffn, lb_logits, out_gain, final_norm, comm):
    S = x.shape[0]
    nm0, nm1 = norm_mix[0:1], norm_mix[1:2]
    nf0, nf1 = norm_ffn[0:1], norm_ffn[1:2]
    w = comm.first_weights()

    proj, u0, got = _norm_mm(x, nm0, w["hin"], "hgrn_in", rider=comm.gather_rider(LATE_WEIGHTS_A))
    w.update(comm.gathered(LATE_WEIGHTS_A, got))
    (o, og, states), got = _hgrn_fwd(proj, lb_logits, out_gain, rider=comm.gather_rider(LATE_WEIGHTS_B))
    w.update(comm.gathered(LATE_WEIGHTS_B, got))
    fin_tn = w["fin0"].shape[2]
    h1 = _mm_res(x, og, w["hout"], "hgrn_out")
    z0, u1, _ = _norm_mm(h1, nf0, w["fin0"], "ffn0_in", out_dtype=BF16)
    h2, act0 = _swiglu_mm_res(h1, z0, w["fdn0"], "ffn0_down")
    cos, sin = _rope_tables(S)
    G = len(ATTN_GROUPS)
    w_groups = w["qkv"].transpose(1, 0, 2).reshape(D_MODEL, 3, G, GROUP_W)
    a_g, u2 = zip(*[_qkv_dilated(h2, nm1, w_groups[:, :, gi, :].reshape(D_MODEL, 3 * GROUP_W), cos, sin, d)
                    for gi, (_, d) in enumerate(ATTN_GROUPS)])
    o_g, lse_g = zip(*[_attn_fwd(a) for a in a_g])
    oa = _attn_merge(o_g, lse_g)
    h3 = _mm_res(h2, oa, w["aout"], "attn_out")
    z1, u3, _ = _norm_mm(h3, nf1, w["fin1"], "ffn1_in", out_dtype=BF16)
    h4, act1 = _swiglu_mm_res(h3, z1, w["fdn1"], "ffn1_down")
    dh4, loss, d_final = _loss_head(h4, final_norm, target)

    grads, small = {}, {"final_norm": d_final}

    def ffn_bwd(dh, h_in, u_in, z, act, gain, w_in, w_dn, tag, ride=None):
        dz = _mm_nt_swiglu_bwd(dh, w_dn, z, tag + "_down_dx")
        g_dn = _mm_tn(act, dh, 1, D_MODEL, D_MODEL, tag + "_down_dw")[0]
        g_in = _mm_tn(u_in, dz, N_CHIPS, fin_tn, fin_tn, tag + "_in_dw")
        rider = None if ride is None else ride(g_in, g_dn)
        dh_in, dgain, got = _mm_nt_normbwd(dz, w_in, h_in, gain, dh, tag + "_in_dx", rider=rider)
        return dh_in, dgain, g_in, g_dn, got

    dh3, d_nf1, grads["fin1"], grads["fdn1"], _ = ffn_bwd(dh4, h3, u3, z1, act1, nf1, w["fin1"], w["fdn1"], "ffn1")
    doa = _mm_nt(dh3, w["aout"][None], "attn_out_dx")
    grads["aout"] = _mm_tn(oa, dh3, 1, D_MODEL, D_MODEL, "attn_out_dw")[0]
    merged = _attn_merge_bwd(o_g, lse_g, doa)
    G = len(ATTN_GROUPS)
    das = [_attn_bwd(a_g[gi], merged[gi], lse_g[gi], merged[G + gi]) for gi in range(G)]
    dqkv = None
    for gi in range(G):
        dqkv = _undilate_group(das[gi], dqkv, cos, sin, gi)
    n_qkv = w["qkv"].shape[2]
    grads["qkv"] = _mm_tn(u2[0], dqkv, N_CHIPS, n_qkv, n_qkv, "attn_qkv_dw")
    dh2, d_nm1, _ = _mm_nt_normbwd(dqkv, w["qkv"], h2, nm1, dh3, "attn_qkv_dx")

    def ride_early(g_in, g_dn):
        return comm.pair_rider({**grads, "fin0": g_in, "fdn0": g_dn}, "early")

    dh1, d_nf0, _, _, got = ffn_bwd(dh2, h1, u1, z0, act0, nf0, w["fin0"], w["fdn0"], "ffn0", ride=ride_early)
    comm.paired("early", got)
    dog = _mm_nt(dh1, w["hout"][None], "hgrn_out_dx")
    (dproj, dlb, dgn), got = _hgrn_bwd(proj, lb_logits, out_gain, o, states, dog, rider=comm.exchange_rider("early"))
    comm.exchanged("early", got)
    late = {"hout": _mm_tn(og, dh1, 1, D_MODEL, D_MODEL, "hgrn_out_dw")[0],
            "hin": _mm_tn(u0, dproj, N_CHIPS, D_MODEL, D_MODEL, "hgrn_in_dw")}
    comm.pair_now(late, "late")
    dx, d_nm0, got = _mm_nt_normbwd(dproj, w["hin"], x, nm0, dh1, "hgrn_in_dx", rider=comm.exchange_rider("late"))
    comm.exchanged("late", got)

    small["norm_mix"] = jnp.concatenate([d_nm0, d_nm1], axis=0)
    small["norm_ffn"] = jnp.concatenate([d_nf0, d_nf1], axis=0)
    small["lb"] = dlb.reshape(1, HGRN_HEADS * HEAD)
    small["out_norm"] = dgn.reshape(HGRN_HEADS, HEAD)
    return loss, dx, small


def _place():
    x, y, c = lax.axis_index("x"), lax.axis_index("y"), lax.axis_index("c")
    others = [(1 - x, y), (x, 1 - y), (1 - x, 1 - y)]
    return x, y, c, others


ANY = pl.BlockSpec(memory_space=pl.ANY)


class _GatherRider:
    def __init__(self, shards):
        self.operands = list(shards)
        n = self.n = len(shards)
        self.out_shape = [jax.ShapeDtypeStruct((N_CHIPS,) + s.shape, s.dtype) for s in shards]
        self.scratch = [pltpu.SemaphoreType.DMA((3 * n,)), pltpu.SemaphoreType.DMA((3 * n,)),
                        pltpu.SemaphoreType.DMA((3 * n,)), pltpu.SemaphoreType.DMA((3 * n,)),
                        pltpu.SemaphoreType.DMA((n,)), pltpu.SemaphoreType.DMA((n,))]

    def _copies(self, ins, outs, sems):
        ici_send, ici_recv, _, _, own_send, own_recv = sems
        x, y, c, others = _place()
        me = 2 * x + y
        own = [pltpu.make_async_remote_copy(
            src_ref=ins[a], dst_ref=outs[a].at[me], send_sem=own_send.at[a], recv_sem=own_recv.at[a],
            device_id=(x, y, 1 - c), device_id_type=MESH) for a in range(self.n)]
        sends = [pltpu.make_async_remote_copy(
            src_ref=ins[a].at[c], dst_ref=outs[a].at[me, c], send_sem=ici_send.at[a * 3 + k], recv_sem=ici_recv.at[a * 3 + k],
            device_id=(ox, oy, c), device_id_type=MESH) for a in range(self.n) for k, (ox, oy) in enumerate(others)]
        return own, sends

    def start(self, ins, outs, sems):
        own, sends = self._copies(ins, outs, sems)
        for cp in own + sends:
            cp.start()

    def finish(self, ins, outs, sems):
        ici_send, ici_recv, d2d_send, d2d_recv, _, _ = sems
        x, y, c, others = _place()
        sibling = (x, y, 1 - c)
        own, sends = self._copies(ins, outs, sems)
        passes = []
        for a in range(self.n):
            for k, (ox, oy) in enumerate(others):
                s = a * 3 + k
                got = outs[a].at[2 * ox + oy, c]
                pltpu.make_async_remote_copy(
                    src_ref=got, dst_ref=got, send_sem=ici_send.at[s], recv_sem=ici_recv.at[s],
                    device_id=(ox, oy, c), device_id_type=MESH).wait_recv()
                fwd = pltpu.make_async_remote_copy(
                    src_ref=got, dst_ref=got, send_sem=d2d_send.at[s], recv_sem=d2d_recv.at[s],
                    device_id=sibling, device_id_type=MESH)
                fwd.start()
                passes.append(fwd)
        for a in range(self.n):
            for k, (ox, oy) in enumerate(others):
                s = a * 3 + k
                theirs = outs[a].at[2 * ox + oy, 1 - c]
                pltpu.make_async_remote_copy(
                    src_ref=theirs, dst_ref=theirs, send_sem=d2d_send.at[s], recv_sem=d2d_recv.at[s],
                    device_id=sibling, device_id_type=MESH).wait_recv()
        for cp in own:
            cp.wait()
        for cp in sends + passes:
            cp.wait_send()


class _PairRider:
    def __init__(self, grads):
        self.operands = list(grads)
        n = self.n = len(grads)
        self.out_shape = [jax.ShapeDtypeStruct((N_CHIPS,) + g.shape[2:], F32) for g in grads]
        self.scratch = [pltpu.SemaphoreType.DMA((N_CHIPS * n,)), pltpu.SemaphoreType.DMA((N_CHIPS * n,))]

    def _copies(self, ins, outs, sems):
        send_sem, recv_sem = sems
        x, y, c, _ = _place()
        return [pltpu.make_async_remote_copy(
            src_ref=ins[a].at[j, 1 - c], dst_ref=outs[a].at[j], send_sem=send_sem.at[a * N_CHIPS + j],
            recv_sem=recv_sem.at[a * N_CHIPS + j], device_id=(x, y, 1 - c), device_id_type=MESH)
            for a in range(self.n) for j in range(N_CHIPS)]

    def start(self, ins, outs, sems):
        for cp in self._copies(ins, outs, sems):
            cp.start()

    def finish(self, ins, outs, sems):
        for cp in self._copies(ins, outs, sems):
            cp.wait()


class _ExchangeRider:
    def __init__(self, parts):
        self.operands = list(parts)
        n = self.n = len(parts)
        self.out_shape = [jax.ShapeDtypeStruct(p.shape, p.dtype) for p in parts]
        self.scratch = [pltpu.SemaphoreType.DMA((3 * n,)), pltpu.SemaphoreType.DMA((3 * n,))]

    def _copies(self, ins, outs, sems):
        send_sem, recv_sem = sems
        x, y, c, others = _place()
        me = 2 * x + y
        return [pltpu.make_async_remote_copy(
            src_ref=ins[a].at[2 * ox + oy], dst_ref=outs[a].at[me], send_sem=send_sem.at[a * 3 + k],
            recv_sem=recv_sem.at[a * 3 + k], device_id=(ox, oy, c), device_id_type=MESH)
            for a in range(self.n) for k, (ox, oy) in enumerate(others)]

    def start(self, ins, outs, sems):
        for cp in self._copies(ins, outs, sems):
            cp.start()

    def finish(self, ins, outs, sems):
        send_sem, recv_sem = sems
        x, y, c, others = _place()
        for a in range(self.n):
            for k, (ox, oy) in enumerate(others):
                s = a * 3 + k
                got = outs[a].at[2 * ox + oy]
                pltpu.make_async_remote_copy(
                    src_ref=got, dst_ref=got, send_sem=send_sem.at[s], recv_sem=recv_sem.at[s],
                    device_id=(ox, oy, c), device_id_type=MESH).wait_recv()
        for cp in self._copies(ins, outs, sems):
            cp.wait_send()


def _run_rider(rider, name):
    n = rider.n

    def body(*refs):
        ins, outs, sems = refs[:n], refs[n:2 * n], refs[2 * n:]
        rider.start(ins, outs, sems)
        rider.finish(ins, outs, sems)

    return pl.pallas_call(
        body, name=name, in_specs=[ANY] * n, out_specs=[ANY] * n,
        out_shape=rider.out_shape, scratch_shapes=rider.scratch)(*rider.operands)


def _ride(rider, body, n_in, n_out, first, last):
    if rider is None:
        return body
    n = rider.n

    def wrapped(*refs):
        host_in, r_in = refs[:n_in], refs[n_in:n_in + n]
        host_out = refs[n_in + n:n_in + n + n_out]
        r_out = refs[n_in + n + n_out:n_in + 2 * n + n_out]
        rest = refs[n_in + 2 * n + n_out:]
        host_scr, sems = rest[:len(rest) - len(rider.scratch)], rest[len(rest) - len(rider.scratch):]

        @pl.when(first())
        def _():
            rider.start(r_in, r_out, sems)

        body(*host_in, *host_out, *host_scr)

        @pl.when(last())
        def _():
            rider.finish(r_in, r_out, sems)

    return wrapped


def _rider_args(rider):
    if rider is None:
        return [], [], [], [], []
    return rider.operands, [ANY] * rider.n, [ANY] * rider.n, rider.out_shape, rider.scratch


def _pair_sum(g, got, c_idx):
    _, _, r, cw = g.shape
    tr = _row_tile(r, cw)

    def body(c_ref, g_ref, got_ref, pb_ref):
        pb_ref[...] = (g_ref[...] + got_ref[...]).astype(BF16)

    blk = pl.BlockSpec((None, tr, cw), lambda j, i, c_ref: (j, i, 0))
    return pl.pallas_call(
        body, name="grad_pair_sum",
        grid_spec=pltpu.PrefetchScalarGridSpec(
            num_scalar_prefetch=1, grid=(N_CHIPS, r // tr),
            in_specs=[pl.BlockSpec((None, None, tr, cw), lambda j, i, c_ref: (j, c_ref[0], i, 0)), blk],
            out_specs=blk),
        out_shape=jax.ShapeDtypeStruct((N_CHIPS, r, cw), BF16),
        compiler_params=_params(("parallel", "parallel")))(c_idx, g, got)


def _chip_sum(g, sib, got, place):
    _, _, r, cw = g.shape
    tr = _row_tile(r, cw)

    def body(place_ref, g_ref, sib_ref, got_ref, t_ref):
        me = place_ref[0]
        own = g_ref[...] + sib_ref[...]
        acc = None
        for s in range(N_CHIPS):
            term = jnp.where(me == s, own, got_ref[s].astype(F32))
            acc = term if acc is None else acc + term
        t_ref[...] = acc

    return pl.pallas_call(
        body, name="grad_chip_sum",
        grid_spec=pltpu.PrefetchScalarGridSpec(
            num_scalar_prefetch=1, grid=(r // tr,),
            in_specs=[pl.BlockSpec((None, None, tr, cw), lambda i, pr: (pr[0], pr[1], i, 0)),
                      pl.BlockSpec((None, tr, cw), lambda i, pr: (pr[0], i, 0)),
                      pl.BlockSpec((N_CHIPS, tr, cw), lambda i, pr: (0, i, 0))],
            out_specs=pl.BlockSpec((tr, cw), lambda i, pr: (i, 0))),
        out_shape=jax.ShapeDtypeStruct((r, cw), F32),
        compiler_params=_params(("parallel",)))(place, g, sib, got)


def _pair_share(halves):
    n = len(halves)

    def body(*refs):
        ins, outs = refs[:n], refs[n:2 * n]
        send_sem, recv_sem = refs[2 * n:]
        x, y, c, _ = _place()
        cps = [pltpu.make_async_remote_copy(
            src_ref=ins[a], dst_ref=outs[a], send_sem=send_sem.at[a], recv_sem=recv_sem.at[a],
            device_id=(x, y, 1 - c), device_id_type=MESH) for a in range(n)]
        for cp in cps:
            cp.start()
        for cp in cps:
            cp.wait()

    return pl.pallas_call(
        body, name="grad_pair_share",
        in_specs=[ANY] * n, out_specs=[ANY] * n,
        out_shape=[jax.ShapeDtypeStruct(h.shape, F32) for h in halves],
        scratch_shapes=[pltpu.SemaphoreType.DMA((n,)), pltpu.SemaphoreType.DMA((n,))],
        )(*halves)


def _small_allreduce(pack):
    m_per, ncol = pack.shape
    n_dev = 8

    def body(x_ref, sum_ref, all_ref, send_sems, recv_sems, local_sem):
        x, y, c, others = _place()
        me, sibling = (x, y, c), (x, y, 1 - c)

        def rows(px, py, pc):
            return all_ref.at[pl.ds((4 * px + 2 * py + pc) * m_per, m_per), :]

        def copy(k, block, to, src=None):
            return pltpu.make_async_remote_copy(
                src_ref=rows(*block) if src is None else src, dst_ref=rows(*block),
                send_sem=send_sems.at[k], recv_sem=recv_sems.at[k], device_id=to, device_id_type=MESH)

        mine = pltpu.make_async_copy(x_ref, rows(*me), local_sem)
        mine.start()
        first = [copy(0, me, sibling, src=x_ref)]
        first += [copy(1 + j, me, (*chip, c), src=x_ref) for j, chip in enumerate(others)]
        for cp in first:
            cp.start()
        passed = [copy(4 + j, (*chip, c), sibling) for j, chip in enumerate(others)]
        for j, chip in enumerate(others):
            copy(1 + j, (*chip, c), me).wait_recv()
            passed[j].start()
        copy(0, sibling, me).wait_recv()
        for j, chip in enumerate(others):
            copy(4 + j, (*chip, 1 - c), me).wait_recv()
        for cp in first + passed:
            cp.wait_send()
        mine.wait()
        acc = all_ref[0:m_per, :]
        for dvc in range(1, n_dev):
            acc = acc + all_ref[dvc * m_per:(dvc + 1) * m_per, :]
        sum_ref[...] = acc

    return pl.pallas_call(
        body, name="small_allreduce",
        in_specs=[pl.BlockSpec(memory_space=pltpu.VMEM)],
        out_specs=pl.BlockSpec(memory_space=pltpu.VMEM),
        out_shape=jax.ShapeDtypeStruct((m_per, ncol), F32),
        scratch_shapes=[pltpu.VMEM((n_dev * m_per, ncol), F32),
                        pltpu.SemaphoreType.DMA((7,)), pltpu.SemaphoreType.DMA((7,)), pltpu.SemaphoreType.DMA],
        )(pack)


def _adam_math(w, g, m, v):
    m = ADAM_B1 * m + (1.0 - ADAM_B1) * g
    v = ADAM_B2 * v + (1.0 - ADAM_B2) * (g * g)
    m_hat = m / (1.0 - ADAM_B1 ** ADAM_STEP)
    v_hat = v / (1.0 - ADAM_B2 ** ADAM_STEP)
    delta = -ADAM_LR * (m_hat / (jnp.sqrt(v_hat) + ADAM_EPS) + ADAM_WD * w)
    return delta, m, v


def _adamw(halves, c_idx, w, m, v, name):
    L = len(halves)
    r, C = halves[0][0].shape
    tr = _row_tile(r, C, 1024 * 1024)
    nt = r // tr

    def body(c_ref, *refs):
        g_refs, (w_ref, m_ref, v_ref), (g_ref, d_ref, nm_ref, nv_ref) = refs[:2 * L], refs[2 * L:2 * L + 3], refs[2 * L + 3:]
        own = pl.program_id(1) == c_ref[0]
        g = None
        for l in range(L):
            cand = jnp.where(own, g_refs[2 * l][...], g_refs[2 * l + 1][...])
            g = cand if g is None else jnp.where(pl.program_id(0) == l, cand, g)
        g_ref[...] = g
        d_ref[...], nm_ref[...], nv_ref[...] = _adam_math(w_ref[...], g, m_ref[...], v_ref[...])

    def half(l, mine):
        def index(ll, h, i, c_ref):
            read = (h == c_ref[0]) if mine else (h != c_ref[0])
            return jnp.where(jnp.logical_and(ll == l, read), i, 0), 0
        return pl.BlockSpec((tr, C), index)

    full = pl.BlockSpec((None, tr, C), lambda ll, h, i, c_ref: (ll, h * nt + i, 0))
    shp = jax.ShapeDtypeStruct((L, 2 * r, C), F32)
    g_specs = [half(l, mine) for l in range(L) for mine in (True, False)]
    return pl.pallas_call(
        body, name=name,
        grid_spec=pltpu.PrefetchScalarGridSpec(
            num_scalar_prefetch=1, grid=(L, 2, nt),
            in_specs=g_specs + [full] * 3, out_specs=[full] * 4),
        out_shape=[shp] * 4,
        compiler_params=_params(("arbitrary", "arbitrary", "arbitrary")))(
            c_idx, *[a for pair in halves for a in pair], w, m, v)


def _small_update(gsum, logits_pack, w, m, v):
    def body(gs_ref, lg_ref, w_ref, m_ref, v_ref, g_ref, d_ref, nm_ref, nv_ref):
        g_ref[...] = gs_ref[...]
        l0, l1, l2 = lg_ref[0:1, :], lg_ref[1:2, :], lg_ref[2:3, :]
        mx = jnp.maximum(jnp.maximum(l0, l1), l2)
        e0, e1, e2 = jnp.exp(l0 - mx), jnp.exp(l1 - mx), jnp.exp(l2 - mx)
        tot = e0 + e1 + e2
        p0, p1, p2 = e0 / tot, e1 / tot, e2 / tot
        dlb = gs_ref[4:5, :]
        g_ref[4:5, :] = dlb * p0 * (1.0 - p0)
        g_ref[5:6, :] = -dlb * p0 * p1
        g_ref[6:7, :] = -dlb * p0 * p2
        d_ref[...], nm_ref[...], nv_ref[...] = _adam_math(w_ref[...], g_ref[...], m_ref[...], v_ref[...])

    full = pl.BlockSpec(memory_space=pltpu.VMEM)
    shp = jax.ShapeDtypeStruct(gsum.shape, F32)
    return pl.pallas_call(
        body, name="small_update", in_specs=[full] * 5, out_specs=[full] * 4, out_shape=[shp] * 4)(
            gsum, logits_pack, w, m, v)


def _pack_small(norm_mix, norm_ffn, lb3, out_norm, final_norm, extra=None):
    ncol = norm_mix.shape[1]
    on = jnp.pad(out_norm.reshape(1, -1), ((0, 0), (0, ncol - out_norm.size)))
    rows = [norm_mix, norm_ffn, lb3, on, final_norm.reshape(1, ncol)]
    if extra is not None:
        rows.append(extra)
    used = sum(r.shape[0] for r in rows)
    rows.append(jnp.zeros((SMALL_ROWS - used, ncol), F32))
    return jnp.concatenate(rows, axis=0)


WEIGHT_NAMES = ("hin", "hout", "qkv", "aout", "fin0", "fin1", "fdn0", "fdn1")
FIRST_WEIGHTS = ("hin",)
LATE_WEIGHTS_A = ("hout", "fin0", "fdn0")
LATE_WEIGHTS_B = ("qkv", "aout", "fin1", "fdn1")


def _split_weights(hgrn_w_in, hgrn_w_out, attn_w_qkv, attn_w_out, ffn_w_in, ffn_w_down):
    return {"hin": hgrn_w_in[0], "hout": hgrn_w_out[0], "qkv": attn_w_qkv[0], "aout": attn_w_out[0],
            "fin0": ffn_w_in[0], "fin1": ffn_w_in[1], "fdn0": ffn_w_down[0], "fdn1": ffn_w_down[1]}


def _halves(v):
    r, c = v.shape
    return v.reshape(2, r // 2, c)


def _full_weights(gathered):
    out = {}
    for k, g in gathered.items():
        _, _, r, c = g.shape
        if k in ("hin", "qkv", "fin0", "fin1"):
            out[k] = g.reshape(N_CHIPS, 2 * r, c)
        else:
            out[k] = g.reshape(N_CHIPS * 2 * r, c)
    return out


class _StepComm:
    def __init__(self, shards, c_idx, me_idx):
        self.shards, self.c_idx, self.me_idx = shards, c_idx, me_idx
        self.halves = {}
        self._stage = {}

    def gather_rider(self, names):
        return _GatherRider([_halves(self.shards[k].astype(BF16)) for k in names])

    def gathered(self, names, got):
        return _full_weights(dict(zip(names, got)))

    def first_weights(self):
        return self.gathered(FIRST_WEIGHTS, _run_rider(self.gather_rider(FIRST_WEIGHTS), "gather_first"))

    def pair_rider(self, grads, tag):
        names = list(grads)
        g4 = []
        for k in names:
            r, c = self.shards[k].shape
            g4.append(grads[k].reshape(N_CHIPS, 2, r // 2, c))
        self._stage[tag] = (names, g4)
        return _PairRider(g4)

    def pair_now(self, grads, tag):
        self.paired(tag, _run_rider(self.pair_rider(grads, tag), "grad_pair_exchange_" + tag))

    def paired(self, tag, got):
        names, g4 = self._stage[tag]
        self._stage[tag] = (names, [(g, s, _pair_sum(g, s, self.c_idx)) for g, s in zip(g4, got)])

    def exchange_rider(self, tag):
        return _ExchangeRider([s[2] for s in self._stage[tag][1]])

    def exchanged(self, tag, got):
        names, sums = self._stage.pop(tag)
        place = jnp.concatenate([self.me_idx, self.c_idx])
        for k, (g, sib, _), recv in zip(names, sums, got):
            self.halves[k] = _chip_sum(g, sib, recv, place)

    def shared_halves(self):
        mine = [self.halves[k] for k in WEIGHT_NAMES]
        return dict(zip(WEIGHT_NAMES, zip(mine, _pair_share(mine))))


def kernel(x, norm_mix, norm_ffn, hgrn_w_in, hgrn_lb_logits, hgrn_out_norm, hgrn_w_out, attn_w_qkv, attn_w_out, ffn_w_in, ffn_w_down, final_norm, loss_target, m_norm_mix, m_norm_ffn, m_hgrn_w_in, m_hgrn_lb_logits, m_hgrn_out_norm, m_hgrn_w_out, m_attn_w_qkv, m_attn_w_out, m_ffn_w_in, m_ffn_w_down, m_final_norm, v_norm_mix, v_norm_ffn, v_hgrn_w_in, v_hgrn_lb_logits, v_hgrn_out_norm, v_hgrn_w_out, v_attn_w_qkv, v_attn_w_out, v_ffn_w_in, v_ffn_w_down, v_final_norm):
    S = x.shape[1]
    xi, yi, ci = lax.axis_index("x"), lax.axis_index("y"), lax.axis_index("c")
    c_idx = jnp.reshape(ci, (1,)).astype(jnp.int32)
    me_idx = jnp.reshape(2 * xi + yi, (1,)).astype(jnp.int32)

    w_own = _split_weights(hgrn_w_in, hgrn_w_out, attn_w_qkv, attn_w_out, ffn_w_in, ffn_w_down)

    comm = _StepComm(w_own, c_idx, me_idx)
    loss, dx, small = _local_step(
        x.reshape(S, D_MODEL), loss_target.reshape(S, D_MODEL), norm_mix, norm_ffn, hgrn_lb_logits,
        hgrn_out_norm, final_norm.reshape(1, D_MODEL), comm)

    halves = comm.shared_halves()
    updated = {}
    for tensor, layers, (wt, mt, vt) in (
            ("hgrn_w_in", ("hin",), (hgrn_w_in, m_hgrn_w_in, v_hgrn_w_in)),
            ("hgrn_w_out", ("hout",), (hgrn_w_out, m_hgrn_w_out, v_hgrn_w_out)),
            ("attn_w_qkv", ("qkv",), (attn_w_qkv, m_attn_w_qkv, v_attn_w_qkv)),
            ("attn_w_out", ("aout",), (attn_w_out, m_attn_w_out, v_attn_w_out)),
            ("ffn_w_in", ("fin0", "fin1"), (ffn_w_in, m_ffn_w_in, v_ffn_w_in)),
            ("ffn_w_down", ("fdn0", "fdn1"), (ffn_w_down, m_ffn_w_down, v_ffn_w_down))):
        updated[tensor] = _adamw([halves[k] for k in layers], c_idx, wt, mt, vt, "adamw_" + tensor)

    loss_row = jnp.pad(loss, ((0, 0), (0, D_MODEL - loss.shape[1])))
    lb3 = jnp.concatenate([small["lb"], jnp.zeros((2, D_MODEL), F32)], axis=0)
    on_grad = jnp.sum(small["out_norm"], axis=0, keepdims=True)
    pack = _pack_small(small["norm_mix"], small["norm_ffn"], lb3, on_grad, small["final_norm"], loss_row)
    gsum = _small_allreduce(pack)
    w_s = _pack_small(norm_mix, norm_ffn, hgrn_lb_logits, hgrn_out_norm, final_norm)
    m_s = _pack_small(m_norm_mix, m_norm_ffn, m_hgrn_lb_logits, m_hgrn_out_norm, m_final_norm)
    v_s = _pack_small(v_norm_mix, v_norm_ffn, v_hgrn_lb_logits, v_hgrn_out_norm, v_final_norm)
    lg_pack = jnp.pad(hgrn_lb_logits, ((0, 8 - hgrn_lb_logits.shape[0]), (0, 0)))
    sg, sd, sm, sv = _small_update(gsum, lg_pack, w_s, m_s, v_s)

    def unpack(p):
        return (p[0:2], p[2:4], p[4:7], p[7:8, :HEAD], p[8])

    def assemble(p, which):
        nmx, nff, lbl, onm, fnm = unpack(p)
        hin, hout, qkv, aout, fin, fdn = [updated[t][which] for t in
                                          ("hgrn_w_in", "hgrn_w_out", "attn_w_qkv", "attn_w_out", "ffn_w_in", "ffn_w_down")]
        return (nmx, nff, hin, lbl, onm, hout, qkv, aout, fin, fdn, fnm)

    total_loss = gsum[9, 0]
    return (total_loss, dx.reshape(1, S, D_MODEL), *assemble(sg, 0), *assemble(sd, 1), *assemble(sm, 2), *assemble(sv, 3))
```

```python
import functools

import jax
import jax.numpy as jnp
from jax import lax
from jax.experimental import pallas as pl
from jax.experimental.pallas import tpu as pltpu

F32 = jnp.float32
BF16 = jnp.bfloat16
MESH = pl.DeviceIdType.MESH

D_MODEL = 1024
HEAD = 128
HGRN_HEADS = 8
HGRN_CHUNK = 64
HGRN_HEADS_PER_STEP = 2
ATTN_GROUPS = ((128, 1), (512, 4), (2048, 16))
ATTN_SPAN = 128
HEADS_PER_GROUP = 4
GROUP_W = HEADS_PER_GROUP * HEAD
D_FF = 2816
NORM_EPS = 1e-6
ROPE_THETA = 10000.0
NEG = -1e30

ADAM_LR, ADAM_B1, ADAM_B2, ADAM_EPS, ADAM_WD, ADAM_STEP = 0.001, 0.9, 0.999, 1e-08, 0.01, 10

N_CHIPS = 4
VMEM_LIMIT = 56 * 1024 * 1024
SMALL_ROWS = 16


def _params(sem=None):
    return pltpu.CompilerParams(dimension_semantics=sem, vmem_limit_bytes=VMEM_LIMIT)


def _row_tile(rows, cols, budget_bytes=3 * 512 * 1024):
    best = 8
    for t in range(8, rows + 1, 8):
        if rows % t == 0 and t * cols * 4 <= budget_bytes:
            best = t
    assert rows % best == 0
    return best


def _grid_corner(i, j):
    return jnp.logical_and(pl.program_id(0) == i, pl.program_id(1) == j)


def _sigmoid(v):
    return 0.5 * jnp.tanh(0.5 * v) + 0.5


def _dot(a, b):
    return jnp.dot(a, b, preferred_element_type=F32)


def _dot_nt(a, b):
    return lax.dot_general(a, b, (((1,), (1,)), ((), ())), preferred_element_type=F32)


def _dot_tn(a, b):
    return lax.dot_general(a, b, (((0,), (0,)), ((), ())), preferred_element_type=F32)


def _dot_exact(ones, b):
    ones = ones.astype(BF16)
    hi = b.astype(BF16)
    rest = b - hi.astype(F32)
    mid = rest.astype(BF16)
    low = (rest - mid.astype(F32)).astype(BF16)
    return _dot(ones, hi) + _dot(ones, mid) + _dot(ones, low)


def _rstd(v):
    return lax.rsqrt(jnp.mean(v * v, axis=-1, keepdims=True) + NORM_EPS)


def _norm_mm(h, gain, w3, name, out_dtype=F32, tm=2048, rider=None):
    S, K = h.shape
    J, _, n = w3.shape
    gi = S // tm

    def body(h_ref, g_ref, w_ref, y_ref, u_ref):
        @pl.when(pl.program_id(1) == 0)
        def _():
            v = h_ref[...]
            u_ref[...] = (v * _rstd(v) * g_ref[...]).astype(BF16)

        y_ref[...] = _dot(u_ref[...], w_ref[pl.program_id(1)]).astype(y_ref.dtype)

    r_ops, r_in, r_out, r_shape, r_scr = _rider_args(rider)
    res = pl.pallas_call(
        _ride(rider, body, 3, 2, functools.partial(_grid_corner, 0, 0), functools.partial(_grid_corner, gi - 1, J - 1)),
        name=name, grid=(gi, J),
        in_specs=[pl.BlockSpec((tm, K), lambda i, j: (i, 0)),
                  pl.BlockSpec((1, K), lambda i, j: (0, 0)),
                  pl.BlockSpec((J, K, n), lambda i, j: (0, 0, 0), pipeline_mode=pl.Buffered(1))] + r_in,
        out_specs=[pl.BlockSpec((tm, n), lambda i, j: (i, j)), pl.BlockSpec((tm, K), lambda i, j: (i, 0))] + r_out,
        out_shape=[jax.ShapeDtypeStruct((S, J * n), out_dtype), jax.ShapeDtypeStruct((S, K), BF16)] + r_shape,
        scratch_shapes=r_scr,
        compiler_params=_params(("arbitrary", "arbitrary")))(h, gain, w3, *r_ops)
    return res[0], res[1], res[2:]


def _mm_res(h, a, w2, name, tm=1024):
    S, N = h.shape
    K = a.shape[1]

    def body(h_ref, a_ref, w_ref, o_ref):
        o_ref[...] = h_ref[...] + _dot(a_ref[...], w_ref[...])

    return pl.pallas_call(
        body, name=name, grid=(S // tm,),
        in_specs=[pl.BlockSpec((tm, N), lambda i: (i, 0)),
                  pl.BlockSpec((tm, K), lambda i: (i, 0)),
                  pl.BlockSpec((K, N), lambda i: (0, 0))],
        out_specs=pl.BlockSpec((tm, N), lambda i: (i, 0)),
        out_shape=jax.ShapeDtypeStruct((S, N), F32),
        compiler_params=_params(("parallel",)))(h, a, w2)


def _swiglu(z_ref, F):
    g = z_ref[:, :F].astype(F32)
    return (g * _sigmoid(g) * z_ref[:, F:].astype(F32)).astype(BF16)


def _swiglu_mm_res(h, z, w2, name, tm=512):
    S, N = h.shape
    F = w2.shape[0]

    def body(h_ref, z_ref, w_ref, o_ref, a_ref):
        a = _swiglu(z_ref, F)
        a_ref[...] = a
        o_ref[...] = h_ref[...] + _dot(a, w_ref[...])

    return pl.pallas_call(
        body, name=name, grid=(S // tm,),
        in_specs=[pl.BlockSpec((tm, N), lambda i: (i, 0)),
                  pl.BlockSpec((tm, 2 * F), lambda i: (i, 0)),
                  pl.BlockSpec((F, N), lambda i: (0, 0), pipeline_mode=pl.Buffered(1))],
        out_specs=[pl.BlockSpec((tm, N), lambda i: (i, 0)), pl.BlockSpec((tm, F), lambda i: (i, 0))],
        out_shape=[jax.ShapeDtypeStruct((S, N), F32), jax.ShapeDtypeStruct((S, F), BF16)],
        compiler_params=_params(("parallel",)))(h, z, w2)


def _dy_specs(dy, J, n, tm):
    if dy.ndim == 3:
        return [pl.BlockSpec((None, tm, n), functools.partial(lambda i, j: (j, i, 0), j=j)) for j in range(J)]
    return [pl.BlockSpec((tm, n), functools.partial(lambda i, j: (i, j), j=j)) for j in range(J)]


def _acc_nt(dy_refs, w_ref):
    acc = None
    for j, r in enumerate(dy_refs):
        t = _dot_nt(r[...].astype(BF16), w_ref[j])
        acc = t if acc is None else acc + t
    return acc


def _mm_nt(dy, w3, name, out_dtype=F32, tm=1024):
    J, K, n = w3.shape
    S = dy.shape[-2]

    def body(*refs):
        dy_refs, w_ref, o_ref = refs[:J], refs[J], refs[J + 1]
        o_ref[...] = _acc_nt(dy_refs, w_ref).astype(o_ref.dtype)

    return pl.pallas_call(
        body, name=name, grid=(S // tm,),
        in_specs=_dy_specs(dy, J, n, tm) + [pl.BlockSpec((J, K, n), lambda i: (0, 0, 0))],
        out_specs=pl.BlockSpec((tm, K), lambda i: (i, 0)),
        out_shape=jax.ShapeDtypeStruct((S, K), out_dtype),
        compiler_params=_params(("parallel",)))(*([dy] * J), w3)


def _mm_nt_normbwd(dy, w3, h, gain, dh, name, tm=512, rider=None):
    J, K, n = w3.shape
    S = h.shape[0]
    steps = S // tm

    def body(*refs):
        dy_refs, w_ref, h_ref, g_ref, dh_ref, o_ref, dg_ref = refs[:J], *refs[J:]
        du = _acc_nt(dy_refs, w_ref)
        v = h_ref[...]
        r = _rstd(v)
        xh = v * r
        dyg = du * g_ref[...]
        o_ref[...] = dh_ref[...] + r * (dyg - xh * jnp.mean(dyg * xh, axis=-1, keepdims=True))

        @pl.when(pl.program_id(0) == 0)
        def _():
            dg_ref[...] = jnp.zeros_like(dg_ref)

        dg_ref[...] += jnp.sum(du * xh, axis=0, keepdims=True)

    row = pl.BlockSpec((tm, K), lambda i: (i, 0))
    vec = pl.BlockSpec((1, K), lambda i: (0, 0))
    r_ops, r_in, r_out, r_shape, r_scr = _rider_args(rider)
    res = pl.pallas_call(
        _ride(rider, body, J + 4, 2, lambda: pl.program_id(0) == 0, lambda: pl.program_id(0) == steps - 1),
        name=name, grid=(steps,),
        in_specs=_dy_specs(dy, J, n, tm) + [pl.BlockSpec((J, K, n), lambda i: (0, 0, 0)), row, vec, row] + r_in,
        out_specs=[row, vec] + r_out,
        out_shape=[jax.ShapeDtypeStruct((S, K), F32), jax.ShapeDtypeStruct((1, K), F32)] + r_shape,
        scratch_shapes=r_scr,
        compiler_params=_params(("arbitrary",)))(*([dy] * J), w3, h, gain, dh, *r_ops)
    return res[0], res[1], res[2:]


def _mm_nt_swiglu_bwd(dh, w2, z, name, tm=512, chunks=11):
    F, N = w2.shape
    S = dh.shape[0]
    fc = F // chunks
    assert fc * chunks == F and fc % HEAD == 0

    def body(dh_ref, w_ref, z_ref, o_ref):
        dhb = dh_ref[...].astype(BF16)
        da = [_dot_nt(dhb, w_ref[c * fc:(c + 1) * fc, :]) for c in range(chunks)]
        for c in range(chunks):
            g = z_ref[:, c * fc:(c + 1) * fc].astype(F32)
            u = z_ref[:, F + c * fc:F + (c + 1) * fc].astype(F32)
            sg = _sigmoid(g)
            o_ref[:, c * fc:(c + 1) * fc] = (da[c] * u * (sg * (1.0 + g * (1.0 - sg)))).astype(BF16)
            o_ref[:, F + c * fc:F + (c + 1) * fc] = (da[c] * (g * sg)).astype(BF16)

    return pl.pallas_call(
        body, name=name, grid=(S // tm,),
        in_specs=[pl.BlockSpec((tm, N), lambda i: (i, 0)),
                  pl.BlockSpec((F, N), lambda i: (0, 0), pipeline_mode=pl.Buffered(1)),
                  pl.BlockSpec((tm, 2 * F), lambda i: (i, 0))],
        out_specs=pl.BlockSpec((tm, 2 * F), lambda i: (i, 0)),
        out_shape=jax.ShapeDtypeStruct((S, 2 * F), BF16),
        compiler_params=_params(("parallel",)))(dh, w2, z)


def _mm_tn(x, dy, J, n, tn, name):
    tpn = n // tn
    ts = 2048 if x.shape[1] <= 1536 else 1024
    S, K = x.shape
    if dy.ndim == 3:
        dy_spec = pl.BlockSpec((None, ts, tn), lambda c, s: (c // tpn, s, c % tpn))
    else:
        dy_spec = pl.BlockSpec((ts, tn), lambda c, s: (s, c))

    def body(x_ref, dy_ref, o_ref):
        @pl.when(pl.program_id(1) == 0)
        def _():
            o_ref[...] = jnp.zeros_like(o_ref)

        o_ref[...] += _dot_tn(x_ref[...], dy_ref[...].astype(BF16))

    return pl.pallas_call(
        body, name=name, grid=(J * tpn, S // ts),
        in_specs=[pl.BlockSpec((ts, K), lambda c, s: (s, 0)), dy_spec],
        out_specs=pl.BlockSpec((None, K, tn), lambda c, s: (c // tpn, 0, c % tpn)),
        out_shape=jax.ShapeDtypeStruct((J, K, n), F32),
        compiler_params=_params(("parallel", "arbitrary")))(x, dy)


def _loss_head(h, gain, target, tm=1024):
    S, K = h.shape

    def body(h_ref, g_ref, t_ref, dh_ref, loss_ref, dg_ref):
        v = h_ref[...]
        r = _rstd(v)
        xh = v * r
        g = g_ref[...]
        dy = (xh * g - t_ref[...]) * (1.0 / K)
        dyg = dy * g
        dh_ref[...] = r * (dyg - xh * jnp.mean(dyg * xh, axis=-1, keepdims=True))

        @pl.when(pl.program_id(0) == 0)
        def _():
            loss_ref[...] = jnp.zeros_like(loss_ref)
            dg_ref[...] = jnp.zeros_like(dg_ref)

        part = jnp.sum(jnp.sum(dy * dy, axis=-1, keepdims=True), axis=0, keepdims=True) * (0.5 * K)
        lane = lax.broadcasted_iota(jnp.int32, loss_ref.shape, 1)
        loss_ref[...] += jnp.where(lane == 0, part, 0.0)
        dg_ref[...] += jnp.sum(dy * xh, axis=0, keepdims=True)

    row = pl.BlockSpec((tm, K), lambda i: (i, 0))
    vec = pl.BlockSpec((1, K), lambda i: (0, 0))
    return pl.pallas_call(
        body, name="loss_head", grid=(S // tm,),
        in_specs=[row, vec, row],
        out_specs=[row, pl.BlockSpec((1, HEAD), lambda i: (0, 0)), vec],
        out_shape=[jax.ShapeDtypeStruct((S, K), F32), jax.ShapeDtypeStruct((1, HEAD), F32),
                   jax.ShapeDtypeStruct((1, K), F32)],
        compiler_params=_params(("arbitrary",)))(h, gain, target)


def _lower_bound(lg_ref):
    l0, l1, l2 = lg_ref[0:1, :], lg_ref[1:2, :], lg_ref[2:3, :]
    mx = jnp.maximum(jnp.maximum(l0, l1), l2)
    e0, e1, e2 = jnp.exp(l0 - mx), jnp.exp(l1 - mx), jnp.exp(l2 - mx)
    return e0 / (e0 + e1 + e2)


def _chunks(v, ncb):
    C = HGRN_CHUNK
    return [v[c * C:(c + 1) * C] for c in range(ncb)]


def _rows(parts):
    return jnp.concatenate(parts, axis=0)


def _block_gates(qz, fz, lb, ncb):
    C = HGRN_CHUNK
    row = lax.broadcasted_iota(jnp.int32, (C, C), 0)
    col = lax.broadcasted_iota(jnp.int32, (C, C), 1)
    tri = (col <= row).astype(F32)
    first_half = lax.broadcasted_iota(jnp.int32, (C, HEAD), 0) < C // 2
    sig = _sigmoid(fz)
    fg = lb + (1.0 - lb) * sig
    key = 1.0 - fg
    lg = jnp.log(fg)
    lgs = _chunks(lg, ncb)
    b = _rows([_dot_exact(tri, v) for v in lgs])
    r_c = [jnp.sum(jnp.where(first_half, v, 0.0), axis=0, keepdims=True) for v in lgs]
    bl_c = [jnp.sum(v, axis=0, keepdims=True) for v in lgs]
    r = _rows([jnp.broadcast_to(v, (C, HEAD)) for v in r_c])
    e_br, e_rb = jnp.exp(b - r), jnp.exp(r - b)
    e_b = e_br * _rows([jnp.broadcast_to(jnp.exp(v), (C, HEAD)) for v in r_c])
    e_lb = e_rb * _rows([jnp.broadcast_to(jnp.exp(e - v), (C, HEAD)) for e, v in zip(bl_c, r_c)])
    sq = _sigmoid(qz)
    qy = qz * sq
    return sig, fg, key, (e_br, e_rb, e_b, e_lb), bl_c, sq, qy


def _hgrn_fwd(proj, logits, gain, tb=1024, rider=None):
    S = proj.shape[0]
    H, C = HGRN_HEADS, HGRN_CHUNK
    ncb = tb // C

    def one_head(q_ref, f_ref, i_ref, g_ref, lg_ref, gn_ref, o_ref, og_ref, st_ref, state):
        @pl.when(pl.program_id(1) == 0)
        def _():
            state[...] = jnp.zeros_like(state)

        lb = _lower_bound(lg_ref)
        causal = lax.broadcasted_iota(jnp.int32, (C, C), 1) <= lax.broadcasted_iota(jnp.int32, (C, C), 0)
        qz, fz, gz = q_ref[...], f_ref[...], g_ref[...]
        _, _, key, (e_br, e_rb, e_b, e_lb), bl_c, _, qy = _block_gates(qz, fz, lb, ncb)
        qs = _chunks((qy * e_br).astype(BF16), ncb)
        ks = _chunks((key * e_rb).astype(BF16), ncb)
        qb = _chunks((qy * e_b).astype(BF16), ncb)
        ke = _chunks((key * e_lb).astype(BF16), ncb)
        vb = _chunks(i_ref[...].astype(BF16), ncb)
        a = [jnp.where(causal, _dot_nt(qs[c], ks[c]), 0.0).astype(BF16) for c in range(ncb)]
        upd = [_dot_tn(vb[c], ke[c]) for c in range(ncb)]
        o_intra = [_dot(a[c], vb[c]) for c in range(ncb)]
        st = state[...]
        e_l = [jnp.exp(v) for v in bl_c]
        sts = []
        for c in range(ncb):
            sts.append(st)
            st = st * e_l[c] + upd[c]
        state[...] = st
        for c in range(ncb):
            st_ref[c] = sts[c]
        o = _rows([_dot_nt(qb[c], sts[c].astype(BF16)) + o_intra[c] for c in range(ncb)])
        o_ref[...] = o
        og_ref[...] = ((o * _rstd(o) * gn_ref[...]) * (gz * _sigmoid(gz))).astype(BF16)

    def body(q_ref, f_ref, i_ref, g_ref, lg_ref, gn_ref, o_ref, og_ref, st_ref, state):
        for hs in range(HP):
            cols = slice(hs * HEAD, (hs + 1) * HEAD)
            one_head(q_ref.at[:, cols], f_ref.at[:, cols], i_ref.at[:, cols], g_ref.at[:, cols], lg_ref.at[:, cols],
                     gn_ref, o_ref.at[:, cols], og_ref.at[:, cols], st_ref.at[hs], state.at[hs])

    HP, wide = HGRN_HEADS_PER_STEP, HGRN_HEADS_PER_STEP * HEAD
    hg = H // HP

    def part(p):
        return pl.BlockSpec((tb, wide), functools.partial(lambda h, i, p: (i, p * hg + h), p=p))

    nb = S // tb
    r_ops, r_in, r_out, r_shape, r_scr = _rider_args(rider)
    res = pl.pallas_call(
        _ride(rider, body, 6, 3, functools.partial(_grid_corner, 0, 0), functools.partial(_grid_corner, hg - 1, nb - 1)),
        name="hgrn_fwd", grid=(hg, nb),
        in_specs=[part(0), part(1), part(2), part(3),
                  pl.BlockSpec((3, wide), lambda h, i: (0, h)),
                  pl.BlockSpec((1, HEAD), lambda h, i: (0, 0))] + r_in,
        out_specs=[pl.BlockSpec((tb, wide), lambda h, i: (i, h)),
                   pl.BlockSpec((tb, wide), lambda h, i: (i, h)),
                   pl.BlockSpec((HP, ncb, HEAD, HEAD), lambda h, i: (h, i, 0, 0))] + r_out,
        out_shape=[jax.ShapeDtypeStruct((S, H * HEAD), F32),
                   jax.ShapeDtypeStruct((S, H * HEAD), BF16),
                   jax.ShapeDtypeStruct((H, S // C, HEAD, HEAD), F32)] + r_shape,
        scratch_shapes=[pltpu.VMEM((HP, HEAD, HEAD), F32)] + r_scr,
        compiler_params=_params(("arbitrary", "arbitrary")))(proj, proj, proj, proj, logits, gain, *r_ops)
    return res[:3], res[3:]


def _hgrn_bwd(proj, logits, gain, o, states, dog, tb=1024, rider=None):
    S = proj.shape[0]
    H, C = HGRN_HEADS, HGRN_CHUNK
    ncb = tb // C
    nb = S // tb

    def one_head(q_ref, f_ref, i_ref, g_ref, lg_ref, gn_ref, o_ref, st_ref, dog_ref,
                 dp_ref, dlb_ref, dgn_ref, dstate, dst_scr):
        @pl.when(pl.program_id(1) == 0)
        def _():
            dstate[...] = jnp.zeros_like(dstate)
            dlb_ref[...] = jnp.zeros_like(dlb_ref)
            dgn_ref[...] = jnp.zeros_like(dgn_ref)

        lb = _lower_bound(lg_ref)
        oml = 1.0 - lb
        gn = gn_ref[...]
        row = lax.broadcasted_iota(jnp.int32, (C, C), 0)
        col = lax.broadcasted_iota(jnp.int32, (C, C), 1)
        causal = col <= row
        tri_up = (col >= row).astype(F32)
        qz, fz, gz = q_ref[...], f_ref[...], g_ref[...]
        sig, fg, key, (e_br, e_rb, e_b, e_lb), bl_c, sq, qy = _block_gates(qz, fz, lb, ncb)
        qs_v, ks_v = (qy * e_br).astype(BF16), (key * e_rb).astype(BF16)
        qb_v, ke_v = (qy * e_b).astype(BF16), (key * e_lb).astype(BF16)
        qs, ks, qb, ke = _chunks(qs_v, ncb), _chunks(ks_v, ncb), _chunks(qb_v, ncb), _chunks(ke_v, ncb)
        vb = _chunks(i_ref[...].astype(BF16), ncb)
        ov = o_ref[...]
        rs = _rstd(ov)
        xh = ov * rs
        sg = _sigmoid(gz)
        dog_v = dog_ref[...]
        dgz = dog_v * (xh * gn) * (sg * (1.0 + gz * (1.0 - sg)))
        don = dog_v * (gz * sg)
        dgn_ref[...] += jnp.sum(don * xh, axis=0, keepdims=True)
        dyg = don * gn
        do = rs * (dyg - xh * jnp.mean(dyg * xh, axis=-1, keepdims=True))
        dob = _chunks(do.astype(BF16), ncb)
        CH = range(ncb)
        a = [jnp.where(causal, _dot_nt(qs[c], ks[c]), 0.0).astype(BF16) for c in CH]
        da = [jnp.where(causal, _dot_nt(dob[c], vb[c]), 0.0).astype(BF16) for c in CH]
        wst = [_dot_tn(dob[c], qb[c]) for c in CH]
        dv_in = [_dot_tn(a[c], dob[c]) for c in CH]
        dqs = [_dot(da[c], ks[c]) for c in CH]
        dks = [_dot_tn(da[c], qs[c]) for c in CH]
        e_l = [jnp.exp(v) for v in bl_c]
        dst = dstate[...]
        for c in reversed(range(ncb)):
            dst_scr[c] = dst
            dst = wst[c] + dst * e_l[c]
        dstate[...] = dst
        dst1b = [dst_scr[c].astype(BF16) for c in CH]
        dqb = [_dot(dob[c], st_ref[c].astype(BF16)) for c in CH]
        dke = [_dot(vb[c], dst1b[c]) for c in CH]
        dv = [dv_in[c] + _dot_nt(ke[c], dst1b[c]) for c in CH]
        dbl_st = [jnp.sum(dst_scr[c] * st_ref[c], axis=0, keepdims=True) * e_l[c] for c in CH]
        dqs, dks, dqb, dke, dv = _rows(dqs), _rows(dks), _rows(dqb), _rows(dke), _rows(dv)
        dke_ke = dke * ke_v.astype(F32)
        db = dqs * qs_v.astype(F32) - dks * ks_v.astype(F32) + dqb * qb_v.astype(F32) - dke_ke
        dlg = []
        for c, (db_c, kk_c) in enumerate(zip(_chunks(db, ncb), _chunks(dke_ke, ncb))):
            dbl = jnp.sum(kk_c, axis=0, keepdims=True) + dbl_st[c]
            dlg.append(_dot_exact(tri_up, db_c) + dbl)
        dlg = _rows(dlg)
        dkey = dks * e_rb + dke * e_lb
        dqy = dqs * e_br + dqb * e_b
        dfg = dlg / fg - dkey
        dlb_ref[...] += jnp.sum(dfg * (1.0 - sig), axis=0, keepdims=True)
        dp_ref[0] = (dqy * (sq * (1.0 + qz * (1.0 - sq)))).astype(BF16)
        dp_ref[1] = (dfg * oml * sig * (1.0 - sig)).astype(BF16)
        dp_ref[2] = dv.astype(BF16)
        dp_ref[3] = dgz.astype(BF16)

    def body(q_ref, f_ref, i_ref, g_ref, lg_ref, gn_ref, o_ref, st_ref, dog_ref,
             dp_ref, dlb_ref, dgn_ref, dstate, dst_scr):
        for hs in range(HP):
            cols = slice(hs * HEAD, (hs + 1) * HEAD)
            one_head(q_ref.at[:, cols], f_ref.at[:, cols], i_ref.at[:, cols], g_ref.at[:, cols], lg_ref.at[:, cols],
                     gn_ref, o_ref.at[:, cols], st_ref.at[hs], dog_ref.at[:, cols],
                     dp_ref.at[:, :, cols], dlb_ref.at[hs], dgn_ref.at[hs], dstate.at[hs], dst_scr)

    HP, wide = HGRN_HEADS_PER_STEP, HGRN_HEADS_PER_STEP * HEAD
    hg = H // HP

    def part(p):
        return pl.BlockSpec((tb, wide), functools.partial(lambda h, i, p: (nb - 1 - i, p * hg + h), p=p))

    blk = pl.BlockSpec((tb, wide), lambda h, i: (nb - 1 - i, h))
    acc = pl.BlockSpec((HP, 1, HEAD), lambda h, i: (h, 0, 0))
    r_ops, r_in, r_out, r_shape, r_scr = _rider_args(rider)
    res = pl.pallas_call(
        _ride(rider, body, 9, 3, functools.partial(_grid_corner, 0, 0), functools.partial(_grid_corner, hg - 1, nb - 1)),
        name="hgrn_bwd", grid=(hg, nb),
        in_specs=[part(0), part(1), part(2), part(3),
                  pl.BlockSpec((3, wide), lambda h, i: (0, h)),
                  pl.BlockSpec((1, HEAD), lambda h, i: (0, 0)),
                  blk,
                  pl.BlockSpec((HP, ncb, HEAD, HEAD), lambda h, i: (h, nb - 1 - i, 0, 0)),
                  blk] + r_in,
        out_specs=[pl.BlockSpec((4, tb, wide), lambda h, i: (0, nb - 1 - i, h)), acc, acc] + r_out,
        out_shape=[jax.ShapeDtypeStruct((4, S, H * HEAD), BF16),
                   jax.ShapeDtypeStruct((H, 1, HEAD), F32),
                   jax.ShapeDtypeStruct((H, 1, HEAD), F32)] + r_shape,
        scratch_shapes=[pltpu.VMEM((HP, HEAD, HEAD), F32), pltpu.VMEM((ncb, HEAD, HEAD), F32)] + r_scr,
        compiler_params=_params(("arbitrary", "arbitrary")))(
            proj, proj, proj, proj, logits, gain, o, states, dog, *r_ops)
    return res[:3], res[3:]


def _rope(v, cos, sin):
    return v * cos + pltpu.roll(v, HEAD // 2, 1) * sin


def _lane_pick(tile, hh):
    lane = lax.broadcasted_iota(jnp.int32, tile.shape, 1)
    return jnp.sum(jnp.where(lane == hh, tile, 0.0), axis=-1, keepdims=True)


def _lane_place(cols):
    rows = cols[0].shape[0]
    lane = lax.broadcasted_iota(jnp.int32, (rows, HEAD), 1)
    tile = jnp.zeros((rows, HEAD), F32)
    for hh, v in enumerate(cols):
        tile = jnp.where(lane == hh, v, tile)
    return tile


def _band_masks():
    qi = lax.broadcasted_iota(jnp.int32, (ATTN_SPAN, ATTN_SPAN), 0)
    kj = lax.broadcasted_iota(jnp.int32, (ATTN_SPAN, ATTN_SPAN), 1)
    return kj <= qi, kj >= qi


ATTN_TILE_BLOCKS = 8


def _attn_fwd(a):
    d, L, _ = a.shape
    B, W = min(ATTN_TILE_BLOCKS, a.shape[1] // ATTN_SPAN), ATTN_SPAN
    T = B * W
    assert L % T == 0
    steps = L // T
    scale = HEAD ** -0.5

    def body(q_ref, kc_ref, kp_ref, vc_ref, vp_ref, o_ref, lse_ref):
        n = pl.program_id(1)
        mask_c, mask_p0 = _band_masks()
        first = jnp.logical_and(mask_p0, n > 0)
        units = [(b, hh) for b in range(B) for hh in range(HEADS_PER_GROUP)]
        rows = [slice(b * W, (b + 1) * W) for b in range(B)]
        cols = [slice(hh * HEAD, (hh + 1) * HEAD) for hh in range(HEADS_PER_GROUP)]

        def prev_keys(ref, tile, b, hh):
            return ref[:, cols[hh]] if b == 0 else tile[rows[b - 1], cols[hh]]

        s_c = [jnp.where(mask_c, _dot_nt(q_ref[rows[b], cols[hh]], kc_ref[rows[b], cols[hh]]) * scale, NEG) for b, hh in units]
        s_p = [jnp.where(first if b == 0 else mask_p0,
                         _dot_nt(q_ref[rows[b], cols[hh]], prev_keys(kp_ref, kc_ref, b, hh)) * scale, NEG) for b, hh in units]
        m = [jnp.maximum(jnp.max(x, axis=-1, keepdims=True), jnp.max(y, axis=-1, keepdims=True)) for x, y in zip(s_c, s_p)]
        p_c = [jnp.exp(x - mm) for x, mm in zip(s_c, m)]
        p_p = [jnp.exp(y - mm) for y, mm in zip(s_p, m)]
        l = [jnp.sum(x, axis=-1, keepdims=True) + jnp.sum(y, axis=-1, keepdims=True) for x, y in zip(p_c, p_p)]
        acc = [_dot(p_c[i].astype(BF16), vc_ref[rows[b], cols[hh]]) + _dot(p_p[i].astype(BF16), prev_keys(vp_ref, vc_ref, b, hh))
               for i, (b, hh) in enumerate(units)]
        for i, (b, hh) in enumerate(units):
            o_ref[rows[b], cols[hh]] = (acc[i] / l[i]).astype(BF16)
        for b in range(B):
            lse_ref[rows[b], :] = _lane_place([m[i] + jnp.log(l[i]) for i, (bb, _) in enumerate(units) if bb == b])

    def cur(part):
        return pl.BlockSpec((None, T, GROUP_W), functools.partial(lambda r, n, p: (r, n, p), p=part))

    def prev(part):
        return pl.BlockSpec((None, W, GROUP_W), functools.partial(lambda r, n, p: (r, jnp.maximum(n * B - 1, 0), p), p=part))

    return pl.pallas_call(
        body, name=f"attn_fwd_d{d}", grid=(d, steps),
        in_specs=[cur(0), cur(1), prev(1), cur(2), prev(2)],
        out_specs=[pl.BlockSpec((None, T, GROUP_W), lambda r, n: (r, n, 0)), pl.BlockSpec((None, T, HEAD), lambda r, n: (r, n, 0))],
        out_shape=[jax.ShapeDtypeStruct((d, L, GROUP_W), BF16), jax.ShapeDtypeStruct((d, L, HEAD), F32)],
        compiler_params=_params(("parallel", "arbitrary")))(a, a, a, a, a)


def _attn_bwd(a, do, lse, dd):
    d, L, _ = a.shape
    B, W = min(ATTN_TILE_BLOCKS, a.shape[1] // ATTN_SPAN), ATTN_SPAN
    T = B * W
    assert L % T == 0
    steps = L // T
    scale = HEAD ** -0.5

    def body(qc_ref, qn_ref, kp_ref, kc_ref, vp_ref, vc_ref, doc_ref, don_ref, lc_ref, ln_ref, ddc_ref, ddn_ref, da_ref):
        n = pl.program_id(1)
        mask_c, mask_p0 = _band_masks()
        first = jnp.logical_and(mask_p0, n > 0)
        last = jnp.logical_and(mask_p0, n < steps - 1)
        H4 = range(HEADS_PER_GROUP)
        units = [(b, hh) for b in range(B) for hh in H4]
        rows = [slice(b * W, (b + 1) * W) for b in range(B)]
        cols = [slice(hh * HEAD, (hh + 1) * HEAD) for hh in H4]
        q = {u: qc_ref[rows[u[0]], cols[u[1]]] for u in units}
        k = {u: kc_ref[rows[u[0]], cols[u[1]]] for u in units}
        v = {u: vc_ref[rows[u[0]], cols[u[1]]] for u in units}
        g_o = {u: doc_ref[rows[u[0]], cols[u[1]]] for u in units}
        kb = {(b, hh): kp_ref[:, cols[hh]] if b == 0 else k[(b - 1, hh)] for b, hh in units}
        vb = {(b, hh): vp_ref[:, cols[hh]] if b == 0 else v[(b - 1, hh)] for b, hh in units}
        lse_t = {(b, hh): _lane_pick(lc_ref[rows[b], :], hh) for b, hh in units}
        dd_t = {(b, hh): _lane_pick(ddc_ref[rows[b], :], hh) for b, hh in units}
        p_c = {u: jnp.where(mask_c, jnp.exp(_dot_nt(q[u], k[u]) * scale - lse_t[u]), 0.0) for u in units}
        p_p = {u: jnp.where(first if u[0] == 0 else mask_p0, jnp.exp(_dot_nt(q[u], kb[u]) * scale - lse_t[u]), 0.0) for u in units}
        ds_c = {u: (p_c[u] * (_dot_nt(g_o[u], v[u]) + dd_t[u])).astype(BF16) for u in units}
        ds_p = {u: (p_p[u] * (_dot_nt(g_o[u], vb[u]) + dd_t[u])).astype(BF16) for u in units}
        qn = [qn_ref[:, c] for c in cols]
        g_n = [don_ref[:, c] for c in cols]
        p_n = [jnp.where(last, jnp.exp(_dot_nt(qn[hh], k[(B - 1, hh)]) * scale - _lane_pick(ln_ref[...], hh)), 0.0) for hh in H4]
        ds_n = [(p_n[hh] * (_dot_nt(g_n[hh], v[(B - 1, hh)]) + _lane_pick(ddn_ref[...], hh))).astype(BF16) for hh in H4]
        dq = {u: (_dot(ds_c[u], k[u]) + _dot(ds_p[u], kb[u])) * scale for u in units}
        dk, dv = {}, {}
        for b, hh in units:
            if b < B - 1:
                nxt = (b + 1, hh)
                dk[(b, hh)] = (_dot_tn(ds_c[(b, hh)], q[(b, hh)]) + _dot_tn(ds_p[nxt], q[nxt])) * scale
                dv[(b, hh)] = _dot_tn(p_c[(b, hh)].astype(BF16), g_o[(b, hh)]) + _dot_tn(p_p[nxt].astype(BF16), g_o[nxt])
            else:
                dk[(b, hh)] = (_dot_tn(ds_c[(b, hh)], q[(b, hh)]) + _dot_tn(ds_n[hh], qn[hh])) * scale
                dv[(b, hh)] = _dot_tn(p_c[(b, hh)].astype(BF16), g_o[(b, hh)]) + _dot_tn(p_n[hh].astype(BF16), g_n[hh])
        for b, hh in units:
            da_ref[rows[b], cols[hh]] = dq[(b, hh)].astype(BF16)
            da_ref[rows[b], GROUP_W + hh * HEAD:GROUP_W + (hh + 1) * HEAD] = dk[(b, hh)].astype(BF16)
            da_ref[rows[b], 2 * GROUP_W + hh * HEAD:2 * GROUP_W + (hh + 1) * HEAD] = dv[(b, hh)].astype(BF16)

    nb = L // W

    def cur(width, part):
        return pl.BlockSpec((None, T, width), functools.partial(lambda r, n, p: (r, n, p), p=part))

    def prev(width, part):
        return pl.BlockSpec((None, W, width), functools.partial(lambda r, n, p: (r, jnp.maximum(n * B - 1, 0), p), p=part))

    def nxt(width, part):
        return pl.BlockSpec((None, W, width), functools.partial(lambda r, n, p: (r, jnp.minimum(n * B + B, nb - 1), p), p=part))

    g = GROUP_W
    return pl.pallas_call(
        body, name=f"attn_bwd_d{d}", grid=(d, steps),
        in_specs=[cur(g, 0), nxt(g, 0), prev(g, 1), cur(g, 1), prev(g, 2), cur(g, 2),
                  cur(g, 0), nxt(g, 0), cur(HEAD, 0), nxt(HEAD, 0), cur(HEAD, 0), nxt(HEAD, 0)],
        out_specs=pl.BlockSpec((None, T, 3 * g), lambda r, n: (r, n, 0)),
        out_shape=jax.ShapeDtypeStruct((d, L, 3 * g), BF16),
        compiler_params=_params(("parallel", "arbitrary")))(
            a, a, a, a, a, a, do, do, lse, lse, dd, dd)


def _softmax3(ls):
    mx = jnp.maximum(jnp.maximum(ls[0], ls[1]), ls[2])
    es = [jnp.exp(v - mx) for v in ls]
    tot = es[0] + es[1] + es[2]
    return [e / tot for e in es]


HEAD_COLS = [slice(hh * HEAD, (hh + 1) * HEAD) for hh in range(HEADS_PER_GROUP)]


def _group_spec(d, tm):
    return pl.BlockSpec((d, tm // d, GROUP_W), lambda i: (0, i, 0))


def _gather_heads(ref, scr, d, tm):
    if d == 1:
        return [ref[0, :, cols].astype(F32) for cols in HEAD_COLS]
    for hh, cols in enumerate(HEAD_COLS):
        for r in range(d):
            scr.at[hh][pl.ds(r, tm // d, stride=d), :] = ref[r, :, cols].astype(F32)
    return [scr[hh] for hh in range(HEADS_PER_GROUP)]


def _tile_spec(d, tm):
    return pl.BlockSpec((d, tm // d, HEAD), lambda i: (0, i, 0))


def _gather_tile(ref, scr, d, tm):
    if d == 1:
        return ref[0]
    for r in range(d):
        scr[pl.ds(r, tm // d, stride=d), :] = ref[r]
    return scr[...]


def _scatter_tile(val, scr, ref, d, tm):
    if d == 1:
        ref[0] = val
        return
    scr[...] = val
    for r in range(d):
        ref[r] = scr[pl.ds(r, tm // d, stride=d), :]


def _scatter_heads(vals, scr, ref, d, tm):
    if d == 1:
        for cols, v in zip(HEAD_COLS, vals):
            ref[0, :, cols] = v.astype(ref.dtype)
        return
    for hh, v in enumerate(vals):
        scr[hh] = v
    for hh, cols in enumerate(HEAD_COLS):
        for r in range(d):
            ref[r, :, cols] = scr.at[hh][pl.ds(r, tm // d, stride=d), :].astype(ref.dtype)


def _qkv_dilated(h, gain, wg, cos, sin, d, tm=2048):
    S, K = h.shape

    def body(h_ref, g_ref, w_ref, cos_ref, sin_ref, out_ref, u_ref, y_scr):
        p = pl.program_id(1)

        @pl.when(p == 0)
        def _():
            v = h_ref[...]
            u_ref[...] = (v * _rstd(v) * g_ref[...]).astype(BF16)

        y = _dot(u_ref[...], w_ref[...])
        heads = [slice(hh * HEAD, (hh + 1) * HEAD) for hh in range(HEADS_PER_GROUP)]
        if d > 1:
            for hh, cols in enumerate(heads):
                y_scr[hh] = y[:, cols]

        def rows_of(hh, r):
            return y[:, heads[hh]] if d == 1 else y_scr.at[hh][pl.ds(r, tm // d, stride=d), :]

        @pl.when(p < 2)
        def _():
            for r in range(d):
                rows = slice(None) if d == 1 else pl.ds(r, tm // d, stride=d)
                cr, sr = cos_ref[rows, :], sin_ref[rows, :]
                for hh, cols in enumerate(heads):
                    out_ref[r, :, cols] = _rope(rows_of(hh, r), cr, sr).astype(BF16)

        @pl.when(p == 2)
        def _():
            for r in range(d):
                for hh, cols in enumerate(heads):
                    out_ref[r, :, cols] = rows_of(hh, r).astype(BF16)

    tab = pl.BlockSpec((tm, HEAD), lambda i, p: (i, 0))
    return pl.pallas_call(
        body, name=f"attn_qkv_d{d}", grid=(S // tm, 3),
        in_specs=[pl.BlockSpec((tm, K), lambda i, p: (i, 0)),
                  pl.BlockSpec((1, K), lambda i, p: (0, 0)),
                  pl.BlockSpec((K, GROUP_W), lambda i, p: (0, p)), tab, tab],
        out_specs=[pl.BlockSpec((d, tm // d, GROUP_W), lambda i, p: (0, i, p)), pl.BlockSpec((tm, K), lambda i, p: (i, 0))],
        out_shape=[jax.ShapeDtypeStruct((d, S // d, 3 * GROUP_W), BF16), jax.ShapeDtypeStruct((S, K), BF16)],
        scratch_shapes=[pltpu.VMEM((HEADS_PER_GROUP, tm, HEAD), F32)],
        compiler_params=_params(("parallel", "arbitrary")))(h, gain, wg, cos, sin)


def _undilate_group(da, dqkv, cos, sin, g, tm=2048):
    d, L, _ = da.shape
    S = d * L
    G = len(ATTN_GROUPS)

    def body(*refs):
        da_ref, cos_ref, sin_ref, out_ref, scr = refs[0], refs[1], refs[2], refs[-2], refs[-1]
        p = pl.program_id(1)
        heads = [slice(hh * HEAD, (hh + 1) * HEAD) for hh in range(HEADS_PER_GROUP)]
        if d > 1:
            for hh, cols in enumerate(heads):
                for r in range(d):
                    scr.at[hh][pl.ds(r, tm // d, stride=d), :] = da_ref[r, :, cols].astype(F32)

        def tokens(hh):
            return da_ref[0, :, heads[hh]].astype(F32) if d == 1 else scr[hh]

        @pl.when(p < 2)
        def _():
            cr, sr = cos_ref[...], -sin_ref[...]
            for hh, cols in enumerate(heads):
                out_ref[:, cols] = _rope(tokens(hh), cr, sr).astype(BF16)

        @pl.when(p == 2)
        def _():
            for hh, cols in enumerate(heads):
                out_ref[:, cols] = tokens(hh).astype(BF16)

    tab = pl.BlockSpec((tm, HEAD), lambda i, p: (i, 0))
    operands = (da, cos, sin) if dqkv is None else (da, cos, sin, dqkv)
    return pl.pallas_call(
        body, name=f"attn_undilate_d{d}", grid=(S // tm, 3),
        in_specs=[pl.BlockSpec((d, tm // d, GROUP_W), lambda i, p: (0, i, p)), tab, tab] + ([] if dqkv is None else [ANY]),
        out_specs=pl.BlockSpec((tm, GROUP_W), lambda i, p: (i, p * G + g)),
        out_shape=jax.ShapeDtypeStruct((S, 3 * G * GROUP_W), BF16),
        input_output_aliases={} if dqkv is None else {3: 0},
        scratch_shapes=[pltpu.VMEM((HEADS_PER_GROUP, tm, HEAD), F32)],
        compiler_params=_params(("parallel", "arbitrary")))(*operands)


def _attn_merge(os_, lses, h, w2, tm=1024):
    G = len(os_)
    S, N = h.shape

    def body(*refs):
        o_refs, l_refs, h_ref, w_ref, res_ref, out_ref = refs[:G], refs[G:2 * G], *refs[2 * G:2 * G + 4]
        scr = refs[2 * G + 4:]
        o = [_gather_heads(o_refs[g], scr[g], d, tm) for g, (_, d) in enumerate(ATTN_GROUPS)]
        l = [_gather_tile(l_refs[g], scr[G + g].at[0], d, tm) for g, (_, d) in enumerate(ATTN_GROUPS)]
        for hh in range(HEADS_PER_GROUP):
            al = _softmax3([_lane_pick(l[g], hh) for g in range(G)])
            for g in range(G):
                out_ref[:, g * GROUP_W + hh * HEAD:g * GROUP_W + (hh + 1) * HEAD] = (o[g][hh] * al[g]).astype(BF16)
        res_ref[...] = h_ref[...] + _dot(out_ref[...], w_ref[...])

    specs = [_group_spec(d, tm) for _, d in ATTN_GROUPS]
    row = pl.BlockSpec((tm, N), lambda i: (i, 0))
    return pl.pallas_call(
        body, name="attn_merge_out", grid=(S // tm,),
        in_specs=specs + [_tile_spec(d, tm) for _, d in ATTN_GROUPS] + [
            row, pl.BlockSpec((G * GROUP_W, N), lambda i: (0, 0), pipeline_mode=pl.Buffered(1))],
        out_specs=[row, pl.BlockSpec((tm, G * GROUP_W), lambda i: (i, 0))],
        out_shape=[jax.ShapeDtypeStruct((S, N), F32), jax.ShapeDtypeStruct((S, G * GROUP_W), BF16)],
        scratch_shapes=[pltpu.VMEM((HEADS_PER_GROUP, tm, HEAD), F32)] * (2 * G),
        compiler_params=_params(("parallel",)))(*os_, *lses, h, w2)


def _attn_merge_bwd(os_, lses, dh, w2, tm=512):
    G = len(os_)
    S, N = dh.shape

    def body(*refs):
        o_refs, l_refs, dh_ref, w_ref = refs[:G], refs[G:2 * G], refs[2 * G], refs[2 * G + 1]
        do_refs, dd_refs = refs[2 * G + 2:3 * G + 2], refs[3 * G + 2:4 * G + 2]
        scr = refs[4 * G + 2:]
        doa = _dot_nt(dh_ref[...].astype(BF16), w_ref[...])
        o = [_gather_heads(o_refs[g], scr[g], d, tm) for g, (_, d) in enumerate(ATTN_GROUPS)]
        l = [_gather_tile(l_refs[g], scr[G + g].at[0], d, tm) for g, (_, d) in enumerate(ATTN_GROUPS)]
        do = [[None] * HEADS_PER_GROUP for _ in range(G)]
        dd = [[None] * HEADS_PER_GROUP for _ in range(G)]
        for hh in range(HEADS_PER_GROUP):
            al = _softmax3([_lane_pick(l[g], hh) for g in range(G)])
            mix = None
            for g in range(G):
                dg = doa[:, g * GROUP_W + hh * HEAD:g * GROUP_W + (hh + 1) * HEAD]
                do[g][hh] = dg * al[g]
                t = al[g] * jnp.sum(dg * o[g][hh], axis=-1, keepdims=True)
                mix = t if mix is None else mix + t
            for g in range(G):
                dd[g][hh] = -al[g] * mix
        for g, (_, d) in enumerate(ATTN_GROUPS):
            _scatter_heads(do[g], scr[2 * G + g], do_refs[g], d, tm)
            _scatter_tile(_lane_place(dd[g]), scr[3 * G + g].at[0], dd_refs[g], d, tm)

    specs = [_group_spec(d, tm) for _, d in ATTN_GROUPS]
    tiles = [_tile_spec(d, tm) for _, d in ATTN_GROUPS]
    do_shapes = [jax.ShapeDtypeStruct((d, S // d, GROUP_W), BF16) for _, d in ATTN_GROUPS]
    dd_shapes = [jax.ShapeDtypeStruct((d, S // d, HEAD), F32) for _, d in ATTN_GROUPS]
    return pl.pallas_call(
        body, name="attn_merge_bwd", grid=(S // tm,),
        in_specs=specs + tiles + [pl.BlockSpec((tm, N), lambda i: (i, 0)),
                                  pl.BlockSpec((G * GROUP_W, N), lambda i: (0, 0), pipeline_mode=pl.Buffered(1))],
        out_specs=specs + tiles,
        out_shape=do_shapes + dd_shapes,
        scratch_shapes=[pltpu.VMEM((HEADS_PER_GROUP, tm, HEAD), F32)] * (4 * G),
        compiler_params=_params(("parallel",)))(*os_, *lses, dh, w2)


def _rope_tables(S):
    inv_freq = 1.0 / (ROPE_THETA ** (jnp.arange(0, HEAD, 2, dtype=F32) / HEAD))
    ang = jnp.arange(S, dtype=F32)[:, None] * inv_freq[None, :]
    cos, sin = jnp.cos(ang), jnp.sin(ang)
    return jnp.concatenate([cos, cos], axis=-1), jnp.concatenate([-sin, sin], axis=-1)


def _local_step(x, target, norm_mix, norm_ffn, lb_logits, out_gain, final_norm, comm):
    S = x.shape[0]
    nm0, nm1 = norm_mix[0:1], norm_mix[1:2]
    nf0, nf1 = norm_ffn[0:1], norm_ffn[1:2]
    w = comm.first_weights()

    proj, u0, got = _norm_mm(x, nm0, w["hin"], "hgrn_in", rider=comm.gather_rider(LATE_WEIGHTS_A))
    w.update(comm.gathered(LATE_WEIGHTS_A, got))
    (o, og, states), got = _hgrn_fwd(proj, lb_logits, out_gain, rider=comm.gather_rider(LATE_WEIGHTS_B))
    w.update(comm.gathered(LATE_WEIGHTS_B, got))
    fin_tn = w["fin0"].shape[2]
    h1 = _mm_res(x, og, w["hout"], "hgrn_out")
    z0, u1, _ = _norm_mm(h1, nf0, w["fin0"], "ffn0_in", out_dtype=BF16)
    h2, act0 = _swiglu_mm_res(h1, z0, w["fdn0"], "ffn0_down")
    cos, sin = _rope_tables(S)
    G = len(ATTN_GROUPS)
    w_groups = w["qkv"].transpose(1, 0, 2).reshape(D_MODEL, 3, G, GROUP_W)
    a_g, u2 = zip(*[_qkv_dilated(h2, nm1, w_groups[:, :, gi, :].reshape(D_MODEL, 3 * GROUP_W), cos, sin, d)
                    for gi, (_, d) in enumerate(ATTN_GROUPS)])
    o_g, lse_g = zip(*[_attn_fwd(a) for a in a_g])
    h3, oa = _attn_merge(o_g, lse_g, h2, w["aout"])
    z1, u3, _ = _norm_mm(h3, nf1, w["fin1"], "ffn1_in", out_dtype=BF16)
    h4, act1 = _swiglu_mm_res(h3, z1, w["fdn1"], "ffn1_down")
    dh4, loss, d_final = _loss_head(h4, final_norm, target)

    grads, small = {}, {"final_norm": d_final}

    def ffn_bwd(dh, h_in, u_in, z, act, gain, w_in, w_dn, tag, ride=None):
        dz = _mm_nt_swiglu_bwd(dh, w_dn, z, tag + "_down_dx")
        g_dn = _mm_tn(act, dh, 1, D_MODEL, D_MODEL, tag + "_down_dw")[0]
        g_in = _mm_tn(u_in, dz, N_CHIPS, fin_tn, fin_tn, tag + "_in_dw")
        rider = None if ride is None else ride(g_in, g_dn)
        dh_in, dgain, got = _mm_nt_normbwd(dz, w_in, h_in, gain, dh, tag + "_in_dx", rider=rider)
        return dh_in, dgain, g_in, g_dn, got

    dh3, d_nf1, grads["fin1"], grads["fdn1"], _ = ffn_bwd(dh4, h3, u3, z1, act1, nf1, w["fin1"], w["fdn1"], "ffn1")
    grads["aout"] = _mm_tn(oa, dh3, 1, D_MODEL, D_MODEL, "attn_out_dw")[0]
    merged = _attn_merge_bwd(o_g, lse_g, dh3, w["aout"])
    G = len(ATTN_GROUPS)
    das = [_attn_bwd(a_g[gi], merged[gi], lse_g[gi], merged[G + gi]) for gi in range(G)]
    dqkv = None
    for gi in range(G):
        dqkv = _undilate_group(das[gi], dqkv, cos, sin, gi)
    n_qkv = w["qkv"].shape[2]
    grads["qkv"] = _mm_tn(u2[0], dqkv, N_CHIPS, n_qkv, n_qkv, "attn_qkv_dw")
    dh2, d_nm1, _ = _mm_nt_normbwd(dqkv, w["qkv"], h2, nm1, dh3, "attn_qkv_dx")

    def ride_early(g_in, g_dn):
        return comm.pair_rider({**grads, "fin0": g_in, "fdn0": g_dn}, "early")

    dh1, d_nf0, _, _, got = ffn_bwd(dh2, h1, u1, z0, act0, nf0, w["fin0"], w["fdn0"], "ffn0", ride=ride_early)
    comm.paired("early", got)
    dog = _mm_nt(dh1, w["hout"][None], "hgrn_out_dx")
    (dproj, dlb, dgn), got = _hgrn_bwd(proj, lb_logits, out_gain, o, states, dog, rider=comm.exchange_rider("early"))
    comm.exchanged("early", got)
    late = {"hout": _mm_tn(og, dh1, 1, D_MODEL, D_MODEL, "hgrn_out_dw")[0],
            "hin": _mm_tn(u0, dproj, N_CHIPS, D_MODEL, D_MODEL, "hgrn_in_dw")}
    comm.pair_now(late, "late")
    dx, d_nm0, got = _mm_nt_normbwd(dproj, w["hin"], x, nm0, dh1, "hgrn_in_dx", rider=comm.exchange_rider("late"))
    comm.exchanged("late", got)

    small["norm_mix"] = jnp.concatenate([d_nm0, d_nm1], axis=0)
    small["norm_ffn"] = jnp.concatenate([d_nf0, d_nf1], axis=0)
    small["lb"] = dlb.reshape(1, HGRN_HEADS * HEAD)
    small["out_norm"] = dgn.reshape(HGRN_HEADS, HEAD)
    return loss, dx, small


def _place():
    x, y, c = lax.axis_index("x"), lax.axis_index("y"), lax.axis_index("c")
    others = [(1 - x, y), (x, 1 - y), (1 - x, 1 - y)]
    return x, y, c, others


ANY = pl.BlockSpec(memory_space=pl.ANY)


class _GatherRider:
    def __init__(self, shards):
        self.operands = list(shards)
        n = self.n = len(shards)
        self.out_shape = [jax.ShapeDtypeStruct((N_CHIPS,) + s.shape, s.dtype) for s in shards]
        self.scratch = [pltpu.SemaphoreType.DMA((3 * n,)), pltpu.SemaphoreType.DMA((3 * n,)),
                        pltpu.SemaphoreType.DMA((3 * n,)), pltpu.SemaphoreType.DMA((3 * n,)),
                        pltpu.SemaphoreType.DMA((n,)), pltpu.SemaphoreType.DMA((n,))]

    def _copies(self, ins, outs, sems):
        ici_send, ici_recv, _, _, own_send, own_recv = sems
        x, y, c, others = _place()
        me = 2 * x + y
        own = [pltpu.make_async_remote_copy(
            src_ref=ins[a], dst_ref=outs[a].at[me], send_sem=own_send.at[a], recv_sem=own_recv.at[a],
            device_id=(x, y, 1 - c), device_id_type=MESH) for a in range(self.n)]
        sends = [pltpu.make_async_remote_copy(
            src_ref=ins[a].at[c], dst_ref=outs[a].at[me, c], send_sem=ici_send.at[a * 3 + k], recv_sem=ici_recv.at[a * 3 + k],
            device_id=(ox, oy, c), device_id_type=MESH) for a in range(self.n) for k, (ox, oy) in enumerate(others)]
        return own, sends

    def start(self, ins, outs, sems):
        own, sends = self._copies(ins, outs, sems)
        for cp in own + sends:
            cp.start()

    def finish(self, ins, outs, sems):
        ici_send, ici_recv, d2d_send, d2d_recv, _, _ = sems
        x, y, c, others = _place()
        sibling = (x, y, 1 - c)
        own, sends = self._copies(ins, outs, sems)
        passes = []
        for a in range(self.n):
            for k, (ox, oy) in enumerate(others):
                s = a * 3 + k
                got = outs[a].at[2 * ox + oy, c]
                pltpu.make_async_remote_copy(
                    src_ref=got, dst_ref=got, send_sem=ici_send.at[s], recv_sem=ici_recv.at[s],
                    device_id=(ox, oy, c), device_id_type=MESH).wait_recv()
                fwd = pltpu.make_async_remote_copy(
                    src_ref=got, dst_ref=got, send_sem=d2d_send.at[s], recv_sem=d2d_recv.at[s],
                    device_id=sibling, device_id_type=MESH)
                fwd.start()
                passes.append(fwd)
        for a in range(self.n):
            for k, (ox, oy) in enumerate(others):
                s = a * 3 + k
                theirs = outs[a].at[2 * ox + oy, 1 - c]
                pltpu.make_async_remote_copy(
                    src_ref=theirs, dst_ref=theirs, send_sem=d2d_send.at[s], recv_sem=d2d_recv.at[s],
                    device_id=sibling, device_id_type=MESH).wait_recv()
        for cp in own:
            cp.wait()
        for cp in sends + passes:
            cp.wait_send()


class _PairRider:
    def __init__(self, grads):
        self.operands = list(grads)
        n = self.n = len(grads)
        self.out_shape = [jax.ShapeDtypeStruct((N_CHIPS,) + g.shape[2:], F32) for g in grads]
        self.scratch = [pltpu.SemaphoreType.DMA((N_CHIPS * n,)), pltpu.SemaphoreType.DMA((N_CHIPS * n,))]

    def _copies(self, ins, outs, sems):
        send_sem, recv_sem = sems
        x, y, c, _ = _place()
        return [pltpu.make_async_remote_copy(
            src_ref=ins[a].at[j, 1 - c], dst_ref=outs[a].at[j], send_sem=send_sem.at[a * N_CHIPS + j],
            recv_sem=recv_sem.at[a * N_CHIPS + j], device_id=(x, y, 1 - c), device_id_type=MESH)
            for a in range(self.n) for j in range(N_CHIPS)]

    def start(self, ins, outs, sems):
        for cp in self._copies(ins, outs, sems):
            cp.start()

    def finish(self, ins, outs, sems):
        for cp in self._copies(ins, outs, sems):
            cp.wait()


class _ExchangeRider:
    def __init__(self, parts):
        self.operands = list(parts)
        n = self.n = len(parts)
        self.out_shape = [jax.ShapeDtypeStruct(p.shape, p.dtype) for p in parts]
        self.scratch = [pltpu.SemaphoreType.DMA((3 * n,)), pltpu.SemaphoreType.DMA((3 * n,))]

    def _copies(self, ins, outs, sems):
        send_sem, recv_sem = sems
        x, y, c, others = _place()
        me = 2 * x + y
        return [pltpu.make_async_remote_copy(
            src_ref=ins[a].at[2 * ox + oy], dst_ref=outs[a].at[me], send_sem=send_sem.at[a * 3 + k],
            recv_sem=recv_sem.at[a * 3 + k], device_id=(ox, oy, c), device_id_type=MESH)
            for a in range(self.n) for k, (ox, oy) in enumerate(others)]

    def start(self, ins, outs, sems):
        for cp in self._copies(ins, outs, sems):
            cp.start()

    def finish(self, ins, outs, sems):
        send_sem, recv_sem = sems
        x, y, c, others = _place()
        for a in range(self.n):
            for k, (ox, oy) in enumerate(others):
                s = a * 3 + k
                got = outs[a].at[2 * ox + oy]
                pltpu.make_async_remote_copy(
                    src_ref=got, dst_ref=got, send_sem=send_sem.at[s], recv_sem=recv_sem.at[s],
                    device_id=(ox, oy, c), device_id_type=MESH).wait_recv()
        for cp in self._copies(ins, outs, sems):
            cp.wait_send()


def _run_rider(rider, name):
    n = rider.n

    def body(*refs):
        ins, outs, sems = refs[:n], refs[n:2 * n], refs[2 * n:]
        rider.start(ins, outs, sems)
        rider.finish(ins, outs, sems)

    return pl.pallas_call(
        body, name=name, in_specs=[ANY] * n, out_specs=[ANY] * n,
        out_shape=rider.out_shape, scratch_shapes=rider.scratch)(*rider.operands)


def _ride(rider, body, n_in, n_out, first, last):
    if rider is None:
        return body
    n = rider.n

    def wrapped(*refs):
        host_in, r_in = refs[:n_in], refs[n_in:n_in + n]
        host_out = refs[n_in + n:n_in + n + n_out]
        r_out = refs[n_in + n + n_out:n_in + 2 * n + n_out]
        rest = refs[n_in + 2 * n + n_out:]
        host_scr, sems = rest[:len(rest) - len(rider.scratch)], rest[len(rest) - len(rider.scratch):]

        @pl.when(first())
        def _():
            rider.start(r_in, r_out, sems)

        body(*host_in, *host_out, *host_scr)

        @pl.when(last())
        def _():
            rider.finish(r_in, r_out, sems)

    return wrapped


def _rider_args(rider):
    if rider is None:
        return [], [], [], [], []
    return rider.operands, [ANY] * rider.n, [ANY] * rider.n, rider.out_shape, rider.scratch


def _pair_sum(g, got, c_idx):
    _, _, r, cw = g.shape
    tr = _row_tile(r, cw)

    def body(c_ref, g_ref, got_ref, pb_ref):
        pb_ref[...] = (g_ref[...] + got_ref[...]).astype(BF16)

    blk = pl.BlockSpec((None, tr, cw), lambda j, i, c_ref: (j, i, 0))
    return pl.pallas_call(
        body, name="grad_pair_sum",
        grid_spec=pltpu.PrefetchScalarGridSpec(
            num_scalar_prefetch=1, grid=(N_CHIPS, r // tr),
            in_specs=[pl.BlockSpec((None, None, tr, cw), lambda j, i, c_ref: (j, c_ref[0], i, 0)), blk],
            out_specs=blk),
        out_shape=jax.ShapeDtypeStruct((N_CHIPS, r, cw), BF16),
        compiler_params=_params(("parallel", "parallel")))(c_idx, g, got)


def _chip_sum(g, sib, got, place):
    _, _, r, cw = g.shape
    tr = _row_tile(r, cw)

    def body(place_ref, g_ref, sib_ref, got_ref, t_ref):
        me = place_ref[0]
        own = g_ref[...] + sib_ref[...]
        acc = None
        for s in range(N_CHIPS):
            term = jnp.where(me == s, own, got_ref[s].astype(F32))
            acc = term if acc is None else acc + term
        t_ref[...] = acc

    return pl.pallas_call(
        body, name="grad_chip_sum",
        grid_spec=pltpu.PrefetchScalarGridSpec(
            num_scalar_prefetch=1, grid=(r // tr,),
            in_specs=[pl.BlockSpec((None, None, tr, cw), lambda i, pr: (pr[0], pr[1], i, 0)),
                      pl.BlockSpec((None, tr, cw), lambda i, pr: (pr[0], i, 0)),
                      pl.BlockSpec((N_CHIPS, tr, cw), lambda i, pr: (0, i, 0))],
            out_specs=pl.BlockSpec((tr, cw), lambda i, pr: (i, 0))),
        out_shape=jax.ShapeDtypeStruct((r, cw), F32),
        compiler_params=_params(("parallel",)))(place, g, sib, got)


def _pair_share(halves):
    n = len(halves)

    def body(*refs):
        ins, outs = refs[:n], refs[n:2 * n]
        send_sem, recv_sem = refs[2 * n:]
        x, y, c, _ = _place()
        cps = [pltpu.make_async_remote_copy(
            src_ref=ins[a], dst_ref=outs[a], send_sem=send_sem.at[a], recv_sem=recv_sem.at[a],
            device_id=(x, y, 1 - c), device_id_type=MESH) for a in range(n)]
        for cp in cps:
            cp.start()
        for cp in cps:
            cp.wait()

    return pl.pallas_call(
        body, name="grad_pair_share",
        in_specs=[ANY] * n, out_specs=[ANY] * n,
        out_shape=[jax.ShapeDtypeStruct(h.shape, F32) for h in halves],
        scratch_shapes=[pltpu.SemaphoreType.DMA((n,)), pltpu.SemaphoreType.DMA((n,))],
        )(*halves)


def _small_allreduce(pack):
    m_per, ncol = pack.shape
    n_dev = 8

    def body(x_ref, sum_ref, all_ref, send_sems, recv_sems, local_sem):
        x, y, c, others = _place()
        me, sibling = (x, y, c), (x, y, 1 - c)

        def rows(px, py, pc):
            return all_ref.at[pl.ds((4 * px + 2 * py + pc) * m_per, m_per), :]

        def copy(k, block, to, src=None):
            return pltpu.make_async_remote_copy(
                src_ref=rows(*block) if src is None else src, dst_ref=rows(*block),
                send_sem=send_sems.at[k], recv_sem=recv_sems.at[k], device_id=to, device_id_type=MESH)

        mine = pltpu.make_async_copy(x_ref, rows(*me), local_sem)
        mine.start()
        first = [copy(0, me, sibling, src=x_ref)]
        first += [copy(1 + j, me, (*chip, c), src=x_ref) for j, chip in enumerate(others)]
        for cp in first:
            cp.start()
        passed = [copy(4 + j, (*chip, c), sibling) for j, chip in enumerate(others)]
        for j, chip in enumerate(others):
            copy(1 + j, (*chip, c), me).wait_recv()
            passed[j].start()
        copy(0, sibling, me).wait_recv()
        for j, chip in enumerate(others):
            copy(4 + j, (*chip, 1 - c), me).wait_recv()
        for cp in first + passed:
            cp.wait_send()
        mine.wait()
        acc = all_ref[0:m_per, :]
        for dvc in range(1, n_dev):
            acc = acc + all_ref[dvc * m_per:(dvc + 1) * m_per, :]
        sum_ref[...] = acc

    return pl.pallas_call(
        body, name="small_allreduce",
        in_specs=[pl.BlockSpec(memory_space=pltpu.VMEM)],
        out_specs=pl.BlockSpec(memory_space=pltpu.VMEM),
        out_shape=jax.ShapeDtypeStruct((m_per, ncol), F32),
        scratch_shapes=[pltpu.VMEM((n_dev * m_per, ncol), F32),
                        pltpu.SemaphoreType.DMA((7,)), pltpu.SemaphoreType.DMA((7,)), pltpu.SemaphoreType.DMA],
        )(pack)


def _adam_math(w, g, m, v):
    m = ADAM_B1 * m + (1.0 - ADAM_B1) * g
    v = ADAM_B2 * v + (1.0 - ADAM_B2) * (g * g)
    m_hat = m / (1.0 - ADAM_B1 ** ADAM_STEP)
    v_hat = v / (1.0 - ADAM_B2 ** ADAM_STEP)
    delta = -ADAM_LR * (m_hat / (jnp.sqrt(v_hat) + ADAM_EPS) + ADAM_WD * w)
    return delta, m, v


def _adamw(halves, c_idx, w, m, v, name):
    L = len(halves)
    r, C = halves[0][0].shape
    tr = _row_tile(r, C, 1024 * 1024)
    nt = r // tr

    def body(c_ref, *refs):
        g_refs, (w_ref, m_ref, v_ref), (g_ref, d_ref, nm_ref, nv_ref) = refs[:2 * L], refs[2 * L:2 * L + 3], refs[2 * L + 3:]
        own = pl.program_id(1) == c_ref[0]
        g = None
        for l in range(L):
            cand = jnp.where(own, g_refs[2 * l][...], g_refs[2 * l + 1][...])
            g = cand if g is None else jnp.where(pl.program_id(0) == l, cand, g)
        g_ref[...] = g
        d_ref[...], nm_ref[...], nv_ref[...] = _adam_math(w_ref[...], g, m_ref[...], v_ref[...])

    def half(l, mine):
        def index(ll, h, i, c_ref):
            read = (h == c_ref[0]) if mine else (h != c_ref[0])
            return jnp.where(jnp.logical_and(ll == l, read), i, 0), 0
        return pl.BlockSpec((tr, C), index)

    full = pl.BlockSpec((None, tr, C), lambda ll, h, i, c_ref: (ll, h * nt + i, 0))
    shp = jax.ShapeDtypeStruct((L, 2 * r, C), F32)
    g_specs = [half(l, mine) for l in range(L) for mine in (True, False)]
    return pl.pallas_call(
        body, name=name,
        grid_spec=pltpu.PrefetchScalarGridSpec(
            num_scalar_prefetch=1, grid=(L, 2, nt),
            in_specs=g_specs + [full] * 3, out_specs=[full] * 4),
        out_shape=[shp] * 4,
        compiler_params=_params(("arbitrary", "arbitrary", "arbitrary")))(
            c_idx, *[a for pair in halves for a in pair], w, m, v)


def _small_update(gsum, logits_pack, w, m, v):
    def body(gs_ref, lg_ref, w_ref, m_ref, v_ref, g_ref, d_ref, nm_ref, nv_ref):
        g_ref[...] = gs_ref[...]
        l0, l1, l2 = lg_ref[0:1, :], lg_ref[1:2, :], lg_ref[2:3, :]
        mx = jnp.maximum(jnp.maximum(l0, l1), l2)
        e0, e1, e2 = jnp.exp(l0 - mx), jnp.exp(l1 - mx), jnp.exp(l2 - mx)
        tot = e0 + e1 + e2
        p0, p1, p2 = e0 / tot, e1 / tot, e2 / tot
        dlb = gs_ref[4:5, :]
        g_ref[4:5, :] = dlb * p0 * (1.0 - p0)
        g_ref[5:6, :] = -dlb * p0 * p1
        g_ref[6:7, :] = -dlb * p0 * p2
        d_ref[...], nm_ref[...], nv_ref[...] = _adam_math(w_ref[...], g_ref[...], m_ref[...], v_ref[...])

    full = pl.BlockSpec(memory_space=pltpu.VMEM)
    shp = jax.ShapeDtypeStruct(gsum.shape, F32)
    return pl.pallas_call(
        body, name="small_update", in_specs=[full] * 5, out_specs=[full] * 4, out_shape=[shp] * 4)(
            gsum, logits_pack, w, m, v)


def _pack_small(norm_mix, norm_ffn, lb3, out_norm, final_norm, extra=None):
    ncol = norm_mix.shape[1]
    on = jnp.pad(out_norm.reshape(1, -1), ((0, 0), (0, ncol - out_norm.size)))
    rows = [norm_mix, norm_ffn, lb3, on, final_norm.reshape(1, ncol)]
    if extra is not None:
        rows.append(extra)
    used = sum(r.shape[0] for r in rows)
    rows.append(jnp.zeros((SMALL_ROWS - used, ncol), F32))
    return jnp.concatenate(rows, axis=0)


WEIGHT_NAMES = ("hin", "hout", "qkv", "aout", "fin0", "fin1", "fdn0", "fdn1")
FIRST_WEIGHTS = ("hin",)
LATE_WEIGHTS_A = ("hout", "fin0", "fdn0")
LATE_WEIGHTS_B = ("qkv", "aout", "fin1", "fdn1")


def _split_weights(hgrn_w_in, hgrn_w_out, attn_w_qkv, attn_w_out, ffn_w_in, ffn_w_down):
    return {"hin": hgrn_w_in[0], "hout": hgrn_w_out[0], "qkv": attn_w_qkv[0], "aout": attn_w_out[0],
            "fin0": ffn_w_in[0], "fin1": ffn_w_in[1], "fdn0": ffn_w_down[0], "fdn1": ffn_w_down[1]}


def _halves(v):
    r, c = v.shape
    return v.reshape(2, r // 2, c)


def _full_weights(gathered):
    out = {}
    for k, g in gathered.items():
        _, _, r, c = g.shape
        if k in ("hin", "qkv", "fin0", "fin1"):
            out[k] = g.reshape(N_CHIPS, 2 * r, c)
        else:
            out[k] = g.reshape(N_CHIPS * 2 * r, c)
    return out


class _StepComm:
    def __init__(self, shards, c_idx, me_idx):
        self.shards, self.c_idx, self.me_idx = shards, c_idx, me_idx
        self.halves = {}
        self._stage = {}

    def gather_rider(self, names):
        return _GatherRider([_halves(self.shards[k].astype(BF16)) for k in names])

    def gathered(self, names, got):
        return _full_weights(dict(zip(names, got)))

    def first_weights(self):
        return self.gathered(FIRST_WEIGHTS, _run_rider(self.gather_rider(FIRST_WEIGHTS), "gather_first"))

    def pair_rider(self, grads, tag):
        names = list(grads)
        g4 = []
        for k in names:
            r, c = self.shards[k].shape
            g4.append(grads[k].reshape(N_CHIPS, 2, r // 2, c))
        self._stage[tag] = (names, g4)
        return _PairRider(g4)

    def pair_now(self, grads, tag):
        self.paired(tag, _run_rider(self.pair_rider(grads, tag), "grad_pair_exchange_" + tag))

    def paired(self, tag, got):
        names, g4 = self._stage[tag]
        self._stage[tag] = (names, [(g, s, _pair_sum(g, s, self.c_idx)) for g, s in zip(g4, got)])

    def exchange_rider(self, tag):
        return _ExchangeRider([s[2] for s in self._stage[tag][1]])

    def exchanged(self, tag, got):
        names, sums = self._stage.pop(tag)
        place = jnp.concatenate([self.me_idx, self.c_idx])
        for k, (g, sib, _), recv in zip(names, sums, got):
            self.halves[k] = _chip_sum(g, sib, recv, place)

    def shared_halves(self):
        mine = [self.halves[k] for k in WEIGHT_NAMES]
        return dict(zip(WEIGHT_NAMES, zip(mine, _pair_share(mine))))


def kernel(x, norm_mix, norm_ffn, hgrn_w_in, hgrn_lb_logits, hgrn_out_norm, hgrn_w_out, attn_w_qkv, attn_w_out, ffn_w_in, ffn_w_down, final_norm, loss_target, m_norm_mix, m_norm_ffn, m_hgrn_w_in, m_hgrn_lb_logits, m_hgrn_out_norm, m_hgrn_w_out, m_attn_w_qkv, m_attn_w_out, m_ffn_w_in, m_ffn_w_down, m_final_norm, v_norm_mix, v_norm_ffn, v_hgrn_w_in, v_hgrn_lb_logits, v_hgrn_out_norm, v_hgrn_w_out, v_attn_w_qkv, v_attn_w_out, v_ffn_w_in, v_ffn_w_down, v_final_norm):
    S = x.shape[1]
    xi, yi, ci = lax.axis_index("x"), lax.axis_index("y"), lax.axis_index("c")
    c_idx = jnp.reshape(ci, (1,)).astype(jnp.int32)
    me_idx = jnp.reshape(2 * xi + yi, (1,)).astype(jnp.int32)

    w_own = _split_weights(hgrn_w_in, hgrn_w_out, attn_w_qkv, attn_w_out, ffn_w_in, ffn_w_down)

    comm = _StepComm(w_own, c_idx, me_idx)
    loss, dx, small = _local_step(
        x.reshape(S, D_MODEL), loss_target.reshape(S, D_MODEL), norm_mix, norm_ffn, hgrn_lb_logits,
        hgrn_out_norm, final_norm.reshape(1, D_MODEL), comm)

    halves = comm.shared_halves()
    updated = {}
    for tensor, layers, (wt, mt, vt) in (
            ("hgrn_w_in", ("hin",), (hgrn_w_in, m_hgrn_w_in, v_hgrn_w_in)),
            ("hgrn_w_out", ("hout",), (hgrn_w_out, m_hgrn_w_out, v_hgrn_w_out)),
            ("attn_w_qkv", ("qkv",), (attn_w_qkv, m_attn_w_qkv, v_attn_w_qkv)),
            ("attn_w_out", ("aout",), (attn_w_out, m_attn_w_out, v_attn_w_out)),
            ("ffn_w_in", ("fin0", "fin1"), (ffn_w_in, m_ffn_w_in, v_ffn_w_in)),
            ("ffn_w_down", ("fdn0", "fdn1"), (ffn_w_down, m_ffn_w_down, v_ffn_w_down))):
        updated[tensor] = _adamw([halves[k] for k in layers], c_idx, wt, mt, vt, "adamw_" + tensor)

    loss_row = jnp.pad(loss, ((0, 0), (0, D_MODEL - loss.shape[1])))
    lb3 = jnp.concatenate([small["lb"], jnp.zeros((2, D_MODEL), F32)], axis=0)
    on_grad = jnp.sum(small["out_norm"], axis=0, keepdims=True)
    pack = _pack_small(small["norm_mix"], small["norm_ffn"], lb3, on_grad, small["final_norm"], loss_row)
    gsum = _small_allreduce(pack)
    w_s = _pack_small(norm_mix, norm_ffn, hgrn_lb_logits, hgrn_out_norm, final_norm)
    m_s = _pack_small(m_norm_mix, m_norm_ffn, m_hgrn_lb_logits, m_hgrn_out_norm, m_final_norm)
    v_s = _pack_small(v_norm_mix, v_norm_ffn, v_hgrn_lb_logits, v_hgrn_out_norm, v_final_norm)
    lg_pack = jnp.pad(hgrn_lb_logits, ((0, 8 - hgrn_lb_logits.shape[0]), (0, 0)))
    sg, sd, sm, sv = _small_update(gsum, lg_pack, w_s, m_s, v_s)

    def unpack(p):
        return (p[0:2], p[2:4], p[4:7], p[7:8, :HEAD], p[8])

    def assemble(p, which):
        nmx, nff, lbl, onm, fnm = unpack(p)
        hin, hout, qkv, aout, fin, fdn = [updated[t][which] for t in
                                          ("hgrn_w_in", "hgrn_w_out", "attn_w_qkv", "attn_w_out", "ffn_w_in", "ffn_w_down")]
        return (nmx, nff, hin, lbl, onm, hout, qkv, aout, fin, fdn, fnm)

    total_loss = gsum[9, 0]
    return (total_loss, dx.reshape(1, S, D_MODEL), *assemble(sg, 0), *assemble(sd, 1), *assemble(sm, 2), *assemble(sv, 3))
```

```python
import functools

import jax
import jax.numpy as jnp
import numpy as np
from jax import lax
from jax.experimental import pallas as pl
from jax.experimental.pallas import tpu as pltpu

F32 = jnp.float32
BF16 = jnp.bfloat16
MESH = pl.DeviceIdType.MESH

D_MODEL = 1024
HEAD = 128
HGRN_HEADS = 8
HGRN_CHUNK = 64
HGRN_HEADS_PER_STEP = 2
ATTN_GROUPS = ((128, 1), (512, 4), (2048, 16))
ATTN_SPAN = 128
HEADS_PER_GROUP = 4
GROUP_W = HEADS_PER_GROUP * HEAD
D_FF = 2816
NORM_EPS = 1e-6
ROPE_THETA = 10000.0
NEG = -1e30

ADAM_LR, ADAM_B1, ADAM_B2, ADAM_EPS, ADAM_WD, ADAM_STEP = 0.001, 0.9, 0.999, 1e-08, 0.01, 10

N_CHIPS = 4
VMEM_LIMIT = 56 * 1024 * 1024
SMALL_ROWS = 16


def _params(sem=None):
    return pltpu.CompilerParams(dimension_semantics=sem, vmem_limit_bytes=VMEM_LIMIT)


def _row_tile(rows, cols, budget_bytes=3 * 512 * 1024):
    best = 8
    for t in range(8, rows + 1, 8):
        if rows % t == 0 and t * cols * 4 <= budget_bytes:
            best = t
    assert rows % best == 0
    return best


def _grid_corner(i, j):
    return jnp.logical_and(pl.program_id(0) == i, pl.program_id(1) == j)


def _sigmoid(v):
    return 0.5 * jnp.tanh(0.5 * v) + 0.5


def _dot(a, b):
    return jnp.dot(a, b, preferred_element_type=F32)


def _dot_nt(a, b):
    return lax.dot_general(a, b, (((1,), (1,)), ((), ())), preferred_element_type=F32)


def _dot_tn(a, b):
    return lax.dot_general(a, b, (((0,), (0,)), ((), ())), preferred_element_type=F32)


def _dot_exact(ones, b):
    ones = ones.astype(BF16)
    hi = b.astype(BF16)
    rest = b - hi.astype(F32)
    mid = rest.astype(BF16)
    low = (rest - mid.astype(F32)).astype(BF16)
    return _dot(ones, hi) + _dot(ones, mid) + _dot(ones, low)


def _rstd(v):
    return lax.rsqrt(jnp.mean(v * v, axis=-1, keepdims=True) + NORM_EPS)


def _norm_mm(h, gain, w3, name, out_dtype=F32, tm=2048, rider=None):
    S, K = h.shape
    J, _, n = w3.shape
    gi = S // tm

    def body(h_ref, g_ref, w_ref, y_ref, u_ref):
        @pl.when(pl.program_id(1) == 0)
        def _():
            v = h_ref[...]
            u_ref[...] = (v * _rstd(v) * g_ref[...]).astype(BF16)

        y_ref[...] = _dot(u_ref[...], w_ref[pl.program_id(1)]).astype(y_ref.dtype)

    r_ops, r_in, r_out, r_shape, r_scr = _rider_args(rider)
    res = pl.pallas_call(
        _ride(rider, body, 3, 2, functools.partial(_grid_corner, 0, 0), functools.partial(_grid_corner, gi - 1, J - 1)),
        name=name, grid=(gi, J),
        in_specs=[pl.BlockSpec((tm, K), lambda i, j: (i, 0)),
                  pl.BlockSpec((1, K), lambda i, j: (0, 0)),
                  pl.BlockSpec((J, K, n), lambda i, j: (0, 0, 0), pipeline_mode=pl.Buffered(1))] + r_in,
        out_specs=[pl.BlockSpec((tm, n), lambda i, j: (i, j)), pl.BlockSpec((tm, K), lambda i, j: (i, 0))] + r_out,
        out_shape=[jax.ShapeDtypeStruct((S, J * n), out_dtype), jax.ShapeDtypeStruct((S, K), BF16)] + r_shape,
        scratch_shapes=r_scr,
        compiler_params=_params(("arbitrary", "arbitrary")))(h, gain, w3, *r_ops)
    return res[0], res[1], res[2:]


def _mm_res(h, a, w2, name, tm=1024):
    S, N = h.shape
    K = a.shape[1]

    def body(h_ref, a_ref, w_ref, o_ref):
        o_ref[...] = h_ref[...] + _dot(a_ref[...], w_ref[...])

    return pl.pallas_call(
        body, name=name, grid=(S // tm,),
        in_specs=[pl.BlockSpec((tm, N), lambda i: (i, 0)),
                  pl.BlockSpec((tm, K), lambda i: (i, 0)),
                  pl.BlockSpec((K, N), lambda i: (0, 0))],
        out_specs=pl.BlockSpec((tm, N), lambda i: (i, 0)),
        out_shape=jax.ShapeDtypeStruct((S, N), F32),
        compiler_params=_params(("parallel",)))(h, a, w2)


def _swiglu(z_ref, F):
    g = z_ref[:, :F].astype(F32)
    return (g * _sigmoid(g) * z_ref[:, F:].astype(F32)).astype(BF16)


def _swiglu_mm_res(h, z, w2, name, tm=512):
    S, N = h.shape
    F = w2.shape[0]

    def body(h_ref, z_ref, w_ref, o_ref, a_ref):
        a = _swiglu(z_ref, F)
        a_ref[...] = a
        o_ref[...] = h_ref[...] + _dot(a, w_ref[...])

    return pl.pallas_call(
        body, name=name, grid=(S // tm,),
        in_specs=[pl.BlockSpec((tm, N), lambda i: (i, 0)),
                  pl.BlockSpec((tm, 2 * F), lambda i: (i, 0)),
                  pl.BlockSpec((F, N), lambda i: (0, 0), pipeline_mode=pl.Buffered(1))],
        out_specs=[pl.BlockSpec((tm, N), lambda i: (i, 0)), pl.BlockSpec((tm, F), lambda i: (i, 0))],
        out_shape=[jax.ShapeDtypeStruct((S, N), F32), jax.ShapeDtypeStruct((S, F), BF16)],
        compiler_params=_params(("parallel",)))(h, z, w2)


def _dy_specs(dy, J, n, tm):
    if dy.ndim == 3:
        return [pl.BlockSpec((None, tm, n), functools.partial(lambda i, j: (j, i, 0), j=j)) for j in range(J)]
    return [pl.BlockSpec((tm, n), functools.partial(lambda i, j: (i, j), j=j)) for j in range(J)]


def _acc_nt(dy_refs, w_ref):
    acc = None
    for j, r in enumerate(dy_refs):
        t = _dot_nt(r[...].astype(BF16), w_ref[j])
        acc = t if acc is None else acc + t
    return acc


def _mm_nt(dy, w3, name, out_dtype=F32, tm=1024):
    J, K, n = w3.shape
    S = dy.shape[-2]

    def body(*refs):
        dy_refs, w_ref, o_ref = refs[:J], refs[J], refs[J + 1]
        o_ref[...] = _acc_nt(dy_refs, w_ref).astype(o_ref.dtype)

    return pl.pallas_call(
        body, name=name, grid=(S // tm,),
        in_specs=_dy_specs(dy, J, n, tm) + [pl.BlockSpec((J, K, n), lambda i: (0, 0, 0))],
        out_specs=pl.BlockSpec((tm, K), lambda i: (i, 0)),
        out_shape=jax.ShapeDtypeStruct((S, K), out_dtype),
        compiler_params=_params(("parallel",)))(*([dy] * J), w3)


def _mm_nt_normbwd(dy, w3, h, gain, dh, name, tm=512, rider=None):
    J, K, n = w3.shape
    S = h.shape[0]
    steps = S // tm

    def body(*refs):
        dy_refs, w_ref, h_ref, g_ref, dh_ref, o_ref, dg_ref = refs[:J], *refs[J:]
        du = _acc_nt(dy_refs, w_ref)
        v = h_ref[...]
        r = _rstd(v)
        xh = v * r
        dyg = du * g_ref[...]
        o_ref[...] = dh_ref[...] + r * (dyg - xh * jnp.mean(dyg * xh, axis=-1, keepdims=True))

        @pl.when(pl.program_id(0) == 0)
        def _():
            dg_ref[...] = jnp.zeros_like(dg_ref)

        dg_ref[...] += jnp.sum(du * xh, axis=0, keepdims=True)

    row = pl.BlockSpec((tm, K), lambda i: (i, 0))
    vec = pl.BlockSpec((1, K), lambda i: (0, 0))
    r_ops, r_in, r_out, r_shape, r_scr = _rider_args(rider)
    res = pl.pallas_call(
        _ride(rider, body, J + 4, 2, lambda: pl.program_id(0) == 0, lambda: pl.program_id(0) == steps - 1),
        name=name, grid=(steps,),
        in_specs=_dy_specs(dy, J, n, tm) + [pl.BlockSpec((J, K, n), lambda i: (0, 0, 0)), row, vec, row] + r_in,
        out_specs=[row, vec] + r_out,
        out_shape=[jax.ShapeDtypeStruct((S, K), F32), jax.ShapeDtypeStruct((1, K), F32)] + r_shape,
        scratch_shapes=r_scr,
        compiler_params=_params(("arbitrary",)))(*([dy] * J), w3, h, gain, dh, *r_ops)
    return res[0], res[1], res[2:]


def _mm_nt_swiglu_bwd(dh, w2, z, name, tm=512, chunks=11):
    F, N = w2.shape
    S = dh.shape[0]
    fc = F // chunks
    assert fc * chunks == F and fc % HEAD == 0

    def body(dh_ref, w_ref, z_ref, o_ref):
        dhb = dh_ref[...].astype(BF16)
        da = [_dot_nt(dhb, w_ref[c * fc:(c + 1) * fc, :]) for c in range(chunks)]
        for c in range(chunks):
            g = z_ref[:, c * fc:(c + 1) * fc].astype(F32)
            u = z_ref[:, F + c * fc:F + (c + 1) * fc].astype(F32)
            sg = _sigmoid(g)
            o_ref[:, c * fc:(c + 1) * fc] = (da[c] * u * (sg * (1.0 + g * (1.0 - sg)))).astype(BF16)
            o_ref[:, F + c * fc:F + (c + 1) * fc] = (da[c] * (g * sg)).astype(BF16)

    return pl.pallas_call(
        body, name=name, grid=(S // tm,),
        in_specs=[pl.BlockSpec((tm, N), lambda i: (i, 0)),
                  pl.BlockSpec((F, N), lambda i: (0, 0), pipeline_mode=pl.Buffered(1)),
                  pl.BlockSpec((tm, 2 * F), lambda i: (i, 0))],
        out_specs=pl.BlockSpec((tm, 2 * F), lambda i: (i, 0)),
        out_shape=jax.ShapeDtypeStruct((S, 2 * F), BF16),
        compiler_params=_params(("parallel",)))(dh, w2, z)


def _mm_tn(x, dy, J, n, tn, name):
    tpn = n // tn
    ts = 2048 if x.shape[1] <= 1536 else 1024
    S, K = x.shape
    if dy.ndim == 3:
        dy_spec = pl.BlockSpec((None, ts, tn), lambda c, s: (c // tpn, s, c % tpn))
    else:
        dy_spec = pl.BlockSpec((ts, tn), lambda c, s: (s, c))

    def body(x_ref, dy_ref, o_ref):
        @pl.when(pl.program_id(1) == 0)
        def _():
            o_ref[...] = jnp.zeros_like(o_ref)

        o_ref[...] += _dot_tn(x_ref[...], dy_ref[...].astype(BF16))

    return pl.pallas_call(
        body, name=name, grid=(J * tpn, S // ts),
        in_specs=[pl.BlockSpec((ts, K), lambda c, s: (s, 0)), dy_spec],
        out_specs=pl.BlockSpec((None, K, tn), lambda c, s: (c // tpn, 0, c % tpn)),
        out_shape=jax.ShapeDtypeStruct((J, K, n), F32),
        compiler_params=_params(("parallel", "arbitrary")))(x, dy)


def _loss_head(h, gain, target, tm=1024):
    S, K = h.shape

    def body(h_ref, g_ref, t_ref, dh_ref, loss_ref, dg_ref):
        v = h_ref[...]
        r = _rstd(v)
        xh = v * r
        g = g_ref[...]
        dy = (xh * g - t_ref[...]) * (1.0 / K)
        dyg = dy * g
        dh_ref[...] = r * (dyg - xh * jnp.mean(dyg * xh, axis=-1, keepdims=True))

        @pl.when(pl.program_id(0) == 0)
        def _():
            loss_ref[...] = jnp.zeros_like(loss_ref)
            dg_ref[...] = jnp.zeros_like(dg_ref)

        part = jnp.sum(jnp.sum(dy * dy, axis=-1, keepdims=True), axis=0, keepdims=True) * (0.5 * K)
        lane = lax.broadcasted_iota(jnp.int32, loss_ref.shape, 1)
        loss_ref[...] += jnp.where(lane == 0, part, 0.0)
        dg_ref[...] += jnp.sum(dy * xh, axis=0, keepdims=True)

    row = pl.BlockSpec((tm, K), lambda i: (i, 0))
    vec = pl.BlockSpec((1, K), lambda i: (0, 0))
    return pl.pallas_call(
        body, name="loss_head", grid=(S // tm,),
        in_specs=[row, vec, row],
        out_specs=[row, pl.BlockSpec((1, HEAD), lambda i: (0, 0)), vec],
        out_shape=[jax.ShapeDtypeStruct((S, K), F32), jax.ShapeDtypeStruct((1, HEAD), F32),
                   jax.ShapeDtypeStruct((1, K), F32)],
        compiler_params=_params(("arbitrary",)))(h, gain, target)


def _lower_bound(lg_ref):
    l0, l1, l2 = lg_ref[0:1, :], lg_ref[1:2, :], lg_ref[2:3, :]
    mx = jnp.maximum(jnp.maximum(l0, l1), l2)
    e0, e1, e2 = jnp.exp(l0 - mx), jnp.exp(l1 - mx), jnp.exp(l2 - mx)
    return e0 / (e0 + e1 + e2)


def _chunks(v, ncb):
    C = HGRN_CHUNK
    return [v[c * C:(c + 1) * C] for c in range(ncb)]


def _rows(parts):
    return jnp.concatenate(parts, axis=0)


def _block_gates(qz, fz, lb, ncb):
    C = HGRN_CHUNK
    row = lax.broadcasted_iota(jnp.int32, (C, C), 0)
    col = lax.broadcasted_iota(jnp.int32, (C, C), 1)
    tri = (col <= row).astype(F32)
    first_half = lax.broadcasted_iota(jnp.int32, (C, HEAD), 0) < C // 2
    sig = _sigmoid(fz)
    fg = lb + (1.0 - lb) * sig
    key = 1.0 - fg
    lg = jnp.log(fg)
    lgs = _chunks(lg, ncb)
    b = _rows([_dot_exact(tri, v) for v in lgs])
    r_c = [jnp.sum(jnp.where(first_half, v, 0.0), axis=0, keepdims=True) for v in lgs]
    bl_c = [jnp.sum(v, axis=0, keepdims=True) for v in lgs]
    r = _rows([jnp.broadcast_to(v, (C, HEAD)) for v in r_c])
    e_br, e_rb = jnp.exp(b - r), jnp.exp(r - b)
    e_b = e_br * _rows([jnp.broadcast_to(jnp.exp(v), (C, HEAD)) for v in r_c])
    e_lb = e_rb * _rows([jnp.broadcast_to(jnp.exp(e - v), (C, HEAD)) for e, v in zip(bl_c, r_c)])
    sq = _sigmoid(qz)
    qy = qz * sq
    return sig, fg, key, (e_br, e_rb, e_b, e_lb), bl_c, sq, qy


def _hgrn_fwd(proj, logits, gain, tb=1024, rider=None):
    S = proj.shape[0]
    H, C = HGRN_HEADS, HGRN_CHUNK
    ncb = tb // C

    def one_head(q_ref, f_ref, i_ref, g_ref, lg_ref, gn_ref, o_ref, og_ref, st_ref, state):
        @pl.when(pl.program_id(1) == 0)
        def _():
            state[...] = jnp.zeros_like(state)

        lb = _lower_bound(lg_ref)
        causal = lax.broadcasted_iota(jnp.int32, (C, C), 1) <= lax.broadcasted_iota(jnp.int32, (C, C), 0)
        qz, fz, gz = q_ref[...], f_ref[...], g_ref[...]
        _, _, key, (e_br, e_rb, e_b, e_lb), bl_c, _, qy = _block_gates(qz, fz, lb, ncb)
        qs = _chunks((qy * e_br).astype(BF16), ncb)
        ks = _chunks((key * e_rb).astype(BF16), ncb)
        qb = _chunks((qy * e_b).astype(BF16), ncb)
        ke = _chunks((key * e_lb).astype(BF16), ncb)
        vb = _chunks(i_ref[...].astype(BF16), ncb)
        a = [jnp.where(causal, _dot_nt(qs[c], ks[c]), 0.0).astype(BF16) for c in range(ncb)]
        upd = [_dot_tn(vb[c], ke[c]) for c in range(ncb)]
        o_intra = [_dot(a[c], vb[c]) for c in range(ncb)]
        st = state[...]
        e_l = [jnp.exp(v) for v in bl_c]
        sts = []
        for c in range(ncb):
            sts.append(st)
            st = st * e_l[c] + upd[c]
        state[...] = st
        for c in range(ncb):
            st_ref[c] = sts[c]
        o = _rows([_dot_nt(qb[c], sts[c].astype(BF16)) + o_intra[c] for c in range(ncb)])
        o_ref[...] = o
        og_ref[...] = ((o * _rstd(o) * gn_ref[...]) * (gz * _sigmoid(gz))).astype(BF16)

    def body(q_ref, f_ref, i_ref, g_ref, lg_ref, gn_ref, o_ref, og_ref, st_ref, state):
        for hs in range(HP):
            cols = slice(hs * HEAD, (hs + 1) * HEAD)
            one_head(q_ref.at[:, cols], f_ref.at[:, cols], i_ref.at[:, cols], g_ref.at[:, cols], lg_ref.at[:, cols],
                     gn_ref, o_ref.at[:, cols], og_ref.at[:, cols], st_ref.at[hs], state.at[hs])

    HP, wide = HGRN_HEADS_PER_STEP, HGRN_HEADS_PER_STEP * HEAD
    hg = H // HP

    def part(p):
        return pl.BlockSpec((tb, wide), functools.partial(lambda h, i, p: (i, p * hg + h), p=p))

    nb = S // tb
    r_ops, r_in, r_out, r_shape, r_scr = _rider_args(rider)
    res = pl.pallas_call(
        _ride(rider, body, 6, 3, functools.partial(_grid_corner, 0, 0), functools.partial(_grid_corner, hg - 1, nb - 1)),
        name="hgrn_fwd", grid=(hg, nb),
        in_specs=[part(0), part(1), part(2), part(3),
                  pl.BlockSpec((3, wide), lambda h, i: (0, h)),
                  pl.BlockSpec((1, HEAD), lambda h, i: (0, 0))] + r_in,
        out_specs=[pl.BlockSpec((tb, wide), lambda h, i: (i, h)),
                   pl.BlockSpec((tb, wide), lambda h, i: (i, h)),
                   pl.BlockSpec((HP, ncb, HEAD, HEAD), lambda h, i: (h, i, 0, 0))] + r_out,
        out_shape=[jax.ShapeDtypeStruct((S, H * HEAD), F32),
                   jax.ShapeDtypeStruct((S, H * HEAD), BF16),
                   jax.ShapeDtypeStruct((H, S // C, HEAD, HEAD), F32)] + r_shape,
        scratch_shapes=[pltpu.VMEM((HP, HEAD, HEAD), F32)] + r_scr,
        compiler_params=_params(("arbitrary", "arbitrary")))(proj, proj, proj, proj, logits, gain, *r_ops)
    return res[:3], res[3:]


def _hgrn_bwd(proj, logits, gain, o, states, dog, tb=1024, rider=None):
    S = proj.shape[0]
    H, C = HGRN_HEADS, HGRN_CHUNK
    ncb = tb // C
    nb = S // tb

    def one_head(q_ref, f_ref, i_ref, g_ref, lg_ref, gn_ref, o_ref, st_ref, dog_ref,
                 dp_ref, dlb_ref, dgn_ref, dstate, dst_scr):
        @pl.when(pl.program_id(1) == 0)
        def _():
            dstate[...] = jnp.zeros_like(dstate)
            dlb_ref[...] = jnp.zeros_like(dlb_ref)
            dgn_ref[...] = jnp.zeros_like(dgn_ref)

        lb = _lower_bound(lg_ref)
        oml = 1.0 - lb
        gn = gn_ref[...]
        row = lax.broadcasted_iota(jnp.int32, (C, C), 0)
        col = lax.broadcasted_iota(jnp.int32, (C, C), 1)
        causal = col <= row
        tri_up = (col >= row).astype(F32)
        qz, fz, gz = q_ref[...], f_ref[...], g_ref[...]
        sig, fg, key, (e_br, e_rb, e_b, e_lb), bl_c, sq, qy = _block_gates(qz, fz, lb, ncb)
        qs_v, ks_v = (qy * e_br).astype(BF16), (key * e_rb).astype(BF16)
        qb_v, ke_v = (qy * e_b).astype(BF16), (key * e_lb).astype(BF16)
        qs, ks, qb, ke = _chunks(qs_v, ncb), _chunks(ks_v, ncb), _chunks(qb_v, ncb), _chunks(ke_v, ncb)
        vb = _chunks(i_ref[...].astype(BF16), ncb)
        ov = o_ref[...]
        rs = _rstd(ov)
        xh = ov * rs
        sg = _sigmoid(gz)
        dog_v = dog_ref[...]
        dgz = dog_v * (xh * gn) * (sg * (1.0 + gz * (1.0 - sg)))
        don = dog_v * (gz * sg)
        dgn_ref[...] += jnp.sum(don * xh, axis=0, keepdims=True)
        dyg = don * gn
        do = rs * (dyg - xh * jnp.mean(dyg * xh, axis=-1, keepdims=True))
        dob = _chunks(do.astype(BF16), ncb)
        CH = range(ncb)
        a = [jnp.where(causal, _dot_nt(qs[c], ks[c]), 0.0).astype(BF16) for c in CH]
        da = [jnp.where(causal, _dot_nt(dob[c], vb[c]), 0.0).astype(BF16) for c in CH]
        wst = [_dot_tn(dob[c], qb[c]) for c in CH]
        dv_in = [_dot_tn(a[c], dob[c]) for c in CH]
        dqs = [_dot(da[c], ks[c]) for c in CH]
        dks = [_dot_tn(da[c], qs[c]) for c in CH]
        e_l = [jnp.exp(v) for v in bl_c]
        dst = dstate[...]
        for c in reversed(range(ncb)):
            dst_scr[c] = dst
            dst = wst[c] + dst * e_l[c]
        dstate[...] = dst
        dst1b = [dst_scr[c].astype(BF16) for c in CH]
        dqb = [_dot(dob[c], st_ref[c].astype(BF16)) for c in CH]
        dke = [_dot(vb[c], dst1b[c]) for c in CH]
        dv = [dv_in[c] + _dot_nt(ke[c], dst1b[c]) for c in CH]
        dbl_st = [jnp.sum(dst_scr[c] * st_ref[c], axis=0, keepdims=True) * e_l[c] for c in CH]
        dqs, dks, dqb, dke, dv = _rows(dqs), _rows(dks), _rows(dqb), _rows(dke), _rows(dv)
        dke_ke = dke * ke_v.astype(F32)
        db = dqs * qs_v.astype(F32) - dks * ks_v.astype(F32) + dqb * qb_v.astype(F32) - dke_ke
        dlg = []
        for c, (db_c, kk_c) in enumerate(zip(_chunks(db, ncb), _chunks(dke_ke, ncb))):
            dbl = jnp.sum(kk_c, axis=0, keepdims=True) + dbl_st[c]
            dlg.append(_dot_exact(tri_up, db_c) + dbl)
        dlg = _rows(dlg)
        dkey = dks * e_rb + dke * e_lb
        dqy = dqs * e_br + dqb * e_b
        dfg = dlg / fg - dkey
        dlb_ref[...] += jnp.sum(dfg * (1.0 - sig), axis=0, keepdims=True)
        dp_ref[0] = (dqy * (sq * (1.0 + qz * (1.0 - sq)))).astype(BF16)
        dp_ref[1] = (dfg * oml * sig * (1.0 - sig)).astype(BF16)
        dp_ref[2] = dv.astype(BF16)
        dp_ref[3] = dgz.astype(BF16)

    def body(q_ref, f_ref, i_ref, g_ref, lg_ref, gn_ref, o_ref, st_ref, dog_ref,
             dp_ref, dlb_ref, dgn_ref, dstate, dst_scr):
        for hs in range(HP):
            cols = slice(hs * HEAD, (hs + 1) * HEAD)
            one_head(q_ref.at[:, cols], f_ref.at[:, cols], i_ref.at[:, cols], g_ref.at[:, cols], lg_ref.at[:, cols],
                     gn_ref, o_ref.at[:, cols], st_ref.at[hs], dog_ref.at[:, cols],
                     dp_ref.at[:, :, cols], dlb_ref.at[hs], dgn_ref.at[hs], dstate.at[hs], dst_scr)

    HP, wide = HGRN_HEADS_PER_STEP, HGRN_HEADS_PER_STEP * HEAD
    hg = H // HP

    def part(p):
        return pl.BlockSpec((tb, wide), functools.partial(lambda h, i, p: (nb - 1 - i, p * hg + h), p=p))

    blk = pl.BlockSpec((tb, wide), lambda h, i: (nb - 1 - i, h))
    acc = pl.BlockSpec((HP, 1, HEAD), lambda h, i: (h, 0, 0))
    r_ops, r_in, r_out, r_shape, r_scr = _rider_args(rider)
    res = pl.pallas_call(
        _ride(rider, body, 9, 3, functools.partial(_grid_corner, 0, 0), functools.partial(_grid_corner, hg - 1, nb - 1)),
        name="hgrn_bwd", grid=(hg, nb),
        in_specs=[part(0), part(1), part(2), part(3),
                  pl.BlockSpec((3, wide), lambda h, i: (0, h)),
                  pl.BlockSpec((1, HEAD), lambda h, i: (0, 0)),
                  blk,
                  pl.BlockSpec((HP, ncb, HEAD, HEAD), lambda h, i: (h, nb - 1 - i, 0, 0)),
                  blk] + r_in,
        out_specs=[pl.BlockSpec((4, tb, wide), lambda h, i: (0, nb - 1 - i, h)), acc, acc] + r_out,
        out_shape=[jax.ShapeDtypeStruct((4, S, H * HEAD), BF16),
                   jax.ShapeDtypeStruct((H, 1, HEAD), F32),
                   jax.ShapeDtypeStruct((H, 1, HEAD), F32)] + r_shape,
        scratch_shapes=[pltpu.VMEM((HP, HEAD, HEAD), F32), pltpu.VMEM((ncb, HEAD, HEAD), F32)] + r_scr,
        compiler_params=_params(("arbitrary", "arbitrary")))(
            proj, proj, proj, proj, logits, gain, o, states, dog, *r_ops)
    return res[:3], res[3:]


def _rope(v, cos, sin):
    return v * cos + pltpu.roll(v, HEAD // 2, 1) * sin


def _lane_pick(tile, hh):
    lane = lax.broadcasted_iota(jnp.int32, tile.shape, 1)
    return jnp.sum(jnp.where(lane == hh, tile, 0.0), axis=-1, keepdims=True)


def _lane_place(cols):
    rows = cols[0].shape[0]
    lane = lax.broadcasted_iota(jnp.int32, (rows, HEAD), 1)
    tile = jnp.zeros((rows, HEAD), F32)
    for hh, v in enumerate(cols):
        tile = jnp.where(lane == hh, v, tile)
    return tile


def _band_masks():
    qi = lax.broadcasted_iota(jnp.int32, (ATTN_SPAN, ATTN_SPAN), 0)
    kj = lax.broadcasted_iota(jnp.int32, (ATTN_SPAN, ATTN_SPAN), 1)
    return kj <= qi, kj >= qi


ATTN_TILE_BLOCKS = 8


def _attn_fwd(a):
    d, L, _ = a.shape
    B, W = min(ATTN_TILE_BLOCKS, a.shape[1] // ATTN_SPAN), ATTN_SPAN
    T = B * W
    assert L % T == 0
    steps = L // T
    scale = HEAD ** -0.5

    def body(q_ref, kc_ref, kp_ref, vc_ref, vp_ref, o_ref, lse_ref):
        n = pl.program_id(1)
        mask_c, mask_p0 = _band_masks()
        first = jnp.logical_and(mask_p0, n > 0)
        units = [(b, hh) for b in range(B) for hh in range(HEADS_PER_GROUP)]
        rows = [slice(b * W, (b + 1) * W) for b in range(B)]
        cols = [slice(hh * HEAD, (hh + 1) * HEAD) for hh in range(HEADS_PER_GROUP)]

        def prev_keys(ref, tile, b, hh):
            return ref[:, cols[hh]] if b == 0 else tile[rows[b - 1], cols[hh]]

        s_c = [jnp.where(mask_c, _dot_nt(q_ref[rows[b], cols[hh]], kc_ref[rows[b], cols[hh]]) * scale, NEG) for b, hh in units]
        s_p = [jnp.where(first if b == 0 else mask_p0,
                         _dot_nt(q_ref[rows[b], cols[hh]], prev_keys(kp_ref, kc_ref, b, hh)) * scale, NEG) for b, hh in units]
        m = [jnp.maximum(jnp.max(x, axis=-1, keepdims=True), jnp.max(y, axis=-1, keepdims=True)) for x, y in zip(s_c, s_p)]
        p_c = [jnp.exp(x - mm) for x, mm in zip(s_c, m)]
        p_p = [jnp.exp(y - mm) for y, mm in zip(s_p, m)]
        l = [jnp.sum(x, axis=-1, keepdims=True) + jnp.sum(y, axis=-1, keepdims=True) for x, y in zip(p_c, p_p)]
        acc = [_dot(p_c[i].astype(BF16), vc_ref[rows[b], cols[hh]]) + _dot(p_p[i].astype(BF16), prev_keys(vp_ref, vc_ref, b, hh))
               for i, (b, hh) in enumerate(units)]
        for i, (b, hh) in enumerate(units):
            o_ref[rows[b], cols[hh]] = (acc[i] / l[i]).astype(BF16)
        for b in range(B):
            lse_ref[rows[b], :] = _lane_place([m[i] + jnp.log(l[i]) for i, (bb, _) in enumerate(units) if bb == b])

    def cur(part):
        return pl.BlockSpec((None, T, GROUP_W), functools.partial(lambda r, n, p: (r, n, p), p=part))

    def prev(part):
        return pl.BlockSpec((None, W, GROUP_W), functools.partial(lambda r, n, p: (r, jnp.maximum(n * B - 1, 0), p), p=part))

    return pl.pallas_call(
        body, name=f"attn_fwd_d{d}", grid=(d, steps),
        in_specs=[cur(0), cur(1), prev(1), cur(2), prev(2)],
        out_specs=[pl.BlockSpec((None, T, GROUP_W), lambda r, n: (r, n, 0)), pl.BlockSpec((None, T, HEAD), lambda r, n: (r, n, 0))],
        out_shape=[jax.ShapeDtypeStruct((d, L, GROUP_W), BF16), jax.ShapeDtypeStruct((d, L, HEAD), F32)],
        compiler_params=_params(("parallel", "arbitrary")))(a, a, a, a, a)


def _attn_bwd(a, do, lse, dd):
    d, L, _ = a.shape
    B, W = min(ATTN_TILE_BLOCKS, a.shape[1] // ATTN_SPAN), ATTN_SPAN
    T = B * W
    assert L % T == 0
    steps = L // T
    scale = HEAD ** -0.5

    def body(qc_ref, qn_ref, kp_ref, kc_ref, vp_ref, vc_ref, doc_ref, don_ref, lc_ref, ln_ref, ddc_ref, ddn_ref, da_ref):
        n = pl.program_id(1)
        mask_c, mask_p0 = _band_masks()
        first = jnp.logical_and(mask_p0, n > 0)
        last = jnp.logical_and(mask_p0, n < steps - 1)
        H4 = range(HEADS_PER_GROUP)
        units = [(b, hh) for b in range(B) for hh in H4]
        rows = [slice(b * W, (b + 1) * W) for b in range(B)]
        cols = [slice(hh * HEAD, (hh + 1) * HEAD) for hh in H4]
        q = {u: qc_ref[rows[u[0]], cols[u[1]]] for u in units}
        k = {u: kc_ref[rows[u[0]], cols[u[1]]] for u in units}
        v = {u: vc_ref[rows[u[0]], cols[u[1]]] for u in units}
        g_o = {u: doc_ref[rows[u[0]], cols[u[1]]] for u in units}
        kb = {(b, hh): kp_ref[:, cols[hh]] if b == 0 else k[(b - 1, hh)] for b, hh in units}
        vb = {(b, hh): vp_ref[:, cols[hh]] if b == 0 else v[(b - 1, hh)] for b, hh in units}
        lse_t = {(b, hh): _lane_pick(lc_ref[rows[b], :], hh) for b, hh in units}
        dd_t = {(b, hh): _lane_pick(ddc_ref[rows[b], :], hh) for b, hh in units}
        p_c = {u: jnp.where(mask_c, jnp.exp(_dot_nt(q[u], k[u]) * scale - lse_t[u]), 0.0) for u in units}
        p_p = {u: jnp.where(first if u[0] == 0 else mask_p0, jnp.exp(_dot_nt(q[u], kb[u]) * scale - lse_t[u]), 0.0) for u in units}
        ds_c = {u: (p_c[u] * (_dot_nt(g_o[u], v[u]) + dd_t[u])).astype(BF16) for u in units}
        ds_p = {u: (p_p[u] * (_dot_nt(g_o[u], vb[u]) + dd_t[u])).astype(BF16) for u in units}
        qn = [qn_ref[:, c] for c in cols]
        g_n = [don_ref[:, c] for c in cols]
        p_n = [jnp.where(last, jnp.exp(_dot_nt(qn[hh], k[(B - 1, hh)]) * scale - _lane_pick(ln_ref[...], hh)), 0.0) for hh in H4]
        ds_n = [(p_n[hh] * (_dot_nt(g_n[hh], v[(B - 1, hh)]) + _lane_pick(ddn_ref[...], hh))).astype(BF16) for hh in H4]
        dq = {u: (_dot(ds_c[u], k[u]) + _dot(ds_p[u], kb[u])) * scale for u in units}
        dk, dv = {}, {}
        for b, hh in units:
            if b < B - 1:
                nxt = (b + 1, hh)
                dk[(b, hh)] = (_dot_tn(ds_c[(b, hh)], q[(b, hh)]) + _dot_tn(ds_p[nxt], q[nxt])) * scale
                dv[(b, hh)] = _dot_tn(p_c[(b, hh)].astype(BF16), g_o[(b, hh)]) + _dot_tn(p_p[nxt].astype(BF16), g_o[nxt])
            else:
                dk[(b, hh)] = (_dot_tn(ds_c[(b, hh)], q[(b, hh)]) + _dot_tn(ds_n[hh], qn[hh])) * scale
                dv[(b, hh)] = _dot_tn(p_c[(b, hh)].astype(BF16), g_o[(b, hh)]) + _dot_tn(p_n[hh].astype(BF16), g_n[hh])
        for b, hh in units:
            da_ref[rows[b], cols[hh]] = dq[(b, hh)].astype(BF16)
            da_ref[rows[b], GROUP_W + hh * HEAD:GROUP_W + (hh + 1) * HEAD] = dk[(b, hh)].astype(BF16)
            da_ref[rows[b], 2 * GROUP_W + hh * HEAD:2 * GROUP_W + (hh + 1) * HEAD] = dv[(b, hh)].astype(BF16)

    nb = L // W

    def cur(width, part):
        return pl.BlockSpec((None, T, width), functools.partial(lambda r, n, p: (r, n, p), p=part))

    def prev(width, part):
        return pl.BlockSpec((None, W, width), functools.partial(lambda r, n, p: (r, jnp.maximum(n * B - 1, 0), p), p=part))

    def nxt(width, part):
        return pl.BlockSpec((None, W, width), functools.partial(lambda r, n, p: (r, jnp.minimum(n * B + B, nb - 1), p), p=part))

    g = GROUP_W
    return pl.pallas_call(
        body, name=f"attn_bwd_d{d}", grid=(d, steps),
        in_specs=[cur(g, 0), nxt(g, 0), prev(g, 1), cur(g, 1), prev(g, 2), cur(g, 2),
                  cur(g, 0), nxt(g, 0), cur(HEAD, 0), nxt(HEAD, 0), cur(HEAD, 0), nxt(HEAD, 0)],
        out_specs=pl.BlockSpec((None, T, 3 * g), lambda r, n: (r, n, 0)),
        out_shape=jax.ShapeDtypeStruct((d, L, 3 * g), BF16),
        compiler_params=_params(("parallel", "arbitrary")))(
            a, a, a, a, a, a, do, do, lse, lse, dd, dd)


def _softmax3(ls):
    mx = jnp.maximum(jnp.maximum(ls[0], ls[1]), ls[2])
    es = [jnp.exp(v - mx) for v in ls]
    tot = es[0] + es[1] + es[2]
    return [e / tot for e in es]


HEAD_COLS = [slice(hh * HEAD, (hh + 1) * HEAD) for hh in range(HEADS_PER_GROUP)]


def _group_spec(d, tm):
    return pl.BlockSpec((d, tm // d, GROUP_W), lambda i: (0, i, 0))


def _gather_heads(ref, scr, d, tm):
    if d == 1:
        return [ref[0, :, cols].astype(F32) for cols in HEAD_COLS]
    for hh, cols in enumerate(HEAD_COLS):
        for r in range(d):
            scr.at[hh][pl.ds(r, tm // d, stride=d), :] = ref[r, :, cols].astype(F32)
    return [scr[hh] for hh in range(HEADS_PER_GROUP)]


def _tile_spec(d, tm):
    return pl.BlockSpec((d, tm // d, HEAD), lambda i: (0, i, 0))


def _gather_tile(ref, scr, d, tm):
    if d == 1:
        return ref[0]
    for r in range(d):
        scr[pl.ds(r, tm // d, stride=d), :] = ref[r]
    return scr[...]


def _scatter_tile(val, scr, ref, d, tm):
    if d == 1:
        ref[0] = val
        return
    scr[...] = val
    for r in range(d):
        ref[r] = scr[pl.ds(r, tm // d, stride=d), :]


def _scatter_heads(vals, scr, ref, d, tm):
    if d == 1:
        for cols, v in zip(HEAD_COLS, vals):
            ref[0, :, cols] = v.astype(ref.dtype)
        return
    for hh, v in enumerate(vals):
        scr[hh] = v
    for hh, cols in enumerate(HEAD_COLS):
        for r in range(d):
            ref[r, :, cols] = scr.at[hh][pl.ds(r, tm // d, stride=d), :].astype(ref.dtype)


def _qkv_dilated(h, gain, w4, gi, cos, sin, d, tm=2048):
    S, K = h.shape
    n_shard = w4.shape[2]
    assert n_shard % HEAD == 0

    def head_cols(hh):
        def index(i, p):
            c = p * (len(ATTN_GROUPS) * GROUP_W) + gi * GROUP_W + hh * HEAD
            return c // n_shard, 0, (c % n_shard) // HEAD
        return pl.BlockSpec((None, K, HEAD), index)

    def body(h_ref, g_ref, *refs):
        w_refs, (cos_ref, sin_ref, out_ref, u_ref, y_scr) = refs[:HEADS_PER_GROUP], refs[HEADS_PER_GROUP:]
        p = pl.program_id(1)

        @pl.when(p == 0)
        def _():
            v = h_ref[...]
            u_ref[...] = (v * _rstd(v) * g_ref[...]).astype(BF16)

        y = _dot(u_ref[...], jnp.concatenate([r[...] for r in w_refs], axis=1))
        heads = [slice(hh * HEAD, (hh + 1) * HEAD) for hh in range(HEADS_PER_GROUP)]
        if d > 1:
            for hh, cols in enumerate(heads):
                y_scr[hh] = y[:, cols]

        def rows_of(hh, r):
            return y[:, heads[hh]] if d == 1 else y_scr.at[hh][pl.ds(r, tm // d, stride=d), :]

        @pl.when(p < 2)
        def _():
            for r in range(d):
                rows = slice(None) if d == 1 else pl.ds(r, tm // d, stride=d)
                cr, sr = cos_ref[rows, :], sin_ref[rows, :]
                for hh, cols in enumerate(heads):
                    out_ref[r, :, cols] = _rope(rows_of(hh, r), cr, sr).astype(BF16)

        @pl.when(p == 2)
        def _():
            for r in range(d):
                for hh, cols in enumerate(heads):
                    out_ref[r, :, cols] = rows_of(hh, r).astype(BF16)

    tab = pl.BlockSpec((tm, HEAD), lambda i, p: (i, 0))
    return pl.pallas_call(
        body, name=f"attn_qkv_d{d}", grid=(S // tm, 3),
        in_specs=[pl.BlockSpec((tm, K), lambda i, p: (i, 0)),
                  pl.BlockSpec((1, K), lambda i, p: (0, 0)),
                  *[head_cols(hh) for hh in range(HEADS_PER_GROUP)], tab, tab],
        out_specs=[pl.BlockSpec((d, tm // d, GROUP_W), lambda i, p: (0, i, p)), pl.BlockSpec((tm, K), lambda i, p: (i, 0))],
        out_shape=[jax.ShapeDtypeStruct((d, S // d, 3 * GROUP_W), BF16), jax.ShapeDtypeStruct((S, K), BF16)],
        scratch_shapes=[pltpu.VMEM((HEADS_PER_GROUP, tm, HEAD), F32)],
        compiler_params=_params(("parallel", "arbitrary")))(h, gain, *[w4] * HEADS_PER_GROUP, cos, sin)


def _undilate_group(da, dqkv, cos, sin, g, tm=2048):
    d, L, _ = da.shape
    S = d * L
    G = len(ATTN_GROUPS)

    def body(*refs):
        da_ref, cos_ref, sin_ref, out_ref, scr = refs[0], refs[1], refs[2], refs[-2], refs[-1]
        p = pl.program_id(1)
        heads = [slice(hh * HEAD, (hh + 1) * HEAD) for hh in range(HEADS_PER_GROUP)]
        if d > 1:
            for hh, cols in enumerate(heads):
                for r in range(d):
                    scr.at[hh][pl.ds(r, tm // d, stride=d), :] = da_ref[r, :, cols].astype(F32)

        def tokens(hh):
            return da_ref[0, :, heads[hh]].astype(F32) if d == 1 else scr[hh]

        @pl.when(p < 2)
        def _():
            cr, sr = cos_ref[...], -sin_ref[...]
            for hh, cols in enumerate(heads):
                out_ref[:, cols] = _rope(tokens(hh), cr, sr).astype(BF16)

        @pl.when(p == 2)
        def _():
            for hh, cols in enumerate(heads):
                out_ref[:, cols] = tokens(hh).astype(BF16)

    tab = pl.BlockSpec((tm, HEAD), lambda i, p: (i, 0))
    operands = (da, cos, sin) if dqkv is None else (da, cos, sin, dqkv)
    return pl.pallas_call(
        body, name=f"attn_undilate_d{d}", grid=(S // tm, 3),
        in_specs=[pl.BlockSpec((d, tm // d, GROUP_W), lambda i, p: (0, i, p)), tab, tab] + ([] if dqkv is None else [ANY]),
        out_specs=pl.BlockSpec((tm, GROUP_W), lambda i, p: (i, p * G + g)),
        out_shape=jax.ShapeDtypeStruct((S, 3 * G * GROUP_W), BF16),
        input_output_aliases={} if dqkv is None else {3: 0},
        scratch_shapes=[pltpu.VMEM((HEADS_PER_GROUP, tm, HEAD), F32)],
        compiler_params=_params(("parallel", "arbitrary")))(*operands)


def _attn_merge(os_, lses, h, w2, tm=1024):
    G = len(os_)
    S, N = h.shape

    def body(*refs):
        o_refs, l_refs, h_ref, w_ref, res_ref, out_ref = refs[:G], refs[G:2 * G], *refs[2 * G:2 * G + 4]
        scr = refs[2 * G + 4:]
        o = [_gather_heads(o_refs[g], scr[g], d, tm) for g, (_, d) in enumerate(ATTN_GROUPS)]
        l = [_gather_tile(l_refs[g], scr[G + g].at[0], d, tm) for g, (_, d) in enumerate(ATTN_GROUPS)]
        for hh in range(HEADS_PER_GROUP):
            al = _softmax3([_lane_pick(l[g], hh) for g in range(G)])
            for g in range(G):
                out_ref[:, g * GROUP_W + hh * HEAD:g * GROUP_W + (hh + 1) * HEAD] = (o[g][hh] * al[g]).astype(BF16)
        res_ref[...] = h_ref[...] + _dot(out_ref[...], w_ref[...])

    specs = [_group_spec(d, tm) for _, d in ATTN_GROUPS]
    row = pl.BlockSpec((tm, N), lambda i: (i, 0))
    return pl.pallas_call(
        body, name="attn_merge_out", grid=(S // tm,),
        in_specs=specs + [_tile_spec(d, tm) for _, d in ATTN_GROUPS] + [
            row, pl.BlockSpec((G * GROUP_W, N), lambda i: (0, 0), pipeline_mode=pl.Buffered(1))],
        out_specs=[row, pl.BlockSpec((tm, G * GROUP_W), lambda i: (i, 0))],
        out_shape=[jax.ShapeDtypeStruct((S, N), F32), jax.ShapeDtypeStruct((S, G * GROUP_W), BF16)],
        scratch_shapes=[pltpu.VMEM((HEADS_PER_GROUP, tm, HEAD), F32)] * (2 * G),
        compiler_params=_params(("parallel",)))(*os_, *lses, h, w2)


def _attn_merge_bwd(os_, lses, dh, w2, tm=512):
    G = len(os_)
    S, N = dh.shape

    def body(*refs):
        o_refs, l_refs, dh_ref, w_ref = refs[:G], refs[G:2 * G], refs[2 * G], refs[2 * G + 1]
        do_refs, dd_refs = refs[2 * G + 2:3 * G + 2], refs[3 * G + 2:4 * G + 2]
        scr = refs[4 * G + 2:]
        doa = _dot_nt(dh_ref[...].astype(BF16), w_ref[...])
        o = [_gather_heads(o_refs[g], scr[g], d, tm) for g, (_, d) in enumerate(ATTN_GROUPS)]
        l = [_gather_tile(l_refs[g], scr[G + g].at[0], d, tm) for g, (_, d) in enumerate(ATTN_GROUPS)]
        do = [[None] * HEADS_PER_GROUP for _ in range(G)]
        dd = [[None] * HEADS_PER_GROUP for _ in range(G)]
        for hh in range(HEADS_PER_GROUP):
            al = _softmax3([_lane_pick(l[g], hh) for g in range(G)])
            mix = None
            for g in range(G):
                dg = doa[:, g * GROUP_W + hh * HEAD:g * GROUP_W + (hh + 1) * HEAD]
                do[g][hh] = dg * al[g]
                t = al[g] * jnp.sum(dg * o[g][hh], axis=-1, keepdims=True)
                mix = t if mix is None else mix + t
            for g in range(G):
                dd[g][hh] = -al[g] * mix
        for g, (_, d) in enumerate(ATTN_GROUPS):
            _scatter_heads(do[g], scr[2 * G + g], do_refs[g], d, tm)
            _scatter_tile(_lane_place(dd[g]), scr[3 * G + g].at[0], dd_refs[g], d, tm)

    specs = [_group_spec(d, tm) for _, d in ATTN_GROUPS]
    tiles = [_tile_spec(d, tm) for _, d in ATTN_GROUPS]
    do_shapes = [jax.ShapeDtypeStruct((d, S // d, GROUP_W), BF16) for _, d in ATTN_GROUPS]
    dd_shapes = [jax.ShapeDtypeStruct((d, S // d, HEAD), F32) for _, d in ATTN_GROUPS]
    return pl.pallas_call(
        body, name="attn_merge_bwd", grid=(S // tm,),
        in_specs=specs + tiles + [pl.BlockSpec((tm, N), lambda i: (i, 0)),
                                  pl.BlockSpec((G * GROUP_W, N), lambda i: (0, 0), pipeline_mode=pl.Buffered(1))],
        out_specs=specs + tiles,
        out_shape=do_shapes + dd_shapes,
        scratch_shapes=[pltpu.VMEM((HEADS_PER_GROUP, tm, HEAD), F32)] * (4 * G),
        compiler_params=_params(("parallel",)))(*os_, *lses, dh, w2)


def _rope_tables(S):
    inv_freq = (1.0 / (np.float32(ROPE_THETA) ** (np.arange(0, HEAD, 2, dtype=np.float32) / np.float32(HEAD))))
    ang = (np.arange(S, dtype=np.float32)[:, None] * inv_freq.astype(np.float32)[None, :]).astype(np.float64)
    cos, sin = np.cos(ang).astype(np.float32), np.sin(ang).astype(np.float32)
    return jnp.asarray(np.concatenate([cos, cos], axis=-1)), jnp.asarray(np.concatenate([-sin, sin], axis=-1))


def _local_step(x, target, norm_mix, norm_ffn, lb_logits, out_gain, final_norm, comm):
    S = x.shape[0]
    nm0, nm1 = norm_mix[0:1], norm_mix[1:2]
    nf0, nf1 = norm_ffn[0:1], norm_ffn[1:2]
    w = comm.first_weights()

    proj, u0, got = _norm_mm(x, nm0, w["hin"], "hgrn_in", rider=comm.gather_rider(LATE_WEIGHTS_A))
    w.update(comm.gathered(LATE_WEIGHTS_A, got))
    (o, og, states), got = _hgrn_fwd(proj, lb_logits, out_gain, rider=comm.gather_rider(LATE_WEIGHTS_B))
    w.update(comm.gathered(LATE_WEIGHTS_B, got))
    fin_tn = w["fin0"].shape[2]
    h1 = _mm_res(x, og, w["hout"], "hgrn_out")
    z0, u1, _ = _norm_mm(h1, nf0, w["fin0"], "ffn0_in", out_dtype=BF16)
    h2, act0 = _swiglu_mm_res(h1, z0, w["fdn0"], "ffn0_down")
    cos, sin = _rope_tables(S)
    G = len(ATTN_GROUPS)
    a_g, u2 = zip(*[_qkv_dilated(h2, nm1, w["qkv"], gi, cos, sin, d) for gi, (_, d) in enumerate(ATTN_GROUPS)])
    o_g, lse_g = zip(*[_attn_fwd(a) for a in a_g])
    h3, oa = _attn_merge(o_g, lse_g, h2, w["aout"])
    z1, u3, _ = _norm_mm(h3, nf1, w["fin1"], "ffn1_in", out_dtype=BF16)
    h4, act1 = _swiglu_mm_res(h3, z1, w["fdn1"], "ffn1_down")
    dh4, loss, d_final = _loss_head(h4, final_norm, target)

    grads, small = {}, {"final_norm": d_final}

    def ffn_bwd(dh, h_in, u_in, z, act, gain, w_in, w_dn, tag, ride=None):
        dz = _mm_nt_swiglu_bwd(dh, w_dn, z, tag + "_down_dx")
        g_dn = _mm_tn(act, dh, 1, D_MODEL, D_MODEL, tag + "_down_dw")[0]
        g_in = _mm_tn(u_in, dz, N_CHIPS, fin_tn, fin_tn, tag + "_in_dw")
        rider = None if ride is None else ride(g_in, g_dn)
        dh_in, dgain, got = _mm_nt_normbwd(dz, w_in, h_in, gain, dh, tag + "_in_dx", rider=rider)
        return dh_in, dgain, g_in, g_dn, got

    dh3, d_nf1, grads["fin1"], grads["fdn1"], _ = ffn_bwd(dh4, h3, u3, z1, act1, nf1, w["fin1"], w["fdn1"], "ffn1")
    grads["aout"] = _mm_tn(oa, dh3, 1, D_MODEL, D_MODEL, "attn_out_dw")[0]
    merged = _attn_merge_bwd(o_g, lse_g, dh3, w["aout"])
    G = len(ATTN_GROUPS)
    das = [_attn_bwd(a_g[gi], merged[gi], lse_g[gi], merged[G + gi]) for gi in range(G)]
    dqkv = None
    for gi in range(G):
        dqkv = _undilate_group(das[gi], dqkv, cos, sin, gi)
    n_qkv = w["qkv"].shape[2]
    grads["qkv"] = _mm_tn(u2[0], dqkv, N_CHIPS, n_qkv, n_qkv, "attn_qkv_dw")
    dh2, d_nm1, _ = _mm_nt_normbwd(dqkv, w["qkv"], h2, nm1, dh3, "attn_qkv_dx")

    def ride_early(g_in, g_dn):
        return comm.pair_rider({**grads, "fin0": g_in, "fdn0": g_dn}, "early")

    dh1, d_nf0, _, _, got = ffn_bwd(dh2, h1, u1, z0, act0, nf0, w["fin0"], w["fdn0"], "ffn0", ride=ride_early)
    comm.paired("early", got)
    dog = _mm_nt(dh1, w["hout"][None], "hgrn_out_dx")
    (dproj, dlb, dgn), got = _hgrn_bwd(proj, lb_logits, out_gain, o, states, dog, rider=comm.exchange_rider("early"))
    comm.exchanged("early", got)
    late = {"hout": _mm_tn(og, dh1, 1, D_MODEL, D_MODEL, "hgrn_out_dw")[0],
            "hin": _mm_tn(u0, dproj, N_CHIPS, D_MODEL, D_MODEL, "hgrn_in_dw")}
    comm.pair_now(late, "late")
    dx, d_nm0, got = _mm_nt_normbwd(dproj, w["hin"], x, nm0, dh1, "hgrn_in_dx", rider=comm.exchange_rider("late"))
    comm.exchanged("late", got)

    small["norm_mix"] = jnp.concatenate([d_nm0, d_nm1], axis=0)
    small["norm_ffn"] = jnp.concatenate([d_nf0, d_nf1], axis=0)
    small["lb"] = dlb.reshape(1, HGRN_HEADS * HEAD)
    small["out_norm"] = dgn.reshape(HGRN_HEADS, HEAD)
    return loss, dx, small


def _place():
    x, y, c = lax.axis_index("x"), lax.axis_index("y"), lax.axis_index("c")
    others = [(1 - x, y), (x, 1 - y), (1 - x, 1 - y)]
    return x, y, c, others


ANY = pl.BlockSpec(memory_space=pl.ANY)


class _GatherRider:
    def __init__(self, shards):
        self.operands = list(shards)
        n = self.n = len(shards)
        self.out_shape = [jax.ShapeDtypeStruct((N_CHIPS,) + s.shape, s.dtype) for s in shards]
        self.scratch = [pltpu.SemaphoreType.DMA((3 * n,)), pltpu.SemaphoreType.DMA((3 * n,)),
                        pltpu.SemaphoreType.DMA((3 * n,)), pltpu.SemaphoreType.DMA((3 * n,)),
                        pltpu.SemaphoreType.DMA((n,)), pltpu.SemaphoreType.DMA((n,))]

    def _copies(self, ins, outs, sems):
        ici_send, ici_recv, _, _, own_send, own_recv = sems
        x, y, c, others = _place()
        me = 2 * x + y
        own = [pltpu.make_async_remote_copy(
            src_ref=ins[a], dst_ref=outs[a].at[me], send_sem=own_send.at[a], recv_sem=own_recv.at[a],
            device_id=(x, y, 1 - c), device_id_type=MESH) for a in range(self.n)]
        sends = [pltpu.make_async_remote_copy(
            src_ref=ins[a].at[c], dst_ref=outs[a].at[me, c], send_sem=ici_send.at[a * 3 + k], recv_sem=ici_recv.at[a * 3 + k],
            device_id=(ox, oy, c), device_id_type=MESH) for a in range(self.n) for k, (ox, oy) in enumerate(others)]
        return own, sends

    def start(self, ins, outs, sems):
        own, sends = self._copies(ins, outs, sems)
        for cp in own + sends:
            cp.start()

    def finish(self, ins, outs, sems):
        ici_send, ici_recv, d2d_send, d2d_recv, _, _ = sems
        x, y, c, others = _place()
        sibling = (x, y, 1 - c)
        own, sends = self._copies(ins, outs, sems)
        passes = []
        for a in range(self.n):
            for k, (ox, oy) in enumerate(others):
                s = a * 3 + k
                got = outs[a].at[2 * ox + oy, c]
                pltpu.make_async_remote_copy(
                    src_ref=got, dst_ref=got, send_sem=ici_send.at[s], recv_sem=ici_recv.at[s],
                    device_id=(ox, oy, c), device_id_type=MESH).wait_recv()
                fwd = pltpu.make_async_remote_copy(
                    src_ref=got, dst_ref=got, send_sem=d2d_send.at[s], recv_sem=d2d_recv.at[s],
                    device_id=sibling, device_id_type=MESH)
                fwd.start()
                passes.append(fwd)
        for a in range(self.n):
            for k, (ox, oy) in enumerate(others):
                s = a * 3 + k
                theirs = outs[a].at[2 * ox + oy, 1 - c]
                pltpu.make_async_remote_copy(
                    src_ref=theirs, dst_ref=theirs, send_sem=d2d_send.at[s], recv_sem=d2d_recv.at[s],
                    device_id=sibling, device_id_type=MESH).wait_recv()
        for cp in own:
            cp.wait()
        for cp in sends + passes:
            cp.wait_send()


class _PairRider:
    def __init__(self, grads):
        self.operands = list(grads)
        n = self.n = len(grads)
        self.out_shape = [jax.ShapeDtypeStruct((N_CHIPS,) + g.shape[2:], F32) for g in grads]
        self.scratch = [pltpu.SemaphoreType.DMA((N_CHIPS * n,)), pltpu.SemaphoreType.DMA((N_CHIPS * n,))]

    def _copies(self, ins, outs, sems):
        send_sem, recv_sem = sems
        x, y, c, _ = _place()
        return [pltpu.make_async_remote_copy(
            src_ref=ins[a].at[j, 1 - c], dst_ref=outs[a].at[j], send_sem=send_sem.at[a * N_CHIPS + j],
            recv_sem=recv_sem.at[a * N_CHIPS + j], device_id=(x, y, 1 - c), device_id_type=MESH)
            for a in range(self.n) for j in range(N_CHIPS)]

    def start(self, ins, outs, sems):
        for cp in self._copies(ins, outs, sems):
            cp.start()

    def finish(self, ins, outs, sems):
        for cp in self._copies(ins, outs, sems):
            cp.wait()


class _ExchangeRider:
    def __init__(self, parts):
        self.operands = list(parts)
        n = self.n = len(parts)
        self.out_shape = [jax.ShapeDtypeStruct(p.shape, p.dtype) for p in parts]
        self.scratch = [pltpu.SemaphoreType.DMA((3 * n,)), pltpu.SemaphoreType.DMA((3 * n,))]

    def _copies(self, ins, outs, sems):
        send_sem, recv_sem = sems
        x, y, c, others = _place()
        me = 2 * x + y
        return [pltpu.make_async_remote_copy(
            src_ref=ins[a].at[2 * ox + oy], dst_ref=outs[a].at[me], send_sem=send_sem.at[a * 3 + k],
            recv_sem=recv_sem.at[a * 3 + k], device_id=(ox, oy, c), device_id_type=MESH)
            for a in range(self.n) for k, (ox, oy) in enumerate(others)]

    def start(self, ins, outs, sems):
        for cp in self._copies(ins, outs, sems):
            cp.start()

    def finish(self, ins, outs, sems):
        send_sem, recv_sem = sems
        x, y, c, others = _place()
        for a in range(self.n):
            for k, (ox, oy) in enumerate(others):
                s = a * 3 + k
                got = outs[a].at[2 * ox + oy]
                pltpu.make_async_remote_copy(
                    src_ref=got, dst_ref=got, send_sem=send_sem.at[s], recv_sem=recv_sem.at[s],
                    device_id=(ox, oy, c), device_id_type=MESH).wait_recv()
        for cp in self._copies(ins, outs, sems):
            cp.wait_send()


def _run_rider(rider, name):
    n = rider.n

    def body(*refs):
        ins, outs, sems = refs[:n], refs[n:2 * n], refs[2 * n:]
        rider.start(ins, outs, sems)
        rider.finish(ins, outs, sems)

    return pl.pallas_call(
        body, name=name, in_specs=[ANY] * n, out_specs=[ANY] * n,
        out_shape=rider.out_shape, scratch_shapes=rider.scratch)(*rider.operands)


def _ride(rider, body, n_in, n_out, first, last):
    if rider is None:
        return body
    n = rider.n

    def wrapped(*refs):
        host_in, r_in = refs[:n_in], refs[n_in:n_in + n]
        host_out = refs[n_in + n:n_in + n + n_out]
        r_out = refs[n_in + n + n_out:n_in + 2 * n + n_out]
        rest = refs[n_in + 2 * n + n_out:]
        host_scr, sems = rest[:len(rest) - len(rider.scratch)], rest[len(rest) - len(rider.scratch):]

        @pl.when(first())
        def _():
            rider.start(r_in, r_out, sems)

        body(*host_in, *host_out, *host_scr)

        @pl.when(last())
        def _():
            rider.finish(r_in, r_out, sems)

    return wrapped


def _rider_args(rider):
    if rider is None:
        return [], [], [], [], []
    return rider.operands, [ANY] * rider.n, [ANY] * rider.n, rider.out_shape, rider.scratch


def _pair_sum(g, got, c_idx):
    _, _, r, cw = g.shape
    tr = _row_tile(r, cw)

    def body(c_ref, g_ref, got_ref, pb_ref):
        pb_ref[...] = (g_ref[...] + got_ref[...]).astype(BF16)

    blk = pl.BlockSpec((None, tr, cw), lambda j, i, c_ref: (j, i, 0))
    return pl.pallas_call(
        body, name="grad_pair_sum",
        grid_spec=pltpu.PrefetchScalarGridSpec(
            num_scalar_prefetch=1, grid=(N_CHIPS, r // tr),
            in_specs=[pl.BlockSpec((None, None, tr, cw), lambda j, i, c_ref: (j, c_ref[0], i, 0)), blk],
            out_specs=blk),
        out_shape=jax.ShapeDtypeStruct((N_CHIPS, r, cw), BF16),
        compiler_params=_params(("parallel", "parallel")))(c_idx, g, got)


def _chip_sum(g, sib, got, place):
    _, _, r, cw = g.shape
    tr = _row_tile(r, cw)

    def body(place_ref, g_ref, sib_ref, got_ref, t_ref):
        me = place_ref[0]
        own = g_ref[...] + sib_ref[...]
        acc = None
        for s in range(N_CHIPS):
            term = jnp.where(me == s, own, got_ref[s].astype(F32))
            acc = term if acc is None else acc + term
        t_ref[...] = acc

    return pl.pallas_call(
        body, name="grad_chip_sum",
        grid_spec=pltpu.PrefetchScalarGridSpec(
            num_scalar_prefetch=1, grid=(r // tr,),
            in_specs=[pl.BlockSpec((None, None, tr, cw), lambda i, pr: (pr[0], pr[1], i, 0)),
                      pl.BlockSpec((None, tr, cw), lambda i, pr: (pr[0], i, 0)),
                      pl.BlockSpec((N_CHIPS, tr, cw), lambda i, pr: (0, i, 0))],
            out_specs=pl.BlockSpec((tr, cw), lambda i, pr: (i, 0))),
        out_shape=jax.ShapeDtypeStruct((r, cw), F32),
        compiler_params=_params(("parallel",)))(place, g, sib, got)


def _pair_share(halves):
    n = len(halves)

    def body(*refs):
        ins, outs = refs[:n], refs[n:2 * n]
        send_sem, recv_sem = refs[2 * n:]
        x, y, c, _ = _place()
        cps = [pltpu.make_async_remote_copy(
            src_ref=ins[a], dst_ref=outs[a], send_sem=send_sem.at[a], recv_sem=recv_sem.at[a],
            device_id=(x, y, 1 - c), device_id_type=MESH) for a in range(n)]
        for cp in cps:
            cp.start()
        for cp in cps:
            cp.wait()

    return pl.pallas_call(
        body, name="grad_pair_share",
        in_specs=[ANY] * n, out_specs=[ANY] * n,
        out_shape=[jax.ShapeDtypeStruct(h.shape, F32) for h in halves],
        scratch_shapes=[pltpu.SemaphoreType.DMA((n,)), pltpu.SemaphoreType.DMA((n,))],
        )(*halves)


def _small_allreduce(pack):
    m_per, ncol = pack.shape
    n_dev = 8

    def body(x_ref, sum_ref, all_ref, send_sems, recv_sems, local_sem):
        x, y, c, others = _place()
        me, sibling = (x, y, c), (x, y, 1 - c)

        def rows(px, py, pc):
            return all_ref.at[pl.ds((4 * px + 2 * py + pc) * m_per, m_per), :]

        def copy(k, block, to, src=None):
            return pltpu.make_async_remote_copy(
                src_ref=rows(*block) if src is None else src, dst_ref=rows(*block),
                send_sem=send_sems.at[k], recv_sem=recv_sems.at[k], device_id=to, device_id_type=MESH)

        mine = pltpu.make_async_copy(x_ref, rows(*me), local_sem)
        mine.start()
        first = [copy(0, me, sibling, src=x_ref)]
        first += [copy(1 + j, me, (*chip, c), src=x_ref) for j, chip in enumerate(others)]
        for cp in first:
            cp.start()
        passed = [copy(4 + j, (*chip, c), sibling) for j, chip in enumerate(others)]
        for j, chip in enumerate(others):
            copy(1 + j, (*chip, c), me).wait_recv()
            passed[j].start()
        copy(0, sibling, me).wait_recv()
        for j, chip in enumerate(others):
            copy(4 + j, (*chip, 1 - c), me).wait_recv()
        for cp in first + passed:
            cp.wait_send()
        mine.wait()
        acc = all_ref[0:m_per, :]
        for dvc in range(1, n_dev):
            acc = acc + all_ref[dvc * m_per:(dvc + 1) * m_per, :]
        sum_ref[...] = acc

    return pl.pallas_call(
        body, name="small_allreduce",
        in_specs=[pl.BlockSpec(memory_space=pltpu.VMEM)],
        out_specs=pl.BlockSpec(memory_space=pltpu.VMEM),
        out_shape=jax.ShapeDtypeStruct((m_per, ncol), F32),
        scratch_shapes=[pltpu.VMEM((n_dev * m_per, ncol), F32),
                        pltpu.SemaphoreType.DMA((7,)), pltpu.SemaphoreType.DMA((7,)), pltpu.SemaphoreType.DMA],
        )(pack)


def _adam_math(w, g, m, v):
    m = ADAM_B1 * m + (1.0 - ADAM_B1) * g
    v = ADAM_B2 * v + (1.0 - ADAM_B2) * (g * g)
    m_hat = m / (1.0 - ADAM_B1 ** ADAM_STEP)
    v_hat = v / (1.0 - ADAM_B2 ** ADAM_STEP)
    delta = -ADAM_LR * (m_hat / (jnp.sqrt(v_hat) + ADAM_EPS) + ADAM_WD * w)
    return delta, m, v


def _adamw(halves, c_idx, w, m, v, name):
    L = len(halves)
    r, C = halves[0][0].shape
    tr = _row_tile(r, C, 1024 * 1024)
    nt = r // tr

    def body(c_ref, *refs):
        g_refs, (w_ref, m_ref, v_ref), (g_ref, d_ref, nm_ref, nv_ref) = refs[:2 * L], refs[2 * L:2 * L + 3], refs[2 * L + 3:]
        own = pl.program_id(1) == c_ref[0]
        g = None
        for l in range(L):
            cand = jnp.where(own, g_refs[2 * l][...], g_refs[2 * l + 1][...])
            g = cand if g is None else jnp.where(pl.program_id(0) == l, cand, g)
        g_ref[...] = g
        d_ref[...], nm_ref[...], nv_ref[...] = _adam_math(w_ref[...], g, m_ref[...], v_ref[...])

    def half(l, mine):
        def index(ll, h, i, c_ref):
            read = (h == c_ref[0]) if mine else (h != c_ref[0])
            return jnp.where(jnp.logical_and(ll == l, read), i, 0), 0
        return pl.BlockSpec((tr, C), index)

    full = pl.BlockSpec((None, tr, C), lambda ll, h, i, c_ref: (ll, h * nt + i, 0))
    shp = jax.ShapeDtypeStruct((L, 2 * r, C), F32)
    g_specs = [half(l, mine) for l in range(L) for mine in (True, False)]
    return pl.pallas_call(
        body, name=name,
        grid_spec=pltpu.PrefetchScalarGridSpec(
            num_scalar_prefetch=1, grid=(L, 2, nt),
            in_specs=g_specs + [full] * 3, out_specs=[full] * 4),
        out_shape=[shp] * 4,
        compiler_params=_params(("arbitrary", "arbitrary", "arbitrary")))(
            c_idx, *[a for pair in halves for a in pair], w, m, v)


def _small_update(gsum, logits_pack, w, m, v):
    def body(gs_ref, lg_ref, w_ref, m_ref, v_ref, g_ref, d_ref, nm_ref, nv_ref):
        g_ref[...] = gs_ref[...]
        l0, l1, l2 = lg_ref[0:1, :], lg_ref[1:2, :], lg_ref[2:3, :]
        mx = jnp.maximum(jnp.maximum(l0, l1), l2)
        e0, e1, e2 = jnp.exp(l0 - mx), jnp.exp(l1 - mx), jnp.exp(l2 - mx)
        tot = e0 + e1 + e2
        p0, p1, p2 = e0 / tot, e1 / tot, e2 / tot
        dlb = gs_ref[4:5, :]
        g_ref[4:5, :] = dlb * p0 * (1.0 - p0)
        g_ref[5:6, :] = -dlb * p0 * p1
        g_ref[6:7, :] = -dlb * p0 * p2
        d_ref[...], nm_ref[...], nv_ref[...] = _adam_math(w_ref[...], g_ref[...], m_ref[...], v_ref[...])

    full = pl.BlockSpec(memory_space=pltpu.VMEM)
    shp = jax.ShapeDtypeStruct(gsum.shape, F32)
    return pl.pallas_call(
        body, name="small_update", in_specs=[full] * 5, out_specs=[full] * 4, out_shape=[shp] * 4)(
            gsum, logits_pack, w, m, v)


def _pack_small(norm_mix, norm_ffn, lb3, out_norm, final_norm, extra=None):
    ncol = norm_mix.shape[1]
    on = jnp.pad(out_norm.reshape(1, -1), ((0, 0), (0, ncol - out_norm.size)))
    rows = [norm_mix, norm_ffn, lb3, on, final_norm.reshape(1, ncol)]
    if extra is not None:
        rows.append(extra)
    used = sum(r.shape[0] for r in rows)
    rows.append(jnp.zeros((SMALL_ROWS - used, ncol), F32))
    return jnp.concatenate(rows, axis=0)


WEIGHT_NAMES = ("hin", "hout", "qkv", "aout", "fin0", "fin1", "fdn0", "fdn1")
FIRST_WEIGHTS = ("hin",)
LATE_WEIGHTS_A = ("hout", "fin0", "fdn0")
LATE_WEIGHTS_B = ("qkv", "aout", "fin1", "fdn1")


def _split_weights(hgrn_w_in, hgrn_w_out, attn_w_qkv, attn_w_out, ffn_w_in, ffn_w_down):
    return {"hin": hgrn_w_in[0], "hout": hgrn_w_out[0], "qkv": attn_w_qkv[0], "aout": attn_w_out[0],
            "fin0": ffn_w_in[0], "fin1": ffn_w_in[1], "fdn0": ffn_w_down[0], "fdn1": ffn_w_down[1]}


def _halves(v):
    r, c = v.shape
    return v.reshape(2, r // 2, c)


def _full_weights(gathered):
    out = {}
    for k, g in gathered.items():
        _, _, r, c = g.shape
        if k in ("hin", "qkv", "fin0", "fin1"):
            out[k] = g.reshape(N_CHIPS, 2 * r, c)
        else:
            out[k] = g.reshape(N_CHIPS * 2 * r, c)
    return out


class _StepComm:
    def __init__(self, shards, c_idx, me_idx):
        self.shards, self.c_idx, self.me_idx = shards, c_idx, me_idx
        self.halves = {}
        self._stage = {}

    def gather_rider(self, names):
        return _GatherRider([_halves(self.shards[k].astype(BF16)) for k in names])

    def gathered(self, names, got):
        return _full_weights(dict(zip(names, got)))

    def first_weights(self):
        return self.gathered(FIRST_WEIGHTS, _run_rider(self.gather_rider(FIRST_WEIGHTS), "gather_first"))

    def pair_rider(self, grads, tag):
        names = list(grads)
        g4 = []
        for k in names:
            r, c = self.shards[k].shape
            g4.append(grads[k].reshape(N_CHIPS, 2, r // 2, c))
        self._stage[tag] = (names, g4)
        return _PairRider(g4)

    def pair_now(self, grads, tag):
        self.paired(tag, _run_rider(self.pair_rider(grads, tag), "grad_pair_exchange_" + tag))

    def paired(self, tag, got):
        names, g4 = self._stage[tag]
        self._stage[tag] = (names, [(g, s, _pair_sum(g, s, self.c_idx)) for g, s in zip(g4, got)])

    def exchange_rider(self, tag):
        return _ExchangeRider([s[2] for s in self._stage[tag][1]])

    def exchanged(self, tag, got):
        names, sums = self._stage.pop(tag)
        place = jnp.concatenate([self.me_idx, self.c_idx])
        for k, (g, sib, _), recv in zip(names, sums, got):
            self.halves[k] = _chip_sum(g, sib, recv, place)

    def shared_halves(self):
        mine = [self.halves[k] for k in WEIGHT_NAMES]
        return dict(zip(WEIGHT_NAMES, zip(mine, _pair_share(mine))))


def kernel(x, norm_mix, norm_ffn, hgrn_w_in, hgrn_lb_logits, hgrn_out_norm, hgrn_w_out, attn_w_qkv, attn_w_out, ffn_w_in, ffn_w_down, final_norm, loss_target, m_norm_mix, m_norm_ffn, m_hgrn_w_in, m_hgrn_lb_logits, m_hgrn_out_norm, m_hgrn_w_out, m_attn_w_qkv, m_attn_w_out, m_ffn_w_in, m_ffn_w_down, m_final_norm, v_norm_mix, v_norm_ffn, v_hgrn_w_in, v_hgrn_lb_logits, v_hgrn_out_norm, v_hgrn_w_out, v_attn_w_qkv, v_attn_w_out, v_ffn_w_in, v_ffn_w_down, v_final_norm):
    S = x.shape[1]
    xi, yi, ci = lax.axis_index("x"), lax.axis_index("y"), lax.axis_index("c")
    c_idx = jnp.reshape(ci, (1,)).astype(jnp.int32)
    me_idx = jnp.reshape(2 * xi + yi, (1,)).astype(jnp.int32)

    w_own = _split_weights(hgrn_w_in, hgrn_w_out, attn_w_qkv, attn_w_out, ffn_w_in, ffn_w_down)

    comm = _StepComm(w_own, c_idx, me_idx)
    loss, dx, small = _local_step(
        x.reshape(S, D_MODEL), loss_target.reshape(S, D_MODEL), norm_mix, norm_ffn, hgrn_lb_logits,
        hgrn_out_norm, final_norm.reshape(1, D_MODEL), comm)

    halves = comm.shared_halves()
    updated = {}
    for tensor, layers, (wt, mt, vt) in (
            ("hgrn_w_in", ("hin",), (hgrn_w_in, m_hgrn_w_in, v_hgrn_w_in)),
            ("hgrn_w_out", ("hout",), (hgrn_w_out, m_hgrn_w_out, v_hgrn_w_out)),
            ("attn_w_qkv", ("qkv",), (attn_w_qkv, m_attn_w_qkv, v_attn_w_qkv)),
            ("attn_w_out", ("aout",), (attn_w_out, m_attn_w_out, v_attn_w_out)),
            ("ffn_w_in", ("fin0", "fin1"), (ffn_w_in, m_ffn_w_in, v_ffn_w_in)),
            ("ffn_w_down", ("fdn0", "fdn1"), (ffn_w_down, m_ffn_w_down, v_ffn_w_down))):
        updated[tensor] = _adamw([halves[k] for k in layers], c_idx, wt, mt, vt, "adamw_" + tensor)

    loss_row = jnp.pad(loss, ((0, 0), (0, D_MODEL - loss.shape[1])))
    lb3 = jnp.concatenate([small["lb"], jnp.zeros((2, D_MODEL), F32)], axis=0)
    on_grad = jnp.sum(small["out_norm"], axis=0, keepdims=True)
    pack = _pack_small(small["norm_mix"], small["norm_ffn"], lb3, on_grad, small["final_norm"], loss_row)
    gsum = _small_allreduce(pack)
    w_s = _pack_small(norm_mix, norm_ffn, hgrn_lb_logits, hgrn_out_norm, final_norm)
    m_s = _pack_small(m_norm_mix, m_norm_ffn, m_hgrn_lb_logits, m_hgrn_out_norm, m_final_norm)
    v_s = _pack_small(v_norm_mix, v_norm_ffn, v_hgrn_lb_logits, v_hgrn_out_norm, v_final_norm)
    lg_pack = jnp.pad(hgrn_lb_logits, ((0, 8 - hgrn_lb_logits.shape[0]), (0, 0)))
    sg, sd, sm, sv = _small_update(gsum, lg_pack, w_s, m_s, v_s)

    def unpack(p):
        return (p[0:2], p[2:4], p[4:7], p[7:8, :HEAD], p[8])

    def assemble(p, which):
        nmx, nff, lbl, onm, fnm = unpack(p)
        hin, hout, qkv, aout, fin, fdn = [updated[t][which] for t in
                                          ("hgrn_w_in", "hgrn_w_out", "attn_w_qkv", "attn_w_out", "ffn_w_in", "ffn_w_down")]
        return (nmx, nff, hin, lbl, onm, hout, qkv, aout, fin, fdn, fnm)

    total_loss = gsum[9, 0]
    return (total_loss, dx.reshape(1, S, D_MODEL), *assemble(sg, 0), *assemble(sd, 1), *assemble(sm, 2), *assemble(sv, 3))
```

```python
import functools

import jax
import jax.numpy as jnp
import numpy as np
from jax import lax
from jax.experimental import pallas as pl
from jax.experimental.pallas import tpu as pltpu

F32 = jnp.float32
BF16 = jnp.bfloat16
MESH = pl.DeviceIdType.MESH

D_MODEL = 1024
HEAD = 128
HGRN_HEADS = 8
HGRN_CHUNK = 64
HGRN_HEADS_PER_STEP = 2
ATTN_GROUPS = ((128, 1), (512, 4), (2048, 16))
ATTN_SPAN = 128
HEADS_PER_GROUP = 4
GROUP_W = HEADS_PER_GROUP * HEAD
D_FF = 2816
NORM_EPS = 1e-6
ROPE_THETA = 10000.0
NEG = -1e30

ADAM_LR, ADAM_B1, ADAM_B2, ADAM_EPS, ADAM_WD, ADAM_STEP = 0.001, 0.9, 0.999, 1e-08, 0.01, 10

N_CHIPS = 4
VMEM_LIMIT = 56 * 1024 * 1024
SMALL_ROWS = 16


def _params(sem=None):
    return pltpu.CompilerParams(dimension_semantics=sem, vmem_limit_bytes=VMEM_LIMIT)


def _row_tile(rows, cols, budget_bytes=3 * 512 * 1024):
    best = 8
    for t in range(8, rows + 1, 8):
        if rows % t == 0 and t * cols * 4 <= budget_bytes:
            best = t
    assert rows % best == 0
    return best


def _grid_corner(i, j):
    return jnp.logical_and(pl.program_id(0) == i, pl.program_id(1) == j)


def _sigmoid(v):
    return 0.5 * jnp.tanh(0.5 * v) + 0.5


def _dot(a, b):
    return jnp.dot(a, b, preferred_element_type=F32)


def _dot_nt(a, b):
    return lax.dot_general(a, b, (((1,), (1,)), ((), ())), preferred_element_type=F32)


def _dot_tn(a, b):
    return lax.dot_general(a, b, (((0,), (0,)), ((), ())), preferred_element_type=F32)


def _dot_exact(ones, b):
    ones = ones.astype(BF16)
    hi = b.astype(BF16)
    rest = b - hi.astype(F32)
    mid = rest.astype(BF16)
    low = (rest - mid.astype(F32)).astype(BF16)
    return _dot(ones, hi) + _dot(ones, mid) + _dot(ones, low)


def _rstd(v):
    return lax.rsqrt(jnp.mean(v * v, axis=-1, keepdims=True) + NORM_EPS)


def _norm_mm(h, gain, w3, name, out_dtype=F32, tm=2048, rider=None):
    S, K = h.shape
    J, _, n = w3.shape
    gi = S // tm

    def body(h_ref, g_ref, w_ref, y_ref, u_ref):
        @pl.when(pl.program_id(1) == 0)
        def _():
            v = h_ref[...]
            u_ref[...] = (v * _rstd(v) * g_ref[...]).astype(BF16)

        y_ref[...] = _dot(u_ref[...], w_ref[pl.program_id(1)]).astype(y_ref.dtype)

    r_ops, r_in, r_out, r_shape, r_scr = _rider_args(rider)
    res = pl.pallas_call(
        _ride(rider, body, 3, 2, functools.partial(_grid_corner, 0, 0), functools.partial(_grid_corner, gi - 1, J - 1)),
        name=name, grid=(gi, J),
        in_specs=[pl.BlockSpec((tm, K), lambda i, j: (i, 0)),
                  pl.BlockSpec((1, K), lambda i, j: (0, 0)),
                  pl.BlockSpec((J, K, n), lambda i, j: (0, 0, 0), pipeline_mode=pl.Buffered(1))] + r_in,
        out_specs=[pl.BlockSpec((tm, n), lambda i, j: (i, j)), pl.BlockSpec((tm, K), lambda i, j: (i, 0))] + r_out,
        out_shape=[jax.ShapeDtypeStruct((S, J * n), out_dtype), jax.ShapeDtypeStruct((S, K), BF16)] + r_shape,
        scratch_shapes=r_scr,
        compiler_params=_params(("arbitrary", "arbitrary")))(h, gain, w3, *r_ops)
    return res[0], res[1], res[2:]


def _mm_res(h, a, w2, name, tm=1024):
    S, N = h.shape
    K = a.shape[1]

    def body(h_ref, a_ref, w_ref, o_ref):
        o_ref[...] = h_ref[...] + _dot(a_ref[...], w_ref[...])

    return pl.pallas_call(
        body, name=name, grid=(S // tm,),
        in_specs=[pl.BlockSpec((tm, N), lambda i: (i, 0)),
                  pl.BlockSpec((tm, K), lambda i: (i, 0)),
                  pl.BlockSpec((K, N), lambda i: (0, 0))],
        out_specs=pl.BlockSpec((tm, N), lambda i: (i, 0)),
        out_shape=jax.ShapeDtypeStruct((S, N), F32),
        compiler_params=_params(("parallel",)))(h, a, w2)


def _swiglu(z_ref, F):
    g = z_ref[:, :F].astype(F32)
    return (g * _sigmoid(g) * z_ref[:, F:].astype(F32)).astype(BF16)


def _swiglu_mm_res(h, z, w2, name, tm=512):
    S, N = h.shape
    F = w2.shape[0]

    def body(h_ref, z_ref, w_ref, o_ref, a_ref):
        a = _swiglu(z_ref, F)
        a_ref[...] = a
        o_ref[...] = h_ref[...] + _dot(a, w_ref[...])

    return pl.pallas_call(
        body, name=name, grid=(S // tm,),
        in_specs=[pl.BlockSpec((tm, N), lambda i: (i, 0)),
                  pl.BlockSpec((tm, 2 * F), lambda i: (i, 0)),
                  pl.BlockSpec((F, N), lambda i: (0, 0), pipeline_mode=pl.Buffered(1))],
        out_specs=[pl.BlockSpec((tm, N), lambda i: (i, 0)), pl.BlockSpec((tm, F), lambda i: (i, 0))],
        out_shape=[jax.ShapeDtypeStruct((S, N), F32), jax.ShapeDtypeStruct((S, F), BF16)],
        compiler_params=_params(("parallel",)))(h, z, w2)


def _dy_specs(dy, J, n, tm):
    if dy.ndim == 3:
        return [pl.BlockSpec((None, tm, n), functools.partial(lambda i, j: (j, i, 0), j=j)) for j in range(J)]
    return [pl.BlockSpec((tm, n), functools.partial(lambda i, j: (i, j), j=j)) for j in range(J)]


def _acc_nt(dy_refs, w_ref):
    acc = None
    for j, r in enumerate(dy_refs):
        t = _dot_nt(r[...].astype(BF16), w_ref[j])
        acc = t if acc is None else acc + t
    return acc


def _mm_nt(dy, w3, name, out_dtype=F32, tm=1024):
    J, K, n = w3.shape
    S = dy.shape[-2]

    def body(*refs):
        dy_refs, w_ref, o_ref = refs[:J], refs[J], refs[J + 1]
        o_ref[...] = _acc_nt(dy_refs, w_ref).astype(o_ref.dtype)

    return pl.pallas_call(
        body, name=name, grid=(S // tm,),
        in_specs=_dy_specs(dy, J, n, tm) + [pl.BlockSpec((J, K, n), lambda i: (0, 0, 0))],
        out_specs=pl.BlockSpec((tm, K), lambda i: (i, 0)),
        out_shape=jax.ShapeDtypeStruct((S, K), out_dtype),
        compiler_params=_params(("parallel",)))(*([dy] * J), w3)


def _mm_nt_normbwd(dy, w3, h, gain, dh, name, tm=512, rider=None):
    J, K, n = w3.shape
    S = h.shape[0]
    steps = S // tm

    def body(*refs):
        dy_refs, w_ref, h_ref, g_ref, dh_ref, o_ref, dg_ref = refs[:J], *refs[J:]
        du = _acc_nt(dy_refs, w_ref)
        v = h_ref[...]
        r = _rstd(v)
        xh = v * r
        dyg = du * g_ref[...]
        o_ref[...] = dh_ref[...] + r * (dyg - xh * jnp.mean(dyg * xh, axis=-1, keepdims=True))

        @pl.when(pl.program_id(0) == 0)
        def _():
            dg_ref[...] = jnp.zeros_like(dg_ref)

        dg_ref[...] += jnp.sum(du * xh, axis=0, keepdims=True)

    row = pl.BlockSpec((tm, K), lambda i: (i, 0))
    vec = pl.BlockSpec((1, K), lambda i: (0, 0))
    r_ops, r_in, r_out, r_shape, r_scr = _rider_args(rider)
    res = pl.pallas_call(
        _ride(rider, body, J + 4, 2, lambda: pl.program_id(0) == 0, lambda: pl.program_id(0) == steps - 1),
        name=name, grid=(steps,),
        in_specs=_dy_specs(dy, J, n, tm) + [pl.BlockSpec((J, K, n), lambda i: (0, 0, 0)), row, vec, row] + r_in,
        out_specs=[row, vec] + r_out,
        out_shape=[jax.ShapeDtypeStruct((S, K), F32), jax.ShapeDtypeStruct((1, K), F32)] + r_shape,
        scratch_shapes=r_scr,
        compiler_params=_params(("arbitrary",)))(*([dy] * J), w3, h, gain, dh, *r_ops)
    return res[0], res[1], res[2:]


def _mm_nt_swiglu_bwd(dh, w2, z, name, tm=512, chunks=11):
    F, N = w2.shape
    S = dh.shape[0]
    fc = F // chunks
    assert fc * chunks == F and fc % HEAD == 0

    def body(dh_ref, w_ref, z_ref, o_ref):
        dhb = dh_ref[...].astype(BF16)
        da = [_dot_nt(dhb, w_ref[c * fc:(c + 1) * fc, :]) for c in range(chunks)]
        for c in range(chunks):
            g = z_ref[:, c * fc:(c + 1) * fc].astype(F32)
            u = z_ref[:, F + c * fc:F + (c + 1) * fc].astype(F32)
            sg = _sigmoid(g)
            o_ref[:, c * fc:(c + 1) * fc] = (da[c] * u * (sg * (1.0 + g * (1.0 - sg)))).astype(BF16)
            o_ref[:, F + c * fc:F + (c + 1) * fc] = (da[c] * (g * sg)).astype(BF16)

    return pl.pallas_call(
        body, name=name, grid=(S // tm,),
        in_specs=[pl.BlockSpec((tm, N), lambda i: (i, 0)),
                  pl.BlockSpec((F, N), lambda i: (0, 0), pipeline_mode=pl.Buffered(1)),
                  pl.BlockSpec((tm, 2 * F), lambda i: (i, 0))],
        out_specs=pl.BlockSpec((tm, 2 * F), lambda i: (i, 0)),
        out_shape=jax.ShapeDtypeStruct((S, 2 * F), BF16),
        compiler_params=_params(("parallel",)))(dh, w2, z)


def _mm_tn(x, dy, J, n, tn, name):
    tpn = n // tn
    ts = 2048 if x.shape[1] <= 1536 else 1024
    S, K = x.shape
    if dy.ndim == 3:
        dy_spec = pl.BlockSpec((None, ts, tn), lambda c, s: (c // tpn, s, c % tpn))
    else:
        dy_spec = pl.BlockSpec((ts, tn), lambda c, s: (s, c))

    def body(x_ref, dy_ref, o_ref):
        @pl.when(pl.program_id(1) == 0)
        def _():
            o_ref[...] = jnp.zeros_like(o_ref)

        o_ref[...] += _dot_tn(x_ref[...], dy_ref[...].astype(BF16))

    return pl.pallas_call(
        body, name=name, grid=(J * tpn, S // ts),
        in_specs=[pl.BlockSpec((ts, K), lambda c, s: (s, 0)), dy_spec],
        out_specs=pl.BlockSpec((None, K, tn), lambda c, s: (c // tpn, 0, c % tpn)),
        out_shape=jax.ShapeDtypeStruct((J, K, n), F32),
        compiler_params=_params(("parallel", "arbitrary")))(x, dy)


def _loss_head(h, gain, target, tm=1024):
    S, K = h.shape

    def body(h_ref, g_ref, t_ref, dh_ref, loss_ref, dg_ref):
        v = h_ref[...]
        r = _rstd(v)
        xh = v * r
        g = g_ref[...]
        dy = (xh * g - t_ref[...]) * (1.0 / K)
        dyg = dy * g
        dh_ref[...] = r * (dyg - xh * jnp.mean(dyg * xh, axis=-1, keepdims=True))

        @pl.when(pl.program_id(0) == 0)
        def _():
            loss_ref[...] = jnp.zeros_like(loss_ref)
            dg_ref[...] = jnp.zeros_like(dg_ref)

        part = jnp.sum(jnp.sum(dy * dy, axis=-1, keepdims=True), axis=0, keepdims=True) * (0.5 * K)
        lane = lax.broadcasted_iota(jnp.int32, loss_ref.shape, 1)
        loss_ref[...] += jnp.where(lane == 0, part, 0.0)
        dg_ref[...] += jnp.sum(dy * xh, axis=0, keepdims=True)

    row = pl.BlockSpec((tm, K), lambda i: (i, 0))
    vec = pl.BlockSpec((1, K), lambda i: (0, 0))
    return pl.pallas_call(
        body, name="loss_head", grid=(S // tm,),
        in_specs=[row, vec, row],
        out_specs=[row, pl.BlockSpec((1, HEAD), lambda i: (0, 0)), vec],
        out_shape=[jax.ShapeDtypeStruct((S, K), F32), jax.ShapeDtypeStruct((1, HEAD), F32),
                   jax.ShapeDtypeStruct((1, K), F32)],
        compiler_params=_params(("arbitrary",)))(h, gain, target)


def _lower_bound(lg_ref):
    l0, l1, l2 = lg_ref[0:1, :], lg_ref[1:2, :], lg_ref[2:3, :]
    mx = jnp.maximum(jnp.maximum(l0, l1), l2)
    e0, e1, e2 = jnp.exp(l0 - mx), jnp.exp(l1 - mx), jnp.exp(l2 - mx)
    return e0 / (e0 + e1 + e2)


def _chunks(v, ncb):
    C = HGRN_CHUNK
    return [v[c * C:(c + 1) * C] for c in range(ncb)]


def _rows(parts):
    return jnp.concatenate(parts, axis=0)


def _block_gates(qz, fz, lb, ncb):
    C = HGRN_CHUNK
    row = lax.broadcasted_iota(jnp.int32, (C, C), 0)
    col = lax.broadcasted_iota(jnp.int32, (C, C), 1)
    tri = (col <= row).astype(F32)
    first_half = lax.broadcasted_iota(jnp.int32, (C, HEAD), 0) < C // 2
    sig = _sigmoid(fz)
    fg = lb + (1.0 - lb) * sig
    key = 1.0 - fg
    lg = jnp.log(fg)
    lgs = _chunks(lg, ncb)
    b = _rows([_dot_exact(tri, v) for v in lgs])
    r_c = [jnp.sum(jnp.where(first_half, v, 0.0), axis=0, keepdims=True) for v in lgs]
    bl_c = [jnp.sum(v, axis=0, keepdims=True) for v in lgs]
    r = _rows([jnp.broadcast_to(v, (C, HEAD)) for v in r_c])
    e_br, e_rb = jnp.exp(b - r), jnp.exp(r - b)
    e_b = e_br * _rows([jnp.broadcast_to(jnp.exp(v), (C, HEAD)) for v in r_c])
    e_lb = e_rb * _rows([jnp.broadcast_to(jnp.exp(e - v), (C, HEAD)) for e, v in zip(bl_c, r_c)])
    sq = _sigmoid(qz)
    qy = qz * sq
    return sig, fg, key, (e_br, e_rb, e_b, e_lb), bl_c, sq, qy


def _hgrn_fwd(proj, logits, gain, tb=1024, rider=None):
    S = proj.shape[0]
    H, C = HGRN_HEADS, HGRN_CHUNK
    ncb = tb // C

    def one_head(q_ref, f_ref, i_ref, g_ref, lg_ref, gn_ref, o_ref, og_ref, st_ref, state):
        @pl.when(pl.program_id(1) == 0)
        def _():
            state[...] = jnp.zeros_like(state)

        lb = _lower_bound(lg_ref)
        causal = lax.broadcasted_iota(jnp.int32, (C, C), 1) <= lax.broadcasted_iota(jnp.int32, (C, C), 0)
        qz, fz, gz = q_ref[...], f_ref[...], g_ref[...]
        _, _, key, (e_br, e_rb, e_b, e_lb), bl_c, _, qy = _block_gates(qz, fz, lb, ncb)
        qs = _chunks((qy * e_br).astype(BF16), ncb)
        ks = _chunks((key * e_rb).astype(BF16), ncb)
        qb = _chunks((qy * e_b).astype(BF16), ncb)
        ke = _chunks((key * e_lb).astype(BF16), ncb)
        vb = _chunks(i_ref[...].astype(BF16), ncb)
        a = [jnp.where(causal, _dot_nt(qs[c], ks[c]), 0.0).astype(BF16) for c in range(ncb)]
        upd = [_dot_tn(vb[c], ke[c]) for c in range(ncb)]
        o_intra = [_dot(a[c], vb[c]) for c in range(ncb)]
        st = state[...]
        e_l = [jnp.exp(v) for v in bl_c]
        sts = []
        for c in range(ncb):
            sts.append(st)
            st = st * e_l[c] + upd[c]
        state[...] = st
        for c in range(ncb):
            st_ref[c] = sts[c]
        o = _rows([_dot_nt(qb[c], sts[c].astype(BF16)) + o_intra[c] for c in range(ncb)])
        o_ref[...] = o
        og_ref[...] = ((o * _rstd(o) * gn_ref[...]) * (gz * _sigmoid(gz))).astype(BF16)

    def body(q_ref, f_ref, i_ref, g_ref, lg_ref, gn_ref, o_ref, og_ref, st_ref, state):
        for hs in range(HP):
            cols = slice(hs * HEAD, (hs + 1) * HEAD)
            one_head(q_ref.at[:, cols], f_ref.at[:, cols], i_ref.at[:, cols], g_ref.at[:, cols], lg_ref.at[:, cols],
                     gn_ref, o_ref.at[:, cols], og_ref.at[:, cols], st_ref.at[hs], state.at[hs])

    HP, wide = HGRN_HEADS_PER_STEP, HGRN_HEADS_PER_STEP * HEAD
    hg = H // HP

    def part(p):
        return pl.BlockSpec((tb, wide), functools.partial(lambda h, i, p: (i, p * hg + h), p=p))

    nb = S // tb
    r_ops, r_in, r_out, r_shape, r_scr = _rider_args(rider)
    res = pl.pallas_call(
        _ride(rider, body, 6, 3, functools.partial(_grid_corner, 0, 0), functools.partial(_grid_corner, hg - 1, nb - 1)),
        name="hgrn_fwd", grid=(hg, nb),
        in_specs=[part(0), part(1), part(2), part(3),
                  pl.BlockSpec((3, wide), lambda h, i: (0, h)),
                  pl.BlockSpec((1, HEAD), lambda h, i: (0, 0))] + r_in,
        out_specs=[pl.BlockSpec((tb, wide), lambda h, i: (i, h)),
                   pl.BlockSpec((tb, wide), lambda h, i: (i, h)),
                   pl.BlockSpec((HP, ncb, HEAD, HEAD), lambda h, i: (h, i, 0, 0))] + r_out,
        out_shape=[jax.ShapeDtypeStruct((S, H * HEAD), F32),
                   jax.ShapeDtypeStruct((S, H * HEAD), BF16),
                   jax.ShapeDtypeStruct((H, S // C, HEAD, HEAD), F32)] + r_shape,
        scratch_shapes=[pltpu.VMEM((HP, HEAD, HEAD), F32)] + r_scr,
        compiler_params=_params(("arbitrary", "arbitrary")))(proj, proj, proj, proj, logits, gain, *r_ops)
    return res[:3], res[3:]


def _hgrn_bwd(proj, logits, gain, o, states, dog, tb=1024, rider=None):
    S = proj.shape[0]
    H, C = HGRN_HEADS, HGRN_CHUNK
    ncb = tb // C
    nb = S // tb

    def one_head(q_ref, f_ref, i_ref, g_ref, lg_ref, gn_ref, o_ref, st_ref, dog_ref,
                 dp_ref, dlb_ref, dgn_ref, dstate, dst_scr):
        @pl.when(pl.program_id(1) == 0)
        def _():
            dstate[...] = jnp.zeros_like(dstate)
            dlb_ref[...] = jnp.zeros_like(dlb_ref)
            dgn_ref[...] = jnp.zeros_like(dgn_ref)

        lb = _lower_bound(lg_ref)
        oml = 1.0 - lb
        gn = gn_ref[...]
        row = lax.broadcasted_iota(jnp.int32, (C, C), 0)
        col = lax.broadcasted_iota(jnp.int32, (C, C), 1)
        causal = col <= row
        tri_up = (col >= row).astype(F32)
        qz, fz, gz = q_ref[...], f_ref[...], g_ref[...]
        sig, fg, key, (e_br, e_rb, e_b, e_lb), bl_c, sq, qy = _block_gates(qz, fz, lb, ncb)
        qs_v, ks_v = (qy * e_br).astype(BF16), (key * e_rb).astype(BF16)
        qb_v, ke_v = (qy * e_b).astype(BF16), (key * e_lb).astype(BF16)
        qs, ks, qb, ke = _chunks(qs_v, ncb), _chunks(ks_v, ncb), _chunks(qb_v, ncb), _chunks(ke_v, ncb)
        vb = _chunks(i_ref[...].astype(BF16), ncb)
        ov = o_ref[...]
        rs = _rstd(ov)
        xh = ov * rs
        sg = _sigmoid(gz)
        dog_v = dog_ref[...]
        dgz = dog_v * (xh * gn) * (sg * (1.0 + gz * (1.0 - sg)))
        don = dog_v * (gz * sg)
        dgn_ref[...] += jnp.sum(don * xh, axis=0, keepdims=True)
        dyg = don * gn
        do = rs * (dyg - xh * jnp.mean(dyg * xh, axis=-1, keepdims=True))
        dob = _chunks(do.astype(BF16), ncb)
        CH = range(ncb)
        a = [jnp.where(causal, _dot_nt(qs[c], ks[c]), 0.0).astype(BF16) for c in CH]
        da = [jnp.where(causal, _dot_nt(dob[c], vb[c]), 0.0).astype(BF16) for c in CH]
        wst = [_dot_tn(dob[c], qb[c]) for c in CH]
        dv_in = [_dot_tn(a[c], dob[c]) for c in CH]
        dqs = [_dot(da[c], ks[c]) for c in CH]
        dks = [_dot_tn(da[c], qs[c]) for c in CH]
        e_l = [jnp.exp(v) for v in bl_c]
        dst = dstate[...]
        for c in reversed(range(ncb)):
            dst_scr[c] = dst
            dst = wst[c] + dst * e_l[c]
        dstate[...] = dst
        dst1b = [dst_scr[c].astype(BF16) for c in CH]
        dqb = [_dot(dob[c], st_ref[c].astype(BF16)) for c in CH]
        dke = [_dot(vb[c], dst1b[c]) for c in CH]
        dv = [dv_in[c] + _dot_nt(ke[c], dst1b[c]) for c in CH]
        dbl_st = [jnp.sum(dst_scr[c] * st_ref[c], axis=0, keepdims=True) * e_l[c] for c in CH]
        dqs, dks, dqb, dke, dv = _rows(dqs), _rows(dks), _rows(dqb), _rows(dke), _rows(dv)
        dke_ke = dke * ke_v.astype(F32)
        db = dqs * qs_v.astype(F32) - dks * ks_v.astype(F32) + dqb * qb_v.astype(F32) - dke_ke
        dlg = []
        for c, (db_c, kk_c) in enumerate(zip(_chunks(db, ncb), _chunks(dke_ke, ncb))):
            dbl = jnp.sum(kk_c, axis=0, keepdims=True) + dbl_st[c]
            dlg.append(_dot_exact(tri_up, db_c) + dbl)
        dlg = _rows(dlg)
        dkey = dks * e_rb + dke * e_lb
        dqy = dqs * e_br + dqb * e_b
        dfg = dlg / fg - dkey
        dlb_ref[...] += jnp.sum(dfg * (1.0 - sig), axis=0, keepdims=True)
        dp_ref[0] = (dqy * (sq * (1.0 + qz * (1.0 - sq)))).astype(BF16)
        dp_ref[1] = (dfg * oml * sig * (1.0 - sig)).astype(BF16)
        dp_ref[2] = dv.astype(BF16)
        dp_ref[3] = dgz.astype(BF16)

    def body(q_ref, f_ref, i_ref, g_ref, lg_ref, gn_ref, o_ref, st_ref, dog_ref,
             dp_ref, dlb_ref, dgn_ref, dstate, dst_scr):
        for hs in range(HP):
            cols = slice(hs * HEAD, (hs + 1) * HEAD)
            one_head(q_ref.at[:, cols], f_ref.at[:, cols], i_ref.at[:, cols], g_ref.at[:, cols], lg_ref.at[:, cols],
                     gn_ref, o_ref.at[:, cols], st_ref.at[hs], dog_ref.at[:, cols],
                     dp_ref.at[:, :, cols], dlb_ref.at[hs], dgn_ref.at[hs], dstate.at[hs], dst_scr)

    HP, wide = HGRN_HEADS_PER_STEP, HGRN_HEADS_PER_STEP * HEAD
    hg = H // HP

    def part(p):
        return pl.BlockSpec((tb, wide), functools.partial(lambda h, i, p: (nb - 1 - i, p * hg + h), p=p))

    blk = pl.BlockSpec((tb, wide), lambda h, i: (nb - 1 - i, h))
    acc = pl.BlockSpec((HP, 1, HEAD), lambda h, i: (h, 0, 0))
    r_ops, r_in, r_out, r_shape, r_scr = _rider_args(rider)
    res = pl.pallas_call(
        _ride(rider, body, 9, 3, functools.partial(_grid_corner, 0, 0), functools.partial(_grid_corner, hg - 1, nb - 1)),
        name="hgrn_bwd", grid=(hg, nb),
        in_specs=[part(0), part(1), part(2), part(3),
                  pl.BlockSpec((3, wide), lambda h, i: (0, h)),
                  pl.BlockSpec((1, HEAD), lambda h, i: (0, 0)),
                  blk,
                  pl.BlockSpec((HP, ncb, HEAD, HEAD), lambda h, i: (h, nb - 1 - i, 0, 0)),
                  blk] + r_in,
        out_specs=[pl.BlockSpec((4, tb, wide), lambda h, i: (0, nb - 1 - i, h)), acc, acc] + r_out,
        out_shape=[jax.ShapeDtypeStruct((4, S, H * HEAD), BF16),
                   jax.ShapeDtypeStruct((H, 1, HEAD), F32),
                   jax.ShapeDtypeStruct((H, 1, HEAD), F32)] + r_shape,
        scratch_shapes=[pltpu.VMEM((HP, HEAD, HEAD), F32), pltpu.VMEM((ncb, HEAD, HEAD), F32)] + r_scr,
        compiler_params=_params(("arbitrary", "arbitrary")))(
            proj, proj, proj, proj, logits, gain, o, states, dog, *r_ops)
    return res[:3], res[3:]


def _rope(v, cos, sin):
    return v * cos + pltpu.roll(v, HEAD // 2, 1) * sin


def _lane_pick(tile, hh):
    lane = lax.broadcasted_iota(jnp.int32, tile.shape, 1)
    return jnp.sum(jnp.where(lane == hh, tile, 0.0), axis=-1, keepdims=True)


def _lane_place(cols):
    rows = cols[0].shape[0]
    lane = lax.broadcasted_iota(jnp.int32, (rows, HEAD), 1)
    tile = jnp.zeros((rows, HEAD), F32)
    for hh, v in enumerate(cols):
        tile = jnp.where(lane == hh, v, tile)
    return tile


def _band_masks():
    qi = lax.broadcasted_iota(jnp.int32, (ATTN_SPAN, ATTN_SPAN), 0)
    kj = lax.broadcasted_iota(jnp.int32, (ATTN_SPAN, ATTN_SPAN), 1)
    return kj <= qi, kj >= qi


ATTN_TILE_BLOCKS = 8


def _attn_fwd(a):
    d, L, _ = a.shape
    B, W = min(ATTN_TILE_BLOCKS, a.shape[1] // ATTN_SPAN), ATTN_SPAN
    T = B * W
    assert L % T == 0
    steps = L // T
    scale = HEAD ** -0.5

    def body(q_ref, kc_ref, kp_ref, vc_ref, vp_ref, o_ref, lse_ref):
        n = pl.program_id(1)
        mask_c, mask_p0 = _band_masks()
        first = jnp.logical_and(mask_p0, n > 0)
        units = [(b, hh) for b in range(B) for hh in range(HEADS_PER_GROUP)]
        rows = [slice(b * W, (b + 1) * W) for b in range(B)]
        cols = [slice(hh * HEAD, (hh + 1) * HEAD) for hh in range(HEADS_PER_GROUP)]

        def prev_keys(ref, tile, b, hh):
            return ref[:, cols[hh]] if b == 0 else tile[rows[b - 1], cols[hh]]

        s_c = [jnp.where(mask_c, _dot_nt(q_ref[rows[b], cols[hh]], kc_ref[rows[b], cols[hh]]) * scale, NEG) for b, hh in units]
        s_p = [jnp.where(first if b == 0 else mask_p0,
                         _dot_nt(q_ref[rows[b], cols[hh]], prev_keys(kp_ref, kc_ref, b, hh)) * scale, NEG) for b, hh in units]
        m = [jnp.maximum(jnp.max(x, axis=-1, keepdims=True), jnp.max(y, axis=-1, keepdims=True)) for x, y in zip(s_c, s_p)]
        p_c = [jnp.exp(x - mm) for x, mm in zip(s_c, m)]
        p_p = [jnp.exp(y - mm) for y, mm in zip(s_p, m)]
        l = [jnp.sum(x, axis=-1, keepdims=True) + jnp.sum(y, axis=-1, keepdims=True) for x, y in zip(p_c, p_p)]
        acc = [_dot(p_c[i].astype(BF16), vc_ref[rows[b], cols[hh]]) + _dot(p_p[i].astype(BF16), prev_keys(vp_ref, vc_ref, b, hh))
               for i, (b, hh) in enumerate(units)]
        for i, (b, hh) in enumerate(units):
            o_ref[rows[b], cols[hh]] = (acc[i] / l[i]).astype(BF16)
        for b in range(B):
            lse_ref[rows[b], :] = _lane_place([m[i] + jnp.log(l[i]) for i, (bb, _) in enumerate(units) if bb == b])

    def cur(part):
        return pl.BlockSpec((None, T, GROUP_W), functools.partial(lambda r, n, p: (r, n, p), p=part))

    def prev(part):
        return pl.BlockSpec((None, W, GROUP_W), functools.partial(lambda r, n, p: (r, jnp.maximum(n * B - 1, 0), p), p=part))

    return pl.pallas_call(
        body, name=f"attn_fwd_d{d}", grid=(d, steps),
        in_specs=[cur(0), cur(1), prev(1), cur(2), prev(2)],
        out_specs=[pl.BlockSpec((None, T, GROUP_W), lambda r, n: (r, n, 0)), pl.BlockSpec((None, T, HEAD), lambda r, n: (r, n, 0))],
        out_shape=[jax.ShapeDtypeStruct((d, L, GROUP_W), BF16), jax.ShapeDtypeStruct((d, L, HEAD), F32)],
        compiler_params=_params(("parallel", "arbitrary")))(a, a, a, a, a)


def _attn_bwd(a, do, lse, dd):
    d, L, _ = a.shape
    B, W = min(ATTN_TILE_BLOCKS, a.shape[1] // ATTN_SPAN), ATTN_SPAN
    T = B * W
    assert L % T == 0
    steps = L // T
    scale = HEAD ** -0.5

    def body(qc_ref, qn_ref, kp_ref, kc_ref, vp_ref, vc_ref, doc_ref, don_ref, lc_ref, ln_ref, ddc_ref, ddn_ref, da_ref):
        n = pl.program_id(1)
        mask_c, mask_p0 = _band_masks()
        first = jnp.logical_and(mask_p0, n > 0)
        last = jnp.logical_and(mask_p0, n < steps - 1)
        H4 = range(HEADS_PER_GROUP)
        units = [(b, hh) for b in range(B) for hh in H4]
        rows = [slice(b * W, (b + 1) * W) for b in range(B)]
        cols = [slice(hh * HEAD, (hh + 1) * HEAD) for hh in H4]
        q = {u: qc_ref[rows[u[0]], cols[u[1]]] for u in units}
        k = {u: kc_ref[rows[u[0]], cols[u[1]]] for u in units}
        v = {u: vc_ref[rows[u[0]], cols[u[1]]] for u in units}
        g_o = {u: doc_ref[rows[u[0]], cols[u[1]]] for u in units}
        kb = {(b, hh): kp_ref[:, cols[hh]] if b == 0 else k[(b - 1, hh)] for b, hh in units}
        vb = {(b, hh): vp_ref[:, cols[hh]] if b == 0 else v[(b - 1, hh)] for b, hh in units}
        lse_t = {(b, hh): _lane_pick(lc_ref[rows[b], :], hh) for b, hh in units}
        dd_t = {(b, hh): _lane_pick(ddc_ref[rows[b], :], hh) for b, hh in units}
        p_c = {u: jnp.where(mask_c, jnp.exp(_dot_nt(q[u], k[u]) * scale - lse_t[u]), 0.0) for u in units}
        p_p = {u: jnp.where(first if u[0] == 0 else mask_p0, jnp.exp(_dot_nt(q[u], kb[u]) * scale - lse_t[u]), 0.0) for u in units}
        ds_c = {u: (p_c[u] * (_dot_nt(g_o[u], v[u]) + dd_t[u])).astype(BF16) for u in units}
        ds_p = {u: (p_p[u] * (_dot_nt(g_o[u], vb[u]) + dd_t[u])).astype(BF16) for u in units}
        qn = [qn_ref[:, c] for c in cols]
        g_n = [don_ref[:, c] for c in cols]
        p_n = [jnp.where(last, jnp.exp(_dot_nt(qn[hh], k[(B - 1, hh)]) * scale - _lane_pick(ln_ref[...], hh)), 0.0) for hh in H4]
        ds_n = [(p_n[hh] * (_dot_nt(g_n[hh], v[(B - 1, hh)]) + _lane_pick(ddn_ref[...], hh))).astype(BF16) for hh in H4]
        dq = {u: (_dot(ds_c[u], k[u]) + _dot(ds_p[u], kb[u])) * scale for u in units}
        dk, dv = {}, {}
        for b, hh in units:
            if b < B - 1:
                nxt = (b + 1, hh)
                dk[(b, hh)] = (_dot_tn(ds_c[(b, hh)], q[(b, hh)]) + _dot_tn(ds_p[nxt], q[nxt])) * scale
                dv[(b, hh)] = _dot_tn(p_c[(b, hh)].astype(BF16), g_o[(b, hh)]) + _dot_tn(p_p[nxt].astype(BF16), g_o[nxt])
            else:
                dk[(b, hh)] = (_dot_tn(ds_c[(b, hh)], q[(b, hh)]) + _dot_tn(ds_n[hh], qn[hh])) * scale
                dv[(b, hh)] = _dot_tn(p_c[(b, hh)].astype(BF16), g_o[(b, hh)]) + _dot_tn(p_n[hh].astype(BF16), g_n[hh])
        for b, hh in units:
            da_ref[rows[b], cols[hh]] = dq[(b, hh)].astype(BF16)
            da_ref[rows[b], GROUP_W + hh * HEAD:GROUP_W + (hh + 1) * HEAD] = dk[(b, hh)].astype(BF16)
            da_ref[rows[b], 2 * GROUP_W + hh * HEAD:2 * GROUP_W + (hh + 1) * HEAD] = dv[(b, hh)].astype(BF16)

    nb = L // W

    def cur(width, part):
        return pl.BlockSpec((None, T, width), functools.partial(lambda r, n, p: (r, n, p), p=part))

    def prev(width, part):
        return pl.BlockSpec((None, W, width), functools.partial(lambda r, n, p: (r, jnp.maximum(n * B - 1, 0), p), p=part))

    def nxt(width, part):
        return pl.BlockSpec((None, W, width), functools.partial(lambda r, n, p: (r, jnp.minimum(n * B + B, nb - 1), p), p=part))

    g = GROUP_W
    return pl.pallas_call(
        body, name=f"attn_bwd_d{d}", grid=(d, steps),
        in_specs=[cur(g, 0), nxt(g, 0), prev(g, 1), cur(g, 1), prev(g, 2), cur(g, 2),
                  cur(g, 0), nxt(g, 0), cur(HEAD, 0), nxt(HEAD, 0), cur(HEAD, 0), nxt(HEAD, 0)],
        out_specs=pl.BlockSpec((None, T, 3 * g), lambda r, n: (r, n, 0)),
        out_shape=jax.ShapeDtypeStruct((d, L, 3 * g), BF16),
        compiler_params=_params(("parallel", "arbitrary")))(
            a, a, a, a, a, a, do, do, lse, lse, dd, dd)


def _softmax3(ls):
    mx = jnp.maximum(jnp.maximum(ls[0], ls[1]), ls[2])
    es = [jnp.exp(v - mx) for v in ls]
    tot = es[0] + es[1] + es[2]
    return [e / tot for e in es]


HEAD_COLS = [slice(hh * HEAD, (hh + 1) * HEAD) for hh in range(HEADS_PER_GROUP)]


def _group_spec(d, tm):
    return pl.BlockSpec((d, tm // d, GROUP_W), lambda i: (0, i, 0))


def _gather_heads(ref, scr, d, tm):
    if d == 1:
        return [ref[0, :, cols].astype(F32) for cols in HEAD_COLS]
    for hh, cols in enumerate(HEAD_COLS):
        for r in range(d):
            scr.at[hh][pl.ds(r, tm // d, stride=d), :] = ref[r, :, cols].astype(F32)
    return [scr[hh] for hh in range(HEADS_PER_GROUP)]


def _tile_spec(d, tm):
    return pl.BlockSpec((d, tm // d, HEAD), lambda i: (0, i, 0))


def _gather_tile(ref, scr, d, tm):
    if d == 1:
        return ref[0]
    for r in range(d):
        scr[pl.ds(r, tm // d, stride=d), :] = ref[r]
    return scr[...]


def _scatter_tile(val, scr, ref, d, tm):
    if d == 1:
        ref[0] = val
        return
    scr[...] = val
    for r in range(d):
        ref[r] = scr[pl.ds(r, tm // d, stride=d), :]


def _scatter_heads(vals, scr, ref, d, tm):
    if d == 1:
        for cols, v in zip(HEAD_COLS, vals):
            ref[0, :, cols] = v.astype(ref.dtype)
        return
    for hh, v in enumerate(vals):
        scr[hh] = v
    for hh, cols in enumerate(HEAD_COLS):
        for r in range(d):
            ref[r, :, cols] = scr.at[hh][pl.ds(r, tm // d, stride=d), :].astype(ref.dtype)


def _qkv_dilated(h, gain, w4, gi, cos, sin, d, tm=2048):
    S, K = h.shape
    n_shard = w4.shape[2]
    assert n_shard % HEAD == 0

    def head_cols(hh):
        def index(i, p):
            c = p * (len(ATTN_GROUPS) * GROUP_W) + gi * GROUP_W + hh * HEAD
            return c // n_shard, 0, (c % n_shard) // HEAD
        return pl.BlockSpec((None, K, HEAD), index)

    def body(h_ref, g_ref, *refs):
        w_refs, (cos_ref, sin_ref, out_ref, u_ref, y_scr) = refs[:HEADS_PER_GROUP], refs[HEADS_PER_GROUP:]
        p = pl.program_id(1)

        @pl.when(p == 0)
        def _():
            v = h_ref[...]
            u_ref[...] = (v * _rstd(v) * g_ref[...]).astype(BF16)

        y = _dot(u_ref[...], jnp.concatenate([r[...] for r in w_refs], axis=1))
        heads = [slice(hh * HEAD, (hh + 1) * HEAD) for hh in range(HEADS_PER_GROUP)]
        if d > 1:
            for hh, cols in enumerate(heads):
                y_scr[hh] = y[:, cols]

        def rows_of(hh, r):
            return y[:, heads[hh]] if d == 1 else y_scr.at[hh][pl.ds(r, tm // d, stride=d), :]

        @pl.when(p < 2)
        def _():
            for r in range(d):
                rows = slice(None) if d == 1 else pl.ds(r, tm // d, stride=d)
                cr, sr = cos_ref[rows, :], sin_ref[rows, :]
                for hh, cols in enumerate(heads):
                    out_ref[r, :, cols] = _rope(rows_of(hh, r), cr, sr).astype(BF16)

        @pl.when(p == 2)
        def _():
            for r in range(d):
                for hh, cols in enumerate(heads):
                    out_ref[r, :, cols] = rows_of(hh, r).astype(BF16)

    tab = pl.BlockSpec((tm, HEAD), lambda i, p: (i, 0))
    return pl.pallas_call(
        body, name=f"attn_qkv_d{d}", grid=(S // tm, 3),
        in_specs=[pl.BlockSpec((tm, K), lambda i, p: (i, 0)),
                  pl.BlockSpec((1, K), lambda i, p: (0, 0)),
                  *[head_cols(hh) for hh in range(HEADS_PER_GROUP)], tab, tab],
        out_specs=[pl.BlockSpec((d, tm // d, GROUP_W), lambda i, p: (0, i, p)), pl.BlockSpec((tm, K), lambda i, p: (i, 0))],
        out_shape=[jax.ShapeDtypeStruct((d, S // d, 3 * GROUP_W), BF16), jax.ShapeDtypeStruct((S, K), BF16)],
        scratch_shapes=[pltpu.VMEM((HEADS_PER_GROUP, tm, HEAD), F32)],
        compiler_params=_params(("parallel", "arbitrary")))(h, gain, *[w4] * HEADS_PER_GROUP, cos, sin)


def _undilate_group(da, dqkv, cos, sin, g, tm=2048):
    d, L, _ = da.shape
    S = d * L
    G = len(ATTN_GROUPS)

    def body(*refs):
        da_ref, cos_ref, sin_ref, out_ref, scr = refs[0], refs[1], refs[2], refs[-2], refs[-1]
        p = pl.program_id(1)
        heads = [slice(hh * HEAD, (hh + 1) * HEAD) for hh in range(HEADS_PER_GROUP)]
        if d > 1:
            for hh, cols in enumerate(heads):
                for r in range(d):
                    scr.at[hh][pl.ds(r, tm // d, stride=d), :] = da_ref[r, :, cols].astype(F32)

        def tokens(hh):
            return da_ref[0, :, heads[hh]].astype(F32) if d == 1 else scr[hh]

        @pl.when(p < 2)
        def _():
            cr, sr = cos_ref[...], -sin_ref[...]
            for hh, cols in enumerate(heads):
                out_ref[:, cols] = _rope(tokens(hh), cr, sr).astype(BF16)

        @pl.when(p == 2)
        def _():
            for hh, cols in enumerate(heads):
                out_ref[:, cols] = tokens(hh).astype(BF16)

    tab = pl.BlockSpec((tm, HEAD), lambda i, p: (i, 0))
    operands = (da, cos, sin) if dqkv is None else (da, cos, sin, dqkv)
    return pl.pallas_call(
        body, name=f"attn_undilate_d{d}", grid=(S // tm, 3),
        in_specs=[pl.BlockSpec((d, tm // d, GROUP_W), lambda i, p: (0, i, p)), tab, tab] + ([] if dqkv is None else [ANY]),
        out_specs=pl.BlockSpec((tm, GROUP_W), lambda i, p: (i, p * G + g)),
        out_shape=jax.ShapeDtypeStruct((S, 3 * G * GROUP_W), BF16),
        input_output_aliases={} if dqkv is None else {3: 0},
        scratch_shapes=[pltpu.VMEM((HEADS_PER_GROUP, tm, HEAD), F32)],
        compiler_params=_params(("parallel", "arbitrary")))(*operands)


def _attn_merge(os_, lses, h, w2, tm=1024):
    G = len(os_)
    S, N = h.shape

    def body(*refs):
        o_refs, l_refs, h_ref, w_ref, res_ref, out_ref = refs[:G], refs[G:2 * G], *refs[2 * G:2 * G + 4]
        scr = refs[2 * G + 4:]
        o = [_gather_heads(o_refs[g], scr[g], d, tm) for g, (_, d) in enumerate(ATTN_GROUPS)]
        l = [_gather_tile(l_refs[g], scr[G + g].at[0], d, tm) for g, (_, d) in enumerate(ATTN_GROUPS)]
        for hh in range(HEADS_PER_GROUP):
            al = _softmax3([_lane_pick(l[g], hh) for g in range(G)])
            for g in range(G):
                out_ref[:, g * GROUP_W + hh * HEAD:g * GROUP_W + (hh + 1) * HEAD] = (o[g][hh] * al[g]).astype(BF16)
        res_ref[...] = h_ref[...] + _dot(out_ref[...], w_ref[...])

    specs = [_group_spec(d, tm) for _, d in ATTN_GROUPS]
    row = pl.BlockSpec((tm, N), lambda i: (i, 0))
    return pl.pallas_call(
        body, name="attn_merge_out", grid=(S // tm,),
        in_specs=specs + [_tile_spec(d, tm) for _, d in ATTN_GROUPS] + [
            row, pl.BlockSpec((G * GROUP_W, N), lambda i: (0, 0), pipeline_mode=pl.Buffered(1))],
        out_specs=[row, pl.BlockSpec((tm, G * GROUP_W), lambda i: (i, 0))],
        out_shape=[jax.ShapeDtypeStruct((S, N), F32), jax.ShapeDtypeStruct((S, G * GROUP_W), BF16)],
        scratch_shapes=[pltpu.VMEM((HEADS_PER_GROUP, tm, HEAD), F32)] * (2 * G),
        compiler_params=_params(("parallel",)))(*os_, *lses, h, w2)


def _attn_merge_bwd(os_, lses, dh, w2, tm=512):
    G = len(os_)
    S, N = dh.shape

    def body(*refs):
        o_refs, l_refs, dh_ref, w_ref = refs[:G], refs[G:2 * G], refs[2 * G], refs[2 * G + 1]
        do_refs, dd_refs = refs[2 * G + 2:3 * G + 2], refs[3 * G + 2:4 * G + 2]
        scr = refs[4 * G + 2:]
        doa = _dot_nt(dh_ref[...].astype(BF16), w_ref[...])
        o = [_gather_heads(o_refs[g], scr[g], d, tm) for g, (_, d) in enumerate(ATTN_GROUPS)]
        l = [_gather_tile(l_refs[g], scr[G + g].at[0], d, tm) for g, (_, d) in enumerate(ATTN_GROUPS)]
        do = [[None] * HEADS_PER_GROUP for _ in range(G)]
        dd = [[None] * HEADS_PER_GROUP for _ in range(G)]
        for hh in range(HEADS_PER_GROUP):
            al = _softmax3([_lane_pick(l[g], hh) for g in range(G)])
            mix = None
            for g in range(G):
                dg = doa[:, g * GROUP_W + hh * HEAD:g * GROUP_W + (hh + 1) * HEAD]
                do[g][hh] = dg * al[g]
                t = al[g] * jnp.sum(dg * o[g][hh], axis=-1, keepdims=True)
                mix = t if mix is None else mix + t
            for g in range(G):
                dd[g][hh] = -al[g] * mix
        for g, (_, d) in enumerate(ATTN_GROUPS):
            _scatter_heads(do[g], scr[2 * G + g], do_refs[g], d, tm)
            _scatter_tile(_lane_place(dd[g]), scr[3 * G + g].at[0], dd_refs[g], d, tm)

    specs = [_group_spec(d, tm) for _, d in ATTN_GROUPS]
    tiles = [_tile_spec(d, tm) for _, d in ATTN_GROUPS]
    do_shapes = [jax.ShapeDtypeStruct((d, S // d, GROUP_W), BF16) for _, d in ATTN_GROUPS]
    dd_shapes = [jax.ShapeDtypeStruct((d, S // d, HEAD), F32) for _, d in ATTN_GROUPS]
    return pl.pallas_call(
        body, name="attn_merge_bwd", grid=(S // tm,),
        in_specs=specs + tiles + [pl.BlockSpec((tm, N), lambda i: (i, 0)),
                                  pl.BlockSpec((G * GROUP_W, N), lambda i: (0, 0), pipeline_mode=pl.Buffered(1))],
        out_specs=specs + tiles,
        out_shape=do_shapes + dd_shapes,
        scratch_shapes=[pltpu.VMEM((HEADS_PER_GROUP, tm, HEAD), F32)] * (4 * G),
        compiler_params=_params(("parallel",)))(*os_, *lses, dh, w2)


def _rope_tables(S):
    inv_freq = (1.0 / (np.float32(ROPE_THETA) ** (np.arange(0, HEAD, 2, dtype=np.float32) / np.float32(HEAD))))
    ang = (np.arange(S, dtype=np.float32)[:, None] * inv_freq.astype(np.float32)[None, :]).astype(np.float64)
    cos, sin = np.cos(ang).astype(np.float32), np.sin(ang).astype(np.float32)
    return jnp.asarray(np.concatenate([cos, cos], axis=-1)), jnp.asarray(np.concatenate([-sin, sin], axis=-1))


def _local_step(x, target, norm_mix, norm_ffn, lb_logits, out_gain, final_norm, comm):
    S = x.shape[0]
    nm0, nm1 = norm_mix[0:1], norm_mix[1:2]
    nf0, nf1 = norm_ffn[0:1], norm_ffn[1:2]
    w = comm.first_weights()

    proj, u0, got = _norm_mm(x, nm0, w["hin"], "hgrn_in", rider=comm.gather_rider(LATE_WEIGHTS_A))
    w.update(comm.gathered(LATE_WEIGHTS_A, got))
    (o, og, states), got = _hgrn_fwd(proj, lb_logits, out_gain, rider=comm.gather_rider(LATE_WEIGHTS_B))
    w.update(comm.gathered(LATE_WEIGHTS_B, got))
    fin_tn = w["fin0"].shape[2]
    h1 = _mm_res(x, og, w["hout"], "hgrn_out")
    z0, u1, _ = _norm_mm(h1, nf0, w["fin0"], "ffn0_in", out_dtype=BF16)
    h2, act0 = _swiglu_mm_res(h1, z0, w["fdn0"], "ffn0_down")
    cos, sin = _rope_tables(S)
    G = len(ATTN_GROUPS)
    a_g, u2 = zip(*[_qkv_dilated(h2, nm1, w["qkv"], gi, cos, sin, d) for gi, (_, d) in enumerate(ATTN_GROUPS)])
    o_g, lse_g = zip(*[_attn_fwd(a) for a in a_g])
    h3, oa = _attn_merge(o_g, lse_g, h2, w["aout"])
    z1, u3, _ = _norm_mm(h3, nf1, w["fin1"], "ffn1_in", out_dtype=BF16)
    h4, act1 = _swiglu_mm_res(h3, z1, w["fdn1"], "ffn1_down")
    dh4, loss, d_final = _loss_head(h4, final_norm, target)

    grads, small = {}, {"final_norm": d_final}

    def ffn_bwd(dh, h_in, u_in, z, act, gain, w_in, w_dn, tag, ride=None):
        dz = _mm_nt_swiglu_bwd(dh, w_dn, z, tag + "_down_dx")
        g_dn = _mm_tn(act, dh, 1, D_MODEL, D_MODEL, tag + "_down_dw")[0]
        g_in = _mm_tn(u_in, dz, N_CHIPS, fin_tn, fin_tn, tag + "_in_dw")
        rider = None if ride is None else ride(g_in, g_dn)
        dh_in, dgain, got = _mm_nt_normbwd(dz, w_in, h_in, gain, dh, tag + "_in_dx", rider=rider)
        return dh_in, dgain, g_in, g_dn, got

    dh3, d_nf1, grads["fin1"], grads["fdn1"], _ = ffn_bwd(dh4, h3, u3, z1, act1, nf1, w["fin1"], w["fdn1"], "ffn1")
    grads["aout"] = _mm_tn(oa, dh3, 1, D_MODEL, D_MODEL, "attn_out_dw")[0]
    merged = _attn_merge_bwd(o_g, lse_g, dh3, w["aout"])
    G = len(ATTN_GROUPS)
    das = [_attn_bwd(a_g[gi], merged[gi], lse_g[gi], merged[G + gi]) for gi in range(G)]
    dqkv = None
    for gi in range(G):
        dqkv = _undilate_group(das[gi], dqkv, cos, sin, gi)
    n_qkv = w["qkv"].shape[2]
    grads["qkv"] = _mm_tn(u2[0], dqkv, N_CHIPS, n_qkv, n_qkv, "attn_qkv_dw")
    dh2, d_nm1, _ = _mm_nt_normbwd(dqkv, w["qkv"], h2, nm1, dh3, "attn_qkv_dx")

    def ride_early(g_in, g_dn):
        return comm.pair_rider({**grads, "fin0": g_in, "fdn0": g_dn}, "early")

    dh1, d_nf0, _, _, got = ffn_bwd(dh2, h1, u1, z0, act0, nf0, w["fin0"], w["fdn0"], "ffn0", ride=ride_early)
    comm.paired("early", got)
    dog = _mm_nt(dh1, w["hout"][None], "hgrn_out_dx")
    (dproj, dlb, dgn), got = _hgrn_bwd(proj, lb_logits, out_gain, o, states, dog, rider=comm.exchange_rider("early"))
    comm.exchanged("early", got)
    late = {"hout": _mm_tn(og, dh1, 1, D_MODEL, D_MODEL, "hgrn_out_dw")[0],
            "hin": _mm_tn(u0, dproj, N_CHIPS, D_MODEL, D_MODEL, "hgrn_in_dw")}
    comm.pair_now(late, "late")
    dx, d_nm0, got = _mm_nt_normbwd(dproj, w["hin"], x, nm0, dh1, "hgrn_in_dx", rider=comm.exchange_rider("late"))
    comm.exchanged("late", got)

    small["norm_mix"] = jnp.concatenate([d_nm0, d_nm1], axis=0)
    small["norm_ffn"] = jnp.concatenate([d_nf0, d_nf1], axis=0)
    small["lb"] = dlb.reshape(1, HGRN_HEADS * HEAD)
    small["out_norm"] = dgn.reshape(HGRN_HEADS, HEAD)
    return loss, dx, small


def _place():
    x, y, c = lax.axis_index("x"), lax.axis_index("y"), lax.axis_index("c")
    others = [(1 - x, y), (x, 1 - y), (1 - x, 1 - y)]
    return x, y, c, others


ANY = pl.BlockSpec(memory_space=pl.ANY)


class _GatherRider:
    def __init__(self, shards):
        self.operands = list(shards)
        n = self.n = len(shards)
        self.out_shape = [jax.ShapeDtypeStruct((N_CHIPS,) + s.shape, s.dtype) for s in shards]
        self.scratch = [pltpu.SemaphoreType.DMA((3 * n,)), pltpu.SemaphoreType.DMA((3 * n,)),
                        pltpu.SemaphoreType.DMA((3 * n,)), pltpu.SemaphoreType.DMA((3 * n,)),
                        pltpu.SemaphoreType.DMA((n,)), pltpu.SemaphoreType.DMA((n,))]

    def _copies(self, ins, outs, sems):
        ici_send, ici_recv, _, _, own_send, own_recv = sems
        x, y, c, others = _place()
        me = 2 * x + y
        own = [pltpu.make_async_remote_copy(
            src_ref=ins[a], dst_ref=outs[a].at[me], send_sem=own_send.at[a], recv_sem=own_recv.at[a],
            device_id=(x, y, 1 - c), device_id_type=MESH) for a in range(self.n)]
        sends = [pltpu.make_async_remote_copy(
            src_ref=ins[a].at[c], dst_ref=outs[a].at[me, c], send_sem=ici_send.at[a * 3 + k], recv_sem=ici_recv.at[a * 3 + k],
            device_id=(ox, oy, c), device_id_type=MESH) for a in range(self.n) for k, (ox, oy) in enumerate(others)]
        return own, sends

    def start(self, ins, outs, sems):
        own, sends = self._copies(ins, outs, sems)
        for cp in own + sends:
            cp.start()

    def finish(self, ins, outs, sems):
        ici_send, ici_recv, d2d_send, d2d_recv, _, _ = sems
        x, y, c, others = _place()
        sibling = (x, y, 1 - c)
        own, sends = self._copies(ins, outs, sems)
        passes = []
        for a in range(self.n):
            for k, (ox, oy) in enumerate(others):
                s = a * 3 + k
                got = outs[a].at[2 * ox + oy, c]
                pltpu.make_async_remote_copy(
                    src_ref=got, dst_ref=got, send_sem=ici_send.at[s], recv_sem=ici_recv.at[s],
                    device_id=(ox, oy, c), device_id_type=MESH).wait_recv()
                fwd = pltpu.make_async_remote_copy(
                    src_ref=got, dst_ref=got, send_sem=d2d_send.at[s], recv_sem=d2d_recv.at[s],
                    device_id=sibling, device_id_type=MESH)
                fwd.start()
                passes.append(fwd)
        for a in range(self.n):
            for k, (ox, oy) in enumerate(others):
                s = a * 3 + k
                theirs = outs[a].at[2 * ox + oy, 1 - c]
                pltpu.make_async_remote_copy(
                    src_ref=theirs, dst_ref=theirs, send_sem=d2d_send.at[s], recv_sem=d2d_recv.at[s],
                    device_id=sibling, device_id_type=MESH).wait_recv()
        for cp in own:
            cp.wait()
        for cp in sends + passes:
            cp.wait_send()


class _PairRider:
    def __init__(self, grads):
        self.operands = list(grads)
        n = self.n = len(grads)
        self.out_shape = [jax.ShapeDtypeStruct((N_CHIPS,) + g.shape[2:], F32) for g in grads]
        self.scratch = [pltpu.SemaphoreType.DMA((N_CHIPS * n,)), pltpu.SemaphoreType.DMA((N_CHIPS * n,))]

    def _copies(self, ins, outs, sems):
        send_sem, recv_sem = sems
        x, y, c, _ = _place()
        return [pltpu.make_async_remote_copy(
            src_ref=ins[a].at[j, 1 - c], dst_ref=outs[a].at[j], send_sem=send_sem.at[a * N_CHIPS + j],
            recv_sem=recv_sem.at[a * N_CHIPS + j], device_id=(x, y, 1 - c), device_id_type=MESH)
            for a in range(self.n) for j in range(N_CHIPS)]

    def start(self, ins, outs, sems):
        for cp in self._copies(ins, outs, sems):
            cp.start()

    def finish(self, ins, outs, sems):
        for cp in self._copies(ins, outs, sems):
            cp.wait()


class _ExchangeRider:
    def __init__(self, parts):
        self.operands = list(parts)
        n = self.n = len(parts)
        self.out_shape = [jax.ShapeDtypeStruct(p.shape, p.dtype) for p in parts]
        self.scratch = [pltpu.SemaphoreType.DMA((3 * n,)), pltpu.SemaphoreType.DMA((3 * n,))]

    def _copies(self, ins, outs, sems):
        send_sem, recv_sem = sems
        x, y, c, others = _place()
        me = 2 * x + y
        return [pltpu.make_async_remote_copy(
            src_ref=ins[a].at[2 * ox + oy], dst_ref=outs[a].at[me], send_sem=send_sem.at[a * 3 + k],
            recv_sem=recv_sem.at[a * 3 + k], device_id=(ox, oy, c), device_id_type=MESH)
            for a in range(self.n) for k, (ox, oy) in enumerate(others)]

    def start(self, ins, outs, sems):
        for cp in self._copies(ins, outs, sems):
            cp.start()

    def finish(self, ins, outs, sems):
        send_sem, recv_sem = sems
        x, y, c, others = _place()
        for a in range(self.n):
            for k, (ox, oy) in enumerate(others):
                s = a * 3 + k
                got = outs[a].at[2 * ox + oy]
                pltpu.make_async_remote_copy(
                    src_ref=got, dst_ref=got, send_sem=send_sem.at[s], recv_sem=recv_sem.at[s],
                    device_id=(ox, oy, c), device_id_type=MESH).wait_recv()
        for cp in self._copies(ins, outs, sems):
            cp.wait_send()


def _run_rider(rider, name):
    n = rider.n

    def body(*refs):
        ins, outs, sems = refs[:n], refs[n:2 * n], refs[2 * n:]
        rider.start(ins, outs, sems)
        rider.finish(ins, outs, sems)

    return pl.pallas_call(
        body, name=name, in_specs=[ANY] * n, out_specs=[ANY] * n,
        out_shape=rider.out_shape, scratch_shapes=rider.scratch)(*rider.operands)


def _ride(rider, body, n_in, n_out, first, last):
    if rider is None:
        return body
    n = rider.n

    def wrapped(*refs):
        host_in, r_in = refs[:n_in], refs[n_in:n_in + n]
        host_out = refs[n_in + n:n_in + n + n_out]
        r_out = refs[n_in + n + n_out:n_in + 2 * n + n_out]
        rest = refs[n_in + 2 * n + n_out:]
        host_scr, sems = rest[:len(rest) - len(rider.scratch)], rest[len(rest) - len(rider.scratch):]

        @pl.when(first())
        def _():
            rider.start(r_in, r_out, sems)

        body(*host_in, *host_out, *host_scr)

        @pl.when(last())
        def _():
            rider.finish(r_in, r_out, sems)

    return wrapped


def _rider_args(rider):
    if rider is None:
        return [], [], [], [], []
    return rider.operands, [ANY] * rider.n, [ANY] * rider.n, rider.out_shape, rider.scratch


def _pair_sum(g, got, c_idx):
    _, _, r, cw = g.shape
    tr = _row_tile(r, cw)

    def body(c_ref, g_ref, got_ref, pb_ref):
        pb_ref[...] = (g_ref[...] + got_ref[...]).astype(BF16)

    blk = pl.BlockSpec((None, tr, cw), lambda j, i, c_ref: (j, i, 0))
    return pl.pallas_call(
        body, name="grad_pair_sum",
        grid_spec=pltpu.PrefetchScalarGridSpec(
            num_scalar_prefetch=1, grid=(N_CHIPS, r // tr),
            in_specs=[pl.BlockSpec((None, None, tr, cw), lambda j, i, c_ref: (j, c_ref[0], i, 0)), blk],
            out_specs=blk),
        out_shape=jax.ShapeDtypeStruct((N_CHIPS, r, cw), BF16),
        compiler_params=_params(("parallel", "parallel")))(c_idx, g, got)


def _chip_sum(g, sib, got, place):
    _, _, r, cw = g.shape
    tr = _row_tile(r, cw)

    def body(place_ref, g_ref, sib_ref, got_ref, t_ref):
        me = place_ref[0]
        own = g_ref[...] + sib_ref[...]
        acc = None
        for s in range(N_CHIPS):
            term = jnp.where(me == s, own, got_ref[s].astype(F32))
            acc = term if acc is None else acc + term
        t_ref[...] = acc

    return pl.pallas_call(
        body, name="grad_chip_sum",
        grid_spec=pltpu.PrefetchScalarGridSpec(
            num_scalar_prefetch=1, grid=(r // tr,),
            in_specs=[pl.BlockSpec((None, None, tr, cw), lambda i, pr: (pr[0], pr[1], i, 0)),
                      pl.BlockSpec((None, tr, cw), lambda i, pr: (pr[0], i, 0)),
                      pl.BlockSpec((N_CHIPS, tr, cw), lambda i, pr: (0, i, 0))],
            out_specs=pl.BlockSpec((tr, cw), lambda i, pr: (i, 0))),
        out_shape=jax.ShapeDtypeStruct((r, cw), F32),
        compiler_params=_params(("parallel",)))(place, g, sib, got)


def _pair_share(halves):
    n = len(halves)

    def body(*refs):
        ins, outs = refs[:n], refs[n:2 * n]
        send_sem, recv_sem = refs[2 * n:]
        x, y, c, _ = _place()
        cps = [pltpu.make_async_remote_copy(
            src_ref=ins[a], dst_ref=outs[a], send_sem=send_sem.at[a], recv_sem=recv_sem.at[a],
            device_id=(x, y, 1 - c), device_id_type=MESH) for a in range(n)]
        for cp in cps:
            cp.start()
        for cp in cps:
            cp.wait()

    return pl.pallas_call(
        body, name="grad_pair_share",
        in_specs=[ANY] * n, out_specs=[ANY] * n,
        out_shape=[jax.ShapeDtypeStruct(h.shape, F32) for h in halves],
        scratch_shapes=[pltpu.SemaphoreType.DMA((n,)), pltpu.SemaphoreType.DMA((n,))],
        )(*halves)


def _small_allreduce(pack):
    m_per, ncol = pack.shape
    n_dev = 8

    def body(x_ref, sum_ref, all_ref, send_sems, recv_sems, local_sem):
        x, y, c, others = _place()
        me, sibling = (x, y, c), (x, y, 1 - c)

        def rows(px, py, pc):
            return all_ref.at[pl.ds((4 * px + 2 * py + pc) * m_per, m_per), :]

        def copy(k, block, to, src=None):
            return pltpu.make_async_remote_copy(
                src_ref=rows(*block) if src is None else src, dst_ref=rows(*block),
                send_sem=send_sems.at[k], recv_sem=recv_sems.at[k], device_id=to, device_id_type=MESH)

        mine = pltpu.make_async_copy(x_ref, rows(*me), local_sem)
        mine.start()
        first = [copy(0, me, sibling, src=x_ref)]
        first += [copy(1 + j, me, (*chip, c), src=x_ref) for j, chip in enumerate(others)]
        for cp in first:
            cp.start()
        passed = [copy(4 + j, (*chip, c), sibling) for j, chip in enumerate(others)]
        for j, chip in enumerate(others):
            copy(1 + j, (*chip, c), me).wait_recv()
            passed[j].start()
        copy(0, sibling, me).wait_recv()
        for j, chip in enumerate(others):
            copy(4 + j, (*chip, 1 - c), me).wait_recv()
        for cp in first + passed:
            cp.wait_send()
        mine.wait()
        acc = all_ref[0:m_per, :]
        for dvc in range(1, n_dev):
            acc = acc + all_ref[dvc * m_per:(dvc + 1) * m_per, :]
        sum_ref[...] = acc

    return pl.pallas_call(
        body, name="small_allreduce",
        in_specs=[pl.BlockSpec(memory_space=pltpu.VMEM)],
        out_specs=pl.BlockSpec(memory_space=pltpu.VMEM),
        out_shape=jax.ShapeDtypeStruct((m_per, ncol), F32),
        scratch_shapes=[pltpu.VMEM((n_dev * m_per, ncol), F32),
                        pltpu.SemaphoreType.DMA((7,)), pltpu.SemaphoreType.DMA((7,)), pltpu.SemaphoreType.DMA],
        )(pack)


def _adam_math(w, g, m, v):
    m = ADAM_B1 * m + (1.0 - ADAM_B1) * g
    v = ADAM_B2 * v + (1.0 - ADAM_B2) * (g * g)
    m_hat = m / (1.0 - ADAM_B1 ** ADAM_STEP)
    v_hat = v / (1.0 - ADAM_B2 ** ADAM_STEP)
    delta = -ADAM_LR * (m_hat / (jnp.sqrt(v_hat) + ADAM_EPS) + ADAM_WD * w)
    return delta, m, v


def _adamw(halves, c_idx, w, m, v, name):
    L = len(halves)
    r, C = halves[0][0].shape
    tr = _row_tile(r, C, 1024 * 1024)
    nt = r // tr

    def body(c_ref, *refs):
        g_refs, (w_ref, m_ref, v_ref), (g_ref, d_ref, nm_ref, nv_ref) = refs[:2 * L], refs[2 * L:2 * L + 3], refs[2 * L + 3:]
        own = pl.program_id(1) == c_ref[0]
        g = None
        for l in range(L):
            cand = jnp.where(own, g_refs[2 * l][...], g_refs[2 * l + 1][...])
            g = cand if g is None else jnp.where(pl.program_id(0) == l, cand, g)
        g_ref[...] = g
        d_ref[...], nm_ref[...], nv_ref[...] = _adam_math(w_ref[...], g, m_ref[...], v_ref[...])

    def half(l, mine):
        def index(ll, h, i, c_ref):
            read = (h == c_ref[0]) if mine else (h != c_ref[0])
            return jnp.where(jnp.logical_and(ll == l, read), i, 0), 0
        return pl.BlockSpec((tr, C), index)

    full = pl.BlockSpec((None, tr, C), lambda ll, h, i, c_ref: (ll, h * nt + i, 0))
    shp = jax.ShapeDtypeStruct((L, 2 * r, C), F32)
    g_specs = [half(l, mine) for l in range(L) for mine in (True, False)]
    return pl.pallas_call(
        body, name=name,
        grid_spec=pltpu.PrefetchScalarGridSpec(
            num_scalar_prefetch=1, grid=(L, 2, nt),
            in_specs=g_specs + [full] * 3, out_specs=[full] * 4),
        out_shape=[shp] * 4,
        compiler_params=_params(("arbitrary", "arbitrary", "arbitrary")))(
            c_idx, *[a for pair in halves for a in pair], w, m, v)


SMALL_ROW_SPANS = ((0, 2), (2, 4), (4, 7), (7, 8), (8, 9))


def _small_update(gsum, w, m, v):
    n = len(SMALL_ROW_SPANS)

    def body(gs_ref, *refs):
        w_refs, m_refs, v_refs = refs[:n], refs[n:2 * n], refs[2 * n:3 * n]
        g_refs, d_refs, nm_refs, nv_refs = [refs[(3 + k) * n:(4 + k) * n] for k in range(4)]
        lg_ref = w_refs[2]
        l0, l1, l2 = lg_ref[0:1, :], lg_ref[1:2, :], lg_ref[2:3, :]
        mx = jnp.maximum(jnp.maximum(l0, l1), l2)
        e0, e1, e2 = jnp.exp(l0 - mx), jnp.exp(l1 - mx), jnp.exp(l2 - mx)
        tot = e0 + e1 + e2
        p0, p1, p2 = e0 / tot, e1 / tot, e2 / tot
        dlb = gs_ref[4:5, :]
        for k, (r0, r1) in enumerate(SMALL_ROW_SPANS):
            if k == 2:
                g = jnp.concatenate([dlb * p0 * (1.0 - p0), -dlb * p0 * p1, -dlb * p0 * p2], axis=0)
            else:
                g = gs_ref[r0:r1, 0:w_refs[k].shape[1]]
            g_refs[k][...] = g
            d_refs[k][...], nm_refs[k][...], nv_refs[k][...] = _adam_math(w_refs[k][...], g, m_refs[k][...], v_refs[k][...])

    full = pl.BlockSpec(memory_space=pltpu.VMEM)
    shapes = [jax.ShapeDtypeStruct(a.shape, F32) for a in w]
    out = pl.pallas_call(
        body, name="small_update", in_specs=[full] * (1 + 3 * n), out_specs=[full] * (4 * n), out_shape=shapes * 4)(
            gsum, *w, *m, *v)
    return [out[k * n:(k + 1) * n] for k in range(4)]


def _pack_small(norm_mix, norm_ffn, lb3, out_norm, final_norm, extra=None):
    ncol = norm_mix.shape[1]
    on = jnp.pad(out_norm.reshape(1, -1), ((0, 0), (0, ncol - out_norm.size)))
    rows = [norm_mix, norm_ffn, lb3, on, final_norm.reshape(1, ncol)]
    if extra is not None:
        rows.append(extra)
    used = sum(r.shape[0] for r in rows)
    rows.append(jnp.zeros((SMALL_ROWS - used, ncol), F32))
    return jnp.concatenate(rows, axis=0)


WEIGHT_NAMES = ("hin", "hout", "qkv", "aout", "fin0", "fin1", "fdn0", "fdn1")
FIRST_WEIGHTS = ("hin",)
LATE_WEIGHTS_A = ("hout", "fin0", "fdn0")
LATE_WEIGHTS_B = ("qkv", "aout", "fin1", "fdn1")


def _split_weights(hgrn_w_in, hgrn_w_out, attn_w_qkv, attn_w_out, ffn_w_in, ffn_w_down):
    return {"hin": hgrn_w_in[0], "hout": hgrn_w_out[0], "qkv": attn_w_qkv[0], "aout": attn_w_out[0],
            "fin0": ffn_w_in[0], "fin1": ffn_w_in[1], "fdn0": ffn_w_down[0], "fdn1": ffn_w_down[1]}


def _halves(v):
    r, c = v.shape
    return v.reshape(2, r // 2, c)


def _full_weights(gathered):
    out = {}
    for k, g in gathered.items():
        _, _, r, c = g.shape
        if k in ("hin", "qkv", "fin0", "fin1"):
            out[k] = g.reshape(N_CHIPS, 2 * r, c)
        else:
            out[k] = g.reshape(N_CHIPS * 2 * r, c)
    return out


class _StepComm:
    def __init__(self, shards, c_idx, me_idx):
        self.shards, self.c_idx, self.me_idx = shards, c_idx, me_idx
        self.halves = {}
        self._stage = {}

    def gather_rider(self, names):
        return _GatherRider([_halves(self.shards[k].astype(BF16)) for k in names])

    def gathered(self, names, got):
        return _full_weights(dict(zip(names, got)))

    def first_weights(self):
        return self.gathered(FIRST_WEIGHTS, _run_rider(self.gather_rider(FIRST_WEIGHTS), "gather_first"))

    def pair_rider(self, grads, tag):
        names = list(grads)
        g4 = []
        for k in names:
            r, c = self.shards[k].shape
            g4.append(grads[k].reshape(N_CHIPS, 2, r // 2, c))
        self._stage[tag] = (names, g4)
        return _PairRider(g4)

    def pair_now(self, grads, tag):
        self.paired(tag, _run_rider(self.pair_rider(grads, tag), "grad_pair_exchange_" + tag))

    def paired(self, tag, got):
        names, g4 = self._stage[tag]
        self._stage[tag] = (names, [(g, s, _pair_sum(g, s, self.c_idx)) for g, s in zip(g4, got)])

    def exchange_rider(self, tag):
        return _ExchangeRider([s[2] for s in self._stage[tag][1]])

    def exchanged(self, tag, got):
        names, sums = self._stage.pop(tag)
        place = jnp.concatenate([self.me_idx, self.c_idx])
        for k, (g, sib, _), recv in zip(names, sums, got):
            self.halves[k] = _chip_sum(g, sib, recv, place)

    def shared_halves(self):
        mine = [self.halves[k] for k in WEIGHT_NAMES]
        return dict(zip(WEIGHT_NAMES, zip(mine, _pair_share(mine))))


def kernel(x, norm_mix, norm_ffn, hgrn_w_in, hgrn_lb_logits, hgrn_out_norm, hgrn_w_out, attn_w_qkv, attn_w_out, ffn_w_in, ffn_w_down, final_norm, loss_target, m_norm_mix, m_norm_ffn, m_hgrn_w_in, m_hgrn_lb_logits, m_hgrn_out_norm, m_hgrn_w_out, m_attn_w_qkv, m_attn_w_out, m_ffn_w_in, m_ffn_w_down, m_final_norm, v_norm_mix, v_norm_ffn, v_hgrn_w_in, v_hgrn_lb_logits, v_hgrn_out_norm, v_hgrn_w_out, v_attn_w_qkv, v_attn_w_out, v_ffn_w_in, v_ffn_w_down, v_final_norm):
    S = x.shape[1]
    xi, yi, ci = lax.axis_index("x"), lax.axis_index("y"), lax.axis_index("c")
    c_idx = jnp.reshape(ci, (1,)).astype(jnp.int32)
    me_idx = jnp.reshape(2 * xi + yi, (1,)).astype(jnp.int32)

    w_own = _split_weights(hgrn_w_in, hgrn_w_out, attn_w_qkv, attn_w_out, ffn_w_in, ffn_w_down)

    comm = _StepComm(w_own, c_idx, me_idx)
    loss, dx, small = _local_step(
        x.reshape(S, D_MODEL), loss_target.reshape(S, D_MODEL), norm_mix, norm_ffn, hgrn_lb_logits,
        hgrn_out_norm, final_norm.reshape(1, D_MODEL), comm)

    halves = comm.shared_halves()
    updated = {}
    for tensor, layers, (wt, mt, vt) in (
            ("hgrn_w_in", ("hin",), (hgrn_w_in, m_hgrn_w_in, v_hgrn_w_in)),
            ("hgrn_w_out", ("hout",), (hgrn_w_out, m_hgrn_w_out, v_hgrn_w_out)),
            ("attn_w_qkv", ("qkv",), (attn_w_qkv, m_attn_w_qkv, v_attn_w_qkv)),
            ("attn_w_out", ("aout",), (attn_w_out, m_attn_w_out, v_attn_w_out)),
            ("ffn_w_in", ("fin0", "fin1"), (ffn_w_in, m_ffn_w_in, v_ffn_w_in)),
            ("ffn_w_down", ("fdn0", "fdn1"), (ffn_w_down, m_ffn_w_down, v_ffn_w_down))):
        updated[tensor] = _adamw([halves[k] for k in layers], c_idx, wt, mt, vt, "adamw_" + tensor)

    loss_row = jnp.pad(loss, ((0, 0), (0, D_MODEL - loss.shape[1])))
    lb3 = jnp.concatenate([small["lb"], jnp.zeros((2, D_MODEL), F32)], axis=0)
    on_grad = jnp.sum(small["out_norm"], axis=0, keepdims=True)
    pack = _pack_small(small["norm_mix"], small["norm_ffn"], lb3, on_grad, small["final_norm"], loss_row)
    gsum = _small_allreduce(pack)
    fn2 = (1, D_MODEL)
    sg, sd, sm, sv = _small_update(
        gsum, (norm_mix, norm_ffn, hgrn_lb_logits, hgrn_out_norm, final_norm.reshape(fn2)),
        (m_norm_mix, m_norm_ffn, m_hgrn_lb_logits, m_hgrn_out_norm, m_final_norm.reshape(fn2)),
        (v_norm_mix, v_norm_ffn, v_hgrn_lb_logits, v_hgrn_out_norm, v_final_norm.reshape(fn2)))

    def assemble(p, which):
        nmx, nff, lbl, onm, fnm = p
        fnm = fnm.reshape(D_MODEL)
        hin, hout, qkv, aout, fin, fdn = [updated[t][which] for t in
                                          ("hgrn_w_in", "hgrn_w_out", "attn_w_qkv", "attn_w_out", "ffn_w_in", "ffn_w_down")]
        return (nmx, nff, hin, lbl, onm, hout, qkv, aout, fin, fdn, fnm)

    total_loss = gsum[9, 0]
    return (total_loss, dx.reshape(1, S, D_MODEL), *assemble(sg, 0), *assemble(sd, 1), *assemble(sm, 2), *assemble(sv, 3))
```

```python
import functools

import jax
import jax.numpy as jnp
import numpy as np
from jax import lax
from jax.experimental import pallas as pl
from jax.experimental.pallas import tpu as pltpu

F32 = jnp.float32
BF16 = jnp.bfloat16
MESH = pl.DeviceIdType.MESH

D_MODEL = 1024
HEAD = 128
HGRN_HEADS = 8
HGRN_CHUNK = 64
HGRN_HEADS_PER_STEP = 2
ATTN_GROUPS = ((128, 1), (512, 4), (2048, 16))
ATTN_SPAN = 128
HEADS_PER_GROUP = 4
GROUP_W = HEADS_PER_GROUP * HEAD
D_FF = 2816
NORM_EPS = 1e-6
ROPE_THETA = 10000.0
NEG = -1e30

ADAM_LR, ADAM_B1, ADAM_B2, ADAM_EPS, ADAM_WD, ADAM_STEP = 0.001, 0.9, 0.999, 1e-08, 0.01, 10

N_CHIPS = 4
VMEM_LIMIT = 56 * 1024 * 1024
SMALL_ROWS = 16


def _params(sem=None):
    return pltpu.CompilerParams(dimension_semantics=sem, vmem_limit_bytes=VMEM_LIMIT)


def _row_tile(rows, cols, budget_bytes=3 * 512 * 1024):
    best = 8
    for t in range(8, rows + 1, 8):
        if rows % t == 0 and t * cols * 4 <= budget_bytes:
            best = t
    assert rows % best == 0
    return best


def _grid_corner(i, j):
    return jnp.logical_and(pl.program_id(0) == i, pl.program_id(1) == j)


def _sigmoid(v):
    return 0.5 * jnp.tanh(0.5 * v) + 0.5


def _dot(a, b):
    return jnp.dot(a, b, preferred_element_type=F32)


def _dot_nt(a, b):
    return lax.dot_general(a, b, (((1,), (1,)), ((), ())), preferred_element_type=F32)


def _dot_tn(a, b):
    return lax.dot_general(a, b, (((0,), (0,)), ((), ())), preferred_element_type=F32)


def _dot_exact(ones, b):
    ones = ones.astype(BF16)
    hi = b.astype(BF16)
    rest = b - hi.astype(F32)
    mid = rest.astype(BF16)
    low = (rest - mid.astype(F32)).astype(BF16)
    return _dot(ones, hi) + _dot(ones, mid) + _dot(ones, low)


def _rstd(v):
    return lax.rsqrt(jnp.mean(v * v, axis=-1, keepdims=True) + NORM_EPS)


def _norm_mm(h, gain, w3, name, out_dtype=F32, tm=2048, rider=None):
    S, K = h.shape
    J, _, n = w3.shape
    gi = S // tm

    def body(h_ref, g_ref, w_ref, y_ref, u_ref):
        @pl.when(pl.program_id(1) == 0)
        def _():
            v = h_ref[...]
            u_ref[...] = (v * _rstd(v) * g_ref[...]).astype(BF16)

        y_ref[...] = _dot(u_ref[...], w_ref[pl.program_id(1)]).astype(y_ref.dtype)

    r_ops, r_in, r_out, r_shape, r_scr = _rider_args(rider)
    res = pl.pallas_call(
        _ride(rider, body, 3, 2, functools.partial(_grid_corner, 0, 0), functools.partial(_grid_corner, gi - 1, J - 1)),
        name=name, grid=(gi, J),
        in_specs=[pl.BlockSpec((tm, K), lambda i, j: (i, 0)),
                  pl.BlockSpec((1, K), lambda i, j: (0, 0)),
                  pl.BlockSpec((J, K, n), lambda i, j: (0, 0, 0), pipeline_mode=pl.Buffered(1))] + r_in,
        out_specs=[pl.BlockSpec((tm, n), lambda i, j: (i, j)), pl.BlockSpec((tm, K), lambda i, j: (i, 0))] + r_out,
        out_shape=[jax.ShapeDtypeStruct((S, J * n), out_dtype), jax.ShapeDtypeStruct((S, K), BF16)] + r_shape,
        scratch_shapes=r_scr,
        compiler_params=_params(("arbitrary", "arbitrary")))(h, gain, w3, *r_ops)
    return res[0], res[1], res[2:]


def _mm_res(h, a, w2, name, tm=1024):
    S, N = h.shape
    K = a.shape[1]

    def body(h_ref, a_ref, w_ref, o_ref):
        o_ref[...] = h_ref[...] + _dot(a_ref[...], w_ref[...])

    return pl.pallas_call(
        body, name=name, grid=(S // tm,),
        in_specs=[pl.BlockSpec((tm, N), lambda i: (i, 0)),
                  pl.BlockSpec((tm, K), lambda i: (i, 0)),
                  pl.BlockSpec((K, N), lambda i: (0, 0))],
        out_specs=pl.BlockSpec((tm, N), lambda i: (i, 0)),
        out_shape=jax.ShapeDtypeStruct((S, N), F32),
        compiler_params=_params(("parallel",)))(h, a, w2)


def _swiglu(z_ref, F):
    g = z_ref[:, :F].astype(F32)
    return (g * _sigmoid(g) * z_ref[:, F:].astype(F32)).astype(BF16)


def _swiglu_mm_res(h, z, w2, name, tm=512):
    S, N = h.shape
    F = w2.shape[0]

    def body(h_ref, z_ref, w_ref, o_ref, a_ref):
        a = _swiglu(z_ref, F)
        a_ref[...] = a
        o_ref[...] = h_ref[...] + _dot(a, w_ref[...])

    return pl.pallas_call(
        body, name=name, grid=(S // tm,),
        in_specs=[pl.BlockSpec((tm, N), lambda i: (i, 0)),
                  pl.BlockSpec((tm, 2 * F), lambda i: (i, 0)),
                  pl.BlockSpec((F, N), lambda i: (0, 0), pipeline_mode=pl.Buffered(1))],
        out_specs=[pl.BlockSpec((tm, N), lambda i: (i, 0)), pl.BlockSpec((tm, F), lambda i: (i, 0))],
        out_shape=[jax.ShapeDtypeStruct((S, N), F32), jax.ShapeDtypeStruct((S, F), BF16)],
        compiler_params=_params(("parallel",)))(h, z, w2)


def _dy_specs(dy, J, n, tm):
    if dy.ndim == 3:
        return [pl.BlockSpec((None, tm, n), functools.partial(lambda i, j: (j, i, 0), j=j)) for j in range(J)]
    return [pl.BlockSpec((tm, n), functools.partial(lambda i, j: (i, j), j=j)) for j in range(J)]


def _acc_nt(dy_refs, w_ref):
    acc = None
    for j, r in enumerate(dy_refs):
        t = _dot_nt(r[...].astype(BF16), w_ref[j])
        acc = t if acc is None else acc + t
    return acc


def _mm_nt(dy, w3, name, out_dtype=F32, tm=1024):
    J, K, n = w3.shape
    S = dy.shape[-2]

    def body(*refs):
        dy_refs, w_ref, o_ref = refs[:J], refs[J], refs[J + 1]
        o_ref[...] = _acc_nt(dy_refs, w_ref).astype(o_ref.dtype)

    return pl.pallas_call(
        body, name=name, grid=(S // tm,),
        in_specs=_dy_specs(dy, J, n, tm) + [pl.BlockSpec((J, K, n), lambda i: (0, 0, 0))],
        out_specs=pl.BlockSpec((tm, K), lambda i: (i, 0)),
        out_shape=jax.ShapeDtypeStruct((S, K), out_dtype),
        compiler_params=_params(("parallel",)))(*([dy] * J), w3)


def _mm_nt_normbwd(dy, w3, h, gain, dh, name, tm=512, rider=None):
    J, K, n = w3.shape
    S = h.shape[0]
    steps = S // tm

    def body(*refs):
        dy_refs, w_ref, h_ref, g_ref, dh_ref, o_ref, dg_ref = refs[:J], *refs[J:]
        du = _acc_nt(dy_refs, w_ref)
        v = h_ref[...]
        r = _rstd(v)
        xh = v * r
        dyg = du * g_ref[...]
        o_ref[...] = dh_ref[...] + r * (dyg - xh * jnp.mean(dyg * xh, axis=-1, keepdims=True))

        @pl.when(pl.program_id(0) == 0)
        def _():
            dg_ref[...] = jnp.zeros_like(dg_ref)

        dg_ref[...] += jnp.sum(du * xh, axis=0, keepdims=True)

    row = pl.BlockSpec((tm, K), lambda i: (i, 0))
    vec = pl.BlockSpec((1, K), lambda i: (0, 0))
    r_ops, r_in, r_out, r_shape, r_scr = _rider_args(rider)
    res = pl.pallas_call(
        _ride(rider, body, J + 4, 2, lambda: pl.program_id(0) == 0, lambda: pl.program_id(0) == steps - 1),
        name=name, grid=(steps,),
        in_specs=_dy_specs(dy, J, n, tm) + [pl.BlockSpec((J, K, n), lambda i: (0, 0, 0)), row, vec, row] + r_in,
        out_specs=[row, vec] + r_out,
        out_shape=[jax.ShapeDtypeStruct((S, K), F32), jax.ShapeDtypeStruct((1, K), F32)] + r_shape,
        scratch_shapes=r_scr,
        compiler_params=_params(("arbitrary",)))(*([dy] * J), w3, h, gain, dh, *r_ops)
    return res[0], res[1], res[2:]


def _mm_nt_swiglu_bwd(dh, w2, z, name, tm=512, chunks=11):
    F, N = w2.shape
    S = dh.shape[0]
    fc = F // chunks
    assert fc * chunks == F and fc % HEAD == 0

    def body(dh_ref, w_ref, z_ref, o_ref):
        dhb = dh_ref[...].astype(BF16)
        da = [_dot_nt(dhb, w_ref[c * fc:(c + 1) * fc, :]) for c in range(chunks)]
        for c in range(chunks):
            g = z_ref[:, c * fc:(c + 1) * fc].astype(F32)
            u = z_ref[:, F + c * fc:F + (c + 1) * fc].astype(F32)
            sg = _sigmoid(g)
            o_ref[:, c * fc:(c + 1) * fc] = (da[c] * u * (sg * (1.0 + g * (1.0 - sg)))).astype(BF16)
            o_ref[:, F + c * fc:F + (c + 1) * fc] = (da[c] * (g * sg)).astype(BF16)

    return pl.pallas_call(
        body, name=name, grid=(S // tm,),
        in_specs=[pl.BlockSpec((tm, N), lambda i: (i, 0)),
                  pl.BlockSpec((F, N), lambda i: (0, 0), pipeline_mode=pl.Buffered(1)),
                  pl.BlockSpec((tm, 2 * F), lambda i: (i, 0))],
        out_specs=pl.BlockSpec((tm, 2 * F), lambda i: (i, 0)),
        out_shape=jax.ShapeDtypeStruct((S, 2 * F), BF16),
        compiler_params=_params(("parallel",)))(dh, w2, z)


def _mm_tn(x, dy, J, n, tn, name):
    tpn = n // tn
    ts = 2048 if x.shape[1] <= 1536 else 1024
    S, K = x.shape
    if dy.ndim == 3:
        dy_spec = pl.BlockSpec((None, ts, tn), lambda c, s: (c // tpn, s, c % tpn))
    else:
        dy_spec = pl.BlockSpec((ts, tn), lambda c, s: (s, c))

    def body(x_ref, dy_ref, o_ref):
        @pl.when(pl.program_id(1) == 0)
        def _():
            o_ref[...] = jnp.zeros_like(o_ref)

        o_ref[...] += _dot_tn(x_ref[...], dy_ref[...].astype(BF16))

    return pl.pallas_call(
        body, name=name, grid=(J * tpn, S // ts),
        in_specs=[pl.BlockSpec((ts, K), lambda c, s: (s, 0)), dy_spec],
        out_specs=pl.BlockSpec((None, K, tn), lambda c, s: (c // tpn, 0, c % tpn)),
        out_shape=jax.ShapeDtypeStruct((J, K, n), F32),
        compiler_params=_params(("parallel", "arbitrary")))(x, dy)


def _loss_head(h, gain, target, tm=1024):
    S, K = h.shape

    def body(h_ref, g_ref, t_ref, dh_ref, loss_ref, dg_ref):
        v = h_ref[...]
        r = _rstd(v)
        xh = v * r
        g = g_ref[...]
        dy = (xh * g - t_ref[...]) * (1.0 / K)
        dyg = dy * g
        dh_ref[...] = r * (dyg - xh * jnp.mean(dyg * xh, axis=-1, keepdims=True))

        @pl.when(pl.program_id(0) == 0)
        def _():
            loss_ref[...] = jnp.zeros_like(loss_ref)
            dg_ref[...] = jnp.zeros_like(dg_ref)

        part = jnp.sum(jnp.sum(dy * dy, axis=-1, keepdims=True), axis=0, keepdims=True) * (0.5 * K)
        lane = lax.broadcasted_iota(jnp.int32, loss_ref.shape, 1)
        loss_ref[...] += jnp.where(lane == 0, part, 0.0)
        dg_ref[...] += jnp.sum(dy * xh, axis=0, keepdims=True)

    row = pl.BlockSpec((tm, K), lambda i: (i, 0))
    vec = pl.BlockSpec((1, K), lambda i: (0, 0))
    return pl.pallas_call(
        body, name="loss_head", grid=(S // tm,),
        in_specs=[row, vec, row],
        out_specs=[row, pl.BlockSpec((1, HEAD), lambda i: (0, 0)), vec],
        out_shape=[jax.ShapeDtypeStruct((S, K), F32), jax.ShapeDtypeStruct((1, HEAD), F32),
                   jax.ShapeDtypeStruct((1, K), F32)],
        compiler_params=_params(("arbitrary",)))(h, gain, target)


def _lower_bound(lg_ref):
    l0, l1, l2 = lg_ref[0:1, :], lg_ref[1:2, :], lg_ref[2:3, :]
    mx = jnp.maximum(jnp.maximum(l0, l1), l2)
    e0, e1, e2 = jnp.exp(l0 - mx), jnp.exp(l1 - mx), jnp.exp(l2 - mx)
    return e0 / (e0 + e1 + e2)


def _chunks(v, ncb):
    C = HGRN_CHUNK
    return [v[c * C:(c + 1) * C] for c in range(ncb)]


def _rows(parts):
    return jnp.concatenate(parts, axis=0)


def _block_gates(qz, fz, lb, ncb):
    C = HGRN_CHUNK
    row = lax.broadcasted_iota(jnp.int32, (C, C), 0)
    col = lax.broadcasted_iota(jnp.int32, (C, C), 1)
    tri = (col <= row).astype(F32)
    first_half = lax.broadcasted_iota(jnp.int32, (C, HEAD), 0) < C // 2
    sig = _sigmoid(fz)
    fg = lb + (1.0 - lb) * sig
    key = 1.0 - fg
    lg = jnp.log(fg)
    lgs = _chunks(lg, ncb)
    b = _rows([_dot_exact(tri, v) for v in lgs])
    r_c = [jnp.sum(jnp.where(first_half, v, 0.0), axis=0, keepdims=True) for v in lgs]
    bl_c = [jnp.sum(v, axis=0, keepdims=True) for v in lgs]
    r = _rows([jnp.broadcast_to(v, (C, HEAD)) for v in r_c])
    e_br, e_rb = jnp.exp(b - r), jnp.exp(r - b)
    e_b = e_br * _rows([jnp.broadcast_to(jnp.exp(v), (C, HEAD)) for v in r_c])
    e_lb = e_rb * _rows([jnp.broadcast_to(jnp.exp(e - v), (C, HEAD)) for e, v in zip(bl_c, r_c)])
    sq = _sigmoid(qz)
    qy = qz * sq
    return sig, fg, key, (e_br, e_rb, e_b, e_lb), bl_c, sq, qy


def _hgrn_fwd(proj, logits, gain, tb=1024, rider=None):
    S = proj.shape[0]
    H, C = HGRN_HEADS, HGRN_CHUNK
    ncb = tb // C

    def one_head(q_ref, f_ref, i_ref, g_ref, lg_ref, gn_ref, o_ref, og_ref, st_ref, state):
        @pl.when(pl.program_id(1) == 0)
        def _():
            state[...] = jnp.zeros_like(state)

        lb = _lower_bound(lg_ref)
        causal = lax.broadcasted_iota(jnp.int32, (C, C), 1) <= lax.broadcasted_iota(jnp.int32, (C, C), 0)
        qz, fz, gz = q_ref[...], f_ref[...], g_ref[...]
        _, _, key, (e_br, e_rb, e_b, e_lb), bl_c, _, qy = _block_gates(qz, fz, lb, ncb)
        qs = _chunks((qy * e_br).astype(BF16), ncb)
        ks = _chunks((key * e_rb).astype(BF16), ncb)
        qb = _chunks((qy * e_b).astype(BF16), ncb)
        ke = _chunks((key * e_lb).astype(BF16), ncb)
        vb = _chunks(i_ref[...].astype(BF16), ncb)
        a = [jnp.where(causal, _dot_nt(qs[c], ks[c]), 0.0).astype(BF16) for c in range(ncb)]
        upd = [_dot_tn(vb[c], ke[c]) for c in range(ncb)]
        o_intra = [_dot(a[c], vb[c]) for c in range(ncb)]
        st = state[...]
        e_l = [jnp.exp(v) for v in bl_c]
        sts = []
        for c in range(ncb):
            sts.append(st)
            st = st * e_l[c] + upd[c]
        state[...] = st
        for c in range(ncb):
            st_ref[c] = sts[c]
        o = _rows([_dot_nt(qb[c], sts[c].astype(BF16)) + o_intra[c] for c in range(ncb)])
        o_ref[...] = o
        og_ref[...] = ((o * _rstd(o) * gn_ref[...]) * (gz * _sigmoid(gz))).astype(BF16)

    def body(q_ref, f_ref, i_ref, g_ref, lg_ref, gn_ref, o_ref, og_ref, st_ref, state):
        for hs in range(HP):
            cols = slice(hs * HEAD, (hs + 1) * HEAD)
            one_head(q_ref.at[:, cols], f_ref.at[:, cols], i_ref.at[:, cols], g_ref.at[:, cols], lg_ref.at[:, cols],
                     gn_ref, o_ref.at[:, cols], og_ref.at[:, cols], st_ref.at[hs], state.at[hs])

    HP, wide = HGRN_HEADS_PER_STEP, HGRN_HEADS_PER_STEP * HEAD
    hg = H // HP

    def part(p):
        return pl.BlockSpec((tb, wide), functools.partial(lambda h, i, p: (i, p * hg + h), p=p))

    nb = S // tb
    r_ops, r_in, r_out, r_shape, r_scr = _rider_args(rider)
    res = pl.pallas_call(
        _ride(rider, body, 6, 3, functools.partial(_grid_corner, 0, 0), functools.partial(_grid_corner, hg - 1, nb - 1)),
        name="hgrn_fwd", grid=(hg, nb),
        in_specs=[part(0), part(1), part(2), part(3),
                  pl.BlockSpec((3, wide), lambda h, i: (0, h)),
                  pl.BlockSpec((1, HEAD), lambda h, i: (0, 0))] + r_in,
        out_specs=[pl.BlockSpec((tb, wide), lambda h, i: (i, h)),
                   pl.BlockSpec((tb, wide), lambda h, i: (i, h)),
                   pl.BlockSpec((HP, ncb, HEAD, HEAD), lambda h, i: (h, i, 0, 0))] + r_out,
        out_shape=[jax.ShapeDtypeStruct((S, H * HEAD), F32),
                   jax.ShapeDtypeStruct((S, H * HEAD), BF16),
                   jax.ShapeDtypeStruct((H, S // C, HEAD, HEAD), F32)] + r_shape,
        scratch_shapes=[pltpu.VMEM((HP, HEAD, HEAD), F32)] + r_scr,
        compiler_params=_params(("arbitrary", "arbitrary")))(proj, proj, proj, proj, logits, gain, *r_ops)
    return res[:3], res[3:]


def _hgrn_bwd(proj, logits, gain, o, states, dog, tb=1024, rider=None):
    S = proj.shape[0]
    H, C = HGRN_HEADS, HGRN_CHUNK
    ncb = tb // C
    nb = S // tb

    def one_head(q_ref, f_ref, i_ref, g_ref, lg_ref, gn_ref, o_ref, st_ref, dog_ref,
                 dp_ref, dlb_ref, dgn_ref, dstate, dst_scr):
        @pl.when(pl.program_id(1) == 0)
        def _():
            dstate[...] = jnp.zeros_like(dstate)
            dlb_ref[...] = jnp.zeros_like(dlb_ref)
            dgn_ref[...] = jnp.zeros_like(dgn_ref)

        lb = _lower_bound(lg_ref)
        oml = 1.0 - lb
        gn = gn_ref[...]
        row = lax.broadcasted_iota(jnp.int32, (C, C), 0)
        col = lax.broadcasted_iota(jnp.int32, (C, C), 1)
        causal = col <= row
        tri_up = (col >= row).astype(F32)
        qz, fz, gz = q_ref[...], f_ref[...], g_ref[...]
        sig, fg, key, (e_br, e_rb, e_b, e_lb), bl_c, sq, qy = _block_gates(qz, fz, lb, ncb)
        qs_v, ks_v = (qy * e_br).astype(BF16), (key * e_rb).astype(BF16)
        qb_v, ke_v = (qy * e_b).astype(BF16), (key * e_lb).astype(BF16)
        qs, ks, qb, ke = _chunks(qs_v, ncb), _chunks(ks_v, ncb), _chunks(qb_v, ncb), _chunks(ke_v, ncb)
        vb = _chunks(i_ref[...].astype(BF16), ncb)
        ov = o_ref[...]
        rs = _rstd(ov)
        xh = ov * rs
        sg = _sigmoid(gz)
        dog_v = dog_ref[...]
        dgz = dog_v * (xh * gn) * (sg * (1.0 + gz * (1.0 - sg)))
        don = dog_v * (gz * sg)
        dgn_ref[...] += jnp.sum(don * xh, axis=0, keepdims=True)
        dyg = don * gn
        do = rs * (dyg - xh * jnp.mean(dyg * xh, axis=-1, keepdims=True))
        dob = _chunks(do.astype(BF16), ncb)
        CH = range(ncb)
        a = [jnp.where(causal, _dot_nt(qs[c], ks[c]), 0.0).astype(BF16) for c in CH]
        da = [jnp.where(causal, _dot_nt(dob[c], vb[c]), 0.0).astype(BF16) for c in CH]
        wst = [_dot_tn(dob[c], qb[c]) for c in CH]
        dv_in = [_dot_tn(a[c], dob[c]) for c in CH]
        dqs = [_dot(da[c], ks[c]) for c in CH]
        dks = [_dot_tn(da[c], qs[c]) for c in CH]
        e_l = [jnp.exp(v) for v in bl_c]
        dst = dstate[...]
        for c in reversed(range(ncb)):
            dst_scr[c] = dst
            dst = wst[c] + dst * e_l[c]
        dstate[...] = dst
        dst1b = [dst_scr[c].astype(BF16) for c in CH]
        dqb = [_dot(dob[c], st_ref[c].astype(BF16)) for c in CH]
        dke = [_dot(vb[c], dst1b[c]) for c in CH]
        dv = [dv_in[c] + _dot_nt(ke[c], dst1b[c]) for c in CH]
        dbl_st = [jnp.sum(dst_scr[c] * st_ref[c], axis=0, keepdims=True) * e_l[c] for c in CH]
        dqs, dks, dqb, dke, dv = _rows(dqs), _rows(dks), _rows(dqb), _rows(dke), _rows(dv)
        dke_ke = dke * ke_v.astype(F32)
        db = dqs * qs_v.astype(F32) - dks * ks_v.astype(F32) + dqb * qb_v.astype(F32) - dke_ke
        dlg = []
        for c, (db_c, kk_c) in enumerate(zip(_chunks(db, ncb), _chunks(dke_ke, ncb))):
            dbl = jnp.sum(kk_c, axis=0, keepdims=True) + dbl_st[c]
            dlg.append(_dot_exact(tri_up, db_c) + dbl)
        dlg = _rows(dlg)
        dkey = dks * e_rb + dke * e_lb
        dqy = dqs * e_br + dqb * e_b
        dfg = dlg / fg - dkey
        dlb_ref[...] += jnp.sum(dfg * (1.0 - sig), axis=0, keepdims=True)
        dp_ref[0] = (dqy * (sq * (1.0 + qz * (1.0 - sq)))).astype(BF16)
        dp_ref[1] = (dfg * oml * sig * (1.0 - sig)).astype(BF16)
        dp_ref[2] = dv.astype(BF16)
        dp_ref[3] = dgz.astype(BF16)

    def body(q_ref, f_ref, i_ref, g_ref, lg_ref, gn_ref, o_ref, st_ref, dog_ref,
             dp_ref, dlb_ref, dgn_ref, dstate, dst_scr):
        for hs in range(HP):
            cols = slice(hs * HEAD, (hs + 1) * HEAD)
            one_head(q_ref.at[:, cols], f_ref.at[:, cols], i_ref.at[:, cols], g_ref.at[:, cols], lg_ref.at[:, cols],
                     gn_ref, o_ref.at[:, cols], st_ref.at[hs], dog_ref.at[:, cols],
                     dp_ref.at[:, :, cols], dlb_ref.at[hs], dgn_ref.at[hs], dstate.at[hs], dst_scr)

    HP, wide = HGRN_HEADS_PER_STEP, HGRN_HEADS_PER_STEP * HEAD
    hg = H // HP

    def part(p):
        return pl.BlockSpec((tb, wide), functools.partial(lambda h, i, p: (nb - 1 - i, p * hg + h), p=p))

    blk = pl.BlockSpec((tb, wide), lambda h, i: (nb - 1 - i, h))
    acc = pl.BlockSpec((HP, 1, HEAD), lambda h, i: (h, 0, 0))
    r_ops, r_in, r_out, r_shape, r_scr = _rider_args(rider)
    res = pl.pallas_call(
        _ride(rider, body, 9, 3, functools.partial(_grid_corner, 0, 0), functools.partial(_grid_corner, hg - 1, nb - 1)),
        name="hgrn_bwd", grid=(hg, nb),
        in_specs=[part(0), part(1), part(2), part(3),
                  pl.BlockSpec((3, wide), lambda h, i: (0, h)),
                  pl.BlockSpec((1, HEAD), lambda h, i: (0, 0)),
                  blk,
                  pl.BlockSpec((HP, ncb, HEAD, HEAD), lambda h, i: (h, nb - 1 - i, 0, 0)),
                  blk] + r_in,
        out_specs=[pl.BlockSpec((4, tb, wide), lambda h, i: (0, nb - 1 - i, h)), acc, acc] + r_out,
        out_shape=[jax.ShapeDtypeStruct((4, S, H * HEAD), BF16),
                   jax.ShapeDtypeStruct((H, 1, HEAD), F32),
                   jax.ShapeDtypeStruct((H, 1, HEAD), F32)] + r_shape,
        scratch_shapes=[pltpu.VMEM((HP, HEAD, HEAD), F32), pltpu.VMEM((ncb, HEAD, HEAD), F32)] + r_scr,
        compiler_params=_params(("arbitrary", "arbitrary")))(
            proj, proj, proj, proj, logits, gain, o, states, dog, *r_ops)
    return res[:3], res[3:]


def _rope(v, cos, sin):
    return v * cos + pltpu.roll(v, HEAD // 2, 1) * sin


def _lane_pick(tile, hh):
    lane = lax.broadcasted_iota(jnp.int32, tile.shape, 1)
    return jnp.sum(jnp.where(lane == hh, tile, 0.0), axis=-1, keepdims=True)


def _lane_place(cols):
    rows = cols[0].shape[0]
    lane = lax.broadcasted_iota(jnp.int32, (rows, HEAD), 1)
    tile = jnp.zeros((rows, HEAD), F32)
    for hh, v in enumerate(cols):
        tile = jnp.where(lane == hh, v, tile)
    return tile


def _band_masks():
    qi = lax.broadcasted_iota(jnp.int32, (ATTN_SPAN, ATTN_SPAN), 0)
    kj = lax.broadcasted_iota(jnp.int32, (ATTN_SPAN, ATTN_SPAN), 1)
    return kj <= qi, kj >= qi


ATTN_TILE_BLOCKS = 8


def _attn_fwd(a):
    d, L, _ = a.shape
    B, W = min(ATTN_TILE_BLOCKS, a.shape[1] // ATTN_SPAN), ATTN_SPAN
    T = B * W
    assert L % T == 0
    steps = L // T
    scale = HEAD ** -0.5

    def body(q_ref, kc_ref, kp_ref, vc_ref, vp_ref, o_ref, lse_ref):
        n = pl.program_id(1)
        mask_c, mask_p0 = _band_masks()
        first = jnp.logical_and(mask_p0, n > 0)
        units = [(b, hh) for b in range(B) for hh in range(HEADS_PER_GROUP)]
        rows = [slice(b * W, (b + 1) * W) for b in range(B)]
        cols = [slice(hh * HEAD, (hh + 1) * HEAD) for hh in range(HEADS_PER_GROUP)]

        def prev_keys(ref, tile, b, hh):
            return ref[:, cols[hh]] if b == 0 else tile[rows[b - 1], cols[hh]]

        s_c = [jnp.where(mask_c, _dot_nt(q_ref[rows[b], cols[hh]], kc_ref[rows[b], cols[hh]]) * scale, NEG) for b, hh in units]
        s_p = [jnp.where(first if b == 0 else mask_p0,
                         _dot_nt(q_ref[rows[b], cols[hh]], prev_keys(kp_ref, kc_ref, b, hh)) * scale, NEG) for b, hh in units]
        m = [jnp.maximum(jnp.max(x, axis=-1, keepdims=True), jnp.max(y, axis=-1, keepdims=True)) for x, y in zip(s_c, s_p)]
        p_c = [jnp.exp(x - mm) for x, mm in zip(s_c, m)]
        p_p = [jnp.exp(y - mm) for y, mm in zip(s_p, m)]
        l = [jnp.sum(x, axis=-1, keepdims=True) + jnp.sum(y, axis=-1, keepdims=True) for x, y in zip(p_c, p_p)]
        acc = [_dot(p_c[i].astype(BF16), vc_ref[rows[b], cols[hh]]) + _dot(p_p[i].astype(BF16), prev_keys(vp_ref, vc_ref, b, hh))
               for i, (b, hh) in enumerate(units)]
        for i, (b, hh) in enumerate(units):
            o_ref[rows[b], cols[hh]] = (acc[i] / l[i]).astype(BF16)
        for b in range(B):
            lse_ref[rows[b], :] = _lane_place([m[i] + jnp.log(l[i]) for i, (bb, _) in enumerate(units) if bb == b])

    def cur(part):
        return pl.BlockSpec((None, T, GROUP_W), functools.partial(lambda r, n, p: (r, n, p), p=part))

    def prev(part):
        return pl.BlockSpec((None, W, GROUP_W), functools.partial(lambda r, n, p: (r, jnp.maximum(n * B - 1, 0), p), p=part))

    return pl.pallas_call(
        body, name=f"attn_fwd_d{d}", grid=(d, steps),
        in_specs=[cur(0), cur(1), prev(1), cur(2), prev(2)],
        out_specs=[pl.BlockSpec((None, T, GROUP_W), lambda r, n: (r, n, 0)), pl.BlockSpec((None, T, HEAD), lambda r, n: (r, n, 0))],
        out_shape=[jax.ShapeDtypeStruct((d, L, GROUP_W), BF16), jax.ShapeDtypeStruct((d, L, HEAD), F32)],
        compiler_params=_params(("parallel", "arbitrary")))(a, a, a, a, a)


def _attn_bwd(a, do, lse, dd):
    d, L, _ = a.shape
    B, W = min(ATTN_TILE_BLOCKS, a.shape[1] // ATTN_SPAN), ATTN_SPAN
    T = B * W
    assert L % T == 0
    steps = L // T
    scale = HEAD ** -0.5

    def body(qc_ref, qn_ref, kp_ref, kc_ref, vp_ref, vc_ref, doc_ref, don_ref, lc_ref, ln_ref, ddc_ref, ddn_ref, da_ref):
        n = pl.program_id(1)
        mask_c, mask_p0 = _band_masks()
        first = jnp.logical_and(mask_p0, n > 0)
        last = jnp.logical_and(mask_p0, n < steps - 1)
        H4 = range(HEADS_PER_GROUP)
        units = [(b, hh) for b in range(B) for hh in H4]
        rows = [slice(b * W, (b + 1) * W) for b in range(B)]
        cols = [slice(hh * HEAD, (hh + 1) * HEAD) for hh in H4]
        q = {u: qc_ref[rows[u[0]], cols[u[1]]] for u in units}
        k = {u: kc_ref[rows[u[0]], cols[u[1]]] for u in units}
        v = {u: vc_ref[rows[u[0]], cols[u[1]]] for u in units}
        g_o = {u: doc_ref[rows[u[0]], cols[u[1]]] for u in units}
        kb = {(b, hh): kp_ref[:, cols[hh]] if b == 0 else k[(b - 1, hh)] for b, hh in units}
        vb = {(b, hh): vp_ref[:, cols[hh]] if b == 0 else v[(b - 1, hh)] for b, hh in units}
        lse_t = {(b, hh): _lane_pick(lc_ref[rows[b], :], hh) for b, hh in units}
        dd_t = {(b, hh): _lane_pick(ddc_ref[rows[b], :], hh) for b, hh in units}
        p_c = {u: jnp.where(mask_c, jnp.exp(_dot_nt(q[u], k[u]) * scale - lse_t[u]), 0.0) for u in units}
        p_p = {u: jnp.where(first if u[0] == 0 else mask_p0, jnp.exp(_dot_nt(q[u], kb[u]) * scale - lse_t[u]), 0.0) for u in units}
        ds_c = {u: (p_c[u] * (_dot_nt(g_o[u], v[u]) + dd_t[u])).astype(BF16) for u in units}
        ds_p = {u: (p_p[u] * (_dot_nt(g_o[u], vb[u]) + dd_t[u])).astype(BF16) for u in units}
        qn = [qn_ref[:, c] for c in cols]
        g_n = [don_ref[:, c] for c in cols]
        p_n = [jnp.where(last, jnp.exp(_dot_nt(qn[hh], k[(B - 1, hh)]) * scale - _lane_pick(ln_ref[...], hh)), 0.0) for hh in H4]
        ds_n = [(p_n[hh] * (_dot_nt(g_n[hh], v[(B - 1, hh)]) + _lane_pick(ddn_ref[...], hh))).astype(BF16) for hh in H4]
        dq = {u: (_dot(ds_c[u], k[u]) + _dot(ds_p[u], kb[u])) * scale for u in units}
        dk, dv = {}, {}
        for b, hh in units:
            if b < B - 1:
                nxt = (b + 1, hh)
                dk[(b, hh)] = (_dot_tn(ds_c[(b, hh)], q[(b, hh)]) + _dot_tn(ds_p[nxt], q[nxt])) * scale
                dv[(b, hh)] = _dot_tn(p_c[(b, hh)].astype(BF16), g_o[(b, hh)]) + _dot_tn(p_p[nxt].astype(BF16), g_o[nxt])
            else:
                dk[(b, hh)] = (_dot_tn(ds_c[(b, hh)], q[(b, hh)]) + _dot_tn(ds_n[hh], qn[hh])) * scale
                dv[(b, hh)] = _dot_tn(p_c[(b, hh)].astype(BF16), g_o[(b, hh)]) + _dot_tn(p_n[hh].astype(BF16), g_n[hh])
        for b, hh in units:
            da_ref[rows[b], cols[hh]] = dq[(b, hh)].astype(BF16)
            da_ref[rows[b], GROUP_W + hh * HEAD:GROUP_W + (hh + 1) * HEAD] = dk[(b, hh)].astype(BF16)
            da_ref[rows[b], 2 * GROUP_W + hh * HEAD:2 * GROUP_W + (hh + 1) * HEAD] = dv[(b, hh)].astype(BF16)

    nb = L // W

    def cur(width, part):
        return pl.BlockSpec((None, T, width), functools.partial(lambda r, n, p: (r, n, p), p=part))

    def prev(width, part):
        return pl.BlockSpec((None, W, width), functools.partial(lambda r, n, p: (r, jnp.maximum(n * B - 1, 0), p), p=part))

    def nxt(width, part):
        return pl.BlockSpec((None, W, width), functools.partial(lambda r, n, p: (r, jnp.minimum(n * B + B, nb - 1), p), p=part))

    g = GROUP_W
    return pl.pallas_call(
        body, name=f"attn_bwd_d{d}", grid=(d, steps),
        in_specs=[cur(g, 0), nxt(g, 0), prev(g, 1), cur(g, 1), prev(g, 2), cur(g, 2),
                  cur(g, 0), nxt(g, 0), cur(HEAD, 0), nxt(HEAD, 0), cur(HEAD, 0), nxt(HEAD, 0)],
        out_specs=pl.BlockSpec((None, T, 3 * g), lambda r, n: (r, n, 0)),
        out_shape=jax.ShapeDtypeStruct((d, L, 3 * g), BF16),
        compiler_params=_params(("parallel", "arbitrary")))(
            a, a, a, a, a, a, do, do, lse, lse, dd, dd)


def _softmax3(ls):
    mx = jnp.maximum(jnp.maximum(ls[0], ls[1]), ls[2])
    es = [jnp.exp(v - mx) for v in ls]
    tot = es[0] + es[1] + es[2]
    return [e / tot for e in es]


HEAD_COLS = [slice(hh * HEAD, (hh + 1) * HEAD) for hh in range(HEADS_PER_GROUP)]


def _group_spec(d, tm):
    return pl.BlockSpec((d, tm // d, GROUP_W), lambda i: (0, i, 0))


def _gather_heads(ref, scr, d, tm):
    if d == 1:
        return [ref[0, :, cols].astype(F32) for cols in HEAD_COLS]
    for hh, cols in enumerate(HEAD_COLS):
        for r in range(d):
            scr.at[hh][pl.ds(r, tm // d, stride=d), :] = ref[r, :, cols].astype(F32)
    return [scr[hh] for hh in range(HEADS_PER_GROUP)]


def _tile_spec(d, tm):
    return pl.BlockSpec((d, tm // d, HEAD), lambda i: (0, i, 0))


def _gather_tile(ref, scr, d, tm):
    if d == 1:
        return ref[0]
    for r in range(d):
        scr[pl.ds(r, tm // d, stride=d), :] = ref[r]
    return scr[...]


def _scatter_tile(val, scr, ref, d, tm):
    if d == 1:
        ref[0] = val
        return
    scr[...] = val
    for r in range(d):
        ref[r] = scr[pl.ds(r, tm // d, stride=d), :]


def _scatter_heads(vals, scr, ref, d, tm):
    if d == 1:
        for cols, v in zip(HEAD_COLS, vals):
            ref[0, :, cols] = v.astype(ref.dtype)
        return
    for hh, v in enumerate(vals):
        scr[hh] = v
    for hh, cols in enumerate(HEAD_COLS):
        for r in range(d):
            ref[r, :, cols] = scr.at[hh][pl.ds(r, tm // d, stride=d), :].astype(ref.dtype)


def _qkv_dilated(h, gain, w4, gi, cos, sin, d, tm=2048):
    S, K = h.shape
    n_shard = w4.shape[2]
    assert n_shard % HEAD == 0

    def head_cols(hh):
        def index(i, p):
            c = p * (len(ATTN_GROUPS) * GROUP_W) + gi * GROUP_W + hh * HEAD
            return c // n_shard, 0, (c % n_shard) // HEAD
        return pl.BlockSpec((None, K, HEAD), index)

    def body(h_ref, g_ref, *refs):
        w_refs, (cos_ref, sin_ref, out_ref, u_ref, y_scr) = refs[:HEADS_PER_GROUP], refs[HEADS_PER_GROUP:]
        p = pl.program_id(1)

        @pl.when(p == 0)
        def _():
            v = h_ref[...]
            u_ref[...] = (v * _rstd(v) * g_ref[...]).astype(BF16)

        y = _dot(u_ref[...], jnp.concatenate([r[...] for r in w_refs], axis=1))
        heads = [slice(hh * HEAD, (hh + 1) * HEAD) for hh in range(HEADS_PER_GROUP)]
        if d > 1:
            for hh, cols in enumerate(heads):
                y_scr[hh] = y[:, cols]

        def rows_of(hh, r):
            return y[:, heads[hh]] if d == 1 else y_scr.at[hh][pl.ds(r, tm // d, stride=d), :]

        @pl.when(p < 2)
        def _():
            for r in range(d):
                rows = slice(None) if d == 1 else pl.ds(r, tm // d, stride=d)
                cr, sr = cos_ref[rows, :], sin_ref[rows, :]
                for hh, cols in enumerate(heads):
                    out_ref[r, :, cols] = _rope(rows_of(hh, r), cr, sr).astype(BF16)

        @pl.when(p == 2)
        def _():
            for r in range(d):
                for hh, cols in enumerate(heads):
                    out_ref[r, :, cols] = rows_of(hh, r).astype(BF16)

    tab = pl.BlockSpec((tm, HEAD), lambda i, p: (i, 0))
    return pl.pallas_call(
        body, name=f"attn_qkv_d{d}", grid=(S // tm, 3),
        in_specs=[pl.BlockSpec((tm, K), lambda i, p: (i, 0)),
                  pl.BlockSpec((1, K), lambda i, p: (0, 0)),
                  *[head_cols(hh) for hh in range(HEADS_PER_GROUP)], tab, tab],
        out_specs=[pl.BlockSpec((d, tm // d, GROUP_W), lambda i, p: (0, i, p)), pl.BlockSpec((tm, K), lambda i, p: (i, 0))],
        out_shape=[jax.ShapeDtypeStruct((d, S // d, 3 * GROUP_W), BF16), jax.ShapeDtypeStruct((S, K), BF16)],
        scratch_shapes=[pltpu.VMEM((HEADS_PER_GROUP, tm, HEAD), F32)],
        compiler_params=_params(("parallel", "arbitrary")))(h, gain, *[w4] * HEADS_PER_GROUP, cos, sin)


def _undilate_group(da, dqkv, cos, sin, g, tm=2048):
    d, L, _ = da.shape
    S = d * L
    G = len(ATTN_GROUPS)

    def body(*refs):
        da_ref, cos_ref, sin_ref, out_ref, scr = refs[0], refs[1], refs[2], refs[-2], refs[-1]
        p = pl.program_id(1)
        heads = [slice(hh * HEAD, (hh + 1) * HEAD) for hh in range(HEADS_PER_GROUP)]
        if d > 1:
            for hh, cols in enumerate(heads):
                for r in range(d):
                    scr.at[hh][pl.ds(r, tm // d, stride=d), :] = da_ref[r, :, cols].astype(F32)

        def tokens(hh):
            return da_ref[0, :, heads[hh]].astype(F32) if d == 1 else scr[hh]

        @pl.when(p < 2)
        def _():
            cr, sr = cos_ref[...], -sin_ref[...]
            for hh, cols in enumerate(heads):
                out_ref[:, cols] = _rope(tokens(hh), cr, sr).astype(BF16)

        @pl.when(p == 2)
        def _():
            for hh, cols in enumerate(heads):
                out_ref[:, cols] = tokens(hh).astype(BF16)

    tab = pl.BlockSpec((tm, HEAD), lambda i, p: (i, 0))
    operands = (da, cos, sin) if dqkv is None else (da, cos, sin, dqkv)
    return pl.pallas_call(
        body, name=f"attn_undilate_d{d}", grid=(S // tm, 3),
        in_specs=[pl.BlockSpec((d, tm // d, GROUP_W), lambda i, p: (0, i, p)), tab, tab] + ([] if dqkv is None else [ANY]),
        out_specs=pl.BlockSpec((tm, GROUP_W), lambda i, p: (i, p * G + g)),
        out_shape=jax.ShapeDtypeStruct((S, 3 * G * GROUP_W), BF16),
        input_output_aliases={} if dqkv is None else {3: 0},
        scratch_shapes=[pltpu.VMEM((HEADS_PER_GROUP, tm, HEAD), F32)],
        compiler_params=_params(("parallel", "arbitrary")))(*operands)


def _attn_merge(os_, lses, h, w2, tm=1024):
    G = len(os_)
    S, N = h.shape

    def body(*refs):
        o_refs, l_refs, h_ref, w_ref, res_ref, out_ref = refs[:G], refs[G:2 * G], *refs[2 * G:2 * G + 4]
        scr = refs[2 * G + 4:]
        o = [_gather_heads(o_refs[g], scr[g], d, tm) for g, (_, d) in enumerate(ATTN_GROUPS)]
        l = [_gather_tile(l_refs[g], scr[G + g].at[0], d, tm) for g, (_, d) in enumerate(ATTN_GROUPS)]
        for hh in range(HEADS_PER_GROUP):
            al = _softmax3([_lane_pick(l[g], hh) for g in range(G)])
            for g in range(G):
                out_ref[:, g * GROUP_W + hh * HEAD:g * GROUP_W + (hh + 1) * HEAD] = (o[g][hh] * al[g]).astype(BF16)
        res_ref[...] = h_ref[...] + _dot(out_ref[...], w_ref[...])

    specs = [_group_spec(d, tm) for _, d in ATTN_GROUPS]
    row = pl.BlockSpec((tm, N), lambda i: (i, 0))
    return pl.pallas_call(
        body, name="attn_merge_out", grid=(S // tm,),
        in_specs=specs + [_tile_spec(d, tm) for _, d in ATTN_GROUPS] + [
            row, pl.BlockSpec((G * GROUP_W, N), lambda i: (0, 0), pipeline_mode=pl.Buffered(1))],
        out_specs=[row, pl.BlockSpec((tm, G * GROUP_W), lambda i: (i, 0))],
        out_shape=[jax.ShapeDtypeStruct((S, N), F32), jax.ShapeDtypeStruct((S, G * GROUP_W), BF16)],
        scratch_shapes=[pltpu.VMEM((HEADS_PER_GROUP, tm, HEAD), F32)] * (2 * G),
        compiler_params=_params(("parallel",)))(*os_, *lses, h, w2)


def _attn_merge_bwd(os_, lses, dh, w2, tm=512):
    G = len(os_)
    S, N = dh.shape

    def body(*refs):
        o_refs, l_refs, dh_ref, w_ref = refs[:G], refs[G:2 * G], refs[2 * G], refs[2 * G + 1]
        do_refs, dd_refs = refs[2 * G + 2:3 * G + 2], refs[3 * G + 2:4 * G + 2]
        scr = refs[4 * G + 2:]
        doa = _dot_nt(dh_ref[...].astype(BF16), w_ref[...])
        o = [_gather_heads(o_refs[g], scr[g], d, tm) for g, (_, d) in enumerate(ATTN_GROUPS)]
        l = [_gather_tile(l_refs[g], scr[G + g].at[0], d, tm) for g, (_, d) in enumerate(ATTN_GROUPS)]
        do = [[None] * HEADS_PER_GROUP for _ in range(G)]
        dd = [[None] * HEADS_PER_GROUP for _ in range(G)]
        for hh in range(HEADS_PER_GROUP):
            al = _softmax3([_lane_pick(l[g], hh) for g in range(G)])
            mix = None
            for g in range(G):
                dg = doa[:, g * GROUP_W + hh * HEAD:g * GROUP_W + (hh + 1) * HEAD]
                do[g][hh] = dg * al[g]
                t = al[g] * jnp.sum(dg * o[g][hh], axis=-1, keepdims=True)
                mix = t if mix is None else mix + t
            for g in range(G):
                dd[g][hh] = -al[g] * mix
        for g, (_, d) in enumerate(ATTN_GROUPS):
            _scatter_heads(do[g], scr[2 * G + g], do_refs[g], d, tm)
            _scatter_tile(_lane_place(dd[g]), scr[3 * G + g].at[0], dd_refs[g], d, tm)

    specs = [_group_spec(d, tm) for _, d in ATTN_GROUPS]
    tiles = [_tile_spec(d, tm) for _, d in ATTN_GROUPS]
    do_shapes = [jax.ShapeDtypeStruct((d, S // d, GROUP_W), BF16) for _, d in ATTN_GROUPS]
    dd_shapes = [jax.ShapeDtypeStruct((d, S // d, HEAD), F32) for _, d in ATTN_GROUPS]
    return pl.pallas_call(
        body, name="attn_merge_bwd", grid=(S // tm,),
        in_specs=specs + tiles + [pl.BlockSpec((tm, N), lambda i: (i, 0)),
                                  pl.BlockSpec((G * GROUP_W, N), lambda i: (0, 0), pipeline_mode=pl.Buffered(1))],
        out_specs=specs + tiles,
        out_shape=do_shapes + dd_shapes,
        scratch_shapes=[pltpu.VMEM((HEADS_PER_GROUP, tm, HEAD), F32)] * (4 * G),
        compiler_params=_params(("parallel",)))(*os_, *lses, dh, w2)


def _rope_tables(S):
    inv_freq = (1.0 / (np.float32(ROPE_THETA) ** (np.arange(0, HEAD, 2, dtype=np.float32) / np.float32(HEAD))))
    ang = (np.arange(S, dtype=np.float32)[:, None] * inv_freq.astype(np.float32)[None, :]).astype(np.float64)
    cos, sin = np.cos(ang).astype(np.float32), np.sin(ang).astype(np.float32)
    return jnp.asarray(np.concatenate([cos, cos], axis=-1)), jnp.asarray(np.concatenate([-sin, sin], axis=-1))


def _local_step(x, target, norm_mix, norm_ffn, lb_logits, out_gain, final_norm, comm):
    S = x.shape[0]
    nm0, nm1 = norm_mix[0:1], norm_mix[1:2]
    nf0, nf1 = norm_ffn[0:1], norm_ffn[1:2]
    w = comm.first_weights()

    proj, u0, got = _norm_mm(x, nm0, w["hin"], "hgrn_in", rider=comm.gather_rider(LATE_WEIGHTS_A))
    w.update(comm.gathered(LATE_WEIGHTS_A, got))
    (o, og, states), got = _hgrn_fwd(proj, lb_logits, out_gain, rider=comm.gather_rider(LATE_WEIGHTS_B))
    w.update(comm.gathered(LATE_WEIGHTS_B, got))
    fin_tn = w["fin0"].shape[2]
    h1 = _mm_res(x, og, w["hout"], "hgrn_out")
    z0, u1, _ = _norm_mm(h1, nf0, w["fin0"], "ffn0_in", out_dtype=BF16)
    h2, act0 = _swiglu_mm_res(h1, z0, w["fdn0"], "ffn0_down")
    cos, sin = _rope_tables(S)
    G = len(ATTN_GROUPS)
    a_g, u2 = zip(*[_qkv_dilated(h2, nm1, w["qkv"], gi, cos, sin, d) for gi, (_, d) in enumerate(ATTN_GROUPS)])
    o_g, lse_g = zip(*[_attn_fwd(a) for a in a_g])
    h3, oa = _attn_merge(o_g, lse_g, h2, w["aout"])
    z1, u3, _ = _norm_mm(h3, nf1, w["fin1"], "ffn1_in", out_dtype=BF16)
    h4, act1 = _swiglu_mm_res(h3, z1, w["fdn1"], "ffn1_down")
    dh4, loss, d_final = _loss_head(h4, final_norm, target)

    grads, small = {}, {"final_norm": d_final}

    def ffn_bwd(dh, h_in, u_in, z, act, gain, w_in, w_dn, tag, ride=None):
        dz = _mm_nt_swiglu_bwd(dh, w_dn, z, tag + "_down_dx")
        g_dn = _mm_tn(act, dh, 1, D_MODEL, D_MODEL, tag + "_down_dw")[0]
        g_in = _mm_tn(u_in, dz, N_CHIPS, fin_tn, fin_tn, tag + "_in_dw")
        rider = None if ride is None else ride(g_in, g_dn)
        dh_in, dgain, got = _mm_nt_normbwd(dz, w_in, h_in, gain, dh, tag + "_in_dx", rider=rider)
        return dh_in, dgain, g_in, g_dn, got

    dh3, d_nf1, grads["fin1"], grads["fdn1"], _ = ffn_bwd(dh4, h3, u3, z1, act1, nf1, w["fin1"], w["fdn1"], "ffn1")
    grads["aout"] = _mm_tn(oa, dh3, 1, D_MODEL, D_MODEL, "attn_out_dw")[0]
    merged = _attn_merge_bwd(o_g, lse_g, dh3, w["aout"])
    G = len(ATTN_GROUPS)
    das = [_attn_bwd(a_g[gi], merged[gi], lse_g[gi], merged[G + gi]) for gi in range(G)]
    dqkv = None
    for gi in range(G):
        dqkv = _undilate_group(das[gi], dqkv, cos, sin, gi)
    n_qkv = w["qkv"].shape[2]
    grads["qkv"] = _mm_tn(u2[0], dqkv, N_CHIPS, n_qkv, n_qkv, "attn_qkv_dw")
    dh2, d_nm1, _ = _mm_nt_normbwd(dqkv, w["qkv"], h2, nm1, dh3, "attn_qkv_dx")

    def ride_early(g_in, g_dn):
        return comm.pair_rider({**grads, "fin0": g_in, "fdn0": g_dn}, "early")

    dh1, d_nf0, _, _, got = ffn_bwd(dh2, h1, u1, z0, act0, nf0, w["fin0"], w["fdn0"], "ffn0", ride=ride_early)
    comm.paired("early", got)
    dog = _mm_nt(dh1, w["hout"][None], "hgrn_out_dx")
    (dproj, dlb, dgn), got = _hgrn_bwd(proj, lb_logits, out_gain, o, states, dog, rider=comm.exchange_rider("early"))
    comm.exchanged("early", got)
    late = {"hout": _mm_tn(og, dh1, 1, D_MODEL, D_MODEL, "hgrn_out_dw")[0],
            "hin": _mm_tn(u0, dproj, N_CHIPS, D_MODEL, D_MODEL, "hgrn_in_dw")}
    comm.pair_now(late, "late")
    dx, d_nm0, got = _mm_nt_normbwd(dproj, w["hin"], x, nm0, dh1, "hgrn_in_dx", rider=comm.exchange_rider("late"))
    comm.exchanged("late", got)

    small["norm_mix"] = jnp.concatenate([d_nm0, d_nm1], axis=0)
    small["norm_ffn"] = jnp.concatenate([d_nf0, d_nf1], axis=0)
    small["lb"] = dlb.reshape(1, HGRN_HEADS * HEAD)
    small["out_norm"] = dgn.reshape(HGRN_HEADS, HEAD)
    return loss, dx, small


def _place():
    x, y, c = lax.axis_index("x"), lax.axis_index("y"), lax.axis_index("c")
    others = [(1 - x, y), (x, 1 - y), (1 - x, 1 - y)]
    return x, y, c, others


ANY = pl.BlockSpec(memory_space=pl.ANY)


class _GatherRider:
    def __init__(self, shards):
        self.operands = list(shards)
        n = self.n = len(shards)
        self.out_shape = [jax.ShapeDtypeStruct((N_CHIPS,) + s.shape, s.dtype) for s in shards]
        self.scratch = [pltpu.SemaphoreType.DMA((3 * n,)), pltpu.SemaphoreType.DMA((3 * n,)),
                        pltpu.SemaphoreType.DMA((3 * n,)), pltpu.SemaphoreType.DMA((3 * n,)),
                        pltpu.SemaphoreType.DMA((n,)), pltpu.SemaphoreType.DMA((n,))]

    def _copies(self, ins, outs, sems):
        ici_send, ici_recv, _, _, own_send, own_recv = sems
        x, y, c, others = _place()
        me = 2 * x + y
        own = [pltpu.make_async_remote_copy(
            src_ref=ins[a], dst_ref=outs[a].at[me], send_sem=own_send.at[a], recv_sem=own_recv.at[a],
            device_id=(x, y, 1 - c), device_id_type=MESH) for a in range(self.n)]
        sends = [pltpu.make_async_remote_copy(
            src_ref=ins[a].at[c], dst_ref=outs[a].at[me, c], send_sem=ici_send.at[a * 3 + k], recv_sem=ici_recv.at[a * 3 + k],
            device_id=(ox, oy, c), device_id_type=MESH) for a in range(self.n) for k, (ox, oy) in enumerate(others)]
        return own, sends

    def start(self, ins, outs, sems):
        own, sends = self._copies(ins, outs, sems)
        for cp in own + sends:
            cp.start()

    def finish(self, ins, outs, sems):
        ici_send, ici_recv, d2d_send, d2d_recv, _, _ = sems
        x, y, c, others = _place()
        sibling = (x, y, 1 - c)
        own, sends = self._copies(ins, outs, sems)
        passes = []
        for a in range(self.n):
            for k, (ox, oy) in enumerate(others):
                s = a * 3 + k
                got = outs[a].at[2 * ox + oy, c]
                pltpu.make_async_remote_copy(
                    src_ref=got, dst_ref=got, send_sem=ici_send.at[s], recv_sem=ici_recv.at[s],
                    device_id=(ox, oy, c), device_id_type=MESH).wait_recv()
                fwd = pltpu.make_async_remote_copy(
                    src_ref=got, dst_ref=got, send_sem=d2d_send.at[s], recv_sem=d2d_recv.at[s],
                    device_id=sibling, device_id_type=MESH)
                fwd.start()
                passes.append(fwd)
        for a in range(self.n):
            for k, (ox, oy) in enumerate(others):
                s = a * 3 + k
                theirs = outs[a].at[2 * ox + oy, 1 - c]
                pltpu.make_async_remote_copy(
                    src_ref=theirs, dst_ref=theirs, send_sem=d2d_send.at[s], recv_sem=d2d_recv.at[s],
                    device_id=sibling, device_id_type=MESH).wait_recv()
        for cp in own:
            cp.wait()
        for cp in sends + passes:
            cp.wait_send()


class _PairRider:
    def __init__(self, grads):
        self.operands = list(grads)
        n = self.n = len(grads)
        self.out_shape = [jax.ShapeDtypeStruct((N_CHIPS,) + g.shape[2:], F32) for g in grads]
        self.scratch = [pltpu.SemaphoreType.DMA((N_CHIPS * n,)), pltpu.SemaphoreType.DMA((N_CHIPS * n,))]

    def _copies(self, ins, outs, sems):
        send_sem, recv_sem = sems
        x, y, c, _ = _place()
        return [pltpu.make_async_remote_copy(
            src_ref=ins[a].at[j, 1 - c], dst_ref=outs[a].at[j], send_sem=send_sem.at[a * N_CHIPS + j],
            recv_sem=recv_sem.at[a * N_CHIPS + j], device_id=(x, y, 1 - c), device_id_type=MESH)
            for a in range(self.n) for j in range(N_CHIPS)]

    def start(self, ins, outs, sems):
        for cp in self._copies(ins, outs, sems):
            cp.start()

    def finish(self, ins, outs, sems):
        for cp in self._copies(ins, outs, sems):
            cp.wait()


class _ExchangeRider:
    def __init__(self, parts):
        self.operands = list(parts)
        n = self.n = len(parts)
        self.out_shape = [jax.ShapeDtypeStruct(p.shape, p.dtype) for p in parts]
        self.scratch = [pltpu.SemaphoreType.DMA((3 * n,)), pltpu.SemaphoreType.DMA((3 * n,))]

    def _copies(self, ins, outs, sems):
        send_sem, recv_sem = sems
        x, y, c, others = _place()
        me = 2 * x + y
        return [pltpu.make_async_remote_copy(
            src_ref=ins[a].at[2 * ox + oy], dst_ref=outs[a].at[me], send_sem=send_sem.at[a * 3 + k],
            recv_sem=recv_sem.at[a * 3 + k], device_id=(ox, oy, c), device_id_type=MESH)
            for a in range(self.n) for k, (ox, oy) in enumerate(others)]

    def start(self, ins, outs, sems):
        for cp in self._copies(ins, outs, sems):
            cp.start()

    def finish(self, ins, outs, sems):
        send_sem, recv_sem = sems
        x, y, c, others = _place()
        for a in range(self.n):
            for k, (ox, oy) in enumerate(others):
                s = a * 3 + k
                got = outs[a].at[2 * ox + oy]
                pltpu.make_async_remote_copy(
                    src_ref=got, dst_ref=got, send_sem=send_sem.at[s], recv_sem=recv_sem.at[s],
                    device_id=(ox, oy, c), device_id_type=MESH).wait_recv()
        for cp in self._copies(ins, outs, sems):
            cp.wait_send()


def _run_rider(rider, name):
    n = rider.n

    def body(*refs):
        ins, outs, sems = refs[:n], refs[n:2 * n], refs[2 * n:]
        rider.start(ins, outs, sems)
        rider.finish(ins, outs, sems)

    return pl.pallas_call(
        body, name=name, in_specs=[ANY] * n, out_specs=[ANY] * n,
        out_shape=rider.out_shape, scratch_shapes=rider.scratch)(*rider.operands)


def _ride(rider, body, n_in, n_out, first, last):
    if rider is None:
        return body
    n = rider.n

    def wrapped(*refs):
        host_in, r_in = refs[:n_in], refs[n_in:n_in + n]
        host_out = refs[n_in + n:n_in + n + n_out]
        r_out = refs[n_in + n + n_out:n_in + 2 * n + n_out]
        rest = refs[n_in + 2 * n + n_out:]
        host_scr, sems = rest[:len(rest) - len(rider.scratch)], rest[len(rest) - len(rider.scratch):]

        @pl.when(first())
        def _():
            rider.start(r_in, r_out, sems)

        body(*host_in, *host_out, *host_scr)

        @pl.when(last())
        def _():
            rider.finish(r_in, r_out, sems)

    return wrapped


def _rider_args(rider):
    if rider is None:
        return [], [], [], [], []
    return rider.operands, [ANY] * rider.n, [ANY] * rider.n, rider.out_shape, rider.scratch


def _pair_sum(gs, gots, c_idx):
    n, parts = len(gs), 2

    def body(c_ref, *refs):
        g_refs, got_refs, pb_refs = refs[:n], refs[n:2 * n], refs[2 * n:]
        for k in range(n):
            pb_refs[k][...] = (g_refs[k][...] + got_refs[k][...]).astype(BF16)

    mine = [pl.BlockSpec((None, None, g.shape[2] // parts, g.shape[3]), lambda j, i, c_ref: (j, c_ref[0], i, 0))
            for g in gs]
    blk = [pl.BlockSpec((None, g.shape[2] // parts, g.shape[3]), lambda j, i, c_ref: (j, i, 0)) for g in gs]
    return pl.pallas_call(
        body, name="grad_pair_sum",
        grid_spec=pltpu.PrefetchScalarGridSpec(
            num_scalar_prefetch=1, grid=(N_CHIPS, parts), in_specs=mine + blk, out_specs=blk),
        out_shape=[jax.ShapeDtypeStruct((N_CHIPS,) + g.shape[2:], BF16) for g in gs],
        compiler_params=_params(("parallel", "parallel")))(c_idx, *gs, *gots)


def _chip_sum(gs, sibs, gots, place):
    n, parts = len(gs), 4

    def body(place_ref, *refs):
        g_refs, sib_refs, got_refs, t_refs = refs[:n], refs[n:2 * n], refs[2 * n:3 * n], refs[3 * n:]
        me = place_ref[0]
        for k in range(n):
            own = g_refs[k][...] + sib_refs[k][...]
            acc = None
            for s in range(N_CHIPS):
                term = jnp.where(me == s, own, got_refs[k][s].astype(F32))
                acc = term if acc is None else acc + term
            t_refs[k][...] = acc

    tiles = [(g.shape[2] // parts, g.shape[3]) for g in gs]
    return pl.pallas_call(
        body, name="grad_chip_sum",
        grid_spec=pltpu.PrefetchScalarGridSpec(
            num_scalar_prefetch=1, grid=(parts,),
            in_specs=[pl.BlockSpec((None, None) + t, lambda i, pr: (pr[0], pr[1], i, 0)) for t in tiles]
            + [pl.BlockSpec((None,) + t, lambda i, pr: (pr[0], i, 0)) for t in tiles]
            + [pl.BlockSpec((N_CHIPS,) + t, lambda i, pr: (0, i, 0)) for t in tiles],
            out_specs=[pl.BlockSpec(t, lambda i, pr: (i, 0)) for t in tiles]),
        out_shape=[jax.ShapeDtypeStruct(g.shape[2:], F32) for g in gs],
        compiler_params=_params(("parallel",)))(place, *gs, *sibs, *gots)


def _pair_share(halves):
    n = len(halves)

    def body(*refs):
        ins, outs = refs[:n], refs[n:2 * n]
        send_sem, recv_sem = refs[2 * n:]
        x, y, c, _ = _place()
        cps = [pltpu.make_async_remote_copy(
            src_ref=ins[a], dst_ref=outs[a], send_sem=send_sem.at[a], recv_sem=recv_sem.at[a],
            device_id=(x, y, 1 - c), device_id_type=MESH) for a in range(n)]
        for cp in cps:
            cp.start()
        for cp in cps:
            cp.wait()

    return pl.pallas_call(
        body, name="grad_pair_share",
        in_specs=[ANY] * n, out_specs=[ANY] * n,
        out_shape=[jax.ShapeDtypeStruct(h.shape, F32) for h in halves],
        scratch_shapes=[pltpu.SemaphoreType.DMA((n,)), pltpu.SemaphoreType.DMA((n,))],
        )(*halves)


def _small_allreduce(pack):
    m_per, ncol = pack.shape
    n_dev = 8

    def body(x_ref, sum_ref, all_ref, send_sems, recv_sems, local_sem):
        x, y, c, others = _place()
        me, sibling = (x, y, c), (x, y, 1 - c)

        def rows(px, py, pc):
            return all_ref.at[pl.ds((4 * px + 2 * py + pc) * m_per, m_per), :]

        def copy(k, block, to, src=None):
            return pltpu.make_async_remote_copy(
                src_ref=rows(*block) if src is None else src, dst_ref=rows(*block),
                send_sem=send_sems.at[k], recv_sem=recv_sems.at[k], device_id=to, device_id_type=MESH)

        mine = pltpu.make_async_copy(x_ref, rows(*me), local_sem)
        mine.start()
        first = [copy(0, me, sibling, src=x_ref)]
        first += [copy(1 + j, me, (*chip, c), src=x_ref) for j, chip in enumerate(others)]
        for cp in first:
            cp.start()
        passed = [copy(4 + j, (*chip, c), sibling) for j, chip in enumerate(others)]
        for j, chip in enumerate(others):
            copy(1 + j, (*chip, c), me).wait_recv()
            passed[j].start()
        copy(0, sibling, me).wait_recv()
        for j, chip in enumerate(others):
            copy(4 + j, (*chip, 1 - c), me).wait_recv()
        for cp in first + passed:
            cp.wait_send()
        mine.wait()
        acc = all_ref[0:m_per, :]
        for dvc in range(1, n_dev):
            acc = acc + all_ref[dvc * m_per:(dvc + 1) * m_per, :]
        sum_ref[...] = acc

    return pl.pallas_call(
        body, name="small_allreduce",
        in_specs=[pl.BlockSpec(memory_space=pltpu.VMEM)],
        out_specs=pl.BlockSpec(memory_space=pltpu.VMEM),
        out_shape=jax.ShapeDtypeStruct((m_per, ncol), F32),
        scratch_shapes=[pltpu.VMEM((n_dev * m_per, ncol), F32),
                        pltpu.SemaphoreType.DMA((7,)), pltpu.SemaphoreType.DMA((7,)), pltpu.SemaphoreType.DMA],
        )(pack)


def _adam_math(w, g, m, v):
    m = ADAM_B1 * m + (1.0 - ADAM_B1) * g
    v = ADAM_B2 * v + (1.0 - ADAM_B2) * (g * g)
    m_hat = m / (1.0 - ADAM_B1 ** ADAM_STEP)
    v_hat = v / (1.0 - ADAM_B2 ** ADAM_STEP)
    delta = -ADAM_LR * (m_hat / (jnp.sqrt(v_hat) + ADAM_EPS) + ADAM_WD * w)
    return delta, m, v


def _adamw(halves, c_idx, w, m, v, name):
    L = len(halves)
    r, C = halves[0][0].shape
    tr = _row_tile(r, C, 1024 * 1024)
    nt = r // tr

    def body(c_ref, *refs):
        g_refs, (w_ref, m_ref, v_ref), (g_ref, d_ref, nm_ref, nv_ref) = refs[:2 * L], refs[2 * L:2 * L + 3], refs[2 * L + 3:]
        own = pl.program_id(1) == c_ref[0]
        g = None
        for l in range(L):
            cand = jnp.where(own, g_refs[2 * l][...], g_refs[2 * l + 1][...])
            g = cand if g is None else jnp.where(pl.program_id(0) == l, cand, g)
        g_ref[...] = g
        d_ref[...], nm_ref[...], nv_ref[...] = _adam_math(w_ref[...], g, m_ref[...], v_ref[...])

    def half(l, mine):
        def index(ll, h, i, c_ref):
            read = (h == c_ref[0]) if mine else (h != c_ref[0])
            return jnp.where(jnp.logical_and(ll == l, read), i, 0), 0
        return pl.BlockSpec((tr, C), index)

    full = pl.BlockSpec((None, tr, C), lambda ll, h, i, c_ref: (ll, h * nt + i, 0))
    shp = jax.ShapeDtypeStruct((L, 2 * r, C), F32)
    g_specs = [half(l, mine) for l in range(L) for mine in (True, False)]
    return pl.pallas_call(
        body, name=name,
        grid_spec=pltpu.PrefetchScalarGridSpec(
            num_scalar_prefetch=1, grid=(L, 2, nt),
            in_specs=g_specs + [full] * 3, out_specs=[full] * 4),
        out_shape=[shp] * 4,
        compiler_params=_params(("arbitrary", "arbitrary", "arbitrary")))(
            c_idx, *[a for pair in halves for a in pair], w, m, v)


SMALL_ROW_SPANS = ((0, 2), (2, 4), (4, 7), (7, 8), (8, 9))


def _small_update(gsum, w, m, v):
    n = len(SMALL_ROW_SPANS)

    def body(gs_ref, *refs):
        w_refs, m_refs, v_refs = refs[:n], refs[n:2 * n], refs[2 * n:3 * n]
        g_refs, d_refs, nm_refs, nv_refs = [refs[(3 + k) * n:(4 + k) * n] for k in range(4)]
        lg_ref = w_refs[2]
        l0, l1, l2 = lg_ref[0:1, :], lg_ref[1:2, :], lg_ref[2:3, :]
        mx = jnp.maximum(jnp.maximum(l0, l1), l2)
        e0, e1, e2 = jnp.exp(l0 - mx), jnp.exp(l1 - mx), jnp.exp(l2 - mx)
        tot = e0 + e1 + e2
        p0, p1, p2 = e0 / tot, e1 / tot, e2 / tot
        dlb = gs_ref[4:5, :]
        for k, (r0, r1) in enumerate(SMALL_ROW_SPANS):
            if k == 2:
                g = jnp.concatenate([dlb * p0 * (1.0 - p0), -dlb * p0 * p1, -dlb * p0 * p2], axis=0)
            else:
                g = gs_ref[r0:r1, 0:w_refs[k].shape[1]]
            g_refs[k][...] = g
            d_refs[k][...], nm_refs[k][...], nv_refs[k][...] = _adam_math(w_refs[k][...], g, m_refs[k][...], v_refs[k][...])

    full = pl.BlockSpec(memory_space=pltpu.VMEM)
    shapes = [jax.ShapeDtypeStruct(a.shape, F32) for a in w]
    out = pl.pallas_call(
        body, name="small_update", in_specs=[full] * (1 + 3 * n), out_specs=[full] * (4 * n), out_shape=shapes * 4)(
            gsum, *w, *m, *v)
    return [out[k * n:(k + 1) * n] for k in range(4)]


def _pack_small(norm_mix, norm_ffn, lb3, out_norm, final_norm, extra=None):
    ncol = norm_mix.shape[1]
    on = jnp.pad(out_norm.reshape(1, -1), ((0, 0), (0, ncol - out_norm.size)))
    rows = [norm_mix, norm_ffn, lb3, on, final_norm.reshape(1, ncol)]
    if extra is not None:
        rows.append(extra)
    used = sum(r.shape[0] for r in rows)
    rows.append(jnp.zeros((SMALL_ROWS - used, ncol), F32))
    return jnp.concatenate(rows, axis=0)


WEIGHT_NAMES = ("hin", "hout", "qkv", "aout", "fin0", "fin1", "fdn0", "fdn1")
FIRST_WEIGHTS = ("hin",)
LATE_WEIGHTS_A = ("hout", "fin0", "fdn0")
LATE_WEIGHTS_B = ("qkv", "aout", "fin1", "fdn1")


def _split_weights(hgrn_w_in, hgrn_w_out, attn_w_qkv, attn_w_out, ffn_w_in, ffn_w_down):
    return {"hin": hgrn_w_in[0], "hout": hgrn_w_out[0], "qkv": attn_w_qkv[0], "aout": attn_w_out[0],
            "fin0": ffn_w_in[0], "fin1": ffn_w_in[1], "fdn0": ffn_w_down[0], "fdn1": ffn_w_down[1]}


def _halves(v):
    r, c = v.shape
    return v.reshape(2, r // 2, c)


def _full_weights(gathered):
    out = {}
    for k, g in gathered.items():
        _, _, r, c = g.shape
        if k in ("hin", "qkv", "fin0", "fin1"):
            out[k] = g.reshape(N_CHIPS, 2 * r, c)
        else:
            out[k] = g.reshape(N_CHIPS * 2 * r, c)
    return out


class _StepComm:
    def __init__(self, shards, c_idx, me_idx):
        self.shards, self.c_idx, self.me_idx = shards, c_idx, me_idx
        self.halves = {}
        self._stage = {}

    def gather_rider(self, names):
        return _GatherRider([_halves(self.shards[k].astype(BF16)) for k in names])

    def gathered(self, names, got):
        return _full_weights(dict(zip(names, got)))

    def first_weights(self):
        return self.gathered(FIRST_WEIGHTS, _run_rider(self.gather_rider(FIRST_WEIGHTS), "gather_first"))

    def pair_rider(self, grads, tag):
        names = list(grads)
        g4 = []
        for k in names:
            r, c = self.shards[k].shape
            g4.append(grads[k].reshape(N_CHIPS, 2, r // 2, c))
        self._stage[tag] = (names, g4)
        return _PairRider(g4)

    def pair_now(self, grads, tag):
        self.paired(tag, _run_rider(self.pair_rider(grads, tag), "grad_pair_exchange_" + tag))

    def paired(self, tag, got):
        names, g4 = self._stage[tag]
        self._stage[tag] = (names, list(zip(g4, got, _pair_sum(g4, list(got), self.c_idx))))

    def exchange_rider(self, tag):
        return _ExchangeRider([s[2] for s in self._stage[tag][1]])

    def exchanged(self, tag, got):
        names, sums = self._stage.pop(tag)
        place = jnp.concatenate([self.me_idx, self.c_idx])
        g4, sibs, _ = zip(*sums)
        self.halves.update(zip(names, _chip_sum(list(g4), list(sibs), list(got), place)))

    def shared_halves(self):
        mine = [self.halves[k] for k in WEIGHT_NAMES]
        return dict(zip(WEIGHT_NAMES, zip(mine, _pair_share(mine))))


def kernel(x, norm_mix, norm_ffn, hgrn_w_in, hgrn_lb_logits, hgrn_out_norm, hgrn_w_out, attn_w_qkv, attn_w_out, ffn_w_in, ffn_w_down, final_norm, loss_target, m_norm_mix, m_norm_ffn, m_hgrn_w_in, m_hgrn_lb_logits, m_hgrn_out_norm, m_hgrn_w_out, m_attn_w_qkv, m_attn_w_out, m_ffn_w_in, m_ffn_w_down, m_final_norm, v_norm_mix, v_norm_ffn, v_hgrn_w_in, v_hgrn_lb_logits, v_hgrn_out_norm, v_hgrn_w_out, v_attn_w_qkv, v_attn_w_out, v_ffn_w_in, v_ffn_w_down, v_final_norm):
    S = x.shape[1]
    xi, yi, ci = lax.axis_index("x"), lax.axis_index("y"), lax.axis_index("c")
    c_idx = jnp.reshape(ci, (1,)).astype(jnp.int32)
    me_idx = jnp.reshape(2 * xi + yi, (1,)).astype(jnp.int32)

    w_own = _split_weights(hgrn_w_in, hgrn_w_out, attn_w_qkv, attn_w_out, ffn_w_in, ffn_w_down)

    comm = _StepComm(w_own, c_idx, me_idx)
    loss, dx, small = _local_step(
        x.reshape(S, D_MODEL), loss_target.reshape(S, D_MODEL), norm_mix, norm_ffn, hgrn_lb_logits,
        hgrn_out_norm, final_norm.reshape(1, D_MODEL), comm)

    halves = comm.shared_halves()
    updated = {}
    for tensor, layers, (wt, mt, vt) in (
            ("hgrn_w_in", ("hin",), (hgrn_w_in, m_hgrn_w_in, v_hgrn_w_in)),
            ("hgrn_w_out", ("hout",), (hgrn_w_out, m_hgrn_w_out, v_hgrn_w_out)),
            ("attn_w_qkv", ("qkv",), (attn_w_qkv, m_attn_w_qkv, v_attn_w_qkv)),
            ("attn_w_out", ("aout",), (attn_w_out, m_attn_w_out, v_attn_w_out)),
            ("ffn_w_in", ("fin0", "fin1"), (ffn_w_in, m_ffn_w_in, v_ffn_w_in)),
            ("ffn_w_down", ("fdn0", "fdn1"), (ffn_w_down, m_ffn_w_down, v_ffn_w_down))):
        updated[tensor] = _adamw([halves[k] for k in layers], c_idx, wt, mt, vt, "adamw_" + tensor)

    loss_row = jnp.pad(loss, ((0, 0), (0, D_MODEL - loss.shape[1])))
    lb3 = jnp.concatenate([small["lb"], jnp.zeros((2, D_MODEL), F32)], axis=0)
    on_grad = jnp.sum(small["out_norm"], axis=0, keepdims=True)
    pack = _pack_small(small["norm_mix"], small["norm_ffn"], lb3, on_grad, small["final_norm"], loss_row)
    gsum = _small_allreduce(pack)
    fn2 = (1, D_MODEL)
    sg, sd, sm, sv = _small_update(
        gsum, (norm_mix, norm_ffn, hgrn_lb_logits, hgrn_out_norm, final_norm.reshape(fn2)),
        (m_norm_mix, m_norm_ffn, m_hgrn_lb_logits, m_hgrn_out_norm, m_final_norm.reshape(fn2)),
        (v_norm_mix, v_norm_ffn, v_hgrn_lb_logits, v_hgrn_out_norm, v_final_norm.reshape(fn2)))

    def assemble(p, which):
        nmx, nff, lbl, onm, fnm = p
        fnm = fnm.reshape(D_MODEL)
        hin, hout, qkv, aout, fin, fdn = [updated[t][which] for t in
                                          ("hgrn_w_in", "hgrn_w_out", "attn_w_qkv", "attn_w_out", "ffn_w_in", "ffn_w_down")]
        return (nmx, nff, hin, lbl, onm, hout, qkv, aout, fin, fdn, fnm)

    total_loss = gsum[9, 0]
    return (total_loss, dx.reshape(1, S, D_MODEL), *assemble(sg, 0), *assemble(sd, 1), *assemble(sm, 2), *assemble(sv, 3))
```

```python
import functools

import jax
import jax.numpy as jnp
import numpy as np
from jax import lax
from jax.experimental import pallas as pl
from jax.experimental.pallas import tpu as pltpu

F32 = jnp.float32
BF16 = jnp.bfloat16
MESH = pl.DeviceIdType.MESH

D_MODEL = 1024
HEAD = 128
HGRN_HEADS = 8
HGRN_CHUNK = 64
HGRN_HEADS_PER_STEP = 2
ATTN_GROUPS = ((128, 1), (512, 4), (2048, 16))
ATTN_SPAN = 128
HEADS_PER_GROUP = 4
GROUP_W = HEADS_PER_GROUP * HEAD
D_FF = 2816
NORM_EPS = 1e-6
ROPE_THETA = 10000.0
NEG = -1e30

ADAM_LR, ADAM_B1, ADAM_B2, ADAM_EPS, ADAM_WD, ADAM_STEP = 0.001, 0.9, 0.999, 1e-08, 0.01, 10

N_CHIPS = 4
VMEM_LIMIT = 56 * 1024 * 1024
SMALL_ROWS = 16


def _params(sem=None):
    return pltpu.CompilerParams(dimension_semantics=sem, vmem_limit_bytes=VMEM_LIMIT)


def _row_tile(rows, cols, budget_bytes=3 * 512 * 1024):
    best = 8
    for t in range(8, rows + 1, 8):
        if rows % t == 0 and t * cols * 4 <= budget_bytes:
            best = t
    assert rows % best == 0
    return best


def _grid_corner(i, j):
    return jnp.logical_and(pl.program_id(0) == i, pl.program_id(1) == j)


def _sigmoid(v):
    return 0.5 * jnp.tanh(0.5 * v) + 0.5


def _dot(a, b):
    return jnp.dot(a, b, preferred_element_type=F32)


def _dot_nt(a, b):
    return lax.dot_general(a, b, (((1,), (1,)), ((), ())), preferred_element_type=F32)


def _dot_tn(a, b):
    return lax.dot_general(a, b, (((0,), (0,)), ((), ())), preferred_element_type=F32)


def _dot_exact(ones, b):
    ones = ones.astype(BF16)
    hi = b.astype(BF16)
    rest = b - hi.astype(F32)
    mid = rest.astype(BF16)
    low = (rest - mid.astype(F32)).astype(BF16)
    return _dot(ones, hi) + _dot(ones, mid) + _dot(ones, low)


def _rstd(v):
    return lax.rsqrt(jnp.mean(v * v, axis=-1, keepdims=True) + NORM_EPS)


def _norm_mm(h, gain, w3, name, out_dtype=F32, tm=2048, rider=None):
    S, K = h.shape
    J, _, n = w3.shape
    gi = S // tm

    def body(h_ref, g_ref, w_ref, y_ref, u_ref):
        @pl.when(pl.program_id(1) == 0)
        def _():
            v = h_ref[...]
            u_ref[...] = (v * _rstd(v) * g_ref[...]).astype(BF16)

        y_ref[...] = _dot(u_ref[...], w_ref[pl.program_id(1)]).astype(y_ref.dtype)

    r_ops, r_in, r_out, r_shape, r_scr = _rider_args(rider)
    res = pl.pallas_call(
        _ride(rider, body, 3, 2, functools.partial(_grid_corner, 0, 0), functools.partial(_grid_corner, gi - 1, J - 1)),
        name=name, grid=(gi, J),
        in_specs=[pl.BlockSpec((tm, K), lambda i, j: (i, 0)),
                  pl.BlockSpec((1, K), lambda i, j: (0, 0)),
                  pl.BlockSpec((J, K, n), lambda i, j: (0, 0, 0), pipeline_mode=pl.Buffered(1))] + r_in,
        out_specs=[pl.BlockSpec((tm, n), lambda i, j: (i, j)), pl.BlockSpec((tm, K), lambda i, j: (i, 0))] + r_out,
        out_shape=[jax.ShapeDtypeStruct((S, J * n), out_dtype), jax.ShapeDtypeStruct((S, K), BF16)] + r_shape,
        scratch_shapes=r_scr,
        compiler_params=_params(("arbitrary", "arbitrary")))(h, gain, w3, *r_ops)
    return res[0], res[1], res[2:]


def _mm_res(h, a, w2, name, tm=1024):
    S, N = h.shape
    K = a.shape[1]

    def body(h_ref, a_ref, w_ref, o_ref):
        o_ref[...] = h_ref[...] + _dot(a_ref[...], w_ref[...])

    return pl.pallas_call(
        body, name=name, grid=(S // tm,),
        in_specs=[pl.BlockSpec((tm, N), lambda i: (i, 0)),
                  pl.BlockSpec((tm, K), lambda i: (i, 0)),
                  pl.BlockSpec((K, N), lambda i: (0, 0))],
        out_specs=pl.BlockSpec((tm, N), lambda i: (i, 0)),
        out_shape=jax.ShapeDtypeStruct((S, N), F32),
        compiler_params=_params(("parallel",)))(h, a, w2)


def _swiglu(z_ref, F):
    g = z_ref[:, :F].astype(F32)
    return (g * _sigmoid(g) * z_ref[:, F:].astype(F32)).astype(BF16)


def _swiglu_mm_res(h, z, w2, name, tm=512):
    S, N = h.shape
    F = w2.shape[0]

    def body(h_ref, z_ref, w_ref, o_ref, a_ref):
        a = _swiglu(z_ref, F)
        a_ref[...] = a
        o_ref[...] = h_ref[...] + _dot(a, w_ref[...])

    return pl.pallas_call(
        body, name=name, grid=(S // tm,),
        in_specs=[pl.BlockSpec((tm, N), lambda i: (i, 0)),
                  pl.BlockSpec((tm, 2 * F), lambda i: (i, 0)),
                  pl.BlockSpec((F, N), lambda i: (0, 0), pipeline_mode=pl.Buffered(1))],
        out_specs=[pl.BlockSpec((tm, N), lambda i: (i, 0)), pl.BlockSpec((tm, F), lambda i: (i, 0))],
        out_shape=[jax.ShapeDtypeStruct((S, N), F32), jax.ShapeDtypeStruct((S, F), BF16)],
        compiler_params=_params(("parallel",)))(h, z, w2)


def _dy_specs(dy, J, n, tm):
    if dy.ndim == 3:
        return [pl.BlockSpec((None, tm, n), functools.partial(lambda i, j: (j, i, 0), j=j)) for j in range(J)]
    return [pl.BlockSpec((tm, n), functools.partial(lambda i, j: (i, j), j=j)) for j in range(J)]


def _acc_nt(dy_refs, w_ref):
    acc = None
    for j, r in enumerate(dy_refs):
        t = _dot_nt(r[...].astype(BF16), w_ref[j])
        acc = t if acc is None else acc + t
    return acc


def _mm_nt(dy, w3, name, out_dtype=F32, tm=1024):
    J, K, n = w3.shape
    S = dy.shape[-2]

    def body(*refs):
        dy_refs, w_ref, o_ref = refs[:J], refs[J], refs[J + 1]
        o_ref[...] = _acc_nt(dy_refs, w_ref).astype(o_ref.dtype)

    return pl.pallas_call(
        body, name=name, grid=(S // tm,),
        in_specs=_dy_specs(dy, J, n, tm) + [pl.BlockSpec((J, K, n), lambda i: (0, 0, 0))],
        out_specs=pl.BlockSpec((tm, K), lambda i: (i, 0)),
        out_shape=jax.ShapeDtypeStruct((S, K), out_dtype),
        compiler_params=_params(("parallel",)))(*([dy] * J), w3)


def _mm_nt_normbwd(dy, w3, h, gain, dh, name, tm=512, rider=None):
    J, K, n = w3.shape
    S = h.shape[0]
    steps = S // tm

    def body(*refs):
        dy_refs, w_ref, h_ref, g_ref, dh_ref, o_ref, dg_ref = refs[:J], *refs[J:]
        du = _acc_nt(dy_refs, w_ref)
        v = h_ref[...]
        r = _rstd(v)
        xh = v * r
        dyg = du * g_ref[...]
        o_ref[...] = dh_ref[...] + r * (dyg - xh * jnp.mean(dyg * xh, axis=-1, keepdims=True))

        @pl.when(pl.program_id(0) == 0)
        def _():
            dg_ref[...] = jnp.zeros_like(dg_ref)

        dg_ref[...] += jnp.sum(du * xh, axis=0, keepdims=True)

    row = pl.BlockSpec((tm, K), lambda i: (i, 0))
    vec = pl.BlockSpec((1, K), lambda i: (0, 0))
    r_ops, r_in, r_out, r_shape, r_scr = _rider_args(rider)
    res = pl.pallas_call(
        _ride(rider, body, J + 4, 2, lambda: pl.program_id(0) == 0, lambda: pl.program_id(0) == steps - 1),
        name=name, grid=(steps,),
        in_specs=_dy_specs(dy, J, n, tm) + [pl.BlockSpec((J, K, n), lambda i: (0, 0, 0)), row, vec, row] + r_in,
        out_specs=[row, vec] + r_out,
        out_shape=[jax.ShapeDtypeStruct((S, K), F32), jax.ShapeDtypeStruct((1, K), F32)] + r_shape,
        scratch_shapes=r_scr,
        compiler_params=_params(("arbitrary",)))(*([dy] * J), w3, h, gain, dh, *r_ops)
    return res[0], res[1], res[2:]


def _mm_nt_swiglu_bwd(dh, w2, z, name, tm=512, chunks=11):
    F, N = w2.shape
    S = dh.shape[0]
    fc = F // chunks
    assert fc * chunks == F and fc % HEAD == 0

    def body(dh_ref, w_ref, z_ref, o_ref):
        dhb = dh_ref[...].astype(BF16)
        da = [_dot_nt(dhb, w_ref[c * fc:(c + 1) * fc, :]) for c in range(chunks)]
        for c in range(chunks):
            g = z_ref[:, c * fc:(c + 1) * fc].astype(F32)
            u = z_ref[:, F + c * fc:F + (c + 1) * fc].astype(F32)
            sg = _sigmoid(g)
            o_ref[:, c * fc:(c + 1) * fc] = (da[c] * u * (sg * (1.0 + g * (1.0 - sg)))).astype(BF16)
            o_ref[:, F + c * fc:F + (c + 1) * fc] = (da[c] * (g * sg)).astype(BF16)

    return pl.pallas_call(
        body, name=name, grid=(S // tm,),
        in_specs=[pl.BlockSpec((tm, N), lambda i: (i, 0)),
                  pl.BlockSpec((F, N), lambda i: (0, 0), pipeline_mode=pl.Buffered(1)),
                  pl.BlockSpec((tm, 2 * F), lambda i: (i, 0))],
        out_specs=pl.BlockSpec((tm, 2 * F), lambda i: (i, 0)),
        out_shape=jax.ShapeDtypeStruct((S, 2 * F), BF16),
        compiler_params=_params(("parallel",)))(dh, w2, z)


def _mm_tn(x, dy, J, n, tn, name):
    tpn = n // tn
    ts = 2048 if x.shape[1] <= 1536 else 1024
    S, K = x.shape
    if dy.ndim == 3:
        dy_spec = pl.BlockSpec((None, ts, tn), lambda c, s: (c // tpn, s, c % tpn))
    else:
        dy_spec = pl.BlockSpec((ts, tn), lambda c, s: (s, c))

    def body(x_ref, dy_ref, o_ref):
        @pl.when(pl.program_id(1) == 0)
        def _():
            o_ref[...] = jnp.zeros_like(o_ref)

        o_ref[...] += _dot_tn(x_ref[...], dy_ref[...].astype(BF16))

    return pl.pallas_call(
        body, name=name, grid=(J * tpn, S // ts),
        in_specs=[pl.BlockSpec((ts, K), lambda c, s: (s, 0)), dy_spec],
        out_specs=pl.BlockSpec((None, K, tn), lambda c, s: (c // tpn, 0, c % tpn)),
        out_shape=jax.ShapeDtypeStruct((J, K, n), F32),
        compiler_params=_params(("parallel", "arbitrary")))(x, dy)


def _loss_head(h, gain, target, tm=1024):
    S, K = h.shape

    def body(h_ref, g_ref, t_ref, dh_ref, loss_ref, dg_ref):
        v = h_ref[...]
        r = _rstd(v)
        xh = v * r
        g = g_ref[...]
        dy = (xh * g - t_ref[...]) * (1.0 / K)
        dyg = dy * g
        dh_ref[...] = r * (dyg - xh * jnp.mean(dyg * xh, axis=-1, keepdims=True))

        @pl.when(pl.program_id(0) == 0)
        def _():
            loss_ref[...] = jnp.zeros_like(loss_ref)
            dg_ref[...] = jnp.zeros_like(dg_ref)

        part = jnp.sum(jnp.sum(dy * dy, axis=-1, keepdims=True), axis=0, keepdims=True) * (0.5 * K)
        lane = lax.broadcasted_iota(jnp.int32, loss_ref.shape, 1)
        loss_ref[...] += jnp.where(lane == 0, part, 0.0)
        dg_ref[...] += jnp.sum(dy * xh, axis=0, keepdims=True)

    row = pl.BlockSpec((tm, K), lambda i: (i, 0))
    vec = pl.BlockSpec((1, K), lambda i: (0, 0))
    return pl.pallas_call(
        body, name="loss_head", grid=(S // tm,),
        in_specs=[row, vec, row],
        out_specs=[row, pl.BlockSpec((1, HEAD), lambda i: (0, 0)), vec],
        out_shape=[jax.ShapeDtypeStruct((S, K), F32), jax.ShapeDtypeStruct((1, HEAD), F32),
                   jax.ShapeDtypeStruct((1, K), F32)],
        compiler_params=_params(("arbitrary",)))(h, gain, target)


def _lower_bound(lg_ref):
    l0, l1, l2 = lg_ref[0:1, :], lg_ref[1:2, :], lg_ref[2:3, :]
    mx = jnp.maximum(jnp.maximum(l0, l1), l2)
    e0, e1, e2 = jnp.exp(l0 - mx), jnp.exp(l1 - mx), jnp.exp(l2 - mx)
    return e0 / (e0 + e1 + e2)


def _chunks(v, ncb):
    C = HGRN_CHUNK
    return [v[c * C:(c + 1) * C] for c in range(ncb)]


def _rows(parts):
    return jnp.concatenate(parts, axis=0)


def _block_gates(qz, fz, lb, ncb):
    C = HGRN_CHUNK
    row = lax.broadcasted_iota(jnp.int32, (C, C), 0)
    col = lax.broadcasted_iota(jnp.int32, (C, C), 1)
    tri = (col <= row).astype(F32)
    first_half = lax.broadcasted_iota(jnp.int32, (C, HEAD), 0) < C // 2
    sig = _sigmoid(fz)
    fg = lb + (1.0 - lb) * sig
    key = 1.0 - fg
    lg = jnp.log(fg)
    lgs = _chunks(lg, ncb)
    b = _rows([_dot_exact(tri, v) for v in lgs])
    r_c = [jnp.sum(jnp.where(first_half, v, 0.0), axis=0, keepdims=True) for v in lgs]
    bl_c = [jnp.sum(v, axis=0, keepdims=True) for v in lgs]
    r = _rows([jnp.broadcast_to(v, (C, HEAD)) for v in r_c])
    e_br, e_rb = jnp.exp(b - r), jnp.exp(r - b)
    e_b = e_br * _rows([jnp.broadcast_to(jnp.exp(v), (C, HEAD)) for v in r_c])
    e_lb = e_rb * _rows([jnp.broadcast_to(jnp.exp(e - v), (C, HEAD)) for e, v in zip(bl_c, r_c)])
    sq = _sigmoid(qz)
    qy = qz * sq
    return sig, fg, key, (e_br, e_rb, e_b, e_lb), bl_c, sq, qy


def _hgrn_fwd(proj, logits, gain, tb=1024, rider=None):
    S = proj.shape[0]
    H, C = HGRN_HEADS, HGRN_CHUNK
    ncb = tb // C

    def one_head(q_ref, f_ref, i_ref, g_ref, lg_ref, gn_ref, o_ref, og_ref, st_ref, state):
        @pl.when(pl.program_id(1) == 0)
        def _():
            state[...] = jnp.zeros_like(state)

        lb = _lower_bound(lg_ref)
        causal = lax.broadcasted_iota(jnp.int32, (C, C), 1) <= lax.broadcasted_iota(jnp.int32, (C, C), 0)
        qz, fz, gz = q_ref[...], f_ref[...], g_ref[...]
        _, _, key, (e_br, e_rb, e_b, e_lb), bl_c, _, qy = _block_gates(qz, fz, lb, ncb)
        qs = _chunks((qy * e_br).astype(BF16), ncb)
        ks = _chunks((key * e_rb).astype(BF16), ncb)
        qb = _chunks((qy * e_b).astype(BF16), ncb)
        ke = _chunks((key * e_lb).astype(BF16), ncb)
        vb = _chunks(i_ref[...].astype(BF16), ncb)
        a = [jnp.where(causal, _dot_nt(qs[c], ks[c]), 0.0).astype(BF16) for c in range(ncb)]
        upd = [_dot_tn(vb[c], ke[c]) for c in range(ncb)]
        o_intra = [_dot(a[c], vb[c]) for c in range(ncb)]
        st = state[...]
        e_l = [jnp.exp(v) for v in bl_c]
        sts = []
        for c in range(ncb):
            sts.append(st)
            st = st * e_l[c] + upd[c]
        state[...] = st
        for c in range(ncb):
            st_ref[c] = sts[c]
        o = _rows([_dot_nt(qb[c], sts[c].astype(BF16)) + o_intra[c] for c in range(ncb)])
        o_ref[...] = o
        og_ref[...] = ((o * _rstd(o) * gn_ref[...]) * (gz * _sigmoid(gz))).astype(BF16)

    def body(q_ref, f_ref, i_ref, g_ref, lg_ref, gn_ref, o_ref, og_ref, st_ref, state):
        for hs in range(HP):
            cols = slice(hs * HEAD, (hs + 1) * HEAD)
            one_head(q_ref.at[:, cols], f_ref.at[:, cols], i_ref.at[:, cols], g_ref.at[:, cols], lg_ref.at[:, cols],
                     gn_ref, o_ref.at[:, cols], og_ref.at[:, cols], st_ref.at[hs], state.at[hs])

    HP, wide = HGRN_HEADS_PER_STEP, HGRN_HEADS_PER_STEP * HEAD
    hg = H // HP

    def part(p):
        return pl.BlockSpec((tb, wide), functools.partial(lambda h, i, p: (i, p * hg + h), p=p))

    nb = S // tb
    r_ops, r_in, r_out, r_shape, r_scr = _rider_args(rider)
    res = pl.pallas_call(
        _ride(rider, body, 6, 3, functools.partial(_grid_corner, 0, 0), functools.partial(_grid_corner, hg - 1, nb - 1)),
        name="hgrn_fwd", grid=(hg, nb),
        in_specs=[part(0), part(1), part(2), part(3),
                  pl.BlockSpec((3, wide), lambda h, i: (0, h)),
                  pl.BlockSpec((1, HEAD), lambda h, i: (0, 0))] + r_in,
        out_specs=[pl.BlockSpec((tb, wide), lambda h, i: (i, h)),
                   pl.BlockSpec((tb, wide), lambda h, i: (i, h)),
                   pl.BlockSpec((HP, ncb, HEAD, HEAD), lambda h, i: (h, i, 0, 0))] + r_out,
        out_shape=[jax.ShapeDtypeStruct((S, H * HEAD), F32),
                   jax.ShapeDtypeStruct((S, H * HEAD), BF16),
                   jax.ShapeDtypeStruct((H, S // C, HEAD, HEAD), F32)] + r_shape,
        scratch_shapes=[pltpu.VMEM((HP, HEAD, HEAD), F32)] + r_scr,
        compiler_params=_params(("arbitrary", "arbitrary")))(proj, proj, proj, proj, logits, gain, *r_ops)
    return res[:3], res[3:]


def _hgrn_bwd(proj, logits, gain, o, states, dog, tb=1024, rider=None):
    S = proj.shape[0]
    H, C = HGRN_HEADS, HGRN_CHUNK
    ncb = tb // C
    nb = S // tb

    def one_head(q_ref, f_ref, i_ref, g_ref, lg_ref, gn_ref, o_ref, st_ref, dog_ref,
                 dp_ref, dlb_ref, dgn_ref, dstate, dst_scr):
        @pl.when(pl.program_id(1) == 0)
        def _():
            dstate[...] = jnp.zeros_like(dstate)
            dlb_ref[...] = jnp.zeros_like(dlb_ref)
            dgn_ref[...] = jnp.zeros_like(dgn_ref)

        lb = _lower_bound(lg_ref)
        oml = 1.0 - lb
        gn = gn_ref[...]
        row = lax.broadcasted_iota(jnp.int32, (C, C), 0)
        col = lax.broadcasted_iota(jnp.int32, (C, C), 1)
        causal = col <= row
        tri_up = (col >= row).astype(F32)
        qz, fz, gz = q_ref[...], f_ref[...], g_ref[...]
        sig, fg, key, (e_br, e_rb, e_b, e_lb), bl_c, sq, qy = _block_gates(qz, fz, lb, ncb)
        qs_v, ks_v = (qy * e_br).astype(BF16), (key * e_rb).astype(BF16)
        qb_v, ke_v = (qy * e_b).astype(BF16), (key * e_lb).astype(BF16)
        qs, ks, qb, ke = _chunks(qs_v, ncb), _chunks(ks_v, ncb), _chunks(qb_v, ncb), _chunks(ke_v, ncb)
        vb = _chunks(i_ref[...].astype(BF16), ncb)
        ov = o_ref[...]
        rs = _rstd(ov)
        xh = ov * rs
        sg = _sigmoid(gz)
        dog_v = dog_ref[...]
        dgz = dog_v * (xh * gn) * (sg * (1.0 + gz * (1.0 - sg)))
        don = dog_v * (gz * sg)
        dgn_ref[...] += jnp.sum(don * xh, axis=0, keepdims=True)
        dyg = don * gn
        do = rs * (dyg - xh * jnp.mean(dyg * xh, axis=-1, keepdims=True))
        dob = _chunks(do.astype(BF16), ncb)
        CH = range(ncb)
        a = [jnp.where(causal, _dot_nt(qs[c], ks[c]), 0.0).astype(BF16) for c in CH]
        da = [jnp.where(causal, _dot_nt(dob[c], vb[c]), 0.0).astype(BF16) for c in CH]
        wst = [_dot_tn(dob[c], qb[c]) for c in CH]
        dv_in = [_dot_tn(a[c], dob[c]) for c in CH]
        dqs = [_dot(da[c], ks[c]) for c in CH]
        dks = [_dot_tn(da[c], qs[c]) for c in CH]
        e_l = [jnp.exp(v) for v in bl_c]
        dst = dstate[...]
        for c in reversed(range(ncb)):
            dst_scr[c] = dst
            dst = wst[c] + dst * e_l[c]
        dstate[...] = dst
        dst1b = [dst_scr[c].astype(BF16) for c in CH]
        dqb = [_dot(dob[c], st_ref[c].astype(BF16)) for c in CH]
        dke = [_dot(vb[c], dst1b[c]) for c in CH]
        dv = [dv_in[c] + _dot_nt(ke[c], dst1b[c]) for c in CH]
        dbl_st = [jnp.sum(dst_scr[c] * st_ref[c], axis=0, keepdims=True) * e_l[c] for c in CH]
        dqs, dks, dqb, dke, dv = _rows(dqs), _rows(dks), _rows(dqb), _rows(dke), _rows(dv)
        dke_ke = dke * ke_v.astype(F32)
        db = dqs * qs_v.astype(F32) - dks * ks_v.astype(F32) + dqb * qb_v.astype(F32) - dke_ke
        dlg = []
        for c, (db_c, kk_c) in enumerate(zip(_chunks(db, ncb), _chunks(dke_ke, ncb))):
            dbl = jnp.sum(kk_c, axis=0, keepdims=True) + dbl_st[c]
            dlg.append(_dot_exact(tri_up, db_c) + dbl)
        dlg = _rows(dlg)
        dkey = dks * e_rb + dke * e_lb
        dqy = dqs * e_br + dqb * e_b
        dfg = dlg / fg - dkey
        dlb_ref[...] += jnp.sum(dfg * (1.0 - sig), axis=0, keepdims=True)
        dp_ref[0] = (dqy * (sq * (1.0 + qz * (1.0 - sq)))).astype(BF16)
        dp_ref[1] = (dfg * oml * sig * (1.0 - sig)).astype(BF16)
        dp_ref[2] = dv.astype(BF16)
        dp_ref[3] = dgz.astype(BF16)

    def body(q_ref, f_ref, i_ref, g_ref, lg_ref, gn_ref, o_ref, st_ref, dog_ref,
             dp_ref, dlb_ref, dgn_ref, dstate, dst_scr):
        for hs in range(HP):
            cols = slice(hs * HEAD, (hs + 1) * HEAD)
            one_head(q_ref.at[:, cols], f_ref.at[:, cols], i_ref.at[:, cols], g_ref.at[:, cols], lg_ref.at[:, cols],
                     gn_ref, o_ref.at[:, cols], st_ref.at[hs], dog_ref.at[:, cols],
                     dp_ref.at[:, :, cols], dlb_ref.at[hs], dgn_ref.at[hs], dstate.at[hs], dst_scr)

    HP, wide = HGRN_HEADS_PER_STEP, HGRN_HEADS_PER_STEP * HEAD
    hg = H // HP

    def part(p):
        return pl.BlockSpec((tb, wide), functools.partial(lambda h, i, p: (nb - 1 - i, p * hg + h), p=p))

    blk = pl.BlockSpec((tb, wide), lambda h, i: (nb - 1 - i, h))
    acc = pl.BlockSpec((HP, 1, HEAD), lambda h, i: (h, 0, 0))
    r_ops, r_in, r_out, r_shape, r_scr = _rider_args(rider)
    res = pl.pallas_call(
        _ride(rider, body, 9, 3, functools.partial(_grid_corner, 0, 0), functools.partial(_grid_corner, hg - 1, nb - 1)),
        name="hgrn_bwd", grid=(hg, nb),
        in_specs=[part(0), part(1), part(2), part(3),
                  pl.BlockSpec((3, wide), lambda h, i: (0, h)),
                  pl.BlockSpec((1, HEAD), lambda h, i: (0, 0)),
                  blk,
                  pl.BlockSpec((HP, ncb, HEAD, HEAD), lambda h, i: (h, nb - 1 - i, 0, 0)),
                  blk] + r_in,
        out_specs=[pl.BlockSpec((4, tb, wide), lambda h, i: (0, nb - 1 - i, h)), acc, acc] + r_out,
        out_shape=[jax.ShapeDtypeStruct((4, S, H * HEAD), BF16),
                   jax.ShapeDtypeStruct((H, 1, HEAD), F32),
                   jax.ShapeDtypeStruct((H, 1, HEAD), F32)] + r_shape,
        scratch_shapes=[pltpu.VMEM((HP, HEAD, HEAD), F32), pltpu.VMEM((ncb, HEAD, HEAD), F32)] + r_scr,
        compiler_params=_params(("arbitrary", "arbitrary")))(
            proj, proj, proj, proj, logits, gain, o, states, dog, *r_ops)
    return res[:3], res[3:]


def _rope(v, cos, sin):
    return v * cos + pltpu.roll(v, HEAD // 2, 1) * sin


def _lane_pick(tile, hh):
    lane = lax.broadcasted_iota(jnp.int32, tile.shape, 1)
    return jnp.sum(jnp.where(lane == hh, tile, 0.0), axis=-1, keepdims=True)


def _lane_place(cols):
    rows = cols[0].shape[0]
    lane = lax.broadcasted_iota(jnp.int32, (rows, HEAD), 1)
    tile = jnp.zeros((rows, HEAD), F32)
    for hh, v in enumerate(cols):
        tile = jnp.where(lane == hh, v, tile)
    return tile


def _band_masks():
    qi = lax.broadcasted_iota(jnp.int32, (ATTN_SPAN, ATTN_SPAN), 0)
    kj = lax.broadcasted_iota(jnp.int32, (ATTN_SPAN, ATTN_SPAN), 1)
    return kj <= qi, kj >= qi


ATTN_TILE_BLOCKS = 8


def _attn_fwd(a):
    d, L, _ = a.shape
    B, W = min(ATTN_TILE_BLOCKS, a.shape[1] // ATTN_SPAN), ATTN_SPAN
    T = B * W
    assert L % T == 0
    steps = L // T
    scale = HEAD ** -0.5

    def body(q_ref, kc_ref, kp_ref, vc_ref, vp_ref, o_ref, lse_ref):
        n = pl.program_id(1)
        mask_c, mask_p0 = _band_masks()
        first = jnp.logical_and(mask_p0, n > 0)
        units = [(b, hh) for b in range(B) for hh in range(HEADS_PER_GROUP)]
        rows = [slice(b * W, (b + 1) * W) for b in range(B)]
        cols = [slice(hh * HEAD, (hh + 1) * HEAD) for hh in range(HEADS_PER_GROUP)]

        def prev_keys(ref, tile, b, hh):
            return ref[:, cols[hh]] if b == 0 else tile[rows[b - 1], cols[hh]]

        s_c = [jnp.where(mask_c, _dot_nt(q_ref[rows[b], cols[hh]], kc_ref[rows[b], cols[hh]]) * scale, NEG) for b, hh in units]
        s_p = [jnp.where(first if b == 0 else mask_p0,
                         _dot_nt(q_ref[rows[b], cols[hh]], prev_keys(kp_ref, kc_ref, b, hh)) * scale, NEG) for b, hh in units]
        m = [jnp.maximum(jnp.max(x, axis=-1, keepdims=True), jnp.max(y, axis=-1, keepdims=True)) for x, y in zip(s_c, s_p)]
        p_c = [jnp.exp(x - mm) for x, mm in zip(s_c, m)]
        p_p = [jnp.exp(y - mm) for y, mm in zip(s_p, m)]
        l = [jnp.sum(x, axis=-1, keepdims=True) + jnp.sum(y, axis=-1, keepdims=True) for x, y in zip(p_c, p_p)]
        acc = [_dot(p_c[i].astype(BF16), vc_ref[rows[b], cols[hh]]) + _dot(p_p[i].astype(BF16), prev_keys(vp_ref, vc_ref, b, hh))
               for i, (b, hh) in enumerate(units)]
        for i, (b, hh) in enumerate(units):
            o_ref[rows[b], cols[hh]] = (acc[i] / l[i]).astype(BF16)
        for b in range(B):
            lse_ref[rows[b], :] = _lane_place([m[i] + jnp.log(l[i]) for i, (bb, _) in enumerate(units) if bb == b])

    def cur(part):
        return pl.BlockSpec((None, T, GROUP_W), functools.partial(lambda r, n, p: (r, n, p), p=part))

    def prev(part):
        return pl.BlockSpec((None, W, GROUP_W), functools.partial(lambda r, n, p: (r, jnp.maximum(n * B - 1, 0), p), p=part))

    return pl.pallas_call(
        body, name=f"attn_fwd_d{d}", grid=(d, steps),
        in_specs=[cur(0), cur(1), prev(1), cur(2), prev(2)],
        out_specs=[pl.BlockSpec((None, T, GROUP_W), lambda r, n: (r, n, 0)), pl.BlockSpec((None, T, HEAD), lambda r, n: (r, n, 0))],
        out_shape=[jax.ShapeDtypeStruct((d, L, GROUP_W), BF16), jax.ShapeDtypeStruct((d, L, HEAD), F32)],
        compiler_params=_params(("parallel", "arbitrary")))(a, a, a, a, a)


def _attn_bwd(a, do, lse, dd):
    d, L, _ = a.shape
    B, W = min(ATTN_TILE_BLOCKS, a.shape[1] // ATTN_SPAN), ATTN_SPAN
    T = B * W
    assert L % T == 0
    steps = L // T
    scale = HEAD ** -0.5

    def body(qc_ref, qn_ref, kp_ref, kc_ref, vp_ref, vc_ref, doc_ref, don_ref, lc_ref, ln_ref, ddc_ref, ddn_ref, da_ref):
        n = pl.program_id(1)
        mask_c, mask_p0 = _band_masks()
        first = jnp.logical_and(mask_p0, n > 0)
        last = jnp.logical_and(mask_p0, n < steps - 1)
        H4 = range(HEADS_PER_GROUP)
        units = [(b, hh) for b in range(B) for hh in H4]
        rows = [slice(b * W, (b + 1) * W) for b in range(B)]
        cols = [slice(hh * HEAD, (hh + 1) * HEAD) for hh in H4]
        q = {u: qc_ref[rows[u[0]], cols[u[1]]] for u in units}
        k = {u: kc_ref[rows[u[0]], cols[u[1]]] for u in units}
        v = {u: vc_ref[rows[u[0]], cols[u[1]]] for u in units}
        g_o = {u: doc_ref[rows[u[0]], cols[u[1]]] for u in units}
        kb = {(b, hh): kp_ref[:, cols[hh]] if b == 0 else k[(b - 1, hh)] for b, hh in units}
        vb = {(b, hh): vp_ref[:, cols[hh]] if b == 0 else v[(b - 1, hh)] for b, hh in units}
        lse_t = {(b, hh): _lane_pick(lc_ref[rows[b], :], hh) for b, hh in units}
        dd_t = {(b, hh): _lane_pick(ddc_ref[rows[b], :], hh) for b, hh in units}
        p_c = {u: jnp.where(mask_c, jnp.exp(_dot_nt(q[u], k[u]) * scale - lse_t[u]), 0.0) for u in units}
        p_p = {u: jnp.where(first if u[0] == 0 else mask_p0, jnp.exp(_dot_nt(q[u], kb[u]) * scale - lse_t[u]), 0.0) for u in units}
        ds_c = {u: (p_c[u] * (_dot_nt(g_o[u], v[u]) + dd_t[u])).astype(BF16) for u in units}
        ds_p = {u: (p_p[u] * (_dot_nt(g_o[u], vb[u]) + dd_t[u])).astype(BF16) for u in units}
        qn = [qn_ref[:, c] for c in cols]
        g_n = [don_ref[:, c] for c in cols]
        p_n = [jnp.where(last, jnp.exp(_dot_nt(qn[hh], k[(B - 1, hh)]) * scale - _lane_pick(ln_ref[...], hh)), 0.0) for hh in H4]
        ds_n = [(p_n[hh] * (_dot_nt(g_n[hh], v[(B - 1, hh)]) + _lane_pick(ddn_ref[...], hh))).astype(BF16) for hh in H4]
        dq = {u: (_dot(ds_c[u], k[u]) + _dot(ds_p[u], kb[u])) * scale for u in units}
        dk, dv = {}, {}
        for b, hh in units:
            if b < B - 1:
                nxt = (b + 1, hh)
                dk[(b, hh)] = (_dot_tn(ds_c[(b, hh)], q[(b, hh)]) + _dot_tn(ds_p[nxt], q[nxt])) * scale
                dv[(b, hh)] = _dot_tn(p_c[(b, hh)].astype(BF16), g_o[(b, hh)]) + _dot_tn(p_p[nxt].astype(BF16), g_o[nxt])
            else:
                dk[(b, hh)] = (_dot_tn(ds_c[(b, hh)], q[(b, hh)]) + _dot_tn(ds_n[hh], qn[hh])) * scale
                dv[(b, hh)] = _dot_tn(p_c[(b, hh)].astype(BF16), g_o[(b, hh)]) + _dot_tn(p_n[hh].astype(BF16), g_n[hh])
        for b, hh in units:
            da_ref[rows[b], cols[hh]] = dq[(b, hh)].astype(BF16)
            da_ref[rows[b], GROUP_W + hh * HEAD:GROUP_W + (hh + 1) * HEAD] = dk[(b, hh)].astype(BF16)
            da_ref[rows[b], 2 * GROUP_W + hh * HEAD:2 * GROUP_W + (hh + 1) * HEAD] = dv[(b, hh)].astype(BF16)

    nb = L // W

    def cur(width, part):
        return pl.BlockSpec((None, T, width), functools.partial(lambda r, n, p: (r, n, p), p=part))

    def prev(width, part):
        return pl.BlockSpec((None, W, width), functools.partial(lambda r, n, p: (r, jnp.maximum(n * B - 1, 0), p), p=part))

    def nxt(width, part):
        return pl.BlockSpec((None, W, width), functools.partial(lambda r, n, p: (r, jnp.minimum(n * B + B, nb - 1), p), p=part))

    g = GROUP_W
    return pl.pallas_call(
        body, name=f"attn_bwd_d{d}", grid=(d, steps),
        in_specs=[cur(g, 0), nxt(g, 0), prev(g, 1), cur(g, 1), prev(g, 2), cur(g, 2),
                  cur(g, 0), nxt(g, 0), cur(HEAD, 0), nxt(HEAD, 0), cur(HEAD, 0), nxt(HEAD, 0)],
        out_specs=pl.BlockSpec((None, T, 3 * g), lambda r, n: (r, n, 0)),
        out_shape=jax.ShapeDtypeStruct((d, L, 3 * g), BF16),
        compiler_params=_params(("parallel", "arbitrary")))(
            a, a, a, a, a, a, do, do, lse, lse, dd, dd)


def _softmax3(ls):
    mx = jnp.maximum(jnp.maximum(ls[0], ls[1]), ls[2])
    es = [jnp.exp(v - mx) for v in ls]
    tot = es[0] + es[1] + es[2]
    return [e / tot for e in es]


HEAD_COLS = [slice(hh * HEAD, (hh + 1) * HEAD) for hh in range(HEADS_PER_GROUP)]


def _group_spec(d, tm):
    return pl.BlockSpec((d, tm // d, GROUP_W), lambda i: (0, i, 0))


def _gather_heads(ref, scr, d, tm):
    if d == 1:
        return [ref[0, :, cols].astype(F32) for cols in HEAD_COLS]
    for hh, cols in enumerate(HEAD_COLS):
        for r in range(d):
            scr.at[hh][pl.ds(r, tm // d, stride=d), :] = ref[r, :, cols].astype(F32)
    return [scr[hh] for hh in range(HEADS_PER_GROUP)]


def _tile_spec(d, tm):
    return pl.BlockSpec((d, tm // d, HEAD), lambda i: (0, i, 0))


def _gather_tile(ref, scr, d, tm):
    if d == 1:
        return ref[0]
    for r in range(d):
        scr[pl.ds(r, tm // d, stride=d), :] = ref[r]
    return scr[...]


def _scatter_tile(val, scr, ref, d, tm):
    if d == 1:
        ref[0] = val
        return
    scr[...] = val
    for r in range(d):
        ref[r] = scr[pl.ds(r, tm // d, stride=d), :]


def _scatter_heads(vals, scr, ref, d, tm):
    if d == 1:
        for cols, v in zip(HEAD_COLS, vals):
            ref[0, :, cols] = v.astype(ref.dtype)
        return
    for hh, v in enumerate(vals):
        scr[hh] = v
    for hh, cols in enumerate(HEAD_COLS):
        for r in range(d):
            ref[r, :, cols] = scr.at[hh][pl.ds(r, tm // d, stride=d), :].astype(ref.dtype)


def _qkv_dilated(h, gain, w4, gi, cos, sin, d, tm=2048):
    S, K = h.shape
    n_shard = w4.shape[2]
    assert n_shard % HEAD == 0

    def head_cols(hh):
        def index(i, p):
            c = p * (len(ATTN_GROUPS) * GROUP_W) + gi * GROUP_W + hh * HEAD
            return c // n_shard, 0, (c % n_shard) // HEAD
        return pl.BlockSpec((None, K, HEAD), index)

    def body(h_ref, g_ref, *refs):
        w_refs, (cos_ref, sin_ref, out_ref, u_ref, y_scr) = refs[:HEADS_PER_GROUP], refs[HEADS_PER_GROUP:]
        p = pl.program_id(1)

        @pl.when(p == 0)
        def _():
            v = h_ref[...]
            u_ref[...] = (v * _rstd(v) * g_ref[...]).astype(BF16)

        y = _dot(u_ref[...], jnp.concatenate([r[...] for r in w_refs], axis=1))
        heads = [slice(hh * HEAD, (hh + 1) * HEAD) for hh in range(HEADS_PER_GROUP)]
        if d > 1:
            for hh, cols in enumerate(heads):
                y_scr[hh] = y[:, cols]

        def rows_of(hh, r):
            return y[:, heads[hh]] if d == 1 else y_scr.at[hh][pl.ds(r, tm // d, stride=d), :]

        @pl.when(p < 2)
        def _():
            for r in range(d):
                rows = slice(None) if d == 1 else pl.ds(r, tm // d, stride=d)
                cr, sr = cos_ref[rows, :], sin_ref[rows, :]
                for hh, cols in enumerate(heads):
                    out_ref[r, :, cols] = _rope(rows_of(hh, r), cr, sr).astype(BF16)

        @pl.when(p == 2)
        def _():
            for r in range(d):
                for hh, cols in enumerate(heads):
                    out_ref[r, :, cols] = rows_of(hh, r).astype(BF16)

    tab = pl.BlockSpec((tm, HEAD), lambda i, p: (i, 0))
    return pl.pallas_call(
        body, name=f"attn_qkv_d{d}", grid=(S // tm, 3),
        in_specs=[pl.BlockSpec((tm, K), lambda i, p: (i, 0)),
                  pl.BlockSpec((1, K), lambda i, p: (0, 0)),
                  *[head_cols(hh) for hh in range(HEADS_PER_GROUP)], tab, tab],
        out_specs=[pl.BlockSpec((d, tm // d, GROUP_W), lambda i, p: (0, i, p)), pl.BlockSpec((tm, K), lambda i, p: (i, 0))],
        out_shape=[jax.ShapeDtypeStruct((d, S // d, 3 * GROUP_W), BF16), jax.ShapeDtypeStruct((S, K), BF16)],
        scratch_shapes=[pltpu.VMEM((HEADS_PER_GROUP, tm, HEAD), F32)],
        compiler_params=_params(("parallel", "arbitrary")))(h, gain, *[w4] * HEADS_PER_GROUP, cos, sin)


def _undilate_group(da, dqkv, cos, sin, g, tm=2048):
    d, L, _ = da.shape
    S = d * L
    G = len(ATTN_GROUPS)

    def body(*refs):
        da_ref, cos_ref, sin_ref, out_ref, scr = refs[0], refs[1], refs[2], refs[-2], refs[-1]
        p = pl.program_id(1)
        heads = [slice(hh * HEAD, (hh + 1) * HEAD) for hh in range(HEADS_PER_GROUP)]
        if d > 1:
            for hh, cols in enumerate(heads):
                for r in range(d):
                    scr.at[hh][pl.ds(r, tm // d, stride=d), :] = da_ref[r, :, cols].astype(F32)

        def tokens(hh):
            return da_ref[0, :, heads[hh]].astype(F32) if d == 1 else scr[hh]

        @pl.when(p < 2)
        def _():
            cr, sr = cos_ref[...], -sin_ref[...]
            for hh, cols in enumerate(heads):
                out_ref[:, cols] = _rope(tokens(hh), cr, sr).astype(BF16)

        @pl.when(p == 2)
        def _():
            for hh, cols in enumerate(heads):
                out_ref[:, cols] = tokens(hh).astype(BF16)

    tab = pl.BlockSpec((tm, HEAD), lambda i, p: (i, 0))
    operands = (da, cos, sin) if dqkv is None else (da, cos, sin, dqkv)
    return pl.pallas_call(
        body, name=f"attn_undilate_d{d}", grid=(S // tm, 3),
        in_specs=[pl.BlockSpec((d, tm // d, GROUP_W), lambda i, p: (0, i, p)), tab, tab] + ([] if dqkv is None else [ANY]),
        out_specs=pl.BlockSpec((tm, GROUP_W), lambda i, p: (i, p * G + g)),
        out_shape=jax.ShapeDtypeStruct((S, 3 * G * GROUP_W), BF16),
        input_output_aliases={} if dqkv is None else {3: 0},
        scratch_shapes=[pltpu.VMEM((HEADS_PER_GROUP, tm, HEAD), F32)],
        compiler_params=_params(("parallel", "arbitrary")))(*operands)


def _attn_merge(os_, lses, h, w2, tm=1024):
    G = len(os_)
    S, N = h.shape

    def body(*refs):
        o_refs, l_refs, h_ref, w_ref, res_ref, out_ref = refs[:G], refs[G:2 * G], *refs[2 * G:2 * G + 4]
        scr = refs[2 * G + 4:]
        o = [_gather_heads(o_refs[g], scr[g], d, tm) for g, (_, d) in enumerate(ATTN_GROUPS)]
        l = [_gather_tile(l_refs[g], scr[G + g].at[0], d, tm) for g, (_, d) in enumerate(ATTN_GROUPS)]
        for hh in range(HEADS_PER_GROUP):
            al = _softmax3([_lane_pick(l[g], hh) for g in range(G)])
            for g in range(G):
                out_ref[:, g * GROUP_W + hh * HEAD:g * GROUP_W + (hh + 1) * HEAD] = (o[g][hh] * al[g]).astype(BF16)
        res_ref[...] = h_ref[...] + _dot(out_ref[...], w_ref[...])

    specs = [_group_spec(d, tm) for _, d in ATTN_GROUPS]
    row = pl.BlockSpec((tm, N), lambda i: (i, 0))
    return pl.pallas_call(
        body, name="attn_merge_out", grid=(S // tm,),
        in_specs=specs + [_tile_spec(d, tm) for _, d in ATTN_GROUPS] + [
            row, pl.BlockSpec((G * GROUP_W, N), lambda i: (0, 0), pipeline_mode=pl.Buffered(1))],
        out_specs=[row, pl.BlockSpec((tm, G * GROUP_W), lambda i: (i, 0))],
        out_shape=[jax.ShapeDtypeStruct((S, N), F32), jax.ShapeDtypeStruct((S, G * GROUP_W), BF16)],
        scratch_shapes=[pltpu.VMEM((HEADS_PER_GROUP, tm, HEAD), F32)] * (2 * G),
        compiler_params=_params(("parallel",)))(*os_, *lses, h, w2)


def _attn_merge_bwd(os_, lses, dh, w2, tm=512):
    G = len(os_)
    S, N = dh.shape

    def body(*refs):
        o_refs, l_refs, dh_ref, w_ref = refs[:G], refs[G:2 * G], refs[2 * G], refs[2 * G + 1]
        do_refs, dd_refs = refs[2 * G + 2:3 * G + 2], refs[3 * G + 2:4 * G + 2]
        scr = refs[4 * G + 2:]
        doa = _dot_nt(dh_ref[...].astype(BF16), w_ref[...])
        o = [_gather_heads(o_refs[g], scr[g], d, tm) for g, (_, d) in enumerate(ATTN_GROUPS)]
        l = [_gather_tile(l_refs[g], scr[G + g].at[0], d, tm) for g, (_, d) in enumerate(ATTN_GROUPS)]
        do = [[None] * HEADS_PER_GROUP for _ in range(G)]
        dd = [[None] * HEADS_PER_GROUP for _ in range(G)]
        for hh in range(HEADS_PER_GROUP):
            al = _softmax3([_lane_pick(l[g], hh) for g in range(G)])
            mix = None
            for g in range(G):
                dg = doa[:, g * GROUP_W + hh * HEAD:g * GROUP_W + (hh + 1) * HEAD]
                do[g][hh] = dg * al[g]
                t = al[g] * jnp.sum(dg * o[g][hh], axis=-1, keepdims=True)
                mix = t if mix is None else mix + t
            for g in range(G):
                dd[g][hh] = -al[g] * mix
        for g, (_, d) in enumerate(ATTN_GROUPS):
            _scatter_heads(do[g], scr[2 * G + g], do_refs[g], d, tm)
            _scatter_tile(_lane_place(dd[g]), scr[3 * G + g].at[0], dd_refs[g], d, tm)

    specs = [_group_spec(d, tm) for _, d in ATTN_GROUPS]
    tiles = [_tile_spec(d, tm) for _, d in ATTN_GROUPS]
    do_shapes = [jax.ShapeDtypeStruct((d, S // d, GROUP_W), BF16) for _, d in ATTN_GROUPS]
    dd_shapes = [jax.ShapeDtypeStruct((d, S // d, HEAD), F32) for _, d in ATTN_GROUPS]
    return pl.pallas_call(
        body, name="attn_merge_bwd", grid=(S // tm,),
        in_specs=specs + tiles + [pl.BlockSpec((tm, N), lambda i: (i, 0)),
                                  pl.BlockSpec((G * GROUP_W, N), lambda i: (0, 0), pipeline_mode=pl.Buffered(1))],
        out_specs=specs + tiles,
        out_shape=do_shapes + dd_shapes,
        scratch_shapes=[pltpu.VMEM((HEADS_PER_GROUP, tm, HEAD), F32)] * (4 * G),
        compiler_params=_params(("parallel",)))(*os_, *lses, dh, w2)


def _rope_tables(S):
    inv_freq = (1.0 / (np.float32(ROPE_THETA) ** (np.arange(0, HEAD, 2, dtype=np.float32) / np.float32(HEAD))))
    ang = (np.arange(S, dtype=np.float32)[:, None] * inv_freq.astype(np.float32)[None, :]).astype(np.float64)
    cos, sin = np.cos(ang).astype(np.float32), np.sin(ang).astype(np.float32)
    return jnp.asarray(np.concatenate([cos, cos], axis=-1)), jnp.asarray(np.concatenate([-sin, sin], axis=-1))


def _local_step(x, target, norm_mix, norm_ffn, lb_logits, out_gain, final_norm, comm):
    S = x.shape[0]
    nm0, nm1 = norm_mix[0:1], norm_mix[1:2]
    nf0, nf1 = norm_ffn[0:1], norm_ffn[1:2]
    w = comm.first_weights()

    proj, u0, got = _norm_mm(x, nm0, w["hin"], "hgrn_in", rider=comm.gather_rider(LATE_WEIGHTS_A))
    w.update(comm.gathered(LATE_WEIGHTS_A, got))
    (o, og, states), got = _hgrn_fwd(proj, lb_logits, out_gain, rider=comm.gather_rider(LATE_WEIGHTS_B))
    w.update(comm.gathered(LATE_WEIGHTS_B, got))
    fin_tn = w["fin0"].shape[2]
    h1 = _mm_res(x, og, w["hout"], "hgrn_out")
    z0, u1, _ = _norm_mm(h1, nf0, w["fin0"], "ffn0_in", out_dtype=BF16)
    h2, act0 = _swiglu_mm_res(h1, z0, w["fdn0"], "ffn0_down")
    cos, sin = _rope_tables(S)
    G = len(ATTN_GROUPS)
    a_g, u2 = zip(*[_qkv_dilated(h2, nm1, w["qkv"], gi, cos, sin, d) for gi, (_, d) in enumerate(ATTN_GROUPS)])
    o_g, lse_g = zip(*[_attn_fwd(a) for a in a_g])
    h3, oa = _attn_merge(o_g, lse_g, h2, w["aout"])
    z1, u3, _ = _norm_mm(h3, nf1, w["fin1"], "ffn1_in", out_dtype=BF16)
    h4, act1 = _swiglu_mm_res(h3, z1, w["fdn1"], "ffn1_down")
    dh4, loss, d_final = _loss_head(h4, final_norm, target)

    grads, small = {}, {"final_norm": d_final}

    def ffn_bwd(dh, h_in, u_in, z, act, gain, w_in, w_dn, tag, ride=None):
        dz = _mm_nt_swiglu_bwd(dh, w_dn, z, tag + "_down_dx")
        g_dn = _mm_tn(act, dh, 1, D_MODEL, D_MODEL, tag + "_down_dw")[0]
        g_in = _mm_tn(u_in, dz, N_CHIPS, fin_tn, fin_tn, tag + "_in_dw")
        rider = None if ride is None else ride(g_in, g_dn)
        dh_in, dgain, got = _mm_nt_normbwd(dz, w_in, h_in, gain, dh, tag + "_in_dx", rider=rider)
        return dh_in, dgain, g_in, g_dn, got

    dh3, d_nf1, grads["fin1"], grads["fdn1"], _ = ffn_bwd(dh4, h3, u3, z1, act1, nf1, w["fin1"], w["fdn1"], "ffn1")
    grads["aout"] = _mm_tn(oa, dh3, 1, D_MODEL, D_MODEL, "attn_out_dw")[0]
    merged = _attn_merge_bwd(o_g, lse_g, dh3, w["aout"])
    G = len(ATTN_GROUPS)
    das = [_attn_bwd(a_g[gi], merged[gi], lse_g[gi], merged[G + gi]) for gi in range(G)]
    dqkv = None
    for gi in range(G):
        dqkv = _undilate_group(das[gi], dqkv, cos, sin, gi)
    n_qkv = w["qkv"].shape[2]
    grads["qkv"] = _mm_tn(u2[0], dqkv, N_CHIPS, n_qkv, n_qkv, "attn_qkv_dw")
    dh2, d_nm1, _ = _mm_nt_normbwd(dqkv, w["qkv"], h2, nm1, dh3, "attn_qkv_dx")

    def ride_early(g_in, g_dn):
        return comm.pair_rider({**grads, "fin0": g_in, "fdn0": g_dn}, "early")

    dh1, d_nf0, _, _, got = ffn_bwd(dh2, h1, u1, z0, act0, nf0, w["fin0"], w["fdn0"], "ffn0", ride=ride_early)
    comm.paired("early", got)
    dog = _mm_nt(dh1, w["hout"][None], "hgrn_out_dx")
    (dproj, dlb, dgn), got = _hgrn_bwd(proj, lb_logits, out_gain, o, states, dog, rider=comm.exchange_rider("early"))
    comm.exchanged("early", got)
    late = {"hout": _mm_tn(og, dh1, 1, D_MODEL, D_MODEL, "hgrn_out_dw")[0],
            "hin": _mm_tn(u0, dproj, N_CHIPS, D_MODEL, D_MODEL, "hgrn_in_dw")}
    comm.pair_now(late, "late")
    dx, d_nm0, got = _mm_nt_normbwd(dproj, w["hin"], x, nm0, dh1, "hgrn_in_dx", rider=comm.exchange_rider("late"))
    comm.exchanged("late", got)

    small["norm_mix"] = jnp.concatenate([d_nm0, d_nm1], axis=0)
    small["norm_ffn"] = jnp.concatenate([d_nf0, d_nf1], axis=0)
    small["lb"] = dlb.reshape(1, HGRN_HEADS * HEAD)
    small["out_norm"] = dgn.reshape(HGRN_HEADS, HEAD)
    return loss, dx, small


def _place():
    x, y, c = lax.axis_index("x"), lax.axis_index("y"), lax.axis_index("c")
    others = [(1 - x, y), (x, 1 - y), (1 - x, 1 - y)]
    return x, y, c, others


ANY = pl.BlockSpec(memory_space=pl.ANY)


class _GatherRider:
    def __init__(self, shards):
        self.operands = list(shards)
        n = self.n = len(shards)
        self.out_shape = [jax.ShapeDtypeStruct((N_CHIPS,) + s.shape, s.dtype) for s in shards]
        self.scratch = [pltpu.SemaphoreType.DMA((3 * n,)), pltpu.SemaphoreType.DMA((3 * n,)),
                        pltpu.SemaphoreType.DMA((3 * n,)), pltpu.SemaphoreType.DMA((3 * n,)),
                        pltpu.SemaphoreType.DMA((n,)), pltpu.SemaphoreType.DMA((n,))]

    def _copies(self, ins, outs, sems):
        ici_send, ici_recv, _, _, own_send, own_recv = sems
        x, y, c, others = _place()
        me = 2 * x + y
        own = [pltpu.make_async_remote_copy(
            src_ref=ins[a], dst_ref=outs[a].at[me], send_sem=own_send.at[a], recv_sem=own_recv.at[a],
            device_id=(x, y, 1 - c), device_id_type=MESH) for a in range(self.n)]
        sends = [pltpu.make_async_remote_copy(
            src_ref=ins[a].at[c], dst_ref=outs[a].at[me, c], send_sem=ici_send.at[a * 3 + k], recv_sem=ici_recv.at[a * 3 + k],
            device_id=(ox, oy, c), device_id_type=MESH) for a in range(self.n) for k, (ox, oy) in enumerate(others)]
        return own, sends

    def start(self, ins, outs, sems):
        own, sends = self._copies(ins, outs, sems)
        for cp in own + sends:
            cp.start()

    def finish(self, ins, outs, sems):
        ici_send, ici_recv, d2d_send, d2d_recv, _, _ = sems
        x, y, c, others = _place()
        sibling = (x, y, 1 - c)
        own, sends = self._copies(ins, outs, sems)
        passes = []
        for a in range(self.n):
            for k, (ox, oy) in enumerate(others):
                s = a * 3 + k
                got = outs[a].at[2 * ox + oy, c]
                pltpu.make_async_remote_copy(
                    src_ref=got, dst_ref=got, send_sem=ici_send.at[s], recv_sem=ici_recv.at[s],
                    device_id=(ox, oy, c), device_id_type=MESH).wait_recv()
                fwd = pltpu.make_async_remote_copy(
                    src_ref=got, dst_ref=got, send_sem=d2d_send.at[s], recv_sem=d2d_recv.at[s],
                    device_id=sibling, device_id_type=MESH)
                fwd.start()
                passes.append(fwd)
        for a in range(self.n):
            for k, (ox, oy) in enumerate(others):
                s = a * 3 + k
                theirs = outs[a].at[2 * ox + oy, 1 - c]
                pltpu.make_async_remote_copy(
                    src_ref=theirs, dst_ref=theirs, send_sem=d2d_send.at[s], recv_sem=d2d_recv.at[s],
                    device_id=sibling, device_id_type=MESH).wait_recv()
        for cp in own:
            cp.wait()
        for cp in sends + passes:
            cp.wait_send()


class _PairRider:
    def __init__(self, grads):
        self.operands = list(grads)
        n = self.n = len(grads)
        self.out_shape = [jax.ShapeDtypeStruct((N_CHIPS,) + g.shape[2:], F32) for g in grads]
        self.scratch = [pltpu.SemaphoreType.DMA((N_CHIPS * n,)), pltpu.SemaphoreType.DMA((N_CHIPS * n,))]

    def _copies(self, ins, outs, sems):
        send_sem, recv_sem = sems
        x, y, c, _ = _place()
        return [pltpu.make_async_remote_copy(
            src_ref=ins[a].at[j, 1 - c], dst_ref=outs[a].at[j], send_sem=send_sem.at[a * N_CHIPS + j],
            recv_sem=recv_sem.at[a * N_CHIPS + j], device_id=(x, y, 1 - c), device_id_type=MESH)
            for a in range(self.n) for j in range(N_CHIPS)]

    def start(self, ins, outs, sems):
        for cp in self._copies(ins, outs, sems):
            cp.start()

    def finish(self, ins, outs, sems):
        for cp in self._copies(ins, outs, sems):
            cp.wait()


class _ExchangeRider:
    def __init__(self, parts):
        self.operands = list(parts)
        n = self.n = len(parts)
        self.out_shape = [jax.ShapeDtypeStruct(p.shape, p.dtype) for p in parts]
        self.scratch = [pltpu.SemaphoreType.DMA((3 * n,)), pltpu.SemaphoreType.DMA((3 * n,))]

    def _copies(self, ins, outs, sems):
        send_sem, recv_sem = sems
        x, y, c, others = _place()
        me = 2 * x + y
        return [pltpu.make_async_remote_copy(
            src_ref=ins[a].at[2 * ox + oy], dst_ref=outs[a].at[me], send_sem=send_sem.at[a * 3 + k],
            recv_sem=recv_sem.at[a * 3 + k], device_id=(ox, oy, c), device_id_type=MESH)
            for a in range(self.n) for k, (ox, oy) in enumerate(others)]

    def start(self, ins, outs, sems):
        for cp in self._copies(ins, outs, sems):
            cp.start()

    def finish(self, ins, outs, sems):
        send_sem, recv_sem = sems
        x, y, c, others = _place()
        for a in range(self.n):
            for k, (ox, oy) in enumerate(others):
                s = a * 3 + k
                got = outs[a].at[2 * ox + oy]
                pltpu.make_async_remote_copy(
                    src_ref=got, dst_ref=got, send_sem=send_sem.at[s], recv_sem=recv_sem.at[s],
                    device_id=(ox, oy, c), device_id_type=MESH).wait_recv()
        for cp in self._copies(ins, outs, sems):
            cp.wait_send()


def _run_rider(rider, name):
    n = rider.n

    def body(*refs):
        ins, outs, sems = refs[:n], refs[n:2 * n], refs[2 * n:]
        rider.start(ins, outs, sems)
        rider.finish(ins, outs, sems)

    return pl.pallas_call(
        body, name=name, in_specs=[ANY] * n, out_specs=[ANY] * n,
        out_shape=rider.out_shape, scratch_shapes=rider.scratch)(*rider.operands)


def _ride(rider, body, n_in, n_out, first, last):
    if rider is None:
        return body
    n = rider.n

    def wrapped(*refs):
        host_in, r_in = refs[:n_in], refs[n_in:n_in + n]
        host_out = refs[n_in + n:n_in + n + n_out]
        r_out = refs[n_in + n + n_out:n_in + 2 * n + n_out]
        rest = refs[n_in + 2 * n + n_out:]
        host_scr, sems = rest[:len(rest) - len(rider.scratch)], rest[len(rest) - len(rider.scratch):]

        @pl.when(first())
        def _():
            rider.start(r_in, r_out, sems)

        body(*host_in, *host_out, *host_scr)

        @pl.when(last())
        def _():
            rider.finish(r_in, r_out, sems)

    return wrapped


def _rider_args(rider):
    if rider is None:
        return [], [], [], [], []
    return rider.operands, [ANY] * rider.n, [ANY] * rider.n, rider.out_shape, rider.scratch


def _pair_sum(gs, gots, c_idx):
    n, parts = len(gs), 2

    def body(c_ref, *refs):
        g_refs, got_refs, pb_refs = refs[:n], refs[n:2 * n], refs[2 * n:]
        for k in range(n):
            pb_refs[k][...] = (g_refs[k][...] + got_refs[k][...]).astype(BF16)

    mine = [pl.BlockSpec((None, None, g.shape[2] // parts, g.shape[3]), lambda j, i, c_ref: (j, c_ref[0], i, 0))
            for g in gs]
    blk = [pl.BlockSpec((None, g.shape[2] // parts, g.shape[3]), lambda j, i, c_ref: (j, i, 0)) for g in gs]
    return pl.pallas_call(
        body, name="grad_pair_sum",
        grid_spec=pltpu.PrefetchScalarGridSpec(
            num_scalar_prefetch=1, grid=(N_CHIPS, parts), in_specs=mine + blk, out_specs=blk),
        out_shape=[jax.ShapeDtypeStruct((N_CHIPS,) + g.shape[2:], BF16) for g in gs],
        compiler_params=_params(("parallel", "parallel")))(c_idx, *gs, *gots)


def _chip_sum(gs, sibs, gots, place):
    n, parts = len(gs), 4

    def body(place_ref, *refs):
        g_refs, sib_refs, got_refs, t_refs = refs[:n], refs[n:2 * n], refs[2 * n:3 * n], refs[3 * n:]
        me = place_ref[0]
        for k in range(n):
            own = g_refs[k][...] + sib_refs[k][...]
            acc = None
            for s in range(N_CHIPS):
                term = jnp.where(me == s, own, got_refs[k][s].astype(F32))
                acc = term if acc is None else acc + term
            t_refs[k][...] = acc

    tiles = [(g.shape[2] // parts, g.shape[3]) for g in gs]
    return pl.pallas_call(
        body, name="grad_chip_sum",
        grid_spec=pltpu.PrefetchScalarGridSpec(
            num_scalar_prefetch=1, grid=(parts,),
            in_specs=[pl.BlockSpec((None, None) + t, lambda i, pr: (pr[0], pr[1], i, 0)) for t in tiles]
            + [pl.BlockSpec((None,) + t, lambda i, pr: (pr[0], i, 0)) for t in tiles]
            + [pl.BlockSpec((N_CHIPS,) + t, lambda i, pr: (0, i, 0)) for t in tiles],
            out_specs=[pl.BlockSpec(t, lambda i, pr: (i, 0)) for t in tiles]),
        out_shape=[jax.ShapeDtypeStruct(g.shape[2:], F32) for g in gs],
        compiler_params=_params(("parallel",)))(place, *gs, *sibs, *gots)


def _pair_share(halves):
    n = len(halves)

    def body(*refs):
        ins, outs = refs[:n], refs[n:2 * n]
        send_sem, recv_sem = refs[2 * n:]
        x, y, c, _ = _place()
        cps = [pltpu.make_async_remote_copy(
            src_ref=ins[a], dst_ref=outs[a], send_sem=send_sem.at[a], recv_sem=recv_sem.at[a],
            device_id=(x, y, 1 - c), device_id_type=MESH) for a in range(n)]
        for cp in cps:
            cp.start()
        for cp in cps:
            cp.wait()

    return pl.pallas_call(
        body, name="grad_pair_share",
        in_specs=[ANY] * n, out_specs=[ANY] * n,
        out_shape=[jax.ShapeDtypeStruct(h.shape, F32) for h in halves],
        scratch_shapes=[pltpu.SemaphoreType.DMA((n,)), pltpu.SemaphoreType.DMA((n,))],
        )(*halves)


def _small_allreduce(pack):
    m_per, ncol = pack.shape
    n_dev = 8

    def body(x_ref, sum_ref, all_ref, send_sems, recv_sems, local_sem):
        x, y, c, others = _place()
        me, sibling = (x, y, c), (x, y, 1 - c)

        def rows(px, py, pc):
            return all_ref.at[pl.ds((4 * px + 2 * py + pc) * m_per, m_per), :]

        def copy(k, block, to, src=None):
            return pltpu.make_async_remote_copy(
                src_ref=rows(*block) if src is None else src, dst_ref=rows(*block),
                send_sem=send_sems.at[k], recv_sem=recv_sems.at[k], device_id=to, device_id_type=MESH)

        mine = pltpu.make_async_copy(x_ref, rows(*me), local_sem)
        mine.start()
        first = [copy(0, me, sibling, src=x_ref)]
        first += [copy(1 + j, me, (*chip, c), src=x_ref) for j, chip in enumerate(others)]
        for cp in first:
            cp.start()
        passed = [copy(4 + j, (*chip, c), sibling) for j, chip in enumerate(others)]
        for j, chip in enumerate(others):
            copy(1 + j, (*chip, c), me).wait_recv()
            passed[j].start()
        copy(0, sibling, me).wait_recv()
        for j, chip in enumerate(others):
            copy(4 + j, (*chip, 1 - c), me).wait_recv()
        for cp in first + passed:
            cp.wait_send()
        mine.wait()
        acc = all_ref[0:m_per, :]
        for dvc in range(1, n_dev):
            acc = acc + all_ref[dvc * m_per:(dvc + 1) * m_per, :]
        sum_ref[...] = acc

    return pl.pallas_call(
        body, name="small_allreduce",
        in_specs=[pl.BlockSpec(memory_space=pltpu.VMEM)],
        out_specs=pl.BlockSpec(memory_space=pltpu.VMEM),
        out_shape=jax.ShapeDtypeStruct((m_per, ncol), F32),
        scratch_shapes=[pltpu.VMEM((n_dev * m_per, ncol), F32),
                        pltpu.SemaphoreType.DMA((7,)), pltpu.SemaphoreType.DMA((7,)), pltpu.SemaphoreType.DMA],
        )(pack)


def _adam_math(w, g, m, v):
    m = ADAM_B1 * m + (1.0 - ADAM_B1) * g
    v = ADAM_B2 * v + (1.0 - ADAM_B2) * (g * g)
    m_hat = m / (1.0 - ADAM_B1 ** ADAM_STEP)
    v_hat = v / (1.0 - ADAM_B2 ** ADAM_STEP)
    delta = -ADAM_LR * (m_hat / (jnp.sqrt(v_hat) + ADAM_EPS) + ADAM_WD * w)
    return delta, m, v


def _adamw(halves, c_idx, w, m, v, name):
    L = len(halves)
    r, C = halves[0][0].shape
    tr = _row_tile(r, C, 2 * 1024 * 1024)
    nt = r // tr

    def body(c_ref, *refs):
        g_refs, (w_ref, m_ref, v_ref), (g_ref, d_ref, nm_ref, nv_ref) = refs[:2 * L], refs[2 * L:2 * L + 3], refs[2 * L + 3:]
        own = pl.program_id(1) == c_ref[0]
        g = None
        for l in range(L):
            cand = jnp.where(own, g_refs[2 * l][...], g_refs[2 * l + 1][...])
            g = cand if g is None else jnp.where(pl.program_id(0) == l, cand, g)
        g_ref[...] = g
        d_ref[...], nm_ref[...], nv_ref[...] = _adam_math(w_ref[...], g, m_ref[...], v_ref[...])

    def half(l, mine):
        def index(ll, h, i, c_ref):
            read = (h == c_ref[0]) if mine else (h != c_ref[0])
            return jnp.where(jnp.logical_and(ll == l, read), i, 0), 0
        return pl.BlockSpec((tr, C), index)

    full = pl.BlockSpec((None, tr, C), lambda ll, h, i, c_ref: (ll, h * nt + i, 0))
    shp = jax.ShapeDtypeStruct((L, 2 * r, C), F32)
    g_specs = [half(l, mine) for l in range(L) for mine in (True, False)]
    return pl.pallas_call(
        body, name=name,
        grid_spec=pltpu.PrefetchScalarGridSpec(
            num_scalar_prefetch=1, grid=(L, 2, nt),
            in_specs=g_specs + [full] * 3, out_specs=[full] * 4),
        out_shape=[shp] * 4,
        compiler_params=_params(("arbitrary", "arbitrary", "arbitrary")))(
            c_idx, *[a for pair in halves for a in pair], w, m, v)


SMALL_ROW_SPANS = ((0, 2), (2, 4), (4, 7), (7, 8), (8, 9))


def _small_update(gsum, w, m, v):
    n = len(SMALL_ROW_SPANS)

    def body(gs_ref, *refs):
        w_refs, m_refs, v_refs = refs[:n], refs[n:2 * n], refs[2 * n:3 * n]
        g_refs, d_refs, nm_refs, nv_refs = [refs[(3 + k) * n:(4 + k) * n] for k in range(4)]
        lg_ref = w_refs[2]
        l0, l1, l2 = lg_ref[0:1, :], lg_ref[1:2, :], lg_ref[2:3, :]
        mx = jnp.maximum(jnp.maximum(l0, l1), l2)
        e0, e1, e2 = jnp.exp(l0 - mx), jnp.exp(l1 - mx), jnp.exp(l2 - mx)
        tot = e0 + e1 + e2
        p0, p1, p2 = e0 / tot, e1 / tot, e2 / tot
        dlb = gs_ref[4:5, :]
        for k, (r0, r1) in enumerate(SMALL_ROW_SPANS):
            if k == 2:
                g = jnp.concatenate([dlb * p0 * (1.0 - p0), -dlb * p0 * p1, -dlb * p0 * p2], axis=0)
            else:
                g = gs_ref[r0:r1, 0:w_refs[k].shape[1]]
            g_refs[k][...] = g
            d_refs[k][...], nm_refs[k][...], nv_refs[k][...] = _adam_math(w_refs[k][...], g, m_refs[k][...], v_refs[k][...])

    full = pl.BlockSpec(memory_space=pltpu.VMEM)
    shapes = [jax.ShapeDtypeStruct(a.shape, F32) for a in w]
    out = pl.pallas_call(
        body, name="small_update", in_specs=[full] * (1 + 3 * n), out_specs=[full] * (4 * n), out_shape=shapes * 4)(
            gsum, *w, *m, *v)
    return [out[k * n:(k + 1) * n] for k in range(4)]


def _pack_small(norm_mix, norm_ffn, lb3, out_norm, final_norm, extra=None):
    ncol = norm_mix.shape[1]
    on = jnp.pad(out_norm.reshape(1, -1), ((0, 0), (0, ncol - out_norm.size)))
    rows = [norm_mix, norm_ffn, lb3, on, final_norm.reshape(1, ncol)]
    if extra is not None:
        rows.append(extra)
    used = sum(r.shape[0] for r in rows)
    rows.append(jnp.zeros((SMALL_ROWS - used, ncol), F32))
    return jnp.concatenate(rows, axis=0)


WEIGHT_NAMES = ("hin", "hout", "qkv", "aout", "fin0", "fin1", "fdn0", "fdn1")
FIRST_WEIGHTS = ("hin",)
LATE_WEIGHTS_A = ("hout", "fin0", "fdn0")
LATE_WEIGHTS_B = ("qkv", "aout", "fin1", "fdn1")


def _split_weights(hgrn_w_in, hgrn_w_out, attn_w_qkv, attn_w_out, ffn_w_in, ffn_w_down):
    return {"hin": hgrn_w_in[0], "hout": hgrn_w_out[0], "qkv": attn_w_qkv[0], "aout": attn_w_out[0],
            "fin0": ffn_w_in[0], "fin1": ffn_w_in[1], "fdn0": ffn_w_down[0], "fdn1": ffn_w_down[1]}


def _halves(v):
    r, c = v.shape
    return v.reshape(2, r // 2, c)


def _full_weights(gathered):
    out = {}
    for k, g in gathered.items():
        _, _, r, c = g.shape
        if k in ("hin", "qkv", "fin0", "fin1"):
            out[k] = g.reshape(N_CHIPS, 2 * r, c)
        else:
            out[k] = g.reshape(N_CHIPS * 2 * r, c)
    return out


class _StepComm:
    def __init__(self, shards, c_idx, me_idx):
        self.shards, self.c_idx, self.me_idx = shards, c_idx, me_idx
        self.halves = {}
        self._stage = {}

    def gather_rider(self, names):
        return _GatherRider([_halves(self.shards[k].astype(BF16)) for k in names])

    def gathered(self, names, got):
        return _full_weights(dict(zip(names, got)))

    def first_weights(self):
        return self.gathered(FIRST_WEIGHTS, _run_rider(self.gather_rider(FIRST_WEIGHTS), "gather_first"))

    def pair_rider(self, grads, tag):
        names = list(grads)
        g4 = []
        for k in names:
            r, c = self.shards[k].shape
            g4.append(grads[k].reshape(N_CHIPS, 2, r // 2, c))
        self._stage[tag] = (names, g4)
        return _PairRider(g4)

    def pair_now(self, grads, tag):
        self.paired(tag, _run_rider(self.pair_rider(grads, tag), "grad_pair_exchange_" + tag))

    def paired(self, tag, got):
        names, g4 = self._stage[tag]
        self._stage[tag] = (names, list(zip(g4, got, _pair_sum(g4, list(got), self.c_idx))))

    def exchange_rider(self, tag):
        return _ExchangeRider([s[2] for s in self._stage[tag][1]])

    def exchanged(self, tag, got):
        names, sums = self._stage.pop(tag)
        place = jnp.concatenate([self.me_idx, self.c_idx])
        g4, sibs, _ = zip(*sums)
        self.halves.update(zip(names, _chip_sum(list(g4), list(sibs), list(got), place)))

    def shared_halves(self):
        mine = [self.halves[k] for k in WEIGHT_NAMES]
        return dict(zip(WEIGHT_NAMES, zip(mine, _pair_share(mine))))


def kernel(x, norm_mix, norm_ffn, hgrn_w_in, hgrn_lb_logits, hgrn_out_norm, hgrn_w_out, attn_w_qkv, attn_w_out, ffn_w_in, ffn_w_down, final_norm, loss_target, m_norm_mix, m_norm_ffn, m_hgrn_w_in, m_hgrn_lb_logits, m_hgrn_out_norm, m_hgrn_w_out, m_attn_w_qkv, m_attn_w_out, m_ffn_w_in, m_ffn_w_down, m_final_norm, v_norm_mix, v_norm_ffn, v_hgrn_w_in, v_hgrn_lb_logits, v_hgrn_out_norm, v_hgrn_w_out, v_attn_w_qkv, v_attn_w_out, v_ffn_w_in, v_ffn_w_down, v_final_norm):
    S = x.shape[1]
    xi, yi, ci = lax.axis_index("x"), lax.axis_index("y"), lax.axis_index("c")
    c_idx = jnp.reshape(ci, (1,)).astype(jnp.int32)
    me_idx = jnp.reshape(2 * xi + yi, (1,)).astype(jnp.int32)

    w_own = _split_weights(hgrn_w_in, hgrn_w_out, attn_w_qkv, attn_w_out, ffn_w_in, ffn_w_down)

    comm = _StepComm(w_own, c_idx, me_idx)
    loss, dx, small = _local_step(
        x.reshape(S, D_MODEL), loss_target.reshape(S, D_MODEL), norm_mix, norm_ffn, hgrn_lb_logits,
        hgrn_out_norm, final_norm.reshape(1, D_MODEL), comm)

    halves = comm.shared_halves()
    updated = {}
    for tensor, layers, (wt, mt, vt) in (
            ("hgrn_w_in", ("hin",), (hgrn_w_in, m_hgrn_w_in, v_hgrn_w_in)),
            ("hgrn_w_out", ("hout",), (hgrn_w_out, m_hgrn_w_out, v_hgrn_w_out)),
            ("attn_w_qkv", ("qkv",), (attn_w_qkv, m_attn_w_qkv, v_attn_w_qkv)),
            ("attn_w_out", ("aout",), (attn_w_out, m_attn_w_out, v_attn_w_out)),
            ("ffn_w_in", ("fin0", "fin1"), (ffn_w_in, m_ffn_w_in, v_ffn_w_in)),
            ("ffn_w_down", ("fdn0", "fdn1"), (ffn_w_down, m_ffn_w_down, v_ffn_w_down))):
        updated[tensor] = _adamw([halves[k] for k in layers], c_idx, wt, mt, vt, "adamw_" + tensor)

    loss_row = jnp.pad(loss, ((0, 0), (0, D_MODEL - loss.shape[1])))
    lb3 = jnp.concatenate([small["lb"], jnp.zeros((2, D_MODEL), F32)], axis=0)
    on_grad = jnp.sum(small["out_norm"], axis=0, keepdims=True)
    pack = _pack_small(small["norm_mix"], small["norm_ffn"], lb3, on_grad, small["final_norm"], loss_row)
    gsum = _small_allreduce(pack)
    fn2 = (1, D_MODEL)
    sg, sd, sm, sv = _small_update(
        gsum, (norm_mix, norm_ffn, hgrn_lb_logits, hgrn_out_norm, final_norm.reshape(fn2)),
        (m_norm_mix, m_norm_ffn, m_hgrn_lb_logits, m_hgrn_out_norm, m_final_norm.reshape(fn2)),
        (v_norm_mix, v_norm_ffn, v_hgrn_lb_logits, v_hgrn_out_norm, v_final_norm.reshape(fn2)))

    def assemble(p, which):
        nmx, nff, lbl, onm, fnm = p
        fnm = fnm.reshape(D_MODEL)
        hin, hout, qkv, aout, fin, fdn = [updated[t][which] for t in
                                          ("hgrn_w_in", "hgrn_w_out", "attn_w_qkv", "attn_w_out", "ffn_w_in", "ffn_w_down")]
        return (nmx, nff, hin, lbl, onm, hout, qkv, aout, fin, fdn, fnm)

    total_loss = gsum[9, 0]
    return (total_loss, dx.reshape(1, S, D_MODEL), *assemble(sg, 0), *assemble(sd, 1), *assemble(sm, 2), *assemble(sv, 3))
```

```python
import functools

import jax
import jax.numpy as jnp
import numpy as np
from jax import lax
from jax.experimental import pallas as pl
from jax.experimental.pallas import tpu as pltpu

F32 = jnp.float32
BF16 = jnp.bfloat16
MESH = pl.DeviceIdType.MESH

D_MODEL = 1024
HEAD = 128
HGRN_HEADS = 8
HGRN_CHUNK = 64
HGRN_HEADS_PER_STEP = 2
ATTN_GROUPS = ((128, 1), (512, 4), (2048, 16))
ATTN_SPAN = 128
HEADS_PER_GROUP = 4
GROUP_W = HEADS_PER_GROUP * HEAD
D_FF = 2816
NORM_EPS = 1e-6
ROPE_THETA = 10000.0
NEG = -1e30

ADAM_LR, ADAM_B1, ADAM_B2, ADAM_EPS, ADAM_WD, ADAM_STEP = 0.001, 0.9, 0.999, 1e-08, 0.01, 10

N_CHIPS = 4
VMEM_LIMIT = 56 * 1024 * 1024
SMALL_ROWS = 16


def _params(sem=None):
    return pltpu.CompilerParams(dimension_semantics=sem, vmem_limit_bytes=VMEM_LIMIT)


def _row_tile(rows, cols, budget_bytes=3 * 512 * 1024):
    best = 8
    for t in range(8, rows + 1, 8):
        if rows % t == 0 and t * cols * 4 <= budget_bytes:
            best = t
    assert rows % best == 0
    return best


def _grid_corner(i, j):
    return jnp.logical_and(pl.program_id(0) == i, pl.program_id(1) == j)


def _sigmoid(v):
    return 0.5 * jnp.tanh(0.5 * v) + 0.5


def _dot(a, b):
    return jnp.dot(a, b, preferred_element_type=F32)


def _dot_nt(a, b):
    return lax.dot_general(a, b, (((1,), (1,)), ((), ())), preferred_element_type=F32)


def _dot_tn(a, b):
    return lax.dot_general(a, b, (((0,), (0,)), ((), ())), preferred_element_type=F32)


def _dot_exact(ones, b):
    ones = ones.astype(BF16)
    hi = b.astype(BF16)
    rest = b - hi.astype(F32)
    mid = rest.astype(BF16)
    low = (rest - mid.astype(F32)).astype(BF16)
    return _dot(ones, hi) + _dot(ones, mid) + _dot(ones, low)


def _rstd(v):
    return lax.rsqrt(jnp.mean(v * v, axis=-1, keepdims=True) + NORM_EPS)


def _norm_mm(h, gain, w3, name, out_dtype=F32, tm=2048, rider=None):
    S, K = h.shape
    J, _, n = w3.shape
    gi = S // tm

    def body(h_ref, g_ref, w_ref, y_ref, u_ref):
        @pl.when(pl.program_id(1) == 0)
        def _():
            v = h_ref[...]
            u_ref[...] = (v * _rstd(v) * g_ref[...]).astype(BF16)

        y_ref[...] = _dot(u_ref[...], w_ref[pl.program_id(1)]).astype(y_ref.dtype)

    r_ops, r_in, r_out, r_shape, r_scr = _rider_args(rider)
    res = pl.pallas_call(
        _ride(rider, body, 3, 2, functools.partial(_grid_corner, 0, 0), functools.partial(_grid_corner, gi - 1, J - 1)),
        name=name, grid=(gi, J),
        in_specs=[pl.BlockSpec((tm, K), lambda i, j: (i, 0)),
                  pl.BlockSpec((1, K), lambda i, j: (0, 0)),
                  pl.BlockSpec((J, K, n), lambda i, j: (0, 0, 0), pipeline_mode=pl.Buffered(1))] + r_in,
        out_specs=[pl.BlockSpec((tm, n), lambda i, j: (i, j)), pl.BlockSpec((tm, K), lambda i, j: (i, 0))] + r_out,
        out_shape=[jax.ShapeDtypeStruct((S, J * n), out_dtype), jax.ShapeDtypeStruct((S, K), BF16)] + r_shape,
        scratch_shapes=r_scr,
        compiler_params=_params(("arbitrary", "arbitrary")))(h, gain, w3, *r_ops)
    return res[0], res[1], res[2:]


def _mm_res(h, a, w2, name, tm=1024):
    S, N = h.shape
    K = a.shape[1]

    def body(h_ref, a_ref, w_ref, o_ref):
        o_ref[...] = h_ref[...] + _dot(a_ref[...], w_ref[...])

    return pl.pallas_call(
        body, name=name, grid=(S // tm,),
        in_specs=[pl.BlockSpec((tm, N), lambda i: (i, 0)),
                  pl.BlockSpec((tm, K), lambda i: (i, 0)),
                  pl.BlockSpec((K, N), lambda i: (0, 0))],
        out_specs=pl.BlockSpec((tm, N), lambda i: (i, 0)),
        out_shape=jax.ShapeDtypeStruct((S, N), F32),
        compiler_params=_params(("parallel",)))(h, a, w2)


def _swiglu(z_ref, F):
    g = z_ref[:, :F].astype(F32)
    return (g * _sigmoid(g) * z_ref[:, F:].astype(F32)).astype(BF16)


def _swiglu_mm_res(h, z, w2, name, tm=512):
    S, N = h.shape
    F = w2.shape[0]
    steps, slots = S // tm, 3
    assert steps >= slots - 1

    def body(h_ref, z_hbm, w_ref, o_ref, a_ref, z_buf, sem):
        s = pl.program_id(0)

        def fetch(step):
            slot = step % slots
            return pltpu.make_async_copy(z_hbm.at[pl.ds(step * tm, tm), :], z_buf.at[slot], sem.at[slot])

        @pl.when(s == 0)
        def _():
            for k in range(slots - 1):
                fetch(k).start()

        @pl.when(s + slots - 1 < steps)
        def _():
            fetch(s + slots - 1).start()

        fetch(s).wait()
        a = _swiglu(z_buf.at[s % slots], F)
        a_ref[...] = a
        o_ref[...] = h_ref[...] + _dot(a, w_ref[...])

    return pl.pallas_call(
        body, name=name, grid=(steps,),
        in_specs=[pl.BlockSpec((tm, N), lambda i: (i, 0)), ANY,
                  pl.BlockSpec((F, N), lambda i: (0, 0), pipeline_mode=pl.Buffered(1))],
        out_specs=[pl.BlockSpec((tm, N), lambda i: (i, 0)), pl.BlockSpec((tm, F), lambda i: (i, 0))],
        out_shape=[jax.ShapeDtypeStruct((S, N), F32), jax.ShapeDtypeStruct((S, F), BF16)],
        scratch_shapes=[pltpu.VMEM((slots, tm, 2 * F), BF16), pltpu.SemaphoreType.DMA((slots,))],
        compiler_params=_params(("arbitrary",)))(h, z, w2)


def _dy_specs(dy, J, n, tm):
    if dy.ndim == 3:
        return [pl.BlockSpec((None, tm, n), functools.partial(lambda i, j: (j, i, 0), j=j)) for j in range(J)]
    return [pl.BlockSpec((tm, n), functools.partial(lambda i, j: (i, j), j=j)) for j in range(J)]


def _acc_nt(dy_refs, w_ref):
    acc = None
    for j, r in enumerate(dy_refs):
        t = _dot_nt(r[...].astype(BF16), w_ref[j])
        acc = t if acc is None else acc + t
    return acc


def _mm_nt(dy, w3, name, out_dtype=F32, tm=1024):
    J, K, n = w3.shape
    S = dy.shape[-2]

    def body(*refs):
        dy_refs, w_ref, o_ref = refs[:J], refs[J], refs[J + 1]
        o_ref[...] = _acc_nt(dy_refs, w_ref).astype(o_ref.dtype)

    return pl.pallas_call(
        body, name=name, grid=(S // tm,),
        in_specs=_dy_specs(dy, J, n, tm) + [pl.BlockSpec((J, K, n), lambda i: (0, 0, 0))],
        out_specs=pl.BlockSpec((tm, K), lambda i: (i, 0)),
        out_shape=jax.ShapeDtypeStruct((S, K), out_dtype),
        compiler_params=_params(("parallel",)))(*([dy] * J), w3)


def _mm_nt_normbwd(dy, w3, h, gain, dh, name, tm=512, rider=None):
    J, K, n = w3.shape
    S = h.shape[0]
    steps = S // tm

    def body(*refs):
        dy_refs, w_ref, h_ref, g_ref, dh_ref, o_ref, dg_ref = refs[:J], *refs[J:]
        du = _acc_nt(dy_refs, w_ref)
        v = h_ref[...]
        r = _rstd(v)
        xh = v * r
        dyg = du * g_ref[...]
        o_ref[...] = dh_ref[...] + r * (dyg - xh * jnp.mean(dyg * xh, axis=-1, keepdims=True))

        @pl.when(pl.program_id(0) == 0)
        def _():
            dg_ref[...] = jnp.zeros_like(dg_ref)

        dg_ref[...] += jnp.sum(du * xh, axis=0, keepdims=True)

    row = pl.BlockSpec((tm, K), lambda i: (i, 0))
    vec = pl.BlockSpec((1, K), lambda i: (0, 0))
    r_ops, r_in, r_out, r_shape, r_scr = _rider_args(rider)
    res = pl.pallas_call(
        _ride(rider, body, J + 4, 2, lambda: pl.program_id(0) == 0, lambda: pl.program_id(0) == steps - 1),
        name=name, grid=(steps,),
        in_specs=_dy_specs(dy, J, n, tm) + [pl.BlockSpec((J, K, n), lambda i: (0, 0, 0)), row, vec, row] + r_in,
        out_specs=[row, vec] + r_out,
        out_shape=[jax.ShapeDtypeStruct((S, K), F32), jax.ShapeDtypeStruct((1, K), F32)] + r_shape,
        scratch_shapes=r_scr,
        compiler_params=_params(("arbitrary",)))(*([dy] * J), w3, h, gain, dh, *r_ops)
    return res[0], res[1], res[2:]


def _mm_nt_swiglu_bwd(dh, w2, z, name, tm=512, chunks=11):
    F, N = w2.shape
    S = dh.shape[0]
    fc = F // chunks
    assert fc * chunks == F and fc % HEAD == 0

    def body(dh_ref, w_ref, z_ref, o_ref):
        dhb = dh_ref[...].astype(BF16)
        da = [_dot_nt(dhb, w_ref[c * fc:(c + 1) * fc, :]) for c in range(chunks)]
        for c in range(chunks):
            g = z_ref[:, c * fc:(c + 1) * fc].astype(F32)
            u = z_ref[:, F + c * fc:F + (c + 1) * fc].astype(F32)
            sg = _sigmoid(g)
            o_ref[:, c * fc:(c + 1) * fc] = (da[c] * u * (sg * (1.0 + g * (1.0 - sg)))).astype(BF16)
            o_ref[:, F + c * fc:F + (c + 1) * fc] = (da[c] * (g * sg)).astype(BF16)

    return pl.pallas_call(
        body, name=name, grid=(S // tm,),
        in_specs=[pl.BlockSpec((tm, N), lambda i: (i, 0)),
                  pl.BlockSpec((F, N), lambda i: (0, 0), pipeline_mode=pl.Buffered(1)),
                  pl.BlockSpec((tm, 2 * F), lambda i: (i, 0))],
        out_specs=pl.BlockSpec((tm, 2 * F), lambda i: (i, 0)),
        out_shape=jax.ShapeDtypeStruct((S, 2 * F), BF16),
        compiler_params=_params(("parallel",)))(dh, w2, z)


def _mm_tn(x, dy, J, n, tn, name):
    tpn = n // tn
    ts = 2048 if x.shape[1] <= 1536 else 1024
    S, K = x.shape
    if dy.ndim == 3:
        dy_spec = pl.BlockSpec((None, ts, tn), lambda c, s: (c // tpn, s, c % tpn))
    else:
        dy_spec = pl.BlockSpec((ts, tn), lambda c, s: (s, c))

    def body(x_ref, dy_ref, o_ref):
        @pl.when(pl.program_id(1) == 0)
        def _():
            o_ref[...] = jnp.zeros_like(o_ref)

        o_ref[...] += _dot_tn(x_ref[...], dy_ref[...].astype(BF16))

    return pl.pallas_call(
        body, name=name, grid=(J * tpn, S // ts),
        in_specs=[pl.BlockSpec((ts, K), lambda c, s: (s, 0)), dy_spec],
        out_specs=pl.BlockSpec((None, K, tn), lambda c, s: (c // tpn, 0, c % tpn)),
        out_shape=jax.ShapeDtypeStruct((J, K, n), F32),
        compiler_params=_params(("parallel", "arbitrary")))(x, dy)


def _loss_head(h, gain, target, tm=1024):
    S, K = h.shape

    def body(h_ref, g_ref, t_ref, dh_ref, loss_ref, dg_ref):
        v = h_ref[...]
        r = _rstd(v)
        xh = v * r
        g = g_ref[...]
        dy = (xh * g - t_ref[...]) * (1.0 / K)
        dyg = dy * g
        dh_ref[...] = r * (dyg - xh * jnp.mean(dyg * xh, axis=-1, keepdims=True))

        @pl.when(pl.program_id(0) == 0)
        def _():
            loss_ref[...] = jnp.zeros_like(loss_ref)
            dg_ref[...] = jnp.zeros_like(dg_ref)

        part = jnp.sum(jnp.sum(dy * dy, axis=-1, keepdims=True), axis=0, keepdims=True) * (0.5 * K)
        lane = lax.broadcasted_iota(jnp.int32, loss_ref.shape, 1)
        loss_ref[...] += jnp.where(lane == 0, part, 0.0)
        dg_ref[...] += jnp.sum(dy * xh, axis=0, keepdims=True)

    row = pl.BlockSpec((tm, K), lambda i: (i, 0))
    vec = pl.BlockSpec((1, K), lambda i: (0, 0))
    return pl.pallas_call(
        body, name="loss_head", grid=(S // tm,),
        in_specs=[row, vec, row],
        out_specs=[row, pl.BlockSpec((1, HEAD), lambda i: (0, 0)), vec],
        out_shape=[jax.ShapeDtypeStruct((S, K), F32), jax.ShapeDtypeStruct((1, HEAD), F32),
                   jax.ShapeDtypeStruct((1, K), F32)],
        compiler_params=_params(("arbitrary",)))(h, gain, target)


def _lower_bound(lg_ref):
    l0, l1, l2 = lg_ref[0:1, :], lg_ref[1:2, :], lg_ref[2:3, :]
    mx = jnp.maximum(jnp.maximum(l0, l1), l2)
    e0, e1, e2 = jnp.exp(l0 - mx), jnp.exp(l1 - mx), jnp.exp(l2 - mx)
    return e0 / (e0 + e1 + e2)


def _chunks(v, ncb):
    C = HGRN_CHUNK
    return [v[c * C:(c + 1) * C] for c in range(ncb)]


def _rows(parts):
    return jnp.concatenate(parts, axis=0)


def _block_gates(qz, fz, lb, ncb):
    C = HGRN_CHUNK
    row = lax.broadcasted_iota(jnp.int32, (C, C), 0)
    col = lax.broadcasted_iota(jnp.int32, (C, C), 1)
    tri = (col <= row).astype(F32)
    first_half = lax.broadcasted_iota(jnp.int32, (C, HEAD), 0) < C // 2
    sig = _sigmoid(fz)
    fg = lb + (1.0 - lb) * sig
    key = 1.0 - fg
    lg = jnp.log(fg)
    lgs = _chunks(lg, ncb)
    b = _rows([_dot_exact(tri, v) for v in lgs])
    r_c = [jnp.sum(jnp.where(first_half, v, 0.0), axis=0, keepdims=True) for v in lgs]
    bl_c = [jnp.sum(v, axis=0, keepdims=True) for v in lgs]
    r = _rows([jnp.broadcast_to(v, (C, HEAD)) for v in r_c])
    e_br, e_rb = jnp.exp(b - r), jnp.exp(r - b)
    e_b = e_br * _rows([jnp.broadcast_to(jnp.exp(v), (C, HEAD)) for v in r_c])
    e_lb = e_rb * _rows([jnp.broadcast_to(jnp.exp(e - v), (C, HEAD)) for e, v in zip(bl_c, r_c)])
    sq = _sigmoid(qz)
    qy = qz * sq
    return sig, fg, key, (e_br, e_rb, e_b, e_lb), bl_c, sq, qy


def _hgrn_fwd(proj, logits, gain, tb=1024, rider=None):
    S = proj.shape[0]
    H, C = HGRN_HEADS, HGRN_CHUNK
    ncb = tb // C

    def one_head(q_ref, f_ref, i_ref, g_ref, lg_ref, gn_ref, o_ref, og_ref, st_ref, state):
        @pl.when(pl.program_id(1) == 0)
        def _():
            state[...] = jnp.zeros_like(state)

        lb = _lower_bound(lg_ref)
        causal = lax.broadcasted_iota(jnp.int32, (C, C), 1) <= lax.broadcasted_iota(jnp.int32, (C, C), 0)
        qz, fz, gz = q_ref[...], f_ref[...], g_ref[...]
        _, _, key, (e_br, e_rb, e_b, e_lb), bl_c, _, qy = _block_gates(qz, fz, lb, ncb)
        qs = _chunks((qy * e_br).astype(BF16), ncb)
        ks = _chunks((key * e_rb).astype(BF16), ncb)
        qb = _chunks((qy * e_b).astype(BF16), ncb)
        ke = _chunks((key * e_lb).astype(BF16), ncb)
        vb = _chunks(i_ref[...].astype(BF16), ncb)
        a = [jnp.where(causal, _dot_nt(qs[c], ks[c]), 0.0).astype(BF16) for c in range(ncb)]
        upd = [_dot_tn(vb[c], ke[c]) for c in range(ncb)]
        o_intra = [_dot(a[c], vb[c]) for c in range(ncb)]
        st = state[...]
        e_l = [jnp.exp(v) for v in bl_c]
        sts = []
        for c in range(ncb):
            sts.append(st)
            st = st * e_l[c] + upd[c]
        state[...] = st
        for c in range(ncb):
            st_ref[c] = sts[c]
        o = _rows([_dot_nt(qb[c], sts[c].astype(BF16)) + o_intra[c] for c in range(ncb)])
        o_ref[...] = o
        og_ref[...] = ((o * _rstd(o) * gn_ref[...]) * (gz * _sigmoid(gz))).astype(BF16)

    def body(q_ref, f_ref, i_ref, g_ref, lg_ref, gn_ref, o_ref, og_ref, st_ref, state):
        for hs in range(HP):
            cols = slice(hs * HEAD, (hs + 1) * HEAD)
            one_head(q_ref.at[:, cols], f_ref.at[:, cols], i_ref.at[:, cols], g_ref.at[:, cols], lg_ref.at[:, cols],
                     gn_ref, o_ref.at[:, cols], og_ref.at[:, cols], st_ref.at[hs], state.at[hs])

    HP, wide = HGRN_HEADS_PER_STEP, HGRN_HEADS_PER_STEP * HEAD
    hg = H // HP

    def part(p):
        return pl.BlockSpec((tb, wide), functools.partial(lambda h, i, p: (i, p * hg + h), p=p))

    nb = S // tb
    r_ops, r_in, r_out, r_shape, r_scr = _rider_args(rider)
    res = pl.pallas_call(
        _ride(rider, body, 6, 3, functools.partial(_grid_corner, 0, 0), functools.partial(_grid_corner, hg - 1, nb - 1)),
        name="hgrn_fwd", grid=(hg, nb),
        in_specs=[part(0), part(1), part(2), part(3),
                  pl.BlockSpec((3, wide), lambda h, i: (0, h)),
                  pl.BlockSpec((1, HEAD), lambda h, i: (0, 0))] + r_in,
        out_specs=[pl.BlockSpec((tb, wide), lambda h, i: (i, h)),
                   pl.BlockSpec((tb, wide), lambda h, i: (i, h)),
                   pl.BlockSpec((HP, ncb, HEAD, HEAD), lambda h, i: (h, i, 0, 0))] + r_out,
        out_shape=[jax.ShapeDtypeStruct((S, H * HEAD), F32),
                   jax.ShapeDtypeStruct((S, H * HEAD), BF16),
                   jax.ShapeDtypeStruct((H, S // C, HEAD, HEAD), F32)] + r_shape,
        scratch_shapes=[pltpu.VMEM((HP, HEAD, HEAD), F32)] + r_scr,
        compiler_params=_params(("arbitrary", "arbitrary")))(proj, proj, proj, proj, logits, gain, *r_ops)
    return res[:3], res[3:]


def _hgrn_bwd(proj, logits, gain, o, states, dog, tb=1024, rider=None):
    S = proj.shape[0]
    H, C = HGRN_HEADS, HGRN_CHUNK
    ncb = tb // C
    nb = S // tb

    def one_head(q_ref, f_ref, i_ref, g_ref, lg_ref, gn_ref, o_ref, st_ref, dog_ref,
                 dp_ref, dlb_ref, dgn_ref, dstate, dst_scr):
        @pl.when(pl.program_id(1) == 0)
        def _():
            dstate[...] = jnp.zeros_like(dstate)
            dlb_ref[...] = jnp.zeros_like(dlb_ref)
            dgn_ref[...] = jnp.zeros_like(dgn_ref)

        lb = _lower_bound(lg_ref)
        oml = 1.0 - lb
        gn = gn_ref[...]
        row = lax.broadcasted_iota(jnp.int32, (C, C), 0)
        col = lax.broadcasted_iota(jnp.int32, (C, C), 1)
        causal = col <= row
        tri_up = (col >= row).astype(F32)
        qz, fz, gz = q_ref[...], f_ref[...], g_ref[...]
        sig, fg, key, (e_br, e_rb, e_b, e_lb), bl_c, sq, qy = _block_gates(qz, fz, lb, ncb)
        qs_v, ks_v = (qy * e_br).astype(BF16), (key * e_rb).astype(BF16)
        qb_v, ke_v = (qy * e_b).astype(BF16), (key * e_lb).astype(BF16)
        qs, ks, qb, ke = _chunks(qs_v, ncb), _chunks(ks_v, ncb), _chunks(qb_v, ncb), _chunks(ke_v, ncb)
        vb = _chunks(i_ref[...].astype(BF16), ncb)
        ov = o_ref[...]
        rs = _rstd(ov)
        xh = ov * rs
        sg = _sigmoid(gz)
        dog_v = dog_ref[...]
        dgz = dog_v * (xh * gn) * (sg * (1.0 + gz * (1.0 - sg)))
        don = dog_v * (gz * sg)
        dgn_ref[...] += jnp.sum(don * xh, axis=0, keepdims=True)
        dyg = don * gn
        do = rs * (dyg - xh * jnp.mean(dyg * xh, axis=-1, keepdims=True))
        dob = _chunks(do.astype(BF16), ncb)
        CH = range(ncb)
        a = [jnp.where(causal, _dot_nt(qs[c], ks[c]), 0.0).astype(BF16) for c in CH]
        da = [jnp.where(causal, _dot_nt(dob[c], vb[c]), 0.0).astype(BF16) for c in CH]
        wst = [_dot_tn(dob[c], qb[c]) for c in CH]
        dv_in = [_dot_tn(a[c], dob[c]) for c in CH]
        dqs = [_dot(da[c], ks[c]) for c in CH]
        dks = [_dot_tn(da[c], qs[c]) for c in CH]
        e_l = [jnp.exp(v) for v in bl_c]
        dst = dstate[...]
        for c in reversed(range(ncb)):
            dst_scr[c] = dst
            dst = wst[c] + dst * e_l[c]
        dstate[...] = dst
        dst1b = [dst_scr[c].astype(BF16) for c in CH]
        dqb = [_dot(dob[c], st_ref[c].astype(BF16)) for c in CH]
        dke = [_dot(vb[c], dst1b[c]) for c in CH]
        dv = [dv_in[c] + _dot_nt(ke[c], dst1b[c]) for c in CH]
        dbl_st = [jnp.sum(dst_scr[c] * st_ref[c], axis=0, keepdims=True) * e_l[c] for c in CH]
        dqs, dks, dqb, dke, dv = _rows(dqs), _rows(dks), _rows(dqb), _rows(dke), _rows(dv)
        dke_ke = dke * ke_v.astype(F32)
        db = dqs * qs_v.astype(F32) - dks * ks_v.astype(F32) + dqb * qb_v.astype(F32) - dke_ke
        dlg = []
        for c, (db_c, kk_c) in enumerate(zip(_chunks(db, ncb), _chunks(dke_ke, ncb))):
            dbl = jnp.sum(kk_c, axis=0, keepdims=True) + dbl_st[c]
            dlg.append(_dot_exact(tri_up, db_c) + dbl)
        dlg = _rows(dlg)
        dkey = dks * e_rb + dke * e_lb
        dqy = dqs * e_br + dqb * e_b
        dfg = dlg / fg - dkey
        dlb_ref[...] += jnp.sum(dfg * (1.0 - sig), axis=0, keepdims=True)
        dp_ref[0] = (dqy * (sq * (1.0 + qz * (1.0 - sq)))).astype(BF16)
        dp_ref[1] = (dfg * oml * sig * (1.0 - sig)).astype(BF16)
        dp_ref[2] = dv.astype(BF16)
        dp_ref[3] = dgz.astype(BF16)

    def body(q_ref, f_ref, i_ref, g_ref, lg_ref, gn_ref, o_ref, st_ref, dog_ref,
             dp_ref, dlb_ref, dgn_ref, dstate, dst_scr):
        for hs in range(HP):
            cols = slice(hs * HEAD, (hs + 1) * HEAD)
            one_head(q_ref.at[:, cols], f_ref.at[:, cols], i_ref.at[:, cols], g_ref.at[:, cols], lg_ref.at[:, cols],
                     gn_ref, o_ref.at[:, cols], st_ref.at[hs], dog_ref.at[:, cols],
                     dp_ref.at[:, :, cols], dlb_ref.at[hs], dgn_ref.at[hs], dstate.at[hs], dst_scr)

    HP, wide = HGRN_HEADS_PER_STEP, HGRN_HEADS_PER_STEP * HEAD
    hg = H // HP

    def part(p):
        return pl.BlockSpec((tb, wide), functools.partial(lambda h, i, p: (nb - 1 - i, p * hg + h), p=p))

    blk = pl.BlockSpec((tb, wide), lambda h, i: (nb - 1 - i, h))
    acc = pl.BlockSpec((HP, 1, HEAD), lambda h, i: (h, 0, 0))
    r_ops, r_in, r_out, r_shape, r_scr = _rider_args(rider)
    res = pl.pallas_call(
        _ride(rider, body, 9, 3, functools.partial(_grid_corner, 0, 0), functools.partial(_grid_corner, hg - 1, nb - 1)),
        name="hgrn_bwd", grid=(hg, nb),
        in_specs=[part(0), part(1), part(2), part(3),
                  pl.BlockSpec((3, wide), lambda h, i: (0, h)),
                  pl.BlockSpec((1, HEAD), lambda h, i: (0, 0)),
                  blk,
                  pl.BlockSpec((HP, ncb, HEAD, HEAD), lambda h, i: (h, nb - 1 - i, 0, 0)),
                  blk] + r_in,
        out_specs=[pl.BlockSpec((4, tb, wide), lambda h, i: (0, nb - 1 - i, h)), acc, acc] + r_out,
        out_shape=[jax.ShapeDtypeStruct((4, S, H * HEAD), BF16),
                   jax.ShapeDtypeStruct((H, 1, HEAD), F32),
                   jax.ShapeDtypeStruct((H, 1, HEAD), F32)] + r_shape,
        scratch_shapes=[pltpu.VMEM((HP, HEAD, HEAD), F32), pltpu.VMEM((ncb, HEAD, HEAD), F32)] + r_scr,
        compiler_params=_params(("arbitrary", "arbitrary")))(
            proj, proj, proj, proj, logits, gain, o, states, dog, *r_ops)
    return res[:3], res[3:]


def _rope(v, cos, sin):
    return v * cos + pltpu.roll(v, HEAD // 2, 1) * sin


def _lane_pick(tile, hh):
    lane = lax.broadcasted_iota(jnp.int32, tile.shape, 1)
    return jnp.sum(jnp.where(lane == hh, tile, 0.0), axis=-1, keepdims=True)


def _lane_place(cols):
    rows = cols[0].shape[0]
    lane = lax.broadcasted_iota(jnp.int32, (rows, HEAD), 1)
    tile = jnp.zeros((rows, HEAD), F32)
    for hh, v in enumerate(cols):
        tile = jnp.where(lane == hh, v, tile)
    return tile


def _band_masks():
    qi = lax.broadcasted_iota(jnp.int32, (ATTN_SPAN, ATTN_SPAN), 0)
    kj = lax.broadcasted_iota(jnp.int32, (ATTN_SPAN, ATTN_SPAN), 1)
    return kj <= qi, kj >= qi


ATTN_TILE_BLOCKS = 8


def _attn_fwd(a):
    d, L, _ = a.shape
    B, W = min(ATTN_TILE_BLOCKS, a.shape[1] // ATTN_SPAN), ATTN_SPAN
    T = B * W
    assert L % T == 0
    steps = L // T
    scale = HEAD ** -0.5

    def body(q_ref, kc_ref, kp_ref, vc_ref, vp_ref, o_ref, lse_ref):
        n = pl.program_id(1)
        mask_c, mask_p0 = _band_masks()
        first = jnp.logical_and(mask_p0, n > 0)
        units = [(b, hh) for b in range(B) for hh in range(HEADS_PER_GROUP)]
        rows = [slice(b * W, (b + 1) * W) for b in range(B)]
        cols = [slice(hh * HEAD, (hh + 1) * HEAD) for hh in range(HEADS_PER_GROUP)]

        def prev_keys(ref, tile, b, hh):
            return ref[:, cols[hh]] if b == 0 else tile[rows[b - 1], cols[hh]]

        s_c = [jnp.where(mask_c, _dot_nt(q_ref[rows[b], cols[hh]], kc_ref[rows[b], cols[hh]]) * scale, NEG) for b, hh in units]
        s_p = [jnp.where(first if b == 0 else mask_p0,
                         _dot_nt(q_ref[rows[b], cols[hh]], prev_keys(kp_ref, kc_ref, b, hh)) * scale, NEG) for b, hh in units]
        m = [jnp.maximum(jnp.max(x, axis=-1, keepdims=True), jnp.max(y, axis=-1, keepdims=True)) for x, y in zip(s_c, s_p)]
        p_c = [jnp.exp(x - mm) for x, mm in zip(s_c, m)]
        p_p = [jnp.exp(y - mm) for y, mm in zip(s_p, m)]
        l = [jnp.sum(x, axis=-1, keepdims=True) + jnp.sum(y, axis=-1, keepdims=True) for x, y in zip(p_c, p_p)]
        acc = [_dot(p_c[i].astype(BF16), vc_ref[rows[b], cols[hh]]) + _dot(p_p[i].astype(BF16), prev_keys(vp_ref, vc_ref, b, hh))
               for i, (b, hh) in enumerate(units)]
        for i, (b, hh) in enumerate(units):
            o_ref[rows[b], cols[hh]] = (acc[i] / l[i]).astype(BF16)
        for b in range(B):
            lse_ref[rows[b], :] = _lane_place([m[i] + jnp.log(l[i]) for i, (bb, _) in enumerate(units) if bb == b])

    def cur(part):
        return pl.BlockSpec((None, T, GROUP_W), functools.partial(lambda r, n, p: (r, n, p), p=part))

    def prev(part):
        return pl.BlockSpec((None, W, GROUP_W), functools.partial(lambda r, n, p: (r, jnp.maximum(n * B - 1, 0), p), p=part))

    return pl.pallas_call(
        body, name=f"attn_fwd_d{d}", grid=(d, steps),
        in_specs=[cur(0), cur(1), prev(1), cur(2), prev(2)],
        out_specs=[pl.BlockSpec((None, T, GROUP_W), lambda r, n: (r, n, 0)), pl.BlockSpec((None, T, HEAD), lambda r, n: (r, n, 0))],
        out_shape=[jax.ShapeDtypeStruct((d, L, GROUP_W), BF16), jax.ShapeDtypeStruct((d, L, HEAD), F32)],
        compiler_params=_params(("parallel", "arbitrary")))(a, a, a, a, a)


def _attn_bwd(a, do, lse, dd):
    d, L, _ = a.shape
    B, W = min(ATTN_TILE_BLOCKS, a.shape[1] // ATTN_SPAN), ATTN_SPAN
    T = B * W
    assert L % T == 0
    steps = L // T
    scale = HEAD ** -0.5

    def body(qc_ref, qn_ref, kp_ref, kc_ref, vp_ref, vc_ref, doc_ref, don_ref, lc_ref, ln_ref, ddc_ref, ddn_ref, da_ref):
        n = pl.program_id(1)
        mask_c, mask_p0 = _band_masks()
        first = jnp.logical_and(mask_p0, n > 0)
        last = jnp.logical_and(mask_p0, n < steps - 1)
        H4 = range(HEADS_PER_GROUP)
        units = [(b, hh) for b in range(B) for hh in H4]
        rows = [slice(b * W, (b + 1) * W) for b in range(B)]
        cols = [slice(hh * HEAD, (hh + 1) * HEAD) for hh in H4]
        q = {u: qc_ref[rows[u[0]], cols[u[1]]] for u in units}
        k = {u: kc_ref[rows[u[0]], cols[u[1]]] for u in units}
        v = {u: vc_ref[rows[u[0]], cols[u[1]]] for u in units}
        g_o = {u: doc_ref[rows[u[0]], cols[u[1]]] for u in units}
        kb = {(b, hh): kp_ref[:, cols[hh]] if b == 0 else k[(b - 1, hh)] for b, hh in units}
        vb = {(b, hh): vp_ref[:, cols[hh]] if b == 0 else v[(b - 1, hh)] for b, hh in units}
        lse_t = {(b, hh): _lane_pick(lc_ref[rows[b], :], hh) for b, hh in units}
        dd_t = {(b, hh): _lane_pick(ddc_ref[rows[b], :], hh) for b, hh in units}
        p_c = {u: jnp.where(mask_c, jnp.exp(_dot_nt(q[u], k[u]) * scale - lse_t[u]), 0.0) for u in units}
        p_p = {u: jnp.where(first if u[0] == 0 else mask_p0, jnp.exp(_dot_nt(q[u], kb[u]) * scale - lse_t[u]), 0.0) for u in units}
        ds_c = {u: (p_c[u] * (_dot_nt(g_o[u], v[u]) + dd_t[u])).astype(BF16) for u in units}
        ds_p = {u: (p_p[u] * (_dot_nt(g_o[u], vb[u]) + dd_t[u])).astype(BF16) for u in units}
        qn = [qn_ref[:, c] for c in cols]
        g_n = [don_ref[:, c] for c in cols]
        p_n = [jnp.where(last, jnp.exp(_dot_nt(qn[hh], k[(B - 1, hh)]) * scale - _lane_pick(ln_ref[...], hh)), 0.0) for hh in H4]
        ds_n = [(p_n[hh] * (_dot_nt(g_n[hh], v[(B - 1, hh)]) + _lane_pick(ddn_ref[...], hh))).astype(BF16) for hh in H4]
        dq = {u: (_dot(ds_c[u], k[u]) + _dot(ds_p[u], kb[u])) * scale for u in units}
        dk, dv = {}, {}
        for b, hh in units:
            if b < B - 1:
                nxt = (b + 1, hh)
                dk[(b, hh)] = (_dot_tn(ds_c[(b, hh)], q[(b, hh)]) + _dot_tn(ds_p[nxt], q[nxt])) * scale
                dv[(b, hh)] = _dot_tn(p_c[(b, hh)].astype(BF16), g_o[(b, hh)]) + _dot_tn(p_p[nxt].astype(BF16), g_o[nxt])
            else:
                dk[(b, hh)] = (_dot_tn(ds_c[(b, hh)], q[(b, hh)]) + _dot_tn(ds_n[hh], qn[hh])) * scale
                dv[(b, hh)] = _dot_tn(p_c[(b, hh)].astype(BF16), g_o[(b, hh)]) + _dot_tn(p_n[hh].astype(BF16), g_n[hh])
        for b, hh in units:
            da_ref[rows[b], cols[hh]] = dq[(b, hh)].astype(BF16)
            da_ref[rows[b], GROUP_W + hh * HEAD:GROUP_W + (hh + 1) * HEAD] = dk[(b, hh)].astype(BF16)
            da_ref[rows[b], 2 * GROUP_W + hh * HEAD:2 * GROUP_W + (hh + 1) * HEAD] = dv[(b, hh)].astype(BF16)

    nb = L // W

    def cur(width, part):
        return pl.BlockSpec((None, T, width), functools.partial(lambda r, n, p: (r, n, p), p=part))

    def prev(width, part):
        return pl.BlockSpec((None, W, width), functools.partial(lambda r, n, p: (r, jnp.maximum(n * B - 1, 0), p), p=part))

    def nxt(width, part):
        return pl.BlockSpec((None, W, width), functools.partial(lambda r, n, p: (r, jnp.minimum(n * B + B, nb - 1), p), p=part))

    g = GROUP_W
    return pl.pallas_call(
        body, name=f"attn_bwd_d{d}", grid=(d, steps),
        in_specs=[cur(g, 0), nxt(g, 0), prev(g, 1), cur(g, 1), prev(g, 2), cur(g, 2),
                  cur(g, 0), nxt(g, 0), cur(HEAD, 0), nxt(HEAD, 0), cur(HEAD, 0), nxt(HEAD, 0)],
        out_specs=pl.BlockSpec((None, T, 3 * g), lambda r, n: (r, n, 0)),
        out_shape=jax.ShapeDtypeStruct((d, L, 3 * g), BF16),
        compiler_params=_params(("parallel", "arbitrary")))(
            a, a, a, a, a, a, do, do, lse, lse, dd, dd)


def _softmax3(ls):
    mx = jnp.maximum(jnp.maximum(ls[0], ls[1]), ls[2])
    es = [jnp.exp(v - mx) for v in ls]
    tot = es[0] + es[1] + es[2]
    return [e / tot for e in es]


HEAD_COLS = [slice(hh * HEAD, (hh + 1) * HEAD) for hh in range(HEADS_PER_GROUP)]


def _group_spec(d, tm):
    return pl.BlockSpec((d, tm // d, GROUP_W), lambda i: (0, i, 0))


def _gather_heads(ref, scr, d, tm):
    if d == 1:
        return [ref[0, :, cols].astype(F32) for cols in HEAD_COLS]
    for hh, cols in enumerate(HEAD_COLS):
        for r in range(d):
            scr.at[hh][pl.ds(r, tm // d, stride=d), :] = ref[r, :, cols].astype(F32)
    return [scr[hh] for hh in range(HEADS_PER_GROUP)]


def _tile_spec(d, tm):
    return pl.BlockSpec((d, tm // d, HEAD), lambda i: (0, i, 0))


def _gather_tile(ref, scr, d, tm):
    if d == 1:
        return ref[0]
    for r in range(d):
        scr[pl.ds(r, tm // d, stride=d), :] = ref[r]
    return scr[...]


def _scatter_tile(val, scr, ref, d, tm):
    if d == 1:
        ref[0] = val
        return
    scr[...] = val
    for r in range(d):
        ref[r] = scr[pl.ds(r, tm // d, stride=d), :]


def _scatter_heads(vals, scr, ref, d, tm):
    if d == 1:
        for cols, v in zip(HEAD_COLS, vals):
            ref[0, :, cols] = v.astype(ref.dtype)
        return
    for hh, v in enumerate(vals):
        scr[hh] = v
    for hh, cols in enumerate(HEAD_COLS):
        for r in range(d):
            ref[r, :, cols] = scr.at[hh][pl.ds(r, tm // d, stride=d), :].astype(ref.dtype)


def _qkv_dilated(h, gain, w4, gi, cos, sin, d, tm=2048):
    S, K = h.shape
    n_shard = w4.shape[2]
    assert n_shard % HEAD == 0

    def head_cols(hh):
        def index(i, p):
            c = p * (len(ATTN_GROUPS) * GROUP_W) + gi * GROUP_W + hh * HEAD
            return c // n_shard, 0, (c % n_shard) // HEAD
        return pl.BlockSpec((None, K, HEAD), index)

    def body(h_ref, g_ref, *refs):
        w_refs, (cos_ref, sin_ref, out_ref, u_ref, y_scr) = refs[:HEADS_PER_GROUP], refs[HEADS_PER_GROUP:]
        p = pl.program_id(1)

        @pl.when(p == 0)
        def _():
            v = h_ref[...]
            u_ref[...] = (v * _rstd(v) * g_ref[...]).astype(BF16)

        y = _dot(u_ref[...], jnp.concatenate([r[...] for r in w_refs], axis=1))
        heads = [slice(hh * HEAD, (hh + 1) * HEAD) for hh in range(HEADS_PER_GROUP)]
        if d > 1:
            for hh, cols in enumerate(heads):
                y_scr[hh] = y[:, cols]

        def rows_of(hh, r):
            return y[:, heads[hh]] if d == 1 else y_scr.at[hh][pl.ds(r, tm // d, stride=d), :]

        @pl.when(p < 2)
        def _():
            for r in range(d):
                rows = slice(None) if d == 1 else pl.ds(r, tm // d, stride=d)
                cr, sr = cos_ref[rows, :], sin_ref[rows, :]
                for hh, cols in enumerate(heads):
                    out_ref[r, :, cols] = _rope(rows_of(hh, r), cr, sr).astype(BF16)

        @pl.when(p == 2)
        def _():
            for r in range(d):
                for hh, cols in enumerate(heads):
                    out_ref[r, :, cols] = rows_of(hh, r).astype(BF16)

    tab = pl.BlockSpec((tm, HEAD), lambda i, p: (i, 0))
    return pl.pallas_call(
        body, name=f"attn_qkv_d{d}", grid=(S // tm, 3),
        in_specs=[pl.BlockSpec((tm, K), lambda i, p: (i, 0)),
                  pl.BlockSpec((1, K), lambda i, p: (0, 0)),
                  *[head_cols(hh) for hh in range(HEADS_PER_GROUP)], tab, tab],
        out_specs=[pl.BlockSpec((d, tm // d, GROUP_W), lambda i, p: (0, i, p)), pl.BlockSpec((tm, K), lambda i, p: (i, 0))],
        out_shape=[jax.ShapeDtypeStruct((d, S // d, 3 * GROUP_W), BF16), jax.ShapeDtypeStruct((S, K), BF16)],
        scratch_shapes=[pltpu.VMEM((HEADS_PER_GROUP, tm, HEAD), F32)],
        compiler_params=_params(("parallel", "arbitrary")))(h, gain, *[w4] * HEADS_PER_GROUP, cos, sin)


def _undilate_group(da, dqkv, cos, sin, g, tm=2048):
    d, L, _ = da.shape
    S = d * L
    G = len(ATTN_GROUPS)

    def body(*refs):
        da_ref, cos_ref, sin_ref, out_ref, scr = refs[0], refs[1], refs[2], refs[-2], refs[-1]
        p = pl.program_id(1)
        heads = [slice(hh * HEAD, (hh + 1) * HEAD) for hh in range(HEADS_PER_GROUP)]
        if d > 1:
            for hh, cols in enumerate(heads):
                for r in range(d):
                    scr.at[hh][pl.ds(r, tm // d, stride=d), :] = da_ref[r, :, cols].astype(F32)

        def tokens(hh):
            return da_ref[0, :, heads[hh]].astype(F32) if d == 1 else scr[hh]

        @pl.when(p < 2)
        def _():
            cr, sr = cos_ref[...], -sin_ref[...]
            for hh, cols in enumerate(heads):
                out_ref[:, cols] = _rope(tokens(hh), cr, sr).astype(BF16)

        @pl.when(p == 2)
        def _():
            for hh, cols in enumerate(heads):
                out_ref[:, cols] = tokens(hh).astype(BF16)

    tab = pl.BlockSpec((tm, HEAD), lambda i, p: (i, 0))
    operands = (da, cos, sin) if dqkv is None else (da, cos, sin, dqkv)
    return pl.pallas_call(
        body, name=f"attn_undilate_d{d}", grid=(S // tm, 3),
        in_specs=[pl.BlockSpec((d, tm // d, GROUP_W), lambda i, p: (0, i, p)), tab, tab] + ([] if dqkv is None else [ANY]),
        out_specs=pl.BlockSpec((tm, GROUP_W), lambda i, p: (i, p * G + g)),
        out_shape=jax.ShapeDtypeStruct((S, 3 * G * GROUP_W), BF16),
        input_output_aliases={} if dqkv is None else {3: 0},
        scratch_shapes=[pltpu.VMEM((HEADS_PER_GROUP, tm, HEAD), F32)],
        compiler_params=_params(("parallel", "arbitrary")))(*operands)


def _attn_merge(os_, lses, h, w2, tm=1024):
    G = len(os_)
    S, N = h.shape

    def body(*refs):
        o_refs, l_refs, h_ref, w_ref, res_ref, out_ref = refs[:G], refs[G:2 * G], *refs[2 * G:2 * G + 4]
        scr = refs[2 * G + 4:]
        o = [_gather_heads(o_refs[g], scr[g], d, tm) for g, (_, d) in enumerate(ATTN_GROUPS)]
        l = [_gather_tile(l_refs[g], scr[G + g].at[0], d, tm) for g, (_, d) in enumerate(ATTN_GROUPS)]
        for hh in range(HEADS_PER_GROUP):
            al = _softmax3([_lane_pick(l[g], hh) for g in range(G)])
            for g in range(G):
                out_ref[:, g * GROUP_W + hh * HEAD:g * GROUP_W + (hh + 1) * HEAD] = (o[g][hh] * al[g]).astype(BF16)
        res_ref[...] = h_ref[...] + _dot(out_ref[...], w_ref[...])

    specs = [_group_spec(d, tm) for _, d in ATTN_GROUPS]
    row = pl.BlockSpec((tm, N), lambda i: (i, 0))
    return pl.pallas_call(
        body, name="attn_merge_out", grid=(S // tm,),
        in_specs=specs + [_tile_spec(d, tm) for _, d in ATTN_GROUPS] + [
            row, pl.BlockSpec((G * GROUP_W, N), lambda i: (0, 0), pipeline_mode=pl.Buffered(1))],
        out_specs=[row, pl.BlockSpec((tm, G * GROUP_W), lambda i: (i, 0))],
        out_shape=[jax.ShapeDtypeStruct((S, N), F32), jax.ShapeDtypeStruct((S, G * GROUP_W), BF16)],
        scratch_shapes=[pltpu.VMEM((HEADS_PER_GROUP, tm, HEAD), F32)] * (2 * G),
        compiler_params=_params(("parallel",)))(*os_, *lses, h, w2)


def _attn_merge_bwd(os_, lses, dh, w2, tm=512):
    G = len(os_)
    S, N = dh.shape

    def body(*refs):
        o_refs, l_refs, dh_ref, w_ref = refs[:G], refs[G:2 * G], refs[2 * G], refs[2 * G + 1]
        do_refs, dd_refs = refs[2 * G + 2:3 * G + 2], refs[3 * G + 2:4 * G + 2]
        scr = refs[4 * G + 2:]
        doa = _dot_nt(dh_ref[...].astype(BF16), w_ref[...])
        o = [_gather_heads(o_refs[g], scr[g], d, tm) for g, (_, d) in enumerate(ATTN_GROUPS)]
        l = [_gather_tile(l_refs[g], scr[G + g].at[0], d, tm) for g, (_, d) in enumerate(ATTN_GROUPS)]
        do = [[None] * HEADS_PER_GROUP for _ in range(G)]
        dd = [[None] * HEADS_PER_GROUP for _ in range(G)]
        for hh in range(HEADS_PER_GROUP):
            al = _softmax3([_lane_pick(l[g], hh) for g in range(G)])
            mix = None
            for g in range(G):
                dg = doa[:, g * GROUP_W + hh * HEAD:g * GROUP_W + (hh + 1) * HEAD]
                do[g][hh] = dg * al[g]
                t = al[g] * jnp.sum(dg * o[g][hh], axis=-1, keepdims=True)
                mix = t if mix is None else mix + t
            for g in range(G):
                dd[g][hh] = -al[g] * mix
        for g, (_, d) in enumerate(ATTN_GROUPS):
            _scatter_heads(do[g], scr[2 * G + g], do_refs[g], d, tm)
            _scatter_tile(_lane_place(dd[g]), scr[3 * G + g].at[0], dd_refs[g], d, tm)

    specs = [_group_spec(d, tm) for _, d in ATTN_GROUPS]
    tiles = [_tile_spec(d, tm) for _, d in ATTN_GROUPS]
    do_shapes = [jax.ShapeDtypeStruct((d, S // d, GROUP_W), BF16) for _, d in ATTN_GROUPS]
    dd_shapes = [jax.ShapeDtypeStruct((d, S // d, HEAD), F32) for _, d in ATTN_GROUPS]
    return pl.pallas_call(
        body, name="attn_merge_bwd", grid=(S // tm,),
        in_specs=specs + tiles + [pl.BlockSpec((tm, N), lambda i: (i, 0)),
                                  pl.BlockSpec((G * GROUP_W, N), lambda i: (0, 0), pipeline_mode=pl.Buffered(1))],
        out_specs=specs + tiles,
        out_shape=do_shapes + dd_shapes,
        scratch_shapes=[pltpu.VMEM((HEADS_PER_GROUP, tm, HEAD), F32)] * (4 * G),
        compiler_params=_params(("parallel",)))(*os_, *lses, dh, w2)


def _rope_tables(S):
    inv_freq = (1.0 / (np.float32(ROPE_THETA) ** (np.arange(0, HEAD, 2, dtype=np.float32) / np.float32(HEAD))))
    ang = (np.arange(S, dtype=np.float32)[:, None] * inv_freq.astype(np.float32)[None, :]).astype(np.float64)
    cos, sin = np.cos(ang).astype(np.float32), np.sin(ang).astype(np.float32)
    return jnp.asarray(np.concatenate([cos, cos], axis=-1)), jnp.asarray(np.concatenate([-sin, sin], axis=-1))


def _local_step(x, target, norm_mix, norm_ffn, lb_logits, out_gain, final_norm, comm):
    S = x.shape[0]
    nm0, nm1 = norm_mix[0:1], norm_mix[1:2]
    nf0, nf1 = norm_ffn[0:1], norm_ffn[1:2]
    w = comm.first_weights()

    proj, u0, got = _norm_mm(x, nm0, w["hin"], "hgrn_in", rider=comm.gather_rider(LATE_WEIGHTS_A))
    w.update(comm.gathered(LATE_WEIGHTS_A, got))
    (o, og, states), got = _hgrn_fwd(proj, lb_logits, out_gain, rider=comm.gather_rider(LATE_WEIGHTS_B))
    w.update(comm.gathered(LATE_WEIGHTS_B, got))
    fin_tn = w["fin0"].shape[2]
    h1 = _mm_res(x, og, w["hout"], "hgrn_out")
    z0, u1, _ = _norm_mm(h1, nf0, w["fin0"], "ffn0_in", out_dtype=BF16)
    h2, act0 = _swiglu_mm_res(h1, z0, w["fdn0"], "ffn0_down")
    cos, sin = _rope_tables(S)
    G = len(ATTN_GROUPS)
    a_g, u2 = zip(*[_qkv_dilated(h2, nm1, w["qkv"], gi, cos, sin, d) for gi, (_, d) in enumerate(ATTN_GROUPS)])
    o_g, lse_g = zip(*[_attn_fwd(a) for a in a_g])
    h3, oa = _attn_merge(o_g, lse_g, h2, w["aout"])
    z1, u3, _ = _norm_mm(h3, nf1, w["fin1"], "ffn1_in", out_dtype=BF16)
    h4, act1 = _swiglu_mm_res(h3, z1, w["fdn1"], "ffn1_down")
    dh4, loss, d_final = _loss_head(h4, final_norm, target)

    grads, small = {}, {"final_norm": d_final}

    def ffn_bwd(dh, h_in, u_in, z, act, gain, w_in, w_dn, tag, ride=None):
        dz = _mm_nt_swiglu_bwd(dh, w_dn, z, tag + "_down_dx")
        g_dn = _mm_tn(act, dh, 1, D_MODEL, D_MODEL, tag + "_down_dw")[0]
        g_in = _mm_tn(u_in, dz, N_CHIPS, fin_tn, fin_tn, tag + "_in_dw")
        rider = None if ride is None else ride(g_in, g_dn)
        dh_in, dgain, got = _mm_nt_normbwd(dz, w_in, h_in, gain, dh, tag + "_in_dx", rider=rider)
        return dh_in, dgain, g_in, g_dn, got

    dh3, d_nf1, grads["fin1"], grads["fdn1"], _ = ffn_bwd(dh4, h3, u3, z1, act1, nf1, w["fin1"], w["fdn1"], "ffn1")
    grads["aout"] = _mm_tn(oa, dh3, 1, D_MODEL, D_MODEL, "attn_out_dw")[0]
    merged = _attn_merge_bwd(o_g, lse_g, dh3, w["aout"])
    G = len(ATTN_GROUPS)
    das = [_attn_bwd(a_g[gi], merged[gi], lse_g[gi], merged[G + gi]) for gi in range(G)]
    dqkv = None
    for gi in range(G):
        dqkv = _undilate_group(das[gi], dqkv, cos, sin, gi)
    n_qkv = w["qkv"].shape[2]
    grads["qkv"] = _mm_tn(u2[0], dqkv, N_CHIPS, n_qkv, n_qkv, "attn_qkv_dw")
    dh2, d_nm1, _ = _mm_nt_normbwd(dqkv, w["qkv"], h2, nm1, dh3, "attn_qkv_dx")

    def ride_early(g_in, g_dn):
        return comm.pair_rider({**grads, "fin0": g_in, "fdn0": g_dn}, "early")

    dh1, d_nf0, _, _, got = ffn_bwd(dh2, h1, u1, z0, act0, nf0, w["fin0"], w["fdn0"], "ffn0", ride=ride_early)
    comm.paired("early", got)
    dog = _mm_nt(dh1, w["hout"][None], "hgrn_out_dx")
    (dproj, dlb, dgn), got = _hgrn_bwd(proj, lb_logits, out_gain, o, states, dog, rider=comm.exchange_rider("early"))
    comm.exchanged("early", got)
    late = {"hout": _mm_tn(og, dh1, 1, D_MODEL, D_MODEL, "hgrn_out_dw")[0],
            "hin": _mm_tn(u0, dproj, N_CHIPS, D_MODEL, D_MODEL, "hgrn_in_dw")}
    comm.pair_now(late, "late")
    dx, d_nm0, got = _mm_nt_normbwd(dproj, w["hin"], x, nm0, dh1, "hgrn_in_dx", rider=comm.exchange_rider("late"))
    comm.exchanged("late", got)

    small["norm_mix"] = jnp.concatenate([d_nm0, d_nm1], axis=0)
    small["norm_ffn"] = jnp.concatenate([d_nf0, d_nf1], axis=0)
    small["lb"] = dlb.reshape(1, HGRN_HEADS * HEAD)
    small["out_norm"] = dgn.reshape(HGRN_HEADS, HEAD)
    return loss, dx, small


def _place():
    x, y, c = lax.axis_index("x"), lax.axis_index("y"), lax.axis_index("c")
    others = [(1 - x, y), (x, 1 - y), (1 - x, 1 - y)]
    return x, y, c, others


ANY = pl.BlockSpec(memory_space=pl.ANY)


class _GatherRider:
    def __init__(self, shards):
        self.operands = list(shards)
        n = self.n = len(shards)
        self.out_shape = [jax.ShapeDtypeStruct((N_CHIPS,) + s.shape, s.dtype) for s in shards]
        self.scratch = [pltpu.SemaphoreType.DMA((3 * n,)), pltpu.SemaphoreType.DMA((3 * n,)),
                        pltpu.SemaphoreType.DMA((3 * n,)), pltpu.SemaphoreType.DMA((3 * n,)),
                        pltpu.SemaphoreType.DMA((n,)), pltpu.SemaphoreType.DMA((n,))]

    def _copies(self, ins, outs, sems):
        ici_send, ici_recv, _, _, own_send, own_recv = sems
        x, y, c, others = _place()
        me = 2 * x + y
        own = [pltpu.make_async_remote_copy(
            src_ref=ins[a], dst_ref=outs[a].at[me], send_sem=own_send.at[a], recv_sem=own_recv.at[a],
            device_id=(x, y, 1 - c), device_id_type=MESH) for a in range(self.n)]
        sends = [pltpu.make_async_remote_copy(
            src_ref=ins[a].at[c], dst_ref=outs[a].at[me, c], send_sem=ici_send.at[a * 3 + k], recv_sem=ici_recv.at[a * 3 + k],
            device_id=(ox, oy, c), device_id_type=MESH) for a in range(self.n) for k, (ox, oy) in enumerate(others)]
        return own, sends

    def start(self, ins, outs, sems):
        own, sends = self._copies(ins, outs, sems)
        for cp in own + sends:
            cp.start()

    def finish(self, ins, outs, sems):
        ici_send, ici_recv, d2d_send, d2d_recv, _, _ = sems
        x, y, c, others = _place()
        sibling = (x, y, 1 - c)
        own, sends = self._copies(ins, outs, sems)
        passes = []
        for a in range(self.n):
            for k, (ox, oy) in enumerate(others):
                s = a * 3 + k
                got = outs[a].at[2 * ox + oy, c]
                pltpu.make_async_remote_copy(
                    src_ref=got, dst_ref=got, send_sem=ici_send.at[s], recv_sem=ici_recv.at[s],
                    device_id=(ox, oy, c), device_id_type=MESH).wait_recv()
                fwd = pltpu.make_async_remote_copy(
                    src_ref=got, dst_ref=got, send_sem=d2d_send.at[s], recv_sem=d2d_recv.at[s],
                    device_id=sibling, device_id_type=MESH)
                fwd.start()
                passes.append(fwd)
        for a in range(self.n):
            for k, (ox, oy) in enumerate(others):
                s = a * 3 + k
                theirs = outs[a].at[2 * ox + oy, 1 - c]
                pltpu.make_async_remote_copy(
                    src_ref=theirs, dst_ref=theirs, send_sem=d2d_send.at[s], recv_sem=d2d_recv.at[s],
                    device_id=sibling, device_id_type=MESH).wait_recv()
        for cp in own:
            cp.wait()
        for cp in sends + passes:
            cp.wait_send()


class _PairRider:
    def __init__(self, grads):
        self.operands = list(grads)
        n = self.n = len(grads)
        self.out_shape = [jax.ShapeDtypeStruct((N_CHIPS,) + g.shape[2:], F32) for g in grads]
        self.scratch = [pltpu.SemaphoreType.DMA((N_CHIPS * n,)), pltpu.SemaphoreType.DMA((N_CHIPS * n,))]

    def _copies(self, ins, outs, sems):
        send_sem, recv_sem = sems
        x, y, c, _ = _place()
        return [pltpu.make_async_remote_copy(
            src_ref=ins[a].at[j, 1 - c], dst_ref=outs[a].at[j], send_sem=send_sem.at[a * N_CHIPS + j],
            recv_sem=recv_sem.at[a * N_CHIPS + j], device_id=(x, y, 1 - c), device_id_type=MESH)
            for a in range(self.n) for j in range(N_CHIPS)]

    def start(self, ins, outs, sems):
        for cp in self._copies(ins, outs, sems):
            cp.start()

    def finish(self, ins, outs, sems):
        for cp in self._copies(ins, outs, sems):
            cp.wait()


class _ExchangeRider:
    def __init__(self, parts):
        self.operands = list(parts)
        n = self.n = len(parts)
        self.out_shape = [jax.ShapeDtypeStruct(p.shape, p.dtype) for p in parts]
        self.scratch = [pltpu.SemaphoreType.DMA((3 * n,)), pltpu.SemaphoreType.DMA((3 * n,))]

    def _copies(self, ins, outs, sems):
        send_sem, recv_sem = sems
        x, y, c, others = _place()
        me = 2 * x + y
        return [pltpu.make_async_remote_copy(
            src_ref=ins[a].at[2 * ox + oy], dst_ref=outs[a].at[me], send_sem=send_sem.at[a * 3 + k],
            recv_sem=recv_sem.at[a * 3 + k], device_id=(ox, oy, c), device_id_type=MESH)
            for a in range(self.n) for k, (ox, oy) in enumerate(others)]

    def start(self, ins, outs, sems):
        for cp in self._copies(ins, outs, sems):
            cp.start()

    def finish(self, ins, outs, sems):
        send_sem, recv_sem = sems
        x, y, c, others = _place()
        for a in range(self.n):
            for k, (ox, oy) in enumerate(others):
                s = a * 3 + k
                got = outs[a].at[2 * ox + oy]
                pltpu.make_async_remote_copy(
                    src_ref=got, dst_ref=got, send_sem=send_sem.at[s], recv_sem=recv_sem.at[s],
                    device_id=(ox, oy, c), device_id_type=MESH).wait_recv()
        for cp in self._copies(ins, outs, sems):
            cp.wait_send()


def _run_rider(rider, name):
    n = rider.n

    def body(*refs):
        ins, outs, sems = refs[:n], refs[n:2 * n], refs[2 * n:]
        rider.start(ins, outs, sems)
        rider.finish(ins, outs, sems)

    return pl.pallas_call(
        body, name=name, in_specs=[ANY] * n, out_specs=[ANY] * n,
        out_shape=rider.out_shape, scratch_shapes=rider.scratch)(*rider.operands)


def _ride(rider, body, n_in, n_out, first, last):
    if rider is None:
        return body
    n = rider.n

    def wrapped(*refs):
        host_in, r_in = refs[:n_in], refs[n_in:n_in + n]
        host_out = refs[n_in + n:n_in + n + n_out]
        r_out = refs[n_in + n + n_out:n_in + 2 * n + n_out]
        rest = refs[n_in + 2 * n + n_out:]
        host_scr, sems = rest[:len(rest) - len(rider.scratch)], rest[len(rest) - len(rider.scratch):]

        @pl.when(first())
        def _():
            rider.start(r_in, r_out, sems)

        body(*host_in, *host_out, *host_scr)

        @pl.when(last())
        def _():
            rider.finish(r_in, r_out, sems)

    return wrapped


def _rider_args(rider):
    if rider is None:
        return [], [], [], [], []
    return rider.operands, [ANY] * rider.n, [ANY] * rider.n, rider.out_shape, rider.scratch


def _pair_sum(gs, gots, c_idx):
    n, parts = len(gs), 2

    def body(c_ref, *refs):
        g_refs, got_refs, pb_refs = refs[:n], refs[n:2 * n], refs[2 * n:]
        for k in range(n):
            pb_refs[k][...] = (g_refs[k][...] + got_refs[k][...]).astype(BF16)

    mine = [pl.BlockSpec((None, None, g.shape[2] // parts, g.shape[3]), lambda j, i, c_ref: (j, c_ref[0], i, 0))
            for g in gs]
    blk = [pl.BlockSpec((None, g.shape[2] // parts, g.shape[3]), lambda j, i, c_ref: (j, i, 0)) for g in gs]
    return pl.pallas_call(
        body, name="grad_pair_sum",
        grid_spec=pltpu.PrefetchScalarGridSpec(
            num_scalar_prefetch=1, grid=(N_CHIPS, parts), in_specs=mine + blk, out_specs=blk),
        out_shape=[jax.ShapeDtypeStruct((N_CHIPS,) + g.shape[2:], BF16) for g in gs],
        compiler_params=_params(("parallel", "parallel")))(c_idx, *gs, *gots)


def _chip_sum(gs, sibs, gots, place):
    n, parts = len(gs), 4

    def body(place_ref, *refs):
        g_refs, sib_refs, got_refs, t_refs = refs[:n], refs[n:2 * n], refs[2 * n:3 * n], refs[3 * n:]
        me = place_ref[0]
        for k in range(n):
            own = g_refs[k][...] + sib_refs[k][...]
            acc = None
            for s in range(N_CHIPS):
                term = jnp.where(me == s, own, got_refs[k][s].astype(F32))
                acc = term if acc is None else acc + term
            t_refs[k][...] = acc

    tiles = [(g.shape[2] // parts, g.shape[3]) for g in gs]
    return pl.pallas_call(
        body, name="grad_chip_sum",
        grid_spec=pltpu.PrefetchScalarGridSpec(
            num_scalar_prefetch=1, grid=(parts,),
            in_specs=[pl.BlockSpec((None, None) + t, lambda i, pr: (pr[0], pr[1], i, 0)) for t in tiles]
            + [pl.BlockSpec((None,) + t, lambda i, pr: (pr[0], i, 0)) for t in tiles]
            + [pl.BlockSpec((N_CHIPS,) + t, lambda i, pr: (0, i, 0)) for t in tiles],
            out_specs=[pl.BlockSpec(t, lambda i, pr: (i, 0)) for t in tiles]),
        out_shape=[jax.ShapeDtypeStruct(g.shape[2:], F32) for g in gs],
        compiler_params=_params(("parallel",)))(place, *gs, *sibs, *gots)


def _pair_share(halves):
    n = len(halves)

    def body(*refs):
        ins, outs = refs[:n], refs[n:2 * n]
        send_sem, recv_sem = refs[2 * n:]
        x, y, c, _ = _place()
        cps = [pltpu.make_async_remote_copy(
            src_ref=ins[a], dst_ref=outs[a], send_sem=send_sem.at[a], recv_sem=recv_sem.at[a],
            device_id=(x, y, 1 - c), device_id_type=MESH) for a in range(n)]
        for cp in cps:
            cp.start()
        for cp in cps:
            cp.wait()

    return pl.pallas_call(
        body, name="grad_pair_share",
        in_specs=[ANY] * n, out_specs=[ANY] * n,
        out_shape=[jax.ShapeDtypeStruct(h.shape, F32) for h in halves],
        scratch_shapes=[pltpu.SemaphoreType.DMA((n,)), pltpu.SemaphoreType.DMA((n,))],
        )(*halves)


def _small_allreduce(pack):
    m_per, ncol = pack.shape
    n_dev = 8

    def body(x_ref, sum_ref, all_ref, send_sems, recv_sems, local_sem):
        x, y, c, others = _place()
        me, sibling = (x, y, c), (x, y, 1 - c)

        def rows(px, py, pc):
            return all_ref.at[pl.ds((4 * px + 2 * py + pc) * m_per, m_per), :]

        def copy(k, block, to, src=None):
            return pltpu.make_async_remote_copy(
                src_ref=rows(*block) if src is None else src, dst_ref=rows(*block),
                send_sem=send_sems.at[k], recv_sem=recv_sems.at[k], device_id=to, device_id_type=MESH)

        mine = pltpu.make_async_copy(x_ref, rows(*me), local_sem)
        mine.start()
        first = [copy(0, me, sibling, src=x_ref)]
        first += [copy(1 + j, me, (*chip, c), src=x_ref) for j, chip in enumerate(others)]
        for cp in first:
            cp.start()
        passed = [copy(4 + j, (*chip, c), sibling) for j, chip in enumerate(others)]
        for j, chip in enumerate(others):
            copy(1 + j, (*chip, c), me).wait_recv()
            passed[j].start()
        copy(0, sibling, me).wait_recv()
        for j, chip in enumerate(others):
            copy(4 + j, (*chip, 1 - c), me).wait_recv()
        for cp in first + passed:
            cp.wait_send()
        mine.wait()
        acc = all_ref[0:m_per, :]
        for dvc in range(1, n_dev):
            acc = acc + all_ref[dvc * m_per:(dvc + 1) * m_per, :]
        sum_ref[...] = acc

    return pl.pallas_call(
        body, name="small_allreduce",
        in_specs=[pl.BlockSpec(memory_space=pltpu.VMEM)],
        out_specs=pl.BlockSpec(memory_space=pltpu.VMEM),
        out_shape=jax.ShapeDtypeStruct((m_per, ncol), F32),
        scratch_shapes=[pltpu.VMEM((n_dev * m_per, ncol), F32),
                        pltpu.SemaphoreType.DMA((7,)), pltpu.SemaphoreType.DMA((7,)), pltpu.SemaphoreType.DMA],
        )(pack)


def _adam_math(w, g, m, v):
    m = ADAM_B1 * m + (1.0 - ADAM_B1) * g
    v = ADAM_B2 * v + (1.0 - ADAM_B2) * (g * g)
    m_hat = m / (1.0 - ADAM_B1 ** ADAM_STEP)
    v_hat = v / (1.0 - ADAM_B2 ** ADAM_STEP)
    delta = -ADAM_LR * (m_hat / (jnp.sqrt(v_hat) + ADAM_EPS) + ADAM_WD * w)
    return delta, m, v


def _adamw(halves, c_idx, w, m, v, name):
    L = len(halves)
    r, C = halves[0][0].shape
    tr = _row_tile(r, C, 2 * 1024 * 1024)
    nt = r // tr

    def body(c_ref, *refs):
        g_refs, (w_ref, m_ref, v_ref), (g_ref, d_ref, nm_ref, nv_ref) = refs[:2 * L], refs[2 * L:2 * L + 3], refs[2 * L + 3:]
        own = pl.program_id(1) == c_ref[0]
        g = None
        for l in range(L):
            cand = jnp.where(own, g_refs[2 * l][...], g_refs[2 * l + 1][...])
            g = cand if g is None else jnp.where(pl.program_id(0) == l, cand, g)
        g_ref[...] = g
        d_ref[...], nm_ref[...], nv_ref[...] = _adam_math(w_ref[...], g, m_ref[...], v_ref[...])

    def half(l, mine):
        def index(ll, h, i, c_ref):
            read = (h == c_ref[0]) if mine else (h != c_ref[0])
            return jnp.where(jnp.logical_and(ll == l, read), i, 0), 0
        return pl.BlockSpec((tr, C), index)

    full = pl.BlockSpec((None, tr, C), lambda ll, h, i, c_ref: (ll, h * nt + i, 0))
    shp = jax.ShapeDtypeStruct((L, 2 * r, C), F32)
    g_specs = [half(l, mine) for l in range(L) for mine in (True, False)]
    return pl.pallas_call(
        body, name=name,
        grid_spec=pltpu.PrefetchScalarGridSpec(
            num_scalar_prefetch=1, grid=(L, 2, nt),
            in_specs=g_specs + [full] * 3, out_specs=[full] * 4),
        out_shape=[shp] * 4,
        compiler_params=_params(("arbitrary", "arbitrary", "arbitrary")))(
            c_idx, *[a for pair in halves for a in pair], w, m, v)


SMALL_ROW_SPANS = ((0, 2), (2, 4), (4, 7), (7, 8), (8, 9))


def _small_update(gsum, w, m, v):
    n = len(SMALL_ROW_SPANS)

    def body(gs_ref, *refs):
        w_refs, m_refs, v_refs = refs[:n], refs[n:2 * n], refs[2 * n:3 * n]
        g_refs, d_refs, nm_refs, nv_refs = [refs[(3 + k) * n:(4 + k) * n] for k in range(4)]
        lg_ref = w_refs[2]
        l0, l1, l2 = lg_ref[0:1, :], lg_ref[1:2, :], lg_ref[2:3, :]
        mx = jnp.maximum(jnp.maximum(l0, l1), l2)
        e0, e1, e2 = jnp.exp(l0 - mx), jnp.exp(l1 - mx), jnp.exp(l2 - mx)
        tot = e0 + e1 + e2
        p0, p1, p2 = e0 / tot, e1 / tot, e2 / tot
        dlb = gs_ref[4:5, :]
        for k, (r0, r1) in enumerate(SMALL_ROW_SPANS):
            if k == 2:
                g = jnp.concatenate([dlb * p0 * (1.0 - p0), -dlb * p0 * p1, -dlb * p0 * p2], axis=0)
            else:
                g = gs_ref[r0:r1, 0:w_refs[k].shape[1]]
            g_refs[k][...] = g
            d_refs[k][...], nm_refs[k][...], nv_refs[k][...] = _adam_math(w_refs[k][...], g, m_refs[k][...], v_refs[k][...])

    full = pl.BlockSpec(memory_space=pltpu.VMEM)
    shapes = [jax.ShapeDtypeStruct(a.shape, F32) for a in w]
    out = pl.pallas_call(
        body, name="small_update", in_specs=[full] * (1 + 3 * n), out_specs=[full] * (4 * n), out_shape=shapes * 4)(
            gsum, *w, *m, *v)
    return [out[k * n:(k + 1) * n] for k in range(4)]


def _pack_small(norm_mix, norm_ffn, lb3, out_norm, final_norm, extra=None):
    ncol = norm_mix.shape[1]
    on = jnp.pad(out_norm.reshape(1, -1), ((0, 0), (0, ncol - out_norm.size)))
    rows = [norm_mix, norm_ffn, lb3, on, final_norm.reshape(1, ncol)]
    if extra is not None:
        rows.append(extra)
    used = sum(r.shape[0] for r in rows)
    rows.append(jnp.zeros((SMALL_ROWS - used, ncol), F32))
    return jnp.concatenate(rows, axis=0)


WEIGHT_NAMES = ("hin", "hout", "qkv", "aout", "fin0", "fin1", "fdn0", "fdn1")
FIRST_WEIGHTS = ("hin",)
LATE_WEIGHTS_A = ("hout", "fin0", "fdn0")
LATE_WEIGHTS_B = ("qkv", "aout", "fin1", "fdn1")


def _split_weights(hgrn_w_in, hgrn_w_out, attn_w_qkv, attn_w_out, ffn_w_in, ffn_w_down):
    return {"hin": hgrn_w_in[0], "hout": hgrn_w_out[0], "qkv": attn_w_qkv[0], "aout": attn_w_out[0],
            "fin0": ffn_w_in[0], "fin1": ffn_w_in[1], "fdn0": ffn_w_down[0], "fdn1": ffn_w_down[1]}


def _halves(v):
    r, c = v.shape
    return v.reshape(2, r // 2, c)


def _full_weights(gathered):
    out = {}
    for k, g in gathered.items():
        _, _, r, c = g.shape
        if k in ("hin", "qkv", "fin0", "fin1"):
            out[k] = g.reshape(N_CHIPS, 2 * r, c)
        else:
            out[k] = g.reshape(N_CHIPS * 2 * r, c)
    return out


class _StepComm:
    def __init__(self, shards, c_idx, me_idx):
        self.shards, self.c_idx, self.me_idx = shards, c_idx, me_idx
        self.halves = {}
        self._stage = {}

    def gather_rider(self, names):
        return _GatherRider([_halves(self.shards[k].astype(BF16)) for k in names])

    def gathered(self, names, got):
        return _full_weights(dict(zip(names, got)))

    def first_weights(self):
        return self.gathered(FIRST_WEIGHTS, _run_rider(self.gather_rider(FIRST_WEIGHTS), "gather_first"))

    def pair_rider(self, grads, tag):
        names = list(grads)
        g4 = []
        for k in names:
            r, c = self.shards[k].shape
            g4.append(grads[k].reshape(N_CHIPS, 2, r // 2, c))
        self._stage[tag] = (names, g4)
        return _PairRider(g4)

    def pair_now(self, grads, tag):
        self.paired(tag, _run_rider(self.pair_rider(grads, tag), "grad_pair_exchange_" + tag))

    def paired(self, tag, got):
        names, g4 = self._stage[tag]
        self._stage[tag] = (names, list(zip(g4, got, _pair_sum(g4, list(got), self.c_idx))))

    def exchange_rider(self, tag):
        return _ExchangeRider([s[2] for s in self._stage[tag][1]])

    def exchanged(self, tag, got):
        names, sums = self._stage.pop(tag)
        place = jnp.concatenate([self.me_idx, self.c_idx])
        g4, sibs, _ = zip(*sums)
        self.halves.update(zip(names, _chip_sum(list(g4), list(sibs), list(got), place)))

    def shared_halves(self):
        mine = [self.halves[k] for k in WEIGHT_NAMES]
        return dict(zip(WEIGHT_NAMES, zip(mine, _pair_share(mine))))


def kernel(x, norm_mix, norm_ffn, hgrn_w_in, hgrn_lb_logits, hgrn_out_norm, hgrn_w_out, attn_w_qkv, attn_w_out, ffn_w_in, ffn_w_down, final_norm, loss_target, m_norm_mix, m_norm_ffn, m_hgrn_w_in, m_hgrn_lb_logits, m_hgrn_out_norm, m_hgrn_w_out, m_attn_w_qkv, m_attn_w_out, m_ffn_w_in, m_ffn_w_down, m_final_norm, v_norm_mix, v_norm_ffn, v_hgrn_w_in, v_hgrn_lb_logits, v_hgrn_out_norm, v_hgrn_w_out, v_attn_w_qkv, v_attn_w_out, v_ffn_w_in, v_ffn_w_down, v_final_norm):
    S = x.shape[1]
    xi, yi, ci = lax.axis_index("x"), lax.axis_index("y"), lax.axis_index("c")
    c_idx = jnp.reshape(ci, (1,)).astype(jnp.int32)
    me_idx = jnp.reshape(2 * xi + yi, (1,)).astype(jnp.int32)

    w_own = _split_weights(hgrn_w_in, hgrn_w_out, attn_w_qkv, attn_w_out, ffn_w_in, ffn_w_down)

    comm = _StepComm(w_own, c_idx, me_idx)
    loss, dx, small = _local_step(
        x.reshape(S, D_MODEL), loss_target.reshape(S, D_MODEL), norm_mix, norm_ffn, hgrn_lb_logits,
        hgrn_out_norm, final_norm.reshape(1, D_MODEL), comm)

    halves = comm.shared_halves()
    updated = {}
    for tensor, layers, (wt, mt, vt) in (
            ("hgrn_w_in", ("hin",), (hgrn_w_in, m_hgrn_w_in, v_hgrn_w_in)),
            ("hgrn_w_out", ("hout",), (hgrn_w_out, m_hgrn_w_out, v_hgrn_w_out)),
            ("attn_w_qkv", ("qkv",), (attn_w_qkv, m_attn_w_qkv, v_attn_w_qkv)),
            ("attn_w_out", ("aout",), (attn_w_out, m_attn_w_out, v_attn_w_out)),
            ("ffn_w_in", ("fin0", "fin1"), (ffn_w_in, m_ffn_w_in, v_ffn_w_in)),
            ("ffn_w_down", ("fdn0", "fdn1"), (ffn_w_down, m_ffn_w_down, v_ffn_w_down))):
        updated[tensor] = _adamw([halves[k] for k in layers], c_idx, wt, mt, vt, "adamw_" + tensor)

    loss_row = jnp.pad(loss, ((0, 0), (0, D_MODEL - loss.shape[1])))
    lb3 = jnp.concatenate([small["lb"], jnp.zeros((2, D_MODEL), F32)], axis=0)
    on_grad = jnp.sum(small["out_norm"], axis=0, keepdims=True)
    pack = _pack_small(small["norm_mix"], small["norm_ffn"], lb3, on_grad, small["final_norm"], loss_row)
    gsum = _small_allreduce(pack)
    fn2 = (1, D_MODEL)
    sg, sd, sm, sv = _small_update(
        gsum, (norm_mix, norm_ffn, hgrn_lb_logits, hgrn_out_norm, final_norm.reshape(fn2)),
        (m_norm_mix, m_norm_ffn, m_hgrn_lb_logits, m_hgrn_out_norm, m_final_norm.reshape(fn2)),
        (v_norm_mix, v_norm_ffn, v_hgrn_lb_logits, v_hgrn_out_norm, v_final_norm.reshape(fn2)))

    def assemble(p, which):
        nmx, nff, lbl, onm, fnm = p
        fnm = fnm.reshape(D_MODEL)
        hin, hout, qkv, aout, fin, fdn = [updated[t][which] for t in
                                          ("hgrn_w_in", "hgrn_w_out", "attn_w_qkv", "attn_w_out", "ffn_w_in", "ffn_w_down")]
        return (nmx, nff, hin, lbl, onm, hout, qkv, aout, fin, fdn, fnm)

    total_loss = gsum[9, 0]
    return (total_loss, dx.reshape(1, S, D_MODEL), *assemble(sg, 0), *assemble(sd, 1), *assemble(sm, 2), *assemble(sv, 3))
```

```python
import functools

import jax
import jax.numpy as jnp
import numpy as np
from jax import lax
from jax.experimental import pallas as pl
from jax.experimental.pallas import tpu as pltpu

F32 = jnp.float32
BF16 = jnp.bfloat16
MESH = pl.DeviceIdType.MESH

D_MODEL = 1024
HEAD = 128
HGRN_HEADS = 8
HGRN_CHUNK = 64
HGRN_HEADS_PER_STEP = 2
ATTN_GROUPS = ((128, 1), (512, 4), (2048, 16))
ATTN_SPAN = 128
HEADS_PER_GROUP = 4
GROUP_W = HEADS_PER_GROUP * HEAD
D_FF = 2816
NORM_EPS = 1e-6
ROPE_THETA = 10000.0
NEG = -1e30

ADAM_LR, ADAM_B1, ADAM_B2, ADAM_EPS, ADAM_WD, ADAM_STEP = 0.001, 0.9, 0.999, 1e-08, 0.01, 10

N_CHIPS = 4
VMEM_LIMIT = 56 * 1024 * 1024
SMALL_ROWS = 16


def _params(sem=None):
    return pltpu.CompilerParams(dimension_semantics=sem, vmem_limit_bytes=VMEM_LIMIT)


def _row_tile(rows, cols, budget_bytes=3 * 512 * 1024):
    best = 8
    for t in range(8, rows + 1, 8):
        if rows % t == 0 and t * cols * 4 <= budget_bytes:
            best = t
    assert rows % best == 0
    return best


def _grid_corner(i, j):
    return jnp.logical_and(pl.program_id(0) == i, pl.program_id(1) == j)


def _sigmoid(v):
    return 0.5 * jnp.tanh(0.5 * v) + 0.5


def _dot(a, b):
    return jnp.dot(a, b, preferred_element_type=F32)


def _dot_nt(a, b):
    return lax.dot_general(a, b, (((1,), (1,)), ((), ())), preferred_element_type=F32)


def _dot_tn(a, b):
    return lax.dot_general(a, b, (((0,), (0,)), ((), ())), preferred_element_type=F32)


def _dot_exact(ones, b):
    ones = ones.astype(BF16)
    hi = b.astype(BF16)
    rest = b - hi.astype(F32)
    mid = rest.astype(BF16)
    low = (rest - mid.astype(F32)).astype(BF16)
    return _dot(ones, hi) + _dot(ones, mid) + _dot(ones, low)


def _rstd(v):
    return lax.rsqrt(jnp.mean(v * v, axis=-1, keepdims=True) + NORM_EPS)


def _norm_mm(h, gain, w3, name, out_dtype=F32, tm=2048, rider=None):
    S, K = h.shape
    J, _, n = w3.shape
    gi = S // tm

    def body(h_ref, g_ref, w_ref, y_ref, u_ref):
        @pl.when(pl.program_id(1) == 0)
        def _():
            v = h_ref[...]
            u_ref[...] = (v * _rstd(v) * g_ref[...]).astype(BF16)

        y_ref[...] = _dot(u_ref[...], w_ref[pl.program_id(1)]).astype(y_ref.dtype)

    r_ops, r_in, r_out, r_shape, r_scr = _rider_args(rider)
    res = pl.pallas_call(
        _ride(rider, body, 3, 2, functools.partial(_grid_corner, 0, 0), functools.partial(_grid_corner, gi - 1, J - 1)),
        name=name, grid=(gi, J),
        in_specs=[pl.BlockSpec((tm, K), lambda i, j: (i, 0)),
                  pl.BlockSpec((1, K), lambda i, j: (0, 0)),
                  pl.BlockSpec((J, K, n), lambda i, j: (0, 0, 0), pipeline_mode=pl.Buffered(1))] + r_in,
        out_specs=[pl.BlockSpec((tm, n), lambda i, j: (i, j)), pl.BlockSpec((tm, K), lambda i, j: (i, 0))] + r_out,
        out_shape=[jax.ShapeDtypeStruct((S, J * n), out_dtype), jax.ShapeDtypeStruct((S, K), BF16)] + r_shape,
        scratch_shapes=r_scr,
        compiler_params=_params(("arbitrary", "arbitrary")))(h, gain, w3, *r_ops)
    return res[0], res[1], res[2:]


def _mm_res(h, a, w2, name, tm=1024):
    S, N = h.shape
    K = a.shape[1]

    def body(h_ref, a_ref, w_ref, o_ref):
        o_ref[...] = h_ref[...] + _dot(a_ref[...], w_ref[...])

    return pl.pallas_call(
        body, name=name, grid=(S // tm,),
        in_specs=[pl.BlockSpec((tm, N), lambda i: (i, 0)),
                  pl.BlockSpec((tm, K), lambda i: (i, 0)),
                  pl.BlockSpec((K, N), lambda i: (0, 0))],
        out_specs=pl.BlockSpec((tm, N), lambda i: (i, 0)),
        out_shape=jax.ShapeDtypeStruct((S, N), F32),
        compiler_params=_params(("parallel",)))(h, a, w2)


def _swiglu(z_ref, F):
    g = z_ref[:, :F].astype(F32)
    return (g * _sigmoid(g) * z_ref[:, F:].astype(F32)).astype(BF16)


def _swiglu_mm_res(h, z, w2, name, tm=512):
    S, N = h.shape
    F = w2.shape[0]
    steps, slots = S // tm, 3
    assert steps >= slots - 1

    def body(h_ref, z_hbm, w_ref, o_ref, a_ref, z_buf, sem):
        s = pl.program_id(0)

        def fetch(step):
            slot = step % slots
            return pltpu.make_async_copy(z_hbm.at[pl.ds(step * tm, tm), :], z_buf.at[slot], sem.at[slot])

        @pl.when(s == 0)
        def _():
            for k in range(slots - 1):
                fetch(k).start()

        @pl.when(s + slots - 1 < steps)
        def _():
            fetch(s + slots - 1).start()

        fetch(s).wait()
        a = _swiglu(z_buf.at[s % slots], F)
        a_ref[...] = a
        o_ref[...] = h_ref[...] + _dot(a, w_ref[...])

    return pl.pallas_call(
        body, name=name, grid=(steps,),
        in_specs=[pl.BlockSpec((tm, N), lambda i: (i, 0)), ANY,
                  pl.BlockSpec((F, N), lambda i: (0, 0), pipeline_mode=pl.Buffered(1))],
        out_specs=[pl.BlockSpec((tm, N), lambda i: (i, 0)), pl.BlockSpec((tm, F), lambda i: (i, 0))],
        out_shape=[jax.ShapeDtypeStruct((S, N), F32), jax.ShapeDtypeStruct((S, F), BF16)],
        scratch_shapes=[pltpu.VMEM((slots, tm, 2 * F), BF16), pltpu.SemaphoreType.DMA((slots,))],
        compiler_params=_params(("arbitrary",)))(h, z, w2)


def _dy_specs(dy, J, n, tm):
    if dy.ndim == 3:
        return [pl.BlockSpec((None, tm, n), functools.partial(lambda i, j: (j, i, 0), j=j)) for j in range(J)]
    return [pl.BlockSpec((tm, n), functools.partial(lambda i, j: (i, j), j=j)) for j in range(J)]


def _acc_nt(dy_refs, w_ref):
    acc = None
    for j, r in enumerate(dy_refs):
        t = _dot_nt(r[...].astype(BF16), w_ref[j])
        acc = t if acc is None else acc + t
    return acc


def _mm_nt(dy, w3, name, out_dtype=F32, tm=1024):
    J, K, n = w3.shape
    S = dy.shape[-2]

    def body(*refs):
        dy_refs, w_ref, o_ref = refs[:J], refs[J], refs[J + 1]
        o_ref[...] = _acc_nt(dy_refs, w_ref).astype(o_ref.dtype)

    return pl.pallas_call(
        body, name=name, grid=(S // tm,),
        in_specs=_dy_specs(dy, J, n, tm) + [pl.BlockSpec((J, K, n), lambda i: (0, 0, 0))],
        out_specs=pl.BlockSpec((tm, K), lambda i: (i, 0)),
        out_shape=jax.ShapeDtypeStruct((S, K), out_dtype),
        compiler_params=_params(("parallel",)))(*([dy] * J), w3)


def _mm_nt_normbwd(dy, w3, h, gain, dh, name, tm=512, rider=None):
    J, K, n = w3.shape
    S = h.shape[0]
    steps = S // tm

    def body(*refs):
        dy_refs, w_ref, h_ref, g_ref, dh_ref, o_ref, dg_ref = refs[:J], *refs[J:]
        du = _acc_nt(dy_refs, w_ref)
        v = h_ref[...]
        r = _rstd(v)
        xh = v * r
        dyg = du * g_ref[...]
        o_ref[...] = dh_ref[...] + r * (dyg - xh * jnp.mean(dyg * xh, axis=-1, keepdims=True))

        @pl.when(pl.program_id(0) == 0)
        def _():
            dg_ref[...] = jnp.zeros_like(dg_ref)

        dg_ref[...] += jnp.sum(du * xh, axis=0, keepdims=True)

    row = pl.BlockSpec((tm, K), lambda i: (i, 0))
    vec = pl.BlockSpec((1, K), lambda i: (0, 0))
    r_ops, r_in, r_out, r_shape, r_scr = _rider_args(rider)
    res = pl.pallas_call(
        _ride(rider, body, J + 4, 2, lambda: pl.program_id(0) == 0, lambda: pl.program_id(0) == steps - 1),
        name=name, grid=(steps,),
        in_specs=_dy_specs(dy, J, n, tm) + [pl.BlockSpec((J, K, n), lambda i: (0, 0, 0)), row, vec, row] + r_in,
        out_specs=[row, vec] + r_out,
        out_shape=[jax.ShapeDtypeStruct((S, K), F32), jax.ShapeDtypeStruct((1, K), F32)] + r_shape,
        scratch_shapes=r_scr,
        compiler_params=_params(("arbitrary",)))(*([dy] * J), w3, h, gain, dh, *r_ops)
    return res[0], res[1], res[2:]


def _mm_nt_swiglu_bwd(dh, w2, z, name, tm=512, chunks=11):
    F, N = w2.shape
    S = dh.shape[0]
    fc = F // chunks
    assert fc * chunks == F and fc % HEAD == 0

    def body(dh_ref, w_ref, z_ref, o_ref):
        dhb = dh_ref[...].astype(BF16)
        da = [_dot_nt(dhb, w_ref[c * fc:(c + 1) * fc, :]) for c in range(chunks)]
        for c in range(chunks):
            g = z_ref[:, c * fc:(c + 1) * fc].astype(F32)
            u = z_ref[:, F + c * fc:F + (c + 1) * fc].astype(F32)
            sg = _sigmoid(g)
            o_ref[:, c * fc:(c + 1) * fc] = (da[c] * u * (sg * (1.0 + g * (1.0 - sg)))).astype(BF16)
            o_ref[:, F + c * fc:F + (c + 1) * fc] = (da[c] * (g * sg)).astype(BF16)

    return pl.pallas_call(
        body, name=name, grid=(S // tm,),
        in_specs=[pl.BlockSpec((tm, N), lambda i: (i, 0)),
                  pl.BlockSpec((F, N), lambda i: (0, 0), pipeline_mode=pl.Buffered(1)),
                  pl.BlockSpec((tm, 2 * F), lambda i: (i, 0))],
        out_specs=pl.BlockSpec((tm, 2 * F), lambda i: (i, 0)),
        out_shape=jax.ShapeDtypeStruct((S, 2 * F), BF16),
        compiler_params=_params(("parallel",)))(dh, w2, z)


def _mm_tn(x, dy, J, n, tn, name):
    tpn = n // tn
    ts = 2048 if x.shape[1] <= 1536 else 1024
    S, K = x.shape
    if dy.ndim == 3:
        dy_spec = pl.BlockSpec((None, ts, tn), lambda c, s: (c // tpn, s, c % tpn))
    else:
        dy_spec = pl.BlockSpec((ts, tn), lambda c, s: (s, c))

    def body(x_ref, dy_ref, o_ref):
        @pl.when(pl.program_id(1) == 0)
        def _():
            o_ref[...] = jnp.zeros_like(o_ref)

        o_ref[...] += _dot_tn(x_ref[...], dy_ref[...].astype(BF16))

    return pl.pallas_call(
        body, name=name, grid=(J * tpn, S // ts),
        in_specs=[pl.BlockSpec((ts, K), lambda c, s: (s, 0)), dy_spec],
        out_specs=pl.BlockSpec((None, K, tn), lambda c, s: (c // tpn, 0, c % tpn)),
        out_shape=jax.ShapeDtypeStruct((J, K, n), F32),
        compiler_params=_params(("parallel", "arbitrary")))(x, dy)


def _loss_head(h, gain, target, tm=1024):
    S, K = h.shape
    steps, slots = S // tm, 3
    assert steps >= slots - 1

    def body(h_hbm, g_ref, t_hbm, dh_ref, loss_ref, dg_ref, h_buf, t_buf, sem):
        s = pl.program_id(0)

        def fetch(step):
            slot, rows = step % slots, pl.ds(step * tm, tm)
            return [pltpu.make_async_copy(h_hbm.at[rows, :], h_buf.at[slot], sem.at[0, slot]),
                    pltpu.make_async_copy(t_hbm.at[rows, :], t_buf.at[slot], sem.at[1, slot])]

        @pl.when(s == 0)
        def _():
            for k in range(slots - 1):
                for cp in fetch(k):
                    cp.start()

        @pl.when(s + slots - 1 < steps)
        def _():
            for cp in fetch(s + slots - 1):
                cp.start()

        for cp in fetch(s):
            cp.wait()
        v = h_buf[s % slots]
        r = _rstd(v)
        xh = v * r
        g = g_ref[...]
        dy = (xh * g - t_buf[s % slots]) * (1.0 / K)
        dyg = dy * g
        dh_ref[...] = r * (dyg - xh * jnp.mean(dyg * xh, axis=-1, keepdims=True))

        @pl.when(pl.program_id(0) == 0)
        def _():
            loss_ref[...] = jnp.zeros_like(loss_ref)
            dg_ref[...] = jnp.zeros_like(dg_ref)

        part = jnp.sum(jnp.sum(dy * dy, axis=-1, keepdims=True), axis=0, keepdims=True) * (0.5 * K)
        lane = lax.broadcasted_iota(jnp.int32, loss_ref.shape, 1)
        loss_ref[...] += jnp.where(lane == 0, part, 0.0)
        dg_ref[...] += jnp.sum(dy * xh, axis=0, keepdims=True)

    row = pl.BlockSpec((tm, K), lambda i: (i, 0))
    vec = pl.BlockSpec((1, K), lambda i: (0, 0))
    return pl.pallas_call(
        body, name="loss_head", grid=(steps,),
        in_specs=[ANY, vec, ANY],
        out_specs=[row, pl.BlockSpec((1, HEAD), lambda i: (0, 0)), vec],
        out_shape=[jax.ShapeDtypeStruct((S, K), F32), jax.ShapeDtypeStruct((1, HEAD), F32),
                   jax.ShapeDtypeStruct((1, K), F32)],
        scratch_shapes=[pltpu.VMEM((slots, tm, K), F32)] * 2 + [pltpu.SemaphoreType.DMA((2, slots))],
        compiler_params=_params(("arbitrary",)))(h, gain, target)


def _lower_bound(lg_ref):
    l0, l1, l2 = lg_ref[0:1, :], lg_ref[1:2, :], lg_ref[2:3, :]
    mx = jnp.maximum(jnp.maximum(l0, l1), l2)
    e0, e1, e2 = jnp.exp(l0 - mx), jnp.exp(l1 - mx), jnp.exp(l2 - mx)
    return e0 / (e0 + e1 + e2)


def _chunks(v, ncb):
    C = HGRN_CHUNK
    return [v[c * C:(c + 1) * C] for c in range(ncb)]


def _rows(parts):
    return jnp.concatenate(parts, axis=0)


def _block_gates(qz, fz, lb, ncb):
    C = HGRN_CHUNK
    row = lax.broadcasted_iota(jnp.int32, (C, C), 0)
    col = lax.broadcasted_iota(jnp.int32, (C, C), 1)
    tri = (col <= row).astype(F32)
    first_half = lax.broadcasted_iota(jnp.int32, (C, HEAD), 0) < C // 2
    sig = _sigmoid(fz)
    fg = lb + (1.0 - lb) * sig
    key = 1.0 - fg
    lg = jnp.log(fg)
    lgs = _chunks(lg, ncb)
    b = _rows([_dot_exact(tri, v) for v in lgs])
    r_c = [jnp.sum(jnp.where(first_half, v, 0.0), axis=0, keepdims=True) for v in lgs]
    bl_c = [jnp.sum(v, axis=0, keepdims=True) for v in lgs]
    r = _rows([jnp.broadcast_to(v, (C, HEAD)) for v in r_c])
    e_br, e_rb = jnp.exp(b - r), jnp.exp(r - b)
    e_b = e_br * _rows([jnp.broadcast_to(jnp.exp(v), (C, HEAD)) for v in r_c])
    e_lb = e_rb * _rows([jnp.broadcast_to(jnp.exp(e - v), (C, HEAD)) for e, v in zip(bl_c, r_c)])
    sq = _sigmoid(qz)
    qy = qz * sq
    return sig, fg, key, (e_br, e_rb, e_b, e_lb), bl_c, sq, qy


def _hgrn_fwd(proj, logits, gain, tb=1024, rider=None):
    S = proj.shape[0]
    H, C = HGRN_HEADS, HGRN_CHUNK
    ncb = tb // C

    def one_head(q_ref, f_ref, i_ref, g_ref, lg_ref, gn_ref, o_ref, og_ref, st_ref, state):
        @pl.when(pl.program_id(1) == 0)
        def _():
            state[...] = jnp.zeros_like(state)

        lb = _lower_bound(lg_ref)
        causal = lax.broadcasted_iota(jnp.int32, (C, C), 1) <= lax.broadcasted_iota(jnp.int32, (C, C), 0)
        qz, fz, gz = q_ref[...], f_ref[...], g_ref[...]
        _, _, key, (e_br, e_rb, e_b, e_lb), bl_c, _, qy = _block_gates(qz, fz, lb, ncb)
        qs = _chunks((qy * e_br).astype(BF16), ncb)
        ks = _chunks((key * e_rb).astype(BF16), ncb)
        qb = _chunks((qy * e_b).astype(BF16), ncb)
        ke = _chunks((key * e_lb).astype(BF16), ncb)
        vb = _chunks(i_ref[...].astype(BF16), ncb)
        a = [jnp.where(causal, _dot_nt(qs[c], ks[c]), 0.0).astype(BF16) for c in range(ncb)]
        upd = [_dot_tn(vb[c], ke[c]) for c in range(ncb)]
        o_intra = [_dot(a[c], vb[c]) for c in range(ncb)]
        st = state[...]
        e_l = [jnp.exp(v) for v in bl_c]
        sts = []
        for c in range(ncb):
            sts.append(st)
            st = st * e_l[c] + upd[c]
        state[...] = st
        for c in range(ncb):
            st_ref[c] = sts[c]
        o = _rows([_dot_nt(qb[c], sts[c].astype(BF16)) + o_intra[c] for c in range(ncb)])
        o_ref[...] = o
        og_ref[...] = ((o * _rstd(o) * gn_ref[...]) * (gz * _sigmoid(gz))).astype(BF16)

    def body(q_ref, f_ref, i_ref, g_ref, lg_ref, gn_ref, o_ref, og_ref, st_ref, state):
        for hs in range(HP):
            cols = slice(hs * HEAD, (hs + 1) * HEAD)
            one_head(q_ref.at[:, cols], f_ref.at[:, cols], i_ref.at[:, cols], g_ref.at[:, cols], lg_ref.at[:, cols],
                     gn_ref, o_ref.at[:, cols], og_ref.at[:, cols], st_ref.at[hs], state.at[hs])

    HP, wide = HGRN_HEADS_PER_STEP, HGRN_HEADS_PER_STEP * HEAD
    hg = H // HP

    def part(p):
        return pl.BlockSpec((tb, wide), functools.partial(lambda h, i, p: (i, p * hg + h), p=p))

    nb = S // tb
    r_ops, r_in, r_out, r_shape, r_scr = _rider_args(rider)
    res = pl.pallas_call(
        _ride(rider, body, 6, 3, functools.partial(_grid_corner, 0, 0), functools.partial(_grid_corner, hg - 1, nb - 1)),
        name="hgrn_fwd", grid=(hg, nb),
        in_specs=[part(0), part(1), part(2), part(3),
                  pl.BlockSpec((3, wide), lambda h, i: (0, h)),
                  pl.BlockSpec((1, HEAD), lambda h, i: (0, 0))] + r_in,
        out_specs=[pl.BlockSpec((tb, wide), lambda h, i: (i, h)),
                   pl.BlockSpec((tb, wide), lambda h, i: (i, h)),
                   pl.BlockSpec((HP, ncb, HEAD, HEAD), lambda h, i: (h, i, 0, 0))] + r_out,
        out_shape=[jax.ShapeDtypeStruct((S, H * HEAD), F32),
                   jax.ShapeDtypeStruct((S, H * HEAD), BF16),
                   jax.ShapeDtypeStruct((H, S // C, HEAD, HEAD), F32)] + r_shape,
        scratch_shapes=[pltpu.VMEM((HP, HEAD, HEAD), F32)] + r_scr,
        compiler_params=_params(("arbitrary", "arbitrary")))(proj, proj, proj, proj, logits, gain, *r_ops)
    return res[:3], res[3:]


def _hgrn_bwd(proj, logits, gain, o, states, dog, tb=1024, rider=None):
    S = proj.shape[0]
    H, C = HGRN_HEADS, HGRN_CHUNK
    ncb = tb // C
    nb = S // tb

    def one_head(q_ref, f_ref, i_ref, g_ref, lg_ref, gn_ref, o_ref, st_ref, dog_ref,
                 dp_ref, dlb_ref, dgn_ref, dstate, dst_scr):
        @pl.when(pl.program_id(1) == 0)
        def _():
            dstate[...] = jnp.zeros_like(dstate)
            dlb_ref[...] = jnp.zeros_like(dlb_ref)
            dgn_ref[...] = jnp.zeros_like(dgn_ref)

        lb = _lower_bound(lg_ref)
        oml = 1.0 - lb
        gn = gn_ref[...]
        row = lax.broadcasted_iota(jnp.int32, (C, C), 0)
        col = lax.broadcasted_iota(jnp.int32, (C, C), 1)
        causal = col <= row
        tri_up = (col >= row).astype(F32)
        qz, fz, gz = q_ref[...], f_ref[...], g_ref[...]
        sig, fg, key, (e_br, e_rb, e_b, e_lb), bl_c, sq, qy = _block_gates(qz, fz, lb, ncb)
        qs_v, ks_v = (qy * e_br).astype(BF16), (key * e_rb).astype(BF16)
        qb_v, ke_v = (qy * e_b).astype(BF16), (key * e_lb).astype(BF16)
        qs, ks, qb, ke = _chunks(qs_v, ncb), _chunks(ks_v, ncb), _chunks(qb_v, ncb), _chunks(ke_v, ncb)
        vb = _chunks(i_ref[...].astype(BF16), ncb)
        ov = o_ref[...]
        rs = _rstd(ov)
        xh = ov * rs
        sg = _sigmoid(gz)
        dog_v = dog_ref[...]
        dgz = dog_v * (xh * gn) * (sg * (1.0 + gz * (1.0 - sg)))
        don = dog_v * (gz * sg)
        dgn_ref[...] += jnp.sum(don * xh, axis=0, keepdims=True)
        dyg = don * gn
        do = rs * (dyg - xh * jnp.mean(dyg * xh, axis=-1, keepdims=True))
        dob = _chunks(do.astype(BF16), ncb)
        CH = range(ncb)
        a = [jnp.where(causal, _dot_nt(qs[c], ks[c]), 0.0).astype(BF16) for c in CH]
        da = [jnp.where(causal, _dot_nt(dob[c], vb[c]), 0.0).astype(BF16) for c in CH]
        wst = [_dot_tn(dob[c], qb[c]) for c in CH]
        dv_in = [_dot_tn(a[c], dob[c]) for c in CH]
        dqs = [_dot(da[c], ks[c]) for c in CH]
        dks = [_dot_tn(da[c], qs[c]) for c in CH]
        e_l = [jnp.exp(v) for v in bl_c]
        dst = dstate[...]
        for c in reversed(range(ncb)):
            dst_scr[c] = dst
            dst = wst[c] + dst * e_l[c]
        dstate[...] = dst
        dst1b = [dst_scr[c].astype(BF16) for c in CH]
        dqb = [_dot(dob[c], st_ref[c].astype(BF16)) for c in CH]
        dke = [_dot(vb[c], dst1b[c]) for c in CH]
        dv = [dv_in[c] + _dot_nt(ke[c], dst1b[c]) for c in CH]
        dbl_st = [jnp.sum(dst_scr[c] * st_ref[c], axis=0, keepdims=True) * e_l[c] for c in CH]
        dqs, dks, dqb, dke, dv = _rows(dqs), _rows(dks), _rows(dqb), _rows(dke), _rows(dv)
        dke_ke = dke * ke_v.astype(F32)
        db = dqs * qs_v.astype(F32) - dks * ks_v.astype(F32) + dqb * qb_v.astype(F32) - dke_ke
        dlg = []
        for c, (db_c, kk_c) in enumerate(zip(_chunks(db, ncb), _chunks(dke_ke, ncb))):
            dbl = jnp.sum(kk_c, axis=0, keepdims=True) + dbl_st[c]
            dlg.append(_dot_exact(tri_up, db_c) + dbl)
        dlg = _rows(dlg)
        dkey = dks * e_rb + dke * e_lb
        dqy = dqs * e_br + dqb * e_b
        dfg = dlg / fg - dkey
        dlb_ref[...] += jnp.sum(dfg * (1.0 - sig), axis=0, keepdims=True)
        dp_ref[0] = (dqy * (sq * (1.0 + qz * (1.0 - sq)))).astype(BF16)
        dp_ref[1] = (dfg * oml * sig * (1.0 - sig)).astype(BF16)
        dp_ref[2] = dv.astype(BF16)
        dp_ref[3] = dgz.astype(BF16)

    def body(q_ref, f_ref, i_ref, g_ref, lg_ref, gn_ref, o_ref, st_ref, dog_ref,
             dp_ref, dlb_ref, dgn_ref, dstate, dst_scr):
        for hs in range(HP):
            cols = slice(hs * HEAD, (hs + 1) * HEAD)
            one_head(q_ref.at[:, cols], f_ref.at[:, cols], i_ref.at[:, cols], g_ref.at[:, cols], lg_ref.at[:, cols],
                     gn_ref, o_ref.at[:, cols], st_ref.at[hs], dog_ref.at[:, cols],
                     dp_ref.at[:, :, cols], dlb_ref.at[hs], dgn_ref.at[hs], dstate.at[hs], dst_scr)

    HP, wide = HGRN_HEADS_PER_STEP, HGRN_HEADS_PER_STEP * HEAD
    hg = H // HP

    def part(p):
        return pl.BlockSpec((tb, wide), functools.partial(lambda h, i, p: (nb - 1 - i, p * hg + h), p=p))

    blk = pl.BlockSpec((tb, wide), lambda h, i: (nb - 1 - i, h))
    acc = pl.BlockSpec((HP, 1, HEAD), lambda h, i: (h, 0, 0))
    r_ops, r_in, r_out, r_shape, r_scr = _rider_args(rider)
    res = pl.pallas_call(
        _ride(rider, body, 9, 3, functools.partial(_grid_corner, 0, 0), functools.partial(_grid_corner, hg - 1, nb - 1)),
        name="hgrn_bwd", grid=(hg, nb),
        in_specs=[part(0), part(1), part(2), part(3),
                  pl.BlockSpec((3, wide), lambda h, i: (0, h)),
                  pl.BlockSpec((1, HEAD), lambda h, i: (0, 0)),
                  blk,
                  pl.BlockSpec((HP, ncb, HEAD, HEAD), lambda h, i: (h, nb - 1 - i, 0, 0)),
                  blk] + r_in,
        out_specs=[pl.BlockSpec((4, tb, wide), lambda h, i: (0, nb - 1 - i, h)), acc, acc] + r_out,
        out_shape=[jax.ShapeDtypeStruct((4, S, H * HEAD), BF16),
                   jax.ShapeDtypeStruct((H, 1, HEAD), F32),
                   jax.ShapeDtypeStruct((H, 1, HEAD), F32)] + r_shape,
        scratch_shapes=[pltpu.VMEM((HP, HEAD, HEAD), F32), pltpu.VMEM((ncb, HEAD, HEAD), F32)] + r_scr,
        compiler_params=_params(("arbitrary", "arbitrary")))(
            proj, proj, proj, proj, logits, gain, o, states, dog, *r_ops)
    return res[:3], res[3:]


def _rope(v, cos, sin):
    return v * cos + pltpu.roll(v, HEAD // 2, 1) * sin


def _lane_pick(tile, hh):
    lane = lax.broadcasted_iota(jnp.int32, tile.shape, 1)
    return jnp.sum(jnp.where(lane == hh, tile, 0.0), axis=-1, keepdims=True)


def _lane_place(cols):
    rows = cols[0].shape[0]
    lane = lax.broadcasted_iota(jnp.int32, (rows, HEAD), 1)
    tile = jnp.zeros((rows, HEAD), F32)
    for hh, v in enumerate(cols):
        tile = jnp.where(lane == hh, v, tile)
    return tile


def _band_masks():
    qi = lax.broadcasted_iota(jnp.int32, (ATTN_SPAN, ATTN_SPAN), 0)
    kj = lax.broadcasted_iota(jnp.int32, (ATTN_SPAN, ATTN_SPAN), 1)
    return kj <= qi, kj >= qi


ATTN_TILE_BLOCKS = 8


def _attn_fwd(a):
    d, L, _ = a.shape
    B, W = min(ATTN_TILE_BLOCKS, a.shape[1] // ATTN_SPAN), ATTN_SPAN
    T = B * W
    assert L % T == 0
    steps = L // T
    scale = HEAD ** -0.5

    def body(q_ref, kc_ref, kp_ref, vc_ref, vp_ref, o_ref, lse_ref):
        n = pl.program_id(1)
        mask_c, mask_p0 = _band_masks()
        first = jnp.logical_and(mask_p0, n > 0)
        units = [(b, hh) for b in range(B) for hh in range(HEADS_PER_GROUP)]
        rows = [slice(b * W, (b + 1) * W) for b in range(B)]
        cols = [slice(hh * HEAD, (hh + 1) * HEAD) for hh in range(HEADS_PER_GROUP)]

        def prev_keys(ref, tile, b, hh):
            return ref[:, cols[hh]] if b == 0 else tile[rows[b - 1], cols[hh]]

        s_c = [jnp.where(mask_c, _dot_nt(q_ref[rows[b], cols[hh]], kc_ref[rows[b], cols[hh]]) * scale, NEG) for b, hh in units]
        s_p = [jnp.where(first if b == 0 else mask_p0,
                         _dot_nt(q_ref[rows[b], cols[hh]], prev_keys(kp_ref, kc_ref, b, hh)) * scale, NEG) for b, hh in units]
        m = [jnp.maximum(jnp.max(x, axis=-1, keepdims=True), jnp.max(y, axis=-1, keepdims=True)) for x, y in zip(s_c, s_p)]
        p_c = [jnp.exp(x - mm) for x, mm in zip(s_c, m)]
        p_p = [jnp.exp(y - mm) for y, mm in zip(s_p, m)]
        l = [jnp.sum(x, axis=-1, keepdims=True) + jnp.sum(y, axis=-1, keepdims=True) for x, y in zip(p_c, p_p)]
        acc = [_dot(p_c[i].astype(BF16), vc_ref[rows[b], cols[hh]]) + _dot(p_p[i].astype(BF16), prev_keys(vp_ref, vc_ref, b, hh))
               for i, (b, hh) in enumerate(units)]
        for i, (b, hh) in enumerate(units):
            o_ref[rows[b], cols[hh]] = (acc[i] / l[i]).astype(BF16)
        for b in range(B):
            lse_ref[rows[b], :] = _lane_place([m[i] + jnp.log(l[i]) for i, (bb, _) in enumerate(units) if bb == b])

    def cur(part):
        return pl.BlockSpec((None, T, GROUP_W), functools.partial(lambda r, n, p: (r, n, p), p=part))

    def prev(part):
        return pl.BlockSpec((None, W, GROUP_W), functools.partial(lambda r, n, p: (r, jnp.maximum(n * B - 1, 0), p), p=part))

    return pl.pallas_call(
        body, name=f"attn_fwd_d{d}", grid=(d, steps),
        in_specs=[cur(0), cur(1), prev(1), cur(2), prev(2)],
        out_specs=[pl.BlockSpec((None, T, GROUP_W), lambda r, n: (r, n, 0)), pl.BlockSpec((None, T, HEAD), lambda r, n: (r, n, 0))],
        out_shape=[jax.ShapeDtypeStruct((d, L, GROUP_W), BF16), jax.ShapeDtypeStruct((d, L, HEAD), F32)],
        compiler_params=_params(("parallel", "arbitrary")))(a, a, a, a, a)


def _attn_bwd(a, do, lse, dd):
    d, L, _ = a.shape
    B, W = min(ATTN_TILE_BLOCKS, a.shape[1] // ATTN_SPAN), ATTN_SPAN
    T = B * W
    assert L % T == 0
    steps = L // T
    scale = HEAD ** -0.5

    def body(qc_ref, qn_ref, kp_ref, kc_ref, vp_ref, vc_ref, doc_ref, don_ref, lc_ref, ln_ref, ddc_ref, ddn_ref, da_ref):
        n = pl.program_id(1)
        mask_c, mask_p0 = _band_masks()
        first = jnp.logical_and(mask_p0, n > 0)
        last = jnp.logical_and(mask_p0, n < steps - 1)
        H4 = range(HEADS_PER_GROUP)
        units = [(b, hh) for b in range(B) for hh in H4]
        rows = [slice(b * W, (b + 1) * W) for b in range(B)]
        cols = [slice(hh * HEAD, (hh + 1) * HEAD) for hh in H4]
        q = {u: qc_ref[rows[u[0]], cols[u[1]]] for u in units}
        k = {u: kc_ref[rows[u[0]], cols[u[1]]] for u in units}
        v = {u: vc_ref[rows[u[0]], cols[u[1]]] for u in units}
        g_o = {u: doc_ref[rows[u[0]], cols[u[1]]] for u in units}
        kb = {(b, hh): kp_ref[:, cols[hh]] if b == 0 else k[(b - 1, hh)] for b, hh in units}
        vb = {(b, hh): vp_ref[:, cols[hh]] if b == 0 else v[(b - 1, hh)] for b, hh in units}
        lse_t = {(b, hh): _lane_pick(lc_ref[rows[b], :], hh) for b, hh in units}
        dd_t = {(b, hh): _lane_pick(ddc_ref[rows[b], :], hh) for b, hh in units}
        p_c = {u: jnp.where(mask_c, jnp.exp(_dot_nt(q[u], k[u]) * scale - lse_t[u]), 0.0) for u in units}
        p_p = {u: jnp.where(first if u[0] == 0 else mask_p0, jnp.exp(_dot_nt(q[u], kb[u]) * scale - lse_t[u]), 0.0) for u in units}
        ds_c = {u: (p_c[u] * (_dot_nt(g_o[u], v[u]) + dd_t[u])).astype(BF16) for u in units}
        ds_p = {u: (p_p[u] * (_dot_nt(g_o[u], vb[u]) + dd_t[u])).astype(BF16) for u in units}
        qn = [qn_ref[:, c] for c in cols]
        g_n = [don_ref[:, c] for c in cols]
        p_n = [jnp.where(last, jnp.exp(_dot_nt(qn[hh], k[(B - 1, hh)]) * scale - _lane_pick(ln_ref[...], hh)), 0.0) for hh in H4]
        ds_n = [(p_n[hh] * (_dot_nt(g_n[hh], v[(B - 1, hh)]) + _lane_pick(ddn_ref[...], hh))).astype(BF16) for hh in H4]
        dq = {u: (_dot(ds_c[u], k[u]) + _dot(ds_p[u], kb[u])) * scale for u in units}
        dk, dv = {}, {}
        for b, hh in units:
            if b < B - 1:
                nxt = (b + 1, hh)
                dk[(b, hh)] = (_dot_tn(ds_c[(b, hh)], q[(b, hh)]) + _dot_tn(ds_p[nxt], q[nxt])) * scale
                dv[(b, hh)] = _dot_tn(p_c[(b, hh)].astype(BF16), g_o[(b, hh)]) + _dot_tn(p_p[nxt].astype(BF16), g_o[nxt])
            else:
                dk[(b, hh)] = (_dot_tn(ds_c[(b, hh)], q[(b, hh)]) + _dot_tn(ds_n[hh], qn[hh])) * scale
                dv[(b, hh)] = _dot_tn(p_c[(b, hh)].astype(BF16), g_o[(b, hh)]) + _dot_tn(p_n[hh].astype(BF16), g_n[hh])
        for b, hh in units:
            da_ref[rows[b], cols[hh]] = dq[(b, hh)].astype(BF16)
            da_ref[rows[b], GROUP_W + hh * HEAD:GROUP_W + (hh + 1) * HEAD] = dk[(b, hh)].astype(BF16)
            da_ref[rows[b], 2 * GROUP_W + hh * HEAD:2 * GROUP_W + (hh + 1) * HEAD] = dv[(b, hh)].astype(BF16)

    nb = L // W

    def cur(width, part):
        return pl.BlockSpec((None, T, width), functools.partial(lambda r, n, p: (r, n, p), p=part))

    def prev(width, part):
        return pl.BlockSpec((None, W, width), functools.partial(lambda r, n, p: (r, jnp.maximum(n * B - 1, 0), p), p=part))

    def nxt(width, part):
        return pl.BlockSpec((None, W, width), functools.partial(lambda r, n, p: (r, jnp.minimum(n * B + B, nb - 1), p), p=part))

    g = GROUP_W
    return pl.pallas_call(
        body, name=f"attn_bwd_d{d}", grid=(d, steps),
        in_specs=[cur(g, 0), nxt(g, 0), prev(g, 1), cur(g, 1), prev(g, 2), cur(g, 2),
                  cur(g, 0), nxt(g, 0), cur(HEAD, 0), nxt(HEAD, 0), cur(HEAD, 0), nxt(HEAD, 0)],
        out_specs=pl.BlockSpec((None, T, 3 * g), lambda r, n: (r, n, 0)),
        out_shape=jax.ShapeDtypeStruct((d, L, 3 * g), BF16),
        compiler_params=_params(("parallel", "arbitrary")))(
            a, a, a, a, a, a, do, do, lse, lse, dd, dd)


def _softmax3(ls):
    mx = jnp.maximum(jnp.maximum(ls[0], ls[1]), ls[2])
    es = [jnp.exp(v - mx) for v in ls]
    tot = es[0] + es[1] + es[2]
    return [e / tot for e in es]


HEAD_COLS = [slice(hh * HEAD, (hh + 1) * HEAD) for hh in range(HEADS_PER_GROUP)]


def _group_spec(d, tm):
    return pl.BlockSpec((d, tm // d, GROUP_W), lambda i: (0, i, 0))


def _gather_heads(ref, scr, d, tm):
    if d == 1:
        return [ref[0, :, cols].astype(F32) for cols in HEAD_COLS]
    for hh, cols in enumerate(HEAD_COLS):
        for r in range(d):
            scr.at[hh][pl.ds(r, tm // d, stride=d), :] = ref[r, :, cols].astype(F32)
    return [scr[hh] for hh in range(HEADS_PER_GROUP)]


def _tile_spec(d, tm):
    return pl.BlockSpec((d, tm // d, HEAD), lambda i: (0, i, 0))


def _gather_tile(ref, scr, d, tm):
    if d == 1:
        return ref[0]
    for r in range(d):
        scr[pl.ds(r, tm // d, stride=d), :] = ref[r]
    return scr[...]


def _scatter_tile(val, scr, ref, d, tm):
    if d == 1:
        ref[0] = val
        return
    scr[...] = val
    for r in range(d):
        ref[r] = scr[pl.ds(r, tm // d, stride=d), :]


def _scatter_heads(vals, scr, ref, d, tm):
    if d == 1:
        for cols, v in zip(HEAD_COLS, vals):
            ref[0, :, cols] = v.astype(ref.dtype)
        return
    for hh, v in enumerate(vals):
        scr[hh] = v
    for hh, cols in enumerate(HEAD_COLS):
        for r in range(d):
            ref[r, :, cols] = scr.at[hh][pl.ds(r, tm // d, stride=d), :].astype(ref.dtype)


def _qkv_dilated(h, gain, w4, gi, cos, sin, d, tm=2048):
    S, K = h.shape
    n_shard = w4.shape[2]
    assert n_shard % HEAD == 0

    def head_cols(hh):
        def index(i, p):
            c = p * (len(ATTN_GROUPS) * GROUP_W) + gi * GROUP_W + hh * HEAD
            return c // n_shard, 0, (c % n_shard) // HEAD
        return pl.BlockSpec((None, K, HEAD), index)

    def body(h_ref, g_ref, *refs):
        w_refs, (cos_ref, sin_ref, out_ref, u_ref, y_scr) = refs[:HEADS_PER_GROUP], refs[HEADS_PER_GROUP:]
        p = pl.program_id(1)

        @pl.when(p == 0)
        def _():
            v = h_ref[...]
            u_ref[...] = (v * _rstd(v) * g_ref[...]).astype(BF16)

        y = _dot(u_ref[...], jnp.concatenate([r[...] for r in w_refs], axis=1))
        heads = [slice(hh * HEAD, (hh + 1) * HEAD) for hh in range(HEADS_PER_GROUP)]
        if d > 1:
            for hh, cols in enumerate(heads):
                y_scr[hh] = y[:, cols]

        def rows_of(hh, r):
            return y[:, heads[hh]] if d == 1 else y_scr.at[hh][pl.ds(r, tm // d, stride=d), :]

        @pl.when(p < 2)
        def _():
            for r in range(d):
                rows = slice(None) if d == 1 else pl.ds(r, tm // d, stride=d)
                cr, sr = cos_ref[rows, :], sin_ref[rows, :]
                for hh, cols in enumerate(heads):
                    out_ref[r, :, cols] = _rope(rows_of(hh, r), cr, sr).astype(BF16)

        @pl.when(p == 2)
        def _():
            for r in range(d):
                for hh, cols in enumerate(heads):
                    out_ref[r, :, cols] = rows_of(hh, r).astype(BF16)

    tab = pl.BlockSpec((tm, HEAD), lambda i, p: (i, 0))
    return pl.pallas_call(
        body, name=f"attn_qkv_d{d}", grid=(S // tm, 3),
        in_specs=[pl.BlockSpec((tm, K), lambda i, p: (i, 0)),
                  pl.BlockSpec((1, K), lambda i, p: (0, 0)),
                  *[head_cols(hh) for hh in range(HEADS_PER_GROUP)], tab, tab],
        out_specs=[pl.BlockSpec((d, tm // d, GROUP_W), lambda i, p: (0, i, p)), pl.BlockSpec((tm, K), lambda i, p: (i, 0))],
        out_shape=[jax.ShapeDtypeStruct((d, S // d, 3 * GROUP_W), BF16), jax.ShapeDtypeStruct((S, K), BF16)],
        scratch_shapes=[pltpu.VMEM((HEADS_PER_GROUP, tm, HEAD), F32)],
        compiler_params=_params(("parallel", "arbitrary")))(h, gain, *[w4] * HEADS_PER_GROUP, cos, sin)


def _undilate_group(da, dqkv, cos, sin, g, tm=2048):
    d, L, _ = da.shape
    S = d * L
    G = len(ATTN_GROUPS)

    def body(*refs):
        da_ref, cos_ref, sin_ref, out_ref, scr = refs[0], refs[1], refs[2], refs[-2], refs[-1]
        p = pl.program_id(1)
        heads = [slice(hh * HEAD, (hh + 1) * HEAD) for hh in range(HEADS_PER_GROUP)]
        if d > 1:
            for hh, cols in enumerate(heads):
                for r in range(d):
                    scr.at[hh][pl.ds(r, tm // d, stride=d), :] = da_ref[r, :, cols].astype(F32)

        def tokens(hh):
            return da_ref[0, :, heads[hh]].astype(F32) if d == 1 else scr[hh]

        @pl.when(p < 2)
        def _():
            cr, sr = cos_ref[...], -sin_ref[...]
            for hh, cols in enumerate(heads):
                out_ref[:, cols] = _rope(tokens(hh), cr, sr).astype(BF16)

        @pl.when(p == 2)
        def _():
            for hh, cols in enumerate(heads):
                out_ref[:, cols] = tokens(hh).astype(BF16)

    tab = pl.BlockSpec((tm, HEAD), lambda i, p: (i, 0))
    operands = (da, cos, sin) if dqkv is None else (da, cos, sin, dqkv)
    return pl.pallas_call(
        body, name=f"attn_undilate_d{d}", grid=(S // tm, 3),
        in_specs=[pl.BlockSpec((d, tm // d, GROUP_W), lambda i, p: (0, i, p)), tab, tab] + ([] if dqkv is None else [ANY]),
        out_specs=pl.BlockSpec((tm, GROUP_W), lambda i, p: (i, p * G + g)),
        out_shape=jax.ShapeDtypeStruct((S, 3 * G * GROUP_W), BF16),
        input_output_aliases={} if dqkv is None else {3: 0},
        scratch_shapes=[pltpu.VMEM((HEADS_PER_GROUP, tm, HEAD), F32)],
        compiler_params=_params(("parallel", "arbitrary")))(*operands)


def _attn_merge(os_, lses, h, w2, tm=1024):
    G = len(os_)
    S, N = h.shape

    def body(*refs):
        o_refs, l_refs, h_ref, w_ref, res_ref, out_ref = refs[:G], refs[G:2 * G], *refs[2 * G:2 * G + 4]
        scr = refs[2 * G + 4:]
        o = [_gather_heads(o_refs[g], scr[g], d, tm) for g, (_, d) in enumerate(ATTN_GROUPS)]
        l = [_gather_tile(l_refs[g], scr[G + g].at[0], d, tm) for g, (_, d) in enumerate(ATTN_GROUPS)]
        for hh in range(HEADS_PER_GROUP):
            al = _softmax3([_lane_pick(l[g], hh) for g in range(G)])
            for g in range(G):
                out_ref[:, g * GROUP_W + hh * HEAD:g * GROUP_W + (hh + 1) * HEAD] = (o[g][hh] * al[g]).astype(BF16)
        res_ref[...] = h_ref[...] + _dot(out_ref[...], w_ref[...])

    specs = [_group_spec(d, tm) for _, d in ATTN_GROUPS]
    row = pl.BlockSpec((tm, N), lambda i: (i, 0))
    return pl.pallas_call(
        body, name="attn_merge_out", grid=(S // tm,),
        in_specs=specs + [_tile_spec(d, tm) for _, d in ATTN_GROUPS] + [
            row, pl.BlockSpec((G * GROUP_W, N), lambda i: (0, 0), pipeline_mode=pl.Buffered(1))],
        out_specs=[row, pl.BlockSpec((tm, G * GROUP_W), lambda i: (i, 0))],
        out_shape=[jax.ShapeDtypeStruct((S, N), F32), jax.ShapeDtypeStruct((S, G * GROUP_W), BF16)],
        scratch_shapes=[pltpu.VMEM((HEADS_PER_GROUP, tm, HEAD), F32)] * (2 * G),
        compiler_params=_params(("parallel",)))(*os_, *lses, h, w2)


def _attn_merge_bwd(os_, lses, dh, w2, tm=512):
    G = len(os_)
    S, N = dh.shape

    def body(*refs):
        o_refs, l_refs, dh_ref, w_ref = refs[:G], refs[G:2 * G], refs[2 * G], refs[2 * G + 1]
        do_refs, dd_refs = refs[2 * G + 2:3 * G + 2], refs[3 * G + 2:4 * G + 2]
        scr = refs[4 * G + 2:]
        doa = _dot_nt(dh_ref[...].astype(BF16), w_ref[...])
        o = [_gather_heads(o_refs[g], scr[g], d, tm) for g, (_, d) in enumerate(ATTN_GROUPS)]
        l = [_gather_tile(l_refs[g], scr[G + g].at[0], d, tm) for g, (_, d) in enumerate(ATTN_GROUPS)]
        do = [[None] * HEADS_PER_GROUP for _ in range(G)]
        dd = [[None] * HEADS_PER_GROUP for _ in range(G)]
        for hh in range(HEADS_PER_GROUP):
            al = _softmax3([_lane_pick(l[g], hh) for g in range(G)])
            mix = None
            for g in range(G):
                dg = doa[:, g * GROUP_W + hh * HEAD:g * GROUP_W + (hh + 1) * HEAD]
                do[g][hh] = dg * al[g]
                t = al[g] * jnp.sum(dg * o[g][hh], axis=-1, keepdims=True)
                mix = t if mix is None else mix + t
            for g in range(G):
                dd[g][hh] = -al[g] * mix
        for g, (_, d) in enumerate(ATTN_GROUPS):
            _scatter_heads(do[g], scr[2 * G + g], do_refs[g], d, tm)
            _scatter_tile(_lane_place(dd[g]), scr[3 * G + g].at[0], dd_refs[g], d, tm)

    specs = [_group_spec(d, tm) for _, d in ATTN_GROUPS]
    tiles = [_tile_spec(d, tm) for _, d in ATTN_GROUPS]
    do_shapes = [jax.ShapeDtypeStruct((d, S // d, GROUP_W), BF16) for _, d in ATTN_GROUPS]
    dd_shapes = [jax.ShapeDtypeStruct((d, S // d, HEAD), F32) for _, d in ATTN_GROUPS]
    return pl.pallas_call(
        body, name="attn_merge_bwd", grid=(S // tm,),
        in_specs=specs + tiles + [pl.BlockSpec((tm, N), lambda i: (i, 0)),
                                  pl.BlockSpec((G * GROUP_W, N), lambda i: (0, 0), pipeline_mode=pl.Buffered(1))],
        out_specs=specs + tiles,
        out_shape=do_shapes + dd_shapes,
        scratch_shapes=[pltpu.VMEM((HEADS_PER_GROUP, tm, HEAD), F32)] * (4 * G),
        compiler_params=_params(("parallel",)))(*os_, *lses, dh, w2)


def _rope_tables(S):
    inv_freq = (1.0 / (np.float32(ROPE_THETA) ** (np.arange(0, HEAD, 2, dtype=np.float32) / np.float32(HEAD))))
    ang = (np.arange(S, dtype=np.float32)[:, None] * inv_freq.astype(np.float32)[None, :]).astype(np.float64)
    cos, sin = np.cos(ang).astype(np.float32), np.sin(ang).astype(np.float32)
    return jnp.asarray(np.concatenate([cos, cos], axis=-1)), jnp.asarray(np.concatenate([-sin, sin], axis=-1))


def _local_step(x, target, norm_mix, norm_ffn, lb_logits, out_gain, final_norm, comm):
    S = x.shape[0]
    nm0, nm1 = norm_mix[0:1], norm_mix[1:2]
    nf0, nf1 = norm_ffn[0:1], norm_ffn[1:2]
    w = comm.first_weights()

    proj, u0, got = _norm_mm(x, nm0, w["hin"], "hgrn_in", rider=comm.gather_rider(LATE_WEIGHTS_A))
    w.update(comm.gathered(LATE_WEIGHTS_A, got))
    (o, og, states), got = _hgrn_fwd(proj, lb_logits, out_gain, rider=comm.gather_rider(LATE_WEIGHTS_B))
    w.update(comm.gathered(LATE_WEIGHTS_B, got))
    fin_tn = w["fin0"].shape[2]
    h1 = _mm_res(x, og, w["hout"], "hgrn_out")
    z0, u1, _ = _norm_mm(h1, nf0, w["fin0"], "ffn0_in", out_dtype=BF16)
    h2, act0 = _swiglu_mm_res(h1, z0, w["fdn0"], "ffn0_down")
    cos, sin = _rope_tables(S)
    G = len(ATTN_GROUPS)
    a_g, u2 = zip(*[_qkv_dilated(h2, nm1, w["qkv"], gi, cos, sin, d) for gi, (_, d) in enumerate(ATTN_GROUPS)])
    o_g, lse_g = zip(*[_attn_fwd(a) for a in a_g])
    h3, oa = _attn_merge(o_g, lse_g, h2, w["aout"])
    z1, u3, _ = _norm_mm(h3, nf1, w["fin1"], "ffn1_in", out_dtype=BF16)
    h4, act1 = _swiglu_mm_res(h3, z1, w["fdn1"], "ffn1_down")
    dh4, loss, d_final = _loss_head(h4, final_norm, target)

    grads, small = {}, {"final_norm": d_final}

    def ffn_bwd(dh, h_in, u_in, z, act, gain, w_in, w_dn, tag, ride=None):
        dz = _mm_nt_swiglu_bwd(dh, w_dn, z, tag + "_down_dx")
        g_dn = _mm_tn(act, dh, 1, D_MODEL, D_MODEL, tag + "_down_dw")[0]
        g_in = _mm_tn(u_in, dz, N_CHIPS, fin_tn, fin_tn, tag + "_in_dw")
        rider = None if ride is None else ride(g_in, g_dn)
        dh_in, dgain, got = _mm_nt_normbwd(dz, w_in, h_in, gain, dh, tag + "_in_dx", rider=rider)
        return dh_in, dgain, g_in, g_dn, got

    dh3, d_nf1, grads["fin1"], grads["fdn1"], _ = ffn_bwd(dh4, h3, u3, z1, act1, nf1, w["fin1"], w["fdn1"], "ffn1")
    grads["aout"] = _mm_tn(oa, dh3, 1, D_MODEL, D_MODEL, "attn_out_dw")[0]
    merged = _attn_merge_bwd(o_g, lse_g, dh3, w["aout"])
    G = len(ATTN_GROUPS)
    das = [_attn_bwd(a_g[gi], merged[gi], lse_g[gi], merged[G + gi]) for gi in range(G)]
    dqkv = None
    for gi in range(G):
        dqkv = _undilate_group(das[gi], dqkv, cos, sin, gi)
    n_qkv = w["qkv"].shape[2]
    grads["qkv"] = _mm_tn(u2[0], dqkv, N_CHIPS, n_qkv, n_qkv, "attn_qkv_dw")
    dh2, d_nm1, _ = _mm_nt_normbwd(dqkv, w["qkv"], h2, nm1, dh3, "attn_qkv_dx")

    def ride_early(g_in, g_dn):
        return comm.pair_rider({**grads, "fin0": g_in, "fdn0": g_dn}, "early")

    dh1, d_nf0, _, _, got = ffn_bwd(dh2, h1, u1, z0, act0, nf0, w["fin0"], w["fdn0"], "ffn0", ride=ride_early)
    comm.paired("early", got)
    dog = _mm_nt(dh1, w["hout"][None], "hgrn_out_dx")
    (dproj, dlb, dgn), got = _hgrn_bwd(proj, lb_logits, out_gain, o, states, dog, rider=comm.exchange_rider("early"))
    comm.exchanged("early", got)
    late = {"hout": _mm_tn(og, dh1, 1, D_MODEL, D_MODEL, "hgrn_out_dw")[0],
            "hin": _mm_tn(u0, dproj, N_CHIPS, D_MODEL, D_MODEL, "hgrn_in_dw")}
    comm.pair_now(late, "late")
    dx, d_nm0, got = _mm_nt_normbwd(dproj, w["hin"], x, nm0, dh1, "hgrn_in_dx", rider=comm.exchange_rider("late"))
    comm.exchanged("late", got)

    small["norm_mix"] = jnp.concatenate([d_nm0, d_nm1], axis=0)
    small["norm_ffn"] = jnp.concatenate([d_nf0, d_nf1], axis=0)
    small["lb"] = dlb.reshape(1, HGRN_HEADS * HEAD)
    small["out_norm"] = dgn.reshape(HGRN_HEADS, HEAD)
    return loss, dx, small


def _place():
    x, y, c = lax.axis_index("x"), lax.axis_index("y"), lax.axis_index("c")
    others = [(1 - x, y), (x, 1 - y), (1 - x, 1 - y)]
    return x, y, c, others


ANY = pl.BlockSpec(memory_space=pl.ANY)


class _GatherRider:
    def __init__(self, shards):
        self.operands = list(shards)
        n = self.n = len(shards)
        self.out_shape = [jax.ShapeDtypeStruct((N_CHIPS,) + s.shape, s.dtype) for s in shards]
        self.scratch = [pltpu.SemaphoreType.DMA((3 * n,)), pltpu.SemaphoreType.DMA((3 * n,)),
                        pltpu.SemaphoreType.DMA((3 * n,)), pltpu.SemaphoreType.DMA((3 * n,)),
                        pltpu.SemaphoreType.DMA((n,)), pltpu.SemaphoreType.DMA((n,))]

    def _copies(self, ins, outs, sems):
        ici_send, ici_recv, _, _, own_send, own_recv = sems
        x, y, c, others = _place()
        me = 2 * x + y
        own = [pltpu.make_async_remote_copy(
            src_ref=ins[a], dst_ref=outs[a].at[me], send_sem=own_send.at[a], recv_sem=own_recv.at[a],
            device_id=(x, y, 1 - c), device_id_type=MESH) for a in range(self.n)]
        sends = [pltpu.make_async_remote_copy(
            src_ref=ins[a].at[c], dst_ref=outs[a].at[me, c], send_sem=ici_send.at[a * 3 + k], recv_sem=ici_recv.at[a * 3 + k],
            device_id=(ox, oy, c), device_id_type=MESH) for a in range(self.n) for k, (ox, oy) in enumerate(others)]
        return own, sends

    def start(self, ins, outs, sems):
        own, sends = self._copies(ins, outs, sems)
        for cp in own + sends:
            cp.start()

    def finish(self, ins, outs, sems):
        ici_send, ici_recv, d2d_send, d2d_recv, _, _ = sems
        x, y, c, others = _place()
        sibling = (x, y, 1 - c)
        own, sends = self._copies(ins, outs, sems)
        passes = []
        for a in range(self.n):
            for k, (ox, oy) in enumerate(others):
                s = a * 3 + k
                got = outs[a].at[2 * ox + oy, c]
                pltpu.make_async_remote_copy(
                    src_ref=got, dst_ref=got, send_sem=ici_send.at[s], recv_sem=ici_recv.at[s],
                    device_id=(ox, oy, c), device_id_type=MESH).wait_recv()
                fwd = pltpu.make_async_remote_copy(
                    src_ref=got, dst_ref=got, send_sem=d2d_send.at[s], recv_sem=d2d_recv.at[s],
                    device_id=sibling, device_id_type=MESH)
                fwd.start()
                passes.append(fwd)
        for a in range(self.n):
            for k, (ox, oy) in enumerate(others):
                s = a * 3 + k
                theirs = outs[a].at[2 * ox + oy, 1 - c]
                pltpu.make_async_remote_copy(
                    src_ref=theirs, dst_ref=theirs, send_sem=d2d_send.at[s], recv_sem=d2d_recv.at[s],
                    device_id=sibling, device_id_type=MESH).wait_recv()
        for cp in own:
            cp.wait()
        for cp in sends + passes:
            cp.wait_send()


class _PairRider:
    def __init__(self, grads):
        self.operands = list(grads)
        n = self.n = len(grads)
        self.out_shape = [jax.ShapeDtypeStruct((N_CHIPS,) + g.shape[2:], F32) for g in grads]
        self.scratch = [pltpu.SemaphoreType.DMA((N_CHIPS * n,)), pltpu.SemaphoreType.DMA((N_CHIPS * n,))]

    def _copies(self, ins, outs, sems):
        send_sem, recv_sem = sems
        x, y, c, _ = _place()
        return [pltpu.make_async_remote_copy(
            src_ref=ins[a].at[j, 1 - c], dst_ref=outs[a].at[j], send_sem=send_sem.at[a * N_CHIPS + j],
            recv_sem=recv_sem.at[a * N_CHIPS + j], device_id=(x, y, 1 - c), device_id_type=MESH)
            for a in range(self.n) for j in range(N_CHIPS)]

    def start(self, ins, outs, sems):
        for cp in self._copies(ins, outs, sems):
            cp.start()

    def finish(self, ins, outs, sems):
        for cp in self._copies(ins, outs, sems):
            cp.wait()


class _ExchangeRider:
    def __init__(self, parts):
        self.operands = list(parts)
        n = self.n = len(parts)
        self.out_shape = [jax.ShapeDtypeStruct(p.shape, p.dtype) for p in parts]
        self.scratch = [pltpu.SemaphoreType.DMA((3 * n,)), pltpu.SemaphoreType.DMA((3 * n,))]

    def _copies(self, ins, outs, sems):
        send_sem, recv_sem = sems
        x, y, c, others = _place()
        me = 2 * x + y
        return [pltpu.make_async_remote_copy(
            src_ref=ins[a].at[2 * ox + oy], dst_ref=outs[a].at[me], send_sem=send_sem.at[a * 3 + k],
            recv_sem=recv_sem.at[a * 3 + k], device_id=(ox, oy, c), device_id_type=MESH)
            for a in range(self.n) for k, (ox, oy) in enumerate(others)]

    def start(self, ins, outs, sems):
        for cp in self._copies(ins, outs, sems):
            cp.start()

    def finish(self, ins, outs, sems):
        send_sem, recv_sem = sems
        x, y, c, others = _place()
        for a in range(self.n):
            for k, (ox, oy) in enumerate(others):
                s = a * 3 + k
                got = outs[a].at[2 * ox + oy]
                pltpu.make_async_remote_copy(
                    src_ref=got, dst_ref=got, send_sem=send_sem.at[s], recv_sem=recv_sem.at[s],
                    device_id=(ox, oy, c), device_id_type=MESH).wait_recv()
        for cp in self._copies(ins, outs, sems):
            cp.wait_send()


def _run_rider(rider, name):
    n = rider.n

    def body(*refs):
        ins, outs, sems = refs[:n], refs[n:2 * n], refs[2 * n:]
        rider.start(ins, outs, sems)
        rider.finish(ins, outs, sems)

    return pl.pallas_call(
        body, name=name, in_specs=[ANY] * n, out_specs=[ANY] * n,
        out_shape=rider.out_shape, scratch_shapes=rider.scratch)(*rider.operands)


def _ride(rider, body, n_in, n_out, first, last):
    if rider is None:
        return body
    n = rider.n

    def wrapped(*refs):
        host_in, r_in = refs[:n_in], refs[n_in:n_in + n]
        host_out = refs[n_in + n:n_in + n + n_out]
        r_out = refs[n_in + n + n_out:n_in + 2 * n + n_out]
        rest = refs[n_in + 2 * n + n_out:]
        host_scr, sems = rest[:len(rest) - len(rider.scratch)], rest[len(rest) - len(rider.scratch):]

        @pl.when(first())
        def _():
            rider.start(r_in, r_out, sems)

        body(*host_in, *host_out, *host_scr)

        @pl.when(last())
        def _():
            rider.finish(r_in, r_out, sems)

    return wrapped


def _rider_args(rider):
    if rider is None:
        return [], [], [], [], []
    return rider.operands, [ANY] * rider.n, [ANY] * rider.n, rider.out_shape, rider.scratch


def _pair_sum(gs, gots, c_idx):
    n, parts = len(gs), 2

    def body(c_ref, *refs):
        g_refs, got_refs, pb_refs = refs[:n], refs[n:2 * n], refs[2 * n:]
        for k in range(n):
            pb_refs[k][...] = (g_refs[k][...] + got_refs[k][...]).astype(BF16)

    mine = [pl.BlockSpec((None, None, g.shape[2] // parts, g.shape[3]), lambda j, i, c_ref: (j, c_ref[0], i, 0))
            for g in gs]
    blk = [pl.BlockSpec((None, g.shape[2] // parts, g.shape[3]), lambda j, i, c_ref: (j, i, 0)) for g in gs]
    return pl.pallas_call(
        body, name="grad_pair_sum",
        grid_spec=pltpu.PrefetchScalarGridSpec(
            num_scalar_prefetch=1, grid=(N_CHIPS, parts), in_specs=mine + blk, out_specs=blk),
        out_shape=[jax.ShapeDtypeStruct((N_CHIPS,) + g.shape[2:], BF16) for g in gs],
        compiler_params=_params(("parallel", "parallel")))(c_idx, *gs, *gots)


def _chip_sum(gs, sibs, gots, place):
    n, parts = len(gs), 4

    def body(place_ref, *refs):
        g_refs, sib_refs, got_refs, t_refs = refs[:n], refs[n:2 * n], refs[2 * n:3 * n], refs[3 * n:]
        me = place_ref[0]
        for k in range(n):
            own = g_refs[k][...] + sib_refs[k][...]
            acc = None
            for s in range(N_CHIPS):
                term = jnp.where(me == s, own, got_refs[k][s].astype(F32))
                acc = term if acc is None else acc + term
            t_refs[k][...] = acc

    tiles = [(g.shape[2] // parts, g.shape[3]) for g in gs]
    return pl.pallas_call(
        body, name="grad_chip_sum",
        grid_spec=pltpu.PrefetchScalarGridSpec(
            num_scalar_prefetch=1, grid=(parts,),
            in_specs=[pl.BlockSpec((None, None) + t, lambda i, pr: (pr[0], pr[1], i, 0)) for t in tiles]
            + [pl.BlockSpec((None,) + t, lambda i, pr: (pr[0], i, 0)) for t in tiles]
            + [pl.BlockSpec((N_CHIPS,) + t, lambda i, pr: (0, i, 0)) for t in tiles],
            out_specs=[pl.BlockSpec(t, lambda i, pr: (i, 0)) for t in tiles]),
        out_shape=[jax.ShapeDtypeStruct(g.shape[2:], F32) for g in gs],
        compiler_params=_params(("parallel",)))(place, *gs, *sibs, *gots)


def _pair_share(halves):
    n = len(halves)

    def body(*refs):
        ins, outs = refs[:n], refs[n:2 * n]
        send_sem, recv_sem = refs[2 * n:]
        x, y, c, _ = _place()
        cps = [pltpu.make_async_remote_copy(
            src_ref=ins[a], dst_ref=outs[a], send_sem=send_sem.at[a], recv_sem=recv_sem.at[a],
            device_id=(x, y, 1 - c), device_id_type=MESH) for a in range(n)]
        for cp in cps:
            cp.start()
        for cp in cps:
            cp.wait()

    return pl.pallas_call(
        body, name="grad_pair_share",
        in_specs=[ANY] * n, out_specs=[ANY] * n,
        out_shape=[jax.ShapeDtypeStruct(h.shape, F32) for h in halves],
        scratch_shapes=[pltpu.SemaphoreType.DMA((n,)), pltpu.SemaphoreType.DMA((n,))],
        )(*halves)


def _small_allreduce(pack):
    m_per, ncol = pack.shape
    n_dev = 8

    def body(x_ref, sum_ref, all_ref, send_sems, recv_sems, local_sem):
        x, y, c, others = _place()
        me, sibling = (x, y, c), (x, y, 1 - c)

        def rows(px, py, pc):
            return all_ref.at[pl.ds((4 * px + 2 * py + pc) * m_per, m_per), :]

        def copy(k, block, to, src=None):
            return pltpu.make_async_remote_copy(
                src_ref=rows(*block) if src is None else src, dst_ref=rows(*block),
                send_sem=send_sems.at[k], recv_sem=recv_sems.at[k], device_id=to, device_id_type=MESH)

        mine = pltpu.make_async_copy(x_ref, rows(*me), local_sem)
        mine.start()
        first = [copy(0, me, sibling, src=x_ref)]
        first += [copy(1 + j, me, (*chip, c), src=x_ref) for j, chip in enumerate(others)]
        for cp in first:
            cp.start()
        passed = [copy(4 + j, (*chip, c), sibling) for j, chip in enumerate(others)]
        for j, chip in enumerate(others):
            copy(1 + j, (*chip, c), me).wait_recv()
            passed[j].start()
        copy(0, sibling, me).wait_recv()
        for j, chip in enumerate(others):
            copy(4 + j, (*chip, 1 - c), me).wait_recv()
        for cp in first + passed:
            cp.wait_send()
        mine.wait()
        acc = all_ref[0:m_per, :]
        for dvc in range(1, n_dev):
            acc = acc + all_ref[dvc * m_per:(dvc + 1) * m_per, :]
        sum_ref[...] = acc

    return pl.pallas_call(
        body, name="small_allreduce",
        in_specs=[pl.BlockSpec(memory_space=pltpu.VMEM)],
        out_specs=pl.BlockSpec(memory_space=pltpu.VMEM),
        out_shape=jax.ShapeDtypeStruct((m_per, ncol), F32),
        scratch_shapes=[pltpu.VMEM((n_dev * m_per, ncol), F32),
                        pltpu.SemaphoreType.DMA((7,)), pltpu.SemaphoreType.DMA((7,)), pltpu.SemaphoreType.DMA],
        )(pack)


def _adam_math(w, g, m, v):
    m = ADAM_B1 * m + (1.0 - ADAM_B1) * g
    v = ADAM_B2 * v + (1.0 - ADAM_B2) * (g * g)
    m_hat = m / (1.0 - ADAM_B1 ** ADAM_STEP)
    v_hat = v / (1.0 - ADAM_B2 ** ADAM_STEP)
    delta = -ADAM_LR * (m_hat / (jnp.sqrt(v_hat) + ADAM_EPS) + ADAM_WD * w)
    return delta, m, v


def _adamw(halves, c_idx, w, m, v, name):
    L = len(halves)
    r, C = halves[0][0].shape
    tr = _row_tile(r, C, 2 * 1024 * 1024)
    nt = r // tr

    def body(c_ref, *refs):
        g_refs, (w_ref, m_ref, v_ref), (g_ref, d_ref, nm_ref, nv_ref) = refs[:2 * L], refs[2 * L:2 * L + 3], refs[2 * L + 3:]
        own = pl.program_id(1) == c_ref[0]
        g = None
        for l in range(L):
            cand = jnp.where(own, g_refs[2 * l][...], g_refs[2 * l + 1][...])
            g = cand if g is None else jnp.where(pl.program_id(0) == l, cand, g)
        g_ref[...] = g
        d_ref[...], nm_ref[...], nv_ref[...] = _adam_math(w_ref[...], g, m_ref[...], v_ref[...])

    def half(l, mine):
        def index(ll, h, i, c_ref):
            read = (h == c_ref[0]) if mine else (h != c_ref[0])
            return jnp.where(jnp.logical_and(ll == l, read), i, 0), 0
        return pl.BlockSpec((tr, C), index)

    full = pl.BlockSpec((None, tr, C), lambda ll, h, i, c_ref: (ll, h * nt + i, 0))
    shp = jax.ShapeDtypeStruct((L, 2 * r, C), F32)
    g_specs = [half(l, mine) for l in range(L) for mine in (True, False)]
    return pl.pallas_call(
        body, name=name,
        grid_spec=pltpu.PrefetchScalarGridSpec(
            num_scalar_prefetch=1, grid=(L, 2, nt),
            in_specs=g_specs + [full] * 3, out_specs=[full] * 4),
        out_shape=[shp] * 4,
        compiler_params=_params(("arbitrary", "arbitrary", "arbitrary")))(
            c_idx, *[a for pair in halves for a in pair], w, m, v)


SMALL_ROW_SPANS = ((0, 2), (2, 4), (4, 7), (7, 8), (8, 9))


def _small_update(gsum, w, m, v):
    n = len(SMALL_ROW_SPANS)

    def body(gs_ref, *refs):
        w_refs, m_refs, v_refs = refs[:n], refs[n:2 * n], refs[2 * n:3 * n]
        g_refs, d_refs, nm_refs, nv_refs = [refs[(3 + k) * n:(4 + k) * n] for k in range(4)]
        lg_ref = w_refs[2]
        l0, l1, l2 = lg_ref[0:1, :], lg_ref[1:2, :], lg_ref[2:3, :]
        mx = jnp.maximum(jnp.maximum(l0, l1), l2)
        e0, e1, e2 = jnp.exp(l0 - mx), jnp.exp(l1 - mx), jnp.exp(l2 - mx)
        tot = e0 + e1 + e2
        p0, p1, p2 = e0 / tot, e1 / tot, e2 / tot
        dlb = gs_ref[4:5, :]
        for k, (r0, r1) in enumerate(SMALL_ROW_SPANS):
            if k == 2:
                g = jnp.concatenate([dlb * p0 * (1.0 - p0), -dlb * p0 * p1, -dlb * p0 * p2], axis=0)
            else:
                g = gs_ref[r0:r1, 0:w_refs[k].shape[1]]
            g_refs[k][...] = g
            d_refs[k][...], nm_refs[k][...], nv_refs[k][...] = _adam_math(w_refs[k][...], g, m_refs[k][...], v_refs[k][...])

    full = pl.BlockSpec(memory_space=pltpu.VMEM)
    shapes = [jax.ShapeDtypeStruct(a.shape, F32) for a in w]
    out = pl.pallas_call(
        body, name="small_update", in_specs=[full] * (1 + 3 * n), out_specs=[full] * (4 * n), out_shape=shapes * 4)(
            gsum, *w, *m, *v)
    return [out[k * n:(k + 1) * n] for k in range(4)]


def _pack_small(norm_mix, norm_ffn, lb3, out_norm, final_norm, extra=None):
    ncol = norm_mix.shape[1]
    on = jnp.pad(out_norm.reshape(1, -1), ((0, 0), (0, ncol - out_norm.size)))
    rows = [norm_mix, norm_ffn, lb3, on, final_norm.reshape(1, ncol)]
    if extra is not None:
        rows.append(extra)
    used = sum(r.shape[0] for r in rows)
    rows.append(jnp.zeros((SMALL_ROWS - used, ncol), F32))
    return jnp.concatenate(rows, axis=0)


WEIGHT_NAMES = ("hin", "hout", "qkv", "aout", "fin0", "fin1", "fdn0", "fdn1")
FIRST_WEIGHTS = ("hin",)
LATE_WEIGHTS_A = ("hout", "fin0", "fdn0")
LATE_WEIGHTS_B = ("qkv", "aout", "fin1", "fdn1")


def _split_weights(hgrn_w_in, hgrn_w_out, attn_w_qkv, attn_w_out, ffn_w_in, ffn_w_down):
    return {"hin": hgrn_w_in[0], "hout": hgrn_w_out[0], "qkv": attn_w_qkv[0], "aout": attn_w_out[0],
            "fin0": ffn_w_in[0], "fin1": ffn_w_in[1], "fdn0": ffn_w_down[0], "fdn1": ffn_w_down[1]}


def _halves(v):
    r, c = v.shape
    return v.reshape(2, r // 2, c)


def _full_weights(gathered):
    out = {}
    for k, g in gathered.items():
        _, _, r, c = g.shape
        if k in ("hin", "qkv", "fin0", "fin1"):
            out[k] = g.reshape(N_CHIPS, 2 * r, c)
        else:
            out[k] = g.reshape(N_CHIPS * 2 * r, c)
    return out


class _StepComm:
    def __init__(self, shards, c_idx, me_idx):
        self.shards, self.c_idx, self.me_idx = shards, c_idx, me_idx
        self.halves = {}
        self._stage = {}

    def gather_rider(self, names):
        return _GatherRider([_halves(self.shards[k].astype(BF16)) for k in names])

    def gathered(self, names, got):
        return _full_weights(dict(zip(names, got)))

    def first_weights(self):
        return self.gathered(FIRST_WEIGHTS, _run_rider(self.gather_rider(FIRST_WEIGHTS), "gather_first"))

    def pair_rider(self, grads, tag):
        names = list(grads)
        g4 = []
        for k in names:
            r, c = self.shards[k].shape
            g4.append(grads[k].reshape(N_CHIPS, 2, r // 2, c))
        self._stage[tag] = (names, g4)
        return _PairRider(g4)

    def pair_now(self, grads, tag):
        self.paired(tag, _run_rider(self.pair_rider(grads, tag), "grad_pair_exchange_" + tag))

    def paired(self, tag, got):
        names, g4 = self._stage[tag]
        self._stage[tag] = (names, list(zip(g4, got, _pair_sum(g4, list(got), self.c_idx))))

    def exchange_rider(self, tag):
        return _ExchangeRider([s[2] for s in self._stage[tag][1]])

    def exchanged(self, tag, got):
        names, sums = self._stage.pop(tag)
        place = jnp.concatenate([self.me_idx, self.c_idx])
        g4, sibs, _ = zip(*sums)
        self.halves.update(zip(names, _chip_sum(list(g4), list(sibs), list(got), place)))

    def shared_halves(self):
        mine = [self.halves[k] for k in WEIGHT_NAMES]
        return dict(zip(WEIGHT_NAMES, zip(mine, _pair_share(mine))))


def kernel(x, norm_mix, norm_ffn, hgrn_w_in, hgrn_lb_logits, hgrn_out_norm, hgrn_w_out, attn_w_qkv, attn_w_out, ffn_w_in, ffn_w_down, final_norm, loss_target, m_norm_mix, m_norm_ffn, m_hgrn_w_in, m_hgrn_lb_logits, m_hgrn_out_norm, m_hgrn_w_out, m_attn_w_qkv, m_attn_w_out, m_ffn_w_in, m_ffn_w_down, m_final_norm, v_norm_mix, v_norm_ffn, v_hgrn_w_in, v_hgrn_lb_logits, v_hgrn_out_norm, v_hgrn_w_out, v_attn_w_qkv, v_attn_w_out, v_ffn_w_in, v_ffn_w_down, v_final_norm):
    S = x.shape[1]
    xi, yi, ci = lax.axis_index("x"), lax.axis_index("y"), lax.axis_index("c")
    c_idx = jnp.reshape(ci, (1,)).astype(jnp.int32)
    me_idx = jnp.reshape(2 * xi + yi, (1,)).astype(jnp.int32)

    w_own = _split_weights(hgrn_w_in, hgrn_w_out, attn_w_qkv, attn_w_out, ffn_w_in, ffn_w_down)

    comm = _StepComm(w_own, c_idx, me_idx)
    loss, dx, small = _local_step(
        x.reshape(S, D_MODEL), loss_target.reshape(S, D_MODEL), norm_mix, norm_ffn, hgrn_lb_logits,
        hgrn_out_norm, final_norm.reshape(1, D_MODEL), comm)

    halves = comm.shared_halves()
    updated = {}
    for tensor, layers, (wt, mt, vt) in (
            ("hgrn_w_in", ("hin",), (hgrn_w_in, m_hgrn_w_in, v_hgrn_w_in)),
            ("hgrn_w_out", ("hout",), (hgrn_w_out, m_hgrn_w_out, v_hgrn_w_out)),
            ("attn_w_qkv", ("qkv",), (attn_w_qkv, m_attn_w_qkv, v_attn_w_qkv)),
            ("attn_w_out", ("aout",), (attn_w_out, m_attn_w_out, v_attn_w_out)),
            ("ffn_w_in", ("fin0", "fin1"), (ffn_w_in, m_ffn_w_in, v_ffn_w_in)),
            ("ffn_w_down", ("fdn0", "fdn1"), (ffn_w_down, m_ffn_w_down, v_ffn_w_down))):
        updated[tensor] = _adamw([halves[k] for k in layers], c_idx, wt, mt, vt, "adamw_" + tensor)

    loss_row = jnp.pad(loss, ((0, 0), (0, D_MODEL - loss.shape[1])))
    lb3 = jnp.concatenate([small["lb"], jnp.zeros((2, D_MODEL), F32)], axis=0)
    on_grad = jnp.sum(small["out_norm"], axis=0, keepdims=True)
    pack = _pack_small(small["norm_mix"], small["norm_ffn"], lb3, on_grad, small["final_norm"], loss_row)
    gsum = _small_allreduce(pack)
    fn2 = (1, D_MODEL)
    sg, sd, sm, sv = _small_update(
        gsum, (norm_mix, norm_ffn, hgrn_lb_logits, hgrn_out_norm, final_norm.reshape(fn2)),
        (m_norm_mix, m_norm_ffn, m_hgrn_lb_logits, m_hgrn_out_norm, m_final_norm.reshape(fn2)),
        (v_norm_mix, v_norm_ffn, v_hgrn_lb_logits, v_hgrn_out_norm, v_final_norm.reshape(fn2)))

    def assemble(p, which):
        nmx, nff, lbl, onm, fnm = p
        fnm = fnm.reshape(D_MODEL)
        hin, hout, qkv, aout, fin, fdn = [updated[t][which] for t in
                                          ("hgrn_w_in", "hgrn_w_out", "attn_w_qkv", "attn_w_out", "ffn_w_in", "ffn_w_down")]
        return (nmx, nff, hin, lbl, onm, hout, qkv, aout, fin, fdn, fnm)

    total_loss = gsum[9, 0]
    return (total_loss, dx.reshape(1, S, D_MODEL), *assemble(sg, 0), *assemble(sd, 1), *assemble(sm, 2), *assemble(sv, 3))
```

```python
import functools

import jax
import jax.numpy as jnp
import numpy as np
from jax import lax
from jax.experimental import pallas as pl
from jax.experimental.pallas import tpu as pltpu

F32 = jnp.float32
BF16 = jnp.bfloat16
MESH = pl.DeviceIdType.MESH

D_MODEL = 1024
HEAD = 128
HGRN_HEADS = 8
HGRN_CHUNK = 64
HGRN_HEADS_PER_STEP = 2
ATTN_GROUPS = ((128, 1), (512, 4), (2048, 16))
ATTN_SPAN = 128
HEADS_PER_GROUP = 4
GROUP_W = HEADS_PER_GROUP * HEAD
D_FF = 2816
NORM_EPS = 1e-6
ROPE_THETA = 10000.0
NEG = -1e30

ADAM_LR, ADAM_B1, ADAM_B2, ADAM_EPS, ADAM_WD, ADAM_STEP = 0.001, 0.9, 0.999, 1e-08, 0.01, 10

N_CHIPS = 4
VMEM_LIMIT = 56 * 1024 * 1024
SMALL_ROWS = 16


def _params(sem=None):
    return pltpu.CompilerParams(dimension_semantics=sem, vmem_limit_bytes=VMEM_LIMIT)


def _row_tile(rows, cols, budget_bytes=3 * 512 * 1024):
    best = 8
    for t in range(8, rows + 1, 8):
        if rows % t == 0 and t * cols * 4 <= budget_bytes:
            best = t
    assert rows % best == 0
    return best


def _grid_corner(i, j):
    return jnp.logical_and(pl.program_id(0) == i, pl.program_id(1) == j)


def _sigmoid(v):
    return 0.5 * jnp.tanh(0.5 * v) + 0.5


def _dot(a, b):
    return jnp.dot(a, b, preferred_element_type=F32)


def _dot_nt(a, b):
    return lax.dot_general(a, b, (((1,), (1,)), ((), ())), preferred_element_type=F32)


def _dot_tn(a, b):
    return lax.dot_general(a, b, (((0,), (0,)), ((), ())), preferred_element_type=F32)


def _dot_exact(ones, b):
    ones = ones.astype(BF16)
    hi = b.astype(BF16)
    rest = b - hi.astype(F32)
    mid = rest.astype(BF16)
    low = (rest - mid.astype(F32)).astype(BF16)
    return _dot(ones, hi) + _dot(ones, mid) + _dot(ones, low)


def _rstd(v):
    return lax.rsqrt(jnp.mean(v * v, axis=-1, keepdims=True) + NORM_EPS)


def _norm_mm(h, gain, w3, name, out_dtype=F32, tm=2048, rider=None):
    S, K = h.shape
    J, _, n = w3.shape
    gi = S // tm

    def body(h_ref, g_ref, w_ref, y_ref, u_ref):
        @pl.when(pl.program_id(1) == 0)
        def _():
            v = h_ref[...]
            u_ref[...] = (v * _rstd(v) * g_ref[...]).astype(BF16)

        y_ref[...] = _dot(u_ref[...], w_ref[pl.program_id(1)]).astype(y_ref.dtype)

    r_ops, r_in, r_out, r_shape, r_scr = _rider_args(rider)
    res = pl.pallas_call(
        _ride(rider, body, 3, 2, functools.partial(_grid_corner, 0, 0), functools.partial(_grid_corner, gi - 1, J - 1)),
        name=name, grid=(gi, J),
        in_specs=[pl.BlockSpec((tm, K), lambda i, j: (i, 0)),
                  pl.BlockSpec((1, K), lambda i, j: (0, 0)),
                  pl.BlockSpec((J, K, n), lambda i, j: (0, 0, 0), pipeline_mode=pl.Buffered(1))] + r_in,
        out_specs=[pl.BlockSpec((tm, n), lambda i, j: (i, j)), pl.BlockSpec((tm, K), lambda i, j: (i, 0))] + r_out,
        out_shape=[jax.ShapeDtypeStruct((S, J * n), out_dtype), jax.ShapeDtypeStruct((S, K), BF16)] + r_shape,
        scratch_shapes=r_scr,
        compiler_params=_params(("arbitrary", "arbitrary")))(h, gain, w3, *r_ops)
    return res[0], res[1], res[2:]


def _mm_res(h, a, w2, name, tm=1024):
    S, N = h.shape
    K = a.shape[1]
    steps, slots = S // tm, 3
    assert steps >= slots - 1

    def body(h_hbm, a_ref, w_ref, o_ref, h_buf, sem):
        s = pl.program_id(0)

        def fetch(step):
            slot = step % slots
            return pltpu.make_async_copy(h_hbm.at[pl.ds(step * tm, tm), :], h_buf.at[slot], sem.at[slot])

        @pl.when(s == 0)
        def _():
            for k in range(slots - 1):
                fetch(k).start()

        @pl.when(s + slots - 1 < steps)
        def _():
            fetch(s + slots - 1).start()

        fetch(s).wait()
        o_ref[...] = h_buf[s % slots] + _dot(a_ref[...], w_ref[...])

    return pl.pallas_call(
        body, name=name, grid=(steps,),
        in_specs=[ANY, pl.BlockSpec((tm, K), lambda i: (i, 0)), pl.BlockSpec((K, N), lambda i: (0, 0))],
        out_specs=pl.BlockSpec((tm, N), lambda i: (i, 0)),
        out_shape=jax.ShapeDtypeStruct((S, N), F32),
        scratch_shapes=[pltpu.VMEM((slots, tm, N), F32), pltpu.SemaphoreType.DMA((slots,))],
        compiler_params=_params(("arbitrary",)))(h, a, w2)


def _swiglu(z_ref, F):
    g = z_ref[:, :F].astype(F32)
    return (g * _sigmoid(g) * z_ref[:, F:].astype(F32)).astype(BF16)


def _swiglu_mm_res(h, z, w2, name, tm=512):
    S, N = h.shape
    F = w2.shape[0]
    steps, slots = S // tm, 3
    assert steps >= slots - 1

    def body(h_ref, z_hbm, w_ref, o_ref, a_ref, z_buf, sem):
        s = pl.program_id(0)

        def fetch(step):
            slot = step % slots
            return pltpu.make_async_copy(z_hbm.at[pl.ds(step * tm, tm), :], z_buf.at[slot], sem.at[slot])

        @pl.when(s == 0)
        def _():
            for k in range(slots - 1):
                fetch(k).start()

        @pl.when(s + slots - 1 < steps)
        def _():
            fetch(s + slots - 1).start()

        fetch(s).wait()
        a = _swiglu(z_buf.at[s % slots], F)
        a_ref[...] = a
        o_ref[...] = h_ref[...] + _dot(a, w_ref[...])

    return pl.pallas_call(
        body, name=name, grid=(steps,),
        in_specs=[pl.BlockSpec((tm, N), lambda i: (i, 0)), ANY,
                  pl.BlockSpec((F, N), lambda i: (0, 0), pipeline_mode=pl.Buffered(1))],
        out_specs=[pl.BlockSpec((tm, N), lambda i: (i, 0)), pl.BlockSpec((tm, F), lambda i: (i, 0))],
        out_shape=[jax.ShapeDtypeStruct((S, N), F32), jax.ShapeDtypeStruct((S, F), BF16)],
        scratch_shapes=[pltpu.VMEM((slots, tm, 2 * F), BF16), pltpu.SemaphoreType.DMA((slots,))],
        compiler_params=_params(("arbitrary",)))(h, z, w2)


def _dy_specs(dy, J, n, tm):
    if dy.ndim == 3:
        return [pl.BlockSpec((None, tm, n), functools.partial(lambda i, j: (j, i, 0), j=j)) for j in range(J)]
    return [pl.BlockSpec((tm, n), functools.partial(lambda i, j: (i, j), j=j)) for j in range(J)]


def _acc_nt(dy_refs, w_ref):
    acc = None
    for j, r in enumerate(dy_refs):
        t = _dot_nt(r[...].astype(BF16), w_ref[j])
        acc = t if acc is None else acc + t
    return acc


def _mm_nt(dy, w3, name, out_dtype=F32, tm=1024):
    J, K, n = w3.shape
    S = dy.shape[-2]

    def body(*refs):
        dy_refs, w_ref, o_ref = refs[:J], refs[J], refs[J + 1]
        o_ref[...] = _acc_nt(dy_refs, w_ref).astype(o_ref.dtype)

    return pl.pallas_call(
        body, name=name, grid=(S // tm,),
        in_specs=_dy_specs(dy, J, n, tm) + [pl.BlockSpec((J, K, n), lambda i: (0, 0, 0))],
        out_specs=pl.BlockSpec((tm, K), lambda i: (i, 0)),
        out_shape=jax.ShapeDtypeStruct((S, K), out_dtype),
        compiler_params=_params(("parallel",)))(*([dy] * J), w3)


def _mm_nt_normbwd(dy, w3, h, gain, dh, name, tm=512, rider=None):
    J, K, n = w3.shape
    S = h.shape[0]
    steps = S // tm

    def body(*refs):
        dy_refs, w_ref, h_ref, g_ref, dh_ref, o_ref, dg_ref = refs[:J], *refs[J:]
        du = _acc_nt(dy_refs, w_ref)
        v = h_ref[...]
        r = _rstd(v)
        xh = v * r
        dyg = du * g_ref[...]
        o_ref[...] = dh_ref[...] + r * (dyg - xh * jnp.mean(dyg * xh, axis=-1, keepdims=True))

        @pl.when(pl.program_id(0) == 0)
        def _():
            dg_ref[...] = jnp.zeros_like(dg_ref)

        dg_ref[...] += jnp.sum(du * xh, axis=0, keepdims=True)

    row = pl.BlockSpec((tm, K), lambda i: (i, 0))
    vec = pl.BlockSpec((1, K), lambda i: (0, 0))
    r_ops, r_in, r_out, r_shape, r_scr = _rider_args(rider)
    res = pl.pallas_call(
        _ride(rider, body, J + 4, 2, lambda: pl.program_id(0) == 0, lambda: pl.program_id(0) == steps - 1),
        name=name, grid=(steps,),
        in_specs=_dy_specs(dy, J, n, tm) + [pl.BlockSpec((J, K, n), lambda i: (0, 0, 0)), row, vec, row] + r_in,
        out_specs=[row, vec] + r_out,
        out_shape=[jax.ShapeDtypeStruct((S, K), F32), jax.ShapeDtypeStruct((1, K), F32)] + r_shape,
        scratch_shapes=r_scr,
        compiler_params=_params(("arbitrary",)))(*([dy] * J), w3, h, gain, dh, *r_ops)
    return res[0], res[1], res[2:]


def _mm_nt_swiglu_bwd(dh, w2, z, name, tm=512, chunks=11):
    F, N = w2.shape
    S = dh.shape[0]
    fc = F // chunks
    assert fc * chunks == F and fc % HEAD == 0

    def body(dh_ref, w_ref, z_ref, o_ref):
        dhb = dh_ref[...].astype(BF16)
        da = [_dot_nt(dhb, w_ref[c * fc:(c + 1) * fc, :]) for c in range(chunks)]
        for c in range(chunks):
            g = z_ref[:, c * fc:(c + 1) * fc].astype(F32)
            u = z_ref[:, F + c * fc:F + (c + 1) * fc].astype(F32)
            sg = _sigmoid(g)
            o_ref[:, c * fc:(c + 1) * fc] = (da[c] * u * (sg * (1.0 + g * (1.0 - sg)))).astype(BF16)
            o_ref[:, F + c * fc:F + (c + 1) * fc] = (da[c] * (g * sg)).astype(BF16)

    return pl.pallas_call(
        body, name=name, grid=(S // tm,),
        in_specs=[pl.BlockSpec((tm, N), lambda i: (i, 0)),
                  pl.BlockSpec((F, N), lambda i: (0, 0), pipeline_mode=pl.Buffered(1)),
                  pl.BlockSpec((tm, 2 * F), lambda i: (i, 0))],
        out_specs=pl.BlockSpec((tm, 2 * F), lambda i: (i, 0)),
        out_shape=jax.ShapeDtypeStruct((S, 2 * F), BF16),
        compiler_params=_params(("parallel",)))(dh, w2, z)


def _mm_tn(x, dy, J, n, tn, name):
    tpn = n // tn
    ts = 2048 if x.shape[1] <= 1536 else 1024
    S, K = x.shape
    if dy.ndim == 3:
        dy_spec = pl.BlockSpec((None, ts, tn), lambda c, s: (c // tpn, s, c % tpn))
    else:
        dy_spec = pl.BlockSpec((ts, tn), lambda c, s: (s, c))

    def body(x_ref, dy_ref, o_ref):
        @pl.when(pl.program_id(1) == 0)
        def _():
            o_ref[...] = jnp.zeros_like(o_ref)

        o_ref[...] += _dot_tn(x_ref[...], dy_ref[...].astype(BF16))

    return pl.pallas_call(
        body, name=name, grid=(J * tpn, S // ts),
        in_specs=[pl.BlockSpec((ts, K), lambda c, s: (s, 0)), dy_spec],
        out_specs=pl.BlockSpec((None, K, tn), lambda c, s: (c // tpn, 0, c % tpn)),
        out_shape=jax.ShapeDtypeStruct((J, K, n), F32),
        compiler_params=_params(("parallel", "arbitrary")))(x, dy)


def _loss_head(h, gain, target, tm=1024):
    S, K = h.shape
    steps, slots = S // tm, 3
    assert steps >= slots - 1

    def body(h_hbm, g_ref, t_hbm, dh_ref, loss_ref, dg_ref, h_buf, t_buf, sem):
        s = pl.program_id(0)

        def fetch(step):
            slot, rows = step % slots, pl.ds(step * tm, tm)
            return [pltpu.make_async_copy(h_hbm.at[rows, :], h_buf.at[slot], sem.at[0, slot]),
                    pltpu.make_async_copy(t_hbm.at[rows, :], t_buf.at[slot], sem.at[1, slot])]

        @pl.when(s == 0)
        def _():
            for k in range(slots - 1):
                for cp in fetch(k):
                    cp.start()

        @pl.when(s + slots - 1 < steps)
        def _():
            for cp in fetch(s + slots - 1):
                cp.start()

        for cp in fetch(s):
            cp.wait()
        v = h_buf[s % slots]
        r = _rstd(v)
        xh = v * r
        g = g_ref[...]
        dy = (xh * g - t_buf[s % slots]) * (1.0 / K)
        dyg = dy * g
        dh_ref[...] = r * (dyg - xh * jnp.mean(dyg * xh, axis=-1, keepdims=True))

        @pl.when(pl.program_id(0) == 0)
        def _():
            loss_ref[...] = jnp.zeros_like(loss_ref)
            dg_ref[...] = jnp.zeros_like(dg_ref)

        part = jnp.sum(jnp.sum(dy * dy, axis=-1, keepdims=True), axis=0, keepdims=True) * (0.5 * K)
        lane = lax.broadcasted_iota(jnp.int32, loss_ref.shape, 1)
        loss_ref[...] += jnp.where(lane == 0, part, 0.0)
        dg_ref[...] += jnp.sum(dy * xh, axis=0, keepdims=True)

    row = pl.BlockSpec((tm, K), lambda i: (i, 0))
    vec = pl.BlockSpec((1, K), lambda i: (0, 0))
    return pl.pallas_call(
        body, name="loss_head", grid=(steps,),
        in_specs=[ANY, vec, ANY],
        out_specs=[row, pl.BlockSpec((1, HEAD), lambda i: (0, 0)), vec],
        out_shape=[jax.ShapeDtypeStruct((S, K), F32), jax.ShapeDtypeStruct((1, HEAD), F32),
                   jax.ShapeDtypeStruct((1, K), F32)],
        scratch_shapes=[pltpu.VMEM((slots, tm, K), F32)] * 2 + [pltpu.SemaphoreType.DMA((2, slots))],
        compiler_params=_params(("arbitrary",)))(h, gain, target)


def _lower_bound(lg_ref):
    l0, l1, l2 = lg_ref[0:1, :], lg_ref[1:2, :], lg_ref[2:3, :]
    mx = jnp.maximum(jnp.maximum(l0, l1), l2)
    e0, e1, e2 = jnp.exp(l0 - mx), jnp.exp(l1 - mx), jnp.exp(l2 - mx)
    return e0 / (e0 + e1 + e2)


def _chunks(v, ncb):
    C = HGRN_CHUNK
    return [v[c * C:(c + 1) * C] for c in range(ncb)]


def _rows(parts):
    return jnp.concatenate(parts, axis=0)


def _block_gates(qz, fz, lb, ncb):
    C = HGRN_CHUNK
    row = lax.broadcasted_iota(jnp.int32, (C, C), 0)
    col = lax.broadcasted_iota(jnp.int32, (C, C), 1)
    tri = (col <= row).astype(F32)
    first_half = lax.broadcasted_iota(jnp.int32, (C, HEAD), 0) < C // 2
    sig = _sigmoid(fz)
    fg = lb + (1.0 - lb) * sig
    key = 1.0 - fg
    lg = jnp.log(fg)
    lgs = _chunks(lg, ncb)
    b = _rows([_dot_exact(tri, v) for v in lgs])
    r_c = [jnp.sum(jnp.where(first_half, v, 0.0), axis=0, keepdims=True) for v in lgs]
    bl_c = [jnp.sum(v, axis=0, keepdims=True) for v in lgs]
    r = _rows([jnp.broadcast_to(v, (C, HEAD)) for v in r_c])
    e_br, e_rb = jnp.exp(b - r), jnp.exp(r - b)
    e_b = e_br * _rows([jnp.broadcast_to(jnp.exp(v), (C, HEAD)) for v in r_c])
    e_lb = e_rb * _rows([jnp.broadcast_to(jnp.exp(e - v), (C, HEAD)) for e, v in zip(bl_c, r_c)])
    sq = _sigmoid(qz)
    qy = qz * sq
    return sig, fg, key, (e_br, e_rb, e_b, e_lb), bl_c, sq, qy


def _hgrn_fwd(proj, logits, gain, tb=1024, rider=None):
    S = proj.shape[0]
    H, C = HGRN_HEADS, HGRN_CHUNK
    ncb = tb // C

    def one_head(q_ref, f_ref, i_ref, g_ref, lg_ref, gn_ref, o_ref, og_ref, st_ref, state):
        @pl.when(pl.program_id(1) == 0)
        def _():
            state[...] = jnp.zeros_like(state)

        lb = _lower_bound(lg_ref)
        causal = lax.broadcasted_iota(jnp.int32, (C, C), 1) <= lax.broadcasted_iota(jnp.int32, (C, C), 0)
        qz, fz, gz = q_ref[...], f_ref[...], g_ref[...]
        _, _, key, (e_br, e_rb, e_b, e_lb), bl_c, _, qy = _block_gates(qz, fz, lb, ncb)
        qs = _chunks((qy * e_br).astype(BF16), ncb)
        ks = _chunks((key * e_rb).astype(BF16), ncb)
        qb = _chunks((qy * e_b).astype(BF16), ncb)
        ke = _chunks((key * e_lb).astype(BF16), ncb)
        vb = _chunks(i_ref[...].astype(BF16), ncb)
        a = [jnp.where(causal, _dot_nt(qs[c], ks[c]), 0.0).astype(BF16) for c in range(ncb)]
        upd = [_dot_tn(vb[c], ke[c]) for c in range(ncb)]
        o_intra = [_dot(a[c], vb[c]) for c in range(ncb)]
        st = state[...]
        e_l = [jnp.exp(v) for v in bl_c]
        sts = []
        for c in range(ncb):
            sts.append(st)
            st = st * e_l[c] + upd[c]
        state[...] = st
        for c in range(ncb):
            st_ref[c] = sts[c]
        o = _rows([_dot_nt(qb[c], sts[c].astype(BF16)) + o_intra[c] for c in range(ncb)])
        o_ref[...] = o
        og_ref[...] = ((o * _rstd(o) * gn_ref[...]) * (gz * _sigmoid(gz))).astype(BF16)

    def body(q_ref, f_ref, i_ref, g_ref, lg_ref, gn_ref, o_ref, og_ref, st_ref, state):
        for hs in range(HP):
            cols = slice(hs * HEAD, (hs + 1) * HEAD)
            one_head(q_ref.at[:, cols], f_ref.at[:, cols], i_ref.at[:, cols], g_ref.at[:, cols], lg_ref.at[:, cols],
                     gn_ref, o_ref.at[:, cols], og_ref.at[:, cols], st_ref.at[hs], state.at[hs])

    HP, wide = HGRN_HEADS_PER_STEP, HGRN_HEADS_PER_STEP * HEAD
    hg = H // HP

    def part(p):
        return pl.BlockSpec((tb, wide), functools.partial(lambda h, i, p: (i, p * hg + h), p=p))

    nb = S // tb
    r_ops, r_in, r_out, r_shape, r_scr = _rider_args(rider)
    res = pl.pallas_call(
        _ride(rider, body, 6, 3, functools.partial(_grid_corner, 0, 0), functools.partial(_grid_corner, hg - 1, nb - 1)),
        name="hgrn_fwd", grid=(hg, nb),
        in_specs=[part(0), part(1), part(2), part(3),
                  pl.BlockSpec((3, wide), lambda h, i: (0, h)),
                  pl.BlockSpec((1, HEAD), lambda h, i: (0, 0))] + r_in,
        out_specs=[pl.BlockSpec((tb, wide), lambda h, i: (i, h)),
                   pl.BlockSpec((tb, wide), lambda h, i: (i, h)),
                   pl.BlockSpec((HP, ncb, HEAD, HEAD), lambda h, i: (h, i, 0, 0))] + r_out,
        out_shape=[jax.ShapeDtypeStruct((S, H * HEAD), F32),
                   jax.ShapeDtypeStruct((S, H * HEAD), BF16),
                   jax.ShapeDtypeStruct((H, S // C, HEAD, HEAD), F32)] + r_shape,
        scratch_shapes=[pltpu.VMEM((HP, HEAD, HEAD), F32)] + r_scr,
        compiler_params=_params(("arbitrary", "arbitrary")))(proj, proj, proj, proj, logits, gain, *r_ops)
    return res[:3], res[3:]


def _hgrn_bwd(proj, logits, gain, o, states, dog, tb=1024, rider=None):
    S = proj.shape[0]
    H, C = HGRN_HEADS, HGRN_CHUNK
    ncb = tb // C
    nb = S // tb

    def one_head(q_ref, f_ref, i_ref, g_ref, lg_ref, gn_ref, o_ref, st_ref, dog_ref,
                 dp_ref, dlb_ref, dgn_ref, dstate, dst_scr):
        @pl.when(pl.program_id(1) == 0)
        def _():
            dstate[...] = jnp.zeros_like(dstate)
            dlb_ref[...] = jnp.zeros_like(dlb_ref)
            dgn_ref[...] = jnp.zeros_like(dgn_ref)

        lb = _lower_bound(lg_ref)
        oml = 1.0 - lb
        gn = gn_ref[...]
        row = lax.broadcasted_iota(jnp.int32, (C, C), 0)
        col = lax.broadcasted_iota(jnp.int32, (C, C), 1)
        causal = col <= row
        tri_up = (col >= row).astype(F32)
        qz, fz, gz = q_ref[...], f_ref[...], g_ref[...]
        sig, fg, key, (e_br, e_rb, e_b, e_lb), bl_c, sq, qy = _block_gates(qz, fz, lb, ncb)
        qs_v, ks_v = (qy * e_br).astype(BF16), (key * e_rb).astype(BF16)
        qb_v, ke_v = (qy * e_b).astype(BF16), (key * e_lb).astype(BF16)
        qs, ks, qb, ke = _chunks(qs_v, ncb), _chunks(ks_v, ncb), _chunks(qb_v, ncb), _chunks(ke_v, ncb)
        vb = _chunks(i_ref[...].astype(BF16), ncb)
        ov = o_ref[...]
        rs = _rstd(ov)
        xh = ov * rs
        sg = _sigmoid(gz)
        dog_v = dog_ref[...]
        dgz = dog_v * (xh * gn) * (sg * (1.0 + gz * (1.0 - sg)))
        don = dog_v * (gz * sg)
        dgn_ref[...] += jnp.sum(don * xh, axis=0, keepdims=True)
        dyg = don * gn
        do = rs * (dyg - xh * jnp.mean(dyg * xh, axis=-1, keepdims=True))
        dob = _chunks(do.astype(BF16), ncb)
        CH = range(ncb)
        a = [jnp.where(causal, _dot_nt(qs[c], ks[c]), 0.0).astype(BF16) for c in CH]
        da = [jnp.where(causal, _dot_nt(dob[c], vb[c]), 0.0).astype(BF16) for c in CH]
        wst = [_dot_tn(dob[c], qb[c]) for c in CH]
        dv_in = [_dot_tn(a[c], dob[c]) for c in CH]
        dqs = [_dot(da[c], ks[c]) for c in CH]
        dks = [_dot_tn(da[c], qs[c]) for c in CH]
        e_l = [jnp.exp(v) for v in bl_c]
        dst = dstate[...]
        for c in reversed(range(ncb)):
            dst_scr[c] = dst
            dst = wst[c] + dst * e_l[c]
        dstate[...] = dst
        dst1b = [dst_scr[c].astype(BF16) for c in CH]
        dqb = [_dot(dob[c], st_ref[c].astype(BF16)) for c in CH]
        dke = [_dot(vb[c], dst1b[c]) for c in CH]
        dv = [dv_in[c] + _dot_nt(ke[c], dst1b[c]) for c in CH]
        dbl_st = [jnp.sum(dst_scr[c] * st_ref[c], axis=0, keepdims=True) * e_l[c] for c in CH]
        dqs, dks, dqb, dke, dv = _rows(dqs), _rows(dks), _rows(dqb), _rows(dke), _rows(dv)
        dke_ke = dke * ke_v.astype(F32)
        db = dqs * qs_v.astype(F32) - dks * ks_v.astype(F32) + dqb * qb_v.astype(F32) - dke_ke
        dlg = []
        for c, (db_c, kk_c) in enumerate(zip(_chunks(db, ncb), _chunks(dke_ke, ncb))):
            dbl = jnp.sum(kk_c, axis=0, keepdims=True) + dbl_st[c]
            dlg.append(_dot_exact(tri_up, db_c) + dbl)
        dlg = _rows(dlg)
        dkey = dks * e_rb + dke * e_lb
        dqy = dqs * e_br + dqb * e_b
        dfg = dlg / fg - dkey
        dlb_ref[...] += jnp.sum(dfg * (1.0 - sig), axis=0, keepdims=True)
        dp_ref[0] = (dqy * (sq * (1.0 + qz * (1.0 - sq)))).astype(BF16)
        dp_ref[1] = (dfg * oml * sig * (1.0 - sig)).astype(BF16)
        dp_ref[2] = dv.astype(BF16)
        dp_ref[3] = dgz.astype(BF16)

    def body(q_ref, f_ref, i_ref, g_ref, lg_ref, gn_ref, o_ref, st_ref, dog_ref,
             dp_ref, dlb_ref, dgn_ref, dstate, dst_scr):
        for hs in range(HP):
            cols = slice(hs * HEAD, (hs + 1) * HEAD)
            one_head(q_ref.at[:, cols], f_ref.at[:, cols], i_ref.at[:, cols], g_ref.at[:, cols], lg_ref.at[:, cols],
                     gn_ref, o_ref.at[:, cols], st_ref.at[hs], dog_ref.at[:, cols],
                     dp_ref.at[:, :, cols], dlb_ref.at[hs], dgn_ref.at[hs], dstate.at[hs], dst_scr)

    HP, wide = HGRN_HEADS_PER_STEP, HGRN_HEADS_PER_STEP * HEAD
    hg = H // HP

    def part(p):
        return pl.BlockSpec((tb, wide), functools.partial(lambda h, i, p: (nb - 1 - i, p * hg + h), p=p))

    blk = pl.BlockSpec((tb, wide), lambda h, i: (nb - 1 - i, h))
    acc = pl.BlockSpec((HP, 1, HEAD), lambda h, i: (h, 0, 0))
    r_ops, r_in, r_out, r_shape, r_scr = _rider_args(rider)
    res = pl.pallas_call(
        _ride(rider, body, 9, 3, functools.partial(_grid_corner, 0, 0), functools.partial(_grid_corner, hg - 1, nb - 1)),
        name="hgrn_bwd", grid=(hg, nb),
        in_specs=[part(0), part(1), part(2), part(3),
                  pl.BlockSpec((3, wide), lambda h, i: (0, h)),
                  pl.BlockSpec((1, HEAD), lambda h, i: (0, 0)),
                  blk,
                  pl.BlockSpec((HP, ncb, HEAD, HEAD), lambda h, i: (h, nb - 1 - i, 0, 0)),
                  blk] + r_in,
        out_specs=[pl.BlockSpec((4, tb, wide), lambda h, i: (0, nb - 1 - i, h)), acc, acc] + r_out,
        out_shape=[jax.ShapeDtypeStruct((4, S, H * HEAD), BF16),
                   jax.ShapeDtypeStruct((H, 1, HEAD), F32),
                   jax.ShapeDtypeStruct((H, 1, HEAD), F32)] + r_shape,
        scratch_shapes=[pltpu.VMEM((HP, HEAD, HEAD), F32), pltpu.VMEM((ncb, HEAD, HEAD), F32)] + r_scr,
        compiler_params=_params(("arbitrary", "arbitrary")))(
            proj, proj, proj, proj, logits, gain, o, states, dog, *r_ops)
    return res[:3], res[3:]


def _rope(v, cos, sin):
    return v * cos + pltpu.roll(v, HEAD // 2, 1) * sin


def _lane_pick(tile, hh):
    lane = lax.broadcasted_iota(jnp.int32, tile.shape, 1)
    return jnp.sum(jnp.where(lane == hh, tile, 0.0), axis=-1, keepdims=True)


def _lane_place(cols):
    rows = cols[0].shape[0]
    lane = lax.broadcasted_iota(jnp.int32, (rows, HEAD), 1)
    tile = jnp.zeros((rows, HEAD), F32)
    for hh, v in enumerate(cols):
        tile = jnp.where(lane == hh, v, tile)
    return tile


def _band_masks():
    qi = lax.broadcasted_iota(jnp.int32, (ATTN_SPAN, ATTN_SPAN), 0)
    kj = lax.broadcasted_iota(jnp.int32, (ATTN_SPAN, ATTN_SPAN), 1)
    return kj <= qi, kj >= qi


ATTN_TILE_BLOCKS = 8


def _attn_fwd(a):
    d, L, _ = a.shape
    B, W = min(ATTN_TILE_BLOCKS, a.shape[1] // ATTN_SPAN), ATTN_SPAN
    T = B * W
    assert L % T == 0
    steps = L // T
    scale = HEAD ** -0.5

    def body(q_ref, kc_ref, kp_ref, vc_ref, vp_ref, o_ref, lse_ref):
        n = pl.program_id(1)
        mask_c, mask_p0 = _band_masks()
        first = jnp.logical_and(mask_p0, n > 0)
        units = [(b, hh) for b in range(B) for hh in range(HEADS_PER_GROUP)]
        rows = [slice(b * W, (b + 1) * W) for b in range(B)]
        cols = [slice(hh * HEAD, (hh + 1) * HEAD) for hh in range(HEADS_PER_GROUP)]

        def prev_keys(ref, tile, b, hh):
            return ref[:, cols[hh]] if b == 0 else tile[rows[b - 1], cols[hh]]

        s_c = [jnp.where(mask_c, _dot_nt(q_ref[rows[b], cols[hh]], kc_ref[rows[b], cols[hh]]) * scale, NEG) for b, hh in units]
        s_p = [jnp.where(first if b == 0 else mask_p0,
                         _dot_nt(q_ref[rows[b], cols[hh]], prev_keys(kp_ref, kc_ref, b, hh)) * scale, NEG) for b, hh in units]
        m = [jnp.maximum(jnp.max(x, axis=-1, keepdims=True), jnp.max(y, axis=-1, keepdims=True)) for x, y in zip(s_c, s_p)]
        p_c = [jnp.exp(x - mm) for x, mm in zip(s_c, m)]
        p_p = [jnp.exp(y - mm) for y, mm in zip(s_p, m)]
        l = [jnp.sum(x, axis=-1, keepdims=True) + jnp.sum(y, axis=-1, keepdims=True) for x, y in zip(p_c, p_p)]
        acc = [_dot(p_c[i].astype(BF16), vc_ref[rows[b], cols[hh]]) + _dot(p_p[i].astype(BF16), prev_keys(vp_ref, vc_ref, b, hh))
               for i, (b, hh) in enumerate(units)]
        for i, (b, hh) in enumerate(units):
            o_ref[rows[b], cols[hh]] = (acc[i] / l[i]).astype(BF16)
        for b in range(B):
            lse_ref[rows[b], :] = _lane_place([m[i] + jnp.log(l[i]) for i, (bb, _) in enumerate(units) if bb == b])

    def cur(part):
        return pl.BlockSpec((None, T, GROUP_W), functools.partial(lambda r, n, p: (r, n, p), p=part))

    def prev(part):
        return pl.BlockSpec((None, W, GROUP_W), functools.partial(lambda r, n, p: (r, jnp.maximum(n * B - 1, 0), p), p=part))

    return pl.pallas_call(
        body, name=f"attn_fwd_d{d}", grid=(d, steps),
        in_specs=[cur(0), cur(1), prev(1), cur(2), prev(2)],
        out_specs=[pl.BlockSpec((None, T, GROUP_W), lambda r, n: (r, n, 0)), pl.BlockSpec((None, T, HEAD), lambda r, n: (r, n, 0))],
        out_shape=[jax.ShapeDtypeStruct((d, L, GROUP_W), BF16), jax.ShapeDtypeStruct((d, L, HEAD), F32)],
        compiler_params=_params(("parallel", "arbitrary")))(a, a, a, a, a)


def _attn_bwd(a, do, lse, dd):
    d, L, _ = a.shape
    B, W = min(ATTN_TILE_BLOCKS, a.shape[1] // ATTN_SPAN), ATTN_SPAN
    T = B * W
    assert L % T == 0
    steps = L // T
    scale = HEAD ** -0.5

    def body(qc_ref, qn_ref, kp_ref, kc_ref, vp_ref, vc_ref, doc_ref, don_ref, lc_ref, ln_ref, ddc_ref, ddn_ref, da_ref):
        n = pl.program_id(1)
        mask_c, mask_p0 = _band_masks()
        first = jnp.logical_and(mask_p0, n > 0)
        last = jnp.logical_and(mask_p0, n < steps - 1)
        H4 = range(HEADS_PER_GROUP)
        units = [(b, hh) for b in range(B) for hh in H4]
        rows = [slice(b * W, (b + 1) * W) for b in range(B)]
        cols = [slice(hh * HEAD, (hh + 1) * HEAD) for hh in H4]
        q = {u: qc_ref[rows[u[0]], cols[u[1]]] for u in units}
        k = {u: kc_ref[rows[u[0]], cols[u[1]]] for u in units}
        v = {u: vc_ref[rows[u[0]], cols[u[1]]] for u in units}
        g_o = {u: doc_ref[rows[u[0]], cols[u[1]]] for u in units}
        kb = {(b, hh): kp_ref[:, cols[hh]] if b == 0 else k[(b - 1, hh)] for b, hh in units}
        vb = {(b, hh): vp_ref[:, cols[hh]] if b == 0 else v[(b - 1, hh)] for b, hh in units}
        lse_t = {(b, hh): _lane_pick(lc_ref[rows[b], :], hh) for b, hh in units}
        dd_t = {(b, hh): _lane_pick(ddc_ref[rows[b], :], hh) for b, hh in units}
        p_c = {u: jnp.where(mask_c, jnp.exp(_dot_nt(q[u], k[u]) * scale - lse_t[u]), 0.0) for u in units}
        p_p = {u: jnp.where(first if u[0] == 0 else mask_p0, jnp.exp(_dot_nt(q[u], kb[u]) * scale - lse_t[u]), 0.0) for u in units}
        ds_c = {u: (p_c[u] * (_dot_nt(g_o[u], v[u]) + dd_t[u])).astype(BF16) for u in units}
        ds_p = {u: (p_p[u] * (_dot_nt(g_o[u], vb[u]) + dd_t[u])).astype(BF16) for u in units}
        qn = [qn_ref[:, c] for c in cols]
        g_n = [don_ref[:, c] for c in cols]
        p_n = [jnp.where(last, jnp.exp(_dot_nt(qn[hh], k[(B - 1, hh)]) * scale - _lane_pick(ln_ref[...], hh)), 0.0) for hh in H4]
        ds_n = [(p_n[hh] * (_dot_nt(g_n[hh], v[(B - 1, hh)]) + _lane_pick(ddn_ref[...], hh))).astype(BF16) for hh in H4]
        dq = {u: (_dot(ds_c[u], k[u]) + _dot(ds_p[u], kb[u])) * scale for u in units}
        dk, dv = {}, {}
        for b, hh in units:
            if b < B - 1:
                nxt = (b + 1, hh)
                dk[(b, hh)] = (_dot_tn(ds_c[(b, hh)], q[(b, hh)]) + _dot_tn(ds_p[nxt], q[nxt])) * scale
                dv[(b, hh)] = _dot_tn(p_c[(b, hh)].astype(BF16), g_o[(b, hh)]) + _dot_tn(p_p[nxt].astype(BF16), g_o[nxt])
            else:
                dk[(b, hh)] = (_dot_tn(ds_c[(b, hh)], q[(b, hh)]) + _dot_tn(ds_n[hh], qn[hh])) * scale
                dv[(b, hh)] = _dot_tn(p_c[(b, hh)].astype(BF16), g_o[(b, hh)]) + _dot_tn(p_n[hh].astype(BF16), g_n[hh])
        for b, hh in units:
            da_ref[rows[b], cols[hh]] = dq[(b, hh)].astype(BF16)
            da_ref[rows[b], GROUP_W + hh * HEAD:GROUP_W + (hh + 1) * HEAD] = dk[(b, hh)].astype(BF16)
            da_ref[rows[b], 2 * GROUP_W + hh * HEAD:2 * GROUP_W + (hh + 1) * HEAD] = dv[(b, hh)].astype(BF16)

    nb = L // W

    def cur(width, part):
        return pl.BlockSpec((None, T, width), functools.partial(lambda r, n, p: (r, n, p), p=part))

    def prev(width, part):
        return pl.BlockSpec((None, W, width), functools.partial(lambda r, n, p: (r, jnp.maximum(n * B - 1, 0), p), p=part))

    def nxt(width, part):
        return pl.BlockSpec((None, W, width), functools.partial(lambda r, n, p: (r, jnp.minimum(n * B + B, nb - 1), p), p=part))

    g = GROUP_W
    return pl.pallas_call(
        body, name=f"attn_bwd_d{d}", grid=(d, steps),
        in_specs=[cur(g, 0), nxt(g, 0), prev(g, 1), cur(g, 1), prev(g, 2), cur(g, 2),
                  cur(g, 0), nxt(g, 0), cur(HEAD, 0), nxt(HEAD, 0), cur(HEAD, 0), nxt(HEAD, 0)],
        out_specs=pl.BlockSpec((None, T, 3 * g), lambda r, n: (r, n, 0)),
        out_shape=jax.ShapeDtypeStruct((d, L, 3 * g), BF16),
        compiler_params=_params(("parallel", "arbitrary")))(
            a, a, a, a, a, a, do, do, lse, lse, dd, dd)


def _softmax3(ls):
    mx = jnp.maximum(jnp.maximum(ls[0], ls[1]), ls[2])
    es = [jnp.exp(v - mx) for v in ls]
    tot = es[0] + es[1] + es[2]
    return [e / tot for e in es]


HEAD_COLS = [slice(hh * HEAD, (hh + 1) * HEAD) for hh in range(HEADS_PER_GROUP)]


def _group_spec(d, tm):
    return pl.BlockSpec((d, tm // d, GROUP_W), lambda i: (0, i, 0))


def _gather_heads(ref, scr, d, tm):
    if d == 1:
        return [ref[0, :, cols].astype(F32) for cols in HEAD_COLS]
    for hh, cols in enumerate(HEAD_COLS):
        for r in range(d):
            scr.at[hh][pl.ds(r, tm // d, stride=d), :] = ref[r, :, cols].astype(F32)
    return [scr[hh] for hh in range(HEADS_PER_GROUP)]


def _tile_spec(d, tm):
    return pl.BlockSpec((d, tm // d, HEAD), lambda i: (0, i, 0))


def _gather_tile(ref, scr, d, tm):
    if d == 1:
        return ref[0]
    for r in range(d):
        scr[pl.ds(r, tm // d, stride=d), :] = ref[r]
    return scr[...]


def _scatter_tile(val, scr, ref, d, tm):
    if d == 1:
        ref[0] = val
        return
    scr[...] = val
    for r in range(d):
        ref[r] = scr[pl.ds(r, tm // d, stride=d), :]


def _scatter_heads(vals, scr, ref, d, tm):
    if d == 1:
        for cols, v in zip(HEAD_COLS, vals):
            ref[0, :, cols] = v.astype(ref.dtype)
        return
    for hh, v in enumerate(vals):
        scr[hh] = v
    for hh, cols in enumerate(HEAD_COLS):
        for r in range(d):
            ref[r, :, cols] = scr.at[hh][pl.ds(r, tm // d, stride=d), :].astype(ref.dtype)


def _qkv_dilated(h, gain, w4, gi, cos, sin, d, tm=2048):
    S, K = h.shape
    n_shard = w4.shape[2]
    assert n_shard % HEAD == 0

    def head_cols(hh):
        def index(i, p):
            c = p * (len(ATTN_GROUPS) * GROUP_W) + gi * GROUP_W + hh * HEAD
            return c // n_shard, 0, (c % n_shard) // HEAD
        return pl.BlockSpec((None, K, HEAD), index)

    def body(h_ref, g_ref, *refs):
        w_refs, (cos_ref, sin_ref, out_ref, u_ref, y_scr) = refs[:HEADS_PER_GROUP], refs[HEADS_PER_GROUP:]
        p = pl.program_id(1)

        @pl.when(p == 0)
        def _():
            v = h_ref[...]
            u_ref[...] = (v * _rstd(v) * g_ref[...]).astype(BF16)

        y = _dot(u_ref[...], jnp.concatenate([r[...] for r in w_refs], axis=1))
        heads = [slice(hh * HEAD, (hh + 1) * HEAD) for hh in range(HEADS_PER_GROUP)]
        if d > 1:
            for hh, cols in enumerate(heads):
                y_scr[hh] = y[:, cols]

        def rows_of(hh, r):
            return y[:, heads[hh]] if d == 1 else y_scr.at[hh][pl.ds(r, tm // d, stride=d), :]

        @pl.when(p < 2)
        def _():
            for r in range(d):
                rows = slice(None) if d == 1 else pl.ds(r, tm // d, stride=d)
                cr, sr = cos_ref[rows, :], sin_ref[rows, :]
                for hh, cols in enumerate(heads):
                    out_ref[r, :, cols] = _rope(rows_of(hh, r), cr, sr).astype(BF16)

        @pl.when(p == 2)
        def _():
            for r in range(d):
                for hh, cols in enumerate(heads):
                    out_ref[r, :, cols] = rows_of(hh, r).astype(BF16)

    tab = pl.BlockSpec((tm, HEAD), lambda i, p: (i, 0))
    return pl.pallas_call(
        body, name=f"attn_qkv_d{d}", grid=(S // tm, 3),
        in_specs=[pl.BlockSpec((tm, K), lambda i, p: (i, 0)),
                  pl.BlockSpec((1, K), lambda i, p: (0, 0)),
                  *[head_cols(hh) for hh in range(HEADS_PER_GROUP)], tab, tab],
        out_specs=[pl.BlockSpec((d, tm // d, GROUP_W), lambda i, p: (0, i, p)), pl.BlockSpec((tm, K), lambda i, p: (i, 0))],
        out_shape=[jax.ShapeDtypeStruct((d, S // d, 3 * GROUP_W), BF16), jax.ShapeDtypeStruct((S, K), BF16)],
        scratch_shapes=[pltpu.VMEM((HEADS_PER_GROUP, tm, HEAD), F32)],
        compiler_params=_params(("parallel", "arbitrary")))(h, gain, *[w4] * HEADS_PER_GROUP, cos, sin)


def _undilate_group(da, dqkv, cos, sin, g, tm=2048):
    d, L, _ = da.shape
    S = d * L
    G = len(ATTN_GROUPS)

    def body(*refs):
        da_ref, cos_ref, sin_ref, out_ref, scr = refs[0], refs[1], refs[2], refs[-2], refs[-1]
        p = pl.program_id(1)
        heads = [slice(hh * HEAD, (hh + 1) * HEAD) for hh in range(HEADS_PER_GROUP)]
        if d > 1:
            for hh, cols in enumerate(heads):
                for r in range(d):
                    scr.at[hh][pl.ds(r, tm // d, stride=d), :] = da_ref[r, :, cols].astype(F32)

        def tokens(hh):
            return da_ref[0, :, heads[hh]].astype(F32) if d == 1 else scr[hh]

        @pl.when(p < 2)
        def _():
            cr, sr = cos_ref[...], -sin_ref[...]
            for hh, cols in enumerate(heads):
                out_ref[:, cols] = _rope(tokens(hh), cr, sr).astype(BF16)

        @pl.when(p == 2)
        def _():
            for hh, cols in enumerate(heads):
                out_ref[:, cols] = tokens(hh).astype(BF16)

    tab = pl.BlockSpec((tm, HEAD), lambda i, p: (i, 0))
    operands = (da, cos, sin) if dqkv is None else (da, cos, sin, dqkv)
    return pl.pallas_call(
        body, name=f"attn_undilate_d{d}", grid=(S // tm, 3),
        in_specs=[pl.BlockSpec((d, tm // d, GROUP_W), lambda i, p: (0, i, p)), tab, tab] + ([] if dqkv is None else [ANY]),
        out_specs=pl.BlockSpec((tm, GROUP_W), lambda i, p: (i, p * G + g)),
        out_shape=jax.ShapeDtypeStruct((S, 3 * G * GROUP_W), BF16),
        input_output_aliases={} if dqkv is None else {3: 0},
        scratch_shapes=[pltpu.VMEM((HEADS_PER_GROUP, tm, HEAD), F32)],
        compiler_params=_params(("parallel", "arbitrary")))(*operands)


def _attn_merge(os_, lses, h, w2, tm=1024):
    G = len(os_)
    S, N = h.shape

    def body(*refs):
        o_refs, l_refs, h_ref, w_ref, res_ref, out_ref = refs[:G], refs[G:2 * G], *refs[2 * G:2 * G + 4]
        scr = refs[2 * G + 4:]
        o = [_gather_heads(o_refs[g], scr[g], d, tm) for g, (_, d) in enumerate(ATTN_GROUPS)]
        l = [_gather_tile(l_refs[g], scr[G + g].at[0], d, tm) for g, (_, d) in enumerate(ATTN_GROUPS)]
        for hh in range(HEADS_PER_GROUP):
            al = _softmax3([_lane_pick(l[g], hh) for g in range(G)])
            for g in range(G):
                out_ref[:, g * GROUP_W + hh * HEAD:g * GROUP_W + (hh + 1) * HEAD] = (o[g][hh] * al[g]).astype(BF16)
        res_ref[...] = h_ref[...] + _dot(out_ref[...], w_ref[...])

    specs = [_group_spec(d, tm) for _, d in ATTN_GROUPS]
    row = pl.BlockSpec((tm, N), lambda i: (i, 0))
    return pl.pallas_call(
        body, name="attn_merge_out", grid=(S // tm,),
        in_specs=specs + [_tile_spec(d, tm) for _, d in ATTN_GROUPS] + [
            row, pl.BlockSpec((G * GROUP_W, N), lambda i: (0, 0), pipeline_mode=pl.Buffered(1))],
        out_specs=[row, pl.BlockSpec((tm, G * GROUP_W), lambda i: (i, 0))],
        out_shape=[jax.ShapeDtypeStruct((S, N), F32), jax.ShapeDtypeStruct((S, G * GROUP_W), BF16)],
        scratch_shapes=[pltpu.VMEM((HEADS_PER_GROUP, tm, HEAD), F32)] * (2 * G),
        compiler_params=_params(("parallel",)))(*os_, *lses, h, w2)


def _attn_merge_bwd(os_, lses, dh, w2, tm=512):
    G = len(os_)
    S, N = dh.shape

    def body(*refs):
        o_refs, l_refs, dh_ref, w_ref = refs[:G], refs[G:2 * G], refs[2 * G], refs[2 * G + 1]
        do_refs, dd_refs = refs[2 * G + 2:3 * G + 2], refs[3 * G + 2:4 * G + 2]
        scr = refs[4 * G + 2:]
        doa = _dot_nt(dh_ref[...].astype(BF16), w_ref[...])
        o = [_gather_heads(o_refs[g], scr[g], d, tm) for g, (_, d) in enumerate(ATTN_GROUPS)]
        l = [_gather_tile(l_refs[g], scr[G + g].at[0], d, tm) for g, (_, d) in enumerate(ATTN_GROUPS)]
        do = [[None] * HEADS_PER_GROUP for _ in range(G)]
        dd = [[None] * HEADS_PER_GROUP for _ in range(G)]
        for hh in range(HEADS_PER_GROUP):
            al = _softmax3([_lane_pick(l[g], hh) for g in range(G)])
            mix = None
            for g in range(G):
                dg = doa[:, g * GROUP_W + hh * HEAD:g * GROUP_W + (hh + 1) * HEAD]
                do[g][hh] = dg * al[g]
                t = al[g] * jnp.sum(dg * o[g][hh], axis=-1, keepdims=True)
                mix = t if mix is None else mix + t
            for g in range(G):
                dd[g][hh] = -al[g] * mix
        for g, (_, d) in enumerate(ATTN_GROUPS):
            _scatter_heads(do[g], scr[2 * G + g], do_refs[g], d, tm)
            _scatter_tile(_lane_place(dd[g]), scr[3 * G + g].at[0], dd_refs[g], d, tm)

    specs = [_group_spec(d, tm) for _, d in ATTN_GROUPS]
    tiles = [_tile_spec(d, tm) for _, d in ATTN_GROUPS]
    do_shapes = [jax.ShapeDtypeStruct((d, S // d, GROUP_W), BF16) for _, d in ATTN_GROUPS]
    dd_shapes = [jax.ShapeDtypeStruct((d, S // d, HEAD), F32) for _, d in ATTN_GROUPS]
    return pl.pallas_call(
        body, name="attn_merge_bwd", grid=(S // tm,),
        in_specs=specs + tiles + [pl.BlockSpec((tm, N), lambda i: (i, 0)),
                                  pl.BlockSpec((G * GROUP_W, N), lambda i: (0, 0), pipeline_mode=pl.Buffered(1))],
        out_specs=specs + tiles,
        out_shape=do_shapes + dd_shapes,
        scratch_shapes=[pltpu.VMEM((HEADS_PER_GROUP, tm, HEAD), F32)] * (4 * G),
        compiler_params=_params(("parallel",)))(*os_, *lses, dh, w2)


def _rope_tables(S):
    inv_freq = (1.0 / (np.float32(ROPE_THETA) ** (np.arange(0, HEAD, 2, dtype=np.float32) / np.float32(HEAD))))
    ang = (np.arange(S, dtype=np.float32)[:, None] * inv_freq.astype(np.float32)[None, :]).astype(np.float64)
    cos, sin = np.cos(ang).astype(np.float32), np.sin(ang).astype(np.float32)
    return jnp.asarray(np.concatenate([cos, cos], axis=-1)), jnp.asarray(np.concatenate([-sin, sin], axis=-1))


def _local_step(x, target, norm_mix, norm_ffn, lb_logits, out_gain, final_norm, comm):
    S = x.shape[0]
    nm0, nm1 = norm_mix[0:1], norm_mix[1:2]
    nf0, nf1 = norm_ffn[0:1], norm_ffn[1:2]
    w = comm.first_weights()

    proj, u0, got = _norm_mm(x, nm0, w["hin"], "hgrn_in", rider=comm.gather_rider(LATE_WEIGHTS_A))
    w.update(comm.gathered(LATE_WEIGHTS_A, got))
    (o, og, states), got = _hgrn_fwd(proj, lb_logits, out_gain, rider=comm.gather_rider(LATE_WEIGHTS_B))
    w.update(comm.gathered(LATE_WEIGHTS_B, got))
    fin_tn = w["fin0"].shape[2]
    h1 = _mm_res(x, og, w["hout"], "hgrn_out")
    z0, u1, _ = _norm_mm(h1, nf0, w["fin0"], "ffn0_in", out_dtype=BF16)
    h2, act0 = _swiglu_mm_res(h1, z0, w["fdn0"], "ffn0_down")
    cos, sin = _rope_tables(S)
    G = len(ATTN_GROUPS)
    a_g, u2 = zip(*[_qkv_dilated(h2, nm1, w["qkv"], gi, cos, sin, d) for gi, (_, d) in enumerate(ATTN_GROUPS)])
    o_g, lse_g = zip(*[_attn_fwd(a) for a in a_g])
    h3, oa = _attn_merge(o_g, lse_g, h2, w["aout"])
    z1, u3, _ = _norm_mm(h3, nf1, w["fin1"], "ffn1_in", out_dtype=BF16)
    h4, act1 = _swiglu_mm_res(h3, z1, w["fdn1"], "ffn1_down")
    dh4, loss, d_final = _loss_head(h4, final_norm, target)

    grads, small = {}, {"final_norm": d_final}

    def ffn_bwd(dh, h_in, u_in, z, act, gain, w_in, w_dn, tag, ride=None):
        dz = _mm_nt_swiglu_bwd(dh, w_dn, z, tag + "_down_dx")
        g_dn = _mm_tn(act, dh, 1, D_MODEL, D_MODEL, tag + "_down_dw")[0]
        g_in = _mm_tn(u_in, dz, N_CHIPS, fin_tn, fin_tn, tag + "_in_dw")
        rider = None if ride is None else ride(g_in, g_dn)
        dh_in, dgain, got = _mm_nt_normbwd(dz, w_in, h_in, gain, dh, tag + "_in_dx", rider=rider)
        return dh_in, dgain, g_in, g_dn, got

    dh3, d_nf1, grads["fin1"], grads["fdn1"], _ = ffn_bwd(dh4, h3, u3, z1, act1, nf1, w["fin1"], w["fdn1"], "ffn1")
    grads["aout"] = _mm_tn(oa, dh3, 1, D_MODEL, D_MODEL, "attn_out_dw")[0]
    merged = _attn_merge_bwd(o_g, lse_g, dh3, w["aout"])
    G = len(ATTN_GROUPS)
    das = [_attn_bwd(a_g[gi], merged[gi], lse_g[gi], merged[G + gi]) for gi in range(G)]
    dqkv = None
    for gi in range(G):
        dqkv = _undilate_group(das[gi], dqkv, cos, sin, gi)
    n_qkv = w["qkv"].shape[2]
    grads["qkv"] = _mm_tn(u2[0], dqkv, N_CHIPS, n_qkv, n_qkv, "attn_qkv_dw")
    dh2, d_nm1, _ = _mm_nt_normbwd(dqkv, w["qkv"], h2, nm1, dh3, "attn_qkv_dx")

    def ride_early(g_in, g_dn):
        return comm.pair_rider({**grads, "fin0": g_in, "fdn0": g_dn}, "early")

    dh1, d_nf0, _, _, got = ffn_bwd(dh2, h1, u1, z0, act0, nf0, w["fin0"], w["fdn0"], "ffn0", ride=ride_early)
    comm.paired("early", got)
    dog = _mm_nt(dh1, w["hout"][None], "hgrn_out_dx")
    (dproj, dlb, dgn), got = _hgrn_bwd(proj, lb_logits, out_gain, o, states, dog, rider=comm.exchange_rider("early"))
    comm.exchanged("early", got)
    late = {"hout": _mm_tn(og, dh1, 1, D_MODEL, D_MODEL, "hgrn_out_dw")[0],
            "hin": _mm_tn(u0, dproj, N_CHIPS, D_MODEL, D_MODEL, "hgrn_in_dw")}
    comm.pair_now(late, "late")
    dx, d_nm0, got = _mm_nt_normbwd(dproj, w["hin"], x, nm0, dh1, "hgrn_in_dx", rider=comm.exchange_rider("late"))
    comm.exchanged("late", got)

    small["norm_mix"] = jnp.concatenate([d_nm0, d_nm1], axis=0)
    small["norm_ffn"] = jnp.concatenate([d_nf0, d_nf1], axis=0)
    small["lb"] = dlb.reshape(1, HGRN_HEADS * HEAD)
    small["out_norm"] = dgn.reshape(HGRN_HEADS, HEAD)
    return loss, dx, small


def _place():
    x, y, c = lax.axis_index("x"), lax.axis_index("y"), lax.axis_index("c")
    others = [(1 - x, y), (x, 1 - y), (1 - x, 1 - y)]
    return x, y, c, others


ANY = pl.BlockSpec(memory_space=pl.ANY)


class _GatherRider:
    def __init__(self, shards):
        self.operands = list(shards)
        n = self.n = len(shards)
        self.out_shape = [jax.ShapeDtypeStruct((N_CHIPS,) + s.shape, s.dtype) for s in shards]
        self.scratch = [pltpu.SemaphoreType.DMA((3 * n,)), pltpu.SemaphoreType.DMA((3 * n,)),
                        pltpu.SemaphoreType.DMA((3 * n,)), pltpu.SemaphoreType.DMA((3 * n,)),
                        pltpu.SemaphoreType.DMA((n,)), pltpu.SemaphoreType.DMA((n,))]

    def _copies(self, ins, outs, sems):
        ici_send, ici_recv, _, _, own_send, own_recv = sems
        x, y, c, others = _place()
        me = 2 * x + y
        own = [pltpu.make_async_remote_copy(
            src_ref=ins[a], dst_ref=outs[a].at[me], send_sem=own_send.at[a], recv_sem=own_recv.at[a],
            device_id=(x, y, 1 - c), device_id_type=MESH) for a in range(self.n)]
        sends = [pltpu.make_async_remote_copy(
            src_ref=ins[a].at[c], dst_ref=outs[a].at[me, c], send_sem=ici_send.at[a * 3 + k], recv_sem=ici_recv.at[a * 3 + k],
            device_id=(ox, oy, c), device_id_type=MESH) for a in range(self.n) for k, (ox, oy) in enumerate(others)]
        return own, sends

    def start(self, ins, outs, sems):
        own, sends = self._copies(ins, outs, sems)
        for cp in own + sends:
            cp.start()

    def finish(self, ins, outs, sems):
        ici_send, ici_recv, d2d_send, d2d_recv, _, _ = sems
        x, y, c, others = _place()
        sibling = (x, y, 1 - c)
        own, sends = self._copies(ins, outs, sems)
        passes = []
        for a in range(self.n):
            for k, (ox, oy) in enumerate(others):
                s = a * 3 + k
                got = outs[a].at[2 * ox + oy, c]
                pltpu.make_async_remote_copy(
                    src_ref=got, dst_ref=got, send_sem=ici_send.at[s], recv_sem=ici_recv.at[s],
                    device_id=(ox, oy, c), device_id_type=MESH).wait_recv()
                fwd = pltpu.make_async_remote_copy(
                    src_ref=got, dst_ref=got, send_sem=d2d_send.at[s], recv_sem=d2d_recv.at[s],
                    device_id=sibling, device_id_type=MESH)
                fwd.start()
                passes.append(fwd)
        for a in range(self.n):
            for k, (ox, oy) in enumerate(others):
                s = a * 3 + k
                theirs = outs[a].at[2 * ox + oy, 1 - c]
                pltpu.make_async_remote_copy(
                    src_ref=theirs, dst_ref=theirs, send_sem=d2d_send.at[s], recv_sem=d2d_recv.at[s],
                    device_id=sibling, device_id_type=MESH).wait_recv()
        for cp in own:
            cp.wait()
        for cp in sends + passes:
            cp.wait_send()


class _PairRider:
    def __init__(self, grads):
        self.operands = list(grads)
        n = self.n = len(grads)
        self.out_shape = [jax.ShapeDtypeStruct((N_CHIPS,) + g.shape[2:], F32) for g in grads]
        self.scratch = [pltpu.SemaphoreType.DMA((N_CHIPS * n,)), pltpu.SemaphoreType.DMA((N_CHIPS * n,))]

    def _copies(self, ins, outs, sems):
        send_sem, recv_sem = sems
        x, y, c, _ = _place()
        return [pltpu.make_async_remote_copy(
            src_ref=ins[a].at[j, 1 - c], dst_ref=outs[a].at[j], send_sem=send_sem.at[a * N_CHIPS + j],
            recv_sem=recv_sem.at[a * N_CHIPS + j], device_id=(x, y, 1 - c), device_id_type=MESH)
            for a in range(self.n) for j in range(N_CHIPS)]

    def start(self, ins, outs, sems):
        for cp in self._copies(ins, outs, sems):
            cp.start()

    def finish(self, ins, outs, sems):
        for cp in self._copies(ins, outs, sems):
            cp.wait()


class _ExchangeRider:
    def __init__(self, parts):
        self.operands = list(parts)
        n = self.n = len(parts)
        self.out_shape = [jax.ShapeDtypeStruct(p.shape, p.dtype) for p in parts]
        self.scratch = [pltpu.SemaphoreType.DMA((3 * n,)), pltpu.SemaphoreType.DMA((3 * n,))]

    def _copies(self, ins, outs, sems):
        send_sem, recv_sem = sems
        x, y, c, others = _place()
        me = 2 * x + y
        return [pltpu.make_async_remote_copy(
            src_ref=ins[a].at[2 * ox + oy], dst_ref=outs[a].at[me], send_sem=send_sem.at[a * 3 + k],
            recv_sem=recv_sem.at[a * 3 + k], device_id=(ox, oy, c), device_id_type=MESH)
            for a in range(self.n) for k, (ox, oy) in enumerate(others)]

    def start(self, ins, outs, sems):
        for cp in self._copies(ins, outs, sems):
            cp.start()

    def finish(self, ins, outs, sems):
        send_sem, recv_sem = sems
        x, y, c, others = _place()
        for a in range(self.n):
            for k, (ox, oy) in enumerate(others):
                s = a * 3 + k
                got = outs[a].at[2 * ox + oy]
                pltpu.make_async_remote_copy(
                    src_ref=got, dst_ref=got, send_sem=send_sem.at[s], recv_sem=recv_sem.at[s],
                    device_id=(ox, oy, c), device_id_type=MESH).wait_recv()
        for cp in self._copies(ins, outs, sems):
            cp.wait_send()


def _run_rider(rider, name):
    n = rider.n

    def body(*refs):
        ins, outs, sems = refs[:n], refs[n:2 * n], refs[2 * n:]
        rider.start(ins, outs, sems)
        rider.finish(ins, outs, sems)

    return pl.pallas_call(
        body, name=name, in_specs=[ANY] * n, out_specs=[ANY] * n,
        out_shape=rider.out_shape, scratch_shapes=rider.scratch)(*rider.operands)


def _ride(rider, body, n_in, n_out, first, last):
    if rider is None:
        return body
    n = rider.n

    def wrapped(*refs):
        host_in, r_in = refs[:n_in], refs[n_in:n_in + n]
        host_out = refs[n_in + n:n_in + n + n_out]
        r_out = refs[n_in + n + n_out:n_in + 2 * n + n_out]
        rest = refs[n_in + 2 * n + n_out:]
        host_scr, sems = rest[:len(rest) - len(rider.scratch)], rest[len(rest) - len(rider.scratch):]

        @pl.when(first())
        def _():
            rider.start(r_in, r_out, sems)

        body(*host_in, *host_out, *host_scr)

        @pl.when(last())
        def _():
            rider.finish(r_in, r_out, sems)

    return wrapped


def _rider_args(rider):
    if rider is None:
        return [], [], [], [], []
    return rider.operands, [ANY] * rider.n, [ANY] * rider.n, rider.out_shape, rider.scratch


def _pair_sum(gs, gots, c_idx):
    n, parts = len(gs), 2

    def body(c_ref, *refs):
        g_refs, got_refs, pb_refs = refs[:n], refs[n:2 * n], refs[2 * n:]
        for k in range(n):
            pb_refs[k][...] = (g_refs[k][...] + got_refs[k][...]).astype(BF16)

    mine = [pl.BlockSpec((None, None, g.shape[2] // parts, g.shape[3]), lambda j, i, c_ref: (j, c_ref[0], i, 0))
            for g in gs]
    blk = [pl.BlockSpec((None, g.shape[2] // parts, g.shape[3]), lambda j, i, c_ref: (j, i, 0)) for g in gs]
    return pl.pallas_call(
        body, name="grad_pair_sum",
        grid_spec=pltpu.PrefetchScalarGridSpec(
            num_scalar_prefetch=1, grid=(N_CHIPS, parts), in_specs=mine + blk, out_specs=blk),
        out_shape=[jax.ShapeDtypeStruct((N_CHIPS,) + g.shape[2:], BF16) for g in gs],
        compiler_params=_params(("parallel", "parallel")))(c_idx, *gs, *gots)


def _chip_sum(gs, sibs, gots, place):
    n, parts = len(gs), 4

    def body(place_ref, *refs):
        g_refs, sib_refs, got_refs, t_refs = refs[:n], refs[n:2 * n], refs[2 * n:3 * n], refs[3 * n:]
        me = place_ref[0]
        for k in range(n):
            own = g_refs[k][...] + sib_refs[k][...]
            acc = None
            for s in range(N_CHIPS):
                term = jnp.where(me == s, own, got_refs[k][s].astype(F32))
                acc = term if acc is None else acc + term
            t_refs[k][...] = acc

    tiles = [(g.shape[2] // parts, g.shape[3]) for g in gs]
    return pl.pallas_call(
        body, name="grad_chip_sum",
        grid_spec=pltpu.PrefetchScalarGridSpec(
            num_scalar_prefetch=1, grid=(parts,),
            in_specs=[pl.BlockSpec((None, None) + t, lambda i, pr: (pr[0], pr[1], i, 0)) for t in tiles]
            + [pl.BlockSpec((None,) + t, lambda i, pr: (pr[0], i, 0)) for t in tiles]
            + [pl.BlockSpec((N_CHIPS,) + t, lambda i, pr: (0, i, 0)) for t in tiles],
            out_specs=[pl.BlockSpec(t, lambda i, pr: (i, 0)) for t in tiles]),
        out_shape=[jax.ShapeDtypeStruct(g.shape[2:], F32) for g in gs],
        compiler_params=_params(("parallel",)))(place, *gs, *sibs, *gots)


def _pair_share(halves):
    n = len(halves)

    def body(*refs):
        ins, outs = refs[:n], refs[n:2 * n]
        send_sem, recv_sem = refs[2 * n:]
        x, y, c, _ = _place()
        cps = [pltpu.make_async_remote_copy(
            src_ref=ins[a], dst_ref=outs[a], send_sem=send_sem.at[a], recv_sem=recv_sem.at[a],
            device_id=(x, y, 1 - c), device_id_type=MESH) for a in range(n)]
        for cp in cps:
            cp.start()
        for cp in cps:
            cp.wait()

    return pl.pallas_call(
        body, name="grad_pair_share",
        in_specs=[ANY] * n, out_specs=[ANY] * n,
        out_shape=[jax.ShapeDtypeStruct(h.shape, F32) for h in halves],
        scratch_shapes=[pltpu.SemaphoreType.DMA((n,)), pltpu.SemaphoreType.DMA((n,))],
        )(*halves)


def _small_allreduce(pack):
    m_per, ncol = pack.shape
    n_dev = 8

    def body(x_ref, sum_ref, all_ref, send_sems, recv_sems, local_sem):
        x, y, c, others = _place()
        me, sibling = (x, y, c), (x, y, 1 - c)

        def rows(px, py, pc):
            return all_ref.at[pl.ds((4 * px + 2 * py + pc) * m_per, m_per), :]

        def copy(k, block, to, src=None):
            return pltpu.make_async_remote_copy(
                src_ref=rows(*block) if src is None else src, dst_ref=rows(*block),
                send_sem=send_sems.at[k], recv_sem=recv_sems.at[k], device_id=to, device_id_type=MESH)

        mine = pltpu.make_async_copy(x_ref, rows(*me), local_sem)
        mine.start()
        first = [copy(0, me, sibling, src=x_ref)]
        first += [copy(1 + j, me, (*chip, c), src=x_ref) for j, chip in enumerate(others)]
        for cp in first:
            cp.start()
        passed = [copy(4 + j, (*chip, c), sibling) for j, chip in enumerate(others)]
        for j, chip in enumerate(others):
            copy(1 + j, (*chip, c), me).wait_recv()
            passed[j].start()
        copy(0, sibling, me).wait_recv()
        for j, chip in enumerate(others):
            copy(4 + j, (*chip, 1 - c), me).wait_recv()
        for cp in first + passed:
            cp.wait_send()
        mine.wait()
        acc = all_ref[0:m_per, :]
        for dvc in range(1, n_dev):
            acc = acc + all_ref[dvc * m_per:(dvc + 1) * m_per, :]
        sum_ref[...] = acc

    return pl.pallas_call(
        body, name="small_allreduce",
        in_specs=[pl.BlockSpec(memory_space=pltpu.VMEM)],
        out_specs=pl.BlockSpec(memory_space=pltpu.VMEM),
        out_shape=jax.ShapeDtypeStruct((m_per, ncol), F32),
        scratch_shapes=[pltpu.VMEM((n_dev * m_per, ncol), F32),
                        pltpu.SemaphoreType.DMA((7,)), pltpu.SemaphoreType.DMA((7,)), pltpu.SemaphoreType.DMA],
        )(pack)


def _adam_math(w, g, m, v):
    m = ADAM_B1 * m + (1.0 - ADAM_B1) * g
    v = ADAM_B2 * v + (1.0 - ADAM_B2) * (g * g)
    m_hat = m / (1.0 - ADAM_B1 ** ADAM_STEP)
    v_hat = v / (1.0 - ADAM_B2 ** ADAM_STEP)
    delta = -ADAM_LR * (m_hat / (jnp.sqrt(v_hat) + ADAM_EPS) + ADAM_WD * w)
    return delta, m, v


def _adamw(halves, c_idx, w, m, v, name):
    L = len(halves)
    r, C = halves[0][0].shape
    tr = _row_tile(r, C, 2 * 1024 * 1024)
    nt = r // tr

    def body(c_ref, *refs):
        g_refs, (w_ref, m_ref, v_ref), (g_ref, d_ref, nm_ref, nv_ref) = refs[:2 * L], refs[2 * L:2 * L + 3], refs[2 * L + 3:]
        own = pl.program_id(1) == c_ref[0]
        g = None
        for l in range(L):
            cand = jnp.where(own, g_refs[2 * l][...], g_refs[2 * l + 1][...])
            g = cand if g is None else jnp.where(pl.program_id(0) == l, cand, g)
        g_ref[...] = g
        d_ref[...], nm_ref[...], nv_ref[...] = _adam_math(w_ref[...], g, m_ref[...], v_ref[...])

    def half(l, mine):
        def index(ll, h, i, c_ref):
            read = (h == c_ref[0]) if mine else (h != c_ref[0])
            return jnp.where(jnp.logical_and(ll == l, read), i, 0), 0
        return pl.BlockSpec((tr, C), index)

    full = pl.BlockSpec((None, tr, C), lambda ll, h, i, c_ref: (ll, h * nt + i, 0))
    shp = jax.ShapeDtypeStruct((L, 2 * r, C), F32)
    g_specs = [half(l, mine) for l in range(L) for mine in (True, False)]
    return pl.pallas_call(
        body, name=name,
        grid_spec=pltpu.PrefetchScalarGridSpec(
            num_scalar_prefetch=1, grid=(L, 2, nt),
            in_specs=g_specs + [full] * 3, out_specs=[full] * 4),
        out_shape=[shp] * 4,
        compiler_params=_params(("arbitrary", "arbitrary", "arbitrary")))(
            c_idx, *[a for pair in halves for a in pair], w, m, v)


SMALL_ROW_SPANS = ((0, 2), (2, 4), (4, 7), (7, 8), (8, 9))


def _small_update(gsum, w, m, v):
    n = len(SMALL_ROW_SPANS)

    def body(gs_ref, *refs):
        w_refs, m_refs, v_refs = refs[:n], refs[n:2 * n], refs[2 * n:3 * n]
        g_refs, d_refs, nm_refs, nv_refs = [refs[(3 + k) * n:(4 + k) * n] for k in range(4)]
        lg_ref = w_refs[2]
        l0, l1, l2 = lg_ref[0:1, :], lg_ref[1:2, :], lg_ref[2:3, :]
        mx = jnp.maximum(jnp.maximum(l0, l1), l2)
        e0, e1, e2 = jnp.exp(l0 - mx), jnp.exp(l1 - mx), jnp.exp(l2 - mx)
        tot = e0 + e1 + e2
        p0, p1, p2 = e0 / tot, e1 / tot, e2 / tot
        dlb = gs_ref[4:5, :]
        for k, (r0, r1) in enumerate(SMALL_ROW_SPANS):
            if k == 2:
                g = jnp.concatenate([dlb * p0 * (1.0 - p0), -dlb * p0 * p1, -dlb * p0 * p2], axis=0)
            else:
                g = gs_ref[r0:r1, 0:w_refs[k].shape[1]]
            g_refs[k][...] = g
            d_refs[k][...], nm_refs[k][...], nv_refs[k][...] = _adam_math(w_refs[k][...], g, m_refs[k][...], v_refs[k][...])

    full = pl.BlockSpec(memory_space=pltpu.VMEM)
    shapes = [jax.ShapeDtypeStruct(a.shape, F32) for a in w]
    out = pl.pallas_call(
        body, name="small_update", in_specs=[full] * (1 + 3 * n), out_specs=[full] * (4 * n), out_shape=shapes * 4)(
            gsum, *w, *m, *v)
    return [out[k * n:(k + 1) * n] for k in range(4)]


def _pack_small(norm_mix, norm_ffn, lb3, out_norm, final_norm, extra=None):
    ncol = norm_mix.shape[1]
    on = jnp.pad(out_norm.reshape(1, -1), ((0, 0), (0, ncol - out_norm.size)))
    rows = [norm_mix, norm_ffn, lb3, on, final_norm.reshape(1, ncol)]
    if extra is not None:
        rows.append(extra)
    used = sum(r.shape[0] for r in rows)
    rows.append(jnp.zeros((SMALL_ROWS - used, ncol), F32))
    return jnp.concatenate(rows, axis=0)


WEIGHT_NAMES = ("hin", "hout", "qkv", "aout", "fin0", "fin1", "fdn0", "fdn1")
FIRST_WEIGHTS = ("hin",)
LATE_WEIGHTS_A = ("hout", "fin0", "fdn0")
LATE_WEIGHTS_B = ("qkv", "aout", "fin1", "fdn1")


def _split_weights(hgrn_w_in, hgrn_w_out, attn_w_qkv, attn_w_out, ffn_w_in, ffn_w_down):
    return {"hin": hgrn_w_in[0], "hout": hgrn_w_out[0], "qkv": attn_w_qkv[0], "aout": attn_w_out[0],
            "fin0": ffn_w_in[0], "fin1": ffn_w_in[1], "fdn0": ffn_w_down[0], "fdn1": ffn_w_down[1]}


def _halves(v):
    r, c = v.shape
    return v.reshape(2, r // 2, c)


def _full_weights(gathered):
    out = {}
    for k, g in gathered.items():
        _, _, r, c = g.shape
        if k in ("hin", "qkv", "fin0", "fin1"):
            out[k] = g.reshape(N_CHIPS, 2 * r, c)
        else:
            out[k] = g.reshape(N_CHIPS * 2 * r, c)
    return out


class _StepComm:
    def __init__(self, shards, c_idx, me_idx):
        self.shards, self.c_idx, self.me_idx = shards, c_idx, me_idx
        self.halves = {}
        self._stage = {}

    def gather_rider(self, names):
        return _GatherRider([_halves(self.shards[k].astype(BF16)) for k in names])

    def gathered(self, names, got):
        return _full_weights(dict(zip(names, got)))

    def first_weights(self):
        return self.gathered(FIRST_WEIGHTS, _run_rider(self.gather_rider(FIRST_WEIGHTS), "gather_first"))

    def pair_rider(self, grads, tag):
        names = list(grads)
        g4 = []
        for k in names:
            r, c = self.shards[k].shape
            g4.append(grads[k].reshape(N_CHIPS, 2, r // 2, c))
        self._stage[tag] = (names, g4)
        return _PairRider(g4)

    def pair_now(self, grads, tag):
        self.paired(tag, _run_rider(self.pair_rider(grads, tag), "grad_pair_exchange_" + tag))

    def paired(self, tag, got):
        names, g4 = self._stage[tag]
        self._stage[tag] = (names, list(zip(g4, got, _pair_sum(g4, list(got), self.c_idx))))

    def exchange_rider(self, tag):
        return _ExchangeRider([s[2] for s in self._stage[tag][1]])

    def exchanged(self, tag, got):
        names, sums = self._stage.pop(tag)
        place = jnp.concatenate([self.me_idx, self.c_idx])
        g4, sibs, _ = zip(*sums)
        self.halves.update(zip(names, _chip_sum(list(g4), list(sibs), list(got), place)))

    def shared_halves(self):
        mine = [self.halves[k] for k in WEIGHT_NAMES]
        return dict(zip(WEIGHT_NAMES, zip(mine, _pair_share(mine))))


def kernel(x, norm_mix, norm_ffn, hgrn_w_in, hgrn_lb_logits, hgrn_out_norm, hgrn_w_out, attn_w_qkv, attn_w_out, ffn_w_in, ffn_w_down, final_norm, loss_target, m_norm_mix, m_norm_ffn, m_hgrn_w_in, m_hgrn_lb_logits, m_hgrn_out_norm, m_hgrn_w_out, m_attn_w_qkv, m_attn_w_out, m_ffn_w_in, m_ffn_w_down, m_final_norm, v_norm_mix, v_norm_ffn, v_hgrn_w_in, v_hgrn_lb_logits, v_hgrn_out_norm, v_hgrn_w_out, v_attn_w_qkv, v_attn_w_out, v_ffn_w_in, v_ffn_w_down, v_final_norm):
    S = x.shape[1]
    xi, yi, ci = lax.axis_index("x"), lax.axis_index("y"), lax.axis_index("c")
    c_idx = jnp.reshape(ci, (1,)).astype(jnp.int32)
    me_idx = jnp.reshape(2 * xi + yi, (1,)).astype(jnp.int32)

    w_own = _split_weights(hgrn_w_in, hgrn_w_out, attn_w_qkv, attn_w_out, ffn_w_in, ffn_w_down)

    comm = _StepComm(w_own, c_idx, me_idx)
    loss, dx, small = _local_step(
        x.reshape(S, D_MODEL), loss_target.reshape(S, D_MODEL), norm_mix, norm_ffn, hgrn_lb_logits,
        hgrn_out_norm, final_norm.reshape(1, D_MODEL), comm)

    halves = comm.shared_halves()
    updated = {}
    for tensor, layers, (wt, mt, vt) in (
            ("hgrn_w_in", ("hin",), (hgrn_w_in, m_hgrn_w_in, v_hgrn_w_in)),
            ("hgrn_w_out", ("hout",), (hgrn_w_out, m_hgrn_w_out, v_hgrn_w_out)),
            ("attn_w_qkv", ("qkv",), (attn_w_qkv, m_attn_w_qkv, v_attn_w_qkv)),
            ("attn_w_out", ("aout",), (attn_w_out, m_attn_w_out, v_attn_w_out)),
            ("ffn_w_in", ("fin0", "fin1"), (ffn_w_in, m_ffn_w_in, v_ffn_w_in)),
            ("ffn_w_down", ("fdn0", "fdn1"), (ffn_w_down, m_ffn_w_down, v_ffn_w_down))):
        updated[tensor] = _adamw([halves[k] for k in layers], c_idx, wt, mt, vt, "adamw_" + tensor)

    loss_row = jnp.pad(loss, ((0, 0), (0, D_MODEL - loss.shape[1])))
    lb3 = jnp.concatenate([small["lb"], jnp.zeros((2, D_MODEL), F32)], axis=0)
    on_grad = jnp.sum(small["out_norm"], axis=0, keepdims=True)
    pack = _pack_small(small["norm_mix"], small["norm_ffn"], lb3, on_grad, small["final_norm"], loss_row)
    gsum = _small_allreduce(pack)
    fn2 = (1, D_MODEL)
    sg, sd, sm, sv = _small_update(
        gsum, (norm_mix, norm_ffn, hgrn_lb_logits, hgrn_out_norm, final_norm.reshape(fn2)),
        (m_norm_mix, m_norm_ffn, m_hgrn_lb_logits, m_hgrn_out_norm, m_final_norm.reshape(fn2)),
        (v_norm_mix, v_norm_ffn, v_hgrn_lb_logits, v_hgrn_out_norm, v_final_norm.reshape(fn2)))

    def assemble(p, which):
        nmx, nff, lbl, onm, fnm = p
        fnm = fnm.reshape(D_MODEL)
        hin, hout, qkv, aout, fin, fdn = [updated[t][which] for t in
                                          ("hgrn_w_in", "hgrn_w_out", "attn_w_qkv", "attn_w_out", "ffn_w_in", "ffn_w_down")]
        return (nmx, nff, hin, lbl, onm, hout, qkv, aout, fin, fdn, fnm)

    total_loss = gsum[9, 0]
    return (total_loss, dx.reshape(1, S, D_MODEL), *assemble(sg, 0), *assemble(sd, 1), *assemble(sm, 2), *assemble(sv, 3))
```
